```python
import math
import jax, jax.numpy as jnp
from jax import lax
import numpy as np

D_MODEL = 1024
BATCH = 8
SEQ = 4096
DEPTH = 2

D_MIX = D_MODEL
MLA_HEADS = 8
QK_NOPE_DIM = 64
QK_ROPE_DIM = 32
V_HEAD_DIM = 64
Q_LORA_RANK = 256
KV_LORA_RANK = 128
MLA_WIDTH = MLA_HEADS * V_HEAD_DIM
ROPE_THETA = 10000.0
Q_BLOCK = 128
SSM_WIDTH = D_MIX - MLA_WIDTH
SSM_GROUP = 16
SSM_GROUPS = SSM_WIDTH // SSM_GROUP
SSM_STATE = 64
DT_MIN = 0.001
DT_MAX = 0.1
IN_WIDTH = Q_LORA_RANK + KV_LORA_RANK + QK_ROPE_DIM + SSM_WIDTH
MEM_LEN = 256
X_HEADS = 4
X_HEAD_DIM = D_MODEL // X_HEADS
D_FF = -(-8 * D_MODEL // (3 * 256)) * 256
EPS = 1e-6

kernel_name = 'hybrid_mla_s5_memory_decoder'


def rmsnorm(x, g):
    xf = x.astype(jnp.float32)
    y = xf * lax.rsqrt(jnp.mean(xf * xf, axis=-1, keepdims=True) + EPS)
    return (y * g.astype(jnp.float32)).astype(x.dtype)


def apply_rope(x, cos, sin):
    xf = x.astype(jnp.float32)
    half = xf.shape[-1] // 2
    x1, x2 = xf[..., :half], xf[..., half:]
    return jnp.concatenate([x1 * cos - x2 * sin, x2 * cos + x1 * sin], axis=-1).astype(x.dtype)


def mla_group(c_q, c_kv, k_r, q_norm_g, w_uq, kv_norm_g, w_ukv, cos, sin):
    B, S, _ = c_q.shape
    q = (rmsnorm(c_q, q_norm_g) @ w_uq).reshape(B, S, MLA_HEADS, QK_NOPE_DIM + QK_ROPE_DIM)
    q_nope = q[..., :QK_NOPE_DIM]
    q_rope = apply_rope(q[..., QK_NOPE_DIM:], cos[:, :, None, :], sin[:, :, None, :])
    kv = (rmsnorm(c_kv, kv_norm_g) @ w_ukv).reshape(B, S, MLA_HEADS, QK_NOPE_DIM + V_HEAD_DIM)
    k_nope = kv[..., :QK_NOPE_DIM].transpose(0, 2, 1, 3)
    v = kv[..., QK_NOPE_DIM:].transpose(0, 2, 1, 3)
    k_rope = apply_rope(k_r, cos, sin)
    scale = (QK_NOPE_DIM + QK_ROPE_DIM) ** -0.5
    qn = (q_nope * scale).transpose(0, 2, 1, 3)
    qr = (q_rope * scale).transpose(0, 2, 1, 3)
    outs = []
    for i in range(S // Q_BLOCK):
        lo, hi = i * Q_BLOCK, (i + 1) * Q_BLOCK
        s = (jnp.einsum('bhqd,bhkd->bhqk', qn[:, :, lo:hi], k_nope[:, :, :hi])
             + jnp.einsum('bhqr,bkr->bhqk', qr[:, :, lo:hi], k_rope[:, :hi])).astype(jnp.float32)
        mask = jnp.arange(hi)[None, :] <= jnp.arange(lo, hi)[:, None]
        s = jnp.where(mask, s, -jnp.inf)
        p = jax.nn.softmax(s, axis=-1).astype(v.dtype)
        outs.append(jnp.einsum('bhqk,bhkd->bqhd', p, v[:, :, :hi]))
    o = jnp.concatenate(outs, axis=1)
    return o.reshape(B, S, MLA_WIDTH)


def s5_group(u, lam_re, lam_im, log_dt, b_re, b_im, c_re, c_im, d, w_glu, b_glu):
    B, S, _ = u.shape
    f32 = jnp.float32
    uf = u.astype(f32)
    ug = uf.reshape(B, S, SSM_GROUPS, SSM_GROUP)
    lam = lax.complex(lam_re.astype(f32), lam_im.astype(f32))
    dt = jnp.exp(log_dt.astype(f32))[:, None]
    a_bar = jnp.exp(lam * dt)
    b_mat = lax.complex(b_re.astype(f32), b_im.astype(f32))
    b_bar = ((a_bar - 1.0) / lam)[..., None] * b_mat
    bu = jnp.einsum('bsgc,gpc->bsgp', ug.astype(jnp.complex64), b_bar)
    a_elems = jnp.broadcast_to(a_bar, bu.shape)

    def combine(e1, e2):
        a1, x1 = e1
        a2, x2 = e2
        return a1 * a2, a2 * x1 + x2

    _, states = lax.associative_scan(combine, (a_elems, bu), axis=1)
    c_mat = lax.complex(c_re.astype(f32), c_im.astype(f32))
    y = jnp.einsum('bsgp,gcp->bsgc', states, c_mat).real.reshape(B, S, SSM_WIDTH)
    y = y + d.astype(f32) * uf
    g = jax.nn.gelu(y)
    y = y * jax.nn.sigmoid(g @ w_glu.astype(f32) + b_glu.astype(f32))
    return y.astype(u.dtype)


def memory_cross_attention(hn, memn, w_xq, w_xkv, w_xo):
    B, S, _ = hn.shape
    M = memn.shape[1]
    q = (hn @ w_xq).reshape(B, S, X_HEADS, X_HEAD_DIM)
    kv = (memn @ w_xkv).reshape(B, M, 2, X_HEADS, X_HEAD_DIM)
    k, v = kv[:, :, 0], kv[:, :, 1]
    s = jnp.einsum('bshd,bmhd->bhsm', q, k).astype(jnp.float32) * (X_HEAD_DIM ** -0.5)
    p = jax.nn.softmax(s, axis=-1).astype(v.dtype)
    o = jnp.einsum('bhsm,bmhd->bshd', p, v).reshape(B, S, D_MODEL)
    return o @ w_xo


def swiglu(hn, w_gate, w_up, w_down):
    return (jax.nn.silu(hn @ w_gate) * (hn @ w_up)) @ w_down


def _fwd_setup_inputs(seed: int = 0) -> dict:
    key = jax.random.key(seed)
    ks = jax.random.split(key, 40)
    f32 = jnp.float32

    def nrm(k, shape, fan_in):
        return jax.random.normal(k, shape, f32) * (fan_in ** -0.5)

    def gain(k, shape):
        return 1.0 + 0.05 * jax.random.normal(k, shape, f32)

    L = DEPTH
    x = jax.random.normal(ks[0], (BATCH, SEQ, D_MODEL), f32)
    mem = jax.random.normal(ks[1], (BATCH, MEM_LEN, D_MODEL), f32)
    start = jax.random.randint(ks[2], (BATCH, 1), 0, 1024, dtype=jnp.int32)
    positions = start + jnp.arange(SEQ, dtype=jnp.int32)[None, :]
    n_idx = jnp.arange(SSM_STATE, dtype=f32)
    ssm_lambda_re = -0.5 * jnp.exp(0.05 * jax.random.normal(ks[9], (L, SSM_GROUPS, SSM_STATE), f32))
    ssm_lambda_im = jnp.pi * n_idx + 0.01 * jax.random.normal(ks[10], (L, SSM_GROUPS, SSM_STATE), f32)
    ssm_log_dt = jax.random.uniform(ks[11], (L, SSM_GROUPS), f32, math.log(DT_MIN), math.log(DT_MAX))
    return {
        'x': x,
        'mem': mem,
        'positions': positions,
        'norm_mix_g': gain(ks[3], (L, D_MODEL)),
        'w_in': nrm(ks[4], (L, D_MODEL, IN_WIDTH), D_MODEL),
        'q_norm_g': gain(ks[5], (L, Q_LORA_RANK)),
        'w_uq': nrm(ks[6], (L, Q_LORA_RANK, MLA_HEADS * (QK_NOPE_DIM + QK_ROPE_DIM)), Q_LORA_RANK),
        'kv_norm_g': gain(ks[7], (L, KV_LORA_RANK)),
        'w_ukv': nrm(ks[8], (L, KV_LORA_RANK, MLA_HEADS * (QK_NOPE_DIM + V_HEAD_DIM)), KV_LORA_RANK),
        'ssm_lambda_re': ssm_lambda_re,
        'ssm_lambda_im': ssm_lambda_im,
        'ssm_log_dt': ssm_log_dt,
        'ssm_b_re': nrm(ks[12], (L, SSM_GROUPS, SSM_STATE, SSM_GROUP), 2 * SSM_GROUP),
        'ssm_b_im': nrm(ks[13], (L, SSM_GROUPS, SSM_STATE, SSM_GROUP), 2 * SSM_GROUP),
        'ssm_c_re': nrm(ks[14], (L, SSM_GROUPS, SSM_GROUP, SSM_STATE), 2 * SSM_STATE),
        'ssm_c_im': nrm(ks[15], (L, SSM_GROUPS, SSM_GROUP, SSM_STATE), 2 * SSM_STATE),
        'ssm_d': jax.random.normal(ks[16], (L, SSM_WIDTH), f32),
        'ssm_w_glu': nrm(ks[17], (L, SSM_WIDTH, SSM_WIDTH), SSM_WIDTH),
        'ssm_b_glu': 0.01 * jax.random.normal(ks[18], (L, SSM_WIDTH), f32),
        'attn_out_g': gain(ks[19], (L, MLA_WIDTH)),
        'ssm_out_g': gain(ks[20], (L, SSM_WIDTH)),
        'w_out': nrm(ks[21], (L, D_MIX, D_MODEL), D_MIX),
        'norm_x_g': gain(ks[22], (L, D_MODEL)),
        'mem_norm_g': gain(ks[23], (L, D_MODEL)),
        'w_xq': nrm(ks[24], (L, D_MODEL, D_MODEL), D_MODEL),
        'w_xkv': nrm(ks[25], (L, D_MODEL, 2 * D_MODEL), D_MODEL),
        'w_xo': nrm(ks[26], (L, D_MODEL, D_MODEL), D_MODEL),
        'norm_ffn_g': gain(ks[27], (L, D_MODEL)),
        'w_gate': nrm(ks[28], (L, D_MODEL, D_FF), D_MODEL),
        'w_up': nrm(ks[29], (L, D_MODEL, D_FF), D_MODEL),
        'w_down': nrm(ks[30], (L, D_FF, D_MODEL), D_FF),
        'final_norm_g': gain(ks[31], (D_MODEL,)),
    }


def _fwd_reference(x, mem, positions, norm_mix_g, w_in, q_norm_g, w_uq, kv_norm_g, w_ukv,
              ssm_lambda_re, ssm_lambda_im, ssm_log_dt, ssm_b_re, ssm_b_im, ssm_c_re, ssm_c_im,
              ssm_d, ssm_w_glu, ssm_b_glu, attn_out_g, ssm_out_g, w_out, norm_x_g, mem_norm_g,
              w_xq, w_xkv, w_xo, norm_ffn_g, w_gate, w_up, w_down, final_norm_g):
    freqs = ROPE_THETA ** (-jnp.arange(0, QK_ROPE_DIM, 2, dtype=jnp.float32) / QK_ROPE_DIM)
    ang = positions.astype(jnp.float32)[..., None] * freqs
    cos, sin = jnp.cos(ang), jnp.sin(ang)
    split_at = [Q_LORA_RANK, Q_LORA_RANK + KV_LORA_RANK, Q_LORA_RANK + KV_LORA_RANK + QK_ROPE_DIM]
    h = x
    for l in range(DEPTH):
        xn = rmsnorm(h, norm_mix_g[l])
        proj = xn @ w_in[l]
        c_q, c_kv, k_r, u = jnp.split(proj, split_at, axis=-1)
        a_out = mla_group(c_q, c_kv, k_r, q_norm_g[l], w_uq[l], kv_norm_g[l], w_ukv[l], cos, sin)
        s_out = s5_group(u, ssm_lambda_re[l], ssm_lambda_im[l], ssm_log_dt[l], ssm_b_re[l], ssm_b_im[l],
                         ssm_c_re[l], ssm_c_im[l], ssm_d[l], ssm_w_glu[l], ssm_b_glu[l])
        mixed = jnp.concatenate([rmsnorm(a_out, attn_out_g[l]), rmsnorm(s_out, ssm_out_g[l])], axis=-1)
        h = h + mixed @ w_out[l]
        h = h + memory_cross_attention(rmsnorm(h, norm_x_g[l]), rmsnorm(mem, mem_norm_g[l]),
                                       w_xq[l], w_xkv[l], w_xo[l])
        h = h + swiglu(rmsnorm(h, norm_ffn_g[l]), w_gate[l], w_up[l], w_down[l])
    return rmsnorm(h, final_norm_g)


import jax as _jax
import jax.numpy as _jnp

TWIN_FORMAT = 'train_step'
FWD_PARAMS = ['x', 'mem', 'positions', 'norm_mix_g', 'w_in', 'q_norm_g', 'w_uq', 'kv_norm_g', 'w_ukv', 'ssm_lambda_re', 'ssm_lambda_im', 'ssm_log_dt', 'ssm_b_re', 'ssm_b_im', 'ssm_c_re', 'ssm_c_im', 'ssm_d', 'ssm_w_glu', 'ssm_b_glu', 'attn_out_g', 'ssm_out_g', 'w_out', 'norm_x_g', 'mem_norm_g', 'w_xq', 'w_xkv', 'w_xo', 'norm_ffn_g', 'w_gate', 'w_up', 'w_down', 'final_norm_g']
TWIN_WEIGHTS = ['norm_mix_g', 'w_in', 'q_norm_g', 'w_uq', 'kv_norm_g', 'w_ukv', 'ssm_lambda_re', 'ssm_lambda_im', 'ssm_log_dt', 'ssm_b_re', 'ssm_b_im', 'ssm_c_re', 'ssm_c_im', 'ssm_d', 'ssm_w_glu', 'ssm_b_glu', 'attn_out_g', 'ssm_out_g', 'w_out', 'norm_x_g', 'mem_norm_g', 'w_xq', 'w_xkv', 'w_xo', 'norm_ffn_g', 'w_gate', 'w_up', 'w_down', 'final_norm_g']
TWIN_DIFF_INPUT = 'x'
TWIN_INPUTS = ['x', 'mem', 'positions', 'norm_mix_g', 'w_in', 'q_norm_g', 'w_uq', 'kv_norm_g', 'w_ukv', 'ssm_lambda_re', 'ssm_lambda_im', 'ssm_log_dt', 'ssm_b_re', 'ssm_b_im', 'ssm_c_re', 'ssm_c_im', 'ssm_d', 'ssm_w_glu', 'ssm_b_glu', 'attn_out_g', 'ssm_out_g', 'w_out', 'norm_x_g', 'mem_norm_g', 'w_xq', 'w_xkv', 'w_xo', 'norm_ffn_g', 'w_gate', 'w_up', 'w_down', 'final_norm_g', 'loss_target', 'm_norm_mix_g', 'm_w_in', 'm_q_norm_g', 'm_w_uq', 'm_kv_norm_g', 'm_w_ukv', 'm_ssm_lambda_re', 'm_ssm_lambda_im', 'm_ssm_log_dt', 'm_ssm_b_re', 'm_ssm_b_im', 'm_ssm_c_re', 'm_ssm_c_im', 'm_ssm_d', 'm_ssm_w_glu', 'm_ssm_b_glu', 'm_attn_out_g', 'm_ssm_out_g', 'm_w_out', 'm_norm_x_g', 'm_mem_norm_g', 'm_w_xq', 'm_w_xkv', 'm_w_xo', 'm_norm_ffn_g', 'm_w_gate', 'm_w_up', 'm_w_down', 'm_final_norm_g', 'v_norm_mix_g', 'v_w_in', 'v_q_norm_g', 'v_w_uq', 'v_kv_norm_g', 'v_w_ukv', 'v_ssm_lambda_re', 'v_ssm_lambda_im', 'v_ssm_log_dt', 'v_ssm_b_re', 'v_ssm_b_im', 'v_ssm_c_re', 'v_ssm_c_im', 'v_ssm_d', 'v_ssm_w_glu', 'v_ssm_b_glu', 'v_attn_out_g', 'v_ssm_out_g', 'v_w_out', 'v_norm_x_g', 'v_mem_norm_g', 'v_w_xq', 'v_w_xkv', 'v_w_xo', 'v_norm_ffn_g', 'v_w_gate', 'v_w_up', 'v_w_down', 'v_final_norm_g']
TWIN_OUTPUTS = ['loss', 'grad_x', 'grad_norm_mix_g', 'grad_w_in', 'grad_q_norm_g', 'grad_w_uq', 'grad_kv_norm_g', 'grad_w_ukv', 'grad_ssm_lambda_re', 'grad_ssm_lambda_im', 'grad_ssm_log_dt', 'grad_ssm_b_re', 'grad_ssm_b_im', 'grad_ssm_c_re', 'grad_ssm_c_im', 'grad_ssm_d', 'grad_ssm_w_glu', 'grad_ssm_b_glu', 'grad_attn_out_g', 'grad_ssm_out_g', 'grad_w_out', 'grad_norm_x_g', 'grad_mem_norm_g', 'grad_w_xq', 'grad_w_xkv', 'grad_w_xo', 'grad_norm_ffn_g', 'grad_w_gate', 'grad_w_up', 'grad_w_down', 'grad_final_norm_g', 'delta_norm_mix_g', 'delta_w_in', 'delta_q_norm_g', 'delta_w_uq', 'delta_kv_norm_g', 'delta_w_ukv', 'delta_ssm_lambda_re', 'delta_ssm_lambda_im', 'delta_ssm_log_dt', 'delta_ssm_b_re', 'delta_ssm_b_im', 'delta_ssm_c_re', 'delta_ssm_c_im', 'delta_ssm_d', 'delta_ssm_w_glu', 'delta_ssm_b_glu', 'delta_attn_out_g', 'delta_ssm_out_g', 'delta_w_out', 'delta_norm_x_g', 'delta_mem_norm_g', 'delta_w_xq', 'delta_w_xkv', 'delta_w_xo', 'delta_norm_ffn_g', 'delta_w_gate', 'delta_w_up', 'delta_w_down', 'delta_final_norm_g', 'new_m_norm_mix_g', 'new_m_w_in', 'new_m_q_norm_g', 'new_m_w_uq', 'new_m_kv_norm_g', 'new_m_w_ukv', 'new_m_ssm_lambda_re', 'new_m_ssm_lambda_im', 'new_m_ssm_log_dt', 'new_m_ssm_b_re', 'new_m_ssm_b_im', 'new_m_ssm_c_re', 'new_m_ssm_c_im', 'new_m_ssm_d', 'new_m_ssm_w_glu', 'new_m_ssm_b_glu', 'new_m_attn_out_g', 'new_m_ssm_out_g', 'new_m_w_out', 'new_m_norm_x_g', 'new_m_mem_norm_g', 'new_m_w_xq', 'new_m_w_xkv', 'new_m_w_xo', 'new_m_norm_ffn_g', 'new_m_w_gate', 'new_m_w_up', 'new_m_w_down', 'new_m_final_norm_g', 'new_v_norm_mix_g', 'new_v_w_in', 'new_v_q_norm_g', 'new_v_w_uq', 'new_v_kv_norm_g', 'new_v_w_ukv', 'new_v_ssm_lambda_re', 'new_v_ssm_lambda_im', 'new_v_ssm_log_dt', 'new_v_ssm_b_re', 'new_v_ssm_b_im', 'new_v_ssm_c_re', 'new_v_ssm_c_im', 'new_v_ssm_d', 'new_v_ssm_w_glu', 'new_v_ssm_b_glu', 'new_v_attn_out_g', 'new_v_ssm_out_g', 'new_v_w_out', 'new_v_norm_x_g', 'new_v_mem_norm_g', 'new_v_w_xq', 'new_v_w_xkv', 'new_v_w_xo', 'new_v_norm_ffn_g', 'new_v_w_gate', 'new_v_w_up', 'new_v_w_down', 'new_v_final_norm_g']
TWIN_LEAF_KINDS = {'loss': 'loss', 'grad_x': 'grad_x', 'grad_norm_mix_g': 'grad_w', 'grad_w_in': 'grad_w', 'grad_q_norm_g': 'grad_w', 'grad_w_uq': 'grad_w', 'grad_kv_norm_g': 'grad_w', 'grad_w_ukv': 'grad_w', 'grad_ssm_lambda_re': 'grad_w', 'grad_ssm_lambda_im': 'grad_w', 'grad_ssm_log_dt': 'grad_w', 'grad_ssm_b_re': 'grad_w', 'grad_ssm_b_im': 'grad_w', 'grad_ssm_c_re': 'grad_w', 'grad_ssm_c_im': 'grad_w', 'grad_ssm_d': 'grad_w', 'grad_ssm_w_glu': 'grad_w', 'grad_ssm_b_glu': 'grad_w', 'grad_attn_out_g': 'grad_w', 'grad_ssm_out_g': 'grad_w', 'grad_w_out': 'grad_w', 'grad_norm_x_g': 'grad_w', 'grad_mem_norm_g': 'grad_w', 'grad_w_xq': 'grad_w', 'grad_w_xkv': 'grad_w', 'grad_w_xo': 'grad_w', 'grad_norm_ffn_g': 'grad_w', 'grad_w_gate': 'grad_w', 'grad_w_up': 'grad_w', 'grad_w_down': 'grad_w', 'grad_final_norm_g': 'grad_w', 'delta_norm_mix_g': 'delta_w', 'delta_w_in': 'delta_w', 'delta_q_norm_g': 'delta_w', 'delta_w_uq': 'delta_w', 'delta_kv_norm_g': 'delta_w', 'delta_w_ukv': 'delta_w', 'delta_ssm_lambda_re': 'delta_w', 'delta_ssm_lambda_im': 'delta_w', 'delta_ssm_log_dt': 'delta_w', 'delta_ssm_b_re': 'delta_w', 'delta_ssm_b_im': 'delta_w', 'delta_ssm_c_re': 'delta_w', 'delta_ssm_c_im': 'delta_w', 'delta_ssm_d': 'delta_w', 'delta_ssm_w_glu': 'delta_w', 'delta_ssm_b_glu': 'delta_w', 'delta_attn_out_g': 'delta_w', 'delta_ssm_out_g': 'delta_w', 'delta_w_out': 'delta_w', 'delta_norm_x_g': 'delta_w', 'delta_mem_norm_g': 'delta_w', 'delta_w_xq': 'delta_w', 'delta_w_xkv': 'delta_w', 'delta_w_xo': 'delta_w', 'delta_norm_ffn_g': 'delta_w', 'delta_w_gate': 'delta_w', 'delta_w_up': 'delta_w', 'delta_w_down': 'delta_w', 'delta_final_norm_g': 'delta_w', 'new_m_norm_mix_g': 'new_m', 'new_m_w_in': 'new_m', 'new_m_q_norm_g': 'new_m', 'new_m_w_uq': 'new_m', 'new_m_kv_norm_g': 'new_m', 'new_m_w_ukv': 'new_m', 'new_m_ssm_lambda_re': 'new_m', 'new_m_ssm_lambda_im': 'new_m', 'new_m_ssm_log_dt': 'new_m', 'new_m_ssm_b_re': 'new_m', 'new_m_ssm_b_im': 'new_m', 'new_m_ssm_c_re': 'new_m', 'new_m_ssm_c_im': 'new_m', 'new_m_ssm_d': 'new_m', 'new_m_ssm_w_glu': 'new_m', 'new_m_ssm_b_glu': 'new_m', 'new_m_attn_out_g': 'new_m', 'new_m_ssm_out_g': 'new_m', 'new_m_w_out': 'new_m', 'new_m_norm_x_g': 'new_m', 'new_m_mem_norm_g': 'new_m', 'new_m_w_xq': 'new_m', 'new_m_w_xkv': 'new_m', 'new_m_w_xo': 'new_m', 'new_m_norm_ffn_g': 'new_m', 'new_m_w_gate': 'new_m', 'new_m_w_up': 'new_m', 'new_m_w_down': 'new_m', 'new_m_final_norm_g': 'new_m', 'new_v_norm_mix_g': 'new_v', 'new_v_w_in': 'new_v', 'new_v_q_norm_g': 'new_v', 'new_v_w_uq': 'new_v', 'new_v_kv_norm_g': 'new_v', 'new_v_w_ukv': 'new_v', 'new_v_ssm_lambda_re': 'new_v', 'new_v_ssm_lambda_im': 'new_v', 'new_v_ssm_log_dt': 'new_v', 'new_v_ssm_b_re': 'new_v', 'new_v_ssm_b_im': 'new_v', 'new_v_ssm_c_re': 'new_v', 'new_v_ssm_c_im': 'new_v', 'new_v_ssm_d': 'new_v', 'new_v_ssm_w_glu': 'new_v', 'new_v_ssm_b_glu': 'new_v', 'new_v_attn_out_g': 'new_v', 'new_v_ssm_out_g': 'new_v', 'new_v_w_out': 'new_v', 'new_v_norm_x_g': 'new_v', 'new_v_mem_norm_g': 'new_v', 'new_v_w_xq': 'new_v', 'new_v_w_xkv': 'new_v', 'new_v_w_xo': 'new_v', 'new_v_norm_ffn_g': 'new_v', 'new_v_w_gate': 'new_v', 'new_v_w_up': 'new_v', 'new_v_w_down': 'new_v', 'new_v_final_norm_g': 'new_v'}


def _forward(args):
    return _fwd_reference(*[args[k] for k in FWD_PARAMS])


def _output_shape():
    out = _jax.eval_shape(lambda: _forward(_fwd_setup_inputs(0)))
    return out.shape, out.dtype

N_MICROBATCH = 1
ADAM_LR = 0.001
ADAM_B1 = 0.9
ADAM_B2 = 0.999
ADAM_EPS = 1e-08
ADAM_WD = 0.01
ADAM_STEP = 10
PER_EXAMPLE_BATCH_AXIS = {'x': 0, 'mem': 0, 'positions': 0, 'loss_target': 0}
SHARED_INPUTS = []
_WEIGHT_DTYPES = {'norm_mix_g': _jnp.float32, 'w_in': _jnp.float32, 'q_norm_g': _jnp.float32, 'w_uq': _jnp.float32, 'kv_norm_g': _jnp.float32, 'w_ukv': _jnp.float32, 'ssm_lambda_re': _jnp.float32, 'ssm_lambda_im': _jnp.float32, 'ssm_log_dt': _jnp.float32, 'ssm_b_re': _jnp.float32, 'ssm_b_im': _jnp.float32, 'ssm_c_re': _jnp.float32, 'ssm_c_im': _jnp.float32, 'ssm_d': _jnp.float32, 'ssm_w_glu': _jnp.float32, 'ssm_b_glu': _jnp.float32, 'attn_out_g': _jnp.float32, 'ssm_out_g': _jnp.float32, 'w_out': _jnp.float32, 'norm_x_g': _jnp.float32, 'mem_norm_g': _jnp.float32, 'w_xq': _jnp.float32, 'w_xkv': _jnp.float32, 'w_xo': _jnp.float32, 'norm_ffn_g': _jnp.float32, 'w_gate': _jnp.float32, 'w_up': _jnp.float32, 'w_down': _jnp.float32, 'final_norm_g': _jnp.float32}
MOMENT_SCALE = {'norm_mix_g': 1.667397e-01, 'w_in': 1.759334e-01, 'q_norm_g': 1.409921e-01, 'w_uq': 8.221826e-02, 'kv_norm_g': 4.508722e-01, 'w_ukv': 1.444105e-01, 'ssm_lambda_re': 9.579459e-03, 'ssm_lambda_im': 9.008666e-03, 'ssm_log_dt': 8.288379e+00, 'ssm_b_re': 5.042748e-03, 'ssm_b_im': 5.205782e-03, 'ssm_c_re': 1.032396e-02, 'ssm_c_im': 1.017149e-02, 'ssm_d': 1.347646e-01, 'ssm_w_glu': 3.619467e-02, 'ssm_b_glu': 5.753978e-02, 'attn_out_g': 1.904937e-01, 'ssm_out_g': 1.239171e-01, 'w_out': 1.583474e-01, 'norm_x_g': 1.453784e-02, 'mem_norm_g': 2.855576e-02, 'w_xq': 1.331468e-02, 'w_xkv': 1.928316e-02, 'w_xo': 2.343586e-02, 'norm_ffn_g': 9.404174e-02, 'w_gate': 4.026987e-02, 'w_up': 3.931735e-02, 'w_down': 6.528663e-02, 'final_norm_g': 3.219848e+01}


def _to_microbatches(a, axis):
    t = _jnp.moveaxis(a, axis, 0)
    t = t.reshape((N_MICROBATCH, t.shape[0] // N_MICROBATCH) + t.shape[1:])
    return _jnp.moveaxis(t, 1, axis + 1)


def setup_inputs(seed: int = 0) -> dict:
    inp = _fwd_setup_inputs(seed)
    key = _jax.random.fold_in(_jax.random.key(seed), 7919)
    shape, _ = _output_shape()
    out = dict(inp)
    out["loss_target"] = _jax.random.normal(_jax.random.fold_in(key, 0), shape, _jnp.float32)
    for i, name in enumerate(TWIN_WEIGHTS):
        w = inp[name].astype(_jnp.float32)
        if MOMENT_SCALE is None:
            s = _jnp.sqrt(_jnp.mean(_jnp.square(w)) + 1e-30)
        else:
            s = MOMENT_SCALE[name]
        km, kv = _jax.random.split(_jax.random.fold_in(key, i + 1))
        out[name] = w
        out["m_" + name] = s * _jax.random.normal(km, w.shape, _jnp.float32)
        out["v_" + name] = (s * s) * _jax.random.uniform(kv, w.shape, _jnp.float32, 0.5, 1.5)
    if N_MICROBATCH > 1:
        for name, axis in PER_EXAMPLE_BATCH_AXIS.items():
            out[name] = _to_microbatches(out[name], axis)
    return {'x': out['x'], 'mem': out['mem'], 'positions': out['positions'], 'norm_mix_g': out['norm_mix_g'], 'w_in': out['w_in'], 'q_norm_g': out['q_norm_g'], 'w_uq': out['w_uq'], 'kv_norm_g': out['kv_norm_g'], 'w_ukv': out['w_ukv'], 'ssm_lambda_re': out['ssm_lambda_re'], 'ssm_lambda_im': out['ssm_lambda_im'], 'ssm_log_dt': out['ssm_log_dt'], 'ssm_b_re': out['ssm_b_re'], 'ssm_b_im': out['ssm_b_im'], 'ssm_c_re': out['ssm_c_re'], 'ssm_c_im': out['ssm_c_im'], 'ssm_d': out['ssm_d'], 'ssm_w_glu': out['ssm_w_glu'], 'ssm_b_glu': out['ssm_b_glu'], 'attn_out_g': out['attn_out_g'], 'ssm_out_g': out['ssm_out_g'], 'w_out': out['w_out'], 'norm_x_g': out['norm_x_g'], 'mem_norm_g': out['mem_norm_g'], 'w_xq': out['w_xq'], 'w_xkv': out['w_xkv'], 'w_xo': out['w_xo'], 'norm_ffn_g': out['norm_ffn_g'], 'w_gate': out['w_gate'], 'w_up': out['w_up'], 'w_down': out['w_down'], 'final_norm_g': out['final_norm_g'], 'loss_target': out['loss_target'], 'm_norm_mix_g': out['m_norm_mix_g'], 'm_w_in': out['m_w_in'], 'm_q_norm_g': out['m_q_norm_g'], 'm_w_uq': out['m_w_uq'], 'm_kv_norm_g': out['m_kv_norm_g'], 'm_w_ukv': out['m_w_ukv'], 'm_ssm_lambda_re': out['m_ssm_lambda_re'], 'm_ssm_lambda_im': out['m_ssm_lambda_im'], 'm_ssm_log_dt': out['m_ssm_log_dt'], 'm_ssm_b_re': out['m_ssm_b_re'], 'm_ssm_b_im': out['m_ssm_b_im'], 'm_ssm_c_re': out['m_ssm_c_re'], 'm_ssm_c_im': out['m_ssm_c_im'], 'm_ssm_d': out['m_ssm_d'], 'm_ssm_w_glu': out['m_ssm_w_glu'], 'm_ssm_b_glu': out['m_ssm_b_glu'], 'm_attn_out_g': out['m_attn_out_g'], 'm_ssm_out_g': out['m_ssm_out_g'], 'm_w_out': out['m_w_out'], 'm_norm_x_g': out['m_norm_x_g'], 'm_mem_norm_g': out['m_mem_norm_g'], 'm_w_xq': out['m_w_xq'], 'm_w_xkv': out['m_w_xkv'], 'm_w_xo': out['m_w_xo'], 'm_norm_ffn_g': out['m_norm_ffn_g'], 'm_w_gate': out['m_w_gate'], 'm_w_up': out['m_w_up'], 'm_w_down': out['m_w_down'], 'm_final_norm_g': out['m_final_norm_g'], 'v_norm_mix_g': out['v_norm_mix_g'], 'v_w_in': out['v_w_in'], 'v_q_norm_g': out['v_q_norm_g'], 'v_w_uq': out['v_w_uq'], 'v_kv_norm_g': out['v_kv_norm_g'], 'v_w_ukv': out['v_w_ukv'], 'v_ssm_lambda_re': out['v_ssm_lambda_re'], 'v_ssm_lambda_im': out['v_ssm_lambda_im'], 'v_ssm_log_dt': out['v_ssm_log_dt'], 'v_ssm_b_re': out['v_ssm_b_re'], 'v_ssm_b_im': out['v_ssm_b_im'], 'v_ssm_c_re': out['v_ssm_c_re'], 'v_ssm_c_im': out['v_ssm_c_im'], 'v_ssm_d': out['v_ssm_d'], 'v_ssm_w_glu': out['v_ssm_w_glu'], 'v_ssm_b_glu': out['v_ssm_b_glu'], 'v_attn_out_g': out['v_attn_out_g'], 'v_ssm_out_g': out['v_ssm_out_g'], 'v_w_out': out['v_w_out'], 'v_norm_x_g': out['v_norm_x_g'], 'v_mem_norm_g': out['v_mem_norm_g'], 'v_w_xq': out['v_w_xq'], 'v_w_xkv': out['v_w_xkv'], 'v_w_xo': out['v_w_xo'], 'v_norm_ffn_g': out['v_norm_ffn_g'], 'v_w_gate': out['v_w_gate'], 'v_w_up': out['v_w_up'], 'v_w_down': out['v_w_down'], 'v_final_norm_g': out['v_final_norm_g']}


def _loss(weights, diff, rest, loss_target):
    with _jax.named_scope("forward"):
        args = {**rest, TWIN_DIFF_INPUT: diff, **{k: w.astype(_WEIGHT_DTYPES[k]) for k, w in weights.items()}}
        y = _forward(args)
    with _jax.named_scope("loss_head"):
        err = _jnp.square(y.astype(_jnp.float32) - loss_target)
        return 0.5 * _jnp.sum(_jnp.mean(err, axis=-1)) if err.ndim else 0.5 * err


def _adamw(w, g, m, v):
    m = ADAM_B1 * m + (1.0 - ADAM_B1) * g
    v = ADAM_B2 * v + (1.0 - ADAM_B2) * _jnp.square(g)
    m_hat = m / (1.0 - ADAM_B1 ** ADAM_STEP)
    v_hat = v / (1.0 - ADAM_B2 ** ADAM_STEP)
    delta = -ADAM_LR * (m_hat / (_jnp.sqrt(v_hat) + ADAM_EPS) + ADAM_WD * w)
    return delta, m, v


def reference(x, mem, positions, norm_mix_g, w_in, q_norm_g, w_uq, kv_norm_g, w_ukv, ssm_lambda_re, ssm_lambda_im, ssm_log_dt, ssm_b_re, ssm_b_im, ssm_c_re, ssm_c_im, ssm_d, ssm_w_glu, ssm_b_glu, attn_out_g, ssm_out_g, w_out, norm_x_g, mem_norm_g, w_xq, w_xkv, w_xo, norm_ffn_g, w_gate, w_up, w_down, final_norm_g, loss_target, m_norm_mix_g, m_w_in, m_q_norm_g, m_w_uq, m_kv_norm_g, m_w_ukv, m_ssm_lambda_re, m_ssm_lambda_im, m_ssm_log_dt, m_ssm_b_re, m_ssm_b_im, m_ssm_c_re, m_ssm_c_im, m_ssm_d, m_ssm_w_glu, m_ssm_b_glu, m_attn_out_g, m_ssm_out_g, m_w_out, m_norm_x_g, m_mem_norm_g, m_w_xq, m_w_xkv, m_w_xo, m_norm_ffn_g, m_w_gate, m_w_up, m_w_down, m_final_norm_g, v_norm_mix_g, v_w_in, v_q_norm_g, v_w_uq, v_kv_norm_g, v_w_ukv, v_ssm_lambda_re, v_ssm_lambda_im, v_ssm_log_dt, v_ssm_b_re, v_ssm_b_im, v_ssm_c_re, v_ssm_c_im, v_ssm_d, v_ssm_w_glu, v_ssm_b_glu, v_attn_out_g, v_ssm_out_g, v_w_out, v_norm_x_g, v_mem_norm_g, v_w_xq, v_w_xkv, v_w_xo, v_norm_ffn_g, v_w_gate, v_w_up, v_w_down, v_final_norm_g):
    given = dict(x=x, mem=mem, positions=positions, norm_mix_g=norm_mix_g, w_in=w_in, q_norm_g=q_norm_g, w_uq=w_uq, kv_norm_g=kv_norm_g, w_ukv=w_ukv, ssm_lambda_re=ssm_lambda_re, ssm_lambda_im=ssm_lambda_im, ssm_log_dt=ssm_log_dt, ssm_b_re=ssm_b_re, ssm_b_im=ssm_b_im, ssm_c_re=ssm_c_re, ssm_c_im=ssm_c_im, ssm_d=ssm_d, ssm_w_glu=ssm_w_glu, ssm_b_glu=ssm_b_glu, attn_out_g=attn_out_g, ssm_out_g=ssm_out_g, w_out=w_out, norm_x_g=norm_x_g, mem_norm_g=mem_norm_g, w_xq=w_xq, w_xkv=w_xkv, w_xo=w_xo, norm_ffn_g=norm_ffn_g, w_gate=w_gate, w_up=w_up, w_down=w_down, final_norm_g=final_norm_g, loss_target=loss_target, m_norm_mix_g=m_norm_mix_g, m_w_in=m_w_in, m_q_norm_g=m_q_norm_g, m_w_uq=m_w_uq, m_kv_norm_g=m_kv_norm_g, m_w_ukv=m_w_ukv, m_ssm_lambda_re=m_ssm_lambda_re, m_ssm_lambda_im=m_ssm_lambda_im, m_ssm_log_dt=m_ssm_log_dt, m_ssm_b_re=m_ssm_b_re, m_ssm_b_im=m_ssm_b_im, m_ssm_c_re=m_ssm_c_re, m_ssm_c_im=m_ssm_c_im, m_ssm_d=m_ssm_d, m_ssm_w_glu=m_ssm_w_glu, m_ssm_b_glu=m_ssm_b_glu, m_attn_out_g=m_attn_out_g, m_ssm_out_g=m_ssm_out_g, m_w_out=m_w_out, m_norm_x_g=m_norm_x_g, m_mem_norm_g=m_mem_norm_g, m_w_xq=m_w_xq, m_w_xkv=m_w_xkv, m_w_xo=m_w_xo, m_norm_ffn_g=m_norm_ffn_g, m_w_gate=m_w_gate, m_w_up=m_w_up, m_w_down=m_w_down, m_final_norm_g=m_final_norm_g, v_norm_mix_g=v_norm_mix_g, v_w_in=v_w_in, v_q_norm_g=v_q_norm_g, v_w_uq=v_w_uq, v_kv_norm_g=v_kv_norm_g, v_w_ukv=v_w_ukv, v_ssm_lambda_re=v_ssm_lambda_re, v_ssm_lambda_im=v_ssm_lambda_im, v_ssm_log_dt=v_ssm_log_dt, v_ssm_b_re=v_ssm_b_re, v_ssm_b_im=v_ssm_b_im, v_ssm_c_re=v_ssm_c_re, v_ssm_c_im=v_ssm_c_im, v_ssm_d=v_ssm_d, v_ssm_w_glu=v_ssm_w_glu, v_ssm_b_glu=v_ssm_b_glu, v_attn_out_g=v_attn_out_g, v_ssm_out_g=v_ssm_out_g, v_w_out=v_w_out, v_norm_x_g=v_norm_x_g, v_mem_norm_g=v_mem_norm_g, v_w_xq=v_w_xq, v_w_xkv=v_w_xkv, v_w_xo=v_w_xo, v_norm_ffn_g=v_norm_ffn_g, v_w_gate=v_w_gate, v_w_up=v_w_up, v_w_down=v_w_down, v_final_norm_g=v_final_norm_g)
    weights = {n: given[n] for n in TWIN_WEIGHTS}
    shared = {n: given[n] for n in SHARED_INPUTS}
    per_example = {n: given[n] for n in ['x', 'mem', 'positions']}
    grad_fn = _jax.value_and_grad(_loss, argnums=(0, 1))

    def one_microbatch(ex, loss_target):
        ex = dict(ex)
        diff = ex.pop(TWIN_DIFF_INPUT)
        return grad_fn(weights, diff, {**shared, **ex}, loss_target)

    if N_MICROBATCH == 1:
        loss, (grad_w, grad_x) = one_microbatch(per_example, given["loss_target"])
    else:
        def body(carry, xs):
            loss_sum, grad_sum = carry
            l_k, (gw_k, gx_k) = one_microbatch(xs[0], xs[1])
            with _jax.named_scope("update"):
                return (loss_sum + l_k, _jax.tree.map(_jnp.add, grad_sum, gw_k)), gx_k

        init = (_jnp.zeros((), _jnp.float32), _jax.tree.map(_jnp.zeros_like, weights))
        (loss, grad_w), grad_x = _jax.lax.scan(body, init, (per_example, given["loss_target"]))
    with _jax.named_scope("update"):
        delta_w, new_m, new_v = {}, {}, {}
        for n in TWIN_WEIGHTS:
            delta_w[n], new_m[n], new_v[n] = _adamw(weights[n], grad_w[n], given["m_" + n], given["v_" + n])
    return (loss, grad_x, *[grad_w[n] for n in TWIN_WEIGHTS], *[delta_w[n] for n in TWIN_WEIGHTS],
            *[new_m[n] for n in TWIN_WEIGHTS], *[new_v[n] for n in TWIN_WEIGHTS])
```

```python
import functools
import math

import jax
import jax.numpy as jnp
from jax import lax
from jax.experimental import pallas as pl
from jax.experimental.pallas import tpu as pltpu

f32, bf16 = jnp.float32, jnp.bfloat16

D_MODEL = 1024
DEPTH = 2
MLA_HEADS = 8
QK_NOPE = 64
QK_ROPE = 32
V_HEAD = 64
Q_LORA = 256
KV_LORA = 128
MLA_WIDTH = MLA_HEADS * V_HEAD
ROPE_THETA = 10000.0
SSM_WIDTH = 512
SSM_GROUP = 16
SSM_GROUPS = 32
SSM_STATE = 64
IN_WIDTH = Q_LORA + KV_LORA + QK_ROPE + SSM_WIDTH
X_HEADS = 4
X_HEAD_DIM = D_MODEL // X_HEADS
D_FF = 2816
EPS = 1e-6
ADAM_LR, ADAM_B1, ADAM_B2, ADAM_EPS, ADAM_WD, ADAM_STEP = 0.001, 0.9, 0.999, 1e-08, 0.01, 10

LANES = 128
SUBLANES = 8
HEAD_PAD = 128
MLA_PAD = MLA_HEADS * HEAD_PAD
SSM_MACRO = 4
MACRO_CH = SSM_WIDTH // SSM_MACRO
MACRO_ST = SSM_GROUPS // SSM_MACRO * SSM_STATE
VMEM_LIMIT = 56 * 1024 * 1024

WEIGHTS = ['norm_mix_g', 'w_in', 'q_norm_g', 'w_uq', 'kv_norm_g', 'w_ukv', 'ssm_lambda_re', 'ssm_lambda_im',
           'ssm_log_dt', 'ssm_b_re', 'ssm_b_im', 'ssm_c_re', 'ssm_c_im', 'ssm_d', 'ssm_w_glu', 'ssm_b_glu',
           'attn_out_g', 'ssm_out_g', 'w_out', 'norm_x_g', 'mem_norm_g', 'w_xq', 'w_xkv', 'w_xo', 'norm_ffn_g',
           'w_gate', 'w_up', 'w_down', 'final_norm_g']
SHARDED = {'w_in': 1, 'w_uq': 2, 'w_ukv': 2, 'ssm_w_glu': 1, 'w_out': 1, 'w_xq': 1, 'w_xkv': 2, 'w_xo': 1,
           'w_gate': 2, 'w_up': 2, 'w_down': 1}
SMALL = [n for n in WEIGHTS if n not in SHARDED]
MESH = pl.DeviceIdType.MESH


def _pc(body, **kw):
    return pl.pallas_call(body, **kw)


def _pick(n, prefs):
    for p in prefs:
        if n % p == 0:
            return p
    return n


def _cp(sem=None):
    return pltpu.CompilerParams(dimension_semantics=sem, vmem_limit_bytes=VMEM_LIMIT)


def _mm(pairs, mode, out_dtype, res=None, name="mm"):
    a0, b0 = pairs[0]
    if mode == 'nn':
        (M, K), N = a0.shape, b0.shape[1]
        dims = (((1,), (0,)), ((), ()))
    elif mode == 'nt':
        (M, K), N = a0.shape, b0.shape[0]
        dims = (((1,), (1,)), ((), ()))
    else:
        (K, M), N = a0.shape, b0.shape[1]
        dims = (((0,), (0,)), ((), ()))
    tm = _pick(M, (512, 256, 128))
    tn = _pick(N, (512, 256, 128))
    tk = _pick(K, (512, 256, 128))
    nk = K // tk
    npair = len(pairs)
    has_res = res is not None

    def body(*refs):
        ins = refs[:2 * npair]
        res_ref = refs[2 * npair] if has_res else None
        o_ref, acc = refs[2 * npair + has_res], refs[2 * npair + has_res + 1]
        k = pl.program_id(2)

        @pl.when(k == 0)
        def _():
            acc[...] = jnp.zeros_like(acc)

        s = None
        for p in range(npair):
            d = lax.dot_general(ins[2 * p][...].astype(bf16), ins[2 * p + 1][...].astype(bf16), dims,
                                preferred_element_type=f32)
            s = d if s is None else s + d
        acc[...] += s

        @pl.when(k == nk - 1)
        def _():
            r = acc[...]
            if has_res:
                r = r + res_ref[...]
            o_ref[...] = r.astype(out_dtype)

    if mode == 'nn':
        a_spec = pl.BlockSpec((tm, tk), lambda i, j, k: (i, k))
        b_spec = pl.BlockSpec((tk, tn), lambda i, j, k: (k, j))
    elif mode == 'nt':
        a_spec = pl.BlockSpec((tm, tk), lambda i, j, k: (i, k))
        b_spec = pl.BlockSpec((tn, tk), lambda i, j, k: (j, k))
    else:
        a_spec = pl.BlockSpec((tk, tm), lambda i, j, k: (k, i))
        b_spec = pl.BlockSpec((tk, tn), lambda i, j, k: (k, j))
    o_spec = pl.BlockSpec((tm, tn), lambda i, j, k: (i, j))
    in_specs = [a_spec, b_spec] * npair + ([o_spec] if has_res else [])
    args = [t for p in pairs for t in p] + ([res] if has_res else [])
    return _pc(body, grid=(M // tm, N // tn, nk), in_specs=in_specs, out_specs=o_spec,
               out_shape=jax.ShapeDtypeStruct((M, N), out_dtype),
               scratch_shapes=[pltpu.VMEM((tm, tn), f32)],
               compiler_params=_cp(("parallel", "parallel", "arbitrary")), name=name)(*args)


def _rms_fwd(x, g, *, col0=0, width=None, n_valid=None, out_dtype=bf16, name="rms_fwd"):
    S = x.shape[0]
    width = width or x.shape[1]
    n_valid = n_valid or width
    ts = _pick(S, (256, 128))
    cb = col0 // width

    def body(x_ref, g_ref, o_ref):
        xv = x_ref[...]
        ms = jnp.sum(xv * xv, axis=-1, keepdims=True) * (1.0 / n_valid)
        o_ref[...] = (xv * lax.rsqrt(ms + EPS) * g_ref[...]).astype(out_dtype)

    return _pc(body, grid=(S // ts,),
               in_specs=[pl.BlockSpec((ts, width), lambda i: (i, cb)), pl.BlockSpec((1, width), lambda i: (0, 0))],
               out_specs=pl.BlockSpec((ts, width), lambda i: (i, 0)),
               out_shape=jax.ShapeDtypeStruct((S, width), out_dtype),
               compiler_params=_cp(("parallel",)), name=name)(x, g.reshape(1, width))


def _rms_bwd(x, g, dy, *, col0=0, dcol0=0, width=None, n_valid=None, res=None, out_dtype=f32, delta=False,
             name="rms_bwd"):
    S = x.shape[0]
    width = width or x.shape[1]
    n_valid = n_valid or width
    ts = _pick(S, (256, 128))
    cb, dcb = col0 // width, dcol0 // width
    has_res = res is not None

    def body(*refs):
        x_ref, g_ref, dy_ref = refs[:3]
        res_ref = refs[3] if has_res else None
        outs = refs[3 + has_res:]
        dx_ref, dg_ref = outs[0], outs[1]
        i = pl.program_id(0)
        xv = x_ref[...]
        gv = g_ref[...]
        dyv = dy_ref[...].astype(f32)
        rstd = lax.rsqrt(jnp.sum(xv * xv, axis=-1, keepdims=True) * (1.0 / n_valid) + EPS)
        xh = xv * rstd
        dxh = dyv * gv
        mean = jnp.sum(dxh * xh, axis=-1, keepdims=True) * (1.0 / n_valid)
        dx = rstd * (dxh - xh * mean)
        if delta:
            d_ref = outs[2]
            for h in range(width // LANES):
                sl = slice(h * LANES, (h + 1) * LANES)
                dsum = jnp.sum(dx[:, sl] * xv[:, sl], axis=-1, keepdims=True)
                d_ref[:, sl] = jnp.broadcast_to(dsum, (ts, LANES))
        if has_res:
            dx = dx + res_ref[...]
        dx_ref[...] = dx.astype(out_dtype)

        @pl.when(i == 0)
        def _():
            dg_ref[...] = jnp.zeros_like(dg_ref)

        dg_ref[...] += jnp.sum(dyv * xh, axis=0, keepdims=True)

    blk = lambda c: pl.BlockSpec((ts, width), lambda i: (i, c))
    in_specs = [blk(cb), pl.BlockSpec((1, width), lambda i: (0, 0)), blk(dcb)] + ([blk(0)] if has_res else [])
    out_specs = [blk(0), pl.BlockSpec((1, width), lambda i: (0, 0))] + ([blk(0)] if delta else [])
    out_shape = [jax.ShapeDtypeStruct((S, width), out_dtype), jax.ShapeDtypeStruct((1, width), f32)] + (
        [jax.ShapeDtypeStruct((S, width), f32)] if delta else [])
    args = [x, g.reshape(1, width), dy] + ([res] if has_res else [])
    return _pc(body, grid=(S // ts,), in_specs=in_specs, out_specs=out_specs, out_shape=out_shape,
               compiler_params=_cp(("arbitrary",)), name=name)(*args)


def _loss_head(h, g, target, name="loss_head"):
    S, D = h.shape
    ts = _pick(S, (256, 128))

    def body(h_ref, g_ref, t_ref, dh_ref, dg_ref, loss_ref):
        i = pl.program_id(0)
        xv = h_ref[...]
        gv = g_ref[...]
        rstd = lax.rsqrt(jnp.sum(xv * xv, axis=-1, keepdims=True) * (1.0 / D) + EPS)
        xh = xv * rstd
        err = xh * gv - t_ref[...]
        dyv = err * (1.0 / D)
        dxh = dyv * gv
        mean = jnp.sum(dxh * xh, axis=-1, keepdims=True) * (1.0 / D)
        dh_ref[...] = rstd * (dxh - xh * mean)

        @pl.when(i == 0)
        def _():
            dg_ref[...] = jnp.zeros_like(dg_ref)
            loss_ref[...] = jnp.zeros_like(loss_ref)

        dg_ref[...] += jnp.sum(dyv * xh, axis=0, keepdims=True)
        part = jnp.sum(jnp.sum(err * err, axis=-1, keepdims=True), axis=0, keepdims=True) * (0.5 / D)
        loss_ref[...] += jnp.broadcast_to(part, (1, LANES))

    blk = pl.BlockSpec((ts, D), lambda i: (i, 0))
    row = pl.BlockSpec((1, D), lambda i: (0, 0))
    return _pc(body, grid=(S // ts,), in_specs=[blk, row, blk],
               out_specs=[blk, row, pl.BlockSpec((1, LANES), lambda i: (0, 0))],
               out_shape=[jax.ShapeDtypeStruct((S, D), f32), jax.ShapeDtypeStruct((1, D), f32),
                          jax.ShapeDtypeStruct((1, LANES), f32)],
               compiler_params=_cp(("arbitrary",)), name=name)(h, g.reshape(1, D), target)


def _rope_apply(x, tc, s1, s2):
    return x * tc + pltpu.roll(x, LANES - 16, 1) * s1 + pltpu.roll(x, 16, 1) * s2


def _rope_apply_t(dy, tc, s1, s2):
    return dy * tc + pltpu.roll(dy * s1, 16, 1) + pltpu.roll(dy * s2, LANES - 16, 1)


def _rope_fwd(q, kv, proj, tabs, name="rope_fwd"):
    S = q.shape[0]
    ts = _pick(S, (256, 128))
    scale = (QK_NOPE + QK_ROPE) ** -0.5

    def body(q_ref, kk_ref, kvv_ref, kr_ref, tc_ref, s1_ref, s2_ref, qh_ref, kh_ref, vh_ref):
        tc, s1, s2 = tc_ref[...], s1_ref[...], s2_ref[...]
        krr = _rope_apply(pltpu.roll(kr_ref[...], QK_NOPE, 1), tc, s1, s2)
        for h in range(MLA_HEADS):
            sl = slice(h * HEAD_PAD, (h + 1) * HEAD_PAD)
            qh_ref[:, sl] = (_rope_apply(q_ref[:, sl], tc, s1, s2) * scale).astype(bf16)
            kh_ref[:, sl] = (kk_ref[:, sl] + krr).astype(bf16)
        vh_ref[...] = kvv_ref[...].astype(bf16)

    wide = lambda c: pl.BlockSpec((ts, MLA_PAD), lambda i: (i, c))
    tab = pl.BlockSpec((ts, LANES), lambda i: (i, 0))
    return _pc(body, grid=(S // ts,),
               in_specs=[wide(0), wide(0), wide(1), pl.BlockSpec((ts, LANES), lambda i: (i, 3)), tab, tab, tab],
               out_specs=[wide(0)] * 3, out_shape=[jax.ShapeDtypeStruct((S, MLA_PAD), bf16)] * 3,
               compiler_params=_cp(("parallel",)), name=name)(q, kv, kv, proj, *tabs)


def _rope_bwd(dqh, dkh, dvh, tabs, name="rope_bwd"):
    S = dqh.shape[0]
    ts = _pick(S, (256, 128))
    scale = (QK_NOPE + QK_ROPE) ** -0.5

    def body(dq_ref, dk_ref, dv_ref, tc_ref, s1_ref, s2_ref, oq_ref, okv_ref, okr_ref):
        tc, s1, s2 = tc_ref[...], s1_ref[...], s2_ref[...]
        ksum = None
        for h in range(MLA_HEADS):
            sl = slice(h * HEAD_PAD, (h + 1) * HEAD_PAD)
            oq_ref[:, sl] = (_rope_apply_t(dq_ref[:, sl], tc, s1, s2) * scale).astype(bf16)
            dk = dk_ref[:, sl]
            okv_ref[:, sl] = dk.astype(bf16)
            ksum = dk if ksum is None else ksum + dk
        okv_ref[:, MLA_PAD:] = dv_ref[...].astype(bf16)
        dkr = pltpu.roll(_rope_apply_t(ksum, tc, s1, s2), LANES - QK_NOPE, 1)
        lane = lax.broadcasted_iota(jnp.int32, (ts, LANES), 1)
        okr_ref[...] = jnp.where(lane < QK_ROPE, dkr, 0.0).astype(bf16)

    wide = pl.BlockSpec((ts, MLA_PAD), lambda i: (i, 0))
    tab = pl.BlockSpec((ts, LANES), lambda i: (i, 0))
    return _pc(body, grid=(S // ts,), in_specs=[wide, wide, wide, tab, tab, tab],
               out_specs=[wide, pl.BlockSpec((ts, 2 * MLA_PAD), lambda i: (i, 0)), tab],
               out_shape=[jax.ShapeDtypeStruct((S, MLA_PAD), bf16), jax.ShapeDtypeStruct((S, 2 * MLA_PAD), bf16),
                          jax.ShapeDtypeStruct((S, LANES), bf16)],
               compiler_params=_cp(("parallel",)), name=name)(dqh, dkh, dvh, *tabs)


def _attn_fwd(qh, kh, vh, name="attn_fwd"):
    S = qh.shape[0]
    tq = tk = _pick(S, (256, 128))
    nq, nk = S // tq, S // tk

    def body(q_ref, k_ref, v_ref, o_ref, lse_ref, m_sc, l_sc, acc_sc):
        i, j = pl.program_id(1), pl.program_id(2)

        @pl.when(j == 0)
        def _():
            m_sc[...] = jnp.full_like(m_sc, -1e30)
            l_sc[...] = jnp.zeros_like(l_sc)
            acc_sc[...] = jnp.zeros_like(acc_sc)

        @pl.when(j <= i)
        def _():
            s = lax.dot_general(q_ref[...], k_ref[...], (((1,), (1,)), ((), ())), preferred_element_type=f32)
            row = i * tq + lax.broadcasted_iota(jnp.int32, (tq, tk), 0)
            col = j * tk + lax.broadcasted_iota(jnp.int32, (tq, tk), 1)
            s = jnp.where(col <= row, s, -1e30)
            m_prev = m_sc[...]
            m_new = jnp.maximum(m_prev, jnp.max(s, axis=-1, keepdims=True))
            alpha = jnp.exp(m_prev - m_new)
            p = jnp.exp(s - m_new)
            l_sc[...] = alpha * l_sc[...] + jnp.sum(p, axis=-1, keepdims=True)
            acc_sc[...] = alpha * acc_sc[...] + jnp.dot(p.astype(bf16), v_ref[...], preferred_element_type=f32)
            m_sc[...] = m_new

        @pl.when(j == nk - 1)
        def _():
            l = l_sc[...]
            o_ref[...] = acc_sc[...] / l
            lse_ref[...] = jnp.broadcast_to(m_sc[...] + jnp.log(l), (tq, LANES))

    qspec = pl.BlockSpec((tq, HEAD_PAD), lambda h, i, j: (i, h))
    kspec = pl.BlockSpec((tk, HEAD_PAD), lambda h, i, j: (jnp.minimum(j, i), h))
    return _pc(body, grid=(MLA_HEADS, nq, nk), in_specs=[qspec, kspec, kspec], out_specs=[qspec, qspec],
               out_shape=[jax.ShapeDtypeStruct((S, MLA_PAD), f32)] * 2,
               scratch_shapes=[pltpu.VMEM((tq, 1), f32), pltpu.VMEM((tq, 1), f32), pltpu.VMEM((tq, HEAD_PAD), f32)],
               compiler_params=_cp(("parallel", "parallel", "arbitrary")), name=name)(qh, kh, vh)


def _attn_bwd_dq(qh, kh, vh, do, lse, delta, name="attn_bwd_dq"):
    S = qh.shape[0]
    tq = tk = _pick(S, (256, 128))
    nq, nk = S // tq, S // tk

    def body(q_ref, k_ref, v_ref, do_ref, lse_ref, dl_ref, dq_ref, acc):
        i, j = pl.program_id(1), pl.program_id(2)

        @pl.when(j == 0)
        def _():
            acc[...] = jnp.zeros_like(acc)

        @pl.when(j <= i)
        def _():
            s = lax.dot_general(q_ref[...], k_ref[...], (((1,), (1,)), ((), ())), preferred_element_type=f32)
            row = i * tq + lax.broadcasted_iota(jnp.int32, (tq, tk), 0)
            col = j * tk + lax.broadcasted_iota(jnp.int32, (tq, tk), 1)
            p = jnp.where(col <= row, jnp.exp(s - lse_ref[:, :1]), 0.0)
            dp = lax.dot_general(do_ref[...], v_ref[...], (((1,), (1,)), ((), ())), preferred_element_type=f32)
            ds = p * (dp - dl_ref[:, :1])
            acc[...] += jnp.dot(ds.astype(bf16), k_ref[...], preferred_element_type=f32)

        @pl.when(j == nk - 1)
        def _():
            dq_ref[...] = acc[...]

    qspec = pl.BlockSpec((tq, HEAD_PAD), lambda h, i, j: (i, h))
    kspec = pl.BlockSpec((tk, HEAD_PAD), lambda h, i, j: (jnp.minimum(j, i), h))
    return _pc(body, grid=(MLA_HEADS, nq, nk), in_specs=[qspec, kspec, kspec, qspec, qspec, qspec], out_specs=qspec,
               out_shape=jax.ShapeDtypeStruct((S, MLA_PAD), f32),
               scratch_shapes=[pltpu.VMEM((tq, HEAD_PAD), f32)],
               compiler_params=_cp(("parallel", "parallel", "arbitrary")), name=name)(qh, kh, vh, do, lse, delta)


def _attn_bwd_dkv(qh, kh, vh, do, lse, delta, name="attn_bwd_dkv"):
    S = qh.shape[0]
    tq = tk = _pick(S, (256, 128))
    nq, nk = S // tq, S // tk

    def body(q_ref, k_ref, v_ref, do_ref, lse_ref, dl_ref, dk_ref, dv_ref, dk_acc, dv_acc):
        j, i = pl.program_id(1), pl.program_id(2)

        @pl.when(i == 0)
        def _():
            dk_acc[...] = jnp.zeros_like(dk_acc)
            dv_acc[...] = jnp.zeros_like(dv_acc)

        @pl.when(i >= j)
        def _():
            s = lax.dot_general(q_ref[...], k_ref[...], (((1,), (1,)), ((), ())), preferred_element_type=f32)
            row = i * tq + lax.broadcasted_iota(jnp.int32, (tq, tk), 0)
            col = j * tk + lax.broadcasted_iota(jnp.int32, (tq, tk), 1)
            p = jnp.where(col <= row, jnp.exp(s - lse_ref[:, :1]), 0.0)
            dov = do_ref[...]
            dp = lax.dot_general(dov, v_ref[...], (((1,), (1,)), ((), ())), preferred_element_type=f32)
            ds = p * (dp - dl_ref[:, :1])
            tn = (((0,), (0,)), ((), ()))
            dv_acc[...] += lax.dot_general(p.astype(bf16), dov, tn, preferred_element_type=f32)
            dk_acc[...] += lax.dot_general(ds.astype(bf16), q_ref[...], tn, preferred_element_type=f32)

        @pl.when(i == nq - 1)
        def _():
            dk_ref[...] = dk_acc[...]
            dv_ref[...] = dv_acc[...]

    qspec = pl.BlockSpec((tq, HEAD_PAD), lambda h, j, i: (jnp.maximum(i, j), h))
    kspec = pl.BlockSpec((tk, HEAD_PAD), lambda h, j, i: (j, h))
    return _pc(body, grid=(MLA_HEADS, nk, nq), in_specs=[qspec, kspec, kspec, qspec, qspec, qspec],
               out_specs=[kspec, kspec], out_shape=[jax.ShapeDtypeStruct((S, MLA_PAD), f32)] * 2,
               scratch_shapes=[pltpu.VMEM((tk, HEAD_PAD), f32)] * 2,
               compiler_params=_cp(("parallel", "parallel", "arbitrary")), name=name)(qh, kh, vh, do, lse, delta)


def _xattn_fwd(q, kv, name="xattn_fwd"):
    S = q.shape[0]
    M = kv.shape[0]
    tq = _pick(S, (256, 128))
    scale = X_HEAD_DIM ** -0.5

    def body(q_ref, kv_ref, o_ref):
        for h in range(X_HEADS):
            sl = slice(h * X_HEAD_DIM, (h + 1) * X_HEAD_DIM)
            k = kv_ref[:, sl]
            v = kv_ref[:, D_MODEL + h * X_HEAD_DIM:D_MODEL + (h + 1) * X_HEAD_DIM]
            s = lax.dot_general(q_ref[:, sl], k, (((1,), (1,)), ((), ())), preferred_element_type=f32) * scale
            e = jnp.exp(s - jnp.max(s, axis=-1, keepdims=True))
            p = e / jnp.sum(e, axis=-1, keepdims=True)
            o_ref[:, sl] = jnp.dot(p.astype(bf16), v, preferred_element_type=f32).astype(bf16)

    blk = pl.BlockSpec((tq, D_MODEL), lambda i: (i, 0))
    return _pc(body, grid=(S // tq,), in_specs=[blk, pl.BlockSpec((M, 2 * D_MODEL), lambda i: (0, 0))],
               out_specs=blk, out_shape=jax.ShapeDtypeStruct((S, D_MODEL), bf16),
               compiler_params=_cp(("parallel",)), name=name)(q, kv)


def _xattn_bwd(q, kv, do, name="xattn_bwd"):
    S = q.shape[0]
    M = kv.shape[0]
    tq = _pick(S, (256, 128))
    scale = X_HEAD_DIM ** -0.5

    def body(q_ref, kv_ref, do_ref, dq_ref, dkv_ref):
        i = pl.program_id(0)

        @pl.when(i == 0)
        def _():
            dkv_ref[...] = jnp.zeros_like(dkv_ref)

        nt = (((1,), (1,)), ((), ()))
        tn = (((0,), (0,)), ((), ()))
        for h in range(X_HEADS):
            sl = slice(h * X_HEAD_DIM, (h + 1) * X_HEAD_DIM)
            vsl = slice(D_MODEL + h * X_HEAD_DIM, D_MODEL + (h + 1) * X_HEAD_DIM)
            k, v, qv, dov = kv_ref[:, sl], kv_ref[:, vsl], q_ref[:, sl], do_ref[:, sl]
            s = lax.dot_general(qv, k, nt, preferred_element_type=f32) * scale
            e = jnp.exp(s - jnp.max(s, axis=-1, keepdims=True))
            p = e / jnp.sum(e, axis=-1, keepdims=True)
            dp = lax.dot_general(dov, v, nt, preferred_element_type=f32)
            ds = (p * (dp - jnp.sum(dp * p, axis=-1, keepdims=True)) * scale).astype(bf16)
            dq_ref[:, sl] = jnp.dot(ds, k, preferred_element_type=f32).astype(bf16)
            dkv_ref[:, sl] += lax.dot_general(ds, qv, tn, preferred_element_type=f32)
            dkv_ref[:, vsl] += lax.dot_general(p.astype(bf16), dov, tn, preferred_element_type=f32)

    blk = pl.BlockSpec((tq, D_MODEL), lambda i: (i, 0))
    full = pl.BlockSpec((M, 2 * D_MODEL), lambda i: (0, 0))
    return _pc(body, grid=(S // tq,), in_specs=[blk, full, blk], out_specs=[blk, full],
               out_shape=[jax.ShapeDtypeStruct((S, D_MODEL), bf16), jax.ShapeDtypeStruct((M, 2 * D_MODEL), f32)],
               compiler_params=_cp(("arbitrary",)), name=name)(q, kv, do)


def _apow_init(a_ref, ap_ref):
    ar, ai = a_ref[:, :MACRO_ST], a_ref[:, MACRO_ST:]
    pr, pi = ar, ai
    for r in range(SUBLANES):
        ap_ref[r:r + 1, :MACRO_ST] = pr
        ap_ref[r:r + 1, MACRO_ST:] = pi
        pr, pi = pr * ar - pi * ai, pr * ai + pi * ar


def _scan_block(src_ref, dst_ref, ap_ref, carry, n_rows, reverse):
    P = MACRO_ST
    n_chunks = n_rows // SUBLANES
    rid = lax.broadcasted_iota(jnp.int32, (SUBLANES, P), 0)
    sgn = -1.0 if reverse else 1.0
    if reverse:
        apr = jnp.zeros((SUBLANES, P), f32)
        api = jnp.zeros((SUBLANES, P), f32)
        for r in range(SUBLANES):
            apr = jnp.where(rid == r, ap_ref[SUBLANES - 1 - r:SUBLANES - r, :P], apr)
            api = jnp.where(rid == r, -ap_ref[SUBLANES - 1 - r:SUBLANES - r, P:], api)
    else:
        apr, api = ap_ref[:, :P], ap_ref[:, P:]

    def chunk(c, carry):
        cr, ci = carry
        r0 = pl.multiple_of(((n_chunks - 1 - c) if reverse else c) * SUBLANES, SUBLANES)
        xr = src_ref[pl.ds(r0, SUBLANES), :P]
        xi = src_ref[pl.ds(r0, SUBLANES), P:]
        for sh in (1, 2, 4):
            pr = ap_ref[sh - 1:sh, :P]
            pi = ap_ref[sh - 1:sh, P:] * sgn
            if reverse:
                sr = jnp.where(rid < SUBLANES - sh, pltpu.roll(xr, SUBLANES - sh, 0), 0.0)
                si = jnp.where(rid < SUBLANES - sh, pltpu.roll(xi, SUBLANES - sh, 0), 0.0)
            else:
                sr = jnp.where(rid >= sh, pltpu.roll(xr, sh, 0), 0.0)
                si = jnp.where(rid >= sh, pltpu.roll(xi, sh, 0), 0.0)
            xr, xi = xr + pr * sr - pi * si, xi + pr * si + pi * sr
        xr = xr + apr * cr - api * ci
        xi = xi + apr * ci + api * cr
        dst_ref[pl.ds(r0, SUBLANES), :P] = xr
        dst_ref[pl.ds(r0, SUBLANES), P:] = xi
        edge = 0 if reverse else SUBLANES - 1
        ncr = jnp.sum(jnp.where(rid == edge, xr, 0.0), axis=0, keepdims=True)
        nci = jnp.sum(jnp.where(rid == edge, xi, 0.0), axis=0, keepdims=True)
        return ncr, nci

    return lax.fori_loop(0, n_chunks, chunk, carry)


def _ssm_fwd(proj, bm, cm, a, d, name="ssm_fwd"):
    S = proj.shape[0]
    tS = _pick(S, (256, 128))
    nb = S // tS
    P2 = 2 * MACRO_ST
    ucol0 = (D_MODEL - SSM_WIDTH) // MACRO_CH

    def body(u_ref, b_ref, c_ref, a_ref, d_ref, y_ref, xc_ref, bu_sc, ap_sc, car_sc):
        t = pl.program_id(1)

        @pl.when(t == 0)
        def _():
            _apow_init(a_ref, ap_sc)
            car_sc[...] = jnp.zeros_like(car_sc)

        uv = u_ref[...]
        bu_sc[...] = jnp.dot(uv.astype(bf16), b_ref[...], preferred_element_type=f32)
        xc_ref[...] = car_sc[...]
        cr, ci = _scan_block(bu_sc, bu_sc, ap_sc, (car_sc[:, :MACRO_ST], car_sc[:, MACRO_ST:]), tS, False)
        car_sc[:, :MACRO_ST] = cr
        car_sc[:, MACRO_ST:] = ci
        y_ref[...] = jnp.dot(bu_sc[...].astype(bf16), c_ref[...], preferred_element_type=f32) + d_ref[...] * uv

    return _pc(body, grid=(SSM_MACRO, nb),
               in_specs=[pl.BlockSpec((tS, MACRO_CH), lambda m, t: (t, ucol0 + m)),
                         pl.BlockSpec((None, MACRO_CH, P2), lambda m, t: (m, 0, 0)),
                         pl.BlockSpec((None, P2, MACRO_CH), lambda m, t: (m, 0, 0)),
                         pl.BlockSpec((None, 1, P2), lambda m, t: (m, 0, 0)),
                         pl.BlockSpec((1, MACRO_CH), lambda m, t: (0, m))],
               out_specs=[pl.BlockSpec((tS, MACRO_CH), lambda m, t: (t, m)),
                          pl.BlockSpec((None, None, 1, P2), lambda m, t: (m, t, 0, 0))],
               out_shape=[jax.ShapeDtypeStruct((S, SSM_WIDTH), f32), jax.ShapeDtypeStruct((SSM_MACRO, nb, 1, P2), f32)],
               scratch_shapes=[pltpu.VMEM((tS, P2), f32), pltpu.VMEM((SUBLANES, P2), f32), pltpu.VMEM((1, P2), f32)],
               compiler_params=_cp(("arbitrary", "arbitrary")), name=name)(proj, bm, cm, a, d.reshape(1, SSM_WIDTH))


def _ssm_bwd(proj, dy, xc, bm, cm, a, d, name="ssm_bwd"):
    S = proj.shape[0]
    tS = _pick(S, (256, 128))
    nb = S // tS
    P = MACRO_ST
    P2 = 2 * P
    ucol0 = (D_MODEL - SSM_WIDTH) // MACRO_CH

    def body(u_ref, dy_ref, xc_ref, b_ref, c_ref, a_ref, d_ref, du_ref, db_ref, dc_ref, da_ref, dd_ref,
             x_sc, g_sc, ap_sc, gcar_sc):
        t = pl.program_id(1)

        @pl.when(t == 0)
        def _():
            _apow_init(a_ref, ap_sc)
            gcar_sc[...] = jnp.zeros_like(gcar_sc)
            db_ref[...] = jnp.zeros_like(db_ref)
            dc_ref[...] = jnp.zeros_like(dc_ref)
            da_ref[...] = jnp.zeros_like(da_ref)
            dd_ref[...] = jnp.zeros_like(dd_ref)

        nt = (((1,), (1,)), ((), ()))
        tn = (((0,), (0,)), ((), ()))
        uv = u_ref[...]
        ub = uv.astype(bf16)
        dyv = dy_ref[...]
        dyb = dyv.astype(bf16)
        x_sc[...] = jnp.dot(ub, b_ref[...], preferred_element_type=f32)
        xin = xc_ref[...]
        _scan_block(x_sc, x_sc, ap_sc, (xin[:, :P], xin[:, P:]), tS, False)
        g_sc[...] = lax.dot_general(dyb, c_ref[...], nt, preferred_element_type=f32)
        gr, gi = _scan_block(g_sc, g_sc, ap_sc, (gcar_sc[:, :P], gcar_sc[:, P:]), tS, True)
        gcar_sc[:, :P] = gr
        gcar_sc[:, P:] = gi
        xv = x_sc[...]
        gv = g_sc[...]
        gb = gv.astype(bf16)
        dc_ref[...] += lax.dot_general(xv.astype(bf16), dyb, tn, preferred_element_type=f32)
        db_ref[...] += lax.dot_general(ub, gb, tn, preferred_element_type=f32)
        du_ref[...] = lax.dot_general(gb, b_ref[...], nt, preferred_element_type=f32) + d_ref[...] * dyv
        dd_ref[...] += jnp.sum(dyv * uv, axis=0, keepdims=True)
        rid = lax.broadcasted_iota(jnp.int32, (tS, P2), 0)
        xp = jnp.where(rid == 0, jnp.broadcast_to(xin, (tS, P2)), pltpu.roll(xv, 1, 0))
        xpr, xpi, ggr, ggi = xp[:, :P], xp[:, P:], gv[:, :P], gv[:, P:]
        da_ref[:, :P] += jnp.sum(ggr * xpr + ggi * xpi, axis=0, keepdims=True)
        da_ref[:, P:] += jnp.sum(ggi * xpr - ggr * xpi, axis=0, keepdims=True)

    rev = lambda t: nb - 1 - t
    return _pc(body, grid=(SSM_MACRO, nb),
               in_specs=[pl.BlockSpec((tS, MACRO_CH), lambda m, t: (rev(t), ucol0 + m)),
                         pl.BlockSpec((tS, MACRO_CH), lambda m, t: (rev(t), m)),
                         pl.BlockSpec((None, None, 1, P2), lambda m, t: (m, rev(t), 0, 0)),
                         pl.BlockSpec((None, MACRO_CH, P2), lambda m, t: (m, 0, 0)),
                         pl.BlockSpec((None, P2, MACRO_CH), lambda m, t: (m, 0, 0)),
                         pl.BlockSpec((None, 1, P2), lambda m, t: (m, 0, 0)),
                         pl.BlockSpec((1, MACRO_CH), lambda m, t: (0, m))],
               out_specs=[pl.BlockSpec((tS, MACRO_CH), lambda m, t: (rev(t), m)),
                          pl.BlockSpec((None, MACRO_CH, P2), lambda m, t: (m, 0, 0)),
                          pl.BlockSpec((None, P2, MACRO_CH), lambda m, t: (m, 0, 0)),
                          pl.BlockSpec((None, 1, P2), lambda m, t: (m, 0, 0)),
                          pl.BlockSpec((1, MACRO_CH), lambda m, t: (0, m))],
               out_shape=[jax.ShapeDtypeStruct((S, SSM_WIDTH), f32),
                          jax.ShapeDtypeStruct((SSM_MACRO, MACRO_CH, P2), f32),
                          jax.ShapeDtypeStruct((SSM_MACRO, P2, MACRO_CH), f32),
                          jax.ShapeDtypeStruct((SSM_MACRO, 1, P2), f32),
                          jax.ShapeDtypeStruct((1, SSM_WIDTH), f32)],
               scratch_shapes=[pltpu.VMEM((tS, P2), f32), pltpu.VMEM((tS, P2), f32),
                               pltpu.VMEM((SUBLANES, P2), f32), pltpu.VMEM((1, P2), f32)],
               compiler_params=_cp(("arbitrary", "arbitrary")), name=name)(
        proj, dy, xc, bm, cm, a, d.reshape(1, SSM_WIDTH))


_GELU_K = math.sqrt(2.0 / math.pi)
_GELU_C = 0.044715


def _glu_fwd(y, w, b, g, name="glu_fwd"):
    S, W = y.shape
    ts = _pick(S, (256, 128))

    def body(y_ref, w_ref, b_ref, g_ref, z_ref, sn_ref, ge_ref):
        yv = y_ref[...]
        cdf = 0.5 * (1.0 + jnp.tanh(_GELU_K * (yv + _GELU_C * (yv * yv * yv))))
        ge = (yv * cdf).astype(bf16)
        z = jnp.dot(ge, w_ref[...], preferred_element_type=f32) + b_ref[...]
        s = yv * jax.nn.sigmoid(z)
        rstd = lax.rsqrt(jnp.sum(s * s, axis=-1, keepdims=True) * (1.0 / W) + EPS)
        z_ref[...] = z
        sn_ref[...] = (s * rstd * g_ref[...]).astype(bf16)
        ge_ref[...] = ge

    blk = pl.BlockSpec((ts, W), lambda i: (i, 0))
    row = pl.BlockSpec((1, W), lambda i: (0, 0))
    return _pc(body, grid=(S // ts,), in_specs=[blk, pl.BlockSpec((W, W), lambda i: (0, 0)), row, row],
               out_specs=[blk, blk, blk],
               out_shape=[jax.ShapeDtypeStruct((S, W), f32), jax.ShapeDtypeStruct((S, W), bf16),
                          jax.ShapeDtypeStruct((S, W), bf16)],
               compiler_params=_cp(("parallel",)), name=name)(y, w, b.reshape(1, W), g.reshape(1, W))


def _glu_bwd(y, z, dmixed, w, g, name="glu_bwd"):
    S, W = y.shape
    ts = _pick(S, (256, 128))
    dcb = MLA_PAD // W

    def body(y_ref, z_ref, dsn_ref, w_ref, g_ref, dy_ref, dz_ref, dg_ref, db_ref):
        i = pl.program_id(0)
        yv, zv, gv = y_ref[...], z_ref[...], g_ref[...]
        sig = jax.nn.sigmoid(zv)
        s = yv * sig
        rstd = lax.rsqrt(jnp.sum(s * s, axis=-1, keepdims=True) * (1.0 / W) + EPS)
        sh = s * rstd
        dsn = dsn_ref[...]
        dsh = dsn * gv
        ds = rstd * (dsh - sh * (jnp.sum(dsh * sh, axis=-1, keepdims=True) * (1.0 / W)))
        dz = ds * s * (1.0 - sig)
        dzb = dz.astype(bf16)
        dge = lax.dot_general(dzb, w_ref[...], (((1,), (1,)), ((), ())), preferred_element_type=f32)
        t = jnp.tanh(_GELU_K * (yv + _GELU_C * (yv * yv * yv)))
        dgelu = 0.5 * (1.0 + t) + 0.5 * yv * (1.0 - t * t) * _GELU_K * (1.0 + 3.0 * _GELU_C * yv * yv)
        dy_ref[...] = ds * sig + dge * dgelu
        dz_ref[...] = dzb

        @pl.when(i == 0)
        def _():
            dg_ref[...] = jnp.zeros_like(dg_ref)
            db_ref[...] = jnp.zeros_like(db_ref)

        dg_ref[...] += jnp.sum(dsn * sh, axis=0, keepdims=True)
        db_ref[...] += jnp.sum(dz, axis=0, keepdims=True)

    blk = pl.BlockSpec((ts, W), lambda i: (i, 0))
    row = pl.BlockSpec((1, W), lambda i: (0, 0))
    return _pc(body, grid=(S // ts,),
               in_specs=[blk, blk, pl.BlockSpec((ts, W), lambda i: (i, dcb)), pl.BlockSpec((W, W), lambda i: (0, 0)), row],
               out_specs=[blk, blk, row, row],
               out_shape=[jax.ShapeDtypeStruct((S, W), f32), jax.ShapeDtypeStruct((S, W), bf16),
                          jax.ShapeDtypeStruct((1, W), f32), jax.ShapeDtypeStruct((1, W), f32)],
               compiler_params=_cp(("arbitrary",)), name=name)(y, z, dmixed, w, g.reshape(1, W))


def _ffn_up(hn, wg, wu, name="ffn_up"):
    S, K = hn.shape
    F = wg.shape[1]
    tm, tn = _pick(S, (512, 256, 128)), _pick(F, (256, 128))

    def body(h_ref, wg_ref, wu_ref, g_ref, u_ref, a_ref):
        hv = h_ref[...]
        gv = jnp.dot(hv, wg_ref[...], preferred_element_type=f32)
        uv = jnp.dot(hv, wu_ref[...], preferred_element_type=f32)
        g_ref[...] = gv
        u_ref[...] = uv
        a_ref[...] = (gv * jax.nn.sigmoid(gv) * uv).astype(bf16)

    wspec = pl.BlockSpec((K, tn), lambda i, j: (0, j))
    ospec = pl.BlockSpec((tm, tn), lambda i, j: (i, j))
    return _pc(body, grid=(S // tm, F // tn), in_specs=[pl.BlockSpec((tm, K), lambda i, j: (i, 0)), wspec, wspec],
               out_specs=[ospec] * 3,
               out_shape=[jax.ShapeDtypeStruct((S, F), f32), jax.ShapeDtypeStruct((S, F), f32),
                          jax.ShapeDtypeStruct((S, F), bf16)],
               compiler_params=_cp(("parallel", "parallel")), name=name)(hn, wg, wu)


def _ffn_bwd_act(dh, wd, gate, up, name="ffn_bwd_act"):
    S, K = dh.shape
    F = wd.shape[0]
    tm, tn = _pick(S, (512, 256, 128)), _pick(F, (256, 128))

    def body(dh_ref, wd_ref, g_ref, u_ref, dg_ref, du_ref):
        dact = lax.dot_general(dh_ref[...].astype(bf16), wd_ref[...], (((1,), (1,)), ((), ())),
                               preferred_element_type=f32)
        gv, uv = g_ref[...], u_ref[...]
        sig = jax.nn.sigmoid(gv)
        dg_ref[...] = (dact * uv * (sig * (1.0 + gv * (1.0 - sig)))).astype(bf16)
        du_ref[...] = (dact * (gv * sig)).astype(bf16)

    ospec = pl.BlockSpec((tm, tn), lambda i, j: (i, j))
    return _pc(body, grid=(S // tm, F // tn),
               in_specs=[pl.BlockSpec((tm, K), lambda i, j: (i, 0)), pl.BlockSpec((tn, K), lambda i, j: (j, 0)),
                         ospec, ospec],
               out_specs=[ospec] * 2, out_shape=[jax.ShapeDtypeStruct((S, F), bf16)] * 2,
               compiler_params=_cp(("parallel", "parallel")), name=name)(dh, wd, gate, up)


def _pad_heads(w, per_head, pieces):
    K = w.shape[0]
    w3 = w.reshape(K, MLA_HEADS, per_head)
    out = jnp.zeros((K, MLA_HEADS, HEAD_PAD), w.dtype)
    for s0, s1, d0 in pieces:
        out = out.at[:, :, d0:d0 + (s1 - s0)].set(w3[:, :, s0:s1])
    return out.reshape(K, MLA_PAD)


def _unpad_heads(wp, per_head, pieces):
    K = wp.shape[0]
    w3 = wp.reshape(K, MLA_HEADS, HEAD_PAD)
    out = jnp.zeros((K, MLA_HEADS, per_head), wp.dtype)
    for s0, s1, d0 in pieces:
        out = out.at[:, :, s0:s1].set(w3[:, :, d0:d0 + (s1 - s0)])
    return out.reshape(K, MLA_HEADS * per_head)


_Q_PIECES = [(0, QK_NOPE + QK_ROPE, 0)]
_K_PIECES = [(0, QK_NOPE, 0)]
_V_PIECES = [(QK_NOPE, QK_NOPE + V_HEAD, 0)]
_KR0 = Q_LORA + KV_LORA


def _pack_win(w):
    z = jnp.zeros((w.shape[0], LANES - QK_ROPE), w.dtype)
    return jnp.concatenate([w[:, :_KR0 + QK_ROPE], z, w[:, _KR0 + QK_ROPE:]], axis=1)


def _unpack_win(wp):
    return jnp.concatenate([wp[:, :_KR0 + QK_ROPE], wp[:, _KR0 + LANES:]], axis=1)


def _pack_wout(w):
    wa = w[:MLA_WIDTH].reshape(MLA_HEADS, V_HEAD, D_MODEL)
    wa = jnp.concatenate([wa, jnp.zeros_like(wa)], axis=1).reshape(MLA_PAD, D_MODEL)
    return jnp.concatenate([wa, w[MLA_WIDTH:]], axis=0)


def _unpack_wout(wp):
    wa = wp[:MLA_PAD].reshape(MLA_HEADS, HEAD_PAD, D_MODEL)[:, :V_HEAD].reshape(MLA_WIDTH, D_MODEL)
    return jnp.concatenate([wa, wp[MLA_PAD:]], axis=0)


def _pad_gain(g):
    g2 = g.reshape(MLA_HEADS, V_HEAD)
    return jnp.concatenate([g2, jnp.zeros_like(g2)], axis=1).reshape(MLA_PAD)


def _unpad_gain(gp):
    return gp.reshape(MLA_HEADS, HEAD_PAD)[:, :V_HEAD].reshape(MLA_WIDTH)


def _ssm_prep(lam_re, lam_im, log_dt, b_re, b_im, c_re, c_im):
    lam = lax.complex(lam_re, lam_im)
    dt = jnp.exp(log_dt)[:, None]
    a_bar = jnp.exp(lam * dt)
    b_bar = ((a_bar - 1.0) / lam)[..., None] * lax.complex(b_re, b_im)
    G8 = SSM_GROUPS // SSM_MACRO
    eye = jnp.eye(G8, dtype=f32)

    def bmat(part):
        p4 = part.reshape(SSM_MACRO, G8, SSM_STATE, SSM_GROUP)
        return jnp.einsum('mgpc,gh->mgchp', p4, eye).reshape(SSM_MACRO, MACRO_CH, MACRO_ST)

    def cmat(part):
        p4 = part.reshape(SSM_MACRO, G8, SSM_GROUP, SSM_STATE)
        return jnp.einsum('mgcp,gh->mgphc', p4, eye).reshape(SSM_MACRO, MACRO_ST, MACRO_CH)

    bm = jnp.concatenate([bmat(b_bar.real), bmat(b_bar.imag)], axis=2)
    cm = jnp.concatenate([cmat(c_re), -cmat(c_im)], axis=1)
    a4 = a_bar.reshape(SSM_MACRO, 1, MACRO_ST)
    a = jnp.concatenate([a4.real, a4.imag], axis=2)
    return bm, cm, a


def _rope_tables(positions):
    freqs = ROPE_THETA ** (-jnp.arange(0, QK_ROPE, 2, dtype=f32) / QK_ROPE)
    ang = positions.astype(f32)[:, None] * freqs
    cos, sin = jnp.cos(ang), jnp.sin(ang)
    S = positions.shape[0]
    half = QK_ROPE // 2
    one, zero = jnp.ones((S, QK_NOPE), f32), jnp.zeros((S, half), f32)
    z64, z32 = jnp.zeros((S, QK_NOPE), f32), jnp.zeros((S, LANES - QK_NOPE - QK_ROPE), f32)
    tc = jnp.concatenate([one, cos, cos, z32], axis=1)
    s1 = jnp.concatenate([z64, -sin, zero, z32], axis=1)
    s2 = jnp.concatenate([z64, zero, sin, z32], axis=1)
    return tc, s1, s2


def _layer_params(W, l):
    p = {}
    p['win'] = _pack_win(W['w_in'][l])
    p['wuq'] = _pad_heads(W['w_uq'][l], QK_NOPE + QK_ROPE, _Q_PIECES)
    wukv = W['w_ukv'][l]
    p['wukv'] = jnp.concatenate([_pad_heads(wukv, QK_NOPE + V_HEAD, _K_PIECES),
                                 _pad_heads(wukv, QK_NOPE + V_HEAD, _V_PIECES)], axis=1)
    p['wout'] = _pack_wout(W['w_out'][l])
    p['attn_g'] = _pad_gain(W['attn_out_g'][l])
    return p


def _forward_layer(h, memn_in, tabs, W, l, name):
    p = _layer_params(W, l)
    sv = {'h0': h, 'p': p}
    xn = _rms_fwd(h, W['norm_mix_g'][l], name=name + "rms_mix")
    proj = _mm([(xn, p['win'])], 'nn', f32, name=name + "mm_in")
    cqn = _rms_fwd(proj, W['q_norm_g'][l], col0=0, width=Q_LORA, name=name + "rms_q")
    ckvn = _rms_fwd(proj, W['kv_norm_g'][l], col0=Q_LORA, width=KV_LORA, name=name + "rms_kv")
    q = _mm([(cqn, p['wuq'])], 'nn', f32, name=name + "mm_uq")
    kv = _mm([(ckvn, p['wukv'])], 'nn', f32, name=name + "mm_ukv")
    qh, kh, vh = _rope_fwd(q, kv, proj, tabs, name=name + "rope")
    oh, lse = _attn_fwd(qh, kh, vh, name=name + "attn")
    an = _rms_fwd(oh, p['attn_g'], n_valid=MLA_WIDTH, name=name + "rms_attn")
    bm, cm, a = W['ssm'][l]
    bmb, cmb = bm.astype(bf16), cm.astype(bf16)
    y, xc = _ssm_fwd(proj, bmb, cmb, a, W['ssm_d'][l], name=name + "ssm")
    z, sn, ge = _glu_fwd(y, W['ssm_w_glu'][l], W['ssm_b_glu'][l], W['ssm_out_g'][l], name=name + "glu")
    h1a = _mm([(an, p['wout'][:MLA_PAD])], 'nn', f32, res=h, name=name + "mm_out_a")
    h1 = _mm([(sn, p['wout'][MLA_PAD:])], 'nn', f32, res=h1a, name=name + "mm_out_s")
    hn2 = _rms_fwd(h1, W['norm_x_g'][l], name=name + "rms_x")
    memn = _rms_fwd(memn_in, W['mem_norm_g'][l], name=name + "rms_mem")
    qx = _mm([(hn2, W['w_xq'][l])], 'nn', bf16, name=name + "mm_xq")
    kvx = _mm([(memn, W['w_xkv'][l])], 'nn', bf16, name=name + "mm_xkv")
    ox = _xattn_fwd(qx, kvx, name=name + "xattn")
    h2 = _mm([(ox, W['w_xo'][l])], 'nn', f32, res=h1, name=name + "mm_xo")
    hn3 = _rms_fwd(h2, W['norm_ffn_g'][l], name=name + "rms_ffn")
    gate, up, act = _ffn_up(hn3, W['w_gate'][l], W['w_up'][l], name=name + "ffn_up")
    h3 = _mm([(act, W['w_down'][l])], 'nn', f32, res=h2, name=name + "mm_down")
    sv.update(xn=xn, proj=proj, cqn=cqn, ckvn=ckvn, qh=qh, kh=kh, vh=vh, oh=oh, lse=lse, an=an, bmb=bmb, cmb=cmb,
              a=a, y=y, xc=xc, z=z, sn=sn, ge=ge, h1=h1, hn2=hn2, memn=memn, qx=qx, kvx=kvx, ox=ox, h2=h2, hn3=hn3,
              gate=gate, up=up, act=act)
    return h3, sv


def _backward_layer(dh3, sv, memn_in, tabs, W, l, name):
    p = sv['p']
    G = {}
    G['w_down'] = _mm([(sv['act'], dh3)], 'tn', f32, name=name + "dw_down")
    dgate, dup = _ffn_bwd_act(dh3, W['w_down'][l], sv['gate'], sv['up'], name=name + "ffn_bwd_act")
    dhn3 = _mm([(dgate, W['w_gate'][l]), (dup, W['w_up'][l])], 'nt', f32, name=name + "mm_dffn")
    G['w_gate'] = _mm([(sv['hn3'], dgate)], 'tn', f32, name=name + "dw_gate")
    G['w_up'] = _mm([(sv['hn3'], dup)], 'tn', f32, name=name + "dw_up")
    dh2, dg = _rms_bwd(sv['h2'], W['norm_ffn_g'][l], dhn3, res=dh3, name=name + "rmsb_ffn")
    G['norm_ffn_g'] = dg[0]
    G['w_xo'] = _mm([(sv['ox'], dh2)], 'tn', f32, name=name + "dw_xo")
    dox = _mm([(dh2, W['w_xo'][l])], 'nt', bf16, name=name + "mm_dxo")
    dqx, dkvx = _xattn_bwd(sv['qx'], sv['kvx'], dox, name=name + "xattn_bwd")
    G['w_xq'] = _mm([(sv['hn2'], dqx)], 'tn', f32, name=name + "dw_xq")
    G['w_xkv'] = _mm([(sv['memn'], dkvx)], 'tn', f32, name=name + "dw_xkv")
    dhn2 = _mm([(dqx, W['w_xq'][l])], 'nt', f32, name=name + "mm_dxq")
    dmemn = _mm([(dkvx, W['w_xkv'][l])], 'nt', f32, name=name + "mm_dxkv")
    dh1, dg = _rms_bwd(sv['h1'], W['norm_x_g'][l], dhn2, res=dh2, name=name + "rmsb_x")
    G['norm_x_g'] = dg[0]
    _, dg = _rms_bwd(memn_in, W['mem_norm_g'][l], dmemn, name=name + "rmsb_mem")
    G['mem_norm_g'] = dg[0]
    dwo_a = _mm([(sv['an'], dh1)], 'tn', f32, name=name + "dw_out_a")
    dwo_s = _mm([(sv['sn'], dh1)], 'tn', f32, name=name + "dw_out_s")
    G['w_out'] = _unpack_wout(jnp.concatenate([dwo_a, dwo_s], axis=0))
    dmixed = _mm([(dh1, p['wout'])], 'nt', f32, name=name + "mm_dout")
    dy, dz, dg, db = _glu_bwd(sv['y'], sv['z'], dmixed, W['ssm_w_glu'][l], W['ssm_out_g'][l], name=name + "glu_bwd")
    G['ssm_out_g'], G['ssm_b_glu'] = dg[0], db[0]
    G['ssm_w_glu'] = _mm([(sv['ge'], dz)], 'tn', f32, name=name + "dw_glu")
    du, dbm, dcm, da, dd = _ssm_bwd(sv['proj'], dy, sv['xc'], sv['bmb'], sv['cmb'], sv['a'], W['ssm_d'][l],
                                    name=name + "ssm_bwd")
    G['ssm_d'] = dd[0]
    G['ssm_raw'] = (dbm, dcm, da)
    doh, dg, delta = _rms_bwd(sv['oh'], p['attn_g'], dmixed, width=MLA_PAD, n_valid=MLA_WIDTH, delta=True,
                              out_dtype=bf16, name=name + "rmsb_attn")
    G['attn_out_g'] = _unpad_gain(dg[0])
    dqh = _attn_bwd_dq(sv['qh'], sv['kh'], sv['vh'], doh, sv['lse'], delta, name=name + "attn_dq")
    dkh, dvh = _attn_bwd_dkv(sv['qh'], sv['kh'], sv['vh'], doh, sv['lse'], delta, name=name + "attn_dkv")
    dq, dkv, dkr = _rope_bwd(dqh, dkh, dvh, tabs, name=name + "rope_bwd")
    G['w_uq'] = _unpad_heads(_mm([(sv['cqn'], dq)], 'tn', f32, name=name + "dw_uq"), QK_NOPE + QK_ROPE, _Q_PIECES)
    dwukv = _mm([(sv['ckvn'], dkv)], 'tn', f32, name=name + "dw_ukv")
    G['w_ukv'] = (_unpad_heads(dwukv[:, :MLA_PAD], QK_NOPE + V_HEAD, _K_PIECES)
                  + _unpad_heads(dwukv[:, MLA_PAD:], QK_NOPE + V_HEAD, _V_PIECES))
    dcqn = _mm([(dq, p['wuq'])], 'nt', f32, name=name + "mm_duq")
    dckvn = _mm([(dkv, p['wukv'])], 'nt', f32, name=name + "mm_dukv")
    dcq, dg = _rms_bwd(sv['proj'], W['q_norm_g'][l], dcqn, col0=0, width=Q_LORA, out_dtype=bf16, name=name + "rmsb_q")
    G['q_norm_g'] = dg[0]
    dckv, dg = _rms_bwd(sv['proj'], W['kv_norm_g'][l], dckvn, col0=Q_LORA, width=KV_LORA, out_dtype=bf16,
                        name=name + "rmsb_kv")
    G['kv_norm_g'] = dg[0]
    dproj = jnp.concatenate([dcq, dckv, dkr, du.astype(bf16)], axis=1)
    G['w_in'] = _unpack_win(_mm([(sv['xn'], dproj)], 'tn', f32, name=name + "dw_in"))
    dxn = _mm([(dproj, p['win'])], 'nt', f32, name=name + "mm_din")
    dh0, dg = _rms_bwd(sv['h0'], W['norm_mix_g'][l], dxn, res=dh1, name=name + "rmsb_mix")
    G['norm_mix_g'] = dg[0]
    return dh0, G


def _local_step(x, mem, positions, target, W):
    tabs = _rope_tables(positions)
    ssm_in = [(W['ssm_lambda_re'][l], W['ssm_lambda_im'][l], W['ssm_log_dt'][l], W['ssm_b_re'][l], W['ssm_b_im'][l],
               W['ssm_c_re'][l], W['ssm_c_im'][l]) for l in range(DEPTH)]
    preps = [jax.vjp(_ssm_prep, *ssm_in[l]) for l in range(DEPTH)]
    W = dict(W)
    W['ssm'] = [preps[l][0] for l in range(DEPTH)]
    h = x
    saved = []
    for l in range(DEPTH):
        h, sv = _forward_layer(h, mem, tabs, W, l, f"l{l}_")
        saved.append(sv)
    dh, dgf, loss = _loss_head(h, W['final_norm_g'], target)
    grads = [None] * DEPTH
    for l in reversed(range(DEPTH)):
        dh, G = _backward_layer(dh, saved[l], mem, tabs, W, l, f"l{l}b_")
        dbm, dcm, da = G.pop('ssm_raw')
        names = ['ssm_lambda_re', 'ssm_lambda_im', 'ssm_log_dt', 'ssm_b_re', 'ssm_b_im', 'ssm_c_re', 'ssm_c_im']
        for n, g in zip(names, preps[l][1]((dbm, dcm, da))):
            G[n] = g
        grads[l] = G
    out = {n: jnp.stack([grads[l][n] for l in range(DEPTH)]) for n in grads[0]}
    out['final_norm_g'] = dgf[0]
    return loss[0, 0], dh, out


_HBM = pl.BlockSpec(memory_space=pltpu.HBM)


def _me():
    return lax.axis_index("x"), lax.axis_index("y"), lax.axis_index("c")


def _chip_peers(x, y, c):
    devs = [(1 - x, y, c), (x, 1 - y, c), (1 - x, 1 - y, c)]
    return devs, [2 * d[0] + d[1] for d in devs]


def _allgather_chips(xs, name):
    n = len(xs)

    def body(*refs):
        src, dst = refs[:n], refs[n:2 * n]
        send, recv, loc = refs[2 * n:]
        x, y, c = _me()
        jme = 2 * x + y
        devs, js = _chip_peers(x, y, c)
        started = []
        for i in range(n):
            own = pltpu.make_async_copy(src[i], dst[i].at[:, pl.ds(jme, 1)], loc.at[i])
            own.start()
            started.append(own)
            for k in range(3):
                cp = pltpu.make_async_remote_copy(src[i], dst[i].at[:, pl.ds(jme, 1)], send.at[3 * i + k],
                                                  recv.at[3 * i + k], device_id=devs[k], device_id_type=MESH)
                cp.start()
                started.append(cp)
        for i in range(n):
            for k in range(3):
                pltpu.make_async_remote_copy(src[i], dst[i].at[:, pl.ds(js[k], 1)], send.at[3 * i + k],
                                             recv.at[3 * i + k], device_id=devs[k], device_id_type=MESH).wait_recv()
        for i in range(n):
            started[4 * i].wait()
            for k in range(3):
                started[4 * i + 1 + k].wait_send()

    ins = [t.reshape(t.shape[0], 1, *t.shape[1:]) for t in xs]
    outs = [jax.ShapeDtypeStruct((t.shape[0], 4, *t.shape[1:]), t.dtype) for t in xs]
    return _pc(body, in_specs=[_HBM] * n, out_specs=[_HBM] * n, out_shape=outs,
               scratch_shapes=[pltpu.SemaphoreType.DMA((3 * n,)), pltpu.SemaphoreType.DMA((3 * n,)),
                               pltpu.SemaphoreType.DMA((n,))],
               compiler_params=pltpu.CompilerParams(has_side_effects=True), name=name)(*ins)


def _exchange_halves(gs, name):
    n = len(gs)

    def body(*refs):
        src, dst = refs[:n], refs[n:2 * n]
        send, recv = refs[2 * n:]
        x, y, c = _me()
        cps = []
        for i in range(n):
            r2 = gs[i].shape[2] // 2
            rows = pl.ds(pl.multiple_of((1 - c) * r2, SUBLANES), r2)
            cp = pltpu.make_async_remote_copy(src[i].at[:, :, rows, :], dst[i], send.at[i], recv.at[i],
                                              device_id=(x, y, 1 - c), device_id_type=MESH)
            cp.start()
            cps.append(cp)
        for cp in cps:
            cp.wait()

    outs = [jax.ShapeDtypeStruct((2, 4, g.shape[2] // 2, g.shape[3]), g.dtype) for g in gs]
    return _pc(body, in_specs=[_HBM] * n, out_specs=[_HBM] * n, out_shape=outs,
               scratch_shapes=[pltpu.SemaphoreType.DMA((n,)), pltpu.SemaphoreType.DMA((n,))],
               compiler_params=pltpu.CompilerParams(has_side_effects=True), name=name)(*gs)


def _scatter_chips(ps, name):
    n = len(ps)

    def body(*refs):
        src, dst = refs[:n], refs[n:2 * n]
        send, recv = refs[2 * n:]
        x, y, c = _me()
        devs, js = _chip_peers(x, y, c)
        cps = []
        for i in range(n):
            for k in range(3):
                cp = pltpu.make_async_remote_copy(src[i].at[:, pl.ds(js[k], 1)], dst[i].at[k], send.at[3 * i + k],
                                                  recv.at[3 * i + k], device_id=devs[k], device_id_type=MESH)
                cp.start()
                cps.append(cp)
        for cp in cps:
            cp.wait()

    outs = [jax.ShapeDtypeStruct((3, 2, 1, p.shape[2], p.shape[3]), p.dtype) for p in ps]
    return _pc(body, in_specs=[_HBM] * n, out_specs=[_HBM] * n, out_shape=outs,
               scratch_shapes=[pltpu.SemaphoreType.DMA((3 * n,)), pltpu.SemaphoreType.DMA((3 * n,))],
               compiler_params=pltpu.CompilerParams(has_side_effects=True), name=name)(*ps)


def _join_halves(hs, name):
    n = len(hs)

    def body(*refs):
        src, dst = refs[:n], refs[n:2 * n]
        send, recv, loc = refs[2 * n:]
        x, y, c = _me()
        cps = []
        for i in range(n):
            r2 = hs[i].shape[1]
            mine = pl.ds(pl.multiple_of(c * r2, SUBLANES), r2)
            own = pltpu.make_async_copy(src[i], dst[i].at[:, mine, :], loc.at[i])
            own.start()
            cp = pltpu.make_async_remote_copy(src[i], dst[i].at[:, mine, :], send.at[i], recv.at[i],
                                              device_id=(x, y, 1 - c), device_id_type=MESH)
            cp.start()
            cps.append((own, cp))
        for i in range(n):
            r2 = hs[i].shape[1]
            theirs = pl.ds(pl.multiple_of((1 - c) * r2, SUBLANES), r2)
            pltpu.make_async_remote_copy(src[i], dst[i].at[:, theirs, :], send.at[i], recv.at[i],
                                         device_id=(x, y, 1 - c), device_id_type=MESH).wait_recv()
        for own, cp in cps:
            own.wait()
            cp.wait_send()

    outs = [jax.ShapeDtypeStruct((2, 2 * h.shape[1], h.shape[2]), h.dtype) for h in hs]
    return _pc(body, in_specs=[_HBM] * n, out_specs=[_HBM] * n, out_shape=outs,
               scratch_shapes=[pltpu.SemaphoreType.DMA((n,)), pltpu.SemaphoreType.DMA((n,)),
                               pltpu.SemaphoreType.DMA((n,))],
               compiler_params=pltpu.CompilerParams(has_side_effects=True), name=name)(*hs)


def _row_tile(r):
    return _pick(r, (256, 128, 64, 32, 16, 8))


def _add_half(g, r1, cidx, name):
    _, _, r, n = g.shape
    r2 = r // 2
    tr = _row_tile(r2)
    nb = r2 // tr

    def body(c_ref, g_ref, r_ref, o_ref):
        o_ref[...] = g_ref[...] + r_ref[...]

    blk = lambda f: pl.BlockSpec((None, None, tr, n), f)
    gs = pltpu.PrefetchScalarGridSpec(
        num_scalar_prefetch=1, grid=(2, 4, nb),
        in_specs=[blk(lambda l, j, i, c: (l, j, c[0] * nb + i, 0)), blk(lambda l, j, i, c: (l, j, i, 0))],
        out_specs=blk(lambda l, j, i, c: (l, j, i, 0)))
    return _pc(body, grid_spec=gs, out_shape=jax.ShapeDtypeStruct((2, 4, r2, n), f32),
               compiler_params=_cp(("parallel", "parallel", "parallel")), name=name)(cidx, g, r1)


def _add_chips(p, r3, jidx, name):
    _, _, r2, n = p.shape
    tr = _row_tile(r2)

    def body(j_ref, p_ref, a_ref, b_ref, c_ref, o_ref):
        o_ref[...] = ((p_ref[...] + a_ref[...]) + b_ref[...]) + c_ref[...]

    rblk = lambda k: pl.BlockSpec((None, None, None, tr, n), lambda l, i, j: (k, l, 0, i, 0))
    gs = pltpu.PrefetchScalarGridSpec(
        num_scalar_prefetch=1, grid=(2, r2 // tr),
        in_specs=[pl.BlockSpec((None, None, tr, n), lambda l, i, j: (l, j[0], i, 0)), rblk(0), rblk(1), rblk(2)],
        out_specs=pl.BlockSpec((None, tr, n), lambda l, i, j: (l, i, 0)))
    return _pc(body, grid_spec=gs, out_shape=jax.ShapeDtypeStruct((2, r2, n), f32),
               compiler_params=_cp(("parallel", "parallel")), name=name)(jidx, p, r3, r3, r3)


def _adamw(w, g, m, v, name):
    rows, n = w.shape
    tr = _row_tile(rows)
    c1 = 1.0 / (1.0 - ADAM_B1 ** ADAM_STEP)
    c2 = 1.0 / (1.0 - ADAM_B2 ** ADAM_STEP)

    def body(w_ref, g_ref, m_ref, v_ref, d_ref, mo_ref, vo_ref):
        gv = g_ref[...]
        m2 = ADAM_B1 * m_ref[...] + (1.0 - ADAM_B1) * gv
        v2 = ADAM_B2 * v_ref[...] + (1.0 - ADAM_B2) * (gv * gv)
        d_ref[...] = -ADAM_LR * ((m2 * c1) / (jnp.sqrt(v2 * c2) + ADAM_EPS) + ADAM_WD * w_ref[...])
        mo_ref[...] = m2
        vo_ref[...] = v2

    blk = pl.BlockSpec((tr, n), lambda i: (i, 0))
    return _pc(body, grid=(rows // tr,), in_specs=[blk] * 4, out_specs=[blk] * 3,
               out_shape=[jax.ShapeDtypeStruct((rows, n), f32)] * 3,
               compiler_params=_cp(("parallel",)), name=name)(w, g, m, v)


def _full_from_gathered(name, t):
    L, _, r, n = t.shape
    if SHARDED[name] == 1:
        return t.reshape(L, 4 * r, n)
    return t.transpose(0, 2, 1, 3).reshape(L, r, 4 * n)


def _shard_major(name, g):
    L, R, C = g.shape
    if SHARDED[name] == 1:
        return g.reshape(L, 4, R // 4, C)
    return g.reshape(L, R, 4, C // 4).transpose(0, 2, 1, 3)


_SMALL_ROWS = 288


def _pack_small(d):
    flat = jnp.concatenate([d[n].reshape(-1) for n in SMALL])
    total = 2 * 4 * _SMALL_ROWS * LANES
    flat = jnp.concatenate([flat, jnp.zeros((total - flat.shape[0],), f32)])
    return flat.reshape(2, 4, _SMALL_ROWS, LANES)


def _unpack_small(t, like):
    flat = t.reshape(-1)
    out, off = {}, 0
    for n in SMALL:
        sz = math.prod(like[n].shape)
        out[n] = flat[off:off + sz].reshape(like[n].shape)
        off += sz
    return out


def kernel(x, mem, positions, norm_mix_g, w_in, q_norm_g, w_uq, kv_norm_g, w_ukv, ssm_lambda_re, ssm_lambda_im, ssm_log_dt, ssm_b_re, ssm_b_im, ssm_c_re, ssm_c_im, ssm_d, ssm_w_glu, ssm_b_glu, attn_out_g, ssm_out_g, w_out, norm_x_g, mem_norm_g, w_xq, w_xkv, w_xo, norm_ffn_g, w_gate, w_up, w_down, final_norm_g, loss_target, m_norm_mix_g, m_w_in, m_q_norm_g, m_w_uq, m_kv_norm_g, m_w_ukv, m_ssm_lambda_re, m_ssm_lambda_im, m_ssm_log_dt, m_ssm_b_re, m_ssm_b_im, m_ssm_c_re, m_ssm_c_im, m_ssm_d, m_ssm_w_glu, m_ssm_b_glu, m_attn_out_g, m_ssm_out_g, m_w_out, m_norm_x_g, m_mem_norm_g, m_w_xq, m_w_xkv, m_w_xo, m_norm_ffn_g, m_w_gate, m_w_up, m_w_down, m_final_norm_g, v_norm_mix_g, v_w_in, v_q_norm_g, v_w_uq, v_kv_norm_g, v_w_ukv, v_ssm_lambda_re, v_ssm_lambda_im, v_ssm_log_dt, v_ssm_b_re, v_ssm_b_im, v_ssm_c_re, v_ssm_c_im, v_ssm_d, v_ssm_w_glu, v_ssm_b_glu, v_attn_out_g, v_ssm_out_g, v_w_out, v_norm_x_g, v_mem_norm_g, v_w_xq, v_w_xkv, v_w_xo, v_norm_ffn_g, v_w_gate, v_w_up, v_w_down, v_final_norm_g):
    given = dict(locals())
    w = {n: given[n] for n in WEIGHTS}
    m = {n: given["m_" + n] for n in WEIGHTS}
    v = {n: given["v_" + n] for n in WEIGHTS}
    big = list(SHARDED)

    gathered = _allgather_chips([w[n].astype(bf16) for n in big], "allgather_weights")
    W = {n: _full_from_gathered(n, t) for n, t in zip(big, gathered)}
    W.update({n: w[n] for n in SMALL})

    loss, dx, grads = _local_step(x[0], mem[0], positions[0], loss_target[0], W)
    loss = lax.psum(loss, ("x", "y", "c"))

    cidx = lax.axis_index("c").astype(jnp.int32).reshape(1)
    jidx = (2 * lax.axis_index("x") + lax.axis_index("y")).astype(jnp.int32).reshape(1)
    names = big + ["small"]
    gs = [_shard_major(n, grads[n]) for n in big] + [_pack_small(grads)]
    r1 = _exchange_halves(gs, "grad_exchange_halves")
    ps = [_add_half(g, r, cidx, f"grad_add_half_{n}") for n, g, r in zip(names, gs, r1)]
    r3 = _scatter_chips(ps, "grad_scatter_chips")
    hs = [_add_chips(p, r, jidx, f"grad_add_chips_{n}") for n, p, r in zip(names, ps, r3)]
    gfull = _join_halves(hs, "grad_join_halves")

    out_g, out_d, out_m, out_v = {}, {}, {}, {}
    for n, g in zip(big, gfull[:-1]):
        shp = w[n].shape
        two = lambda t: t.reshape(shp[0] * shp[1], shp[2])
        d_, m_, v_ = _adamw(two(w[n]), two(g), two(m[n]), two(v[n]), f"adamw_{n}")
        out_g[n], out_d[n], out_m[n], out_v[n] = g, d_.reshape(shp), m_.reshape(shp), v_.reshape(shp)
    gsm = _allgather_chips([gfull[-1]], "allgather_small")[0]
    rows2 = 2 * 4 * _SMALL_ROWS
    flat = lambda d: _pack_small(d).reshape(rows2, LANES)
    d_, m_, v_ = _adamw(flat(w), gsm.reshape(rows2, LANES), flat(m), flat(v), "adamw_small")
    for dst, t in ((out_g, gsm), (out_d, d_), (out_m, m_), (out_v, v_)):
        dst.update(_unpack_small(t, w))

    return (loss, dx.reshape(x.shape), *[out_g[n] for n in WEIGHTS], *[out_d[n] for n in WEIGHTS],
            *[out_m[n] for n in WEIGHTS], *[out_v[n] for n in WEIGHTS])
```

```python
import functools
import math

import jax
import jax.numpy as jnp
from jax import lax
from jax.experimental import pallas as pl
from jax.experimental.pallas import tpu as pltpu

f32, bf16 = jnp.float32, jnp.bfloat16

D_MODEL = 1024
DEPTH = 2
MLA_HEADS = 8
QK_NOPE = 64
QK_ROPE = 32
V_HEAD = 64
Q_LORA = 256
KV_LORA = 128
MLA_WIDTH = MLA_HEADS * V_HEAD
ROPE_THETA = 10000.0
SSM_WIDTH = 512
SSM_GROUP = 16
SSM_GROUPS = 32
SSM_STATE = 64
IN_WIDTH = Q_LORA + KV_LORA + QK_ROPE + SSM_WIDTH
X_HEADS = 4
X_HEAD_DIM = D_MODEL // X_HEADS
D_FF = 2816
EPS = 1e-6
ADAM_LR, ADAM_B1, ADAM_B2, ADAM_EPS, ADAM_WD, ADAM_STEP = 0.001, 0.9, 0.999, 1e-08, 0.01, 10

LANES = 128
SUBLANES = 8
HEAD_PAD = 128
MLA_PAD = MLA_HEADS * HEAD_PAD
SSM_MACRO = 4
MACRO_CH = SSM_WIDTH // SSM_MACRO
MACRO_ST = SSM_GROUPS // SSM_MACRO * SSM_STATE
VMEM_LIMIT = 56 * 1024 * 1024

WEIGHTS = ['norm_mix_g', 'w_in', 'q_norm_g', 'w_uq', 'kv_norm_g', 'w_ukv', 'ssm_lambda_re', 'ssm_lambda_im',
           'ssm_log_dt', 'ssm_b_re', 'ssm_b_im', 'ssm_c_re', 'ssm_c_im', 'ssm_d', 'ssm_w_glu', 'ssm_b_glu',
           'attn_out_g', 'ssm_out_g', 'w_out', 'norm_x_g', 'mem_norm_g', 'w_xq', 'w_xkv', 'w_xo', 'norm_ffn_g',
           'w_gate', 'w_up', 'w_down', 'final_norm_g']
SHARDED = {'w_in': 1, 'w_uq': 2, 'w_ukv': 2, 'ssm_w_glu': 1, 'w_out': 1, 'w_xq': 1, 'w_xkv': 2, 'w_xo': 1,
           'w_gate': 2, 'w_up': 2, 'w_down': 1}
SMALL = [n for n in WEIGHTS if n not in SHARDED]
MESH = pl.DeviceIdType.MESH


def _pc(body, **kw):
    return pl.pallas_call(body, **kw)


def _pick(n, prefs):
    for p in prefs:
        if n % p == 0:
            return p
    return n


def _cp(sem=None):
    return pltpu.CompilerParams(dimension_semantics=sem, vmem_limit_bytes=VMEM_LIMIT)


_TILE_CANDS = (1024, 1408, 512, 256, 128)
MM_VMEM_BUDGET = 40 * 1024 * 1024


def _mm_tiles(M, K, N, a_bytes, b_bytes, o_bytes, npair, has_res, need_acc):
    best = None
    for tm in _TILE_CANDS:
        for tk in _TILE_CANDS:
            if M % tm or K % tk:
                continue
            vm = npair * (2 * tm * tk * a_bytes + 2 * tk * N * b_bytes) + 2 * tm * N * o_bytes
            vm += tm * N * 4 * (1 + need_acc + 2 * has_res)
            if a_bytes == 4:
                vm += npair * tm * tk * 2
            if b_bytes == 4:
                vm += npair * tk * N * 2
            if vm <= MM_VMEM_BUDGET and (best is None or tm * tk > best[0]):
                best = (tm * tk, tm, tk)
    if best is None:
        return _pick(M, (256, 128)), _pick(K, (256, 128))
    return best[1], best[2]


def _mm(pairs, mode, out_dtype, res=None, name="mm"):
    a0, b0 = pairs[0]
    if mode == 'nn':
        (M, K), N = a0.shape, b0.shape[1]
        dims = (((1,), (0,)), ((), ()))
    elif mode == 'nt':
        (M, K), N = a0.shape, b0.shape[0]
        dims = (((1,), (1,)), ((), ()))
    else:
        (K, M), N = a0.shape, b0.shape[1]
        dims = (((0,), (0,)), ((), ()))
    npair = len(pairs)
    has_res = res is not None
    direct = out_dtype == f32
    tm, tk = _mm_tiles(M, K, N, a0.dtype.itemsize, b0.dtype.itemsize, jnp.dtype(out_dtype).itemsize, npair, has_res,
                       not direct)
    nk = K // tk

    def body(*refs):
        ins = refs[:2 * npair]
        res_ref = refs[2 * npair] if has_res else None
        o_ref = refs[2 * npair + has_res]
        acc = o_ref if direct else refs[2 * npair + has_res + 1]
        k = pl.program_id(1)
        s = None
        for p in range(npair):
            d = lax.dot_general(ins[2 * p][...].astype(bf16), ins[2 * p + 1][...].astype(bf16), dims,
                                preferred_element_type=f32)
            s = d if s is None else s + d

        @pl.when(k == 0)
        def _():
            acc[...] = s

        @pl.when(k > 0)
        def _():
            acc[...] += s

        if has_res or not direct:
            @pl.when(k == nk - 1)
            def _():
                r = acc[...]
                if has_res:
                    r = r + res_ref[...]
                o_ref[...] = r.astype(out_dtype)

    if mode == 'nn':
        a_spec = pl.BlockSpec((tm, tk), lambda i, k: (i, k))
        b_spec = pl.BlockSpec((tk, N), lambda i, k: (k, 0))
    elif mode == 'nt':
        a_spec = pl.BlockSpec((tm, tk), lambda i, k: (i, k))
        b_spec = pl.BlockSpec((N, tk), lambda i, k: (0, k))
    else:
        a_spec = pl.BlockSpec((tk, tm), lambda i, k: (k, i))
        b_spec = pl.BlockSpec((tk, N), lambda i, k: (k, 0))
    o_spec = pl.BlockSpec((tm, N), lambda i, k: (i, 0))
    in_specs = [a_spec, b_spec] * npair + ([o_spec] if has_res else [])
    args = [t for p in pairs for t in p] + ([res] if has_res else [])
    return _pc(body, grid=(M // tm, nk), in_specs=in_specs, out_specs=o_spec,
               out_shape=jax.ShapeDtypeStruct((M, N), out_dtype),
               scratch_shapes=[] if direct else [pltpu.VMEM((tm, N), f32)],
               compiler_params=_cp(("parallel", "arbitrary")), name=name)(*args)


def _rms_fwd(x, g, *, col0=0, width=None, n_valid=None, out_dtype=bf16, name="rms_fwd"):
    S = x.shape[0]
    width = width or x.shape[1]
    n_valid = n_valid or width
    ts = _pick(S, (512, 256, 128))
    cb = col0 // width

    def body(x_ref, g_ref, o_ref):
        xv = x_ref[...]
        ms = jnp.sum(xv * xv, axis=-1, keepdims=True) * (1.0 / n_valid)
        o_ref[...] = (xv * lax.rsqrt(ms + EPS) * g_ref[...]).astype(out_dtype)

    return _pc(body, grid=(S // ts,),
               in_specs=[pl.BlockSpec((ts, width), lambda i: (i, cb)), pl.BlockSpec((1, width), lambda i: (0, 0))],
               out_specs=pl.BlockSpec((ts, width), lambda i: (i, 0)),
               out_shape=jax.ShapeDtypeStruct((S, width), out_dtype),
               compiler_params=_cp(("parallel",)), name=name)(x, g.reshape(1, width))


def _rms_bwd(x, g, dy, *, col0=0, dcol0=0, width=None, n_valid=None, res=None, out_dtype=f32, delta=False,
             name="rms_bwd"):
    S = x.shape[0]
    width = width or x.shape[1]
    n_valid = n_valid or width
    ts = _pick(S, (512, 256, 128))
    cb, dcb = col0 // width, dcol0 // width
    has_res = res is not None

    def body(*refs):
        x_ref, g_ref, dy_ref = refs[:3]
        res_ref = refs[3] if has_res else None
        outs = refs[3 + has_res:]
        dx_ref, dg_ref = outs[0], outs[1]
        i = pl.program_id(0)
        xv = x_ref[...]
        gv = g_ref[...]
        dyv = dy_ref[...].astype(f32)
        rstd = lax.rsqrt(jnp.sum(xv * xv, axis=-1, keepdims=True) * (1.0 / n_valid) + EPS)
        xh = xv * rstd
        dxh = dyv * gv
        mean = jnp.sum(dxh * xh, axis=-1, keepdims=True) * (1.0 / n_valid)
        dx = rstd * (dxh - xh * mean)
        if delta:
            d_ref = outs[2]
            for h in range(width // LANES):
                sl = slice(h * LANES, (h + 1) * LANES)
                dsum = jnp.sum(dx[:, sl] * xv[:, sl], axis=-1, keepdims=True)
                d_ref[:, sl] = jnp.broadcast_to(dsum, (ts, LANES))
        if has_res:
            dx = dx + res_ref[...]
        dx_ref[...] = dx.astype(out_dtype)

        @pl.when(i == 0)
        def _():
            dg_ref[...] = jnp.zeros_like(dg_ref)

        dg_ref[...] += jnp.sum(dyv * xh, axis=0, keepdims=True)

    blk = lambda c: pl.BlockSpec((ts, width), lambda i: (i, c))
    in_specs = [blk(cb), pl.BlockSpec((1, width), lambda i: (0, 0)), blk(dcb)] + ([blk(0)] if has_res else [])
    out_specs = [blk(0), pl.BlockSpec((1, width), lambda i: (0, 0))] + ([blk(0)] if delta else [])
    out_shape = [jax.ShapeDtypeStruct((S, width), out_dtype), jax.ShapeDtypeStruct((1, width), f32)] + (
        [jax.ShapeDtypeStruct((S, width), f32)] if delta else [])
    args = [x, g.reshape(1, width), dy] + ([res] if has_res else [])
    return _pc(body, grid=(S // ts,), in_specs=in_specs, out_specs=out_specs, out_shape=out_shape,
               compiler_params=_cp(("arbitrary",)), name=name)(*args)


def _loss_head(h, g, target, name="loss_head"):
    S, D = h.shape
    ts = _pick(S, (512, 256, 128))

    def body(h_ref, g_ref, t_ref, dh_ref, dg_ref, loss_ref):
        i = pl.program_id(0)
        xv = h_ref[...]
        gv = g_ref[...]
        rstd = lax.rsqrt(jnp.sum(xv * xv, axis=-1, keepdims=True) * (1.0 / D) + EPS)
        xh = xv * rstd
        err = xh * gv - t_ref[...]
        dyv = err * (1.0 / D)
        dxh = dyv * gv
        mean = jnp.sum(dxh * xh, axis=-1, keepdims=True) * (1.0 / D)
        dh_ref[...] = rstd * (dxh - xh * mean)

        @pl.when(i == 0)
        def _():
            dg_ref[...] = jnp.zeros_like(dg_ref)
            loss_ref[...] = jnp.zeros_like(loss_ref)

        dg_ref[...] += jnp.sum(dyv * xh, axis=0, keepdims=True)
        part = jnp.sum(jnp.sum(err * err, axis=-1, keepdims=True), axis=0, keepdims=True) * (0.5 / D)
        loss_ref[...] += jnp.broadcast_to(part, (1, LANES))

    blk = pl.BlockSpec((ts, D), lambda i: (i, 0))
    row = pl.BlockSpec((1, D), lambda i: (0, 0))
    return _pc(body, grid=(S // ts,), in_specs=[blk, row, blk],
               out_specs=[blk, row, pl.BlockSpec((1, LANES), lambda i: (0, 0))],
               out_shape=[jax.ShapeDtypeStruct((S, D), f32), jax.ShapeDtypeStruct((1, D), f32),
                          jax.ShapeDtypeStruct((1, LANES), f32)],
               compiler_params=_cp(("arbitrary",)), name=name)(h, g.reshape(1, D), target)


def _rope_apply(x, tc, s1, s2):
    return x * tc + pltpu.roll(x, LANES - 16, 1) * s1 + pltpu.roll(x, 16, 1) * s2


def _rope_apply_t(dy, tc, s1, s2):
    return dy * tc + pltpu.roll(dy * s1, 16, 1) + pltpu.roll(dy * s2, LANES - 16, 1)


def _rope_fwd(q, kv, proj, tabs, name="rope_fwd"):
    S = q.shape[0]
    ts = _pick(S, (512, 256, 128))
    scale = (QK_NOPE + QK_ROPE) ** -0.5

    def body(q_ref, kk_ref, kvv_ref, kr_ref, tc_ref, s1_ref, s2_ref, qh_ref, kh_ref, vh_ref):
        tc, s1, s2 = tc_ref[...], s1_ref[...], s2_ref[...]
        krr = _rope_apply(pltpu.roll(kr_ref[...], QK_NOPE, 1), tc, s1, s2)
        for h in range(MLA_HEADS):
            sl = slice(h * HEAD_PAD, (h + 1) * HEAD_PAD)
            qh_ref[:, sl] = (_rope_apply(q_ref[:, sl], tc, s1, s2) * scale).astype(bf16)
            kh_ref[:, sl] = (kk_ref[:, sl] + krr).astype(bf16)
        vh_ref[...] = kvv_ref[...].astype(bf16)

    wide = lambda c: pl.BlockSpec((ts, MLA_PAD), lambda i: (i, c))
    tab = pl.BlockSpec((ts, LANES), lambda i: (i, 0))
    return _pc(body, grid=(S // ts,),
               in_specs=[wide(0), wide(0), wide(1), pl.BlockSpec((ts, LANES), lambda i: (i, 3)), tab, tab, tab],
               out_specs=[wide(0)] * 3, out_shape=[jax.ShapeDtypeStruct((S, MLA_PAD), bf16)] * 3,
               compiler_params=_cp(("parallel",)), name=name)(q, kv, kv, proj, *tabs)


def _rope_bwd(dqh, dkh, dvh, tabs, name="rope_bwd"):
    S = dqh.shape[0]
    ts = _pick(S, (512, 256, 128))
    scale = (QK_NOPE + QK_ROPE) ** -0.5

    def body(dq_ref, dk_ref, dv_ref, tc_ref, s1_ref, s2_ref, oq_ref, okv_ref, okr_ref):
        tc, s1, s2 = tc_ref[...], s1_ref[...], s2_ref[...]
        ksum = None
        for h in range(MLA_HEADS):
            sl = slice(h * HEAD_PAD, (h + 1) * HEAD_PAD)
            oq_ref[:, sl] = (_rope_apply_t(dq_ref[:, sl], tc, s1, s2) * scale).astype(bf16)
            dk = dk_ref[:, sl]
            okv_ref[:, sl] = dk.astype(bf16)
            ksum = dk if ksum is None else ksum + dk
        okv_ref[:, MLA_PAD:] = dv_ref[...].astype(bf16)
        dkr = pltpu.roll(_rope_apply_t(ksum, tc, s1, s2), LANES - QK_NOPE, 1)
        lane = lax.broadcasted_iota(jnp.int32, (ts, LANES), 1)
        okr_ref[...] = jnp.where(lane < QK_ROPE, dkr, 0.0).astype(bf16)

    wide = pl.BlockSpec((ts, MLA_PAD), lambda i: (i, 0))
    tab = pl.BlockSpec((ts, LANES), lambda i: (i, 0))
    return _pc(body, grid=(S // ts,), in_specs=[wide, wide, wide, tab, tab, tab],
               out_specs=[wide, pl.BlockSpec((ts, 2 * MLA_PAD), lambda i: (i, 0)), tab],
               out_shape=[jax.ShapeDtypeStruct((S, MLA_PAD), bf16), jax.ShapeDtypeStruct((S, 2 * MLA_PAD), bf16),
                          jax.ShapeDtypeStruct((S, LANES), bf16)],
               compiler_params=_cp(("parallel",)), name=name)(dqh, dkh, dvh, *tabs)


ATT_BLK = 1024


def _attn_fwd(qh, kh, vh, name="attn_fwd"):
    S = qh.shape[0]
    tq = tk = min(S, ATT_BLK)
    nq, nk = S // tq, S // tk

    def body(q_ref, k_ref, v_ref, o_ref, lse_ref, m_sc, l_sc, acc_sc):
        i, j = pl.program_id(1), pl.program_id(2)

        @pl.when(j == 0)
        def _():
            m_sc[...] = jnp.full_like(m_sc, -1e30)
            l_sc[...] = jnp.zeros_like(l_sc)
            acc_sc[...] = jnp.zeros_like(acc_sc)

        def step(masked):
            s = lax.dot_general(q_ref[...], k_ref[...], (((1,), (1,)), ((), ())), preferred_element_type=f32)
            if masked:
                row = lax.broadcasted_iota(jnp.int32, (tq, tk), 0)
                col = lax.broadcasted_iota(jnp.int32, (tq, tk), 1)
                s = jnp.where(col <= row, s, -1e30)
            m_prev = m_sc[...]
            m_new = jnp.maximum(m_prev, jnp.max(s, axis=-1, keepdims=True))
            alpha = jnp.exp(m_prev - m_new)
            p = jnp.exp(s - m_new)
            l_sc[...] = alpha * l_sc[...] + jnp.sum(p, axis=-1, keepdims=True)
            acc_sc[...] = alpha * acc_sc[...] + jnp.dot(p.astype(bf16), v_ref[...], preferred_element_type=f32)
            m_sc[...] = m_new

        pl.when(j < i)(functools.partial(step, False))
        pl.when(j == i)(functools.partial(step, True))

        @pl.when(j == nk - 1)
        def _():
            l = l_sc[...]
            o_ref[...] = acc_sc[...] / l
            lse_ref[...] = jnp.broadcast_to(m_sc[...] + jnp.log(l), (tq, LANES))

    qspec = pl.BlockSpec((tq, HEAD_PAD), lambda h, i, j: (i, h))
    kspec = pl.BlockSpec((tk, HEAD_PAD), lambda h, i, j: (jnp.minimum(j, i), h))
    return _pc(body, grid=(MLA_HEADS, nq, nk), in_specs=[qspec, kspec, kspec], out_specs=[qspec, qspec],
               out_shape=[jax.ShapeDtypeStruct((S, MLA_PAD), f32)] * 2,
               scratch_shapes=[pltpu.VMEM((tq, 1), f32), pltpu.VMEM((tq, 1), f32), pltpu.VMEM((tq, HEAD_PAD), f32)],
               compiler_params=_cp(("parallel", "parallel", "arbitrary")), name=name)(qh, kh, vh)


def _attn_bwd(qh, kh, vh, do, lse, delta, name="attn_bwd"):
    S = qh.shape[0]
    tq = tk = min(S, ATT_BLK)
    nq, nk = S // tq, S // tk

    def body(q_ref, k_ref, v_ref, do_ref, lse_ref, dl_ref, dq_ref, dk_ref, dv_ref):
        j, i = pl.program_id(1), pl.program_id(2)

        @pl.when((j == 0) & (i == 0))
        def _():
            dq_ref[...] = jnp.zeros_like(dq_ref)

        @pl.when(i == 0)
        def _():
            dk_ref[...] = jnp.zeros_like(dk_ref)
            dv_ref[...] = jnp.zeros_like(dv_ref)

        def step(masked):
            nt = (((1,), (1,)), ((), ()))
            tn = (((0,), (0,)), ((), ()))
            qv, kv_, dov = q_ref[...], k_ref[...], do_ref[...]
            s = lax.dot_general(qv, kv_, nt, preferred_element_type=f32)
            p = jnp.exp(s - lse_ref[:, :1])
            if masked:
                row = lax.broadcasted_iota(jnp.int32, (tq, tk), 0)
                col = lax.broadcasted_iota(jnp.int32, (tq, tk), 1)
                p = jnp.where(col <= row, p, 0.0)
            dp = lax.dot_general(dov, v_ref[...], nt, preferred_element_type=f32)
            ds = (p * (dp - dl_ref[:, :1])).astype(bf16)
            dv_ref[...] += lax.dot_general(p.astype(bf16), dov, tn, preferred_element_type=f32)
            dk_ref[...] += lax.dot_general(ds, qv, tn, preferred_element_type=f32)
            rows = pl.ds(pl.multiple_of(i * tq, tq), tq)
            dq_ref[rows, :] += jnp.dot(ds, kv_, preferred_element_type=f32)

        pl.when(i > j)(functools.partial(step, False))
        pl.when(i == j)(functools.partial(step, True))

    qspec = pl.BlockSpec((tq, HEAD_PAD), lambda h, j, i: (jnp.maximum(i, j), h))
    kspec = pl.BlockSpec((tk, HEAD_PAD), lambda h, j, i: (j, h))
    colspec = pl.BlockSpec((S, HEAD_PAD), lambda h, j, i: (0, h))
    return _pc(body, grid=(MLA_HEADS, nk, nq), in_specs=[qspec, kspec, kspec, qspec, qspec, qspec],
               out_specs=[colspec, kspec, kspec], out_shape=[jax.ShapeDtypeStruct((S, MLA_PAD), f32)] * 3,
               compiler_params=_cp(("parallel", "arbitrary", "arbitrary")), name=name)(qh, kh, vh, do, lse, delta)


def _xattn_fwd(q, kv, name="xattn_fwd"):
    S = q.shape[0]
    M = kv.shape[0]
    tq = _pick(S, (256, 128))
    scale = X_HEAD_DIM ** -0.5

    def body(q_ref, kv_ref, o_ref):
        for h in range(X_HEADS):
            sl = slice(h * X_HEAD_DIM, (h + 1) * X_HEAD_DIM)
            k = kv_ref[:, sl]
            v = kv_ref[:, D_MODEL + h * X_HEAD_DIM:D_MODEL + (h + 1) * X_HEAD_DIM]
            s = lax.dot_general(q_ref[:, sl], k, (((1,), (1,)), ((), ())), preferred_element_type=f32) * scale
            e = jnp.exp(s - jnp.max(s, axis=-1, keepdims=True))
            p = e / jnp.sum(e, axis=-1, keepdims=True)
            o_ref[:, sl] = jnp.dot(p.astype(bf16), v, preferred_element_type=f32).astype(bf16)

    blk = pl.BlockSpec((tq, D_MODEL), lambda i: (i, 0))
    return _pc(body, grid=(S // tq,), in_specs=[blk, pl.BlockSpec((M, 2 * D_MODEL), lambda i: (0, 0))],
               out_specs=blk, out_shape=jax.ShapeDtypeStruct((S, D_MODEL), bf16),
               compiler_params=_cp(("parallel",)), name=name)(q, kv)


def _xattn_bwd(q, kv, do, name="xattn_bwd"):
    S = q.shape[0]
    M = kv.shape[0]
    tq = _pick(S, (256, 128))
    scale = X_HEAD_DIM ** -0.5

    def body(q_ref, kv_ref, do_ref, dq_ref, dkv_ref):
        i = pl.program_id(0)

        @pl.when(i == 0)
        def _():
            dkv_ref[...] = jnp.zeros_like(dkv_ref)

        nt = (((1,), (1,)), ((), ()))
        tn = (((0,), (0,)), ((), ()))
        for h in range(X_HEADS):
            sl = slice(h * X_HEAD_DIM, (h + 1) * X_HEAD_DIM)
            vsl = slice(D_MODEL + h * X_HEAD_DIM, D_MODEL + (h + 1) * X_HEAD_DIM)
            k, v, qv, dov = kv_ref[:, sl], kv_ref[:, vsl], q_ref[:, sl], do_ref[:, sl]
            s = lax.dot_general(qv, k, nt, preferred_element_type=f32) * scale
            e = jnp.exp(s - jnp.max(s, axis=-1, keepdims=True))
            p = e / jnp.sum(e, axis=-1, keepdims=True)
            dp = lax.dot_general(dov, v, nt, preferred_element_type=f32)
            ds = (p * (dp - jnp.sum(dp * p, axis=-1, keepdims=True)) * scale).astype(bf16)
            dq_ref[:, sl] = jnp.dot(ds, k, preferred_element_type=f32).astype(bf16)
            dkv_ref[:, sl] += lax.dot_general(ds, qv, tn, preferred_element_type=f32)
            dkv_ref[:, vsl] += lax.dot_general(p.astype(bf16), dov, tn, preferred_element_type=f32)

    blk = pl.BlockSpec((tq, D_MODEL), lambda i: (i, 0))
    full = pl.BlockSpec((M, 2 * D_MODEL), lambda i: (0, 0))
    return _pc(body, grid=(S // tq,), in_specs=[blk, full, blk], out_specs=[blk, full],
               out_shape=[jax.ShapeDtypeStruct((S, D_MODEL), bf16), jax.ShapeDtypeStruct((M, 2 * D_MODEL), f32)],
               compiler_params=_cp(("arbitrary",)), name=name)(q, kv, do)


def _apow_init(a_ref, ap_ref):
    ar, ai = a_ref[:, :MACRO_ST], a_ref[:, MACRO_ST:]
    pr, pi = ar, ai
    for r in range(SUBLANES):
        ap_ref[r:r + 1, :MACRO_ST] = pr
        ap_ref[r:r + 1, MACRO_ST:] = pi
        pr, pi = pr * ar - pi * ai, pr * ai + pi * ar


def _scan_block(src_ref, dst_ref, ap_ref, carry, n_rows, reverse):
    P = MACRO_ST
    n_chunks = n_rows // SUBLANES
    rid = lax.broadcasted_iota(jnp.int32, (SUBLANES, P), 0)
    sgn = -1.0 if reverse else 1.0
    if reverse:
        apr = jnp.zeros((SUBLANES, P), f32)
        api = jnp.zeros((SUBLANES, P), f32)
        for r in range(SUBLANES):
            apr = jnp.where(rid == r, ap_ref[SUBLANES - 1 - r:SUBLANES - r, :P], apr)
            api = jnp.where(rid == r, -ap_ref[SUBLANES - 1 - r:SUBLANES - r, P:], api)
    else:
        apr, api = ap_ref[:, :P], ap_ref[:, P:]

    def chunk(c, carry):
        cr, ci = carry
        r0 = pl.multiple_of(((n_chunks - 1 - c) if reverse else c) * SUBLANES, SUBLANES)
        xr = src_ref[pl.ds(r0, SUBLANES), :P]
        xi = src_ref[pl.ds(r0, SUBLANES), P:]
        for sh in (1, 2, 4):
            pr = ap_ref[sh - 1:sh, :P]
            pi = ap_ref[sh - 1:sh, P:] * sgn
            if reverse:
                sr = jnp.where(rid < SUBLANES - sh, pltpu.roll(xr, SUBLANES - sh, 0), 0.0)
                si = jnp.where(rid < SUBLANES - sh, pltpu.roll(xi, SUBLANES - sh, 0), 0.0)
            else:
                sr = jnp.where(rid >= sh, pltpu.roll(xr, sh, 0), 0.0)
                si = jnp.where(rid >= sh, pltpu.roll(xi, sh, 0), 0.0)
            xr, xi = xr + pr * sr - pi * si, xi + pr * si + pi * sr
        xr = xr + apr * cr - api * ci
        xi = xi + apr * ci + api * cr
        dst_ref[pl.ds(r0, SUBLANES), :P] = xr
        dst_ref[pl.ds(r0, SUBLANES), P:] = xi
        edge = 0 if reverse else SUBLANES - 1
        ncr = jnp.sum(jnp.where(rid == edge, xr, 0.0), axis=0, keepdims=True)
        nci = jnp.sum(jnp.where(rid == edge, xi, 0.0), axis=0, keepdims=True)
        return ncr, nci

    return lax.fori_loop(0, n_chunks, chunk, carry)


def _ssm_fwd(proj, bm, cm, a, d, name="ssm_fwd"):
    S = proj.shape[0]
    tS = _pick(S, (256, 128))
    nb = S // tS
    P2 = 2 * MACRO_ST
    ucol0 = (D_MODEL - SSM_WIDTH) // MACRO_CH

    def body(u_ref, b_ref, c_ref, a_ref, d_ref, y_ref, xc_ref, bu_sc, ap_sc, car_sc):
        t = pl.program_id(1)

        @pl.when(t == 0)
        def _():
            _apow_init(a_ref, ap_sc)
            car_sc[...] = jnp.zeros_like(car_sc)

        uv = u_ref[...]
        bu_sc[...] = jnp.dot(uv.astype(bf16), b_ref[...], preferred_element_type=f32)
        xc_ref[...] = car_sc[...]
        cr, ci = _scan_block(bu_sc, bu_sc, ap_sc, (car_sc[:, :MACRO_ST], car_sc[:, MACRO_ST:]), tS, False)
        car_sc[:, :MACRO_ST] = cr
        car_sc[:, MACRO_ST:] = ci
        y_ref[...] = jnp.dot(bu_sc[...].astype(bf16), c_ref[...], preferred_element_type=f32) + d_ref[...] * uv

    return _pc(body, grid=(SSM_MACRO, nb),
               in_specs=[pl.BlockSpec((tS, MACRO_CH), lambda m, t: (t, ucol0 + m)),
                         pl.BlockSpec((None, MACRO_CH, P2), lambda m, t: (m, 0, 0)),
                         pl.BlockSpec((None, P2, MACRO_CH), lambda m, t: (m, 0, 0)),
                         pl.BlockSpec((None, 1, P2), lambda m, t: (m, 0, 0)),
                         pl.BlockSpec((1, MACRO_CH), lambda m, t: (0, m))],
               out_specs=[pl.BlockSpec((tS, MACRO_CH), lambda m, t: (t, m)),
                          pl.BlockSpec((None, None, 1, P2), lambda m, t: (m, t, 0, 0))],
               out_shape=[jax.ShapeDtypeStruct((S, SSM_WIDTH), f32), jax.ShapeDtypeStruct((SSM_MACRO, nb, 1, P2), f32)],
               scratch_shapes=[pltpu.VMEM((tS, P2), f32), pltpu.VMEM((SUBLANES, P2), f32), pltpu.VMEM((1, P2), f32)],
               compiler_params=_cp(("arbitrary", "arbitrary")), name=name)(proj, bm, cm, a, d.reshape(1, SSM_WIDTH))


def _ssm_bwd(proj, dy, xc, bm, cm, a, d, name="ssm_bwd"):
    S = proj.shape[0]
    tS = _pick(S, (256, 128))
    nb = S // tS
    P = MACRO_ST
    P2 = 2 * P
    ucol0 = (D_MODEL - SSM_WIDTH) // MACRO_CH

    def body(u_ref, dy_ref, xc_ref, b_ref, c_ref, a_ref, d_ref, du_ref, db_ref, dc_ref, da_ref, dd_ref,
             x_sc, g_sc, ap_sc, gcar_sc):
        t = pl.program_id(1)

        @pl.when(t == 0)
        def _():
            _apow_init(a_ref, ap_sc)
            gcar_sc[...] = jnp.zeros_like(gcar_sc)
            db_ref[...] = jnp.zeros_like(db_ref)
            dc_ref[...] = jnp.zeros_like(dc_ref)
            da_ref[...] = jnp.zeros_like(da_ref)
            dd_ref[...] = jnp.zeros_like(dd_ref)

        nt = (((1,), (1,)), ((), ()))
        tn = (((0,), (0,)), ((), ()))
        uv = u_ref[...]
        ub = uv.astype(bf16)
        dyv = dy_ref[...]
        dyb = dyv.astype(bf16)
        x_sc[...] = jnp.dot(ub, b_ref[...], preferred_element_type=f32)
        xin = xc_ref[...]
        _scan_block(x_sc, x_sc, ap_sc, (xin[:, :P], xin[:, P:]), tS, False)
        g_sc[...] = lax.dot_general(dyb, c_ref[...], nt, preferred_element_type=f32)
        gr, gi = _scan_block(g_sc, g_sc, ap_sc, (gcar_sc[:, :P], gcar_sc[:, P:]), tS, True)
        gcar_sc[:, :P] = gr
        gcar_sc[:, P:] = gi
        xv = x_sc[...]
        gv = g_sc[...]
        gb = gv.astype(bf16)
        dc_ref[...] += lax.dot_general(xv.astype(bf16), dyb, tn, preferred_element_type=f32)
        db_ref[...] += lax.dot_general(ub, gb, tn, preferred_element_type=f32)
        du_ref[...] = lax.dot_general(gb, b_ref[...], nt, preferred_element_type=f32) + d_ref[...] * dyv
        dd_ref[...] += jnp.sum(dyv * uv, axis=0, keepdims=True)
        rid = lax.broadcasted_iota(jnp.int32, (tS, P2), 0)
        xp = jnp.where(rid == 0, jnp.broadcast_to(xin, (tS, P2)), pltpu.roll(xv, 1, 0))
        xpr, xpi, ggr, ggi = xp[:, :P], xp[:, P:], gv[:, :P], gv[:, P:]
        da_ref[:, :P] += jnp.sum(ggr * xpr + ggi * xpi, axis=0, keepdims=True)
        da_ref[:, P:] += jnp.sum(ggi * xpr - ggr * xpi, axis=0, keepdims=True)

    rev = lambda t: nb - 1 - t
    return _pc(body, grid=(SSM_MACRO, nb),
               in_specs=[pl.BlockSpec((tS, MACRO_CH), lambda m, t: (rev(t), ucol0 + m)),
                         pl.BlockSpec((tS, MACRO_CH), lambda m, t: (rev(t), m)),
                         pl.BlockSpec((None, None, 1, P2), lambda m, t: (m, rev(t), 0, 0)),
                         pl.BlockSpec((None, MACRO_CH, P2), lambda m, t: (m, 0, 0)),
                         pl.BlockSpec((None, P2, MACRO_CH), lambda m, t: (m, 0, 0)),
                         pl.BlockSpec((None, 1, P2), lambda m, t: (m, 0, 0)),
                         pl.BlockSpec((1, MACRO_CH), lambda m, t: (0, m))],
               out_specs=[pl.BlockSpec((tS, MACRO_CH), lambda m, t: (rev(t), m)),
                          pl.BlockSpec((None, MACRO_CH, P2), lambda m, t: (m, 0, 0)),
                          pl.BlockSpec((None, P2, MACRO_CH), lambda m, t: (m, 0, 0)),
                          pl.BlockSpec((None, 1, P2), lambda m, t: (m, 0, 0)),
                          pl.BlockSpec((1, MACRO_CH), lambda m, t: (0, m))],
               out_shape=[jax.ShapeDtypeStruct((S, SSM_WIDTH), f32),
                          jax.ShapeDtypeStruct((SSM_MACRO, MACRO_CH, P2), f32),
                          jax.ShapeDtypeStruct((SSM_MACRO, P2, MACRO_CH), f32),
                          jax.ShapeDtypeStruct((SSM_MACRO, 1, P2), f32),
                          jax.ShapeDtypeStruct((1, SSM_WIDTH), f32)],
               scratch_shapes=[pltpu.VMEM((tS, P2), f32), pltpu.VMEM((tS, P2), f32),
                               pltpu.VMEM((SUBLANES, P2), f32), pltpu.VMEM((1, P2), f32)],
               compiler_params=_cp(("arbitrary", "arbitrary")), name=name)(
        proj, dy, xc, bm, cm, a, d.reshape(1, SSM_WIDTH))


_GELU_K = math.sqrt(2.0 / math.pi)
_GELU_C = 0.044715


def _glu_fwd(y, w, b, g, name="glu_fwd"):
    S, W = y.shape
    ts = _pick(S, (512, 256, 128))

    def body(y_ref, w_ref, b_ref, g_ref, z_ref, sn_ref, ge_ref):
        yv = y_ref[...]
        cdf = 0.5 * (1.0 + jnp.tanh(_GELU_K * (yv + _GELU_C * (yv * yv * yv))))
        ge = (yv * cdf).astype(bf16)
        z = jnp.dot(ge, w_ref[...], preferred_element_type=f32) + b_ref[...]
        s = yv * jax.nn.sigmoid(z)
        rstd = lax.rsqrt(jnp.sum(s * s, axis=-1, keepdims=True) * (1.0 / W) + EPS)
        z_ref[...] = z
        sn_ref[...] = (s * rstd * g_ref[...]).astype(bf16)
        ge_ref[...] = ge

    blk = pl.BlockSpec((ts, W), lambda i: (i, 0))
    row = pl.BlockSpec((1, W), lambda i: (0, 0))
    return _pc(body, grid=(S // ts,), in_specs=[blk, pl.BlockSpec((W, W), lambda i: (0, 0)), row, row],
               out_specs=[blk, blk, blk],
               out_shape=[jax.ShapeDtypeStruct((S, W), f32), jax.ShapeDtypeStruct((S, W), bf16),
                          jax.ShapeDtypeStruct((S, W), bf16)],
               compiler_params=_cp(("parallel",)), name=name)(y, w, b.reshape(1, W), g.reshape(1, W))


def _glu_bwd(y, z, dmixed, w, g, name="glu_bwd"):
    S, W = y.shape
    ts = _pick(S, (512, 256, 128))
    dcb = MLA_PAD // W

    def body(y_ref, z_ref, dsn_ref, w_ref, g_ref, dy_ref, dz_ref, dg_ref, db_ref):
        i = pl.program_id(0)
        yv, zv, gv = y_ref[...], z_ref[...], g_ref[...]
        sig = jax.nn.sigmoid(zv)
        s = yv * sig
        rstd = lax.rsqrt(jnp.sum(s * s, axis=-1, keepdims=True) * (1.0 / W) + EPS)
        sh = s * rstd
        dsn = dsn_ref[...]
        dsh = dsn * gv
        ds = rstd * (dsh - sh * (jnp.sum(dsh * sh, axis=-1, keepdims=True) * (1.0 / W)))
        dz = ds * s * (1.0 - sig)
        dzb = dz.astype(bf16)
        dge = lax.dot_general(dzb, w_ref[...], (((1,), (1,)), ((), ())), preferred_element_type=f32)
        t = jnp.tanh(_GELU_K * (yv + _GELU_C * (yv * yv * yv)))
        dgelu = 0.5 * (1.0 + t) + 0.5 * yv * (1.0 - t * t) * _GELU_K * (1.0 + 3.0 * _GELU_C * yv * yv)
        dy_ref[...] = ds * sig + dge * dgelu
        dz_ref[...] = dzb

        @pl.when(i == 0)
        def _():
            dg_ref[...] = jnp.zeros_like(dg_ref)
            db_ref[...] = jnp.zeros_like(db_ref)

        dg_ref[...] += jnp.sum(dsn * sh, axis=0, keepdims=True)
        db_ref[...] += jnp.sum(dz, axis=0, keepdims=True)

    blk = pl.BlockSpec((ts, W), lambda i: (i, 0))
    row = pl.BlockSpec((1, W), lambda i: (0, 0))
    return _pc(body, grid=(S // ts,),
               in_specs=[blk, blk, pl.BlockSpec((ts, W), lambda i: (i, dcb)), pl.BlockSpec((W, W), lambda i: (0, 0)), row],
               out_specs=[blk, blk, row, row],
               out_shape=[jax.ShapeDtypeStruct((S, W), f32), jax.ShapeDtypeStruct((S, W), bf16),
                          jax.ShapeDtypeStruct((1, W), f32), jax.ShapeDtypeStruct((1, W), f32)],
               compiler_params=_cp(("arbitrary",)), name=name)(y, z, dmixed, w, g.reshape(1, W))


def _ffn_up(hn, wg, wu, name="ffn_up"):
    S, K = hn.shape
    F = wg.shape[1]
    tm, tn = _pick(S, (512, 256, 128)), _pick(F, (1408, 256, 128))

    def body(h_ref, wg_ref, wu_ref, g_ref, u_ref, a_ref):
        hv = h_ref[...]
        gv = jnp.dot(hv, wg_ref[...], preferred_element_type=f32)
        uv = jnp.dot(hv, wu_ref[...], preferred_element_type=f32)
        g_ref[...] = gv
        u_ref[...] = uv
        a_ref[...] = (gv * jax.nn.sigmoid(gv) * uv).astype(bf16)

    wspec = pl.BlockSpec((K, tn), lambda i, j: (0, j))
    ospec = pl.BlockSpec((tm, tn), lambda i, j: (i, j))
    return _pc(body, grid=(S // tm, F // tn), in_specs=[pl.BlockSpec((tm, K), lambda i, j: (i, 0)), wspec, wspec],
               out_specs=[ospec] * 3,
               out_shape=[jax.ShapeDtypeStruct((S, F), f32), jax.ShapeDtypeStruct((S, F), f32),
                          jax.ShapeDtypeStruct((S, F), bf16)],
               compiler_params=_cp(("parallel", "parallel")), name=name)(hn, wg, wu)


def _ffn_bwd_act(dh, wd, gate, up, name="ffn_bwd_act"):
    S, K = dh.shape
    F = wd.shape[0]
    tm, tn = _pick(S, (512, 256, 128)), _pick(F, (1408, 256, 128))

    def body(dh_ref, wd_ref, g_ref, u_ref, dg_ref, du_ref):
        dact = lax.dot_general(dh_ref[...].astype(bf16), wd_ref[...], (((1,), (1,)), ((), ())),
                               preferred_element_type=f32)
        gv, uv = g_ref[...], u_ref[...]
        sig = jax.nn.sigmoid(gv)
        dg_ref[...] = (dact * uv * (sig * (1.0 + gv * (1.0 - sig)))).astype(bf16)
        du_ref[...] = (dact * (gv * sig)).astype(bf16)

    ospec = pl.BlockSpec((tm, tn), lambda i, j: (i, j))
    return _pc(body, grid=(S // tm, F // tn),
               in_specs=[pl.BlockSpec((tm, K), lambda i, j: (i, 0)), pl.BlockSpec((tn, K), lambda i, j: (j, 0)),
                         ospec, ospec],
               out_specs=[ospec] * 2, out_shape=[jax.ShapeDtypeStruct((S, F), bf16)] * 2,
               compiler_params=_cp(("parallel", "parallel")), name=name)(dh, wd, gate, up)


def _pad_heads(w, per_head, pieces):
    K = w.shape[0]
    w3 = w.reshape(K, MLA_HEADS, per_head)
    out = jnp.zeros((K, MLA_HEADS, HEAD_PAD), w.dtype)
    for s0, s1, d0 in pieces:
        out = out.at[:, :, d0:d0 + (s1 - s0)].set(w3[:, :, s0:s1])
    return out.reshape(K, MLA_PAD)


def _unpad_heads(wp, per_head, pieces):
    K = wp.shape[0]
    w3 = wp.reshape(K, MLA_HEADS, HEAD_PAD)
    out = jnp.zeros((K, MLA_HEADS, per_head), wp.dtype)
    for s0, s1, d0 in pieces:
        out = out.at[:, :, s0:s1].set(w3[:, :, d0:d0 + (s1 - s0)])
    return out.reshape(K, MLA_HEADS * per_head)


_Q_PIECES = [(0, QK_NOPE + QK_ROPE, 0)]
_K_PIECES = [(0, QK_NOPE, 0)]
_V_PIECES = [(QK_NOPE, QK_NOPE + V_HEAD, 0)]
_KR0 = Q_LORA + KV_LORA


def _pack_win(w):
    z = jnp.zeros((w.shape[0], LANES - QK_ROPE), w.dtype)
    return jnp.concatenate([w[:, :_KR0 + QK_ROPE], z, w[:, _KR0 + QK_ROPE:]], axis=1)


def _unpack_win(wp):
    return jnp.concatenate([wp[:, :_KR0 + QK_ROPE], wp[:, _KR0 + LANES:]], axis=1)


def _pack_wout(w):
    wa = w[:MLA_WIDTH].reshape(MLA_HEADS, V_HEAD, D_MODEL)
    wa = jnp.concatenate([wa, jnp.zeros_like(wa)], axis=1).reshape(MLA_PAD, D_MODEL)
    return jnp.concatenate([wa, w[MLA_WIDTH:]], axis=0)


def _unpack_wout(wp):
    wa = wp[:MLA_PAD].reshape(MLA_HEADS, HEAD_PAD, D_MODEL)[:, :V_HEAD].reshape(MLA_WIDTH, D_MODEL)
    return jnp.concatenate([wa, wp[MLA_PAD:]], axis=0)


def _pad_gain(g):
    g2 = g.reshape(MLA_HEADS, V_HEAD)
    return jnp.concatenate([g2, jnp.zeros_like(g2)], axis=1).reshape(MLA_PAD)


def _unpad_gain(gp):
    return gp.reshape(MLA_HEADS, HEAD_PAD)[:, :V_HEAD].reshape(MLA_WIDTH)


def _ssm_prep(lam_re, lam_im, log_dt, b_re, b_im, c_re, c_im):
    lam = lax.complex(lam_re, lam_im)
    dt = jnp.exp(log_dt)[:, None]
    a_bar = jnp.exp(lam * dt)
    b_bar = ((a_bar - 1.0) / lam)[..., None] * lax.complex(b_re, b_im)
    G8 = SSM_GROUPS // SSM_MACRO
    eye = jnp.eye(G8, dtype=f32)

    def bmat(part):
        p4 = part.reshape(SSM_MACRO, G8, SSM_STATE, SSM_GROUP)
        return jnp.einsum('mgpc,gh->mgchp', p4, eye).reshape(SSM_MACRO, MACRO_CH, MACRO_ST)

    def cmat(part):
        p4 = part.reshape(SSM_MACRO, G8, SSM_GROUP, SSM_STATE)
        return jnp.einsum('mgcp,gh->mgphc', p4, eye).reshape(SSM_MACRO, MACRO_ST, MACRO_CH)

    bm = jnp.concatenate([bmat(b_bar.real), bmat(b_bar.imag)], axis=2)
    cm = jnp.concatenate([cmat(c_re), -cmat(c_im)], axis=1)
    a4 = a_bar.reshape(SSM_MACRO, 1, MACRO_ST)
    a = jnp.concatenate([a4.real, a4.imag], axis=2)
    return bm, cm, a


def _rope_tables(positions):
    freqs = ROPE_THETA ** (-jnp.arange(0, QK_ROPE, 2, dtype=f32) / QK_ROPE)
    ang = positions.astype(f32)[:, None] * freqs
    cos, sin = jnp.cos(ang), jnp.sin(ang)
    S = positions.shape[0]
    half = QK_ROPE // 2
    one, zero = jnp.ones((S, QK_NOPE), f32), jnp.zeros((S, half), f32)
    z64, z32 = jnp.zeros((S, QK_NOPE), f32), jnp.zeros((S, LANES - QK_NOPE - QK_ROPE), f32)
    tc = jnp.concatenate([one, cos, cos, z32], axis=1)
    s1 = jnp.concatenate([z64, -sin, zero, z32], axis=1)
    s2 = jnp.concatenate([z64, zero, sin, z32], axis=1)
    return tc, s1, s2


def _layer_params(W, l):
    p = {}
    p['win'] = _pack_win(W['w_in'][l])
    p['wuq'] = _pad_heads(W['w_uq'][l], QK_NOPE + QK_ROPE, _Q_PIECES)
    wukv = W['w_ukv'][l]
    p['wukv'] = jnp.concatenate([_pad_heads(wukv, QK_NOPE + V_HEAD, _K_PIECES),
                                 _pad_heads(wukv, QK_NOPE + V_HEAD, _V_PIECES)], axis=1)
    p['wout'] = _pack_wout(W['w_out'][l])
    p['attn_g'] = _pad_gain(W['attn_out_g'][l])
    return p


def _forward_layer(h, memn_in, tabs, W, l, name):
    p = _layer_params(W, l)
    sv = {'h0': h, 'p': p}
    xn = _rms_fwd(h, W['norm_mix_g'][l], name=name + "rms_mix")
    proj = _mm([(xn, p['win'])], 'nn', f32, name=name + "mm_in")
    cqn = _rms_fwd(proj, W['q_norm_g'][l], col0=0, width=Q_LORA, name=name + "rms_q")
    ckvn = _rms_fwd(proj, W['kv_norm_g'][l], col0=Q_LORA, width=KV_LORA, name=name + "rms_kv")
    q = _mm([(cqn, p['wuq'])], 'nn', f32, name=name + "mm_uq")
    kv = _mm([(ckvn, p['wukv'])], 'nn', f32, name=name + "mm_ukv")
    qh, kh, vh = _rope_fwd(q, kv, proj, tabs, name=name + "rope")
    oh, lse = _attn_fwd(qh, kh, vh, name=name + "attn")
    an = _rms_fwd(oh, p['attn_g'], n_valid=MLA_WIDTH, name=name + "rms_attn")
    bm, cm, a = W['ssm'][l]
    bmb, cmb = bm.astype(bf16), cm.astype(bf16)
    y, xc = _ssm_fwd(proj, bmb, cmb, a, W['ssm_d'][l], name=name + "ssm")
    z, sn, ge = _glu_fwd(y, W['ssm_w_glu'][l], W['ssm_b_glu'][l], W['ssm_out_g'][l], name=name + "glu")
    h1a = _mm([(an, p['wout'][:MLA_PAD])], 'nn', f32, res=h, name=name + "mm_out_a")
    h1 = _mm([(sn, p['wout'][MLA_PAD:])], 'nn', f32, res=h1a, name=name + "mm_out_s")
    hn2 = _rms_fwd(h1, W['norm_x_g'][l], name=name + "rms_x")
    memn = _rms_fwd(memn_in, W['mem_norm_g'][l], name=name + "rms_mem")
    qx = _mm([(hn2, W['w_xq'][l])], 'nn', bf16, name=name + "mm_xq")
    kvx = _mm([(memn, W['w_xkv'][l])], 'nn', bf16, name=name + "mm_xkv")
    ox = _xattn_fwd(qx, kvx, name=name + "xattn")
    h2 = _mm([(ox, W['w_xo'][l])], 'nn', f32, res=h1, name=name + "mm_xo")
    hn3 = _rms_fwd(h2, W['norm_ffn_g'][l], name=name + "rms_ffn")
    gate, up, act = _ffn_up(hn3, W['w_gate'][l], W['w_up'][l], name=name + "ffn_up")
    h3 = _mm([(act, W['w_down'][l])], 'nn', f32, res=h2, name=name + "mm_down")
    sv.update(xn=xn, proj=proj, cqn=cqn, ckvn=ckvn, qh=qh, kh=kh, vh=vh, oh=oh, lse=lse, an=an, bmb=bmb, cmb=cmb,
              a=a, y=y, xc=xc, z=z, sn=sn, ge=ge, h1=h1, hn2=hn2, memn=memn, qx=qx, kvx=kvx, ox=ox, h2=h2, hn3=hn3,
              gate=gate, up=up, act=act)
    return h3, sv


def _backward_layer(dh3, sv, memn_in, tabs, W, l, name):
    p = sv['p']
    G = {}
    G['w_down'] = _mm([(sv['act'], dh3)], 'tn', f32, name=name + "dw_down")
    dgate, dup = _ffn_bwd_act(dh3, W['w_down'][l], sv['gate'], sv['up'], name=name + "ffn_bwd_act")
    dhn3 = _mm([(dgate, W['w_gate'][l]), (dup, W['w_up'][l])], 'nt', f32, name=name + "mm_dffn")
    G['w_gate'] = _mm([(sv['hn3'], dgate)], 'tn', f32, name=name + "dw_gate")
    G['w_up'] = _mm([(sv['hn3'], dup)], 'tn', f32, name=name + "dw_up")
    dh2, dg = _rms_bwd(sv['h2'], W['norm_ffn_g'][l], dhn3, res=dh3, name=name + "rmsb_ffn")
    G['norm_ffn_g'] = dg[0]
    G['w_xo'] = _mm([(sv['ox'], dh2)], 'tn', f32, name=name + "dw_xo")
    dox = _mm([(dh2, W['w_xo'][l])], 'nt', bf16, name=name + "mm_dxo")
    dqx, dkvx = _xattn_bwd(sv['qx'], sv['kvx'], dox, name=name + "xattn_bwd")
    G['w_xq'] = _mm([(sv['hn2'], dqx)], 'tn', f32, name=name + "dw_xq")
    G['w_xkv'] = _mm([(sv['memn'], dkvx)], 'tn', f32, name=name + "dw_xkv")
    dhn2 = _mm([(dqx, W['w_xq'][l])], 'nt', f32, name=name + "mm_dxq")
    dmemn = _mm([(dkvx, W['w_xkv'][l])], 'nt', f32, name=name + "mm_dxkv")
    dh1, dg = _rms_bwd(sv['h1'], W['norm_x_g'][l], dhn2, res=dh2, name=name + "rmsb_x")
    G['norm_x_g'] = dg[0]
    _, dg = _rms_bwd(memn_in, W['mem_norm_g'][l], dmemn, name=name + "rmsb_mem")
    G['mem_norm_g'] = dg[0]
    dwo_a = _mm([(sv['an'], dh1)], 'tn', f32, name=name + "dw_out_a")
    dwo_s = _mm([(sv['sn'], dh1)], 'tn', f32, name=name + "dw_out_s")
    G['w_out'] = _unpack_wout(jnp.concatenate([dwo_a, dwo_s], axis=0))
    dmixed = _mm([(dh1, p['wout'])], 'nt', f32, name=name + "mm_dout")
    dy, dz, dg, db = _glu_bwd(sv['y'], sv['z'], dmixed, W['ssm_w_glu'][l], W['ssm_out_g'][l], name=name + "glu_bwd")
    G['ssm_out_g'], G['ssm_b_glu'] = dg[0], db[0]
    G['ssm_w_glu'] = _mm([(sv['ge'], dz)], 'tn', f32, name=name + "dw_glu")
    du, dbm, dcm, da, dd = _ssm_bwd(sv['proj'], dy, sv['xc'], sv['bmb'], sv['cmb'], sv['a'], W['ssm_d'][l],
                                    name=name + "ssm_bwd")
    G['ssm_d'] = dd[0]
    G['ssm_raw'] = (dbm, dcm, da)
    doh, dg, delta = _rms_bwd(sv['oh'], p['attn_g'], dmixed, width=MLA_PAD, n_valid=MLA_WIDTH, delta=True,
                              out_dtype=bf16, name=name + "rmsb_attn")
    G['attn_out_g'] = _unpad_gain(dg[0])
    dqh, dkh, dvh = _attn_bwd(sv['qh'], sv['kh'], sv['vh'], doh, sv['lse'], delta, name=name + "attn_bwd")
    dq, dkv, dkr = _rope_bwd(dqh, dkh, dvh, tabs, name=name + "rope_bwd")
    G['w_uq'] = _unpad_heads(_mm([(sv['cqn'], dq)], 'tn', f32, name=name + "dw_uq"), QK_NOPE + QK_ROPE, _Q_PIECES)
    dwukv = _mm([(sv['ckvn'], dkv)], 'tn', f32, name=name + "dw_ukv")
    G['w_ukv'] = (_unpad_heads(dwukv[:, :MLA_PAD], QK_NOPE + V_HEAD, _K_PIECES)
                  + _unpad_heads(dwukv[:, MLA_PAD:], QK_NOPE + V_HEAD, _V_PIECES))
    dcqn = _mm([(dq, p['wuq'])], 'nt', f32, name=name + "mm_duq")
    dckvn = _mm([(dkv, p['wukv'])], 'nt', f32, name=name + "mm_dukv")
    dcq, dg = _rms_bwd(sv['proj'], W['q_norm_g'][l], dcqn, col0=0, width=Q_LORA, out_dtype=bf16, name=name + "rmsb_q")
    G['q_norm_g'] = dg[0]
    dckv, dg = _rms_bwd(sv['proj'], W['kv_norm_g'][l], dckvn, col0=Q_LORA, width=KV_LORA, out_dtype=bf16,
                        name=name + "rmsb_kv")
    G['kv_norm_g'] = dg[0]
    dproj = jnp.concatenate([dcq, dckv, dkr, du.astype(bf16)], axis=1)
    G['w_in'] = _unpack_win(_mm([(sv['xn'], dproj)], 'tn', f32, name=name + "dw_in"))
    dxn = _mm([(dproj, p['win'])], 'nt', f32, name=name + "mm_din")
    dh0, dg = _rms_bwd(sv['h0'], W['norm_mix_g'][l], dxn, res=dh1, name=name + "rmsb_mix")
    G['norm_mix_g'] = dg[0]
    return dh0, G


def _local_step(x, mem, positions, target, W):
    tabs = _rope_tables(positions)
    ssm_in = [(W['ssm_lambda_re'][l], W['ssm_lambda_im'][l], W['ssm_log_dt'][l], W['ssm_b_re'][l], W['ssm_b_im'][l],
               W['ssm_c_re'][l], W['ssm_c_im'][l]) for l in range(DEPTH)]
    preps = [jax.vjp(_ssm_prep, *ssm_in[l]) for l in range(DEPTH)]
    W = dict(W)
    W['ssm'] = [preps[l][0] for l in range(DEPTH)]
    h = x
    saved = []
    for l in range(DEPTH):
        h, sv = _forward_layer(h, mem, tabs, W, l, f"l{l}_")
        saved.append(sv)
    dh, dgf, loss = _loss_head(h, W['final_norm_g'], target)
    grads = [None] * DEPTH
    for l in reversed(range(DEPTH)):
        dh, G = _backward_layer(dh, saved[l], mem, tabs, W, l, f"l{l}b_")
        dbm, dcm, da = G.pop('ssm_raw')
        names = ['ssm_lambda_re', 'ssm_lambda_im', 'ssm_log_dt', 'ssm_b_re', 'ssm_b_im', 'ssm_c_re', 'ssm_c_im']
        for n, g in zip(names, preps[l][1]((dbm, dcm, da))):
            G[n] = g
        grads[l] = G
    out = {n: jnp.stack([grads[l][n] for l in range(DEPTH)]) for n in grads[0]}
    out['final_norm_g'] = dgf[0]
    return loss[0, 0], dh, out


_HBM = pl.BlockSpec(memory_space=pltpu.HBM)


def _me():
    return lax.axis_index("x"), lax.axis_index("y"), lax.axis_index("c")


def _chip_peers(x, y, c):
    devs = [(1 - x, y, c), (x, 1 - y, c), (1 - x, 1 - y, c)]
    return devs, [2 * d[0] + d[1] for d in devs]


def _allgather_chips(xs, name):
    n = len(xs)

    def body(*refs):
        src, dst = refs[:n], refs[n:2 * n]
        send, recv, loc = refs[2 * n:]
        x, y, c = _me()
        jme = 2 * x + y
        devs, js = _chip_peers(x, y, c)
        started = []
        for i in range(n):
            own = pltpu.make_async_copy(src[i], dst[i].at[:, pl.ds(jme, 1)], loc.at[i])
            own.start()
            started.append(own)
            for k in range(3):
                cp = pltpu.make_async_remote_copy(src[i], dst[i].at[:, pl.ds(jme, 1)], send.at[3 * i + k],
                                                  recv.at[3 * i + k], device_id=devs[k], device_id_type=MESH)
                cp.start()
                started.append(cp)
        for i in range(n):
            for k in range(3):
                pltpu.make_async_remote_copy(src[i], dst[i].at[:, pl.ds(js[k], 1)], send.at[3 * i + k],
                                             recv.at[3 * i + k], device_id=devs[k], device_id_type=MESH).wait_recv()
        for i in range(n):
            started[4 * i].wait()
            for k in range(3):
                started[4 * i + 1 + k].wait_send()

    ins = [t.reshape(t.shape[0], 1, *t.shape[1:]) for t in xs]
    outs = [jax.ShapeDtypeStruct((t.shape[0], 4, *t.shape[1:]), t.dtype) for t in xs]
    return _pc(body, in_specs=[_HBM] * n, out_specs=[_HBM] * n, out_shape=outs,
               scratch_shapes=[pltpu.SemaphoreType.DMA((3 * n,)), pltpu.SemaphoreType.DMA((3 * n,)),
                               pltpu.SemaphoreType.DMA((n,))],
               compiler_params=pltpu.CompilerParams(has_side_effects=True), name=name)(*ins)


def _exchange_halves(gs, name):
    n = len(gs)

    def body(*refs):
        src, dst = refs[:n], refs[n:2 * n]
        send, recv = refs[2 * n:]
        x, y, c = _me()
        cps = []
        for i in range(n):
            r2 = gs[i].shape[2] // 2
            rows = pl.ds(pl.multiple_of((1 - c) * r2, SUBLANES), r2)
            cp = pltpu.make_async_remote_copy(src[i].at[:, :, rows, :], dst[i], send.at[i], recv.at[i],
                                              device_id=(x, y, 1 - c), device_id_type=MESH)
            cp.start()
            cps.append(cp)
        for cp in cps:
            cp.wait()

    outs = [jax.ShapeDtypeStruct((2, 4, g.shape[2] // 2, g.shape[3]), g.dtype) for g in gs]
    return _pc(body, in_specs=[_HBM] * n, out_specs=[_HBM] * n, out_shape=outs,
               scratch_shapes=[pltpu.SemaphoreType.DMA((n,)), pltpu.SemaphoreType.DMA((n,))],
               compiler_params=pltpu.CompilerParams(has_side_effects=True), name=name)(*gs)


def _scatter_chips(ps, name):
    n = len(ps)

    def body(*refs):
        src, dst = refs[:n], refs[n:2 * n]
        send, recv = refs[2 * n:]
        x, y, c = _me()
        devs, js = _chip_peers(x, y, c)
        cps = []
        for i in range(n):
            for k in range(3):
                cp = pltpu.make_async_remote_copy(src[i].at[:, pl.ds(js[k], 1)], dst[i].at[k], send.at[3 * i + k],
                                                  recv.at[3 * i + k], device_id=devs[k], device_id_type=MESH)
                cp.start()
                cps.append(cp)
        for cp in cps:
            cp.wait()

    outs = [jax.ShapeDtypeStruct((3, 2, 1, p.shape[2], p.shape[3]), p.dtype) for p in ps]
    return _pc(body, in_specs=[_HBM] * n, out_specs=[_HBM] * n, out_shape=outs,
               scratch_shapes=[pltpu.SemaphoreType.DMA((3 * n,)), pltpu.SemaphoreType.DMA((3 * n,))],
               compiler_params=pltpu.CompilerParams(has_side_effects=True), name=name)(*ps)


def _join_halves(hs, name):
    n = len(hs)

    def body(*refs):
        src, dst = refs[:n], refs[n:2 * n]
        send, recv, loc = refs[2 * n:]
        x, y, c = _me()
        cps = []
        for i in range(n):
            r2 = hs[i].shape[1]
            mine = pl.ds(pl.multiple_of(c * r2, SUBLANES), r2)
            own = pltpu.make_async_copy(src[i], dst[i].at[:, mine, :], loc.at[i])
            own.start()
            cp = pltpu.make_async_remote_copy(src[i], dst[i].at[:, mine, :], send.at[i], recv.at[i],
                                              device_id=(x, y, 1 - c), device_id_type=MESH)
            cp.start()
            cps.append((own, cp))
        for i in range(n):
            r2 = hs[i].shape[1]
            theirs = pl.ds(pl.multiple_of((1 - c) * r2, SUBLANES), r2)
            pltpu.make_async_remote_copy(src[i], dst[i].at[:, theirs, :], send.at[i], recv.at[i],
                                         device_id=(x, y, 1 - c), device_id_type=MESH).wait_recv()
        for own, cp in cps:
            own.wait()
            cp.wait_send()

    outs = [jax.ShapeDtypeStruct((2, 2 * h.shape[1], h.shape[2]), h.dtype) for h in hs]
    return _pc(body, in_specs=[_HBM] * n, out_specs=[_HBM] * n, out_shape=outs,
               scratch_shapes=[pltpu.SemaphoreType.DMA((n,)), pltpu.SemaphoreType.DMA((n,)),
                               pltpu.SemaphoreType.DMA((n,))],
               compiler_params=pltpu.CompilerParams(has_side_effects=True), name=name)(*hs)


def _row_tile(r):
    return _pick(r, (256, 128, 64, 32, 16, 8))


def _add_half(g, r1, cidx, name):
    _, _, r, n = g.shape
    r2 = r // 2
    tr = _row_tile(r2)
    nb = r2 // tr

    def body(c_ref, g_ref, r_ref, o_ref):
        o_ref[...] = g_ref[...] + r_ref[...]

    blk = lambda f: pl.BlockSpec((None, None, tr, n), f)
    gs = pltpu.PrefetchScalarGridSpec(
        num_scalar_prefetch=1, grid=(2, 4, nb),
        in_specs=[blk(lambda l, j, i, c: (l, j, c[0] * nb + i, 0)), blk(lambda l, j, i, c: (l, j, i, 0))],
        out_specs=blk(lambda l, j, i, c: (l, j, i, 0)))
    return _pc(body, grid_spec=gs, out_shape=jax.ShapeDtypeStruct((2, 4, r2, n), f32),
               compiler_params=_cp(("parallel", "parallel", "parallel")), name=name)(cidx, g, r1)


def _add_chips(p, r3, jidx, name):
    _, _, r2, n = p.shape
    tr = _row_tile(r2)

    def body(j_ref, p_ref, a_ref, b_ref, c_ref, o_ref):
        o_ref[...] = ((p_ref[...] + a_ref[...]) + b_ref[...]) + c_ref[...]

    rblk = lambda k: pl.BlockSpec((None, None, None, tr, n), lambda l, i, j: (k, l, 0, i, 0))
    gs = pltpu.PrefetchScalarGridSpec(
        num_scalar_prefetch=1, grid=(2, r2 // tr),
        in_specs=[pl.BlockSpec((None, None, tr, n), lambda l, i, j: (l, j[0], i, 0)), rblk(0), rblk(1), rblk(2)],
        out_specs=pl.BlockSpec((None, tr, n), lambda l, i, j: (l, i, 0)))
    return _pc(body, grid_spec=gs, out_shape=jax.ShapeDtypeStruct((2, r2, n), f32),
               compiler_params=_cp(("parallel", "parallel")), name=name)(jidx, p, r3, r3, r3)


def _adamw(w, g, m, v, name):
    rows, n = w.shape
    tr = _row_tile(rows)
    c1 = 1.0 / (1.0 - ADAM_B1 ** ADAM_STEP)
    c2 = 1.0 / (1.0 - ADAM_B2 ** ADAM_STEP)

    def body(w_ref, g_ref, m_ref, v_ref, d_ref, mo_ref, vo_ref):
        gv = g_ref[...]
        m2 = ADAM_B1 * m_ref[...] + (1.0 - ADAM_B1) * gv
        v2 = ADAM_B2 * v_ref[...] + (1.0 - ADAM_B2) * (gv * gv)
        d_ref[...] = -ADAM_LR * ((m2 * c1) / (jnp.sqrt(v2 * c2) + ADAM_EPS) + ADAM_WD * w_ref[...])
        mo_ref[...] = m2
        vo_ref[...] = v2

    blk = pl.BlockSpec((tr, n), lambda i: (i, 0))
    return _pc(body, grid=(rows // tr,), in_specs=[blk] * 4, out_specs=[blk] * 3,
               out_shape=[jax.ShapeDtypeStruct((rows, n), f32)] * 3,
               compiler_params=_cp(("parallel",)), name=name)(w, g, m, v)


def _full_from_gathered(name, t):
    L, _, r, n = t.shape
    if SHARDED[name] == 1:
        return t.reshape(L, 4 * r, n)
    return t.transpose(0, 2, 1, 3).reshape(L, r, 4 * n)


def _shard_major(name, g):
    L, R, C = g.shape
    if SHARDED[name] == 1:
        return g.reshape(L, 4, R // 4, C)
    return g.reshape(L, R, 4, C // 4).transpose(0, 2, 1, 3)


_SMALL_ROWS = 288


def _pack_small(d):
    flat = jnp.concatenate([d[n].reshape(-1) for n in SMALL])
    total = 2 * 4 * _SMALL_ROWS * LANES
    flat = jnp.concatenate([flat, jnp.zeros((total - flat.shape[0],), f32)])
    return flat.reshape(2, 4, _SMALL_ROWS, LANES)


def _unpack_small(t, like):
    flat = t.reshape(-1)
    out, off = {}, 0
    for n in SMALL:
        sz = math.prod(like[n].shape)
        out[n] = flat[off:off + sz].reshape(like[n].shape)
        off += sz
    return out


def kernel(x, mem, positions, norm_mix_g, w_in, q_norm_g, w_uq, kv_norm_g, w_ukv, ssm_lambda_re, ssm_lambda_im, ssm_log_dt, ssm_b_re, ssm_b_im, ssm_c_re, ssm_c_im, ssm_d, ssm_w_glu, ssm_b_glu, attn_out_g, ssm_out_g, w_out, norm_x_g, mem_norm_g, w_xq, w_xkv, w_xo, norm_ffn_g, w_gate, w_up, w_down, final_norm_g, loss_target, m_norm_mix_g, m_w_in, m_q_norm_g, m_w_uq, m_kv_norm_g, m_w_ukv, m_ssm_lambda_re, m_ssm_lambda_im, m_ssm_log_dt, m_ssm_b_re, m_ssm_b_im, m_ssm_c_re, m_ssm_c_im, m_ssm_d, m_ssm_w_glu, m_ssm_b_glu, m_attn_out_g, m_ssm_out_g, m_w_out, m_norm_x_g, m_mem_norm_g, m_w_xq, m_w_xkv, m_w_xo, m_norm_ffn_g, m_w_gate, m_w_up, m_w_down, m_final_norm_g, v_norm_mix_g, v_w_in, v_q_norm_g, v_w_uq, v_kv_norm_g, v_w_ukv, v_ssm_lambda_re, v_ssm_lambda_im, v_ssm_log_dt, v_ssm_b_re, v_ssm_b_im, v_ssm_c_re, v_ssm_c_im, v_ssm_d, v_ssm_w_glu, v_ssm_b_glu, v_attn_out_g, v_ssm_out_g, v_w_out, v_norm_x_g, v_mem_norm_g, v_w_xq, v_w_xkv, v_w_xo, v_norm_ffn_g, v_w_gate, v_w_up, v_w_down, v_final_norm_g):
    given = dict(locals())
    w = {n: given[n] for n in WEIGHTS}
    m = {n: given["m_" + n] for n in WEIGHTS}
    v = {n: given["v_" + n] for n in WEIGHTS}
    big = list(SHARDED)

    gathered = _allgather_chips([w[n].astype(bf16) for n in big], "allgather_weights")
    W = {n: _full_from_gathered(n, t) for n, t in zip(big, gathered)}
    W.update({n: w[n] for n in SMALL})

    loss, dx, grads = _local_step(x[0], mem[0], positions[0], loss_target[0], W)
    loss = lax.psum(loss, ("x", "y", "c"))

    cidx = lax.axis_index("c").astype(jnp.int32).reshape(1)
    jidx = (2 * lax.axis_index("x") + lax.axis_index("y")).astype(jnp.int32).reshape(1)
    names = big + ["small"]
    gs = [_shard_major(n, grads[n]) for n in big] + [_pack_small(grads)]
    r1 = _exchange_halves(gs, "grad_exchange_halves")
    ps = [_add_half(g, r, cidx, f"grad_add_half_{n}") for n, g, r in zip(names, gs, r1)]
    r3 = _scatter_chips(ps, "grad_scatter_chips")
    hs = [_add_chips(p, r, jidx, f"grad_add_chips_{n}") for n, p, r in zip(names, ps, r3)]
    gfull = _join_halves(hs, "grad_join_halves")

    out_g, out_d, out_m, out_v = {}, {}, {}, {}
    for n, g in zip(big, gfull[:-1]):
        shp = w[n].shape
        two = lambda t: t.reshape(shp[0] * shp[1], shp[2])
        d_, m_, v_ = _adamw(two(w[n]), two(g), two(m[n]), two(v[n]), f"adamw_{n}")
        out_g[n], out_d[n], out_m[n], out_v[n] = g, d_.reshape(shp), m_.reshape(shp), v_.reshape(shp)
    gsm = _allgather_chips([gfull[-1]], "allgather_small")[0]
    rows2 = 2 * 4 * _SMALL_ROWS
    flat = lambda d: _pack_small(d).reshape(rows2, LANES)
    d_, m_, v_ = _adamw(flat(w), gsm.reshape(rows2, LANES), flat(m), flat(v), "adamw_small")
    for dst, t in ((out_g, gsm), (out_d, d_), (out_m, m_), (out_v, v_)):
        dst.update(_unpack_small(t, w))

    return (loss, dx.reshape(x.shape), *[out_g[n] for n in WEIGHTS], *[out_d[n] for n in WEIGHTS],
            *[out_m[n] for n in WEIGHTS], *[out_v[n] for n in WEIGHTS])
```

```python
import functools
import math

import jax
import jax.numpy as jnp
from jax import lax
from jax.experimental import pallas as pl
from jax.experimental.pallas import tpu as pltpu

f32, bf16 = jnp.float32, jnp.bfloat16

D_MODEL = 1024
DEPTH = 2
MLA_HEADS = 8
QK_NOPE = 64
QK_ROPE = 32
V_HEAD = 64
Q_LORA = 256
KV_LORA = 128
MLA_WIDTH = MLA_HEADS * V_HEAD
ROPE_THETA = 10000.0
SSM_WIDTH = 512
SSM_GROUP = 16
SSM_GROUPS = 32
SSM_STATE = 64
IN_WIDTH = Q_LORA + KV_LORA + QK_ROPE + SSM_WIDTH
X_HEADS = 4
X_HEAD_DIM = D_MODEL // X_HEADS
D_FF = 2816
EPS = 1e-6
ADAM_LR, ADAM_B1, ADAM_B2, ADAM_EPS, ADAM_WD, ADAM_STEP = 0.001, 0.9, 0.999, 1e-08, 0.01, 10

LANES = 128
SUBLANES = 8
HEAD_PAD = 128
MLA_PAD = MLA_HEADS * HEAD_PAD
SSM_MACRO = 4
MACRO_CH = SSM_WIDTH // SSM_MACRO
MACRO_ST = SSM_GROUPS // SSM_MACRO * SSM_STATE
VMEM_LIMIT = 56 * 1024 * 1024
GRAD_TRANSIT = bf16

WEIGHTS = ['norm_mix_g', 'w_in', 'q_norm_g', 'w_uq', 'kv_norm_g', 'w_ukv', 'ssm_lambda_re', 'ssm_lambda_im',
           'ssm_log_dt', 'ssm_b_re', 'ssm_b_im', 'ssm_c_re', 'ssm_c_im', 'ssm_d', 'ssm_w_glu', 'ssm_b_glu',
           'attn_out_g', 'ssm_out_g', 'w_out', 'norm_x_g', 'mem_norm_g', 'w_xq', 'w_xkv', 'w_xo', 'norm_ffn_g',
           'w_gate', 'w_up', 'w_down', 'final_norm_g']
SHARDED = {'w_in': 1, 'w_uq': 2, 'w_ukv': 2, 'ssm_w_glu': 1, 'w_out': 1, 'w_xq': 1, 'w_xkv': 2, 'w_xo': 1,
           'w_gate': 2, 'w_up': 2, 'w_down': 1}
SMALL = [n for n in WEIGHTS if n not in SHARDED]
MESH = pl.DeviceIdType.MESH


def _pc(body, **kw):
    return pl.pallas_call(body, **kw)


def _pick(n, prefs):
    for p in prefs:
        if n % p == 0:
            return p
    return n


def _cp(sem=None):
    return pltpu.CompilerParams(dimension_semantics=sem, vmem_limit_bytes=VMEM_LIMIT)


_TILE_CANDS = (1024, 1408, 512, 256, 128)
MM_VMEM_BUDGET = 40 * 1024 * 1024


def _mm_tiles(M, K, N, a_bytes, b_bytes, o_bytes, npair, has_res, need_acc):
    best = None
    for tm in _TILE_CANDS:
        for tk in _TILE_CANDS:
            if M % tm or K % tk:
                continue
            vm = npair * (2 * tm * tk * a_bytes + 2 * tk * N * b_bytes) + 2 * tm * N * o_bytes
            vm += tm * N * 4 * (1 + need_acc + 2 * has_res)
            if a_bytes == 4:
                vm += npair * tm * tk * 2
            if b_bytes == 4:
                vm += npair * tk * N * 2
            if vm <= MM_VMEM_BUDGET and (best is None or tm * tk > best[0]):
                best = (tm * tk, tm, tk)
    if best is None:
        return _pick(M, (256, 128)), _pick(K, (256, 128))
    return best[1], best[2]


def _mm(pairs, mode, out_dtype, res=None, name="mm"):
    a0, b0 = pairs[0]
    if mode == 'nn':
        (M, K), N = a0.shape, b0.shape[1]
        dims = (((1,), (0,)), ((), ()))
    elif mode == 'nt':
        (M, K), N = a0.shape, b0.shape[0]
        dims = (((1,), (1,)), ((), ()))
    else:
        (K, M), N = a0.shape, b0.shape[1]
        dims = (((0,), (0,)), ((), ()))
    npair = len(pairs)
    has_res = res is not None
    direct = out_dtype == f32
    tm, tk = _mm_tiles(M, K, N, a0.dtype.itemsize, b0.dtype.itemsize, jnp.dtype(out_dtype).itemsize, npair, has_res,
                       not direct)
    nk = K // tk

    def body(*refs):
        ins = refs[:2 * npair]
        res_ref = refs[2 * npair] if has_res else None
        o_ref = refs[2 * npair + has_res]
        acc = o_ref if direct else refs[2 * npair + has_res + 1]
        k = pl.program_id(1)
        s = None
        for p in range(npair):
            d = lax.dot_general(ins[2 * p][...].astype(bf16), ins[2 * p + 1][...].astype(bf16), dims,
                                preferred_element_type=f32)
            s = d if s is None else s + d

        @pl.when(k == 0)
        def _():
            acc[...] = s

        @pl.when(k > 0)
        def _():
            acc[...] += s

        if has_res or not direct:
            @pl.when(k == nk - 1)
            def _():
                r = acc[...]
                if has_res:
                    r = r + res_ref[...]
                o_ref[...] = r.astype(out_dtype)

    if mode == 'nn':
        a_spec = pl.BlockSpec((tm, tk), lambda i, k: (i, k))
        b_spec = pl.BlockSpec((tk, N), lambda i, k: (k, 0))
    elif mode == 'nt':
        a_spec = pl.BlockSpec((tm, tk), lambda i, k: (i, k))
        b_spec = pl.BlockSpec((N, tk), lambda i, k: (0, k))
    else:
        a_spec = pl.BlockSpec((tk, tm), lambda i, k: (k, i))
        b_spec = pl.BlockSpec((tk, N), lambda i, k: (k, 0))
    o_spec = pl.BlockSpec((tm, N), lambda i, k: (i, 0))
    in_specs = [a_spec, b_spec] * npair + ([o_spec] if has_res else [])
    args = [t for p in pairs for t in p] + ([res] if has_res else [])
    return _pc(body, grid=(M // tm, nk), in_specs=in_specs, out_specs=o_spec,
               out_shape=jax.ShapeDtypeStruct((M, N), out_dtype),
               scratch_shapes=[] if direct else [pltpu.VMEM((tm, N), f32)],
               compiler_params=_cp(("parallel", "arbitrary")), name=name)(*args)


def _rms_fwd(x, g, *, col0=0, width=None, n_valid=None, out_dtype=bf16, name="rms_fwd"):
    S = x.shape[0]
    width = width or x.shape[1]
    n_valid = n_valid or width
    ts = _pick(S, (512, 256, 128))
    cb = col0 // width

    def body(x_ref, g_ref, o_ref):
        xv = x_ref[...]
        ms = jnp.sum(xv * xv, axis=-1, keepdims=True) * (1.0 / n_valid)
        o_ref[...] = (xv * lax.rsqrt(ms + EPS) * g_ref[...]).astype(out_dtype)

    return _pc(body, grid=(S // ts,),
               in_specs=[pl.BlockSpec((ts, width), lambda i: (i, cb)), pl.BlockSpec((1, width), lambda i: (0, 0))],
               out_specs=pl.BlockSpec((ts, width), lambda i: (i, 0)),
               out_shape=jax.ShapeDtypeStruct((S, width), out_dtype),
               compiler_params=_cp(("parallel",)), name=name)(x, g.reshape(1, width))


def _rms_bwd(x, g, dy, *, col0=0, dcol0=0, width=None, n_valid=None, res=None, out_dtype=f32, delta=False,
             name="rms_bwd"):
    S = x.shape[0]
    width = width or x.shape[1]
    n_valid = n_valid or width
    ts = _pick(S, (512, 256, 128))
    cb, dcb = col0 // width, dcol0 // width
    has_res = res is not None

    def body(*refs):
        x_ref, g_ref, dy_ref = refs[:3]
        res_ref = refs[3] if has_res else None
        outs = refs[3 + has_res:]
        dx_ref, dg_ref = outs[0], outs[1]
        i = pl.program_id(0)
        xv = x_ref[...]
        gv = g_ref[...]
        dyv = dy_ref[...].astype(f32)
        rstd = lax.rsqrt(jnp.sum(xv * xv, axis=-1, keepdims=True) * (1.0 / n_valid) + EPS)
        xh = xv * rstd
        dxh = dyv * gv
        mean = jnp.sum(dxh * xh, axis=-1, keepdims=True) * (1.0 / n_valid)
        dx = rstd * (dxh - xh * mean)
        if delta:
            d_ref = outs[2]
            for h in range(width // LANES):
                sl = slice(h * LANES, (h + 1) * LANES)
                dsum = jnp.sum(dx[:, sl] * xv[:, sl], axis=-1, keepdims=True)
                d_ref[:, sl] = jnp.broadcast_to(dsum, (ts, LANES))
        if has_res:
            dx = dx + res_ref[...]
        dx_ref[...] = dx.astype(out_dtype)

        @pl.when(i == 0)
        def _():
            dg_ref[...] = jnp.zeros_like(dg_ref)

        dg_ref[...] += jnp.sum(dyv * xh, axis=0, keepdims=True)

    blk = lambda c: pl.BlockSpec((ts, width), lambda i: (i, c))
    in_specs = [blk(cb), pl.BlockSpec((1, width), lambda i: (0, 0)), blk(dcb)] + ([blk(0)] if has_res else [])
    out_specs = [blk(0), pl.BlockSpec((1, width), lambda i: (0, 0))] + ([blk(0)] if delta else [])
    out_shape = [jax.ShapeDtypeStruct((S, width), out_dtype), jax.ShapeDtypeStruct((1, width), f32)] + (
        [jax.ShapeDtypeStruct((S, width), f32)] if delta else [])
    args = [x, g.reshape(1, width), dy] + ([res] if has_res else [])
    return _pc(body, grid=(S // ts,), in_specs=in_specs, out_specs=out_specs, out_shape=out_shape,
               compiler_params=_cp(("arbitrary",)), name=name)(*args)


def _loss_head(h, g, target, name="loss_head"):
    S, D = h.shape
    ts = _pick(S, (512, 256, 128))

    def body(h_ref, g_ref, t_ref, dh_ref, dg_ref, loss_ref):
        i = pl.program_id(0)
        xv = h_ref[...]
        gv = g_ref[...]
        rstd = lax.rsqrt(jnp.sum(xv * xv, axis=-1, keepdims=True) * (1.0 / D) + EPS)
        xh = xv * rstd
        err = xh * gv - t_ref[...]
        dyv = err * (1.0 / D)
        dxh = dyv * gv
        mean = jnp.sum(dxh * xh, axis=-1, keepdims=True) * (1.0 / D)
        dh_ref[...] = rstd * (dxh - xh * mean)

        @pl.when(i == 0)
        def _():
            dg_ref[...] = jnp.zeros_like(dg_ref)
            loss_ref[...] = jnp.zeros_like(loss_ref)

        dg_ref[...] += jnp.sum(dyv * xh, axis=0, keepdims=True)
        part = jnp.sum(jnp.sum(err * err, axis=-1, keepdims=True), axis=0, keepdims=True) * (0.5 / D)
        loss_ref[...] += jnp.broadcast_to(part, (1, LANES))

    blk = pl.BlockSpec((ts, D), lambda i: (i, 0))
    row = pl.BlockSpec((1, D), lambda i: (0, 0))
    return _pc(body, grid=(S // ts,), in_specs=[blk, row, blk],
               out_specs=[blk, row, pl.BlockSpec((1, LANES), lambda i: (0, 0))],
               out_shape=[jax.ShapeDtypeStruct((S, D), f32), jax.ShapeDtypeStruct((1, D), f32),
                          jax.ShapeDtypeStruct((1, LANES), f32)],
               compiler_params=_cp(("arbitrary",)), name=name)(h, g.reshape(1, D), target)


def _rope_apply(x, tc, s1, s2):
    return x * tc + pltpu.roll(x, LANES - 16, 1) * s1 + pltpu.roll(x, 16, 1) * s2


def _rope_apply_t(dy, tc, s1, s2):
    return dy * tc + pltpu.roll(dy * s1, 16, 1) + pltpu.roll(dy * s2, LANES - 16, 1)


def _rope_fwd(q, kv, proj, tabs, name="rope_fwd"):
    S = q.shape[0]
    ts = _pick(S, (512, 256, 128))
    scale = (QK_NOPE + QK_ROPE) ** -0.5

    def body(q_ref, kk_ref, kvv_ref, kr_ref, tc_ref, s1_ref, s2_ref, qh_ref, kh_ref, vh_ref):
        tc, s1, s2 = tc_ref[...], s1_ref[...], s2_ref[...]
        krr = _rope_apply(pltpu.roll(kr_ref[...], QK_NOPE, 1), tc, s1, s2)
        for h in range(MLA_HEADS):
            sl = slice(h * HEAD_PAD, (h + 1) * HEAD_PAD)
            qh_ref[:, sl] = (_rope_apply(q_ref[:, sl], tc, s1, s2) * scale).astype(bf16)
            kh_ref[:, sl] = (kk_ref[:, sl] + krr).astype(bf16)
        vh_ref[...] = kvv_ref[...].astype(bf16)

    wide = lambda c: pl.BlockSpec((ts, MLA_PAD), lambda i: (i, c))
    tab = pl.BlockSpec((ts, LANES), lambda i: (i, 0))
    return _pc(body, grid=(S // ts,),
               in_specs=[wide(0), wide(0), wide(1), pl.BlockSpec((ts, LANES), lambda i: (i, 3)), tab, tab, tab],
               out_specs=[wide(0)] * 3, out_shape=[jax.ShapeDtypeStruct((S, MLA_PAD), bf16)] * 3,
               compiler_params=_cp(("parallel",)), name=name)(q, kv, kv, proj, *tabs)


def _rope_bwd(dqh, dkh, dvh, tabs, name="rope_bwd"):
    S = dqh.shape[0]
    ts = _pick(S, (512, 256, 128))
    scale = (QK_NOPE + QK_ROPE) ** -0.5

    def body(dq_ref, dk_ref, dv_ref, tc_ref, s1_ref, s2_ref, oq_ref, okv_ref, okr_ref):
        tc, s1, s2 = tc_ref[...], s1_ref[...], s2_ref[...]
        ksum = None
        for h in range(MLA_HEADS):
            sl = slice(h * HEAD_PAD, (h + 1) * HEAD_PAD)
            oq_ref[:, sl] = (_rope_apply_t(dq_ref[:, sl], tc, s1, s2) * scale).astype(bf16)
            dk = dk_ref[:, sl]
            okv_ref[:, sl] = dk.astype(bf16)
            ksum = dk if ksum is None else ksum + dk
        okv_ref[:, MLA_PAD:] = dv_ref[...].astype(bf16)
        dkr = pltpu.roll(_rope_apply_t(ksum, tc, s1, s2), LANES - QK_NOPE, 1)
        lane = lax.broadcasted_iota(jnp.int32, (ts, LANES), 1)
        okr_ref[...] = jnp.where(lane < QK_ROPE, dkr, 0.0).astype(bf16)

    wide = pl.BlockSpec((ts, MLA_PAD), lambda i: (i, 0))
    tab = pl.BlockSpec((ts, LANES), lambda i: (i, 0))
    return _pc(body, grid=(S // ts,), in_specs=[wide, wide, wide, tab, tab, tab],
               out_specs=[wide, pl.BlockSpec((ts, 2 * MLA_PAD), lambda i: (i, 0)), tab],
               out_shape=[jax.ShapeDtypeStruct((S, MLA_PAD), bf16), jax.ShapeDtypeStruct((S, 2 * MLA_PAD), bf16),
                          jax.ShapeDtypeStruct((S, LANES), bf16)],
               compiler_params=_cp(("parallel",)), name=name)(dqh, dkh, dvh, *tabs)


ATT_BLK = 1024


def _attn_fwd(qh, kh, vh, name="attn_fwd"):
    S = qh.shape[0]
    tq = tk = min(S, ATT_BLK)
    nq, nk = S // tq, S // tk

    def body(q_ref, k_ref, v_ref, o_ref, lse_ref, m_sc, l_sc, acc_sc):
        i, j = pl.program_id(1), pl.program_id(2)

        @pl.when(j == 0)
        def _():
            m_sc[...] = jnp.full_like(m_sc, -1e30)
            l_sc[...] = jnp.zeros_like(l_sc)
            acc_sc[...] = jnp.zeros_like(acc_sc)

        def step(masked):
            s = lax.dot_general(q_ref[...], k_ref[...], (((1,), (1,)), ((), ())), preferred_element_type=f32)
            if masked:
                row = lax.broadcasted_iota(jnp.int32, (tq, tk), 0)
                col = lax.broadcasted_iota(jnp.int32, (tq, tk), 1)
                s = jnp.where(col <= row, s, -1e30)
            m_prev = m_sc[...]
            m_new = jnp.maximum(m_prev, jnp.max(s, axis=-1, keepdims=True))
            alpha = jnp.exp(m_prev - m_new)
            p = jnp.exp(s - m_new)
            l_sc[...] = alpha * l_sc[...] + jnp.sum(p, axis=-1, keepdims=True)
            acc_sc[...] = alpha * acc_sc[...] + jnp.dot(p.astype(bf16), v_ref[...], preferred_element_type=f32)
            m_sc[...] = m_new

        pl.when(j < i)(functools.partial(step, False))
        pl.when(j == i)(functools.partial(step, True))

        @pl.when(j == nk - 1)
        def _():
            l = l_sc[...]
            o_ref[...] = acc_sc[...] / l
            lse_ref[...] = jnp.broadcast_to(m_sc[...] + jnp.log(l), (tq, LANES))

    qspec = pl.BlockSpec((tq, HEAD_PAD), lambda h, i, j: (i, h))
    kspec = pl.BlockSpec((tk, HEAD_PAD), lambda h, i, j: (jnp.minimum(j, i), h))
    return _pc(body, grid=(MLA_HEADS, nq, nk), in_specs=[qspec, kspec, kspec], out_specs=[qspec, qspec],
               out_shape=[jax.ShapeDtypeStruct((S, MLA_PAD), f32)] * 2,
               scratch_shapes=[pltpu.VMEM((tq, 1), f32), pltpu.VMEM((tq, 1), f32), pltpu.VMEM((tq, HEAD_PAD), f32)],
               compiler_params=_cp(("parallel", "parallel", "arbitrary")), name=name)(qh, kh, vh)


def _attn_bwd(qh, kh, vh, do, lse, delta, name="attn_bwd"):
    S = qh.shape[0]
    tq = tk = min(S, ATT_BLK)
    nq, nk = S // tq, S // tk

    def body(q_ref, k_ref, v_ref, do_ref, lse_ref, dl_ref, dq_ref, dk_ref, dv_ref):
        j, i = pl.program_id(1), pl.program_id(2)

        @pl.when((j == 0) & (i == 0))
        def _():
            dq_ref[...] = jnp.zeros_like(dq_ref)

        @pl.when(i == 0)
        def _():
            dk_ref[...] = jnp.zeros_like(dk_ref)
            dv_ref[...] = jnp.zeros_like(dv_ref)

        def step(masked):
            nt = (((1,), (1,)), ((), ()))
            tn = (((0,), (0,)), ((), ()))
            qv, kv_, dov = q_ref[...], k_ref[...], do_ref[...]
            s = lax.dot_general(qv, kv_, nt, preferred_element_type=f32)
            p = jnp.exp(s - lse_ref[:, :1])
            if masked:
                row = lax.broadcasted_iota(jnp.int32, (tq, tk), 0)
                col = lax.broadcasted_iota(jnp.int32, (tq, tk), 1)
                p = jnp.where(col <= row, p, 0.0)
            dp = lax.dot_general(dov, v_ref[...], nt, preferred_element_type=f32)
            ds = (p * (dp - dl_ref[:, :1])).astype(bf16)
            dv_ref[...] += lax.dot_general(p.astype(bf16), dov, tn, preferred_element_type=f32)
            dk_ref[...] += lax.dot_general(ds, qv, tn, preferred_element_type=f32)
            rows = pl.ds(pl.multiple_of(i * tq, tq), tq)
            dq_ref[rows, :] += jnp.dot(ds, kv_, preferred_element_type=f32)

        pl.when(i > j)(functools.partial(step, False))
        pl.when(i == j)(functools.partial(step, True))

    qspec = pl.BlockSpec((tq, HEAD_PAD), lambda h, j, i: (jnp.maximum(i, j), h))
    kspec = pl.BlockSpec((tk, HEAD_PAD), lambda h, j, i: (j, h))
    colspec = pl.BlockSpec((S, HEAD_PAD), lambda h, j, i: (0, h))
    return _pc(body, grid=(MLA_HEADS, nk, nq), in_specs=[qspec, kspec, kspec, qspec, qspec, qspec],
               out_specs=[colspec, kspec, kspec], out_shape=[jax.ShapeDtypeStruct((S, MLA_PAD), f32)] * 3,
               compiler_params=_cp(("parallel", "arbitrary", "arbitrary")), name=name)(qh, kh, vh, do, lse, delta)


def _xattn_fwd(q, kv, name="xattn_fwd"):
    S = q.shape[0]
    M = kv.shape[0]
    tq = _pick(S, (256, 128))
    scale = X_HEAD_DIM ** -0.5

    def body(q_ref, kv_ref, o_ref):
        for h in range(X_HEADS):
            sl = slice(h * X_HEAD_DIM, (h + 1) * X_HEAD_DIM)
            k = kv_ref[:, sl]
            v = kv_ref[:, D_MODEL + h * X_HEAD_DIM:D_MODEL + (h + 1) * X_HEAD_DIM]
            s = lax.dot_general(q_ref[:, sl], k, (((1,), (1,)), ((), ())), preferred_element_type=f32) * scale
            e = jnp.exp(s - jnp.max(s, axis=-1, keepdims=True))
            p = e / jnp.sum(e, axis=-1, keepdims=True)
            o_ref[:, sl] = jnp.dot(p.astype(bf16), v, preferred_element_type=f32).astype(bf16)

    blk = pl.BlockSpec((tq, D_MODEL), lambda i: (i, 0))
    return _pc(body, grid=(S // tq,), in_specs=[blk, pl.BlockSpec((M, 2 * D_MODEL), lambda i: (0, 0))],
               out_specs=blk, out_shape=jax.ShapeDtypeStruct((S, D_MODEL), bf16),
               compiler_params=_cp(("parallel",)), name=name)(q, kv)


def _xattn_bwd(q, kv, do, name="xattn_bwd"):
    S = q.shape[0]
    M = kv.shape[0]
    tq = _pick(S, (256, 128))
    scale = X_HEAD_DIM ** -0.5

    def body(q_ref, kv_ref, do_ref, dq_ref, dkv_ref):
        i = pl.program_id(0)

        @pl.when(i == 0)
        def _():
            dkv_ref[...] = jnp.zeros_like(dkv_ref)

        nt = (((1,), (1,)), ((), ()))
        tn = (((0,), (0,)), ((), ()))
        for h in range(X_HEADS):
            sl = slice(h * X_HEAD_DIM, (h + 1) * X_HEAD_DIM)
            vsl = slice(D_MODEL + h * X_HEAD_DIM, D_MODEL + (h + 1) * X_HEAD_DIM)
            k, v, qv, dov = kv_ref[:, sl], kv_ref[:, vsl], q_ref[:, sl], do_ref[:, sl]
            s = lax.dot_general(qv, k, nt, preferred_element_type=f32) * scale
            e = jnp.exp(s - jnp.max(s, axis=-1, keepdims=True))
            p = e / jnp.sum(e, axis=-1, keepdims=True)
            dp = lax.dot_general(dov, v, nt, preferred_element_type=f32)
            ds = (p * (dp - jnp.sum(dp * p, axis=-1, keepdims=True)) * scale).astype(bf16)
            dq_ref[:, sl] = jnp.dot(ds, k, preferred_element_type=f32).astype(bf16)
            dkv_ref[:, sl] += lax.dot_general(ds, qv, tn, preferred_element_type=f32)
            dkv_ref[:, vsl] += lax.dot_general(p.astype(bf16), dov, tn, preferred_element_type=f32)

    blk = pl.BlockSpec((tq, D_MODEL), lambda i: (i, 0))
    full = pl.BlockSpec((M, 2 * D_MODEL), lambda i: (0, 0))
    return _pc(body, grid=(S // tq,), in_specs=[blk, full, blk], out_specs=[blk, full],
               out_shape=[jax.ShapeDtypeStruct((S, D_MODEL), bf16), jax.ShapeDtypeStruct((M, 2 * D_MODEL), f32)],
               compiler_params=_cp(("arbitrary",)), name=name)(q, kv, do)


def _apow_init(a_ref, ap_ref):
    ar, ai = a_ref[:, :MACRO_ST], a_ref[:, MACRO_ST:]
    pr, pi = ar, ai
    for r in range(SUBLANES):
        ap_ref[r:r + 1, :MACRO_ST] = pr
        ap_ref[r:r + 1, MACRO_ST:] = pi
        pr, pi = pr * ar - pi * ai, pr * ai + pi * ar


def _scan_block(src_ref, dst_ref, ap_ref, carry, n_rows, reverse):
    P = MACRO_ST
    n_chunks = n_rows // SUBLANES
    rid = lax.broadcasted_iota(jnp.int32, (SUBLANES, P), 0)
    sgn = -1.0 if reverse else 1.0
    if reverse:
        apr = jnp.zeros((SUBLANES, P), f32)
        api = jnp.zeros((SUBLANES, P), f32)
        for r in range(SUBLANES):
            apr = jnp.where(rid == r, ap_ref[SUBLANES - 1 - r:SUBLANES - r, :P], apr)
            api = jnp.where(rid == r, -ap_ref[SUBLANES - 1 - r:SUBLANES - r, P:], api)
    else:
        apr, api = ap_ref[:, :P], ap_ref[:, P:]

    def chunk(c, carry):
        cr, ci = carry
        r0 = pl.multiple_of(((n_chunks - 1 - c) if reverse else c) * SUBLANES, SUBLANES)
        xr = src_ref[pl.ds(r0, SUBLANES), :P]
        xi = src_ref[pl.ds(r0, SUBLANES), P:]
        for sh in (1, 2, 4):
            pr = ap_ref[sh - 1:sh, :P]
            pi = ap_ref[sh - 1:sh, P:] * sgn
            if reverse:
                sr = jnp.where(rid < SUBLANES - sh, pltpu.roll(xr, SUBLANES - sh, 0), 0.0)
                si = jnp.where(rid < SUBLANES - sh, pltpu.roll(xi, SUBLANES - sh, 0), 0.0)
            else:
                sr = jnp.where(rid >= sh, pltpu.roll(xr, sh, 0), 0.0)
                si = jnp.where(rid >= sh, pltpu.roll(xi, sh, 0), 0.0)
            xr, xi = xr + pr * sr - pi * si, xi + pr * si + pi * sr
        xr = xr + apr * cr - api * ci
        xi = xi + apr * ci + api * cr
        dst_ref[pl.ds(r0, SUBLANES), :P] = xr
        dst_ref[pl.ds(r0, SUBLANES), P:] = xi
        edge = 0 if reverse else SUBLANES - 1
        ncr = jnp.sum(jnp.where(rid == edge, xr, 0.0), axis=0, keepdims=True)
        nci = jnp.sum(jnp.where(rid == edge, xi, 0.0), axis=0, keepdims=True)
        return ncr, nci

    return lax.fori_loop(0, n_chunks, chunk, carry)


def _ssm_fwd(proj, bm, cm, a, d, name="ssm_fwd"):
    S = proj.shape[0]
    tS = _pick(S, (256, 128))
    nb = S // tS
    P2 = 2 * MACRO_ST
    ucol0 = (D_MODEL - SSM_WIDTH) // MACRO_CH

    def body(u_ref, b_ref, c_ref, a_ref, d_ref, y_ref, xc_ref, bu_sc, ap_sc, car_sc):
        t = pl.program_id(1)

        @pl.when(t == 0)
        def _():
            _apow_init(a_ref, ap_sc)
            car_sc[...] = jnp.zeros_like(car_sc)

        uv = u_ref[...]
        bu_sc[...] = jnp.dot(uv.astype(bf16), b_ref[...], preferred_element_type=f32)
        xc_ref[...] = car_sc[...]
        cr, ci = _scan_block(bu_sc, bu_sc, ap_sc, (car_sc[:, :MACRO_ST], car_sc[:, MACRO_ST:]), tS, False)
        car_sc[:, :MACRO_ST] = cr
        car_sc[:, MACRO_ST:] = ci
        y_ref[...] = jnp.dot(bu_sc[...].astype(bf16), c_ref[...], preferred_element_type=f32) + d_ref[...] * uv

    return _pc(body, grid=(SSM_MACRO, nb),
               in_specs=[pl.BlockSpec((tS, MACRO_CH), lambda m, t: (t, ucol0 + m)),
                         pl.BlockSpec((None, MACRO_CH, P2), lambda m, t: (m, 0, 0)),
                         pl.BlockSpec((None, P2, MACRO_CH), lambda m, t: (m, 0, 0)),
                         pl.BlockSpec((None, 1, P2), lambda m, t: (m, 0, 0)),
                         pl.BlockSpec((1, MACRO_CH), lambda m, t: (0, m))],
               out_specs=[pl.BlockSpec((tS, MACRO_CH), lambda m, t: (t, m)),
                          pl.BlockSpec((None, None, 1, P2), lambda m, t: (m, t, 0, 0))],
               out_shape=[jax.ShapeDtypeStruct((S, SSM_WIDTH), f32), jax.ShapeDtypeStruct((SSM_MACRO, nb, 1, P2), f32)],
               scratch_shapes=[pltpu.VMEM((tS, P2), f32), pltpu.VMEM((SUBLANES, P2), f32), pltpu.VMEM((1, P2), f32)],
               compiler_params=_cp(("arbitrary", "arbitrary")), name=name)(proj, bm, cm, a, d.reshape(1, SSM_WIDTH))


def _ssm_bwd(proj, dy, xc, bm, cm, a, d, name="ssm_bwd"):
    S = proj.shape[0]
    tS = _pick(S, (256, 128))
    nb = S // tS
    P = MACRO_ST
    P2 = 2 * P
    ucol0 = (D_MODEL - SSM_WIDTH) // MACRO_CH

    def body(u_ref, dy_ref, xc_ref, b_ref, c_ref, a_ref, d_ref, du_ref, db_ref, dc_ref, da_ref, dd_ref,
             x_sc, g_sc, ap_sc, gcar_sc):
        t = pl.program_id(1)

        @pl.when(t == 0)
        def _():
            _apow_init(a_ref, ap_sc)
            gcar_sc[...] = jnp.zeros_like(gcar_sc)
            db_ref[...] = jnp.zeros_like(db_ref)
            dc_ref[...] = jnp.zeros_like(dc_ref)
            da_ref[...] = jnp.zeros_like(da_ref)
            dd_ref[...] = jnp.zeros_like(dd_ref)

        nt = (((1,), (1,)), ((), ()))
        tn = (((0,), (0,)), ((), ()))
        uv = u_ref[...]
        ub = uv.astype(bf16)
        dyv = dy_ref[...]
        dyb = dyv.astype(bf16)
        x_sc[...] = jnp.dot(ub, b_ref[...], preferred_element_type=f32)
        xin = xc_ref[...]
        _scan_block(x_sc, x_sc, ap_sc, (xin[:, :P], xin[:, P:]), tS, False)
        g_sc[...] = lax.dot_general(dyb, c_ref[...], nt, preferred_element_type=f32)
        gr, gi = _scan_block(g_sc, g_sc, ap_sc, (gcar_sc[:, :P], gcar_sc[:, P:]), tS, True)
        gcar_sc[:, :P] = gr
        gcar_sc[:, P:] = gi
        xv = x_sc[...]
        gv = g_sc[...]
        gb = gv.astype(bf16)
        dc_ref[...] += lax.dot_general(xv.astype(bf16), dyb, tn, preferred_element_type=f32)
        db_ref[...] += lax.dot_general(ub, gb, tn, preferred_element_type=f32)
        du_ref[...] = lax.dot_general(gb, b_ref[...], nt, preferred_element_type=f32) + d_ref[...] * dyv
        dd_ref[...] += jnp.sum(dyv * uv, axis=0, keepdims=True)
        rid = lax.broadcasted_iota(jnp.int32, (tS, P2), 0)
        xp = jnp.where(rid == 0, jnp.broadcast_to(xin, (tS, P2)), pltpu.roll(xv, 1, 0))
        xpr, xpi, ggr, ggi = xp[:, :P], xp[:, P:], gv[:, :P], gv[:, P:]
        da_ref[:, :P] += jnp.sum(ggr * xpr + ggi * xpi, axis=0, keepdims=True)
        da_ref[:, P:] += jnp.sum(ggi * xpr - ggr * xpi, axis=0, keepdims=True)

    rev = lambda t: nb - 1 - t
    return _pc(body, grid=(SSM_MACRO, nb),
               in_specs=[pl.BlockSpec((tS, MACRO_CH), lambda m, t: (rev(t), ucol0 + m)),
                         pl.BlockSpec((tS, MACRO_CH), lambda m, t: (rev(t), m)),
                         pl.BlockSpec((None, None, 1, P2), lambda m, t: (m, rev(t), 0, 0)),
                         pl.BlockSpec((None, MACRO_CH, P2), lambda m, t: (m, 0, 0)),
                         pl.BlockSpec((None, P2, MACRO_CH), lambda m, t: (m, 0, 0)),
                         pl.BlockSpec((None, 1, P2), lambda m, t: (m, 0, 0)),
                         pl.BlockSpec((1, MACRO_CH), lambda m, t: (0, m))],
               out_specs=[pl.BlockSpec((tS, MACRO_CH), lambda m, t: (rev(t), m)),
                          pl.BlockSpec((None, MACRO_CH, P2), lambda m, t: (m, 0, 0)),
                          pl.BlockSpec((None, P2, MACRO_CH), lambda m, t: (m, 0, 0)),
                          pl.BlockSpec((None, 1, P2), lambda m, t: (m, 0, 0)),
                          pl.BlockSpec((1, MACRO_CH), lambda m, t: (0, m))],
               out_shape=[jax.ShapeDtypeStruct((S, SSM_WIDTH), f32),
                          jax.ShapeDtypeStruct((SSM_MACRO, MACRO_CH, P2), f32),
                          jax.ShapeDtypeStruct((SSM_MACRO, P2, MACRO_CH), f32),
                          jax.ShapeDtypeStruct((SSM_MACRO, 1, P2), f32),
                          jax.ShapeDtypeStruct((1, SSM_WIDTH), f32)],
               scratch_shapes=[pltpu.VMEM((tS, P2), f32), pltpu.VMEM((tS, P2), f32),
                               pltpu.VMEM((SUBLANES, P2), f32), pltpu.VMEM((1, P2), f32)],
               compiler_params=_cp(("arbitrary", "arbitrary")), name=name)(
        proj, dy, xc, bm, cm, a, d.reshape(1, SSM_WIDTH))


_GELU_K = math.sqrt(2.0 / math.pi)
_GELU_C = 0.044715


def _glu_fwd(y, w, b, g, name="glu_fwd"):
    S, W = y.shape
    ts = _pick(S, (512, 256, 128))

    def body(y_ref, w_ref, b_ref, g_ref, z_ref, sn_ref, ge_ref):
        yv = y_ref[...]
        cdf = 0.5 * (1.0 + jnp.tanh(_GELU_K * (yv + _GELU_C * (yv * yv * yv))))
        ge = (yv * cdf).astype(bf16)
        z = jnp.dot(ge, w_ref[...], preferred_element_type=f32) + b_ref[...]
        s = yv * jax.nn.sigmoid(z)
        rstd = lax.rsqrt(jnp.sum(s * s, axis=-1, keepdims=True) * (1.0 / W) + EPS)
        z_ref[...] = z
        sn_ref[...] = (s * rstd * g_ref[...]).astype(bf16)
        ge_ref[...] = ge

    blk = pl.BlockSpec((ts, W), lambda i: (i, 0))
    row = pl.BlockSpec((1, W), lambda i: (0, 0))
    return _pc(body, grid=(S // ts,), in_specs=[blk, pl.BlockSpec((W, W), lambda i: (0, 0)), row, row],
               out_specs=[blk, blk, blk],
               out_shape=[jax.ShapeDtypeStruct((S, W), f32), jax.ShapeDtypeStruct((S, W), bf16),
                          jax.ShapeDtypeStruct((S, W), bf16)],
               compiler_params=_cp(("parallel",)), name=name)(y, w, b.reshape(1, W), g.reshape(1, W))


def _glu_bwd(y, z, dmixed, w, g, name="glu_bwd"):
    S, W = y.shape
    ts = _pick(S, (512, 256, 128))
    dcb = MLA_PAD // W

    def body(y_ref, z_ref, dsn_ref, w_ref, g_ref, dy_ref, dz_ref, dg_ref, db_ref):
        i = pl.program_id(0)
        yv, zv, gv = y_ref[...], z_ref[...], g_ref[...]
        sig = jax.nn.sigmoid(zv)
        s = yv * sig
        rstd = lax.rsqrt(jnp.sum(s * s, axis=-1, keepdims=True) * (1.0 / W) + EPS)
        sh = s * rstd
        dsn = dsn_ref[...]
        dsh = dsn * gv
        ds = rstd * (dsh - sh * (jnp.sum(dsh * sh, axis=-1, keepdims=True) * (1.0 / W)))
        dz = ds * s * (1.0 - sig)
        dzb = dz.astype(bf16)
        dge = lax.dot_general(dzb, w_ref[...], (((1,), (1,)), ((), ())), preferred_element_type=f32)
        t = jnp.tanh(_GELU_K * (yv + _GELU_C * (yv * yv * yv)))
        dgelu = 0.5 * (1.0 + t) + 0.5 * yv * (1.0 - t * t) * _GELU_K * (1.0 + 3.0 * _GELU_C * yv * yv)
        dy_ref[...] = ds * sig + dge * dgelu
        dz_ref[...] = dzb

        @pl.when(i == 0)
        def _():
            dg_ref[...] = jnp.zeros_like(dg_ref)
            db_ref[...] = jnp.zeros_like(db_ref)

        dg_ref[...] += jnp.sum(dsn * sh, axis=0, keepdims=True)
        db_ref[...] += jnp.sum(dz, axis=0, keepdims=True)

    blk = pl.BlockSpec((ts, W), lambda i: (i, 0))
    row = pl.BlockSpec((1, W), lambda i: (0, 0))
    return _pc(body, grid=(S // ts,),
               in_specs=[blk, blk, pl.BlockSpec((ts, W), lambda i: (i, dcb)), pl.BlockSpec((W, W), lambda i: (0, 0)), row],
               out_specs=[blk, blk, row, row],
               out_shape=[jax.ShapeDtypeStruct((S, W), f32), jax.ShapeDtypeStruct((S, W), bf16),
                          jax.ShapeDtypeStruct((1, W), f32), jax.ShapeDtypeStruct((1, W), f32)],
               compiler_params=_cp(("arbitrary",)), name=name)(y, z, dmixed, w, g.reshape(1, W))


def _ffn_up(hn, wg, wu, name="ffn_up"):
    S, K = hn.shape
    F = wg.shape[1]
    tm, tn = _pick(S, (512, 256, 128)), _pick(F, (1408, 256, 128))

    def body(h_ref, wg_ref, wu_ref, g_ref, u_ref, a_ref):
        hv = h_ref[...]
        gv = jnp.dot(hv, wg_ref[...], preferred_element_type=f32)
        uv = jnp.dot(hv, wu_ref[...], preferred_element_type=f32)
        g_ref[...] = gv
        u_ref[...] = uv
        a_ref[...] = (gv * jax.nn.sigmoid(gv) * uv).astype(bf16)

    wspec = pl.BlockSpec((K, tn), lambda i, j: (0, j))
    ospec = pl.BlockSpec((tm, tn), lambda i, j: (i, j))
    return _pc(body, grid=(S // tm, F // tn), in_specs=[pl.BlockSpec((tm, K), lambda i, j: (i, 0)), wspec, wspec],
               out_specs=[ospec] * 3,
               out_shape=[jax.ShapeDtypeStruct((S, F), f32), jax.ShapeDtypeStruct((S, F), f32),
                          jax.ShapeDtypeStruct((S, F), bf16)],
               compiler_params=_cp(("parallel", "parallel")), name=name)(hn, wg, wu)


def _ffn_bwd_act(dh, wd, gate, up, name="ffn_bwd_act"):
    S, K = dh.shape
    F = wd.shape[0]
    tm, tn = _pick(S, (512, 256, 128)), _pick(F, (1408, 256, 128))

    def body(dh_ref, wd_ref, g_ref, u_ref, dg_ref, du_ref):
        dact = lax.dot_general(dh_ref[...].astype(bf16), wd_ref[...], (((1,), (1,)), ((), ())),
                               preferred_element_type=f32)
        gv, uv = g_ref[...], u_ref[...]
        sig = jax.nn.sigmoid(gv)
        dg_ref[...] = (dact * uv * (sig * (1.0 + gv * (1.0 - sig)))).astype(bf16)
        du_ref[...] = (dact * (gv * sig)).astype(bf16)

    ospec = pl.BlockSpec((tm, tn), lambda i, j: (i, j))
    return _pc(body, grid=(S // tm, F // tn),
               in_specs=[pl.BlockSpec((tm, K), lambda i, j: (i, 0)), pl.BlockSpec((tn, K), lambda i, j: (j, 0)),
                         ospec, ospec],
               out_specs=[ospec] * 2, out_shape=[jax.ShapeDtypeStruct((S, F), bf16)] * 2,
               compiler_params=_cp(("parallel", "parallel")), name=name)(dh, wd, gate, up)


def _pad_heads(w, per_head, pieces):
    K = w.shape[0]
    w3 = w.reshape(K, MLA_HEADS, per_head)
    out = jnp.zeros((K, MLA_HEADS, HEAD_PAD), w.dtype)
    for s0, s1, d0 in pieces:
        out = out.at[:, :, d0:d0 + (s1 - s0)].set(w3[:, :, s0:s1])
    return out.reshape(K, MLA_PAD)


def _unpad_heads(wp, per_head, pieces):
    K = wp.shape[0]
    w3 = wp.reshape(K, MLA_HEADS, HEAD_PAD)
    out = jnp.zeros((K, MLA_HEADS, per_head), wp.dtype)
    for s0, s1, d0 in pieces:
        out = out.at[:, :, s0:s1].set(w3[:, :, d0:d0 + (s1 - s0)])
    return out.reshape(K, MLA_HEADS * per_head)


_Q_PIECES = [(0, QK_NOPE + QK_ROPE, 0)]
_K_PIECES = [(0, QK_NOPE, 0)]
_V_PIECES = [(QK_NOPE, QK_NOPE + V_HEAD, 0)]
_KR0 = Q_LORA + KV_LORA


def _pack_win(w):
    z = jnp.zeros((w.shape[0], LANES - QK_ROPE), w.dtype)
    return jnp.concatenate([w[:, :_KR0 + QK_ROPE], z, w[:, _KR0 + QK_ROPE:]], axis=1)


def _unpack_win(wp):
    return jnp.concatenate([wp[:, :_KR0 + QK_ROPE], wp[:, _KR0 + LANES:]], axis=1)


def _pack_wout(w):
    wa = w[:MLA_WIDTH].reshape(MLA_HEADS, V_HEAD, D_MODEL)
    wa = jnp.concatenate([wa, jnp.zeros_like(wa)], axis=1).reshape(MLA_PAD, D_MODEL)
    return jnp.concatenate([wa, w[MLA_WIDTH:]], axis=0)


def _unpack_wout(wp):
    wa = wp[:MLA_PAD].reshape(MLA_HEADS, HEAD_PAD, D_MODEL)[:, :V_HEAD].reshape(MLA_WIDTH, D_MODEL)
    return jnp.concatenate([wa, wp[MLA_PAD:]], axis=0)


def _pad_gain(g):
    g2 = g.reshape(MLA_HEADS, V_HEAD)
    return jnp.concatenate([g2, jnp.zeros_like(g2)], axis=1).reshape(MLA_PAD)


def _unpad_gain(gp):
    return gp.reshape(MLA_HEADS, HEAD_PAD)[:, :V_HEAD].reshape(MLA_WIDTH)


def _ssm_prep(lam_re, lam_im, log_dt, b_re, b_im, c_re, c_im):
    lam = lax.complex(lam_re, lam_im)
    dt = jnp.exp(log_dt)[:, None]
    a_bar = jnp.exp(lam * dt)
    b_bar = ((a_bar - 1.0) / lam)[..., None] * lax.complex(b_re, b_im)
    G8 = SSM_GROUPS // SSM_MACRO
    eye = jnp.eye(G8, dtype=f32)

    def bmat(part):
        p4 = part.reshape(SSM_MACRO, G8, SSM_STATE, SSM_GROUP)
        return jnp.einsum('mgpc,gh->mgchp', p4, eye).reshape(SSM_MACRO, MACRO_CH, MACRO_ST)

    def cmat(part):
        p4 = part.reshape(SSM_MACRO, G8, SSM_GROUP, SSM_STATE)
        return jnp.einsum('mgcp,gh->mgphc', p4, eye).reshape(SSM_MACRO, MACRO_ST, MACRO_CH)

    bm = jnp.concatenate([bmat(b_bar.real), bmat(b_bar.imag)], axis=2)
    cm = jnp.concatenate([cmat(c_re), -cmat(c_im)], axis=1)
    a4 = a_bar.reshape(SSM_MACRO, 1, MACRO_ST)
    a = jnp.concatenate([a4.real, a4.imag], axis=2)
    return bm, cm, a


def _rope_tables(positions):
    freqs = ROPE_THETA ** (-jnp.arange(0, QK_ROPE, 2, dtype=f32) / QK_ROPE)
    ang = positions.astype(f32)[:, None] * freqs
    cos, sin = jnp.cos(ang), jnp.sin(ang)
    S = positions.shape[0]
    half = QK_ROPE // 2
    one, zero = jnp.ones((S, QK_NOPE), f32), jnp.zeros((S, half), f32)
    z64, z32 = jnp.zeros((S, QK_NOPE), f32), jnp.zeros((S, LANES - QK_NOPE - QK_ROPE), f32)
    tc = jnp.concatenate([one, cos, cos, z32], axis=1)
    s1 = jnp.concatenate([z64, -sin, zero, z32], axis=1)
    s2 = jnp.concatenate([z64, zero, sin, z32], axis=1)
    return tc, s1, s2


def _layer_params(W, l):
    p = {}
    p['win'] = _pack_win(W['w_in'][l])
    p['wuq'] = _pad_heads(W['w_uq'][l], QK_NOPE + QK_ROPE, _Q_PIECES)
    wukv = W['w_ukv'][l]
    p['wukv'] = jnp.concatenate([_pad_heads(wukv, QK_NOPE + V_HEAD, _K_PIECES),
                                 _pad_heads(wukv, QK_NOPE + V_HEAD, _V_PIECES)], axis=1)
    p['wout'] = _pack_wout(W['w_out'][l])
    p['attn_g'] = _pad_gain(W['attn_out_g'][l])
    return p


def _forward_layer(h, memn_in, tabs, W, l, name):
    p = _layer_params(W, l)
    sv = {'h0': h, 'p': p}
    xn = _rms_fwd(h, W['norm_mix_g'][l], name=name + "rms_mix")
    proj = _mm([(xn, p['win'])], 'nn', f32, name=name + "mm_in")
    cqn = _rms_fwd(proj, W['q_norm_g'][l], col0=0, width=Q_LORA, name=name + "rms_q")
    ckvn = _rms_fwd(proj, W['kv_norm_g'][l], col0=Q_LORA, width=KV_LORA, name=name + "rms_kv")
    q = _mm([(cqn, p['wuq'])], 'nn', f32, name=name + "mm_uq")
    kv = _mm([(ckvn, p['wukv'])], 'nn', f32, name=name + "mm_ukv")
    qh, kh, vh = _rope_fwd(q, kv, proj, tabs, name=name + "rope")
    oh, lse = _attn_fwd(qh, kh, vh, name=name + "attn")
    an = _rms_fwd(oh, p['attn_g'], n_valid=MLA_WIDTH, name=name + "rms_attn")
    bm, cm, a = W['ssm'][l]
    bmb, cmb = bm.astype(bf16), cm.astype(bf16)
    y, xc = _ssm_fwd(proj, bmb, cmb, a, W['ssm_d'][l], name=name + "ssm")
    z, sn, ge = _glu_fwd(y, W['ssm_w_glu'][l], W['ssm_b_glu'][l], W['ssm_out_g'][l], name=name + "glu")
    h1a = _mm([(an, p['wout'][:MLA_PAD])], 'nn', f32, res=h, name=name + "mm_out_a")
    h1 = _mm([(sn, p['wout'][MLA_PAD:])], 'nn', f32, res=h1a, name=name + "mm_out_s")
    hn2 = _rms_fwd(h1, W['norm_x_g'][l], name=name + "rms_x")
    memn = _rms_fwd(memn_in, W['mem_norm_g'][l], name=name + "rms_mem")
    qx = _mm([(hn2, W['w_xq'][l])], 'nn', bf16, name=name + "mm_xq")
    kvx = _mm([(memn, W['w_xkv'][l])], 'nn', bf16, name=name + "mm_xkv")
    ox = _xattn_fwd(qx, kvx, name=name + "xattn")
    h2 = _mm([(ox, W['w_xo'][l])], 'nn', f32, res=h1, name=name + "mm_xo")
    hn3 = _rms_fwd(h2, W['norm_ffn_g'][l], name=name + "rms_ffn")
    gate, up, act = _ffn_up(hn3, W['w_gate'][l], W['w_up'][l], name=name + "ffn_up")
    h3 = _mm([(act, W['w_down'][l])], 'nn', f32, res=h2, name=name + "mm_down")
    sv.update(xn=xn, proj=proj, cqn=cqn, ckvn=ckvn, qh=qh, kh=kh, vh=vh, oh=oh, lse=lse, an=an, bmb=bmb, cmb=cmb,
              a=a, y=y, xc=xc, z=z, sn=sn, ge=ge, h1=h1, hn2=hn2, memn=memn, qx=qx, kvx=kvx, ox=ox, h2=h2, hn3=hn3,
              gate=gate, up=up, act=act)
    return h3, sv


def _backward_layer(dh3, sv, memn_in, tabs, W, l, name):
    p = sv['p']
    G = {}
    G['w_down'] = _mm([(sv['act'], dh3)], 'tn', f32, name=name + "dw_down")
    dgate, dup = _ffn_bwd_act(dh3, W['w_down'][l], sv['gate'], sv['up'], name=name + "ffn_bwd_act")
    dhn3 = _mm([(dgate, W['w_gate'][l]), (dup, W['w_up'][l])], 'nt', f32, name=name + "mm_dffn")
    G['w_gate'] = _mm([(sv['hn3'], dgate)], 'tn', f32, name=name + "dw_gate")
    G['w_up'] = _mm([(sv['hn3'], dup)], 'tn', f32, name=name + "dw_up")
    dh2, dg = _rms_bwd(sv['h2'], W['norm_ffn_g'][l], dhn3, res=dh3, name=name + "rmsb_ffn")
    G['norm_ffn_g'] = dg[0]
    G['w_xo'] = _mm([(sv['ox'], dh2)], 'tn', f32, name=name + "dw_xo")
    dox = _mm([(dh2, W['w_xo'][l])], 'nt', bf16, name=name + "mm_dxo")
    dqx, dkvx = _xattn_bwd(sv['qx'], sv['kvx'], dox, name=name + "xattn_bwd")
    G['w_xq'] = _mm([(sv['hn2'], dqx)], 'tn', f32, name=name + "dw_xq")
    G['w_xkv'] = _mm([(sv['memn'], dkvx)], 'tn', f32, name=name + "dw_xkv")
    dhn2 = _mm([(dqx, W['w_xq'][l])], 'nt', f32, name=name + "mm_dxq")
    dmemn = _mm([(dkvx, W['w_xkv'][l])], 'nt', f32, name=name + "mm_dxkv")
    dh1, dg = _rms_bwd(sv['h1'], W['norm_x_g'][l], dhn2, res=dh2, name=name + "rmsb_x")
    G['norm_x_g'] = dg[0]
    _, dg = _rms_bwd(memn_in, W['mem_norm_g'][l], dmemn, name=name + "rmsb_mem")
    G['mem_norm_g'] = dg[0]
    dwo_a = _mm([(sv['an'], dh1)], 'tn', f32, name=name + "dw_out_a")
    dwo_s = _mm([(sv['sn'], dh1)], 'tn', f32, name=name + "dw_out_s")
    G['w_out'] = _unpack_wout(jnp.concatenate([dwo_a, dwo_s], axis=0))
    dmixed = _mm([(dh1, p['wout'])], 'nt', f32, name=name + "mm_dout")
    dy, dz, dg, db = _glu_bwd(sv['y'], sv['z'], dmixed, W['ssm_w_glu'][l], W['ssm_out_g'][l], name=name + "glu_bwd")
    G['ssm_out_g'], G['ssm_b_glu'] = dg[0], db[0]
    G['ssm_w_glu'] = _mm([(sv['ge'], dz)], 'tn', f32, name=name + "dw_glu")
    du, dbm, dcm, da, dd = _ssm_bwd(sv['proj'], dy, sv['xc'], sv['bmb'], sv['cmb'], sv['a'], W['ssm_d'][l],
                                    name=name + "ssm_bwd")
    G['ssm_d'] = dd[0]
    G['ssm_raw'] = (dbm, dcm, da)
    doh, dg, delta = _rms_bwd(sv['oh'], p['attn_g'], dmixed, width=MLA_PAD, n_valid=MLA_WIDTH, delta=True,
                              out_dtype=bf16, name=name + "rmsb_attn")
    G['attn_out_g'] = _unpad_gain(dg[0])
    dqh, dkh, dvh = _attn_bwd(sv['qh'], sv['kh'], sv['vh'], doh, sv['lse'], delta, name=name + "attn_bwd")
    dq, dkv, dkr = _rope_bwd(dqh, dkh, dvh, tabs, name=name + "rope_bwd")
    G['w_uq'] = _unpad_heads(_mm([(sv['cqn'], dq)], 'tn', f32, name=name + "dw_uq"), QK_NOPE + QK_ROPE, _Q_PIECES)
    dwukv = _mm([(sv['ckvn'], dkv)], 'tn', f32, name=name + "dw_ukv")
    G['w_ukv'] = (_unpad_heads(dwukv[:, :MLA_PAD], QK_NOPE + V_HEAD, _K_PIECES)
                  + _unpad_heads(dwukv[:, MLA_PAD:], QK_NOPE + V_HEAD, _V_PIECES))
    dcqn = _mm([(dq, p['wuq'])], 'nt', f32, name=name + "mm_duq")
    dckvn = _mm([(dkv, p['wukv'])], 'nt', f32, name=name + "mm_dukv")
    dcq, dg = _rms_bwd(sv['proj'], W['q_norm_g'][l], dcqn, col0=0, width=Q_LORA, out_dtype=bf16, name=name + "rmsb_q")
    G['q_norm_g'] = dg[0]
    dckv, dg = _rms_bwd(sv['proj'], W['kv_norm_g'][l], dckvn, col0=Q_LORA, width=KV_LORA, out_dtype=bf16,
                        name=name + "rmsb_kv")
    G['kv_norm_g'] = dg[0]
    dproj = jnp.concatenate([dcq, dckv, dkr, du.astype(bf16)], axis=1)
    G['w_in'] = _unpack_win(_mm([(sv['xn'], dproj)], 'tn', f32, name=name + "dw_in"))
    dxn = _mm([(dproj, p['win'])], 'nt', f32, name=name + "mm_din")
    dh0, dg = _rms_bwd(sv['h0'], W['norm_mix_g'][l], dxn, res=dh1, name=name + "rmsb_mix")
    G['norm_mix_g'] = dg[0]
    return dh0, G


def _local_step(x, mem, positions, target, W):
    tabs = _rope_tables(positions)
    ssm_in = [(W['ssm_lambda_re'][l], W['ssm_lambda_im'][l], W['ssm_log_dt'][l], W['ssm_b_re'][l], W['ssm_b_im'][l],
               W['ssm_c_re'][l], W['ssm_c_im'][l]) for l in range(DEPTH)]
    preps = [jax.vjp(_ssm_prep, *ssm_in[l]) for l in range(DEPTH)]
    W = dict(W)
    W['ssm'] = [preps[l][0] for l in range(DEPTH)]
    h = x
    saved = []
    for l in range(DEPTH):
        h, sv = _forward_layer(h, mem, tabs, W, l, f"l{l}_")
        saved.append(sv)
    dh, dgf, loss = _loss_head(h, W['final_norm_g'], target)
    grads = [None] * DEPTH
    for l in reversed(range(DEPTH)):
        dh, G = _backward_layer(dh, saved[l], mem, tabs, W, l, f"l{l}b_")
        dbm, dcm, da = G.pop('ssm_raw')
        names = ['ssm_lambda_re', 'ssm_lambda_im', 'ssm_log_dt', 'ssm_b_re', 'ssm_b_im', 'ssm_c_re', 'ssm_c_im']
        for n, g in zip(names, preps[l][1]((dbm, dcm, da))):
            G[n] = g
        grads[l] = G
    out = {n: jnp.stack([grads[l][n] for l in range(DEPTH)]) for n in grads[0]}
    out['final_norm_g'] = dgf[0]
    return loss[0, 0], dh, out


_HBM = pl.BlockSpec(memory_space=pltpu.HBM)


def _me():
    return lax.axis_index("x"), lax.axis_index("y"), lax.axis_index("c")


def _chip_peers(x, y, c):
    devs = [(1 - x, y, c), (x, 1 - y, c), (1 - x, 1 - y, c)]
    return devs, [2 * d[0] + d[1] for d in devs]


def _allgather_chips(xs, name):
    n = len(xs)

    def body(*refs):
        src, dst = refs[:n], refs[n:2 * n]
        send, recv = refs[2 * n:]
        x, y, c = _me()
        jme = 2 * x + y
        sib = (x, y, 1 - c)
        devs, js = _chip_peers(x, y, c)
        half, other = pl.ds(c, 1), pl.ds(1 - c, 1)
        first = []
        for i in range(n):
            for k in range(3):
                cp = pltpu.make_async_remote_copy(src[i].at[half], dst[i].at[half, pl.ds(jme, 1)], send.at[6 * i + k],
                                                  recv.at[6 * i + k], device_id=devs[k], device_id_type=MESH)
                cp.start()
                first.append(cp)
        passed = []
        for i in range(n):
            for k in range(3):
                slot = dst[i].at[half, pl.ds(js[k], 1)]
                pltpu.make_async_remote_copy(src[i].at[half], slot, send.at[6 * i + k], recv.at[6 * i + k],
                                             device_id=devs[k], device_id_type=MESH).wait_recv()
                fw = pltpu.make_async_remote_copy(slot, slot, send.at[6 * i + 3 + k], recv.at[6 * i + 3 + k],
                                                  device_id=sib, device_id_type=MESH)
                fw.start()
                passed.append(fw)
        for i in range(n):
            for k in range(3):
                slot = dst[i].at[other, pl.ds(js[k], 1)]
                pltpu.make_async_remote_copy(slot, slot, send.at[6 * i + 3 + k], recv.at[6 * i + 3 + k],
                                             device_id=sib, device_id_type=MESH).wait_recv()
        for cp in first + passed:
            cp.wait_send()

    ins = [t.reshape(t.shape[0], 1, *t.shape[1:]) for t in xs]
    outs = [jax.ShapeDtypeStruct((t.shape[0], 4, *t.shape[1:]), t.dtype) for t in xs]
    return _pc(body, in_specs=[_HBM] * n, out_specs=[_HBM] * n, out_shape=outs,
               scratch_shapes=[pltpu.SemaphoreType.DMA((6 * n,)), pltpu.SemaphoreType.DMA((6 * n,))],
               compiler_params=pltpu.CompilerParams(has_side_effects=True), name=name)(*ins)


def _fill_own(gathered, own):
    jme = (2 * lax.axis_index("x") + lax.axis_index("y")).astype(jnp.int32)
    zero = jnp.int32(0)
    return lax.dynamic_update_slice(gathered, own[:, None], (zero, jme, zero, zero))


def _exchange_halves(gs, name):
    n = len(gs)

    def body(*refs):
        src, dst = refs[:n], refs[n:2 * n]
        send, recv = refs[2 * n:]
        x, y, c = _me()
        cps = []
        for i in range(n):
            cp = pltpu.make_async_remote_copy(src[i].at[pl.ds(1 - c, 1)], dst[i], send.at[i], recv.at[i],
                                              device_id=(x, y, 1 - c), device_id_type=MESH)
            cp.start()
            cps.append(cp)
        for cp in cps:
            cp.wait()

    outs = [jax.ShapeDtypeStruct((1, *g.shape[1:]), g.dtype) for g in gs]
    return _pc(body, in_specs=[_HBM] * n, out_specs=[_HBM] * n, out_shape=outs,
               scratch_shapes=[pltpu.SemaphoreType.DMA((n,)), pltpu.SemaphoreType.DMA((n,))],
               compiler_params=pltpu.CompilerParams(has_side_effects=True), name=name)(*gs)


def _scatter_chips(ps, name):
    n = len(ps)

    def body(*refs):
        src, dst = refs[:n], refs[n:2 * n]
        send, recv = refs[2 * n:]
        x, y, c = _me()
        devs, js = _chip_peers(x, y, c)
        cps = []
        for i in range(n):
            for k in range(3):
                cp = pltpu.make_async_remote_copy(src[i].at[pl.ds(js[k], 1)], dst[i].at[k], send.at[3 * i + k],
                                                  recv.at[3 * i + k], device_id=devs[k], device_id_type=MESH)
                cp.start()
                cps.append(cp)
        for cp in cps:
            cp.wait()

    outs = [jax.ShapeDtypeStruct((3, 1, *p.shape[1:]), p.dtype) for p in ps]
    return _pc(body, in_specs=[_HBM] * n, out_specs=[_HBM] * n, out_shape=outs,
               scratch_shapes=[pltpu.SemaphoreType.DMA((3 * n,)), pltpu.SemaphoreType.DMA((3 * n,))],
               compiler_params=pltpu.CompilerParams(has_side_effects=True), name=name)(*ps)


def _swap_sibling(hs, name):
    n = len(hs)

    def body(*refs):
        src, dst = refs[:n], refs[n:2 * n]
        send, recv = refs[2 * n:]
        x, y, c = _me()
        cps = []
        for i in range(n):
            cp = pltpu.make_async_remote_copy(src[i], dst[i], send.at[i], recv.at[i], device_id=(x, y, 1 - c),
                                              device_id_type=MESH)
            cp.start()
            cps.append(cp)
        for cp in cps:
            cp.wait()

    outs = [jax.ShapeDtypeStruct(h.shape, h.dtype) for h in hs]
    return _pc(body, in_specs=[_HBM] * n, out_specs=[_HBM] * n, out_shape=outs,
               scratch_shapes=[pltpu.SemaphoreType.DMA((n,)), pltpu.SemaphoreType.DMA((n,))],
               compiler_params=pltpu.CompilerParams(has_side_effects=True), name=name)(*hs)


def _row_tile(r):
    return _pick(r, (256, 128, 64, 32, 16, 8))


def _add_half(g, r1, cidx, name):
    _, _, r, n = g.shape
    tr = _row_tile(r)

    def body(c_ref, g_ref, r_ref, o_ref):
        o_ref[...] = (g_ref[...] + r_ref[...]).astype(GRAD_TRANSIT)

    blk = lambda f: pl.BlockSpec((None, None, tr, n), f)
    gs = pltpu.PrefetchScalarGridSpec(
        num_scalar_prefetch=1, grid=(4, r // tr),
        in_specs=[blk(lambda j, i, c: (c[0], j, i, 0)), blk(lambda j, i, c: (0, j, i, 0))],
        out_specs=pl.BlockSpec((None, tr, n), lambda j, i, c: (j, i, 0)))
    return _pc(body, grid_spec=gs, out_shape=jax.ShapeDtypeStruct((4, r, n), GRAD_TRANSIT),
               compiler_params=_cp(("parallel", "parallel")), name=name)(cidx, g, r1)


def _add_chips(p, r3, jidx, name):
    _, r, n = p.shape
    tr = _row_tile(r)

    def body(j_ref, p_ref, a_ref, b_ref, c_ref, o_ref):
        o_ref[...] = ((p_ref[...].astype(f32) + a_ref[...].astype(f32)) + b_ref[...].astype(f32)) + c_ref[...].astype(f32)

    rblk = lambda k: pl.BlockSpec((None, None, tr, n), lambda i, j: (k, 0, i, 0))
    gs = pltpu.PrefetchScalarGridSpec(
        num_scalar_prefetch=1, grid=(r // tr,),
        in_specs=[pl.BlockSpec((None, tr, n), lambda i, j: (j[0], i, 0)), rblk(0), rblk(1), rblk(2)],
        out_specs=pl.BlockSpec((tr, n), lambda i, j: (i, 0)))
    return _pc(body, grid_spec=gs, out_shape=jax.ShapeDtypeStruct((r, n), f32),
               compiler_params=_cp(("parallel",)), name=name)(jidx, p, r3, r3, r3)


def _adamw_halves(w, mine, theirs, m, v, cidx, name):
    _, r, n = w.shape
    tr = _row_tile(r)
    c1 = 1.0 / (1.0 - ADAM_B1 ** ADAM_STEP)
    c2 = 1.0 / (1.0 - ADAM_B2 ** ADAM_STEP)

    def body(c_ref, w_ref, a_ref, b_ref, m_ref, v_ref, g_ref, d_ref, mo_ref, vo_ref):
        gv = jnp.where(pl.program_id(0) == c_ref[0], a_ref[...], b_ref[...])
        m2 = ADAM_B1 * m_ref[...] + (1.0 - ADAM_B1) * gv
        v2 = ADAM_B2 * v_ref[...] + (1.0 - ADAM_B2) * (gv * gv)
        g_ref[...] = gv
        d_ref[...] = -ADAM_LR * ((m2 * c1) / (jnp.sqrt(v2 * c2) + ADAM_EPS) + ADAM_WD * w_ref[...])
        mo_ref[...] = m2
        vo_ref[...] = v2

    full = pl.BlockSpec((None, tr, n), lambda l, i, c: (l, i, 0))
    half = pl.BlockSpec((tr, n), lambda l, i, c: (i, 0))
    gs = pltpu.PrefetchScalarGridSpec(num_scalar_prefetch=1, grid=(2, r // tr),
                                      in_specs=[full, half, half, full, full], out_specs=[full] * 4)
    return _pc(body, grid_spec=gs, out_shape=[jax.ShapeDtypeStruct(w.shape, f32)] * 4,
               compiler_params=_cp(("parallel", "parallel")), name=name)(cidx, w, mine, theirs, m, v)


def _adamw(w, g, m, v, name):
    rows, n = w.shape
    tr = _row_tile(rows)
    c1 = 1.0 / (1.0 - ADAM_B1 ** ADAM_STEP)
    c2 = 1.0 / (1.0 - ADAM_B2 ** ADAM_STEP)

    def body(w_ref, g_ref, m_ref, v_ref, d_ref, mo_ref, vo_ref):
        gv = g_ref[...]
        m2 = ADAM_B1 * m_ref[...] + (1.0 - ADAM_B1) * gv
        v2 = ADAM_B2 * v_ref[...] + (1.0 - ADAM_B2) * (gv * gv)
        d_ref[...] = -ADAM_LR * ((m2 * c1) / (jnp.sqrt(v2 * c2) + ADAM_EPS) + ADAM_WD * w_ref[...])
        mo_ref[...] = m2
        vo_ref[...] = v2

    blk = pl.BlockSpec((tr, n), lambda i: (i, 0))
    return _pc(body, grid=(rows // tr,), in_specs=[blk] * 4, out_specs=[blk] * 3,
               out_shape=[jax.ShapeDtypeStruct((rows, n), f32)] * 3,
               compiler_params=_cp(("parallel",)), name=name)(w, g, m, v)


def _full_from_gathered(name, t):
    L, _, r, n = t.shape
    if SHARDED[name] == 1:
        return t.reshape(L, 4 * r, n)
    return t.transpose(0, 2, 1, 3).reshape(L, r, 4 * n)


def _shard_major(name, g):
    L, R, C = g.shape
    if SHARDED[name] == 1:
        return g.reshape(L, 4, R // 4, C)
    return g.reshape(L, R, 4, C // 4).transpose(0, 2, 1, 3)


_SMALL_ROWS = 288


def _pack_small(d):
    flat = jnp.concatenate([d[n].reshape(-1) for n in SMALL])
    total = 2 * 4 * _SMALL_ROWS * LANES
    flat = jnp.concatenate([flat, jnp.zeros((total - flat.shape[0],), f32)])
    return flat.reshape(2, 4, _SMALL_ROWS, LANES)


def _unpack_small(t, like):
    flat = t.reshape(-1)
    out, off = {}, 0
    for n in SMALL:
        sz = math.prod(like[n].shape)
        out[n] = flat[off:off + sz].reshape(like[n].shape)
        off += sz
    return out


def kernel(x, mem, positions, norm_mix_g, w_in, q_norm_g, w_uq, kv_norm_g, w_ukv, ssm_lambda_re, ssm_lambda_im, ssm_log_dt, ssm_b_re, ssm_b_im, ssm_c_re, ssm_c_im, ssm_d, ssm_w_glu, ssm_b_glu, attn_out_g, ssm_out_g, w_out, norm_x_g, mem_norm_g, w_xq, w_xkv, w_xo, norm_ffn_g, w_gate, w_up, w_down, final_norm_g, loss_target, m_norm_mix_g, m_w_in, m_q_norm_g, m_w_uq, m_kv_norm_g, m_w_ukv, m_ssm_lambda_re, m_ssm_lambda_im, m_ssm_log_dt, m_ssm_b_re, m_ssm_b_im, m_ssm_c_re, m_ssm_c_im, m_ssm_d, m_ssm_w_glu, m_ssm_b_glu, m_attn_out_g, m_ssm_out_g, m_w_out, m_norm_x_g, m_mem_norm_g, m_w_xq, m_w_xkv, m_w_xo, m_norm_ffn_g, m_w_gate, m_w_up, m_w_down, m_final_norm_g, v_norm_mix_g, v_w_in, v_q_norm_g, v_w_uq, v_kv_norm_g, v_w_ukv, v_ssm_lambda_re, v_ssm_lambda_im, v_ssm_log_dt, v_ssm_b_re, v_ssm_b_im, v_ssm_c_re, v_ssm_c_im, v_ssm_d, v_ssm_w_glu, v_ssm_b_glu, v_attn_out_g, v_ssm_out_g, v_w_out, v_norm_x_g, v_mem_norm_g, v_w_xq, v_w_xkv, v_w_xo, v_norm_ffn_g, v_w_gate, v_w_up, v_w_down, v_final_norm_g):
    given = dict(locals())
    w = {n: given[n] for n in WEIGHTS}
    m = {n: given["m_" + n] for n in WEIGHTS}
    v = {n: given["v_" + n] for n in WEIGHTS}
    big = list(SHARDED)

    shards = [w[n].astype(bf16) for n in big]
    gathered = _allgather_chips(shards, "allgather_weights")
    W = {n: _full_from_gathered(n, _fill_own(t, s)) for n, t, s in zip(big, gathered, shards)}
    W.update({n: w[n] for n in SMALL})

    loss, dx, grads = _local_step(x[0], mem[0], positions[0], loss_target[0], W)
    loss = lax.psum(loss, ("x", "y", "c"))

    cidx = lax.axis_index("c").astype(jnp.int32).reshape(1)
    jidx = (2 * lax.axis_index("x") + lax.axis_index("y")).astype(jnp.int32).reshape(1)
    names = big + ["small"]
    gs = [_shard_major(n, grads[n]) for n in big] + [_pack_small(grads)]
    r1 = _exchange_halves(gs, "grad_exchange_halves")
    ps = [_add_half(g, r, cidx, f"grad_add_half_{n}") for n, g, r in zip(names, gs, r1)]
    r3 = _scatter_chips(ps, "grad_scatter_chips")
    hs = [_add_chips(p, r, jidx, f"grad_add_chips_{n}") for n, p, r in zip(names, ps, r3)]
    ts = _swap_sibling(hs, "grad_swap_sibling")

    out_g, out_d, out_m, out_v = {}, {}, {}, {}
    for n, h, t in zip(big, hs[:-1], ts[:-1]):
        out_g[n], out_d[n], out_m[n], out_v[n] = _adamw_halves(w[n], h, t, m[n], v[n], cidx, f"adamw_{n}")
    both = jnp.stack([hs[-1], ts[-1]])
    piece = jnp.where(cidx[0] == 0, both, both[::-1])
    gsm = _fill_own(_allgather_chips([piece], "allgather_small")[0], piece)
    rows2 = 2 * 4 * _SMALL_ROWS
    flat = lambda d: _pack_small(d).reshape(rows2, LANES)
    d_, m_, v_ = _adamw(flat(w), gsm.reshape(rows2, LANES), flat(m), flat(v), "adamw_small")
    for dst, t in ((out_g, gsm), (out_d, d_), (out_m, m_), (out_v, v_)):
        dst.update(_unpack_small(t, w))

    return (loss, dx.reshape(x.shape), *[out_g[n] for n in WEIGHTS], *[out_d[n] for n in WEIGHTS],
            *[out_m[n] for n in WEIGHTS], *[out_v[n] for n in WEIGHTS])
```

```python
import functools
import math

import jax
import jax.numpy as jnp
from jax import lax
from jax.experimental import pallas as pl
from jax.experimental.pallas import tpu as pltpu

f32, bf16 = jnp.float32, jnp.bfloat16

D_MODEL = 1024
DEPTH = 2
MLA_HEADS = 8
QK_NOPE = 64
QK_ROPE = 32
V_HEAD = 64
Q_LORA = 256
KV_LORA = 128
MLA_WIDTH = MLA_HEADS * V_HEAD
ROPE_THETA = 10000.0
SSM_WIDTH = 512
SSM_GROUP = 16
SSM_GROUPS = 32
SSM_STATE = 64
IN_WIDTH = Q_LORA + KV_LORA + QK_ROPE + SSM_WIDTH
X_HEADS = 4
X_HEAD_DIM = D_MODEL // X_HEADS
D_FF = 2816
EPS = 1e-6
ADAM_LR, ADAM_B1, ADAM_B2, ADAM_EPS, ADAM_WD, ADAM_STEP = 0.001, 0.9, 0.999, 1e-08, 0.01, 10

LANES = 128
SUBLANES = 8
HEAD_PAD = 128
MLA_PAD = MLA_HEADS * HEAD_PAD
SSM_MACRO = 4
MACRO_CH = SSM_WIDTH // SSM_MACRO
MACRO_ST = SSM_GROUPS // SSM_MACRO * SSM_STATE
VMEM_LIMIT = 56 * 1024 * 1024
GRAD_TRANSIT = bf16

WEIGHTS = ['norm_mix_g', 'w_in', 'q_norm_g', 'w_uq', 'kv_norm_g', 'w_ukv', 'ssm_lambda_re', 'ssm_lambda_im',
           'ssm_log_dt', 'ssm_b_re', 'ssm_b_im', 'ssm_c_re', 'ssm_c_im', 'ssm_d', 'ssm_w_glu', 'ssm_b_glu',
           'attn_out_g', 'ssm_out_g', 'w_out', 'norm_x_g', 'mem_norm_g', 'w_xq', 'w_xkv', 'w_xo', 'norm_ffn_g',
           'w_gate', 'w_up', 'w_down', 'final_norm_g']
SHARDED = {'w_in': 1, 'w_uq': 2, 'w_ukv': 2, 'ssm_w_glu': 1, 'w_out': 1, 'w_xq': 1, 'w_xkv': 2, 'w_xo': 1,
           'w_gate': 2, 'w_up': 2, 'w_down': 1}
SMALL = [n for n in WEIGHTS if n not in SHARDED]
MESH = pl.DeviceIdType.MESH


def _pc(body, **kw):
    return pl.pallas_call(body, **kw)


def _pick(n, prefs):
    for p in prefs:
        if n % p == 0:
            return p
    return n


def _cp(sem=None):
    return pltpu.CompilerParams(dimension_semantics=sem, vmem_limit_bytes=VMEM_LIMIT)


_TILE_CANDS = (1024, 1408, 512, 256, 128)
MM_VMEM_BUDGET = 40 * 1024 * 1024


def _mm_tiles(M, K, N, a_bytes, b_bytes, o_bytes, npair, has_res, need_acc):
    best = None
    for tm in _TILE_CANDS:
        for tk in _TILE_CANDS:
            if M % tm or K % tk:
                continue
            vm = npair * (2 * tm * tk * a_bytes + 2 * tk * N * b_bytes) + 2 * tm * N * o_bytes
            vm += tm * N * 4 * (1 + need_acc + 2 * has_res)
            if a_bytes == 4:
                vm += npair * tm * tk * 2
            if b_bytes == 4:
                vm += npair * tk * N * 2
            if vm <= MM_VMEM_BUDGET and (best is None or tm * tk > best[0]):
                best = (tm * tk, tm, tk)
    if best is None:
        return _pick(M, (256, 128)), _pick(K, (256, 128))
    return best[1], best[2]


def _mm(pairs, mode, out_dtype, res=None, name="mm"):
    a0, b0 = pairs[0]
    if mode == 'nn':
        (M, K), N = a0.shape, b0.shape[1]
        dims = (((1,), (0,)), ((), ()))
    elif mode == 'nt':
        (M, K), N = a0.shape, b0.shape[0]
        dims = (((1,), (1,)), ((), ()))
    else:
        (K, M), N = a0.shape, b0.shape[1]
        dims = (((0,), (0,)), ((), ()))
    npair = len(pairs)
    has_res = res is not None
    direct = out_dtype == f32
    tm, tk = _mm_tiles(M, K, N, a0.dtype.itemsize, b0.dtype.itemsize, jnp.dtype(out_dtype).itemsize, npair, has_res,
                       not direct)
    nk = K // tk

    def body(*refs):
        ins = refs[:2 * npair]
        res_ref = refs[2 * npair] if has_res else None
        o_ref = refs[2 * npair + has_res]
        acc = o_ref if direct else refs[2 * npair + has_res + 1]
        k = pl.program_id(1)
        s = None
        for p in range(npair):
            d = lax.dot_general(ins[2 * p][...].astype(bf16), ins[2 * p + 1][...].astype(bf16), dims,
                                preferred_element_type=f32)
            s = d if s is None else s + d

        @pl.when(k == 0)
        def _():
            acc[...] = s

        @pl.when(k > 0)
        def _():
            acc[...] += s

        if has_res or not direct:
            @pl.when(k == nk - 1)
            def _():
                r = acc[...]
                if has_res:
                    r = r + res_ref[...]
                o_ref[...] = r.astype(out_dtype)

    if mode == 'nn':
        a_spec = pl.BlockSpec((tm, tk), lambda i, k: (i, k))
        b_spec = pl.BlockSpec((tk, N), lambda i, k: (k, 0))
    elif mode == 'nt':
        a_spec = pl.BlockSpec((tm, tk), lambda i, k: (i, k))
        b_spec = pl.BlockSpec((N, tk), lambda i, k: (0, k))
    else:
        a_spec = pl.BlockSpec((tk, tm), lambda i, k: (k, i))
        b_spec = pl.BlockSpec((tk, N), lambda i, k: (k, 0))
    o_spec = pl.BlockSpec((tm, N), lambda i, k: (i, 0))
    in_specs = [a_spec, b_spec] * npair + ([o_spec] if has_res else [])
    args = [t for p in pairs for t in p] + ([res] if has_res else [])
    return _pc(body, grid=(M // tm, nk), in_specs=in_specs, out_specs=o_spec,
               out_shape=jax.ShapeDtypeStruct((M, N), out_dtype),
               scratch_shapes=[] if direct else [pltpu.VMEM((tm, N), f32)],
               compiler_params=_cp(("parallel", "arbitrary")), name=name)(*args)


def _rms_fwd(x, g, *, col0=0, width=None, n_valid=None, out_dtype=bf16, name="rms_fwd"):
    S = x.shape[0]
    width = width or x.shape[1]
    n_valid = n_valid or width
    ts = _pick(S, (512, 256, 128))
    cb = col0 // width

    def body(x_ref, g_ref, o_ref):
        xv = x_ref[...]
        ms = jnp.sum(xv * xv, axis=-1, keepdims=True) * (1.0 / n_valid)
        o_ref[...] = (xv * lax.rsqrt(ms + EPS) * g_ref[...]).astype(out_dtype)

    return _pc(body, grid=(S // ts,),
               in_specs=[pl.BlockSpec((ts, width), lambda i: (i, cb)), pl.BlockSpec((1, width), lambda i: (0, 0))],
               out_specs=pl.BlockSpec((ts, width), lambda i: (i, 0)),
               out_shape=jax.ShapeDtypeStruct((S, width), out_dtype),
               compiler_params=_cp(("parallel",)), name=name)(x, g.reshape(1, width))


def _rms_bwd(x, g, dy, *, col0=0, dcol0=0, width=None, n_valid=None, res=None, out_dtype=f32, delta=False,
             name="rms_bwd"):
    S = x.shape[0]
    width = width or x.shape[1]
    n_valid = n_valid or width
    ts = _pick(S, (512, 256, 128))
    cb, dcb = col0 // width, dcol0 // width
    has_res = res is not None

    def body(*refs):
        x_ref, g_ref, dy_ref = refs[:3]
        res_ref = refs[3] if has_res else None
        outs = refs[3 + has_res:]
        dx_ref, dg_ref = outs[0], outs[1]
        i = pl.program_id(0)
        xv = x_ref[...]
        gv = g_ref[...]
        dyv = dy_ref[...].astype(f32)
        rstd = lax.rsqrt(jnp.sum(xv * xv, axis=-1, keepdims=True) * (1.0 / n_valid) + EPS)
        xh = xv * rstd
        dxh = dyv * gv
        mean = jnp.sum(dxh * xh, axis=-1, keepdims=True) * (1.0 / n_valid)
        dx = rstd * (dxh - xh * mean)
        if delta:
            d_ref = outs[2]
            for h in range(width // LANES):
                sl = slice(h * LANES, (h + 1) * LANES)
                dsum = jnp.sum(dx[:, sl] * xv[:, sl], axis=-1, keepdims=True)
                d_ref[:, sl] = jnp.broadcast_to(dsum, (ts, LANES))
        if has_res:
            dx = dx + res_ref[...]
        dx_ref[...] = dx.astype(out_dtype)

        @pl.when(i == 0)
        def _():
            dg_ref[...] = jnp.zeros_like(dg_ref)

        dg_ref[...] += jnp.sum(dyv * xh, axis=0, keepdims=True)

    blk = lambda c: pl.BlockSpec((ts, width), lambda i: (i, c))
    in_specs = [blk(cb), pl.BlockSpec((1, width), lambda i: (0, 0)), blk(dcb)] + ([blk(0)] if has_res else [])
    out_specs = [blk(0), pl.BlockSpec((1, width), lambda i: (0, 0))] + ([blk(0)] if delta else [])
    out_shape = [jax.ShapeDtypeStruct((S, width), out_dtype), jax.ShapeDtypeStruct((1, width), f32)] + (
        [jax.ShapeDtypeStruct((S, width), f32)] if delta else [])
    args = [x, g.reshape(1, width), dy] + ([res] if has_res else [])
    return _pc(body, grid=(S // ts,), in_specs=in_specs, out_specs=out_specs, out_shape=out_shape,
               compiler_params=_cp(("arbitrary",)), name=name)(*args)


def _loss_head(h, g, target, name="loss_head"):
    S, D = h.shape
    ts = _pick(S, (512, 256, 128))

    def body(h_ref, g_ref, t_ref, dh_ref, dg_ref, loss_ref):
        i = pl.program_id(0)
        xv = h_ref[...]
        gv = g_ref[...]
        rstd = lax.rsqrt(jnp.sum(xv * xv, axis=-1, keepdims=True) * (1.0 / D) + EPS)
        xh = xv * rstd
        err = xh * gv - t_ref[...]
        dyv = err * (1.0 / D)
        dxh = dyv * gv
        mean = jnp.sum(dxh * xh, axis=-1, keepdims=True) * (1.0 / D)
        dh_ref[...] = rstd * (dxh - xh * mean)

        @pl.when(i == 0)
        def _():
            dg_ref[...] = jnp.zeros_like(dg_ref)
            loss_ref[...] = jnp.zeros_like(loss_ref)

        dg_ref[...] += jnp.sum(dyv * xh, axis=0, keepdims=True)
        part = jnp.sum(jnp.sum(err * err, axis=-1, keepdims=True), axis=0, keepdims=True) * (0.5 / D)
        loss_ref[...] += jnp.broadcast_to(part, (1, LANES))

    blk = pl.BlockSpec((ts, D), lambda i: (i, 0))
    row = pl.BlockSpec((1, D), lambda i: (0, 0))
    return _pc(body, grid=(S // ts,), in_specs=[blk, row, blk],
               out_specs=[blk, row, pl.BlockSpec((1, LANES), lambda i: (0, 0))],
               out_shape=[jax.ShapeDtypeStruct((S, D), f32), jax.ShapeDtypeStruct((1, D), f32),
                          jax.ShapeDtypeStruct((1, LANES), f32)],
               compiler_params=_cp(("arbitrary",)), name=name)(h, g.reshape(1, D), target)


def _rope_apply(x, tc, s1, s2):
    return x * tc + pltpu.roll(x, LANES - 16, 1) * s1 + pltpu.roll(x, 16, 1) * s2


def _rope_apply_t(dy, tc, s1, s2):
    return dy * tc + pltpu.roll(dy * s1, 16, 1) + pltpu.roll(dy * s2, LANES - 16, 1)


def _rope_fwd(q, kv, proj, tabs, name="rope_fwd"):
    S = q.shape[0]
    ts = _pick(S, (512, 256, 128))
    scale = (QK_NOPE + QK_ROPE) ** -0.5

    def body(q_ref, kk_ref, kvv_ref, kr_ref, tc_ref, s1_ref, s2_ref, qh_ref, kh_ref, vh_ref):
        tc, s1, s2 = tc_ref[...], s1_ref[...], s2_ref[...]
        krr = _rope_apply(pltpu.roll(kr_ref[...], QK_NOPE, 1), tc, s1, s2)
        for h in range(MLA_HEADS):
            sl = slice(h * HEAD_PAD, (h + 1) * HEAD_PAD)
            qh_ref[:, sl] = (_rope_apply(q_ref[:, sl], tc, s1, s2) * scale).astype(bf16)
            kh_ref[:, sl] = (kk_ref[:, sl] + krr).astype(bf16)
        vh_ref[...] = kvv_ref[...].astype(bf16)

    wide = lambda c: pl.BlockSpec((ts, MLA_PAD), lambda i: (i, c))
    tab = pl.BlockSpec((ts, LANES), lambda i: (i, 0))
    return _pc(body, grid=(S // ts,),
               in_specs=[wide(0), wide(0), wide(1), pl.BlockSpec((ts, LANES), lambda i: (i, 3)), tab, tab, tab],
               out_specs=[wide(0)] * 3, out_shape=[jax.ShapeDtypeStruct((S, MLA_PAD), bf16)] * 3,
               compiler_params=_cp(("parallel",)), name=name)(q, kv, kv, proj, *tabs)


def _rope_bwd(dqh, dkh, dvh, tabs, name="rope_bwd"):
    S = dqh.shape[0]
    ts = _pick(S, (512, 256, 128))
    scale = (QK_NOPE + QK_ROPE) ** -0.5

    def body(dq_ref, dk_ref, dv_ref, tc_ref, s1_ref, s2_ref, oq_ref, okv_ref, okr_ref):
        tc, s1, s2 = tc_ref[...], s1_ref[...], s2_ref[...]
        ksum = None
        for h in range(MLA_HEADS):
            sl = slice(h * HEAD_PAD, (h + 1) * HEAD_PAD)
            oq_ref[:, sl] = (_rope_apply_t(dq_ref[:, sl], tc, s1, s2) * scale).astype(bf16)
            dk = dk_ref[:, sl]
            okv_ref[:, sl] = dk.astype(bf16)
            ksum = dk if ksum is None else ksum + dk
        okv_ref[:, MLA_PAD:] = dv_ref[...].astype(bf16)
        dkr = pltpu.roll(_rope_apply_t(ksum, tc, s1, s2), LANES - QK_NOPE, 1)
        lane = lax.broadcasted_iota(jnp.int32, (ts, LANES), 1)
        okr_ref[...] = jnp.where(lane < QK_ROPE, dkr, 0.0).astype(bf16)

    wide = pl.BlockSpec((ts, MLA_PAD), lambda i: (i, 0))
    tab = pl.BlockSpec((ts, LANES), lambda i: (i, 0))
    return _pc(body, grid=(S // ts,), in_specs=[wide, wide, wide, tab, tab, tab],
               out_specs=[wide, pl.BlockSpec((ts, 2 * MLA_PAD), lambda i: (i, 0)), tab],
               out_shape=[jax.ShapeDtypeStruct((S, MLA_PAD), bf16), jax.ShapeDtypeStruct((S, 2 * MLA_PAD), bf16),
                          jax.ShapeDtypeStruct((S, LANES), bf16)],
               compiler_params=_cp(("parallel",)), name=name)(dqh, dkh, dvh, *tabs)


ATT_BLK = 1024


def _attn_fwd(qh, kh, vh, name="attn_fwd"):
    S = qh.shape[0]
    tq = tk = min(S, ATT_BLK)
    nq, nk = S // tq, S // tk

    def body(q_ref, k_ref, v_ref, o_ref, lse_ref, m_sc, l_sc, acc_sc):
        i, j = pl.program_id(1), pl.program_id(2)

        @pl.when(j == 0)
        def _():
            m_sc[...] = jnp.full_like(m_sc, -1e30)
            l_sc[...] = jnp.zeros_like(l_sc)
            acc_sc[...] = jnp.zeros_like(acc_sc)

        def step(masked):
            s = lax.dot_general(q_ref[...], k_ref[...], (((1,), (1,)), ((), ())), preferred_element_type=f32)
            if masked:
                row = lax.broadcasted_iota(jnp.int32, (tq, tk), 0)
                col = lax.broadcasted_iota(jnp.int32, (tq, tk), 1)
                s = jnp.where(col <= row, s, -1e30)
            m_prev = m_sc[...]
            m_new = jnp.maximum(m_prev, jnp.max(s, axis=-1, keepdims=True))
            alpha = jnp.exp(m_prev - m_new)
            p = jnp.exp(s - m_new)
            l_sc[...] = alpha * l_sc[...] + jnp.sum(p, axis=-1, keepdims=True)
            acc_sc[...] = alpha * acc_sc[...] + jnp.dot(p.astype(bf16), v_ref[...], preferred_element_type=f32)
            m_sc[...] = m_new

        pl.when(j < i)(functools.partial(step, False))
        pl.when(j == i)(functools.partial(step, True))

        @pl.when(j == nk - 1)
        def _():
            l = l_sc[...]
            o_ref[...] = acc_sc[...] / l
            lse_ref[...] = jnp.broadcast_to(m_sc[...] + jnp.log(l), (tq, LANES))

    qspec = pl.BlockSpec((tq, HEAD_PAD), lambda h, i, j: (i, h))
    kspec = pl.BlockSpec((tk, HEAD_PAD), lambda h, i, j: (jnp.minimum(j, i), h))
    return _pc(body, grid=(MLA_HEADS, nq, nk), in_specs=[qspec, kspec, kspec], out_specs=[qspec, qspec],
               out_shape=[jax.ShapeDtypeStruct((S, MLA_PAD), f32)] * 2,
               scratch_shapes=[pltpu.VMEM((tq, 1), f32), pltpu.VMEM((tq, 1), f32), pltpu.VMEM((tq, HEAD_PAD), f32)],
               compiler_params=_cp(("parallel", "parallel", "arbitrary")), name=name)(qh, kh, vh)


def _attn_bwd(qh, kh, vh, do, lse, delta, name="attn_bwd"):
    S = qh.shape[0]
    tq = tk = min(S, ATT_BLK)
    nq, nk = S // tq, S // tk

    def body(q_ref, k_ref, v_ref, do_ref, lse_ref, dl_ref, dq_ref, dk_ref, dv_ref):
        j, i = pl.program_id(1), pl.program_id(2)

        @pl.when((j == 0) & (i == 0))
        def _():
            dq_ref[...] = jnp.zeros_like(dq_ref)

        @pl.when(i == 0)
        def _():
            dk_ref[...] = jnp.zeros_like(dk_ref)
            dv_ref[...] = jnp.zeros_like(dv_ref)

        def step(masked):
            nt = (((1,), (1,)), ((), ()))
            tn = (((0,), (0,)), ((), ()))
            qv, kv_, dov = q_ref[...], k_ref[...], do_ref[...]
            s = lax.dot_general(qv, kv_, nt, preferred_element_type=f32)
            p = jnp.exp(s - lse_ref[:, :1])
            if masked:
                row = lax.broadcasted_iota(jnp.int32, (tq, tk), 0)
                col = lax.broadcasted_iota(jnp.int32, (tq, tk), 1)
                p = jnp.where(col <= row, p, 0.0)
            dp = lax.dot_general(dov, v_ref[...], nt, preferred_element_type=f32)
            ds = (p * (dp - dl_ref[:, :1])).astype(bf16)
            dv_ref[...] += lax.dot_general(p.astype(bf16), dov, tn, preferred_element_type=f32)
            dk_ref[...] += lax.dot_general(ds, qv, tn, preferred_element_type=f32)
            rows = pl.ds(pl.multiple_of(i * tq, tq), tq)
            dq_ref[rows, :] += jnp.dot(ds, kv_, preferred_element_type=f32)

        pl.when(i > j)(functools.partial(step, False))
        pl.when(i == j)(functools.partial(step, True))

    qspec = pl.BlockSpec((tq, HEAD_PAD), lambda h, j, i: (jnp.maximum(i, j), h))
    kspec = pl.BlockSpec((tk, HEAD_PAD), lambda h, j, i: (j, h))
    colspec = pl.BlockSpec((S, HEAD_PAD), lambda h, j, i: (0, h))
    return _pc(body, grid=(MLA_HEADS, nk, nq), in_specs=[qspec, kspec, kspec, qspec, qspec, qspec],
               out_specs=[colspec, kspec, kspec], out_shape=[jax.ShapeDtypeStruct((S, MLA_PAD), f32)] * 3,
               compiler_params=_cp(("parallel", "arbitrary", "arbitrary")), name=name)(qh, kh, vh, do, lse, delta)


def _xattn_fwd(q, kv, name="xattn_fwd"):
    S = q.shape[0]
    M = kv.shape[0]
    tq = _pick(S, (256, 128))
    scale = X_HEAD_DIM ** -0.5

    def body(q_ref, kv_ref, o_ref):
        for h in range(X_HEADS):
            sl = slice(h * X_HEAD_DIM, (h + 1) * X_HEAD_DIM)
            k = kv_ref[:, sl]
            v = kv_ref[:, D_MODEL + h * X_HEAD_DIM:D_MODEL + (h + 1) * X_HEAD_DIM]
            s = lax.dot_general(q_ref[:, sl], k, (((1,), (1,)), ((), ())), preferred_element_type=f32) * scale
            e = jnp.exp(s - jnp.max(s, axis=-1, keepdims=True))
            p = e / jnp.sum(e, axis=-1, keepdims=True)
            o_ref[:, sl] = jnp.dot(p.astype(bf16), v, preferred_element_type=f32).astype(bf16)

    blk = pl.BlockSpec((tq, D_MODEL), lambda i: (i, 0))
    return _pc(body, grid=(S // tq,), in_specs=[blk, pl.BlockSpec((M, 2 * D_MODEL), lambda i: (0, 0))],
               out_specs=blk, out_shape=jax.ShapeDtypeStruct((S, D_MODEL), bf16),
               compiler_params=_cp(("parallel",)), name=name)(q, kv)


def _xattn_bwd(q, kv, do, name="xattn_bwd"):
    S = q.shape[0]
    M = kv.shape[0]
    tq = _pick(S, (256, 128))
    scale = X_HEAD_DIM ** -0.5

    def body(q_ref, kv_ref, do_ref, dq_ref, dkv_ref):
        i = pl.program_id(0)

        @pl.when(i == 0)
        def _():
            dkv_ref[...] = jnp.zeros_like(dkv_ref)

        nt = (((1,), (1,)), ((), ()))
        tn = (((0,), (0,)), ((), ()))
        for h in range(X_HEADS):
            sl = slice(h * X_HEAD_DIM, (h + 1) * X_HEAD_DIM)
            vsl = slice(D_MODEL + h * X_HEAD_DIM, D_MODEL + (h + 1) * X_HEAD_DIM)
            k, v, qv, dov = kv_ref[:, sl], kv_ref[:, vsl], q_ref[:, sl], do_ref[:, sl]
            s = lax.dot_general(qv, k, nt, preferred_element_type=f32) * scale
            e = jnp.exp(s - jnp.max(s, axis=-1, keepdims=True))
            p = e / jnp.sum(e, axis=-1, keepdims=True)
            dp = lax.dot_general(dov, v, nt, preferred_element_type=f32)
            ds = (p * (dp - jnp.sum(dp * p, axis=-1, keepdims=True)) * scale).astype(bf16)
            dq_ref[:, sl] = jnp.dot(ds, k, preferred_element_type=f32).astype(bf16)
            dkv_ref[:, sl] += lax.dot_general(ds, qv, tn, preferred_element_type=f32)
            dkv_ref[:, vsl] += lax.dot_general(p.astype(bf16), dov, tn, preferred_element_type=f32)

    blk = pl.BlockSpec((tq, D_MODEL), lambda i: (i, 0))
    full = pl.BlockSpec((M, 2 * D_MODEL), lambda i: (0, 0))
    return _pc(body, grid=(S // tq,), in_specs=[blk, full, blk], out_specs=[blk, full],
               out_shape=[jax.ShapeDtypeStruct((S, D_MODEL), bf16), jax.ShapeDtypeStruct((M, 2 * D_MODEL), f32)],
               compiler_params=_cp(("arbitrary",)), name=name)(q, kv, do)


ST_CHUNKS = MACRO_ST // LANES


def _apow_init(a_ref, ap_ref, bp_ref, seg):
    P = MACRO_ST
    ar, ai = a_ref[:, :P], a_ref[:, P:]
    pr, pi = ar, ai
    for r in range(seg):
        ap_ref[r:r + 1, :P] = pr
        ap_ref[r:r + 1, P:] = pi
        if r < seg - 1:
            pr, pi = pr * ar - pi * ai, pr * ai + pi * ar
    br, bi = pr, pi
    for k in range(SUBLANES):
        bp_ref[k:k + 1, :P] = pr
        bp_ref[k:k + 1, P:] = pi
        pr, pi = pr * br - pi * bi, pr * bi + pi * br


def _to_chunks(val, sc_ref):
    for c in range(2 * ST_CHUNKS):
        sc_ref[c] = val[:, c * LANES:(c + 1) * LANES]


def _from_chunks(sc_ref):
    return jnp.concatenate([sc_ref[c] for c in range(2 * ST_CHUNKS)], axis=1)


def _scan_block(sc_ref, ap_ref, bp_ref, carry_ref, seg, reverse):
    P = MACRO_ST
    sgn = -1.0 if reverse else 1.0
    rid = lax.broadcasted_iota(jnp.int32, (SUBLANES, LANES), 0)
    for c in range(ST_CHUNKS):
        lr, li = slice(c * LANES, (c + 1) * LANES), slice(P + c * LANES, P + (c + 1) * LANES)
        ci_ = ST_CHUNKS + c
        ar, ai = ap_ref[0:1, lr], sgn * ap_ref[0:1, li]
        xr = xi = None
        for i in range(seg):
            rows = pl.ds(seg - 1 - i if reverse else i, SUBLANES, stride=seg)
            sr, si = sc_ref[c, rows, :], sc_ref[ci_, rows, :]
            if i == 0:
                xr, xi = sr, si
            else:
                xr, xi = ar * xr - ai * xi + sr, ar * xi + ai * xr + si
                sc_ref[c, rows, :] = xr
                sc_ref[ci_, rows, :] = xi
        for sh in (1, 2, 4):
            pr, pi = bp_ref[sh - 1:sh, lr], sgn * bp_ref[sh - 1:sh, li]
            if reverse:
                tr = jnp.where(rid < SUBLANES - sh, pltpu.roll(xr, SUBLANES - sh, 0), 0.0)
                ti = jnp.where(rid < SUBLANES - sh, pltpu.roll(xi, SUBLANES - sh, 0), 0.0)
            else:
                tr = jnp.where(rid >= sh, pltpu.roll(xr, sh, 0), 0.0)
                ti = jnp.where(rid >= sh, pltpu.roll(xi, sh, 0), 0.0)
            xr, xi = xr + pr * tr - pi * ti, xi + pr * ti + pi * tr
        if reverse:
            bpr = jnp.zeros((SUBLANES, LANES), f32)
            bpi = jnp.zeros((SUBLANES, LANES), f32)
            for r in range(SUBLANES):
                bpr = jnp.where(rid == r, bp_ref[SUBLANES - 1 - r:SUBLANES - r, lr], bpr)
                bpi = jnp.where(rid == r, -bp_ref[SUBLANES - 1 - r:SUBLANES - r, li], bpi)
        else:
            bpr, bpi = bp_ref[:, lr], bp_ref[:, li]
        cr, cim = carry_ref[:, lr], carry_ref[:, li]
        xr, xi = xr + bpr * cr - bpi * cim, xi + bpr * cim + bpi * cr
        edge = 0 if reverse else SUBLANES - 1
        carry_ref[:, lr] = jnp.sum(jnp.where(rid == edge, xr, 0.0), axis=0, keepdims=True)
        carry_ref[:, li] = jnp.sum(jnp.where(rid == edge, xi, 0.0), axis=0, keepdims=True)
        if reverse:
            er = jnp.where(rid == SUBLANES - 1, cr, pltpu.roll(xr, SUBLANES - 1, 0))
            ei = jnp.where(rid == SUBLANES - 1, cim, pltpu.roll(xi, SUBLANES - 1, 0))
        else:
            er = jnp.where(rid == 0, cr, pltpu.roll(xr, 1, 0))
            ei = jnp.where(rid == 0, cim, pltpu.roll(xi, 1, 0))
        for i in range(seg):
            rows = pl.ds(seg - 1 - i if reverse else i, SUBLANES, stride=seg)
            pr, pi = ap_ref[i:i + 1, lr], sgn * ap_ref[i:i + 1, li]
            sc_ref[c, rows, :] += pr * er - pi * ei
            sc_ref[ci_, rows, :] += pr * ei + pi * er


def _ssm_fwd(proj, bm, cm, a, d, name="ssm_fwd"):
    S = proj.shape[0]
    tS = _pick(S, (256, 128))
    nb = S // tS
    P2 = 2 * MACRO_ST
    seg = tS // SUBLANES
    ucol0 = (D_MODEL - SSM_WIDTH) // MACRO_CH

    def body(u_ref, b_ref, c_ref, a_ref, d_ref, y_ref, xc_ref, bu_sc, ap_sc, bp_sc, car_sc):
        t = pl.program_id(1)

        @pl.when(t == 0)
        def _():
            _apow_init(a_ref, ap_sc, bp_sc, seg)
            car_sc[...] = jnp.zeros_like(car_sc)

        uv = u_ref[...]
        _to_chunks(jnp.dot(uv.astype(bf16), b_ref[...], preferred_element_type=f32), bu_sc)
        xc_ref[...] = car_sc[...]
        _scan_block(bu_sc, ap_sc, bp_sc, car_sc, seg, False)
        y_ref[...] = jnp.dot(_from_chunks(bu_sc).astype(bf16), c_ref[...], preferred_element_type=f32) + d_ref[...] * uv

    return _pc(body, grid=(SSM_MACRO, nb),
               in_specs=[pl.BlockSpec((tS, MACRO_CH), lambda m, t: (t, ucol0 + m)),
                         pl.BlockSpec((None, MACRO_CH, P2), lambda m, t: (m, 0, 0)),
                         pl.BlockSpec((None, P2, MACRO_CH), lambda m, t: (m, 0, 0)),
                         pl.BlockSpec((None, 1, P2), lambda m, t: (m, 0, 0)),
                         pl.BlockSpec((1, MACRO_CH), lambda m, t: (0, m))],
               out_specs=[pl.BlockSpec((tS, MACRO_CH), lambda m, t: (t, m)),
                          pl.BlockSpec((None, None, 1, P2), lambda m, t: (m, t, 0, 0))],
               out_shape=[jax.ShapeDtypeStruct((S, SSM_WIDTH), f32), jax.ShapeDtypeStruct((SSM_MACRO, nb, 1, P2), f32)],
               scratch_shapes=[pltpu.VMEM((2 * ST_CHUNKS, tS, LANES), f32), pltpu.VMEM((seg, P2), f32),
                               pltpu.VMEM((SUBLANES, P2), f32), pltpu.VMEM((1, P2), f32)],
               compiler_params=_cp(("arbitrary", "arbitrary")), name=name)(proj, bm, cm, a, d.reshape(1, SSM_WIDTH))


def _ssm_bwd(proj, dy, xc, bm, cm, a, d, name="ssm_bwd"):
    S = proj.shape[0]
    tS = _pick(S, (256, 128))
    nb = S // tS
    P = MACRO_ST
    P2 = 2 * P
    seg = tS // SUBLANES
    ucol0 = (D_MODEL - SSM_WIDTH) // MACRO_CH

    def body(u_ref, dy_ref, xc_ref, b_ref, c_ref, a_ref, d_ref, du_ref, db_ref, dc_ref, da_ref, dd_ref,
             x_sc, g_sc, ap_sc, bp_sc, xcar_sc, gcar_sc):
        t = pl.program_id(1)

        @pl.when(t == 0)
        def _():
            _apow_init(a_ref, ap_sc, bp_sc, seg)
            gcar_sc[...] = jnp.zeros_like(gcar_sc)
            db_ref[...] = jnp.zeros_like(db_ref)
            dc_ref[...] = jnp.zeros_like(dc_ref)
            da_ref[...] = jnp.zeros_like(da_ref)
            dd_ref[...] = jnp.zeros_like(dd_ref)

        nt = (((1,), (1,)), ((), ()))
        tn = (((0,), (0,)), ((), ()))
        uv = u_ref[...]
        ub = uv.astype(bf16)
        dyv = dy_ref[...]
        dyb = dyv.astype(bf16)
        _to_chunks(jnp.dot(ub, b_ref[...], preferred_element_type=f32), x_sc)
        xin = xc_ref[...]
        xcar_sc[...] = xin
        _scan_block(x_sc, ap_sc, bp_sc, xcar_sc, seg, False)
        _to_chunks(lax.dot_general(dyb, c_ref[...], nt, preferred_element_type=f32), g_sc)
        _scan_block(g_sc, ap_sc, bp_sc, gcar_sc, seg, True)
        xv = _from_chunks(x_sc)
        gv = _from_chunks(g_sc)
        gb = gv.astype(bf16)
        dc_ref[...] += lax.dot_general(xv.astype(bf16), dyb, tn, preferred_element_type=f32)
        db_ref[...] += lax.dot_general(ub, gb, tn, preferred_element_type=f32)
        du_ref[...] = lax.dot_general(gb, b_ref[...], nt, preferred_element_type=f32) + d_ref[...] * dyv
        dd_ref[...] += jnp.sum(dyv * uv, axis=0, keepdims=True)
        rid = lax.broadcasted_iota(jnp.int32, (tS, P2), 0)
        xp = jnp.where(rid == 0, jnp.broadcast_to(xin, (tS, P2)), pltpu.roll(xv, 1, 0))
        xpr, xpi, ggr, ggi = xp[:, :P], xp[:, P:], gv[:, :P], gv[:, P:]
        da_ref[:, :P] += jnp.sum(ggr * xpr + ggi * xpi, axis=0, keepdims=True)
        da_ref[:, P:] += jnp.sum(ggi * xpr - ggr * xpi, axis=0, keepdims=True)

    rev = lambda t: nb - 1 - t
    return _pc(body, grid=(SSM_MACRO, nb),
               in_specs=[pl.BlockSpec((tS, MACRO_CH), lambda m, t: (rev(t), ucol0 + m)),
                         pl.BlockSpec((tS, MACRO_CH), lambda m, t: (rev(t), m)),
                         pl.BlockSpec((None, None, 1, P2), lambda m, t: (m, rev(t), 0, 0)),
                         pl.BlockSpec((None, MACRO_CH, P2), lambda m, t: (m, 0, 0)),
                         pl.BlockSpec((None, P2, MACRO_CH), lambda m, t: (m, 0, 0)),
                         pl.BlockSpec((None, 1, P2), lambda m, t: (m, 0, 0)),
                         pl.BlockSpec((1, MACRO_CH), lambda m, t: (0, m))],
               out_specs=[pl.BlockSpec((tS, MACRO_CH), lambda m, t: (rev(t), m)),
                          pl.BlockSpec((None, MACRO_CH, P2), lambda m, t: (m, 0, 0)),
                          pl.BlockSpec((None, P2, MACRO_CH), lambda m, t: (m, 0, 0)),
                          pl.BlockSpec((None, 1, P2), lambda m, t: (m, 0, 0)),
                          pl.BlockSpec((1, MACRO_CH), lambda m, t: (0, m))],
               out_shape=[jax.ShapeDtypeStruct((S, SSM_WIDTH), f32),
                          jax.ShapeDtypeStruct((SSM_MACRO, MACRO_CH, P2), f32),
                          jax.ShapeDtypeStruct((SSM_MACRO, P2, MACRO_CH), f32),
                          jax.ShapeDtypeStruct((SSM_MACRO, 1, P2), f32),
                          jax.ShapeDtypeStruct((1, SSM_WIDTH), f32)],
               scratch_shapes=[pltpu.VMEM((2 * ST_CHUNKS, tS, LANES), f32), pltpu.VMEM((2 * ST_CHUNKS, tS, LANES), f32),
                               pltpu.VMEM((seg, P2), f32), pltpu.VMEM((SUBLANES, P2), f32),
                               pltpu.VMEM((1, P2), f32), pltpu.VMEM((1, P2), f32)],
               compiler_params=_cp(("arbitrary", "arbitrary")), name=name)(
        proj, dy, xc, bm, cm, a, d.reshape(1, SSM_WIDTH))


_GELU_K = math.sqrt(2.0 / math.pi)
_GELU_C = 0.044715


def _glu_fwd(y, w, b, g, name="glu_fwd"):
    S, W = y.shape
    ts = _pick(S, (512, 256, 128))

    def body(y_ref, w_ref, b_ref, g_ref, z_ref, sn_ref, ge_ref):
        yv = y_ref[...]
        cdf = 0.5 * (1.0 + jnp.tanh(_GELU_K * (yv + _GELU_C * (yv * yv * yv))))
        ge = (yv * cdf).astype(bf16)
        z = jnp.dot(ge, w_ref[...], preferred_element_type=f32) + b_ref[...]
        s = yv * jax.nn.sigmoid(z)
        rstd = lax.rsqrt(jnp.sum(s * s, axis=-1, keepdims=True) * (1.0 / W) + EPS)
        z_ref[...] = z
        sn_ref[...] = (s * rstd * g_ref[...]).astype(bf16)
        ge_ref[...] = ge

    blk = pl.BlockSpec((ts, W), lambda i: (i, 0))
    row = pl.BlockSpec((1, W), lambda i: (0, 0))
    return _pc(body, grid=(S // ts,), in_specs=[blk, pl.BlockSpec((W, W), lambda i: (0, 0)), row, row],
               out_specs=[blk, blk, blk],
               out_shape=[jax.ShapeDtypeStruct((S, W), f32), jax.ShapeDtypeStruct((S, W), bf16),
                          jax.ShapeDtypeStruct((S, W), bf16)],
               compiler_params=_cp(("parallel",)), name=name)(y, w, b.reshape(1, W), g.reshape(1, W))


def _glu_bwd(y, z, dmixed, w, g, name="glu_bwd"):
    S, W = y.shape
    ts = _pick(S, (512, 256, 128))
    dcb = MLA_PAD // W

    def body(y_ref, z_ref, dsn_ref, w_ref, g_ref, dy_ref, dz_ref, dg_ref, db_ref):
        i = pl.program_id(0)
        yv, zv, gv = y_ref[...], z_ref[...], g_ref[...]
        sig = jax.nn.sigmoid(zv)
        s = yv * sig
        rstd = lax.rsqrt(jnp.sum(s * s, axis=-1, keepdims=True) * (1.0 / W) + EPS)
        sh = s * rstd
        dsn = dsn_ref[...]
        dsh = dsn * gv
        ds = rstd * (dsh - sh * (jnp.sum(dsh * sh, axis=-1, keepdims=True) * (1.0 / W)))
        dz = ds * s * (1.0 - sig)
        dzb = dz.astype(bf16)
        dge = lax.dot_general(dzb, w_ref[...], (((1,), (1,)), ((), ())), preferred_element_type=f32)
        t = jnp.tanh(_GELU_K * (yv + _GELU_C * (yv * yv * yv)))
        dgelu = 0.5 * (1.0 + t) + 0.5 * yv * (1.0 - t * t) * _GELU_K * (1.0 + 3.0 * _GELU_C * yv * yv)
        dy_ref[...] = ds * sig + dge * dgelu
        dz_ref[...] = dzb

        @pl.when(i == 0)
        def _():
            dg_ref[...] = jnp.zeros_like(dg_ref)
            db_ref[...] = jnp.zeros_like(db_ref)

        dg_ref[...] += jnp.sum(dsn * sh, axis=0, keepdims=True)
        db_ref[...] += jnp.sum(dz, axis=0, keepdims=True)

    blk = pl.BlockSpec((ts, W), lambda i: (i, 0))
    row = pl.BlockSpec((1, W), lambda i: (0, 0))
    return _pc(body, grid=(S // ts,),
               in_specs=[blk, blk, pl.BlockSpec((ts, W), lambda i: (i, dcb)), pl.BlockSpec((W, W), lambda i: (0, 0)), row],
               out_specs=[blk, blk, row, row],
               out_shape=[jax.ShapeDtypeStruct((S, W), f32), jax.ShapeDtypeStruct((S, W), bf16),
                          jax.ShapeDtypeStruct((1, W), f32), jax.ShapeDtypeStruct((1, W), f32)],
               compiler_params=_cp(("arbitrary",)), name=name)(y, z, dmixed, w, g.reshape(1, W))


def _ffn_up(hn, wg, wu, name="ffn_up"):
    S, K = hn.shape
    F = wg.shape[1]
    tm, tn = _pick(S, (512, 256, 128)), _pick(F, (1408, 256, 128))

    def body(h_ref, wg_ref, wu_ref, g_ref, u_ref, a_ref):
        hv = h_ref[...]
        gv = jnp.dot(hv, wg_ref[...], preferred_element_type=f32)
        uv = jnp.dot(hv, wu_ref[...], preferred_element_type=f32)
        g_ref[...] = gv
        u_ref[...] = uv
        a_ref[...] = (gv * jax.nn.sigmoid(gv) * uv).astype(bf16)

    wspec = pl.BlockSpec((K, tn), lambda i, j: (0, j))
    ospec = pl.BlockSpec((tm, tn), lambda i, j: (i, j))
    return _pc(body, grid=(S // tm, F // tn), in_specs=[pl.BlockSpec((tm, K), lambda i, j: (i, 0)), wspec, wspec],
               out_specs=[ospec] * 3,
               out_shape=[jax.ShapeDtypeStruct((S, F), f32), jax.ShapeDtypeStruct((S, F), f32),
                          jax.ShapeDtypeStruct((S, F), bf16)],
               compiler_params=_cp(("parallel", "parallel")), name=name)(hn, wg, wu)


def _ffn_bwd_act(dh, wd, gate, up, name="ffn_bwd_act"):
    S, K = dh.shape
    F = wd.shape[0]
    tm, tn = _pick(S, (512, 256, 128)), _pick(F, (1408, 256, 128))

    def body(dh_ref, wd_ref, g_ref, u_ref, dg_ref, du_ref):
        dact = lax.dot_general(dh_ref[...].astype(bf16), wd_ref[...], (((1,), (1,)), ((), ())),
                               preferred_element_type=f32)
        gv, uv = g_ref[...], u_ref[...]
        sig = jax.nn.sigmoid(gv)
        dg_ref[...] = (dact * uv * (sig * (1.0 + gv * (1.0 - sig)))).astype(bf16)
        du_ref[...] = (dact * (gv * sig)).astype(bf16)

    ospec = pl.BlockSpec((tm, tn), lambda i, j: (i, j))
    return _pc(body, grid=(S // tm, F // tn),
               in_specs=[pl.BlockSpec((tm, K), lambda i, j: (i, 0)), pl.BlockSpec((tn, K), lambda i, j: (j, 0)),
                         ospec, ospec],
               out_specs=[ospec] * 2, out_shape=[jax.ShapeDtypeStruct((S, F), bf16)] * 2,
               compiler_params=_cp(("parallel", "parallel")), name=name)(dh, wd, gate, up)


def _pad_heads(w, per_head, pieces):
    K = w.shape[0]
    w3 = w.reshape(K, MLA_HEADS, per_head)
    out = jnp.zeros((K, MLA_HEADS, HEAD_PAD), w.dtype)
    for s0, s1, d0 in pieces:
        out = out.at[:, :, d0:d0 + (s1 - s0)].set(w3[:, :, s0:s1])
    return out.reshape(K, MLA_PAD)


def _unpad_heads(wp, per_head, pieces):
    K = wp.shape[0]
    w3 = wp.reshape(K, MLA_HEADS, HEAD_PAD)
    out = jnp.zeros((K, MLA_HEADS, per_head), wp.dtype)
    for s0, s1, d0 in pieces:
        out = out.at[:, :, s0:s1].set(w3[:, :, d0:d0 + (s1 - s0)])
    return out.reshape(K, MLA_HEADS * per_head)


_Q_PIECES = [(0, QK_NOPE + QK_ROPE, 0)]
_K_PIECES = [(0, QK_NOPE, 0)]
_V_PIECES = [(QK_NOPE, QK_NOPE + V_HEAD, 0)]
_KR0 = Q_LORA + KV_LORA


def _pack_win(w):
    z = jnp.zeros((w.shape[0], LANES - QK_ROPE), w.dtype)
    return jnp.concatenate([w[:, :_KR0 + QK_ROPE], z, w[:, _KR0 + QK_ROPE:]], axis=1)


def _unpack_win(wp):
    return jnp.concatenate([wp[:, :_KR0 + QK_ROPE], wp[:, _KR0 + LANES:]], axis=1)


def _pack_wout(w):
    wa = w[:MLA_WIDTH].reshape(MLA_HEADS, V_HEAD, D_MODEL)
    wa = jnp.concatenate([wa, jnp.zeros_like(wa)], axis=1).reshape(MLA_PAD, D_MODEL)
    return jnp.concatenate([wa, w[MLA_WIDTH:]], axis=0)


def _unpack_wout(wp):
    wa = wp[:MLA_PAD].reshape(MLA_HEADS, HEAD_PAD, D_MODEL)[:, :V_HEAD].reshape(MLA_WIDTH, D_MODEL)
    return jnp.concatenate([wa, wp[MLA_PAD:]], axis=0)


def _pad_gain(g):
    g2 = g.reshape(MLA_HEADS, V_HEAD)
    return jnp.concatenate([g2, jnp.zeros_like(g2)], axis=1).reshape(MLA_PAD)


def _unpad_gain(gp):
    return gp.reshape(MLA_HEADS, HEAD_PAD)[:, :V_HEAD].reshape(MLA_WIDTH)


def _ssm_prep(lam_re, lam_im, log_dt, b_re, b_im, c_re, c_im):
    lam = lax.complex(lam_re, lam_im)
    dt = jnp.exp(log_dt)[:, None]
    a_bar = jnp.exp(lam * dt)
    b_bar = ((a_bar - 1.0) / lam)[..., None] * lax.complex(b_re, b_im)
    G8 = SSM_GROUPS // SSM_MACRO
    eye = jnp.eye(G8, dtype=f32)

    def bmat(part):
        p4 = part.reshape(SSM_MACRO, G8, SSM_STATE, SSM_GROUP)
        return jnp.einsum('mgpc,gh->mgchp', p4, eye).reshape(SSM_MACRO, MACRO_CH, MACRO_ST)

    def cmat(part):
        p4 = part.reshape(SSM_MACRO, G8, SSM_GROUP, SSM_STATE)
        return jnp.einsum('mgcp,gh->mgphc', p4, eye).reshape(SSM_MACRO, MACRO_ST, MACRO_CH)

    bm = jnp.concatenate([bmat(b_bar.real), bmat(b_bar.imag)], axis=2)
    cm = jnp.concatenate([cmat(c_re), -cmat(c_im)], axis=1)
    a4 = a_bar.reshape(SSM_MACRO, 1, MACRO_ST)
    a = jnp.concatenate([a4.real, a4.imag], axis=2)
    return bm, cm, a


def _rope_tables(positions):
    freqs = ROPE_THETA ** (-jnp.arange(0, QK_ROPE, 2, dtype=f32) / QK_ROPE)
    ang = positions.astype(f32)[:, None] * freqs
    cos, sin = jnp.cos(ang), jnp.sin(ang)
    S = positions.shape[0]
    half = QK_ROPE // 2
    one, zero = jnp.ones((S, QK_NOPE), f32), jnp.zeros((S, half), f32)
    z64, z32 = jnp.zeros((S, QK_NOPE), f32), jnp.zeros((S, LANES - QK_NOPE - QK_ROPE), f32)
    tc = jnp.concatenate([one, cos, cos, z32], axis=1)
    s1 = jnp.concatenate([z64, -sin, zero, z32], axis=1)
    s2 = jnp.concatenate([z64, zero, sin, z32], axis=1)
    return tc, s1, s2


def _layer_params(W, l):
    p = {}
    p['win'] = _pack_win(W['w_in'][l])
    p['wuq'] = _pad_heads(W['w_uq'][l], QK_NOPE + QK_ROPE, _Q_PIECES)
    wukv = W['w_ukv'][l]
    p['wukv'] = jnp.concatenate([_pad_heads(wukv, QK_NOPE + V_HEAD, _K_PIECES),
                                 _pad_heads(wukv, QK_NOPE + V_HEAD, _V_PIECES)], axis=1)
    p['wout'] = _pack_wout(W['w_out'][l])
    p['attn_g'] = _pad_gain(W['attn_out_g'][l])
    return p


def _forward_layer(h, memn_in, tabs, W, l, name):
    p = _layer_params(W, l)
    sv = {'h0': h, 'p': p}
    xn = _rms_fwd(h, W['norm_mix_g'][l], name=name + "rms_mix")
    proj = _mm([(xn, p['win'])], 'nn', f32, name=name + "mm_in")
    cqn = _rms_fwd(proj, W['q_norm_g'][l], col0=0, width=Q_LORA, name=name + "rms_q")
    ckvn = _rms_fwd(proj, W['kv_norm_g'][l], col0=Q_LORA, width=KV_LORA, name=name + "rms_kv")
    q = _mm([(cqn, p['wuq'])], 'nn', f32, name=name + "mm_uq")
    kv = _mm([(ckvn, p['wukv'])], 'nn', f32, name=name + "mm_ukv")
    qh, kh, vh = _rope_fwd(q, kv, proj, tabs, name=name + "rope")
    oh, lse = _attn_fwd(qh, kh, vh, name=name + "attn")
    an = _rms_fwd(oh, p['attn_g'], n_valid=MLA_WIDTH, name=name + "rms_attn")
    bm, cm, a = W['ssm'][l]
    bmb, cmb = bm.astype(bf16), cm.astype(bf16)
    y, xc = _ssm_fwd(proj, bmb, cmb, a, W['ssm_d'][l], name=name + "ssm")
    z, sn, ge = _glu_fwd(y, W['ssm_w_glu'][l], W['ssm_b_glu'][l], W['ssm_out_g'][l], name=name + "glu")
    h1a = _mm([(an, p['wout'][:MLA_PAD])], 'nn', f32, res=h, name=name + "mm_out_a")
    h1 = _mm([(sn, p['wout'][MLA_PAD:])], 'nn', f32, res=h1a, name=name + "mm_out_s")
    hn2 = _rms_fwd(h1, W['norm_x_g'][l], name=name + "rms_x")
    memn = _rms_fwd(memn_in, W['mem_norm_g'][l], name=name + "rms_mem")
    qx = _mm([(hn2, W['w_xq'][l])], 'nn', bf16, name=name + "mm_xq")
    kvx = _mm([(memn, W['w_xkv'][l])], 'nn', bf16, name=name + "mm_xkv")
    ox = _xattn_fwd(qx, kvx, name=name + "xattn")
    h2 = _mm([(ox, W['w_xo'][l])], 'nn', f32, res=h1, name=name + "mm_xo")
    hn3 = _rms_fwd(h2, W['norm_ffn_g'][l], name=name + "rms_ffn")
    gate, up, act = _ffn_up(hn3, W['w_gate'][l], W['w_up'][l], name=name + "ffn_up")
    h3 = _mm([(act, W['w_down'][l])], 'nn', f32, res=h2, name=name + "mm_down")
    sv.update(xn=xn, proj=proj, cqn=cqn, ckvn=ckvn, qh=qh, kh=kh, vh=vh, oh=oh, lse=lse, an=an, bmb=bmb, cmb=cmb,
              a=a, y=y, xc=xc, z=z, sn=sn, ge=ge, h1=h1, hn2=hn2, memn=memn, qx=qx, kvx=kvx, ox=ox, h2=h2, hn3=hn3,
              gate=gate, up=up, act=act)
    return h3, sv


def _backward_layer(dh3, sv, memn_in, tabs, W, l, name):
    p = sv['p']
    G = {}
    G['w_down'] = _mm([(sv['act'], dh3)], 'tn', f32, name=name + "dw_down")
    dgate, dup = _ffn_bwd_act(dh3, W['w_down'][l], sv['gate'], sv['up'], name=name + "ffn_bwd_act")
    dhn3 = _mm([(dgate, W['w_gate'][l]), (dup, W['w_up'][l])], 'nt', f32, name=name + "mm_dffn")
    G['w_gate'] = _mm([(sv['hn3'], dgate)], 'tn', f32, name=name + "dw_gate")
    G['w_up'] = _mm([(sv['hn3'], dup)], 'tn', f32, name=name + "dw_up")
    dh2, dg = _rms_bwd(sv['h2'], W['norm_ffn_g'][l], dhn3, res=dh3, name=name + "rmsb_ffn")
    G['norm_ffn_g'] = dg[0]
    G['w_xo'] = _mm([(sv['ox'], dh2)], 'tn', f32, name=name + "dw_xo")
    dox = _mm([(dh2, W['w_xo'][l])], 'nt', bf16, name=name + "mm_dxo")
    dqx, dkvx = _xattn_bwd(sv['qx'], sv['kvx'], dox, name=name + "xattn_bwd")
    G['w_xq'] = _mm([(sv['hn2'], dqx)], 'tn', f32, name=name + "dw_xq")
    G['w_xkv'] = _mm([(sv['memn'], dkvx)], 'tn', f32, name=name + "dw_xkv")
    dhn2 = _mm([(dqx, W['w_xq'][l])], 'nt', f32, name=name + "mm_dxq")
    dmemn = _mm([(dkvx, W['w_xkv'][l])], 'nt', f32, name=name + "mm_dxkv")
    dh1, dg = _rms_bwd(sv['h1'], W['norm_x_g'][l], dhn2, res=dh2, name=name + "rmsb_x")
    G['norm_x_g'] = dg[0]
    _, dg = _rms_bwd(memn_in, W['mem_norm_g'][l], dmemn, name=name + "rmsb_mem")
    G['mem_norm_g'] = dg[0]
    dwo_a = _mm([(sv['an'], dh1)], 'tn', f32, name=name + "dw_out_a")
    dwo_s = _mm([(sv['sn'], dh1)], 'tn', f32, name=name + "dw_out_s")
    G['w_out'] = _unpack_wout(jnp.concatenate([dwo_a, dwo_s], axis=0))
    dmixed = _mm([(dh1, p['wout'])], 'nt', f32, name=name + "mm_dout")
    dy, dz, dg, db = _glu_bwd(sv['y'], sv['z'], dmixed, W['ssm_w_glu'][l], W['ssm_out_g'][l], name=name + "glu_bwd")
    G['ssm_out_g'], G['ssm_b_glu'] = dg[0], db[0]
    G['ssm_w_glu'] = _mm([(sv['ge'], dz)], 'tn', f32, name=name + "dw_glu")
    du, dbm, dcm, da, dd = _ssm_bwd(sv['proj'], dy, sv['xc'], sv['bmb'], sv['cmb'], sv['a'], W['ssm_d'][l],
                                    name=name + "ssm_bwd")
    G['ssm_d'] = dd[0]
    G['ssm_raw'] = (dbm, dcm, da)
    doh, dg, delta = _rms_bwd(sv['oh'], p['attn_g'], dmixed, width=MLA_PAD, n_valid=MLA_WIDTH, delta=True,
                              out_dtype=bf16, name=name + "rmsb_attn")
    G['attn_out_g'] = _unpad_gain(dg[0])
    dqh, dkh, dvh = _attn_bwd(sv['qh'], sv['kh'], sv['vh'], doh, sv['lse'], delta, name=name + "attn_bwd")
    dq, dkv, dkr = _rope_bwd(dqh, dkh, dvh, tabs, name=name + "rope_bwd")
    G['w_uq'] = _unpad_heads(_mm([(sv['cqn'], dq)], 'tn', f32, name=name + "dw_uq"), QK_NOPE + QK_ROPE, _Q_PIECES)
    dwukv = _mm([(sv['ckvn'], dkv)], 'tn', f32, name=name + "dw_ukv")
    G['w_ukv'] = (_unpad_heads(dwukv[:, :MLA_PAD], QK_NOPE + V_HEAD, _K_PIECES)
                  + _unpad_heads(dwukv[:, MLA_PAD:], QK_NOPE + V_HEAD, _V_PIECES))
    dcqn = _mm([(dq, p['wuq'])], 'nt', f32, name=name + "mm_duq")
    dckvn = _mm([(dkv, p['wukv'])], 'nt', f32, name=name + "mm_dukv")
    dcq, dg = _rms_bwd(sv['proj'], W['q_norm_g'][l], dcqn, col0=0, width=Q_LORA, out_dtype=bf16, name=name + "rmsb_q")
    G['q_norm_g'] = dg[0]
    dckv, dg = _rms_bwd(sv['proj'], W['kv_norm_g'][l], dckvn, col0=Q_LORA, width=KV_LORA, out_dtype=bf16,
                        name=name + "rmsb_kv")
    G['kv_norm_g'] = dg[0]
    dproj = jnp.concatenate([dcq, dckv, dkr, du.astype(bf16)], axis=1)
    G['w_in'] = _unpack_win(_mm([(sv['xn'], dproj)], 'tn', f32, name=name + "dw_in"))
    dxn = _mm([(dproj, p['win'])], 'nt', f32, name=name + "mm_din")
    dh0, dg = _rms_bwd(sv['h0'], W['norm_mix_g'][l], dxn, res=dh1, name=name + "rmsb_mix")
    G['norm_mix_g'] = dg[0]
    return dh0, G


def _local_step(x, mem, positions, target, W):
    tabs = _rope_tables(positions)
    ssm_in = [(W['ssm_lambda_re'][l], W['ssm_lambda_im'][l], W['ssm_log_dt'][l], W['ssm_b_re'][l], W['ssm_b_im'][l],
               W['ssm_c_re'][l], W['ssm_c_im'][l]) for l in range(DEPTH)]
    preps = [jax.vjp(_ssm_prep, *ssm_in[l]) for l in range(DEPTH)]
    W = dict(W)
    W['ssm'] = [preps[l][0] for l in range(DEPTH)]
    h = x
    saved = []
    for l in range(DEPTH):
        h, sv = _forward_layer(h, mem, tabs, W, l, f"l{l}_")
        saved.append(sv)
    dh, dgf, loss = _loss_head(h, W['final_norm_g'], target)
    grads = [None] * DEPTH
    for l in reversed(range(DEPTH)):
        dh, G = _backward_layer(dh, saved[l], mem, tabs, W, l, f"l{l}b_")
        dbm, dcm, da = G.pop('ssm_raw')
        names = ['ssm_lambda_re', 'ssm_lambda_im', 'ssm_log_dt', 'ssm_b_re', 'ssm_b_im', 'ssm_c_re', 'ssm_c_im']
        for n, g in zip(names, preps[l][1]((dbm, dcm, da))):
            G[n] = g
        grads[l] = G
    out = {n: jnp.stack([grads[l][n] for l in range(DEPTH)]) for n in grads[0]}
    out['final_norm_g'] = dgf[0]
    return loss[0, 0], dh, out


_HBM = pl.BlockSpec(memory_space=pltpu.HBM)


def _me():
    return lax.axis_index("x"), lax.axis_index("y"), lax.axis_index("c")


def _chip_peers(x, y, c):
    devs = [(1 - x, y, c), (x, 1 - y, c), (1 - x, 1 - y, c)]
    return devs, [2 * d[0] + d[1] for d in devs]


def _allgather_chips(xs, name):
    n = len(xs)

    def body(*refs):
        src, dst = refs[:n], refs[n:2 * n]
        send, recv = refs[2 * n:]
        x, y, c = _me()
        jme = 2 * x + y
        sib = (x, y, 1 - c)
        devs, js = _chip_peers(x, y, c)
        half, other = pl.ds(c, 1), pl.ds(1 - c, 1)
        first = []
        for i in range(n):
            for k in range(3):
                cp = pltpu.make_async_remote_copy(src[i].at[half], dst[i].at[half, pl.ds(jme, 1)], send.at[6 * i + k],
                                                  recv.at[6 * i + k], device_id=devs[k], device_id_type=MESH)
                cp.start()
                first.append(cp)
        passed = []
        for i in range(n):
            for k in range(3):
                slot = dst[i].at[half, pl.ds(js[k], 1)]
                pltpu.make_async_remote_copy(src[i].at[half], slot, send.at[6 * i + k], recv.at[6 * i + k],
                                             device_id=devs[k], device_id_type=MESH).wait_recv()
                fw = pltpu.make_async_remote_copy(slot, slot, send.at[6 * i + 3 + k], recv.at[6 * i + 3 + k],
                                                  device_id=sib, device_id_type=MESH)
                fw.start()
                passed.append(fw)
        for i in range(n):
            for k in range(3):
                slot = dst[i].at[other, pl.ds(js[k], 1)]
                pltpu.make_async_remote_copy(slot, slot, send.at[6 * i + 3 + k], recv.at[6 * i + 3 + k],
                                             device_id=sib, device_id_type=MESH).wait_recv()
        for cp in first + passed:
            cp.wait_send()

    ins = [t.reshape(t.shape[0], 1, *t.shape[1:]) for t in xs]
    outs = [jax.ShapeDtypeStruct((t.shape[0], 4, *t.shape[1:]), t.dtype) for t in xs]
    return _pc(body, in_specs=[_HBM] * n, out_specs=[_HBM] * n, out_shape=outs,
               scratch_shapes=[pltpu.SemaphoreType.DMA((6 * n,)), pltpu.SemaphoreType.DMA((6 * n,))],
               compiler_params=pltpu.CompilerParams(has_side_effects=True), name=name)(*ins)


def _fill_own(gathered, own):
    jme = (2 * lax.axis_index("x") + lax.axis_index("y")).astype(jnp.int32)
    zero = jnp.int32(0)
    return lax.dynamic_update_slice(gathered, own[:, None], (zero, jme, zero, zero))


def _exchange_halves(gs, name):
    n = len(gs)

    def body(*refs):
        src, dst = refs[:n], refs[n:2 * n]
        send, recv = refs[2 * n:]
        x, y, c = _me()
        cps = []
        for i in range(n):
            cp = pltpu.make_async_remote_copy(src[i].at[pl.ds(1 - c, 1)], dst[i], send.at[i], recv.at[i],
                                              device_id=(x, y, 1 - c), device_id_type=MESH)
            cp.start()
            cps.append(cp)
        for cp in cps:
            cp.wait()

    outs = [jax.ShapeDtypeStruct((1, *g.shape[1:]), g.dtype) for g in gs]
    return _pc(body, in_specs=[_HBM] * n, out_specs=[_HBM] * n, out_shape=outs,
               scratch_shapes=[pltpu.SemaphoreType.DMA((n,)), pltpu.SemaphoreType.DMA((n,))],
               compiler_params=pltpu.CompilerParams(has_side_effects=True), name=name)(*gs)


def _scatter_chips(ps, name):
    n = len(ps)

    def body(*refs):
        src, dst = refs[:n], refs[n:2 * n]
        send, recv = refs[2 * n:]
        x, y, c = _me()
        devs, js = _chip_peers(x, y, c)
        cps = []
        for i in range(n):
            for k in range(3):
                cp = pltpu.make_async_remote_copy(src[i].at[pl.ds(js[k], 1)], dst[i].at[k], send.at[3 * i + k],
                                                  recv.at[3 * i + k], device_id=devs[k], device_id_type=MESH)
                cp.start()
                cps.append(cp)
        for cp in cps:
            cp.wait()

    outs = [jax.ShapeDtypeStruct((3, 1, *p.shape[1:]), p.dtype) for p in ps]
    return _pc(body, in_specs=[_HBM] * n, out_specs=[_HBM] * n, out_shape=outs,
               scratch_shapes=[pltpu.SemaphoreType.DMA((3 * n,)), pltpu.SemaphoreType.DMA((3 * n,))],
               compiler_params=pltpu.CompilerParams(has_side_effects=True), name=name)(*ps)


def _swap_sibling(hs, name):
    n = len(hs)

    def body(*refs):
        src, dst = refs[:n], refs[n:2 * n]
        send, recv = refs[2 * n:]
        x, y, c = _me()
        cps = []
        for i in range(n):
            cp = pltpu.make_async_remote_copy(src[i], dst[i], send.at[i], recv.at[i], device_id=(x, y, 1 - c),
                                              device_id_type=MESH)
            cp.start()
            cps.append(cp)
        for cp in cps:
            cp.wait()

    outs = [jax.ShapeDtypeStruct(h.shape, h.dtype) for h in hs]
    return _pc(body, in_specs=[_HBM] * n, out_specs=[_HBM] * n, out_shape=outs,
               scratch_shapes=[pltpu.SemaphoreType.DMA((n,)), pltpu.SemaphoreType.DMA((n,))],
               compiler_params=pltpu.CompilerParams(has_side_effects=True), name=name)(*hs)


def _row_tile(r):
    return _pick(r, (256, 128, 64, 32, 16, 8))


def _add_half(g, r1, cidx, name):
    _, _, r, n = g.shape
    tr = _row_tile(r)

    def body(c_ref, g_ref, r_ref, o_ref):
        o_ref[...] = (g_ref[...] + r_ref[...]).astype(GRAD_TRANSIT)

    blk = lambda f: pl.BlockSpec((None, None, tr, n), f)
    gs = pltpu.PrefetchScalarGridSpec(
        num_scalar_prefetch=1, grid=(4, r // tr),
        in_specs=[blk(lambda j, i, c: (c[0], j, i, 0)), blk(lambda j, i, c: (0, j, i, 0))],
        out_specs=pl.BlockSpec((None, tr, n), lambda j, i, c: (j, i, 0)))
    return _pc(body, grid_spec=gs, out_shape=jax.ShapeDtypeStruct((4, r, n), GRAD_TRANSIT),
               compiler_params=_cp(("parallel", "parallel")), name=name)(cidx, g, r1)


def _add_chips(p, r3, jidx, name):
    _, r, n = p.shape
    tr = _row_tile(r)

    def body(j_ref, p_ref, a_ref, b_ref, c_ref, o_ref):
        o_ref[...] = ((p_ref[...].astype(f32) + a_ref[...].astype(f32)) + b_ref[...].astype(f32)) + c_ref[...].astype(f32)

    rblk = lambda k: pl.BlockSpec((None, None, tr, n), lambda i, j: (k, 0, i, 0))
    gs = pltpu.PrefetchScalarGridSpec(
        num_scalar_prefetch=1, grid=(r // tr,),
        in_specs=[pl.BlockSpec((None, tr, n), lambda i, j: (j[0], i, 0)), rblk(0), rblk(1), rblk(2)],
        out_specs=pl.BlockSpec((tr, n), lambda i, j: (i, 0)))
    return _pc(body, grid_spec=gs, out_shape=jax.ShapeDtypeStruct((r, n), f32),
               compiler_params=_cp(("parallel",)), name=name)(jidx, p, r3, r3, r3)


def _adamw_halves(w, mine, theirs, m, v, cidx, name):
    _, r, n = w.shape
    tr = _row_tile(r)
    c1 = 1.0 / (1.0 - ADAM_B1 ** ADAM_STEP)
    c2 = 1.0 / (1.0 - ADAM_B2 ** ADAM_STEP)

    def body(c_ref, w_ref, a_ref, b_ref, m_ref, v_ref, g_ref, d_ref, mo_ref, vo_ref):
        gv = jnp.where(pl.program_id(0) == c_ref[0], a_ref[...], b_ref[...])
        m2 = ADAM_B1 * m_ref[...] + (1.0 - ADAM_B1) * gv
        v2 = ADAM_B2 * v_ref[...] + (1.0 - ADAM_B2) * (gv * gv)
        g_ref[...] = gv
        d_ref[...] = -ADAM_LR * ((m2 * c1) / (jnp.sqrt(v2 * c2) + ADAM_EPS) + ADAM_WD * w_ref[...])
        mo_ref[...] = m2
        vo_ref[...] = v2

    full = pl.BlockSpec((None, tr, n), lambda l, i, c: (l, i, 0))
    half = pl.BlockSpec((tr, n), lambda l, i, c: (i, 0))
    gs = pltpu.PrefetchScalarGridSpec(num_scalar_prefetch=1, grid=(2, r // tr),
                                      in_specs=[full, half, half, full, full], out_specs=[full] * 4)
    return _pc(body, grid_spec=gs, out_shape=[jax.ShapeDtypeStruct(w.shape, f32)] * 4,
               compiler_params=_cp(("parallel", "parallel")), name=name)(cidx, w, mine, theirs, m, v)


def _adamw(w, g, m, v, name):
    rows, n = w.shape
    tr = _row_tile(rows)
    c1 = 1.0 / (1.0 - ADAM_B1 ** ADAM_STEP)
    c2 = 1.0 / (1.0 - ADAM_B2 ** ADAM_STEP)

    def body(w_ref, g_ref, m_ref, v_ref, d_ref, mo_ref, vo_ref):
        gv = g_ref[...]
        m2 = ADAM_B1 * m_ref[...] + (1.0 - ADAM_B1) * gv
        v2 = ADAM_B2 * v_ref[...] + (1.0 - ADAM_B2) * (gv * gv)
        d_ref[...] = -ADAM_LR * ((m2 * c1) / (jnp.sqrt(v2 * c2) + ADAM_EPS) + ADAM_WD * w_ref[...])
        mo_ref[...] = m2
        vo_ref[...] = v2

    blk = pl.BlockSpec((tr, n), lambda i: (i, 0))
    return _pc(body, grid=(rows // tr,), in_specs=[blk] * 4, out_specs=[blk] * 3,
               out_shape=[jax.ShapeDtypeStruct((rows, n), f32)] * 3,
               compiler_params=_cp(("parallel",)), name=name)(w, g, m, v)


def _full_from_gathered(name, t):
    L, _, r, n = t.shape
    if SHARDED[name] == 1:
        return t.reshape(L, 4 * r, n)
    return t.transpose(0, 2, 1, 3).reshape(L, r, 4 * n)


def _shard_major(name, g):
    L, R, C = g.shape
    if SHARDED[name] == 1:
        return g.reshape(L, 4, R // 4, C)
    return g.reshape(L, R, 4, C // 4).transpose(0, 2, 1, 3)


_SMALL_ROWS = 288


def _pack_small(d):
    flat = jnp.concatenate([d[n].reshape(-1) for n in SMALL])
    total = 2 * 4 * _SMALL_ROWS * LANES
    flat = jnp.concatenate([flat, jnp.zeros((total - flat.shape[0],), f32)])
    return flat.reshape(2, 4, _SMALL_ROWS, LANES)


def _unpack_small(t, like):
    flat = t.reshape(-1)
    out, off = {}, 0
    for n in SMALL:
        sz = math.prod(like[n].shape)
        out[n] = flat[off:off + sz].reshape(like[n].shape)
        off += sz
    return out


def kernel(x, mem, positions, norm_mix_g, w_in, q_norm_g, w_uq, kv_norm_g, w_ukv, ssm_lambda_re, ssm_lambda_im, ssm_log_dt, ssm_b_re, ssm_b_im, ssm_c_re, ssm_c_im, ssm_d, ssm_w_glu, ssm_b_glu, attn_out_g, ssm_out_g, w_out, norm_x_g, mem_norm_g, w_xq, w_xkv, w_xo, norm_ffn_g, w_gate, w_up, w_down, final_norm_g, loss_target, m_norm_mix_g, m_w_in, m_q_norm_g, m_w_uq, m_kv_norm_g, m_w_ukv, m_ssm_lambda_re, m_ssm_lambda_im, m_ssm_log_dt, m_ssm_b_re, m_ssm_b_im, m_ssm_c_re, m_ssm_c_im, m_ssm_d, m_ssm_w_glu, m_ssm_b_glu, m_attn_out_g, m_ssm_out_g, m_w_out, m_norm_x_g, m_mem_norm_g, m_w_xq, m_w_xkv, m_w_xo, m_norm_ffn_g, m_w_gate, m_w_up, m_w_down, m_final_norm_g, v_norm_mix_g, v_w_in, v_q_norm_g, v_w_uq, v_kv_norm_g, v_w_ukv, v_ssm_lambda_re, v_ssm_lambda_im, v_ssm_log_dt, v_ssm_b_re, v_ssm_b_im, v_ssm_c_re, v_ssm_c_im, v_ssm_d, v_ssm_w_glu, v_ssm_b_glu, v_attn_out_g, v_ssm_out_g, v_w_out, v_norm_x_g, v_mem_norm_g, v_w_xq, v_w_xkv, v_w_xo, v_norm_ffn_g, v_w_gate, v_w_up, v_w_down, v_final_norm_g):
    given = dict(locals())
    w = {n: given[n] for n in WEIGHTS}
    m = {n: given["m_" + n] for n in WEIGHTS}
    v = {n: given["v_" + n] for n in WEIGHTS}
    big = list(SHARDED)

    shards = [w[n].astype(bf16) for n in big]
    gathered = _allgather_chips(shards, "allgather_weights")
    W = {n: _full_from_gathered(n, _fill_own(t, s)) for n, t, s in zip(big, gathered, shards)}
    W.update({n: w[n] for n in SMALL})

    loss, dx, grads = _local_step(x[0], mem[0], positions[0], loss_target[0], W)
    loss = lax.psum(loss, ("x", "y", "c"))

    cidx = lax.axis_index("c").astype(jnp.int32).reshape(1)
    jidx = (2 * lax.axis_index("x") + lax.axis_index("y")).astype(jnp.int32).reshape(1)
    names = big + ["small"]
    gs = [_shard_major(n, grads[n]) for n in big] + [_pack_small(grads)]
    r1 = _exchange_halves(gs, "grad_exchange_halves")
    ps = [_add_half(g, r, cidx, f"grad_add_half_{n}") for n, g, r in zip(names, gs, r1)]
    r3 = _scatter_chips(ps, "grad_scatter_chips")
    hs = [_add_chips(p, r, jidx, f"grad_add_chips_{n}") for n, p, r in zip(names, ps, r3)]
    ts = _swap_sibling(hs, "grad_swap_sibling")

    out_g, out_d, out_m, out_v = {}, {}, {}, {}
    for n, h, t in zip(big, hs[:-1], ts[:-1]):
        out_g[n], out_d[n], out_m[n], out_v[n] = _adamw_halves(w[n], h, t, m[n], v[n], cidx, f"adamw_{n}")
    both = jnp.stack([hs[-1], ts[-1]])
    piece = jnp.where(cidx[0] == 0, both, both[::-1])
    gsm = _fill_own(_allgather_chips([piece], "allgather_small")[0], piece)
    rows2 = 2 * 4 * _SMALL_ROWS
    flat = lambda d: _pack_small(d).reshape(rows2, LANES)
    d_, m_, v_ = _adamw(flat(w), gsm.reshape(rows2, LANES), flat(m), flat(v), "adamw_small")
    for dst, t in ((out_g, gsm), (out_d, d_), (out_m, m_), (out_v, v_)):
        dst.update(_unpack_small(t, w))

    return (loss, dx.reshape(x.shape), *[out_g[n] for n in WEIGHTS], *[out_d[n] for n in WEIGHTS],
            *[out_m[n] for n in WEIGHTS], *[out_v[n] for n in WEIGHTS])
```

```python
import functools
import math

import jax
import jax.numpy as jnp
from jax import lax
from jax.experimental import pallas as pl
from jax.experimental.pallas import tpu as pltpu

f32, bf16 = jnp.float32, jnp.bfloat16

D_MODEL = 1024
DEPTH = 2
MLA_HEADS = 8
QK_NOPE = 64
QK_ROPE = 32
V_HEAD = 64
Q_LORA = 256
KV_LORA = 128
MLA_WIDTH = MLA_HEADS * V_HEAD
ROPE_THETA = 10000.0
SSM_WIDTH = 512
SSM_GROUP = 16
SSM_GROUPS = 32
SSM_STATE = 64
IN_WIDTH = Q_LORA + KV_LORA + QK_ROPE + SSM_WIDTH
X_HEADS = 4
X_HEAD_DIM = D_MODEL // X_HEADS
D_FF = 2816
EPS = 1e-6
ADAM_LR, ADAM_B1, ADAM_B2, ADAM_EPS, ADAM_WD, ADAM_STEP = 0.001, 0.9, 0.999, 1e-08, 0.01, 10

LANES = 128
SUBLANES = 8
HEAD_PAD = 128
MLA_PAD = MLA_HEADS * HEAD_PAD
SSM_MACRO = 4
MACRO_CH = SSM_WIDTH // SSM_MACRO
MACRO_ST = SSM_GROUPS // SSM_MACRO * SSM_STATE
VMEM_LIMIT = 56 * 1024 * 1024
GRAD_TRANSIT = bf16

WEIGHTS = ['norm_mix_g', 'w_in', 'q_norm_g', 'w_uq', 'kv_norm_g', 'w_ukv', 'ssm_lambda_re', 'ssm_lambda_im',
           'ssm_log_dt', 'ssm_b_re', 'ssm_b_im', 'ssm_c_re', 'ssm_c_im', 'ssm_d', 'ssm_w_glu', 'ssm_b_glu',
           'attn_out_g', 'ssm_out_g', 'w_out', 'norm_x_g', 'mem_norm_g', 'w_xq', 'w_xkv', 'w_xo', 'norm_ffn_g',
           'w_gate', 'w_up', 'w_down', 'final_norm_g']
SHARDED = {'w_in': 1, 'w_uq': 2, 'w_ukv': 2, 'ssm_w_glu': 1, 'w_out': 1, 'w_xq': 1, 'w_xkv': 2, 'w_xo': 1,
           'w_gate': 2, 'w_up': 2, 'w_down': 1}
SMALL = [n for n in WEIGHTS if n not in SHARDED]
MESH = pl.DeviceIdType.MESH


def _pc(body, **kw):
    return pl.pallas_call(body, **kw)


def _pick(n, prefs):
    for p in prefs:
        if n % p == 0:
            return p
    return n


def _cp(sem=None):
    return pltpu.CompilerParams(dimension_semantics=sem, vmem_limit_bytes=VMEM_LIMIT)


_TILE_CANDS = (1024, 1408, 512, 256, 128)
MM_VMEM_BUDGET = 40 * 1024 * 1024


def _mm_tiles(M, K, N, a_bytes, b_bytes, o_bytes, npair, has_res, need_acc):
    best = None
    for tm in _TILE_CANDS:
        for tk in _TILE_CANDS:
            if M % tm or K % tk:
                continue
            vm = npair * (2 * tm * tk * a_bytes + 2 * tk * N * b_bytes) + 2 * tm * N * o_bytes
            vm += tm * N * 4 * (1 + need_acc + 2 * has_res)
            if a_bytes == 4:
                vm += npair * tm * tk * 2
            if b_bytes == 4:
                vm += npair * tk * N * 2
            if vm <= MM_VMEM_BUDGET and (best is None or tm * tk > best[0]):
                best = (tm * tk, tm, tk)
    if best is None:
        return _pick(M, (256, 128)), _pick(K, (256, 128))
    return best[1], best[2]


def _mm(pairs, mode, out_dtype, res=None, name="mm"):
    a0, b0 = pairs[0]
    if mode == 'nn':
        (M, K), N = a0.shape, b0.shape[1]
        dims = (((1,), (0,)), ((), ()))
    elif mode == 'nt':
        (M, K), N = a0.shape, b0.shape[0]
        dims = (((1,), (1,)), ((), ()))
    else:
        (K, M), N = a0.shape, b0.shape[1]
        dims = (((0,), (0,)), ((), ()))
    npair = len(pairs)
    has_res = res is not None
    direct = out_dtype == f32
    tm, tk = _mm_tiles(M, K, N, a0.dtype.itemsize, b0.dtype.itemsize, jnp.dtype(out_dtype).itemsize, npair, has_res,
                       not direct)
    nk = K // tk

    def body(*refs):
        ins = refs[:2 * npair]
        res_ref = refs[2 * npair] if has_res else None
        o_ref = refs[2 * npair + has_res]
        acc = o_ref if direct else refs[2 * npair + has_res + 1]
        k = pl.program_id(1)
        s = None
        for p in range(npair):
            d = lax.dot_general(ins[2 * p][...].astype(bf16), ins[2 * p + 1][...].astype(bf16), dims,
                                preferred_element_type=f32)
            s = d if s is None else s + d

        @pl.when(k == 0)
        def _():
            acc[...] = s

        @pl.when(k > 0)
        def _():
            acc[...] += s

        if has_res or not direct:
            @pl.when(k == nk - 1)
            def _():
                r = acc[...]
                if has_res:
                    r = r + res_ref[...]
                o_ref[...] = r.astype(out_dtype)

    if mode == 'nn':
        a_spec = pl.BlockSpec((tm, tk), lambda i, k: (i, k))
        b_spec = pl.BlockSpec((tk, N), lambda i, k: (k, 0))
    elif mode == 'nt':
        a_spec = pl.BlockSpec((tm, tk), lambda i, k: (i, k))
        b_spec = pl.BlockSpec((N, tk), lambda i, k: (0, k))
    else:
        a_spec = pl.BlockSpec((tk, tm), lambda i, k: (k, i))
        b_spec = pl.BlockSpec((tk, N), lambda i, k: (k, 0))
    o_spec = pl.BlockSpec((tm, N), lambda i, k: (i, 0))
    in_specs = [a_spec, b_spec] * npair + ([o_spec] if has_res else [])
    args = [t for p in pairs for t in p] + ([res] if has_res else [])
    return _pc(body, grid=(M // tm, nk), in_specs=in_specs, out_specs=o_spec,
               out_shape=jax.ShapeDtypeStruct((M, N), out_dtype),
               scratch_shapes=[] if direct else [pltpu.VMEM((tm, N), f32)],
               compiler_params=_cp(("parallel", "arbitrary")), name=name)(*args)


def _rms_fwd(x, g, *, col0=0, width=None, n_valid=None, out_dtype=bf16, name="rms_fwd"):
    S = x.shape[0]
    width = width or x.shape[1]
    n_valid = n_valid or width
    ts = _pick(S, (512, 256, 128))
    cb = col0 // width

    def body(x_ref, g_ref, o_ref):
        xv = x_ref[...]
        ms = jnp.sum(xv * xv, axis=-1, keepdims=True) * (1.0 / n_valid)
        o_ref[...] = (xv * lax.rsqrt(ms + EPS) * g_ref[...]).astype(out_dtype)

    return _pc(body, grid=(S // ts,),
               in_specs=[pl.BlockSpec((ts, width), lambda i: (i, cb)), pl.BlockSpec((1, width), lambda i: (0, 0))],
               out_specs=pl.BlockSpec((ts, width), lambda i: (i, 0)),
               out_shape=jax.ShapeDtypeStruct((S, width), out_dtype),
               compiler_params=_cp(("parallel",)), name=name)(x, g.reshape(1, width))


def _rms_bwd(x, g, dy, *, col0=0, dcol0=0, width=None, n_valid=None, res=None, out_dtype=f32, delta=False,
             name="rms_bwd"):
    S = x.shape[0]
    width = width or x.shape[1]
    n_valid = n_valid or width
    ts = _pick(S, (512, 256, 128))
    cb, dcb = col0 // width, dcol0 // width
    has_res = res is not None

    def body(*refs):
        x_ref, g_ref, dy_ref = refs[:3]
        res_ref = refs[3] if has_res else None
        outs = refs[3 + has_res:]
        dx_ref, dg_ref = outs[0], outs[1]
        i = pl.program_id(0)
        xv = x_ref[...]
        gv = g_ref[...]
        dyv = dy_ref[...].astype(f32)
        rstd = lax.rsqrt(jnp.sum(xv * xv, axis=-1, keepdims=True) * (1.0 / n_valid) + EPS)
        xh = xv * rstd
        dxh = dyv * gv
        mean = jnp.sum(dxh * xh, axis=-1, keepdims=True) * (1.0 / n_valid)
        dx = rstd * (dxh - xh * mean)
        if delta:
            d_ref = outs[2]
            for h in range(width // LANES):
                sl = slice(h * LANES, (h + 1) * LANES)
                dsum = jnp.sum(dx[:, sl] * xv[:, sl], axis=-1, keepdims=True)
                d_ref[:, sl] = jnp.broadcast_to(dsum, (ts, LANES))
        if has_res:
            dx = dx + res_ref[...]
        dx_ref[...] = dx.astype(out_dtype)

        @pl.when(i == 0)
        def _():
            dg_ref[...] = jnp.zeros_like(dg_ref)

        dg_ref[...] += jnp.sum(dyv * xh, axis=0, keepdims=True)

    blk = lambda c: pl.BlockSpec((ts, width), lambda i: (i, c))
    in_specs = [blk(cb), pl.BlockSpec((1, width), lambda i: (0, 0)), blk(dcb)] + ([blk(0)] if has_res else [])
    out_specs = [blk(0), pl.BlockSpec((1, width), lambda i: (0, 0))] + ([blk(0)] if delta else [])
    out_shape = [jax.ShapeDtypeStruct((S, width), out_dtype), jax.ShapeDtypeStruct((1, width), f32)] + (
        [jax.ShapeDtypeStruct((S, width), f32)] if delta else [])
    args = [x, g.reshape(1, width), dy] + ([res] if has_res else [])
    return _pc(body, grid=(S // ts,), in_specs=in_specs, out_specs=out_specs, out_shape=out_shape,
               compiler_params=_cp(("arbitrary",)), name=name)(*args)


def _loss_head(h, g, target, name="loss_head"):
    S, D = h.shape
    ts = _pick(S, (512, 256, 128))

    def body(h_ref, g_ref, t_ref, dh_ref, dg_ref, loss_ref):
        i = pl.program_id(0)
        xv = h_ref[...]
        gv = g_ref[...]
        rstd = lax.rsqrt(jnp.sum(xv * xv, axis=-1, keepdims=True) * (1.0 / D) + EPS)
        xh = xv * rstd
        err = xh * gv - t_ref[...]
        dyv = err * (1.0 / D)
        dxh = dyv * gv
        mean = jnp.sum(dxh * xh, axis=-1, keepdims=True) * (1.0 / D)
        dh_ref[...] = rstd * (dxh - xh * mean)

        @pl.when(i == 0)
        def _():
            dg_ref[...] = jnp.zeros_like(dg_ref)
            loss_ref[...] = jnp.zeros_like(loss_ref)

        dg_ref[...] += jnp.sum(dyv * xh, axis=0, keepdims=True)
        part = jnp.sum(jnp.sum(err * err, axis=-1, keepdims=True), axis=0, keepdims=True) * (0.5 / D)
        loss_ref[...] += jnp.broadcast_to(part, (1, LANES))

    blk = pl.BlockSpec((ts, D), lambda i: (i, 0))
    row = pl.BlockSpec((1, D), lambda i: (0, 0))
    return _pc(body, grid=(S // ts,), in_specs=[blk, row, blk],
               out_specs=[blk, row, pl.BlockSpec((1, LANES), lambda i: (0, 0))],
               out_shape=[jax.ShapeDtypeStruct((S, D), f32), jax.ShapeDtypeStruct((1, D), f32),
                          jax.ShapeDtypeStruct((1, LANES), f32)],
               compiler_params=_cp(("arbitrary",)), name=name)(h, g.reshape(1, D), target)


def _rope_apply(x, tc, s1, s2):
    return x * tc + pltpu.roll(x, LANES - 16, 1) * s1 + pltpu.roll(x, 16, 1) * s2


def _rope_apply_t(dy, tc, s1, s2):
    return dy * tc + pltpu.roll(dy * s1, 16, 1) + pltpu.roll(dy * s2, LANES - 16, 1)


def _rope_fwd(q, kv, proj, tabs, name="rope_fwd"):
    S = q.shape[0]
    ts = _pick(S, (512, 256, 128))
    scale = (QK_NOPE + QK_ROPE) ** -0.5

    def body(q_ref, kk_ref, kvv_ref, kr_ref, tc_ref, s1_ref, s2_ref, qh_ref, kh_ref, vh_ref):
        tc, s1, s2 = tc_ref[...], s1_ref[...], s2_ref[...]
        krr = _rope_apply(pltpu.roll(kr_ref[...], QK_NOPE, 1), tc, s1, s2)
        for h in range(MLA_HEADS):
            sl = slice(h * HEAD_PAD, (h + 1) * HEAD_PAD)
            qh_ref[:, sl] = (_rope_apply(q_ref[:, sl], tc, s1, s2) * scale).astype(bf16)
            kh_ref[:, sl] = (kk_ref[:, sl] + krr).astype(bf16)
        vh_ref[...] = kvv_ref[...].astype(bf16)

    wide = lambda c: pl.BlockSpec((ts, MLA_PAD), lambda i: (i, c))
    tab = pl.BlockSpec((ts, LANES), lambda i: (i, 0))
    return _pc(body, grid=(S // ts,),
               in_specs=[wide(0), wide(0), wide(1), pl.BlockSpec((ts, LANES), lambda i: (i, 3)), tab, tab, tab],
               out_specs=[wide(0)] * 3, out_shape=[jax.ShapeDtypeStruct((S, MLA_PAD), bf16)] * 3,
               compiler_params=_cp(("parallel",)), name=name)(q, kv, kv, proj, *tabs)


def _rope_bwd(dqh, dkh, dvh, tabs, name="rope_bwd"):
    S = dqh.shape[0]
    ts = _pick(S, (512, 256, 128))
    scale = (QK_NOPE + QK_ROPE) ** -0.5

    def body(dq_ref, dk_ref, dv_ref, tc_ref, s1_ref, s2_ref, oq_ref, okv_ref, okr_ref):
        tc, s1, s2 = tc_ref[...], s1_ref[...], s2_ref[...]
        ksum = None
        for h in range(MLA_HEADS):
            sl = slice(h * HEAD_PAD, (h + 1) * HEAD_PAD)
            oq_ref[:, sl] = (_rope_apply_t(dq_ref[:, sl], tc, s1, s2) * scale).astype(bf16)
            dk = dk_ref[:, sl]
            okv_ref[:, sl] = dk.astype(bf16)
            ksum = dk if ksum is None else ksum + dk
        okv_ref[:, MLA_PAD:] = dv_ref[...].astype(bf16)
        dkr = pltpu.roll(_rope_apply_t(ksum, tc, s1, s2), LANES - QK_NOPE, 1)
        lane = lax.broadcasted_iota(jnp.int32, (ts, LANES), 1)
        okr_ref[...] = jnp.where(lane < QK_ROPE, dkr, 0.0).astype(bf16)

    wide = pl.BlockSpec((ts, MLA_PAD), lambda i: (i, 0))
    tab = pl.BlockSpec((ts, LANES), lambda i: (i, 0))
    return _pc(body, grid=(S // ts,), in_specs=[wide, wide, wide, tab, tab, tab],
               out_specs=[wide, pl.BlockSpec((ts, 2 * MLA_PAD), lambda i: (i, 0)), tab],
               out_shape=[jax.ShapeDtypeStruct((S, MLA_PAD), bf16), jax.ShapeDtypeStruct((S, 2 * MLA_PAD), bf16),
                          jax.ShapeDtypeStruct((S, LANES), bf16)],
               compiler_params=_cp(("parallel",)), name=name)(dqh, dkh, dvh, *tabs)


ATT_BLK = 1024


def _attn_fwd(qh, kh, vh, name="attn_fwd"):
    S = qh.shape[0]
    tq = tk = min(S, ATT_BLK)
    nq, nk = S // tq, S // tk

    def body(q_ref, k_ref, v_ref, o_ref, lse_ref, m_sc, l_sc, acc_sc):
        i, j = pl.program_id(1), pl.program_id(2)

        @pl.when(j == 0)
        def _():
            m_sc[...] = jnp.full_like(m_sc, -1e30)
            l_sc[...] = jnp.zeros_like(l_sc)
            acc_sc[...] = jnp.zeros_like(acc_sc)

        def step(masked):
            s = lax.dot_general(q_ref[...], k_ref[...], (((1,), (1,)), ((), ())), preferred_element_type=f32)
            if masked:
                row = lax.broadcasted_iota(jnp.int32, (tq, tk), 0)
                col = lax.broadcasted_iota(jnp.int32, (tq, tk), 1)
                s = jnp.where(col <= row, s, -1e30)
            m_prev = m_sc[...]
            m_new = jnp.maximum(m_prev, jnp.max(s, axis=-1, keepdims=True))
            alpha = jnp.exp(m_prev - m_new)
            p = jnp.exp(s - m_new)
            l_sc[...] = alpha * l_sc[...] + jnp.sum(p, axis=-1, keepdims=True)
            acc_sc[...] = alpha * acc_sc[...] + jnp.dot(p.astype(bf16), v_ref[...], preferred_element_type=f32)
            m_sc[...] = m_new

        pl.when(j < i)(functools.partial(step, False))
        pl.when(j == i)(functools.partial(step, True))

        @pl.when(j == nk - 1)
        def _():
            l = l_sc[...]
            o_ref[...] = acc_sc[...] / l
            lse_ref[...] = jnp.broadcast_to(m_sc[...] + jnp.log(l), (tq, LANES))

    qspec = pl.BlockSpec((tq, HEAD_PAD), lambda h, i, j: (i, h))
    kspec = pl.BlockSpec((tk, HEAD_PAD), lambda h, i, j: (jnp.minimum(j, i), h))
    return _pc(body, grid=(MLA_HEADS, nq, nk), in_specs=[qspec, kspec, kspec], out_specs=[qspec, qspec],
               out_shape=[jax.ShapeDtypeStruct((S, MLA_PAD), f32)] * 2,
               scratch_shapes=[pltpu.VMEM((tq, 1), f32), pltpu.VMEM((tq, 1), f32), pltpu.VMEM((tq, HEAD_PAD), f32)],
               compiler_params=_cp(("parallel", "parallel", "arbitrary")), name=name)(qh, kh, vh)


def _attn_bwd(qh, kh, vh, do, lse, delta, name="attn_bwd"):
    S = qh.shape[0]
    tq = tk = min(S, ATT_BLK)
    nq, nk = S // tq, S // tk

    def body(q_ref, k_ref, v_ref, do_ref, lse_ref, dl_ref, dq_ref, dk_ref, dv_ref):
        j, i = pl.program_id(1), pl.program_id(2)

        @pl.when((j == 0) & (i == 0))
        def _():
            dq_ref[...] = jnp.zeros_like(dq_ref)

        @pl.when(i == 0)
        def _():
            dk_ref[...] = jnp.zeros_like(dk_ref)
            dv_ref[...] = jnp.zeros_like(dv_ref)

        def step(masked):
            nt = (((1,), (1,)), ((), ()))
            tn = (((0,), (0,)), ((), ()))
            qv, kv_, dov = q_ref[...], k_ref[...], do_ref[...]
            s = lax.dot_general(qv, kv_, nt, preferred_element_type=f32)
            p = jnp.exp(s - lse_ref[:, :1])
            if masked:
                row = lax.broadcasted_iota(jnp.int32, (tq, tk), 0)
                col = lax.broadcasted_iota(jnp.int32, (tq, tk), 1)
                p = jnp.where(col <= row, p, 0.0)
            dp = lax.dot_general(dov, v_ref[...], nt, preferred_element_type=f32)
            ds = (p * (dp - dl_ref[:, :1])).astype(bf16)
            dv_ref[...] += lax.dot_general(p.astype(bf16), dov, tn, preferred_element_type=f32)
            dk_ref[...] += lax.dot_general(ds, qv, tn, preferred_element_type=f32)
            rows = pl.ds(pl.multiple_of(i * tq, tq), tq)
            dq_ref[rows, :] += jnp.dot(ds, kv_, preferred_element_type=f32)

        pl.when(i > j)(functools.partial(step, False))
        pl.when(i == j)(functools.partial(step, True))

    qspec = pl.BlockSpec((tq, HEAD_PAD), lambda h, j, i: (jnp.maximum(i, j), h))
    kspec = pl.BlockSpec((tk, HEAD_PAD), lambda h, j, i: (j, h))
    colspec = pl.BlockSpec((S, HEAD_PAD), lambda h, j, i: (0, h))
    return _pc(body, grid=(MLA_HEADS, nk, nq), in_specs=[qspec, kspec, kspec, qspec, qspec, qspec],
               out_specs=[colspec, kspec, kspec], out_shape=[jax.ShapeDtypeStruct((S, MLA_PAD), f32)] * 3,
               compiler_params=_cp(("parallel", "arbitrary", "arbitrary")), name=name)(qh, kh, vh, do, lse, delta)


def _xattn_fwd(q, kv, name="xattn_fwd"):
    S = q.shape[0]
    M = kv.shape[0]
    tq = _pick(S, (256, 128))
    scale = X_HEAD_DIM ** -0.5

    def body(q_ref, kv_ref, o_ref):
        for h in range(X_HEADS):
            sl = slice(h * X_HEAD_DIM, (h + 1) * X_HEAD_DIM)
            k = kv_ref[:, sl]
            v = kv_ref[:, D_MODEL + h * X_HEAD_DIM:D_MODEL + (h + 1) * X_HEAD_DIM]
            s = lax.dot_general(q_ref[:, sl], k, (((1,), (1,)), ((), ())), preferred_element_type=f32) * scale
            e = jnp.exp(s - jnp.max(s, axis=-1, keepdims=True))
            p = e / jnp.sum(e, axis=-1, keepdims=True)
            o_ref[:, sl] = jnp.dot(p.astype(bf16), v, preferred_element_type=f32).astype(bf16)

    blk = pl.BlockSpec((tq, D_MODEL), lambda i: (i, 0))
    return _pc(body, grid=(S // tq,), in_specs=[blk, pl.BlockSpec((M, 2 * D_MODEL), lambda i: (0, 0))],
               out_specs=blk, out_shape=jax.ShapeDtypeStruct((S, D_MODEL), bf16),
               compiler_params=_cp(("parallel",)), name=name)(q, kv)


def _xattn_bwd(q, kv, do, name="xattn_bwd"):
    S = q.shape[0]
    M = kv.shape[0]
    tq = _pick(S, (256, 128))
    scale = X_HEAD_DIM ** -0.5

    def body(q_ref, kv_ref, do_ref, dq_ref, dkv_ref):
        i = pl.program_id(0)

        @pl.when(i == 0)
        def _():
            dkv_ref[...] = jnp.zeros_like(dkv_ref)

        nt = (((1,), (1,)), ((), ()))
        tn = (((0,), (0,)), ((), ()))
        for h in range(X_HEADS):
            sl = slice(h * X_HEAD_DIM, (h + 1) * X_HEAD_DIM)
            vsl = slice(D_MODEL + h * X_HEAD_DIM, D_MODEL + (h + 1) * X_HEAD_DIM)
            k, v, qv, dov = kv_ref[:, sl], kv_ref[:, vsl], q_ref[:, sl], do_ref[:, sl]
            s = lax.dot_general(qv, k, nt, preferred_element_type=f32) * scale
            e = jnp.exp(s - jnp.max(s, axis=-1, keepdims=True))
            p = e / jnp.sum(e, axis=-1, keepdims=True)
            dp = lax.dot_general(dov, v, nt, preferred_element_type=f32)
            ds = (p * (dp - jnp.sum(dp * p, axis=-1, keepdims=True)) * scale).astype(bf16)
            dq_ref[:, sl] = jnp.dot(ds, k, preferred_element_type=f32).astype(bf16)
            dkv_ref[:, sl] += lax.dot_general(ds, qv, tn, preferred_element_type=f32)
            dkv_ref[:, vsl] += lax.dot_general(p.astype(bf16), dov, tn, preferred_element_type=f32)

    blk = pl.BlockSpec((tq, D_MODEL), lambda i: (i, 0))
    full = pl.BlockSpec((M, 2 * D_MODEL), lambda i: (0, 0))
    return _pc(body, grid=(S // tq,), in_specs=[blk, full, blk], out_specs=[blk, full],
               out_shape=[jax.ShapeDtypeStruct((S, D_MODEL), bf16), jax.ShapeDtypeStruct((M, 2 * D_MODEL), f32)],
               compiler_params=_cp(("arbitrary",)), name=name)(q, kv, do)


ST_CHUNKS = 1


def _apow_init(a_ref, ap_ref, bp_ref, seg):
    P = MACRO_ST
    ar, ai = a_ref[:, :P], a_ref[:, P:]
    pr, pi = ar, ai
    for r in range(seg):
        ap_ref[r:r + 1, :P] = pr
        ap_ref[r:r + 1, P:] = pi
        if r < seg - 1:
            pr, pi = pr * ar - pi * ai, pr * ai + pi * ar
    br, bi = pr, pi
    for k in range(SUBLANES):
        bp_ref[k:k + 1, :P] = pr
        bp_ref[k:k + 1, P:] = pi
        pr, pi = pr * br - pi * bi, pr * bi + pi * br


def _segment_perm(tS):
    seg = tS // SUBLANES
    rows = jnp.arange(tS)
    src = (rows % SUBLANES) * seg + rows // SUBLANES
    return (src[:, None] == jnp.arange(tS)[None, :]).astype(f32)


def _unpermute_rows(pt, v):
    hi = v.astype(bf16)
    r1 = v - hi.astype(f32)
    mid = r1.astype(bf16)
    lo = (r1 - mid.astype(f32)).astype(bf16)
    out = jnp.dot(pt, jnp.concatenate([hi, mid, lo], axis=1), preferred_element_type=f32)
    w = v.shape[1]
    return (out[:, :w] + out[:, w:2 * w]) + out[:, 2 * w:]


def _scan_block(sc_ref, ap_ref, bp_ref, carry_ref, e_ref, seg, reverse):
    P = MACRO_ST
    sgn = -1.0 if reverse else 1.0
    CH = P // ST_CHUNKS
    rid = lax.broadcasted_iota(jnp.int32, (SUBLANES, CH), 0)
    for c in range(ST_CHUNKS):
        lr, li = slice(c * CH, (c + 1) * CH), slice(P + c * CH, P + (c + 1) * CH)
        ar, ai = ap_ref[0:1, lr], sgn * ap_ref[0:1, li]
        xr = xi = None
        for i in range(seg):
            r = seg - 1 - i if reverse else i
            rows = slice(SUBLANES * r, SUBLANES * (r + 1))
            sr, si = sc_ref[rows, lr], sc_ref[rows, li]
            if i == 0:
                xr, xi = sr, si
            else:
                xr, xi = ar * xr - ai * xi + sr, ar * xi + ai * xr + si
                sc_ref[rows, lr] = xr
                sc_ref[rows, li] = xi
        for sh in (1, 2, 4):
            pr, pi = bp_ref[sh - 1:sh, lr], sgn * bp_ref[sh - 1:sh, li]
            if reverse:
                tr = jnp.where(rid < SUBLANES - sh, pltpu.roll(xr, SUBLANES - sh, 0), 0.0)
                ti = jnp.where(rid < SUBLANES - sh, pltpu.roll(xi, SUBLANES - sh, 0), 0.0)
            else:
                tr = jnp.where(rid >= sh, pltpu.roll(xr, sh, 0), 0.0)
                ti = jnp.where(rid >= sh, pltpu.roll(xi, sh, 0), 0.0)
            xr, xi = xr + pr * tr - pi * ti, xi + pr * ti + pi * tr
        if reverse:
            bpr = jnp.zeros((SUBLANES, CH), f32)
            bpi = jnp.zeros((SUBLANES, CH), f32)
            for r in range(SUBLANES):
                bpr = jnp.where(rid == r, bp_ref[SUBLANES - 1 - r:SUBLANES - r, lr], bpr)
                bpi = jnp.where(rid == r, -bp_ref[SUBLANES - 1 - r:SUBLANES - r, li], bpi)
        else:
            bpr, bpi = bp_ref[:, lr], bp_ref[:, li]
        cr, cim = carry_ref[:, lr], carry_ref[:, li]
        xr, xi = xr + bpr * cr - bpi * cim, xi + bpr * cim + bpi * cr
        edge = 0 if reverse else SUBLANES - 1
        carry_ref[:, lr] = jnp.sum(jnp.where(rid == edge, xr, 0.0), axis=0, keepdims=True)
        carry_ref[:, li] = jnp.sum(jnp.where(rid == edge, xi, 0.0), axis=0, keepdims=True)
        if reverse:
            er = jnp.where(rid == SUBLANES - 1, cr, pltpu.roll(xr, SUBLANES - 1, 0))
            ei = jnp.where(rid == SUBLANES - 1, cim, pltpu.roll(xi, SUBLANES - 1, 0))
        else:
            er = jnp.where(rid == 0, cr, pltpu.roll(xr, 1, 0))
            ei = jnp.where(rid == 0, cim, pltpu.roll(xi, 1, 0))
        if e_ref is not None:
            e_ref[:, lr] = er
            e_ref[:, li] = ei
        for i in range(seg):
            r = seg - 1 - i if reverse else i
            rows = slice(SUBLANES * r, SUBLANES * (r + 1))
            pr, pi = ap_ref[i:i + 1, lr], sgn * ap_ref[i:i + 1, li]
            sc_ref[rows, lr] += pr * er - pi * ei
            sc_ref[rows, li] += pr * ei + pi * er


def _ssm_fwd(proj, bm, cm, a, d, name="ssm_fwd"):
    S = proj.shape[0]
    tS = _pick(S, (256, 128))
    nb = S // tS
    P2 = 2 * MACRO_ST
    seg = tS // SUBLANES
    ucol0 = (D_MODEL - SSM_WIDTH) // MACRO_CH

    perm = _segment_perm(tS)

    def body(u_ref, b_ref, c_ref, a_ref, d_ref, pm_ref, pt_ref, y_ref, xc_ref, bu_sc, ap_sc, bp_sc, car_sc):
        t = pl.program_id(1)

        @pl.when(t == 0)
        def _():
            _apow_init(a_ref, ap_sc, bp_sc, seg)
            car_sc[...] = jnp.zeros_like(car_sc)

        uv = u_ref[...]
        up = jnp.dot(pm_ref[...], uv.astype(bf16), preferred_element_type=f32).astype(bf16)
        bu_sc[...] = jnp.dot(up, b_ref[...], preferred_element_type=f32)
        xc_ref[...] = car_sc[...]
        _scan_block(bu_sc, ap_sc, bp_sc, car_sc, None, seg, False)
        yp = jnp.dot(bu_sc[...].astype(bf16), c_ref[...], preferred_element_type=f32)
        y_ref[...] = _unpermute_rows(pt_ref[...], yp) + d_ref[...] * uv

    sq = pl.BlockSpec((tS, tS), lambda m, t: (0, 0))
    return _pc(body, grid=(SSM_MACRO, nb),
               in_specs=[pl.BlockSpec((tS, MACRO_CH), lambda m, t: (t, ucol0 + m)),
                         pl.BlockSpec((None, MACRO_CH, P2), lambda m, t: (m, 0, 0)),
                         pl.BlockSpec((None, P2, MACRO_CH), lambda m, t: (m, 0, 0)),
                         pl.BlockSpec((None, 1, P2), lambda m, t: (m, 0, 0)),
                         pl.BlockSpec((1, MACRO_CH), lambda m, t: (0, m)), sq, sq],
               out_specs=[pl.BlockSpec((tS, MACRO_CH), lambda m, t: (t, m)),
                          pl.BlockSpec((None, None, 1, P2), lambda m, t: (m, t, 0, 0))],
               out_shape=[jax.ShapeDtypeStruct((S, SSM_WIDTH), f32), jax.ShapeDtypeStruct((SSM_MACRO, nb, 1, P2), f32)],
               scratch_shapes=[pltpu.VMEM((tS, P2), f32), pltpu.VMEM((seg, P2), f32),
                               pltpu.VMEM((SUBLANES, P2), f32), pltpu.VMEM((1, P2), f32)],
               compiler_params=_cp(("arbitrary", "arbitrary")), name=name)(
        proj, bm, cm, a, d.reshape(1, SSM_WIDTH), perm.astype(bf16), perm.T.astype(bf16))


def _ssm_bwd(proj, dy, xc, bm, cm, a, d, name="ssm_bwd"):
    S = proj.shape[0]
    tS = _pick(S, (256, 128))
    nb = S // tS
    P = MACRO_ST
    P2 = 2 * P
    seg = tS // SUBLANES
    ucol0 = (D_MODEL - SSM_WIDTH) // MACRO_CH

    perm = _segment_perm(tS)

    def body(u_ref, dy_ref, xc_ref, b_ref, c_ref, a_ref, d_ref, pm_ref, pt_ref, du_ref, db_ref, dc_ref, da_ref, dd_ref,
             x_sc, g_sc, ap_sc, bp_sc, e_sc, xcar_sc, gcar_sc):
        t = pl.program_id(1)

        @pl.when(t == 0)
        def _():
            _apow_init(a_ref, ap_sc, bp_sc, seg)
            gcar_sc[...] = jnp.zeros_like(gcar_sc)
            db_ref[...] = jnp.zeros_like(db_ref)
            dc_ref[...] = jnp.zeros_like(dc_ref)
            da_ref[...] = jnp.zeros_like(da_ref)
            dd_ref[...] = jnp.zeros_like(dd_ref)

        nt = (((1,), (1,)), ((), ()))
        tn = (((0,), (0,)), ((), ()))
        uv = u_ref[...]
        dyv = dy_ref[...]
        pm = pm_ref[...]
        ub = jnp.dot(pm, uv.astype(bf16), preferred_element_type=f32).astype(bf16)
        dyb = jnp.dot(pm, dyv.astype(bf16), preferred_element_type=f32).astype(bf16)
        x_sc[...] = jnp.dot(ub, b_ref[...], preferred_element_type=f32)
        xcar_sc[...] = xc_ref[...]
        _scan_block(x_sc, ap_sc, bp_sc, xcar_sc, e_sc, seg, False)
        g_sc[...] = lax.dot_general(dyb, c_ref[...], nt, preferred_element_type=f32)
        _scan_block(g_sc, ap_sc, bp_sc, gcar_sc, None, seg, True)
        xv = x_sc[...]
        gv = g_sc[...]
        gb = gv.astype(bf16)
        dc_ref[...] += lax.dot_general(xv.astype(bf16), dyb, tn, preferred_element_type=f32)
        db_ref[...] += lax.dot_general(ub, gb, tn, preferred_element_type=f32)
        dup = lax.dot_general(gb, b_ref[...], nt, preferred_element_type=f32)
        du_ref[...] = _unpermute_rows(pt_ref[...], dup) + d_ref[...] * dyv
        dd_ref[...] += jnp.sum(dyv * uv, axis=0, keepdims=True)
        xp = jnp.concatenate([e_sc[...], xv[:tS - SUBLANES]], axis=0)
        xpr, xpi, ggr, ggi = xp[:, :P], xp[:, P:], gv[:, :P], gv[:, P:]
        da_ref[:, :P] += jnp.sum(ggr * xpr + ggi * xpi, axis=0, keepdims=True)
        da_ref[:, P:] += jnp.sum(ggi * xpr - ggr * xpi, axis=0, keepdims=True)

    rev = lambda t: nb - 1 - t
    return _pc(body, grid=(SSM_MACRO, nb),
               in_specs=[pl.BlockSpec((tS, MACRO_CH), lambda m, t: (rev(t), ucol0 + m)),
                         pl.BlockSpec((tS, MACRO_CH), lambda m, t: (rev(t), m)),
                         pl.BlockSpec((None, None, 1, P2), lambda m, t: (m, rev(t), 0, 0)),
                         pl.BlockSpec((None, MACRO_CH, P2), lambda m, t: (m, 0, 0)),
                         pl.BlockSpec((None, P2, MACRO_CH), lambda m, t: (m, 0, 0)),
                         pl.BlockSpec((None, 1, P2), lambda m, t: (m, 0, 0)),
                         pl.BlockSpec((1, MACRO_CH), lambda m, t: (0, m)),
                         pl.BlockSpec((tS, tS), lambda m, t: (0, 0)), pl.BlockSpec((tS, tS), lambda m, t: (0, 0))],
               out_specs=[pl.BlockSpec((tS, MACRO_CH), lambda m, t: (rev(t), m)),
                          pl.BlockSpec((None, MACRO_CH, P2), lambda m, t: (m, 0, 0)),
                          pl.BlockSpec((None, P2, MACRO_CH), lambda m, t: (m, 0, 0)),
                          pl.BlockSpec((None, 1, P2), lambda m, t: (m, 0, 0)),
                          pl.BlockSpec((1, MACRO_CH), lambda m, t: (0, m))],
               out_shape=[jax.ShapeDtypeStruct((S, SSM_WIDTH), f32),
                          jax.ShapeDtypeStruct((SSM_MACRO, MACRO_CH, P2), f32),
                          jax.ShapeDtypeStruct((SSM_MACRO, P2, MACRO_CH), f32),
                          jax.ShapeDtypeStruct((SSM_MACRO, 1, P2), f32),
                          jax.ShapeDtypeStruct((1, SSM_WIDTH), f32)],
               scratch_shapes=[pltpu.VMEM((tS, P2), f32), pltpu.VMEM((tS, P2), f32),
                               pltpu.VMEM((seg, P2), f32), pltpu.VMEM((SUBLANES, P2), f32), pltpu.VMEM((SUBLANES, P2), f32),
                               pltpu.VMEM((1, P2), f32), pltpu.VMEM((1, P2), f32)],
               compiler_params=_cp(("arbitrary", "arbitrary")), name=name)(
        proj, dy, xc, bm, cm, a, d.reshape(1, SSM_WIDTH), perm.astype(bf16), perm.T.astype(bf16))


_GELU_K = math.sqrt(2.0 / math.pi)
_GELU_C = 0.044715


def _glu_fwd(y, w, b, g, name="glu_fwd"):
    S, W = y.shape
    ts = _pick(S, (512, 256, 128))

    def body(y_ref, w_ref, b_ref, g_ref, z_ref, sn_ref, ge_ref):
        yv = y_ref[...]
        cdf = 0.5 * (1.0 + jnp.tanh(_GELU_K * (yv + _GELU_C * (yv * yv * yv))))
        ge = (yv * cdf).astype(bf16)
        z = jnp.dot(ge, w_ref[...], preferred_element_type=f32) + b_ref[...]
        s = yv * jax.nn.sigmoid(z)
        rstd = lax.rsqrt(jnp.sum(s * s, axis=-1, keepdims=True) * (1.0 / W) + EPS)
        z_ref[...] = z
        sn_ref[...] = (s * rstd * g_ref[...]).astype(bf16)
        ge_ref[...] = ge

    blk = pl.BlockSpec((ts, W), lambda i: (i, 0))
    row = pl.BlockSpec((1, W), lambda i: (0, 0))
    return _pc(body, grid=(S // ts,), in_specs=[blk, pl.BlockSpec((W, W), lambda i: (0, 0)), row, row],
               out_specs=[blk, blk, blk],
               out_shape=[jax.ShapeDtypeStruct((S, W), f32), jax.ShapeDtypeStruct((S, W), bf16),
                          jax.ShapeDtypeStruct((S, W), bf16)],
               compiler_params=_cp(("parallel",)), name=name)(y, w, b.reshape(1, W), g.reshape(1, W))


def _glu_bwd(y, z, dmixed, w, g, name="glu_bwd"):
    S, W = y.shape
    ts = _pick(S, (512, 256, 128))
    dcb = MLA_PAD // W

    def body(y_ref, z_ref, dsn_ref, w_ref, g_ref, dy_ref, dz_ref, dg_ref, db_ref):
        i = pl.program_id(0)
        yv, zv, gv = y_ref[...], z_ref[...], g_ref[...]
        sig = jax.nn.sigmoid(zv)
        s = yv * sig
        rstd = lax.rsqrt(jnp.sum(s * s, axis=-1, keepdims=True) * (1.0 / W) + EPS)
        sh = s * rstd
        dsn = dsn_ref[...]
        dsh = dsn * gv
        ds = rstd * (dsh - sh * (jnp.sum(dsh * sh, axis=-1, keepdims=True) * (1.0 / W)))
        dz = ds * s * (1.0 - sig)
        dzb = dz.astype(bf16)
        dge = lax.dot_general(dzb, w_ref[...], (((1,), (1,)), ((), ())), preferred_element_type=f32)
        t = jnp.tanh(_GELU_K * (yv + _GELU_C * (yv * yv * yv)))
        dgelu = 0.5 * (1.0 + t) + 0.5 * yv * (1.0 - t * t) * _GELU_K * (1.0 + 3.0 * _GELU_C * yv * yv)
        dy_ref[...] = ds * sig + dge * dgelu
        dz_ref[...] = dzb

        @pl.when(i == 0)
        def _():
            dg_ref[...] = jnp.zeros_like(dg_ref)
            db_ref[...] = jnp.zeros_like(db_ref)

        dg_ref[...] += jnp.sum(dsn * sh, axis=0, keepdims=True)
        db_ref[...] += jnp.sum(dz, axis=0, keepdims=True)

    blk = pl.BlockSpec((ts, W), lambda i: (i, 0))
    row = pl.BlockSpec((1, W), lambda i: (0, 0))
    return _pc(body, grid=(S // ts,),
               in_specs=[blk, blk, pl.BlockSpec((ts, W), lambda i: (i, dcb)), pl.BlockSpec((W, W), lambda i: (0, 0)), row],
               out_specs=[blk, blk, row, row],
               out_shape=[jax.ShapeDtypeStruct((S, W), f32), jax.ShapeDtypeStruct((S, W), bf16),
                          jax.ShapeDtypeStruct((1, W), f32), jax.ShapeDtypeStruct((1, W), f32)],
               compiler_params=_cp(("arbitrary",)), name=name)(y, z, dmixed, w, g.reshape(1, W))


def _ffn_up(hn, wg, wu, name="ffn_up"):
    S, K = hn.shape
    F = wg.shape[1]
    tm, tn = _pick(S, (512, 256, 128)), _pick(F, (1408, 256, 128))

    def body(h_ref, wg_ref, wu_ref, g_ref, u_ref, a_ref):
        hv = h_ref[...]
        gv = jnp.dot(hv, wg_ref[...], preferred_element_type=f32)
        uv = jnp.dot(hv, wu_ref[...], preferred_element_type=f32)
        g_ref[...] = gv
        u_ref[...] = uv
        a_ref[...] = (gv * jax.nn.sigmoid(gv) * uv).astype(bf16)

    wspec = pl.BlockSpec((K, tn), lambda i, j: (0, j))
    ospec = pl.BlockSpec((tm, tn), lambda i, j: (i, j))
    return _pc(body, grid=(S // tm, F // tn), in_specs=[pl.BlockSpec((tm, K), lambda i, j: (i, 0)), wspec, wspec],
               out_specs=[ospec] * 3,
               out_shape=[jax.ShapeDtypeStruct((S, F), f32), jax.ShapeDtypeStruct((S, F), f32),
                          jax.ShapeDtypeStruct((S, F), bf16)],
               compiler_params=_cp(("parallel", "parallel")), name=name)(hn, wg, wu)


def _ffn_bwd_act(dh, wd, gate, up, name="ffn_bwd_act"):
    S, K = dh.shape
    F = wd.shape[0]
    tm, tn = _pick(S, (512, 256, 128)), _pick(F, (1408, 256, 128))

    def body(dh_ref, wd_ref, g_ref, u_ref, dg_ref, du_ref):
        dact = lax.dot_general(dh_ref[...].astype(bf16), wd_ref[...], (((1,), (1,)), ((), ())),
                               preferred_element_type=f32)
        gv, uv = g_ref[...], u_ref[...]
        sig = jax.nn.sigmoid(gv)
        dg_ref[...] = (dact * uv * (sig * (1.0 + gv * (1.0 - sig)))).astype(bf16)
        du_ref[...] = (dact * (gv * sig)).astype(bf16)

    ospec = pl.BlockSpec((tm, tn), lambda i, j: (i, j))
    return _pc(body, grid=(S // tm, F // tn),
               in_specs=[pl.BlockSpec((tm, K), lambda i, j: (i, 0)), pl.BlockSpec((tn, K), lambda i, j: (j, 0)),
                         ospec, ospec],
               out_specs=[ospec] * 2, out_shape=[jax.ShapeDtypeStruct((S, F), bf16)] * 2,
               compiler_params=_cp(("parallel", "parallel")), name=name)(dh, wd, gate, up)


def _pad_heads(w, per_head, pieces):
    K = w.shape[0]
    w3 = w.reshape(K, MLA_HEADS, per_head)
    out = jnp.zeros((K, MLA_HEADS, HEAD_PAD), w.dtype)
    for s0, s1, d0 in pieces:
        out = out.at[:, :, d0:d0 + (s1 - s0)].set(w3[:, :, s0:s1])
    return out.reshape(K, MLA_PAD)


def _unpad_heads(wp, per_head, pieces):
    K = wp.shape[0]
    w3 = wp.reshape(K, MLA_HEADS, HEAD_PAD)
    out = jnp.zeros((K, MLA_HEADS, per_head), wp.dtype)
    for s0, s1, d0 in pieces:
        out = out.at[:, :, s0:s1].set(w3[:, :, d0:d0 + (s1 - s0)])
    return out.reshape(K, MLA_HEADS * per_head)


_Q_PIECES = [(0, QK_NOPE + QK_ROPE, 0)]
_K_PIECES = [(0, QK_NOPE, 0)]
_V_PIECES = [(QK_NOPE, QK_NOPE + V_HEAD, 0)]
_KR0 = Q_LORA + KV_LORA


def _pack_win(w):
    z = jnp.zeros((w.shape[0], LANES - QK_ROPE), w.dtype)
    return jnp.concatenate([w[:, :_KR0 + QK_ROPE], z, w[:, _KR0 + QK_ROPE:]], axis=1)


def _unpack_win(wp):
    return jnp.concatenate([wp[:, :_KR0 + QK_ROPE], wp[:, _KR0 + LANES:]], axis=1)


def _pack_wout(w):
    wa = w[:MLA_WIDTH].reshape(MLA_HEADS, V_HEAD, D_MODEL)
    wa = jnp.concatenate([wa, jnp.zeros_like(wa)], axis=1).reshape(MLA_PAD, D_MODEL)
    return jnp.concatenate([wa, w[MLA_WIDTH:]], axis=0)


def _unpack_wout(wp):
    wa = wp[:MLA_PAD].reshape(MLA_HEADS, HEAD_PAD, D_MODEL)[:, :V_HEAD].reshape(MLA_WIDTH, D_MODEL)
    return jnp.concatenate([wa, wp[MLA_PAD:]], axis=0)


def _pad_gain(g):
    g2 = g.reshape(MLA_HEADS, V_HEAD)
    return jnp.concatenate([g2, jnp.zeros_like(g2)], axis=1).reshape(MLA_PAD)


def _unpad_gain(gp):
    return gp.reshape(MLA_HEADS, HEAD_PAD)[:, :V_HEAD].reshape(MLA_WIDTH)


def _ssm_prep(lam_re, lam_im, log_dt, b_re, b_im, c_re, c_im):
    lam = lax.complex(lam_re, lam_im)
    dt = jnp.exp(log_dt)[:, None]
    a_bar = jnp.exp(lam * dt)
    b_bar = ((a_bar - 1.0) / lam)[..., None] * lax.complex(b_re, b_im)
    G8 = SSM_GROUPS // SSM_MACRO
    eye = jnp.eye(G8, dtype=f32)

    def bmat(part):
        p4 = part.reshape(SSM_MACRO, G8, SSM_STATE, SSM_GROUP)
        return jnp.einsum('mgpc,gh->mgchp', p4, eye).reshape(SSM_MACRO, MACRO_CH, MACRO_ST)

    def cmat(part):
        p4 = part.reshape(SSM_MACRO, G8, SSM_GROUP, SSM_STATE)
        return jnp.einsum('mgcp,gh->mgphc', p4, eye).reshape(SSM_MACRO, MACRO_ST, MACRO_CH)

    bm = jnp.concatenate([bmat(b_bar.real), bmat(b_bar.imag)], axis=2)
    cm = jnp.concatenate([cmat(c_re), -cmat(c_im)], axis=1)
    a4 = a_bar.reshape(SSM_MACRO, 1, MACRO_ST)
    a = jnp.concatenate([a4.real, a4.imag], axis=2)
    return bm, cm, a


def _rope_tables(positions):
    freqs = ROPE_THETA ** (-jnp.arange(0, QK_ROPE, 2, dtype=f32) / QK_ROPE)
    ang = positions.astype(f32)[:, None] * freqs
    cos, sin = jnp.cos(ang), jnp.sin(ang)
    S = positions.shape[0]
    half = QK_ROPE // 2
    one, zero = jnp.ones((S, QK_NOPE), f32), jnp.zeros((S, half), f32)
    z64, z32 = jnp.zeros((S, QK_NOPE), f32), jnp.zeros((S, LANES - QK_NOPE - QK_ROPE), f32)
    tc = jnp.concatenate([one, cos, cos, z32], axis=1)
    s1 = jnp.concatenate([z64, -sin, zero, z32], axis=1)
    s2 = jnp.concatenate([z64, zero, sin, z32], axis=1)
    return tc, s1, s2


def _layer_params(W, l):
    p = {}
    p['win'] = _pack_win(W['w_in'][l])
    p['wuq'] = _pad_heads(W['w_uq'][l], QK_NOPE + QK_ROPE, _Q_PIECES)
    wukv = W['w_ukv'][l]
    p['wukv'] = jnp.concatenate([_pad_heads(wukv, QK_NOPE + V_HEAD, _K_PIECES),
                                 _pad_heads(wukv, QK_NOPE + V_HEAD, _V_PIECES)], axis=1)
    p['wout'] = _pack_wout(W['w_out'][l])
    p['attn_g'] = _pad_gain(W['attn_out_g'][l])
    return p


def _forward_layer(h, memn_in, tabs, W, l, name):
    p = _layer_params(W, l)
    sv = {'h0': h, 'p': p}
    xn = _rms_fwd(h, W['norm_mix_g'][l], name=name + "rms_mix")
    proj = _mm([(xn, p['win'])], 'nn', f32, name=name + "mm_in")
    cqn = _rms_fwd(proj, W['q_norm_g'][l], col0=0, width=Q_LORA, name=name + "rms_q")
    ckvn = _rms_fwd(proj, W['kv_norm_g'][l], col0=Q_LORA, width=KV_LORA, name=name + "rms_kv")
    q = _mm([(cqn, p['wuq'])], 'nn', f32, name=name + "mm_uq")
    kv = _mm([(ckvn, p['wukv'])], 'nn', f32, name=name + "mm_ukv")
    qh, kh, vh = _rope_fwd(q, kv, proj, tabs, name=name + "rope")
    oh, lse = _attn_fwd(qh, kh, vh, name=name + "attn")
    an = _rms_fwd(oh, p['attn_g'], n_valid=MLA_WIDTH, name=name + "rms_attn")
    bm, cm, a = W['ssm'][l]
    bmb, cmb = bm.astype(bf16), cm.astype(bf16)
    y, xc = _ssm_fwd(proj, bmb, cmb, a, W['ssm_d'][l], name=name + "ssm")
    z, sn, ge = _glu_fwd(y, W['ssm_w_glu'][l], W['ssm_b_glu'][l], W['ssm_out_g'][l], name=name + "glu")
    h1a = _mm([(an, p['wout'][:MLA_PAD])], 'nn', f32, res=h, name=name + "mm_out_a")
    h1 = _mm([(sn, p['wout'][MLA_PAD:])], 'nn', f32, res=h1a, name=name + "mm_out_s")
    hn2 = _rms_fwd(h1, W['norm_x_g'][l], name=name + "rms_x")
    memn = _rms_fwd(memn_in, W['mem_norm_g'][l], name=name + "rms_mem")
    qx = _mm([(hn2, W['w_xq'][l])], 'nn', bf16, name=name + "mm_xq")
    kvx = _mm([(memn, W['w_xkv'][l])], 'nn', bf16, name=name + "mm_xkv")
    ox = _xattn_fwd(qx, kvx, name=name + "xattn")
    h2 = _mm([(ox, W['w_xo'][l])], 'nn', f32, res=h1, name=name + "mm_xo")
    hn3 = _rms_fwd(h2, W['norm_ffn_g'][l], name=name + "rms_ffn")
    gate, up, act = _ffn_up(hn3, W['w_gate'][l], W['w_up'][l], name=name + "ffn_up")
    h3 = _mm([(act, W['w_down'][l])], 'nn', f32, res=h2, name=name + "mm_down")
    sv.update(xn=xn, proj=proj, cqn=cqn, ckvn=ckvn, qh=qh, kh=kh, vh=vh, oh=oh, lse=lse, an=an, bmb=bmb, cmb=cmb,
              a=a, y=y, xc=xc, z=z, sn=sn, ge=ge, h1=h1, hn2=hn2, memn=memn, qx=qx, kvx=kvx, ox=ox, h2=h2, hn3=hn3,
              gate=gate, up=up, act=act)
    return h3, sv


def _backward_layer(dh3, sv, memn_in, tabs, W, l, name):
    p = sv['p']
    G = {}
    G['w_down'] = _mm([(sv['act'], dh3)], 'tn', f32, name=name + "dw_down")
    dgate, dup = _ffn_bwd_act(dh3, W['w_down'][l], sv['gate'], sv['up'], name=name + "ffn_bwd_act")
    dhn3 = _mm([(dgate, W['w_gate'][l]), (dup, W['w_up'][l])], 'nt', f32, name=name + "mm_dffn")
    G['w_gate'] = _mm([(sv['hn3'], dgate)], 'tn', f32, name=name + "dw_gate")
    G['w_up'] = _mm([(sv['hn3'], dup)], 'tn', f32, name=name + "dw_up")
    dh2, dg = _rms_bwd(sv['h2'], W['norm_ffn_g'][l], dhn3, res=dh3, name=name + "rmsb_ffn")
    G['norm_ffn_g'] = dg[0]
    G['w_xo'] = _mm([(sv['ox'], dh2)], 'tn', f32, name=name + "dw_xo")
    dox = _mm([(dh2, W['w_xo'][l])], 'nt', bf16, name=name + "mm_dxo")
    dqx, dkvx = _xattn_bwd(sv['qx'], sv['kvx'], dox, name=name + "xattn_bwd")
    G['w_xq'] = _mm([(sv['hn2'], dqx)], 'tn', f32, name=name + "dw_xq")
    G['w_xkv'] = _mm([(sv['memn'], dkvx)], 'tn', f32, name=name + "dw_xkv")
    dhn2 = _mm([(dqx, W['w_xq'][l])], 'nt', f32, name=name + "mm_dxq")
    dmemn = _mm([(dkvx, W['w_xkv'][l])], 'nt', f32, name=name + "mm_dxkv")
    dh1, dg = _rms_bwd(sv['h1'], W['norm_x_g'][l], dhn2, res=dh2, name=name + "rmsb_x")
    G['norm_x_g'] = dg[0]
    _, dg = _rms_bwd(memn_in, W['mem_norm_g'][l], dmemn, name=name + "rmsb_mem")
    G['mem_norm_g'] = dg[0]
    dwo_a = _mm([(sv['an'], dh1)], 'tn', f32, name=name + "dw_out_a")
    dwo_s = _mm([(sv['sn'], dh1)], 'tn', f32, name=name + "dw_out_s")
    G['w_out'] = _unpack_wout(jnp.concatenate([dwo_a, dwo_s], axis=0))
    dmixed = _mm([(dh1, p['wout'])], 'nt', f32, name=name + "mm_dout")
    dy, dz, dg, db = _glu_bwd(sv['y'], sv['z'], dmixed, W['ssm_w_glu'][l], W['ssm_out_g'][l], name=name + "glu_bwd")
    G['ssm_out_g'], G['ssm_b_glu'] = dg[0], db[0]
    G['ssm_w_glu'] = _mm([(sv['ge'], dz)], 'tn', f32, name=name + "dw_glu")
    du, dbm, dcm, da, dd = _ssm_bwd(sv['proj'], dy, sv['xc'], sv['bmb'], sv['cmb'], sv['a'], W['ssm_d'][l],
                                    name=name + "ssm_bwd")
    G['ssm_d'] = dd[0]
    G['ssm_raw'] = (dbm, dcm, da)
    doh, dg, delta = _rms_bwd(sv['oh'], p['attn_g'], dmixed, width=MLA_PAD, n_valid=MLA_WIDTH, delta=True,
                              out_dtype=bf16, name=name + "rmsb_attn")
    G['attn_out_g'] = _unpad_gain(dg[0])
    dqh, dkh, dvh = _attn_bwd(sv['qh'], sv['kh'], sv['vh'], doh, sv['lse'], delta, name=name + "attn_bwd")
    dq, dkv, dkr = _rope_bwd(dqh, dkh, dvh, tabs, name=name + "rope_bwd")
    G['w_uq'] = _unpad_heads(_mm([(sv['cqn'], dq)], 'tn', f32, name=name + "dw_uq"), QK_NOPE + QK_ROPE, _Q_PIECES)
    dwukv = _mm([(sv['ckvn'], dkv)], 'tn', f32, name=name + "dw_ukv")
    G['w_ukv'] = (_unpad_heads(dwukv[:, :MLA_PAD], QK_NOPE + V_HEAD, _K_PIECES)
                  + _unpad_heads(dwukv[:, MLA_PAD:], QK_NOPE + V_HEAD, _V_PIECES))
    dcqn = _mm([(dq, p['wuq'])], 'nt', f32, name=name + "mm_duq")
    dckvn = _mm([(dkv, p['wukv'])], 'nt', f32, name=name + "mm_dukv")
    dcq, dg = _rms_bwd(sv['proj'], W['q_norm_g'][l], dcqn, col0=0, width=Q_LORA, out_dtype=bf16, name=name + "rmsb_q")
    G['q_norm_g'] = dg[0]
    dckv, dg = _rms_bwd(sv['proj'], W['kv_norm_g'][l], dckvn, col0=Q_LORA, width=KV_LORA, out_dtype=bf16,
                        name=name + "rmsb_kv")
    G['kv_norm_g'] = dg[0]
    dproj = jnp.concatenate([dcq, dckv, dkr, du.astype(bf16)], axis=1)
    G['w_in'] = _unpack_win(_mm([(sv['xn'], dproj)], 'tn', f32, name=name + "dw_in"))
    dxn = _mm([(dproj, p['win'])], 'nt', f32, name=name + "mm_din")
    dh0, dg = _rms_bwd(sv['h0'], W['norm_mix_g'][l], dxn, res=dh1, name=name + "rmsb_mix")
    G['norm_mix_g'] = dg[0]
    return dh0, G


def _local_step(x, mem, positions, target, W):
    tabs = _rope_tables(positions)
    ssm_in = [(W['ssm_lambda_re'][l], W['ssm_lambda_im'][l], W['ssm_log_dt'][l], W['ssm_b_re'][l], W['ssm_b_im'][l],
               W['ssm_c_re'][l], W['ssm_c_im'][l]) for l in range(DEPTH)]
    preps = [jax.vjp(_ssm_prep, *ssm_in[l]) for l in range(DEPTH)]
    W = dict(W)
    W['ssm'] = [preps[l][0] for l in range(DEPTH)]
    h = x
    saved = []
    for l in range(DEPTH):
        h, sv = _forward_layer(h, mem, tabs, W, l, f"l{l}_")
        saved.append(sv)
    dh, dgf, loss = _loss_head(h, W['final_norm_g'], target)
    grads = [None] * DEPTH
    for l in reversed(range(DEPTH)):
        dh, G = _backward_layer(dh, saved[l], mem, tabs, W, l, f"l{l}b_")
        dbm, dcm, da = G.pop('ssm_raw')
        names = ['ssm_lambda_re', 'ssm_lambda_im', 'ssm_log_dt', 'ssm_b_re', 'ssm_b_im', 'ssm_c_re', 'ssm_c_im']
        for n, g in zip(names, preps[l][1]((dbm, dcm, da))):
            G[n] = g
        grads[l] = G
    out = {n: jnp.stack([grads[l][n] for l in range(DEPTH)]) for n in grads[0]}
    out['final_norm_g'] = dgf[0]
    return loss[0, 0], dh, out


_HBM = pl.BlockSpec(memory_space=pltpu.HBM)


def _me():
    return lax.axis_index("x"), lax.axis_index("y"), lax.axis_index("c")


def _chip_peers(x, y, c):
    devs = [(1 - x, y, c), (x, 1 - y, c), (1 - x, 1 - y, c)]
    return devs, [2 * d[0] + d[1] for d in devs]


def _allgather_chips(xs, name):
    n = len(xs)

    def body(*refs):
        src, dst = refs[:n], refs[n:2 * n]
        send, recv = refs[2 * n:]
        x, y, c = _me()
        jme = 2 * x + y
        sib = (x, y, 1 - c)
        devs, js = _chip_peers(x, y, c)
        half, other = pl.ds(c, 1), pl.ds(1 - c, 1)
        first = []
        for i in range(n):
            for k in range(3):
                cp = pltpu.make_async_remote_copy(src[i].at[half], dst[i].at[half, pl.ds(jme, 1)], send.at[6 * i + k],
                                                  recv.at[6 * i + k], device_id=devs[k], device_id_type=MESH)
                cp.start()
                first.append(cp)
        passed = []
        for i in range(n):
            for k in range(3):
                slot = dst[i].at[half, pl.ds(js[k], 1)]
                pltpu.make_async_remote_copy(src[i].at[half], slot, send.at[6 * i + k], recv.at[6 * i + k],
                                             device_id=devs[k], device_id_type=MESH).wait_recv()
                fw = pltpu.make_async_remote_copy(slot, slot, send.at[6 * i + 3 + k], recv.at[6 * i + 3 + k],
                                                  device_id=sib, device_id_type=MESH)
                fw.start()
                passed.append(fw)
        for i in range(n):
            for k in range(3):
                slot = dst[i].at[other, pl.ds(js[k], 1)]
                pltpu.make_async_remote_copy(slot, slot, send.at[6 * i + 3 + k], recv.at[6 * i + 3 + k],
                                             device_id=sib, device_id_type=MESH).wait_recv()
        for cp in first + passed:
            cp.wait_send()

    ins = [t.reshape(t.shape[0], 1, *t.shape[1:]) for t in xs]
    outs = [jax.ShapeDtypeStruct((t.shape[0], 4, *t.shape[1:]), t.dtype) for t in xs]
    return _pc(body, in_specs=[_HBM] * n, out_specs=[_HBM] * n, out_shape=outs,
               scratch_shapes=[pltpu.SemaphoreType.DMA((6 * n,)), pltpu.SemaphoreType.DMA((6 * n,))],
               compiler_params=pltpu.CompilerParams(has_side_effects=True), name=name)(*ins)


def _fill_own(gathered, own):
    jme = (2 * lax.axis_index("x") + lax.axis_index("y")).astype(jnp.int32)
    zero = jnp.int32(0)
    return lax.dynamic_update_slice(gathered, own[:, None], (zero, jme, zero, zero))


def _exchange_halves(gs, name):
    n = len(gs)

    def body(*refs):
        src, dst = refs[:n], refs[n:2 * n]
        send, recv = refs[2 * n:]
        x, y, c = _me()
        cps = []
        for i in range(n):
            cp = pltpu.make_async_remote_copy(src[i].at[pl.ds(1 - c, 1)], dst[i], send.at[i], recv.at[i],
                                              device_id=(x, y, 1 - c), device_id_type=MESH)
            cp.start()
            cps.append(cp)
        for cp in cps:
            cp.wait()

    outs = [jax.ShapeDtypeStruct((1, *g.shape[1:]), g.dtype) for g in gs]
    return _pc(body, in_specs=[_HBM] * n, out_specs=[_HBM] * n, out_shape=outs,
               scratch_shapes=[pltpu.SemaphoreType.DMA((n,)), pltpu.SemaphoreType.DMA((n,))],
               compiler_params=pltpu.CompilerParams(has_side_effects=True), name=name)(*gs)


def _scatter_chips(ps, name):
    n = len(ps)

    def body(*refs):
        src, dst = refs[:n], refs[n:2 * n]
        send, recv = refs[2 * n:]
        x, y, c = _me()
        devs, js = _chip_peers(x, y, c)
        cps = []
        for i in range(n):
            for k in range(3):
                cp = pltpu.make_async_remote_copy(src[i].at[pl.ds(js[k], 1)], dst[i].at[k], send.at[3 * i + k],
                                                  recv.at[3 * i + k], device_id=devs[k], device_id_type=MESH)
                cp.start()
                cps.append(cp)
        for cp in cps:
            cp.wait()

    outs = [jax.ShapeDtypeStruct((3, 1, *p.shape[1:]), p.dtype) for p in ps]
    return _pc(body, in_specs=[_HBM] * n, out_specs=[_HBM] * n, out_shape=outs,
               scratch_shapes=[pltpu.SemaphoreType.DMA((3 * n,)), pltpu.SemaphoreType.DMA((3 * n,))],
               compiler_params=pltpu.CompilerParams(has_side_effects=True), name=name)(*ps)


def _swap_sibling(hs, name):
    n = len(hs)

    def body(*refs):
        src, dst = refs[:n], refs[n:2 * n]
        send, recv = refs[2 * n:]
        x, y, c = _me()
        cps = []
        for i in range(n):
            cp = pltpu.make_async_remote_copy(src[i], dst[i], send.at[i], recv.at[i], device_id=(x, y, 1 - c),
                                              device_id_type=MESH)
            cp.start()
            cps.append(cp)
        for cp in cps:
            cp.wait()

    outs = [jax.ShapeDtypeStruct(h.shape, h.dtype) for h in hs]
    return _pc(body, in_specs=[_HBM] * n, out_specs=[_HBM] * n, out_shape=outs,
               scratch_shapes=[pltpu.SemaphoreType.DMA((n,)), pltpu.SemaphoreType.DMA((n,))],
               compiler_params=pltpu.CompilerParams(has_side_effects=True), name=name)(*hs)


def _row_tile(r):
    return _pick(r, (256, 128, 64, 32, 16, 8))


def _add_half(g, r1, cidx, name):
    _, _, r, n = g.shape
    tr = _row_tile(r)

    def body(c_ref, g_ref, r_ref, o_ref):
        o_ref[...] = (g_ref[...] + r_ref[...]).astype(GRAD_TRANSIT)

    blk = lambda f: pl.BlockSpec((None, None, tr, n), f)
    gs = pltpu.PrefetchScalarGridSpec(
        num_scalar_prefetch=1, grid=(4, r // tr),
        in_specs=[blk(lambda j, i, c: (c[0], j, i, 0)), blk(lambda j, i, c: (0, j, i, 0))],
        out_specs=pl.BlockSpec((None, tr, n), lambda j, i, c: (j, i, 0)))
    return _pc(body, grid_spec=gs, out_shape=jax.ShapeDtypeStruct((4, r, n), GRAD_TRANSIT),
               compiler_params=_cp(("parallel", "parallel")), name=name)(cidx, g, r1)


def _add_chips(p, r3, jidx, name):
    _, r, n = p.shape
    tr = _row_tile(r)

    def body(j_ref, p_ref, a_ref, b_ref, c_ref, o_ref):
        o_ref[...] = ((p_ref[...].astype(f32) + a_ref[...].astype(f32)) + b_ref[...].astype(f32)) + c_ref[...].astype(f32)

    rblk = lambda k: pl.BlockSpec((None, None, tr, n), lambda i, j: (k, 0, i, 0))
    gs = pltpu.PrefetchScalarGridSpec(
        num_scalar_prefetch=1, grid=(r // tr,),
        in_specs=[pl.BlockSpec((None, tr, n), lambda i, j: (j[0], i, 0)), rblk(0), rblk(1), rblk(2)],
        out_specs=pl.BlockSpec((tr, n), lambda i, j: (i, 0)))
    return _pc(body, grid_spec=gs, out_shape=jax.ShapeDtypeStruct((r, n), f32),
               compiler_params=_cp(("parallel",)), name=name)(jidx, p, r3, r3, r3)


def _adamw_halves(w, mine, theirs, m, v, cidx, name):
    _, r, n = w.shape
    tr = _row_tile(r)
    c1 = 1.0 / (1.0 - ADAM_B1 ** ADAM_STEP)
    c2 = 1.0 / (1.0 - ADAM_B2 ** ADAM_STEP)

    def body(c_ref, w_ref, a_ref, b_ref, m_ref, v_ref, g_ref, d_ref, mo_ref, vo_ref):
        gv = jnp.where(pl.program_id(0) == c_ref[0], a_ref[...], b_ref[...])
        m2 = ADAM_B1 * m_ref[...] + (1.0 - ADAM_B1) * gv
        v2 = ADAM_B2 * v_ref[...] + (1.0 - ADAM_B2) * (gv * gv)
        g_ref[...] = gv
        d_ref[...] = -ADAM_LR * ((m2 * c1) / (jnp.sqrt(v2 * c2) + ADAM_EPS) + ADAM_WD * w_ref[...])
        mo_ref[...] = m2
        vo_ref[...] = v2

    full = pl.BlockSpec((None, tr, n), lambda l, i, c: (l, i, 0))
    half = pl.BlockSpec((tr, n), lambda l, i, c: (i, 0))
    gs = pltpu.PrefetchScalarGridSpec(num_scalar_prefetch=1, grid=(2, r // tr),
                                      in_specs=[full, half, half, full, full], out_specs=[full] * 4)
    return _pc(body, grid_spec=gs, out_shape=[jax.ShapeDtypeStruct(w.shape, f32)] * 4,
               compiler_params=_cp(("parallel", "parallel")), name=name)(cidx, w, mine, theirs, m, v)


def _adamw(w, g, m, v, name):
    rows, n = w.shape
    tr = _row_tile(rows)
    c1 = 1.0 / (1.0 - ADAM_B1 ** ADAM_STEP)
    c2 = 1.0 / (1.0 - ADAM_B2 ** ADAM_STEP)

    def body(w_ref, g_ref, m_ref, v_ref, d_ref, mo_ref, vo_ref):
        gv = g_ref[...]
        m2 = ADAM_B1 * m_ref[...] + (1.0 - ADAM_B1) * gv
        v2 = ADAM_B2 * v_ref[...] + (1.0 - ADAM_B2) * (gv * gv)
        d_ref[...] = -ADAM_LR * ((m2 * c1) / (jnp.sqrt(v2 * c2) + ADAM_EPS) + ADAM_WD * w_ref[...])
        mo_ref[...] = m2
        vo_ref[...] = v2

    blk = pl.BlockSpec((tr, n), lambda i: (i, 0))
    return _pc(body, grid=(rows // tr,), in_specs=[blk] * 4, out_specs=[blk] * 3,
               out_shape=[jax.ShapeDtypeStruct((rows, n), f32)] * 3,
               compiler_params=_cp(("parallel",)), name=name)(w, g, m, v)


def _full_from_gathered(name, t):
    L, _, r, n = t.shape
    if SHARDED[name] == 1:
        return t.reshape(L, 4 * r, n)
    return t.transpose(0, 2, 1, 3).reshape(L, r, 4 * n)


def _shard_major(name, g):
    L, R, C = g.shape
    if SHARDED[name] == 1:
        return g.reshape(L, 4, R // 4, C)
    return g.reshape(L, R, 4, C // 4).transpose(0, 2, 1, 3)


_SMALL_ROWS = 288


def _pack_small(d):
    flat = jnp.concatenate([d[n].reshape(-1) for n in SMALL])
    total = 2 * 4 * _SMALL_ROWS * LANES
    flat = jnp.concatenate([flat, jnp.zeros((total - flat.shape[0],), f32)])
    return flat.reshape(2, 4, _SMALL_ROWS, LANES)


def _unpack_small(t, like):
    flat = t.reshape(-1)
    out, off = {}, 0
    for n in SMALL:
        sz = math.prod(like[n].shape)
        out[n] = flat[off:off + sz].reshape(like[n].shape)
        off += sz
    return out


def kernel(x, mem, positions, norm_mix_g, w_in, q_norm_g, w_uq, kv_norm_g, w_ukv, ssm_lambda_re, ssm_lambda_im, ssm_log_dt, ssm_b_re, ssm_b_im, ssm_c_re, ssm_c_im, ssm_d, ssm_w_glu, ssm_b_glu, attn_out_g, ssm_out_g, w_out, norm_x_g, mem_norm_g, w_xq, w_xkv, w_xo, norm_ffn_g, w_gate, w_up, w_down, final_norm_g, loss_target, m_norm_mix_g, m_w_in, m_q_norm_g, m_w_uq, m_kv_norm_g, m_w_ukv, m_ssm_lambda_re, m_ssm_lambda_im, m_ssm_log_dt, m_ssm_b_re, m_ssm_b_im, m_ssm_c_re, m_ssm_c_im, m_ssm_d, m_ssm_w_glu, m_ssm_b_glu, m_attn_out_g, m_ssm_out_g, m_w_out, m_norm_x_g, m_mem_norm_g, m_w_xq, m_w_xkv, m_w_xo, m_norm_ffn_g, m_w_gate, m_w_up, m_w_down, m_final_norm_g, v_norm_mix_g, v_w_in, v_q_norm_g, v_w_uq, v_kv_norm_g, v_w_ukv, v_ssm_lambda_re, v_ssm_lambda_im, v_ssm_log_dt, v_ssm_b_re, v_ssm_b_im, v_ssm_c_re, v_ssm_c_im, v_ssm_d, v_ssm_w_glu, v_ssm_b_glu, v_attn_out_g, v_ssm_out_g, v_w_out, v_norm_x_g, v_mem_norm_g, v_w_xq, v_w_xkv, v_w_xo, v_norm_ffn_g, v_w_gate, v_w_up, v_w_down, v_final_norm_g):
    given = dict(locals())
    w = {n: given[n] for n in WEIGHTS}
    m = {n: given["m_" + n] for n in WEIGHTS}
    v = {n: given["v_" + n] for n in WEIGHTS}
    big = list(SHARDED)

    shards = [w[n].astype(bf16) for n in big]
    gathered = _allgather_chips(shards, "allgather_weights")
    W = {n: _full_from_gathered(n, _fill_own(t, s)) for n, t, s in zip(big, gathered, shards)}
    W.update({n: w[n] for n in SMALL})

    loss, dx, grads = _local_step(x[0], mem[0], positions[0], loss_target[0], W)
    loss = lax.psum(loss, ("x", "y", "c"))

    cidx = lax.axis_index("c").astype(jnp.int32).reshape(1)
    jidx = (2 * lax.axis_index("x") + lax.axis_index("y")).astype(jnp.int32).reshape(1)
    names = big + ["small"]
    gs = [_shard_major(n, grads[n]) for n in big] + [_pack_small(grads)]
    r1 = _exchange_halves(gs, "grad_exchange_halves")
    ps = [_add_half(g, r, cidx, f"grad_add_half_{n}") for n, g, r in zip(names, gs, r1)]
    r3 = _scatter_chips(ps, "grad_scatter_chips")
    hs = [_add_chips(p, r, jidx, f"grad_add_chips_{n}") for n, p, r in zip(names, ps, r3)]
    ts = _swap_sibling(hs, "grad_swap_sibling")

    out_g, out_d, out_m, out_v = {}, {}, {}, {}
    for n, h, t in zip(big, hs[:-1], ts[:-1]):
        out_g[n], out_d[n], out_m[n], out_v[n] = _adamw_halves(w[n], h, t, m[n], v[n], cidx, f"adamw_{n}")
    both = jnp.stack([hs[-1], ts[-1]])
    piece = jnp.where(cidx[0] == 0, both, both[::-1])
    gsm = _fill_own(_allgather_chips([piece], "allgather_small")[0], piece)
    rows2 = 2 * 4 * _SMALL_ROWS
    flat = lambda d: _pack_small(d).reshape(rows2, LANES)
    d_, m_, v_ = _adamw(flat(w), gsm.reshape(rows2, LANES), flat(m), flat(v), "adamw_small")
    for dst, t in ((out_g, gsm), (out_d, d_), (out_m, m_), (out_v, v_)):
        dst.update(_unpack_small(t, w))

    return (loss, dx.reshape(x.shape), *[out_g[n] for n in WEIGHTS], *[out_d[n] for n in WEIGHTS],
            *[out_m[n] for n in WEIGHTS], *[out_v[n] for n in WEIGHTS])
```

```python
import functools
import math

import jax
import jax.numpy as jnp
from jax import lax
from jax.experimental import pallas as pl
from jax.experimental.pallas import tpu as pltpu

f32, bf16 = jnp.float32, jnp.bfloat16

D_MODEL = 1024
DEPTH = 2
MLA_HEADS = 8
QK_NOPE = 64
QK_ROPE = 32
V_HEAD = 64
Q_LORA = 256
KV_LORA = 128
MLA_WIDTH = MLA_HEADS * V_HEAD
ROPE_THETA = 10000.0
SSM_WIDTH = 512
SSM_GROUP = 16
SSM_GROUPS = 32
SSM_STATE = 64
IN_WIDTH = Q_LORA + KV_LORA + QK_ROPE + SSM_WIDTH
X_HEADS = 4
X_HEAD_DIM = D_MODEL // X_HEADS
D_FF = 2816
EPS = 1e-6
ADAM_LR, ADAM_B1, ADAM_B2, ADAM_EPS, ADAM_WD, ADAM_STEP = 0.001, 0.9, 0.999, 1e-08, 0.01, 10

LANES = 128
SUBLANES = 8
HEAD_PAD = 128
MLA_PAD = MLA_HEADS * HEAD_PAD
SSM_MACRO = 4
MACRO_CH = SSM_WIDTH // SSM_MACRO
MACRO_ST = SSM_GROUPS // SSM_MACRO * SSM_STATE
VMEM_LIMIT = 56 * 1024 * 1024
GRAD_TRANSIT = bf16

WEIGHTS = ['norm_mix_g', 'w_in', 'q_norm_g', 'w_uq', 'kv_norm_g', 'w_ukv', 'ssm_lambda_re', 'ssm_lambda_im',
           'ssm_log_dt', 'ssm_b_re', 'ssm_b_im', 'ssm_c_re', 'ssm_c_im', 'ssm_d', 'ssm_w_glu', 'ssm_b_glu',
           'attn_out_g', 'ssm_out_g', 'w_out', 'norm_x_g', 'mem_norm_g', 'w_xq', 'w_xkv', 'w_xo', 'norm_ffn_g',
           'w_gate', 'w_up', 'w_down', 'final_norm_g']
SHARDED = {'w_in': 1, 'w_uq': 2, 'w_ukv': 2, 'ssm_w_glu': 1, 'w_out': 1, 'w_xq': 1, 'w_xkv': 2, 'w_xo': 1,
           'w_gate': 2, 'w_up': 2, 'w_down': 1}
SMALL = [n for n in WEIGHTS if n not in SHARDED]
TRANSPOSED = {'w_gate': (1, 2), 'w_up': (1, 2), 'ssm_b_re': (2, 3), 'ssm_b_im': (2, 3)}
MESH = pl.DeviceIdType.MESH


def _pc(body, **kw):
    return pl.pallas_call(body, **kw)


def _pick(n, prefs):
    for p in prefs:
        if n % p == 0:
            return p
    return n


def _cp(sem=None):
    return pltpu.CompilerParams(dimension_semantics=sem, vmem_limit_bytes=VMEM_LIMIT)


_TILE_CANDS = (1024, 1408, 512, 256, 128)
MM_VMEM_BUDGET = 40 * 1024 * 1024


def _mm_tiles(M, K, N, a_bytes, b_bytes, o_bytes, npair, has_res, need_acc):
    best = None
    for tm in _TILE_CANDS:
        for tk in _TILE_CANDS:
            if M % tm or K % tk:
                continue
            vm = npair * (2 * tm * tk * a_bytes + 2 * tk * N * b_bytes) + 2 * tm * N * o_bytes
            vm += tm * N * 4 * (1 + need_acc + 2 * has_res)
            if a_bytes == 4:
                vm += npair * tm * tk * 2
            if b_bytes == 4:
                vm += npair * tk * N * 2
            if vm <= MM_VMEM_BUDGET and (best is None or tm * tk > best[0]):
                best = (tm * tk, tm, tk)
    if best is None:
        return _pick(M, (256, 128)), _pick(K, (256, 128))
    return best[1], best[2]


def _mm(pairs, mode, out_dtype, res=None, stack=None, name="mm"):
    a0, b0 = pairs[0]
    if mode == 'nn':
        (M, K), N = a0.shape, b0.shape[1]
        dims = (((1,), (0,)), ((), ()))
    elif mode == 'nt':
        (M, K), N = a0.shape, b0.shape[0]
        dims = (((1,), (1,)), ((), ()))
    else:
        (K, M), N = a0.shape, b0.shape[1]
        dims = (((0,), (0,)), ((), ()))
    npair = len(pairs)
    has_res = res is not None
    direct = out_dtype == f32
    tm, tk = _mm_tiles(M, K, N, a0.dtype.itemsize, b0.dtype.itemsize, jnp.dtype(out_dtype).itemsize, npair, has_res,
                       not direct)
    nk = K // tk

    def body(*refs):
        ins = refs[:2 * npair]
        res_ref = refs[2 * npair] if has_res else None
        has_prev = stack is not None and stack[1] is not None
        o_ref = refs[2 * npair + has_res + has_prev]
        acc = o_ref if direct else refs[2 * npair + has_res + has_prev + 1]
        k = pl.program_id(1)
        s = None
        for p in range(npair):
            d = lax.dot_general(ins[2 * p][...].astype(bf16), ins[2 * p + 1][...].astype(bf16), dims,
                                preferred_element_type=f32)
            s = d if s is None else s + d

        @pl.when(k == 0)
        def _():
            acc[...] = s

        @pl.when(k > 0)
        def _():
            acc[...] += s

        if has_res or not direct:
            @pl.when(k == nk - 1)
            def _():
                r = acc[...]
                if has_res:
                    r = r + res_ref[...]
                o_ref[...] = r.astype(out_dtype)

    if mode == 'nn':
        a_spec = pl.BlockSpec((tm, tk), lambda i, k: (i, k))
        b_spec = pl.BlockSpec((tk, N), lambda i, k: (k, 0))
    elif mode == 'nt':
        a_spec = pl.BlockSpec((tm, tk), lambda i, k: (i, k))
        b_spec = pl.BlockSpec((N, tk), lambda i, k: (0, k))
    else:
        a_spec = pl.BlockSpec((tk, tm), lambda i, k: (k, i))
        b_spec = pl.BlockSpec((tk, N), lambda i, k: (k, 0))
    o_spec = pl.BlockSpec((tm, N), lambda i, k: (i, 0))
    in_specs = [a_spec, b_spec] * npair + ([o_spec] if has_res else [])
    args = [t for p in pairs for t in p] + ([res] if has_res else [])
    out_shape = jax.ShapeDtypeStruct((M, N), out_dtype)
    aliases = {}
    if stack is not None:
        layer, prev = stack
        o_spec = pl.BlockSpec((None, tm, N), lambda i, k: (layer, i, 0))
        out_shape = jax.ShapeDtypeStruct((DEPTH, M, N), out_dtype)
        if prev is not None:
            in_specs.append(pl.BlockSpec(memory_space=pl.ANY))
            args.append(prev)
            aliases = {len(args) - 1: 0}
    return _pc(body, grid=(M // tm, nk), in_specs=in_specs, out_specs=o_spec, out_shape=out_shape,
               scratch_shapes=[] if direct else [pltpu.VMEM((tm, N), f32)], input_output_aliases=aliases,
               compiler_params=_cp(("parallel", "arbitrary")), name=name)(*args)


def _rms_fwd(x, g, *, col0=0, width=None, n_valid=None, out_dtype=bf16, name="rms_fwd"):
    S = x.shape[0]
    width = width or x.shape[1]
    n_valid = n_valid or width
    ts = _pick(S, (512, 256, 128))
    cb = col0 // width

    def body(x_ref, g_ref, o_ref):
        xv = x_ref[...]
        ms = jnp.sum(xv * xv, axis=-1, keepdims=True) * (1.0 / n_valid)
        o_ref[...] = (xv * lax.rsqrt(ms + EPS) * g_ref[...]).astype(out_dtype)

    return _pc(body, grid=(S // ts,),
               in_specs=[pl.BlockSpec((ts, width), lambda i: (i, cb)), pl.BlockSpec((1, width), lambda i: (0, 0))],
               out_specs=pl.BlockSpec((ts, width), lambda i: (i, 0)),
               out_shape=jax.ShapeDtypeStruct((S, width), out_dtype),
               compiler_params=_cp(("parallel",)), name=name)(x, g.reshape(1, width))


def _rms_bwd(x, g, dy, *, col0=0, dcol0=0, width=None, n_valid=None, res=None, out_dtype=f32, delta=False,
             name="rms_bwd"):
    S = x.shape[0]
    width = width or x.shape[1]
    n_valid = n_valid or width
    ts = _pick(S, (512, 256, 128))
    cb, dcb = col0 // width, dcol0 // width
    has_res = res is not None

    def body(*refs):
        x_ref, g_ref, dy_ref = refs[:3]
        res_ref = refs[3] if has_res else None
        outs = refs[3 + has_res:]
        dx_ref, dg_ref = outs[0], outs[1]
        i = pl.program_id(0)
        xv = x_ref[...]
        gv = g_ref[...]
        dyv = dy_ref[...].astype(f32)
        rstd = lax.rsqrt(jnp.sum(xv * xv, axis=-1, keepdims=True) * (1.0 / n_valid) + EPS)
        xh = xv * rstd
        dxh = dyv * gv
        mean = jnp.sum(dxh * xh, axis=-1, keepdims=True) * (1.0 / n_valid)
        dx = rstd * (dxh - xh * mean)
        if delta:
            d_ref = outs[2]
            for h in range(width // LANES):
                sl = slice(h * LANES, (h + 1) * LANES)
                dsum = jnp.sum(dx[:, sl] * xv[:, sl], axis=-1, keepdims=True)
                d_ref[:, sl] = jnp.broadcast_to(dsum, (ts, LANES))
        if has_res:
            dx = dx + res_ref[...]
        dx_ref[...] = dx.astype(out_dtype)

        @pl.when(i == 0)
        def _():
            dg_ref[...] = jnp.zeros_like(dg_ref)

        dg_ref[...] += jnp.sum(dyv * xh, axis=0, keepdims=True)

    blk = lambda c: pl.BlockSpec((ts, width), lambda i: (i, c))
    in_specs = [blk(cb), pl.BlockSpec((1, width), lambda i: (0, 0)), blk(dcb)] + ([blk(0)] if has_res else [])
    out_specs = [blk(0), pl.BlockSpec((1, width), lambda i: (0, 0))] + ([blk(0)] if delta else [])
    out_shape = [jax.ShapeDtypeStruct((S, width), out_dtype), jax.ShapeDtypeStruct((1, width), f32)] + (
        [jax.ShapeDtypeStruct((S, width), f32)] if delta else [])
    args = [x, g.reshape(1, width), dy] + ([res] if has_res else [])
    return _pc(body, grid=(S // ts,), in_specs=in_specs, out_specs=out_specs, out_shape=out_shape,
               compiler_params=_cp(("arbitrary",)), name=name)(*args)


def _loss_head(h, g, target, name="loss_head"):
    S, D = h.shape
    ts = _pick(S, (512, 256, 128))

    def body(h_ref, g_ref, t_ref, dh_ref, dg_ref, loss_ref):
        i = pl.program_id(0)
        xv = h_ref[...]
        gv = g_ref[...]
        rstd = lax.rsqrt(jnp.sum(xv * xv, axis=-1, keepdims=True) * (1.0 / D) + EPS)
        xh = xv * rstd
        err = xh * gv - t_ref[...]
        dyv = err * (1.0 / D)
        dxh = dyv * gv
        mean = jnp.sum(dxh * xh, axis=-1, keepdims=True) * (1.0 / D)
        dh_ref[...] = rstd * (dxh - xh * mean)

        @pl.when(i == 0)
        def _():
            dg_ref[...] = jnp.zeros_like(dg_ref)
            loss_ref[...] = jnp.zeros_like(loss_ref)

        dg_ref[...] += jnp.sum(dyv * xh, axis=0, keepdims=True)
        part = jnp.sum(jnp.sum(err * err, axis=-1, keepdims=True), axis=0, keepdims=True) * (0.5 / D)
        loss_ref[...] += jnp.broadcast_to(part, (1, LANES))

    blk = pl.BlockSpec((ts, D), lambda i: (i, 0))
    row = pl.BlockSpec((1, D), lambda i: (0, 0))
    return _pc(body, grid=(S // ts,), in_specs=[blk, row, blk],
               out_specs=[blk, row, pl.BlockSpec((1, LANES), lambda i: (0, 0))],
               out_shape=[jax.ShapeDtypeStruct((S, D), f32), jax.ShapeDtypeStruct((1, D), f32),
                          jax.ShapeDtypeStruct((1, LANES), f32)],
               compiler_params=_cp(("arbitrary",)), name=name)(h, g.reshape(1, D), target)


def _rope_apply(x, tc, s1, s2):
    return x * tc + pltpu.roll(x, LANES - 16, 1) * s1 + pltpu.roll(x, 16, 1) * s2


def _rope_apply_t(dy, tc, s1, s2):
    return dy * tc + pltpu.roll(dy * s1, 16, 1) + pltpu.roll(dy * s2, LANES - 16, 1)


def _rope_fwd(q, kv, proj, tabs, name="rope_fwd"):
    S = q.shape[0]
    ts = _pick(S, (512, 256, 128))
    scale = (QK_NOPE + QK_ROPE) ** -0.5

    def body(q_ref, kk_ref, kvv_ref, kr_ref, tc_ref, s1_ref, s2_ref, qh_ref, kh_ref, vh_ref):
        tc, s1, s2 = tc_ref[...], s1_ref[...], s2_ref[...]
        krr = _rope_apply(pltpu.roll(kr_ref[...], QK_NOPE, 1), tc, s1, s2)
        for h in range(MLA_HEADS):
            sl = slice(h * HEAD_PAD, (h + 1) * HEAD_PAD)
            qh_ref[:, sl] = (_rope_apply(q_ref[:, sl], tc, s1, s2) * scale).astype(bf16)
            kh_ref[:, sl] = (kk_ref[:, sl] + krr).astype(bf16)
        vh_ref[...] = kvv_ref[...].astype(bf16)

    wide = lambda c: pl.BlockSpec((ts, MLA_PAD), lambda i: (i, c))
    tab = pl.BlockSpec((ts, LANES), lambda i: (i, 0))
    return _pc(body, grid=(S // ts,),
               in_specs=[wide(0), wide(0), wide(1), pl.BlockSpec((ts, LANES), lambda i: (i, 3)), tab, tab, tab],
               out_specs=[wide(0)] * 3, out_shape=[jax.ShapeDtypeStruct((S, MLA_PAD), bf16)] * 3,
               compiler_params=_cp(("parallel",)), name=name)(q, kv, kv, proj, *tabs)


def _rope_bwd(dqh, dkh, dvh, tabs, name="rope_bwd"):
    S = dqh.shape[0]
    ts = _pick(S, (512, 256, 128))
    scale = (QK_NOPE + QK_ROPE) ** -0.5

    def body(dq_ref, dk_ref, dv_ref, tc_ref, s1_ref, s2_ref, oq_ref, okv_ref, okr_ref):
        tc, s1, s2 = tc_ref[...], s1_ref[...], s2_ref[...]
        ksum = None
        for h in range(MLA_HEADS):
            sl = slice(h * HEAD_PAD, (h + 1) * HEAD_PAD)
            oq_ref[:, sl] = (_rope_apply_t(dq_ref[:, sl], tc, s1, s2) * scale).astype(bf16)
            dk = dk_ref[:, sl]
            okv_ref[:, sl] = dk.astype(bf16)
            ksum = dk if ksum is None else ksum + dk
        okv_ref[:, MLA_PAD:] = dv_ref[...].astype(bf16)
        dkr = pltpu.roll(_rope_apply_t(ksum, tc, s1, s2), LANES - QK_NOPE, 1)
        lane = lax.broadcasted_iota(jnp.int32, (ts, LANES), 1)
        okr_ref[...] = jnp.where(lane < QK_ROPE, dkr, 0.0).astype(bf16)

    wide = pl.BlockSpec((ts, MLA_PAD), lambda i: (i, 0))
    tab = pl.BlockSpec((ts, LANES), lambda i: (i, 0))
    return _pc(body, grid=(S // ts,), in_specs=[wide, wide, wide, tab, tab, tab],
               out_specs=[wide, pl.BlockSpec((ts, 2 * MLA_PAD), lambda i: (i, 0)), tab],
               out_shape=[jax.ShapeDtypeStruct((S, MLA_PAD), bf16), jax.ShapeDtypeStruct((S, 2 * MLA_PAD), bf16),
                          jax.ShapeDtypeStruct((S, LANES), bf16)],
               compiler_params=_cp(("parallel",)), name=name)(dqh, dkh, dvh, *tabs)


ATT_BLK = 1024


def _attn_fwd(qh, kh, vh, name="attn_fwd"):
    S = qh.shape[0]
    tq = tk = min(S, ATT_BLK)
    nq, nk = S // tq, S // tk

    def body(q_ref, k_ref, v_ref, o_ref, lse_ref, m_sc, l_sc, acc_sc):
        i, j = pl.program_id(1), pl.program_id(2)

        @pl.when(j == 0)
        def _():
            m_sc[...] = jnp.full_like(m_sc, -1e30)
            l_sc[...] = jnp.zeros_like(l_sc)
            acc_sc[...] = jnp.zeros_like(acc_sc)

        def step(masked):
            s = lax.dot_general(q_ref[...], k_ref[...], (((1,), (1,)), ((), ())), preferred_element_type=f32)
            if masked:
                row = lax.broadcasted_iota(jnp.int32, (tq, tk), 0)
                col = lax.broadcasted_iota(jnp.int32, (tq, tk), 1)
                s = jnp.where(col <= row, s, -1e30)
            m_prev = m_sc[...]
            m_new = jnp.maximum(m_prev, jnp.max(s, axis=-1, keepdims=True))
            alpha = jnp.exp(m_prev - m_new)
            p = jnp.exp(s - m_new)
            l_sc[...] = alpha * l_sc[...] + jnp.sum(p, axis=-1, keepdims=True)
            acc_sc[...] = alpha * acc_sc[...] + jnp.dot(p.astype(bf16), v_ref[...], preferred_element_type=f32)
            m_sc[...] = m_new

        pl.when(j < i)(functools.partial(step, False))
        pl.when(j == i)(functools.partial(step, True))

        @pl.when(j == nk - 1)
        def _():
            l = l_sc[...]
            o_ref[...] = acc_sc[...] / l
            lse_ref[...] = jnp.broadcast_to(m_sc[...] + jnp.log(l), (tq, LANES))

    qspec = pl.BlockSpec((tq, HEAD_PAD), lambda h, i, j: (i, h))
    kspec = pl.BlockSpec((tk, HEAD_PAD), lambda h, i, j: (jnp.minimum(j, i), h))
    return _pc(body, grid=(MLA_HEADS, nq, nk), in_specs=[qspec, kspec, kspec], out_specs=[qspec, qspec],
               out_shape=[jax.ShapeDtypeStruct((S, MLA_PAD), f32)] * 2,
               scratch_shapes=[pltpu.VMEM((tq, 1), f32), pltpu.VMEM((tq, 1), f32), pltpu.VMEM((tq, HEAD_PAD), f32)],
               compiler_params=_cp(("parallel", "parallel", "arbitrary")), name=name)(qh, kh, vh)


def _attn_bwd(qh, kh, vh, do, lse, delta, name="attn_bwd"):
    S = qh.shape[0]
    tq = tk = min(S, ATT_BLK)
    nq, nk = S // tq, S // tk

    def body(q_ref, k_ref, v_ref, do_ref, lse_ref, dl_ref, dq_ref, dk_ref, dv_ref):
        j, i = pl.program_id(1), pl.program_id(2)

        @pl.when((j == 0) & (i == 0))
        def _():
            dq_ref[...] = jnp.zeros_like(dq_ref)

        @pl.when(i == 0)
        def _():
            dk_ref[...] = jnp.zeros_like(dk_ref)
            dv_ref[...] = jnp.zeros_like(dv_ref)

        def step(masked):
            nt = (((1,), (1,)), ((), ()))
            tn = (((0,), (0,)), ((), ()))
            qv, kv_, dov = q_ref[...], k_ref[...], do_ref[...]
            s = lax.dot_general(qv, kv_, nt, preferred_element_type=f32)
            p = jnp.exp(s - lse_ref[:, :1])
            if masked:
                row = lax.broadcasted_iota(jnp.int32, (tq, tk), 0)
                col = lax.broadcasted_iota(jnp.int32, (tq, tk), 1)
                p = jnp.where(col <= row, p, 0.0)
            dp = lax.dot_general(dov, v_ref[...], nt, preferred_element_type=f32)
            ds = (p * (dp - dl_ref[:, :1])).astype(bf16)
            dv_ref[...] += lax.dot_general(p.astype(bf16), dov, tn, preferred_element_type=f32)
            dk_ref[...] += lax.dot_general(ds, qv, tn, preferred_element_type=f32)
            rows = pl.ds(pl.multiple_of(i * tq, tq), tq)
            dq_ref[rows, :] += jnp.dot(ds, kv_, preferred_element_type=f32)

        pl.when(i > j)(functools.partial(step, False))
        pl.when(i == j)(functools.partial(step, True))

    qspec = pl.BlockSpec((tq, HEAD_PAD), lambda h, j, i: (jnp.maximum(i, j), h))
    kspec = pl.BlockSpec((tk, HEAD_PAD), lambda h, j, i: (j, h))
    colspec = pl.BlockSpec((S, HEAD_PAD), lambda h, j, i: (0, h))
    return _pc(body, grid=(MLA_HEADS, nk, nq), in_specs=[qspec, kspec, kspec, qspec, qspec, qspec],
               out_specs=[colspec, kspec, kspec], out_shape=[jax.ShapeDtypeStruct((S, MLA_PAD), f32)] * 3,
               compiler_params=_cp(("parallel", "arbitrary", "arbitrary")), name=name)(qh, kh, vh, do, lse, delta)


def _xattn_fwd(q, kv, name="xattn_fwd"):
    S = q.shape[0]
    M = kv.shape[0]
    tq = _pick(S, (256, 128))
    scale = X_HEAD_DIM ** -0.5

    def body(q_ref, kv_ref, o_ref):
        for h in range(X_HEADS):
            sl = slice(h * X_HEAD_DIM, (h + 1) * X_HEAD_DIM)
            k = kv_ref[:, sl]
            v = kv_ref[:, D_MODEL + h * X_HEAD_DIM:D_MODEL + (h + 1) * X_HEAD_DIM]
            s = lax.dot_general(q_ref[:, sl], k, (((1,), (1,)), ((), ())), preferred_element_type=f32) * scale
            e = jnp.exp(s - jnp.max(s, axis=-1, keepdims=True))
            p = e / jnp.sum(e, axis=-1, keepdims=True)
            o_ref[:, sl] = jnp.dot(p.astype(bf16), v, preferred_element_type=f32).astype(bf16)

    blk = pl.BlockSpec((tq, D_MODEL), lambda i: (i, 0))
    return _pc(body, grid=(S // tq,), in_specs=[blk, pl.BlockSpec((M, 2 * D_MODEL), lambda i: (0, 0))],
               out_specs=blk, out_shape=jax.ShapeDtypeStruct((S, D_MODEL), bf16),
               compiler_params=_cp(("parallel",)), name=name)(q, kv)


def _xattn_bwd(q, kv, do, name="xattn_bwd"):
    S = q.shape[0]
    M = kv.shape[0]
    tq = _pick(S, (256, 128))
    scale = X_HEAD_DIM ** -0.5

    def body(q_ref, kv_ref, do_ref, dq_ref, dkv_ref):
        i = pl.program_id(0)

        @pl.when(i == 0)
        def _():
            dkv_ref[...] = jnp.zeros_like(dkv_ref)

        nt = (((1,), (1,)), ((), ()))
        tn = (((0,), (0,)), ((), ()))
        for h in range(X_HEADS):
            sl = slice(h * X_HEAD_DIM, (h + 1) * X_HEAD_DIM)
            vsl = slice(D_MODEL + h * X_HEAD_DIM, D_MODEL + (h + 1) * X_HEAD_DIM)
            k, v, qv, dov = kv_ref[:, sl], kv_ref[:, vsl], q_ref[:, sl], do_ref[:, sl]
            s = lax.dot_general(qv, k, nt, preferred_element_type=f32) * scale
            e = jnp.exp(s - jnp.max(s, axis=-1, keepdims=True))
            p = e / jnp.sum(e, axis=-1, keepdims=True)
            dp = lax.dot_general(dov, v, nt, preferred_element_type=f32)
            ds = (p * (dp - jnp.sum(dp * p, axis=-1, keepdims=True)) * scale).astype(bf16)
            dq_ref[:, sl] = jnp.dot(ds, k, preferred_element_type=f32).astype(bf16)
            dkv_ref[:, sl] += lax.dot_general(ds, qv, tn, preferred_element_type=f32)
            dkv_ref[:, vsl] += lax.dot_general(p.astype(bf16), dov, tn, preferred_element_type=f32)

    blk = pl.BlockSpec((tq, D_MODEL), lambda i: (i, 0))
    full = pl.BlockSpec((M, 2 * D_MODEL), lambda i: (0, 0))
    return _pc(body, grid=(S // tq,), in_specs=[blk, full, blk], out_specs=[blk, full],
               out_shape=[jax.ShapeDtypeStruct((S, D_MODEL), bf16), jax.ShapeDtypeStruct((M, 2 * D_MODEL), f32)],
               compiler_params=_cp(("arbitrary",)), name=name)(q, kv, do)


ST_CHUNKS = 1


def _apow_init(a_ref, ap_ref, bp_ref, seg):
    P = MACRO_ST
    ar, ai = a_ref[:, :P], a_ref[:, P:]
    pr, pi = ar, ai
    for r in range(seg):
        ap_ref[r:r + 1, :P] = pr
        ap_ref[r:r + 1, P:] = pi
        if r < seg - 1:
            pr, pi = pr * ar - pi * ai, pr * ai + pi * ar
    br, bi = pr, pi
    for k in range(SUBLANES):
        bp_ref[k:k + 1, :P] = pr
        bp_ref[k:k + 1, P:] = pi
        pr, pi = pr * br - pi * bi, pr * bi + pi * br


def _segment_perm(tS):
    seg = tS // SUBLANES
    rows = jnp.arange(tS)
    src = (rows % SUBLANES) * seg + rows // SUBLANES
    return (src[:, None] == jnp.arange(tS)[None, :]).astype(f32)


def _unpermute_rows(pt, v):
    hi = v.astype(bf16)
    r1 = v - hi.astype(f32)
    mid = r1.astype(bf16)
    lo = (r1 - mid.astype(f32)).astype(bf16)
    out = jnp.dot(pt, jnp.concatenate([hi, mid, lo], axis=1), preferred_element_type=f32)
    w = v.shape[1]
    return (out[:, :w] + out[:, w:2 * w]) + out[:, 2 * w:]


def _scan_block(sc_ref, ap_ref, bp_ref, carry_ref, e_ref, seg, reverse):
    P = MACRO_ST
    sgn = -1.0 if reverse else 1.0
    CH = P // ST_CHUNKS
    rid = lax.broadcasted_iota(jnp.int32, (SUBLANES, CH), 0)
    for c in range(ST_CHUNKS):
        lr, li = slice(c * CH, (c + 1) * CH), slice(P + c * CH, P + (c + 1) * CH)
        ar, ai = ap_ref[0:1, lr], sgn * ap_ref[0:1, li]
        xr = xi = None
        for i in range(seg):
            r = seg - 1 - i if reverse else i
            rows = slice(SUBLANES * r, SUBLANES * (r + 1))
            sr, si = sc_ref[rows, lr], sc_ref[rows, li]
            if i == 0:
                xr, xi = sr, si
            else:
                xr, xi = ar * xr - ai * xi + sr, ar * xi + ai * xr + si
                sc_ref[rows, lr] = xr
                sc_ref[rows, li] = xi
        for sh in (1, 2, 4):
            pr, pi = bp_ref[sh - 1:sh, lr], sgn * bp_ref[sh - 1:sh, li]
            if reverse:
                tr = jnp.where(rid < SUBLANES - sh, pltpu.roll(xr, SUBLANES - sh, 0), 0.0)
                ti = jnp.where(rid < SUBLANES - sh, pltpu.roll(xi, SUBLANES - sh, 0), 0.0)
            else:
                tr = jnp.where(rid >= sh, pltpu.roll(xr, sh, 0), 0.0)
                ti = jnp.where(rid >= sh, pltpu.roll(xi, sh, 0), 0.0)
            xr, xi = xr + pr * tr - pi * ti, xi + pr * ti + pi * tr
        if reverse:
            bpr = jnp.zeros((SUBLANES, CH), f32)
            bpi = jnp.zeros((SUBLANES, CH), f32)
            for r in range(SUBLANES):
                bpr = jnp.where(rid == r, bp_ref[SUBLANES - 1 - r:SUBLANES - r, lr], bpr)
                bpi = jnp.where(rid == r, -bp_ref[SUBLANES - 1 - r:SUBLANES - r, li], bpi)
        else:
            bpr, bpi = bp_ref[:, lr], bp_ref[:, li]
        cr, cim = carry_ref[:, lr], carry_ref[:, li]
        xr, xi = xr + bpr * cr - bpi * cim, xi + bpr * cim + bpi * cr
        edge = 0 if reverse else SUBLANES - 1
        carry_ref[:, lr] = jnp.sum(jnp.where(rid == edge, xr, 0.0), axis=0, keepdims=True)
        carry_ref[:, li] = jnp.sum(jnp.where(rid == edge, xi, 0.0), axis=0, keepdims=True)
        if reverse:
            er = jnp.where(rid == SUBLANES - 1, cr, pltpu.roll(xr, SUBLANES - 1, 0))
            ei = jnp.where(rid == SUBLANES - 1, cim, pltpu.roll(xi, SUBLANES - 1, 0))
        else:
            er = jnp.where(rid == 0, cr, pltpu.roll(xr, 1, 0))
            ei = jnp.where(rid == 0, cim, pltpu.roll(xi, 1, 0))
        if e_ref is not None:
            e_ref[:, lr] = er
            e_ref[:, li] = ei
        for i in range(seg):
            r = seg - 1 - i if reverse else i
            rows = slice(SUBLANES * r, SUBLANES * (r + 1))
            pr, pi = ap_ref[i:i + 1, lr], sgn * ap_ref[i:i + 1, li]
            sc_ref[rows, lr] += pr * er - pi * ei
            sc_ref[rows, li] += pr * ei + pi * er


def _ssm_fwd(proj, bm, cm, a, d, name="ssm_fwd"):
    S = proj.shape[0]
    tS = _pick(S, (256, 128))
    nb = S // tS
    P2 = 2 * MACRO_ST
    seg = tS // SUBLANES
    ucol0 = (D_MODEL - SSM_WIDTH) // MACRO_CH

    perm = _segment_perm(tS)

    def body(u_ref, b_ref, c_ref, a_ref, d_ref, pm_ref, pt_ref, y_ref, xc_ref, bu_sc, ap_sc, bp_sc, car_sc):
        t = pl.program_id(1)

        @pl.when(t == 0)
        def _():
            _apow_init(a_ref, ap_sc, bp_sc, seg)
            car_sc[...] = jnp.zeros_like(car_sc)

        uv = u_ref[...]
        up = jnp.dot(pm_ref[...], uv.astype(bf16), preferred_element_type=f32).astype(bf16)
        bu_sc[...] = jnp.dot(up, b_ref[...], preferred_element_type=f32)
        xc_ref[...] = car_sc[...]
        _scan_block(bu_sc, ap_sc, bp_sc, car_sc, None, seg, False)
        yp = jnp.dot(bu_sc[...].astype(bf16), c_ref[...], preferred_element_type=f32)
        y_ref[...] = _unpermute_rows(pt_ref[...], yp) + d_ref[...] * uv

    sq = pl.BlockSpec((tS, tS), lambda m, t: (0, 0))
    return _pc(body, grid=(SSM_MACRO, nb),
               in_specs=[pl.BlockSpec((tS, MACRO_CH), lambda m, t: (t, ucol0 + m)),
                         pl.BlockSpec((None, MACRO_CH, P2), lambda m, t: (m, 0, 0)),
                         pl.BlockSpec((None, P2, MACRO_CH), lambda m, t: (m, 0, 0)),
                         pl.BlockSpec((None, 1, P2), lambda m, t: (m, 0, 0)),
                         pl.BlockSpec((1, MACRO_CH), lambda m, t: (0, m)), sq, sq],
               out_specs=[pl.BlockSpec((tS, MACRO_CH), lambda m, t: (t, m)),
                          pl.BlockSpec((None, None, 1, P2), lambda m, t: (m, t, 0, 0))],
               out_shape=[jax.ShapeDtypeStruct((S, SSM_WIDTH), f32), jax.ShapeDtypeStruct((SSM_MACRO, nb, 1, P2), f32)],
               scratch_shapes=[pltpu.VMEM((tS, P2), f32), pltpu.VMEM((seg, P2), f32),
                               pltpu.VMEM((SUBLANES, P2), f32), pltpu.VMEM((1, P2), f32)],
               compiler_params=_cp(("arbitrary", "arbitrary")), name=name)(
        proj, bm, cm, a, d.reshape(1, SSM_WIDTH), perm.astype(bf16), perm.T.astype(bf16))


def _ssm_bwd(proj, dy, xc, bm, cm, a, d, name="ssm_bwd"):
    S = proj.shape[0]
    tS = _pick(S, (256, 128))
    nb = S // tS
    P = MACRO_ST
    P2 = 2 * P
    seg = tS // SUBLANES
    ucol0 = (D_MODEL - SSM_WIDTH) // MACRO_CH

    perm = _segment_perm(tS)

    def body(u_ref, dy_ref, xc_ref, b_ref, c_ref, a_ref, d_ref, pm_ref, pt_ref, du_ref, db_ref, dc_ref, da_ref, dd_ref,
             x_sc, g_sc, ap_sc, bp_sc, e_sc, xcar_sc, gcar_sc):
        t = pl.program_id(1)

        @pl.when(t == 0)
        def _():
            _apow_init(a_ref, ap_sc, bp_sc, seg)
            gcar_sc[...] = jnp.zeros_like(gcar_sc)
            db_ref[...] = jnp.zeros_like(db_ref)
            dc_ref[...] = jnp.zeros_like(dc_ref)
            da_ref[...] = jnp.zeros_like(da_ref)
            dd_ref[...] = jnp.zeros_like(dd_ref)

        nt = (((1,), (1,)), ((), ()))
        tn = (((0,), (0,)), ((), ()))
        uv = u_ref[...]
        dyv = dy_ref[...]
        pm = pm_ref[...]
        ub = jnp.dot(pm, uv.astype(bf16), preferred_element_type=f32).astype(bf16)
        dyb = jnp.dot(pm, dyv.astype(bf16), preferred_element_type=f32).astype(bf16)
        x_sc[...] = jnp.dot(ub, b_ref[...], preferred_element_type=f32)
        xcar_sc[...] = xc_ref[...]
        _scan_block(x_sc, ap_sc, bp_sc, xcar_sc, e_sc, seg, False)
        g_sc[...] = lax.dot_general(dyb, c_ref[...], nt, preferred_element_type=f32)
        _scan_block(g_sc, ap_sc, bp_sc, gcar_sc, None, seg, True)
        xv = x_sc[...]
        gv = g_sc[...]
        gb = gv.astype(bf16)
        dc_ref[...] += lax.dot_general(xv.astype(bf16), dyb, tn, preferred_element_type=f32)
        db_ref[...] += lax.dot_general(ub, gb, tn, preferred_element_type=f32)
        dup = lax.dot_general(gb, b_ref[...], nt, preferred_element_type=f32)
        du_ref[...] = _unpermute_rows(pt_ref[...], dup) + d_ref[...] * dyv
        dd_ref[...] += jnp.sum(dyv * uv, axis=0, keepdims=True)
        xp = jnp.concatenate([e_sc[...], xv[:tS - SUBLANES]], axis=0)
        xpr, xpi, ggr, ggi = xp[:, :P], xp[:, P:], gv[:, :P], gv[:, P:]
        da_ref[:, :P] += jnp.sum(ggr * xpr + ggi * xpi, axis=0, keepdims=True)
        da_ref[:, P:] += jnp.sum(ggi * xpr - ggr * xpi, axis=0, keepdims=True)

    rev = lambda t: nb - 1 - t
    return _pc(body, grid=(SSM_MACRO, nb),
               in_specs=[pl.BlockSpec((tS, MACRO_CH), lambda m, t: (rev(t), ucol0 + m)),
                         pl.BlockSpec((tS, MACRO_CH), lambda m, t: (rev(t), m)),
                         pl.BlockSpec((None, None, 1, P2), lambda m, t: (m, rev(t), 0, 0)),
                         pl.BlockSpec((None, MACRO_CH, P2), lambda m, t: (m, 0, 0)),
                         pl.BlockSpec((None, P2, MACRO_CH), lambda m, t: (m, 0, 0)),
                         pl.BlockSpec((None, 1, P2), lambda m, t: (m, 0, 0)),
                         pl.BlockSpec((1, MACRO_CH), lambda m, t: (0, m)),
                         pl.BlockSpec((tS, tS), lambda m, t: (0, 0)), pl.BlockSpec((tS, tS), lambda m, t: (0, 0))],
               out_specs=[pl.BlockSpec((tS, MACRO_CH), lambda m, t: (rev(t), m)),
                          pl.BlockSpec((None, MACRO_CH, P2), lambda m, t: (m, 0, 0)),
                          pl.BlockSpec((None, P2, MACRO_CH), lambda m, t: (m, 0, 0)),
                          pl.BlockSpec((None, 1, P2), lambda m, t: (m, 0, 0)),
                          pl.BlockSpec((1, MACRO_CH), lambda m, t: (0, m))],
               out_shape=[jax.ShapeDtypeStruct((S, SSM_WIDTH), f32),
                          jax.ShapeDtypeStruct((SSM_MACRO, MACRO_CH, P2), f32),
                          jax.ShapeDtypeStruct((SSM_MACRO, P2, MACRO_CH), f32),
                          jax.ShapeDtypeStruct((SSM_MACRO, 1, P2), f32),
                          jax.ShapeDtypeStruct((1, SSM_WIDTH), f32)],
               scratch_shapes=[pltpu.VMEM((tS, P2), f32), pltpu.VMEM((tS, P2), f32),
                               pltpu.VMEM((seg, P2), f32), pltpu.VMEM((SUBLANES, P2), f32), pltpu.VMEM((SUBLANES, P2), f32),
                               pltpu.VMEM((1, P2), f32), pltpu.VMEM((1, P2), f32)],
               compiler_params=_cp(("arbitrary", "arbitrary")), name=name)(
        proj, dy, xc, bm, cm, a, d.reshape(1, SSM_WIDTH), perm.astype(bf16), perm.T.astype(bf16))


_GELU_K = math.sqrt(2.0 / math.pi)
_GELU_C = 0.044715


def _glu_fwd(y, w, b, g, name="glu_fwd"):
    S, W = y.shape
    ts = _pick(S, (512, 256, 128))

    def body(y_ref, w_ref, b_ref, g_ref, z_ref, sn_ref, ge_ref):
        yv = y_ref[...]
        cdf = 0.5 * (1.0 + jnp.tanh(_GELU_K * (yv + _GELU_C * (yv * yv * yv))))
        ge = (yv * cdf).astype(bf16)
        z = jnp.dot(ge, w_ref[...], preferred_element_type=f32) + b_ref[...]
        s = yv * jax.nn.sigmoid(z)
        rstd = lax.rsqrt(jnp.sum(s * s, axis=-1, keepdims=True) * (1.0 / W) + EPS)
        z_ref[...] = z
        sn_ref[...] = (s * rstd * g_ref[...]).astype(bf16)
        ge_ref[...] = ge

    blk = pl.BlockSpec((ts, W), lambda i: (i, 0))
    row = pl.BlockSpec((1, W), lambda i: (0, 0))
    return _pc(body, grid=(S // ts,), in_specs=[blk, pl.BlockSpec((W, W), lambda i: (0, 0)), row, row],
               out_specs=[blk, blk, blk],
               out_shape=[jax.ShapeDtypeStruct((S, W), f32), jax.ShapeDtypeStruct((S, W), bf16),
                          jax.ShapeDtypeStruct((S, W), bf16)],
               compiler_params=_cp(("parallel",)), name=name)(y, w, b.reshape(1, W), g.reshape(1, W))


def _glu_bwd(y, z, dmixed, w, g, name="glu_bwd"):
    S, W = y.shape
    ts = _pick(S, (512, 256, 128))
    dcb = MLA_PAD // W

    def body(y_ref, z_ref, dsn_ref, w_ref, g_ref, dy_ref, dz_ref, dg_ref, db_ref):
        i = pl.program_id(0)
        yv, zv, gv = y_ref[...], z_ref[...], g_ref[...]
        sig = jax.nn.sigmoid(zv)
        s = yv * sig
        rstd = lax.rsqrt(jnp.sum(s * s, axis=-1, keepdims=True) * (1.0 / W) + EPS)
        sh = s * rstd
        dsn = dsn_ref[...]
        dsh = dsn * gv
        ds = rstd * (dsh - sh * (jnp.sum(dsh * sh, axis=-1, keepdims=True) * (1.0 / W)))
        dz = ds * s * (1.0 - sig)
        dzb = dz.astype(bf16)
        dge = lax.dot_general(dzb, w_ref[...], (((1,), (1,)), ((), ())), preferred_element_type=f32)
        t = jnp.tanh(_GELU_K * (yv + _GELU_C * (yv * yv * yv)))
        dgelu = 0.5 * (1.0 + t) + 0.5 * yv * (1.0 - t * t) * _GELU_K * (1.0 + 3.0 * _GELU_C * yv * yv)
        dy_ref[...] = ds * sig + dge * dgelu
        dz_ref[...] = dzb

        @pl.when(i == 0)
        def _():
            dg_ref[...] = jnp.zeros_like(dg_ref)
            db_ref[...] = jnp.zeros_like(db_ref)

        dg_ref[...] += jnp.sum(dsn * sh, axis=0, keepdims=True)
        db_ref[...] += jnp.sum(dz, axis=0, keepdims=True)

    blk = pl.BlockSpec((ts, W), lambda i: (i, 0))
    row = pl.BlockSpec((1, W), lambda i: (0, 0))
    return _pc(body, grid=(S // ts,),
               in_specs=[blk, blk, pl.BlockSpec((ts, W), lambda i: (i, dcb)), pl.BlockSpec((W, W), lambda i: (0, 0)), row],
               out_specs=[blk, blk, row, row],
               out_shape=[jax.ShapeDtypeStruct((S, W), f32), jax.ShapeDtypeStruct((S, W), bf16),
                          jax.ShapeDtypeStruct((1, W), f32), jax.ShapeDtypeStruct((1, W), f32)],
               compiler_params=_cp(("arbitrary",)), name=name)(y, z, dmixed, w, g.reshape(1, W))


def _ffn_up(hn, wg, wu, name="ffn_up"):
    S, K = hn.shape
    F = wg.shape[0]
    tm, tn = _pick(S, (512, 256, 128)), _pick(F, (1408, 256, 128))

    def body(h_ref, wg_ref, wu_ref, g_ref, u_ref, a_ref):
        hv = h_ref[...]
        nt = (((1,), (1,)), ((), ()))
        gv = lax.dot_general(hv, wg_ref[...], nt, preferred_element_type=f32)
        uv = lax.dot_general(hv, wu_ref[...], nt, preferred_element_type=f32)
        g_ref[...] = gv
        u_ref[...] = uv
        a_ref[...] = (gv * jax.nn.sigmoid(gv) * uv).astype(bf16)

    wspec = pl.BlockSpec((tn, K), lambda i, j: (j, 0))
    ospec = pl.BlockSpec((tm, tn), lambda i, j: (i, j))
    return _pc(body, grid=(S // tm, F // tn), in_specs=[pl.BlockSpec((tm, K), lambda i, j: (i, 0)), wspec, wspec],
               out_specs=[ospec] * 3,
               out_shape=[jax.ShapeDtypeStruct((S, F), f32), jax.ShapeDtypeStruct((S, F), f32),
                          jax.ShapeDtypeStruct((S, F), bf16)],
               compiler_params=_cp(("parallel", "parallel")), name=name)(hn, wg, wu)


def _ffn_bwd_act(dh, wd, gate, up, name="ffn_bwd_act"):
    S, K = dh.shape
    F = wd.shape[0]
    tm, tn = _pick(S, (512, 256, 128)), _pick(F, (1408, 256, 128))

    def body(dh_ref, wd_ref, g_ref, u_ref, dg_ref, du_ref):
        dact = lax.dot_general(dh_ref[...].astype(bf16), wd_ref[...], (((1,), (1,)), ((), ())),
                               preferred_element_type=f32)
        gv, uv = g_ref[...], u_ref[...]
        sig = jax.nn.sigmoid(gv)
        dg_ref[...] = (dact * uv * (sig * (1.0 + gv * (1.0 - sig)))).astype(bf16)
        du_ref[...] = (dact * (gv * sig)).astype(bf16)

    ospec = pl.BlockSpec((tm, tn), lambda i, j: (i, j))
    return _pc(body, grid=(S // tm, F // tn),
               in_specs=[pl.BlockSpec((tm, K), lambda i, j: (i, 0)), pl.BlockSpec((tn, K), lambda i, j: (j, 0)),
                         ospec, ospec],
               out_specs=[ospec] * 2, out_shape=[jax.ShapeDtypeStruct((S, F), bf16)] * 2,
               compiler_params=_cp(("parallel", "parallel")), name=name)(dh, wd, gate, up)


def _pad_heads(w, per_head, pieces):
    K = w.shape[0]
    w3 = w.reshape(K, MLA_HEADS, per_head)
    out = jnp.zeros((K, MLA_HEADS, HEAD_PAD), w.dtype)
    for s0, s1, d0 in pieces:
        out = out.at[:, :, d0:d0 + (s1 - s0)].set(w3[:, :, s0:s1])
    return out.reshape(K, MLA_PAD)


def _unpad_heads(wp, per_head, pieces):
    K = wp.shape[0]
    w3 = wp.reshape(K, MLA_HEADS, HEAD_PAD)
    out = jnp.zeros((K, MLA_HEADS, per_head), wp.dtype)
    for s0, s1, d0 in pieces:
        out = out.at[:, :, s0:s1].set(w3[:, :, d0:d0 + (s1 - s0)])
    return out.reshape(K, MLA_HEADS * per_head)


_Q_PIECES = [(0, QK_NOPE + QK_ROPE, 0)]
_K_PIECES = [(0, QK_NOPE, 0)]
_V_PIECES = [(QK_NOPE, QK_NOPE + V_HEAD, 0)]
_KR0 = Q_LORA + KV_LORA


def _pack_win(w):
    z = jnp.zeros((w.shape[0], LANES - QK_ROPE), w.dtype)
    return jnp.concatenate([w[:, :_KR0 + QK_ROPE], z, w[:, _KR0 + QK_ROPE:]], axis=1)


def _unpack_win(wp):
    return jnp.concatenate([wp[:, :_KR0 + QK_ROPE], wp[:, _KR0 + LANES:]], axis=1)


def _pack_wout(w):
    wa = w[:MLA_WIDTH].reshape(MLA_HEADS, V_HEAD, D_MODEL)
    wa = jnp.concatenate([wa, jnp.zeros_like(wa)], axis=1).reshape(MLA_PAD, D_MODEL)
    return jnp.concatenate([wa, w[MLA_WIDTH:]], axis=0)


def _unpack_wout(wp):
    wa = wp[:MLA_PAD].reshape(MLA_HEADS, HEAD_PAD, D_MODEL)[:, :V_HEAD].reshape(MLA_WIDTH, D_MODEL)
    return jnp.concatenate([wa, wp[MLA_PAD:]], axis=0)


def _pad_gain(g):
    g2 = g.reshape(MLA_HEADS, V_HEAD)
    return jnp.concatenate([g2, jnp.zeros_like(g2)], axis=1).reshape(MLA_PAD)


def _unpad_gain(gp):
    return gp.reshape(MLA_HEADS, HEAD_PAD)[:, :V_HEAD].reshape(MLA_WIDTH)


def _ssm_prep(lam_re, lam_im, log_dt, b_re, b_im, c_re, c_im):
    lam = lax.complex(lam_re, lam_im)
    dt = jnp.exp(log_dt)[:, None]
    a_bar = jnp.exp(lam * dt)
    b_bar = ((a_bar - 1.0) / lam)[:, None, :] * lax.complex(b_re, b_im)
    G8 = SSM_GROUPS // SSM_MACRO
    eye = jnp.eye(G8, dtype=f32)

    def bmat(part):
        p4 = part.reshape(SSM_MACRO, G8, SSM_GROUP, SSM_STATE)
        return jnp.einsum('mgcp,gh->mgchp', p4, eye).reshape(SSM_MACRO, MACRO_CH, MACRO_ST)

    def cmat(part):
        p4 = part.reshape(SSM_MACRO, G8, SSM_GROUP, SSM_STATE)
        return jnp.einsum('mgcp,gh->mgphc', p4, eye).reshape(SSM_MACRO, MACRO_ST, MACRO_CH)

    bm = jnp.concatenate([bmat(b_bar.real), bmat(b_bar.imag)], axis=2)
    cm = jnp.concatenate([cmat(c_re), -cmat(c_im)], axis=1)
    a4 = a_bar.reshape(SSM_MACRO, 1, MACRO_ST)
    a = jnp.concatenate([a4.real, a4.imag], axis=2)
    return bm, cm, a


def _rope_tables(positions):
    freqs = ROPE_THETA ** (-jnp.arange(0, QK_ROPE, 2, dtype=f32) / QK_ROPE)
    ang = positions.astype(f32)[:, None] * freqs
    cos, sin = jnp.cos(ang), jnp.sin(ang)
    S = positions.shape[0]
    half = QK_ROPE // 2
    one, zero = jnp.ones((S, QK_NOPE), f32), jnp.zeros((S, half), f32)
    z64, z32 = jnp.zeros((S, QK_NOPE), f32), jnp.zeros((S, LANES - QK_NOPE - QK_ROPE), f32)
    tc = jnp.concatenate([one, cos, cos, z32], axis=1)
    s1 = jnp.concatenate([z64, -sin, zero, z32], axis=1)
    s2 = jnp.concatenate([z64, zero, sin, z32], axis=1)
    return tc, s1, s2


def _layer_params(W, l):
    p = {}
    p['win'] = _pack_win(W['w_in'][l])
    p['wuq'] = _pad_heads(W['w_uq'][l], QK_NOPE + QK_ROPE, _Q_PIECES)
    wukv = W['w_ukv'][l]
    p['wukv'] = jnp.concatenate([_pad_heads(wukv, QK_NOPE + V_HEAD, _K_PIECES),
                                 _pad_heads(wukv, QK_NOPE + V_HEAD, _V_PIECES)], axis=1)
    p['wout'] = _pack_wout(W['w_out'][l])
    p['attn_g'] = _pad_gain(W['attn_out_g'][l])
    return p


def _forward_layer(h, memn_in, tabs, W, l, name):
    p = _layer_params(W, l)
    sv = {'h0': h, 'p': p}
    xn = _rms_fwd(h, W['norm_mix_g'][l], name=name + "rms_mix")
    proj = _mm([(xn, p['win'])], 'nn', f32, name=name + "mm_in")
    cqn = _rms_fwd(proj, W['q_norm_g'][l], col0=0, width=Q_LORA, name=name + "rms_q")
    ckvn = _rms_fwd(proj, W['kv_norm_g'][l], col0=Q_LORA, width=KV_LORA, name=name + "rms_kv")
    q = _mm([(cqn, p['wuq'])], 'nn', f32, name=name + "mm_uq")
    kv = _mm([(ckvn, p['wukv'])], 'nn', f32, name=name + "mm_ukv")
    qh, kh, vh = _rope_fwd(q, kv, proj, tabs, name=name + "rope")
    oh, lse = _attn_fwd(qh, kh, vh, name=name + "attn")
    an = _rms_fwd(oh, p['attn_g'], n_valid=MLA_WIDTH, name=name + "rms_attn")
    bm, cm, a = W['ssm'][l]
    bmb, cmb = bm.astype(bf16), cm.astype(bf16)
    y, xc = _ssm_fwd(proj, bmb, cmb, a, W['ssm_d'][l], name=name + "ssm")
    z, sn, ge = _glu_fwd(y, W['ssm_w_glu'][l], W['ssm_b_glu'][l], W['ssm_out_g'][l], name=name + "glu")
    h1a = _mm([(an, p['wout'][:MLA_PAD])], 'nn', f32, res=h, name=name + "mm_out_a")
    h1 = _mm([(sn, p['wout'][MLA_PAD:])], 'nn', f32, res=h1a, name=name + "mm_out_s")
    hn2 = _rms_fwd(h1, W['norm_x_g'][l], name=name + "rms_x")
    memn = _rms_fwd(memn_in, W['mem_norm_g'][l], name=name + "rms_mem")
    qx = _mm([(hn2, W['w_xq'][l])], 'nn', bf16, name=name + "mm_xq")
    kvx = _mm([(memn, W['w_xkv'][l])], 'nn', bf16, name=name + "mm_xkv")
    ox = _xattn_fwd(qx, kvx, name=name + "xattn")
    h2 = _mm([(ox, W['w_xo'][l])], 'nn', f32, res=h1, name=name + "mm_xo")
    hn3 = _rms_fwd(h2, W['norm_ffn_g'][l], name=name + "rms_ffn")
    gate, up, act = _ffn_up(hn3, W['w_gate'][l], W['w_up'][l], name=name + "ffn_up")
    h3 = _mm([(act, W['w_down'][l])], 'nn', f32, res=h2, name=name + "mm_down")
    sv.update(xn=xn, proj=proj, cqn=cqn, ckvn=ckvn, qh=qh, kh=kh, vh=vh, oh=oh, lse=lse, an=an, bmb=bmb, cmb=cmb,
              a=a, y=y, xc=xc, z=z, sn=sn, ge=ge, h1=h1, hn2=hn2, memn=memn, qx=qx, kvx=kvx, ox=ox, h2=h2, hn3=hn3,
              gate=gate, up=up, act=act)
    return h3, sv


STACKED = ('w_down', 'w_gate', 'w_up', 'w_xo', 'w_xq', 'w_xkv', 'ssm_w_glu')


def _backward_layer(dh3, sv, memn_in, tabs, W, l, prev, name):
    p = sv['p']
    G = {}
    st = lambda n: (l, prev.get(n))
    G['w_down'] = _mm([(sv['act'], dh3)], 'tn', f32, stack=st('w_down'), name=name + "dw_down")
    dgate, dup = _ffn_bwd_act(dh3, W['w_down'][l], sv['gate'], sv['up'], name=name + "ffn_bwd_act")
    dhn3 = _mm([(dgate, W['w_gate'][l]), (dup, W['w_up'][l])], 'nn', f32, name=name + "mm_dffn")
    G['w_gate'] = _mm([(dgate, sv['hn3'])], 'tn', f32, stack=st('w_gate'), name=name + "dw_gate")
    G['w_up'] = _mm([(dup, sv['hn3'])], 'tn', f32, stack=st('w_up'), name=name + "dw_up")
    dh2, dg = _rms_bwd(sv['h2'], W['norm_ffn_g'][l], dhn3, res=dh3, name=name + "rmsb_ffn")
    G['norm_ffn_g'] = dg[0]
    G['w_xo'] = _mm([(sv['ox'], dh2)], 'tn', f32, stack=st('w_xo'), name=name + "dw_xo")
    dox = _mm([(dh2, W['w_xo'][l])], 'nt', bf16, name=name + "mm_dxo")
    dqx, dkvx = _xattn_bwd(sv['qx'], sv['kvx'], dox, name=name + "xattn_bwd")
    G['w_xq'] = _mm([(sv['hn2'], dqx)], 'tn', f32, stack=st('w_xq'), name=name + "dw_xq")
    G['w_xkv'] = _mm([(sv['memn'], dkvx)], 'tn', f32, stack=st('w_xkv'), name=name + "dw_xkv")
    dhn2 = _mm([(dqx, W['w_xq'][l])], 'nt', f32, name=name + "mm_dxq")
    dmemn = _mm([(dkvx, W['w_xkv'][l])], 'nt', f32, name=name + "mm_dxkv")
    dh1, dg = _rms_bwd(sv['h1'], W['norm_x_g'][l], dhn2, res=dh2, name=name + "rmsb_x")
    G['norm_x_g'] = dg[0]
    _, dg = _rms_bwd(memn_in, W['mem_norm_g'][l], dmemn, name=name + "rmsb_mem")
    G['mem_norm_g'] = dg[0]
    dwo_a = _mm([(sv['an'], dh1)], 'tn', f32, name=name + "dw_out_a")
    dwo_s = _mm([(sv['sn'], dh1)], 'tn', f32, name=name + "dw_out_s")
    G['w_out'] = _unpack_wout(jnp.concatenate([dwo_a, dwo_s], axis=0))
    dmixed = _mm([(dh1, p['wout'])], 'nt', f32, name=name + "mm_dout")
    dy, dz, dg, db = _glu_bwd(sv['y'], sv['z'], dmixed, W['ssm_w_glu'][l], W['ssm_out_g'][l], name=name + "glu_bwd")
    G['ssm_out_g'], G['ssm_b_glu'] = dg[0], db[0]
    G['ssm_w_glu'] = _mm([(sv['ge'], dz)], 'tn', f32, stack=st('ssm_w_glu'), name=name + "dw_glu")
    du, dbm, dcm, da, dd = _ssm_bwd(sv['proj'], dy, sv['xc'], sv['bmb'], sv['cmb'], sv['a'], W['ssm_d'][l],
                                    name=name + "ssm_bwd")
    G['ssm_d'] = dd[0]
    G['ssm_raw'] = (dbm, dcm, da)
    doh, dg, delta = _rms_bwd(sv['oh'], p['attn_g'], dmixed, width=MLA_PAD, n_valid=MLA_WIDTH, delta=True,
                              out_dtype=bf16, name=name + "rmsb_attn")
    G['attn_out_g'] = _unpad_gain(dg[0])
    dqh, dkh, dvh = _attn_bwd(sv['qh'], sv['kh'], sv['vh'], doh, sv['lse'], delta, name=name + "attn_bwd")
    dq, dkv, dkr = _rope_bwd(dqh, dkh, dvh, tabs, name=name + "rope_bwd")
    G['w_uq'] = _unpad_heads(_mm([(sv['cqn'], dq)], 'tn', f32, name=name + "dw_uq"), QK_NOPE + QK_ROPE, _Q_PIECES)
    dwukv = _mm([(sv['ckvn'], dkv)], 'tn', f32, name=name + "dw_ukv")
    G['w_ukv'] = (_unpad_heads(dwukv[:, :MLA_PAD], QK_NOPE + V_HEAD, _K_PIECES)
                  + _unpad_heads(dwukv[:, MLA_PAD:], QK_NOPE + V_HEAD, _V_PIECES))
    dcqn = _mm([(dq, p['wuq'])], 'nt', f32, name=name + "mm_duq")
    dckvn = _mm([(dkv, p['wukv'])], 'nt', f32, name=name + "mm_dukv")
    dcq, dg = _rms_bwd(sv['proj'], W['q_norm_g'][l], dcqn, col0=0, width=Q_LORA, out_dtype=bf16, name=name + "rmsb_q")
    G['q_norm_g'] = dg[0]
    dckv, dg = _rms_bwd(sv['proj'], W['kv_norm_g'][l], dckvn, col0=Q_LORA, width=KV_LORA, out_dtype=bf16,
                        name=name + "rmsb_kv")
    G['kv_norm_g'] = dg[0]
    dproj = jnp.concatenate([dcq, dckv, dkr, du.astype(bf16)], axis=1)
    G['w_in'] = _unpack_win(_mm([(sv['xn'], dproj)], 'tn', f32, name=name + "dw_in"))
    dxn = _mm([(dproj, p['win'])], 'nt', f32, name=name + "mm_din")
    dh0, dg = _rms_bwd(sv['h0'], W['norm_mix_g'][l], dxn, res=dh1, name=name + "rmsb_mix")
    G['norm_mix_g'] = dg[0]
    return dh0, G


def _local_step(x, mem, positions, target, W):
    tabs = _rope_tables(positions)
    ssm_in = [(W['ssm_lambda_re'][l], W['ssm_lambda_im'][l], W['ssm_log_dt'][l], W['ssm_b_re'][l], W['ssm_b_im'][l],
               W['ssm_c_re'][l], W['ssm_c_im'][l]) for l in range(DEPTH)]
    preps = [jax.vjp(_ssm_prep, *ssm_in[l]) for l in range(DEPTH)]
    W = dict(W)
    W['ssm'] = [preps[l][0] for l in range(DEPTH)]
    h = x
    saved = []
    for l in range(DEPTH):
        h, sv = _forward_layer(h, mem, tabs, W, l, f"l{l}_")
        saved.append(sv)
    dh, dgf, loss = _loss_head(h, W['final_norm_g'], target)
    grads = [None] * DEPTH
    prev = {}
    for l in reversed(range(DEPTH)):
        dh, G = _backward_layer(dh, saved[l], mem, tabs, W, l, prev, f"l{l}b_")
        prev = {n: G[n] for n in STACKED}
        dbm, dcm, da = G.pop('ssm_raw')
        names = ['ssm_lambda_re', 'ssm_lambda_im', 'ssm_log_dt', 'ssm_b_re', 'ssm_b_im', 'ssm_c_re', 'ssm_c_im']
        for n, g in zip(names, preps[l][1]((dbm, dcm, da))):
            G[n] = g
        grads[l] = G
    out = {n: grads[0][n] if n in STACKED else jnp.stack([grads[l][n] for l in range(DEPTH)]) for n in grads[0]}
    out['final_norm_g'] = dgf[0]
    return loss[0, 0], dh, out


_HBM = pl.BlockSpec(memory_space=pltpu.HBM)


def _me():
    return lax.axis_index("x"), lax.axis_index("y"), lax.axis_index("c")


def _chip_peers(x, y, c):
    devs = [(1 - x, y, c), (x, 1 - y, c), (1 - x, 1 - y, c)]
    return devs, [2 * d[0] + d[1] for d in devs]


def _allgather_chips(xs, name):
    n = len(xs)

    def body(*refs):
        src, dst = refs[:n], refs[n:2 * n]
        send, recv = refs[2 * n:]
        x, y, c = _me()
        jme = 2 * x + y
        sib = (x, y, 1 - c)
        devs, js = _chip_peers(x, y, c)
        half, other = pl.ds(c, 1), pl.ds(1 - c, 1)
        first = []
        for i in range(n):
            for k in range(3):
                cp = pltpu.make_async_remote_copy(src[i].at[half], dst[i].at[half, pl.ds(jme, 1)], send.at[6 * i + k],
                                                  recv.at[6 * i + k], device_id=devs[k], device_id_type=MESH)
                cp.start()
                first.append(cp)
        passed = []
        for i in range(n):
            for k in range(3):
                slot = dst[i].at[half, pl.ds(js[k], 1)]
                pltpu.make_async_remote_copy(src[i].at[half], slot, send.at[6 * i + k], recv.at[6 * i + k],
                                             device_id=devs[k], device_id_type=MESH).wait_recv()
                fw = pltpu.make_async_remote_copy(slot, slot, send.at[6 * i + 3 + k], recv.at[6 * i + 3 + k],
                                                  device_id=sib, device_id_type=MESH)
                fw.start()
                passed.append(fw)
        for i in range(n):
            for k in range(3):
                slot = dst[i].at[other, pl.ds(js[k], 1)]
                pltpu.make_async_remote_copy(slot, slot, send.at[6 * i + 3 + k], recv.at[6 * i + 3 + k],
                                             device_id=sib, device_id_type=MESH).wait_recv()
        for cp in first + passed:
            cp.wait_send()

    ins = [t.reshape(t.shape[0], 1, *t.shape[1:]) for t in xs]
    outs = [jax.ShapeDtypeStruct((t.shape[0], 4, *t.shape[1:]), t.dtype) for t in xs]
    return _pc(body, in_specs=[_HBM] * n, out_specs=[_HBM] * n, out_shape=outs,
               scratch_shapes=[pltpu.SemaphoreType.DMA((6 * n,)), pltpu.SemaphoreType.DMA((6 * n,))],
               compiler_params=pltpu.CompilerParams(has_side_effects=True), name=name)(*ins)


def _fill_own(gathered, own):
    jme = (2 * lax.axis_index("x") + lax.axis_index("y")).astype(jnp.int32)
    zero = jnp.int32(0)
    return lax.dynamic_update_slice(gathered, own[:, None], (zero, jme, zero, zero))


def _exchange_halves(gs, name):
    n = len(gs)

    def body(*refs):
        src, dst = refs[:n], refs[n:2 * n]
        send, recv = refs[2 * n:]
        x, y, c = _me()
        cps = []
        for i in range(n):
            cp = pltpu.make_async_remote_copy(src[i].at[pl.ds(1 - c, 1)], dst[i], send.at[i], recv.at[i],
                                              device_id=(x, y, 1 - c), device_id_type=MESH)
            cp.start()
            cps.append(cp)
        for cp in cps:
            cp.wait()

    outs = [jax.ShapeDtypeStruct((1, *g.shape[1:]), g.dtype) for g in gs]
    return _pc(body, in_specs=[_HBM] * n, out_specs=[_HBM] * n, out_shape=outs,
               scratch_shapes=[pltpu.SemaphoreType.DMA((n,)), pltpu.SemaphoreType.DMA((n,))],
               compiler_params=pltpu.CompilerParams(has_side_effects=True), name=name)(*gs)


def _scatter_chips(ps, name):
    n = len(ps)

    def body(*refs):
        src, dst = refs[:n], refs[n:2 * n]
        send, recv = refs[2 * n:]
        x, y, c = _me()
        devs, js = _chip_peers(x, y, c)
        cps = []
        for i in range(n):
            for k in range(3):
                cp = pltpu.make_async_remote_copy(src[i].at[pl.ds(js[k], 1)], dst[i].at[k], send.at[3 * i + k],
                                                  recv.at[3 * i + k], device_id=devs[k], device_id_type=MESH)
                cp.start()
                cps.append(cp)
        for cp in cps:
            cp.wait()

    outs = [jax.ShapeDtypeStruct((3, 1, *p.shape[1:]), p.dtype) for p in ps]
    return _pc(body, in_specs=[_HBM] * n, out_specs=[_HBM] * n, out_shape=outs,
               scratch_shapes=[pltpu.SemaphoreType.DMA((3 * n,)), pltpu.SemaphoreType.DMA((3 * n,))],
               compiler_params=pltpu.CompilerParams(has_side_effects=True), name=name)(*ps)


def _swap_sibling(hs, name):
    n = len(hs)

    def body(*refs):
        src, dst = refs[:n], refs[n:2 * n]
        send, recv = refs[2 * n:]
        x, y, c = _me()
        cps = []
        for i in range(n):
            cp = pltpu.make_async_remote_copy(src[i], dst[i], send.at[i], recv.at[i], device_id=(x, y, 1 - c),
                                              device_id_type=MESH)
            cp.start()
            cps.append(cp)
        for cp in cps:
            cp.wait()

    outs = [jax.ShapeDtypeStruct(h.shape, h.dtype) for h in hs]
    return _pc(body, in_specs=[_HBM] * n, out_specs=[_HBM] * n, out_shape=outs,
               scratch_shapes=[pltpu.SemaphoreType.DMA((n,)), pltpu.SemaphoreType.DMA((n,))],
               compiler_params=pltpu.CompilerParams(has_side_effects=True), name=name)(*hs)


def _row_tile(r):
    return _pick(r, (256, 128, 64, 32, 16, 8))


def _add_half(g, r1, cidx, name):
    _, _, r, n = g.shape
    tr = _row_tile(r)

    def body(c_ref, g_ref, r_ref, o_ref):
        o_ref[...] = (g_ref[...] + r_ref[...]).astype(GRAD_TRANSIT)

    blk = lambda f: pl.BlockSpec((None, None, tr, n), f)
    gs = pltpu.PrefetchScalarGridSpec(
        num_scalar_prefetch=1, grid=(4, r // tr),
        in_specs=[blk(lambda j, i, c: (c[0], j, i, 0)), blk(lambda j, i, c: (0, j, i, 0))],
        out_specs=pl.BlockSpec((None, tr, n), lambda j, i, c: (j, i, 0)))
    return _pc(body, grid_spec=gs, out_shape=jax.ShapeDtypeStruct((4, r, n), GRAD_TRANSIT),
               compiler_params=_cp(("parallel", "parallel")), name=name)(cidx, g, r1)


def _add_chips(p, r3, jidx, name):
    _, r, n = p.shape
    tr = _row_tile(r)

    def body(j_ref, p_ref, a_ref, b_ref, c_ref, o_ref):
        o_ref[...] = ((p_ref[...].astype(f32) + a_ref[...].astype(f32)) + b_ref[...].astype(f32)) + c_ref[...].astype(f32)

    rblk = lambda k: pl.BlockSpec((None, None, tr, n), lambda i, j: (k, 0, i, 0))
    gs = pltpu.PrefetchScalarGridSpec(
        num_scalar_prefetch=1, grid=(r // tr,),
        in_specs=[pl.BlockSpec((None, tr, n), lambda i, j: (j[0], i, 0)), rblk(0), rblk(1), rblk(2)],
        out_specs=pl.BlockSpec((tr, n), lambda i, j: (i, 0)))
    return _pc(body, grid_spec=gs, out_shape=jax.ShapeDtypeStruct((r, n), f32),
               compiler_params=_cp(("parallel",)), name=name)(jidx, p, r3, r3, r3)


def _adamw_halves(w, mine, theirs, m, v, cidx, name):
    _, r, n = w.shape
    tr = _row_tile(r)
    c1 = 1.0 / (1.0 - ADAM_B1 ** ADAM_STEP)
    c2 = 1.0 / (1.0 - ADAM_B2 ** ADAM_STEP)

    def body(c_ref, w_ref, a_ref, b_ref, m_ref, v_ref, g_ref, d_ref, mo_ref, vo_ref):
        gv = jnp.where(pl.program_id(0) == c_ref[0], a_ref[...], b_ref[...])
        m2 = ADAM_B1 * m_ref[...] + (1.0 - ADAM_B1) * gv
        v2 = ADAM_B2 * v_ref[...] + (1.0 - ADAM_B2) * (gv * gv)
        g_ref[...] = gv
        d_ref[...] = -ADAM_LR * ((m2 * c1) / (jnp.sqrt(v2 * c2) + ADAM_EPS) + ADAM_WD * w_ref[...])
        mo_ref[...] = m2
        vo_ref[...] = v2

    full = pl.BlockSpec((None, tr, n), lambda l, i, c: (l, i, 0))
    half = pl.BlockSpec((tr, n), lambda l, i, c: (i, 0))
    gs = pltpu.PrefetchScalarGridSpec(num_scalar_prefetch=1, grid=(2, r // tr),
                                      in_specs=[full, half, half, full, full], out_specs=[full] * 4)
    return _pc(body, grid_spec=gs, out_shape=[jax.ShapeDtypeStruct(w.shape, f32)] * 4,
               compiler_params=_cp(("parallel", "parallel")), name=name)(cidx, w, mine, theirs, m, v)


def _adamw(w, g, m, v, name):
    rows, n = w.shape
    tr = _row_tile(rows)
    c1 = 1.0 / (1.0 - ADAM_B1 ** ADAM_STEP)
    c2 = 1.0 / (1.0 - ADAM_B2 ** ADAM_STEP)

    def body(w_ref, g_ref, m_ref, v_ref, d_ref, mo_ref, vo_ref):
        gv = g_ref[...]
        m2 = ADAM_B1 * m_ref[...] + (1.0 - ADAM_B1) * gv
        v2 = ADAM_B2 * v_ref[...] + (1.0 - ADAM_B2) * (gv * gv)
        d_ref[...] = -ADAM_LR * ((m2 * c1) / (jnp.sqrt(v2 * c2) + ADAM_EPS) + ADAM_WD * w_ref[...])
        mo_ref[...] = m2
        vo_ref[...] = v2

    blk = pl.BlockSpec((tr, n), lambda i: (i, 0))
    return _pc(body, grid=(rows // tr,), in_specs=[blk] * 4, out_specs=[blk] * 3,
               out_shape=[jax.ShapeDtypeStruct((rows, n), f32)] * 3,
               compiler_params=_cp(("parallel",)), name=name)(w, g, m, v)


def _full_from_gathered(name, t):
    L, _, r, n = t.shape
    if SHARDED[name] == 1 or name in TRANSPOSED:
        return t.reshape(L, 4 * r, n)
    return t.transpose(0, 2, 1, 3).reshape(L, r, 4 * n)


def _shard_major(name, g):
    L, R, C = g.shape
    if SHARDED[name] == 1 or name in TRANSPOSED:
        return g.reshape(L, 4, R // 4, C)
    return g.reshape(L, R, 4, C // 4).transpose(0, 2, 1, 3)


_SMALL_ROWS = 288


def _pack_small(d):
    flat = jnp.concatenate([d[n].reshape(-1) for n in SMALL])
    total = 2 * 4 * _SMALL_ROWS * LANES
    flat = jnp.concatenate([flat, jnp.zeros((total - flat.shape[0],), f32)])
    return flat.reshape(2, 4, _SMALL_ROWS, LANES)


def _unpack_small(t, like):
    flat = t.reshape(-1)
    out, off = {}, 0
    for n in SMALL:
        sz = math.prod(like[n].shape)
        out[n] = flat[off:off + sz].reshape(like[n].shape)
        off += sz
    return out


def kernel(x, mem, positions, norm_mix_g, w_in, q_norm_g, w_uq, kv_norm_g, w_ukv, ssm_lambda_re, ssm_lambda_im, ssm_log_dt, ssm_b_re, ssm_b_im, ssm_c_re, ssm_c_im, ssm_d, ssm_w_glu, ssm_b_glu, attn_out_g, ssm_out_g, w_out, norm_x_g, mem_norm_g, w_xq, w_xkv, w_xo, norm_ffn_g, w_gate, w_up, w_down, final_norm_g, loss_target, m_norm_mix_g, m_w_in, m_q_norm_g, m_w_uq, m_kv_norm_g, m_w_ukv, m_ssm_lambda_re, m_ssm_lambda_im, m_ssm_log_dt, m_ssm_b_re, m_ssm_b_im, m_ssm_c_re, m_ssm_c_im, m_ssm_d, m_ssm_w_glu, m_ssm_b_glu, m_attn_out_g, m_ssm_out_g, m_w_out, m_norm_x_g, m_mem_norm_g, m_w_xq, m_w_xkv, m_w_xo, m_norm_ffn_g, m_w_gate, m_w_up, m_w_down, m_final_norm_g, v_norm_mix_g, v_w_in, v_q_norm_g, v_w_uq, v_kv_norm_g, v_w_ukv, v_ssm_lambda_re, v_ssm_lambda_im, v_ssm_log_dt, v_ssm_b_re, v_ssm_b_im, v_ssm_c_re, v_ssm_c_im, v_ssm_d, v_ssm_w_glu, v_ssm_b_glu, v_attn_out_g, v_ssm_out_g, v_w_out, v_norm_x_g, v_mem_norm_g, v_w_xq, v_w_xkv, v_w_xo, v_norm_ffn_g, v_w_gate, v_w_up, v_w_down, v_final_norm_g):
    given = dict(locals())
    swap = lambda n, t: jnp.swapaxes(t, *TRANSPOSED[n]) if n in TRANSPOSED else t
    w = {n: swap(n, given[n]) for n in WEIGHTS}
    m = {n: swap(n, given["m_" + n]) for n in WEIGHTS}
    v = {n: swap(n, given["v_" + n]) for n in WEIGHTS}
    big = list(SHARDED)

    shards = [w[n].astype(bf16) for n in big]
    gathered = _allgather_chips(shards, "allgather_weights")
    W = {n: _full_from_gathered(n, _fill_own(t, s)) for n, t, s in zip(big, gathered, shards)}
    W.update({n: w[n] for n in SMALL})

    loss, dx, grads = _local_step(x[0], mem[0], positions[0], loss_target[0], W)
    loss = lax.psum(loss, ("x", "y", "c"))

    cidx = lax.axis_index("c").astype(jnp.int32).reshape(1)
    jidx = (2 * lax.axis_index("x") + lax.axis_index("y")).astype(jnp.int32).reshape(1)
    names = big + ["small"]
    gs = [_shard_major(n, grads[n]) for n in big] + [_pack_small(grads)]
    r1 = _exchange_halves(gs, "grad_exchange_halves")
    ps = [_add_half(g, r, cidx, f"grad_add_half_{n}") for n, g, r in zip(names, gs, r1)]
    r3 = _scatter_chips(ps, "grad_scatter_chips")
    hs = [_add_chips(p, r, jidx, f"grad_add_chips_{n}") for n, p, r in zip(names, ps, r3)]
    ts = _swap_sibling(hs, "grad_swap_sibling")

    out_g, out_d, out_m, out_v = {}, {}, {}, {}
    for n, h, t in zip(big, hs[:-1], ts[:-1]):
        out_g[n], out_d[n], out_m[n], out_v[n] = _adamw_halves(w[n], h, t, m[n], v[n], cidx, f"adamw_{n}")
    both = jnp.stack([hs[-1], ts[-1]])
    piece = jnp.where(cidx[0] == 0, both, both[::-1])
    gsm = _fill_own(_allgather_chips([piece], "allgather_small")[0], piece)
    rows2 = 2 * 4 * _SMALL_ROWS
    flat = lambda d: _pack_small(d).reshape(rows2, LANES)
    d_, m_, v_ = _adamw(flat(w), gsm.reshape(rows2, LANES), flat(m), flat(v), "adamw_small")
    for dst, t in ((out_g, gsm), (out_d, d_), (out_m, m_), (out_v, v_)):
        dst.update(_unpack_small(t, w))

    outs = [[swap(n, d[n]) for n in WEIGHTS] for d in (out_g, out_d, out_m, out_v)]
    return (loss, dx.reshape(x.shape), *outs[0], *outs[1], *outs[2], *outs[3])
```

```python
import functools
import math

import jax
import jax.numpy as jnp
from jax import lax
from jax.experimental import pallas as pl
from jax.experimental.pallas import tpu as pltpu

f32, bf16 = jnp.float32, jnp.bfloat16

D_MODEL = 1024
DEPTH = 2
MLA_HEADS = 8
QK_NOPE = 64
QK_ROPE = 32
V_HEAD = 64
Q_LORA = 256
KV_LORA = 128
MLA_WIDTH = MLA_HEADS * V_HEAD
ROPE_THETA = 10000.0
SSM_WIDTH = 512
SSM_GROUP = 16
SSM_GROUPS = 32
SSM_STATE = 64
IN_WIDTH = Q_LORA + KV_LORA + QK_ROPE + SSM_WIDTH
X_HEADS = 4
X_HEAD_DIM = D_MODEL // X_HEADS
D_FF = 2816
EPS = 1e-6
ADAM_LR, ADAM_B1, ADAM_B2, ADAM_EPS, ADAM_WD, ADAM_STEP = 0.001, 0.9, 0.999, 1e-08, 0.01, 10

LANES = 128
SUBLANES = 8
HEAD_PAD = 128
MLA_PAD = MLA_HEADS * HEAD_PAD
SSM_MACRO = 4
MACRO_CH = SSM_WIDTH // SSM_MACRO
MACRO_ST = SSM_GROUPS // SSM_MACRO * SSM_STATE
VMEM_LIMIT = 56 * 1024 * 1024
GRAD_TRANSIT = bf16

WEIGHTS = ['norm_mix_g', 'w_in', 'q_norm_g', 'w_uq', 'kv_norm_g', 'w_ukv', 'ssm_lambda_re', 'ssm_lambda_im',
           'ssm_log_dt', 'ssm_b_re', 'ssm_b_im', 'ssm_c_re', 'ssm_c_im', 'ssm_d', 'ssm_w_glu', 'ssm_b_glu',
           'attn_out_g', 'ssm_out_g', 'w_out', 'norm_x_g', 'mem_norm_g', 'w_xq', 'w_xkv', 'w_xo', 'norm_ffn_g',
           'w_gate', 'w_up', 'w_down', 'final_norm_g']
SHARDED = {'w_in': 1, 'w_uq': 2, 'w_ukv': 2, 'ssm_w_glu': 1, 'w_out': 1, 'w_xq': 1, 'w_xkv': 2, 'w_xo': 1,
           'w_gate': 2, 'w_up': 2, 'w_down': 1}
SMALL = [n for n in WEIGHTS if n not in SHARDED]
TRANSPOSED = {'w_gate': (1, 2), 'w_up': (1, 2), 'ssm_b_re': (2, 3), 'ssm_b_im': (2, 3)}
MESH = pl.DeviceIdType.MESH


def _pc(body, **kw):
    return pl.pallas_call(body, **kw)


def _pick(n, prefs):
    for p in prefs:
        if n % p == 0:
            return p
    return n


def _cp(sem=None):
    return pltpu.CompilerParams(dimension_semantics=sem, vmem_limit_bytes=VMEM_LIMIT)


_TILE_CANDS = (1024, 1408, 512, 256, 128)
MM_VMEM_BUDGET = 40 * 1024 * 1024


def _mm_tiles(M, K, N, a_bytes, b_bytes, o_bytes, npair, has_res, need_acc):
    best = None
    for tm in _TILE_CANDS:
        for tk in _TILE_CANDS:
            if M % tm or K % tk:
                continue
            vm = npair * (2 * tm * tk * a_bytes + 2 * tk * N * b_bytes) + 2 * tm * N * o_bytes
            vm += tm * N * 4 * (1 + need_acc + 2 * has_res)
            if a_bytes == 4:
                vm += npair * tm * tk * 2
            if b_bytes == 4:
                vm += npair * tk * N * 2
            if vm <= MM_VMEM_BUDGET and (best is None or tm * tk > best[0]):
                best = (tm * tk, tm, tk)
    if best is None:
        return _pick(M, (256, 128)), _pick(K, (256, 128))
    return best[1], best[2]


def _mm(pairs, mode, out_dtype, res=None, stack=None, name="mm"):
    a0, b0 = pairs[0]
    if mode == 'nn':
        (M, K), N = a0.shape, b0.shape[1]
        dims = (((1,), (0,)), ((), ()))
    elif mode == 'nt':
        (M, K), N = a0.shape, b0.shape[0]
        dims = (((1,), (1,)), ((), ()))
    else:
        (K, M), N = a0.shape, b0.shape[1]
        dims = (((0,), (0,)), ((), ()))
    npair = len(pairs)
    has_res = res is not None
    direct = out_dtype == f32
    tm, tk = _mm_tiles(M, K, N, a0.dtype.itemsize, b0.dtype.itemsize, jnp.dtype(out_dtype).itemsize, npair, has_res,
                       not direct)
    nk = K // tk

    def body(*refs):
        ins = refs[:2 * npair]
        res_ref = refs[2 * npair] if has_res else None
        has_prev = stack is not None and stack[1] is not None
        o_ref = refs[2 * npair + has_res + has_prev]
        acc = o_ref if direct else refs[2 * npair + has_res + has_prev + 1]
        k = pl.program_id(1)
        s = None
        for p in range(npair):
            d = lax.dot_general(ins[2 * p][...].astype(bf16), ins[2 * p + 1][...].astype(bf16), dims,
                                preferred_element_type=f32)
            s = d if s is None else s + d

        @pl.when(k == 0)
        def _():
            acc[...] = s

        @pl.when(k > 0)
        def _():
            acc[...] += s

        if has_res or not direct:
            @pl.when(k == nk - 1)
            def _():
                r = acc[...]
                if has_res:
                    r = r + res_ref[...]
                o_ref[...] = r.astype(out_dtype)

    if mode == 'nn':
        a_spec = pl.BlockSpec((tm, tk), lambda i, k: (i, k))
        b_spec = pl.BlockSpec((tk, N), lambda i, k: (k, 0))
    elif mode == 'nt':
        a_spec = pl.BlockSpec((tm, tk), lambda i, k: (i, k))
        b_spec = pl.BlockSpec((N, tk), lambda i, k: (0, k))
    else:
        a_spec = pl.BlockSpec((tk, tm), lambda i, k: (k, i))
        b_spec = pl.BlockSpec((tk, N), lambda i, k: (k, 0))
    o_spec = pl.BlockSpec((tm, N), lambda i, k: (i, 0))
    in_specs = [a_spec, b_spec] * npair + ([o_spec] if has_res else [])
    args = [t for p in pairs for t in p] + ([res] if has_res else [])
    out_shape = jax.ShapeDtypeStruct((M, N), out_dtype)
    aliases = {}
    if stack is not None:
        layer, prev = stack
        o_spec = pl.BlockSpec((None, tm, N), lambda i, k: (layer, i, 0))
        out_shape = jax.ShapeDtypeStruct((DEPTH, M, N), out_dtype)
        if prev is not None:
            in_specs.append(pl.BlockSpec(memory_space=pl.ANY))
            args.append(prev)
            aliases = {len(args) - 1: 0}
    return _pc(body, grid=(M // tm, nk), in_specs=in_specs, out_specs=o_spec, out_shape=out_shape,
               scratch_shapes=[] if direct else [pltpu.VMEM((tm, N), f32)], input_output_aliases=aliases,
               compiler_params=_cp(("parallel", "arbitrary")), name=name)(*args)


def _rms_fwd(x, g, *, col0=0, width=None, n_valid=None, out_dtype=bf16, name="rms_fwd"):
    S = x.shape[0]
    width = width or x.shape[1]
    n_valid = n_valid or width
    ts = _pick(S, (512, 256, 128))
    cb = col0 // width

    def body(x_ref, g_ref, o_ref):
        xv = x_ref[...]
        ms = jnp.sum(xv * xv, axis=-1, keepdims=True) * (1.0 / n_valid)
        o_ref[...] = (xv * lax.rsqrt(ms + EPS) * g_ref[...]).astype(out_dtype)

    return _pc(body, grid=(S // ts,),
               in_specs=[pl.BlockSpec((ts, width), lambda i: (i, cb)), pl.BlockSpec((1, width), lambda i: (0, 0))],
               out_specs=pl.BlockSpec((ts, width), lambda i: (i, 0)),
               out_shape=jax.ShapeDtypeStruct((S, width), out_dtype),
               compiler_params=_cp(("parallel",)), name=name)(x, g.reshape(1, width))


def _rms_bwd(x, g, dy, *, col0=0, dcol0=0, width=None, n_valid=None, res=None, out_dtype=f32, delta=False,
             name="rms_bwd"):
    S = x.shape[0]
    width = width or x.shape[1]
    n_valid = n_valid or width
    ts = _pick(S, (512, 256, 128))
    cb, dcb = col0 // width, dcol0 // width
    has_res = res is not None

    def body(*refs):
        x_ref, g_ref, dy_ref = refs[:3]
        res_ref = refs[3] if has_res else None
        outs = refs[3 + has_res:]
        dx_ref, dg_ref = outs[0], outs[1]
        i = pl.program_id(0)
        xv = x_ref[...]
        gv = g_ref[...]
        dyv = dy_ref[...].astype(f32)
        rstd = lax.rsqrt(jnp.sum(xv * xv, axis=-1, keepdims=True) * (1.0 / n_valid) + EPS)
        xh = xv * rstd
        dxh = dyv * gv
        mean = jnp.sum(dxh * xh, axis=-1, keepdims=True) * (1.0 / n_valid)
        dx = rstd * (dxh - xh * mean)
        if delta:
            d_ref = outs[2]
            for h in range(width // LANES):
                sl = slice(h * LANES, (h + 1) * LANES)
                dsum = jnp.sum(dx[:, sl] * xv[:, sl], axis=-1, keepdims=True)
                d_ref[:, sl] = jnp.broadcast_to(dsum, (ts, LANES))
        if has_res:
            dx = dx + res_ref[...]
        dx_ref[...] = dx.astype(out_dtype)

        @pl.when(i == 0)
        def _():
            dg_ref[...] = jnp.zeros_like(dg_ref)

        dg_ref[...] += jnp.sum(dyv * xh, axis=0, keepdims=True)

    blk = lambda c: pl.BlockSpec((ts, width), lambda i: (i, c))
    in_specs = [blk(cb), pl.BlockSpec((1, width), lambda i: (0, 0)), blk(dcb)] + ([blk(0)] if has_res else [])
    out_specs = [blk(0), pl.BlockSpec((1, width), lambda i: (0, 0))] + ([blk(0)] if delta else [])
    out_shape = [jax.ShapeDtypeStruct((S, width), out_dtype), jax.ShapeDtypeStruct((1, width), f32)] + (
        [jax.ShapeDtypeStruct((S, width), f32)] if delta else [])
    args = [x, g.reshape(1, width), dy] + ([res] if has_res else [])
    return _pc(body, grid=(S // ts,), in_specs=in_specs, out_specs=out_specs, out_shape=out_shape,
               compiler_params=_cp(("arbitrary",)), name=name)(*args)


def _loss_head(h, g, target, name="loss_head"):
    S, D = h.shape
    ts = _pick(S, (512, 256, 128))

    def body(h_ref, g_ref, t_ref, dh_ref, dg_ref, loss_ref):
        i = pl.program_id(0)
        xv = h_ref[...]
        gv = g_ref[...]
        rstd = lax.rsqrt(jnp.sum(xv * xv, axis=-1, keepdims=True) * (1.0 / D) + EPS)
        xh = xv * rstd
        err = xh * gv - t_ref[...]
        dyv = err * (1.0 / D)
        dxh = dyv * gv
        mean = jnp.sum(dxh * xh, axis=-1, keepdims=True) * (1.0 / D)
        dh_ref[...] = rstd * (dxh - xh * mean)

        @pl.when(i == 0)
        def _():
            dg_ref[...] = jnp.zeros_like(dg_ref)
            loss_ref[...] = jnp.zeros_like(loss_ref)

        dg_ref[...] += jnp.sum(dyv * xh, axis=0, keepdims=True)
        part = jnp.sum(jnp.sum(err * err, axis=-1, keepdims=True), axis=0, keepdims=True) * (0.5 / D)
        loss_ref[...] += jnp.broadcast_to(part, (1, LANES))

    blk = pl.BlockSpec((ts, D), lambda i: (i, 0))
    row = pl.BlockSpec((1, D), lambda i: (0, 0))
    return _pc(body, grid=(S // ts,), in_specs=[blk, row, blk],
               out_specs=[blk, row, pl.BlockSpec((1, LANES), lambda i: (0, 0))],
               out_shape=[jax.ShapeDtypeStruct((S, D), f32), jax.ShapeDtypeStruct((1, D), f32),
                          jax.ShapeDtypeStruct((1, LANES), f32)],
               compiler_params=_cp(("arbitrary",)), name=name)(h, g.reshape(1, D), target)


def _rope_apply(x, tc, s1, s2):
    return x * tc + pltpu.roll(x, LANES - 16, 1) * s1 + pltpu.roll(x, 16, 1) * s2


def _rope_apply_t(dy, tc, s1, s2):
    return dy * tc + pltpu.roll(dy * s1, 16, 1) + pltpu.roll(dy * s2, LANES - 16, 1)


def _rope_fwd(q, kv, proj, tabs, name="rope_fwd"):
    S = q.shape[0]
    ts = _pick(S, (512, 256, 128))
    scale = (QK_NOPE + QK_ROPE) ** -0.5

    def body(q_ref, kk_ref, kvv_ref, kr_ref, tc_ref, s1_ref, s2_ref, qh_ref, kh_ref, vh_ref):
        tc, s1, s2 = tc_ref[...], s1_ref[...], s2_ref[...]
        krr = _rope_apply(pltpu.roll(kr_ref[...], QK_NOPE, 1), tc, s1, s2)
        for h in range(MLA_HEADS):
            sl = slice(h * HEAD_PAD, (h + 1) * HEAD_PAD)
            qh_ref[:, sl] = (_rope_apply(q_ref[:, sl], tc, s1, s2) * scale).astype(bf16)
            kh_ref[:, sl] = (kk_ref[:, sl] + krr).astype(bf16)
        vh_ref[...] = kvv_ref[...].astype(bf16)

    wide = lambda c: pl.BlockSpec((ts, MLA_PAD), lambda i: (i, c))
    tab = pl.BlockSpec((ts, LANES), lambda i: (i, 0))
    return _pc(body, grid=(S // ts,),
               in_specs=[wide(0), wide(0), wide(1), pl.BlockSpec((ts, LANES), lambda i: (i, 3)), tab, tab, tab],
               out_specs=[wide(0)] * 3, out_shape=[jax.ShapeDtypeStruct((S, MLA_PAD), bf16)] * 3,
               compiler_params=_cp(("parallel",)), name=name)(q, kv, kv, proj, *tabs)


def _rope_bwd(dqh, dkh, dvh, tabs, name="rope_bwd"):
    S = dqh.shape[0]
    ts = _pick(S, (512, 256, 128))
    scale = (QK_NOPE + QK_ROPE) ** -0.5

    def body(dq_ref, dk_ref, dv_ref, tc_ref, s1_ref, s2_ref, oq_ref, okv_ref, okr_ref):
        tc, s1, s2 = tc_ref[...], s1_ref[...], s2_ref[...]
        ksum = None
        for h in range(MLA_HEADS):
            sl = slice(h * HEAD_PAD, (h + 1) * HEAD_PAD)
            oq_ref[:, sl] = (_rope_apply_t(dq_ref[:, sl], tc, s1, s2) * scale).astype(bf16)
            dk = dk_ref[:, sl]
            okv_ref[:, sl] = dk.astype(bf16)
            ksum = dk if ksum is None else ksum + dk
        okv_ref[:, MLA_PAD:] = dv_ref[...].astype(bf16)
        dkr = pltpu.roll(_rope_apply_t(ksum, tc, s1, s2), LANES - QK_NOPE, 1)
        lane = lax.broadcasted_iota(jnp.int32, (ts, LANES), 1)
        okr_ref[...] = jnp.where(lane < QK_ROPE, dkr, 0.0).astype(bf16)

    wide = pl.BlockSpec((ts, MLA_PAD), lambda i: (i, 0))
    tab = pl.BlockSpec((ts, LANES), lambda i: (i, 0))
    return _pc(body, grid=(S // ts,), in_specs=[wide, wide, wide, tab, tab, tab],
               out_specs=[wide, pl.BlockSpec((ts, 2 * MLA_PAD), lambda i: (i, 0)), tab],
               out_shape=[jax.ShapeDtypeStruct((S, MLA_PAD), bf16), jax.ShapeDtypeStruct((S, 2 * MLA_PAD), bf16),
                          jax.ShapeDtypeStruct((S, LANES), bf16)],
               compiler_params=_cp(("parallel",)), name=name)(dqh, dkh, dvh, *tabs)


ATT_BLK = 1024


def _attn_fwd(qh, kh, vh, name="attn_fwd"):
    S = qh.shape[0]
    tq = tk = min(S, ATT_BLK)
    nq, nk = S // tq, S // tk

    def body(q_ref, k_ref, v_ref, o_ref, lse_ref, m_sc, l_sc, acc_sc):
        i, j = pl.program_id(1), pl.program_id(2)

        @pl.when(j == 0)
        def _():
            m_sc[...] = jnp.full_like(m_sc, -1e30)
            l_sc[...] = jnp.zeros_like(l_sc)
            acc_sc[...] = jnp.zeros_like(acc_sc)

        def step(masked):
            s = lax.dot_general(q_ref[...], k_ref[...], (((1,), (1,)), ((), ())), preferred_element_type=f32)
            if masked:
                row = lax.broadcasted_iota(jnp.int32, (tq, tk), 0)
                col = lax.broadcasted_iota(jnp.int32, (tq, tk), 1)
                s = jnp.where(col <= row, s, -1e30)
            m_prev = m_sc[...]
            m_new = jnp.maximum(m_prev, jnp.max(s, axis=-1, keepdims=True))
            alpha = jnp.exp(m_prev - m_new)
            p = jnp.exp(s - m_new)
            l_sc[...] = alpha * l_sc[...] + jnp.sum(p, axis=-1, keepdims=True)
            acc_sc[...] = alpha * acc_sc[...] + jnp.dot(p.astype(bf16), v_ref[...], preferred_element_type=f32)
            m_sc[...] = m_new

        pl.when(j < i)(functools.partial(step, False))
        pl.when(j == i)(functools.partial(step, True))

        @pl.when(j == nk - 1)
        def _():
            l = l_sc[...]
            o_ref[...] = acc_sc[...] / l
            lse_ref[...] = jnp.broadcast_to(m_sc[...] + jnp.log(l), (tq, LANES))

    qspec = pl.BlockSpec((tq, HEAD_PAD), lambda h, i, j: (i, h))
    kspec = pl.BlockSpec((tk, HEAD_PAD), lambda h, i, j: (jnp.minimum(j, i), h))
    return _pc(body, grid=(MLA_HEADS, nq, nk), in_specs=[qspec, kspec, kspec], out_specs=[qspec, qspec],
               out_shape=[jax.ShapeDtypeStruct((S, MLA_PAD), f32)] * 2,
               scratch_shapes=[pltpu.VMEM((tq, 1), f32), pltpu.VMEM((tq, 1), f32), pltpu.VMEM((tq, HEAD_PAD), f32)],
               compiler_params=_cp(("parallel", "parallel", "arbitrary")), name=name)(qh, kh, vh)


def _attn_bwd(qh, kh, vh, do, lse, delta, name="attn_bwd"):
    S = qh.shape[0]
    tq = tk = min(S, ATT_BLK)
    nq, nk = S // tq, S // tk

    def body(q_ref, k_ref, v_ref, do_ref, lse_ref, dl_ref, dq_ref, dk_ref, dv_ref):
        j, i = pl.program_id(1), pl.program_id(2)

        @pl.when((j == 0) & (i == 0))
        def _():
            dq_ref[...] = jnp.zeros_like(dq_ref)

        @pl.when(i == 0)
        def _():
            dk_ref[...] = jnp.zeros_like(dk_ref)
            dv_ref[...] = jnp.zeros_like(dv_ref)

        def step(masked):
            nt = (((1,), (1,)), ((), ()))
            tn = (((0,), (0,)), ((), ()))
            qv, kv_, dov = q_ref[...], k_ref[...], do_ref[...]
            s = lax.dot_general(qv, kv_, nt, preferred_element_type=f32)
            p = jnp.exp(s - lse_ref[:, :1])
            if masked:
                row = lax.broadcasted_iota(jnp.int32, (tq, tk), 0)
                col = lax.broadcasted_iota(jnp.int32, (tq, tk), 1)
                p = jnp.where(col <= row, p, 0.0)
            dp = lax.dot_general(dov, v_ref[...], nt, preferred_element_type=f32)
            ds = (p * (dp - dl_ref[:, :1])).astype(bf16)
            dv_ref[...] += lax.dot_general(p.astype(bf16), dov, tn, preferred_element_type=f32)
            dk_ref[...] += lax.dot_general(ds, qv, tn, preferred_element_type=f32)
            rows = pl.ds(pl.multiple_of(i * tq, tq), tq)
            dq_ref[rows, :] += jnp.dot(ds, kv_, preferred_element_type=f32)

        pl.when(i > j)(functools.partial(step, False))
        pl.when(i == j)(functools.partial(step, True))

    qspec = pl.BlockSpec((tq, HEAD_PAD), lambda h, j, i: (jnp.maximum(i, j), h))
    kspec = pl.BlockSpec((tk, HEAD_PAD), lambda h, j, i: (j, h))
    colspec = pl.BlockSpec((S, HEAD_PAD), lambda h, j, i: (0, h))
    return _pc(body, grid=(MLA_HEADS, nk, nq), in_specs=[qspec, kspec, kspec, qspec, qspec, qspec],
               out_specs=[colspec, kspec, kspec], out_shape=[jax.ShapeDtypeStruct((S, MLA_PAD), f32)] * 3,
               compiler_params=_cp(("parallel", "arbitrary", "arbitrary")), name=name)(qh, kh, vh, do, lse, delta)


def _xattn_fwd(q, kv, name="xattn_fwd"):
    S = q.shape[0]
    M = kv.shape[0]
    tq = _pick(S, (256, 128))
    scale = X_HEAD_DIM ** -0.5

    def body(q_ref, kv_ref, o_ref):
        for h in range(X_HEADS):
            sl = slice(h * X_HEAD_DIM, (h + 1) * X_HEAD_DIM)
            k = kv_ref[:, sl]
            v = kv_ref[:, D_MODEL + h * X_HEAD_DIM:D_MODEL + (h + 1) * X_HEAD_DIM]
            s = lax.dot_general(q_ref[:, sl], k, (((1,), (1,)), ((), ())), preferred_element_type=f32) * scale
            e = jnp.exp(s - jnp.max(s, axis=-1, keepdims=True))
            p = e / jnp.sum(e, axis=-1, keepdims=True)
            o_ref[:, sl] = jnp.dot(p.astype(bf16), v, preferred_element_type=f32).astype(bf16)

    blk = pl.BlockSpec((tq, D_MODEL), lambda i: (i, 0))
    return _pc(body, grid=(S // tq,), in_specs=[blk, pl.BlockSpec((M, 2 * D_MODEL), lambda i: (0, 0))],
               out_specs=blk, out_shape=jax.ShapeDtypeStruct((S, D_MODEL), bf16),
               compiler_params=_cp(("parallel",)), name=name)(q, kv)


def _xattn_bwd(q, kv, do, name="xattn_bwd"):
    S = q.shape[0]
    M = kv.shape[0]
    tq = _pick(S, (256, 128))
    scale = X_HEAD_DIM ** -0.5

    def body(q_ref, kv_ref, do_ref, dq_ref, dkv_ref):
        i = pl.program_id(0)

        @pl.when(i == 0)
        def _():
            dkv_ref[...] = jnp.zeros_like(dkv_ref)

        nt = (((1,), (1,)), ((), ()))
        tn = (((0,), (0,)), ((), ()))
        for h in range(X_HEADS):
            sl = slice(h * X_HEAD_DIM, (h + 1) * X_HEAD_DIM)
            vsl = slice(D_MODEL + h * X_HEAD_DIM, D_MODEL + (h + 1) * X_HEAD_DIM)
            k, v, qv, dov = kv_ref[:, sl], kv_ref[:, vsl], q_ref[:, sl], do_ref[:, sl]
            s = lax.dot_general(qv, k, nt, preferred_element_type=f32) * scale
            e = jnp.exp(s - jnp.max(s, axis=-1, keepdims=True))
            p = e / jnp.sum(e, axis=-1, keepdims=True)
            dp = lax.dot_general(dov, v, nt, preferred_element_type=f32)
            ds = (p * (dp - jnp.sum(dp * p, axis=-1, keepdims=True)) * scale).astype(bf16)
            dq_ref[:, sl] = jnp.dot(ds, k, preferred_element_type=f32).astype(bf16)
            dkv_ref[:, sl] += lax.dot_general(ds, qv, tn, preferred_element_type=f32)
            dkv_ref[:, vsl] += lax.dot_general(p.astype(bf16), dov, tn, preferred_element_type=f32)

    blk = pl.BlockSpec((tq, D_MODEL), lambda i: (i, 0))
    full = pl.BlockSpec((M, 2 * D_MODEL), lambda i: (0, 0))
    return _pc(body, grid=(S // tq,), in_specs=[blk, full, blk], out_specs=[blk, full],
               out_shape=[jax.ShapeDtypeStruct((S, D_MODEL), bf16), jax.ShapeDtypeStruct((M, 2 * D_MODEL), f32)],
               compiler_params=_cp(("arbitrary",)), name=name)(q, kv, do)


ST_CHUNKS = 1


def _apow_init(a_ref, ap_ref, bp_ref, seg):
    P = MACRO_ST
    ar, ai = a_ref[:, :P], a_ref[:, P:]
    pr, pi = ar, ai
    for r in range(seg):
        ap_ref[r:r + 1, :P] = pr
        ap_ref[r:r + 1, P:] = pi
        if r < seg - 1:
            pr, pi = pr * ar - pi * ai, pr * ai + pi * ar
    br, bi = pr, pi
    for k in range(SUBLANES):
        bp_ref[k:k + 1, :P] = pr
        bp_ref[k:k + 1, P:] = pi
        pr, pi = pr * br - pi * bi, pr * bi + pi * br


def _segment_perm(tS):
    seg = tS // SUBLANES
    rows = jnp.arange(tS)
    src = (rows % SUBLANES) * seg + rows // SUBLANES
    return (src[:, None] == jnp.arange(tS)[None, :]).astype(f32)


def _unpermute_rows(pt, v):
    hi = v.astype(bf16)
    r1 = v - hi.astype(f32)
    mid = r1.astype(bf16)
    lo = (r1 - mid.astype(f32)).astype(bf16)
    out = jnp.dot(pt, jnp.concatenate([hi, mid, lo], axis=1), preferred_element_type=f32)
    w = v.shape[1]
    return (out[:, :w] + out[:, w:2 * w]) + out[:, 2 * w:]


def _scan_block(sc_ref, ap_ref, bp_ref, carry_ref, e_ref, seg, reverse):
    P = MACRO_ST
    sgn = -1.0 if reverse else 1.0
    CH = P // ST_CHUNKS
    rid = lax.broadcasted_iota(jnp.int32, (SUBLANES, CH), 0)
    for c in range(ST_CHUNKS):
        lr, li = slice(c * CH, (c + 1) * CH), slice(P + c * CH, P + (c + 1) * CH)
        ar, ai = ap_ref[0:1, lr], sgn * ap_ref[0:1, li]
        xr = xi = None
        for i in range(seg):
            r = seg - 1 - i if reverse else i
            rows = slice(SUBLANES * r, SUBLANES * (r + 1))
            sr, si = sc_ref[rows, lr], sc_ref[rows, li]
            if i == 0:
                xr, xi = sr, si
            else:
                xr, xi = ar * xr - ai * xi + sr, ar * xi + ai * xr + si
                sc_ref[rows, lr] = xr
                sc_ref[rows, li] = xi
        for sh in (1, 2, 4):
            pr, pi = bp_ref[sh - 1:sh, lr], sgn * bp_ref[sh - 1:sh, li]
            if reverse:
                tr = jnp.where(rid < SUBLANES - sh, pltpu.roll(xr, SUBLANES - sh, 0), 0.0)
                ti = jnp.where(rid < SUBLANES - sh, pltpu.roll(xi, SUBLANES - sh, 0), 0.0)
            else:
                tr = jnp.where(rid >= sh, pltpu.roll(xr, sh, 0), 0.0)
                ti = jnp.where(rid >= sh, pltpu.roll(xi, sh, 0), 0.0)
            xr, xi = xr + pr * tr - pi * ti, xi + pr * ti + pi * tr
        if reverse:
            bpr = jnp.zeros((SUBLANES, CH), f32)
            bpi = jnp.zeros((SUBLANES, CH), f32)
            for r in range(SUBLANES):
                bpr = jnp.where(rid == r, bp_ref[SUBLANES - 1 - r:SUBLANES - r, lr], bpr)
                bpi = jnp.where(rid == r, -bp_ref[SUBLANES - 1 - r:SUBLANES - r, li], bpi)
        else:
            bpr, bpi = bp_ref[:, lr], bp_ref[:, li]
        cr, cim = carry_ref[:, lr], carry_ref[:, li]
        xr, xi = xr + bpr * cr - bpi * cim, xi + bpr * cim + bpi * cr
        edge = 0 if reverse else SUBLANES - 1
        carry_ref[:, lr] = jnp.sum(jnp.where(rid == edge, xr, 0.0), axis=0, keepdims=True)
        carry_ref[:, li] = jnp.sum(jnp.where(rid == edge, xi, 0.0), axis=0, keepdims=True)
        if reverse:
            er = jnp.where(rid == SUBLANES - 1, cr, pltpu.roll(xr, SUBLANES - 1, 0))
            ei = jnp.where(rid == SUBLANES - 1, cim, pltpu.roll(xi, SUBLANES - 1, 0))
        else:
            er = jnp.where(rid == 0, cr, pltpu.roll(xr, 1, 0))
            ei = jnp.where(rid == 0, cim, pltpu.roll(xi, 1, 0))
        if e_ref is not None:
            e_ref[:, lr] = er
            e_ref[:, li] = ei
        for i in range(seg):
            r = seg - 1 - i if reverse else i
            rows = slice(SUBLANES * r, SUBLANES * (r + 1))
            pr, pi = ap_ref[i:i + 1, lr], sgn * ap_ref[i:i + 1, li]
            sc_ref[rows, lr] += pr * er - pi * ei
            sc_ref[rows, li] += pr * ei + pi * er


def _ssm_fwd(proj, bm, cm, a, d, name="ssm_fwd"):
    S = proj.shape[0]
    tS = _pick(S, (256, 128))
    nb = S // tS
    P2 = 2 * MACRO_ST
    seg = tS // SUBLANES
    ucol0 = (D_MODEL - SSM_WIDTH) // MACRO_CH

    perm = _segment_perm(tS)

    def body(u_ref, b_ref, c_ref, a_ref, d_ref, pm_ref, pt_ref, y_ref, xc_ref, bu_sc, ap_sc, bp_sc, car_sc):
        t = pl.program_id(1)

        @pl.when(t == 0)
        def _():
            _apow_init(a_ref, ap_sc, bp_sc, seg)
            car_sc[...] = jnp.zeros_like(car_sc)

        uv = u_ref[...]
        up = jnp.dot(pm_ref[...], uv.astype(bf16), preferred_element_type=f32).astype(bf16)
        bu_sc[...] = jnp.dot(up, b_ref[...], preferred_element_type=f32)
        xc_ref[...] = car_sc[...]
        _scan_block(bu_sc, ap_sc, bp_sc, car_sc, None, seg, False)
        yp = jnp.dot(bu_sc[...].astype(bf16), c_ref[...], preferred_element_type=f32)
        y_ref[...] = _unpermute_rows(pt_ref[...], yp) + d_ref[...] * uv

    sq = pl.BlockSpec((tS, tS), lambda m, t: (0, 0))
    return _pc(body, grid=(SSM_MACRO, nb),
               in_specs=[pl.BlockSpec((tS, MACRO_CH), lambda m, t: (t, ucol0 + m)),
                         pl.BlockSpec((None, MACRO_CH, P2), lambda m, t: (m, 0, 0)),
                         pl.BlockSpec((None, P2, MACRO_CH), lambda m, t: (m, 0, 0)),
                         pl.BlockSpec((None, 1, P2), lambda m, t: (m, 0, 0)),
                         pl.BlockSpec((1, MACRO_CH), lambda m, t: (0, m)), sq, sq],
               out_specs=[pl.BlockSpec((tS, MACRO_CH), lambda m, t: (t, m)),
                          pl.BlockSpec((None, None, 1, P2), lambda m, t: (m, t, 0, 0))],
               out_shape=[jax.ShapeDtypeStruct((S, SSM_WIDTH), f32), jax.ShapeDtypeStruct((SSM_MACRO, nb, 1, P2), f32)],
               scratch_shapes=[pltpu.VMEM((tS, P2), f32), pltpu.VMEM((seg, P2), f32),
                               pltpu.VMEM((SUBLANES, P2), f32), pltpu.VMEM((1, P2), f32)],
               compiler_params=_cp(("arbitrary", "arbitrary")), name=name)(
        proj, bm, cm, a, d.reshape(1, SSM_WIDTH), perm.astype(bf16), perm.T.astype(bf16))


def _ssm_bwd(proj, dy, xc, bm, cm, a, d, name="ssm_bwd"):
    S = proj.shape[0]
    tS = _pick(S, (256, 128))
    nb = S // tS
    P = MACRO_ST
    P2 = 2 * P
    seg = tS // SUBLANES
    ucol0 = (D_MODEL - SSM_WIDTH) // MACRO_CH

    perm = _segment_perm(tS)

    def body(u_ref, dy_ref, xc_ref, b_ref, c_ref, a_ref, d_ref, pm_ref, pt_ref, du_ref, db_ref, dc_ref, da_ref, dd_ref,
             x_sc, g_sc, ap_sc, bp_sc, e_sc, xcar_sc, gcar_sc):
        t = pl.program_id(1)

        @pl.when(t == 0)
        def _():
            _apow_init(a_ref, ap_sc, bp_sc, seg)
            gcar_sc[...] = jnp.zeros_like(gcar_sc)
            db_ref[...] = jnp.zeros_like(db_ref)
            dc_ref[...] = jnp.zeros_like(dc_ref)
            da_ref[...] = jnp.zeros_like(da_ref)
            dd_ref[...] = jnp.zeros_like(dd_ref)

        nt = (((1,), (1,)), ((), ()))
        tn = (((0,), (0,)), ((), ()))
        uv = u_ref[...]
        dyv = dy_ref[...]
        pm = pm_ref[...]
        ub = jnp.dot(pm, uv.astype(bf16), preferred_element_type=f32).astype(bf16)
        dyb = jnp.dot(pm, dyv.astype(bf16), preferred_element_type=f32).astype(bf16)
        x_sc[...] = jnp.dot(ub, b_ref[...], preferred_element_type=f32)
        xcar_sc[...] = xc_ref[...]
        _scan_block(x_sc, ap_sc, bp_sc, xcar_sc, e_sc, seg, False)
        g_sc[...] = lax.dot_general(dyb, c_ref[...], nt, preferred_element_type=f32)
        _scan_block(g_sc, ap_sc, bp_sc, gcar_sc, None, seg, True)
        xv = x_sc[...]
        gv = g_sc[...]
        gb = gv.astype(bf16)
        dc_ref[...] += lax.dot_general(xv.astype(bf16), dyb, tn, preferred_element_type=f32)
        db_ref[...] += lax.dot_general(ub, gb, tn, preferred_element_type=f32)
        dup = lax.dot_general(gb, b_ref[...], nt, preferred_element_type=f32)
        du_ref[...] = _unpermute_rows(pt_ref[...], dup) + d_ref[...] * dyv
        dd_ref[...] += jnp.sum(dyv * uv, axis=0, keepdims=True)
        xp = jnp.concatenate([e_sc[...], xv[:tS - SUBLANES]], axis=0)
        xpr, xpi, ggr, ggi = xp[:, :P], xp[:, P:], gv[:, :P], gv[:, P:]
        da_ref[:, :P] += jnp.sum(ggr * xpr + ggi * xpi, axis=0, keepdims=True)
        da_ref[:, P:] += jnp.sum(ggi * xpr - ggr * xpi, axis=0, keepdims=True)

    rev = lambda t: nb - 1 - t
    return _pc(body, grid=(SSM_MACRO, nb),
               in_specs=[pl.BlockSpec((tS, MACRO_CH), lambda m, t: (rev(t), ucol0 + m)),
                         pl.BlockSpec((tS, MACRO_CH), lambda m, t: (rev(t), m)),
                         pl.BlockSpec((None, None, 1, P2), lambda m, t: (m, rev(t), 0, 0)),
                         pl.BlockSpec((None, MACRO_CH, P2), lambda m, t: (m, 0, 0)),
                         pl.BlockSpec((None, P2, MACRO_CH), lambda m, t: (m, 0, 0)),
                         pl.BlockSpec((None, 1, P2), lambda m, t: (m, 0, 0)),
                         pl.BlockSpec((1, MACRO_CH), lambda m, t: (0, m)),
                         pl.BlockSpec((tS, tS), lambda m, t: (0, 0)), pl.BlockSpec((tS, tS), lambda m, t: (0, 0))],
               out_specs=[pl.BlockSpec((tS, MACRO_CH), lambda m, t: (rev(t), m)),
                          pl.BlockSpec((None, MACRO_CH, P2), lambda m, t: (m, 0, 0)),
                          pl.BlockSpec((None, P2, MACRO_CH), lambda m, t: (m, 0, 0)),
                          pl.BlockSpec((None, 1, P2), lambda m, t: (m, 0, 0)),
                          pl.BlockSpec((1, MACRO_CH), lambda m, t: (0, m))],
               out_shape=[jax.ShapeDtypeStruct((S, SSM_WIDTH), f32),
                          jax.ShapeDtypeStruct((SSM_MACRO, MACRO_CH, P2), f32),
                          jax.ShapeDtypeStruct((SSM_MACRO, P2, MACRO_CH), f32),
                          jax.ShapeDtypeStruct((SSM_MACRO, 1, P2), f32),
                          jax.ShapeDtypeStruct((1, SSM_WIDTH), f32)],
               scratch_shapes=[pltpu.VMEM((tS, P2), f32), pltpu.VMEM((tS, P2), f32),
                               pltpu.VMEM((seg, P2), f32), pltpu.VMEM((SUBLANES, P2), f32), pltpu.VMEM((SUBLANES, P2), f32),
                               pltpu.VMEM((1, P2), f32), pltpu.VMEM((1, P2), f32)],
               compiler_params=_cp(("arbitrary", "arbitrary")), name=name)(
        proj, dy, xc, bm, cm, a, d.reshape(1, SSM_WIDTH), perm.astype(bf16), perm.T.astype(bf16))


_GELU_K = math.sqrt(2.0 / math.pi)
_GELU_C = 0.044715


def _glu_fwd(y, w, b, g, name="glu_fwd"):
    S, W = y.shape
    ts = _pick(S, (512, 256, 128))

    def body(y_ref, w_ref, b_ref, g_ref, z_ref, sn_ref, ge_ref):
        yv = y_ref[...]
        cdf = 0.5 * (1.0 + jnp.tanh(_GELU_K * (yv + _GELU_C * (yv * yv * yv))))
        ge = (yv * cdf).astype(bf16)
        z = jnp.dot(ge, w_ref[...], preferred_element_type=f32) + b_ref[...]
        s = yv * jax.nn.sigmoid(z)
        rstd = lax.rsqrt(jnp.sum(s * s, axis=-1, keepdims=True) * (1.0 / W) + EPS)
        z_ref[...] = z
        sn_ref[...] = (s * rstd * g_ref[...]).astype(bf16)
        ge_ref[...] = ge

    blk = pl.BlockSpec((ts, W), lambda i: (i, 0))
    row = pl.BlockSpec((1, W), lambda i: (0, 0))
    return _pc(body, grid=(S // ts,), in_specs=[blk, pl.BlockSpec((W, W), lambda i: (0, 0)), row, row],
               out_specs=[blk, blk, blk],
               out_shape=[jax.ShapeDtypeStruct((S, W), f32), jax.ShapeDtypeStruct((S, W), bf16),
                          jax.ShapeDtypeStruct((S, W), bf16)],
               compiler_params=_cp(("parallel",)), name=name)(y, w, b.reshape(1, W), g.reshape(1, W))


def _glu_bwd(y, z, dmixed, w, g, name="glu_bwd"):
    S, W = y.shape
    ts = _pick(S, (512, 256, 128))
    dcb = MLA_PAD // W

    def body(y_ref, z_ref, dsn_ref, w_ref, g_ref, dy_ref, dz_ref, dg_ref, db_ref):
        i = pl.program_id(0)
        yv, zv, gv = y_ref[...], z_ref[...], g_ref[...]
        sig = jax.nn.sigmoid(zv)
        s = yv * sig
        rstd = lax.rsqrt(jnp.sum(s * s, axis=-1, keepdims=True) * (1.0 / W) + EPS)
        sh = s * rstd
        dsn = dsn_ref[...]
        dsh = dsn * gv
        ds = rstd * (dsh - sh * (jnp.sum(dsh * sh, axis=-1, keepdims=True) * (1.0 / W)))
        dz = ds * s * (1.0 - sig)
        dzb = dz.astype(bf16)
        dge = lax.dot_general(dzb, w_ref[...], (((1,), (1,)), ((), ())), preferred_element_type=f32)
        t = jnp.tanh(_GELU_K * (yv + _GELU_C * (yv * yv * yv)))
        dgelu = 0.5 * (1.0 + t) + 0.5 * yv * (1.0 - t * t) * _GELU_K * (1.0 + 3.0 * _GELU_C * yv * yv)
        dy_ref[...] = ds * sig + dge * dgelu
        dz_ref[...] = dzb

        @pl.when(i == 0)
        def _():
            dg_ref[...] = jnp.zeros_like(dg_ref)
            db_ref[...] = jnp.zeros_like(db_ref)

        dg_ref[...] += jnp.sum(dsn * sh, axis=0, keepdims=True)
        db_ref[...] += jnp.sum(dz, axis=0, keepdims=True)

    blk = pl.BlockSpec((ts, W), lambda i: (i, 0))
    row = pl.BlockSpec((1, W), lambda i: (0, 0))
    return _pc(body, grid=(S // ts,),
               in_specs=[blk, blk, pl.BlockSpec((ts, W), lambda i: (i, dcb)), pl.BlockSpec((W, W), lambda i: (0, 0)), row],
               out_specs=[blk, blk, row, row],
               out_shape=[jax.ShapeDtypeStruct((S, W), f32), jax.ShapeDtypeStruct((S, W), bf16),
                          jax.ShapeDtypeStruct((1, W), f32), jax.ShapeDtypeStruct((1, W), f32)],
               compiler_params=_cp(("arbitrary",)), name=name)(y, z, dmixed, w, g.reshape(1, W))


def _ffn_up(hn, wg, wu, name="ffn_up"):
    S, K = hn.shape
    F = wg.shape[0]
    tm, tn = _pick(S, (512, 256, 128)), _pick(F, (1408, 256, 128))

    def body(h_ref, wg_ref, wu_ref, g_ref, u_ref, a_ref):
        hv = h_ref[...]
        nt = (((1,), (1,)), ((), ()))
        gv = lax.dot_general(hv, wg_ref[...], nt, preferred_element_type=f32)
        uv = lax.dot_general(hv, wu_ref[...], nt, preferred_element_type=f32)
        g_ref[...] = gv
        u_ref[...] = uv
        a_ref[...] = (gv * jax.nn.sigmoid(gv) * uv).astype(bf16)

    wspec = pl.BlockSpec((tn, K), lambda i, j: (j, 0))
    ospec = pl.BlockSpec((tm, tn), lambda i, j: (i, j))
    return _pc(body, grid=(S // tm, F // tn), in_specs=[pl.BlockSpec((tm, K), lambda i, j: (i, 0)), wspec, wspec],
               out_specs=[ospec] * 3,
               out_shape=[jax.ShapeDtypeStruct((S, F), f32), jax.ShapeDtypeStruct((S, F), f32),
                          jax.ShapeDtypeStruct((S, F), bf16)],
               compiler_params=_cp(("parallel", "parallel")), name=name)(hn, wg, wu)


def _ffn_bwd_act(dh, wd, gate, up, name="ffn_bwd_act"):
    S, K = dh.shape
    F = wd.shape[0]
    tm, tn = _pick(S, (512, 256, 128)), _pick(F, (1408, 256, 128))

    def body(dh_ref, wd_ref, g_ref, u_ref, dg_ref, du_ref):
        dact = lax.dot_general(dh_ref[...].astype(bf16), wd_ref[...], (((1,), (1,)), ((), ())),
                               preferred_element_type=f32)
        gv, uv = g_ref[...], u_ref[...]
        sig = jax.nn.sigmoid(gv)
        dg_ref[...] = (dact * uv * (sig * (1.0 + gv * (1.0 - sig)))).astype(bf16)
        du_ref[...] = (dact * (gv * sig)).astype(bf16)

    ospec = pl.BlockSpec((tm, tn), lambda i, j: (i, j))
    return _pc(body, grid=(S // tm, F // tn),
               in_specs=[pl.BlockSpec((tm, K), lambda i, j: (i, 0)), pl.BlockSpec((tn, K), lambda i, j: (j, 0)),
                         ospec, ospec],
               out_specs=[ospec] * 2, out_shape=[jax.ShapeDtypeStruct((S, F), bf16)] * 2,
               compiler_params=_cp(("parallel", "parallel")), name=name)(dh, wd, gate, up)


def _pad_heads(w, per_head, pieces):
    K = w.shape[0]
    w3 = w.reshape(K, MLA_HEADS, per_head)
    out = jnp.zeros((K, MLA_HEADS, HEAD_PAD), w.dtype)
    for s0, s1, d0 in pieces:
        out = out.at[:, :, d0:d0 + (s1 - s0)].set(w3[:, :, s0:s1])
    return out.reshape(K, MLA_PAD)


def _unpad_heads(wp, per_head, pieces):
    K = wp.shape[0]
    w3 = wp.reshape(K, MLA_HEADS, HEAD_PAD)
    out = jnp.zeros((K, MLA_HEADS, per_head), wp.dtype)
    for s0, s1, d0 in pieces:
        out = out.at[:, :, s0:s1].set(w3[:, :, d0:d0 + (s1 - s0)])
    return out.reshape(K, MLA_HEADS * per_head)


_Q_PIECES = [(0, QK_NOPE + QK_ROPE, 0)]
_K_PIECES = [(0, QK_NOPE, 0)]
_V_PIECES = [(QK_NOPE, QK_NOPE + V_HEAD, 0)]
_KR0 = Q_LORA + KV_LORA


def _pack_win(w):
    z = jnp.zeros((w.shape[0], LANES - QK_ROPE), w.dtype)
    return jnp.concatenate([w[:, :_KR0 + QK_ROPE], z, w[:, _KR0 + QK_ROPE:]], axis=1)


def _unpack_win(wp):
    return jnp.concatenate([wp[:, :_KR0 + QK_ROPE], wp[:, _KR0 + LANES:]], axis=1)


def _pack_wout(w):
    wa = w[:MLA_WIDTH].reshape(MLA_HEADS, V_HEAD, D_MODEL)
    wa = jnp.concatenate([wa, jnp.zeros_like(wa)], axis=1).reshape(MLA_PAD, D_MODEL)
    return jnp.concatenate([wa, w[MLA_WIDTH:]], axis=0)


def _unpack_wout(wp):
    wa = wp[:MLA_PAD].reshape(MLA_HEADS, HEAD_PAD, D_MODEL)[:, :V_HEAD].reshape(MLA_WIDTH, D_MODEL)
    return jnp.concatenate([wa, wp[MLA_PAD:]], axis=0)


def _pad_gain(g):
    g2 = g.reshape(MLA_HEADS, V_HEAD)
    return jnp.concatenate([g2, jnp.zeros_like(g2)], axis=1).reshape(MLA_PAD)


def _unpad_gain(gp):
    return gp.reshape(MLA_HEADS, HEAD_PAD)[:, :V_HEAD].reshape(MLA_WIDTH)


def _ssm_prep(lam_re, lam_im, log_dt, b_re, b_im, c_re, c_im):
    lam = lax.complex(lam_re, lam_im)
    dt = jnp.exp(log_dt)[:, None]
    a_bar = jnp.exp(lam * dt)
    b_bar = ((a_bar - 1.0) / lam)[:, None, :] * lax.complex(b_re, b_im)
    G8 = SSM_GROUPS // SSM_MACRO
    eye = jnp.eye(G8, dtype=f32)

    def bmat(part):
        p4 = part.reshape(SSM_MACRO, G8, SSM_GROUP, SSM_STATE)
        return jnp.einsum('mgcp,gh->mgchp', p4, eye).reshape(SSM_MACRO, MACRO_CH, MACRO_ST)

    def cmat(part):
        p4 = part.reshape(SSM_MACRO, G8, SSM_GROUP, SSM_STATE)
        return jnp.einsum('mgcp,gh->mgphc', p4, eye).reshape(SSM_MACRO, MACRO_ST, MACRO_CH)

    bm = jnp.concatenate([bmat(b_bar.real), bmat(b_bar.imag)], axis=2)
    cm = jnp.concatenate([cmat(c_re), -cmat(c_im)], axis=1)
    a4 = a_bar.reshape(SSM_MACRO, 1, MACRO_ST)
    a = jnp.concatenate([a4.real, a4.imag], axis=2)
    return bm, cm, a


def _rope_tables(positions):
    freqs = ROPE_THETA ** (-jnp.arange(0, QK_ROPE, 2, dtype=f32) / QK_ROPE)
    ang = positions.astype(f32)[:, None] * freqs
    cos, sin = jnp.cos(ang), jnp.sin(ang)
    S = positions.shape[0]
    half = QK_ROPE // 2
    one, zero = jnp.ones((S, QK_NOPE), f32), jnp.zeros((S, half), f32)
    z64, z32 = jnp.zeros((S, QK_NOPE), f32), jnp.zeros((S, LANES - QK_NOPE - QK_ROPE), f32)
    tc = jnp.concatenate([one, cos, cos, z32], axis=1)
    s1 = jnp.concatenate([z64, -sin, zero, z32], axis=1)
    s2 = jnp.concatenate([z64, zero, sin, z32], axis=1)
    return tc, s1, s2


def _layer_params(W, l):
    p = {}
    p['win'] = _pack_win(W['w_in'][l])
    p['wuq'] = _pad_heads(W['w_uq'][l], QK_NOPE + QK_ROPE, _Q_PIECES)
    wukv = W['w_ukv'][l]
    p['wukv'] = jnp.concatenate([_pad_heads(wukv, QK_NOPE + V_HEAD, _K_PIECES),
                                 _pad_heads(wukv, QK_NOPE + V_HEAD, _V_PIECES)], axis=1)
    p['wout'] = _pack_wout(W['w_out'][l])
    p['attn_g'] = _pad_gain(W['attn_out_g'][l])
    return p


def _forward_layer(h, memn_in, tabs, W, l, name):
    p = _layer_params(W, l)
    sv = {'h0': h, 'p': p}
    xn = _rms_fwd(h, W['norm_mix_g'][l], name=name + "rms_mix")
    proj = _mm([(xn, p['win'])], 'nn', f32, name=name + "mm_in")
    cqn = _rms_fwd(proj, W['q_norm_g'][l], col0=0, width=Q_LORA, name=name + "rms_q")
    ckvn = _rms_fwd(proj, W['kv_norm_g'][l], col0=Q_LORA, width=KV_LORA, name=name + "rms_kv")
    q = _mm([(cqn, p['wuq'])], 'nn', f32, name=name + "mm_uq")
    kv = _mm([(ckvn, p['wukv'])], 'nn', f32, name=name + "mm_ukv")
    qh, kh, vh = _rope_fwd(q, kv, proj, tabs, name=name + "rope")
    oh, lse = _attn_fwd(qh, kh, vh, name=name + "attn")
    an = _rms_fwd(oh, p['attn_g'], n_valid=MLA_WIDTH, name=name + "rms_attn")
    bm, cm, a = W['ssm'][l]
    bmb, cmb = bm.astype(bf16), cm.astype(bf16)
    y, xc = _ssm_fwd(proj, bmb, cmb, a, W['ssm_d'][l], name=name + "ssm")
    z, sn, ge = _glu_fwd(y, W['ssm_w_glu'][l], W['ssm_b_glu'][l], W['ssm_out_g'][l], name=name + "glu")
    h1a = _mm([(an, p['wout'][:MLA_PAD])], 'nn', f32, res=h, name=name + "mm_out_a")
    h1 = _mm([(sn, p['wout'][MLA_PAD:])], 'nn', f32, res=h1a, name=name + "mm_out_s")
    hn2 = _rms_fwd(h1, W['norm_x_g'][l], name=name + "rms_x")
    memn = _rms_fwd(memn_in, W['mem_norm_g'][l], name=name + "rms_mem")
    qx = _mm([(hn2, W['w_xq'][l])], 'nn', bf16, name=name + "mm_xq")
    kvx = _mm([(memn, W['w_xkv'][l])], 'nn', bf16, name=name + "mm_xkv")
    ox = _xattn_fwd(qx, kvx, name=name + "xattn")
    h2 = _mm([(ox, W['w_xo'][l])], 'nn', f32, res=h1, name=name + "mm_xo")
    hn3 = _rms_fwd(h2, W['norm_ffn_g'][l], name=name + "rms_ffn")
    gate, up, act = _ffn_up(hn3, W['w_gate'][l], W['w_up'][l], name=name + "ffn_up")
    h3 = _mm([(act, W['w_down'][l])], 'nn', f32, res=h2, name=name + "mm_down")
    sv.update(xn=xn, proj=proj, cqn=cqn, ckvn=ckvn, qh=qh, kh=kh, vh=vh, oh=oh, lse=lse, an=an, bmb=bmb, cmb=cmb,
              a=a, y=y, xc=xc, z=z, sn=sn, ge=ge, h1=h1, hn2=hn2, memn=memn, qx=qx, kvx=kvx, ox=ox, h2=h2, hn3=hn3,
              gate=gate, up=up, act=act)
    return h3, sv


STACKED = ('w_down', 'w_gate', 'w_up', 'w_xo', 'w_xq', 'w_xkv', 'ssm_w_glu')


def _backward_layer(dh3, sv, memn_in, tabs, W, l, prev, name):
    p = sv['p']
    G = {}
    st = lambda n: (l, prev.get(n))
    G['w_down'] = _mm([(sv['act'], dh3)], 'tn', f32, stack=st('w_down'), name=name + "dw_down")
    dgate, dup = _ffn_bwd_act(dh3, W['w_down'][l], sv['gate'], sv['up'], name=name + "ffn_bwd_act")
    dhn3 = _mm([(dgate, W['w_gate'][l]), (dup, W['w_up'][l])], 'nn', f32, name=name + "mm_dffn")
    G['w_gate'] = _mm([(dgate, sv['hn3'])], 'tn', f32, stack=st('w_gate'), name=name + "dw_gate")
    G['w_up'] = _mm([(dup, sv['hn3'])], 'tn', f32, stack=st('w_up'), name=name + "dw_up")
    dh2, dg = _rms_bwd(sv['h2'], W['norm_ffn_g'][l], dhn3, res=dh3, name=name + "rmsb_ffn")
    G['norm_ffn_g'] = dg[0]
    G['w_xo'] = _mm([(sv['ox'], dh2)], 'tn', f32, stack=st('w_xo'), name=name + "dw_xo")
    dox = _mm([(dh2, W['w_xo'][l])], 'nt', bf16, name=name + "mm_dxo")
    dqx, dkvx = _xattn_bwd(sv['qx'], sv['kvx'], dox, name=name + "xattn_bwd")
    G['w_xq'] = _mm([(sv['hn2'], dqx)], 'tn', f32, stack=st('w_xq'), name=name + "dw_xq")
    G['w_xkv'] = _mm([(sv['memn'], dkvx)], 'tn', f32, stack=st('w_xkv'), name=name + "dw_xkv")
    dhn2 = _mm([(dqx, W['w_xq'][l])], 'nt', f32, name=name + "mm_dxq")
    dmemn = _mm([(dkvx, W['w_xkv'][l])], 'nt', f32, name=name + "mm_dxkv")
    dh1, dg = _rms_bwd(sv['h1'], W['norm_x_g'][l], dhn2, res=dh2, name=name + "rmsb_x")
    G['norm_x_g'] = dg[0]
    _, dg = _rms_bwd(memn_in, W['mem_norm_g'][l], dmemn, name=name + "rmsb_mem")
    G['mem_norm_g'] = dg[0]
    dwo_a = _mm([(sv['an'], dh1)], 'tn', f32, name=name + "dw_out_a")
    dwo_s = _mm([(sv['sn'], dh1)], 'tn', f32, name=name + "dw_out_s")
    G['w_out'] = _unpack_wout(jnp.concatenate([dwo_a, dwo_s], axis=0))
    dmixed = _mm([(dh1, p['wout'])], 'nt', f32, name=name + "mm_dout")
    dy, dz, dg, db = _glu_bwd(sv['y'], sv['z'], dmixed, W['ssm_w_glu'][l], W['ssm_out_g'][l], name=name + "glu_bwd")
    G['ssm_out_g'], G['ssm_b_glu'] = dg[0], db[0]
    G['ssm_w_glu'] = _mm([(sv['ge'], dz)], 'tn', f32, stack=st('ssm_w_glu'), name=name + "dw_glu")
    du, dbm, dcm, da, dd = _ssm_bwd(sv['proj'], dy, sv['xc'], sv['bmb'], sv['cmb'], sv['a'], W['ssm_d'][l],
                                    name=name + "ssm_bwd")
    G['ssm_d'] = dd[0]
    G['ssm_raw'] = (dbm, dcm, da)
    doh, dg, delta = _rms_bwd(sv['oh'], p['attn_g'], dmixed, width=MLA_PAD, n_valid=MLA_WIDTH, delta=True,
                              out_dtype=bf16, name=name + "rmsb_attn")
    G['attn_out_g'] = _unpad_gain(dg[0])
    dqh, dkh, dvh = _attn_bwd(sv['qh'], sv['kh'], sv['vh'], doh, sv['lse'], delta, name=name + "attn_bwd")
    dq, dkv, dkr = _rope_bwd(dqh, dkh, dvh, tabs, name=name + "rope_bwd")
    G['w_uq'] = _unpad_heads(_mm([(sv['cqn'], dq)], 'tn', f32, name=name + "dw_uq"), QK_NOPE + QK_ROPE, _Q_PIECES)
    dwukv = _mm([(sv['ckvn'], dkv)], 'tn', f32, name=name + "dw_ukv")
    G['w_ukv'] = (_unpad_heads(dwukv[:, :MLA_PAD], QK_NOPE + V_HEAD, _K_PIECES)
                  + _unpad_heads(dwukv[:, MLA_PAD:], QK_NOPE + V_HEAD, _V_PIECES))
    dcqn = _mm([(dq, p['wuq'])], 'nt', f32, name=name + "mm_duq")
    dckvn = _mm([(dkv, p['wukv'])], 'nt', f32, name=name + "mm_dukv")
    dcq, dg = _rms_bwd(sv['proj'], W['q_norm_g'][l], dcqn, col0=0, width=Q_LORA, out_dtype=bf16, name=name + "rmsb_q")
    G['q_norm_g'] = dg[0]
    dckv, dg = _rms_bwd(sv['proj'], W['kv_norm_g'][l], dckvn, col0=Q_LORA, width=KV_LORA, out_dtype=bf16,
                        name=name + "rmsb_kv")
    G['kv_norm_g'] = dg[0]
    dproj = jnp.concatenate([dcq, dckv, dkr, du.astype(bf16)], axis=1)
    G['w_in'] = _unpack_win(_mm([(sv['xn'], dproj)], 'tn', f32, name=name + "dw_in"))
    dxn = _mm([(dproj, p['win'])], 'nt', f32, name=name + "mm_din")
    dh0, dg = _rms_bwd(sv['h0'], W['norm_mix_g'][l], dxn, res=dh1, name=name + "rmsb_mix")
    G['norm_mix_g'] = dg[0]
    return dh0, G


def _local_step(x, mem, positions, target, W):
    tabs = _rope_tables(positions)
    ssm_in = [(W['ssm_lambda_re'][l], W['ssm_lambda_im'][l], W['ssm_log_dt'][l], W['ssm_b_re'][l], W['ssm_b_im'][l],
               W['ssm_c_re'][l], W['ssm_c_im'][l]) for l in range(DEPTH)]
    preps = [jax.vjp(_ssm_prep, *ssm_in[l]) for l in range(DEPTH)]
    W = dict(W)
    W['ssm'] = [preps[l][0] for l in range(DEPTH)]
    h = x
    saved = []
    for l in range(DEPTH):
        h, sv = _forward_layer(h, mem, tabs, W, l, f"l{l}_")
        saved.append(sv)
    dh, dgf, loss = _loss_head(h, W['final_norm_g'], target)
    grads = [None] * DEPTH
    prev = {}
    for l in reversed(range(DEPTH)):
        dh, G = _backward_layer(dh, saved[l], mem, tabs, W, l, prev, f"l{l}b_")
        prev = {n: G[n] for n in STACKED}
        dbm, dcm, da = G.pop('ssm_raw')
        names = ['ssm_lambda_re', 'ssm_lambda_im', 'ssm_log_dt', 'ssm_b_re', 'ssm_b_im', 'ssm_c_re', 'ssm_c_im']
        for n, g in zip(names, preps[l][1]((dbm, dcm, da))):
            G[n] = g
        grads[l] = G
    out = {n: grads[0][n] if n in STACKED else jnp.stack([grads[l][n] for l in range(DEPTH)]) for n in grads[0]}
    out['final_norm_g'] = dgf[0]
    return loss[0, 0], dh, out


_HBM = pl.BlockSpec(memory_space=pltpu.HBM)


def _me():
    return lax.axis_index("x"), lax.axis_index("y"), lax.axis_index("c")


def _chip_peers(x, y, c):
    devs = [(1 - x, y, c), (x, 1 - y, c), (1 - x, 1 - y, c)]
    return devs, [2 * d[0] + d[1] for d in devs]


def _allgather_chips(xs, name):
    n = len(xs)

    def body(*refs):
        src, dst = refs[:n], refs[n:2 * n]
        send, recv = refs[2 * n:]
        x, y, c = _me()
        jme = 2 * x + y
        sib = (x, y, 1 - c)
        devs, js = _chip_peers(x, y, c)
        half, other = pl.ds(c, 1), pl.ds(1 - c, 1)
        first = []
        for i in range(n):
            for k in range(3):
                cp = pltpu.make_async_remote_copy(src[i].at[half], dst[i].at[half, pl.ds(jme, 1)], send.at[6 * i + k],
                                                  recv.at[6 * i + k], device_id=devs[k], device_id_type=MESH)
                cp.start()
                first.append(cp)
        passed = []
        for i in range(n):
            for k in range(3):
                slot = dst[i].at[half, pl.ds(js[k], 1)]
                pltpu.make_async_remote_copy(src[i].at[half], slot, send.at[6 * i + k], recv.at[6 * i + k],
                                             device_id=devs[k], device_id_type=MESH).wait_recv()
                fw = pltpu.make_async_remote_copy(slot, slot, send.at[6 * i + 3 + k], recv.at[6 * i + 3 + k],
                                                  device_id=sib, device_id_type=MESH)
                fw.start()
                passed.append(fw)
        for i in range(n):
            for k in range(3):
                slot = dst[i].at[other, pl.ds(js[k], 1)]
                pltpu.make_async_remote_copy(slot, slot, send.at[6 * i + 3 + k], recv.at[6 * i + 3 + k],
                                             device_id=sib, device_id_type=MESH).wait_recv()
        for cp in first + passed:
            cp.wait_send()

    ins = [t.reshape(t.shape[0], 1, *t.shape[1:]) for t in xs]
    outs = [jax.ShapeDtypeStruct((t.shape[0], 4, *t.shape[1:]), t.dtype) for t in xs]
    return _pc(body, in_specs=[_HBM] * n, out_specs=[_HBM] * n, out_shape=outs,
               scratch_shapes=[pltpu.SemaphoreType.DMA((6 * n,)), pltpu.SemaphoreType.DMA((6 * n,))],
               compiler_params=pltpu.CompilerParams(has_side_effects=True), name=name)(*ins)


def _fill_own(gathered, own):
    jme = (2 * lax.axis_index("x") + lax.axis_index("y")).astype(jnp.int32)
    zero = jnp.int32(0)
    return lax.dynamic_update_slice(gathered, own[:, None], (zero, jme, zero, zero))


def _exchange_halves(gs, name):
    n = len(gs)

    def body(*refs):
        src, dst = refs[:n], refs[n:2 * n]
        send, recv = refs[2 * n:]
        x, y, c = _me()
        cps = []
        for i in range(n):
            cp = pltpu.make_async_remote_copy(src[i].at[pl.ds(1 - c, 1)], dst[i], send.at[i], recv.at[i],
                                              device_id=(x, y, 1 - c), device_id_type=MESH)
            cp.start()
            cps.append(cp)
        for cp in cps:
            cp.wait()

    outs = [jax.ShapeDtypeStruct((1, *g.shape[1:]), g.dtype) for g in gs]
    return _pc(body, in_specs=[_HBM] * n, out_specs=[_HBM] * n, out_shape=outs,
               scratch_shapes=[pltpu.SemaphoreType.DMA((n,)), pltpu.SemaphoreType.DMA((n,))],
               compiler_params=pltpu.CompilerParams(has_side_effects=True), name=name)(*gs)


def _scatter_chips(ps, name):
    n = len(ps)

    def body(*refs):
        src, dst = refs[:n], refs[n:2 * n]
        send, recv = refs[2 * n:]
        x, y, c = _me()
        devs, js = _chip_peers(x, y, c)
        cps = []
        for i in range(n):
            for k in range(3):
                cp = pltpu.make_async_remote_copy(src[i].at[pl.ds(js[k], 1)], dst[i].at[k], send.at[3 * i + k],
                                                  recv.at[3 * i + k], device_id=devs[k], device_id_type=MESH)
                cp.start()
                cps.append(cp)
        for cp in cps:
            cp.wait()

    outs = [jax.ShapeDtypeStruct((3, 1, *p.shape[1:]), p.dtype) for p in ps]
    return _pc(body, in_specs=[_HBM] * n, out_specs=[_HBM] * n, out_shape=outs,
               scratch_shapes=[pltpu.SemaphoreType.DMA((3 * n,)), pltpu.SemaphoreType.DMA((3 * n,))],
               compiler_params=pltpu.CompilerParams(has_side_effects=True), name=name)(*ps)


def _swap_sibling(hs, name):
    n = len(hs)

    def body(*refs):
        src, dst = refs[:n], refs[n:2 * n]
        send, recv = refs[2 * n:]
        x, y, c = _me()
        cps = []
        for i in range(n):
            cp = pltpu.make_async_remote_copy(src[i], dst[i], send.at[i], recv.at[i], device_id=(x, y, 1 - c),
                                              device_id_type=MESH)
            cp.start()
            cps.append(cp)
        for cp in cps:
            cp.wait()

    outs = [jax.ShapeDtypeStruct(h.shape, h.dtype) for h in hs]
    return _pc(body, in_specs=[_HBM] * n, out_specs=[_HBM] * n, out_shape=outs,
               scratch_shapes=[pltpu.SemaphoreType.DMA((n,)), pltpu.SemaphoreType.DMA((n,))],
               compiler_params=pltpu.CompilerParams(has_side_effects=True), name=name)(*hs)


ELEMWISE_VMEM_BUDGET = 24 * 1024 * 1024


def _row_tile(r, n, narrays):
    limit = ELEMWISE_VMEM_BUDGET // (2 * 4 * narrays * n)
    best = SUBLANES
    for t in range(16, r + 1, 16):
        if r % t == 0 and t <= limit:
            best = t
    return best


def _add_half(g, r1, cidx, name):
    _, _, r, n = g.shape
    tr = _row_tile(r, n, 3)

    def body(c_ref, g_ref, r_ref, o_ref):
        o_ref[...] = (g_ref[...] + r_ref[...]).astype(GRAD_TRANSIT)

    blk = lambda f: pl.BlockSpec((None, None, tr, n), f)
    gs = pltpu.PrefetchScalarGridSpec(
        num_scalar_prefetch=1, grid=(4, r // tr),
        in_specs=[blk(lambda j, i, c: (c[0], j, i, 0)), blk(lambda j, i, c: (0, j, i, 0))],
        out_specs=pl.BlockSpec((None, tr, n), lambda j, i, c: (j, i, 0)))
    return _pc(body, grid_spec=gs, out_shape=jax.ShapeDtypeStruct((4, r, n), GRAD_TRANSIT),
               compiler_params=_cp(("parallel", "parallel")), name=name)(cidx, g, r1)


def _add_chips(p, r3, jidx, name):
    _, r, n = p.shape
    tr = _row_tile(r, n, 5)

    def body(j_ref, p_ref, a_ref, b_ref, c_ref, o_ref):
        o_ref[...] = ((p_ref[...].astype(f32) + a_ref[...].astype(f32)) + b_ref[...].astype(f32)) + c_ref[...].astype(f32)

    rblk = lambda k: pl.BlockSpec((None, None, tr, n), lambda i, j: (k, 0, i, 0))
    gs = pltpu.PrefetchScalarGridSpec(
        num_scalar_prefetch=1, grid=(r // tr,),
        in_specs=[pl.BlockSpec((None, tr, n), lambda i, j: (j[0], i, 0)), rblk(0), rblk(1), rblk(2)],
        out_specs=pl.BlockSpec((tr, n), lambda i, j: (i, 0)))
    return _pc(body, grid_spec=gs, out_shape=jax.ShapeDtypeStruct((r, n), f32),
               compiler_params=_cp(("parallel",)), name=name)(jidx, p, r3, r3, r3)


def _adamw_halves(w, mine, theirs, m, v, cidx, name):
    _, r, n = w.shape
    tr = _row_tile(r, n, 9)
    c1 = 1.0 / (1.0 - ADAM_B1 ** ADAM_STEP)
    c2 = 1.0 / (1.0 - ADAM_B2 ** ADAM_STEP)

    def body(c_ref, w_ref, a_ref, b_ref, m_ref, v_ref, g_ref, d_ref, mo_ref, vo_ref):
        gv = jnp.where(pl.program_id(0) == c_ref[0], a_ref[...], b_ref[...])
        m2 = ADAM_B1 * m_ref[...] + (1.0 - ADAM_B1) * gv
        v2 = ADAM_B2 * v_ref[...] + (1.0 - ADAM_B2) * (gv * gv)
        g_ref[...] = gv
        d_ref[...] = -ADAM_LR * ((m2 * c1) / (jnp.sqrt(v2 * c2) + ADAM_EPS) + ADAM_WD * w_ref[...])
        mo_ref[...] = m2
        vo_ref[...] = v2

    full = pl.BlockSpec((None, tr, n), lambda l, i, c: (l, i, 0))
    half = pl.BlockSpec((tr, n), lambda l, i, c: (i, 0))
    gs = pltpu.PrefetchScalarGridSpec(num_scalar_prefetch=1, grid=(2, r // tr),
                                      in_specs=[full, half, half, full, full], out_specs=[full] * 4)
    return _pc(body, grid_spec=gs, out_shape=[jax.ShapeDtypeStruct(w.shape, f32)] * 4,
               compiler_params=_cp(("parallel", "parallel")), name=name)(cidx, w, mine, theirs, m, v)


def _adamw_whole(w, g, m, v, name):
    c1 = 1.0 / (1.0 - ADAM_B1 ** ADAM_STEP)
    c2 = 1.0 / (1.0 - ADAM_B2 ** ADAM_STEP)

    def body(w_ref, g_ref, m_ref, v_ref, d_ref, mo_ref, vo_ref):
        gv = g_ref[...]
        m2 = ADAM_B1 * m_ref[...] + (1.0 - ADAM_B1) * gv
        v2 = ADAM_B2 * v_ref[...] + (1.0 - ADAM_B2) * (gv * gv)
        d_ref[...] = -ADAM_LR * ((m2 * c1) / (jnp.sqrt(v2 * c2) + ADAM_EPS) + ADAM_WD * w_ref[...])
        mo_ref[...] = m2
        vo_ref[...] = v2

    return _pc(body, out_shape=[jax.ShapeDtypeStruct(w.shape, f32)] * 3, name=name)(w, g, m, v)


def _full_from_gathered(name, t):
    L, _, r, n = t.shape
    if SHARDED[name] == 1 or name in TRANSPOSED:
        return t.reshape(L, 4 * r, n)
    return t.transpose(0, 2, 1, 3).reshape(L, r, 4 * n)


def _shard_major(name, g):
    L, R, C = g.shape
    if SHARDED[name] == 1 or name in TRANSPOSED:
        return g.reshape(L, 4, R // 4, C)
    return g.reshape(L, R, 4, C // 4).transpose(0, 2, 1, 3)


_SMALL_ROWS = 288


def _pack_small(d):
    flat = jnp.concatenate([d[n].reshape(-1) for n in SMALL])
    total = 2 * 4 * _SMALL_ROWS * LANES
    flat = jnp.concatenate([flat, jnp.zeros((total - flat.shape[0],), f32)])
    return flat.reshape(2, 4, _SMALL_ROWS, LANES)


def _unpack_small(t, like):
    flat = t.reshape(-1)
    out, off = {}, 0
    for n in SMALL:
        sz = math.prod(like[n].shape)
        out[n] = flat[off:off + sz].reshape(like[n].shape)
        off += sz
    return out


def kernel(x, mem, positions, norm_mix_g, w_in, q_norm_g, w_uq, kv_norm_g, w_ukv, ssm_lambda_re, ssm_lambda_im, ssm_log_dt, ssm_b_re, ssm_b_im, ssm_c_re, ssm_c_im, ssm_d, ssm_w_glu, ssm_b_glu, attn_out_g, ssm_out_g, w_out, norm_x_g, mem_norm_g, w_xq, w_xkv, w_xo, norm_ffn_g, w_gate, w_up, w_down, final_norm_g, loss_target, m_norm_mix_g, m_w_in, m_q_norm_g, m_w_uq, m_kv_norm_g, m_w_ukv, m_ssm_lambda_re, m_ssm_lambda_im, m_ssm_log_dt, m_ssm_b_re, m_ssm_b_im, m_ssm_c_re, m_ssm_c_im, m_ssm_d, m_ssm_w_glu, m_ssm_b_glu, m_attn_out_g, m_ssm_out_g, m_w_out, m_norm_x_g, m_mem_norm_g, m_w_xq, m_w_xkv, m_w_xo, m_norm_ffn_g, m_w_gate, m_w_up, m_w_down, m_final_norm_g, v_norm_mix_g, v_w_in, v_q_norm_g, v_w_uq, v_kv_norm_g, v_w_ukv, v_ssm_lambda_re, v_ssm_lambda_im, v_ssm_log_dt, v_ssm_b_re, v_ssm_b_im, v_ssm_c_re, v_ssm_c_im, v_ssm_d, v_ssm_w_glu, v_ssm_b_glu, v_attn_out_g, v_ssm_out_g, v_w_out, v_norm_x_g, v_mem_norm_g, v_w_xq, v_w_xkv, v_w_xo, v_norm_ffn_g, v_w_gate, v_w_up, v_w_down, v_final_norm_g):
    given = dict(locals())
    swap = lambda n, t: jnp.swapaxes(t, *TRANSPOSED[n]) if n in TRANSPOSED else t
    w = {n: swap(n, given[n]) for n in WEIGHTS}
    m = {n: swap(n, given["m_" + n]) for n in WEIGHTS}
    v = {n: swap(n, given["v_" + n]) for n in WEIGHTS}
    big = list(SHARDED)

    shards = [w[n].astype(bf16) for n in big]
    gathered = _allgather_chips(shards, "allgather_weights")
    W = {n: _full_from_gathered(n, _fill_own(t, s)) for n, t, s in zip(big, gathered, shards)}
    W.update({n: w[n] for n in SMALL})

    loss, dx, grads = _local_step(x[0], mem[0], positions[0], loss_target[0], W)
    loss = lax.psum(loss, ("x", "y", "c"))

    cidx = lax.axis_index("c").astype(jnp.int32).reshape(1)
    jidx = (2 * lax.axis_index("x") + lax.axis_index("y")).astype(jnp.int32).reshape(1)
    names = big + ["small"]
    gs = [_shard_major(n, grads[n]) for n in big] + [_pack_small(grads)]
    r1 = _exchange_halves(gs, "grad_exchange_halves")
    ps = [_add_half(g, r, cidx, f"grad_add_half_{n}") for n, g, r in zip(names, gs, r1)]
    r3 = _scatter_chips(ps, "grad_scatter_chips")
    hs = [_add_chips(p, r, jidx, f"grad_add_chips_{n}") for n, p, r in zip(names, ps, r3)]
    ts = _swap_sibling(hs, "grad_swap_sibling")

    out_g, out_d, out_m, out_v = {}, {}, {}, {}
    for n, h, t in zip(big, hs[:-1], ts[:-1]):
        out_g[n], out_d[n], out_m[n], out_v[n] = _adamw_halves(w[n], h, t, m[n], v[n], cidx, f"adamw_{n}")
    both = jnp.stack([hs[-1], ts[-1]])
    piece = jnp.where(cidx[0] == 0, both, both[::-1])
    gsm = _fill_own(_allgather_chips([piece], "allgather_small")[0], piece)
    out_g.update(_unpack_small(gsm, w))
    for n in SMALL:
        two = lambda t: t.reshape(1, -1) if t.ndim == 1 else t
        d_, m_, v_ = _adamw_whole(two(w[n]), two(out_g[n]), two(m[n]), two(v[n]), f"adamw_{n}")
        out_d[n], out_m[n], out_v[n] = (t.reshape(w[n].shape) for t in (d_, m_, v_))

    outs = [[swap(n, d[n]) for n in WEIGHTS] for d in (out_g, out_d, out_m, out_v)]
    return (loss, dx.reshape(x.shape), *outs[0], *outs[1], *outs[2], *outs[3])
```

```python
import functools
import math

import jax
import jax.numpy as jnp
from jax import lax
from jax.experimental import pallas as pl
from jax.experimental.pallas import tpu as pltpu

f32, bf16 = jnp.float32, jnp.bfloat16

D_MODEL = 1024
DEPTH = 2
MLA_HEADS = 8
QK_NOPE = 64
QK_ROPE = 32
V_HEAD = 64
Q_LORA = 256
KV_LORA = 128
MLA_WIDTH = MLA_HEADS * V_HEAD
ROPE_THETA = 10000.0
SSM_WIDTH = 512
SSM_GROUP = 16
SSM_GROUPS = 32
SSM_STATE = 64
IN_WIDTH = Q_LORA + KV_LORA + QK_ROPE + SSM_WIDTH
X_HEADS = 4
X_HEAD_DIM = D_MODEL // X_HEADS
D_FF = 2816
EPS = 1e-6
ADAM_LR, ADAM_B1, ADAM_B2, ADAM_EPS, ADAM_WD, ADAM_STEP = 0.001, 0.9, 0.999, 1e-08, 0.01, 10

LANES = 128
SUBLANES = 8
HEAD_PAD = 128
MLA_PAD = MLA_HEADS * HEAD_PAD
SSM_MACRO = 4
MACRO_CH = SSM_WIDTH // SSM_MACRO
MACRO_ST = SSM_GROUPS // SSM_MACRO * SSM_STATE
VMEM_LIMIT = 56 * 1024 * 1024
GRAD_TRANSIT = bf16

WEIGHTS = ['norm_mix_g', 'w_in', 'q_norm_g', 'w_uq', 'kv_norm_g', 'w_ukv', 'ssm_lambda_re', 'ssm_lambda_im',
           'ssm_log_dt', 'ssm_b_re', 'ssm_b_im', 'ssm_c_re', 'ssm_c_im', 'ssm_d', 'ssm_w_glu', 'ssm_b_glu',
           'attn_out_g', 'ssm_out_g', 'w_out', 'norm_x_g', 'mem_norm_g', 'w_xq', 'w_xkv', 'w_xo', 'norm_ffn_g',
           'w_gate', 'w_up', 'w_down', 'final_norm_g']
SHARDED = {'w_in': 1, 'w_uq': 2, 'w_ukv': 2, 'ssm_w_glu': 1, 'w_out': 1, 'w_xq': 1, 'w_xkv': 2, 'w_xo': 1,
           'w_gate': 2, 'w_up': 2, 'w_down': 1}
SMALL = [n for n in WEIGHTS if n not in SHARDED]
TRANSPOSED = {'w_gate': (1, 2), 'w_up': (1, 2), 'ssm_b_re': (2, 3), 'ssm_b_im': (2, 3)}
MESH = pl.DeviceIdType.MESH


def _pc(body, **kw):
    return pl.pallas_call(body, **kw)


def _pick(n, prefs):
    for p in prefs:
        if n % p == 0:
            return p
    return n


def _cp(sem=None):
    return pltpu.CompilerParams(dimension_semantics=sem, vmem_limit_bytes=VMEM_LIMIT)


_TILE_CANDS = (1024, 1408, 512, 256, 128)
MM_VMEM_BUDGET = 40 * 1024 * 1024


def _mm_tiles(M, K, N, a_bytes, b_bytes, o_bytes, npair, has_res, need_acc):
    best = None
    for tm in _TILE_CANDS:
        for tk in _TILE_CANDS:
            if M % tm or K % tk:
                continue
            vm = npair * (2 * tm * tk * a_bytes + 2 * tk * N * b_bytes) + 2 * tm * N * o_bytes
            vm += tm * N * 4 * (1 + need_acc + 2 * has_res)
            if a_bytes == 4:
                vm += npair * tm * tk * 2
            if b_bytes == 4:
                vm += npair * tk * N * 2
            if vm <= MM_VMEM_BUDGET and (best is None or tm * tk > best[0]):
                best = (tm * tk, tm, tk)
    if best is None:
        return _pick(M, (256, 128)), _pick(K, (256, 128))
    return best[1], best[2]


def _mm(pairs, mode, out_dtype, res=None, stack=None, name="mm"):
    a0, b0 = pairs[0]
    if mode == 'nn':
        (M, K), N = a0.shape, b0.shape[1]
        dims = (((1,), (0,)), ((), ()))
    elif mode == 'nt':
        (M, K), N = a0.shape, b0.shape[0]
        dims = (((1,), (1,)), ((), ()))
    else:
        (K, M), N = a0.shape, b0.shape[1]
        dims = (((0,), (0,)), ((), ()))
    npair = len(pairs)
    has_res = res is not None
    direct = out_dtype == f32
    tm, tk = _mm_tiles(M, K, N, a0.dtype.itemsize, b0.dtype.itemsize, jnp.dtype(out_dtype).itemsize, npair, has_res,
                       not direct)
    nk = K // tk

    def body(*refs):
        ins = refs[:2 * npair]
        res_ref = refs[2 * npair] if has_res else None
        has_prev = stack is not None and stack[1] is not None
        o_ref = refs[2 * npair + has_res + has_prev]
        acc = o_ref if direct else refs[2 * npair + has_res + has_prev + 1]
        k = pl.program_id(1)
        s = None
        for p in range(npair):
            d = lax.dot_general(ins[2 * p][...].astype(bf16), ins[2 * p + 1][...].astype(bf16), dims,
                                preferred_element_type=f32)
            s = d if s is None else s + d

        @pl.when(k == 0)
        def _():
            acc[...] = s

        @pl.when(k > 0)
        def _():
            acc[...] += s

        if has_res or not direct:
            @pl.when(k == nk - 1)
            def _():
                r = acc[...]
                if has_res:
                    r = r + res_ref[...]
                o_ref[...] = r.astype(out_dtype)

    if mode == 'nn':
        a_spec = pl.BlockSpec((tm, tk), lambda i, k: (i, k))
        b_spec = pl.BlockSpec((tk, N), lambda i, k: (k, 0))
    elif mode == 'nt':
        a_spec = pl.BlockSpec((tm, tk), lambda i, k: (i, k))
        b_spec = pl.BlockSpec((N, tk), lambda i, k: (0, k))
    else:
        a_spec = pl.BlockSpec((tk, tm), lambda i, k: (k, i))
        b_spec = pl.BlockSpec((tk, N), lambda i, k: (k, 0))
    o_spec = pl.BlockSpec((tm, N), lambda i, k: (i, 0))
    in_specs = [a_spec, b_spec] * npair + ([o_spec] if has_res else [])
    args = [t for p in pairs for t in p] + ([res] if has_res else [])
    out_shape = jax.ShapeDtypeStruct((M, N), out_dtype)
    aliases = {}
    if stack is not None:
        layer, prev = stack
        o_spec = pl.BlockSpec((None, tm, N), lambda i, k: (layer, i, 0))
        out_shape = jax.ShapeDtypeStruct((DEPTH, M, N), out_dtype)
        if prev is not None:
            in_specs.append(pl.BlockSpec(memory_space=pl.ANY))
            args.append(prev)
            aliases = {len(args) - 1: 0}
    return _pc(body, grid=(M // tm, nk), in_specs=in_specs, out_specs=o_spec, out_shape=out_shape,
               scratch_shapes=[] if direct else [pltpu.VMEM((tm, N), f32)], input_output_aliases=aliases,
               compiler_params=_cp(("parallel", "arbitrary")), name=name)(*args)


def _rms_fwd(x, g, *, col0=0, width=None, n_valid=None, out_dtype=bf16, name="rms_fwd"):
    S = x.shape[0]
    width = width or x.shape[1]
    n_valid = n_valid or width
    ts = _pick(S, (512, 256, 128))
    cb = col0 // width

    def body(x_ref, g_ref, o_ref):
        xv = x_ref[...]
        ms = jnp.sum(xv * xv, axis=-1, keepdims=True) * (1.0 / n_valid)
        o_ref[...] = (xv * lax.rsqrt(ms + EPS) * g_ref[...]).astype(out_dtype)

    return _pc(body, grid=(S // ts,),
               in_specs=[pl.BlockSpec((ts, width), lambda i: (i, cb)), pl.BlockSpec((1, width), lambda i: (0, 0))],
               out_specs=pl.BlockSpec((ts, width), lambda i: (i, 0)),
               out_shape=jax.ShapeDtypeStruct((S, width), out_dtype),
               compiler_params=_cp(("parallel",)), name=name)(x, g.reshape(1, width))


def _rms_bwd(x, g, dy, *, col0=0, dcol0=0, width=None, n_valid=None, res=None, out_dtype=f32, delta=False,
             name="rms_bwd"):
    S = x.shape[0]
    width = width or x.shape[1]
    n_valid = n_valid or width
    ts = _pick(S, (512, 256, 128))
    cb, dcb = col0 // width, dcol0 // width
    has_res = res is not None

    def body(*refs):
        x_ref, g_ref, dy_ref = refs[:3]
        res_ref = refs[3] if has_res else None
        outs = refs[3 + has_res:]
        dx_ref, dg_ref = outs[0], outs[1]
        i = pl.program_id(0)
        xv = x_ref[...]
        gv = g_ref[...]
        dyv = dy_ref[...].astype(f32)
        rstd = lax.rsqrt(jnp.sum(xv * xv, axis=-1, keepdims=True) * (1.0 / n_valid) + EPS)
        xh = xv * rstd
        dxh = dyv * gv
        mean = jnp.sum(dxh * xh, axis=-1, keepdims=True) * (1.0 / n_valid)
        dx = rstd * (dxh - xh * mean)
        if delta:
            d_ref = outs[2]
            for h in range(width // LANES):
                sl = slice(h * LANES, (h + 1) * LANES)
                dsum = jnp.sum(dx[:, sl] * xv[:, sl], axis=-1, keepdims=True)
                d_ref[:, sl] = jnp.broadcast_to(dsum, (ts, LANES))
        if has_res:
            dx = dx + res_ref[...]
        dx_ref[...] = dx.astype(out_dtype)

        @pl.when(i == 0)
        def _():
            dg_ref[...] = jnp.zeros_like(dg_ref)

        dg_ref[...] += jnp.sum(dyv * xh, axis=0, keepdims=True)

    blk = lambda c: pl.BlockSpec((ts, width), lambda i: (i, c))
    in_specs = [blk(cb), pl.BlockSpec((1, width), lambda i: (0, 0)), blk(dcb)] + ([blk(0)] if has_res else [])
    out_specs = [blk(0), pl.BlockSpec((1, width), lambda i: (0, 0))] + ([blk(0)] if delta else [])
    out_shape = [jax.ShapeDtypeStruct((S, width), out_dtype), jax.ShapeDtypeStruct((1, width), f32)] + (
        [jax.ShapeDtypeStruct((S, width), f32)] if delta else [])
    args = [x, g.reshape(1, width), dy] + ([res] if has_res else [])
    return _pc(body, grid=(S // ts,), in_specs=in_specs, out_specs=out_specs, out_shape=out_shape,
               compiler_params=_cp(("arbitrary",)), name=name)(*args)


def _loss_head(h, g, target, name="loss_head"):
    S, D = h.shape
    ts = _pick(S, (512, 256, 128))

    def body(h_ref, g_ref, t_ref, dh_ref, dg_ref, loss_ref):
        i = pl.program_id(0)
        xv = h_ref[...]
        gv = g_ref[...]
        rstd = lax.rsqrt(jnp.sum(xv * xv, axis=-1, keepdims=True) * (1.0 / D) + EPS)
        xh = xv * rstd
        err = xh * gv - t_ref[...]
        dyv = err * (1.0 / D)
        dxh = dyv * gv
        mean = jnp.sum(dxh * xh, axis=-1, keepdims=True) * (1.0 / D)
        dh_ref[...] = rstd * (dxh - xh * mean)

        @pl.when(i == 0)
        def _():
            dg_ref[...] = jnp.zeros_like(dg_ref)
            loss_ref[...] = jnp.zeros_like(loss_ref)

        dg_ref[...] += jnp.sum(dyv * xh, axis=0, keepdims=True)
        part = jnp.sum(jnp.sum(err * err, axis=-1, keepdims=True), axis=0, keepdims=True) * (0.5 / D)
        loss_ref[...] += jnp.broadcast_to(part, (1, LANES))

    blk = pl.BlockSpec((ts, D), lambda i: (i, 0))
    row = pl.BlockSpec((1, D), lambda i: (0, 0))
    return _pc(body, grid=(S // ts,), in_specs=[blk, row, blk],
               out_specs=[blk, row, pl.BlockSpec((1, LANES), lambda i: (0, 0))],
               out_shape=[jax.ShapeDtypeStruct((S, D), f32), jax.ShapeDtypeStruct((1, D), f32),
                          jax.ShapeDtypeStruct((1, LANES), f32)],
               compiler_params=_cp(("arbitrary",)), name=name)(h, g.reshape(1, D), target)


def _rope_apply(x, tc, s1, s2):
    return x * tc + pltpu.roll(x, LANES - 16, 1) * s1 + pltpu.roll(x, 16, 1) * s2


def _rope_apply_t(dy, tc, s1, s2):
    return dy * tc + pltpu.roll(dy * s1, 16, 1) + pltpu.roll(dy * s2, LANES - 16, 1)


def _rope_fwd(q, kv, proj, tabs, name="rope_fwd"):
    S = q.shape[0]
    ts = _pick(S, (512, 256, 128))
    scale = (QK_NOPE + QK_ROPE) ** -0.5

    def body(q_ref, kk_ref, kvv_ref, kr_ref, tc_ref, s1_ref, s2_ref, qh_ref, kh_ref, vh_ref):
        tc, s1, s2 = tc_ref[...], s1_ref[...], s2_ref[...]
        krr = _rope_apply(pltpu.roll(kr_ref[...], QK_NOPE, 1), tc, s1, s2)
        for h in range(MLA_HEADS):
            sl = slice(h * HEAD_PAD, (h + 1) * HEAD_PAD)
            qh_ref[:, sl] = (_rope_apply(q_ref[:, sl], tc, s1, s2) * scale).astype(bf16)
            kh_ref[:, sl] = (kk_ref[:, sl] + krr).astype(bf16)
        vh_ref[...] = kvv_ref[...].astype(bf16)

    wide = lambda c: pl.BlockSpec((ts, MLA_PAD), lambda i: (i, c))
    tab = pl.BlockSpec((ts, LANES), lambda i: (i, 0))
    return _pc(body, grid=(S // ts,),
               in_specs=[wide(0), wide(0), wide(1), pl.BlockSpec((ts, LANES), lambda i: (i, 3)), tab, tab, tab],
               out_specs=[wide(0)] * 3, out_shape=[jax.ShapeDtypeStruct((S, MLA_PAD), bf16)] * 3,
               compiler_params=_cp(("parallel",)), name=name)(q, kv, kv, proj, *tabs)


def _rope_bwd(dqh, dkh, dvh, tabs, name="rope_bwd"):
    S = dqh.shape[0]
    ts = _pick(S, (512, 256, 128))
    scale = (QK_NOPE + QK_ROPE) ** -0.5

    def body(dq_ref, dk_ref, dv_ref, tc_ref, s1_ref, s2_ref, oq_ref, okv_ref, okr_ref):
        tc, s1, s2 = tc_ref[...], s1_ref[...], s2_ref[...]
        ksum = None
        for h in range(MLA_HEADS):
            sl = slice(h * HEAD_PAD, (h + 1) * HEAD_PAD)
            oq_ref[:, sl] = (_rope_apply_t(dq_ref[:, sl], tc, s1, s2) * scale).astype(bf16)
            dk = dk_ref[:, sl]
            okv_ref[:, sl] = dk.astype(bf16)
            ksum = dk if ksum is None else ksum + dk
        okv_ref[:, MLA_PAD:] = dv_ref[...].astype(bf16)
        dkr = pltpu.roll(_rope_apply_t(ksum, tc, s1, s2), LANES - QK_NOPE, 1)
        lane = lax.broadcasted_iota(jnp.int32, (ts, LANES), 1)
        okr_ref[...] = jnp.where(lane < QK_ROPE, dkr, 0.0).astype(bf16)

    wide = pl.BlockSpec((ts, MLA_PAD), lambda i: (i, 0))
    tab = pl.BlockSpec((ts, LANES), lambda i: (i, 0))
    return _pc(body, grid=(S // ts,), in_specs=[wide, wide, wide, tab, tab, tab],
               out_specs=[wide, pl.BlockSpec((ts, 2 * MLA_PAD), lambda i: (i, 0)), tab],
               out_shape=[jax.ShapeDtypeStruct((S, MLA_PAD), bf16), jax.ShapeDtypeStruct((S, 2 * MLA_PAD), bf16),
                          jax.ShapeDtypeStruct((S, LANES), bf16)],
               compiler_params=_cp(("parallel",)), name=name)(dqh, dkh, dvh, *tabs)


ATT_BLK = 1024


def _attn_fwd(qh, kh, vh, plan=None, name="attn_fwd"):
    S = qh.shape[0]
    tq = tk = min(S, ATT_BLK)
    nq, nk = S // tq, S // tk
    npl = plan['n'] if plan else 0

    def body(*refs):
        q_ref, k_ref, v_ref = refs[:3]
        o_ref, lse_ref = refs[3 + npl:5 + npl]
        m_sc, l_sc, acc_sc = refs[5 + 2 * npl:8 + 2 * npl]
        h, i, j = pl.program_id(0), pl.program_id(1), pl.program_id(2)
        if plan:
            pargs = (refs[3:3 + npl], refs[5 + npl:5 + 2 * npl], refs[8 + 2 * npl], refs[9 + 2 * npl])
            first = (i == 0) & (j == 0)
            pl.when((h == 0) & first)(functools.partial(plan['start'], *pargs))
            pl.when((h == (3 * MLA_HEADS) // 4) & first)(functools.partial(plan['forward'], *pargs))
            pl.when((h == MLA_HEADS - 1) & (i == nq - 1) & (j == nk - 1))(functools.partial(plan['finish'], *pargs))

        @pl.when(j == 0)
        def _():
            m_sc[...] = jnp.full_like(m_sc, -1e30)
            l_sc[...] = jnp.zeros_like(l_sc)
            acc_sc[...] = jnp.zeros_like(acc_sc)

        def step(masked):
            s = lax.dot_general(q_ref[...], k_ref[...], (((1,), (1,)), ((), ())), preferred_element_type=f32)
            if masked:
                row = lax.broadcasted_iota(jnp.int32, (tq, tk), 0)
                col = lax.broadcasted_iota(jnp.int32, (tq, tk), 1)
                s = jnp.where(col <= row, s, -1e30)
            m_prev = m_sc[...]
            m_new = jnp.maximum(m_prev, jnp.max(s, axis=-1, keepdims=True))
            alpha = jnp.exp(m_prev - m_new)
            p = jnp.exp(s - m_new)
            l_sc[...] = alpha * l_sc[...] + jnp.sum(p, axis=-1, keepdims=True)
            acc_sc[...] = alpha * acc_sc[...] + jnp.dot(p.astype(bf16), v_ref[...], preferred_element_type=f32)
            m_sc[...] = m_new

        pl.when(j < i)(functools.partial(step, False))
        pl.when(j == i)(functools.partial(step, True))

        @pl.when(j == nk - 1)
        def _():
            l = l_sc[...]
            o_ref[...] = acc_sc[...] / l
            lse_ref[...] = jnp.broadcast_to(m_sc[...] + jnp.log(l), (tq, LANES))

    qspec = pl.BlockSpec((tq, HEAD_PAD), lambda h, i, j: (i, h))
    kspec = pl.BlockSpec((tk, HEAD_PAD), lambda h, i, j: (jnp.minimum(j, i), h))
    anyspec = pl.BlockSpec(memory_space=pl.ANY)
    scratch = [pltpu.VMEM((tq, 1), f32), pltpu.VMEM((tq, 1), f32), pltpu.VMEM((tq, HEAD_PAD), f32)]
    if plan:
        scratch += [pltpu.SemaphoreType.DMA((plan['nsem'],)), pltpu.SemaphoreType.DMA((plan['nsem'],))]
    outs = _pc(body, grid=(MLA_HEADS, nq, nk), in_specs=[qspec, kspec, kspec] + [anyspec] * npl,
               out_specs=[qspec, qspec] + [anyspec] * npl,
               out_shape=[jax.ShapeDtypeStruct((S, MLA_PAD), f32)] * 2 + (plan['outs'] if plan else []),
               scratch_shapes=scratch,
               compiler_params=_cp(("arbitrary", "arbitrary", "arbitrary") if plan else ("parallel", "parallel", "arbitrary")),
               name=name)(qh, kh, vh, *(plan['ins'] if plan else []))
    return outs[0], outs[1], outs[2:]


def _attn_bwd(qh, kh, vh, do, lse, delta, name="attn_bwd"):
    S = qh.shape[0]
    tq = tk = min(S, ATT_BLK)
    nq, nk = S // tq, S // tk

    def body(q_ref, k_ref, v_ref, do_ref, lse_ref, dl_ref, dq_ref, dk_ref, dv_ref):
        j, i = pl.program_id(1), pl.program_id(2)

        @pl.when((j == 0) & (i == 0))
        def _():
            dq_ref[...] = jnp.zeros_like(dq_ref)

        @pl.when(i == 0)
        def _():
            dk_ref[...] = jnp.zeros_like(dk_ref)
            dv_ref[...] = jnp.zeros_like(dv_ref)

        def step(masked):
            nt = (((1,), (1,)), ((), ()))
            tn = (((0,), (0,)), ((), ()))
            qv, kv_, dov = q_ref[...], k_ref[...], do_ref[...]
            s = lax.dot_general(qv, kv_, nt, preferred_element_type=f32)
            p = jnp.exp(s - lse_ref[:, :1])
            if masked:
                row = lax.broadcasted_iota(jnp.int32, (tq, tk), 0)
                col = lax.broadcasted_iota(jnp.int32, (tq, tk), 1)
                p = jnp.where(col <= row, p, 0.0)
            dp = lax.dot_general(dov, v_ref[...], nt, preferred_element_type=f32)
            ds = (p * (dp - dl_ref[:, :1])).astype(bf16)
            dv_ref[...] += lax.dot_general(p.astype(bf16), dov, tn, preferred_element_type=f32)
            dk_ref[...] += lax.dot_general(ds, qv, tn, preferred_element_type=f32)
            rows = pl.ds(pl.multiple_of(i * tq, tq), tq)
            dq_ref[rows, :] += jnp.dot(ds, kv_, preferred_element_type=f32)

        pl.when(i > j)(functools.partial(step, False))
        pl.when(i == j)(functools.partial(step, True))

    qspec = pl.BlockSpec((tq, HEAD_PAD), lambda h, j, i: (jnp.maximum(i, j), h))
    kspec = pl.BlockSpec((tk, HEAD_PAD), lambda h, j, i: (j, h))
    colspec = pl.BlockSpec((S, HEAD_PAD), lambda h, j, i: (0, h))
    return _pc(body, grid=(MLA_HEADS, nk, nq), in_specs=[qspec, kspec, kspec, qspec, qspec, qspec],
               out_specs=[colspec, kspec, kspec], out_shape=[jax.ShapeDtypeStruct((S, MLA_PAD), f32)] * 3,
               compiler_params=_cp(("parallel", "arbitrary", "arbitrary")), name=name)(qh, kh, vh, do, lse, delta)


def _xattn_fwd(q, kv, name="xattn_fwd"):
    S = q.shape[0]
    M = kv.shape[0]
    tq = _pick(S, (256, 128))
    scale = X_HEAD_DIM ** -0.5

    def body(q_ref, kv_ref, o_ref):
        for h in range(X_HEADS):
            sl = slice(h * X_HEAD_DIM, (h + 1) * X_HEAD_DIM)
            k = kv_ref[:, sl]
            v = kv_ref[:, D_MODEL + h * X_HEAD_DIM:D_MODEL + (h + 1) * X_HEAD_DIM]
            s = lax.dot_general(q_ref[:, sl], k, (((1,), (1,)), ((), ())), preferred_element_type=f32) * scale
            e = jnp.exp(s - jnp.max(s, axis=-1, keepdims=True))
            p = e / jnp.sum(e, axis=-1, keepdims=True)
            o_ref[:, sl] = jnp.dot(p.astype(bf16), v, preferred_element_type=f32).astype(bf16)

    blk = pl.BlockSpec((tq, D_MODEL), lambda i: (i, 0))
    return _pc(body, grid=(S // tq,), in_specs=[blk, pl.BlockSpec((M, 2 * D_MODEL), lambda i: (0, 0))],
               out_specs=blk, out_shape=jax.ShapeDtypeStruct((S, D_MODEL), bf16),
               compiler_params=_cp(("parallel",)), name=name)(q, kv)


def _xattn_bwd(q, kv, do, name="xattn_bwd"):
    S = q.shape[0]
    M = kv.shape[0]
    tq = _pick(S, (256, 128))
    scale = X_HEAD_DIM ** -0.5

    def body(q_ref, kv_ref, do_ref, dq_ref, dkv_ref):
        i = pl.program_id(0)

        @pl.when(i == 0)
        def _():
            dkv_ref[...] = jnp.zeros_like(dkv_ref)

        nt = (((1,), (1,)), ((), ()))
        tn = (((0,), (0,)), ((), ()))
        for h in range(X_HEADS):
            sl = slice(h * X_HEAD_DIM, (h + 1) * X_HEAD_DIM)
            vsl = slice(D_MODEL + h * X_HEAD_DIM, D_MODEL + (h + 1) * X_HEAD_DIM)
            k, v, qv, dov = kv_ref[:, sl], kv_ref[:, vsl], q_ref[:, sl], do_ref[:, sl]
            s = lax.dot_general(qv, k, nt, preferred_element_type=f32) * scale
            e = jnp.exp(s - jnp.max(s, axis=-1, keepdims=True))
            p = e / jnp.sum(e, axis=-1, keepdims=True)
            dp = lax.dot_general(dov, v, nt, preferred_element_type=f32)
            ds = (p * (dp - jnp.sum(dp * p, axis=-1, keepdims=True)) * scale).astype(bf16)
            dq_ref[:, sl] = jnp.dot(ds, k, preferred_element_type=f32).astype(bf16)
            dkv_ref[:, sl] += lax.dot_general(ds, qv, tn, preferred_element_type=f32)
            dkv_ref[:, vsl] += lax.dot_general(p.astype(bf16), dov, tn, preferred_element_type=f32)

    blk = pl.BlockSpec((tq, D_MODEL), lambda i: (i, 0))
    full = pl.BlockSpec((M, 2 * D_MODEL), lambda i: (0, 0))
    return _pc(body, grid=(S // tq,), in_specs=[blk, full, blk], out_specs=[blk, full],
               out_shape=[jax.ShapeDtypeStruct((S, D_MODEL), bf16), jax.ShapeDtypeStruct((M, 2 * D_MODEL), f32)],
               compiler_params=_cp(("arbitrary",)), name=name)(q, kv, do)


ST_CHUNKS = 1


def _apow_init(a_ref, ap_ref, bp_ref, seg):
    P = MACRO_ST
    ar, ai = a_ref[:, :P], a_ref[:, P:]
    pr, pi = ar, ai
    for r in range(seg):
        ap_ref[r:r + 1, :P] = pr
        ap_ref[r:r + 1, P:] = pi
        if r < seg - 1:
            pr, pi = pr * ar - pi * ai, pr * ai + pi * ar
    br, bi = pr, pi
    for k in range(SUBLANES):
        bp_ref[k:k + 1, :P] = pr
        bp_ref[k:k + 1, P:] = pi
        pr, pi = pr * br - pi * bi, pr * bi + pi * br


def _segment_perm(tS):
    seg = tS // SUBLANES
    rows = jnp.arange(tS)
    src = (rows % SUBLANES) * seg + rows // SUBLANES
    return (src[:, None] == jnp.arange(tS)[None, :]).astype(f32)


def _unpermute_rows(pt, v):
    hi = v.astype(bf16)
    r1 = v - hi.astype(f32)
    mid = r1.astype(bf16)
    lo = (r1 - mid.astype(f32)).astype(bf16)
    out = jnp.dot(pt, jnp.concatenate([hi, mid, lo], axis=1), preferred_element_type=f32)
    w = v.shape[1]
    return (out[:, :w] + out[:, w:2 * w]) + out[:, 2 * w:]


def _scan_block(sc_ref, ap_ref, bp_ref, carry_ref, e_ref, seg, reverse):
    P = MACRO_ST
    sgn = -1.0 if reverse else 1.0
    CH = P // ST_CHUNKS
    rid = lax.broadcasted_iota(jnp.int32, (SUBLANES, CH), 0)
    for c in range(ST_CHUNKS):
        lr, li = slice(c * CH, (c + 1) * CH), slice(P + c * CH, P + (c + 1) * CH)
        ar, ai = ap_ref[0:1, lr], sgn * ap_ref[0:1, li]
        xr = xi = None
        for i in range(seg):
            r = seg - 1 - i if reverse else i
            rows = slice(SUBLANES * r, SUBLANES * (r + 1))
            sr, si = sc_ref[rows, lr], sc_ref[rows, li]
            if i == 0:
                xr, xi = sr, si
            else:
                xr, xi = ar * xr - ai * xi + sr, ar * xi + ai * xr + si
                sc_ref[rows, lr] = xr
                sc_ref[rows, li] = xi
        for sh in (1, 2, 4):
            pr, pi = bp_ref[sh - 1:sh, lr], sgn * bp_ref[sh - 1:sh, li]
            if reverse:
                tr = jnp.where(rid < SUBLANES - sh, pltpu.roll(xr, SUBLANES - sh, 0), 0.0)
                ti = jnp.where(rid < SUBLANES - sh, pltpu.roll(xi, SUBLANES - sh, 0), 0.0)
            else:
                tr = jnp.where(rid >= sh, pltpu.roll(xr, sh, 0), 0.0)
                ti = jnp.where(rid >= sh, pltpu.roll(xi, sh, 0), 0.0)
            xr, xi = xr + pr * tr - pi * ti, xi + pr * ti + pi * tr
        if reverse:
            bpr = jnp.zeros((SUBLANES, CH), f32)
            bpi = jnp.zeros((SUBLANES, CH), f32)
            for r in range(SUBLANES):
                bpr = jnp.where(rid == r, bp_ref[SUBLANES - 1 - r:SUBLANES - r, lr], bpr)
                bpi = jnp.where(rid == r, -bp_ref[SUBLANES - 1 - r:SUBLANES - r, li], bpi)
        else:
            bpr, bpi = bp_ref[:, lr], bp_ref[:, li]
        cr, cim = carry_ref[:, lr], carry_ref[:, li]
        xr, xi = xr + bpr * cr - bpi * cim, xi + bpr * cim + bpi * cr
        edge = 0 if reverse else SUBLANES - 1
        carry_ref[:, lr] = jnp.sum(jnp.where(rid == edge, xr, 0.0), axis=0, keepdims=True)
        carry_ref[:, li] = jnp.sum(jnp.where(rid == edge, xi, 0.0), axis=0, keepdims=True)
        if reverse:
            er = jnp.where(rid == SUBLANES - 1, cr, pltpu.roll(xr, SUBLANES - 1, 0))
            ei = jnp.where(rid == SUBLANES - 1, cim, pltpu.roll(xi, SUBLANES - 1, 0))
        else:
            er = jnp.where(rid == 0, cr, pltpu.roll(xr, 1, 0))
            ei = jnp.where(rid == 0, cim, pltpu.roll(xi, 1, 0))
        if e_ref is not None:
            e_ref[:, lr] = er
            e_ref[:, li] = ei
        for i in range(seg):
            r = seg - 1 - i if reverse else i
            rows = slice(SUBLANES * r, SUBLANES * (r + 1))
            pr, pi = ap_ref[i:i + 1, lr], sgn * ap_ref[i:i + 1, li]
            sc_ref[rows, lr] += pr * er - pi * ei
            sc_ref[rows, li] += pr * ei + pi * er


def _ssm_fwd(proj, bm, cm, a, d, name="ssm_fwd"):
    S = proj.shape[0]
    tS = _pick(S, (256, 128))
    nb = S // tS
    P2 = 2 * MACRO_ST
    seg = tS // SUBLANES
    ucol0 = (D_MODEL - SSM_WIDTH) // MACRO_CH

    perm = _segment_perm(tS)

    def body(u_ref, b_ref, c_ref, a_ref, d_ref, pm_ref, pt_ref, y_ref, xc_ref, bu_sc, ap_sc, bp_sc, car_sc):
        t = pl.program_id(1)

        @pl.when(t == 0)
        def _():
            _apow_init(a_ref, ap_sc, bp_sc, seg)
            car_sc[...] = jnp.zeros_like(car_sc)

        uv = u_ref[...]
        up = jnp.dot(pm_ref[...], uv.astype(bf16), preferred_element_type=f32).astype(bf16)
        bu_sc[...] = jnp.dot(up, b_ref[...], preferred_element_type=f32)
        xc_ref[...] = car_sc[...]
        _scan_block(bu_sc, ap_sc, bp_sc, car_sc, None, seg, False)
        yp = jnp.dot(bu_sc[...].astype(bf16), c_ref[...], preferred_element_type=f32)
        y_ref[...] = _unpermute_rows(pt_ref[...], yp) + d_ref[...] * uv

    sq = pl.BlockSpec((tS, tS), lambda m, t: (0, 0))
    return _pc(body, grid=(SSM_MACRO, nb),
               in_specs=[pl.BlockSpec((tS, MACRO_CH), lambda m, t: (t, ucol0 + m)),
                         pl.BlockSpec((None, MACRO_CH, P2), lambda m, t: (m, 0, 0)),
                         pl.BlockSpec((None, P2, MACRO_CH), lambda m, t: (m, 0, 0)),
                         pl.BlockSpec((None, 1, P2), lambda m, t: (m, 0, 0)),
                         pl.BlockSpec((1, MACRO_CH), lambda m, t: (0, m)), sq, sq],
               out_specs=[pl.BlockSpec((tS, MACRO_CH), lambda m, t: (t, m)),
                          pl.BlockSpec((None, None, 1, P2), lambda m, t: (m, t, 0, 0))],
               out_shape=[jax.ShapeDtypeStruct((S, SSM_WIDTH), f32), jax.ShapeDtypeStruct((SSM_MACRO, nb, 1, P2), f32)],
               scratch_shapes=[pltpu.VMEM((tS, P2), f32), pltpu.VMEM((seg, P2), f32),
                               pltpu.VMEM((SUBLANES, P2), f32), pltpu.VMEM((1, P2), f32)],
               compiler_params=_cp(("arbitrary", "arbitrary")), name=name)(
        proj, bm, cm, a, d.reshape(1, SSM_WIDTH), perm.astype(bf16), perm.T.astype(bf16))


def _ssm_bwd(proj, dy, xc, bm, cm, a, d, name="ssm_bwd"):
    S = proj.shape[0]
    tS = _pick(S, (256, 128))
    nb = S // tS
    P = MACRO_ST
    P2 = 2 * P
    seg = tS // SUBLANES
    ucol0 = (D_MODEL - SSM_WIDTH) // MACRO_CH

    perm = _segment_perm(tS)

    def body(u_ref, dy_ref, xc_ref, b_ref, c_ref, a_ref, d_ref, pm_ref, pt_ref, du_ref, db_ref, dc_ref, da_ref, dd_ref,
             x_sc, g_sc, ap_sc, bp_sc, e_sc, xcar_sc, gcar_sc):
        t = pl.program_id(1)

        @pl.when(t == 0)
        def _():
            _apow_init(a_ref, ap_sc, bp_sc, seg)
            gcar_sc[...] = jnp.zeros_like(gcar_sc)
            db_ref[...] = jnp.zeros_like(db_ref)
            dc_ref[...] = jnp.zeros_like(dc_ref)
            da_ref[...] = jnp.zeros_like(da_ref)
            dd_ref[...] = jnp.zeros_like(dd_ref)

        nt = (((1,), (1,)), ((), ()))
        tn = (((0,), (0,)), ((), ()))
        uv = u_ref[...]
        dyv = dy_ref[...]
        pm = pm_ref[...]
        ub = jnp.dot(pm, uv.astype(bf16), preferred_element_type=f32).astype(bf16)
        dyb = jnp.dot(pm, dyv.astype(bf16), preferred_element_type=f32).astype(bf16)
        x_sc[...] = jnp.dot(ub, b_ref[...], preferred_element_type=f32)
        xcar_sc[...] = xc_ref[...]
        _scan_block(x_sc, ap_sc, bp_sc, xcar_sc, e_sc, seg, False)
        g_sc[...] = lax.dot_general(dyb, c_ref[...], nt, preferred_element_type=f32)
        _scan_block(g_sc, ap_sc, bp_sc, gcar_sc, None, seg, True)
        xv = x_sc[...]
        gv = g_sc[...]
        gb = gv.astype(bf16)
        dc_ref[...] += lax.dot_general(xv.astype(bf16), dyb, tn, preferred_element_type=f32)
        db_ref[...] += lax.dot_general(ub, gb, tn, preferred_element_type=f32)
        dup = lax.dot_general(gb, b_ref[...], nt, preferred_element_type=f32)
        du_ref[...] = _unpermute_rows(pt_ref[...], dup) + d_ref[...] * dyv
        dd_ref[...] += jnp.sum(dyv * uv, axis=0, keepdims=True)
        xp = jnp.concatenate([e_sc[...], xv[:tS - SUBLANES]], axis=0)
        xpr, xpi, ggr, ggi = xp[:, :P], xp[:, P:], gv[:, :P], gv[:, P:]
        da_ref[:, :P] += jnp.sum(ggr * xpr + ggi * xpi, axis=0, keepdims=True)
        da_ref[:, P:] += jnp.sum(ggi * xpr - ggr * xpi, axis=0, keepdims=True)

    rev = lambda t: nb - 1 - t
    return _pc(body, grid=(SSM_MACRO, nb),
               in_specs=[pl.BlockSpec((tS, MACRO_CH), lambda m, t: (rev(t), ucol0 + m)),
                         pl.BlockSpec((tS, MACRO_CH), lambda m, t: (rev(t), m)),
                         pl.BlockSpec((None, None, 1, P2), lambda m, t: (m, rev(t), 0, 0)),
                         pl.BlockSpec((None, MACRO_CH, P2), lambda m, t: (m, 0, 0)),
                         pl.BlockSpec((None, P2, MACRO_CH), lambda m, t: (m, 0, 0)),
                         pl.BlockSpec((None, 1, P2), lambda m, t: (m, 0, 0)),
                         pl.BlockSpec((1, MACRO_CH), lambda m, t: (0, m)),
                         pl.BlockSpec((tS, tS), lambda m, t: (0, 0)), pl.BlockSpec((tS, tS), lambda m, t: (0, 0))],
               out_specs=[pl.BlockSpec((tS, MACRO_CH), lambda m, t: (rev(t), m)),
                          pl.BlockSpec((None, MACRO_CH, P2), lambda m, t: (m, 0, 0)),
                          pl.BlockSpec((None, P2, MACRO_CH), lambda m, t: (m, 0, 0)),
                          pl.BlockSpec((None, 1, P2), lambda m, t: (m, 0, 0)),
                          pl.BlockSpec((1, MACRO_CH), lambda m, t: (0, m))],
               out_shape=[jax.ShapeDtypeStruct((S, SSM_WIDTH), f32),
                          jax.ShapeDtypeStruct((SSM_MACRO, MACRO_CH, P2), f32),
                          jax.ShapeDtypeStruct((SSM_MACRO, P2, MACRO_CH), f32),
                          jax.ShapeDtypeStruct((SSM_MACRO, 1, P2), f32),
                          jax.ShapeDtypeStruct((1, SSM_WIDTH), f32)],
               scratch_shapes=[pltpu.VMEM((tS, P2), f32), pltpu.VMEM((tS, P2), f32),
                               pltpu.VMEM((seg, P2), f32), pltpu.VMEM((SUBLANES, P2), f32), pltpu.VMEM((SUBLANES, P2), f32),
                               pltpu.VMEM((1, P2), f32), pltpu.VMEM((1, P2), f32)],
               compiler_params=_cp(("arbitrary", "arbitrary")), name=name)(
        proj, dy, xc, bm, cm, a, d.reshape(1, SSM_WIDTH), perm.astype(bf16), perm.T.astype(bf16))


_GELU_K = math.sqrt(2.0 / math.pi)
_GELU_C = 0.044715


def _glu_fwd(y, w, b, g, name="glu_fwd"):
    S, W = y.shape
    ts = _pick(S, (512, 256, 128))

    def body(y_ref, w_ref, b_ref, g_ref, z_ref, sn_ref, ge_ref):
        yv = y_ref[...]
        cdf = 0.5 * (1.0 + jnp.tanh(_GELU_K * (yv + _GELU_C * (yv * yv * yv))))
        ge = (yv * cdf).astype(bf16)
        z = jnp.dot(ge, w_ref[...], preferred_element_type=f32) + b_ref[...]
        s = yv * jax.nn.sigmoid(z)
        rstd = lax.rsqrt(jnp.sum(s * s, axis=-1, keepdims=True) * (1.0 / W) + EPS)
        z_ref[...] = z
        sn_ref[...] = (s * rstd * g_ref[...]).astype(bf16)
        ge_ref[...] = ge

    blk = pl.BlockSpec((ts, W), lambda i: (i, 0))
    row = pl.BlockSpec((1, W), lambda i: (0, 0))
    return _pc(body, grid=(S // ts,), in_specs=[blk, pl.BlockSpec((W, W), lambda i: (0, 0)), row, row],
               out_specs=[blk, blk, blk],
               out_shape=[jax.ShapeDtypeStruct((S, W), f32), jax.ShapeDtypeStruct((S, W), bf16),
                          jax.ShapeDtypeStruct((S, W), bf16)],
               compiler_params=_cp(("parallel",)), name=name)(y, w, b.reshape(1, W), g.reshape(1, W))


def _glu_bwd(y, z, dmixed, w, g, name="glu_bwd"):
    S, W = y.shape
    ts = _pick(S, (512, 256, 128))
    dcb = MLA_PAD // W

    def body(y_ref, z_ref, dsn_ref, w_ref, g_ref, dy_ref, dz_ref, dg_ref, db_ref):
        i = pl.program_id(0)
        yv, zv, gv = y_ref[...], z_ref[...], g_ref[...]
        sig = jax.nn.sigmoid(zv)
        s = yv * sig
        rstd = lax.rsqrt(jnp.sum(s * s, axis=-1, keepdims=True) * (1.0 / W) + EPS)
        sh = s * rstd
        dsn = dsn_ref[...]
        dsh = dsn * gv
        ds = rstd * (dsh - sh * (jnp.sum(dsh * sh, axis=-1, keepdims=True) * (1.0 / W)))
        dz = ds * s * (1.0 - sig)
        dzb = dz.astype(bf16)
        dge = lax.dot_general(dzb, w_ref[...], (((1,), (1,)), ((), ())), preferred_element_type=f32)
        t = jnp.tanh(_GELU_K * (yv + _GELU_C * (yv * yv * yv)))
        dgelu = 0.5 * (1.0 + t) + 0.5 * yv * (1.0 - t * t) * _GELU_K * (1.0 + 3.0 * _GELU_C * yv * yv)
        dy_ref[...] = ds * sig + dge * dgelu
        dz_ref[...] = dzb

        @pl.when(i == 0)
        def _():
            dg_ref[...] = jnp.zeros_like(dg_ref)
            db_ref[...] = jnp.zeros_like(db_ref)

        dg_ref[...] += jnp.sum(dsn * sh, axis=0, keepdims=True)
        db_ref[...] += jnp.sum(dz, axis=0, keepdims=True)

    blk = pl.BlockSpec((ts, W), lambda i: (i, 0))
    row = pl.BlockSpec((1, W), lambda i: (0, 0))
    return _pc(body, grid=(S // ts,),
               in_specs=[blk, blk, pl.BlockSpec((ts, W), lambda i: (i, dcb)), pl.BlockSpec((W, W), lambda i: (0, 0)), row],
               out_specs=[blk, blk, row, row],
               out_shape=[jax.ShapeDtypeStruct((S, W), f32), jax.ShapeDtypeStruct((S, W), bf16),
                          jax.ShapeDtypeStruct((1, W), f32), jax.ShapeDtypeStruct((1, W), f32)],
               compiler_params=_cp(("arbitrary",)), name=name)(y, z, dmixed, w, g.reshape(1, W))


def _ffn_up(hn, wg, wu, name="ffn_up"):
    S, K = hn.shape
    F = wg.shape[0]
    tm, tn = _pick(S, (512, 256, 128)), _pick(F, (1408, 256, 128))

    def body(h_ref, wg_ref, wu_ref, g_ref, u_ref, a_ref):
        hv = h_ref[...]
        nt = (((1,), (1,)), ((), ()))
        gv = lax.dot_general(hv, wg_ref[...], nt, preferred_element_type=f32)
        uv = lax.dot_general(hv, wu_ref[...], nt, preferred_element_type=f32)
        g_ref[...] = gv
        u_ref[...] = uv
        a_ref[...] = (gv * jax.nn.sigmoid(gv) * uv).astype(bf16)

    wspec = pl.BlockSpec((tn, K), lambda i, j: (j, 0))
    ospec = pl.BlockSpec((tm, tn), lambda i, j: (i, j))
    return _pc(body, grid=(S // tm, F // tn), in_specs=[pl.BlockSpec((tm, K), lambda i, j: (i, 0)), wspec, wspec],
               out_specs=[ospec] * 3,
               out_shape=[jax.ShapeDtypeStruct((S, F), f32), jax.ShapeDtypeStruct((S, F), f32),
                          jax.ShapeDtypeStruct((S, F), bf16)],
               compiler_params=_cp(("parallel", "parallel")), name=name)(hn, wg, wu)


def _ffn_bwd_act(dh, wd, gate, up, name="ffn_bwd_act"):
    S, K = dh.shape
    F = wd.shape[0]
    tm, tn = _pick(S, (512, 256, 128)), _pick(F, (1408, 256, 128))

    def body(dh_ref, wd_ref, g_ref, u_ref, dg_ref, du_ref):
        dact = lax.dot_general(dh_ref[...].astype(bf16), wd_ref[...], (((1,), (1,)), ((), ())),
                               preferred_element_type=f32)
        gv, uv = g_ref[...], u_ref[...]
        sig = jax.nn.sigmoid(gv)
        dg_ref[...] = (dact * uv * (sig * (1.0 + gv * (1.0 - sig)))).astype(bf16)
        du_ref[...] = (dact * (gv * sig)).astype(bf16)

    ospec = pl.BlockSpec((tm, tn), lambda i, j: (i, j))
    return _pc(body, grid=(S // tm, F // tn),
               in_specs=[pl.BlockSpec((tm, K), lambda i, j: (i, 0)), pl.BlockSpec((tn, K), lambda i, j: (j, 0)),
                         ospec, ospec],
               out_specs=[ospec] * 2, out_shape=[jax.ShapeDtypeStruct((S, F), bf16)] * 2,
               compiler_params=_cp(("parallel", "parallel")), name=name)(dh, wd, gate, up)


def _pad_heads(w, per_head, pieces):
    K = w.shape[0]
    w3 = w.reshape(K, MLA_HEADS, per_head)
    out = jnp.zeros((K, MLA_HEADS, HEAD_PAD), w.dtype)
    for s0, s1, d0 in pieces:
        out = out.at[:, :, d0:d0 + (s1 - s0)].set(w3[:, :, s0:s1])
    return out.reshape(K, MLA_PAD)


def _unpad_heads(wp, per_head, pieces):
    K = wp.shape[0]
    w3 = wp.reshape(K, MLA_HEADS, HEAD_PAD)
    out = jnp.zeros((K, MLA_HEADS, per_head), wp.dtype)
    for s0, s1, d0 in pieces:
        out = out.at[:, :, s0:s1].set(w3[:, :, d0:d0 + (s1 - s0)])
    return out.reshape(K, MLA_HEADS * per_head)


_Q_PIECES = [(0, QK_NOPE + QK_ROPE, 0)]
_K_PIECES = [(0, QK_NOPE, 0)]
_V_PIECES = [(QK_NOPE, QK_NOPE + V_HEAD, 0)]
_KR0 = Q_LORA + KV_LORA


def _pack_win(w):
    z = jnp.zeros((w.shape[0], LANES - QK_ROPE), w.dtype)
    return jnp.concatenate([w[:, :_KR0 + QK_ROPE], z, w[:, _KR0 + QK_ROPE:]], axis=1)


def _unpack_win(wp):
    return jnp.concatenate([wp[:, :_KR0 + QK_ROPE], wp[:, _KR0 + LANES:]], axis=1)


def _pack_wout(w):
    wa = w[:MLA_WIDTH].reshape(MLA_HEADS, V_HEAD, D_MODEL)
    wa = jnp.concatenate([wa, jnp.zeros_like(wa)], axis=1).reshape(MLA_PAD, D_MODEL)
    return jnp.concatenate([wa, w[MLA_WIDTH:]], axis=0)


def _unpack_wout(wp):
    wa = wp[:MLA_PAD].reshape(MLA_HEADS, HEAD_PAD, D_MODEL)[:, :V_HEAD].reshape(MLA_WIDTH, D_MODEL)
    return jnp.concatenate([wa, wp[MLA_PAD:]], axis=0)


def _pad_gain(g):
    g2 = g.reshape(MLA_HEADS, V_HEAD)
    return jnp.concatenate([g2, jnp.zeros_like(g2)], axis=1).reshape(MLA_PAD)


def _unpad_gain(gp):
    return gp.reshape(MLA_HEADS, HEAD_PAD)[:, :V_HEAD].reshape(MLA_WIDTH)


def _ssm_prep(lam_re, lam_im, log_dt, b_re, b_im, c_re, c_im):
    lam = lax.complex(lam_re, lam_im)
    dt = jnp.exp(log_dt)[:, None]
    a_bar = jnp.exp(lam * dt)
    b_bar = ((a_bar - 1.0) / lam)[:, None, :] * lax.complex(b_re, b_im)
    G8 = SSM_GROUPS // SSM_MACRO
    eye = jnp.eye(G8, dtype=f32)

    def bmat(part):
        p4 = part.reshape(SSM_MACRO, G8, SSM_GROUP, SSM_STATE)
        return jnp.einsum('mgcp,gh->mgchp', p4, eye).reshape(SSM_MACRO, MACRO_CH, MACRO_ST)

    def cmat(part):
        p4 = part.reshape(SSM_MACRO, G8, SSM_GROUP, SSM_STATE)
        return jnp.einsum('mgcp,gh->mgphc', p4, eye).reshape(SSM_MACRO, MACRO_ST, MACRO_CH)

    bm = jnp.concatenate([bmat(b_bar.real), bmat(b_bar.imag)], axis=2)
    cm = jnp.concatenate([cmat(c_re), -cmat(c_im)], axis=1)
    a4 = a_bar.reshape(SSM_MACRO, 1, MACRO_ST)
    a = jnp.concatenate([a4.real, a4.imag], axis=2)
    return bm, cm, a


def _rope_tables(positions):
    freqs = ROPE_THETA ** (-jnp.arange(0, QK_ROPE, 2, dtype=f32) / QK_ROPE)
    ang = positions.astype(f32)[:, None] * freqs
    cos, sin = jnp.cos(ang), jnp.sin(ang)
    S = positions.shape[0]
    half = QK_ROPE // 2
    one, zero = jnp.ones((S, QK_NOPE), f32), jnp.zeros((S, half), f32)
    z64, z32 = jnp.zeros((S, QK_NOPE), f32), jnp.zeros((S, LANES - QK_NOPE - QK_ROPE), f32)
    tc = jnp.concatenate([one, cos, cos, z32], axis=1)
    s1 = jnp.concatenate([z64, -sin, zero, z32], axis=1)
    s2 = jnp.concatenate([z64, zero, sin, z32], axis=1)
    return tc, s1, s2


def _layer_params(W, l):
    p = {}
    p['win'] = _pack_win(W['w_in'][l])
    p['wuq'] = _pad_heads(W['w_uq'][l], QK_NOPE + QK_ROPE, _Q_PIECES)
    wukv = W['w_ukv'][l]
    p['wukv'] = jnp.concatenate([_pad_heads(wukv, QK_NOPE + V_HEAD, _K_PIECES),
                                 _pad_heads(wukv, QK_NOPE + V_HEAD, _V_PIECES)], axis=1)
    p['wout'] = _pack_wout(W['w_out'][l])
    p['attn_g'] = _pad_gain(W['attn_out_g'][l])
    return p


def _forward_layer(h, memn_in, tabs, W, l, name, plan=None):
    p = _layer_params(W, l)
    sv = {'h0': h, 'p': p}
    xn = _rms_fwd(h, W['norm_mix_g'][l], name=name + "rms_mix")
    proj = _mm([(xn, p['win'])], 'nn', f32, name=name + "mm_in")
    cqn = _rms_fwd(proj, W['q_norm_g'][l], col0=0, width=Q_LORA, name=name + "rms_q")
    ckvn = _rms_fwd(proj, W['kv_norm_g'][l], col0=Q_LORA, width=KV_LORA, name=name + "rms_kv")
    q = _mm([(cqn, p['wuq'])], 'nn', f32, name=name + "mm_uq")
    kv = _mm([(ckvn, p['wukv'])], 'nn', f32, name=name + "mm_ukv")
    qh, kh, vh = _rope_fwd(q, kv, proj, tabs, name=name + "rope")
    oh, lse, carried = _attn_fwd(qh, kh, vh, plan=plan, name=name + "attn")
    an = _rms_fwd(oh, p['attn_g'], n_valid=MLA_WIDTH, name=name + "rms_attn")
    bm, cm, a = W['ssm'][l]
    bmb, cmb = bm.astype(bf16), cm.astype(bf16)
    y, xc = _ssm_fwd(proj, bmb, cmb, a, W['ssm_d'][l], name=name + "ssm")
    z, sn, ge = _glu_fwd(y, W['ssm_w_glu'][l], W['ssm_b_glu'][l], W['ssm_out_g'][l], name=name + "glu")
    h1a = _mm([(an, p['wout'][:MLA_PAD])], 'nn', f32, res=h, name=name + "mm_out_a")
    h1 = _mm([(sn, p['wout'][MLA_PAD:])], 'nn', f32, res=h1a, name=name + "mm_out_s")
    hn2 = _rms_fwd(h1, W['norm_x_g'][l], name=name + "rms_x")
    memn = _rms_fwd(memn_in, W['mem_norm_g'][l], name=name + "rms_mem")
    qx = _mm([(hn2, W['w_xq'][l])], 'nn', bf16, name=name + "mm_xq")
    kvx = _mm([(memn, W['w_xkv'][l])], 'nn', bf16, name=name + "mm_xkv")
    ox = _xattn_fwd(qx, kvx, name=name + "xattn")
    h2 = _mm([(ox, W['w_xo'][l])], 'nn', f32, res=h1, name=name + "mm_xo")
    hn3 = _rms_fwd(h2, W['norm_ffn_g'][l], name=name + "rms_ffn")
    gate, up, act = _ffn_up(hn3, W['w_gate'][l], W['w_up'][l], name=name + "ffn_up")
    h3 = _mm([(act, W['w_down'][l])], 'nn', f32, res=h2, name=name + "mm_down")
    sv.update(xn=xn, proj=proj, cqn=cqn, ckvn=ckvn, qh=qh, kh=kh, vh=vh, oh=oh, lse=lse, an=an, bmb=bmb, cmb=cmb,
              a=a, y=y, xc=xc, z=z, sn=sn, ge=ge, h1=h1, hn2=hn2, memn=memn, qx=qx, kvx=kvx, ox=ox, h2=h2, hn3=hn3,
              gate=gate, up=up, act=act)
    return h3, sv, carried


STACKED = ('w_down', 'w_gate', 'w_up', 'w_xo', 'w_xq', 'w_xkv', 'ssm_w_glu')


def _backward_layer(dh3, sv, memn_in, tabs, W, l, prev, name):
    p = sv['p']
    G = {}
    st = lambda n: (l, prev.get(n))
    G['w_down'] = _mm([(sv['act'], dh3)], 'tn', f32, stack=st('w_down'), name=name + "dw_down")
    dgate, dup = _ffn_bwd_act(dh3, W['w_down'][l], sv['gate'], sv['up'], name=name + "ffn_bwd_act")
    dhn3 = _mm([(dgate, W['w_gate'][l]), (dup, W['w_up'][l])], 'nn', f32, name=name + "mm_dffn")
    G['w_gate'] = _mm([(dgate, sv['hn3'])], 'tn', f32, stack=st('w_gate'), name=name + "dw_gate")
    G['w_up'] = _mm([(dup, sv['hn3'])], 'tn', f32, stack=st('w_up'), name=name + "dw_up")
    dh2, dg = _rms_bwd(sv['h2'], W['norm_ffn_g'][l], dhn3, res=dh3, name=name + "rmsb_ffn")
    G['norm_ffn_g'] = dg[0]
    G['w_xo'] = _mm([(sv['ox'], dh2)], 'tn', f32, stack=st('w_xo'), name=name + "dw_xo")
    dox = _mm([(dh2, W['w_xo'][l])], 'nt', bf16, name=name + "mm_dxo")
    dqx, dkvx = _xattn_bwd(sv['qx'], sv['kvx'], dox, name=name + "xattn_bwd")
    G['w_xq'] = _mm([(sv['hn2'], dqx)], 'tn', f32, stack=st('w_xq'), name=name + "dw_xq")
    G['w_xkv'] = _mm([(sv['memn'], dkvx)], 'tn', f32, stack=st('w_xkv'), name=name + "dw_xkv")
    dhn2 = _mm([(dqx, W['w_xq'][l])], 'nt', f32, name=name + "mm_dxq")
    dmemn = _mm([(dkvx, W['w_xkv'][l])], 'nt', f32, name=name + "mm_dxkv")
    dh1, dg = _rms_bwd(sv['h1'], W['norm_x_g'][l], dhn2, res=dh2, name=name + "rmsb_x")
    G['norm_x_g'] = dg[0]
    _, dg = _rms_bwd(memn_in, W['mem_norm_g'][l], dmemn, name=name + "rmsb_mem")
    G['mem_norm_g'] = dg[0]
    dwo_a = _mm([(sv['an'], dh1)], 'tn', f32, name=name + "dw_out_a")
    dwo_s = _mm([(sv['sn'], dh1)], 'tn', f32, name=name + "dw_out_s")
    G['w_out'] = _unpack_wout(jnp.concatenate([dwo_a, dwo_s], axis=0))
    dmixed = _mm([(dh1, p['wout'])], 'nt', f32, name=name + "mm_dout")
    dy, dz, dg, db = _glu_bwd(sv['y'], sv['z'], dmixed, W['ssm_w_glu'][l], W['ssm_out_g'][l], name=name + "glu_bwd")
    G['ssm_out_g'], G['ssm_b_glu'] = dg[0], db[0]
    G['ssm_w_glu'] = _mm([(sv['ge'], dz)], 'tn', f32, stack=st('ssm_w_glu'), name=name + "dw_glu")
    du, dbm, dcm, da, dd = _ssm_bwd(sv['proj'], dy, sv['xc'], sv['bmb'], sv['cmb'], sv['a'], W['ssm_d'][l],
                                    name=name + "ssm_bwd")
    G['ssm_d'] = dd[0]
    G['ssm_raw'] = (dbm, dcm, da)
    doh, dg, delta = _rms_bwd(sv['oh'], p['attn_g'], dmixed, width=MLA_PAD, n_valid=MLA_WIDTH, delta=True,
                              out_dtype=bf16, name=name + "rmsb_attn")
    G['attn_out_g'] = _unpad_gain(dg[0])
    dqh, dkh, dvh = _attn_bwd(sv['qh'], sv['kh'], sv['vh'], doh, sv['lse'], delta, name=name + "attn_bwd")
    dq, dkv, dkr = _rope_bwd(dqh, dkh, dvh, tabs, name=name + "rope_bwd")
    G['w_uq'] = _unpad_heads(_mm([(sv['cqn'], dq)], 'tn', f32, name=name + "dw_uq"), QK_NOPE + QK_ROPE, _Q_PIECES)
    dwukv = _mm([(sv['ckvn'], dkv)], 'tn', f32, name=name + "dw_ukv")
    G['w_ukv'] = (_unpad_heads(dwukv[:, :MLA_PAD], QK_NOPE + V_HEAD, _K_PIECES)
                  + _unpad_heads(dwukv[:, MLA_PAD:], QK_NOPE + V_HEAD, _V_PIECES))
    dcqn = _mm([(dq, p['wuq'])], 'nt', f32, name=name + "mm_duq")
    dckvn = _mm([(dkv, p['wukv'])], 'nt', f32, name=name + "mm_dukv")
    dcq, dg = _rms_bwd(sv['proj'], W['q_norm_g'][l], dcqn, col0=0, width=Q_LORA, out_dtype=bf16, name=name + "rmsb_q")
    G['q_norm_g'] = dg[0]
    dckv, dg = _rms_bwd(sv['proj'], W['kv_norm_g'][l], dckvn, col0=Q_LORA, width=KV_LORA, out_dtype=bf16,
                        name=name + "rmsb_kv")
    G['kv_norm_g'] = dg[0]
    dproj = jnp.concatenate([dcq, dckv, dkr, du.astype(bf16)], axis=1)
    G['w_in'] = _unpack_win(_mm([(sv['xn'], dproj)], 'tn', f32, name=name + "dw_in"))
    dxn = _mm([(dproj, p['win'])], 'nt', f32, name=name + "mm_din")
    dh0, dg = _rms_bwd(sv['h0'], W['norm_mix_g'][l], dxn, res=dh1, name=name + "rmsb_mix")
    G['norm_mix_g'] = dg[0]
    return dh0, G


def _local_step(x, mem, positions, target, W, later=None):
    tabs = _rope_tables(positions)
    ssm_in = [(W['ssm_lambda_re'][l], W['ssm_lambda_im'][l], W['ssm_log_dt'][l], W['ssm_b_re'][l], W['ssm_b_im'][l],
               W['ssm_c_re'][l], W['ssm_c_im'][l]) for l in range(DEPTH)]
    preps = [jax.vjp(_ssm_prep, *ssm_in[l]) for l in range(DEPTH)]
    W = dict(W)
    W['ssm'] = [preps[l][0] for l in range(DEPTH)]
    h = x
    saved = []
    for l in range(DEPTH):
        plan = later[0] if (later is not None and l == 0) else None
        h, sv, carried = _forward_layer(h, mem, tabs, W, l, f"l{l}_", plan)
        saved.append(sv)
        if plan is not None:
            for n, t in later[1](carried).items():
                W[n] = [W[n][0], t]
    dh, dgf, loss = _loss_head(h, W['final_norm_g'], target)
    grads = [None] * DEPTH
    prev = {}
    for l in reversed(range(DEPTH)):
        dh, G = _backward_layer(dh, saved[l], mem, tabs, W, l, prev, f"l{l}b_")
        prev = {n: G[n] for n in STACKED}
        dbm, dcm, da = G.pop('ssm_raw')
        names = ['ssm_lambda_re', 'ssm_lambda_im', 'ssm_log_dt', 'ssm_b_re', 'ssm_b_im', 'ssm_c_re', 'ssm_c_im']
        for n, g in zip(names, preps[l][1]((dbm, dcm, da))):
            G[n] = g
        grads[l] = G
    out = {n: grads[0][n] if n in STACKED else jnp.stack([grads[l][n] for l in range(DEPTH)]) for n in grads[0]}
    out['final_norm_g'] = dgf[0]
    return loss[0, 0], dh, out


_HBM = pl.BlockSpec(memory_space=pltpu.HBM)


def _me():
    return lax.axis_index("x"), lax.axis_index("y"), lax.axis_index("c")


def _chip_peers(x, y, c):
    devs = [(1 - x, y, c), (x, 1 - y, c), (1 - x, 1 - y, c)]
    return devs, [2 * d[0] + d[1] for d in devs]


def _gather_plan(xs, half_first):
    n = len(xs)
    if half_first:
        ins = [t.reshape(2, 1, *t.shape[1:]) for t in xs]
        outs = [jax.ShapeDtypeStruct((2, 4, *t.shape[1:]), t.dtype) for t in xs]
    else:
        ins = [t.reshape(1, 2, t.shape[0] // 2, t.shape[1]) for t in xs]
        outs = [jax.ShapeDtypeStruct((4, 2, t.shape[0] // 2, t.shape[1]), t.dtype) for t in xs]

    def own(ref, h):
        return ref.at[h] if half_first else ref.at[:, h]

    def slot(ref, h, j):
        return ref.at[h, pl.ds(j, 1)] if half_first else ref.at[pl.ds(j, 1), h]

    def copies(src, dst, send, recv):
        x, y, c = _me()
        jme = 2 * x + y
        devs, js = _chip_peers(x, y, c)
        half, other = pl.ds(c, 1), pl.ds(1 - c, 1)
        mk = pltpu.make_async_remote_copy
        for i in range(n):
            for k in range(3):
                out_cp = mk(own(src[i], half), slot(dst[i], half, jme), send.at[6 * i + k], recv.at[6 * i + k],
                            device_id=devs[k], device_id_type=MESH)
                in_cp = mk(own(src[i], half), slot(dst[i], half, js[k]), send.at[6 * i + k], recv.at[6 * i + k],
                           device_id=devs[k], device_id_type=MESH)
                pass_cp = mk(slot(dst[i], half, js[k]), slot(dst[i], half, js[k]), send.at[6 * i + 3 + k],
                             recv.at[6 * i + 3 + k], device_id=(x, y, 1 - c), device_id_type=MESH)
                got_cp = mk(slot(dst[i], other, js[k]), slot(dst[i], other, js[k]), send.at[6 * i + 3 + k],
                            recv.at[6 * i + 3 + k], device_id=(x, y, 1 - c), device_id_type=MESH)
                yield out_cp, in_cp, pass_cp, got_cp

    def start(*refs):
        for out_cp, _, _, _ in copies(*refs):
            out_cp.start()

    def forward(*refs):
        for _, in_cp, pass_cp, _ in copies(*refs):
            in_cp.wait_recv()
            pass_cp.start()

    def finish(*refs):
        for out_cp, _, pass_cp, got_cp in copies(*refs):
            got_cp.wait_recv()
            out_cp.wait_send()
            pass_cp.wait_send()

    return dict(n=n, ins=ins, outs=outs, nsem=6 * n, start=start, forward=forward, finish=finish)


def _plan_refs(plan, refs):
    n = plan['n']
    return refs[:n], refs[n:2 * n], refs[2 * n], refs[2 * n + 1]


def _run_plan(plan, name):
    n = plan['n']

    def body(*refs):
        args = _plan_refs(plan, refs)
        plan['start'](*args)
        plan['forward'](*args)
        plan['finish'](*args)

    return _pc(body, in_specs=[_HBM] * n, out_specs=[_HBM] * n, out_shape=plan['outs'],
               scratch_shapes=[pltpu.SemaphoreType.DMA((plan['nsem'],)), pltpu.SemaphoreType.DMA((plan['nsem'],))],
               compiler_params=pltpu.CompilerParams(has_side_effects=True), name=name)(*plan['ins'])


def _fill_own(gathered, own, half_first):
    jme = (2 * lax.axis_index("x") + lax.axis_index("y")).astype(jnp.int32)
    zero = jnp.int32(0)
    if half_first:
        return lax.dynamic_update_slice(gathered, own[:, None], (zero, jme, zero, zero))
    return lax.dynamic_update_slice(gathered, own.reshape(1, *gathered.shape[1:]), (jme, zero, zero, zero))


def _exchange_halves(gs, name):
    n = len(gs)

    def body(*refs):
        src, dst = refs[:n], refs[n:2 * n]
        send, recv = refs[2 * n:]
        x, y, c = _me()
        cps = []
        for i in range(n):
            cp = pltpu.make_async_remote_copy(src[i].at[pl.ds(1 - c, 1)], dst[i], send.at[i], recv.at[i],
                                              device_id=(x, y, 1 - c), device_id_type=MESH)
            cp.start()
            cps.append(cp)
        for cp in cps:
            cp.wait()

    outs = [jax.ShapeDtypeStruct((1, *g.shape[1:]), g.dtype) for g in gs]
    return _pc(body, in_specs=[_HBM] * n, out_specs=[_HBM] * n, out_shape=outs,
               scratch_shapes=[pltpu.SemaphoreType.DMA((n,)), pltpu.SemaphoreType.DMA((n,))],
               compiler_params=pltpu.CompilerParams(has_side_effects=True), name=name)(*gs)


def _scatter_chips(ps, name):
    n = len(ps)

    def body(*refs):
        src, dst = refs[:n], refs[n:2 * n]
        send, recv = refs[2 * n:]
        x, y, c = _me()
        devs, js = _chip_peers(x, y, c)
        cps = []
        for i in range(n):
            for k in range(3):
                cp = pltpu.make_async_remote_copy(src[i].at[pl.ds(js[k], 1)], dst[i].at[k], send.at[3 * i + k],
                                                  recv.at[3 * i + k], device_id=devs[k], device_id_type=MESH)
                cp.start()
                cps.append(cp)
        for cp in cps:
            cp.wait()

    outs = [jax.ShapeDtypeStruct((3, 1, *p.shape[1:]), p.dtype) for p in ps]
    return _pc(body, in_specs=[_HBM] * n, out_specs=[_HBM] * n, out_shape=outs,
               scratch_shapes=[pltpu.SemaphoreType.DMA((3 * n,)), pltpu.SemaphoreType.DMA((3 * n,))],
               compiler_params=pltpu.CompilerParams(has_side_effects=True), name=name)(*ps)


def _swap_sibling(hs, name):
    n = len(hs)

    def body(*refs):
        src, dst = refs[:n], refs[n:2 * n]
        send, recv = refs[2 * n:]
        x, y, c = _me()
        cps = []
        for i in range(n):
            cp = pltpu.make_async_remote_copy(src[i], dst[i], send.at[i], recv.at[i], device_id=(x, y, 1 - c),
                                              device_id_type=MESH)
            cp.start()
            cps.append(cp)
        for cp in cps:
            cp.wait()

    outs = [jax.ShapeDtypeStruct(h.shape, h.dtype) for h in hs]
    return _pc(body, in_specs=[_HBM] * n, out_specs=[_HBM] * n, out_shape=outs,
               scratch_shapes=[pltpu.SemaphoreType.DMA((n,)), pltpu.SemaphoreType.DMA((n,))],
               compiler_params=pltpu.CompilerParams(has_side_effects=True), name=name)(*hs)


ELEMWISE_VMEM_BUDGET = 24 * 1024 * 1024


def _row_tile(r, n, narrays):
    limit = ELEMWISE_VMEM_BUDGET // (2 * 4 * narrays * n)
    best = SUBLANES
    for t in range(16, r + 1, 16):
        if r % t == 0 and t <= limit:
            best = t
    return best


def _add_half(g, r1, cidx, name):
    _, _, r, n = g.shape
    tr = _row_tile(r, n, 3)

    def body(c_ref, g_ref, r_ref, o_ref):
        o_ref[...] = (g_ref[...] + r_ref[...]).astype(GRAD_TRANSIT)

    blk = lambda f: pl.BlockSpec((None, None, tr, n), f)
    gs = pltpu.PrefetchScalarGridSpec(
        num_scalar_prefetch=1, grid=(4, r // tr),
        in_specs=[blk(lambda j, i, c: (c[0], j, i, 0)), blk(lambda j, i, c: (0, j, i, 0))],
        out_specs=pl.BlockSpec((None, tr, n), lambda j, i, c: (j, i, 0)))
    return _pc(body, grid_spec=gs, out_shape=jax.ShapeDtypeStruct((4, r, n), GRAD_TRANSIT),
               compiler_params=_cp(("parallel", "parallel")), name=name)(cidx, g, r1)


def _add_chips(p, r3, jidx, name):
    _, r, n = p.shape
    tr = _row_tile(r, n, 5)

    def body(j_ref, p_ref, a_ref, b_ref, c_ref, o_ref):
        o_ref[...] = ((p_ref[...].astype(f32) + a_ref[...].astype(f32)) + b_ref[...].astype(f32)) + c_ref[...].astype(f32)

    rblk = lambda k: pl.BlockSpec((None, None, tr, n), lambda i, j: (k, 0, i, 0))
    gs = pltpu.PrefetchScalarGridSpec(
        num_scalar_prefetch=1, grid=(r // tr,),
        in_specs=[pl.BlockSpec((None, tr, n), lambda i, j: (j[0], i, 0)), rblk(0), rblk(1), rblk(2)],
        out_specs=pl.BlockSpec((tr, n), lambda i, j: (i, 0)))
    return _pc(body, grid_spec=gs, out_shape=jax.ShapeDtypeStruct((r, n), f32),
               compiler_params=_cp(("parallel",)), name=name)(jidx, p, r3, r3, r3)


def _adamw_halves(w, mine, theirs, m, v, cidx, name):
    _, r, n = w.shape
    tr = _row_tile(r, n, 9)
    c1 = 1.0 / (1.0 - ADAM_B1 ** ADAM_STEP)
    c2 = 1.0 / (1.0 - ADAM_B2 ** ADAM_STEP)

    def body(c_ref, w_ref, a_ref, b_ref, m_ref, v_ref, g_ref, d_ref, mo_ref, vo_ref):
        gv = jnp.where(pl.program_id(0) == c_ref[0], a_ref[...], b_ref[...])
        m2 = ADAM_B1 * m_ref[...] + (1.0 - ADAM_B1) * gv
        v2 = ADAM_B2 * v_ref[...] + (1.0 - ADAM_B2) * (gv * gv)
        g_ref[...] = gv
        d_ref[...] = -ADAM_LR * ((m2 * c1) / (jnp.sqrt(v2 * c2) + ADAM_EPS) + ADAM_WD * w_ref[...])
        mo_ref[...] = m2
        vo_ref[...] = v2

    full = pl.BlockSpec((None, tr, n), lambda l, i, c: (l, i, 0))
    half = pl.BlockSpec((tr, n), lambda l, i, c: (i, 0))
    gs = pltpu.PrefetchScalarGridSpec(num_scalar_prefetch=1, grid=(2, r // tr),
                                      in_specs=[full, half, half, full, full], out_specs=[full] * 4)
    return _pc(body, grid_spec=gs, out_shape=[jax.ShapeDtypeStruct(w.shape, f32)] * 4,
               compiler_params=_cp(("parallel", "parallel")), name=name)(cidx, w, mine, theirs, m, v)


def _adamw_whole(w, g, m, v, name):
    c1 = 1.0 / (1.0 - ADAM_B1 ** ADAM_STEP)
    c2 = 1.0 / (1.0 - ADAM_B2 ** ADAM_STEP)

    def body(w_ref, g_ref, m_ref, v_ref, d_ref, mo_ref, vo_ref):
        gv = g_ref[...]
        m2 = ADAM_B1 * m_ref[...] + (1.0 - ADAM_B1) * gv
        v2 = ADAM_B2 * v_ref[...] + (1.0 - ADAM_B2) * (gv * gv)
        d_ref[...] = -ADAM_LR * ((m2 * c1) / (jnp.sqrt(v2 * c2) + ADAM_EPS) + ADAM_WD * w_ref[...])
        mo_ref[...] = m2
        vo_ref[...] = v2

    return _pc(body, out_shape=[jax.ShapeDtypeStruct(w.shape, f32)] * 3, name=name)(w, g, m, v)


def _full_from_gathered(name, t):
    r, n = 2 * t.shape[2], t.shape[3]
    if SHARDED[name] == 1 or name in TRANSPOSED:
        return t.reshape(4 * r, n)
    return t.reshape(4, r, n).transpose(1, 0, 2).reshape(r, 4 * n)


def _shard_major(name, g):
    L, R, C = g.shape
    if SHARDED[name] == 1 or name in TRANSPOSED:
        return g.reshape(L, 4, R // 4, C)
    return g.reshape(L, R, 4, C // 4).transpose(0, 2, 1, 3)


_SMALL_ROWS = 288


def _pack_small(d):
    flat = jnp.concatenate([d[n].reshape(-1) for n in SMALL])
    total = 2 * 4 * _SMALL_ROWS * LANES
    flat = jnp.concatenate([flat, jnp.zeros((total - flat.shape[0],), f32)])
    return flat.reshape(2, 4, _SMALL_ROWS, LANES)


def _unpack_small(t, like):
    flat = t.reshape(-1)
    out, off = {}, 0
    for n in SMALL:
        sz = math.prod(like[n].shape)
        out[n] = flat[off:off + sz].reshape(like[n].shape)
        off += sz
    return out


def kernel(x, mem, positions, norm_mix_g, w_in, q_norm_g, w_uq, kv_norm_g, w_ukv, ssm_lambda_re, ssm_lambda_im, ssm_log_dt, ssm_b_re, ssm_b_im, ssm_c_re, ssm_c_im, ssm_d, ssm_w_glu, ssm_b_glu, attn_out_g, ssm_out_g, w_out, norm_x_g, mem_norm_g, w_xq, w_xkv, w_xo, norm_ffn_g, w_gate, w_up, w_down, final_norm_g, loss_target, m_norm_mix_g, m_w_in, m_q_norm_g, m_w_uq, m_kv_norm_g, m_w_ukv, m_ssm_lambda_re, m_ssm_lambda_im, m_ssm_log_dt, m_ssm_b_re, m_ssm_b_im, m_ssm_c_re, m_ssm_c_im, m_ssm_d, m_ssm_w_glu, m_ssm_b_glu, m_attn_out_g, m_ssm_out_g, m_w_out, m_norm_x_g, m_mem_norm_g, m_w_xq, m_w_xkv, m_w_xo, m_norm_ffn_g, m_w_gate, m_w_up, m_w_down, m_final_norm_g, v_norm_mix_g, v_w_in, v_q_norm_g, v_w_uq, v_kv_norm_g, v_w_ukv, v_ssm_lambda_re, v_ssm_lambda_im, v_ssm_log_dt, v_ssm_b_re, v_ssm_b_im, v_ssm_c_re, v_ssm_c_im, v_ssm_d, v_ssm_w_glu, v_ssm_b_glu, v_attn_out_g, v_ssm_out_g, v_w_out, v_norm_x_g, v_mem_norm_g, v_w_xq, v_w_xkv, v_w_xo, v_norm_ffn_g, v_w_gate, v_w_up, v_w_down, v_final_norm_g):
    given = dict(locals())
    swap = lambda n, t: jnp.swapaxes(t, *TRANSPOSED[n]) if n in TRANSPOSED else t
    w = {n: swap(n, given[n]) for n in WEIGHTS}
    m = {n: swap(n, given["m_" + n]) for n in WEIGHTS}
    v = {n: swap(n, given["v_" + n]) for n in WEIGHTS}
    big = list(SHARDED)

    shards = [w[n].astype(bf16) for n in big]

    def full(results, l):
        return {n: _full_from_gathered(n, _fill_own(t, s[l], False)) for n, t, s in zip(big, results, shards)}

    plans = [_gather_plan([s[l] for s in shards], False) for l in range(DEPTH)]
    W = {n: [t, None] for n, t in full(_run_plan(plans[0], "allgather_weights_l0"), 0).items()}
    W.update({n: w[n] for n in SMALL})

    loss, dx, grads = _local_step(x[0], mem[0], positions[0], loss_target[0], W,
                                  later=(plans[1], lambda results: full(results, 1)))
    loss = lax.psum(loss, ("x", "y", "c"))

    cidx = lax.axis_index("c").astype(jnp.int32).reshape(1)
    jidx = (2 * lax.axis_index("x") + lax.axis_index("y")).astype(jnp.int32).reshape(1)
    names = big + ["small"]
    gs = [_shard_major(n, grads[n]) for n in big] + [_pack_small(grads)]
    r1 = _exchange_halves(gs, "grad_exchange_halves")
    ps = [_add_half(g, r, cidx, f"grad_add_half_{n}") for n, g, r in zip(names, gs, r1)]
    r3 = _scatter_chips(ps, "grad_scatter_chips")
    hs = [_add_chips(p, r, jidx, f"grad_add_chips_{n}") for n, p, r in zip(names, ps, r3)]
    ts = _swap_sibling(hs, "grad_swap_sibling")

    out_g, out_d, out_m, out_v = {}, {}, {}, {}
    for n, h, t in zip(big, hs[:-1], ts[:-1]):
        out_g[n], out_d[n], out_m[n], out_v[n] = _adamw_halves(w[n], h, t, m[n], v[n], cidx, f"adamw_{n}")
    both = jnp.stack([hs[-1], ts[-1]])
    piece = jnp.where(cidx[0] == 0, both, both[::-1])
    gsm = _fill_own(_run_plan(_gather_plan([piece], True), "allgather_small")[0], piece, True)
    out_g.update(_unpack_small(gsm, w))
    for n in SMALL:
        two = lambda t: t.reshape(1, -1) if t.ndim == 1 else t
        d_, m_, v_ = _adamw_whole(two(w[n]), two(out_g[n]), two(m[n]), two(v[n]), f"adamw_{n}")
        out_d[n], out_m[n], out_v[n] = (t.reshape(w[n].shape) for t in (d_, m_, v_))

    outs = [[swap(n, d[n]) for n in WEIGHTS] for d in (out_g, out_d, out_m, out_v)]
    return (loss, dx.reshape(x.shape), *outs[0], *outs[1], *outs[2], *outs[3])
```

```python
import functools
import math

import jax
import jax.numpy as jnp
from jax import lax
from jax.experimental import pallas as pl
from jax.experimental.pallas import tpu as pltpu

f32, bf16 = jnp.float32, jnp.bfloat16

D_MODEL = 1024
DEPTH = 2
MLA_HEADS = 8
QK_NOPE = 64
QK_ROPE = 32
V_HEAD = 64
Q_LORA = 256
KV_LORA = 128
MLA_WIDTH = MLA_HEADS * V_HEAD
ROPE_THETA = 10000.0
SSM_WIDTH = 512
SSM_GROUP = 16
SSM_GROUPS = 32
SSM_STATE = 64
IN_WIDTH = Q_LORA + KV_LORA + QK_ROPE + SSM_WIDTH
X_HEADS = 4
X_HEAD_DIM = D_MODEL // X_HEADS
D_FF = 2816
EPS = 1e-6
ADAM_LR, ADAM_B1, ADAM_B2, ADAM_EPS, ADAM_WD, ADAM_STEP = 0.001, 0.9, 0.999, 1e-08, 0.01, 10

LANES = 128
SUBLANES = 8
HEAD_PAD = 128
MLA_PAD = MLA_HEADS * HEAD_PAD
SSM_MACRO = 4
MACRO_CH = SSM_WIDTH // SSM_MACRO
MACRO_ST = SSM_GROUPS // SSM_MACRO * SSM_STATE
VMEM_LIMIT = 56 * 1024 * 1024
GRAD_TRANSIT = bf16

WEIGHTS = ['norm_mix_g', 'w_in', 'q_norm_g', 'w_uq', 'kv_norm_g', 'w_ukv', 'ssm_lambda_re', 'ssm_lambda_im',
           'ssm_log_dt', 'ssm_b_re', 'ssm_b_im', 'ssm_c_re', 'ssm_c_im', 'ssm_d', 'ssm_w_glu', 'ssm_b_glu',
           'attn_out_g', 'ssm_out_g', 'w_out', 'norm_x_g', 'mem_norm_g', 'w_xq', 'w_xkv', 'w_xo', 'norm_ffn_g',
           'w_gate', 'w_up', 'w_down', 'final_norm_g']
SHARDED = {'w_in': 1, 'w_uq': 2, 'w_ukv': 2, 'ssm_w_glu': 1, 'w_out': 1, 'w_xq': 1, 'w_xkv': 2, 'w_xo': 1,
           'w_gate': 2, 'w_up': 2, 'w_down': 1}
SMALL = [n for n in WEIGHTS if n not in SHARDED]
TRANSPOSED = {'w_gate': (1, 2), 'w_up': (1, 2), 'ssm_b_re': (2, 3), 'ssm_b_im': (2, 3)}
MESH = pl.DeviceIdType.MESH


def _pc(body, **kw):
    return pl.pallas_call(body, **kw)


def _pick(n, prefs):
    for p in prefs:
        if n % p == 0:
            return p
    return n


def _cp(sem=None):
    return pltpu.CompilerParams(dimension_semantics=sem, vmem_limit_bytes=VMEM_LIMIT)


_TILE_CANDS = (1024, 1408, 512, 256, 128)
MM_VMEM_BUDGET = 40 * 1024 * 1024


def _mm_tiles(M, K, N, a_bytes, b_bytes, o_bytes, npair, has_res, need_acc):
    best = None
    for tm in _TILE_CANDS:
        for tk in _TILE_CANDS:
            if M % tm or K % tk:
                continue
            vm = npair * (2 * tm * tk * a_bytes + 2 * tk * N * b_bytes) + 2 * tm * N * o_bytes
            vm += tm * N * 4 * (1 + need_acc + 2 * has_res)
            if a_bytes == 4:
                vm += npair * tm * tk * 2
            if b_bytes == 4:
                vm += npair * tk * N * 2
            if vm <= MM_VMEM_BUDGET and (best is None or tm * tk > best[0]):
                best = (tm * tk, tm, tk)
    if best is None:
        return _pick(M, (256, 128)), _pick(K, (256, 128))
    return best[1], best[2]


def _mm(pairs, mode, out_dtype, res=None, name="mm"):
    a0, b0 = pairs[0]
    if mode == 'nn':
        (M, K), N = a0.shape, b0.shape[1]
        dims = (((1,), (0,)), ((), ()))
    elif mode == 'nt':
        (M, K), N = a0.shape, b0.shape[0]
        dims = (((1,), (1,)), ((), ()))
    else:
        (K, M), N = a0.shape, b0.shape[1]
        dims = (((0,), (0,)), ((), ()))
    npair = len(pairs)
    has_res = res is not None
    direct = out_dtype == f32
    tm, tk = _mm_tiles(M, K, N, a0.dtype.itemsize, b0.dtype.itemsize, jnp.dtype(out_dtype).itemsize, npair, has_res,
                       not direct)
    nk = K // tk

    def body(*refs):
        ins = refs[:2 * npair]
        res_ref = refs[2 * npair] if has_res else None
        o_ref = refs[2 * npair + has_res]
        acc = o_ref if direct else refs[2 * npair + has_res + 1]
        k = pl.program_id(1)
        s = None
        for p in range(npair):
            d = lax.dot_general(ins[2 * p][...].astype(bf16), ins[2 * p + 1][...].astype(bf16), dims,
                                preferred_element_type=f32)
            s = d if s is None else s + d

        @pl.when(k == 0)
        def _():
            acc[...] = s

        @pl.when(k > 0)
        def _():
            acc[...] += s

        if has_res or not direct:
            @pl.when(k == nk - 1)
            def _():
                r = acc[...]
                if has_res:
                    r = r + res_ref[...]
                o_ref[...] = r.astype(out_dtype)

    if mode == 'nn':
        a_spec = pl.BlockSpec((tm, tk), lambda i, k: (i, k))
        b_spec = pl.BlockSpec((tk, N), lambda i, k: (k, 0))
    elif mode == 'nt':
        a_spec = pl.BlockSpec((tm, tk), lambda i, k: (i, k))
        b_spec = pl.BlockSpec((N, tk), lambda i, k: (0, k))
    else:
        a_spec = pl.BlockSpec((tk, tm), lambda i, k: (k, i))
        b_spec = pl.BlockSpec((tk, N), lambda i, k: (k, 0))
    o_spec = pl.BlockSpec((tm, N), lambda i, k: (i, 0))
    in_specs = [a_spec, b_spec] * npair + ([o_spec] if has_res else [])
    args = [t for p in pairs for t in p] + ([res] if has_res else [])
    return _pc(body, grid=(M // tm, nk), in_specs=in_specs, out_specs=o_spec,
               out_shape=jax.ShapeDtypeStruct((M, N), out_dtype),
               scratch_shapes=[] if direct else [pltpu.VMEM((tm, N), f32)],
               compiler_params=_cp(("parallel", "arbitrary")), name=name)(*args)


def _rms_fwd(x, g, *, col0=0, width=None, n_valid=None, out_dtype=bf16, name="rms_fwd"):
    S = x.shape[0]
    width = width or x.shape[1]
    n_valid = n_valid or width
    ts = _pick(S, (512, 256, 128))
    cb = col0 // width

    def body(x_ref, g_ref, o_ref):
        xv = x_ref[...]
        ms = jnp.sum(xv * xv, axis=-1, keepdims=True) * (1.0 / n_valid)
        o_ref[...] = (xv * lax.rsqrt(ms + EPS) * g_ref[...]).astype(out_dtype)

    return _pc(body, grid=(S // ts,),
               in_specs=[pl.BlockSpec((ts, width), lambda i: (i, cb)), pl.BlockSpec((1, width), lambda i: (0, 0))],
               out_specs=pl.BlockSpec((ts, width), lambda i: (i, 0)),
               out_shape=jax.ShapeDtypeStruct((S, width), out_dtype),
               compiler_params=_cp(("parallel",)), name=name)(x, g.reshape(1, width))


def _rms_bwd(x, g, dy, *, col0=0, dcol0=0, width=None, n_valid=None, res=None, out_dtype=f32, delta=False,
             name="rms_bwd"):
    S = x.shape[0]
    width = width or x.shape[1]
    n_valid = n_valid or width
    ts = _pick(S, (512, 256, 128))
    cb, dcb = col0 // width, dcol0 // width
    has_res = res is not None

    def body(*refs):
        x_ref, g_ref, dy_ref = refs[:3]
        res_ref = refs[3] if has_res else None
        outs = refs[3 + has_res:]
        dx_ref, dg_ref = outs[0], outs[1]
        i = pl.program_id(0)
        xv = x_ref[...]
        gv = g_ref[...]
        dyv = dy_ref[...].astype(f32)
        rstd = lax.rsqrt(jnp.sum(xv * xv, axis=-1, keepdims=True) * (1.0 / n_valid) + EPS)
        xh = xv * rstd
        dxh = dyv * gv
        mean = jnp.sum(dxh * xh, axis=-1, keepdims=True) * (1.0 / n_valid)
        dx = rstd * (dxh - xh * mean)
        if delta:
            d_ref = outs[2]
            for h in range(width // LANES):
                sl = slice(h * LANES, (h + 1) * LANES)
                dsum = jnp.sum(dx[:, sl] * xv[:, sl], axis=-1, keepdims=True)
                d_ref[:, sl] = jnp.broadcast_to(dsum, (ts, LANES))
        if has_res:
            dx = dx + res_ref[...]
        dx_ref[...] = dx.astype(out_dtype)

        @pl.when(i == 0)
        def _():
            dg_ref[...] = jnp.zeros_like(dg_ref)

        dg_ref[...] += jnp.sum(dyv * xh, axis=0, keepdims=True)

    blk = lambda c: pl.BlockSpec((ts, width), lambda i: (i, c))
    in_specs = [blk(cb), pl.BlockSpec((1, width), lambda i: (0, 0)), blk(dcb)] + ([blk(0)] if has_res else [])
    out_specs = [blk(0), pl.BlockSpec((1, width), lambda i: (0, 0))] + ([blk(0)] if delta else [])
    out_shape = [jax.ShapeDtypeStruct((S, width), out_dtype), jax.ShapeDtypeStruct((1, width), f32)] + (
        [jax.ShapeDtypeStruct((S, width), f32)] if delta else [])
    args = [x, g.reshape(1, width), dy] + ([res] if has_res else [])
    return _pc(body, grid=(S // ts,), in_specs=in_specs, out_specs=out_specs, out_shape=out_shape,
               compiler_params=_cp(("arbitrary",)), name=name)(*args)


def _loss_head(h, g, target, name="loss_head"):
    S, D = h.shape
    ts = _pick(S, (512, 256, 128))

    def body(h_ref, g_ref, t_ref, dh_ref, dg_ref, loss_ref):
        i = pl.program_id(0)
        xv = h_ref[...]
        gv = g_ref[...]
        rstd = lax.rsqrt(jnp.sum(xv * xv, axis=-1, keepdims=True) * (1.0 / D) + EPS)
        xh = xv * rstd
        err = xh * gv - t_ref[...]
        dyv = err * (1.0 / D)
        dxh = dyv * gv
        mean = jnp.sum(dxh * xh, axis=-1, keepdims=True) * (1.0 / D)
        dh_ref[...] = rstd * (dxh - xh * mean)

        @pl.when(i == 0)
        def _():
            dg_ref[...] = jnp.zeros_like(dg_ref)
            loss_ref[...] = jnp.zeros_like(loss_ref)

        dg_ref[...] += jnp.sum(dyv * xh, axis=0, keepdims=True)
        part = jnp.sum(jnp.sum(err * err, axis=-1, keepdims=True), axis=0, keepdims=True) * (0.5 / D)
        loss_ref[...] += jnp.broadcast_to(part, (1, LANES))

    blk = pl.BlockSpec((ts, D), lambda i: (i, 0))
    row = pl.BlockSpec((1, D), lambda i: (0, 0))
    return _pc(body, grid=(S // ts,), in_specs=[blk, row, blk],
               out_specs=[blk, row, pl.BlockSpec((1, LANES), lambda i: (0, 0))],
               out_shape=[jax.ShapeDtypeStruct((S, D), f32), jax.ShapeDtypeStruct((1, D), f32),
                          jax.ShapeDtypeStruct((1, LANES), f32)],
               compiler_params=_cp(("arbitrary",)), name=name)(h, g.reshape(1, D), target)


def _rope_apply(x, tc, s1, s2):
    return x * tc + pltpu.roll(x, LANES - 16, 1) * s1 + pltpu.roll(x, 16, 1) * s2


def _rope_apply_t(dy, tc, s1, s2):
    return dy * tc + pltpu.roll(dy * s1, 16, 1) + pltpu.roll(dy * s2, LANES - 16, 1)


def _rope_fwd(q, kv, proj, tabs, name="rope_fwd"):
    S = q.shape[0]
    ts = _pick(S, (512, 256, 128))
    scale = (QK_NOPE + QK_ROPE) ** -0.5

    def body(q_ref, kk_ref, kvv_ref, kr_ref, tc_ref, s1_ref, s2_ref, qh_ref, kh_ref, vh_ref):
        tc, s1, s2 = tc_ref[...], s1_ref[...], s2_ref[...]
        krr = _rope_apply(pltpu.roll(kr_ref[...], QK_NOPE, 1), tc, s1, s2)
        for h in range(MLA_HEADS):
            sl = slice(h * HEAD_PAD, (h + 1) * HEAD_PAD)
            qh_ref[:, sl] = (_rope_apply(q_ref[:, sl], tc, s1, s2) * scale).astype(bf16)
            kh_ref[:, sl] = (kk_ref[:, sl] + krr).astype(bf16)
        vh_ref[...] = kvv_ref[...].astype(bf16)

    wide = lambda c: pl.BlockSpec((ts, MLA_PAD), lambda i: (i, c))
    tab = pl.BlockSpec((ts, LANES), lambda i: (i, 0))
    return _pc(body, grid=(S // ts,),
               in_specs=[wide(0), wide(0), wide(1), pl.BlockSpec((ts, LANES), lambda i: (i, 3)), tab, tab, tab],
               out_specs=[wide(0)] * 3, out_shape=[jax.ShapeDtypeStruct((S, MLA_PAD), bf16)] * 3,
               compiler_params=_cp(("parallel",)), name=name)(q, kv, kv, proj, *tabs)


def _rope_bwd(dqh, dkh, dvh, tabs, name="rope_bwd"):
    S = dqh.shape[0]
    ts = _pick(S, (512, 256, 128))
    scale = (QK_NOPE + QK_ROPE) ** -0.5

    def body(dq_ref, dk_ref, dv_ref, tc_ref, s1_ref, s2_ref, oq_ref, okv_ref, okr_ref):
        tc, s1, s2 = tc_ref[...], s1_ref[...], s2_ref[...]
        ksum = None
        for h in range(MLA_HEADS):
            sl = slice(h * HEAD_PAD, (h + 1) * HEAD_PAD)
            oq_ref[:, sl] = (_rope_apply_t(dq_ref[:, sl], tc, s1, s2) * scale).astype(bf16)
            dk = dk_ref[:, sl]
            okv_ref[:, sl] = dk.astype(bf16)
            ksum = dk if ksum is None else ksum + dk
        okv_ref[:, MLA_PAD:] = dv_ref[...].astype(bf16)
        dkr = pltpu.roll(_rope_apply_t(ksum, tc, s1, s2), LANES - QK_NOPE, 1)
        lane = lax.broadcasted_iota(jnp.int32, (ts, LANES), 1)
        okr_ref[...] = jnp.where(lane < QK_ROPE, dkr, 0.0).astype(bf16)

    wide = pl.BlockSpec((ts, MLA_PAD), lambda i: (i, 0))
    tab = pl.BlockSpec((ts, LANES), lambda i: (i, 0))
    return _pc(body, grid=(S // ts,), in_specs=[wide, wide, wide, tab, tab, tab],
               out_specs=[wide, pl.BlockSpec((ts, 2 * MLA_PAD), lambda i: (i, 0)), tab],
               out_shape=[jax.ShapeDtypeStruct((S, MLA_PAD), bf16), jax.ShapeDtypeStruct((S, 2 * MLA_PAD), bf16),
                          jax.ShapeDtypeStruct((S, LANES), bf16)],
               compiler_params=_cp(("parallel",)), name=name)(dqh, dkh, dvh, *tabs)


ATT_BLK = 1024


def _attn_fwd(qh, kh, vh, plan=None, name="attn_fwd"):
    S = qh.shape[0]
    tq = tk = min(S, ATT_BLK)
    nq, nk = S // tq, S // tk
    npl = plan['n'] if plan else 0

    def body(*refs):
        q_ref, k_ref, v_ref = refs[:3]
        o_ref, lse_ref = refs[3 + npl:5 + npl]
        m_sc, l_sc, acc_sc = refs[5 + 2 * npl:8 + 2 * npl]
        h, i, j = pl.program_id(0), pl.program_id(1), pl.program_id(2)
        if plan:
            pargs = (refs[3:3 + npl], refs[5 + npl:5 + 2 * npl], refs[8 + 2 * npl], refs[9 + 2 * npl])
            first = (i == 0) & (j == 0)
            pl.when((h == 0) & first)(functools.partial(plan['start'], *pargs))
            pl.when((h == (3 * MLA_HEADS) // 4) & first)(functools.partial(plan['forward'], *pargs))
            pl.when((h == MLA_HEADS - 1) & (i == nq - 1) & (j == nk - 1))(functools.partial(plan['finish'], *pargs))

        @pl.when(j == 0)
        def _():
            m_sc[...] = jnp.full_like(m_sc, -1e30)
            l_sc[...] = jnp.zeros_like(l_sc)
            acc_sc[...] = jnp.zeros_like(acc_sc)

        def step(masked):
            s = lax.dot_general(q_ref[...], k_ref[...], (((1,), (1,)), ((), ())), preferred_element_type=f32)
            if masked:
                row = lax.broadcasted_iota(jnp.int32, (tq, tk), 0)
                col = lax.broadcasted_iota(jnp.int32, (tq, tk), 1)
                s = jnp.where(col <= row, s, -1e30)
            m_prev = m_sc[...]
            m_new = jnp.maximum(m_prev, jnp.max(s, axis=-1, keepdims=True))
            alpha = jnp.exp(m_prev - m_new)
            p = jnp.exp(s - m_new)
            l_sc[...] = alpha * l_sc[...] + jnp.sum(p, axis=-1, keepdims=True)
            acc_sc[...] = alpha * acc_sc[...] + jnp.dot(p.astype(bf16), v_ref[...], preferred_element_type=f32)
            m_sc[...] = m_new

        pl.when(j < i)(functools.partial(step, False))
        pl.when(j == i)(functools.partial(step, True))

        @pl.when(j == nk - 1)
        def _():
            l = l_sc[...]
            o_ref[...] = acc_sc[...] / l
            lse_ref[...] = jnp.broadcast_to(m_sc[...] + jnp.log(l), (tq, LANES))

    qspec = pl.BlockSpec((tq, HEAD_PAD), lambda h, i, j: (i, h))
    kspec = pl.BlockSpec((tk, HEAD_PAD), lambda h, i, j: (jnp.minimum(j, i), h))
    anyspec = pl.BlockSpec(memory_space=pl.ANY)
    scratch = [pltpu.VMEM((tq, 1), f32), pltpu.VMEM((tq, 1), f32), pltpu.VMEM((tq, HEAD_PAD), f32)]
    if plan:
        scratch += [pltpu.SemaphoreType.DMA((plan['nsem'],)), pltpu.SemaphoreType.DMA((plan['nsem'],))]
    outs = _pc(body, grid=(MLA_HEADS, nq, nk), in_specs=[qspec, kspec, kspec] + [anyspec] * npl,
               out_specs=[qspec, qspec] + [anyspec] * npl,
               out_shape=[jax.ShapeDtypeStruct((S, MLA_PAD), f32)] * 2 + (plan['outs'] if plan else []),
               scratch_shapes=scratch,
               compiler_params=_cp(("arbitrary", "arbitrary", "arbitrary") if plan else ("parallel", "parallel", "arbitrary")),
               name=name)(qh, kh, vh, *(plan['ins'] if plan else []))
    return outs[0], outs[1], outs[2:]


def _attn_bwd(qh, kh, vh, do, lse, delta, plan=None, name="attn_bwd"):
    S = qh.shape[0]
    tq = tk = min(S, ATT_BLK)
    nq, nk = S // tq, S // tk
    npl = plan['n'] if plan else 0

    def body(*refs):
        q_ref, k_ref, v_ref, do_ref, lse_ref, dl_ref = refs[:6]
        dq_ref, dk_ref, dv_ref = refs[6 + npl:9 + npl]
        h, j, i = pl.program_id(0), pl.program_id(1), pl.program_id(2)
        if plan:
            pargs = (refs[6:6 + npl], refs[9 + npl:9 + 2 * npl], refs[9 + 2 * npl], refs[10 + 2 * npl])
            pl.when((h == 0) & (j == 0) & (i == 0))(functools.partial(plan['start'], *pargs))
            pl.when((h == MLA_HEADS - 1) & (j == nk - 1) & (i == nq - 1))(functools.partial(plan['finish'], *pargs))

        @pl.when((j == 0) & (i == 0))
        def _():
            dq_ref[...] = jnp.zeros_like(dq_ref)

        @pl.when(i == 0)
        def _():
            dk_ref[...] = jnp.zeros_like(dk_ref)
            dv_ref[...] = jnp.zeros_like(dv_ref)

        def step(masked):
            nt = (((1,), (1,)), ((), ()))
            tn = (((0,), (0,)), ((), ()))
            qv, kv_, dov = q_ref[...], k_ref[...], do_ref[...]
            s = lax.dot_general(qv, kv_, nt, preferred_element_type=f32)
            p = jnp.exp(s - lse_ref[:, :1])
            if masked:
                row = lax.broadcasted_iota(jnp.int32, (tq, tk), 0)
                col = lax.broadcasted_iota(jnp.int32, (tq, tk), 1)
                p = jnp.where(col <= row, p, 0.0)
            dp = lax.dot_general(dov, v_ref[...], nt, preferred_element_type=f32)
            ds = (p * (dp - dl_ref[:, :1])).astype(bf16)
            dv_ref[...] += lax.dot_general(p.astype(bf16), dov, tn, preferred_element_type=f32)
            dk_ref[...] += lax.dot_general(ds, qv, tn, preferred_element_type=f32)
            rows = pl.ds(pl.multiple_of(i * tq, tq), tq)
            dq_ref[rows, :] += jnp.dot(ds, kv_, preferred_element_type=f32)

        pl.when(i > j)(functools.partial(step, False))
        pl.when(i == j)(functools.partial(step, True))

    qspec = pl.BlockSpec((tq, HEAD_PAD), lambda h, j, i: (jnp.maximum(i, j), h))
    kspec = pl.BlockSpec((tk, HEAD_PAD), lambda h, j, i: (j, h))
    colspec = pl.BlockSpec((S, HEAD_PAD), lambda h, j, i: (0, h))
    anyspec = pl.BlockSpec(memory_space=pl.ANY)
    scratch = [pltpu.SemaphoreType.DMA((plan['nsem'],)), pltpu.SemaphoreType.DMA((plan['nsem'],))] if plan else []
    outs = _pc(body, grid=(MLA_HEADS, nk, nq), in_specs=[qspec, kspec, kspec, qspec, qspec, qspec] + [anyspec] * npl,
               out_specs=[colspec, kspec, kspec] + [anyspec] * npl,
               out_shape=[jax.ShapeDtypeStruct((S, MLA_PAD), f32)] * 3 + (plan['outs'] if plan else []),
               scratch_shapes=scratch,
               compiler_params=_cp(("arbitrary" if plan else "parallel", "arbitrary", "arbitrary")),
               name=name)(qh, kh, vh, do, lse, delta, *(plan['ins'] if plan else []))
    return outs[0], outs[1], outs[2], outs[3:]


def _xattn_fwd(q, kv, name="xattn_fwd"):
    S = q.shape[0]
    M = kv.shape[0]
    tq = _pick(S, (256, 128))
    scale = X_HEAD_DIM ** -0.5

    def body(q_ref, kv_ref, o_ref):
        for h in range(X_HEADS):
            sl = slice(h * X_HEAD_DIM, (h + 1) * X_HEAD_DIM)
            k = kv_ref[:, sl]
            v = kv_ref[:, D_MODEL + h * X_HEAD_DIM:D_MODEL + (h + 1) * X_HEAD_DIM]
            s = lax.dot_general(q_ref[:, sl], k, (((1,), (1,)), ((), ())), preferred_element_type=f32) * scale
            e = jnp.exp(s - jnp.max(s, axis=-1, keepdims=True))
            p = e / jnp.sum(e, axis=-1, keepdims=True)
            o_ref[:, sl] = jnp.dot(p.astype(bf16), v, preferred_element_type=f32).astype(bf16)

    blk = pl.BlockSpec((tq, D_MODEL), lambda i: (i, 0))
    return _pc(body, grid=(S // tq,), in_specs=[blk, pl.BlockSpec((M, 2 * D_MODEL), lambda i: (0, 0))],
               out_specs=blk, out_shape=jax.ShapeDtypeStruct((S, D_MODEL), bf16),
               compiler_params=_cp(("parallel",)), name=name)(q, kv)


def _xattn_bwd(q, kv, do, name="xattn_bwd"):
    S = q.shape[0]
    M = kv.shape[0]
    tq = _pick(S, (256, 128))
    scale = X_HEAD_DIM ** -0.5

    def body(q_ref, kv_ref, do_ref, dq_ref, dkv_ref):
        i = pl.program_id(0)

        @pl.when(i == 0)
        def _():
            dkv_ref[...] = jnp.zeros_like(dkv_ref)

        nt = (((1,), (1,)), ((), ()))
        tn = (((0,), (0,)), ((), ()))
        for h in range(X_HEADS):
            sl = slice(h * X_HEAD_DIM, (h + 1) * X_HEAD_DIM)
            vsl = slice(D_MODEL + h * X_HEAD_DIM, D_MODEL + (h + 1) * X_HEAD_DIM)
            k, v, qv, dov = kv_ref[:, sl], kv_ref[:, vsl], q_ref[:, sl], do_ref[:, sl]
            s = lax.dot_general(qv, k, nt, preferred_element_type=f32) * scale
            e = jnp.exp(s - jnp.max(s, axis=-1, keepdims=True))
            p = e / jnp.sum(e, axis=-1, keepdims=True)
            dp = lax.dot_general(dov, v, nt, preferred_element_type=f32)
            ds = (p * (dp - jnp.sum(dp * p, axis=-1, keepdims=True)) * scale).astype(bf16)
            dq_ref[:, sl] = jnp.dot(ds, k, preferred_element_type=f32).astype(bf16)
            dkv_ref[:, sl] += lax.dot_general(ds, qv, tn, preferred_element_type=f32)
            dkv_ref[:, vsl] += lax.dot_general(p.astype(bf16), dov, tn, preferred_element_type=f32)

    blk = pl.BlockSpec((tq, D_MODEL), lambda i: (i, 0))
    full = pl.BlockSpec((M, 2 * D_MODEL), lambda i: (0, 0))
    return _pc(body, grid=(S // tq,), in_specs=[blk, full, blk], out_specs=[blk, full],
               out_shape=[jax.ShapeDtypeStruct((S, D_MODEL), bf16), jax.ShapeDtypeStruct((M, 2 * D_MODEL), f32)],
               compiler_params=_cp(("arbitrary",)), name=name)(q, kv, do)


ST_CHUNKS = 1


def _apow_init(a_ref, ap_ref, bp_ref, seg):
    P = MACRO_ST
    ar, ai = a_ref[:, :P], a_ref[:, P:]
    pr, pi = ar, ai
    for r in range(seg):
        ap_ref[r:r + 1, :P] = pr
        ap_ref[r:r + 1, P:] = pi
        if r < seg - 1:
            pr, pi = pr * ar - pi * ai, pr * ai + pi * ar
    br, bi = pr, pi
    for k in range(SUBLANES):
        bp_ref[k:k + 1, :P] = pr
        bp_ref[k:k + 1, P:] = pi
        pr, pi = pr * br - pi * bi, pr * bi + pi * br


def _segment_perm(tS):
    seg = tS // SUBLANES
    rows = jnp.arange(tS)
    src = (rows % SUBLANES) * seg + rows // SUBLANES
    return (src[:, None] == jnp.arange(tS)[None, :]).astype(f32)


def _unpermute_rows(pt, v):
    hi = v.astype(bf16)
    r1 = v - hi.astype(f32)
    mid = r1.astype(bf16)
    lo = (r1 - mid.astype(f32)).astype(bf16)
    out = jnp.dot(pt, jnp.concatenate([hi, mid, lo], axis=1), preferred_element_type=f32)
    w = v.shape[1]
    return (out[:, :w] + out[:, w:2 * w]) + out[:, 2 * w:]


def _scan_block(sc_ref, ap_ref, bp_ref, carry_ref, e_ref, seg, reverse):
    P = MACRO_ST
    sgn = -1.0 if reverse else 1.0
    CH = P // ST_CHUNKS
    rid = lax.broadcasted_iota(jnp.int32, (SUBLANES, CH), 0)
    for c in range(ST_CHUNKS):
        lr, li = slice(c * CH, (c + 1) * CH), slice(P + c * CH, P + (c + 1) * CH)
        ar, ai = ap_ref[0:1, lr], sgn * ap_ref[0:1, li]
        xr = xi = None
        for i in range(seg):
            r = seg - 1 - i if reverse else i
            rows = slice(SUBLANES * r, SUBLANES * (r + 1))
            sr, si = sc_ref[rows, lr], sc_ref[rows, li]
            if i == 0:
                xr, xi = sr, si
            else:
                xr, xi = ar * xr - ai * xi + sr, ar * xi + ai * xr + si
                sc_ref[rows, lr] = xr
                sc_ref[rows, li] = xi
        for sh in (1, 2, 4):
            pr, pi = bp_ref[sh - 1:sh, lr], sgn * bp_ref[sh - 1:sh, li]
            if reverse:
                tr = jnp.where(rid < SUBLANES - sh, pltpu.roll(xr, SUBLANES - sh, 0), 0.0)
                ti = jnp.where(rid < SUBLANES - sh, pltpu.roll(xi, SUBLANES - sh, 0), 0.0)
            else:
                tr = jnp.where(rid >= sh, pltpu.roll(xr, sh, 0), 0.0)
                ti = jnp.where(rid >= sh, pltpu.roll(xi, sh, 0), 0.0)
            xr, xi = xr + pr * tr - pi * ti, xi + pr * ti + pi * tr
        if reverse:
            bpr = jnp.zeros((SUBLANES, CH), f32)
            bpi = jnp.zeros((SUBLANES, CH), f32)
            for r in range(SUBLANES):
                bpr = jnp.where(rid == r, bp_ref[SUBLANES - 1 - r:SUBLANES - r, lr], bpr)
                bpi = jnp.where(rid == r, -bp_ref[SUBLANES - 1 - r:SUBLANES - r, li], bpi)
        else:
            bpr, bpi = bp_ref[:, lr], bp_ref[:, li]
        cr, cim = carry_ref[:, lr], carry_ref[:, li]
        xr, xi = xr + bpr * cr - bpi * cim, xi + bpr * cim + bpi * cr
        edge = 0 if reverse else SUBLANES - 1
        carry_ref[:, lr] = jnp.sum(jnp.where(rid == edge, xr, 0.0), axis=0, keepdims=True)
        carry_ref[:, li] = jnp.sum(jnp.where(rid == edge, xi, 0.0), axis=0, keepdims=True)
        if reverse:
            er = jnp.where(rid == SUBLANES - 1, cr, pltpu.roll(xr, SUBLANES - 1, 0))
            ei = jnp.where(rid == SUBLANES - 1, cim, pltpu.roll(xi, SUBLANES - 1, 0))
        else:
            er = jnp.where(rid == 0, cr, pltpu.roll(xr, 1, 0))
            ei = jnp.where(rid == 0, cim, pltpu.roll(xi, 1, 0))
        if e_ref is not None:
            e_ref[:, lr] = er
            e_ref[:, li] = ei
        for i in range(seg):
            r = seg - 1 - i if reverse else i
            rows = slice(SUBLANES * r, SUBLANES * (r + 1))
            pr, pi = ap_ref[i:i + 1, lr], sgn * ap_ref[i:i + 1, li]
            sc_ref[rows, lr] += pr * er - pi * ei
            sc_ref[rows, li] += pr * ei + pi * er


def _ssm_fwd(proj, bm, cm, a, d, name="ssm_fwd"):
    S = proj.shape[0]
    tS = _pick(S, (256, 128))
    nb = S // tS
    P2 = 2 * MACRO_ST
    seg = tS // SUBLANES
    ucol0 = (D_MODEL - SSM_WIDTH) // MACRO_CH

    perm = _segment_perm(tS)

    def body(u_ref, b_ref, c_ref, a_ref, d_ref, pm_ref, pt_ref, y_ref, xc_ref, bu_sc, ap_sc, bp_sc, car_sc):
        t = pl.program_id(1)

        @pl.when(t == 0)
        def _():
            _apow_init(a_ref, ap_sc, bp_sc, seg)
            car_sc[...] = jnp.zeros_like(car_sc)

        uv = u_ref[...]
        up = jnp.dot(pm_ref[...], uv.astype(bf16), preferred_element_type=f32).astype(bf16)
        bu_sc[...] = jnp.dot(up, b_ref[...], preferred_element_type=f32)
        xc_ref[...] = car_sc[...]
        _scan_block(bu_sc, ap_sc, bp_sc, car_sc, None, seg, False)
        yp = jnp.dot(bu_sc[...].astype(bf16), c_ref[...], preferred_element_type=f32)
        y_ref[...] = _unpermute_rows(pt_ref[...], yp) + d_ref[...] * uv

    sq = pl.BlockSpec((tS, tS), lambda m, t: (0, 0))
    return _pc(body, grid=(SSM_MACRO, nb),
               in_specs=[pl.BlockSpec((tS, MACRO_CH), lambda m, t: (t, ucol0 + m)),
                         pl.BlockSpec((None, MACRO_CH, P2), lambda m, t: (m, 0, 0)),
                         pl.BlockSpec((None, P2, MACRO_CH), lambda m, t: (m, 0, 0)),
                         pl.BlockSpec((None, 1, P2), lambda m, t: (m, 0, 0)),
                         pl.BlockSpec((1, MACRO_CH), lambda m, t: (0, m)), sq, sq],
               out_specs=[pl.BlockSpec((tS, MACRO_CH), lambda m, t: (t, m)),
                          pl.BlockSpec((None, None, 1, P2), lambda m, t: (m, t, 0, 0))],
               out_shape=[jax.ShapeDtypeStruct((S, SSM_WIDTH), f32), jax.ShapeDtypeStruct((SSM_MACRO, nb, 1, P2), f32)],
               scratch_shapes=[pltpu.VMEM((tS, P2), f32), pltpu.VMEM((seg, P2), f32),
                               pltpu.VMEM((SUBLANES, P2), f32), pltpu.VMEM((1, P2), f32)],
               compiler_params=_cp(("arbitrary", "arbitrary")), name=name)(
        proj, bm, cm, a, d.reshape(1, SSM_WIDTH), perm.astype(bf16), perm.T.astype(bf16))


def _ssm_bwd(proj, dy, xc, bm, cm, a, d, name="ssm_bwd"):
    S = proj.shape[0]
    tS = _pick(S, (256, 128))
    nb = S // tS
    P = MACRO_ST
    P2 = 2 * P
    seg = tS // SUBLANES
    ucol0 = (D_MODEL - SSM_WIDTH) // MACRO_CH

    perm = _segment_perm(tS)

    def body(u_ref, dy_ref, xc_ref, b_ref, c_ref, a_ref, d_ref, pm_ref, pt_ref, du_ref, db_ref, dc_ref, da_ref, dd_ref,
             x_sc, g_sc, ap_sc, bp_sc, e_sc, xcar_sc, gcar_sc):
        t = pl.program_id(1)

        @pl.when(t == 0)
        def _():
            _apow_init(a_ref, ap_sc, bp_sc, seg)
            gcar_sc[...] = jnp.zeros_like(gcar_sc)
            db_ref[...] = jnp.zeros_like(db_ref)
            dc_ref[...] = jnp.zeros_like(dc_ref)
            da_ref[...] = jnp.zeros_like(da_ref)
            dd_ref[...] = jnp.zeros_like(dd_ref)

        nt = (((1,), (1,)), ((), ()))
        tn = (((0,), (0,)), ((), ()))
        uv = u_ref[...]
        dyv = dy_ref[...]
        pm = pm_ref[...]
        ub = jnp.dot(pm, uv.astype(bf16), preferred_element_type=f32).astype(bf16)
        dyb = jnp.dot(pm, dyv.astype(bf16), preferred_element_type=f32).astype(bf16)
        x_sc[...] = jnp.dot(ub, b_ref[...], preferred_element_type=f32)
        xcar_sc[...] = xc_ref[...]
        _scan_block(x_sc, ap_sc, bp_sc, xcar_sc, e_sc, seg, False)
        g_sc[...] = lax.dot_general(dyb, c_ref[...], nt, preferred_element_type=f32)
        _scan_block(g_sc, ap_sc, bp_sc, gcar_sc, None, seg, True)
        xv = x_sc[...]
        gv = g_sc[...]
        gb = gv.astype(bf16)
        dc_ref[...] += lax.dot_general(xv.astype(bf16), dyb, tn, preferred_element_type=f32)
        db_ref[...] += lax.dot_general(ub, gb, tn, preferred_element_type=f32)
        dup = lax.dot_general(gb, b_ref[...], nt, preferred_element_type=f32)
        du_ref[...] = _unpermute_rows(pt_ref[...], dup) + d_ref[...] * dyv
        dd_ref[...] += jnp.sum(dyv * uv, axis=0, keepdims=True)
        xp = jnp.concatenate([e_sc[...], xv[:tS - SUBLANES]], axis=0)
        xpr, xpi, ggr, ggi = xp[:, :P], xp[:, P:], gv[:, :P], gv[:, P:]
        da_ref[:, :P] += jnp.sum(ggr * xpr + ggi * xpi, axis=0, keepdims=True)
        da_ref[:, P:] += jnp.sum(ggi * xpr - ggr * xpi, axis=0, keepdims=True)

    rev = lambda t: nb - 1 - t
    return _pc(body, grid=(SSM_MACRO, nb),
               in_specs=[pl.BlockSpec((tS, MACRO_CH), lambda m, t: (rev(t), ucol0 + m)),
                         pl.BlockSpec((tS, MACRO_CH), lambda m, t: (rev(t), m)),
                         pl.BlockSpec((None, None, 1, P2), lambda m, t: (m, rev(t), 0, 0)),
                         pl.BlockSpec((None, MACRO_CH, P2), lambda m, t: (m, 0, 0)),
                         pl.BlockSpec((None, P2, MACRO_CH), lambda m, t: (m, 0, 0)),
                         pl.BlockSpec((None, 1, P2), lambda m, t: (m, 0, 0)),
                         pl.BlockSpec((1, MACRO_CH), lambda m, t: (0, m)),
                         pl.BlockSpec((tS, tS), lambda m, t: (0, 0)), pl.BlockSpec((tS, tS), lambda m, t: (0, 0))],
               out_specs=[pl.BlockSpec((tS, MACRO_CH), lambda m, t: (rev(t), m)),
                          pl.BlockSpec((None, MACRO_CH, P2), lambda m, t: (m, 0, 0)),
                          pl.BlockSpec((None, P2, MACRO_CH), lambda m, t: (m, 0, 0)),
                          pl.BlockSpec((None, 1, P2), lambda m, t: (m, 0, 0)),
                          pl.BlockSpec((1, MACRO_CH), lambda m, t: (0, m))],
               out_shape=[jax.ShapeDtypeStruct((S, SSM_WIDTH), f32),
                          jax.ShapeDtypeStruct((SSM_MACRO, MACRO_CH, P2), f32),
                          jax.ShapeDtypeStruct((SSM_MACRO, P2, MACRO_CH), f32),
                          jax.ShapeDtypeStruct((SSM_MACRO, 1, P2), f32),
                          jax.ShapeDtypeStruct((1, SSM_WIDTH), f32)],
               scratch_shapes=[pltpu.VMEM((tS, P2), f32), pltpu.VMEM((tS, P2), f32),
                               pltpu.VMEM((seg, P2), f32), pltpu.VMEM((SUBLANES, P2), f32), pltpu.VMEM((SUBLANES, P2), f32),
                               pltpu.VMEM((1, P2), f32), pltpu.VMEM((1, P2), f32)],
               compiler_params=_cp(("arbitrary", "arbitrary")), name=name)(
        proj, dy, xc, bm, cm, a, d.reshape(1, SSM_WIDTH), perm.astype(bf16), perm.T.astype(bf16))


_GELU_K = math.sqrt(2.0 / math.pi)
_GELU_C = 0.044715


def _glu_fwd(y, w, b, g, name="glu_fwd"):
    S, W = y.shape
    ts = _pick(S, (512, 256, 128))

    def body(y_ref, w_ref, b_ref, g_ref, z_ref, sn_ref, ge_ref):
        yv = y_ref[...]
        cdf = 0.5 * (1.0 + jnp.tanh(_GELU_K * (yv + _GELU_C * (yv * yv * yv))))
        ge = (yv * cdf).astype(bf16)
        z = jnp.dot(ge, w_ref[...], preferred_element_type=f32) + b_ref[...]
        s = yv * jax.nn.sigmoid(z)
        rstd = lax.rsqrt(jnp.sum(s * s, axis=-1, keepdims=True) * (1.0 / W) + EPS)
        z_ref[...] = z
        sn_ref[...] = (s * rstd * g_ref[...]).astype(bf16)
        ge_ref[...] = ge

    blk = pl.BlockSpec((ts, W), lambda i: (i, 0))
    row = pl.BlockSpec((1, W), lambda i: (0, 0))
    return _pc(body, grid=(S // ts,), in_specs=[blk, pl.BlockSpec((W, W), lambda i: (0, 0)), row, row],
               out_specs=[blk, blk, blk],
               out_shape=[jax.ShapeDtypeStruct((S, W), f32), jax.ShapeDtypeStruct((S, W), bf16),
                          jax.ShapeDtypeStruct((S, W), bf16)],
               compiler_params=_cp(("parallel",)), name=name)(y, w, b.reshape(1, W), g.reshape(1, W))


def _glu_bwd(y, z, dmixed, w, g, name="glu_bwd"):
    S, W = y.shape
    ts = _pick(S, (512, 256, 128))
    dcb = MLA_PAD // W

    def body(y_ref, z_ref, dsn_ref, w_ref, g_ref, dy_ref, dz_ref, dg_ref, db_ref):
        i = pl.program_id(0)
        yv, zv, gv = y_ref[...], z_ref[...], g_ref[...]
        sig = jax.nn.sigmoid(zv)
        s = yv * sig
        rstd = lax.rsqrt(jnp.sum(s * s, axis=-1, keepdims=True) * (1.0 / W) + EPS)
        sh = s * rstd
        dsn = dsn_ref[...]
        dsh = dsn * gv
        ds = rstd * (dsh - sh * (jnp.sum(dsh * sh, axis=-1, keepdims=True) * (1.0 / W)))
        dz = ds * s * (1.0 - sig)
        dzb = dz.astype(bf16)
        dge = lax.dot_general(dzb, w_ref[...], (((1,), (1,)), ((), ())), preferred_element_type=f32)
        t = jnp.tanh(_GELU_K * (yv + _GELU_C * (yv * yv * yv)))
        dgelu = 0.5 * (1.0 + t) + 0.5 * yv * (1.0 - t * t) * _GELU_K * (1.0 + 3.0 * _GELU_C * yv * yv)
        dy_ref[...] = ds * sig + dge * dgelu
        dz_ref[...] = dzb

        @pl.when(i == 0)
        def _():
            dg_ref[...] = jnp.zeros_like(dg_ref)
            db_ref[...] = jnp.zeros_like(db_ref)

        dg_ref[...] += jnp.sum(dsn * sh, axis=0, keepdims=True)
        db_ref[...] += jnp.sum(dz, axis=0, keepdims=True)

    blk = pl.BlockSpec((ts, W), lambda i: (i, 0))
    row = pl.BlockSpec((1, W), lambda i: (0, 0))
    return _pc(body, grid=(S // ts,),
               in_specs=[blk, blk, pl.BlockSpec((ts, W), lambda i: (i, dcb)), pl.BlockSpec((W, W), lambda i: (0, 0)), row],
               out_specs=[blk, blk, row, row],
               out_shape=[jax.ShapeDtypeStruct((S, W), f32), jax.ShapeDtypeStruct((S, W), bf16),
                          jax.ShapeDtypeStruct((1, W), f32), jax.ShapeDtypeStruct((1, W), f32)],
               compiler_params=_cp(("arbitrary",)), name=name)(y, z, dmixed, w, g.reshape(1, W))


def _ffn_up(hn, wg, wu, name="ffn_up"):
    S, K = hn.shape
    F = wg.shape[0]
    tm, tn = _pick(S, (512, 256, 128)), _pick(F, (1408, 256, 128))

    def body(h_ref, wg_ref, wu_ref, g_ref, u_ref, a_ref):
        hv = h_ref[...]
        nt = (((1,), (1,)), ((), ()))
        gv = lax.dot_general(hv, wg_ref[...], nt, preferred_element_type=f32)
        uv = lax.dot_general(hv, wu_ref[...], nt, preferred_element_type=f32)
        g_ref[...] = gv
        u_ref[...] = uv
        a_ref[...] = (gv * jax.nn.sigmoid(gv) * uv).astype(bf16)

    wspec = pl.BlockSpec((tn, K), lambda i, j: (j, 0))
    ospec = pl.BlockSpec((tm, tn), lambda i, j: (i, j))
    return _pc(body, grid=(S // tm, F // tn), in_specs=[pl.BlockSpec((tm, K), lambda i, j: (i, 0)), wspec, wspec],
               out_specs=[ospec] * 3,
               out_shape=[jax.ShapeDtypeStruct((S, F), f32), jax.ShapeDtypeStruct((S, F), f32),
                          jax.ShapeDtypeStruct((S, F), bf16)],
               compiler_params=_cp(("parallel", "parallel")), name=name)(hn, wg, wu)


def _ffn_bwd_act(dh, wd, gate, up, name="ffn_bwd_act"):
    S, K = dh.shape
    F = wd.shape[0]
    tm, tn = _pick(S, (512, 256, 128)), _pick(F, (1408, 256, 128))

    def body(dh_ref, wd_ref, g_ref, u_ref, dg_ref, du_ref):
        dact = lax.dot_general(dh_ref[...].astype(bf16), wd_ref[...], (((1,), (1,)), ((), ())),
                               preferred_element_type=f32)
        gv, uv = g_ref[...], u_ref[...]
        sig = jax.nn.sigmoid(gv)
        dg_ref[...] = (dact * uv * (sig * (1.0 + gv * (1.0 - sig)))).astype(bf16)
        du_ref[...] = (dact * (gv * sig)).astype(bf16)

    ospec = pl.BlockSpec((tm, tn), lambda i, j: (i, j))
    return _pc(body, grid=(S // tm, F // tn),
               in_specs=[pl.BlockSpec((tm, K), lambda i, j: (i, 0)), pl.BlockSpec((tn, K), lambda i, j: (j, 0)),
                         ospec, ospec],
               out_specs=[ospec] * 2, out_shape=[jax.ShapeDtypeStruct((S, F), bf16)] * 2,
               compiler_params=_cp(("parallel", "parallel")), name=name)(dh, wd, gate, up)


def _pad_heads(w, per_head, pieces):
    K = w.shape[0]
    w3 = w.reshape(K, MLA_HEADS, per_head)
    out = jnp.zeros((K, MLA_HEADS, HEAD_PAD), w.dtype)
    for s0, s1, d0 in pieces:
        out = out.at[:, :, d0:d0 + (s1 - s0)].set(w3[:, :, s0:s1])
    return out.reshape(K, MLA_PAD)


def _unpad_heads(wp, per_head, pieces):
    K = wp.shape[0]
    w3 = wp.reshape(K, MLA_HEADS, HEAD_PAD)
    out = jnp.zeros((K, MLA_HEADS, per_head), wp.dtype)
    for s0, s1, d0 in pieces:
        out = out.at[:, :, s0:s1].set(w3[:, :, d0:d0 + (s1 - s0)])
    return out.reshape(K, MLA_HEADS * per_head)


_Q_PIECES = [(0, QK_NOPE + QK_ROPE, 0)]
_K_PIECES = [(0, QK_NOPE, 0)]
_V_PIECES = [(QK_NOPE, QK_NOPE + V_HEAD, 0)]
_KR0 = Q_LORA + KV_LORA


def _pack_win(w):
    z = jnp.zeros((w.shape[0], LANES - QK_ROPE), w.dtype)
    return jnp.concatenate([w[:, :_KR0 + QK_ROPE], z, w[:, _KR0 + QK_ROPE:]], axis=1)


def _unpack_win(wp):
    return jnp.concatenate([wp[:, :_KR0 + QK_ROPE], wp[:, _KR0 + LANES:]], axis=1)


def _pack_wout(w):
    wa = w[:MLA_WIDTH].reshape(MLA_HEADS, V_HEAD, D_MODEL)
    wa = jnp.concatenate([wa, jnp.zeros_like(wa)], axis=1).reshape(MLA_PAD, D_MODEL)
    return jnp.concatenate([wa, w[MLA_WIDTH:]], axis=0)


def _unpack_wout(wp):
    wa = wp[:MLA_PAD].reshape(MLA_HEADS, HEAD_PAD, D_MODEL)[:, :V_HEAD].reshape(MLA_WIDTH, D_MODEL)
    return jnp.concatenate([wa, wp[MLA_PAD:]], axis=0)


def _pad_gain(g):
    g2 = g.reshape(MLA_HEADS, V_HEAD)
    return jnp.concatenate([g2, jnp.zeros_like(g2)], axis=1).reshape(MLA_PAD)


def _unpad_gain(gp):
    return gp.reshape(MLA_HEADS, HEAD_PAD)[:, :V_HEAD].reshape(MLA_WIDTH)


def _ssm_prep(lam_re, lam_im, log_dt, b_re, b_im, c_re, c_im):
    lam = lax.complex(lam_re, lam_im)
    dt = jnp.exp(log_dt)[:, None]
    a_bar = jnp.exp(lam * dt)
    b_bar = ((a_bar - 1.0) / lam)[:, None, :] * lax.complex(b_re, b_im)
    G8 = SSM_GROUPS // SSM_MACRO
    eye = jnp.eye(G8, dtype=f32)

    def bmat(part):
        p4 = part.reshape(SSM_MACRO, G8, SSM_GROUP, SSM_STATE)
        return jnp.einsum('mgcp,gh->mgchp', p4, eye).reshape(SSM_MACRO, MACRO_CH, MACRO_ST)

    def cmat(part):
        p4 = part.reshape(SSM_MACRO, G8, SSM_GROUP, SSM_STATE)
        return jnp.einsum('mgcp,gh->mgphc', p4, eye).reshape(SSM_MACRO, MACRO_ST, MACRO_CH)

    bm = jnp.concatenate([bmat(b_bar.real), bmat(b_bar.imag)], axis=2)
    cm = jnp.concatenate([cmat(c_re), -cmat(c_im)], axis=1)
    a4 = a_bar.reshape(SSM_MACRO, 1, MACRO_ST)
    a = jnp.concatenate([a4.real, a4.imag], axis=2)
    return bm, cm, a


def _rope_tables(positions):
    freqs = ROPE_THETA ** (-jnp.arange(0, QK_ROPE, 2, dtype=f32) / QK_ROPE)
    ang = positions.astype(f32)[:, None] * freqs
    cos, sin = jnp.cos(ang), jnp.sin(ang)
    S = positions.shape[0]
    half = QK_ROPE // 2
    one, zero = jnp.ones((S, QK_NOPE), f32), jnp.zeros((S, half), f32)
    z64, z32 = jnp.zeros((S, QK_NOPE), f32), jnp.zeros((S, LANES - QK_NOPE - QK_ROPE), f32)
    tc = jnp.concatenate([one, cos, cos, z32], axis=1)
    s1 = jnp.concatenate([z64, -sin, zero, z32], axis=1)
    s2 = jnp.concatenate([z64, zero, sin, z32], axis=1)
    return tc, s1, s2


def _layer_params(W, l):
    p = {}
    p['win'] = _pack_win(W['w_in'][l])
    p['wuq'] = _pad_heads(W['w_uq'][l], QK_NOPE + QK_ROPE, _Q_PIECES)
    wukv = W['w_ukv'][l]
    p['wukv'] = jnp.concatenate([_pad_heads(wukv, QK_NOPE + V_HEAD, _K_PIECES),
                                 _pad_heads(wukv, QK_NOPE + V_HEAD, _V_PIECES)], axis=1)
    p['wout'] = _pack_wout(W['w_out'][l])
    p['attn_g'] = _pad_gain(W['attn_out_g'][l])
    return p


def _forward_layer(h, memn_in, tabs, W, l, name, plan=None):
    p = _layer_params(W, l)
    sv = {'h0': h, 'p': p}
    xn = _rms_fwd(h, W['norm_mix_g'][l], name=name + "rms_mix")
    proj = _mm([(xn, p['win'])], 'nn', f32, name=name + "mm_in")
    cqn = _rms_fwd(proj, W['q_norm_g'][l], col0=0, width=Q_LORA, name=name + "rms_q")
    ckvn = _rms_fwd(proj, W['kv_norm_g'][l], col0=Q_LORA, width=KV_LORA, name=name + "rms_kv")
    q = _mm([(cqn, p['wuq'])], 'nn', f32, name=name + "mm_uq")
    kv = _mm([(ckvn, p['wukv'])], 'nn', f32, name=name + "mm_ukv")
    qh, kh, vh = _rope_fwd(q, kv, proj, tabs, name=name + "rope")
    oh, lse, carried = _attn_fwd(qh, kh, vh, plan=plan, name=name + "attn")
    an = _rms_fwd(oh, p['attn_g'], n_valid=MLA_WIDTH, name=name + "rms_attn")
    bm, cm, a = W['ssm'][l]
    bmb, cmb = bm.astype(bf16), cm.astype(bf16)
    y, xc = _ssm_fwd(proj, bmb, cmb, a, W['ssm_d'][l], name=name + "ssm")
    z, sn, ge = _glu_fwd(y, W['ssm_w_glu'][l], W['ssm_b_glu'][l], W['ssm_out_g'][l], name=name + "glu")
    h1a = _mm([(an, p['wout'][:MLA_PAD])], 'nn', f32, res=h, name=name + "mm_out_a")
    h1 = _mm([(sn, p['wout'][MLA_PAD:])], 'nn', f32, res=h1a, name=name + "mm_out_s")
    hn2 = _rms_fwd(h1, W['norm_x_g'][l], name=name + "rms_x")
    memn = _rms_fwd(memn_in, W['mem_norm_g'][l], name=name + "rms_mem")
    qx = _mm([(hn2, W['w_xq'][l])], 'nn', bf16, name=name + "mm_xq")
    kvx = _mm([(memn, W['w_xkv'][l])], 'nn', bf16, name=name + "mm_xkv")
    ox = _xattn_fwd(qx, kvx, name=name + "xattn")
    h2 = _mm([(ox, W['w_xo'][l])], 'nn', f32, res=h1, name=name + "mm_xo")
    hn3 = _rms_fwd(h2, W['norm_ffn_g'][l], name=name + "rms_ffn")
    gate, up, act = _ffn_up(hn3, W['w_gate'][l], W['w_up'][l], name=name + "ffn_up")
    h3 = _mm([(act, W['w_down'][l])], 'nn', f32, res=h2, name=name + "mm_down")
    sv.update(xn=xn, proj=proj, cqn=cqn, ckvn=ckvn, qh=qh, kh=kh, vh=vh, oh=oh, lse=lse, an=an, bmb=bmb, cmb=cmb,
              a=a, y=y, xc=xc, z=z, sn=sn, ge=ge, h1=h1, hn2=hn2, memn=memn, qx=qx, kvx=kvx, ox=ox, h2=h2, hn3=hn3,
              gate=gate, up=up, act=act)
    return h3, sv, carried


def _backward_layer(dh3, sv, memn_in, tabs, W, l, plan, name):
    p = sv['p']
    G = {}
    G['w_down'] = _mm([(sv['act'], dh3)], 'tn', f32, name=name + "dw_down")
    dgate, dup = _ffn_bwd_act(dh3, W['w_down'][l], sv['gate'], sv['up'], name=name + "ffn_bwd_act")
    dhn3 = _mm([(dgate, W['w_gate'][l]), (dup, W['w_up'][l])], 'nn', f32, name=name + "mm_dffn")
    G['w_gate'] = _mm([(dgate, sv['hn3'])], 'tn', f32, name=name + "dw_gate")
    G['w_up'] = _mm([(dup, sv['hn3'])], 'tn', f32, name=name + "dw_up")
    dh2, dg = _rms_bwd(sv['h2'], W['norm_ffn_g'][l], dhn3, res=dh3, name=name + "rmsb_ffn")
    G['norm_ffn_g'] = dg[0]
    G['w_xo'] = _mm([(sv['ox'], dh2)], 'tn', f32, name=name + "dw_xo")
    dox = _mm([(dh2, W['w_xo'][l])], 'nt', bf16, name=name + "mm_dxo")
    dqx, dkvx = _xattn_bwd(sv['qx'], sv['kvx'], dox, name=name + "xattn_bwd")
    G['w_xq'] = _mm([(sv['hn2'], dqx)], 'tn', f32, name=name + "dw_xq")
    G['w_xkv'] = _mm([(sv['memn'], dkvx)], 'tn', f32, name=name + "dw_xkv")
    dhn2 = _mm([(dqx, W['w_xq'][l])], 'nt', f32, name=name + "mm_dxq")
    dmemn = _mm([(dkvx, W['w_xkv'][l])], 'nt', f32, name=name + "mm_dxkv")
    dh1, dg = _rms_bwd(sv['h1'], W['norm_x_g'][l], dhn2, res=dh2, name=name + "rmsb_x")
    G['norm_x_g'] = dg[0]
    _, dg = _rms_bwd(memn_in, W['mem_norm_g'][l], dmemn, name=name + "rmsb_mem")
    G['mem_norm_g'] = dg[0]
    dwo_a = _mm([(sv['an'], dh1)], 'tn', f32, name=name + "dw_out_a")
    dwo_s = _mm([(sv['sn'], dh1)], 'tn', f32, name=name + "dw_out_s")
    G['w_out'] = _unpack_wout(jnp.concatenate([dwo_a, dwo_s], axis=0))
    dmixed = _mm([(dh1, p['wout'])], 'nt', f32, name=name + "mm_dout")
    dy, dz, dg, db = _glu_bwd(sv['y'], sv['z'], dmixed, W['ssm_w_glu'][l], W['ssm_out_g'][l], name=name + "glu_bwd")
    G['ssm_out_g'], G['ssm_b_glu'] = dg[0], db[0]
    G['ssm_w_glu'] = _mm([(sv['ge'], dz)], 'tn', f32, name=name + "dw_glu")
    du, dbm, dcm, da, dd = _ssm_bwd(sv['proj'], dy, sv['xc'], sv['bmb'], sv['cmb'], sv['a'], W['ssm_d'][l],
                                    name=name + "ssm_bwd")
    G['ssm_d'] = dd[0]
    G['ssm_raw'] = (dbm, dcm, da)
    doh, dg, delta = _rms_bwd(sv['oh'], p['attn_g'], dmixed, width=MLA_PAD, n_valid=MLA_WIDTH, delta=True,
                              out_dtype=bf16, name=name + "rmsb_attn")
    G['attn_out_g'] = _unpad_gain(dg[0])
    dqh, dkh, dvh, carried = _attn_bwd(sv['qh'], sv['kh'], sv['vh'], doh, sv['lse'], delta, plan=plan,
                                       name=name + "attn_bwd")
    dq, dkv, dkr = _rope_bwd(dqh, dkh, dvh, tabs, name=name + "rope_bwd")
    G['w_uq'] = _unpad_heads(_mm([(sv['cqn'], dq)], 'tn', f32, name=name + "dw_uq"), QK_NOPE + QK_ROPE, _Q_PIECES)
    dwukv = _mm([(sv['ckvn'], dkv)], 'tn', f32, name=name + "dw_ukv")
    G['w_ukv'] = (_unpad_heads(dwukv[:, :MLA_PAD], QK_NOPE + V_HEAD, _K_PIECES)
                  + _unpad_heads(dwukv[:, MLA_PAD:], QK_NOPE + V_HEAD, _V_PIECES))
    dcqn = _mm([(dq, p['wuq'])], 'nt', f32, name=name + "mm_duq")
    dckvn = _mm([(dkv, p['wukv'])], 'nt', f32, name=name + "mm_dukv")
    dcq, dg = _rms_bwd(sv['proj'], W['q_norm_g'][l], dcqn, col0=0, width=Q_LORA, out_dtype=bf16, name=name + "rmsb_q")
    G['q_norm_g'] = dg[0]
    dckv, dg = _rms_bwd(sv['proj'], W['kv_norm_g'][l], dckvn, col0=Q_LORA, width=KV_LORA, out_dtype=bf16,
                        name=name + "rmsb_kv")
    G['kv_norm_g'] = dg[0]
    dproj = jnp.concatenate([dcq, dckv, dkr, du.astype(bf16)], axis=1)
    G['w_in'] = _unpack_win(_mm([(sv['xn'], dproj)], 'tn', f32, name=name + "dw_in"))
    dxn = _mm([(dproj, p['win'])], 'nt', f32, name=name + "mm_din")
    dh0, dg = _rms_bwd(sv['h0'], W['norm_mix_g'][l], dxn, res=dh1, name=name + "rmsb_mix")
    G['norm_mix_g'] = dg[0]
    return dh0, G, carried


def _local_step(x, mem, positions, target, W, later=None, hook=None):
    tabs = _rope_tables(positions)
    ssm_in = [(W['ssm_lambda_re'][l], W['ssm_lambda_im'][l], W['ssm_log_dt'][l], W['ssm_b_re'][l], W['ssm_b_im'][l],
               W['ssm_c_re'][l], W['ssm_c_im'][l]) for l in range(DEPTH)]
    preps = [jax.vjp(_ssm_prep, *ssm_in[l]) for l in range(DEPTH)]
    W = dict(W)
    W['ssm'] = [preps[l][0] for l in range(DEPTH)]
    h = x
    saved = []
    for l in range(DEPTH):
        plan = later[0] if (later is not None and l == 0) else None
        h, sv, carried = _forward_layer(h, mem, tabs, W, l, f"l{l}_", plan)
        saved.append(sv)
        if plan is not None:
            for n, t in later[1](carried).items():
                W[n] = [W[n][0], t]
    dh, dgf, loss = _loss_head(h, W['final_norm_g'], target)
    grads = [None] * DEPTH
    plan = None
    for l in reversed(range(DEPTH)):
        dh, G, carried = _backward_layer(dh, saved[l], mem, tabs, W, l, plan, f"l{l}b_")
        if plan is not None:
            hook[1](carried)
        dbm, dcm, da = G.pop('ssm_raw')
        names = ['ssm_lambda_re', 'ssm_lambda_im', 'ssm_log_dt', 'ssm_b_re', 'ssm_b_im', 'ssm_c_re', 'ssm_c_im']
        for n, g in zip(names, preps[l][1]((dbm, dcm, da))):
            G[n] = g
        grads[l] = G
        plan = hook[0](l, G) if (hook is not None and l > 0) else None
    out = {n: [grads[l][n] for l in range(DEPTH)] if n in SHARDED else jnp.stack([grads[l][n] for l in range(DEPTH)])
           for n in grads[0]}
    out['final_norm_g'] = dgf[0]
    return loss[0, 0], dh, out


_HBM = pl.BlockSpec(memory_space=pltpu.HBM)


def _me():
    return lax.axis_index("x"), lax.axis_index("y"), lax.axis_index("c")


def _chip_peers(x, y, c):
    devs = [(1 - x, y, c), (x, 1 - y, c), (1 - x, 1 - y, c)]
    return devs, [2 * d[0] + d[1] for d in devs]


def _gather_plan(xs, half_first):
    n = len(xs)
    if half_first:
        ins = [t.reshape(2, 1, *t.shape[1:]) for t in xs]
        outs = [jax.ShapeDtypeStruct((2, 4, *t.shape[1:]), t.dtype) for t in xs]
    else:
        ins = [t.reshape(1, 2, t.shape[0] // 2, t.shape[1]) for t in xs]
        outs = [jax.ShapeDtypeStruct((4, 2, t.shape[0] // 2, t.shape[1]), t.dtype) for t in xs]

    def own(ref, h):
        return ref.at[h] if half_first else ref.at[:, h]

    def slot(ref, h, j):
        return ref.at[h, pl.ds(j, 1)] if half_first else ref.at[pl.ds(j, 1), h]

    def copies(src, dst, send, recv):
        x, y, c = _me()
        jme = 2 * x + y
        devs, js = _chip_peers(x, y, c)
        half, other = pl.ds(c, 1), pl.ds(1 - c, 1)
        mk = pltpu.make_async_remote_copy
        for i in range(n):
            for k in range(3):
                out_cp = mk(own(src[i], half), slot(dst[i], half, jme), send.at[6 * i + k], recv.at[6 * i + k],
                            device_id=devs[k], device_id_type=MESH)
                in_cp = mk(own(src[i], half), slot(dst[i], half, js[k]), send.at[6 * i + k], recv.at[6 * i + k],
                           device_id=devs[k], device_id_type=MESH)
                pass_cp = mk(slot(dst[i], half, js[k]), slot(dst[i], half, js[k]), send.at[6 * i + 3 + k],
                             recv.at[6 * i + 3 + k], device_id=(x, y, 1 - c), device_id_type=MESH)
                got_cp = mk(slot(dst[i], other, js[k]), slot(dst[i], other, js[k]), send.at[6 * i + 3 + k],
                            recv.at[6 * i + 3 + k], device_id=(x, y, 1 - c), device_id_type=MESH)
                yield out_cp, in_cp, pass_cp, got_cp

    def start(*refs):
        for out_cp, _, _, _ in copies(*refs):
            out_cp.start()

    def forward(*refs):
        for _, in_cp, pass_cp, _ in copies(*refs):
            in_cp.wait_recv()
            pass_cp.start()

    def finish(*refs):
        for out_cp, _, pass_cp, got_cp in copies(*refs):
            got_cp.wait_recv()
            out_cp.wait_send()
            pass_cp.wait_send()

    return dict(n=n, ins=ins, outs=outs, nsem=6 * n, start=start, forward=forward, finish=finish)


def _plan_refs(plan, refs):
    n = plan['n']
    return refs[:n], refs[n:2 * n], refs[2 * n], refs[2 * n + 1]


def _run_plan(plan, name):
    n = plan['n']

    def body(*refs):
        args = _plan_refs(plan, refs)
        plan['start'](*args)
        plan['forward'](*args)
        plan['finish'](*args)

    return _pc(body, in_specs=[_HBM] * n, out_specs=[_HBM] * n, out_shape=plan['outs'],
               scratch_shapes=[pltpu.SemaphoreType.DMA((plan['nsem'],)), pltpu.SemaphoreType.DMA((plan['nsem'],))],
               compiler_params=pltpu.CompilerParams(has_side_effects=True), name=name)(*plan['ins'])


def _fill_own(gathered, own, half_first):
    jme = (2 * lax.axis_index("x") + lax.axis_index("y")).astype(jnp.int32)
    zero = jnp.int32(0)
    if half_first:
        return lax.dynamic_update_slice(gathered, own[:, None], (zero, jme, zero, zero))
    return lax.dynamic_update_slice(gathered, own.reshape(1, *gathered.shape[1:]), (jme, zero, zero, zero))


def _exchange_halves(gs, name):
    n = len(gs)

    def body(*refs):
        src, dst = refs[:n], refs[n:2 * n]
        send, recv = refs[2 * n:]
        x, y, c = _me()
        cps = []
        for i in range(n):
            cp = pltpu.make_async_remote_copy(src[i].at[:, pl.ds(1 - c, 1)], dst[i], send.at[i], recv.at[i],
                                              device_id=(x, y, 1 - c), device_id_type=MESH)
            cp.start()
            cps.append(cp)
        for cp in cps:
            cp.wait()

    outs = [jax.ShapeDtypeStruct((4, 1, *g.shape[2:]), g.dtype) for g in gs]
    return _pc(body, in_specs=[_HBM] * n, out_specs=[_HBM] * n, out_shape=outs,
               scratch_shapes=[pltpu.SemaphoreType.DMA((n,)), pltpu.SemaphoreType.DMA((n,))],
               compiler_params=pltpu.CompilerParams(has_side_effects=True), name=name)(*gs)


def _scatter_plan(ps):
    n = len(ps)

    def copies(src, dst, send, recv):
        x, y, c = _me()
        devs, js = _chip_peers(x, y, c)
        for i in range(n):
            for k in range(3):
                yield pltpu.make_async_remote_copy(src[i].at[pl.ds(js[k], 1)], dst[i].at[k], send.at[3 * i + k],
                                                   recv.at[3 * i + k], device_id=devs[k], device_id_type=MESH)

    def start(*refs):
        for cp in copies(*refs):
            cp.start()

    def finish(*refs):
        for cp in copies(*refs):
            cp.wait()

    outs = [jax.ShapeDtypeStruct((3, 1, *p.shape[1:]), p.dtype) for p in ps]
    return dict(n=n, ins=list(ps), outs=outs, nsem=3 * n, start=start, forward=lambda *refs: None, finish=finish)


def _swap_sibling(hs, name):
    n = len(hs)

    def body(*refs):
        src, dst = refs[:n], refs[n:2 * n]
        send, recv = refs[2 * n:]
        x, y, c = _me()
        cps = []
        for i in range(n):
            cp = pltpu.make_async_remote_copy(src[i], dst[i], send.at[i], recv.at[i], device_id=(x, y, 1 - c),
                                              device_id_type=MESH)
            cp.start()
            cps.append(cp)
        for cp in cps:
            cp.wait()

    outs = [jax.ShapeDtypeStruct(h.shape, h.dtype) for h in hs]
    return _pc(body, in_specs=[_HBM] * n, out_specs=[_HBM] * n, out_shape=outs,
               scratch_shapes=[pltpu.SemaphoreType.DMA((n,)), pltpu.SemaphoreType.DMA((n,))],
               compiler_params=pltpu.CompilerParams(has_side_effects=True), name=name)(*hs)


ELEMWISE_VMEM_BUDGET = 24 * 1024 * 1024


def _row_tile(r, n, narrays):
    limit = ELEMWISE_VMEM_BUDGET // (2 * 4 * narrays * n)
    best = SUBLANES
    for t in range(16, r + 1, 16):
        if r % t == 0 and t <= limit:
            best = t
    return best


def _add_half(g, r1, cidx, name):
    _, _, r, n = g.shape
    tr = _row_tile(r, n, 3)

    def body(c_ref, g_ref, r_ref, o_ref):
        o_ref[...] = (g_ref[...] + r_ref[...]).astype(GRAD_TRANSIT)

    blk = lambda f: pl.BlockSpec((None, None, tr, n), f)
    gs = pltpu.PrefetchScalarGridSpec(
        num_scalar_prefetch=1, grid=(4, r // tr),
        in_specs=[blk(lambda j, i, c: (j, c[0], i, 0)), blk(lambda j, i, c: (j, 0, i, 0))],
        out_specs=pl.BlockSpec((None, tr, n), lambda j, i, c: (j, i, 0)))
    return _pc(body, grid_spec=gs, out_shape=jax.ShapeDtypeStruct((4, r, n), GRAD_TRANSIT),
               compiler_params=_cp(("parallel", "parallel")), name=name)(cidx, g, r1)


def _add_chips(p, r3, jidx, name):
    _, r, n = p.shape
    tr = _row_tile(r, n, 5)

    def body(j_ref, p_ref, a_ref, b_ref, c_ref, o_ref):
        o_ref[...] = ((p_ref[...].astype(f32) + a_ref[...].astype(f32)) + b_ref[...].astype(f32)) + c_ref[...].astype(f32)

    rblk = lambda k: pl.BlockSpec((None, None, tr, n), lambda i, j: (k, 0, i, 0))
    gs = pltpu.PrefetchScalarGridSpec(
        num_scalar_prefetch=1, grid=(r // tr,),
        in_specs=[pl.BlockSpec((None, tr, n), lambda i, j: (j[0], i, 0)), rblk(0), rblk(1), rblk(2)],
        out_specs=pl.BlockSpec((tr, n), lambda i, j: (i, 0)))
    return _pc(body, grid_spec=gs, out_shape=jax.ShapeDtypeStruct((r, n), f32),
               compiler_params=_cp(("parallel",)), name=name)(jidx, p, r3, r3, r3)


def _adamw_halves(w, mine, theirs, m, v, cidx, name):
    L, r, n = w.shape
    r2 = r // 2
    tr = _row_tile(r2, n, 11)
    c1 = 1.0 / (1.0 - ADAM_B1 ** ADAM_STEP)
    c2 = 1.0 / (1.0 - ADAM_B2 ** ADAM_STEP)

    def body(c_ref, w_ref, a0_ref, b0_ref, a1_ref, b1_ref, m_ref, v_ref, g_ref, d_ref, mo_ref, vo_ref):
        l, hf = pl.program_id(0), pl.program_id(1)
        own = hf == c_ref[0]
        gv = jnp.where(l == 0, jnp.where(own, a0_ref[...], b0_ref[...]), jnp.where(own, a1_ref[...], b1_ref[...]))
        m2 = ADAM_B1 * m_ref[...] + (1.0 - ADAM_B1) * gv
        v2 = ADAM_B2 * v_ref[...] + (1.0 - ADAM_B2) * (gv * gv)
        g_ref[...] = gv
        d_ref[...] = -ADAM_LR * ((m2 * c1) / (jnp.sqrt(v2 * c2) + ADAM_EPS) + ADAM_WD * w_ref[...])
        mo_ref[...] = m2
        vo_ref[...] = v2

    full = pl.BlockSpec((None, None, tr, n), lambda l, hf, i, c: (l, hf, i, 0))
    half = pl.BlockSpec((tr, n), lambda l, hf, i, c: (i, 0))
    gs = pltpu.PrefetchScalarGridSpec(num_scalar_prefetch=1, grid=(L, 2, r2 // tr),
                                      in_specs=[full, half, half, half, half, full, full], out_specs=[full] * 4)
    four = lambda t: t.reshape(L, 2, r2, n)
    outs = _pc(body, grid_spec=gs, out_shape=[jax.ShapeDtypeStruct((L, 2, r2, n), f32)] * 4,
               compiler_params=_cp(("parallel", "parallel", "parallel")), name=name)(
        cidx, four(w), mine[0], theirs[0], mine[1], theirs[1], four(m), four(v))
    return [t.reshape(w.shape) for t in outs]


def _adamw_whole(w, g, m, v, name):
    c1 = 1.0 / (1.0 - ADAM_B1 ** ADAM_STEP)
    c2 = 1.0 / (1.0 - ADAM_B2 ** ADAM_STEP)

    def body(w_ref, g_ref, m_ref, v_ref, d_ref, mo_ref, vo_ref):
        gv = g_ref[...]
        m2 = ADAM_B1 * m_ref[...] + (1.0 - ADAM_B1) * gv
        v2 = ADAM_B2 * v_ref[...] + (1.0 - ADAM_B2) * (gv * gv)
        d_ref[...] = -ADAM_LR * ((m2 * c1) / (jnp.sqrt(v2 * c2) + ADAM_EPS) + ADAM_WD * w_ref[...])
        mo_ref[...] = m2
        vo_ref[...] = v2

    return _pc(body, out_shape=[jax.ShapeDtypeStruct(w.shape, f32)] * 3, name=name)(w, g, m, v)


def _full_from_gathered(name, t):
    r, n = 2 * t.shape[2], t.shape[3]
    if SHARDED[name] == 1 or name in TRANSPOSED:
        return t.reshape(4 * r, n)
    return t.reshape(4, r, n).transpose(1, 0, 2).reshape(r, 4 * n)


def _shard_major(name, g):
    R, C = g.shape
    if SHARDED[name] == 1 or name in TRANSPOSED:
        return g.reshape(4, 2, R // 8, C)
    return g.reshape(R, 4, C // 4).transpose(1, 0, 2).reshape(4, 2, R // 2, C // 4)


_SMALL_ROWS = 288


def _pack_small(d):
    flat = jnp.concatenate([d[n].reshape(-1) for n in SMALL])
    total = 2 * 4 * _SMALL_ROWS * LANES
    flat = jnp.concatenate([flat, jnp.zeros((total - flat.shape[0],), f32)])
    return flat.reshape(4, 2, _SMALL_ROWS, LANES)


def _unpack_small(t, like):
    flat = t.reshape(-1)
    out, off = {}, 0
    for n in SMALL:
        sz = math.prod(like[n].shape)
        out[n] = flat[off:off + sz].reshape(like[n].shape)
        off += sz
    return out


def kernel(x, mem, positions, norm_mix_g, w_in, q_norm_g, w_uq, kv_norm_g, w_ukv, ssm_lambda_re, ssm_lambda_im, ssm_log_dt, ssm_b_re, ssm_b_im, ssm_c_re, ssm_c_im, ssm_d, ssm_w_glu, ssm_b_glu, attn_out_g, ssm_out_g, w_out, norm_x_g, mem_norm_g, w_xq, w_xkv, w_xo, norm_ffn_g, w_gate, w_up, w_down, final_norm_g, loss_target, m_norm_mix_g, m_w_in, m_q_norm_g, m_w_uq, m_kv_norm_g, m_w_ukv, m_ssm_lambda_re, m_ssm_lambda_im, m_ssm_log_dt, m_ssm_b_re, m_ssm_b_im, m_ssm_c_re, m_ssm_c_im, m_ssm_d, m_ssm_w_glu, m_ssm_b_glu, m_attn_out_g, m_ssm_out_g, m_w_out, m_norm_x_g, m_mem_norm_g, m_w_xq, m_w_xkv, m_w_xo, m_norm_ffn_g, m_w_gate, m_w_up, m_w_down, m_final_norm_g, v_norm_mix_g, v_w_in, v_q_norm_g, v_w_uq, v_kv_norm_g, v_w_ukv, v_ssm_lambda_re, v_ssm_lambda_im, v_ssm_log_dt, v_ssm_b_re, v_ssm_b_im, v_ssm_c_re, v_ssm_c_im, v_ssm_d, v_ssm_w_glu, v_ssm_b_glu, v_attn_out_g, v_ssm_out_g, v_w_out, v_norm_x_g, v_mem_norm_g, v_w_xq, v_w_xkv, v_w_xo, v_norm_ffn_g, v_w_gate, v_w_up, v_w_down, v_final_norm_g):
    given = dict(locals())
    swap = lambda n, t: jnp.swapaxes(t, *TRANSPOSED[n]) if n in TRANSPOSED else t
    w = {n: swap(n, given[n]) for n in WEIGHTS}
    m = {n: swap(n, given["m_" + n]) for n in WEIGHTS}
    v = {n: swap(n, given["v_" + n]) for n in WEIGHTS}
    big = list(SHARDED)

    shards = [w[n].astype(bf16) for n in big]

    def full(results, l):
        return {n: _full_from_gathered(n, _fill_own(t, s[l], False)) for n, t, s in zip(big, results, shards)}

    plans = [_gather_plan([s[l] for s in shards], False) for l in range(DEPTH)]
    W = {n: [t, None] for n, t in full(_run_plan(plans[0], "allgather_weights_l0"), 0).items()}
    W.update({n: w[n] for n in SMALL})

    cidx = lax.axis_index("c").astype(jnp.int32).reshape(1)
    jidx = (2 * lax.axis_index("x") + lax.axis_index("y")).astype(jnp.int32).reshape(1)
    early = {}

    def chip_sums(gs, names, tag):
        r1 = _exchange_halves(gs, f"grad_exchange_halves_{tag}")
        return [_add_half(g, r, cidx, f"grad_add_half_{tag}_{n}") for n, g, r in zip(names, gs, r1)]

    def launch(l, G):
        early['ps'] = chip_sums([_shard_major(n, G[n]) for n in big], big, f"l{l}")
        return _scatter_plan(early['ps'])

    def done(results):
        early['r3'] = results

    loss, dx, grads = _local_step(x[0], mem[0], positions[0], loss_target[0], W,
                                  later=(plans[1], lambda results: full(results, 1)), hook=(launch, done))
    loss = lax.psum(loss, ("x", "y", "c"))

    names = big + ["small"]
    ps0 = chip_sums([_shard_major(n, grads[n][0]) for n in big] + [_pack_small(grads)], names, "l0")
    r3_0 = _run_plan(_scatter_plan(ps0), "grad_scatter_chips_l0")
    hs0 = [_add_chips(p, r, jidx, f"grad_add_chips_l0_{n}") for n, p, r in zip(names, ps0, r3_0)]
    hs1 = [_add_chips(p, r, jidx, f"grad_add_chips_l1_{n}") for n, p, r in zip(big, early['ps'], early['r3'])]
    ts = _swap_sibling(hs0 + hs1, "grad_swap_sibling")
    ts0, ts1 = ts[:len(hs0)], ts[len(hs0):]

    out_g, out_d, out_m, out_v = {}, {}, {}, {}
    for i, n in enumerate(big):
        out_g[n], out_d[n], out_m[n], out_v[n] = _adamw_halves(w[n], (hs0[i], hs1[i]), (ts0[i], ts1[i]), m[n], v[n],
                                                               cidx, f"adamw_{n}")
    both = jnp.stack([hs0[-1], ts0[-1]])
    piece = jnp.where(cidx[0] == 0, both, both[::-1]).reshape(2 * _SMALL_ROWS, LANES)
    gsm = _fill_own(_run_plan(_gather_plan([piece], False), "allgather_small")[0], piece, False)
    out_g.update(_unpack_small(gsm, w))
    for n in SMALL:
        two = lambda t: t.reshape(1, -1) if t.ndim == 1 else t
        d_, m_, v_ = _adamw_whole(two(w[n]), two(out_g[n]), two(m[n]), two(v[n]), f"adamw_{n}")
        out_d[n], out_m[n], out_v[n] = (t.reshape(w[n].shape) for t in (d_, m_, v_))

    outs = [[swap(n, d[n]) for n in WEIGHTS] for d in (out_g, out_d, out_m, out_v)]
    return (loss, dx.reshape(x.shape), *outs[0], *outs[1], *outs[2], *outs[3])
```

```python
import functools
import math

import jax
import jax.numpy as jnp
from jax import lax
from jax.experimental import pallas as pl
from jax.experimental.pallas import tpu as pltpu

f32, bf16 = jnp.float32, jnp.bfloat16

D_MODEL = 1024
DEPTH = 2
MLA_HEADS = 8
QK_NOPE = 64
QK_ROPE = 32
V_HEAD = 64
Q_LORA = 256
KV_LORA = 128
MLA_WIDTH = MLA_HEADS * V_HEAD
ROPE_THETA = 10000.0
SSM_WIDTH = 512
SSM_GROUP = 16
SSM_GROUPS = 32
SSM_STATE = 64
IN_WIDTH = Q_LORA + KV_LORA + QK_ROPE + SSM_WIDTH
X_HEADS = 4
X_HEAD_DIM = D_MODEL // X_HEADS
D_FF = 2816
EPS = 1e-6
ADAM_LR, ADAM_B1, ADAM_B2, ADAM_EPS, ADAM_WD, ADAM_STEP = 0.001, 0.9, 0.999, 1e-08, 0.01, 10

LANES = 128
SUBLANES = 8
HEAD_PAD = 128
MLA_PAD = MLA_HEADS * HEAD_PAD
SSM_MACRO = 4
MACRO_CH = SSM_WIDTH // SSM_MACRO
MACRO_ST = SSM_GROUPS // SSM_MACRO * SSM_STATE
VMEM_LIMIT = 56 * 1024 * 1024
GRAD_TRANSIT = bf16

WEIGHTS = ['norm_mix_g', 'w_in', 'q_norm_g', 'w_uq', 'kv_norm_g', 'w_ukv', 'ssm_lambda_re', 'ssm_lambda_im',
           'ssm_log_dt', 'ssm_b_re', 'ssm_b_im', 'ssm_c_re', 'ssm_c_im', 'ssm_d', 'ssm_w_glu', 'ssm_b_glu',
           'attn_out_g', 'ssm_out_g', 'w_out', 'norm_x_g', 'mem_norm_g', 'w_xq', 'w_xkv', 'w_xo', 'norm_ffn_g',
           'w_gate', 'w_up', 'w_down', 'final_norm_g']
SHARDED = {'w_in': 1, 'w_uq': 2, 'w_ukv': 2, 'ssm_w_glu': 1, 'w_out': 1, 'w_xq': 1, 'w_xkv': 2, 'w_xo': 1,
           'w_gate': 2, 'w_up': 2, 'w_down': 1}
SMALL = [n for n in WEIGHTS if n not in SHARDED]
EARLY_WEIGHTS = ('w_in', 'w_uq', 'w_ukv')
TRANSPOSED = {'w_gate': (1, 2), 'w_up': (1, 2), 'ssm_b_re': (2, 3), 'ssm_b_im': (2, 3)}
MESH = pl.DeviceIdType.MESH


def _pc(body, **kw):
    return pl.pallas_call(body, **kw)


def _pick(n, prefs):
    for p in prefs:
        if n % p == 0:
            return p
    return n


def _cp(sem=None):
    return pltpu.CompilerParams(dimension_semantics=sem, vmem_limit_bytes=VMEM_LIMIT)


_TILE_CANDS = (1024, 1408, 512, 256, 128)
MM_VMEM_BUDGET = 40 * 1024 * 1024


def _mm_tiles(M, K, N, a_bytes, b_bytes, o_bytes, npair, has_res, need_acc):
    best = None
    for tm in _TILE_CANDS:
        for tk in _TILE_CANDS:
            if M % tm or K % tk:
                continue
            vm = npair * (2 * tm * tk * a_bytes + 2 * tk * N * b_bytes) + 2 * tm * N * o_bytes
            vm += tm * N * 4 * (1 + need_acc + 2 * has_res)
            if a_bytes == 4:
                vm += npair * tm * tk * 2
            if b_bytes == 4:
                vm += npair * tk * N * 2
            if vm <= MM_VMEM_BUDGET and (best is None or tm * tk > best[0]):
                best = (tm * tk, tm, tk)
    if best is None:
        return _pick(M, (256, 128)), _pick(K, (256, 128))
    return best[1], best[2]


def _mm(pairs, mode, out_dtype, res=None, name="mm"):
    a0, b0 = pairs[0]
    if mode == 'nn':
        (M, K), N = a0.shape, b0.shape[1]
        dims = (((1,), (0,)), ((), ()))
    elif mode == 'nt':
        (M, K), N = a0.shape, b0.shape[0]
        dims = (((1,), (1,)), ((), ()))
    else:
        (K, M), N = a0.shape, b0.shape[1]
        dims = (((0,), (0,)), ((), ()))
    npair = len(pairs)
    has_res = res is not None
    direct = out_dtype == f32
    tm, tk = _mm_tiles(M, K, N, a0.dtype.itemsize, b0.dtype.itemsize, jnp.dtype(out_dtype).itemsize, npair, has_res,
                       not direct)
    nk = K // tk

    def body(*refs):
        ins = refs[:2 * npair]
        res_ref = refs[2 * npair] if has_res else None
        o_ref = refs[2 * npair + has_res]
        acc = o_ref if direct else refs[2 * npair + has_res + 1]
        k = pl.program_id(1)
        s = None
        for p in range(npair):
            d = lax.dot_general(ins[2 * p][...].astype(bf16), ins[2 * p + 1][...].astype(bf16), dims,
                                preferred_element_type=f32)
            s = d if s is None else s + d

        @pl.when(k == 0)
        def _():
            acc[...] = s

        @pl.when(k > 0)
        def _():
            acc[...] += s

        if has_res or not direct:
            @pl.when(k == nk - 1)
            def _():
                r = acc[...]
                if has_res:
                    r = r + res_ref[...]
                o_ref[...] = r.astype(out_dtype)

    if mode == 'nn':
        a_spec = pl.BlockSpec((tm, tk), lambda i, k: (i, k))
        b_spec = pl.BlockSpec((tk, N), lambda i, k: (k, 0))
    elif mode == 'nt':
        a_spec = pl.BlockSpec((tm, tk), lambda i, k: (i, k))
        b_spec = pl.BlockSpec((N, tk), lambda i, k: (0, k))
    else:
        a_spec = pl.BlockSpec((tk, tm), lambda i, k: (k, i))
        b_spec = pl.BlockSpec((tk, N), lambda i, k: (k, 0))
    o_spec = pl.BlockSpec((tm, N), lambda i, k: (i, 0))
    in_specs = [a_spec, b_spec] * npair + ([o_spec] if has_res else [])
    args = [t for p in pairs for t in p] + ([res] if has_res else [])
    return _pc(body, grid=(M // tm, nk), in_specs=in_specs, out_specs=o_spec,
               out_shape=jax.ShapeDtypeStruct((M, N), out_dtype),
               scratch_shapes=[] if direct else [pltpu.VMEM((tm, N), f32)],
               compiler_params=_cp(("parallel", "arbitrary")), name=name)(*args)


def _rms_fwd(x, g, *, col0=0, width=None, n_valid=None, out_dtype=bf16, name="rms_fwd"):
    S = x.shape[0]
    width = width or x.shape[1]
    n_valid = n_valid or width
    ts = _pick(S, (512, 256, 128))
    cb = col0 // width

    def body(x_ref, g_ref, o_ref):
        xv = x_ref[...]
        ms = jnp.sum(xv * xv, axis=-1, keepdims=True) * (1.0 / n_valid)
        o_ref[...] = (xv * lax.rsqrt(ms + EPS) * g_ref[...]).astype(out_dtype)

    return _pc(body, grid=(S // ts,),
               in_specs=[pl.BlockSpec((ts, width), lambda i: (i, cb)), pl.BlockSpec((1, width), lambda i: (0, 0))],
               out_specs=pl.BlockSpec((ts, width), lambda i: (i, 0)),
               out_shape=jax.ShapeDtypeStruct((S, width), out_dtype),
               compiler_params=_cp(("parallel",)), name=name)(x, g.reshape(1, width))


def _rms_bwd(x, g, dy, *, col0=0, dcol0=0, width=None, n_valid=None, res=None, out_dtype=f32, delta=False,
             name="rms_bwd"):
    S = x.shape[0]
    width = width or x.shape[1]
    n_valid = n_valid or width
    ts = _pick(S, (512, 256, 128))
    cb, dcb = col0 // width, dcol0 // width
    has_res = res is not None

    def body(*refs):
        x_ref, g_ref, dy_ref = refs[:3]
        res_ref = refs[3] if has_res else None
        outs = refs[3 + has_res:]
        dx_ref, dg_ref = outs[0], outs[1]
        i = pl.program_id(0)
        xv = x_ref[...]
        gv = g_ref[...]
        dyv = dy_ref[...].astype(f32)
        rstd = lax.rsqrt(jnp.sum(xv * xv, axis=-1, keepdims=True) * (1.0 / n_valid) + EPS)
        xh = xv * rstd
        dxh = dyv * gv
        mean = jnp.sum(dxh * xh, axis=-1, keepdims=True) * (1.0 / n_valid)
        dx = rstd * (dxh - xh * mean)
        if delta:
            d_ref = outs[2]
            for h in range(width // LANES):
                sl = slice(h * LANES, (h + 1) * LANES)
                dsum = jnp.sum(dx[:, sl] * xv[:, sl], axis=-1, keepdims=True)
                d_ref[:, sl] = jnp.broadcast_to(dsum, (ts, LANES))
        if has_res:
            dx = dx + res_ref[...]
        dx_ref[...] = dx.astype(out_dtype)

        @pl.when(i == 0)
        def _():
            dg_ref[...] = jnp.zeros_like(dg_ref)

        dg_ref[...] += jnp.sum(dyv * xh, axis=0, keepdims=True)

    blk = lambda c: pl.BlockSpec((ts, width), lambda i: (i, c))
    in_specs = [blk(cb), pl.BlockSpec((1, width), lambda i: (0, 0)), blk(dcb)] + ([blk(0)] if has_res else [])
    out_specs = [blk(0), pl.BlockSpec((1, width), lambda i: (0, 0))] + ([blk(0)] if delta else [])
    out_shape = [jax.ShapeDtypeStruct((S, width), out_dtype), jax.ShapeDtypeStruct((1, width), f32)] + (
        [jax.ShapeDtypeStruct((S, width), f32)] if delta else [])
    args = [x, g.reshape(1, width), dy] + ([res] if has_res else [])
    return _pc(body, grid=(S // ts,), in_specs=in_specs, out_specs=out_specs, out_shape=out_shape,
               compiler_params=_cp(("arbitrary",)), name=name)(*args)


def _loss_head(h, g, target, name="loss_head"):
    S, D = h.shape
    ts = _pick(S, (512, 256, 128))

    def body(h_ref, g_ref, t_ref, dh_ref, dg_ref, loss_ref):
        i = pl.program_id(0)
        xv = h_ref[...]
        gv = g_ref[...]
        rstd = lax.rsqrt(jnp.sum(xv * xv, axis=-1, keepdims=True) * (1.0 / D) + EPS)
        xh = xv * rstd
        err = xh * gv - t_ref[...]
        dyv = err * (1.0 / D)
        dxh = dyv * gv
        mean = jnp.sum(dxh * xh, axis=-1, keepdims=True) * (1.0 / D)
        dh_ref[...] = rstd * (dxh - xh * mean)

        @pl.when(i == 0)
        def _():
            dg_ref[...] = jnp.zeros_like(dg_ref)
            loss_ref[...] = jnp.zeros_like(loss_ref)

        dg_ref[...] += jnp.sum(dyv * xh, axis=0, keepdims=True)
        part = jnp.sum(jnp.sum(err * err, axis=-1, keepdims=True), axis=0, keepdims=True) * (0.5 / D)
        loss_ref[...] += jnp.broadcast_to(part, (1, LANES))

    blk = pl.BlockSpec((ts, D), lambda i: (i, 0))
    row = pl.BlockSpec((1, D), lambda i: (0, 0))
    return _pc(body, grid=(S // ts,), in_specs=[blk, row, blk],
               out_specs=[blk, row, pl.BlockSpec((1, LANES), lambda i: (0, 0))],
               out_shape=[jax.ShapeDtypeStruct((S, D), f32), jax.ShapeDtypeStruct((1, D), f32),
                          jax.ShapeDtypeStruct((1, LANES), f32)],
               compiler_params=_cp(("arbitrary",)), name=name)(h, g.reshape(1, D), target)


def _rope_apply(x, tc, s1, s2):
    return x * tc + pltpu.roll(x, LANES - 16, 1) * s1 + pltpu.roll(x, 16, 1) * s2


def _rope_apply_t(dy, tc, s1, s2):
    return dy * tc + pltpu.roll(dy * s1, 16, 1) + pltpu.roll(dy * s2, LANES - 16, 1)


def _rope_fwd(q, kv, proj, tabs, name="rope_fwd"):
    S = q.shape[0]
    ts = _pick(S, (512, 256, 128))
    scale = (QK_NOPE + QK_ROPE) ** -0.5

    def body(q_ref, kk_ref, kvv_ref, kr_ref, tc_ref, s1_ref, s2_ref, qh_ref, kh_ref, vh_ref):
        tc, s1, s2 = tc_ref[...], s1_ref[...], s2_ref[...]
        krr = _rope_apply(pltpu.roll(kr_ref[...], QK_NOPE, 1), tc, s1, s2)
        for h in range(MLA_HEADS):
            sl = slice(h * HEAD_PAD, (h + 1) * HEAD_PAD)
            qh_ref[:, sl] = (_rope_apply(q_ref[:, sl], tc, s1, s2) * scale).astype(bf16)
            kh_ref[:, sl] = (kk_ref[:, sl] + krr).astype(bf16)
        vh_ref[...] = kvv_ref[...].astype(bf16)

    wide = lambda c: pl.BlockSpec((ts, MLA_PAD), lambda i: (i, c))
    tab = pl.BlockSpec((ts, LANES), lambda i: (i, 0))
    return _pc(body, grid=(S // ts,),
               in_specs=[wide(0), wide(0), wide(1), pl.BlockSpec((ts, LANES), lambda i: (i, 3)), tab, tab, tab],
               out_specs=[wide(0)] * 3, out_shape=[jax.ShapeDtypeStruct((S, MLA_PAD), bf16)] * 3,
               compiler_params=_cp(("parallel",)), name=name)(q, kv, kv, proj, *tabs)


def _rope_bwd(dqh, dkh, dvh, tabs, name="rope_bwd"):
    S = dqh.shape[0]
    ts = _pick(S, (512, 256, 128))
    scale = (QK_NOPE + QK_ROPE) ** -0.5

    def body(dq_ref, dk_ref, dv_ref, tc_ref, s1_ref, s2_ref, oq_ref, okv_ref, okr_ref):
        tc, s1, s2 = tc_ref[...], s1_ref[...], s2_ref[...]
        ksum = None
        for h in range(MLA_HEADS):
            sl = slice(h * HEAD_PAD, (h + 1) * HEAD_PAD)
            oq_ref[:, sl] = (_rope_apply_t(dq_ref[:, sl], tc, s1, s2) * scale).astype(bf16)
            dk = dk_ref[:, sl]
            okv_ref[:, sl] = dk.astype(bf16)
            ksum = dk if ksum is None else ksum + dk
        okv_ref[:, MLA_PAD:] = dv_ref[...].astype(bf16)
        dkr = pltpu.roll(_rope_apply_t(ksum, tc, s1, s2), LANES - QK_NOPE, 1)
        lane = lax.broadcasted_iota(jnp.int32, (ts, LANES), 1)
        okr_ref[...] = jnp.where(lane < QK_ROPE, dkr, 0.0).astype(bf16)

    wide = pl.BlockSpec((ts, MLA_PAD), lambda i: (i, 0))
    tab = pl.BlockSpec((ts, LANES), lambda i: (i, 0))
    return _pc(body, grid=(S // ts,), in_specs=[wide, wide, wide, tab, tab, tab],
               out_specs=[wide, pl.BlockSpec((ts, 2 * MLA_PAD), lambda i: (i, 0)), tab],
               out_shape=[jax.ShapeDtypeStruct((S, MLA_PAD), bf16), jax.ShapeDtypeStruct((S, 2 * MLA_PAD), bf16),
                          jax.ShapeDtypeStruct((S, LANES), bf16)],
               compiler_params=_cp(("parallel",)), name=name)(dqh, dkh, dvh, *tabs)


ATT_BLK = 1024


def _attn_fwd(qh, kh, vh, plan=None, name="attn_fwd"):
    S = qh.shape[0]
    tq = tk = min(S, ATT_BLK)
    nq, nk = S // tq, S // tk
    npl = plan['n'] if plan else 0

    def body(*refs):
        q_ref, k_ref, v_ref = refs[:3]
        o_ref, lse_ref = refs[3 + npl:5 + npl]
        m_sc, l_sc, acc_sc = refs[5 + 2 * npl:8 + 2 * npl]
        h, i, j = pl.program_id(0), pl.program_id(1), pl.program_id(2)
        if plan:
            pargs = (refs[3:3 + npl], refs[5 + npl:5 + 2 * npl], refs[8 + 2 * npl], refs[9 + 2 * npl])
            first = (i == 0) & (j == 0)
            pl.when((h == 0) & first)(functools.partial(plan['start'], *pargs))
            pl.when((h == (3 * MLA_HEADS) // 4) & first)(functools.partial(plan['forward'], *pargs))
            pl.when((h == MLA_HEADS - 1) & (i == nq - 1) & (j == nk - 1))(functools.partial(plan['finish'], *pargs))

        @pl.when(j == 0)
        def _():
            m_sc[...] = jnp.full_like(m_sc, -1e30)
            l_sc[...] = jnp.zeros_like(l_sc)
            acc_sc[...] = jnp.zeros_like(acc_sc)

        def step(masked):
            s = lax.dot_general(q_ref[...], k_ref[...], (((1,), (1,)), ((), ())), preferred_element_type=f32)
            if masked:
                row = lax.broadcasted_iota(jnp.int32, (tq, tk), 0)
                col = lax.broadcasted_iota(jnp.int32, (tq, tk), 1)
                s = jnp.where(col <= row, s, -1e30)
            m_prev = m_sc[...]
            m_new = jnp.maximum(m_prev, jnp.max(s, axis=-1, keepdims=True))
            alpha = jnp.exp(m_prev - m_new)
            p = jnp.exp(s - m_new)
            l_sc[...] = alpha * l_sc[...] + jnp.sum(p, axis=-1, keepdims=True)
            acc_sc[...] = alpha * acc_sc[...] + jnp.dot(p.astype(bf16), v_ref[...], preferred_element_type=f32)
            m_sc[...] = m_new

        pl.when(j < i)(functools.partial(step, False))
        pl.when(j == i)(functools.partial(step, True))

        @pl.when(j == nk - 1)
        def _():
            l = l_sc[...]
            o_ref[...] = acc_sc[...] / l
            lse_ref[...] = jnp.broadcast_to(m_sc[...] + jnp.log(l), (tq, LANES))

    qspec = pl.BlockSpec((tq, HEAD_PAD), lambda h, i, j: (i, h))
    kspec = pl.BlockSpec((tk, HEAD_PAD), lambda h, i, j: (jnp.minimum(j, i), h))
    anyspec = pl.BlockSpec(memory_space=pl.ANY)
    scratch = [pltpu.VMEM((tq, 1), f32), pltpu.VMEM((tq, 1), f32), pltpu.VMEM((tq, HEAD_PAD), f32)]
    if plan:
        scratch += [pltpu.SemaphoreType.DMA((plan['nsem'],)), pltpu.SemaphoreType.DMA((plan['nsem'],))]
    outs = _pc(body, grid=(MLA_HEADS, nq, nk), in_specs=[qspec, kspec, kspec] + [anyspec] * npl,
               out_specs=[qspec, qspec] + [anyspec] * npl,
               out_shape=[jax.ShapeDtypeStruct((S, MLA_PAD), f32)] * 2 + (plan['outs'] if plan else []),
               scratch_shapes=scratch,
               compiler_params=_cp(("arbitrary", "arbitrary", "arbitrary") if plan else ("parallel", "parallel", "arbitrary")),
               name=name)(qh, kh, vh, *(plan['ins'] if plan else []))
    return outs[0], outs[1], outs[2:]


def _attn_bwd(qh, kh, vh, do, lse, delta, plan=None, name="attn_bwd"):
    S = qh.shape[0]
    tq = tk = min(S, ATT_BLK)
    nq, nk = S // tq, S // tk
    npl = plan['n'] if plan else 0

    def body(*refs):
        q_ref, k_ref, v_ref, do_ref, lse_ref, dl_ref = refs[:6]
        dq_ref, dk_ref, dv_ref = refs[6 + npl:9 + npl]
        h, j, i = pl.program_id(0), pl.program_id(1), pl.program_id(2)
        if plan:
            pargs = (refs[6:6 + npl], refs[9 + npl:9 + 2 * npl], refs[9 + 2 * npl], refs[10 + 2 * npl])
            pl.when((h == 0) & (j == 0) & (i == 0))(functools.partial(plan['start'], *pargs))
            pl.when((h == MLA_HEADS - 1) & (j == nk - 1) & (i == nq - 1))(functools.partial(plan['finish'], *pargs))

        @pl.when((j == 0) & (i == 0))
        def _():
            dq_ref[...] = jnp.zeros_like(dq_ref)

        @pl.when(i == 0)
        def _():
            dk_ref[...] = jnp.zeros_like(dk_ref)
            dv_ref[...] = jnp.zeros_like(dv_ref)

        def step(masked):
            nt = (((1,), (1,)), ((), ()))
            tn = (((0,), (0,)), ((), ()))
            qv, kv_, dov = q_ref[...], k_ref[...], do_ref[...]
            s = lax.dot_general(qv, kv_, nt, preferred_element_type=f32)
            p = jnp.exp(s - lse_ref[:, :1])
            if masked:
                row = lax.broadcasted_iota(jnp.int32, (tq, tk), 0)
                col = lax.broadcasted_iota(jnp.int32, (tq, tk), 1)
                p = jnp.where(col <= row, p, 0.0)
            dp = lax.dot_general(dov, v_ref[...], nt, preferred_element_type=f32)
            ds = (p * (dp - dl_ref[:, :1])).astype(bf16)
            dv_ref[...] += lax.dot_general(p.astype(bf16), dov, tn, preferred_element_type=f32)
            dk_ref[...] += lax.dot_general(ds, qv, tn, preferred_element_type=f32)
            rows = pl.ds(pl.multiple_of(i * tq, tq), tq)
            dq_ref[rows, :] += jnp.dot(ds, kv_, preferred_element_type=f32)

        pl.when(i > j)(functools.partial(step, False))
        pl.when(i == j)(functools.partial(step, True))

    qspec = pl.BlockSpec((tq, HEAD_PAD), lambda h, j, i: (jnp.maximum(i, j), h))
    kspec = pl.BlockSpec((tk, HEAD_PAD), lambda h, j, i: (j, h))
    colspec = pl.BlockSpec((S, HEAD_PAD), lambda h, j, i: (0, h))
    anyspec = pl.BlockSpec(memory_space=pl.ANY)
    scratch = [pltpu.SemaphoreType.DMA((plan['nsem'],)), pltpu.SemaphoreType.DMA((plan['nsem'],))] if plan else []
    outs = _pc(body, grid=(MLA_HEADS, nk, nq), in_specs=[qspec, kspec, kspec, qspec, qspec, qspec] + [anyspec] * npl,
               out_specs=[colspec, kspec, kspec] + [anyspec] * npl,
               out_shape=[jax.ShapeDtypeStruct((S, MLA_PAD), f32)] * 3 + (plan['outs'] if plan else []),
               scratch_shapes=scratch,
               compiler_params=_cp(("arbitrary" if plan else "parallel", "arbitrary", "arbitrary")),
               name=name)(qh, kh, vh, do, lse, delta, *(plan['ins'] if plan else []))
    return outs[0], outs[1], outs[2], outs[3:]


def _xattn_fwd(q, kv, name="xattn_fwd"):
    S = q.shape[0]
    M = kv.shape[0]
    tq = _pick(S, (256, 128))
    scale = X_HEAD_DIM ** -0.5

    def body(q_ref, kv_ref, o_ref):
        for h in range(X_HEADS):
            sl = slice(h * X_HEAD_DIM, (h + 1) * X_HEAD_DIM)
            k = kv_ref[:, sl]
            v = kv_ref[:, D_MODEL + h * X_HEAD_DIM:D_MODEL + (h + 1) * X_HEAD_DIM]
            s = lax.dot_general(q_ref[:, sl], k, (((1,), (1,)), ((), ())), preferred_element_type=f32) * scale
            e = jnp.exp(s - jnp.max(s, axis=-1, keepdims=True))
            p = e / jnp.sum(e, axis=-1, keepdims=True)
            o_ref[:, sl] = jnp.dot(p.astype(bf16), v, preferred_element_type=f32).astype(bf16)

    blk = pl.BlockSpec((tq, D_MODEL), lambda i: (i, 0))
    return _pc(body, grid=(S // tq,), in_specs=[blk, pl.BlockSpec((M, 2 * D_MODEL), lambda i: (0, 0))],
               out_specs=blk, out_shape=jax.ShapeDtypeStruct((S, D_MODEL), bf16),
               compiler_params=_cp(("parallel",)), name=name)(q, kv)


def _xattn_bwd(q, kv, do, name="xattn_bwd"):
    S = q.shape[0]
    M = kv.shape[0]
    tq = _pick(S, (256, 128))
    scale = X_HEAD_DIM ** -0.5

    def body(q_ref, kv_ref, do_ref, dq_ref, dkv_ref):
        i = pl.program_id(0)

        @pl.when(i == 0)
        def _():
            dkv_ref[...] = jnp.zeros_like(dkv_ref)

        nt = (((1,), (1,)), ((), ()))
        tn = (((0,), (0,)), ((), ()))
        for h in range(X_HEADS):
            sl = slice(h * X_HEAD_DIM, (h + 1) * X_HEAD_DIM)
            vsl = slice(D_MODEL + h * X_HEAD_DIM, D_MODEL + (h + 1) * X_HEAD_DIM)
            k, v, qv, dov = kv_ref[:, sl], kv_ref[:, vsl], q_ref[:, sl], do_ref[:, sl]
            s = lax.dot_general(qv, k, nt, preferred_element_type=f32) * scale
            e = jnp.exp(s - jnp.max(s, axis=-1, keepdims=True))
            p = e / jnp.sum(e, axis=-1, keepdims=True)
            dp = lax.dot_general(dov, v, nt, preferred_element_type=f32)
            ds = (p * (dp - jnp.sum(dp * p, axis=-1, keepdims=True)) * scale).astype(bf16)
            dq_ref[:, sl] = jnp.dot(ds, k, preferred_element_type=f32).astype(bf16)
            dkv_ref[:, sl] += lax.dot_general(ds, qv, tn, preferred_element_type=f32)
            dkv_ref[:, vsl] += lax.dot_general(p.astype(bf16), dov, tn, preferred_element_type=f32)

    blk = pl.BlockSpec((tq, D_MODEL), lambda i: (i, 0))
    full = pl.BlockSpec((M, 2 * D_MODEL), lambda i: (0, 0))
    return _pc(body, grid=(S // tq,), in_specs=[blk, full, blk], out_specs=[blk, full],
               out_shape=[jax.ShapeDtypeStruct((S, D_MODEL), bf16), jax.ShapeDtypeStruct((M, 2 * D_MODEL), f32)],
               compiler_params=_cp(("arbitrary",)), name=name)(q, kv, do)


ST_CHUNKS = 1


def _apow_init(a_ref, ap_ref, bp_ref, seg):
    P = MACRO_ST
    ar, ai = a_ref[:, :P], a_ref[:, P:]
    pr, pi = ar, ai
    for r in range(seg):
        ap_ref[r:r + 1, :P] = pr
        ap_ref[r:r + 1, P:] = pi
        if r < seg - 1:
            pr, pi = pr * ar - pi * ai, pr * ai + pi * ar
    br, bi = pr, pi
    for k in range(SUBLANES):
        bp_ref[k:k + 1, :P] = pr
        bp_ref[k:k + 1, P:] = pi
        pr, pi = pr * br - pi * bi, pr * bi + pi * br


def _segment_perm(tS):
    seg = tS // SUBLANES
    rows = jnp.arange(tS)
    src = (rows % SUBLANES) * seg + rows // SUBLANES
    return (src[:, None] == jnp.arange(tS)[None, :]).astype(f32)


def _unpermute_rows(pt, v):
    hi = v.astype(bf16)
    r1 = v - hi.astype(f32)
    mid = r1.astype(bf16)
    lo = (r1 - mid.astype(f32)).astype(bf16)
    out = jnp.dot(pt, jnp.concatenate([hi, mid, lo], axis=1), preferred_element_type=f32)
    w = v.shape[1]
    return (out[:, :w] + out[:, w:2 * w]) + out[:, 2 * w:]


def _scan_block(sc_ref, ap_ref, bp_ref, carry_ref, e_ref, seg, reverse):
    P = MACRO_ST
    sgn = -1.0 if reverse else 1.0
    CH = P // ST_CHUNKS
    rid = lax.broadcasted_iota(jnp.int32, (SUBLANES, CH), 0)
    for c in range(ST_CHUNKS):
        lr, li = slice(c * CH, (c + 1) * CH), slice(P + c * CH, P + (c + 1) * CH)
        ar, ai = ap_ref[0:1, lr], sgn * ap_ref[0:1, li]
        xr = xi = None
        for i in range(seg):
            r = seg - 1 - i if reverse else i
            rows = slice(SUBLANES * r, SUBLANES * (r + 1))
            sr, si = sc_ref[rows, lr], sc_ref[rows, li]
            if i == 0:
                xr, xi = sr, si
            else:
                xr, xi = ar * xr - ai * xi + sr, ar * xi + ai * xr + si
                sc_ref[rows, lr] = xr
                sc_ref[rows, li] = xi
        for sh in (1, 2, 4):
            pr, pi = bp_ref[sh - 1:sh, lr], sgn * bp_ref[sh - 1:sh, li]
            if reverse:
                tr = jnp.where(rid < SUBLANES - sh, pltpu.roll(xr, SUBLANES - sh, 0), 0.0)
                ti = jnp.where(rid < SUBLANES - sh, pltpu.roll(xi, SUBLANES - sh, 0), 0.0)
            else:
                tr = jnp.where(rid >= sh, pltpu.roll(xr, sh, 0), 0.0)
                ti = jnp.where(rid >= sh, pltpu.roll(xi, sh, 0), 0.0)
            xr, xi = xr + pr * tr - pi * ti, xi + pr * ti + pi * tr
        if reverse:
            bpr = jnp.zeros((SUBLANES, CH), f32)
            bpi = jnp.zeros((SUBLANES, CH), f32)
            for r in range(SUBLANES):
                bpr = jnp.where(rid == r, bp_ref[SUBLANES - 1 - r:SUBLANES - r, lr], bpr)
                bpi = jnp.where(rid == r, -bp_ref[SUBLANES - 1 - r:SUBLANES - r, li], bpi)
        else:
            bpr, bpi = bp_ref[:, lr], bp_ref[:, li]
        cr, cim = carry_ref[:, lr], carry_ref[:, li]
        xr, xi = xr + bpr * cr - bpi * cim, xi + bpr * cim + bpi * cr
        edge = 0 if reverse else SUBLANES - 1
        carry_ref[:, lr] = jnp.sum(jnp.where(rid == edge, xr, 0.0), axis=0, keepdims=True)
        carry_ref[:, li] = jnp.sum(jnp.where(rid == edge, xi, 0.0), axis=0, keepdims=True)
        if reverse:
            er = jnp.where(rid == SUBLANES - 1, cr, pltpu.roll(xr, SUBLANES - 1, 0))
            ei = jnp.where(rid == SUBLANES - 1, cim, pltpu.roll(xi, SUBLANES - 1, 0))
        else:
            er = jnp.where(rid == 0, cr, pltpu.roll(xr, 1, 0))
            ei = jnp.where(rid == 0, cim, pltpu.roll(xi, 1, 0))
        if e_ref is not None:
            e_ref[:, lr] = er
            e_ref[:, li] = ei
        for i in range(seg):
            r = seg - 1 - i if reverse else i
            rows = slice(SUBLANES * r, SUBLANES * (r + 1))
            pr, pi = ap_ref[i:i + 1, lr], sgn * ap_ref[i:i + 1, li]
            sc_ref[rows, lr] += pr * er - pi * ei
            sc_ref[rows, li] += pr * ei + pi * er


def _ssm_fwd(proj, bm, cm, a, d, name="ssm_fwd"):
    S = proj.shape[0]
    tS = _pick(S, (256, 128))
    nb = S // tS
    P2 = 2 * MACRO_ST
    seg = tS // SUBLANES
    ucol0 = (D_MODEL - SSM_WIDTH) // MACRO_CH

    perm = _segment_perm(tS)

    def body(u_ref, b_ref, c_ref, a_ref, d_ref, pm_ref, pt_ref, y_ref, xc_ref, bu_sc, ap_sc, bp_sc, car_sc):
        t = pl.program_id(1)

        @pl.when(t == 0)
        def _():
            _apow_init(a_ref, ap_sc, bp_sc, seg)
            car_sc[...] = jnp.zeros_like(car_sc)

        uv = u_ref[...]
        up = jnp.dot(pm_ref[...], uv.astype(bf16), preferred_element_type=f32).astype(bf16)
        bu_sc[...] = jnp.dot(up, b_ref[...], preferred_element_type=f32)
        xc_ref[...] = car_sc[...]
        _scan_block(bu_sc, ap_sc, bp_sc, car_sc, None, seg, False)
        yp = jnp.dot(bu_sc[...].astype(bf16), c_ref[...], preferred_element_type=f32)
        y_ref[...] = _unpermute_rows(pt_ref[...], yp) + d_ref[...] * uv

    sq = pl.BlockSpec((tS, tS), lambda m, t: (0, 0))
    return _pc(body, grid=(SSM_MACRO, nb),
               in_specs=[pl.BlockSpec((tS, MACRO_CH), lambda m, t: (t, ucol0 + m)),
                         pl.BlockSpec((None, MACRO_CH, P2), lambda m, t: (m, 0, 0)),
                         pl.BlockSpec((None, P2, MACRO_CH), lambda m, t: (m, 0, 0)),
                         pl.BlockSpec((None, 1, P2), lambda m, t: (m, 0, 0)),
                         pl.BlockSpec((1, MACRO_CH), lambda m, t: (0, m)), sq, sq],
               out_specs=[pl.BlockSpec((tS, MACRO_CH), lambda m, t: (t, m)),
                          pl.BlockSpec((None, None, 1, P2), lambda m, t: (m, t, 0, 0))],
               out_shape=[jax.ShapeDtypeStruct((S, SSM_WIDTH), f32), jax.ShapeDtypeStruct((SSM_MACRO, nb, 1, P2), f32)],
               scratch_shapes=[pltpu.VMEM((tS, P2), f32), pltpu.VMEM((seg, P2), f32),
                               pltpu.VMEM((SUBLANES, P2), f32), pltpu.VMEM((1, P2), f32)],
               compiler_params=_cp(("arbitrary", "arbitrary")), name=name)(
        proj, bm, cm, a, d.reshape(1, SSM_WIDTH), perm.astype(bf16), perm.T.astype(bf16))


def _ssm_bwd(proj, dy, xc, bm, cm, a, d, name="ssm_bwd"):
    S = proj.shape[0]
    tS = _pick(S, (256, 128))
    nb = S // tS
    P = MACRO_ST
    P2 = 2 * P
    seg = tS // SUBLANES
    ucol0 = (D_MODEL - SSM_WIDTH) // MACRO_CH

    perm = _segment_perm(tS)

    def body(u_ref, dy_ref, xc_ref, b_ref, c_ref, a_ref, d_ref, pm_ref, pt_ref, du_ref, db_ref, dc_ref, da_ref, dd_ref,
             x_sc, g_sc, ap_sc, bp_sc, e_sc, xcar_sc, gcar_sc):
        t = pl.program_id(1)

        @pl.when(t == 0)
        def _():
            _apow_init(a_ref, ap_sc, bp_sc, seg)
            gcar_sc[...] = jnp.zeros_like(gcar_sc)
            db_ref[...] = jnp.zeros_like(db_ref)
            dc_ref[...] = jnp.zeros_like(dc_ref)
            da_ref[...] = jnp.zeros_like(da_ref)
            dd_ref[...] = jnp.zeros_like(dd_ref)

        nt = (((1,), (1,)), ((), ()))
        tn = (((0,), (0,)), ((), ()))
        uv = u_ref[...]
        dyv = dy_ref[...]
        pm = pm_ref[...]
        ub = jnp.dot(pm, uv.astype(bf16), preferred_element_type=f32).astype(bf16)
        dyb = jnp.dot(pm, dyv.astype(bf16), preferred_element_type=f32).astype(bf16)
        x_sc[...] = jnp.dot(ub, b_ref[...], preferred_element_type=f32)
        xcar_sc[...] = xc_ref[...]
        _scan_block(x_sc, ap_sc, bp_sc, xcar_sc, e_sc, seg, False)
        g_sc[...] = lax.dot_general(dyb, c_ref[...], nt, preferred_element_type=f32)
        _scan_block(g_sc, ap_sc, bp_sc, gcar_sc, None, seg, True)
        xv = x_sc[...]
        gv = g_sc[...]
        gb = gv.astype(bf16)
        dc_ref[...] += lax.dot_general(xv.astype(bf16), dyb, tn, preferred_element_type=f32)
        db_ref[...] += lax.dot_general(ub, gb, tn, preferred_element_type=f32)
        dup = lax.dot_general(gb, b_ref[...], nt, preferred_element_type=f32)
        du_ref[...] = _unpermute_rows(pt_ref[...], dup) + d_ref[...] * dyv
        dd_ref[...] += jnp.sum(dyv * uv, axis=0, keepdims=True)
        xp = jnp.concatenate([e_sc[...], xv[:tS - SUBLANES]], axis=0)
        xpr, xpi, ggr, ggi = xp[:, :P], xp[:, P:], gv[:, :P], gv[:, P:]
        da_ref[:, :P] += jnp.sum(ggr * xpr + ggi * xpi, axis=0, keepdims=True)
        da_ref[:, P:] += jnp.sum(ggi * xpr - ggr * xpi, axis=0, keepdims=True)

    rev = lambda t: nb - 1 - t
    return _pc(body, grid=(SSM_MACRO, nb),
               in_specs=[pl.BlockSpec((tS, MACRO_CH), lambda m, t: (rev(t), ucol0 + m)),
                         pl.BlockSpec((tS, MACRO_CH), lambda m, t: (rev(t), m)),
                         pl.BlockSpec((None, None, 1, P2), lambda m, t: (m, rev(t), 0, 0)),
                         pl.BlockSpec((None, MACRO_CH, P2), lambda m, t: (m, 0, 0)),
                         pl.BlockSpec((None, P2, MACRO_CH), lambda m, t: (m, 0, 0)),
                         pl.BlockSpec((None, 1, P2), lambda m, t: (m, 0, 0)),
                         pl.BlockSpec((1, MACRO_CH), lambda m, t: (0, m)),
                         pl.BlockSpec((tS, tS), lambda m, t: (0, 0)), pl.BlockSpec((tS, tS), lambda m, t: (0, 0))],
               out_specs=[pl.BlockSpec((tS, MACRO_CH), lambda m, t: (rev(t), m)),
                          pl.BlockSpec((None, MACRO_CH, P2), lambda m, t: (m, 0, 0)),
                          pl.BlockSpec((None, P2, MACRO_CH), lambda m, t: (m, 0, 0)),
                          pl.BlockSpec((None, 1, P2), lambda m, t: (m, 0, 0)),
                          pl.BlockSpec((1, MACRO_CH), lambda m, t: (0, m))],
               out_shape=[jax.ShapeDtypeStruct((S, SSM_WIDTH), f32),
                          jax.ShapeDtypeStruct((SSM_MACRO, MACRO_CH, P2), f32),
                          jax.ShapeDtypeStruct((SSM_MACRO, P2, MACRO_CH), f32),
                          jax.ShapeDtypeStruct((SSM_MACRO, 1, P2), f32),
                          jax.ShapeDtypeStruct((1, SSM_WIDTH), f32)],
               scratch_shapes=[pltpu.VMEM((tS, P2), f32), pltpu.VMEM((tS, P2), f32),
                               pltpu.VMEM((seg, P2), f32), pltpu.VMEM((SUBLANES, P2), f32), pltpu.VMEM((SUBLANES, P2), f32),
                               pltpu.VMEM((1, P2), f32), pltpu.VMEM((1, P2), f32)],
               compiler_params=_cp(("arbitrary", "arbitrary")), name=name)(
        proj, dy, xc, bm, cm, a, d.reshape(1, SSM_WIDTH), perm.astype(bf16), perm.T.astype(bf16))


_GELU_K = math.sqrt(2.0 / math.pi)
_GELU_C = 0.044715


def _glu_fwd(y, w, b, g, name="glu_fwd"):
    S, W = y.shape
    ts = _pick(S, (512, 256, 128))

    def body(y_ref, w_ref, b_ref, g_ref, z_ref, sn_ref, ge_ref):
        yv = y_ref[...]
        cdf = 0.5 * (1.0 + jnp.tanh(_GELU_K * (yv + _GELU_C * (yv * yv * yv))))
        ge = (yv * cdf).astype(bf16)
        z = jnp.dot(ge, w_ref[...], preferred_element_type=f32) + b_ref[...]
        s = yv * jax.nn.sigmoid(z)
        rstd = lax.rsqrt(jnp.sum(s * s, axis=-1, keepdims=True) * (1.0 / W) + EPS)
        z_ref[...] = z
        sn_ref[...] = (s * rstd * g_ref[...]).astype(bf16)
        ge_ref[...] = ge

    blk = pl.BlockSpec((ts, W), lambda i: (i, 0))
    row = pl.BlockSpec((1, W), lambda i: (0, 0))
    return _pc(body, grid=(S // ts,), in_specs=[blk, pl.BlockSpec((W, W), lambda i: (0, 0)), row, row],
               out_specs=[blk, blk, blk],
               out_shape=[jax.ShapeDtypeStruct((S, W), f32), jax.ShapeDtypeStruct((S, W), bf16),
                          jax.ShapeDtypeStruct((S, W), bf16)],
               compiler_params=_cp(("parallel",)), name=name)(y, w, b.reshape(1, W), g.reshape(1, W))


def _glu_bwd(y, z, dmixed, w, g, name="glu_bwd"):
    S, W = y.shape
    ts = _pick(S, (512, 256, 128))
    dcb = MLA_PAD // W

    def body(y_ref, z_ref, dsn_ref, w_ref, g_ref, dy_ref, dz_ref, dg_ref, db_ref):
        i = pl.program_id(0)
        yv, zv, gv = y_ref[...], z_ref[...], g_ref[...]
        sig = jax.nn.sigmoid(zv)
        s = yv * sig
        rstd = lax.rsqrt(jnp.sum(s * s, axis=-1, keepdims=True) * (1.0 / W) + EPS)
        sh = s * rstd
        dsn = dsn_ref[...]
        dsh = dsn * gv
        ds = rstd * (dsh - sh * (jnp.sum(dsh * sh, axis=-1, keepdims=True) * (1.0 / W)))
        dz = ds * s * (1.0 - sig)
        dzb = dz.astype(bf16)
        dge = lax.dot_general(dzb, w_ref[...], (((1,), (1,)), ((), ())), preferred_element_type=f32)
        t = jnp.tanh(_GELU_K * (yv + _GELU_C * (yv * yv * yv)))
        dgelu = 0.5 * (1.0 + t) + 0.5 * yv * (1.0 - t * t) * _GELU_K * (1.0 + 3.0 * _GELU_C * yv * yv)
        dy_ref[...] = ds * sig + dge * dgelu
        dz_ref[...] = dzb

        @pl.when(i == 0)
        def _():
            dg_ref[...] = jnp.zeros_like(dg_ref)
            db_ref[...] = jnp.zeros_like(db_ref)

        dg_ref[...] += jnp.sum(dsn * sh, axis=0, keepdims=True)
        db_ref[...] += jnp.sum(dz, axis=0, keepdims=True)

    blk = pl.BlockSpec((ts, W), lambda i: (i, 0))
    row = pl.BlockSpec((1, W), lambda i: (0, 0))
    return _pc(body, grid=(S // ts,),
               in_specs=[blk, blk, pl.BlockSpec((ts, W), lambda i: (i, dcb)), pl.BlockSpec((W, W), lambda i: (0, 0)), row],
               out_specs=[blk, blk, row, row],
               out_shape=[jax.ShapeDtypeStruct((S, W), f32), jax.ShapeDtypeStruct((S, W), bf16),
                          jax.ShapeDtypeStruct((1, W), f32), jax.ShapeDtypeStruct((1, W), f32)],
               compiler_params=_cp(("arbitrary",)), name=name)(y, z, dmixed, w, g.reshape(1, W))


def _ffn_up(hn, wg, wu, name="ffn_up"):
    S, K = hn.shape
    F = wg.shape[0]
    tm, tn = _pick(S, (512, 256, 128)), _pick(F, (1408, 256, 128))

    def body(h_ref, wg_ref, wu_ref, g_ref, u_ref, a_ref):
        hv = h_ref[...]
        nt = (((1,), (1,)), ((), ()))
        gv = lax.dot_general(hv, wg_ref[...], nt, preferred_element_type=f32)
        uv = lax.dot_general(hv, wu_ref[...], nt, preferred_element_type=f32)
        g_ref[...] = gv
        u_ref[...] = uv
        a_ref[...] = (gv * jax.nn.sigmoid(gv) * uv).astype(bf16)

    wspec = pl.BlockSpec((tn, K), lambda i, j: (j, 0))
    ospec = pl.BlockSpec((tm, tn), lambda i, j: (i, j))
    return _pc(body, grid=(S // tm, F // tn), in_specs=[pl.BlockSpec((tm, K), lambda i, j: (i, 0)), wspec, wspec],
               out_specs=[ospec] * 3,
               out_shape=[jax.ShapeDtypeStruct((S, F), f32), jax.ShapeDtypeStruct((S, F), f32),
                          jax.ShapeDtypeStruct((S, F), bf16)],
               compiler_params=_cp(("parallel", "parallel")), name=name)(hn, wg, wu)


def _ffn_bwd_act(dh, wd, gate, up, name="ffn_bwd_act"):
    S, K = dh.shape
    F = wd.shape[0]
    tm, tn = _pick(S, (512, 256, 128)), _pick(F, (1408, 256, 128))

    def body(dh_ref, wd_ref, g_ref, u_ref, dg_ref, du_ref):
        dact = lax.dot_general(dh_ref[...].astype(bf16), wd_ref[...], (((1,), (1,)), ((), ())),
                               preferred_element_type=f32)
        gv, uv = g_ref[...], u_ref[...]
        sig = jax.nn.sigmoid(gv)
        dg_ref[...] = (dact * uv * (sig * (1.0 + gv * (1.0 - sig)))).astype(bf16)
        du_ref[...] = (dact * (gv * sig)).astype(bf16)

    ospec = pl.BlockSpec((tm, tn), lambda i, j: (i, j))
    return _pc(body, grid=(S // tm, F // tn),
               in_specs=[pl.BlockSpec((tm, K), lambda i, j: (i, 0)), pl.BlockSpec((tn, K), lambda i, j: (j, 0)),
                         ospec, ospec],
               out_specs=[ospec] * 2, out_shape=[jax.ShapeDtypeStruct((S, F), bf16)] * 2,
               compiler_params=_cp(("parallel", "parallel")), name=name)(dh, wd, gate, up)


def _pad_heads(w, per_head, pieces):
    K = w.shape[0]
    w3 = w.reshape(K, MLA_HEADS, per_head)
    out = jnp.zeros((K, MLA_HEADS, HEAD_PAD), w.dtype)
    for s0, s1, d0 in pieces:
        out = out.at[:, :, d0:d0 + (s1 - s0)].set(w3[:, :, s0:s1])
    return out.reshape(K, MLA_PAD)


def _unpad_heads(wp, per_head, pieces):
    K = wp.shape[0]
    w3 = wp.reshape(K, MLA_HEADS, HEAD_PAD)
    out = jnp.zeros((K, MLA_HEADS, per_head), wp.dtype)
    for s0, s1, d0 in pieces:
        out = out.at[:, :, s0:s1].set(w3[:, :, d0:d0 + (s1 - s0)])
    return out.reshape(K, MLA_HEADS * per_head)


_Q_PIECES = [(0, QK_NOPE + QK_ROPE, 0)]
_K_PIECES = [(0, QK_NOPE, 0)]
_V_PIECES = [(QK_NOPE, QK_NOPE + V_HEAD, 0)]
_KR0 = Q_LORA + KV_LORA


def _pack_win(w):
    z = jnp.zeros((w.shape[0], LANES - QK_ROPE), w.dtype)
    return jnp.concatenate([w[:, :_KR0 + QK_ROPE], z, w[:, _KR0 + QK_ROPE:]], axis=1)


def _unpack_win(wp):
    return jnp.concatenate([wp[:, :_KR0 + QK_ROPE], wp[:, _KR0 + LANES:]], axis=1)


def _pack_wout(w):
    wa = w[:MLA_WIDTH].reshape(MLA_HEADS, V_HEAD, D_MODEL)
    wa = jnp.concatenate([wa, jnp.zeros_like(wa)], axis=1).reshape(MLA_PAD, D_MODEL)
    return jnp.concatenate([wa, w[MLA_WIDTH:]], axis=0)


def _unpack_wout(wp):
    wa = wp[:MLA_PAD].reshape(MLA_HEADS, HEAD_PAD, D_MODEL)[:, :V_HEAD].reshape(MLA_WIDTH, D_MODEL)
    return jnp.concatenate([wa, wp[MLA_PAD:]], axis=0)


def _pad_gain(g):
    g2 = g.reshape(MLA_HEADS, V_HEAD)
    return jnp.concatenate([g2, jnp.zeros_like(g2)], axis=1).reshape(MLA_PAD)


def _unpad_gain(gp):
    return gp.reshape(MLA_HEADS, HEAD_PAD)[:, :V_HEAD].reshape(MLA_WIDTH)


def _ssm_prep(lam_re, lam_im, log_dt, b_re, b_im, c_re, c_im):
    lam = lax.complex(lam_re, lam_im)
    dt = jnp.exp(log_dt)[:, None]
    a_bar = jnp.exp(lam * dt)
    b_bar = ((a_bar - 1.0) / lam)[:, None, :] * lax.complex(b_re, b_im)
    G8 = SSM_GROUPS // SSM_MACRO
    eye = jnp.eye(G8, dtype=f32)

    def bmat(part):
        p4 = part.reshape(SSM_MACRO, G8, SSM_GROUP, SSM_STATE)
        return jnp.einsum('mgcp,gh->mgchp', p4, eye).reshape(SSM_MACRO, MACRO_CH, MACRO_ST)

    def cmat(part):
        p4 = part.reshape(SSM_MACRO, G8, SSM_GROUP, SSM_STATE)
        return jnp.einsum('mgcp,gh->mgphc', p4, eye).reshape(SSM_MACRO, MACRO_ST, MACRO_CH)

    bm = jnp.concatenate([bmat(b_bar.real), bmat(b_bar.imag)], axis=2)
    cm = jnp.concatenate([cmat(c_re), -cmat(c_im)], axis=1)
    a4 = a_bar.reshape(SSM_MACRO, 1, MACRO_ST)
    a = jnp.concatenate([a4.real, a4.imag], axis=2)
    return bm, cm, a


def _rope_tables(positions):
    freqs = ROPE_THETA ** (-jnp.arange(0, QK_ROPE, 2, dtype=f32) / QK_ROPE)
    ang = positions.astype(f32)[:, None] * freqs
    cos, sin = jnp.cos(ang), jnp.sin(ang)
    S = positions.shape[0]
    half = QK_ROPE // 2
    one, zero = jnp.ones((S, QK_NOPE), f32), jnp.zeros((S, half), f32)
    z64, z32 = jnp.zeros((S, QK_NOPE), f32), jnp.zeros((S, LANES - QK_NOPE - QK_ROPE), f32)
    tc = jnp.concatenate([one, cos, cos, z32], axis=1)
    s1 = jnp.concatenate([z64, -sin, zero, z32], axis=1)
    s2 = jnp.concatenate([z64, zero, sin, z32], axis=1)
    return tc, s1, s2


def _layer_params(W, l):
    p = {}
    p['win'] = _pack_win(W['w_in'][l])
    p['wuq'] = _pad_heads(W['w_uq'][l], QK_NOPE + QK_ROPE, _Q_PIECES)
    wukv = W['w_ukv'][l]
    p['wukv'] = jnp.concatenate([_pad_heads(wukv, QK_NOPE + V_HEAD, _K_PIECES),
                                 _pad_heads(wukv, QK_NOPE + V_HEAD, _V_PIECES)], axis=1)
    p['attn_g'] = _pad_gain(W['attn_out_g'][l])
    return p


def _forward_layer(h, memn_in, tabs, W, l, name, gather=None):
    p = _layer_params(W, l)
    sv = {'h0': h, 'p': p}
    xn = _rms_fwd(h, W['norm_mix_g'][l], name=name + "rms_mix")
    proj = _mm([(xn, p['win'])], 'nn', f32, name=name + "mm_in")
    cqn = _rms_fwd(proj, W['q_norm_g'][l], col0=0, width=Q_LORA, name=name + "rms_q")
    ckvn = _rms_fwd(proj, W['kv_norm_g'][l], col0=Q_LORA, width=KV_LORA, name=name + "rms_kv")
    q = _mm([(cqn, p['wuq'])], 'nn', f32, name=name + "mm_uq")
    kv = _mm([(ckvn, p['wukv'])], 'nn', f32, name=name + "mm_ukv")
    qh, kh, vh = _rope_fwd(q, kv, proj, tabs, name=name + "rope")
    oh, lse, carried = _attn_fwd(qh, kh, vh, plan=gather[0] if gather else None, name=name + "attn")
    if gather:
        for n, t in gather[1](carried).items():
            W[n][l] = t
    p['wout'] = _pack_wout(W['w_out'][l])
    an = _rms_fwd(oh, p['attn_g'], n_valid=MLA_WIDTH, name=name + "rms_attn")
    bm, cm, a = W['ssm'][l]
    bmb, cmb = bm.astype(bf16), cm.astype(bf16)
    y, xc = _ssm_fwd(proj, bmb, cmb, a, W['ssm_d'][l], name=name + "ssm")
    z, sn, ge = _glu_fwd(y, W['ssm_w_glu'][l], W['ssm_b_glu'][l], W['ssm_out_g'][l], name=name + "glu")
    h1a = _mm([(an, p['wout'][:MLA_PAD])], 'nn', f32, res=h, name=name + "mm_out_a")
    h1 = _mm([(sn, p['wout'][MLA_PAD:])], 'nn', f32, res=h1a, name=name + "mm_out_s")
    hn2 = _rms_fwd(h1, W['norm_x_g'][l], name=name + "rms_x")
    memn = _rms_fwd(memn_in, W['mem_norm_g'][l], name=name + "rms_mem")
    qx = _mm([(hn2, W['w_xq'][l])], 'nn', bf16, name=name + "mm_xq")
    kvx = _mm([(memn, W['w_xkv'][l])], 'nn', bf16, name=name + "mm_xkv")
    ox = _xattn_fwd(qx, kvx, name=name + "xattn")
    h2 = _mm([(ox, W['w_xo'][l])], 'nn', f32, res=h1, name=name + "mm_xo")
    hn3 = _rms_fwd(h2, W['norm_ffn_g'][l], name=name + "rms_ffn")
    gate, up, act = _ffn_up(hn3, W['w_gate'][l], W['w_up'][l], name=name + "ffn_up")
    h3 = _mm([(act, W['w_down'][l])], 'nn', f32, res=h2, name=name + "mm_down")
    sv.update(xn=xn, proj=proj, cqn=cqn, ckvn=ckvn, qh=qh, kh=kh, vh=vh, oh=oh, lse=lse, an=an, bmb=bmb, cmb=cmb,
              a=a, y=y, xc=xc, z=z, sn=sn, ge=ge, h1=h1, hn2=hn2, memn=memn, qx=qx, kvx=kvx, ox=ox, h2=h2, hn3=hn3,
              gate=gate, up=up, act=act)
    return h3, sv


def _backward_layer(dh3, sv, memn_in, tabs, W, l, plan, name):
    p = sv['p']
    G = {}
    G['w_down'] = _mm([(sv['act'], dh3)], 'tn', f32, name=name + "dw_down")
    dgate, dup = _ffn_bwd_act(dh3, W['w_down'][l], sv['gate'], sv['up'], name=name + "ffn_bwd_act")
    dhn3 = _mm([(dgate, W['w_gate'][l]), (dup, W['w_up'][l])], 'nn', f32, name=name + "mm_dffn")
    G['w_gate'] = _mm([(dgate, sv['hn3'])], 'tn', f32, name=name + "dw_gate")
    G['w_up'] = _mm([(dup, sv['hn3'])], 'tn', f32, name=name + "dw_up")
    dh2, dg = _rms_bwd(sv['h2'], W['norm_ffn_g'][l], dhn3, res=dh3, name=name + "rmsb_ffn")
    G['norm_ffn_g'] = dg[0]
    G['w_xo'] = _mm([(sv['ox'], dh2)], 'tn', f32, name=name + "dw_xo")
    dox = _mm([(dh2, W['w_xo'][l])], 'nt', bf16, name=name + "mm_dxo")
    dqx, dkvx = _xattn_bwd(sv['qx'], sv['kvx'], dox, name=name + "xattn_bwd")
    G['w_xq'] = _mm([(sv['hn2'], dqx)], 'tn', f32, name=name + "dw_xq")
    G['w_xkv'] = _mm([(sv['memn'], dkvx)], 'tn', f32, name=name + "dw_xkv")
    dhn2 = _mm([(dqx, W['w_xq'][l])], 'nt', f32, name=name + "mm_dxq")
    dmemn = _mm([(dkvx, W['w_xkv'][l])], 'nt', f32, name=name + "mm_dxkv")
    dh1, dg = _rms_bwd(sv['h1'], W['norm_x_g'][l], dhn2, res=dh2, name=name + "rmsb_x")
    G['norm_x_g'] = dg[0]
    _, dg = _rms_bwd(memn_in, W['mem_norm_g'][l], dmemn, name=name + "rmsb_mem")
    G['mem_norm_g'] = dg[0]
    dwo_a = _mm([(sv['an'], dh1)], 'tn', f32, name=name + "dw_out_a")
    dwo_s = _mm([(sv['sn'], dh1)], 'tn', f32, name=name + "dw_out_s")
    G['w_out'] = _unpack_wout(jnp.concatenate([dwo_a, dwo_s], axis=0))
    dmixed = _mm([(dh1, p['wout'])], 'nt', f32, name=name + "mm_dout")
    dy, dz, dg, db = _glu_bwd(sv['y'], sv['z'], dmixed, W['ssm_w_glu'][l], W['ssm_out_g'][l], name=name + "glu_bwd")
    G['ssm_out_g'], G['ssm_b_glu'] = dg[0], db[0]
    G['ssm_w_glu'] = _mm([(sv['ge'], dz)], 'tn', f32, name=name + "dw_glu")
    du, dbm, dcm, da, dd = _ssm_bwd(sv['proj'], dy, sv['xc'], sv['bmb'], sv['cmb'], sv['a'], W['ssm_d'][l],
                                    name=name + "ssm_bwd")
    G['ssm_d'] = dd[0]
    G['ssm_raw'] = (dbm, dcm, da)
    doh, dg, delta = _rms_bwd(sv['oh'], p['attn_g'], dmixed, width=MLA_PAD, n_valid=MLA_WIDTH, delta=True,
                              out_dtype=bf16, name=name + "rmsb_attn")
    G['attn_out_g'] = _unpad_gain(dg[0])
    dqh, dkh, dvh, carried = _attn_bwd(sv['qh'], sv['kh'], sv['vh'], doh, sv['lse'], delta, plan=plan,
                                       name=name + "attn_bwd")
    dq, dkv, dkr = _rope_bwd(dqh, dkh, dvh, tabs, name=name + "rope_bwd")
    G['w_uq'] = _unpad_heads(_mm([(sv['cqn'], dq)], 'tn', f32, name=name + "dw_uq"), QK_NOPE + QK_ROPE, _Q_PIECES)
    dwukv = _mm([(sv['ckvn'], dkv)], 'tn', f32, name=name + "dw_ukv")
    G['w_ukv'] = (_unpad_heads(dwukv[:, :MLA_PAD], QK_NOPE + V_HEAD, _K_PIECES)
                  + _unpad_heads(dwukv[:, MLA_PAD:], QK_NOPE + V_HEAD, _V_PIECES))
    dcqn = _mm([(dq, p['wuq'])], 'nt', f32, name=name + "mm_duq")
    dckvn = _mm([(dkv, p['wukv'])], 'nt', f32, name=name + "mm_dukv")
    dcq, dg = _rms_bwd(sv['proj'], W['q_norm_g'][l], dcqn, col0=0, width=Q_LORA, out_dtype=bf16, name=name + "rmsb_q")
    G['q_norm_g'] = dg[0]
    dckv, dg = _rms_bwd(sv['proj'], W['kv_norm_g'][l], dckvn, col0=Q_LORA, width=KV_LORA, out_dtype=bf16,
                        name=name + "rmsb_kv")
    G['kv_norm_g'] = dg[0]
    dproj = jnp.concatenate([dcq, dckv, dkr, du.astype(bf16)], axis=1)
    G['w_in'] = _unpack_win(_mm([(sv['xn'], dproj)], 'tn', f32, name=name + "dw_in"))
    dxn = _mm([(dproj, p['win'])], 'nt', f32, name=name + "mm_din")
    dh0, dg = _rms_bwd(sv['h0'], W['norm_mix_g'][l], dxn, res=dh1, name=name + "rmsb_mix")
    G['norm_mix_g'] = dg[0]
    return dh0, G, carried


def _local_step(x, mem, positions, target, W, gathers=None, hook=None):
    tabs = _rope_tables(positions)
    ssm_in = [(W['ssm_lambda_re'][l], W['ssm_lambda_im'][l], W['ssm_log_dt'][l], W['ssm_b_re'][l], W['ssm_b_im'][l],
               W['ssm_c_re'][l], W['ssm_c_im'][l]) for l in range(DEPTH)]
    preps = [jax.vjp(_ssm_prep, *ssm_in[l]) for l in range(DEPTH)]
    W = dict(W)
    W['ssm'] = [preps[l][0] for l in range(DEPTH)]
    h = x
    saved = []
    for l in range(DEPTH):
        h, sv = _forward_layer(h, mem, tabs, W, l, f"l{l}_", gathers[l] if gathers else None)
        saved.append(sv)
    dh, dgf, loss = _loss_head(h, W['final_norm_g'], target)
    grads = [None] * DEPTH
    plan = None
    for l in reversed(range(DEPTH)):
        dh, G, carried = _backward_layer(dh, saved[l], mem, tabs, W, l, plan, f"l{l}b_")
        if plan is not None:
            hook[1](carried)
        dbm, dcm, da = G.pop('ssm_raw')
        names = ['ssm_lambda_re', 'ssm_lambda_im', 'ssm_log_dt', 'ssm_b_re', 'ssm_b_im', 'ssm_c_re', 'ssm_c_im']
        for n, g in zip(names, preps[l][1]((dbm, dcm, da))):
            G[n] = g
        grads[l] = G
        plan = hook[0](l, G) if (hook is not None and l > 0) else None
    out = {n: [grads[l][n] for l in range(DEPTH)] if n in SHARDED else jnp.stack([grads[l][n] for l in range(DEPTH)])
           for n in grads[0]}
    out['final_norm_g'] = dgf[0]
    return loss[0, 0], dh, out


_HBM = pl.BlockSpec(memory_space=pltpu.HBM)


def _me():
    return lax.axis_index("x"), lax.axis_index("y"), lax.axis_index("c")


def _chip_peers(x, y, c):
    devs = [(1 - x, y, c), (x, 1 - y, c), (1 - x, 1 - y, c)]
    return devs, [2 * d[0] + d[1] for d in devs]


def _gather_plan(xs, half_first):
    n = len(xs)
    if half_first:
        ins = [t.reshape(2, 1, *t.shape[1:]) for t in xs]
        outs = [jax.ShapeDtypeStruct((2, 4, *t.shape[1:]), t.dtype) for t in xs]
    else:
        ins = [t.reshape(1, 2, t.shape[0] // 2, t.shape[1]) for t in xs]
        outs = [jax.ShapeDtypeStruct((4, 2, t.shape[0] // 2, t.shape[1]), t.dtype) for t in xs]

    def own(ref, h):
        return ref.at[h] if half_first else ref.at[:, h]

    def slot(ref, h, j):
        return ref.at[h, pl.ds(j, 1)] if half_first else ref.at[pl.ds(j, 1), h]

    def copies(src, dst, send, recv):
        x, y, c = _me()
        jme = 2 * x + y
        devs, js = _chip_peers(x, y, c)
        half, other = pl.ds(c, 1), pl.ds(1 - c, 1)
        mk = pltpu.make_async_remote_copy
        for i in range(n):
            for k in range(3):
                out_cp = mk(own(src[i], half), slot(dst[i], half, jme), send.at[6 * i + k], recv.at[6 * i + k],
                            device_id=devs[k], device_id_type=MESH)
                in_cp = mk(own(src[i], half), slot(dst[i], half, js[k]), send.at[6 * i + k], recv.at[6 * i + k],
                           device_id=devs[k], device_id_type=MESH)
                pass_cp = mk(slot(dst[i], half, js[k]), slot(dst[i], half, js[k]), send.at[6 * i + 3 + k],
                             recv.at[6 * i + 3 + k], device_id=(x, y, 1 - c), device_id_type=MESH)
                got_cp = mk(slot(dst[i], other, js[k]), slot(dst[i], other, js[k]), send.at[6 * i + 3 + k],
                            recv.at[6 * i + 3 + k], device_id=(x, y, 1 - c), device_id_type=MESH)
                yield out_cp, in_cp, pass_cp, got_cp

    def start(*refs):
        for out_cp, _, _, _ in copies(*refs):
            out_cp.start()

    def forward(*refs):
        for _, in_cp, pass_cp, _ in copies(*refs):
            in_cp.wait_recv()
            pass_cp.start()

    def finish(*refs):
        for out_cp, _, pass_cp, got_cp in copies(*refs):
            got_cp.wait_recv()
            out_cp.wait_send()
            pass_cp.wait_send()

    return dict(n=n, ins=ins, outs=outs, nsem=6 * n, start=start, forward=forward, finish=finish)


def _plan_refs(plan, refs):
    n = plan['n']
    return refs[:n], refs[n:2 * n], refs[2 * n], refs[2 * n + 1]


def _run_plan(plan, name):
    n = plan['n']

    def body(*refs):
        args = _plan_refs(plan, refs)
        plan['start'](*args)
        plan['forward'](*args)
        plan['finish'](*args)

    return _pc(body, in_specs=[_HBM] * n, out_specs=[_HBM] * n, out_shape=plan['outs'],
               scratch_shapes=[pltpu.SemaphoreType.DMA((plan['nsem'],)), pltpu.SemaphoreType.DMA((plan['nsem'],))],
               compiler_params=pltpu.CompilerParams(has_side_effects=True), name=name)(*plan['ins'])


def _fill_own(gathered, own, half_first):
    jme = (2 * lax.axis_index("x") + lax.axis_index("y")).astype(jnp.int32)
    zero = jnp.int32(0)
    if half_first:
        return lax.dynamic_update_slice(gathered, own[:, None], (zero, jme, zero, zero))
    return lax.dynamic_update_slice(gathered, own.reshape(1, *gathered.shape[1:]), (jme, zero, zero, zero))


def _exchange_halves(gs, name):
    n = len(gs)

    def body(*refs):
        src, dst = refs[:n], refs[n:2 * n]
        send, recv = refs[2 * n:]
        x, y, c = _me()
        cps = []
        for i in range(n):
            cp = pltpu.make_async_remote_copy(src[i].at[:, pl.ds(1 - c, 1)], dst[i], send.at[i], recv.at[i],
                                              device_id=(x, y, 1 - c), device_id_type=MESH)
            cp.start()
            cps.append(cp)
        for cp in cps:
            cp.wait()

    outs = [jax.ShapeDtypeStruct((4, 1, *g.shape[2:]), g.dtype) for g in gs]
    return _pc(body, in_specs=[_HBM] * n, out_specs=[_HBM] * n, out_shape=outs,
               scratch_shapes=[pltpu.SemaphoreType.DMA((n,)), pltpu.SemaphoreType.DMA((n,))],
               compiler_params=pltpu.CompilerParams(has_side_effects=True), name=name)(*gs)


def _scatter_plan(ps):
    n = len(ps)

    def copies(src, dst, send, recv):
        x, y, c = _me()
        devs, js = _chip_peers(x, y, c)
        for i in range(n):
            for k in range(3):
                yield pltpu.make_async_remote_copy(src[i].at[pl.ds(js[k], 1)], dst[i].at[k], send.at[3 * i + k],
                                                   recv.at[3 * i + k], device_id=devs[k], device_id_type=MESH)

    def start(*refs):
        for cp in copies(*refs):
            cp.start()

    def finish(*refs):
        for cp in copies(*refs):
            cp.wait()

    outs = [jax.ShapeDtypeStruct((3, 1, *p.shape[1:]), p.dtype) for p in ps]
    return dict(n=n, ins=list(ps), outs=outs, nsem=3 * n, start=start, forward=lambda *refs: None, finish=finish)


def _swap_sibling(hs, name):
    n = len(hs)

    def body(*refs):
        src, dst = refs[:n], refs[n:2 * n]
        send, recv = refs[2 * n:]
        x, y, c = _me()
        cps = []
        for i in range(n):
            cp = pltpu.make_async_remote_copy(src[i], dst[i], send.at[i], recv.at[i], device_id=(x, y, 1 - c),
                                              device_id_type=MESH)
            cp.start()
            cps.append(cp)
        for cp in cps:
            cp.wait()

    outs = [jax.ShapeDtypeStruct(h.shape, h.dtype) for h in hs]
    return _pc(body, in_specs=[_HBM] * n, out_specs=[_HBM] * n, out_shape=outs,
               scratch_shapes=[pltpu.SemaphoreType.DMA((n,)), pltpu.SemaphoreType.DMA((n,))],
               compiler_params=pltpu.CompilerParams(has_side_effects=True), name=name)(*hs)


ELEMWISE_VMEM_BUDGET = 24 * 1024 * 1024


def _row_tile(r, n, narrays):
    limit = ELEMWISE_VMEM_BUDGET // (2 * 4 * narrays * n)
    best = SUBLANES
    for t in range(16, r + 1, 16):
        if r % t == 0 and t <= limit:
            best = t
    return best


def _add_half(g, r1, cidx, name):
    _, _, r, n = g.shape
    tr = _row_tile(r, n, 3)

    def body(c_ref, g_ref, r_ref, o_ref):
        o_ref[...] = (g_ref[...] + r_ref[...]).astype(GRAD_TRANSIT)

    blk = lambda f: pl.BlockSpec((None, None, tr, n), f)
    gs = pltpu.PrefetchScalarGridSpec(
        num_scalar_prefetch=1, grid=(4, r // tr),
        in_specs=[blk(lambda j, i, c: (j, c[0], i, 0)), blk(lambda j, i, c: (j, 0, i, 0))],
        out_specs=pl.BlockSpec((None, tr, n), lambda j, i, c: (j, i, 0)))
    return _pc(body, grid_spec=gs, out_shape=jax.ShapeDtypeStruct((4, r, n), GRAD_TRANSIT),
               compiler_params=_cp(("parallel", "parallel")), name=name)(cidx, g, r1)


def _add_chips(p, r3, jidx, name):
    _, r, n = p.shape
    tr = _row_tile(r, n, 5)

    def body(j_ref, p_ref, a_ref, b_ref, c_ref, o_ref):
        o_ref[...] = ((p_ref[...].astype(f32) + a_ref[...].astype(f32)) + b_ref[...].astype(f32)) + c_ref[...].astype(f32)

    rblk = lambda k: pl.BlockSpec((None, None, tr, n), lambda i, j: (k, 0, i, 0))
    gs = pltpu.PrefetchScalarGridSpec(
        num_scalar_prefetch=1, grid=(r // tr,),
        in_specs=[pl.BlockSpec((None, tr, n), lambda i, j: (j[0], i, 0)), rblk(0), rblk(1), rblk(2)],
        out_specs=pl.BlockSpec((tr, n), lambda i, j: (i, 0)))
    return _pc(body, grid_spec=gs, out_shape=jax.ShapeDtypeStruct((r, n), f32),
               compiler_params=_cp(("parallel",)), name=name)(jidx, p, r3, r3, r3)


def _adamw_halves(w, mine, theirs, m, v, cidx, name):
    L, r, n = w.shape
    r2 = r // 2
    tr = _row_tile(r2, n, 11)
    c1 = 1.0 / (1.0 - ADAM_B1 ** ADAM_STEP)
    c2 = 1.0 / (1.0 - ADAM_B2 ** ADAM_STEP)

    def body(c_ref, w_ref, a0_ref, b0_ref, a1_ref, b1_ref, m_ref, v_ref, g_ref, d_ref, mo_ref, vo_ref):
        l, hf = pl.program_id(0), pl.program_id(1)
        own = hf == c_ref[0]
        gv = jnp.where(l == 0, jnp.where(own, a0_ref[...], b0_ref[...]), jnp.where(own, a1_ref[...], b1_ref[...]))
        m2 = ADAM_B1 * m_ref[...] + (1.0 - ADAM_B1) * gv
        v2 = ADAM_B2 * v_ref[...] + (1.0 - ADAM_B2) * (gv * gv)
        g_ref[...] = gv
        d_ref[...] = -ADAM_LR * ((m2 * c1) / (jnp.sqrt(v2 * c2) + ADAM_EPS) + ADAM_WD * w_ref[...])
        mo_ref[...] = m2
        vo_ref[...] = v2

    full = pl.BlockSpec((None, None, tr, n), lambda l, hf, i, c: (l, hf, i, 0))
    half = pl.BlockSpec((tr, n), lambda l, hf, i, c: (i, 0))
    gs = pltpu.PrefetchScalarGridSpec(num_scalar_prefetch=1, grid=(L, 2, r2 // tr),
                                      in_specs=[full, half, half, half, half, full, full], out_specs=[full] * 4)
    four = lambda t: t.reshape(L, 2, r2, n)
    outs = _pc(body, grid_spec=gs, out_shape=[jax.ShapeDtypeStruct((L, 2, r2, n), f32)] * 4,
               compiler_params=_cp(("parallel", "parallel", "parallel")), name=name)(
        cidx, four(w), mine[0], theirs[0], mine[1], theirs[1], four(m), four(v))
    return [t.reshape(w.shape) for t in outs]


def _adamw_whole(w, g, m, v, name):
    c1 = 1.0 / (1.0 - ADAM_B1 ** ADAM_STEP)
    c2 = 1.0 / (1.0 - ADAM_B2 ** ADAM_STEP)

    def body(w_ref, g_ref, m_ref, v_ref, d_ref, mo_ref, vo_ref):
        gv = g_ref[...]
        m2 = ADAM_B1 * m_ref[...] + (1.0 - ADAM_B1) * gv
        v2 = ADAM_B2 * v_ref[...] + (1.0 - ADAM_B2) * (gv * gv)
        d_ref[...] = -ADAM_LR * ((m2 * c1) / (jnp.sqrt(v2 * c2) + ADAM_EPS) + ADAM_WD * w_ref[...])
        mo_ref[...] = m2
        vo_ref[...] = v2

    return _pc(body, out_shape=[jax.ShapeDtypeStruct(w.shape, f32)] * 3, name=name)(w, g, m, v)


def _full_from_gathered(name, t):
    r, n = 2 * t.shape[2], t.shape[3]
    if SHARDED[name] == 1 or name in TRANSPOSED:
        return t.reshape(4 * r, n)
    return t.reshape(4, r, n).transpose(1, 0, 2).reshape(r, 4 * n)


def _shard_major(name, g):
    R, C = g.shape
    if SHARDED[name] == 1 or name in TRANSPOSED:
        return g.reshape(4, 2, R // 8, C)
    return g.reshape(R, 4, C // 4).transpose(1, 0, 2).reshape(4, 2, R // 2, C // 4)


_SMALL_ROWS = 288


def _pack_small(d):
    flat = jnp.concatenate([d[n].reshape(-1) for n in SMALL])
    total = 2 * 4 * _SMALL_ROWS * LANES
    flat = jnp.concatenate([flat, jnp.zeros((total - flat.shape[0],), f32)])
    return flat.reshape(4, 2, _SMALL_ROWS, LANES)


def _unpack_small(t, like):
    flat = t.reshape(-1)
    out, off = {}, 0
    for n in SMALL:
        sz = math.prod(like[n].shape)
        out[n] = flat[off:off + sz].reshape(like[n].shape)
        off += sz
    return out


def kernel(x, mem, positions, norm_mix_g, w_in, q_norm_g, w_uq, kv_norm_g, w_ukv, ssm_lambda_re, ssm_lambda_im, ssm_log_dt, ssm_b_re, ssm_b_im, ssm_c_re, ssm_c_im, ssm_d, ssm_w_glu, ssm_b_glu, attn_out_g, ssm_out_g, w_out, norm_x_g, mem_norm_g, w_xq, w_xkv, w_xo, norm_ffn_g, w_gate, w_up, w_down, final_norm_g, loss_target, m_norm_mix_g, m_w_in, m_q_norm_g, m_w_uq, m_kv_norm_g, m_w_ukv, m_ssm_lambda_re, m_ssm_lambda_im, m_ssm_log_dt, m_ssm_b_re, m_ssm_b_im, m_ssm_c_re, m_ssm_c_im, m_ssm_d, m_ssm_w_glu, m_ssm_b_glu, m_attn_out_g, m_ssm_out_g, m_w_out, m_norm_x_g, m_mem_norm_g, m_w_xq, m_w_xkv, m_w_xo, m_norm_ffn_g, m_w_gate, m_w_up, m_w_down, m_final_norm_g, v_norm_mix_g, v_w_in, v_q_norm_g, v_w_uq, v_kv_norm_g, v_w_ukv, v_ssm_lambda_re, v_ssm_lambda_im, v_ssm_log_dt, v_ssm_b_re, v_ssm_b_im, v_ssm_c_re, v_ssm_c_im, v_ssm_d, v_ssm_w_glu, v_ssm_b_glu, v_attn_out_g, v_ssm_out_g, v_w_out, v_norm_x_g, v_mem_norm_g, v_w_xq, v_w_xkv, v_w_xo, v_norm_ffn_g, v_w_gate, v_w_up, v_w_down, v_final_norm_g):
    given = dict(locals())
    swap = lambda n, t: jnp.swapaxes(t, *TRANSPOSED[n]) if n in TRANSPOSED else t
    w = {n: swap(n, given[n]) for n in WEIGHTS}
    m = {n: swap(n, given["m_" + n]) for n in WEIGHTS}
    v = {n: swap(n, given["v_" + n]) for n in WEIGHTS}
    big = list(SHARDED)

    shards = {n: w[n].astype(bf16) for n in big}
    early = [n for n in big if n in EARLY_WEIGHTS]
    rest = [n for n in big if n not in EARLY_WEIGHTS]

    def full(names, results, l):
        return {n: _full_from_gathered(n, _fill_own(t, shards[n][l], False)) for n, t in zip(names, results)}

    first = _run_plan(_gather_plan([shards[n][l] for l in range(DEPTH) for n in early], False), "allgather_weights_early")
    W = {n: [None] * DEPTH for n in big}
    for l in range(DEPTH):
        for n, t in full(early, first[l * len(early):(l + 1) * len(early)], l).items():
            W[n][l] = t
    gathers = [(_gather_plan([shards[n][l] for n in rest], False), functools.partial(full, rest, l=l))
               for l in range(DEPTH)]
    W.update({n: w[n] for n in SMALL})

    cidx = lax.axis_index("c").astype(jnp.int32).reshape(1)
    jidx = (2 * lax.axis_index("x") + lax.axis_index("y")).astype(jnp.int32).reshape(1)
    early = {}

    def chip_sums(gs, names, tag):
        r1 = _exchange_halves(gs, f"grad_exchange_halves_{tag}")
        return [_add_half(g, r, cidx, f"grad_add_half_{tag}_{n}") for n, g, r in zip(names, gs, r1)]

    def launch(l, G):
        early['ps'] = chip_sums([_shard_major(n, G[n]) for n in big], big, f"l{l}")
        return _scatter_plan(early['ps'])

    def done(results):
        early['r3'] = results

    loss, dx, grads = _local_step(x[0], mem[0], positions[0], loss_target[0], W, gathers=gathers,
                                  hook=(launch, done))
    loss = lax.psum(loss, ("x", "y", "c"))

    names = big + ["small"]
    ps0 = chip_sums([_shard_major(n, grads[n][0]) for n in big] + [_pack_small(grads)], names, "l0")
    r3_0 = _run_plan(_scatter_plan(ps0), "grad_scatter_chips_l0")
    hs0 = [_add_chips(p, r, jidx, f"grad_add_chips_l0_{n}") for n, p, r in zip(names, ps0, r3_0)]
    hs1 = [_add_chips(p, r, jidx, f"grad_add_chips_l1_{n}") for n, p, r in zip(big, early['ps'], early['r3'])]
    ts = _swap_sibling(hs0 + hs1, "grad_swap_sibling")
    ts0, ts1 = ts[:len(hs0)], ts[len(hs0):]

    out_g, out_d, out_m, out_v = {}, {}, {}, {}
    for i, n in enumerate(big):
        out_g[n], out_d[n], out_m[n], out_v[n] = _adamw_halves(w[n], (hs0[i], hs1[i]), (ts0[i], ts1[i]), m[n], v[n],
                                                               cidx, f"adamw_{n}")
    both = jnp.stack([hs0[-1], ts0[-1]])
    piece = jnp.where(cidx[0] == 0, both, both[::-1]).reshape(2 * _SMALL_ROWS, LANES)
    gsm = _fill_own(_run_plan(_gather_plan([piece], False), "allgather_small")[0], piece, False)
    out_g.update(_unpack_small(gsm, w))
    for n in SMALL:
        two = lambda t: t.reshape(1, -1) if t.ndim == 1 else t
        d_, m_, v_ = _adamw_whole(two(w[n]), two(out_g[n]), two(m[n]), two(v[n]), f"adamw_{n}")
        out_d[n], out_m[n], out_v[n] = (t.reshape(w[n].shape) for t in (d_, m_, v_))

    outs = [[swap(n, d[n]) for n in WEIGHTS] for d in (out_g, out_d, out_m, out_v)]
    return (loss, dx.reshape(x.shape), *outs[0], *outs[1], *outs[2], *outs[3])
```

```python
import functools
import math

import jax
import jax.numpy as jnp
from jax import lax
from jax.experimental import pallas as pl
from jax.experimental.pallas import tpu as pltpu

f32, bf16 = jnp.float32, jnp.bfloat16

D_MODEL = 1024
DEPTH = 2
MLA_HEADS = 8
QK_NOPE = 64
QK_ROPE = 32
V_HEAD = 64
Q_LORA = 256
KV_LORA = 128
MLA_WIDTH = MLA_HEADS * V_HEAD
ROPE_THETA = 10000.0
SSM_WIDTH = 512
SSM_GROUP = 16
SSM_GROUPS = 32
SSM_STATE = 64
IN_WIDTH = Q_LORA + KV_LORA + QK_ROPE + SSM_WIDTH
X_HEADS = 4
X_HEAD_DIM = D_MODEL // X_HEADS
D_FF = 2816
EPS = 1e-6
ADAM_LR, ADAM_B1, ADAM_B2, ADAM_EPS, ADAM_WD, ADAM_STEP = 0.001, 0.9, 0.999, 1e-08, 0.01, 10

LANES = 128
SUBLANES = 8
HEAD_PAD = 128
MLA_PAD = MLA_HEADS * HEAD_PAD
SSM_MACRO = 4
MACRO_CH = SSM_WIDTH // SSM_MACRO
MACRO_ST = SSM_GROUPS // SSM_MACRO * SSM_STATE
VMEM_LIMIT = 56 * 1024 * 1024
GRAD_TRANSIT = bf16

WEIGHTS = ['norm_mix_g', 'w_in', 'q_norm_g', 'w_uq', 'kv_norm_g', 'w_ukv', 'ssm_lambda_re', 'ssm_lambda_im',
           'ssm_log_dt', 'ssm_b_re', 'ssm_b_im', 'ssm_c_re', 'ssm_c_im', 'ssm_d', 'ssm_w_glu', 'ssm_b_glu',
           'attn_out_g', 'ssm_out_g', 'w_out', 'norm_x_g', 'mem_norm_g', 'w_xq', 'w_xkv', 'w_xo', 'norm_ffn_g',
           'w_gate', 'w_up', 'w_down', 'final_norm_g']
SHARDED = {'w_in': 1, 'w_uq': 2, 'w_ukv': 2, 'ssm_w_glu': 1, 'w_out': 1, 'w_xq': 1, 'w_xkv': 2, 'w_xo': 1,
           'w_gate': 2, 'w_up': 2, 'w_down': 1}
SMALL = [n for n in WEIGHTS if n not in SHARDED]
EARLY_WEIGHTS = ('w_in', 'w_uq', 'w_ukv')
FFN_WEIGHTS = ('w_gate', 'w_up', 'w_down')
TRANSPOSED = {'w_gate': (1, 2), 'w_up': (1, 2), 'ssm_b_re': (2, 3), 'ssm_b_im': (2, 3)}
MESH = pl.DeviceIdType.MESH


def _pc(body, **kw):
    return pl.pallas_call(body, **kw)


def _pick(n, prefs):
    for p in prefs:
        if n % p == 0:
            return p
    return n


def _cp(sem=None):
    return pltpu.CompilerParams(dimension_semantics=sem, vmem_limit_bytes=VMEM_LIMIT)


_TILE_CANDS = (1024, 1408, 512, 256, 128)
MM_VMEM_BUDGET = 40 * 1024 * 1024


def _mm_tiles(M, K, N, a_bytes, b_bytes, o_bytes, npair, has_res, need_acc):
    best = None
    for tm in _TILE_CANDS:
        for tk in _TILE_CANDS:
            if M % tm or K % tk:
                continue
            vm = npair * (2 * tm * tk * a_bytes + 2 * tk * N * b_bytes) + 2 * tm * N * o_bytes
            vm += tm * N * 4 * (1 + need_acc + 2 * has_res)
            if a_bytes == 4:
                vm += npair * tm * tk * 2
            if b_bytes == 4:
                vm += npair * tk * N * 2
            if vm <= MM_VMEM_BUDGET and (best is None or tm * tk > best[0]):
                best = (tm * tk, tm, tk)
    if best is None:
        return _pick(M, (256, 128)), _pick(K, (256, 128))
    return best[1], best[2]


def _mm(pairs, mode, out_dtype, res=None, name="mm"):
    a0, b0 = pairs[0]
    if mode == 'nn':
        (M, K), N = a0.shape, b0.shape[1]
        dims = (((1,), (0,)), ((), ()))
    elif mode == 'nt':
        (M, K), N = a0.shape, b0.shape[0]
        dims = (((1,), (1,)), ((), ()))
    else:
        (K, M), N = a0.shape, b0.shape[1]
        dims = (((0,), (0,)), ((), ()))
    npair = len(pairs)
    has_res = res is not None
    direct = out_dtype == f32
    tm, tk = _mm_tiles(M, K, N, a0.dtype.itemsize, b0.dtype.itemsize, jnp.dtype(out_dtype).itemsize, npair, has_res,
                       not direct)
    nk = K // tk

    def body(*refs):
        ins = refs[:2 * npair]
        res_ref = refs[2 * npair] if has_res else None
        o_ref = refs[2 * npair + has_res]
        acc = o_ref if direct else refs[2 * npair + has_res + 1]
        k = pl.program_id(1)
        s = None
        for p in range(npair):
            d = lax.dot_general(ins[2 * p][...].astype(bf16), ins[2 * p + 1][...].astype(bf16), dims,
                                preferred_element_type=f32)
            s = d if s is None else s + d

        @pl.when(k == 0)
        def _():
            acc[...] = s

        @pl.when(k > 0)
        def _():
            acc[...] += s

        if has_res or not direct:
            @pl.when(k == nk - 1)
            def _():
                r = acc[...]
                if has_res:
                    r = r + res_ref[...]
                o_ref[...] = r.astype(out_dtype)

    if mode == 'nn':
        a_spec = pl.BlockSpec((tm, tk), lambda i, k: (i, k))
        b_spec = pl.BlockSpec((tk, N), lambda i, k: (k, 0))
    elif mode == 'nt':
        a_spec = pl.BlockSpec((tm, tk), lambda i, k: (i, k))
        b_spec = pl.BlockSpec((N, tk), lambda i, k: (0, k))
    else:
        a_spec = pl.BlockSpec((tk, tm), lambda i, k: (k, i))
        b_spec = pl.BlockSpec((tk, N), lambda i, k: (k, 0))
    o_spec = pl.BlockSpec((tm, N), lambda i, k: (i, 0))
    in_specs = [a_spec, b_spec] * npair + ([o_spec] if has_res else [])
    args = [t for p in pairs for t in p] + ([res] if has_res else [])
    return _pc(body, grid=(M // tm, nk), in_specs=in_specs, out_specs=o_spec,
               out_shape=jax.ShapeDtypeStruct((M, N), out_dtype),
               scratch_shapes=[] if direct else [pltpu.VMEM((tm, N), f32)],
               compiler_params=_cp(("parallel", "arbitrary")), name=name)(*args)


def _rms_fwd(x, g, *, col0=0, width=None, n_valid=None, out_dtype=bf16, name="rms_fwd"):
    S = x.shape[0]
    width = width or x.shape[1]
    n_valid = n_valid or width
    ts = _pick(S, (512, 256, 128))
    cb = col0 // width

    def body(x_ref, g_ref, o_ref):
        xv = x_ref[...]
        ms = jnp.sum(xv * xv, axis=-1, keepdims=True) * (1.0 / n_valid)
        o_ref[...] = (xv * lax.rsqrt(ms + EPS) * g_ref[...]).astype(out_dtype)

    return _pc(body, grid=(S // ts,),
               in_specs=[pl.BlockSpec((ts, width), lambda i: (i, cb)), pl.BlockSpec((1, width), lambda i: (0, 0))],
               out_specs=pl.BlockSpec((ts, width), lambda i: (i, 0)),
               out_shape=jax.ShapeDtypeStruct((S, width), out_dtype),
               compiler_params=_cp(("parallel",)), name=name)(x, g.reshape(1, width))


def _rms_bwd(x, g, dy, *, col0=0, dcol0=0, width=None, n_valid=None, res=None, out_dtype=f32, delta=False,
             name="rms_bwd"):
    S = x.shape[0]
    width = width or x.shape[1]
    n_valid = n_valid or width
    ts = _pick(S, (512, 256, 128))
    cb, dcb = col0 // width, dcol0 // width
    has_res = res is not None

    def body(*refs):
        x_ref, g_ref, dy_ref = refs[:3]
        res_ref = refs[3] if has_res else None
        outs = refs[3 + has_res:]
        dx_ref, dg_ref = outs[0], outs[1]
        i = pl.program_id(0)
        xv = x_ref[...]
        gv = g_ref[...]
        dyv = dy_ref[...].astype(f32)
        rstd = lax.rsqrt(jnp.sum(xv * xv, axis=-1, keepdims=True) * (1.0 / n_valid) + EPS)
        xh = xv * rstd
        dxh = dyv * gv
        mean = jnp.sum(dxh * xh, axis=-1, keepdims=True) * (1.0 / n_valid)
        dx = rstd * (dxh - xh * mean)
        if delta:
            d_ref = outs[2]
            for h in range(width // LANES):
                sl = slice(h * LANES, (h + 1) * LANES)
                dsum = jnp.sum(dx[:, sl] * xv[:, sl], axis=-1, keepdims=True)
                d_ref[:, sl] = jnp.broadcast_to(dsum, (ts, LANES))
        if has_res:
            dx = dx + res_ref[...]
        dx_ref[...] = dx.astype(out_dtype)

        @pl.when(i == 0)
        def _():
            dg_ref[...] = jnp.zeros_like(dg_ref)

        dg_ref[...] += jnp.sum(dyv * xh, axis=0, keepdims=True)

    blk = lambda c: pl.BlockSpec((ts, width), lambda i: (i, c))
    in_specs = [blk(cb), pl.BlockSpec((1, width), lambda i: (0, 0)), blk(dcb)] + ([blk(0)] if has_res else [])
    out_specs = [blk(0), pl.BlockSpec((1, width), lambda i: (0, 0))] + ([blk(0)] if delta else [])
    out_shape = [jax.ShapeDtypeStruct((S, width), out_dtype), jax.ShapeDtypeStruct((1, width), f32)] + (
        [jax.ShapeDtypeStruct((S, width), f32)] if delta else [])
    args = [x, g.reshape(1, width), dy] + ([res] if has_res else [])
    return _pc(body, grid=(S // ts,), in_specs=in_specs, out_specs=out_specs, out_shape=out_shape,
               compiler_params=_cp(("arbitrary",)), name=name)(*args)


def _loss_head(h, g, target, name="loss_head"):
    S, D = h.shape
    ts = _pick(S, (512, 256, 128))

    def body(h_ref, g_ref, t_ref, dh_ref, dg_ref, loss_ref):
        i = pl.program_id(0)
        xv = h_ref[...]
        gv = g_ref[...]
        rstd = lax.rsqrt(jnp.sum(xv * xv, axis=-1, keepdims=True) * (1.0 / D) + EPS)
        xh = xv * rstd
        err = xh * gv - t_ref[...]
        dyv = err * (1.0 / D)
        dxh = dyv * gv
        mean = jnp.sum(dxh * xh, axis=-1, keepdims=True) * (1.0 / D)
        dh_ref[...] = rstd * (dxh - xh * mean)

        @pl.when(i == 0)
        def _():
            dg_ref[...] = jnp.zeros_like(dg_ref)
            loss_ref[...] = jnp.zeros_like(loss_ref)

        dg_ref[...] += jnp.sum(dyv * xh, axis=0, keepdims=True)
        part = jnp.sum(jnp.sum(err * err, axis=-1, keepdims=True), axis=0, keepdims=True) * (0.5 / D)
        loss_ref[...] += jnp.broadcast_to(part, (1, LANES))

    blk = pl.BlockSpec((ts, D), lambda i: (i, 0))
    row = pl.BlockSpec((1, D), lambda i: (0, 0))
    return _pc(body, grid=(S // ts,), in_specs=[blk, row, blk],
               out_specs=[blk, row, pl.BlockSpec((1, LANES), lambda i: (0, 0))],
               out_shape=[jax.ShapeDtypeStruct((S, D), f32), jax.ShapeDtypeStruct((1, D), f32),
                          jax.ShapeDtypeStruct((1, LANES), f32)],
               compiler_params=_cp(("arbitrary",)), name=name)(h, g.reshape(1, D), target)


def _rope_apply(x, tc, s1, s2):
    return x * tc + pltpu.roll(x, LANES - 16, 1) * s1 + pltpu.roll(x, 16, 1) * s2


def _rope_apply_t(dy, tc, s1, s2):
    return dy * tc + pltpu.roll(dy * s1, 16, 1) + pltpu.roll(dy * s2, LANES - 16, 1)


def _rope_fwd(q, kv, proj, tabs, name="rope_fwd"):
    S = q.shape[0]
    ts = _pick(S, (512, 256, 128))
    scale = (QK_NOPE + QK_ROPE) ** -0.5

    def body(q_ref, kk_ref, kvv_ref, kr_ref, tc_ref, s1_ref, s2_ref, qh_ref, kh_ref, vh_ref):
        tc, s1, s2 = tc_ref[...], s1_ref[...], s2_ref[...]
        krr = _rope_apply(pltpu.roll(kr_ref[...], QK_NOPE, 1), tc, s1, s2)
        for h in range(MLA_HEADS):
            sl = slice(h * HEAD_PAD, (h + 1) * HEAD_PAD)
            qh_ref[:, sl] = (_rope_apply(q_ref[:, sl], tc, s1, s2) * scale).astype(bf16)
            kh_ref[:, sl] = (kk_ref[:, sl] + krr).astype(bf16)
        vh_ref[...] = kvv_ref[...].astype(bf16)

    wide = lambda c: pl.BlockSpec((ts, MLA_PAD), lambda i: (i, c))
    tab = pl.BlockSpec((ts, LANES), lambda i: (i, 0))
    return _pc(body, grid=(S // ts,),
               in_specs=[wide(0), wide(0), wide(1), pl.BlockSpec((ts, LANES), lambda i: (i, 3)), tab, tab, tab],
               out_specs=[wide(0)] * 3, out_shape=[jax.ShapeDtypeStruct((S, MLA_PAD), bf16)] * 3,
               compiler_params=_cp(("parallel",)), name=name)(q, kv, kv, proj, *tabs)


def _rope_bwd(dqh, dkh, dvh, tabs, name="rope_bwd"):
    S = dqh.shape[0]
    ts = _pick(S, (512, 256, 128))
    scale = (QK_NOPE + QK_ROPE) ** -0.5

    def body(dq_ref, dk_ref, dv_ref, tc_ref, s1_ref, s2_ref, oq_ref, okv_ref, okr_ref):
        tc, s1, s2 = tc_ref[...], s1_ref[...], s2_ref[...]
        ksum = None
        for h in range(MLA_HEADS):
            sl = slice(h * HEAD_PAD, (h + 1) * HEAD_PAD)
            oq_ref[:, sl] = (_rope_apply_t(dq_ref[:, sl], tc, s1, s2) * scale).astype(bf16)
            dk = dk_ref[:, sl]
            okv_ref[:, sl] = dk.astype(bf16)
            ksum = dk if ksum is None else ksum + dk
        okv_ref[:, MLA_PAD:] = dv_ref[...].astype(bf16)
        dkr = pltpu.roll(_rope_apply_t(ksum, tc, s1, s2), LANES - QK_NOPE, 1)
        lane = lax.broadcasted_iota(jnp.int32, (ts, LANES), 1)
        okr_ref[...] = jnp.where(lane < QK_ROPE, dkr, 0.0).astype(bf16)

    wide = pl.BlockSpec((ts, MLA_PAD), lambda i: (i, 0))
    tab = pl.BlockSpec((ts, LANES), lambda i: (i, 0))
    return _pc(body, grid=(S // ts,), in_specs=[wide, wide, wide, tab, tab, tab],
               out_specs=[wide, pl.BlockSpec((ts, 2 * MLA_PAD), lambda i: (i, 0)), tab],
               out_shape=[jax.ShapeDtypeStruct((S, MLA_PAD), bf16), jax.ShapeDtypeStruct((S, 2 * MLA_PAD), bf16),
                          jax.ShapeDtypeStruct((S, LANES), bf16)],
               compiler_params=_cp(("parallel",)), name=name)(dqh, dkh, dvh, *tabs)


ATT_BLK = 1024


def _attn_fwd(qh, kh, vh, plan=None, name="attn_fwd"):
    S = qh.shape[0]
    tq = tk = min(S, ATT_BLK)
    nq, nk = S // tq, S // tk
    npl = plan['n'] if plan else 0

    def body(*refs):
        q_ref, k_ref, v_ref = refs[:3]
        o_ref, lse_ref = refs[3 + npl:5 + npl]
        m_sc, l_sc, acc_sc = refs[5 + 2 * npl:8 + 2 * npl]
        h, i, j = pl.program_id(0), pl.program_id(1), pl.program_id(2)
        if plan:
            pargs = (refs[3:3 + npl], refs[5 + npl:5 + 2 * npl], refs[8 + 2 * npl], refs[9 + 2 * npl])
            first = (i == 0) & (j == 0)
            pl.when((h == 0) & first)(functools.partial(plan['start'], *pargs))
            pl.when((h == (3 * MLA_HEADS) // 4) & first)(functools.partial(plan['forward'], *pargs))
            pl.when((h == MLA_HEADS - 1) & (i == nq - 1) & (j == nk - 1))(functools.partial(plan['finish'], *pargs))

        @pl.when(j == 0)
        def _():
            m_sc[...] = jnp.full_like(m_sc, -1e30)
            l_sc[...] = jnp.zeros_like(l_sc)
            acc_sc[...] = jnp.zeros_like(acc_sc)

        def step(masked):
            s = lax.dot_general(q_ref[...], k_ref[...], (((1,), (1,)), ((), ())), preferred_element_type=f32)
            if masked:
                row = lax.broadcasted_iota(jnp.int32, (tq, tk), 0)
                col = lax.broadcasted_iota(jnp.int32, (tq, tk), 1)
                s = jnp.where(col <= row, s, -1e30)
            m_prev = m_sc[...]
            m_new = jnp.maximum(m_prev, jnp.max(s, axis=-1, keepdims=True))
            alpha = jnp.exp(m_prev - m_new)
            p = jnp.exp(s - m_new)
            l_sc[...] = alpha * l_sc[...] + jnp.sum(p, axis=-1, keepdims=True)
            acc_sc[...] = alpha * acc_sc[...] + jnp.dot(p.astype(bf16), v_ref[...], preferred_element_type=f32)
            m_sc[...] = m_new

        pl.when(j < i)(functools.partial(step, False))
        pl.when(j == i)(functools.partial(step, True))

        @pl.when(j == nk - 1)
        def _():
            l = l_sc[...]
            o_ref[...] = acc_sc[...] / l
            lse_ref[...] = jnp.broadcast_to(m_sc[...] + jnp.log(l), (tq, LANES))

    qspec = pl.BlockSpec((tq, HEAD_PAD), lambda h, i, j: (i, h))
    kspec = pl.BlockSpec((tk, HEAD_PAD), lambda h, i, j: (jnp.minimum(j, i), h))
    anyspec = pl.BlockSpec(memory_space=pl.ANY)
    scratch = [pltpu.VMEM((tq, 1), f32), pltpu.VMEM((tq, 1), f32), pltpu.VMEM((tq, HEAD_PAD), f32)]
    if plan:
        scratch += [pltpu.SemaphoreType.DMA((plan['nsem'],)), pltpu.SemaphoreType.DMA((plan['nsem'],))]
    outs = _pc(body, grid=(MLA_HEADS, nq, nk), in_specs=[qspec, kspec, kspec] + [anyspec] * npl,
               out_specs=[qspec, qspec] + [anyspec] * npl,
               out_shape=[jax.ShapeDtypeStruct((S, MLA_PAD), f32)] * 2 + (plan['outs'] if plan else []),
               scratch_shapes=scratch,
               compiler_params=_cp(("arbitrary", "arbitrary", "arbitrary") if plan else ("parallel", "parallel", "arbitrary")),
               name=name)(qh, kh, vh, *(plan['ins'] if plan else []))
    return outs[0], outs[1], outs[2:]


def _attn_bwd(qh, kh, vh, do, lse, delta, plan=None, name="attn_bwd"):
    S = qh.shape[0]
    tq = tk = min(S, ATT_BLK)
    nq, nk = S // tq, S // tk
    npl = plan['n'] if plan else 0

    def body(*refs):
        q_ref, k_ref, v_ref, do_ref, lse_ref, dl_ref = refs[:6]
        dq_ref, dk_ref, dv_ref = refs[6 + npl:9 + npl]
        h, j, i = pl.program_id(0), pl.program_id(1), pl.program_id(2)
        if plan:
            pargs = (refs[6:6 + npl], refs[9 + npl:9 + 2 * npl], refs[9 + 2 * npl], refs[10 + 2 * npl])
            pl.when((h == 0) & (j == 0) & (i == 0))(functools.partial(plan['start'], *pargs))
            pl.when((h == MLA_HEADS - 1) & (j == nk - 1) & (i == nq - 1))(functools.partial(plan['finish'], *pargs))

        @pl.when((j == 0) & (i == 0))
        def _():
            dq_ref[...] = jnp.zeros_like(dq_ref)

        @pl.when(i == 0)
        def _():
            dk_ref[...] = jnp.zeros_like(dk_ref)
            dv_ref[...] = jnp.zeros_like(dv_ref)

        def step(masked):
            nt = (((1,), (1,)), ((), ()))
            tn = (((0,), (0,)), ((), ()))
            qv, kv_, dov = q_ref[...], k_ref[...], do_ref[...]
            s = lax.dot_general(qv, kv_, nt, preferred_element_type=f32)
            p = jnp.exp(s - lse_ref[:, :1])
            if masked:
                row = lax.broadcasted_iota(jnp.int32, (tq, tk), 0)
                col = lax.broadcasted_iota(jnp.int32, (tq, tk), 1)
                p = jnp.where(col <= row, p, 0.0)
            dp = lax.dot_general(dov, v_ref[...], nt, preferred_element_type=f32)
            ds = (p * (dp - dl_ref[:, :1])).astype(bf16)
            dv_ref[...] += lax.dot_general(p.astype(bf16), dov, tn, preferred_element_type=f32)
            dk_ref[...] += lax.dot_general(ds, qv, tn, preferred_element_type=f32)
            rows = pl.ds(pl.multiple_of(i * tq, tq), tq)
            dq_ref[rows, :] += jnp.dot(ds, kv_, preferred_element_type=f32)

        pl.when(i > j)(functools.partial(step, False))
        pl.when(i == j)(functools.partial(step, True))

    qspec = pl.BlockSpec((tq, HEAD_PAD), lambda h, j, i: (jnp.maximum(i, j), h))
    kspec = pl.BlockSpec((tk, HEAD_PAD), lambda h, j, i: (j, h))
    colspec = pl.BlockSpec((S, HEAD_PAD), lambda h, j, i: (0, h))
    anyspec = pl.BlockSpec(memory_space=pl.ANY)
    scratch = [pltpu.SemaphoreType.DMA((plan['nsem'],)), pltpu.SemaphoreType.DMA((plan['nsem'],))] if plan else []
    outs = _pc(body, grid=(MLA_HEADS, nk, nq), in_specs=[qspec, kspec, kspec, qspec, qspec, qspec] + [anyspec] * npl,
               out_specs=[colspec, kspec, kspec] + [anyspec] * npl,
               out_shape=[jax.ShapeDtypeStruct((S, MLA_PAD), f32)] * 3 + (plan['outs'] if plan else []),
               scratch_shapes=scratch,
               compiler_params=_cp(("arbitrary" if plan else "parallel", "arbitrary", "arbitrary")),
               name=name)(qh, kh, vh, do, lse, delta, *(plan['ins'] if plan else []))
    return outs[0], outs[1], outs[2], outs[3:]


def _xattn_fwd(q, kv, name="xattn_fwd"):
    S = q.shape[0]
    M = kv.shape[0]
    tq = _pick(S, (256, 128))
    scale = X_HEAD_DIM ** -0.5

    def body(q_ref, kv_ref, o_ref):
        for h in range(X_HEADS):
            sl = slice(h * X_HEAD_DIM, (h + 1) * X_HEAD_DIM)
            k = kv_ref[:, sl]
            v = kv_ref[:, D_MODEL + h * X_HEAD_DIM:D_MODEL + (h + 1) * X_HEAD_DIM]
            s = lax.dot_general(q_ref[:, sl], k, (((1,), (1,)), ((), ())), preferred_element_type=f32) * scale
            e = jnp.exp(s - jnp.max(s, axis=-1, keepdims=True))
            p = e / jnp.sum(e, axis=-1, keepdims=True)
            o_ref[:, sl] = jnp.dot(p.astype(bf16), v, preferred_element_type=f32).astype(bf16)

    blk = pl.BlockSpec((tq, D_MODEL), lambda i: (i, 0))
    return _pc(body, grid=(S // tq,), in_specs=[blk, pl.BlockSpec((M, 2 * D_MODEL), lambda i: (0, 0))],
               out_specs=blk, out_shape=jax.ShapeDtypeStruct((S, D_MODEL), bf16),
               compiler_params=_cp(("parallel",)), name=name)(q, kv)


def _xattn_bwd(q, kv, do, name="xattn_bwd"):
    S = q.shape[0]
    M = kv.shape[0]
    tq = _pick(S, (256, 128))
    scale = X_HEAD_DIM ** -0.5

    def body(q_ref, kv_ref, do_ref, dq_ref, dkv_ref):
        i = pl.program_id(0)

        @pl.when(i == 0)
        def _():
            dkv_ref[...] = jnp.zeros_like(dkv_ref)

        nt = (((1,), (1,)), ((), ()))
        tn = (((0,), (0,)), ((), ()))
        for h in range(X_HEADS):
            sl = slice(h * X_HEAD_DIM, (h + 1) * X_HEAD_DIM)
            vsl = slice(D_MODEL + h * X_HEAD_DIM, D_MODEL + (h + 1) * X_HEAD_DIM)
            k, v, qv, dov = kv_ref[:, sl], kv_ref[:, vsl], q_ref[:, sl], do_ref[:, sl]
            s = lax.dot_general(qv, k, nt, preferred_element_type=f32) * scale
            e = jnp.exp(s - jnp.max(s, axis=-1, keepdims=True))
            p = e / jnp.sum(e, axis=-1, keepdims=True)
            dp = lax.dot_general(dov, v, nt, preferred_element_type=f32)
            ds = (p * (dp - jnp.sum(dp * p, axis=-1, keepdims=True)) * scale).astype(bf16)
            dq_ref[:, sl] = jnp.dot(ds, k, preferred_element_type=f32).astype(bf16)
            dkv_ref[:, sl] += lax.dot_general(ds, qv, tn, preferred_element_type=f32)
            dkv_ref[:, vsl] += lax.dot_general(p.astype(bf16), dov, tn, preferred_element_type=f32)

    blk = pl.BlockSpec((tq, D_MODEL), lambda i: (i, 0))
    full = pl.BlockSpec((M, 2 * D_MODEL), lambda i: (0, 0))
    return _pc(body, grid=(S // tq,), in_specs=[blk, full, blk], out_specs=[blk, full],
               out_shape=[jax.ShapeDtypeStruct((S, D_MODEL), bf16), jax.ShapeDtypeStruct((M, 2 * D_MODEL), f32)],
               compiler_params=_cp(("arbitrary",)), name=name)(q, kv, do)


ST_CHUNKS = 1


def _apow_init(a_ref, ap_ref, bp_ref, seg):
    P = MACRO_ST
    ar, ai = a_ref[:, :P], a_ref[:, P:]
    pr, pi = ar, ai
    for r in range(seg):
        ap_ref[r:r + 1, :P] = pr
        ap_ref[r:r + 1, P:] = pi
        if r < seg - 1:
            pr, pi = pr * ar - pi * ai, pr * ai + pi * ar
    br, bi = pr, pi
    for k in range(SUBLANES):
        bp_ref[k:k + 1, :P] = pr
        bp_ref[k:k + 1, P:] = pi
        pr, pi = pr * br - pi * bi, pr * bi + pi * br


def _segment_perm(tS):
    seg = tS // SUBLANES
    rows = jnp.arange(tS)
    src = (rows % SUBLANES) * seg + rows // SUBLANES
    return (src[:, None] == jnp.arange(tS)[None, :]).astype(f32)


def _unpermute_rows(pt, v):
    hi = v.astype(bf16)
    r1 = v - hi.astype(f32)
    mid = r1.astype(bf16)
    lo = (r1 - mid.astype(f32)).astype(bf16)
    out = jnp.dot(pt, jnp.concatenate([hi, mid, lo], axis=1), preferred_element_type=f32)
    w = v.shape[1]
    return (out[:, :w] + out[:, w:2 * w]) + out[:, 2 * w:]


def _scan_block(sc_ref, ap_ref, bp_ref, carry_ref, e_ref, seg, reverse):
    P = MACRO_ST
    sgn = -1.0 if reverse else 1.0
    CH = P // ST_CHUNKS
    rid = lax.broadcasted_iota(jnp.int32, (SUBLANES, CH), 0)
    for c in range(ST_CHUNKS):
        lr, li = slice(c * CH, (c + 1) * CH), slice(P + c * CH, P + (c + 1) * CH)
        ar, ai = ap_ref[0:1, lr], sgn * ap_ref[0:1, li]
        xr = xi = None
        for i in range(seg):
            r = seg - 1 - i if reverse else i
            rows = slice(SUBLANES * r, SUBLANES * (r + 1))
            sr, si = sc_ref[rows, lr], sc_ref[rows, li]
            if i == 0:
                xr, xi = sr, si
            else:
                xr, xi = ar * xr - ai * xi + sr, ar * xi + ai * xr + si
                sc_ref[rows, lr] = xr
                sc_ref[rows, li] = xi
        for sh in (1, 2, 4):
            pr, pi = bp_ref[sh - 1:sh, lr], sgn * bp_ref[sh - 1:sh, li]
            if reverse:
                tr = jnp.where(rid < SUBLANES - sh, pltpu.roll(xr, SUBLANES - sh, 0), 0.0)
                ti = jnp.where(rid < SUBLANES - sh, pltpu.roll(xi, SUBLANES - sh, 0), 0.0)
            else:
                tr = jnp.where(rid >= sh, pltpu.roll(xr, sh, 0), 0.0)
                ti = jnp.where(rid >= sh, pltpu.roll(xi, sh, 0), 0.0)
            xr, xi = xr + pr * tr - pi * ti, xi + pr * ti + pi * tr
        if reverse:
            bpr = jnp.zeros((SUBLANES, CH), f32)
            bpi = jnp.zeros((SUBLANES, CH), f32)
            for r in range(SUBLANES):
                bpr = jnp.where(rid == r, bp_ref[SUBLANES - 1 - r:SUBLANES - r, lr], bpr)
                bpi = jnp.where(rid == r, -bp_ref[SUBLANES - 1 - r:SUBLANES - r, li], bpi)
        else:
            bpr, bpi = bp_ref[:, lr], bp_ref[:, li]
        cr, cim = carry_ref[:, lr], carry_ref[:, li]
        xr, xi = xr + bpr * cr - bpi * cim, xi + bpr * cim + bpi * cr
        edge = 0 if reverse else SUBLANES - 1
        carry_ref[:, lr] = jnp.sum(jnp.where(rid == edge, xr, 0.0), axis=0, keepdims=True)
        carry_ref[:, li] = jnp.sum(jnp.where(rid == edge, xi, 0.0), axis=0, keepdims=True)
        if reverse:
            er = jnp.where(rid == SUBLANES - 1, cr, pltpu.roll(xr, SUBLANES - 1, 0))
            ei = jnp.where(rid == SUBLANES - 1, cim, pltpu.roll(xi, SUBLANES - 1, 0))
        else:
            er = jnp.where(rid == 0, cr, pltpu.roll(xr, 1, 0))
            ei = jnp.where(rid == 0, cim, pltpu.roll(xi, 1, 0))
        if e_ref is not None:
            e_ref[:, lr] = er
            e_ref[:, li] = ei
        for i in range(seg):
            r = seg - 1 - i if reverse else i
            rows = slice(SUBLANES * r, SUBLANES * (r + 1))
            pr, pi = ap_ref[i:i + 1, lr], sgn * ap_ref[i:i + 1, li]
            sc_ref[rows, lr] += pr * er - pi * ei
            sc_ref[rows, li] += pr * ei + pi * er


def _ssm_fwd(proj, bm, cm, a, d, name="ssm_fwd"):
    S = proj.shape[0]
    tS = _pick(S, (256, 128))
    nb = S // tS
    P2 = 2 * MACRO_ST
    seg = tS // SUBLANES
    ucol0 = (D_MODEL - SSM_WIDTH) // MACRO_CH

    perm = _segment_perm(tS)

    def body(u_ref, b_ref, c_ref, a_ref, d_ref, pm_ref, pt_ref, y_ref, xc_ref, bu_sc, ap_sc, bp_sc, car_sc):
        t = pl.program_id(1)

        @pl.when(t == 0)
        def _():
            _apow_init(a_ref, ap_sc, bp_sc, seg)
            car_sc[...] = jnp.zeros_like(car_sc)

        uv = u_ref[...]
        up = jnp.dot(pm_ref[...], uv.astype(bf16), preferred_element_type=f32).astype(bf16)
        bu_sc[...] = jnp.dot(up, b_ref[...], preferred_element_type=f32)
        xc_ref[...] = car_sc[...]
        _scan_block(bu_sc, ap_sc, bp_sc, car_sc, None, seg, False)
        yp = jnp.dot(bu_sc[...].astype(bf16), c_ref[...], preferred_element_type=f32)
        y_ref[...] = _unpermute_rows(pt_ref[...], yp) + d_ref[...] * uv

    sq = pl.BlockSpec((tS, tS), lambda m, t: (0, 0))
    return _pc(body, grid=(SSM_MACRO, nb),
               in_specs=[pl.BlockSpec((tS, MACRO_CH), lambda m, t: (t, ucol0 + m)),
                         pl.BlockSpec((None, MACRO_CH, P2), lambda m, t: (m, 0, 0)),
                         pl.BlockSpec((None, P2, MACRO_CH), lambda m, t: (m, 0, 0)),
                         pl.BlockSpec((None, 1, P2), lambda m, t: (m, 0, 0)),
                         pl.BlockSpec((1, MACRO_CH), lambda m, t: (0, m)), sq, sq],
               out_specs=[pl.BlockSpec((tS, MACRO_CH), lambda m, t: (t, m)),
                          pl.BlockSpec((None, None, 1, P2), lambda m, t: (m, t, 0, 0))],
               out_shape=[jax.ShapeDtypeStruct((S, SSM_WIDTH), f32), jax.ShapeDtypeStruct((SSM_MACRO, nb, 1, P2), f32)],
               scratch_shapes=[pltpu.VMEM((tS, P2), f32), pltpu.VMEM((seg, P2), f32),
                               pltpu.VMEM((SUBLANES, P2), f32), pltpu.VMEM((1, P2), f32)],
               compiler_params=_cp(("arbitrary", "arbitrary")), name=name)(
        proj, bm, cm, a, d.reshape(1, SSM_WIDTH), perm.astype(bf16), perm.T.astype(bf16))


def _ssm_bwd(proj, dy, xc, bm, cm, a, d, name="ssm_bwd"):
    S = proj.shape[0]
    tS = _pick(S, (256, 128))
    nb = S // tS
    P = MACRO_ST
    P2 = 2 * P
    seg = tS // SUBLANES
    ucol0 = (D_MODEL - SSM_WIDTH) // MACRO_CH

    perm = _segment_perm(tS)

    def body(u_ref, dy_ref, xc_ref, b_ref, c_ref, a_ref, d_ref, pm_ref, pt_ref, du_ref, db_ref, dc_ref, da_ref, dd_ref,
             x_sc, g_sc, ap_sc, bp_sc, e_sc, xcar_sc, gcar_sc):
        t = pl.program_id(1)

        @pl.when(t == 0)
        def _():
            _apow_init(a_ref, ap_sc, bp_sc, seg)
            gcar_sc[...] = jnp.zeros_like(gcar_sc)
            db_ref[...] = jnp.zeros_like(db_ref)
            dc_ref[...] = jnp.zeros_like(dc_ref)
            da_ref[...] = jnp.zeros_like(da_ref)
            dd_ref[...] = jnp.zeros_like(dd_ref)

        nt = (((1,), (1,)), ((), ()))
        tn = (((0,), (0,)), ((), ()))
        uv = u_ref[...]
        dyv = dy_ref[...]
        pm = pm_ref[...]
        ub = jnp.dot(pm, uv.astype(bf16), preferred_element_type=f32).astype(bf16)
        dyb = jnp.dot(pm, dyv.astype(bf16), preferred_element_type=f32).astype(bf16)
        x_sc[...] = jnp.dot(ub, b_ref[...], preferred_element_type=f32)
        xcar_sc[...] = xc_ref[...]
        _scan_block(x_sc, ap_sc, bp_sc, xcar_sc, e_sc, seg, False)
        g_sc[...] = lax.dot_general(dyb, c_ref[...], nt, preferred_element_type=f32)
        _scan_block(g_sc, ap_sc, bp_sc, gcar_sc, None, seg, True)
        xv = x_sc[...]
        gv = g_sc[...]
        gb = gv.astype(bf16)
        dc_ref[...] += lax.dot_general(xv.astype(bf16), dyb, tn, preferred_element_type=f32)
        db_ref[...] += lax.dot_general(ub, gb, tn, preferred_element_type=f32)
        dup = lax.dot_general(gb, b_ref[...], nt, preferred_element_type=f32)
        du_ref[...] = _unpermute_rows(pt_ref[...], dup) + d_ref[...] * dyv
        dd_ref[...] += jnp.sum(dyv * uv, axis=0, keepdims=True)
        xp = jnp.concatenate([e_sc[...], xv[:tS - SUBLANES]], axis=0)
        xpr, xpi, ggr, ggi = xp[:, :P], xp[:, P:], gv[:, :P], gv[:, P:]
        da_ref[:, :P] += jnp.sum(ggr * xpr + ggi * xpi, axis=0, keepdims=True)
        da_ref[:, P:] += jnp.sum(ggi * xpr - ggr * xpi, axis=0, keepdims=True)

    rev = lambda t: nb - 1 - t
    return _pc(body, grid=(SSM_MACRO, nb),
               in_specs=[pl.BlockSpec((tS, MACRO_CH), lambda m, t: (rev(t), ucol0 + m)),
                         pl.BlockSpec((tS, MACRO_CH), lambda m, t: (rev(t), m)),
                         pl.BlockSpec((None, None, 1, P2), lambda m, t: (m, rev(t), 0, 0)),
                         pl.BlockSpec((None, MACRO_CH, P2), lambda m, t: (m, 0, 0)),
                         pl.BlockSpec((None, P2, MACRO_CH), lambda m, t: (m, 0, 0)),
                         pl.BlockSpec((None, 1, P2), lambda m, t: (m, 0, 0)),
                         pl.BlockSpec((1, MACRO_CH), lambda m, t: (0, m)),
                         pl.BlockSpec((tS, tS), lambda m, t: (0, 0)), pl.BlockSpec((tS, tS), lambda m, t: (0, 0))],
               out_specs=[pl.BlockSpec((tS, MACRO_CH), lambda m, t: (rev(t), m)),
                          pl.BlockSpec((None, MACRO_CH, P2), lambda m, t: (m, 0, 0)),
                          pl.BlockSpec((None, P2, MACRO_CH), lambda m, t: (m, 0, 0)),
                          pl.BlockSpec((None, 1, P2), lambda m, t: (m, 0, 0)),
                          pl.BlockSpec((1, MACRO_CH), lambda m, t: (0, m))],
               out_shape=[jax.ShapeDtypeStruct((S, SSM_WIDTH), f32),
                          jax.ShapeDtypeStruct((SSM_MACRO, MACRO_CH, P2), f32),
                          jax.ShapeDtypeStruct((SSM_MACRO, P2, MACRO_CH), f32),
                          jax.ShapeDtypeStruct((SSM_MACRO, 1, P2), f32),
                          jax.ShapeDtypeStruct((1, SSM_WIDTH), f32)],
               scratch_shapes=[pltpu.VMEM((tS, P2), f32), pltpu.VMEM((tS, P2), f32),
                               pltpu.VMEM((seg, P2), f32), pltpu.VMEM((SUBLANES, P2), f32), pltpu.VMEM((SUBLANES, P2), f32),
                               pltpu.VMEM((1, P2), f32), pltpu.VMEM((1, P2), f32)],
               compiler_params=_cp(("arbitrary", "arbitrary")), name=name)(
        proj, dy, xc, bm, cm, a, d.reshape(1, SSM_WIDTH), perm.astype(bf16), perm.T.astype(bf16))


_GELU_K = math.sqrt(2.0 / math.pi)
_GELU_C = 0.044715


def _glu_fwd(y, w, b, g, name="glu_fwd"):
    S, W = y.shape
    ts = _pick(S, (512, 256, 128))

    def body(y_ref, w_ref, b_ref, g_ref, z_ref, sn_ref, ge_ref):
        yv = y_ref[...]
        cdf = 0.5 * (1.0 + jnp.tanh(_GELU_K * (yv + _GELU_C * (yv * yv * yv))))
        ge = (yv * cdf).astype(bf16)
        z = jnp.dot(ge, w_ref[...], preferred_element_type=f32) + b_ref[...]
        s = yv * jax.nn.sigmoid(z)
        rstd = lax.rsqrt(jnp.sum(s * s, axis=-1, keepdims=True) * (1.0 / W) + EPS)
        z_ref[...] = z
        sn_ref[...] = (s * rstd * g_ref[...]).astype(bf16)
        ge_ref[...] = ge

    blk = pl.BlockSpec((ts, W), lambda i: (i, 0))
    row = pl.BlockSpec((1, W), lambda i: (0, 0))
    return _pc(body, grid=(S // ts,), in_specs=[blk, pl.BlockSpec((W, W), lambda i: (0, 0)), row, row],
               out_specs=[blk, blk, blk],
               out_shape=[jax.ShapeDtypeStruct((S, W), f32), jax.ShapeDtypeStruct((S, W), bf16),
                          jax.ShapeDtypeStruct((S, W), bf16)],
               compiler_params=_cp(("parallel",)), name=name)(y, w, b.reshape(1, W), g.reshape(1, W))


def _glu_bwd(y, z, dmixed, w, g, name="glu_bwd"):
    S, W = y.shape
    ts = _pick(S, (512, 256, 128))
    dcb = MLA_PAD // W

    def body(y_ref, z_ref, dsn_ref, w_ref, g_ref, dy_ref, dz_ref, dg_ref, db_ref):
        i = pl.program_id(0)
        yv, zv, gv = y_ref[...], z_ref[...], g_ref[...]
        sig = jax.nn.sigmoid(zv)
        s = yv * sig
        rstd = lax.rsqrt(jnp.sum(s * s, axis=-1, keepdims=True) * (1.0 / W) + EPS)
        sh = s * rstd
        dsn = dsn_ref[...]
        dsh = dsn * gv
        ds = rstd * (dsh - sh * (jnp.sum(dsh * sh, axis=-1, keepdims=True) * (1.0 / W)))
        dz = ds * s * (1.0 - sig)
        dzb = dz.astype(bf16)
        dge = lax.dot_general(dzb, w_ref[...], (((1,), (1,)), ((), ())), preferred_element_type=f32)
        t = jnp.tanh(_GELU_K * (yv + _GELU_C * (yv * yv * yv)))
        dgelu = 0.5 * (1.0 + t) + 0.5 * yv * (1.0 - t * t) * _GELU_K * (1.0 + 3.0 * _GELU_C * yv * yv)
        dy_ref[...] = ds * sig + dge * dgelu
        dz_ref[...] = dzb

        @pl.when(i == 0)
        def _():
            dg_ref[...] = jnp.zeros_like(dg_ref)
            db_ref[...] = jnp.zeros_like(db_ref)

        dg_ref[...] += jnp.sum(dsn * sh, axis=0, keepdims=True)
        db_ref[...] += jnp.sum(dz, axis=0, keepdims=True)

    blk = pl.BlockSpec((ts, W), lambda i: (i, 0))
    row = pl.BlockSpec((1, W), lambda i: (0, 0))
    return _pc(body, grid=(S // ts,),
               in_specs=[blk, blk, pl.BlockSpec((ts, W), lambda i: (i, dcb)), pl.BlockSpec((W, W), lambda i: (0, 0)), row],
               out_specs=[blk, blk, row, row],
               out_shape=[jax.ShapeDtypeStruct((S, W), f32), jax.ShapeDtypeStruct((S, W), bf16),
                          jax.ShapeDtypeStruct((1, W), f32), jax.ShapeDtypeStruct((1, W), f32)],
               compiler_params=_cp(("arbitrary",)), name=name)(y, z, dmixed, w, g.reshape(1, W))


def _ffn_up(hn, wg, wu, name="ffn_up"):
    S, K = hn.shape
    F = wg.shape[0]
    tm, tn = _pick(S, (512, 256, 128)), _pick(F, (1408, 256, 128))

    def body(h_ref, wg_ref, wu_ref, g_ref, u_ref, a_ref):
        hv = h_ref[...]
        nt = (((1,), (1,)), ((), ()))
        gv = lax.dot_general(hv, wg_ref[...], nt, preferred_element_type=f32)
        uv = lax.dot_general(hv, wu_ref[...], nt, preferred_element_type=f32)
        g_ref[...] = gv
        u_ref[...] = uv
        a_ref[...] = (gv * jax.nn.sigmoid(gv) * uv).astype(bf16)

    wspec = pl.BlockSpec((tn, K), lambda i, j: (j, 0))
    ospec = pl.BlockSpec((tm, tn), lambda i, j: (i, j))
    return _pc(body, grid=(S // tm, F // tn), in_specs=[pl.BlockSpec((tm, K), lambda i, j: (i, 0)), wspec, wspec],
               out_specs=[ospec] * 3,
               out_shape=[jax.ShapeDtypeStruct((S, F), f32), jax.ShapeDtypeStruct((S, F), f32),
                          jax.ShapeDtypeStruct((S, F), bf16)],
               compiler_params=_cp(("parallel", "parallel")), name=name)(hn, wg, wu)


def _ffn_bwd_act(dh, wd, gate, up, name="ffn_bwd_act"):
    S, K = dh.shape
    F = wd.shape[0]
    tm, tn = _pick(S, (512, 256, 128)), _pick(F, (1408, 256, 128))

    def body(dh_ref, wd_ref, g_ref, u_ref, dg_ref, du_ref):
        dact = lax.dot_general(dh_ref[...].astype(bf16), wd_ref[...], (((1,), (1,)), ((), ())),
                               preferred_element_type=f32)
        gv, uv = g_ref[...], u_ref[...]
        sig = jax.nn.sigmoid(gv)
        dg_ref[...] = (dact * uv * (sig * (1.0 + gv * (1.0 - sig)))).astype(bf16)
        du_ref[...] = (dact * (gv * sig)).astype(bf16)

    ospec = pl.BlockSpec((tm, tn), lambda i, j: (i, j))
    return _pc(body, grid=(S // tm, F // tn),
               in_specs=[pl.BlockSpec((tm, K), lambda i, j: (i, 0)), pl.BlockSpec((tn, K), lambda i, j: (j, 0)),
                         ospec, ospec],
               out_specs=[ospec] * 2, out_shape=[jax.ShapeDtypeStruct((S, F), bf16)] * 2,
               compiler_params=_cp(("parallel", "parallel")), name=name)(dh, wd, gate, up)


def _pad_heads(w, per_head, pieces):
    K = w.shape[0]
    w3 = w.reshape(K, MLA_HEADS, per_head)
    out = jnp.zeros((K, MLA_HEADS, HEAD_PAD), w.dtype)
    for s0, s1, d0 in pieces:
        out = out.at[:, :, d0:d0 + (s1 - s0)].set(w3[:, :, s0:s1])
    return out.reshape(K, MLA_PAD)


def _unpad_heads(wp, per_head, pieces):
    K = wp.shape[0]
    w3 = wp.reshape(K, MLA_HEADS, HEAD_PAD)
    out = jnp.zeros((K, MLA_HEADS, per_head), wp.dtype)
    for s0, s1, d0 in pieces:
        out = out.at[:, :, s0:s1].set(w3[:, :, d0:d0 + (s1 - s0)])
    return out.reshape(K, MLA_HEADS * per_head)


_Q_PIECES = [(0, QK_NOPE + QK_ROPE, 0)]
_K_PIECES = [(0, QK_NOPE, 0)]
_V_PIECES = [(QK_NOPE, QK_NOPE + V_HEAD, 0)]
_KR0 = Q_LORA + KV_LORA


def _pack_win(w):
    z = jnp.zeros((w.shape[0], LANES - QK_ROPE), w.dtype)
    return jnp.concatenate([w[:, :_KR0 + QK_ROPE], z, w[:, _KR0 + QK_ROPE:]], axis=1)


def _unpack_win(wp):
    return jnp.concatenate([wp[:, :_KR0 + QK_ROPE], wp[:, _KR0 + LANES:]], axis=1)


def _pack_wout(w):
    wa = w[:MLA_WIDTH].reshape(MLA_HEADS, V_HEAD, D_MODEL)
    wa = jnp.concatenate([wa, jnp.zeros_like(wa)], axis=1).reshape(MLA_PAD, D_MODEL)
    return jnp.concatenate([wa, w[MLA_WIDTH:]], axis=0)


def _unpack_wout(wp):
    wa = wp[:MLA_PAD].reshape(MLA_HEADS, HEAD_PAD, D_MODEL)[:, :V_HEAD].reshape(MLA_WIDTH, D_MODEL)
    return jnp.concatenate([wa, wp[MLA_PAD:]], axis=0)


def _pad_gain(g):
    g2 = g.reshape(MLA_HEADS, V_HEAD)
    return jnp.concatenate([g2, jnp.zeros_like(g2)], axis=1).reshape(MLA_PAD)


def _unpad_gain(gp):
    return gp.reshape(MLA_HEADS, HEAD_PAD)[:, :V_HEAD].reshape(MLA_WIDTH)


def _ssm_prep(lam_re, lam_im, log_dt, b_re, b_im, c_re, c_im):
    lam = lax.complex(lam_re, lam_im)
    dt = jnp.exp(log_dt)[:, None]
    a_bar = jnp.exp(lam * dt)
    b_bar = ((a_bar - 1.0) / lam)[:, None, :] * lax.complex(b_re, b_im)
    G8 = SSM_GROUPS // SSM_MACRO
    eye = jnp.eye(G8, dtype=f32)

    def bmat(part):
        p4 = part.reshape(SSM_MACRO, G8, SSM_GROUP, SSM_STATE)
        return jnp.einsum('mgcp,gh->mgchp', p4, eye).reshape(SSM_MACRO, MACRO_CH, MACRO_ST)

    def cmat(part):
        p4 = part.reshape(SSM_MACRO, G8, SSM_GROUP, SSM_STATE)
        return jnp.einsum('mgcp,gh->mgphc', p4, eye).reshape(SSM_MACRO, MACRO_ST, MACRO_CH)

    bm = jnp.concatenate([bmat(b_bar.real), bmat(b_bar.imag)], axis=2)
    cm = jnp.concatenate([cmat(c_re), -cmat(c_im)], axis=1)
    a4 = a_bar.reshape(SSM_MACRO, 1, MACRO_ST)
    a = jnp.concatenate([a4.real, a4.imag], axis=2)
    return bm, cm, a


def _rope_tables(positions):
    freqs = ROPE_THETA ** (-jnp.arange(0, QK_ROPE, 2, dtype=f32) / QK_ROPE)
    ang = positions.astype(f32)[:, None] * freqs
    cos, sin = jnp.cos(ang), jnp.sin(ang)
    S = positions.shape[0]
    half = QK_ROPE // 2
    one, zero = jnp.ones((S, QK_NOPE), f32), jnp.zeros((S, half), f32)
    z64, z32 = jnp.zeros((S, QK_NOPE), f32), jnp.zeros((S, LANES - QK_NOPE - QK_ROPE), f32)
    tc = jnp.concatenate([one, cos, cos, z32], axis=1)
    s1 = jnp.concatenate([z64, -sin, zero, z32], axis=1)
    s2 = jnp.concatenate([z64, zero, sin, z32], axis=1)
    return tc, s1, s2


def _layer_params(W, l):
    p = {}
    p['win'] = _pack_win(W['w_in'][l])
    p['wuq'] = _pad_heads(W['w_uq'][l], QK_NOPE + QK_ROPE, _Q_PIECES)
    wukv = W['w_ukv'][l]
    p['wukv'] = jnp.concatenate([_pad_heads(wukv, QK_NOPE + V_HEAD, _K_PIECES),
                                 _pad_heads(wukv, QK_NOPE + V_HEAD, _V_PIECES)], axis=1)
    p['attn_g'] = _pad_gain(W['attn_out_g'][l])
    return p


def _forward_layer(h, memn_in, tabs, W, l, name, gather=None):
    p = _layer_params(W, l)
    sv = {'h0': h, 'p': p}
    xn = _rms_fwd(h, W['norm_mix_g'][l], name=name + "rms_mix")
    proj = _mm([(xn, p['win'])], 'nn', f32, name=name + "mm_in")
    cqn = _rms_fwd(proj, W['q_norm_g'][l], col0=0, width=Q_LORA, name=name + "rms_q")
    ckvn = _rms_fwd(proj, W['kv_norm_g'][l], col0=Q_LORA, width=KV_LORA, name=name + "rms_kv")
    q = _mm([(cqn, p['wuq'])], 'nn', f32, name=name + "mm_uq")
    kv = _mm([(ckvn, p['wukv'])], 'nn', f32, name=name + "mm_ukv")
    qh, kh, vh = _rope_fwd(q, kv, proj, tabs, name=name + "rope")
    oh, lse, carried = _attn_fwd(qh, kh, vh, plan=gather[0] if gather else None, name=name + "attn")
    if gather:
        for n, t in gather[1](carried).items():
            W[n][l] = t
    p['wout'] = _pack_wout(W['w_out'][l])
    an = _rms_fwd(oh, p['attn_g'], n_valid=MLA_WIDTH, name=name + "rms_attn")
    bm, cm, a = W['ssm'][l]
    bmb, cmb = bm.astype(bf16), cm.astype(bf16)
    y, xc = _ssm_fwd(proj, bmb, cmb, a, W['ssm_d'][l], name=name + "ssm")
    z, sn, ge = _glu_fwd(y, W['ssm_w_glu'][l], W['ssm_b_glu'][l], W['ssm_out_g'][l], name=name + "glu")
    h1a = _mm([(an, p['wout'][:MLA_PAD])], 'nn', f32, res=h, name=name + "mm_out_a")
    h1 = _mm([(sn, p['wout'][MLA_PAD:])], 'nn', f32, res=h1a, name=name + "mm_out_s")
    hn2 = _rms_fwd(h1, W['norm_x_g'][l], name=name + "rms_x")
    memn = _rms_fwd(memn_in, W['mem_norm_g'][l], name=name + "rms_mem")
    qx = _mm([(hn2, W['w_xq'][l])], 'nn', bf16, name=name + "mm_xq")
    kvx = _mm([(memn, W['w_xkv'][l])], 'nn', bf16, name=name + "mm_xkv")
    ox = _xattn_fwd(qx, kvx, name=name + "xattn")
    h2 = _mm([(ox, W['w_xo'][l])], 'nn', f32, res=h1, name=name + "mm_xo")
    hn3 = _rms_fwd(h2, W['norm_ffn_g'][l], name=name + "rms_ffn")
    gate, up, act = _ffn_up(hn3, W['w_gate'][l], W['w_up'][l], name=name + "ffn_up")
    h3 = _mm([(act, W['w_down'][l])], 'nn', f32, res=h2, name=name + "mm_down")
    sv.update(xn=xn, proj=proj, cqn=cqn, ckvn=ckvn, qh=qh, kh=kh, vh=vh, oh=oh, lse=lse, an=an, bmb=bmb, cmb=cmb,
              a=a, y=y, xc=xc, z=z, sn=sn, ge=ge, h1=h1, hn2=hn2, memn=memn, qx=qx, kvx=kvx, ox=ox, h2=h2, hn3=hn3,
              gate=gate, up=up, act=act)
    return h3, sv


def _backward_layer(dh3, sv, memn_in, tabs, W, l, plans, hook, name):
    p = sv['p']
    G = {}
    G['w_down'] = _mm([(sv['act'], dh3)], 'tn', f32, name=name + "dw_down")
    dgate, dup = _ffn_bwd_act(dh3, W['w_down'][l], sv['gate'], sv['up'], name=name + "ffn_bwd_act")
    dhn3 = _mm([(dgate, W['w_gate'][l]), (dup, W['w_up'][l])], 'nn', f32, name=name + "mm_dffn")
    G['w_gate'] = _mm([(dgate, sv['hn3'])], 'tn', f32, name=name + "dw_gate")
    G['w_up'] = _mm([(dup, sv['hn3'])], 'tn', f32, name=name + "dw_up")
    plans = list(plans) + ([hook['ffn'](l, G)] if hook else [])
    dh2, dg = _rms_bwd(sv['h2'], W['norm_ffn_g'][l], dhn3, res=dh3, name=name + "rmsb_ffn")
    G['norm_ffn_g'] = dg[0]
    G['w_xo'] = _mm([(sv['ox'], dh2)], 'tn', f32, name=name + "dw_xo")
    dox = _mm([(dh2, W['w_xo'][l])], 'nt', bf16, name=name + "mm_dxo")
    dqx, dkvx = _xattn_bwd(sv['qx'], sv['kvx'], dox, name=name + "xattn_bwd")
    G['w_xq'] = _mm([(sv['hn2'], dqx)], 'tn', f32, name=name + "dw_xq")
    G['w_xkv'] = _mm([(sv['memn'], dkvx)], 'tn', f32, name=name + "dw_xkv")
    dhn2 = _mm([(dqx, W['w_xq'][l])], 'nt', f32, name=name + "mm_dxq")
    dmemn = _mm([(dkvx, W['w_xkv'][l])], 'nt', f32, name=name + "mm_dxkv")
    dh1, dg = _rms_bwd(sv['h1'], W['norm_x_g'][l], dhn2, res=dh2, name=name + "rmsb_x")
    G['norm_x_g'] = dg[0]
    _, dg = _rms_bwd(memn_in, W['mem_norm_g'][l], dmemn, name=name + "rmsb_mem")
    G['mem_norm_g'] = dg[0]
    dwo_a = _mm([(sv['an'], dh1)], 'tn', f32, name=name + "dw_out_a")
    dwo_s = _mm([(sv['sn'], dh1)], 'tn', f32, name=name + "dw_out_s")
    G['w_out'] = _unpack_wout(jnp.concatenate([dwo_a, dwo_s], axis=0))
    dmixed = _mm([(dh1, p['wout'])], 'nt', f32, name=name + "mm_dout")
    dy, dz, dg, db = _glu_bwd(sv['y'], sv['z'], dmixed, W['ssm_w_glu'][l], W['ssm_out_g'][l], name=name + "glu_bwd")
    G['ssm_out_g'], G['ssm_b_glu'] = dg[0], db[0]
    G['ssm_w_glu'] = _mm([(sv['ge'], dz)], 'tn', f32, name=name + "dw_glu")
    du, dbm, dcm, da, dd = _ssm_bwd(sv['proj'], dy, sv['xc'], sv['bmb'], sv['cmb'], sv['a'], W['ssm_d'][l],
                                    name=name + "ssm_bwd")
    G['ssm_d'] = dd[0]
    G['ssm_raw'] = (dbm, dcm, da)
    doh, dg, delta = _rms_bwd(sv['oh'], p['attn_g'], dmixed, width=MLA_PAD, n_valid=MLA_WIDTH, delta=True,
                              out_dtype=bf16, name=name + "rmsb_attn")
    G['attn_out_g'] = _unpad_gain(dg[0])
    plan = _merge_plans(plans) if plans else None
    dqh, dkh, dvh, carried = _attn_bwd(sv['qh'], sv['kh'], sv['vh'], doh, sv['lse'], delta, plan=plan,
                                       name=name + "attn_bwd")
    if plan:
        plan['done'](carried)
    dq, dkv, dkr = _rope_bwd(dqh, dkh, dvh, tabs, name=name + "rope_bwd")
    G['w_uq'] = _unpad_heads(_mm([(sv['cqn'], dq)], 'tn', f32, name=name + "dw_uq"), QK_NOPE + QK_ROPE, _Q_PIECES)
    dwukv = _mm([(sv['ckvn'], dkv)], 'tn', f32, name=name + "dw_ukv")
    G['w_ukv'] = (_unpad_heads(dwukv[:, :MLA_PAD], QK_NOPE + V_HEAD, _K_PIECES)
                  + _unpad_heads(dwukv[:, MLA_PAD:], QK_NOPE + V_HEAD, _V_PIECES))
    dcqn = _mm([(dq, p['wuq'])], 'nt', f32, name=name + "mm_duq")
    dckvn = _mm([(dkv, p['wukv'])], 'nt', f32, name=name + "mm_dukv")
    dcq, dg = _rms_bwd(sv['proj'], W['q_norm_g'][l], dcqn, col0=0, width=Q_LORA, out_dtype=bf16, name=name + "rmsb_q")
    G['q_norm_g'] = dg[0]
    dckv, dg = _rms_bwd(sv['proj'], W['kv_norm_g'][l], dckvn, col0=Q_LORA, width=KV_LORA, out_dtype=bf16,
                        name=name + "rmsb_kv")
    G['kv_norm_g'] = dg[0]
    dproj = jnp.concatenate([dcq, dckv, dkr, du.astype(bf16)], axis=1)
    G['w_in'] = _unpack_win(_mm([(sv['xn'], dproj)], 'tn', f32, name=name + "dw_in"))
    dxn = _mm([(dproj, p['win'])], 'nt', f32, name=name + "mm_din")
    dh0, dg = _rms_bwd(sv['h0'], W['norm_mix_g'][l], dxn, res=dh1, name=name + "rmsb_mix")
    G['norm_mix_g'] = dg[0]
    return dh0, G


def _local_step(x, mem, positions, target, W, gathers=None, hook=None):
    tabs = _rope_tables(positions)
    ssm_in = [(W['ssm_lambda_re'][l], W['ssm_lambda_im'][l], W['ssm_log_dt'][l], W['ssm_b_re'][l], W['ssm_b_im'][l],
               W['ssm_c_re'][l], W['ssm_c_im'][l]) for l in range(DEPTH)]
    preps = [jax.vjp(_ssm_prep, *ssm_in[l]) for l in range(DEPTH)]
    W = dict(W)
    W['ssm'] = [preps[l][0] for l in range(DEPTH)]
    h = x
    saved = []
    for l in range(DEPTH):
        h, sv = _forward_layer(h, mem, tabs, W, l, f"l{l}_", gathers[l] if gathers else None)
        saved.append(sv)
    dh, dgf, loss = _loss_head(h, W['final_norm_g'], target)
    grads = [None] * DEPTH
    pending = []
    for l in reversed(range(DEPTH)):
        dh, G = _backward_layer(dh, saved[l], mem, tabs, W, l, pending, hook, f"l{l}b_")
        dbm, dcm, da = G.pop('ssm_raw')
        names = ['ssm_lambda_re', 'ssm_lambda_im', 'ssm_log_dt', 'ssm_b_re', 'ssm_b_im', 'ssm_c_re', 'ssm_c_im']
        for n, g in zip(names, preps[l][1]((dbm, dcm, da))):
            G[n] = g
        grads[l] = G
        pending = [hook['rest'](l, G)] if (hook is not None and l > 0) else []
    out = {n: [grads[l][n] for l in range(DEPTH)] if n in SHARDED else jnp.stack([grads[l][n] for l in range(DEPTH)])
           for n in grads[0]}
    out['final_norm_g'] = dgf[0]
    return loss[0, 0], dh, out


_HBM = pl.BlockSpec(memory_space=pltpu.HBM)


def _me():
    return lax.axis_index("x"), lax.axis_index("y"), lax.axis_index("c")


def _chip_peers(x, y, c):
    devs = [(1 - x, y, c), (x, 1 - y, c), (1 - x, 1 - y, c)]
    return devs, [2 * d[0] + d[1] for d in devs]


def _gather_plan(xs, half_first):
    n = len(xs)
    if half_first:
        ins = [t.reshape(2, 1, *t.shape[1:]) for t in xs]
        outs = [jax.ShapeDtypeStruct((2, 4, *t.shape[1:]), t.dtype) for t in xs]
    else:
        ins = [t.reshape(1, 2, t.shape[0] // 2, t.shape[1]) for t in xs]
        outs = [jax.ShapeDtypeStruct((4, 2, t.shape[0] // 2, t.shape[1]), t.dtype) for t in xs]

    def own(ref, h):
        return ref.at[h] if half_first else ref.at[:, h]

    def slot(ref, h, j):
        return ref.at[h, pl.ds(j, 1)] if half_first else ref.at[pl.ds(j, 1), h]

    def copies(src, dst, send, recv):
        x, y, c = _me()
        jme = 2 * x + y
        devs, js = _chip_peers(x, y, c)
        half, other = pl.ds(c, 1), pl.ds(1 - c, 1)
        mk = pltpu.make_async_remote_copy
        for i in range(n):
            for k in range(3):
                out_cp = mk(own(src[i], half), slot(dst[i], half, jme), send.at[6 * i + k], recv.at[6 * i + k],
                            device_id=devs[k], device_id_type=MESH)
                in_cp = mk(own(src[i], half), slot(dst[i], half, js[k]), send.at[6 * i + k], recv.at[6 * i + k],
                           device_id=devs[k], device_id_type=MESH)
                pass_cp = mk(slot(dst[i], half, js[k]), slot(dst[i], half, js[k]), send.at[6 * i + 3 + k],
                             recv.at[6 * i + 3 + k], device_id=(x, y, 1 - c), device_id_type=MESH)
                got_cp = mk(slot(dst[i], other, js[k]), slot(dst[i], other, js[k]), send.at[6 * i + 3 + k],
                            recv.at[6 * i + 3 + k], device_id=(x, y, 1 - c), device_id_type=MESH)
                yield out_cp, in_cp, pass_cp, got_cp

    def start(*refs):
        for out_cp, _, _, _ in copies(*refs):
            out_cp.start()

    def forward(*refs):
        for _, in_cp, pass_cp, _ in copies(*refs):
            in_cp.wait_recv()
            pass_cp.start()

    def finish(*refs):
        for out_cp, _, pass_cp, got_cp in copies(*refs):
            got_cp.wait_recv()
            out_cp.wait_send()
            pass_cp.wait_send()

    return dict(n=n, ins=ins, outs=outs, nsem=6 * n, start=start, forward=forward, finish=finish)


def _plan_refs(plan, refs):
    n = plan['n']
    return refs[:n], refs[n:2 * n], refs[2 * n], refs[2 * n + 1]


def _run_plan(plan, name):
    n = plan['n']

    def body(*refs):
        args = _plan_refs(plan, refs)
        plan['start'](*args)
        plan['forward'](*args)
        plan['finish'](*args)

    return _pc(body, in_specs=[_HBM] * n, out_specs=[_HBM] * n, out_shape=plan['outs'],
               scratch_shapes=[pltpu.SemaphoreType.DMA((plan['nsem'],)), pltpu.SemaphoreType.DMA((plan['nsem'],))],
               compiler_params=pltpu.CompilerParams(has_side_effects=True), name=name)(*plan['ins'])


def _fill_own(gathered, own, half_first):
    jme = (2 * lax.axis_index("x") + lax.axis_index("y")).astype(jnp.int32)
    zero = jnp.int32(0)
    if half_first:
        return lax.dynamic_update_slice(gathered, own[:, None], (zero, jme, zero, zero))
    return lax.dynamic_update_slice(gathered, own.reshape(1, *gathered.shape[1:]), (jme, zero, zero, zero))


def _exchange_halves(gs, name):
    n = len(gs)

    def body(*refs):
        src, dst = refs[:n], refs[n:2 * n]
        send, recv = refs[2 * n:]
        x, y, c = _me()
        cps = []
        for i in range(n):
            cp = pltpu.make_async_remote_copy(src[i].at[:, pl.ds(1 - c, 1)], dst[i], send.at[i], recv.at[i],
                                              device_id=(x, y, 1 - c), device_id_type=MESH)
            cp.start()
            cps.append(cp)
        for cp in cps:
            cp.wait()

    outs = [jax.ShapeDtypeStruct((4, 1, *g.shape[2:]), g.dtype) for g in gs]
    return _pc(body, in_specs=[_HBM] * n, out_specs=[_HBM] * n, out_shape=outs,
               scratch_shapes=[pltpu.SemaphoreType.DMA((n,)), pltpu.SemaphoreType.DMA((n,))],
               compiler_params=pltpu.CompilerParams(has_side_effects=True), name=name)(*gs)


def _scatter_plan(ps, done=None):
    n = len(ps)

    def copies(src, dst, send, recv, off):
        x, y, c = _me()
        devs, js = _chip_peers(x, y, c)
        for i in range(n):
            for k in range(3):
                yield pltpu.make_async_remote_copy(src[i].at[pl.ds(js[k], 1)], dst[i].at[k], send.at[off + 3 * i + k],
                                                   recv.at[off + 3 * i + k], device_id=devs[k], device_id_type=MESH)

    def start(src, dst, send, recv, off=0):
        for cp in copies(src, dst, send, recv, off):
            cp.start()

    def finish(src, dst, send, recv, off=0):
        for cp in copies(src, dst, send, recv, off):
            cp.wait()

    outs = [jax.ShapeDtypeStruct((3, 1, *p.shape[1:]), p.dtype) for p in ps]
    return dict(n=n, ins=list(ps), outs=outs, nsem=3 * n, start=start, forward=lambda *refs: None, finish=finish,
                done=done)


def _merge_plans(plans):
    def run(which):
        def f(src, dst, send, recv):
            o = s = 0
            for p in plans:
                p[which](src[o:o + p['n']], dst[o:o + p['n']], send, recv, off=s)
                o, s = o + p['n'], s + p['nsem']
        return f

    def done(results):
        o = 0
        for p in plans:
            p['done'](results[o:o + p['n']])
            o += p['n']

    return dict(n=sum(p['n'] for p in plans), ins=[t for p in plans for t in p['ins']],
                outs=[t for p in plans for t in p['outs']], nsem=sum(p['nsem'] for p in plans),
                start=run('start'), forward=lambda *refs: None, finish=run('finish'), done=done)


def _swap_sibling(hs, name):
    n = len(hs)

    def body(*refs):
        src, dst = refs[:n], refs[n:2 * n]
        send, recv = refs[2 * n:]
        x, y, c = _me()
        cps = []
        for i in range(n):
            cp = pltpu.make_async_remote_copy(src[i], dst[i], send.at[i], recv.at[i], device_id=(x, y, 1 - c),
                                              device_id_type=MESH)
            cp.start()
            cps.append(cp)
        for cp in cps:
            cp.wait()

    outs = [jax.ShapeDtypeStruct(h.shape, h.dtype) for h in hs]
    return _pc(body, in_specs=[_HBM] * n, out_specs=[_HBM] * n, out_shape=outs,
               scratch_shapes=[pltpu.SemaphoreType.DMA((n,)), pltpu.SemaphoreType.DMA((n,))],
               compiler_params=pltpu.CompilerParams(has_side_effects=True), name=name)(*hs)


ELEMWISE_VMEM_BUDGET = 24 * 1024 * 1024


def _row_tile(r, n, narrays):
    limit = ELEMWISE_VMEM_BUDGET // (2 * 4 * narrays * n)
    best = SUBLANES
    for t in range(16, r + 1, 16):
        if r % t == 0 and t <= limit:
            best = t
    return best


def _add_half(g, r1, cidx, name):
    _, _, r, n = g.shape
    tr = _row_tile(r, n, 3)

    def body(c_ref, g_ref, r_ref, o_ref):
        o_ref[...] = (g_ref[...] + r_ref[...]).astype(GRAD_TRANSIT)

    blk = lambda f: pl.BlockSpec((None, None, tr, n), f)
    gs = pltpu.PrefetchScalarGridSpec(
        num_scalar_prefetch=1, grid=(4, r // tr),
        in_specs=[blk(lambda j, i, c: (j, c[0], i, 0)), blk(lambda j, i, c: (j, 0, i, 0))],
        out_specs=pl.BlockSpec((None, tr, n), lambda j, i, c: (j, i, 0)))
    return _pc(body, grid_spec=gs, out_shape=jax.ShapeDtypeStruct((4, r, n), GRAD_TRANSIT),
               compiler_params=_cp(("parallel", "parallel")), name=name)(cidx, g, r1)


def _add_chips(p, r3, jidx, name):
    _, r, n = p.shape
    tr = _row_tile(r, n, 5)

    def body(j_ref, p_ref, a_ref, b_ref, c_ref, o_ref):
        o_ref[...] = ((p_ref[...].astype(f32) + a_ref[...].astype(f32)) + b_ref[...].astype(f32)) + c_ref[...].astype(f32)

    rblk = lambda k: pl.BlockSpec((None, None, tr, n), lambda i, j: (k, 0, i, 0))
    gs = pltpu.PrefetchScalarGridSpec(
        num_scalar_prefetch=1, grid=(r // tr,),
        in_specs=[pl.BlockSpec((None, tr, n), lambda i, j: (j[0], i, 0)), rblk(0), rblk(1), rblk(2)],
        out_specs=pl.BlockSpec((tr, n), lambda i, j: (i, 0)))
    return _pc(body, grid_spec=gs, out_shape=jax.ShapeDtypeStruct((r, n), f32),
               compiler_params=_cp(("parallel",)), name=name)(jidx, p, r3, r3, r3)


def _adamw_halves(w, mine, theirs, m, v, cidx, name):
    L, r, n = w.shape
    r2 = r // 2
    tr = _row_tile(r2, n, 11)
    c1 = 1.0 / (1.0 - ADAM_B1 ** ADAM_STEP)
    c2 = 1.0 / (1.0 - ADAM_B2 ** ADAM_STEP)

    def body(c_ref, w_ref, a0_ref, b0_ref, a1_ref, b1_ref, m_ref, v_ref, g_ref, d_ref, mo_ref, vo_ref):
        l, hf = pl.program_id(0), pl.program_id(1)
        own = hf == c_ref[0]
        gv = jnp.where(l == 0, jnp.where(own, a0_ref[...], b0_ref[...]), jnp.where(own, a1_ref[...], b1_ref[...]))
        m2 = ADAM_B1 * m_ref[...] + (1.0 - ADAM_B1) * gv
        v2 = ADAM_B2 * v_ref[...] + (1.0 - ADAM_B2) * (gv * gv)
        g_ref[...] = gv
        d_ref[...] = -ADAM_LR * ((m2 * c1) / (jnp.sqrt(v2 * c2) + ADAM_EPS) + ADAM_WD * w_ref[...])
        mo_ref[...] = m2
        vo_ref[...] = v2

    full = pl.BlockSpec((None, None, tr, n), lambda l, hf, i, c: (l, hf, i, 0))
    half = pl.BlockSpec((tr, n), lambda l, hf, i, c: (i, 0))
    gs = pltpu.PrefetchScalarGridSpec(num_scalar_prefetch=1, grid=(L, 2, r2 // tr),
                                      in_specs=[full, half, half, half, half, full, full], out_specs=[full] * 4)
    four = lambda t: t.reshape(L, 2, r2, n)
    outs = _pc(body, grid_spec=gs, out_shape=[jax.ShapeDtypeStruct((L, 2, r2, n), f32)] * 4,
               compiler_params=_cp(("parallel", "parallel", "parallel")), name=name)(
        cidx, four(w), mine[0], theirs[0], mine[1], theirs[1], four(m), four(v))
    return [t.reshape(w.shape) for t in outs]


def _adamw_whole(w, g, m, v, name):
    c1 = 1.0 / (1.0 - ADAM_B1 ** ADAM_STEP)
    c2 = 1.0 / (1.0 - ADAM_B2 ** ADAM_STEP)

    def body(w_ref, g_ref, m_ref, v_ref, d_ref, mo_ref, vo_ref):
        gv = g_ref[...]
        m2 = ADAM_B1 * m_ref[...] + (1.0 - ADAM_B1) * gv
        v2 = ADAM_B2 * v_ref[...] + (1.0 - ADAM_B2) * (gv * gv)
        d_ref[...] = -ADAM_LR * ((m2 * c1) / (jnp.sqrt(v2 * c2) + ADAM_EPS) + ADAM_WD * w_ref[...])
        mo_ref[...] = m2
        vo_ref[...] = v2

    return _pc(body, out_shape=[jax.ShapeDtypeStruct(w.shape, f32)] * 3, name=name)(w, g, m, v)


def _full_from_gathered(name, t):
    r, n = 2 * t.shape[2], t.shape[3]
    if SHARDED[name] == 1 or name in TRANSPOSED:
        return t.reshape(4 * r, n)
    return t.reshape(4, r, n).transpose(1, 0, 2).reshape(r, 4 * n)


def _shard_major(name, g):
    R, C = g.shape
    if SHARDED[name] == 1 or name in TRANSPOSED:
        return g.reshape(4, 2, R // 8, C)
    return g.reshape(R, 4, C // 4).transpose(1, 0, 2).reshape(4, 2, R // 2, C // 4)


_SMALL_ROWS = 288


def _pack_small(d):
    flat = jnp.concatenate([d[n].reshape(-1) for n in SMALL])
    total = 2 * 4 * _SMALL_ROWS * LANES
    flat = jnp.concatenate([flat, jnp.zeros((total - flat.shape[0],), f32)])
    return flat.reshape(4, 2, _SMALL_ROWS, LANES)


def _unpack_small(t, like):
    flat = t.reshape(-1)
    out, off = {}, 0
    for n in SMALL:
        sz = math.prod(like[n].shape)
        out[n] = flat[off:off + sz].reshape(like[n].shape)
        off += sz
    return out


def kernel(x, mem, positions, norm_mix_g, w_in, q_norm_g, w_uq, kv_norm_g, w_ukv, ssm_lambda_re, ssm_lambda_im, ssm_log_dt, ssm_b_re, ssm_b_im, ssm_c_re, ssm_c_im, ssm_d, ssm_w_glu, ssm_b_glu, attn_out_g, ssm_out_g, w_out, norm_x_g, mem_norm_g, w_xq, w_xkv, w_xo, norm_ffn_g, w_gate, w_up, w_down, final_norm_g, loss_target, m_norm_mix_g, m_w_in, m_q_norm_g, m_w_uq, m_kv_norm_g, m_w_ukv, m_ssm_lambda_re, m_ssm_lambda_im, m_ssm_log_dt, m_ssm_b_re, m_ssm_b_im, m_ssm_c_re, m_ssm_c_im, m_ssm_d, m_ssm_w_glu, m_ssm_b_glu, m_attn_out_g, m_ssm_out_g, m_w_out, m_norm_x_g, m_mem_norm_g, m_w_xq, m_w_xkv, m_w_xo, m_norm_ffn_g, m_w_gate, m_w_up, m_w_down, m_final_norm_g, v_norm_mix_g, v_w_in, v_q_norm_g, v_w_uq, v_kv_norm_g, v_w_ukv, v_ssm_lambda_re, v_ssm_lambda_im, v_ssm_log_dt, v_ssm_b_re, v_ssm_b_im, v_ssm_c_re, v_ssm_c_im, v_ssm_d, v_ssm_w_glu, v_ssm_b_glu, v_attn_out_g, v_ssm_out_g, v_w_out, v_norm_x_g, v_mem_norm_g, v_w_xq, v_w_xkv, v_w_xo, v_norm_ffn_g, v_w_gate, v_w_up, v_w_down, v_final_norm_g):
    given = dict(locals())
    swap = lambda n, t: jnp.swapaxes(t, *TRANSPOSED[n]) if n in TRANSPOSED else t
    w = {n: swap(n, given[n]) for n in WEIGHTS}
    m = {n: swap(n, given["m_" + n]) for n in WEIGHTS}
    v = {n: swap(n, given["v_" + n]) for n in WEIGHTS}
    big = list(SHARDED)

    shards = {n: w[n].astype(bf16) for n in big}
    early = [n for n in big if n in EARLY_WEIGHTS]
    rest = [n for n in big if n not in EARLY_WEIGHTS]

    def full(names, results, l):
        return {n: _full_from_gathered(n, _fill_own(t, shards[n][l], False)) for n, t in zip(names, results)}

    first = _run_plan(_gather_plan([shards[n][l] for l in range(DEPTH) for n in early], False), "allgather_weights_early")
    W = {n: [None] * DEPTH for n in big}
    for l in range(DEPTH):
        for n, t in full(early, first[l * len(early):(l + 1) * len(early)], l).items():
            W[n][l] = t
    gathers = [(_gather_plan([shards[n][l] for n in rest], False), functools.partial(full, rest, l=l))
               for l in range(DEPTH)]
    W.update({n: w[n] for n in SMALL})

    cidx = lax.axis_index("c").astype(jnp.int32).reshape(1)
    jidx = (2 * lax.axis_index("x") + lax.axis_index("y")).astype(jnp.int32).reshape(1)
    ffn = [n for n in big if n in FFN_WEIGHTS]
    others = [n for n in big if n not in FFN_WEIGHTS]
    sums, got = {}, {}

    def launch(names, tag, l, G, extra=()):
        keys = [(l, n) for n in names] + [(l, n) for n, _ in extra]
        gs = [_shard_major(n, G[n]) for n in names] + [g for _, g in extra]
        r1 = _exchange_halves(gs, f"grad_exchange_halves_{tag}")
        ps = [_add_half(g, r, cidx, f"grad_add_half_{tag}_{k[1]}") for k, g, r in zip(keys, gs, r1)]
        sums.update(zip(keys, ps))
        return _scatter_plan(ps, done=lambda results: got.update(zip(keys, results)))

    hook = {'ffn': lambda l, G: launch(ffn, f"l{l}_ffn", l, G), 'rest': lambda l, G: launch(others, f"l{l}_rest", l, G)}
    loss, dx, grads = _local_step(x[0], mem[0], positions[0], loss_target[0], W, gathers=gathers, hook=hook)
    loss = lax.psum(loss, ("x", "y", "c"))

    last = launch(others, "l0_rest", 0, {n: grads[n][0] for n in others}, extra=[("small", _pack_small(grads))])
    last['done'](_run_plan(last, "grad_scatter_chips_l0_rest"))
    keys = list(sums)
    hs = dict(zip(keys, [_add_chips(sums[k], got[k], jidx, f"grad_add_chips_l{k[0]}_{k[1]}") for k in keys]))
    ts = dict(zip(keys, _swap_sibling([hs[k] for k in keys], "grad_swap_sibling")))

    out_g, out_d, out_m, out_v = {}, {}, {}, {}
    for n in big:
        mine, theirs = [hs[(l, n)] for l in range(DEPTH)], [ts[(l, n)] for l in range(DEPTH)]
        out_g[n], out_d[n], out_m[n], out_v[n] = _adamw_halves(w[n], mine, theirs, m[n], v[n], cidx, f"adamw_{n}")
    both = jnp.stack([hs[(0, "small")], ts[(0, "small")]])
    piece = jnp.where(cidx[0] == 0, both, both[::-1]).reshape(2 * _SMALL_ROWS, LANES)
    gsm = _fill_own(_run_plan(_gather_plan([piece], False), "allgather_small")[0], piece, False)
    out_g.update(_unpack_small(gsm, w))
    for n in SMALL:
        two = lambda t: t.reshape(1, -1) if t.ndim == 1 else t
        d_, m_, v_ = _adamw_whole(two(w[n]), two(out_g[n]), two(m[n]), two(v[n]), f"adamw_{n}")
        out_d[n], out_m[n], out_v[n] = (t.reshape(w[n].shape) for t in (d_, m_, v_))

    outs = [[swap(n, d[n]) for n in WEIGHTS] for d in (out_g, out_d, out_m, out_v)]
    return (loss, dx.reshape(x.shape), *outs[0], *outs[1], *outs[2], *outs[3])
```

```python
import functools
import math

import jax
import jax.numpy as jnp
from jax import lax
from jax.experimental import pallas as pl
from jax.experimental.pallas import tpu as pltpu

f32, bf16 = jnp.float32, jnp.bfloat16

D_MODEL = 1024
DEPTH = 2
MLA_HEADS = 8
QK_NOPE = 64
QK_ROPE = 32
V_HEAD = 64
Q_LORA = 256
KV_LORA = 128
MLA_WIDTH = MLA_HEADS * V_HEAD
ROPE_THETA = 10000.0
SSM_WIDTH = 512
SSM_GROUP = 16
SSM_GROUPS = 32
SSM_STATE = 64
IN_WIDTH = Q_LORA + KV_LORA + QK_ROPE + SSM_WIDTH
X_HEADS = 4
X_HEAD_DIM = D_MODEL // X_HEADS
D_FF = 2816
EPS = 1e-6
ADAM_LR, ADAM_B1, ADAM_B2, ADAM_EPS, ADAM_WD, ADAM_STEP = 0.001, 0.9, 0.999, 1e-08, 0.01, 10

LANES = 128
SUBLANES = 8
HEAD_PAD = 128
MLA_PAD = MLA_HEADS * HEAD_PAD
SSM_MACRO = 4
MACRO_CH = SSM_WIDTH // SSM_MACRO
MACRO_ST = SSM_GROUPS // SSM_MACRO * SSM_STATE
VMEM_LIMIT = 56 * 1024 * 1024
GRAD_TRANSIT = bf16

WEIGHTS = ['norm_mix_g', 'w_in', 'q_norm_g', 'w_uq', 'kv_norm_g', 'w_ukv', 'ssm_lambda_re', 'ssm_lambda_im',
           'ssm_log_dt', 'ssm_b_re', 'ssm_b_im', 'ssm_c_re', 'ssm_c_im', 'ssm_d', 'ssm_w_glu', 'ssm_b_glu',
           'attn_out_g', 'ssm_out_g', 'w_out', 'norm_x_g', 'mem_norm_g', 'w_xq', 'w_xkv', 'w_xo', 'norm_ffn_g',
           'w_gate', 'w_up', 'w_down', 'final_norm_g']
SHARDED = {'w_in': 1, 'w_uq': 2, 'w_ukv': 2, 'ssm_w_glu': 1, 'w_out': 1, 'w_xq': 1, 'w_xkv': 2, 'w_xo': 1,
           'w_gate': 2, 'w_up': 2, 'w_down': 1}
SMALL = [n for n in WEIGHTS if n not in SHARDED]
EARLY_WEIGHTS = ('w_in', 'w_uq', 'w_ukv')
FFN_WEIGHTS = ('w_gate', 'w_up', 'w_down')
TRANSPOSED = {'w_gate': (1, 2), 'w_up': (1, 2), 'ssm_b_re': (2, 3), 'ssm_b_im': (2, 3)}
MESH = pl.DeviceIdType.MESH


def _pc(body, **kw):
    return pl.pallas_call(body, **kw)


def _pick(n, prefs):
    for p in prefs:
        if n % p == 0:
            return p
    return n


def _cp(sem=None):
    return pltpu.CompilerParams(dimension_semantics=sem, vmem_limit_bytes=VMEM_LIMIT)


_TILE_CANDS = (1024, 1408, 512, 256, 128)
MM_VMEM_BUDGET = 40 * 1024 * 1024


def _mm_tiles(M, K, N, a_bytes, b_bytes, o_bytes, npair, has_res, need_acc):
    best = None
    for tm in _TILE_CANDS:
        for tk in _TILE_CANDS:
            if M % tm or K % tk:
                continue
            vm = npair * (2 * tm * tk * a_bytes + 2 * tk * N * b_bytes) + 2 * tm * N * o_bytes
            vm += tm * N * 4 * (1 + need_acc + 2 * has_res)
            if a_bytes == 4:
                vm += npair * tm * tk * 2
            if b_bytes == 4:
                vm += npair * tk * N * 2
            if vm <= MM_VMEM_BUDGET and (best is None or tm * tk > best[0]):
                best = (tm * tk, tm, tk)
    if best is None:
        return _pick(M, (256, 128)), _pick(K, (256, 128))
    return best[1], best[2]


def _mm(pairs, mode, out_dtype, res=None, name="mm"):
    a0, b0 = pairs[0]
    if mode == 'nn':
        (M, K), N = a0.shape, b0.shape[1]
        dims = (((1,), (0,)), ((), ()))
    elif mode == 'nt':
        (M, K), N = a0.shape, b0.shape[0]
        dims = (((1,), (1,)), ((), ()))
    else:
        (K, M), N = a0.shape, b0.shape[1]
        dims = (((0,), (0,)), ((), ()))
    npair = len(pairs)
    has_res = res is not None
    direct = out_dtype == f32
    tm, tk = _mm_tiles(M, K, N, a0.dtype.itemsize, b0.dtype.itemsize, jnp.dtype(out_dtype).itemsize, npair, has_res,
                       not direct)
    nk = K // tk

    def body(*refs):
        ins = refs[:2 * npair]
        res_ref = refs[2 * npair] if has_res else None
        o_ref = refs[2 * npair + has_res]
        acc = o_ref if direct else refs[2 * npair + has_res + 1]
        k = pl.program_id(1)
        s = None
        for p in range(npair):
            d = lax.dot_general(ins[2 * p][...].astype(bf16), ins[2 * p + 1][...].astype(bf16), dims,
                                preferred_element_type=f32)
            s = d if s is None else s + d

        @pl.when(k == 0)
        def _():
            acc[...] = s

        @pl.when(k > 0)
        def _():
            acc[...] += s

        if has_res or not direct:
            @pl.when(k == nk - 1)
            def _():
                r = acc[...]
                if has_res:
                    r = r + res_ref[...]
                o_ref[...] = r.astype(out_dtype)

    if mode == 'nn':
        a_spec = pl.BlockSpec((tm, tk), lambda i, k: (i, k))
        b_spec = pl.BlockSpec((tk, N), lambda i, k: (k, 0))
    elif mode == 'nt':
        a_spec = pl.BlockSpec((tm, tk), lambda i, k: (i, k))
        b_spec = pl.BlockSpec((N, tk), lambda i, k: (0, k))
    else:
        a_spec = pl.BlockSpec((tk, tm), lambda i, k: (k, i))
        b_spec = pl.BlockSpec((tk, N), lambda i, k: (k, 0))
    o_spec = pl.BlockSpec((tm, N), lambda i, k: (i, 0))
    in_specs = [a_spec, b_spec] * npair + ([o_spec] if has_res else [])
    args = [t for p in pairs for t in p] + ([res] if has_res else [])
    return _pc(body, grid=(M // tm, nk), in_specs=in_specs, out_specs=o_spec,
               out_shape=jax.ShapeDtypeStruct((M, N), out_dtype),
               scratch_shapes=[] if direct else [pltpu.VMEM((tm, N), f32)],
               compiler_params=_cp(("parallel", "arbitrary")), name=name)(*args)


def _rms_fwd(x, g, *, col0=0, width=None, n_valid=None, out_dtype=bf16, name="rms_fwd"):
    S = x.shape[0]
    width = width or x.shape[1]
    n_valid = n_valid or width
    ts = _pick(S, (512, 256, 128))
    cb = col0 // width

    def body(x_ref, g_ref, o_ref):
        xv = x_ref[...]
        ms = jnp.sum(xv * xv, axis=-1, keepdims=True) * (1.0 / n_valid)
        o_ref[...] = (xv * lax.rsqrt(ms + EPS) * g_ref[...]).astype(out_dtype)

    return _pc(body, grid=(S // ts,),
               in_specs=[pl.BlockSpec((ts, width), lambda i: (i, cb)), pl.BlockSpec((1, width), lambda i: (0, 0))],
               out_specs=pl.BlockSpec((ts, width), lambda i: (i, 0)),
               out_shape=jax.ShapeDtypeStruct((S, width), out_dtype),
               compiler_params=_cp(("parallel",)), name=name)(x, g.reshape(1, width))


def _rms_bwd(x, g, dy, *, col0=0, dcol0=0, width=None, n_valid=None, res=None, out_dtype=f32, delta=False,
             name="rms_bwd"):
    S = x.shape[0]
    width = width or x.shape[1]
    n_valid = n_valid or width
    ts = _pick(S, (512, 256, 128))
    cb, dcb = col0 // width, dcol0 // width
    has_res = res is not None

    def body(*refs):
        x_ref, g_ref, dy_ref = refs[:3]
        res_ref = refs[3] if has_res else None
        outs = refs[3 + has_res:]
        dx_ref, dg_ref = outs[0], outs[1]
        i = pl.program_id(0)
        xv = x_ref[...]
        gv = g_ref[...]
        dyv = dy_ref[...].astype(f32)
        rstd = lax.rsqrt(jnp.sum(xv * xv, axis=-1, keepdims=True) * (1.0 / n_valid) + EPS)
        xh = xv * rstd
        dxh = dyv * gv
        mean = jnp.sum(dxh * xh, axis=-1, keepdims=True) * (1.0 / n_valid)
        dx = rstd * (dxh - xh * mean)
        if delta:
            d_ref = outs[2]
            for h in range(width // LANES):
                sl = slice(h * LANES, (h + 1) * LANES)
                dsum = jnp.sum(dx[:, sl] * xv[:, sl], axis=-1, keepdims=True)
                d_ref[:, sl] = jnp.broadcast_to(dsum, (ts, LANES))
        if has_res:
            dx = dx + res_ref[...]
        dx_ref[...] = dx.astype(out_dtype)

        @pl.when(i == 0)
        def _():
            dg_ref[...] = jnp.zeros_like(dg_ref)

        dg_ref[...] += jnp.sum(dyv * xh, axis=0, keepdims=True)

    blk = lambda c: pl.BlockSpec((ts, width), lambda i: (i, c))
    in_specs = [blk(cb), pl.BlockSpec((1, width), lambda i: (0, 0)), blk(dcb)] + ([blk(0)] if has_res else [])
    out_specs = [blk(0), pl.BlockSpec((1, width), lambda i: (0, 0))] + ([blk(0)] if delta else [])
    out_shape = [jax.ShapeDtypeStruct((S, width), out_dtype), jax.ShapeDtypeStruct((1, width), f32)] + (
        [jax.ShapeDtypeStruct((S, width), f32)] if delta else [])
    args = [x, g.reshape(1, width), dy] + ([res] if has_res else [])
    return _pc(body, grid=(S // ts,), in_specs=in_specs, out_specs=out_specs, out_shape=out_shape,
               compiler_params=_cp(("arbitrary",)), name=name)(*args)


def _loss_head(h, g, target, name="loss_head"):
    S, D = h.shape
    ts = _pick(S, (512, 256, 128))

    def body(h_ref, g_ref, t_ref, dh_ref, dg_ref, loss_ref):
        i = pl.program_id(0)
        xv = h_ref[...]
        gv = g_ref[...]
        rstd = lax.rsqrt(jnp.sum(xv * xv, axis=-1, keepdims=True) * (1.0 / D) + EPS)
        xh = xv * rstd
        err = xh * gv - t_ref[...]
        dyv = err * (1.0 / D)
        dxh = dyv * gv
        mean = jnp.sum(dxh * xh, axis=-1, keepdims=True) * (1.0 / D)
        dh_ref[...] = rstd * (dxh - xh * mean)

        @pl.when(i == 0)
        def _():
            dg_ref[...] = jnp.zeros_like(dg_ref)
            loss_ref[...] = jnp.zeros_like(loss_ref)

        dg_ref[...] += jnp.sum(dyv * xh, axis=0, keepdims=True)
        part = jnp.sum(jnp.sum(err * err, axis=-1, keepdims=True), axis=0, keepdims=True) * (0.5 / D)
        loss_ref[...] += jnp.broadcast_to(part, (1, LANES))

    blk = pl.BlockSpec((ts, D), lambda i: (i, 0))
    row = pl.BlockSpec((1, D), lambda i: (0, 0))
    return _pc(body, grid=(S // ts,), in_specs=[blk, row, blk],
               out_specs=[blk, row, pl.BlockSpec((1, LANES), lambda i: (0, 0))],
               out_shape=[jax.ShapeDtypeStruct((S, D), f32), jax.ShapeDtypeStruct((1, D), f32),
                          jax.ShapeDtypeStruct((1, LANES), f32)],
               compiler_params=_cp(("arbitrary",)), name=name)(h, g.reshape(1, D), target)


def _rope_apply(x, tc, s1, s2):
    return x * tc + pltpu.roll(x, LANES - 16, 1) * s1 + pltpu.roll(x, 16, 1) * s2


def _rope_apply_t(dy, tc, s1, s2):
    return dy * tc + pltpu.roll(dy * s1, 16, 1) + pltpu.roll(dy * s2, LANES - 16, 1)


def _rope_fwd(q, kv, proj, tabs, name="rope_fwd"):
    S = q.shape[0]
    ts = _pick(S, (512, 256, 128))
    scale = (QK_NOPE + QK_ROPE) ** -0.5

    def body(q_ref, kk_ref, kvv_ref, kr_ref, tc_ref, s1_ref, s2_ref, qh_ref, kh_ref, vh_ref):
        tc, s1, s2 = tc_ref[...], s1_ref[...], s2_ref[...]
        krr = _rope_apply(pltpu.roll(kr_ref[...], QK_NOPE, 1), tc, s1, s2)
        for h in range(MLA_HEADS):
            sl = slice(h * HEAD_PAD, (h + 1) * HEAD_PAD)
            qh_ref[:, sl] = (_rope_apply(q_ref[:, sl], tc, s1, s2) * scale).astype(bf16)
            kh_ref[:, sl] = (kk_ref[:, sl] + krr).astype(bf16)
        vh_ref[...] = kvv_ref[...].astype(bf16)

    wide = lambda c: pl.BlockSpec((ts, MLA_PAD), lambda i: (i, c))
    tab = pl.BlockSpec((ts, LANES), lambda i: (i, 0))
    return _pc(body, grid=(S // ts,),
               in_specs=[wide(0), wide(0), wide(1), pl.BlockSpec((ts, LANES), lambda i: (i, 3)), tab, tab, tab],
               out_specs=[wide(0)] * 3, out_shape=[jax.ShapeDtypeStruct((S, MLA_PAD), bf16)] * 3,
               compiler_params=_cp(("parallel",)), name=name)(q, kv, kv, proj, *tabs)


def _rope_bwd(dqh, dkh, dvh, tabs, name="rope_bwd"):
    S = dqh.shape[0]
    ts = _pick(S, (512, 256, 128))
    scale = (QK_NOPE + QK_ROPE) ** -0.5

    def body(dq_ref, dk_ref, dv_ref, tc_ref, s1_ref, s2_ref, oq_ref, okv_ref, okr_ref):
        tc, s1, s2 = tc_ref[...], s1_ref[...], s2_ref[...]
        ksum = None
        for h in range(MLA_HEADS):
            sl = slice(h * HEAD_PAD, (h + 1) * HEAD_PAD)
            oq_ref[:, sl] = (_rope_apply_t(dq_ref[:, sl], tc, s1, s2) * scale).astype(bf16)
            dk = dk_ref[:, sl]
            okv_ref[:, sl] = dk.astype(bf16)
            ksum = dk if ksum is None else ksum + dk
        okv_ref[:, MLA_PAD:] = dv_ref[...].astype(bf16)
        dkr = pltpu.roll(_rope_apply_t(ksum, tc, s1, s2), LANES - QK_NOPE, 1)
        lane = lax.broadcasted_iota(jnp.int32, (ts, LANES), 1)
        okr_ref[...] = jnp.where(lane < QK_ROPE, dkr, 0.0).astype(bf16)

    wide = pl.BlockSpec((ts, MLA_PAD), lambda i: (i, 0))
    tab = pl.BlockSpec((ts, LANES), lambda i: (i, 0))
    return _pc(body, grid=(S // ts,), in_specs=[wide, wide, wide, tab, tab, tab],
               out_specs=[wide, pl.BlockSpec((ts, 2 * MLA_PAD), lambda i: (i, 0)), tab],
               out_shape=[jax.ShapeDtypeStruct((S, MLA_PAD), bf16), jax.ShapeDtypeStruct((S, 2 * MLA_PAD), bf16),
                          jax.ShapeDtypeStruct((S, LANES), bf16)],
               compiler_params=_cp(("parallel",)), name=name)(dqh, dkh, dvh, *tabs)


ATT_BLK = 1024


def _attn_fwd(qh, kh, vh, plan=None, name="attn_fwd"):
    S = qh.shape[0]
    tq = tk = min(S, ATT_BLK)
    nq, nk = S // tq, S // tk
    npl = plan['n'] if plan else 0

    def body(*refs):
        q_ref, k_ref, v_ref = refs[:3]
        o_ref, lse_ref = refs[3 + npl:5 + npl]
        m_sc, l_sc, acc_sc = refs[5 + 2 * npl:8 + 2 * npl]
        h, i, j = pl.program_id(0), pl.program_id(1), pl.program_id(2)
        if plan:
            pargs = (refs[3:3 + npl], refs[5 + npl:5 + 2 * npl], refs[8 + 2 * npl], refs[9 + 2 * npl])
            first = (i == 0) & (j == 0)
            pl.when((h == 0) & first)(functools.partial(plan['start'], *pargs))
            pl.when((h == (3 * MLA_HEADS) // 4) & first)(functools.partial(plan['forward'], *pargs))
            pl.when((h == MLA_HEADS - 1) & (i == nq - 1) & (j == nk - 1))(functools.partial(plan['finish'], *pargs))

        @pl.when(j == 0)
        def _():
            m_sc[...] = jnp.full_like(m_sc, -1e30)
            l_sc[...] = jnp.zeros_like(l_sc)
            acc_sc[...] = jnp.zeros_like(acc_sc)

        def step(masked):
            s = lax.dot_general(q_ref[...], k_ref[...], (((1,), (1,)), ((), ())), preferred_element_type=f32)
            if masked:
                row = lax.broadcasted_iota(jnp.int32, (tq, tk), 0)
                col = lax.broadcasted_iota(jnp.int32, (tq, tk), 1)
                s = jnp.where(col <= row, s, -1e30)
            m_prev = m_sc[...]
            m_new = jnp.maximum(m_prev, jnp.max(s, axis=-1, keepdims=True))
            alpha = jnp.exp(m_prev - m_new)
            p = jnp.exp(s - m_new)
            l_sc[...] = alpha * l_sc[...] + jnp.sum(p, axis=-1, keepdims=True)
            acc_sc[...] = alpha * acc_sc[...] + jnp.dot(p.astype(bf16), v_ref[...], preferred_element_type=f32)
            m_sc[...] = m_new

        pl.when(j < i)(functools.partial(step, False))
        pl.when(j == i)(functools.partial(step, True))

        @pl.when(j == nk - 1)
        def _():
            l = l_sc[...]
            o_ref[...] = acc_sc[...] / l
            lse_ref[...] = jnp.broadcast_to(m_sc[...] + jnp.log(l), (tq, LANES))

    qspec = pl.BlockSpec((tq, HEAD_PAD), lambda h, i, j: (i, h))
    kspec = pl.BlockSpec((tk, HEAD_PAD), lambda h, i, j: (jnp.minimum(j, i), h))
    anyspec = pl.BlockSpec(memory_space=pl.ANY)
    scratch = [pltpu.VMEM((tq, 1), f32), pltpu.VMEM((tq, 1), f32), pltpu.VMEM((tq, HEAD_PAD), f32)]
    if plan:
        scratch += [pltpu.SemaphoreType.DMA((plan['nsem'],)), pltpu.SemaphoreType.DMA((plan['nsem'],))]
    outs = _pc(body, grid=(MLA_HEADS, nq, nk), in_specs=[qspec, kspec, kspec] + [anyspec] * npl,
               out_specs=[qspec, qspec] + [anyspec] * npl,
               out_shape=[jax.ShapeDtypeStruct((S, MLA_PAD), f32)] * 2 + (plan['outs'] if plan else []),
               scratch_shapes=scratch,
               compiler_params=_cp(("arbitrary", "arbitrary", "arbitrary") if plan else ("parallel", "parallel", "arbitrary")),
               name=name)(qh, kh, vh, *(plan['ins'] if plan else []))
    return outs[0], outs[1], outs[2:]


def _attn_bwd(qh, kh, vh, do, lse, delta, plan=None, name="attn_bwd"):
    S = qh.shape[0]
    tq = tk = min(S, ATT_BLK)
    nq, nk = S // tq, S // tk
    npl = plan['n'] if plan else 0

    def body(*refs):
        q_ref, k_ref, v_ref, do_ref, lse_ref, dl_ref = refs[:6]
        dq_ref, dk_ref, dv_ref = refs[6 + npl:9 + npl]
        h, j, i = pl.program_id(0), pl.program_id(1), pl.program_id(2)
        if plan:
            pargs = (refs[6:6 + npl], refs[9 + npl:9 + 2 * npl], refs[9 + 2 * npl], refs[10 + 2 * npl])
            pl.when((h == 0) & (j == 0) & (i == 0))(functools.partial(plan['start'], *pargs))
            pl.when((h == MLA_HEADS - 1) & (j == nk - 1) & (i == nq - 1))(functools.partial(plan['finish'], *pargs))

        @pl.when((j == 0) & (i == 0))
        def _():
            dq_ref[...] = jnp.zeros_like(dq_ref)

        @pl.when(i == 0)
        def _():
            dk_ref[...] = jnp.zeros_like(dk_ref)
            dv_ref[...] = jnp.zeros_like(dv_ref)

        def step(masked):
            nt = (((1,), (1,)), ((), ()))
            tn = (((0,), (0,)), ((), ()))
            qv, kv_, dov = q_ref[...], k_ref[...], do_ref[...]
            s = lax.dot_general(qv, kv_, nt, preferred_element_type=f32)
            p = jnp.exp(s - lse_ref[:, :1])
            if masked:
                row = lax.broadcasted_iota(jnp.int32, (tq, tk), 0)
                col = lax.broadcasted_iota(jnp.int32, (tq, tk), 1)
                p = jnp.where(col <= row, p, 0.0)
            dp = lax.dot_general(dov, v_ref[...], nt, preferred_element_type=f32)
            ds = (p * (dp - dl_ref[:, :1])).astype(bf16)
            dv_ref[...] += lax.dot_general(p.astype(bf16), dov, tn, preferred_element_type=f32)
            dk_ref[...] += lax.dot_general(ds, qv, tn, preferred_element_type=f32)
            rows = pl.ds(pl.multiple_of(i * tq, tq), tq)
            dq_ref[rows, :] += jnp.dot(ds, kv_, preferred_element_type=f32)

        pl.when(i > j)(functools.partial(step, False))
        pl.when(i == j)(functools.partial(step, True))

    qspec = pl.BlockSpec((tq, HEAD_PAD), lambda h, j, i: (jnp.maximum(i, j), h))
    kspec = pl.BlockSpec((tk, HEAD_PAD), lambda h, j, i: (j, h))
    colspec = pl.BlockSpec((S, HEAD_PAD), lambda h, j, i: (0, h))
    anyspec = pl.BlockSpec(memory_space=pl.ANY)
    scratch = [pltpu.SemaphoreType.DMA((plan['nsem'],)), pltpu.SemaphoreType.DMA((plan['nsem'],))] if plan else []
    outs = _pc(body, grid=(MLA_HEADS, nk, nq), in_specs=[qspec, kspec, kspec, qspec, qspec, qspec] + [anyspec] * npl,
               out_specs=[colspec, kspec, kspec] + [anyspec] * npl,
               out_shape=[jax.ShapeDtypeStruct((S, MLA_PAD), f32)] * 3 + (plan['outs'] if plan else []),
               scratch_shapes=scratch,
               compiler_params=_cp(("arbitrary" if plan else "parallel", "arbitrary", "arbitrary")),
               name=name)(qh, kh, vh, do, lse, delta, *(plan['ins'] if plan else []))
    return outs[0], outs[1], outs[2], outs[3:]


def _xattn_fwd(q, kv, name="xattn_fwd"):
    S = q.shape[0]
    M = kv.shape[0]
    tq = _pick(S, (256, 128))
    scale = X_HEAD_DIM ** -0.5

    def body(q_ref, kv_ref, o_ref):
        for h in range(X_HEADS):
            sl = slice(h * X_HEAD_DIM, (h + 1) * X_HEAD_DIM)
            k = kv_ref[:, sl]
            v = kv_ref[:, D_MODEL + h * X_HEAD_DIM:D_MODEL + (h + 1) * X_HEAD_DIM]
            s = lax.dot_general(q_ref[:, sl], k, (((1,), (1,)), ((), ())), preferred_element_type=f32) * scale
            e = jnp.exp(s - jnp.max(s, axis=-1, keepdims=True))
            p = e / jnp.sum(e, axis=-1, keepdims=True)
            o_ref[:, sl] = jnp.dot(p.astype(bf16), v, preferred_element_type=f32).astype(bf16)

    blk = pl.BlockSpec((tq, D_MODEL), lambda i: (i, 0))
    return _pc(body, grid=(S // tq,), in_specs=[blk, pl.BlockSpec((M, 2 * D_MODEL), lambda i: (0, 0))],
               out_specs=blk, out_shape=jax.ShapeDtypeStruct((S, D_MODEL), bf16),
               compiler_params=_cp(("parallel",)), name=name)(q, kv)


def _xattn_bwd(q, kv, do, plan=None, name="xattn_bwd"):
    S = q.shape[0]
    M = kv.shape[0]
    tq = _pick(S, (256, 128))
    scale = X_HEAD_DIM ** -0.5
    npl = plan['n'] if plan else 0
    p_in, p_out, p_shapes, p_sems, p_args = _plan_extras(plan)

    def body(*refs):
        q_ref, kv_ref, do_ref = refs[:3]
        dq_ref, dkv_ref = refs[3 + npl:5 + npl]
        i = pl.program_id(0)
        if plan:
            pargs = (refs[3:3 + npl], refs[5 + npl:5 + 2 * npl], refs[5 + 2 * npl], refs[6 + 2 * npl])
            pl.when(i == 0)(functools.partial(plan['start'], *pargs))
            pl.when(i == S // tq - 1)(functools.partial(plan['finish'], *pargs))

        @pl.when(i == 0)
        def _():
            dkv_ref[...] = jnp.zeros_like(dkv_ref)

        nt = (((1,), (1,)), ((), ()))
        tn = (((0,), (0,)), ((), ()))
        for h in range(X_HEADS):
            sl = slice(h * X_HEAD_DIM, (h + 1) * X_HEAD_DIM)
            vsl = slice(D_MODEL + h * X_HEAD_DIM, D_MODEL + (h + 1) * X_HEAD_DIM)
            k, v, qv, dov = kv_ref[:, sl], kv_ref[:, vsl], q_ref[:, sl], do_ref[:, sl]
            s = lax.dot_general(qv, k, nt, preferred_element_type=f32) * scale
            e = jnp.exp(s - jnp.max(s, axis=-1, keepdims=True))
            p = e / jnp.sum(e, axis=-1, keepdims=True)
            dp = lax.dot_general(dov, v, nt, preferred_element_type=f32)
            ds = (p * (dp - jnp.sum(dp * p, axis=-1, keepdims=True)) * scale).astype(bf16)
            dq_ref[:, sl] = jnp.dot(ds, k, preferred_element_type=f32).astype(bf16)
            dkv_ref[:, sl] += lax.dot_general(ds, qv, tn, preferred_element_type=f32)
            dkv_ref[:, vsl] += lax.dot_general(p.astype(bf16), dov, tn, preferred_element_type=f32)

    blk = pl.BlockSpec((tq, D_MODEL), lambda i: (i, 0))
    full = pl.BlockSpec((M, 2 * D_MODEL), lambda i: (0, 0))
    outs = _pc(body, grid=(S // tq,), in_specs=[blk, full, blk] + p_in, out_specs=[blk, full] + p_out,
               out_shape=[jax.ShapeDtypeStruct((S, D_MODEL), bf16), jax.ShapeDtypeStruct((M, 2 * D_MODEL), f32)] + p_shapes,
               scratch_shapes=p_sems, compiler_params=_cp(("arbitrary",)), name=name)(q, kv, do, *p_args)
    return outs[0], outs[1], outs[2:]


ST_CHUNKS = 1


def _apow_init(a_ref, ap_ref, bp_ref, seg):
    P = MACRO_ST
    ar, ai = a_ref[:, :P], a_ref[:, P:]
    pr, pi = ar, ai
    for r in range(seg):
        ap_ref[r:r + 1, :P] = pr
        ap_ref[r:r + 1, P:] = pi
        if r < seg - 1:
            pr, pi = pr * ar - pi * ai, pr * ai + pi * ar
    br, bi = pr, pi
    for k in range(SUBLANES):
        bp_ref[k:k + 1, :P] = pr
        bp_ref[k:k + 1, P:] = pi
        pr, pi = pr * br - pi * bi, pr * bi + pi * br


def _segment_perm(tS):
    seg = tS // SUBLANES
    rows = jnp.arange(tS)
    src = (rows % SUBLANES) * seg + rows // SUBLANES
    return (src[:, None] == jnp.arange(tS)[None, :]).astype(f32)


def _unpermute_rows(pt, v):
    hi = v.astype(bf16)
    r1 = v - hi.astype(f32)
    mid = r1.astype(bf16)
    lo = (r1 - mid.astype(f32)).astype(bf16)
    out = jnp.dot(pt, jnp.concatenate([hi, mid, lo], axis=1), preferred_element_type=f32)
    w = v.shape[1]
    return (out[:, :w] + out[:, w:2 * w]) + out[:, 2 * w:]


def _scan_block(sc_ref, ap_ref, bp_ref, carry_ref, e_ref, seg, reverse):
    P = MACRO_ST
    sgn = -1.0 if reverse else 1.0
    CH = P // ST_CHUNKS
    rid = lax.broadcasted_iota(jnp.int32, (SUBLANES, CH), 0)
    for c in range(ST_CHUNKS):
        lr, li = slice(c * CH, (c + 1) * CH), slice(P + c * CH, P + (c + 1) * CH)
        ar, ai = ap_ref[0:1, lr], sgn * ap_ref[0:1, li]
        xr = xi = None
        for i in range(seg):
            r = seg - 1 - i if reverse else i
            rows = slice(SUBLANES * r, SUBLANES * (r + 1))
            sr, si = sc_ref[rows, lr], sc_ref[rows, li]
            if i == 0:
                xr, xi = sr, si
            else:
                xr, xi = ar * xr - ai * xi + sr, ar * xi + ai * xr + si
                sc_ref[rows, lr] = xr
                sc_ref[rows, li] = xi
        for sh in (1, 2, 4):
            pr, pi = bp_ref[sh - 1:sh, lr], sgn * bp_ref[sh - 1:sh, li]
            if reverse:
                tr = jnp.where(rid < SUBLANES - sh, pltpu.roll(xr, SUBLANES - sh, 0), 0.0)
                ti = jnp.where(rid < SUBLANES - sh, pltpu.roll(xi, SUBLANES - sh, 0), 0.0)
            else:
                tr = jnp.where(rid >= sh, pltpu.roll(xr, sh, 0), 0.0)
                ti = jnp.where(rid >= sh, pltpu.roll(xi, sh, 0), 0.0)
            xr, xi = xr + pr * tr - pi * ti, xi + pr * ti + pi * tr
        if reverse:
            bpr = jnp.zeros((SUBLANES, CH), f32)
            bpi = jnp.zeros((SUBLANES, CH), f32)
            for r in range(SUBLANES):
                bpr = jnp.where(rid == r, bp_ref[SUBLANES - 1 - r:SUBLANES - r, lr], bpr)
                bpi = jnp.where(rid == r, -bp_ref[SUBLANES - 1 - r:SUBLANES - r, li], bpi)
        else:
            bpr, bpi = bp_ref[:, lr], bp_ref[:, li]
        cr, cim = carry_ref[:, lr], carry_ref[:, li]
        xr, xi = xr + bpr * cr - bpi * cim, xi + bpr * cim + bpi * cr
        edge = 0 if reverse else SUBLANES - 1
        carry_ref[:, lr] = jnp.sum(jnp.where(rid == edge, xr, 0.0), axis=0, keepdims=True)
        carry_ref[:, li] = jnp.sum(jnp.where(rid == edge, xi, 0.0), axis=0, keepdims=True)
        if reverse:
            er = jnp.where(rid == SUBLANES - 1, cr, pltpu.roll(xr, SUBLANES - 1, 0))
            ei = jnp.where(rid == SUBLANES - 1, cim, pltpu.roll(xi, SUBLANES - 1, 0))
        else:
            er = jnp.where(rid == 0, cr, pltpu.roll(xr, 1, 0))
            ei = jnp.where(rid == 0, cim, pltpu.roll(xi, 1, 0))
        if e_ref is not None:
            e_ref[:, lr] = er
            e_ref[:, li] = ei
        for i in range(seg):
            r = seg - 1 - i if reverse else i
            rows = slice(SUBLANES * r, SUBLANES * (r + 1))
            pr, pi = ap_ref[i:i + 1, lr], sgn * ap_ref[i:i + 1, li]
            sc_ref[rows, lr] += pr * er - pi * ei
            sc_ref[rows, li] += pr * ei + pi * er


def _ssm_fwd(proj, bm, cm, a, d, name="ssm_fwd"):
    S = proj.shape[0]
    tS = _pick(S, (256, 128))
    nb = S // tS
    P2 = 2 * MACRO_ST
    seg = tS // SUBLANES
    ucol0 = (D_MODEL - SSM_WIDTH) // MACRO_CH

    perm = _segment_perm(tS)

    def body(u_ref, b_ref, c_ref, a_ref, d_ref, pm_ref, pt_ref, y_ref, xc_ref, bu_sc, ap_sc, bp_sc, car_sc):
        t = pl.program_id(1)

        @pl.when(t == 0)
        def _():
            _apow_init(a_ref, ap_sc, bp_sc, seg)
            car_sc[...] = jnp.zeros_like(car_sc)

        uv = u_ref[...]
        up = jnp.dot(pm_ref[...], uv.astype(bf16), preferred_element_type=f32).astype(bf16)
        bu_sc[...] = jnp.dot(up, b_ref[...], preferred_element_type=f32)
        xc_ref[...] = car_sc[...]
        _scan_block(bu_sc, ap_sc, bp_sc, car_sc, None, seg, False)
        yp = jnp.dot(bu_sc[...].astype(bf16), c_ref[...], preferred_element_type=f32)
        y_ref[...] = _unpermute_rows(pt_ref[...], yp) + d_ref[...] * uv

    sq = pl.BlockSpec((tS, tS), lambda m, t: (0, 0))
    return _pc(body, grid=(SSM_MACRO, nb),
               in_specs=[pl.BlockSpec((tS, MACRO_CH), lambda m, t: (t, ucol0 + m)),
                         pl.BlockSpec((None, MACRO_CH, P2), lambda m, t: (m, 0, 0)),
                         pl.BlockSpec((None, P2, MACRO_CH), lambda m, t: (m, 0, 0)),
                         pl.BlockSpec((None, 1, P2), lambda m, t: (m, 0, 0)),
                         pl.BlockSpec((1, MACRO_CH), lambda m, t: (0, m)), sq, sq],
               out_specs=[pl.BlockSpec((tS, MACRO_CH), lambda m, t: (t, m)),
                          pl.BlockSpec((None, None, 1, P2), lambda m, t: (m, t, 0, 0))],
               out_shape=[jax.ShapeDtypeStruct((S, SSM_WIDTH), f32), jax.ShapeDtypeStruct((SSM_MACRO, nb, 1, P2), f32)],
               scratch_shapes=[pltpu.VMEM((tS, P2), f32), pltpu.VMEM((seg, P2), f32),
                               pltpu.VMEM((SUBLANES, P2), f32), pltpu.VMEM((1, P2), f32)],
               compiler_params=_cp(("arbitrary", "arbitrary")), name=name)(
        proj, bm, cm, a, d.reshape(1, SSM_WIDTH), perm.astype(bf16), perm.T.astype(bf16))


def _ssm_bwd(proj, dy, xc, bm, cm, a, d, plan=None, name="ssm_bwd"):
    S = proj.shape[0]
    tS = _pick(S, (256, 128))
    nb = S // tS
    P = MACRO_ST
    P2 = 2 * P
    seg = tS // SUBLANES
    ucol0 = (D_MODEL - SSM_WIDTH) // MACRO_CH

    perm = _segment_perm(tS)
    npl = plan['n'] if plan else 0
    p_in, p_out, p_shapes, p_sems, p_args = _plan_extras(plan)

    def body(*refs):
        u_ref, dy_ref, xc_ref, b_ref, c_ref, a_ref, d_ref, pm_ref, pt_ref = refs[:9]
        du_ref, db_ref, dc_ref, da_ref, dd_ref = refs[9 + npl:14 + npl]
        x_sc, g_sc, ap_sc, bp_sc, e_sc, xcar_sc, gcar_sc = refs[14 + 2 * npl:21 + 2 * npl]
        t = pl.program_id(1)
        if plan:
            mg = pl.program_id(0)
            pargs = (refs[9:9 + npl], refs[14 + npl:14 + 2 * npl], refs[21 + 2 * npl], refs[22 + 2 * npl])
            pl.when((mg == 0) & (t == 0))(functools.partial(plan['start'], *pargs))
            pl.when((mg == SSM_MACRO - 1) & (t == nb - 1))(functools.partial(plan['finish'], *pargs))

        @pl.when(t == 0)
        def _():
            _apow_init(a_ref, ap_sc, bp_sc, seg)
            gcar_sc[...] = jnp.zeros_like(gcar_sc)
            db_ref[...] = jnp.zeros_like(db_ref)
            dc_ref[...] = jnp.zeros_like(dc_ref)
            da_ref[...] = jnp.zeros_like(da_ref)
            dd_ref[...] = jnp.zeros_like(dd_ref)

        nt = (((1,), (1,)), ((), ()))
        tn = (((0,), (0,)), ((), ()))
        uv = u_ref[...]
        dyv = dy_ref[...]
        pm = pm_ref[...]
        ub = jnp.dot(pm, uv.astype(bf16), preferred_element_type=f32).astype(bf16)
        dyb = jnp.dot(pm, dyv.astype(bf16), preferred_element_type=f32).astype(bf16)
        x_sc[...] = jnp.dot(ub, b_ref[...], preferred_element_type=f32)
        xcar_sc[...] = xc_ref[...]
        _scan_block(x_sc, ap_sc, bp_sc, xcar_sc, e_sc, seg, False)
        g_sc[...] = lax.dot_general(dyb, c_ref[...], nt, preferred_element_type=f32)
        _scan_block(g_sc, ap_sc, bp_sc, gcar_sc, None, seg, True)
        xv = x_sc[...]
        gv = g_sc[...]
        gb = gv.astype(bf16)
        dc_ref[...] += lax.dot_general(xv.astype(bf16), dyb, tn, preferred_element_type=f32)
        db_ref[...] += lax.dot_general(ub, gb, tn, preferred_element_type=f32)
        dup = lax.dot_general(gb, b_ref[...], nt, preferred_element_type=f32)
        du_ref[...] = _unpermute_rows(pt_ref[...], dup) + d_ref[...] * dyv
        dd_ref[...] += jnp.sum(dyv * uv, axis=0, keepdims=True)
        xp = jnp.concatenate([e_sc[...], xv[:tS - SUBLANES]], axis=0)
        xpr, xpi, ggr, ggi = xp[:, :P], xp[:, P:], gv[:, :P], gv[:, P:]
        da_ref[:, :P] += jnp.sum(ggr * xpr + ggi * xpi, axis=0, keepdims=True)
        da_ref[:, P:] += jnp.sum(ggi * xpr - ggr * xpi, axis=0, keepdims=True)

    rev = lambda t: nb - 1 - t
    outs = _pc(body, grid=(SSM_MACRO, nb),
               in_specs=[pl.BlockSpec((tS, MACRO_CH), lambda m, t: (rev(t), ucol0 + m)),
                         pl.BlockSpec((tS, MACRO_CH), lambda m, t: (rev(t), m)),
                         pl.BlockSpec((None, None, 1, P2), lambda m, t: (m, rev(t), 0, 0)),
                         pl.BlockSpec((None, MACRO_CH, P2), lambda m, t: (m, 0, 0)),
                         pl.BlockSpec((None, P2, MACRO_CH), lambda m, t: (m, 0, 0)),
                         pl.BlockSpec((None, 1, P2), lambda m, t: (m, 0, 0)),
                         pl.BlockSpec((1, MACRO_CH), lambda m, t: (0, m)),
                         pl.BlockSpec((tS, tS), lambda m, t: (0, 0)), pl.BlockSpec((tS, tS), lambda m, t: (0, 0))] + p_in,
               out_specs=[pl.BlockSpec((tS, MACRO_CH), lambda m, t: (rev(t), m)),
                          pl.BlockSpec((None, MACRO_CH, P2), lambda m, t: (m, 0, 0)),
                          pl.BlockSpec((None, P2, MACRO_CH), lambda m, t: (m, 0, 0)),
                          pl.BlockSpec((None, 1, P2), lambda m, t: (m, 0, 0)),
                          pl.BlockSpec((1, MACRO_CH), lambda m, t: (0, m))] + p_out,
               out_shape=[jax.ShapeDtypeStruct((S, SSM_WIDTH), f32),
                          jax.ShapeDtypeStruct((SSM_MACRO, MACRO_CH, P2), f32),
                          jax.ShapeDtypeStruct((SSM_MACRO, P2, MACRO_CH), f32),
                          jax.ShapeDtypeStruct((SSM_MACRO, 1, P2), f32),
                          jax.ShapeDtypeStruct((1, SSM_WIDTH), f32)] + p_shapes,
               scratch_shapes=[pltpu.VMEM((tS, P2), f32), pltpu.VMEM((tS, P2), f32),
                               pltpu.VMEM((seg, P2), f32), pltpu.VMEM((SUBLANES, P2), f32), pltpu.VMEM((SUBLANES, P2), f32),
                               pltpu.VMEM((1, P2), f32), pltpu.VMEM((1, P2), f32)] + p_sems,
               compiler_params=_cp(("arbitrary", "arbitrary")), name=name)(
        proj, dy, xc, bm, cm, a, d.reshape(1, SSM_WIDTH), perm.astype(bf16), perm.T.astype(bf16), *p_args)
    return outs[0], outs[1], outs[2], outs[3], outs[4], outs[5:]


_GELU_K = math.sqrt(2.0 / math.pi)
_GELU_C = 0.044715


def _glu_fwd(y, w, b, g, name="glu_fwd"):
    S, W = y.shape
    ts = _pick(S, (512, 256, 128))

    def body(y_ref, w_ref, b_ref, g_ref, z_ref, sn_ref, ge_ref):
        yv = y_ref[...]
        cdf = 0.5 * (1.0 + jnp.tanh(_GELU_K * (yv + _GELU_C * (yv * yv * yv))))
        ge = (yv * cdf).astype(bf16)
        z = jnp.dot(ge, w_ref[...], preferred_element_type=f32) + b_ref[...]
        s = yv * jax.nn.sigmoid(z)
        rstd = lax.rsqrt(jnp.sum(s * s, axis=-1, keepdims=True) * (1.0 / W) + EPS)
        z_ref[...] = z
        sn_ref[...] = (s * rstd * g_ref[...]).astype(bf16)
        ge_ref[...] = ge

    blk = pl.BlockSpec((ts, W), lambda i: (i, 0))
    row = pl.BlockSpec((1, W), lambda i: (0, 0))
    return _pc(body, grid=(S // ts,), in_specs=[blk, pl.BlockSpec((W, W), lambda i: (0, 0)), row, row],
               out_specs=[blk, blk, blk],
               out_shape=[jax.ShapeDtypeStruct((S, W), f32), jax.ShapeDtypeStruct((S, W), bf16),
                          jax.ShapeDtypeStruct((S, W), bf16)],
               compiler_params=_cp(("parallel",)), name=name)(y, w, b.reshape(1, W), g.reshape(1, W))


def _glu_bwd(y, z, dmixed, w, g, name="glu_bwd"):
    S, W = y.shape
    ts = _pick(S, (512, 256, 128))
    dcb = MLA_PAD // W

    def body(y_ref, z_ref, dsn_ref, w_ref, g_ref, dy_ref, dz_ref, dg_ref, db_ref):
        i = pl.program_id(0)
        yv, zv, gv = y_ref[...], z_ref[...], g_ref[...]
        sig = jax.nn.sigmoid(zv)
        s = yv * sig
        rstd = lax.rsqrt(jnp.sum(s * s, axis=-1, keepdims=True) * (1.0 / W) + EPS)
        sh = s * rstd
        dsn = dsn_ref[...]
        dsh = dsn * gv
        ds = rstd * (dsh - sh * (jnp.sum(dsh * sh, axis=-1, keepdims=True) * (1.0 / W)))
        dz = ds * s * (1.0 - sig)
        dzb = dz.astype(bf16)
        dge = lax.dot_general(dzb, w_ref[...], (((1,), (1,)), ((), ())), preferred_element_type=f32)
        t = jnp.tanh(_GELU_K * (yv + _GELU_C * (yv * yv * yv)))
        dgelu = 0.5 * (1.0 + t) + 0.5 * yv * (1.0 - t * t) * _GELU_K * (1.0 + 3.0 * _GELU_C * yv * yv)
        dy_ref[...] = ds * sig + dge * dgelu
        dz_ref[...] = dzb

        @pl.when(i == 0)
        def _():
            dg_ref[...] = jnp.zeros_like(dg_ref)
            db_ref[...] = jnp.zeros_like(db_ref)

        dg_ref[...] += jnp.sum(dsn * sh, axis=0, keepdims=True)
        db_ref[...] += jnp.sum(dz, axis=0, keepdims=True)

    blk = pl.BlockSpec((ts, W), lambda i: (i, 0))
    row = pl.BlockSpec((1, W), lambda i: (0, 0))
    return _pc(body, grid=(S // ts,),
               in_specs=[blk, blk, pl.BlockSpec((ts, W), lambda i: (i, dcb)), pl.BlockSpec((W, W), lambda i: (0, 0)), row],
               out_specs=[blk, blk, row, row],
               out_shape=[jax.ShapeDtypeStruct((S, W), f32), jax.ShapeDtypeStruct((S, W), bf16),
                          jax.ShapeDtypeStruct((1, W), f32), jax.ShapeDtypeStruct((1, W), f32)],
               compiler_params=_cp(("arbitrary",)), name=name)(y, z, dmixed, w, g.reshape(1, W))


def _ffn_up(hn, wg, wu, name="ffn_up"):
    S, K = hn.shape
    F = wg.shape[0]
    tm, tn = _pick(S, (512, 256, 128)), _pick(F, (1408, 256, 128))

    def body(h_ref, wg_ref, wu_ref, g_ref, u_ref, a_ref):
        hv = h_ref[...]
        nt = (((1,), (1,)), ((), ()))
        gv = lax.dot_general(hv, wg_ref[...], nt, preferred_element_type=f32)
        uv = lax.dot_general(hv, wu_ref[...], nt, preferred_element_type=f32)
        g_ref[...] = gv
        u_ref[...] = uv
        a_ref[...] = (gv * jax.nn.sigmoid(gv) * uv).astype(bf16)

    wspec = pl.BlockSpec((tn, K), lambda i, j: (j, 0))
    ospec = pl.BlockSpec((tm, tn), lambda i, j: (i, j))
    return _pc(body, grid=(S // tm, F // tn), in_specs=[pl.BlockSpec((tm, K), lambda i, j: (i, 0)), wspec, wspec],
               out_specs=[ospec] * 3,
               out_shape=[jax.ShapeDtypeStruct((S, F), f32), jax.ShapeDtypeStruct((S, F), f32),
                          jax.ShapeDtypeStruct((S, F), bf16)],
               compiler_params=_cp(("parallel", "parallel")), name=name)(hn, wg, wu)


def _ffn_bwd_act(dh, wd, gate, up, name="ffn_bwd_act"):
    S, K = dh.shape
    F = wd.shape[0]
    tm, tn = _pick(S, (512, 256, 128)), _pick(F, (1408, 256, 128))

    def body(dh_ref, wd_ref, g_ref, u_ref, dg_ref, du_ref):
        dact = lax.dot_general(dh_ref[...].astype(bf16), wd_ref[...], (((1,), (1,)), ((), ())),
                               preferred_element_type=f32)
        gv, uv = g_ref[...], u_ref[...]
        sig = jax.nn.sigmoid(gv)
        dg_ref[...] = (dact * uv * (sig * (1.0 + gv * (1.0 - sig)))).astype(bf16)
        du_ref[...] = (dact * (gv * sig)).astype(bf16)

    ospec = pl.BlockSpec((tm, tn), lambda i, j: (i, j))
    return _pc(body, grid=(S // tm, F // tn),
               in_specs=[pl.BlockSpec((tm, K), lambda i, j: (i, 0)), pl.BlockSpec((tn, K), lambda i, j: (j, 0)),
                         ospec, ospec],
               out_specs=[ospec] * 2, out_shape=[jax.ShapeDtypeStruct((S, F), bf16)] * 2,
               compiler_params=_cp(("parallel", "parallel")), name=name)(dh, wd, gate, up)


def _pad_heads(w, per_head, pieces):
    K = w.shape[0]
    w3 = w.reshape(K, MLA_HEADS, per_head)
    out = jnp.zeros((K, MLA_HEADS, HEAD_PAD), w.dtype)
    for s0, s1, d0 in pieces:
        out = out.at[:, :, d0:d0 + (s1 - s0)].set(w3[:, :, s0:s1])
    return out.reshape(K, MLA_PAD)


def _unpad_heads(wp, per_head, pieces):
    K = wp.shape[0]
    w3 = wp.reshape(K, MLA_HEADS, HEAD_PAD)
    out = jnp.zeros((K, MLA_HEADS, per_head), wp.dtype)
    for s0, s1, d0 in pieces:
        out = out.at[:, :, s0:s1].set(w3[:, :, d0:d0 + (s1 - s0)])
    return out.reshape(K, MLA_HEADS * per_head)


_Q_PIECES = [(0, QK_NOPE + QK_ROPE, 0)]
_K_PIECES = [(0, QK_NOPE, 0)]
_V_PIECES = [(QK_NOPE, QK_NOPE + V_HEAD, 0)]
_KR0 = Q_LORA + KV_LORA


def _pack_win(w):
    z = jnp.zeros((w.shape[0], LANES - QK_ROPE), w.dtype)
    return jnp.concatenate([w[:, :_KR0 + QK_ROPE], z, w[:, _KR0 + QK_ROPE:]], axis=1)


def _unpack_win(wp):
    return jnp.concatenate([wp[:, :_KR0 + QK_ROPE], wp[:, _KR0 + LANES:]], axis=1)


def _pack_wout(w):
    wa = w[:MLA_WIDTH].reshape(MLA_HEADS, V_HEAD, D_MODEL)
    wa = jnp.concatenate([wa, jnp.zeros_like(wa)], axis=1).reshape(MLA_PAD, D_MODEL)
    return jnp.concatenate([wa, w[MLA_WIDTH:]], axis=0)


def _unpack_wout(wp):
    wa = wp[:MLA_PAD].reshape(MLA_HEADS, HEAD_PAD, D_MODEL)[:, :V_HEAD].reshape(MLA_WIDTH, D_MODEL)
    return jnp.concatenate([wa, wp[MLA_PAD:]], axis=0)


def _pad_gain(g):
    g2 = g.reshape(MLA_HEADS, V_HEAD)
    return jnp.concatenate([g2, jnp.zeros_like(g2)], axis=1).reshape(MLA_PAD)


def _unpad_gain(gp):
    return gp.reshape(MLA_HEADS, HEAD_PAD)[:, :V_HEAD].reshape(MLA_WIDTH)


def _ssm_prep(lam_re, lam_im, log_dt, b_re, b_im, c_re, c_im):
    lam = lax.complex(lam_re, lam_im)
    dt = jnp.exp(log_dt)[:, None]
    a_bar = jnp.exp(lam * dt)
    b_bar = ((a_bar - 1.0) / lam)[:, None, :] * lax.complex(b_re, b_im)
    G8 = SSM_GROUPS // SSM_MACRO
    eye = jnp.eye(G8, dtype=f32)

    def bmat(part):
        p4 = part.reshape(SSM_MACRO, G8, SSM_GROUP, SSM_STATE)
        return jnp.einsum('mgcp,gh->mgchp', p4, eye).reshape(SSM_MACRO, MACRO_CH, MACRO_ST)

    def cmat(part):
        p4 = part.reshape(SSM_MACRO, G8, SSM_GROUP, SSM_STATE)
        return jnp.einsum('mgcp,gh->mgphc', p4, eye).reshape(SSM_MACRO, MACRO_ST, MACRO_CH)

    bm = jnp.concatenate([bmat(b_bar.real), bmat(b_bar.imag)], axis=2)
    cm = jnp.concatenate([cmat(c_re), -cmat(c_im)], axis=1)
    a4 = a_bar.reshape(SSM_MACRO, 1, MACRO_ST)
    a = jnp.concatenate([a4.real, a4.imag], axis=2)
    return bm, cm, a


def _rope_tables(positions):
    freqs = ROPE_THETA ** (-jnp.arange(0, QK_ROPE, 2, dtype=f32) / QK_ROPE)
    ang = positions.astype(f32)[:, None] * freqs
    cos, sin = jnp.cos(ang), jnp.sin(ang)
    S = positions.shape[0]
    half = QK_ROPE // 2
    one, zero = jnp.ones((S, QK_NOPE), f32), jnp.zeros((S, half), f32)
    z64, z32 = jnp.zeros((S, QK_NOPE), f32), jnp.zeros((S, LANES - QK_NOPE - QK_ROPE), f32)
    tc = jnp.concatenate([one, cos, cos, z32], axis=1)
    s1 = jnp.concatenate([z64, -sin, zero, z32], axis=1)
    s2 = jnp.concatenate([z64, zero, sin, z32], axis=1)
    return tc, s1, s2


def _layer_params(W, l):
    p = {}
    p['win'] = _pack_win(W['w_in'][l])
    p['wuq'] = _pad_heads(W['w_uq'][l], QK_NOPE + QK_ROPE, _Q_PIECES)
    wukv = W['w_ukv'][l]
    p['wukv'] = jnp.concatenate([_pad_heads(wukv, QK_NOPE + V_HEAD, _K_PIECES),
                                 _pad_heads(wukv, QK_NOPE + V_HEAD, _V_PIECES)], axis=1)
    p['attn_g'] = _pad_gain(W['attn_out_g'][l])
    return p


def _forward_layer(h, memn_in, tabs, W, l, name, gather=None):
    p = _layer_params(W, l)
    sv = {'h0': h, 'p': p}
    xn = _rms_fwd(h, W['norm_mix_g'][l], name=name + "rms_mix")
    proj = _mm([(xn, p['win'])], 'nn', f32, name=name + "mm_in")
    cqn = _rms_fwd(proj, W['q_norm_g'][l], col0=0, width=Q_LORA, name=name + "rms_q")
    ckvn = _rms_fwd(proj, W['kv_norm_g'][l], col0=Q_LORA, width=KV_LORA, name=name + "rms_kv")
    q = _mm([(cqn, p['wuq'])], 'nn', f32, name=name + "mm_uq")
    kv = _mm([(ckvn, p['wukv'])], 'nn', f32, name=name + "mm_ukv")
    qh, kh, vh = _rope_fwd(q, kv, proj, tabs, name=name + "rope")
    oh, lse, carried = _attn_fwd(qh, kh, vh, plan=gather[0] if gather else None, name=name + "attn")
    if gather:
        for n, t in gather[1](carried).items():
            W[n][l] = t
    p['wout'] = _pack_wout(W['w_out'][l])
    an = _rms_fwd(oh, p['attn_g'], n_valid=MLA_WIDTH, name=name + "rms_attn")
    bm, cm, a = W['ssm'][l]
    bmb, cmb = bm.astype(bf16), cm.astype(bf16)
    y, xc = _ssm_fwd(proj, bmb, cmb, a, W['ssm_d'][l], name=name + "ssm")
    z, sn, ge = _glu_fwd(y, W['ssm_w_glu'][l], W['ssm_b_glu'][l], W['ssm_out_g'][l], name=name + "glu")
    h1a = _mm([(an, p['wout'][:MLA_PAD])], 'nn', f32, res=h, name=name + "mm_out_a")
    h1 = _mm([(sn, p['wout'][MLA_PAD:])], 'nn', f32, res=h1a, name=name + "mm_out_s")
    hn2 = _rms_fwd(h1, W['norm_x_g'][l], name=name + "rms_x")
    memn = _rms_fwd(memn_in, W['mem_norm_g'][l], name=name + "rms_mem")
    qx = _mm([(hn2, W['w_xq'][l])], 'nn', bf16, name=name + "mm_xq")
    kvx = _mm([(memn, W['w_xkv'][l])], 'nn', bf16, name=name + "mm_xkv")
    ox = _xattn_fwd(qx, kvx, name=name + "xattn")
    h2 = _mm([(ox, W['w_xo'][l])], 'nn', f32, res=h1, name=name + "mm_xo")
    hn3 = _rms_fwd(h2, W['norm_ffn_g'][l], name=name + "rms_ffn")
    gate, up, act = _ffn_up(hn3, W['w_gate'][l], W['w_up'][l], name=name + "ffn_up")
    h3 = _mm([(act, W['w_down'][l])], 'nn', f32, res=h2, name=name + "mm_down")
    sv.update(xn=xn, proj=proj, cqn=cqn, ckvn=ckvn, qh=qh, kh=kh, vh=vh, oh=oh, lse=lse, an=an, bmb=bmb, cmb=cmb,
              a=a, y=y, xc=xc, z=z, sn=sn, ge=ge, h1=h1, hn2=hn2, memn=memn, qx=qx, kvx=kvx, ox=ox, h2=h2, hn3=hn3,
              gate=gate, up=up, act=act)
    return h3, sv


def _backward_layer(dh3, sv, memn_in, tabs, W, l, hook, name):
    p = sv['p']
    G = {}
    G['w_down'] = _mm([(sv['act'], dh3)], 'tn', f32, name=name + "dw_down")
    dgate, dup = _ffn_bwd_act(dh3, W['w_down'][l], sv['gate'], sv['up'], name=name + "ffn_bwd_act")
    dhn3 = _mm([(dgate, W['w_gate'][l]), (dup, W['w_up'][l])], 'nn', f32, name=name + "mm_dffn")
    G['w_gate'] = _mm([(dgate, sv['hn3'])], 'tn', f32, name=name + "dw_gate")
    G['w_up'] = _mm([(dup, sv['hn3'])], 'tn', f32, name=name + "dw_up")
    xplan = hook['ffn'](l, G) if hook else None
    dh2, dg = _rms_bwd(sv['h2'], W['norm_ffn_g'][l], dhn3, res=dh3, name=name + "rmsb_ffn")
    G['norm_ffn_g'] = dg[0]
    G['w_xo'] = _mm([(sv['ox'], dh2)], 'tn', f32, name=name + "dw_xo")
    dox = _mm([(dh2, W['w_xo'][l])], 'nt', bf16, name=name + "mm_dxo")
    dqx, dkvx, carried = _xattn_bwd(sv['qx'], sv['kvx'], dox, plan=xplan, name=name + "xattn_bwd")
    if xplan:
        xplan['done'](carried)
    G['w_xq'] = _mm([(sv['hn2'], dqx)], 'tn', f32, name=name + "dw_xq")
    G['w_xkv'] = _mm([(sv['memn'], dkvx)], 'tn', f32, name=name + "dw_xkv")
    dhn2 = _mm([(dqx, W['w_xq'][l])], 'nt', f32, name=name + "mm_dxq")
    dmemn = _mm([(dkvx, W['w_xkv'][l])], 'nt', f32, name=name + "mm_dxkv")
    dh1, dg = _rms_bwd(sv['h1'], W['norm_x_g'][l], dhn2, res=dh2, name=name + "rmsb_x")
    G['norm_x_g'] = dg[0]
    _, dg = _rms_bwd(memn_in, W['mem_norm_g'][l], dmemn, name=name + "rmsb_mem")
    G['mem_norm_g'] = dg[0]
    dwo_a = _mm([(sv['an'], dh1)], 'tn', f32, name=name + "dw_out_a")
    dwo_s = _mm([(sv['sn'], dh1)], 'tn', f32, name=name + "dw_out_s")
    G['w_out'] = _unpack_wout(jnp.concatenate([dwo_a, dwo_s], axis=0))
    dmixed = _mm([(dh1, p['wout'])], 'nt', f32, name=name + "mm_dout")
    dy, dz, dg, db = _glu_bwd(sv['y'], sv['z'], dmixed, W['ssm_w_glu'][l], W['ssm_out_g'][l], name=name + "glu_bwd")
    G['ssm_out_g'], G['ssm_b_glu'] = dg[0], db[0]
    G['ssm_w_glu'] = _mm([(sv['ge'], dz)], 'tn', f32, name=name + "dw_glu")
    xplan = hook['mid'](l, G) if hook else None
    du, dbm, dcm, da, dd, carried = _ssm_bwd(sv['proj'], dy, sv['xc'], sv['bmb'], sv['cmb'], sv['a'], W['ssm_d'][l],
                                             plan=xplan, name=name + "ssm_bwd")
    if xplan:
        xplan['done'](carried)
    G['ssm_d'] = dd[0]
    G['ssm_raw'] = (dbm, dcm, da)
    doh, dg, delta = _rms_bwd(sv['oh'], p['attn_g'], dmixed, width=MLA_PAD, n_valid=MLA_WIDTH, delta=True,
                              out_dtype=bf16, name=name + "rmsb_attn")
    G['attn_out_g'] = _unpad_gain(dg[0])
    plans = hook['take']() if hook else []
    plan = _merge_plans(plans) if plans else None
    dqh, dkh, dvh, carried = _attn_bwd(sv['qh'], sv['kh'], sv['vh'], doh, sv['lse'], delta, plan=plan,
                                       name=name + "attn_bwd")
    if plan:
        plan['done'](carried)
    dq, dkv, dkr = _rope_bwd(dqh, dkh, dvh, tabs, name=name + "rope_bwd")
    G['w_uq'] = _unpad_heads(_mm([(sv['cqn'], dq)], 'tn', f32, name=name + "dw_uq"), QK_NOPE + QK_ROPE, _Q_PIECES)
    dwukv = _mm([(sv['ckvn'], dkv)], 'tn', f32, name=name + "dw_ukv")
    G['w_ukv'] = (_unpad_heads(dwukv[:, :MLA_PAD], QK_NOPE + V_HEAD, _K_PIECES)
                  + _unpad_heads(dwukv[:, MLA_PAD:], QK_NOPE + V_HEAD, _V_PIECES))
    dcqn = _mm([(dq, p['wuq'])], 'nt', f32, name=name + "mm_duq")
    dckvn = _mm([(dkv, p['wukv'])], 'nt', f32, name=name + "mm_dukv")
    dcq, dg = _rms_bwd(sv['proj'], W['q_norm_g'][l], dcqn, col0=0, width=Q_LORA, out_dtype=bf16, name=name + "rmsb_q")
    G['q_norm_g'] = dg[0]
    dckv, dg = _rms_bwd(sv['proj'], W['kv_norm_g'][l], dckvn, col0=Q_LORA, width=KV_LORA, out_dtype=bf16,
                        name=name + "rmsb_kv")
    G['kv_norm_g'] = dg[0]
    dproj = jnp.concatenate([dcq, dckv, dkr, du.astype(bf16)], axis=1)
    G['w_in'] = _unpack_win(_mm([(sv['xn'], dproj)], 'tn', f32, name=name + "dw_in"))
    dxn = _mm([(dproj, p['win'])], 'nt', f32, name=name + "mm_din")
    dh0, dg = _rms_bwd(sv['h0'], W['norm_mix_g'][l], dxn, res=dh1, name=name + "rmsb_mix")
    G['norm_mix_g'] = dg[0]
    return dh0, G


def _local_step(x, mem, positions, target, W, gathers=None, hook=None):
    tabs = _rope_tables(positions)
    ssm_in = [(W['ssm_lambda_re'][l], W['ssm_lambda_im'][l], W['ssm_log_dt'][l], W['ssm_b_re'][l], W['ssm_b_im'][l],
               W['ssm_c_re'][l], W['ssm_c_im'][l]) for l in range(DEPTH)]
    preps = [jax.vjp(_ssm_prep, *ssm_in[l]) for l in range(DEPTH)]
    W = dict(W)
    W['ssm'] = [preps[l][0] for l in range(DEPTH)]
    h = x
    saved = []
    for l in range(DEPTH):
        h, sv = _forward_layer(h, mem, tabs, W, l, f"l{l}_", gathers[l] if gathers else None)
        saved.append(sv)
    dh, dgf, loss = _loss_head(h, W['final_norm_g'], target)
    grads = [None] * DEPTH
    for l in reversed(range(DEPTH)):
        dh, G = _backward_layer(dh, saved[l], mem, tabs, W, l, hook, f"l{l}b_")
        dbm, dcm, da = G.pop('ssm_raw')
        names = ['ssm_lambda_re', 'ssm_lambda_im', 'ssm_log_dt', 'ssm_b_re', 'ssm_b_im', 'ssm_c_re', 'ssm_c_im']
        for n, g in zip(names, preps[l][1]((dbm, dcm, da))):
            G[n] = g
        grads[l] = G
        if hook is not None and l > 0:
            hook['rest'](l, G)
    out = {n: [grads[l][n] for l in range(DEPTH)] if n in SHARDED else jnp.stack([grads[l][n] for l in range(DEPTH)])
           for n in grads[0]}
    out['final_norm_g'] = dgf[0]
    return loss[0, 0], dh, out


_HBM = pl.BlockSpec(memory_space=pltpu.HBM)


def _me():
    return lax.axis_index("x"), lax.axis_index("y"), lax.axis_index("c")


def _chip_peers(x, y, c):
    devs = [(1 - x, y, c), (x, 1 - y, c), (1 - x, 1 - y, c)]
    return devs, [2 * d[0] + d[1] for d in devs]


def _gather_plan(xs, half_first):
    n = len(xs)
    if half_first:
        ins = [t.reshape(2, 1, *t.shape[1:]) for t in xs]
        outs = [jax.ShapeDtypeStruct((2, 4, *t.shape[1:]), t.dtype) for t in xs]
    else:
        ins = [t.reshape(1, 2, t.shape[0] // 2, t.shape[1]) for t in xs]
        outs = [jax.ShapeDtypeStruct((4, 2, t.shape[0] // 2, t.shape[1]), t.dtype) for t in xs]

    def own(ref, h):
        return ref.at[h] if half_first else ref.at[:, h]

    def slot(ref, h, j):
        return ref.at[h, pl.ds(j, 1)] if half_first else ref.at[pl.ds(j, 1), h]

    def copies(src, dst, send, recv):
        x, y, c = _me()
        jme = 2 * x + y
        devs, js = _chip_peers(x, y, c)
        half, other = pl.ds(c, 1), pl.ds(1 - c, 1)
        mk = pltpu.make_async_remote_copy
        for i in range(n):
            for k in range(3):
                out_cp = mk(own(src[i], half), slot(dst[i], half, jme), send.at[6 * i + k], recv.at[6 * i + k],
                            device_id=devs[k], device_id_type=MESH)
                in_cp = mk(own(src[i], half), slot(dst[i], half, js[k]), send.at[6 * i + k], recv.at[6 * i + k],
                           device_id=devs[k], device_id_type=MESH)
                pass_cp = mk(slot(dst[i], half, js[k]), slot(dst[i], half, js[k]), send.at[6 * i + 3 + k],
                             recv.at[6 * i + 3 + k], device_id=(x, y, 1 - c), device_id_type=MESH)
                got_cp = mk(slot(dst[i], other, js[k]), slot(dst[i], other, js[k]), send.at[6 * i + 3 + k],
                            recv.at[6 * i + 3 + k], device_id=(x, y, 1 - c), device_id_type=MESH)
                yield out_cp, in_cp, pass_cp, got_cp

    def start(*refs):
        for out_cp, _, _, _ in copies(*refs):
            out_cp.start()

    def forward(*refs):
        for _, in_cp, pass_cp, _ in copies(*refs):
            in_cp.wait_recv()
            pass_cp.start()

    def finish(*refs):
        for out_cp, _, pass_cp, got_cp in copies(*refs):
            got_cp.wait_recv()
            out_cp.wait_send()
            pass_cp.wait_send()

    return dict(n=n, ins=ins, outs=outs, nsem=6 * n, start=start, forward=forward, finish=finish)


def _plan_refs(plan, refs):
    n = plan['n']
    return refs[:n], refs[n:2 * n], refs[2 * n], refs[2 * n + 1]


def _run_plan(plan, name):
    n = plan['n']

    def body(*refs):
        args = _plan_refs(plan, refs)
        plan['start'](*args)
        plan['forward'](*args)
        plan['finish'](*args)

    return _pc(body, in_specs=[_HBM] * n, out_specs=[_HBM] * n, out_shape=plan['outs'],
               scratch_shapes=[pltpu.SemaphoreType.DMA((plan['nsem'],)), pltpu.SemaphoreType.DMA((plan['nsem'],))],
               compiler_params=pltpu.CompilerParams(has_side_effects=True), name=name)(*plan['ins'])


def _fill_own(gathered, own, half_first):
    jme = (2 * lax.axis_index("x") + lax.axis_index("y")).astype(jnp.int32)
    zero = jnp.int32(0)
    if half_first:
        return lax.dynamic_update_slice(gathered, own[:, None], (zero, jme, zero, zero))
    return lax.dynamic_update_slice(gathered, own.reshape(1, *gathered.shape[1:]), (jme, zero, zero, zero))


def _exchange_plan(gs, done=None):
    n = len(gs)

    def copies(src, dst, send, recv):
        x, y, c = _me()
        for i in range(n):
            yield pltpu.make_async_remote_copy(src[i].at[:, pl.ds(1 - c, 1)], dst[i], send.at[i], recv.at[i],
                                               device_id=(x, y, 1 - c), device_id_type=MESH)

    def start(*refs):
        for cp in copies(*refs):
            cp.start()

    def finish(*refs):
        for cp in copies(*refs):
            cp.wait()

    outs = [jax.ShapeDtypeStruct((4, 1, *g.shape[2:]), g.dtype) for g in gs]
    return dict(n=n, ins=list(gs), outs=outs, nsem=n, start=start, forward=lambda *refs: None, finish=finish, done=done)


def _plan_extras(plan):
    if not plan:
        return [], [], [], [], []
    anyspec = pl.BlockSpec(memory_space=pl.ANY)
    sems = [pltpu.SemaphoreType.DMA((plan['nsem'],)), pltpu.SemaphoreType.DMA((plan['nsem'],))]
    return [anyspec] * plan['n'], [anyspec] * plan['n'], list(plan['outs']), sems, list(plan['ins'])


def _scatter_plan(ps, done=None):
    n = len(ps)

    def copies(src, dst, send, recv, off):
        x, y, c = _me()
        devs, js = _chip_peers(x, y, c)
        for i in range(n):
            for k in range(3):
                yield pltpu.make_async_remote_copy(src[i].at[pl.ds(js[k], 1)], dst[i].at[k], send.at[off + 3 * i + k],
                                                   recv.at[off + 3 * i + k], device_id=devs[k], device_id_type=MESH)

    def start(src, dst, send, recv, off=0):
        for cp in copies(src, dst, send, recv, off):
            cp.start()

    def finish(src, dst, send, recv, off=0):
        for cp in copies(src, dst, send, recv, off):
            cp.wait()

    outs = [jax.ShapeDtypeStruct((3, 1, *p.shape[1:]), p.dtype) for p in ps]
    return dict(n=n, ins=list(ps), outs=outs, nsem=3 * n, start=start, forward=lambda *refs: None, finish=finish,
                done=done)


def _merge_plans(plans):
    def run(which):
        def f(src, dst, send, recv):
            o = s = 0
            for p in plans:
                p[which](src[o:o + p['n']], dst[o:o + p['n']], send, recv, off=s)
                o, s = o + p['n'], s + p['nsem']
        return f

    def done(results):
        o = 0
        for p in plans:
            p['done'](results[o:o + p['n']])
            o += p['n']

    return dict(n=sum(p['n'] for p in plans), ins=[t for p in plans for t in p['ins']],
                outs=[t for p in plans for t in p['outs']], nsem=sum(p['nsem'] for p in plans),
                start=run('start'), forward=lambda *refs: None, finish=run('finish'), done=done)


def _swap_sibling(hs, name):
    n = len(hs)

    def body(*refs):
        src, dst = refs[:n], refs[n:2 * n]
        send, recv = refs[2 * n:]
        x, y, c = _me()
        cps = []
        for i in range(n):
            cp = pltpu.make_async_remote_copy(src[i], dst[i], send.at[i], recv.at[i], device_id=(x, y, 1 - c),
                                              device_id_type=MESH)
            cp.start()
            cps.append(cp)
        for cp in cps:
            cp.wait()

    outs = [jax.ShapeDtypeStruct(h.shape, h.dtype) for h in hs]
    return _pc(body, in_specs=[_HBM] * n, out_specs=[_HBM] * n, out_shape=outs,
               scratch_shapes=[pltpu.SemaphoreType.DMA((n,)), pltpu.SemaphoreType.DMA((n,))],
               compiler_params=pltpu.CompilerParams(has_side_effects=True), name=name)(*hs)


ELEMWISE_VMEM_BUDGET = 24 * 1024 * 1024


def _row_tile(r, n, narrays):
    limit = ELEMWISE_VMEM_BUDGET // (2 * 4 * narrays * n)
    best = SUBLANES
    for t in range(16, r + 1, 16):
        if r % t == 0 and t <= limit:
            best = t
    return best


def _add_half(g, r1, cidx, name):
    _, _, r, n = g.shape
    tr = _row_tile(r, n, 3)

    def body(c_ref, g_ref, r_ref, o_ref):
        o_ref[...] = (g_ref[...] + r_ref[...]).astype(GRAD_TRANSIT)

    blk = lambda f: pl.BlockSpec((None, None, tr, n), f)
    gs = pltpu.PrefetchScalarGridSpec(
        num_scalar_prefetch=1, grid=(4, r // tr),
        in_specs=[blk(lambda j, i, c: (j, c[0], i, 0)), blk(lambda j, i, c: (j, 0, i, 0))],
        out_specs=pl.BlockSpec((None, tr, n), lambda j, i, c: (j, i, 0)))
    return _pc(body, grid_spec=gs, out_shape=jax.ShapeDtypeStruct((4, r, n), GRAD_TRANSIT),
               compiler_params=_cp(("parallel", "parallel")), name=name)(cidx, g, r1)


def _add_chips(p, r3, jidx, name):
    _, r, n = p.shape
    tr = _row_tile(r, n, 5)

    def body(j_ref, p_ref, a_ref, b_ref, c_ref, o_ref):
        o_ref[...] = ((p_ref[...].astype(f32) + a_ref[...].astype(f32)) + b_ref[...].astype(f32)) + c_ref[...].astype(f32)

    rblk = lambda k: pl.BlockSpec((None, None, tr, n), lambda i, j: (k, 0, i, 0))
    gs = pltpu.PrefetchScalarGridSpec(
        num_scalar_prefetch=1, grid=(r // tr,),
        in_specs=[pl.BlockSpec((None, tr, n), lambda i, j: (j[0], i, 0)), rblk(0), rblk(1), rblk(2)],
        out_specs=pl.BlockSpec((tr, n), lambda i, j: (i, 0)))
    return _pc(body, grid_spec=gs, out_shape=jax.ShapeDtypeStruct((r, n), f32),
               compiler_params=_cp(("parallel",)), name=name)(jidx, p, r3, r3, r3)


def _adamw_halves(w, mine, theirs, m, v, cidx, name):
    L, r, n = w.shape
    r2 = r // 2
    tr = _row_tile(r2, n, 11)
    c1 = 1.0 / (1.0 - ADAM_B1 ** ADAM_STEP)
    c2 = 1.0 / (1.0 - ADAM_B2 ** ADAM_STEP)

    def body(c_ref, w_ref, a0_ref, b0_ref, a1_ref, b1_ref, m_ref, v_ref, g_ref, d_ref, mo_ref, vo_ref):
        l, hf = pl.program_id(0), pl.program_id(1)
        own = hf == c_ref[0]
        gv = jnp.where(l == 0, jnp.where(own, a0_ref[...], b0_ref[...]), jnp.where(own, a1_ref[...], b1_ref[...]))
        m2 = ADAM_B1 * m_ref[...] + (1.0 - ADAM_B1) * gv
        v2 = ADAM_B2 * v_ref[...] + (1.0 - ADAM_B2) * (gv * gv)
        g_ref[...] = gv
        d_ref[...] = -ADAM_LR * ((m2 * c1) / (jnp.sqrt(v2 * c2) + ADAM_EPS) + ADAM_WD * w_ref[...])
        mo_ref[...] = m2
        vo_ref[...] = v2

    full = pl.BlockSpec((None, None, tr, n), lambda l, hf, i, c: (l, hf, i, 0))
    half = pl.BlockSpec((tr, n), lambda l, hf, i, c: (i, 0))
    gs = pltpu.PrefetchScalarGridSpec(num_scalar_prefetch=1, grid=(L, 2, r2 // tr),
                                      in_specs=[full, half, half, half, half, full, full], out_specs=[full] * 4)
    four = lambda t: t.reshape(L, 2, r2, n)
    outs = _pc(body, grid_spec=gs, out_shape=[jax.ShapeDtypeStruct((L, 2, r2, n), f32)] * 4,
               compiler_params=_cp(("parallel", "parallel", "parallel")), name=name)(
        cidx, four(w), mine[0], theirs[0], mine[1], theirs[1], four(m), four(v))
    return [t.reshape(w.shape) for t in outs]


def _adamw_whole(w, g, m, v, name):
    c1 = 1.0 / (1.0 - ADAM_B1 ** ADAM_STEP)
    c2 = 1.0 / (1.0 - ADAM_B2 ** ADAM_STEP)

    def body(w_ref, g_ref, m_ref, v_ref, d_ref, mo_ref, vo_ref):
        gv = g_ref[...]
        m2 = ADAM_B1 * m_ref[...] + (1.0 - ADAM_B1) * gv
        v2 = ADAM_B2 * v_ref[...] + (1.0 - ADAM_B2) * (gv * gv)
        d_ref[...] = -ADAM_LR * ((m2 * c1) / (jnp.sqrt(v2 * c2) + ADAM_EPS) + ADAM_WD * w_ref[...])
        mo_ref[...] = m2
        vo_ref[...] = v2

    return _pc(body, out_shape=[jax.ShapeDtypeStruct(w.shape, f32)] * 3, name=name)(w, g, m, v)


def _full_from_gathered(name, t):
    r, n = 2 * t.shape[2], t.shape[3]
    if SHARDED[name] == 1 or name in TRANSPOSED:
        return t.reshape(4 * r, n)
    return t.reshape(4, r, n).transpose(1, 0, 2).reshape(r, 4 * n)


def _shard_major(name, g):
    R, C = g.shape
    if SHARDED[name] == 1 or name in TRANSPOSED:
        return g.reshape(4, 2, R // 8, C)
    return g.reshape(R, 4, C // 4).transpose(1, 0, 2).reshape(4, 2, R // 2, C // 4)


_SMALL_ROWS = 288


def _pack_small(d):
    flat = jnp.concatenate([d[n].reshape(-1) for n in SMALL])
    total = 2 * 4 * _SMALL_ROWS * LANES
    flat = jnp.concatenate([flat, jnp.zeros((total - flat.shape[0],), f32)])
    return flat.reshape(4, 2, _SMALL_ROWS, LANES)


def _unpack_small(t, like):
    flat = t.reshape(-1)
    out, off = {}, 0
    for n in SMALL:
        sz = math.prod(like[n].shape)
        out[n] = flat[off:off + sz].reshape(like[n].shape)
        off += sz
    return out


def kernel(x, mem, positions, norm_mix_g, w_in, q_norm_g, w_uq, kv_norm_g, w_ukv, ssm_lambda_re, ssm_lambda_im, ssm_log_dt, ssm_b_re, ssm_b_im, ssm_c_re, ssm_c_im, ssm_d, ssm_w_glu, ssm_b_glu, attn_out_g, ssm_out_g, w_out, norm_x_g, mem_norm_g, w_xq, w_xkv, w_xo, norm_ffn_g, w_gate, w_up, w_down, final_norm_g, loss_target, m_norm_mix_g, m_w_in, m_q_norm_g, m_w_uq, m_kv_norm_g, m_w_ukv, m_ssm_lambda_re, m_ssm_lambda_im, m_ssm_log_dt, m_ssm_b_re, m_ssm_b_im, m_ssm_c_re, m_ssm_c_im, m_ssm_d, m_ssm_w_glu, m_ssm_b_glu, m_attn_out_g, m_ssm_out_g, m_w_out, m_norm_x_g, m_mem_norm_g, m_w_xq, m_w_xkv, m_w_xo, m_norm_ffn_g, m_w_gate, m_w_up, m_w_down, m_final_norm_g, v_norm_mix_g, v_w_in, v_q_norm_g, v_w_uq, v_kv_norm_g, v_w_ukv, v_ssm_lambda_re, v_ssm_lambda_im, v_ssm_log_dt, v_ssm_b_re, v_ssm_b_im, v_ssm_c_re, v_ssm_c_im, v_ssm_d, v_ssm_w_glu, v_ssm_b_glu, v_attn_out_g, v_ssm_out_g, v_w_out, v_norm_x_g, v_mem_norm_g, v_w_xq, v_w_xkv, v_w_xo, v_norm_ffn_g, v_w_gate, v_w_up, v_w_down, v_final_norm_g):
    given = dict(locals())
    swap = lambda n, t: jnp.swapaxes(t, *TRANSPOSED[n]) if n in TRANSPOSED else t
    w = {n: swap(n, given[n]) for n in WEIGHTS}
    m = {n: swap(n, given["m_" + n]) for n in WEIGHTS}
    v = {n: swap(n, given["v_" + n]) for n in WEIGHTS}
    big = list(SHARDED)

    shards = {n: w[n].astype(bf16) for n in big}
    early = [n for n in big if n in EARLY_WEIGHTS]
    rest = [n for n in big if n not in EARLY_WEIGHTS]

    def full(names, results, l):
        return {n: _full_from_gathered(n, _fill_own(t, shards[n][l], False)) for n, t in zip(names, results)}

    first = _run_plan(_gather_plan([shards[n][l] for l in range(DEPTH) for n in early], False), "allgather_weights_early")
    W = {n: [None] * DEPTH for n in big}
    for l in range(DEPTH):
        for n, t in full(early, first[l * len(early):(l + 1) * len(early)], l).items():
            W[n][l] = t
    gathers = [(_gather_plan([shards[n][l] for n in rest], False), functools.partial(full, rest, l=l))
               for l in range(DEPTH)]
    W.update({n: w[n] for n in SMALL})

    cidx = lax.axis_index("c").astype(jnp.int32).reshape(1)
    jidx = (2 * lax.axis_index("x") + lax.axis_index("y")).astype(jnp.int32).reshape(1)
    ffn = [n for n in big if n in FFN_WEIGHTS]
    tail = [n for n in big if n in EARLY_WEIGHTS]
    mid = [n for n in big if n not in FFN_WEIGHTS and n not in EARLY_WEIGHTS]
    sums, got = {}, {}
    ready = []

    def exchange(names, tag, l, G, extra=()):
        keys = [(l, n) for n in names] + [(l, n) for n, _ in extra]
        gs = [_shard_major(n, G[n]) for n in names] + [g for _, g in extra]

        def done(r1):
            ps = [_add_half(g, r, cidx, f"grad_add_half_{tag}_{k[1]}") for k, g, r in zip(keys, gs, r1)]
            sums.update(zip(keys, ps))
            ready.append(_scatter_plan(ps, done=lambda results: got.update(zip(keys, results))))

        return _exchange_plan(gs, done)

    def take():
        plans = list(ready)
        ready.clear()
        return plans

    def now(plan, name):
        plan['done'](_run_plan(plan, name))

    hook = {'ffn': lambda l, G: exchange(ffn, f"l{l}_ffn", l, G), 'mid': lambda l, G: exchange(mid, f"l{l}_mid", l, G),
            'rest': lambda l, G: now(exchange(tail, f"l{l}_tail", l, G), f"grad_exchange_halves_l{l}_tail"),
            'take': take}
    loss, dx, grads = _local_step(x[0], mem[0], positions[0], loss_target[0], W, gathers=gathers, hook=hook)
    loss = lax.psum(loss, ("x", "y", "c"))

    now(exchange(tail, "l0_tail", 0, {n: grads[n][0] for n in tail}, extra=[("small", _pack_small(grads))]),
        "grad_exchange_halves_l0_tail")
    now(_merge_plans(take()), "grad_scatter_chips_l0_tail")
    keys = list(sums)
    hs = dict(zip(keys, [_add_chips(sums[k], got[k], jidx, f"grad_add_chips_l{k[0]}_{k[1]}") for k in keys]))
    ts = dict(zip(keys, _swap_sibling([hs[k] for k in keys], "grad_swap_sibling")))

    out_g, out_d, out_m, out_v = {}, {}, {}, {}
    for n in big:
        mine, theirs = [hs[(l, n)] for l in range(DEPTH)], [ts[(l, n)] for l in range(DEPTH)]
        out_g[n], out_d[n], out_m[n], out_v[n] = _adamw_halves(w[n], mine, theirs, m[n], v[n], cidx, f"adamw_{n}")
    both = jnp.stack([hs[(0, "small")], ts[(0, "small")]])
    piece = jnp.where(cidx[0] == 0, both, both[::-1]).reshape(2 * _SMALL_ROWS, LANES)
    gsm = _fill_own(_run_plan(_gather_plan([piece], False), "allgather_small")[0], piece, False)
    out_g.update(_unpack_small(gsm, w))
    for n in SMALL:
        two = lambda t: t.reshape(1, -1) if t.ndim == 1 else t
        d_, m_, v_ = _adamw_whole(two(w[n]), two(out_g[n]), two(m[n]), two(v[n]), f"adamw_{n}")
        out_d[n], out_m[n], out_v[n] = (t.reshape(w[n].shape) for t in (d_, m_, v_))

    outs = [[swap(n, d[n]) for n in WEIGHTS] for d in (out_g, out_d, out_m, out_v)]
    return (loss, dx.reshape(x.shape), *outs[0], *outs[1], *outs[2], *outs[3])
```

```python
import functools
import math

import jax
import jax.numpy as jnp
from jax import lax
from jax.experimental import pallas as pl
from jax.experimental.pallas import tpu as pltpu

f32, bf16 = jnp.float32, jnp.bfloat16

D_MODEL = 1024
DEPTH = 2
MLA_HEADS = 8
QK_NOPE = 64
QK_ROPE = 32
V_HEAD = 64
Q_LORA = 256
KV_LORA = 128
MLA_WIDTH = MLA_HEADS * V_HEAD
ROPE_THETA = 10000.0
SSM_WIDTH = 512
SSM_GROUP = 16
SSM_GROUPS = 32
SSM_STATE = 64
IN_WIDTH = Q_LORA + KV_LORA + QK_ROPE + SSM_WIDTH
X_HEADS = 4
X_HEAD_DIM = D_MODEL // X_HEADS
D_FF = 2816
EPS = 1e-6
ADAM_LR, ADAM_B1, ADAM_B2, ADAM_EPS, ADAM_WD, ADAM_STEP = 0.001, 0.9, 0.999, 1e-08, 0.01, 10

LANES = 128
SUBLANES = 8
HEAD_PAD = 128
MLA_PAD = MLA_HEADS * HEAD_PAD
SSM_MACRO = 4
MACRO_CH = SSM_WIDTH // SSM_MACRO
MACRO_ST = SSM_GROUPS // SSM_MACRO * SSM_STATE
VMEM_LIMIT = 56 * 1024 * 1024
GRAD_TRANSIT = bf16

WEIGHTS = ['norm_mix_g', 'w_in', 'q_norm_g', 'w_uq', 'kv_norm_g', 'w_ukv', 'ssm_lambda_re', 'ssm_lambda_im',
           'ssm_log_dt', 'ssm_b_re', 'ssm_b_im', 'ssm_c_re', 'ssm_c_im', 'ssm_d', 'ssm_w_glu', 'ssm_b_glu',
           'attn_out_g', 'ssm_out_g', 'w_out', 'norm_x_g', 'mem_norm_g', 'w_xq', 'w_xkv', 'w_xo', 'norm_ffn_g',
           'w_gate', 'w_up', 'w_down', 'final_norm_g']
SHARDED = {'w_in': 1, 'w_uq': 2, 'w_ukv': 2, 'ssm_w_glu': 1, 'w_out': 1, 'w_xq': 1, 'w_xkv': 2, 'w_xo': 1,
           'w_gate': 2, 'w_up': 2, 'w_down': 1}
SMALL = [n for n in WEIGHTS if n not in SHARDED]
EARLY_WEIGHTS = ('w_in', 'w_uq', 'w_ukv')
FFN_WEIGHTS = ('w_gate', 'w_up', 'w_down')
TRANSPOSED = {'w_gate': (1, 2), 'w_up': (1, 2), 'ssm_b_re': (2, 3), 'ssm_b_im': (2, 3)}
MESH = pl.DeviceIdType.MESH


def _pc(body, **kw):
    return pl.pallas_call(body, **kw)


def _pick(n, prefs):
    for p in prefs:
        if n % p == 0:
            return p
    return n


def _cp(sem=None):
    return pltpu.CompilerParams(dimension_semantics=sem, vmem_limit_bytes=VMEM_LIMIT)


_TILE_CANDS = (1024, 1408, 512, 256, 128)
MM_VMEM_BUDGET = 40 * 1024 * 1024


def _mm_tiles(M, K, N, a_bytes, b_bytes, o_bytes, npair, has_res, need_acc):
    best = None
    for tm in _TILE_CANDS:
        for tk in _TILE_CANDS:
            if M % tm or K % tk:
                continue
            vm = npair * (2 * tm * tk * a_bytes + 2 * tk * N * b_bytes) + 2 * tm * N * o_bytes
            vm += tm * N * 4 * (1 + need_acc + 2 * has_res)
            if a_bytes == 4:
                vm += npair * tm * tk * 2
            if b_bytes == 4:
                vm += npair * tk * N * 2
            if vm <= MM_VMEM_BUDGET and (best is None or tm * tk > best[0]):
                best = (tm * tk, tm, tk)
    if best is None:
        return _pick(M, (256, 128)), _pick(K, (256, 128))
    return best[1], best[2]


def _mm(pairs, mode, out_dtype, res=None, name="mm"):
    a0, b0 = pairs[0]
    if mode == 'nn':
        (M, K), N = a0.shape, b0.shape[1]
        dims = (((1,), (0,)), ((), ()))
    elif mode == 'nt':
        (M, K), N = a0.shape, b0.shape[0]
        dims = (((1,), (1,)), ((), ()))
    else:
        (K, M), N = a0.shape, b0.shape[1]
        dims = (((0,), (0,)), ((), ()))
    npair = len(pairs)
    has_res = res is not None
    direct = out_dtype == f32
    tm, tk = _mm_tiles(M, K, N, a0.dtype.itemsize, b0.dtype.itemsize, jnp.dtype(out_dtype).itemsize, npair, has_res,
                       not direct)
    nk = K // tk

    def body(*refs):
        ins = refs[:2 * npair]
        res_ref = refs[2 * npair] if has_res else None
        o_ref = refs[2 * npair + has_res]
        acc = o_ref if direct else refs[2 * npair + has_res + 1]
        k = pl.program_id(1)
        s = None
        for p in range(npair):
            d = lax.dot_general(ins[2 * p][...].astype(bf16), ins[2 * p + 1][...].astype(bf16), dims,
                                preferred_element_type=f32)
            s = d if s is None else s + d

        @pl.when(k == 0)
        def _():
            acc[...] = s

        @pl.when(k > 0)
        def _():
            acc[...] += s

        if has_res or not direct:
            @pl.when(k == nk - 1)
            def _():
                r = acc[...]
                if has_res:
                    r = r + res_ref[...]
                o_ref[...] = r.astype(out_dtype)

    if mode == 'nn':
        a_spec = pl.BlockSpec((tm, tk), lambda i, k: (i, k))
        b_spec = pl.BlockSpec((tk, N), lambda i, k: (k, 0))
    elif mode == 'nt':
        a_spec = pl.BlockSpec((tm, tk), lambda i, k: (i, k))
        b_spec = pl.BlockSpec((N, tk), lambda i, k: (0, k))
    else:
        a_spec = pl.BlockSpec((tk, tm), lambda i, k: (k, i))
        b_spec = pl.BlockSpec((tk, N), lambda i, k: (k, 0))
    o_spec = pl.BlockSpec((tm, N), lambda i, k: (i, 0))
    in_specs = [a_spec, b_spec] * npair + ([o_spec] if has_res else [])
    args = [t for p in pairs for t in p] + ([res] if has_res else [])
    return _pc(body, grid=(M // tm, nk), in_specs=in_specs, out_specs=o_spec,
               out_shape=jax.ShapeDtypeStruct((M, N), out_dtype),
               scratch_shapes=[] if direct else [pltpu.VMEM((tm, N), f32)],
               compiler_params=_cp(("parallel", "arbitrary")), name=name)(*args)


def _rms_fwd(x, g, *, col0=0, width=None, n_valid=None, out_dtype=bf16, name="rms_fwd"):
    S = x.shape[0]
    width = width or x.shape[1]
    n_valid = n_valid or width
    ts = _pick(S, (512, 256, 128))
    cb = col0 // width

    def body(x_ref, g_ref, o_ref):
        xv = x_ref[...]
        ms = jnp.sum(xv * xv, axis=-1, keepdims=True) * (1.0 / n_valid)
        o_ref[...] = (xv * lax.rsqrt(ms + EPS) * g_ref[...]).astype(out_dtype)

    return _pc(body, grid=(S // ts,),
               in_specs=[pl.BlockSpec((ts, width), lambda i: (i, cb)), pl.BlockSpec((1, width), lambda i: (0, 0))],
               out_specs=pl.BlockSpec((ts, width), lambda i: (i, 0)),
               out_shape=jax.ShapeDtypeStruct((S, width), out_dtype),
               compiler_params=_cp(("parallel",)), name=name)(x, g.reshape(1, width))


def _rms_bwd(x, g, dy, *, col0=0, dcol0=0, width=None, n_valid=None, res=None, out_dtype=f32, delta=False,
             name="rms_bwd"):
    S = x.shape[0]
    width = width or x.shape[1]
    n_valid = n_valid or width
    ts = _pick(S, (512, 256, 128))
    cb, dcb = col0 // width, dcol0 // width
    has_res = res is not None

    def body(*refs):
        x_ref, g_ref, dy_ref = refs[:3]
        res_ref = refs[3] if has_res else None
        outs = refs[3 + has_res:]
        dx_ref, dg_ref = outs[0], outs[1]
        i = pl.program_id(0)
        xv = x_ref[...]
        gv = g_ref[...]
        dyv = dy_ref[...].astype(f32)
        rstd = lax.rsqrt(jnp.sum(xv * xv, axis=-1, keepdims=True) * (1.0 / n_valid) + EPS)
        xh = xv * rstd
        dxh = dyv * gv
        mean = jnp.sum(dxh * xh, axis=-1, keepdims=True) * (1.0 / n_valid)
        dx = rstd * (dxh - xh * mean)
        if delta:
            d_ref = outs[2]
            for h in range(width // LANES):
                sl = slice(h * LANES, (h + 1) * LANES)
                dsum = jnp.sum(dx[:, sl] * xv[:, sl], axis=-1, keepdims=True)
                d_ref[:, sl] = jnp.broadcast_to(dsum, (ts, LANES))
        if has_res:
            dx = dx + res_ref[...]
        dx_ref[...] = dx.astype(out_dtype)

        @pl.when(i == 0)
        def _():
            dg_ref[...] = jnp.zeros_like(dg_ref)

        dg_ref[...] += jnp.sum(dyv * xh, axis=0, keepdims=True)

    blk = lambda c: pl.BlockSpec((ts, width), lambda i: (i, c))
    in_specs = [blk(cb), pl.BlockSpec((1, width), lambda i: (0, 0)), blk(dcb)] + ([blk(0)] if has_res else [])
    out_specs = [blk(0), pl.BlockSpec((1, width), lambda i: (0, 0))] + ([blk(0)] if delta else [])
    out_shape = [jax.ShapeDtypeStruct((S, width), out_dtype), jax.ShapeDtypeStruct((1, width), f32)] + (
        [jax.ShapeDtypeStruct((S, width), f32)] if delta else [])
    args = [x, g.reshape(1, width), dy] + ([res] if has_res else [])
    return _pc(body, grid=(S // ts,), in_specs=in_specs, out_specs=out_specs, out_shape=out_shape,
               compiler_params=_cp(("arbitrary",)), name=name)(*args)


def _loss_head(h, g, target, name="loss_head"):
    S, D = h.shape
    ts = _pick(S, (512, 256, 128))

    def body(h_ref, g_ref, t_ref, dh_ref, dg_ref, loss_ref):
        i = pl.program_id(0)
        xv = h_ref[...]
        gv = g_ref[...]
        rstd = lax.rsqrt(jnp.sum(xv * xv, axis=-1, keepdims=True) * (1.0 / D) + EPS)
        xh = xv * rstd
        err = xh * gv - t_ref[...]
        dyv = err * (1.0 / D)
        dxh = dyv * gv
        mean = jnp.sum(dxh * xh, axis=-1, keepdims=True) * (1.0 / D)
        dh_ref[...] = rstd * (dxh - xh * mean)

        @pl.when(i == 0)
        def _():
            dg_ref[...] = jnp.zeros_like(dg_ref)
            loss_ref[...] = jnp.zeros_like(loss_ref)

        dg_ref[...] += jnp.sum(dyv * xh, axis=0, keepdims=True)
        part = jnp.sum(jnp.sum(err * err, axis=-1, keepdims=True), axis=0, keepdims=True) * (0.5 / D)
        loss_ref[...] += jnp.broadcast_to(part, (1, LANES))

    blk = pl.BlockSpec((ts, D), lambda i: (i, 0))
    row = pl.BlockSpec((1, D), lambda i: (0, 0))
    return _pc(body, grid=(S // ts,), in_specs=[blk, row, blk],
               out_specs=[blk, row, pl.BlockSpec((1, LANES), lambda i: (0, 0))],
               out_shape=[jax.ShapeDtypeStruct((S, D), f32), jax.ShapeDtypeStruct((1, D), f32),
                          jax.ShapeDtypeStruct((1, LANES), f32)],
               compiler_params=_cp(("arbitrary",)), name=name)(h, g.reshape(1, D), target)


def _rope_apply(x, tc, s1, s2):
    return x * tc + pltpu.roll(x, LANES - 16, 1) * s1 + pltpu.roll(x, 16, 1) * s2


def _rope_apply_t(dy, tc, s1, s2):
    return dy * tc + pltpu.roll(dy * s1, 16, 1) + pltpu.roll(dy * s2, LANES - 16, 1)


def _rope_fwd(q, kv, proj, tabs, name="rope_fwd"):
    S = q.shape[0]
    ts = _pick(S, (512, 256, 128))
    scale = (QK_NOPE + QK_ROPE) ** -0.5

    def body(q_ref, kk_ref, kvv_ref, kr_ref, tc_ref, s1_ref, s2_ref, qh_ref, kh_ref, vh_ref):
        tc, s1, s2 = tc_ref[...], s1_ref[...], s2_ref[...]
        krr = _rope_apply(pltpu.roll(kr_ref[...], QK_NOPE, 1), tc, s1, s2)
        for h in range(MLA_HEADS):
            sl = slice(h * HEAD_PAD, (h + 1) * HEAD_PAD)
            qh_ref[:, sl] = (_rope_apply(q_ref[:, sl], tc, s1, s2) * scale).astype(bf16)
            kh_ref[:, sl] = (kk_ref[:, sl] + krr).astype(bf16)
        vh_ref[...] = kvv_ref[...].astype(bf16)

    wide = lambda c: pl.BlockSpec((ts, MLA_PAD), lambda i: (i, c))
    tab = pl.BlockSpec((ts, LANES), lambda i: (i, 0))
    return _pc(body, grid=(S // ts,),
               in_specs=[wide(0), wide(0), wide(1), pl.BlockSpec((ts, LANES), lambda i: (i, 3)), tab, tab, tab],
               out_specs=[wide(0)] * 3, out_shape=[jax.ShapeDtypeStruct((S, MLA_PAD), bf16)] * 3,
               compiler_params=_cp(("parallel",)), name=name)(q, kv, kv, proj, *tabs)


def _rope_bwd(dqh, dkh, dvh, tabs, name="rope_bwd"):
    S = dqh.shape[0]
    ts = _pick(S, (512, 256, 128))
    scale = (QK_NOPE + QK_ROPE) ** -0.5

    def body(dq_ref, dk_ref, dv_ref, tc_ref, s1_ref, s2_ref, oq_ref, okv_ref, okr_ref):
        tc, s1, s2 = tc_ref[...], s1_ref[...], s2_ref[...]
        ksum = None
        for h in range(MLA_HEADS):
            sl = slice(h * HEAD_PAD, (h + 1) * HEAD_PAD)
            oq_ref[:, sl] = (_rope_apply_t(dq_ref[:, sl], tc, s1, s2) * scale).astype(bf16)
            dk = dk_ref[:, sl]
            okv_ref[:, sl] = dk.astype(bf16)
            ksum = dk if ksum is None else ksum + dk
        okv_ref[:, MLA_PAD:] = dv_ref[...].astype(bf16)
        dkr = pltpu.roll(_rope_apply_t(ksum, tc, s1, s2), LANES - QK_NOPE, 1)
        lane = lax.broadcasted_iota(jnp.int32, (ts, LANES), 1)
        okr_ref[...] = jnp.where(lane < QK_ROPE, dkr, 0.0).astype(bf16)

    wide = pl.BlockSpec((ts, MLA_PAD), lambda i: (i, 0))
    tab = pl.BlockSpec((ts, LANES), lambda i: (i, 0))
    return _pc(body, grid=(S // ts,), in_specs=[wide, wide, wide, tab, tab, tab],
               out_specs=[wide, pl.BlockSpec((ts, 2 * MLA_PAD), lambda i: (i, 0)), tab],
               out_shape=[jax.ShapeDtypeStruct((S, MLA_PAD), bf16), jax.ShapeDtypeStruct((S, 2 * MLA_PAD), bf16),
                          jax.ShapeDtypeStruct((S, LANES), bf16)],
               compiler_params=_cp(("parallel",)), name=name)(dqh, dkh, dvh, *tabs)


ATT_BLK = 1024
_DIAG_QUARTERS = ((0, 0), (1, 0), (1, 1))


def _attn_fwd(qh, kh, vh, plan=None, name="attn_fwd"):
    S = qh.shape[0]
    tq = tk = min(S, ATT_BLK)
    nq, nk = S // tq, S // tk
    npl = plan['n'] if plan else 0

    def body(*refs):
        q_ref, k_ref, v_ref = refs[:3]
        o_ref, lse_ref = refs[3 + npl:5 + npl]
        m_sc, l_sc, acc_sc = refs[5 + 2 * npl:8 + 2 * npl]
        h, i, j = pl.program_id(0), pl.program_id(1), pl.program_id(2)
        if plan:
            pargs = (refs[3:3 + npl], refs[5 + npl:5 + 2 * npl], refs[8 + 2 * npl], refs[9 + 2 * npl])
            first = (i == 0) & (j == 0)
            pl.when((h == 0) & first)(functools.partial(plan['start'], *pargs))
            pl.when((h == (3 * MLA_HEADS) // 4) & first)(functools.partial(plan['forward'], *pargs))
            pl.when((h == MLA_HEADS - 1) & (i == nq - 1) & (j == nk - 1))(functools.partial(plan['finish'], *pargs))

        @pl.when(j == 0)
        def _():
            m_sc[...] = jnp.full_like(m_sc, -1e30)
            l_sc[...] = jnp.zeros_like(l_sc)
            acc_sc[...] = jnp.zeros_like(acc_sc)

        def part(rows, cols, n, masked):
            s = lax.dot_general(q_ref[rows, :], k_ref[cols, :], (((1,), (1,)), ((), ())), preferred_element_type=f32)
            if masked:
                row = lax.broadcasted_iota(jnp.int32, (n, n), 0)
                col = lax.broadcasted_iota(jnp.int32, (n, n), 1)
                s = jnp.where(col <= row, s, -1e30)
            m_prev = m_sc[rows, :]
            m_new = jnp.maximum(m_prev, jnp.max(s, axis=-1, keepdims=True))
            alpha = jnp.exp(m_prev - m_new)
            p = jnp.exp(s - m_new)
            l_sc[rows, :] = alpha * l_sc[rows, :] + jnp.sum(p, axis=-1, keepdims=True)
            acc_sc[rows, :] = alpha * acc_sc[rows, :] + jnp.dot(p.astype(bf16), v_ref[cols, :],
                                                                  preferred_element_type=f32)
            m_sc[rows, :] = m_new

        whole = (slice(0, tq), slice(0, tk), tq)
        pl.when(j < i)(functools.partial(part, *whole, False))
        pl.when(j == i)(functools.partial(part, *whole, True))

        @pl.when(j == nk - 1)
        def _():
            l = l_sc[...]
            o_ref[...] = acc_sc[...] / l
            lse_ref[...] = jnp.broadcast_to(m_sc[...] + jnp.log(l), (tq, LANES))

    qspec = pl.BlockSpec((tq, HEAD_PAD), lambda h, i, j: (i, h))
    kspec = pl.BlockSpec((tk, HEAD_PAD), lambda h, i, j: (jnp.minimum(j, i), h))
    anyspec = pl.BlockSpec(memory_space=pl.ANY)
    scratch = [pltpu.VMEM((tq, 1), f32), pltpu.VMEM((tq, 1), f32), pltpu.VMEM((tq, HEAD_PAD), f32)]
    if plan:
        scratch += [pltpu.SemaphoreType.DMA((plan['nsem'],)), pltpu.SemaphoreType.DMA((plan['nsem'],))]
    outs = _pc(body, grid=(MLA_HEADS, nq, nk), in_specs=[qspec, kspec, kspec] + [anyspec] * npl,
               out_specs=[qspec, qspec] + [anyspec] * npl,
               out_shape=[jax.ShapeDtypeStruct((S, MLA_PAD), f32)] * 2 + (plan['outs'] if plan else []),
               scratch_shapes=scratch,
               compiler_params=_cp(("arbitrary", "arbitrary", "arbitrary") if plan else ("parallel", "parallel", "arbitrary")),
               name=name)(qh, kh, vh, *(plan['ins'] if plan else []))
    return outs[0], outs[1], outs[2:]


def _attn_bwd(qh, kh, vh, do, lse, delta, plan=None, name="attn_bwd"):
    S = qh.shape[0]
    tq = tk = min(S, ATT_BLK)
    nq, nk = S // tq, S // tk
    npl = plan['n'] if plan else 0

    def body(*refs):
        q_ref, k_ref, v_ref, do_ref, lse_ref, dl_ref = refs[:6]
        dq_ref, dk_ref, dv_ref = refs[6 + npl:9 + npl]
        h, j, i = pl.program_id(0), pl.program_id(1), pl.program_id(2)
        if plan:
            pargs = (refs[6:6 + npl], refs[9 + npl:9 + 2 * npl], refs[9 + 2 * npl], refs[10 + 2 * npl])
            pl.when((h == 0) & (j == 0) & (i == 0))(functools.partial(plan['start'], *pargs))
            pl.when((h == MLA_HEADS - 1) & (j == nk - 1) & (i == nq - 1))(functools.partial(plan['finish'], *pargs))

        @pl.when((j == 0) & (i == 0))
        def _():
            dq_ref[...] = jnp.zeros_like(dq_ref)

        @pl.when(i == 0)
        def _():
            dk_ref[...] = jnp.zeros_like(dk_ref)
            dv_ref[...] = jnp.zeros_like(dv_ref)

        def part(r0, c0, n, masked):
            nt = (((1,), (1,)), ((), ()))
            tn = (((0,), (0,)), ((), ()))
            rows, cols = slice(r0, r0 + n), slice(c0, c0 + n)
            qv, kv_, dov = q_ref[rows, :], k_ref[cols, :], do_ref[rows, :]
            s = lax.dot_general(qv, kv_, nt, preferred_element_type=f32)
            p = jnp.exp(s - lse_ref[rows, :1])
            if masked:
                row = lax.broadcasted_iota(jnp.int32, (n, n), 0)
                col = lax.broadcasted_iota(jnp.int32, (n, n), 1)
                p = jnp.where(col <= row, p, 0.0)
            dp = lax.dot_general(dov, v_ref[cols, :], nt, preferred_element_type=f32)
            ds = (p * (dp - dl_ref[rows, :1])).astype(bf16)
            dv_ref[cols, :] += lax.dot_general(p.astype(bf16), dov, tn, preferred_element_type=f32)
            dk_ref[cols, :] += lax.dot_general(ds, qv, tn, preferred_element_type=f32)
            qrows = pl.ds(pl.multiple_of(i * tq + r0, n), n)
            dq_ref[qrows, :] += jnp.dot(ds, kv_, preferred_element_type=f32)

        def below():
            part(0, 0, tq, False)

        def diagonal():
            for r0, c0 in _DIAG_QUARTERS:
                part(r0 * hq, c0 * hq, hq, r0 == c0)

        hq = tq // 2
        pl.when(i > j)(below)
        pl.when(i == j)(diagonal)

    qspec = pl.BlockSpec((tq, HEAD_PAD), lambda h, j, i: (jnp.maximum(i, j), h))
    kspec = pl.BlockSpec((tk, HEAD_PAD), lambda h, j, i: (j, h))
    colspec = pl.BlockSpec((S, HEAD_PAD), lambda h, j, i: (0, h))
    anyspec = pl.BlockSpec(memory_space=pl.ANY)
    scratch = [pltpu.SemaphoreType.DMA((plan['nsem'],)), pltpu.SemaphoreType.DMA((plan['nsem'],))] if plan else []
    outs = _pc(body, grid=(MLA_HEADS, nk, nq), in_specs=[qspec, kspec, kspec, qspec, qspec, qspec] + [anyspec] * npl,
               out_specs=[colspec, kspec, kspec] + [anyspec] * npl,
               out_shape=[jax.ShapeDtypeStruct((S, MLA_PAD), f32)] * 3 + (plan['outs'] if plan else []),
               scratch_shapes=scratch,
               compiler_params=_cp(("arbitrary" if plan else "parallel", "arbitrary", "arbitrary")),
               name=name)(qh, kh, vh, do, lse, delta, *(plan['ins'] if plan else []))
    return outs[0], outs[1], outs[2], outs[3:]


def _xattn_fwd(q, kv, name="xattn_fwd"):
    S = q.shape[0]
    M = kv.shape[0]
    tq = _pick(S, (256, 128))
    scale = X_HEAD_DIM ** -0.5

    def body(q_ref, kv_ref, o_ref):
        for h in range(X_HEADS):
            sl = slice(h * X_HEAD_DIM, (h + 1) * X_HEAD_DIM)
            k = kv_ref[:, sl]
            v = kv_ref[:, D_MODEL + h * X_HEAD_DIM:D_MODEL + (h + 1) * X_HEAD_DIM]
            s = lax.dot_general(q_ref[:, sl], k, (((1,), (1,)), ((), ())), preferred_element_type=f32) * scale
            e = jnp.exp(s - jnp.max(s, axis=-1, keepdims=True))
            p = e / jnp.sum(e, axis=-1, keepdims=True)
            o_ref[:, sl] = jnp.dot(p.astype(bf16), v, preferred_element_type=f32).astype(bf16)

    blk = pl.BlockSpec((tq, D_MODEL), lambda i: (i, 0))
    return _pc(body, grid=(S // tq,), in_specs=[blk, pl.BlockSpec((M, 2 * D_MODEL), lambda i: (0, 0))],
               out_specs=blk, out_shape=jax.ShapeDtypeStruct((S, D_MODEL), bf16),
               compiler_params=_cp(("parallel",)), name=name)(q, kv)


def _xattn_bwd(q, kv, do, plan=None, name="xattn_bwd"):
    S = q.shape[0]
    M = kv.shape[0]
    tq = _pick(S, (256, 128))
    scale = X_HEAD_DIM ** -0.5
    npl = plan['n'] if plan else 0
    p_in, p_out, p_shapes, p_sems, p_args = _plan_extras(plan)

    def body(*refs):
        q_ref, kv_ref, do_ref = refs[:3]
        dq_ref, dkv_ref = refs[3 + npl:5 + npl]
        i = pl.program_id(0)
        if plan:
            pargs = (refs[3:3 + npl], refs[5 + npl:5 + 2 * npl], refs[5 + 2 * npl], refs[6 + 2 * npl])
            pl.when(i == 0)(functools.partial(plan['start'], *pargs))
            pl.when(i == S // tq - 1)(functools.partial(plan['finish'], *pargs))

        @pl.when(i == 0)
        def _():
            dkv_ref[...] = jnp.zeros_like(dkv_ref)

        nt = (((1,), (1,)), ((), ()))
        tn = (((0,), (0,)), ((), ()))
        for h in range(X_HEADS):
            sl = slice(h * X_HEAD_DIM, (h + 1) * X_HEAD_DIM)
            vsl = slice(D_MODEL + h * X_HEAD_DIM, D_MODEL + (h + 1) * X_HEAD_DIM)
            k, v, qv, dov = kv_ref[:, sl], kv_ref[:, vsl], q_ref[:, sl], do_ref[:, sl]
            s = lax.dot_general(qv, k, nt, preferred_element_type=f32) * scale
            e = jnp.exp(s - jnp.max(s, axis=-1, keepdims=True))
            p = e / jnp.sum(e, axis=-1, keepdims=True)
            dp = lax.dot_general(dov, v, nt, preferred_element_type=f32)
            ds = (p * (dp - jnp.sum(dp * p, axis=-1, keepdims=True)) * scale).astype(bf16)
            dq_ref[:, sl] = jnp.dot(ds, k, preferred_element_type=f32).astype(bf16)
            dkv_ref[:, sl] += lax.dot_general(ds, qv, tn, preferred_element_type=f32)
            dkv_ref[:, vsl] += lax.dot_general(p.astype(bf16), dov, tn, preferred_element_type=f32)

    blk = pl.BlockSpec((tq, D_MODEL), lambda i: (i, 0))
    full = pl.BlockSpec((M, 2 * D_MODEL), lambda i: (0, 0))
    outs = _pc(body, grid=(S // tq,), in_specs=[blk, full, blk] + p_in, out_specs=[blk, full] + p_out,
               out_shape=[jax.ShapeDtypeStruct((S, D_MODEL), bf16), jax.ShapeDtypeStruct((M, 2 * D_MODEL), f32)] + p_shapes,
               scratch_shapes=p_sems, compiler_params=_cp(("arbitrary",)), name=name)(q, kv, do, *p_args)
    return outs[0], outs[1], outs[2:]


ST_CHUNKS = 1


def _apow_init(a_ref, ap_ref, bp_ref, seg):
    P = MACRO_ST
    ar, ai = a_ref[:, :P], a_ref[:, P:]
    pr, pi = ar, ai
    for r in range(seg):
        ap_ref[r:r + 1, :P] = pr
        ap_ref[r:r + 1, P:] = pi
        if r < seg - 1:
            pr, pi = pr * ar - pi * ai, pr * ai + pi * ar
    br, bi = pr, pi
    for k in range(SUBLANES):
        bp_ref[k:k + 1, :P] = pr
        bp_ref[k:k + 1, P:] = pi
        pr, pi = pr * br - pi * bi, pr * bi + pi * br


def _segment_perm(tS):
    seg = tS // SUBLANES
    rows = jnp.arange(tS)
    src = (rows % SUBLANES) * seg + rows // SUBLANES
    return (src[:, None] == jnp.arange(tS)[None, :]).astype(f32)


def _unpermute_rows(pt, v):
    hi = v.astype(bf16)
    r1 = v - hi.astype(f32)
    mid = r1.astype(bf16)
    lo = (r1 - mid.astype(f32)).astype(bf16)
    out = jnp.dot(pt, jnp.concatenate([hi, mid, lo], axis=1), preferred_element_type=f32)
    w = v.shape[1]
    return (out[:, :w] + out[:, w:2 * w]) + out[:, 2 * w:]


def _scan_block(sc_ref, ap_ref, bp_ref, carry_ref, e_ref, seg, reverse):
    P = MACRO_ST
    sgn = -1.0 if reverse else 1.0
    CH = P // ST_CHUNKS
    rid = lax.broadcasted_iota(jnp.int32, (SUBLANES, CH), 0)
    for c in range(ST_CHUNKS):
        lr, li = slice(c * CH, (c + 1) * CH), slice(P + c * CH, P + (c + 1) * CH)
        ar, ai = ap_ref[0:1, lr], sgn * ap_ref[0:1, li]
        xr = xi = None
        for i in range(seg):
            r = seg - 1 - i if reverse else i
            rows = slice(SUBLANES * r, SUBLANES * (r + 1))
            sr, si = sc_ref[rows, lr], sc_ref[rows, li]
            if i == 0:
                xr, xi = sr, si
            else:
                xr, xi = ar * xr - ai * xi + sr, ar * xi + ai * xr + si
                sc_ref[rows, lr] = xr
                sc_ref[rows, li] = xi
        for sh in (1, 2, 4):
            pr, pi = bp_ref[sh - 1:sh, lr], sgn * bp_ref[sh - 1:sh, li]
            if reverse:
                tr = jnp.where(rid < SUBLANES - sh, pltpu.roll(xr, SUBLANES - sh, 0), 0.0)
                ti = jnp.where(rid < SUBLANES - sh, pltpu.roll(xi, SUBLANES - sh, 0), 0.0)
            else:
                tr = jnp.where(rid >= sh, pltpu.roll(xr, sh, 0), 0.0)
                ti = jnp.where(rid >= sh, pltpu.roll(xi, sh, 0), 0.0)
            xr, xi = xr + pr * tr - pi * ti, xi + pr * ti + pi * tr
        if reverse:
            bpr = jnp.zeros((SUBLANES, CH), f32)
            bpi = jnp.zeros((SUBLANES, CH), f32)
            for r in range(SUBLANES):
                bpr = jnp.where(rid == r, bp_ref[SUBLANES - 1 - r:SUBLANES - r, lr], bpr)
                bpi = jnp.where(rid == r, -bp_ref[SUBLANES - 1 - r:SUBLANES - r, li], bpi)
        else:
            bpr, bpi = bp_ref[:, lr], bp_ref[:, li]
        cr, cim = carry_ref[:, lr], carry_ref[:, li]
        xr, xi = xr + bpr * cr - bpi * cim, xi + bpr * cim + bpi * cr
        edge = 0 if reverse else SUBLANES - 1
        carry_ref[:, lr] = jnp.sum(jnp.where(rid == edge, xr, 0.0), axis=0, keepdims=True)
        carry_ref[:, li] = jnp.sum(jnp.where(rid == edge, xi, 0.0), axis=0, keepdims=True)
        if reverse:
            er = jnp.where(rid == SUBLANES - 1, cr, pltpu.roll(xr, SUBLANES - 1, 0))
            ei = jnp.where(rid == SUBLANES - 1, cim, pltpu.roll(xi, SUBLANES - 1, 0))
        else:
            er = jnp.where(rid == 0, cr, pltpu.roll(xr, 1, 0))
            ei = jnp.where(rid == 0, cim, pltpu.roll(xi, 1, 0))
        if e_ref is not None:
            e_ref[:, lr] = er
            e_ref[:, li] = ei
        for i in range(seg):
            r = seg - 1 - i if reverse else i
            rows = slice(SUBLANES * r, SUBLANES * (r + 1))
            pr, pi = ap_ref[i:i + 1, lr], sgn * ap_ref[i:i + 1, li]
            sc_ref[rows, lr] += pr * er - pi * ei
            sc_ref[rows, li] += pr * ei + pi * er


def _ssm_fwd(proj, bm, cm, a, d, name="ssm_fwd"):
    S = proj.shape[0]
    tS = _pick(S, (256, 128))
    nb = S // tS
    P2 = 2 * MACRO_ST
    seg = tS // SUBLANES
    ucol0 = (D_MODEL - SSM_WIDTH) // MACRO_CH

    perm = _segment_perm(tS)

    def body(u_ref, b_ref, c_ref, a_ref, d_ref, pm_ref, pt_ref, y_ref, xc_ref, bu_sc, ap_sc, bp_sc, car_sc):
        t = pl.program_id(1)

        @pl.when(t == 0)
        def _():
            _apow_init(a_ref, ap_sc, bp_sc, seg)
            car_sc[...] = jnp.zeros_like(car_sc)

        uv = u_ref[...]
        up = jnp.dot(pm_ref[...], uv.astype(bf16), preferred_element_type=f32).astype(bf16)
        bu_sc[...] = jnp.dot(up, b_ref[...], preferred_element_type=f32)
        xc_ref[...] = car_sc[...]
        _scan_block(bu_sc, ap_sc, bp_sc, car_sc, None, seg, False)
        yp = jnp.dot(bu_sc[...].astype(bf16), c_ref[...], preferred_element_type=f32)
        y_ref[...] = _unpermute_rows(pt_ref[...], yp) + d_ref[...] * uv

    sq = pl.BlockSpec((tS, tS), lambda m, t: (0, 0))
    return _pc(body, grid=(SSM_MACRO, nb),
               in_specs=[pl.BlockSpec((tS, MACRO_CH), lambda m, t: (t, ucol0 + m)),
                         pl.BlockSpec((None, MACRO_CH, P2), lambda m, t: (m, 0, 0)),
                         pl.BlockSpec((None, P2, MACRO_CH), lambda m, t: (m, 0, 0)),
                         pl.BlockSpec((None, 1, P2), lambda m, t: (m, 0, 0)),
                         pl.BlockSpec((1, MACRO_CH), lambda m, t: (0, m)), sq, sq],
               out_specs=[pl.BlockSpec((tS, MACRO_CH), lambda m, t: (t, m)),
                          pl.BlockSpec((None, None, 1, P2), lambda m, t: (m, t, 0, 0))],
               out_shape=[jax.ShapeDtypeStruct((S, SSM_WIDTH), f32), jax.ShapeDtypeStruct((SSM_MACRO, nb, 1, P2), f32)],
               scratch_shapes=[pltpu.VMEM((tS, P2), f32), pltpu.VMEM((seg, P2), f32),
                               pltpu.VMEM((SUBLANES, P2), f32), pltpu.VMEM((1, P2), f32)],
               compiler_params=_cp(("arbitrary", "arbitrary")), name=name)(
        proj, bm, cm, a, d.reshape(1, SSM_WIDTH), perm.astype(bf16), perm.T.astype(bf16))


def _ssm_bwd(proj, dy, xc, bm, cm, a, d, plan=None, name="ssm_bwd"):
    S = proj.shape[0]
    tS = _pick(S, (256, 128))
    nb = S // tS
    P = MACRO_ST
    P2 = 2 * P
    seg = tS // SUBLANES
    ucol0 = (D_MODEL - SSM_WIDTH) // MACRO_CH

    perm = _segment_perm(tS)
    npl = plan['n'] if plan else 0
    p_in, p_out, p_shapes, p_sems, p_args = _plan_extras(plan)

    def body(*refs):
        u_ref, dy_ref, xc_ref, b_ref, c_ref, a_ref, d_ref, pm_ref, pt_ref = refs[:9]
        du_ref, db_ref, dc_ref, da_ref, dd_ref = refs[9 + npl:14 + npl]
        x_sc, g_sc, ap_sc, bp_sc, e_sc, xcar_sc, gcar_sc = refs[14 + 2 * npl:21 + 2 * npl]
        t = pl.program_id(1)
        if plan:
            mg = pl.program_id(0)
            pargs = (refs[9:9 + npl], refs[14 + npl:14 + 2 * npl], refs[21 + 2 * npl], refs[22 + 2 * npl])
            pl.when((mg == 0) & (t == 0))(functools.partial(plan['start'], *pargs))
            pl.when((mg == SSM_MACRO - 1) & (t == nb - 1))(functools.partial(plan['finish'], *pargs))

        @pl.when(t == 0)
        def _():
            _apow_init(a_ref, ap_sc, bp_sc, seg)
            gcar_sc[...] = jnp.zeros_like(gcar_sc)
            db_ref[...] = jnp.zeros_like(db_ref)
            dc_ref[...] = jnp.zeros_like(dc_ref)
            da_ref[...] = jnp.zeros_like(da_ref)
            dd_ref[...] = jnp.zeros_like(dd_ref)

        nt = (((1,), (1,)), ((), ()))
        tn = (((0,), (0,)), ((), ()))
        uv = u_ref[...]
        dyv = dy_ref[...]
        pm = pm_ref[...]
        ub = jnp.dot(pm, uv.astype(bf16), preferred_element_type=f32).astype(bf16)
        dyb = jnp.dot(pm, dyv.astype(bf16), preferred_element_type=f32).astype(bf16)
        x_sc[...] = jnp.dot(ub, b_ref[...], preferred_element_type=f32)
        xcar_sc[...] = xc_ref[...]
        _scan_block(x_sc, ap_sc, bp_sc, xcar_sc, e_sc, seg, False)
        g_sc[...] = lax.dot_general(dyb, c_ref[...], nt, preferred_element_type=f32)
        _scan_block(g_sc, ap_sc, bp_sc, gcar_sc, None, seg, True)
        xv = x_sc[...]
        gv = g_sc[...]
        gb = gv.astype(bf16)
        dc_ref[...] += lax.dot_general(xv.astype(bf16), dyb, tn, preferred_element_type=f32)
        db_ref[...] += lax.dot_general(ub, gb, tn, preferred_element_type=f32)
        dup = lax.dot_general(gb, b_ref[...], nt, preferred_element_type=f32)
        du_ref[...] = _unpermute_rows(pt_ref[...], dup) + d_ref[...] * dyv
        dd_ref[...] += jnp.sum(dyv * uv, axis=0, keepdims=True)
        xp = jnp.concatenate([e_sc[...], xv[:tS - SUBLANES]], axis=0)
        xpr, xpi, ggr, ggi = xp[:, :P], xp[:, P:], gv[:, :P], gv[:, P:]
        da_ref[:, :P] += jnp.sum(ggr * xpr + ggi * xpi, axis=0, keepdims=True)
        da_ref[:, P:] += jnp.sum(ggi * xpr - ggr * xpi, axis=0, keepdims=True)

    rev = lambda t: nb - 1 - t
    outs = _pc(body, grid=(SSM_MACRO, nb),
               in_specs=[pl.BlockSpec((tS, MACRO_CH), lambda m, t: (rev(t), ucol0 + m)),
                         pl.BlockSpec((tS, MACRO_CH), lambda m, t: (rev(t), m)),
                         pl.BlockSpec((None, None, 1, P2), lambda m, t: (m, rev(t), 0, 0)),
                         pl.BlockSpec((None, MACRO_CH, P2), lambda m, t: (m, 0, 0)),
                         pl.BlockSpec((None, P2, MACRO_CH), lambda m, t: (m, 0, 0)),
                         pl.BlockSpec((None, 1, P2), lambda m, t: (m, 0, 0)),
                         pl.BlockSpec((1, MACRO_CH), lambda m, t: (0, m)),
                         pl.BlockSpec((tS, tS), lambda m, t: (0, 0)), pl.BlockSpec((tS, tS), lambda m, t: (0, 0))] + p_in,
               out_specs=[pl.BlockSpec((tS, MACRO_CH), lambda m, t: (rev(t), m)),
                          pl.BlockSpec((None, MACRO_CH, P2), lambda m, t: (m, 0, 0)),
                          pl.BlockSpec((None, P2, MACRO_CH), lambda m, t: (m, 0, 0)),
                          pl.BlockSpec((None, 1, P2), lambda m, t: (m, 0, 0)),
                          pl.BlockSpec((1, MACRO_CH), lambda m, t: (0, m))] + p_out,
               out_shape=[jax.ShapeDtypeStruct((S, SSM_WIDTH), f32),
                          jax.ShapeDtypeStruct((SSM_MACRO, MACRO_CH, P2), f32),
                          jax.ShapeDtypeStruct((SSM_MACRO, P2, MACRO_CH), f32),
                          jax.ShapeDtypeStruct((SSM_MACRO, 1, P2), f32),
                          jax.ShapeDtypeStruct((1, SSM_WIDTH), f32)] + p_shapes,
               scratch_shapes=[pltpu.VMEM((tS, P2), f32), pltpu.VMEM((tS, P2), f32),
                               pltpu.VMEM((seg, P2), f32), pltpu.VMEM((SUBLANES, P2), f32), pltpu.VMEM((SUBLANES, P2), f32),
                               pltpu.VMEM((1, P2), f32), pltpu.VMEM((1, P2), f32)] + p_sems,
               compiler_params=_cp(("arbitrary", "arbitrary")), name=name)(
        proj, dy, xc, bm, cm, a, d.reshape(1, SSM_WIDTH), perm.astype(bf16), perm.T.astype(bf16), *p_args)
    return outs[0], outs[1], outs[2], outs[3], outs[4], outs[5:]


_GELU_K = math.sqrt(2.0 / math.pi)
_GELU_C = 0.044715


def _glu_fwd(y, w, b, g, name="glu_fwd"):
    S, W = y.shape
    ts = _pick(S, (512, 256, 128))

    def body(y_ref, w_ref, b_ref, g_ref, z_ref, sn_ref, ge_ref):
        yv = y_ref[...]
        cdf = 0.5 * (1.0 + jnp.tanh(_GELU_K * (yv + _GELU_C * (yv * yv * yv))))
        ge = (yv * cdf).astype(bf16)
        z = jnp.dot(ge, w_ref[...], preferred_element_type=f32) + b_ref[...]
        s = yv * jax.nn.sigmoid(z)
        rstd = lax.rsqrt(jnp.sum(s * s, axis=-1, keepdims=True) * (1.0 / W) + EPS)
        z_ref[...] = z
        sn_ref[...] = (s * rstd * g_ref[...]).astype(bf16)
        ge_ref[...] = ge

    blk = pl.BlockSpec((ts, W), lambda i: (i, 0))
    row = pl.BlockSpec((1, W), lambda i: (0, 0))
    return _pc(body, grid=(S // ts,), in_specs=[blk, pl.BlockSpec((W, W), lambda i: (0, 0)), row, row],
               out_specs=[blk, blk, blk],
               out_shape=[jax.ShapeDtypeStruct((S, W), f32), jax.ShapeDtypeStruct((S, W), bf16),
                          jax.ShapeDtypeStruct((S, W), bf16)],
               compiler_params=_cp(("parallel",)), name=name)(y, w, b.reshape(1, W), g.reshape(1, W))


def _glu_bwd(y, z, dmixed, w, g, name="glu_bwd"):
    S, W = y.shape
    ts = _pick(S, (512, 256, 128))
    dcb = MLA_PAD // W

    def body(y_ref, z_ref, dsn_ref, w_ref, g_ref, dy_ref, dz_ref, dg_ref, db_ref):
        i = pl.program_id(0)
        yv, zv, gv = y_ref[...], z_ref[...], g_ref[...]
        sig = jax.nn.sigmoid(zv)
        s = yv * sig
        rstd = lax.rsqrt(jnp.sum(s * s, axis=-1, keepdims=True) * (1.0 / W) + EPS)
        sh = s * rstd
        dsn = dsn_ref[...]
        dsh = dsn * gv
        ds = rstd * (dsh - sh * (jnp.sum(dsh * sh, axis=-1, keepdims=True) * (1.0 / W)))
        dz = ds * s * (1.0 - sig)
        dzb = dz.astype(bf16)
        dge = lax.dot_general(dzb, w_ref[...], (((1,), (1,)), ((), ())), preferred_element_type=f32)
        t = jnp.tanh(_GELU_K * (yv + _GELU_C * (yv * yv * yv)))
        dgelu = 0.5 * (1.0 + t) + 0.5 * yv * (1.0 - t * t) * _GELU_K * (1.0 + 3.0 * _GELU_C * yv * yv)
        dy_ref[...] = ds * sig + dge * dgelu
        dz_ref[...] = dzb

        @pl.when(i == 0)
        def _():
            dg_ref[...] = jnp.zeros_like(dg_ref)
            db_ref[...] = jnp.zeros_like(db_ref)

        dg_ref[...] += jnp.sum(dsn * sh, axis=0, keepdims=True)
        db_ref[...] += jnp.sum(dz, axis=0, keepdims=True)

    blk = pl.BlockSpec((ts, W), lambda i: (i, 0))
    row = pl.BlockSpec((1, W), lambda i: (0, 0))
    return _pc(body, grid=(S // ts,),
               in_specs=[blk, blk, pl.BlockSpec((ts, W), lambda i: (i, dcb)), pl.BlockSpec((W, W), lambda i: (0, 0)), row],
               out_specs=[blk, blk, row, row],
               out_shape=[jax.ShapeDtypeStruct((S, W), f32), jax.ShapeDtypeStruct((S, W), bf16),
                          jax.ShapeDtypeStruct((1, W), f32), jax.ShapeDtypeStruct((1, W), f32)],
               compiler_params=_cp(("arbitrary",)), name=name)(y, z, dmixed, w, g.reshape(1, W))


def _ffn_up(hn, wg, wu, name="ffn_up"):
    S, K = hn.shape
    F = wg.shape[0]
    tm, tn = _pick(S, (512, 256, 128)), _pick(F, (1408, 256, 128))

    def body(h_ref, wg_ref, wu_ref, g_ref, u_ref, a_ref):
        hv = h_ref[...]
        nt = (((1,), (1,)), ((), ()))
        gv = lax.dot_general(hv, wg_ref[...], nt, preferred_element_type=f32)
        uv = lax.dot_general(hv, wu_ref[...], nt, preferred_element_type=f32)
        g_ref[...] = gv.astype(bf16)
        u_ref[...] = uv.astype(bf16)
        a_ref[...] = (gv * jax.nn.sigmoid(gv) * uv).astype(bf16)

    wspec = pl.BlockSpec((tn, K), lambda i, j: (j, 0))
    ospec = pl.BlockSpec((tm, tn), lambda i, j: (i, j))
    return _pc(body, grid=(S // tm, F // tn), in_specs=[pl.BlockSpec((tm, K), lambda i, j: (i, 0)), wspec, wspec],
               out_specs=[ospec] * 3,
               out_shape=[jax.ShapeDtypeStruct((S, F), bf16), jax.ShapeDtypeStruct((S, F), bf16),
                          jax.ShapeDtypeStruct((S, F), bf16)],
               compiler_params=_cp(("parallel", "parallel")), name=name)(hn, wg, wu)


def _ffn_bwd_act(dh, wd, gate, up, name="ffn_bwd_act"):
    S, K = dh.shape
    F = wd.shape[0]
    tm, tn = _pick(S, (512, 256, 128)), _pick(F, (1408, 256, 128))

    def body(dh_ref, wd_ref, g_ref, u_ref, dg_ref, du_ref):
        dact = lax.dot_general(dh_ref[...].astype(bf16), wd_ref[...], (((1,), (1,)), ((), ())),
                               preferred_element_type=f32)
        gv, uv = g_ref[...].astype(f32), u_ref[...].astype(f32)
        sig = jax.nn.sigmoid(gv)
        dg_ref[...] = (dact * uv * (sig * (1.0 + gv * (1.0 - sig)))).astype(bf16)
        du_ref[...] = (dact * (gv * sig)).astype(bf16)

    ospec = pl.BlockSpec((tm, tn), lambda i, j: (i, j))
    return _pc(body, grid=(S // tm, F // tn),
               in_specs=[pl.BlockSpec((tm, K), lambda i, j: (i, 0)), pl.BlockSpec((tn, K), lambda i, j: (j, 0)),
                         ospec, ospec],
               out_specs=[ospec] * 2, out_shape=[jax.ShapeDtypeStruct((S, F), bf16)] * 2,
               compiler_params=_cp(("parallel", "parallel")), name=name)(dh, wd, gate, up)


def _pad_heads(w, per_head, pieces):
    K = w.shape[0]
    w3 = w.reshape(K, MLA_HEADS, per_head)
    out = jnp.zeros((K, MLA_HEADS, HEAD_PAD), w.dtype)
    for s0, s1, d0 in pieces:
        out = out.at[:, :, d0:d0 + (s1 - s0)].set(w3[:, :, s0:s1])
    return out.reshape(K, MLA_PAD)


def _unpad_heads(wp, per_head, pieces):
    K = wp.shape[0]
    w3 = wp.reshape(K, MLA_HEADS, HEAD_PAD)
    out = jnp.zeros((K, MLA_HEADS, per_head), wp.dtype)
    for s0, s1, d0 in pieces:
        out = out.at[:, :, s0:s1].set(w3[:, :, d0:d0 + (s1 - s0)])
    return out.reshape(K, MLA_HEADS * per_head)


_Q_PIECES = [(0, QK_NOPE + QK_ROPE, 0)]
_K_PIECES = [(0, QK_NOPE, 0)]
_V_PIECES = [(QK_NOPE, QK_NOPE + V_HEAD, 0)]
_KR0 = Q_LORA + KV_LORA


def _pack_win(w):
    z = jnp.zeros((w.shape[0], LANES - QK_ROPE), w.dtype)
    return jnp.concatenate([w[:, :_KR0 + QK_ROPE], z, w[:, _KR0 + QK_ROPE:]], axis=1)


def _unpack_win(wp):
    return jnp.concatenate([wp[:, :_KR0 + QK_ROPE], wp[:, _KR0 + LANES:]], axis=1)


def _pack_wout(w):
    wa = w[:MLA_WIDTH].reshape(MLA_HEADS, V_HEAD, D_MODEL)
    wa = jnp.concatenate([wa, jnp.zeros_like(wa)], axis=1).reshape(MLA_PAD, D_MODEL)
    return jnp.concatenate([wa, w[MLA_WIDTH:]], axis=0)


def _unpack_wout(wp):
    wa = wp[:MLA_PAD].reshape(MLA_HEADS, HEAD_PAD, D_MODEL)[:, :V_HEAD].reshape(MLA_WIDTH, D_MODEL)
    return jnp.concatenate([wa, wp[MLA_PAD:]], axis=0)


def _pad_gain(g):
    g2 = g.reshape(MLA_HEADS, V_HEAD)
    return jnp.concatenate([g2, jnp.zeros_like(g2)], axis=1).reshape(MLA_PAD)


def _unpad_gain(gp):
    return gp.reshape(MLA_HEADS, HEAD_PAD)[:, :V_HEAD].reshape(MLA_WIDTH)


def _ssm_prep(lam_re, lam_im, log_dt, b_re, b_im, c_re, c_im):
    lam = lax.complex(lam_re, lam_im)
    dt = jnp.exp(log_dt)[:, None]
    a_bar = jnp.exp(lam * dt)
    b_bar = ((a_bar - 1.0) / lam)[:, None, :] * lax.complex(b_re, b_im)
    G8 = SSM_GROUPS // SSM_MACRO
    eye = jnp.eye(G8, dtype=f32)

    def bmat(part):
        p4 = part.reshape(SSM_MACRO, G8, SSM_GROUP, SSM_STATE)
        return jnp.einsum('mgcp,gh->mgchp', p4, eye).reshape(SSM_MACRO, MACRO_CH, MACRO_ST)

    def cmat(part):
        p4 = part.reshape(SSM_MACRO, G8, SSM_GROUP, SSM_STATE)
        return jnp.einsum('mgcp,gh->mgphc', p4, eye).reshape(SSM_MACRO, MACRO_ST, MACRO_CH)

    bm = jnp.concatenate([bmat(b_bar.real), bmat(b_bar.imag)], axis=2)
    cm = jnp.concatenate([cmat(c_re), -cmat(c_im)], axis=1)
    a4 = a_bar.reshape(SSM_MACRO, 1, MACRO_ST)
    a = jnp.concatenate([a4.real, a4.imag], axis=2)
    return bm, cm, a


def _rope_tables(positions):
    freqs = ROPE_THETA ** (-jnp.arange(0, QK_ROPE, 2, dtype=f32) / QK_ROPE)
    ang = positions.astype(f32)[:, None] * freqs
    cos, sin = jnp.cos(ang), jnp.sin(ang)
    S = positions.shape[0]
    half = QK_ROPE // 2
    one, zero = jnp.ones((S, QK_NOPE), f32), jnp.zeros((S, half), f32)
    z64, z32 = jnp.zeros((S, QK_NOPE), f32), jnp.zeros((S, LANES - QK_NOPE - QK_ROPE), f32)
    tc = jnp.concatenate([one, cos, cos, z32], axis=1)
    s1 = jnp.concatenate([z64, -sin, zero, z32], axis=1)
    s2 = jnp.concatenate([z64, zero, sin, z32], axis=1)
    return tc, s1, s2


def _layer_params(W, l):
    p = {}
    p['win'] = _pack_win(W['w_in'][l])
    p['wuq'] = _pad_heads(W['w_uq'][l], QK_NOPE + QK_ROPE, _Q_PIECES)
    wukv = W['w_ukv'][l]
    p['wukv'] = jnp.concatenate([_pad_heads(wukv, QK_NOPE + V_HEAD, _K_PIECES),
                                 _pad_heads(wukv, QK_NOPE + V_HEAD, _V_PIECES)], axis=1)
    p['attn_g'] = _pad_gain(W['attn_out_g'][l])
    return p


def _forward_layer(h, memn_in, tabs, W, l, name, gather=None):
    p = _layer_params(W, l)
    sv = {'h0': h, 'p': p}
    xn = _rms_fwd(h, W['norm_mix_g'][l], name=name + "rms_mix")
    proj = _mm([(xn, p['win'])], 'nn', f32, name=name + "mm_in")
    cqn = _rms_fwd(proj, W['q_norm_g'][l], col0=0, width=Q_LORA, name=name + "rms_q")
    ckvn = _rms_fwd(proj, W['kv_norm_g'][l], col0=Q_LORA, width=KV_LORA, name=name + "rms_kv")
    q = _mm([(cqn, p['wuq'])], 'nn', f32, name=name + "mm_uq")
    kv = _mm([(ckvn, p['wukv'])], 'nn', f32, name=name + "mm_ukv")
    qh, kh, vh = _rope_fwd(q, kv, proj, tabs, name=name + "rope")
    oh, lse, carried = _attn_fwd(qh, kh, vh, plan=gather[0] if gather else None, name=name + "attn")
    if gather:
        for n, t in gather[1](carried).items():
            W[n][l] = t
    p['wout'] = _pack_wout(W['w_out'][l])
    an = _rms_fwd(oh, p['attn_g'], n_valid=MLA_WIDTH, name=name + "rms_attn")
    bm, cm, a = W['ssm'][l]
    bmb, cmb = bm.astype(bf16), cm.astype(bf16)
    y, xc = _ssm_fwd(proj, bmb, cmb, a, W['ssm_d'][l], name=name + "ssm")
    z, sn, ge = _glu_fwd(y, W['ssm_w_glu'][l], W['ssm_b_glu'][l], W['ssm_out_g'][l], name=name + "glu")
    h1a = _mm([(an, p['wout'][:MLA_PAD])], 'nn', f32, res=h, name=name + "mm_out_a")
    h1 = _mm([(sn, p['wout'][MLA_PAD:])], 'nn', f32, res=h1a, name=name + "mm_out_s")
    hn2 = _rms_fwd(h1, W['norm_x_g'][l], name=name + "rms_x")
    memn = _rms_fwd(memn_in, W['mem_norm_g'][l], name=name + "rms_mem")
    qx = _mm([(hn2, W['w_xq'][l])], 'nn', bf16, name=name + "mm_xq")
    kvx = _mm([(memn, W['w_xkv'][l])], 'nn', bf16, name=name + "mm_xkv")
    ox = _xattn_fwd(qx, kvx, name=name + "xattn")
    h2 = _mm([(ox, W['w_xo'][l])], 'nn', f32, res=h1, name=name + "mm_xo")
    hn3 = _rms_fwd(h2, W['norm_ffn_g'][l], name=name + "rms_ffn")
    gate, up, act = _ffn_up(hn3, W['w_gate'][l], W['w_up'][l], name=name + "ffn_up")
    h3 = _mm([(act, W['w_down'][l])], 'nn', f32, res=h2, name=name + "mm_down")
    sv.update(xn=xn, proj=proj, cqn=cqn, ckvn=ckvn, qh=qh, kh=kh, vh=vh, oh=oh, lse=lse, an=an, bmb=bmb, cmb=cmb,
              a=a, y=y, xc=xc, z=z, sn=sn, ge=ge, h1=h1, hn2=hn2, memn=memn, qx=qx, kvx=kvx, ox=ox, h2=h2, hn3=hn3,
              gate=gate, up=up, act=act)
    return h3, sv


def _backward_layer(dh3, sv, memn_in, tabs, W, l, hook, name):
    p = sv['p']
    G = {}
    G['w_down'] = _mm([(sv['act'], dh3)], 'tn', f32, name=name + "dw_down")
    dgate, dup = _ffn_bwd_act(dh3, W['w_down'][l], sv['gate'], sv['up'], name=name + "ffn_bwd_act")
    dhn3 = _mm([(dgate, W['w_gate'][l]), (dup, W['w_up'][l])], 'nn', f32, name=name + "mm_dffn")
    G['w_gate'] = _mm([(dgate, sv['hn3'])], 'tn', f32, name=name + "dw_gate")
    G['w_up'] = _mm([(dup, sv['hn3'])], 'tn', f32, name=name + "dw_up")
    xplan = hook['ffn'](l, G) if hook else None
    dh2, dg = _rms_bwd(sv['h2'], W['norm_ffn_g'][l], dhn3, res=dh3, name=name + "rmsb_ffn")
    G['norm_ffn_g'] = dg[0]
    G['w_xo'] = _mm([(sv['ox'], dh2)], 'tn', f32, name=name + "dw_xo")
    dox = _mm([(dh2, W['w_xo'][l])], 'nt', bf16, name=name + "mm_dxo")
    dqx, dkvx, carried = _xattn_bwd(sv['qx'], sv['kvx'], dox, plan=xplan, name=name + "xattn_bwd")
    if xplan:
        xplan['done'](carried)
    G['w_xq'] = _mm([(sv['hn2'], dqx)], 'tn', f32, name=name + "dw_xq")
    G['w_xkv'] = _mm([(sv['memn'], dkvx)], 'tn', f32, name=name + "dw_xkv")
    dhn2 = _mm([(dqx, W['w_xq'][l])], 'nt', f32, name=name + "mm_dxq")
    dmemn = _mm([(dkvx, W['w_xkv'][l])], 'nt', f32, name=name + "mm_dxkv")
    dh1, dg = _rms_bwd(sv['h1'], W['norm_x_g'][l], dhn2, res=dh2, name=name + "rmsb_x")
    G['norm_x_g'] = dg[0]
    _, dg = _rms_bwd(memn_in, W['mem_norm_g'][l], dmemn, name=name + "rmsb_mem")
    G['mem_norm_g'] = dg[0]
    dwo_a = _mm([(sv['an'], dh1)], 'tn', f32, name=name + "dw_out_a")
    dwo_s = _mm([(sv['sn'], dh1)], 'tn', f32, name=name + "dw_out_s")
    G['w_out'] = _unpack_wout(jnp.concatenate([dwo_a, dwo_s], axis=0))
    dmixed = _mm([(dh1, p['wout'])], 'nt', f32, name=name + "mm_dout")
    dy, dz, dg, db = _glu_bwd(sv['y'], sv['z'], dmixed, W['ssm_w_glu'][l], W['ssm_out_g'][l], name=name + "glu_bwd")
    G['ssm_out_g'], G['ssm_b_glu'] = dg[0], db[0]
    G['ssm_w_glu'] = _mm([(sv['ge'], dz)], 'tn', f32, name=name + "dw_glu")
    xplan = hook['mid'](l, G) if hook else None
    du, dbm, dcm, da, dd, carried = _ssm_bwd(sv['proj'], dy, sv['xc'], sv['bmb'], sv['cmb'], sv['a'], W['ssm_d'][l],
                                             plan=xplan, name=name + "ssm_bwd")
    if xplan:
        xplan['done'](carried)
    G['ssm_d'] = dd[0]
    G['ssm_raw'] = (dbm, dcm, da)
    doh, dg, delta = _rms_bwd(sv['oh'], p['attn_g'], dmixed, width=MLA_PAD, n_valid=MLA_WIDTH, delta=True,
                              out_dtype=bf16, name=name + "rmsb_attn")
    G['attn_out_g'] = _unpad_gain(dg[0])
    plans = hook['take']() if hook else []
    plan = _merge_plans(plans) if plans else None
    dqh, dkh, dvh, carried = _attn_bwd(sv['qh'], sv['kh'], sv['vh'], doh, sv['lse'], delta, plan=plan,
                                       name=name + "attn_bwd")
    if plan:
        plan['done'](carried)
    dq, dkv, dkr = _rope_bwd(dqh, dkh, dvh, tabs, name=name + "rope_bwd")
    G['w_uq'] = _unpad_heads(_mm([(sv['cqn'], dq)], 'tn', f32, name=name + "dw_uq"), QK_NOPE + QK_ROPE, _Q_PIECES)
    dwukv = _mm([(sv['ckvn'], dkv)], 'tn', f32, name=name + "dw_ukv")
    G['w_ukv'] = (_unpad_heads(dwukv[:, :MLA_PAD], QK_NOPE + V_HEAD, _K_PIECES)
                  + _unpad_heads(dwukv[:, MLA_PAD:], QK_NOPE + V_HEAD, _V_PIECES))
    dcqn = _mm([(dq, p['wuq'])], 'nt', f32, name=name + "mm_duq")
    dckvn = _mm([(dkv, p['wukv'])], 'nt', f32, name=name + "mm_dukv")
    dcq, dg = _rms_bwd(sv['proj'], W['q_norm_g'][l], dcqn, col0=0, width=Q_LORA, out_dtype=bf16, name=name + "rmsb_q")
    G['q_norm_g'] = dg[0]
    dckv, dg = _rms_bwd(sv['proj'], W['kv_norm_g'][l], dckvn, col0=Q_LORA, width=KV_LORA, out_dtype=bf16,
                        name=name + "rmsb_kv")
    G['kv_norm_g'] = dg[0]
    dproj = jnp.concatenate([dcq, dckv, dkr, du.astype(bf16)], axis=1)
    G['w_in'] = _unpack_win(_mm([(sv['xn'], dproj)], 'tn', f32, name=name + "dw_in"))
    dxn = _mm([(dproj, p['win'])], 'nt', f32, name=name + "mm_din")
    dh0, dg = _rms_bwd(sv['h0'], W['norm_mix_g'][l], dxn, res=dh1, name=name + "rmsb_mix")
    G['norm_mix_g'] = dg[0]
    return dh0, G


def _local_step(x, mem, positions, target, W, gathers=None, hook=None):
    tabs = _rope_tables(positions)
    ssm_in = [(W['ssm_lambda_re'][l], W['ssm_lambda_im'][l], W['ssm_log_dt'][l], W['ssm_b_re'][l], W['ssm_b_im'][l],
               W['ssm_c_re'][l], W['ssm_c_im'][l]) for l in range(DEPTH)]
    preps = [jax.vjp(_ssm_prep, *ssm_in[l]) for l in range(DEPTH)]
    W = dict(W)
    W['ssm'] = [preps[l][0] for l in range(DEPTH)]
    h = x
    saved = []
    for l in range(DEPTH):
        h, sv = _forward_layer(h, mem, tabs, W, l, f"l{l}_", gathers[l] if gathers else None)
        saved.append(sv)
    dh, dgf, loss = _loss_head(h, W['final_norm_g'], target)
    grads = [None] * DEPTH
    for l in reversed(range(DEPTH)):
        dh, G = _backward_layer(dh, saved[l], mem, tabs, W, l, hook, f"l{l}b_")
        dbm, dcm, da = G.pop('ssm_raw')
        names = ['ssm_lambda_re', 'ssm_lambda_im', 'ssm_log_dt', 'ssm_b_re', 'ssm_b_im', 'ssm_c_re', 'ssm_c_im']
        for n, g in zip(names, preps[l][1]((dbm, dcm, da))):
            G[n] = g
        grads[l] = G
        if hook is not None and l > 0:
            hook['rest'](l, G)
    out = {n: [grads[l][n] for l in range(DEPTH)] if n in SHARDED else jnp.stack([grads[l][n] for l in range(DEPTH)])
           for n in grads[0]}
    out['final_norm_g'] = dgf[0]
    return loss[0, 0], dh, out


_HBM = pl.BlockSpec(memory_space=pltpu.HBM)


def _me():
    return lax.axis_index("x"), lax.axis_index("y"), lax.axis_index("c")


def _chip_peers(x, y, c):
    devs = [(1 - x, y, c), (x, 1 - y, c), (1 - x, 1 - y, c)]
    return devs, [2 * d[0] + d[1] for d in devs]


def _gather_plan(xs, half_first):
    n = len(xs)
    if half_first:
        ins = [t.reshape(2, 1, *t.shape[1:]) for t in xs]
        outs = [jax.ShapeDtypeStruct((2, 4, *t.shape[1:]), t.dtype) for t in xs]
    else:
        ins = [t.reshape(1, 2, t.shape[0] // 2, t.shape[1]) for t in xs]
        outs = [jax.ShapeDtypeStruct((4, 2, t.shape[0] // 2, t.shape[1]), t.dtype) for t in xs]

    def own(ref, h):
        return ref.at[h] if half_first else ref.at[:, h]

    def slot(ref, h, j):
        return ref.at[h, pl.ds(j, 1)] if half_first else ref.at[pl.ds(j, 1), h]

    def copies(src, dst, send, recv):
        x, y, c = _me()
        jme = 2 * x + y
        devs, js = _chip_peers(x, y, c)
        half, other = pl.ds(c, 1), pl.ds(1 - c, 1)
        mk = pltpu.make_async_remote_copy
        for i in range(n):
            for k in range(3):
                out_cp = mk(own(src[i], half), slot(dst[i], half, jme), send.at[6 * i + k], recv.at[6 * i + k],
                            device_id=devs[k], device_id_type=MESH)
                in_cp = mk(own(src[i], half), slot(dst[i], half, js[k]), send.at[6 * i + k], recv.at[6 * i + k],
                           device_id=devs[k], device_id_type=MESH)
                pass_cp = mk(slot(dst[i], half, js[k]), slot(dst[i], half, js[k]), send.at[6 * i + 3 + k],
                             recv.at[6 * i + 3 + k], device_id=(x, y, 1 - c), device_id_type=MESH)
                got_cp = mk(slot(dst[i], other, js[k]), slot(dst[i], other, js[k]), send.at[6 * i + 3 + k],
                            recv.at[6 * i + 3 + k], device_id=(x, y, 1 - c), device_id_type=MESH)
                yield out_cp, in_cp, pass_cp, got_cp

    def start(*refs):
        for out_cp, _, _, _ in copies(*refs):
            out_cp.start()

    def forward(*refs):
        for _, in_cp, pass_cp, _ in copies(*refs):
            in_cp.wait_recv()
            pass_cp.start()

    def finish(*refs):
        for out_cp, _, pass_cp, got_cp in copies(*refs):
            got_cp.wait_recv()
            out_cp.wait_send()
            pass_cp.wait_send()

    return dict(n=n, ins=ins, outs=outs, nsem=6 * n, start=start, forward=forward, finish=finish)


def _plan_refs(plan, refs):
    n = plan['n']
    return refs[:n], refs[n:2 * n], refs[2 * n], refs[2 * n + 1]


def _run_plan(plan, name):
    n = plan['n']

    def body(*refs):
        args = _plan_refs(plan, refs)
        plan['start'](*args)
        plan['forward'](*args)
        plan['finish'](*args)

    return _pc(body, in_specs=[_HBM] * n, out_specs=[_HBM] * n, out_shape=plan['outs'],
               scratch_shapes=[pltpu.SemaphoreType.DMA((plan['nsem'],)), pltpu.SemaphoreType.DMA((plan['nsem'],))],
               compiler_params=pltpu.CompilerParams(has_side_effects=True), name=name)(*plan['ins'])


def _fill_own(gathered, own, half_first):
    jme = (2 * lax.axis_index("x") + lax.axis_index("y")).astype(jnp.int32)
    zero = jnp.int32(0)
    if half_first:
        return lax.dynamic_update_slice(gathered, own[:, None], (zero, jme, zero, zero))
    return lax.dynamic_update_slice(gathered, own.reshape(1, *gathered.shape[1:]), (jme, zero, zero, zero))


def _exchange_plan(gs, done=None):
    n = len(gs)

    def copies(src, dst, send, recv):
        x, y, c = _me()
        for i in range(n):
            yield pltpu.make_async_remote_copy(src[i].at[:, pl.ds(1 - c, 1)], dst[i], send.at[i], recv.at[i],
                                               device_id=(x, y, 1 - c), device_id_type=MESH)

    def start(*refs):
        for cp in copies(*refs):
            cp.start()

    def finish(*refs):
        for cp in copies(*refs):
            cp.wait()

    outs = [jax.ShapeDtypeStruct((4, 1, *g.shape[2:]), g.dtype) for g in gs]
    return dict(n=n, ins=list(gs), outs=outs, nsem=n, start=start, forward=lambda *refs: None, finish=finish, done=done)


def _plan_extras(plan):
    if not plan:
        return [], [], [], [], []
    anyspec = pl.BlockSpec(memory_space=pl.ANY)
    sems = [pltpu.SemaphoreType.DMA((plan['nsem'],)), pltpu.SemaphoreType.DMA((plan['nsem'],))]
    return [anyspec] * plan['n'], [anyspec] * plan['n'], list(plan['outs']), sems, list(plan['ins'])


def _scatter_plan(ps, done=None):
    n = len(ps)

    def copies(src, dst, send, recv, off):
        x, y, c = _me()
        devs, js = _chip_peers(x, y, c)
        for i in range(n):
            for k in range(3):
                yield pltpu.make_async_remote_copy(src[i].at[pl.ds(js[k], 1)], dst[i].at[k], send.at[off + 3 * i + k],
                                                   recv.at[off + 3 * i + k], device_id=devs[k], device_id_type=MESH)

    def start(src, dst, send, recv, off=0):
        for cp in copies(src, dst, send, recv, off):
            cp.start()

    def finish(src, dst, send, recv, off=0):
        for cp in copies(src, dst, send, recv, off):
            cp.wait()

    outs = [jax.ShapeDtypeStruct((3, 1, *p.shape[1:]), p.dtype) for p in ps]
    return dict(n=n, ins=list(ps), outs=outs, nsem=3 * n, start=start, forward=lambda *refs: None, finish=finish,
                done=done)


def _merge_plans(plans):
    def run(which):
        def f(src, dst, send, recv):
            o = s = 0
            for p in plans:
                p[which](src[o:o + p['n']], dst[o:o + p['n']], send, recv, off=s)
                o, s = o + p['n'], s + p['nsem']
        return f

    def done(results):
        o = 0
        for p in plans:
            p['done'](results[o:o + p['n']])
            o += p['n']

    return dict(n=sum(p['n'] for p in plans), ins=[t for p in plans for t in p['ins']],
                outs=[t for p in plans for t in p['outs']], nsem=sum(p['nsem'] for p in plans),
                start=run('start'), forward=lambda *refs: None, finish=run('finish'), done=done)


def _swap_sibling(hs, name):
    n = len(hs)

    def body(*refs):
        src, dst = refs[:n], refs[n:2 * n]
        send, recv = refs[2 * n:]
        x, y, c = _me()
        cps = []
        for i in range(n):
            cp = pltpu.make_async_remote_copy(src[i], dst[i], send.at[i], recv.at[i], device_id=(x, y, 1 - c),
                                              device_id_type=MESH)
            cp.start()
            cps.append(cp)
        for cp in cps:
            cp.wait()

    outs = [jax.ShapeDtypeStruct(h.shape, h.dtype) for h in hs]
    return _pc(body, in_specs=[_HBM] * n, out_specs=[_HBM] * n, out_shape=outs,
               scratch_shapes=[pltpu.SemaphoreType.DMA((n,)), pltpu.SemaphoreType.DMA((n,))],
               compiler_params=pltpu.CompilerParams(has_side_effects=True), name=name)(*hs)


ELEMWISE_VMEM_BUDGET = 24 * 1024 * 1024


def _row_tile(r, n, narrays):
    limit = ELEMWISE_VMEM_BUDGET // (2 * 4 * narrays * n)
    best = SUBLANES
    for t in range(16, r + 1, 16):
        if r % t == 0 and t <= limit:
            best = t
    return best


def _add_half(g, r1, cidx, name):
    _, _, r, n = g.shape
    tr = _row_tile(r, n, 3)

    def body(c_ref, g_ref, r_ref, o_ref):
        o_ref[...] = (g_ref[...] + r_ref[...]).astype(GRAD_TRANSIT)

    blk = lambda f: pl.BlockSpec((None, None, tr, n), f)
    gs = pltpu.PrefetchScalarGridSpec(
        num_scalar_prefetch=1, grid=(4, r // tr),
        in_specs=[blk(lambda j, i, c: (j, c[0], i, 0)), blk(lambda j, i, c: (j, 0, i, 0))],
        out_specs=pl.BlockSpec((None, tr, n), lambda j, i, c: (j, i, 0)))
    return _pc(body, grid_spec=gs, out_shape=jax.ShapeDtypeStruct((4, r, n), GRAD_TRANSIT),
               compiler_params=_cp(("parallel", "parallel")), name=name)(cidx, g, r1)


def _add_chips(p, r3, jidx, name):
    _, r, n = p.shape
    tr = _row_tile(r, n, 5)

    def body(j_ref, p_ref, a_ref, b_ref, c_ref, o_ref):
        o_ref[...] = ((p_ref[...].astype(f32) + a_ref[...].astype(f32)) + b_ref[...].astype(f32)) + c_ref[...].astype(f32)

    rblk = lambda k: pl.BlockSpec((None, None, tr, n), lambda i, j: (k, 0, i, 0))
    gs = pltpu.PrefetchScalarGridSpec(
        num_scalar_prefetch=1, grid=(r // tr,),
        in_specs=[pl.BlockSpec((None, tr, n), lambda i, j: (j[0], i, 0)), rblk(0), rblk(1), rblk(2)],
        out_specs=pl.BlockSpec((tr, n), lambda i, j: (i, 0)))
    return _pc(body, grid_spec=gs, out_shape=jax.ShapeDtypeStruct((r, n), f32),
               compiler_params=_cp(("parallel",)), name=name)(jidx, p, r3, r3, r3)


def _adamw_halves(w, mine, theirs, m, v, cidx, name):
    L, r, n = w.shape
    r2 = r // 2
    tr = _row_tile(r2, n, 11)
    c1 = 1.0 / (1.0 - ADAM_B1 ** ADAM_STEP)
    c2 = 1.0 / (1.0 - ADAM_B2 ** ADAM_STEP)

    def body(c_ref, w_ref, a0_ref, b0_ref, a1_ref, b1_ref, m_ref, v_ref, g_ref, d_ref, mo_ref, vo_ref):
        l, hf = pl.program_id(0), pl.program_id(1)
        own = hf == c_ref[0]
        gv = jnp.where(l == 0, jnp.where(own, a0_ref[...], b0_ref[...]), jnp.where(own, a1_ref[...], b1_ref[...]))
        m2 = ADAM_B1 * m_ref[...] + (1.0 - ADAM_B1) * gv
        v2 = ADAM_B2 * v_ref[...] + (1.0 - ADAM_B2) * (gv * gv)
        g_ref[...] = gv
        d_ref[...] = -ADAM_LR * ((m2 * c1) / (jnp.sqrt(v2 * c2) + ADAM_EPS) + ADAM_WD * w_ref[...])
        mo_ref[...] = m2
        vo_ref[...] = v2

    full = pl.BlockSpec((None, None, tr, n), lambda l, hf, i, c: (l, hf, i, 0))
    half = pl.BlockSpec((tr, n), lambda l, hf, i, c: (i, 0))
    gs = pltpu.PrefetchScalarGridSpec(num_scalar_prefetch=1, grid=(L, 2, r2 // tr),
                                      in_specs=[full, half, half, half, half, full, full], out_specs=[full] * 4)
    four = lambda t: t.reshape(L, 2, r2, n)
    outs = _pc(body, grid_spec=gs, out_shape=[jax.ShapeDtypeStruct((L, 2, r2, n), f32)] * 4,
               compiler_params=_cp(("parallel", "parallel", "parallel")), name=name)(
        cidx, four(w), mine[0], theirs[0], mine[1], theirs[1], four(m), four(v))
    return [t.reshape(w.shape) for t in outs]


def _adamw_whole(w, g, m, v, name):
    c1 = 1.0 / (1.0 - ADAM_B1 ** ADAM_STEP)
    c2 = 1.0 / (1.0 - ADAM_B2 ** ADAM_STEP)

    def body(w_ref, g_ref, m_ref, v_ref, d_ref, mo_ref, vo_ref):
        gv = g_ref[...]
        m2 = ADAM_B1 * m_ref[...] + (1.0 - ADAM_B1) * gv
        v2 = ADAM_B2 * v_ref[...] + (1.0 - ADAM_B2) * (gv * gv)
        d_ref[...] = -ADAM_LR * ((m2 * c1) / (jnp.sqrt(v2 * c2) + ADAM_EPS) + ADAM_WD * w_ref[...])
        mo_ref[...] = m2
        vo_ref[...] = v2

    return _pc(body, out_shape=[jax.ShapeDtypeStruct(w.shape, f32)] * 3, name=name)(w, g, m, v)


def _full_from_gathered(name, t):
    r, n = 2 * t.shape[2], t.shape[3]
    if SHARDED[name] == 1 or name in TRANSPOSED:
        return t.reshape(4 * r, n)
    return t.reshape(4, r, n).transpose(1, 0, 2).reshape(r, 4 * n)


def _shard_major(name, g):
    R, C = g.shape
    if SHARDED[name] == 1 or name in TRANSPOSED:
        return g.reshape(4, 2, R // 8, C)
    return g.reshape(R, 4, C // 4).transpose(1, 0, 2).reshape(4, 2, R // 2, C // 4)


_SMALL_ROWS = 288


def _pack_small(d):
    flat = jnp.concatenate([d[n].reshape(-1) for n in SMALL])
    total = 2 * 4 * _SMALL_ROWS * LANES
    flat = jnp.concatenate([flat, jnp.zeros((total - flat.shape[0],), f32)])
    return flat.reshape(4, 2, _SMALL_ROWS, LANES)


def _unpack_small(t, like):
    flat = t.reshape(-1)
    out, off = {}, 0
    for n in SMALL:
        sz = math.prod(like[n].shape)
        out[n] = flat[off:off + sz].reshape(like[n].shape)
        off += sz
    return out


def kernel(x, mem, positions, norm_mix_g, w_in, q_norm_g, w_uq, kv_norm_g, w_ukv, ssm_lambda_re, ssm_lambda_im, ssm_log_dt, ssm_b_re, ssm_b_im, ssm_c_re, ssm_c_im, ssm_d, ssm_w_glu, ssm_b_glu, attn_out_g, ssm_out_g, w_out, norm_x_g, mem_norm_g, w_xq, w_xkv, w_xo, norm_ffn_g, w_gate, w_up, w_down, final_norm_g, loss_target, m_norm_mix_g, m_w_in, m_q_norm_g, m_w_uq, m_kv_norm_g, m_w_ukv, m_ssm_lambda_re, m_ssm_lambda_im, m_ssm_log_dt, m_ssm_b_re, m_ssm_b_im, m_ssm_c_re, m_ssm_c_im, m_ssm_d, m_ssm_w_glu, m_ssm_b_glu, m_attn_out_g, m_ssm_out_g, m_w_out, m_norm_x_g, m_mem_norm_g, m_w_xq, m_w_xkv, m_w_xo, m_norm_ffn_g, m_w_gate, m_w_up, m_w_down, m_final_norm_g, v_norm_mix_g, v_w_in, v_q_norm_g, v_w_uq, v_kv_norm_g, v_w_ukv, v_ssm_lambda_re, v_ssm_lambda_im, v_ssm_log_dt, v_ssm_b_re, v_ssm_b_im, v_ssm_c_re, v_ssm_c_im, v_ssm_d, v_ssm_w_glu, v_ssm_b_glu, v_attn_out_g, v_ssm_out_g, v_w_out, v_norm_x_g, v_mem_norm_g, v_w_xq, v_w_xkv, v_w_xo, v_norm_ffn_g, v_w_gate, v_w_up, v_w_down, v_final_norm_g):
    given = dict(locals())
    swap = lambda n, t: jnp.swapaxes(t, *TRANSPOSED[n]) if n in TRANSPOSED else t
    w = {n: swap(n, given[n]) for n in WEIGHTS}
    m = {n: swap(n, given["m_" + n]) for n in WEIGHTS}
    v = {n: swap(n, given["v_" + n]) for n in WEIGHTS}
    big = list(SHARDED)

    shards = {n: w[n].astype(bf16) for n in big}
    early = [n for n in big if n in EARLY_WEIGHTS]
    rest = [n for n in big if n not in EARLY_WEIGHTS]

    def full(names, results, l):
        return {n: _full_from_gathered(n, _fill_own(t, shards[n][l], False)) for n, t in zip(names, results)}

    first = _run_plan(_gather_plan([shards[n][l] for l in range(DEPTH) for n in early], False), "allgather_weights_early")
    W = {n: [None] * DEPTH for n in big}
    for l in range(DEPTH):
        for n, t in full(early, first[l * len(early):(l + 1) * len(early)], l).items():
            W[n][l] = t
    gathers = [(_gather_plan([shards[n][l] for n in rest], False), functools.partial(full, rest, l=l))
               for l in range(DEPTH)]
    W.update({n: w[n] for n in SMALL})

    cidx = lax.axis_index("c").astype(jnp.int32).reshape(1)
    jidx = (2 * lax.axis_index("x") + lax.axis_index("y")).astype(jnp.int32).reshape(1)
    ffn = [n for n in big if n in FFN_WEIGHTS]
    tail = [n for n in big if n in EARLY_WEIGHTS]
    mid = [n for n in big if n not in FFN_WEIGHTS and n not in EARLY_WEIGHTS]
    sums, got = {}, {}
    ready = []

    def exchange(names, tag, l, G, extra=()):
        keys = [(l, n) for n in names] + [(l, n) for n, _ in extra]
        gs = [_shard_major(n, G[n]) for n in names] + [g for _, g in extra]

        def done(r1):
            ps = [_add_half(g, r, cidx, f"grad_add_half_{tag}_{k[1]}") for k, g, r in zip(keys, gs, r1)]
            sums.update(zip(keys, ps))
            ready.append(_scatter_plan(ps, done=lambda results: got.update(zip(keys, results))))

        return _exchange_plan(gs, done)

    def take():
        plans = list(ready)
        ready.clear()
        return plans

    def now(plan, name):
        plan['done'](_run_plan(plan, name))

    hook = {'ffn': lambda l, G: exchange(ffn, f"l{l}_ffn", l, G), 'mid': lambda l, G: exchange(mid, f"l{l}_mid", l, G),
            'rest': lambda l, G: now(exchange(tail, f"l{l}_tail", l, G), f"grad_exchange_halves_l{l}_tail"),
            'take': take}
    loss, dx, grads = _local_step(x[0], mem[0], positions[0], loss_target[0], W, gathers=gathers, hook=hook)
    loss = lax.psum(loss, ("x", "y", "c"))

    now(exchange(tail, "l0_tail", 0, {n: grads[n][0] for n in tail}, extra=[("small", _pack_small(grads))]),
        "grad_exchange_halves_l0_tail")
    now(_merge_plans(take()), "grad_scatter_chips_l0_tail")
    keys = list(sums)
    hs = dict(zip(keys, [_add_chips(sums[k], got[k], jidx, f"grad_add_chips_l{k[0]}_{k[1]}") for k in keys]))
    ts = dict(zip(keys, _swap_sibling([hs[k] for k in keys], "grad_swap_sibling")))

    out_g, out_d, out_m, out_v = {}, {}, {}, {}
    for n in big:
        mine, theirs = [hs[(l, n)] for l in range(DEPTH)], [ts[(l, n)] for l in range(DEPTH)]
        out_g[n], out_d[n], out_m[n], out_v[n] = _adamw_halves(w[n], mine, theirs, m[n], v[n], cidx, f"adamw_{n}")
    both = jnp.stack([hs[(0, "small")], ts[(0, "small")]])
    piece = jnp.where(cidx[0] == 0, both, both[::-1]).reshape(2 * _SMALL_ROWS, LANES)
    gsm = _fill_own(_run_plan(_gather_plan([piece], False), "allgather_small")[0], piece, False)
    out_g.update(_unpack_small(gsm, w))
    for n in SMALL:
        two = lambda t: t.reshape(1, -1) if t.ndim == 1 else t
        d_, m_, v_ = _adamw_whole(two(w[n]), two(out_g[n]), two(m[n]), two(v[n]), f"adamw_{n}")
        out_d[n], out_m[n], out_v[n] = (t.reshape(w[n].shape) for t in (d_, m_, v_))

    outs = [[swap(n, d[n]) for n in WEIGHTS] for d in (out_g, out_d, out_m, out_v)]
    return (loss, dx.reshape(x.shape), *outs[0], *outs[1], *outs[2], *outs[3])
```

```python
import functools
import math

import jax
import jax.numpy as jnp
from jax import lax
from jax.experimental import pallas as pl
from jax.experimental.pallas import tpu as pltpu

f32, bf16 = jnp.float32, jnp.bfloat16

D_MODEL = 1024
DEPTH = 2
MLA_HEADS = 8
QK_NOPE = 64
QK_ROPE = 32
V_HEAD = 64
Q_LORA = 256
KV_LORA = 128
MLA_WIDTH = MLA_HEADS * V_HEAD
ROPE_THETA = 10000.0
SSM_WIDTH = 512
SSM_GROUP = 16
SSM_GROUPS = 32
SSM_STATE = 64
IN_WIDTH = Q_LORA + KV_LORA + QK_ROPE + SSM_WIDTH
X_HEADS = 4
X_HEAD_DIM = D_MODEL // X_HEADS
D_FF = 2816
EPS = 1e-6
ADAM_LR, ADAM_B1, ADAM_B2, ADAM_EPS, ADAM_WD, ADAM_STEP = 0.001, 0.9, 0.999, 1e-08, 0.01, 10

LANES = 128
SUBLANES = 8
HEAD_PAD = 128
MLA_PAD = MLA_HEADS * HEAD_PAD
SSM_MACRO = 4
MACRO_CH = SSM_WIDTH // SSM_MACRO
MACRO_ST = SSM_GROUPS // SSM_MACRO * SSM_STATE
VMEM_LIMIT = 56 * 1024 * 1024
GRAD_TRANSIT = bf16

WEIGHTS = ['norm_mix_g', 'w_in', 'q_norm_g', 'w_uq', 'kv_norm_g', 'w_ukv', 'ssm_lambda_re', 'ssm_lambda_im',
           'ssm_log_dt', 'ssm_b_re', 'ssm_b_im', 'ssm_c_re', 'ssm_c_im', 'ssm_d', 'ssm_w_glu', 'ssm_b_glu',
           'attn_out_g', 'ssm_out_g', 'w_out', 'norm_x_g', 'mem_norm_g', 'w_xq', 'w_xkv', 'w_xo', 'norm_ffn_g',
           'w_gate', 'w_up', 'w_down', 'final_norm_g']
SHARDED = {'w_in': 1, 'w_uq': 2, 'w_ukv': 2, 'ssm_w_glu': 1, 'w_out': 1, 'w_xq': 1, 'w_xkv': 2, 'w_xo': 1,
           'w_gate': 2, 'w_up': 2, 'w_down': 1}
SMALL = [n for n in WEIGHTS if n not in SHARDED]
EARLY_WEIGHTS = ('w_in', 'w_uq', 'w_ukv')
FFN_WEIGHTS = ('w_gate', 'w_up', 'w_down')
TRANSPOSED = {'w_gate': (1, 2), 'w_up': (1, 2), 'ssm_b_re': (2, 3), 'ssm_b_im': (2, 3)}
MESH = pl.DeviceIdType.MESH


def _pc(body, **kw):
    return pl.pallas_call(body, **kw)


def _pick(n, prefs):
    for p in prefs:
        if n % p == 0:
            return p
    return n


def _cp(sem=None):
    return pltpu.CompilerParams(dimension_semantics=sem, vmem_limit_bytes=VMEM_LIMIT)


_TILE_CANDS = (1024, 1408, 512, 256, 128)
MM_VMEM_BUDGET = 40 * 1024 * 1024


def _mm_tiles(M, K, N, a_bytes, b_bytes, o_bytes, npair, has_res, need_acc):
    best = None
    for tm in _TILE_CANDS:
        for tk in _TILE_CANDS:
            if M % tm or K % tk:
                continue
            vm = npair * (2 * tm * tk * a_bytes + 2 * tk * N * b_bytes) + 2 * tm * N * o_bytes
            vm += tm * N * 4 * (1 + need_acc + 2 * has_res)
            if a_bytes == 4:
                vm += npair * tm * tk * 2
            if b_bytes == 4:
                vm += npair * tk * N * 2
            if vm <= MM_VMEM_BUDGET and (best is None or tm * tk > best[0]):
                best = (tm * tk, tm, tk)
    if best is None:
        return _pick(M, (256, 128)), _pick(K, (256, 128))
    return best[1], best[2]


def _mm(pairs, mode, out_dtype, res=None, name="mm"):
    a0, b0 = pairs[0]
    if mode == 'nn':
        (M, K), N = a0.shape, b0.shape[1]
        dims = (((1,), (0,)), ((), ()))
    elif mode == 'nt':
        (M, K), N = a0.shape, b0.shape[0]
        dims = (((1,), (1,)), ((), ()))
    else:
        (K, M), N = a0.shape, b0.shape[1]
        dims = (((0,), (0,)), ((), ()))
    npair = len(pairs)
    has_res = res is not None
    direct = out_dtype == f32
    tm, tk = _mm_tiles(M, K, N, a0.dtype.itemsize, b0.dtype.itemsize, jnp.dtype(out_dtype).itemsize, npair, has_res,
                       not direct)
    nk = K // tk

    def body(*refs):
        ins = refs[:2 * npair]
        res_ref = refs[2 * npair] if has_res else None
        o_ref = refs[2 * npair + has_res]
        acc = o_ref if direct else refs[2 * npair + has_res + 1]
        k = pl.program_id(1)
        s = None
        for p in range(npair):
            d = lax.dot_general(ins[2 * p][...].astype(bf16), ins[2 * p + 1][...].astype(bf16), dims,
                                preferred_element_type=f32)
            s = d if s is None else s + d

        @pl.when(k == 0)
        def _():
            acc[...] = s

        @pl.when(k > 0)
        def _():
            acc[...] += s

        if has_res or not direct:
            @pl.when(k == nk - 1)
            def _():
                r = acc[...]
                if has_res:
                    r = r + res_ref[...]
                o_ref[...] = r.astype(out_dtype)

    if mode == 'nn':
        a_spec = pl.BlockSpec((tm, tk), lambda i, k: (i, k))
        b_spec = pl.BlockSpec((tk, N), lambda i, k: (k, 0))
    elif mode == 'nt':
        a_spec = pl.BlockSpec((tm, tk), lambda i, k: (i, k))
        b_spec = pl.BlockSpec((N, tk), lambda i, k: (0, k))
    else:
        a_spec = pl.BlockSpec((tk, tm), lambda i, k: (k, i))
        b_spec = pl.BlockSpec((tk, N), lambda i, k: (k, 0))
    o_spec = pl.BlockSpec((tm, N), lambda i, k: (i, 0))
    in_specs = [a_spec, b_spec] * npair + ([o_spec] if has_res else [])
    args = [t for p in pairs for t in p] + ([res] if has_res else [])
    return _pc(body, grid=(M // tm, nk), in_specs=in_specs, out_specs=o_spec,
               out_shape=jax.ShapeDtypeStruct((M, N), out_dtype),
               scratch_shapes=[] if direct else [pltpu.VMEM((tm, N), f32)],
               compiler_params=_cp(("parallel", "arbitrary")), name=name)(*args)


def _rms_fwd(x, g, *, col0=0, width=None, n_valid=None, out_dtype=bf16, name="rms_fwd"):
    S = x.shape[0]
    width = width or x.shape[1]
    n_valid = n_valid or width
    ts = _pick(S, (512, 256, 128))
    cb = col0 // width

    def body(x_ref, g_ref, o_ref):
        xv = x_ref[...]
        ms = jnp.sum(xv * xv, axis=-1, keepdims=True) * (1.0 / n_valid)
        o_ref[...] = (xv * lax.rsqrt(ms + EPS) * g_ref[...]).astype(out_dtype)

    return _pc(body, grid=(S // ts,),
               in_specs=[pl.BlockSpec((ts, width), lambda i: (i, cb)), pl.BlockSpec((1, width), lambda i: (0, 0))],
               out_specs=pl.BlockSpec((ts, width), lambda i: (i, 0)),
               out_shape=jax.ShapeDtypeStruct((S, width), out_dtype),
               compiler_params=_cp(("parallel",)), name=name)(x, g.reshape(1, width))


def _rms_bwd(x, g, dy, *, col0=0, dcol0=0, width=None, n_valid=None, res=None, out_dtype=f32, delta=False,
             name="rms_bwd"):
    S = x.shape[0]
    width = width or x.shape[1]
    n_valid = n_valid or width
    ts = _pick(S, (512, 256, 128))
    cb, dcb = col0 // width, dcol0 // width
    has_res = res is not None

    def body(*refs):
        x_ref, g_ref, dy_ref = refs[:3]
        res_ref = refs[3] if has_res else None
        outs = refs[3 + has_res:]
        dx_ref, dg_ref = outs[0], outs[1]
        i = pl.program_id(0)
        xv = x_ref[...]
        gv = g_ref[...]
        dyv = dy_ref[...].astype(f32)
        rstd = lax.rsqrt(jnp.sum(xv * xv, axis=-1, keepdims=True) * (1.0 / n_valid) + EPS)
        xh = xv * rstd
        dxh = dyv * gv
        mean = jnp.sum(dxh * xh, axis=-1, keepdims=True) * (1.0 / n_valid)
        dx = rstd * (dxh - xh * mean)
        if delta:
            d_ref = outs[2]
            for h in range(width // LANES):
                sl = slice(h * LANES, (h + 1) * LANES)
                dsum = jnp.sum(dx[:, sl] * xv[:, sl], axis=-1, keepdims=True)
                d_ref[:, sl] = jnp.broadcast_to(dsum, (ts, LANES))
        if has_res:
            dx = dx + res_ref[...]
        dx_ref[...] = dx.astype(out_dtype)

        @pl.when(i == 0)
        def _():
            dg_ref[...] = jnp.zeros_like(dg_ref)

        dg_ref[...] += jnp.sum(dyv * xh, axis=0, keepdims=True)

    blk = lambda c: pl.BlockSpec((ts, width), lambda i: (i, c))
    in_specs = [blk(cb), pl.BlockSpec((1, width), lambda i: (0, 0)), blk(dcb)] + ([blk(0)] if has_res else [])
    out_specs = [blk(0), pl.BlockSpec((1, width), lambda i: (0, 0))] + ([blk(0)] if delta else [])
    out_shape = [jax.ShapeDtypeStruct((S, width), out_dtype), jax.ShapeDtypeStruct((1, width), f32)] + (
        [jax.ShapeDtypeStruct((S, width), f32)] if delta else [])
    args = [x, g.reshape(1, width), dy] + ([res] if has_res else [])
    return _pc(body, grid=(S // ts,), in_specs=in_specs, out_specs=out_specs, out_shape=out_shape,
               compiler_params=_cp(("arbitrary",)), name=name)(*args)


def _loss_head(h, g, target, name="loss_head"):
    S, D = h.shape
    ts = _pick(S, (512, 256, 128))

    def body(h_ref, g_ref, t_ref, dh_ref, dg_ref, loss_ref):
        i = pl.program_id(0)
        xv = h_ref[...]
        gv = g_ref[...]
        rstd = lax.rsqrt(jnp.sum(xv * xv, axis=-1, keepdims=True) * (1.0 / D) + EPS)
        xh = xv * rstd
        err = xh * gv - t_ref[...]
        dyv = err * (1.0 / D)
        dxh = dyv * gv
        mean = jnp.sum(dxh * xh, axis=-1, keepdims=True) * (1.0 / D)
        dh_ref[...] = rstd * (dxh - xh * mean)

        @pl.when(i == 0)
        def _():
            dg_ref[...] = jnp.zeros_like(dg_ref)
            loss_ref[...] = jnp.zeros_like(loss_ref)

        dg_ref[...] += jnp.sum(dyv * xh, axis=0, keepdims=True)
        part = jnp.sum(jnp.sum(err * err, axis=-1, keepdims=True), axis=0, keepdims=True) * (0.5 / D)
        loss_ref[...] += jnp.broadcast_to(part, (1, LANES))

    blk = pl.BlockSpec((ts, D), lambda i: (i, 0))
    row = pl.BlockSpec((1, D), lambda i: (0, 0))
    return _pc(body, grid=(S // ts,), in_specs=[blk, row, blk],
               out_specs=[blk, row, pl.BlockSpec((1, LANES), lambda i: (0, 0))],
               out_shape=[jax.ShapeDtypeStruct((S, D), f32), jax.ShapeDtypeStruct((1, D), f32),
                          jax.ShapeDtypeStruct((1, LANES), f32)],
               compiler_params=_cp(("arbitrary",)), name=name)(h, g.reshape(1, D), target)


def _rope_apply(x, tc, s1, s2):
    return x * tc + pltpu.roll(x, LANES - 16, 1) * s1 + pltpu.roll(x, 16, 1) * s2


def _rope_apply_t(dy, tc, s1, s2):
    return dy * tc + pltpu.roll(dy * s1, 16, 1) + pltpu.roll(dy * s2, LANES - 16, 1)


def _rope_fwd(q, kv, proj, tabs, name="rope_fwd"):
    S = q.shape[0]
    ts = _pick(S, (512, 256, 128))
    scale = (QK_NOPE + QK_ROPE) ** -0.5

    def body(q_ref, kk_ref, kvv_ref, kr_ref, tc_ref, s1_ref, s2_ref, qh_ref, kh_ref, vh_ref):
        tc, s1, s2 = tc_ref[...], s1_ref[...], s2_ref[...]
        krr = _rope_apply(pltpu.roll(kr_ref[...], QK_NOPE, 1), tc, s1, s2)
        for h in range(MLA_HEADS):
            sl = slice(h * HEAD_PAD, (h + 1) * HEAD_PAD)
            qh_ref[:, sl] = (_rope_apply(q_ref[:, sl], tc, s1, s2) * scale).astype(bf16)
            kh_ref[:, sl] = (kk_ref[:, sl] + krr).astype(bf16)
        vh_ref[...] = kvv_ref[...].astype(bf16)

    wide = lambda c: pl.BlockSpec((ts, MLA_PAD), lambda i: (i, c))
    tab = pl.BlockSpec((ts, LANES), lambda i: (i, 0))
    return _pc(body, grid=(S // ts,),
               in_specs=[wide(0), wide(0), wide(1), pl.BlockSpec((ts, LANES), lambda i: (i, 3)), tab, tab, tab],
               out_specs=[wide(0)] * 3, out_shape=[jax.ShapeDtypeStruct((S, MLA_PAD), bf16)] * 3,
               compiler_params=_cp(("parallel",)), name=name)(q, kv, kv, proj, *tabs)


def _rope_bwd(dqh, dkh, dvh, tabs, name="rope_bwd"):
    S = dqh.shape[0]
    ts = _pick(S, (512, 256, 128))
    scale = (QK_NOPE + QK_ROPE) ** -0.5

    def body(dq_ref, dk_ref, dv_ref, tc_ref, s1_ref, s2_ref, oq_ref, okv_ref, okr_ref):
        tc, s1, s2 = tc_ref[...], s1_ref[...], s2_ref[...]
        ksum = None
        for h in range(MLA_HEADS):
            sl = slice(h * HEAD_PAD, (h + 1) * HEAD_PAD)
            oq_ref[:, sl] = (_rope_apply_t(dq_ref[:, sl], tc, s1, s2) * scale).astype(bf16)
            dk = dk_ref[:, sl]
            okv_ref[:, sl] = dk.astype(bf16)
            ksum = dk if ksum is None else ksum + dk
        okv_ref[:, MLA_PAD:] = dv_ref[...].astype(bf16)
        dkr = pltpu.roll(_rope_apply_t(ksum, tc, s1, s2), LANES - QK_NOPE, 1)
        lane = lax.broadcasted_iota(jnp.int32, (ts, LANES), 1)
        okr_ref[...] = jnp.where(lane < QK_ROPE, dkr, 0.0).astype(bf16)

    wide = pl.BlockSpec((ts, MLA_PAD), lambda i: (i, 0))
    tab = pl.BlockSpec((ts, LANES), lambda i: (i, 0))
    return _pc(body, grid=(S // ts,), in_specs=[wide, wide, wide, tab, tab, tab],
               out_specs=[wide, pl.BlockSpec((ts, 2 * MLA_PAD), lambda i: (i, 0)), tab],
               out_shape=[jax.ShapeDtypeStruct((S, MLA_PAD), bf16), jax.ShapeDtypeStruct((S, 2 * MLA_PAD), bf16),
                          jax.ShapeDtypeStruct((S, LANES), bf16)],
               compiler_params=_cp(("parallel",)), name=name)(dqh, dkh, dvh, *tabs)


ATT_BLK = 1024
_DIAG_QUARTERS = ((0, 0), (1, 0), (1, 1))


def _attn_fwd(qh, kh, vh, plan=None, name="attn_fwd"):
    S = qh.shape[0]
    tq = tk = min(S, ATT_BLK)
    nq, nk = S // tq, S // tk
    npl = plan['n'] if plan else 0

    def body(*refs):
        q_ref, k_ref, v_ref = refs[:3]
        o_ref, lse_ref = refs[3 + npl:5 + npl]
        m_sc, l_sc, acc_sc = refs[5 + 2 * npl:8 + 2 * npl]
        h, i, j = pl.program_id(0), pl.program_id(1), pl.program_id(2)
        if plan:
            pargs = (refs[3:3 + npl], refs[5 + npl:5 + 2 * npl], refs[8 + 2 * npl], refs[9 + 2 * npl])
            first = (i == 0) & (j == 0)
            pl.when((h == 0) & first)(functools.partial(plan['start'], *pargs))
            pl.when((h == (3 * MLA_HEADS) // 4) & first)(functools.partial(plan['forward'], *pargs))
            pl.when((h == MLA_HEADS - 1) & (i == nq - 1) & (j == nk - 1))(functools.partial(plan['finish'], *pargs))

        @pl.when(j == 0)
        def _():
            m_sc[...] = jnp.full_like(m_sc, -1e30)
            l_sc[...] = jnp.zeros_like(l_sc)
            acc_sc[...] = jnp.zeros_like(acc_sc)

        def part(rows, cols, n, masked):
            s = lax.dot_general(q_ref[rows, :], k_ref[cols, :], (((1,), (1,)), ((), ())), preferred_element_type=f32)
            if masked:
                row = lax.broadcasted_iota(jnp.int32, (n, n), 0)
                col = lax.broadcasted_iota(jnp.int32, (n, n), 1)
                s = jnp.where(col <= row, s, -1e30)
            m_prev = m_sc[rows, :]
            m_new = jnp.maximum(m_prev, jnp.max(s, axis=-1, keepdims=True))
            alpha = jnp.exp(m_prev - m_new)
            p = jnp.exp(s - m_new)
            l_sc[rows, :] = alpha * l_sc[rows, :] + jnp.sum(p, axis=-1, keepdims=True)
            acc_sc[rows, :] = alpha * acc_sc[rows, :] + jnp.dot(p.astype(bf16), v_ref[cols, :],
                                                                  preferred_element_type=f32)
            m_sc[rows, :] = m_new

        whole = (slice(0, tq), slice(0, tk), tq)
        pl.when(j < i)(functools.partial(part, *whole, False))
        pl.when(j == i)(functools.partial(part, *whole, True))

        @pl.when(j == nk - 1)
        def _():
            l = l_sc[...]
            o_ref[...] = acc_sc[...] / l
            lse_ref[...] = jnp.broadcast_to(m_sc[...] + jnp.log(l), (tq, LANES))

    qspec = pl.BlockSpec((tq, HEAD_PAD), lambda h, i, j: (i, h))
    kspec = pl.BlockSpec((tk, HEAD_PAD), lambda h, i, j: (jnp.minimum(j, i), h))
    anyspec = pl.BlockSpec(memory_space=pl.ANY)
    scratch = [pltpu.VMEM((tq, 1), f32), pltpu.VMEM((tq, 1), f32), pltpu.VMEM((tq, HEAD_PAD), f32)]
    if plan:
        scratch += [pltpu.SemaphoreType.DMA((plan['nsem'],)), pltpu.SemaphoreType.DMA((plan['nsem'],))]
    outs = _pc(body, grid=(MLA_HEADS, nq, nk), in_specs=[qspec, kspec, kspec] + [anyspec] * npl,
               out_specs=[qspec, qspec] + [anyspec] * npl,
               out_shape=[jax.ShapeDtypeStruct((S, MLA_PAD), f32)] * 2 + (plan['outs'] if plan else []),
               scratch_shapes=scratch,
               compiler_params=_cp(("arbitrary", "arbitrary", "arbitrary") if plan else ("parallel", "parallel", "arbitrary")),
               name=name)(qh, kh, vh, *(plan['ins'] if plan else []))
    return outs[0], outs[1], outs[2:]


def _attn_bwd(qh, kh, vh, do, lse, delta, plan=None, name="attn_bwd"):
    S = qh.shape[0]
    tq = tk = min(S, ATT_BLK)
    nq, nk = S // tq, S // tk
    npl = plan['n'] if plan else 0

    def body(*refs):
        q_ref, k_ref, v_ref, do_ref, lse_ref, dl_ref = refs[:6]
        dq_ref, dk_ref, dv_ref = refs[6 + npl:9 + npl]
        h, j, i = pl.program_id(0), pl.program_id(1), pl.program_id(2)
        if plan:
            pargs = (refs[6:6 + npl], refs[9 + npl:9 + 2 * npl], refs[9 + 2 * npl], refs[10 + 2 * npl])
            pl.when((h == 0) & (j == 0) & (i == 0))(functools.partial(plan['start'], *pargs))
            pl.when((h == MLA_HEADS - 1) & (j == nk - 1) & (i == nq - 1))(functools.partial(plan['finish'], *pargs))

        @pl.when((j == 0) & (i == 0))
        def _():
            dq_ref[...] = jnp.zeros_like(dq_ref)

        @pl.when(i == 0)
        def _():
            dk_ref[...] = jnp.zeros_like(dk_ref)
            dv_ref[...] = jnp.zeros_like(dv_ref)

        def part(r0, c0, n, masked):
            nt = (((1,), (1,)), ((), ()))
            tn = (((0,), (0,)), ((), ()))
            rows, cols = slice(r0, r0 + n), slice(c0, c0 + n)
            qv, kv_, dov = q_ref[rows, :], k_ref[cols, :], do_ref[rows, :]
            s = lax.dot_general(qv, kv_, nt, preferred_element_type=f32)
            p = jnp.exp(s - lse_ref[rows, :1])
            if masked:
                row = lax.broadcasted_iota(jnp.int32, (n, n), 0)
                col = lax.broadcasted_iota(jnp.int32, (n, n), 1)
                p = jnp.where(col <= row, p, 0.0)
            dp = lax.dot_general(dov, v_ref[cols, :], nt, preferred_element_type=f32)
            ds = (p * (dp - dl_ref[rows, :1])).astype(bf16)
            dv_ref[cols, :] += lax.dot_general(p.astype(bf16), dov, tn, preferred_element_type=f32)
            dk_ref[cols, :] += lax.dot_general(ds, qv, tn, preferred_element_type=f32)
            qrows = pl.ds(pl.multiple_of(i * tq + r0, n), n)
            dq_ref[qrows, :] += jnp.dot(ds, kv_, preferred_element_type=f32)

        def below():
            part(0, 0, tq, False)

        def diagonal():
            for r0, c0 in _DIAG_QUARTERS:
                part(r0 * hq, c0 * hq, hq, r0 == c0)

        hq = tq // 2
        pl.when(i > j)(below)
        pl.when(i == j)(diagonal)

    qspec = pl.BlockSpec((tq, HEAD_PAD), lambda h, j, i: (jnp.maximum(i, j), h))
    kspec = pl.BlockSpec((tk, HEAD_PAD), lambda h, j, i: (j, h))
    colspec = pl.BlockSpec((S, HEAD_PAD), lambda h, j, i: (0, h))
    anyspec = pl.BlockSpec(memory_space=pl.ANY)
    scratch = [pltpu.SemaphoreType.DMA((plan['nsem'],)), pltpu.SemaphoreType.DMA((plan['nsem'],))] if plan else []
    outs = _pc(body, grid=(MLA_HEADS, nk, nq), in_specs=[qspec, kspec, kspec, qspec, qspec, qspec] + [anyspec] * npl,
               out_specs=[colspec, kspec, kspec] + [anyspec] * npl,
               out_shape=[jax.ShapeDtypeStruct((S, MLA_PAD), f32)] * 3 + (plan['outs'] if plan else []),
               scratch_shapes=scratch,
               compiler_params=_cp(("arbitrary" if plan else "parallel", "arbitrary", "arbitrary")),
               name=name)(qh, kh, vh, do, lse, delta, *(plan['ins'] if plan else []))
    return outs[0], outs[1], outs[2], outs[3:]


def _xattn_fwd(q, kv, name="xattn_fwd"):
    S = q.shape[0]
    M = kv.shape[0]
    tq = _pick(S, (512, 256, 128))
    scale = X_HEAD_DIM ** -0.5

    def body(q_ref, kv_ref, o_ref):
        for h in range(X_HEADS):
            sl = slice(h * X_HEAD_DIM, (h + 1) * X_HEAD_DIM)
            k = kv_ref[:, sl]
            v = kv_ref[:, D_MODEL + h * X_HEAD_DIM:D_MODEL + (h + 1) * X_HEAD_DIM]
            s = lax.dot_general(q_ref[:, sl], k, (((1,), (1,)), ((), ())), preferred_element_type=f32) * scale
            e = jnp.exp(s - jnp.max(s, axis=-1, keepdims=True))
            p = e / jnp.sum(e, axis=-1, keepdims=True)
            o_ref[:, sl] = jnp.dot(p.astype(bf16), v, preferred_element_type=f32).astype(bf16)

    blk = pl.BlockSpec((tq, D_MODEL), lambda i: (i, 0))
    return _pc(body, grid=(S // tq,), in_specs=[blk, pl.BlockSpec((M, 2 * D_MODEL), lambda i: (0, 0))],
               out_specs=blk, out_shape=jax.ShapeDtypeStruct((S, D_MODEL), bf16),
               compiler_params=_cp(("parallel",)), name=name)(q, kv)


def _xattn_bwd(q, kv, do, plan=None, name="xattn_bwd"):
    S = q.shape[0]
    M = kv.shape[0]
    tq = _pick(S, (512, 256, 128))
    scale = X_HEAD_DIM ** -0.5
    npl = plan['n'] if plan else 0
    p_in, p_out, p_shapes, p_sems, p_args = _plan_extras(plan)

    def body(*refs):
        q_ref, kv_ref, do_ref = refs[:3]
        dq_ref, dkv_ref = refs[3 + npl:5 + npl]
        i = pl.program_id(0)
        if plan:
            pargs = (refs[3:3 + npl], refs[5 + npl:5 + 2 * npl], refs[5 + 2 * npl], refs[6 + 2 * npl])
            pl.when(i == 0)(functools.partial(plan['start'], *pargs))
            pl.when(i == S // tq - 1)(functools.partial(plan['finish'], *pargs))

        @pl.when(i == 0)
        def _():
            dkv_ref[...] = jnp.zeros_like(dkv_ref)

        nt = (((1,), (1,)), ((), ()))
        tn = (((0,), (0,)), ((), ()))
        for h in range(X_HEADS):
            sl = slice(h * X_HEAD_DIM, (h + 1) * X_HEAD_DIM)
            vsl = slice(D_MODEL + h * X_HEAD_DIM, D_MODEL + (h + 1) * X_HEAD_DIM)
            k, v, qv, dov = kv_ref[:, sl], kv_ref[:, vsl], q_ref[:, sl], do_ref[:, sl]
            s = lax.dot_general(qv, k, nt, preferred_element_type=f32) * scale
            e = jnp.exp(s - jnp.max(s, axis=-1, keepdims=True))
            p = e / jnp.sum(e, axis=-1, keepdims=True)
            dp = lax.dot_general(dov, v, nt, preferred_element_type=f32)
            ds = (p * (dp - jnp.sum(dp * p, axis=-1, keepdims=True)) * scale).astype(bf16)
            dq_ref[:, sl] = jnp.dot(ds, k, preferred_element_type=f32).astype(bf16)
            dkv_ref[:, sl] += lax.dot_general(ds, qv, tn, preferred_element_type=f32)
            dkv_ref[:, vsl] += lax.dot_general(p.astype(bf16), dov, tn, preferred_element_type=f32)

    blk = pl.BlockSpec((tq, D_MODEL), lambda i: (i, 0))
    full = pl.BlockSpec((M, 2 * D_MODEL), lambda i: (0, 0))
    outs = _pc(body, grid=(S // tq,), in_specs=[blk, full, blk] + p_in, out_specs=[blk, full] + p_out,
               out_shape=[jax.ShapeDtypeStruct((S, D_MODEL), bf16), jax.ShapeDtypeStruct((M, 2 * D_MODEL), f32)] + p_shapes,
               scratch_shapes=p_sems, compiler_params=_cp(("arbitrary",)), name=name)(q, kv, do, *p_args)
    return outs[0], outs[1], outs[2:]


ST_CHUNKS = 1


def _apow_init(a_ref, ap_ref, bp_ref, seg):
    P = MACRO_ST
    ar, ai = a_ref[:, :P], a_ref[:, P:]
    pr, pi = ar, ai
    for r in range(seg):
        ap_ref[r:r + 1, :P] = pr
        ap_ref[r:r + 1, P:] = pi
        if r < seg - 1:
            pr, pi = pr * ar - pi * ai, pr * ai + pi * ar
    br, bi = pr, pi
    for k in range(SUBLANES):
        bp_ref[k:k + 1, :P] = pr
        bp_ref[k:k + 1, P:] = pi
        pr, pi = pr * br - pi * bi, pr * bi + pi * br


def _segment_perm(tS):
    seg = tS // SUBLANES
    rows = jnp.arange(tS)
    src = (rows % SUBLANES) * seg + rows // SUBLANES
    return (src[:, None] == jnp.arange(tS)[None, :]).astype(f32)


def _unpermute_rows(pt, v):
    hi = v.astype(bf16)
    r1 = v - hi.astype(f32)
    mid = r1.astype(bf16)
    lo = (r1 - mid.astype(f32)).astype(bf16)
    out = jnp.dot(pt, jnp.concatenate([hi, mid, lo], axis=1), preferred_element_type=f32)
    w = v.shape[1]
    return (out[:, :w] + out[:, w:2 * w]) + out[:, 2 * w:]


def _scan_block(sc_ref, ap_ref, bp_ref, carry_ref, e_ref, seg, reverse):
    P = MACRO_ST
    sgn = -1.0 if reverse else 1.0
    CH = P // ST_CHUNKS
    rid = lax.broadcasted_iota(jnp.int32, (SUBLANES, CH), 0)
    for c in range(ST_CHUNKS):
        lr, li = slice(c * CH, (c + 1) * CH), slice(P + c * CH, P + (c + 1) * CH)
        ar, ai = ap_ref[0:1, lr], sgn * ap_ref[0:1, li]
        xr = xi = None
        for i in range(seg):
            r = seg - 1 - i if reverse else i
            rows = slice(SUBLANES * r, SUBLANES * (r + 1))
            sr, si = sc_ref[rows, lr], sc_ref[rows, li]
            if i == 0:
                xr, xi = sr, si
            else:
                xr, xi = ar * xr - ai * xi + sr, ar * xi + ai * xr + si
                sc_ref[rows, lr] = xr
                sc_ref[rows, li] = xi
        for sh in (1, 2, 4):
            pr, pi = bp_ref[sh - 1:sh, lr], sgn * bp_ref[sh - 1:sh, li]
            if reverse:
                tr = jnp.where(rid < SUBLANES - sh, pltpu.roll(xr, SUBLANES - sh, 0), 0.0)
                ti = jnp.where(rid < SUBLANES - sh, pltpu.roll(xi, SUBLANES - sh, 0), 0.0)
            else:
                tr = jnp.where(rid >= sh, pltpu.roll(xr, sh, 0), 0.0)
                ti = jnp.where(rid >= sh, pltpu.roll(xi, sh, 0), 0.0)
            xr, xi = xr + pr * tr - pi * ti, xi + pr * ti + pi * tr
        if reverse:
            bpr = jnp.zeros((SUBLANES, CH), f32)
            bpi = jnp.zeros((SUBLANES, CH), f32)
            for r in range(SUBLANES):
                bpr = jnp.where(rid == r, bp_ref[SUBLANES - 1 - r:SUBLANES - r, lr], bpr)
                bpi = jnp.where(rid == r, -bp_ref[SUBLANES - 1 - r:SUBLANES - r, li], bpi)
        else:
            bpr, bpi = bp_ref[:, lr], bp_ref[:, li]
        cr, cim = carry_ref[:, lr], carry_ref[:, li]
        xr, xi = xr + bpr * cr - bpi * cim, xi + bpr * cim + bpi * cr
        edge = 0 if reverse else SUBLANES - 1
        carry_ref[:, lr] = jnp.sum(jnp.where(rid == edge, xr, 0.0), axis=0, keepdims=True)
        carry_ref[:, li] = jnp.sum(jnp.where(rid == edge, xi, 0.0), axis=0, keepdims=True)
        if reverse:
            er = jnp.where(rid == SUBLANES - 1, cr, pltpu.roll(xr, SUBLANES - 1, 0))
            ei = jnp.where(rid == SUBLANES - 1, cim, pltpu.roll(xi, SUBLANES - 1, 0))
        else:
            er = jnp.where(rid == 0, cr, pltpu.roll(xr, 1, 0))
            ei = jnp.where(rid == 0, cim, pltpu.roll(xi, 1, 0))
        if e_ref is not None:
            e_ref[:, lr] = er
            e_ref[:, li] = ei
        for i in range(seg):
            r = seg - 1 - i if reverse else i
            rows = slice(SUBLANES * r, SUBLANES * (r + 1))
            pr, pi = ap_ref[i:i + 1, lr], sgn * ap_ref[i:i + 1, li]
            sc_ref[rows, lr] += pr * er - pi * ei
            sc_ref[rows, li] += pr * ei + pi * er


def _ssm_fwd(proj, bm, cm, a, d, name="ssm_fwd"):
    S = proj.shape[0]
    tS = _pick(S, (256, 128))
    nb = S // tS
    P2 = 2 * MACRO_ST
    seg = tS // SUBLANES
    ucol0 = (D_MODEL - SSM_WIDTH) // MACRO_CH

    perm = _segment_perm(tS)

    def body(u_ref, b_ref, c_ref, a_ref, d_ref, pm_ref, pt_ref, y_ref, xc_ref, bu_sc, ap_sc, bp_sc, car_sc):
        t = pl.program_id(1)

        @pl.when(t == 0)
        def _():
            _apow_init(a_ref, ap_sc, bp_sc, seg)
            car_sc[...] = jnp.zeros_like(car_sc)

        uv = u_ref[...]
        up = jnp.dot(pm_ref[...], uv.astype(bf16), preferred_element_type=f32).astype(bf16)
        bu_sc[...] = jnp.dot(up, b_ref[...], preferred_element_type=f32)
        xc_ref[...] = car_sc[...]
        _scan_block(bu_sc, ap_sc, bp_sc, car_sc, None, seg, False)
        yp = jnp.dot(bu_sc[...].astype(bf16), c_ref[...], preferred_element_type=f32)
        y_ref[...] = _unpermute_rows(pt_ref[...], yp) + d_ref[...] * uv

    sq = pl.BlockSpec((tS, tS), lambda m, t: (0, 0))
    return _pc(body, grid=(SSM_MACRO, nb),
               in_specs=[pl.BlockSpec((tS, MACRO_CH), lambda m, t: (t, ucol0 + m)),
                         pl.BlockSpec((None, MACRO_CH, P2), lambda m, t: (m, 0, 0)),
                         pl.BlockSpec((None, P2, MACRO_CH), lambda m, t: (m, 0, 0)),
                         pl.BlockSpec((None, 1, P2), lambda m, t: (m, 0, 0)),
                         pl.BlockSpec((1, MACRO_CH), lambda m, t: (0, m)), sq, sq],
               out_specs=[pl.BlockSpec((tS, MACRO_CH), lambda m, t: (t, m)),
                          pl.BlockSpec((None, None, 1, P2), lambda m, t: (m, t, 0, 0))],
               out_shape=[jax.ShapeDtypeStruct((S, SSM_WIDTH), f32), jax.ShapeDtypeStruct((SSM_MACRO, nb, 1, P2), f32)],
               scratch_shapes=[pltpu.VMEM((tS, P2), f32), pltpu.VMEM((seg, P2), f32),
                               pltpu.VMEM((SUBLANES, P2), f32), pltpu.VMEM((1, P2), f32)],
               compiler_params=_cp(("arbitrary", "arbitrary")), name=name)(
        proj, bm, cm, a, d.reshape(1, SSM_WIDTH), perm.astype(bf16), perm.T.astype(bf16))


def _ssm_bwd(proj, dy, xc, bm, cm, a, d, plan=None, name="ssm_bwd"):
    S = proj.shape[0]
    tS = _pick(S, (256, 128))
    nb = S // tS
    P = MACRO_ST
    P2 = 2 * P
    seg = tS // SUBLANES
    ucol0 = (D_MODEL - SSM_WIDTH) // MACRO_CH

    perm = _segment_perm(tS)
    npl = plan['n'] if plan else 0
    p_in, p_out, p_shapes, p_sems, p_args = _plan_extras(plan)

    def body(*refs):
        u_ref, dy_ref, xc_ref, b_ref, c_ref, a_ref, d_ref, pm_ref, pt_ref = refs[:9]
        du_ref, db_ref, dc_ref, da_ref, dd_ref = refs[9 + npl:14 + npl]
        x_sc, g_sc, ap_sc, bp_sc, e_sc, xcar_sc, gcar_sc = refs[14 + 2 * npl:21 + 2 * npl]
        t = pl.program_id(1)
        if plan:
            mg = pl.program_id(0)
            pargs = (refs[9:9 + npl], refs[14 + npl:14 + 2 * npl], refs[21 + 2 * npl], refs[22 + 2 * npl])
            pl.when((mg == 0) & (t == 0))(functools.partial(plan['start'], *pargs))
            pl.when((mg == SSM_MACRO - 1) & (t == nb - 1))(functools.partial(plan['finish'], *pargs))

        @pl.when(t == 0)
        def _():
            _apow_init(a_ref, ap_sc, bp_sc, seg)
            gcar_sc[...] = jnp.zeros_like(gcar_sc)
            db_ref[...] = jnp.zeros_like(db_ref)
            dc_ref[...] = jnp.zeros_like(dc_ref)
            da_ref[...] = jnp.zeros_like(da_ref)
            dd_ref[...] = jnp.zeros_like(dd_ref)

        nt = (((1,), (1,)), ((), ()))
        tn = (((0,), (0,)), ((), ()))
        uv = u_ref[...]
        dyv = dy_ref[...]
        pm = pm_ref[...]
        ub = jnp.dot(pm, uv.astype(bf16), preferred_element_type=f32).astype(bf16)
        dyb = jnp.dot(pm, dyv.astype(bf16), preferred_element_type=f32).astype(bf16)
        x_sc[...] = jnp.dot(ub, b_ref[...], preferred_element_type=f32)
        xcar_sc[...] = xc_ref[...]
        _scan_block(x_sc, ap_sc, bp_sc, xcar_sc, e_sc, seg, False)
        g_sc[...] = lax.dot_general(dyb, c_ref[...], nt, preferred_element_type=f32)
        _scan_block(g_sc, ap_sc, bp_sc, gcar_sc, None, seg, True)
        xv = x_sc[...]
        gv = g_sc[...]
        gb = gv.astype(bf16)
        dc_ref[...] += lax.dot_general(xv.astype(bf16), dyb, tn, preferred_element_type=f32)
        db_ref[...] += lax.dot_general(ub, gb, tn, preferred_element_type=f32)
        dup = lax.dot_general(gb, b_ref[...], nt, preferred_element_type=f32)
        du_ref[...] = _unpermute_rows(pt_ref[...], dup) + d_ref[...] * dyv
        dd_ref[...] += jnp.sum(dyv * uv, axis=0, keepdims=True)
        xp = jnp.concatenate([e_sc[...], xv[:tS - SUBLANES]], axis=0)
        xpr, xpi, ggr, ggi = xp[:, :P], xp[:, P:], gv[:, :P], gv[:, P:]
        da_ref[:, :P] += jnp.sum(ggr * xpr + ggi * xpi, axis=0, keepdims=True)
        da_ref[:, P:] += jnp.sum(ggi * xpr - ggr * xpi, axis=0, keepdims=True)

    rev = lambda t: nb - 1 - t
    outs = _pc(body, grid=(SSM_MACRO, nb),
               in_specs=[pl.BlockSpec((tS, MACRO_CH), lambda m, t: (rev(t), ucol0 + m)),
                         pl.BlockSpec((tS, MACRO_CH), lambda m, t: (rev(t), m)),
                         pl.BlockSpec((None, None, 1, P2), lambda m, t: (m, rev(t), 0, 0)),
                         pl.BlockSpec((None, MACRO_CH, P2), lambda m, t: (m, 0, 0)),
                         pl.BlockSpec((None, P2, MACRO_CH), lambda m, t: (m, 0, 0)),
                         pl.BlockSpec((None, 1, P2), lambda m, t: (m, 0, 0)),
                         pl.BlockSpec((1, MACRO_CH), lambda m, t: (0, m)),
                         pl.BlockSpec((tS, tS), lambda m, t: (0, 0)), pl.BlockSpec((tS, tS), lambda m, t: (0, 0))] + p_in,
               out_specs=[pl.BlockSpec((tS, MACRO_CH), lambda m, t: (rev(t), m)),
                          pl.BlockSpec((None, MACRO_CH, P2), lambda m, t: (m, 0, 0)),
                          pl.BlockSpec((None, P2, MACRO_CH), lambda m, t: (m, 0, 0)),
                          pl.BlockSpec((None, 1, P2), lambda m, t: (m, 0, 0)),
                          pl.BlockSpec((1, MACRO_CH), lambda m, t: (0, m))] + p_out,
               out_shape=[jax.ShapeDtypeStruct((S, SSM_WIDTH), f32),
                          jax.ShapeDtypeStruct((SSM_MACRO, MACRO_CH, P2), f32),
                          jax.ShapeDtypeStruct((SSM_MACRO, P2, MACRO_CH), f32),
                          jax.ShapeDtypeStruct((SSM_MACRO, 1, P2), f32),
                          jax.ShapeDtypeStruct((1, SSM_WIDTH), f32)] + p_shapes,
               scratch_shapes=[pltpu.VMEM((tS, P2), f32), pltpu.VMEM((tS, P2), f32),
                               pltpu.VMEM((seg, P2), f32), pltpu.VMEM((SUBLANES, P2), f32), pltpu.VMEM((SUBLANES, P2), f32),
                               pltpu.VMEM((1, P2), f32), pltpu.VMEM((1, P2), f32)] + p_sems,
               compiler_params=_cp(("arbitrary", "arbitrary")), name=name)(
        proj, dy, xc, bm, cm, a, d.reshape(1, SSM_WIDTH), perm.astype(bf16), perm.T.astype(bf16), *p_args)
    return outs[0], outs[1], outs[2], outs[3], outs[4], outs[5:]


_GELU_K = math.sqrt(2.0 / math.pi)
_GELU_C = 0.044715


def _glu_fwd(y, w, b, g, name="glu_fwd"):
    S, W = y.shape
    ts = _pick(S, (512, 256, 128))

    def body(y_ref, w_ref, b_ref, g_ref, z_ref, sn_ref, ge_ref):
        yv = y_ref[...]
        cdf = 0.5 * (1.0 + jnp.tanh(_GELU_K * (yv + _GELU_C * (yv * yv * yv))))
        ge = (yv * cdf).astype(bf16)
        z = jnp.dot(ge, w_ref[...], preferred_element_type=f32) + b_ref[...]
        s = yv * jax.nn.sigmoid(z)
        rstd = lax.rsqrt(jnp.sum(s * s, axis=-1, keepdims=True) * (1.0 / W) + EPS)
        z_ref[...] = z
        sn_ref[...] = (s * rstd * g_ref[...]).astype(bf16)
        ge_ref[...] = ge

    blk = pl.BlockSpec((ts, W), lambda i: (i, 0))
    row = pl.BlockSpec((1, W), lambda i: (0, 0))
    return _pc(body, grid=(S // ts,), in_specs=[blk, pl.BlockSpec((W, W), lambda i: (0, 0)), row, row],
               out_specs=[blk, blk, blk],
               out_shape=[jax.ShapeDtypeStruct((S, W), f32), jax.ShapeDtypeStruct((S, W), bf16),
                          jax.ShapeDtypeStruct((S, W), bf16)],
               compiler_params=_cp(("parallel",)), name=name)(y, w, b.reshape(1, W), g.reshape(1, W))


def _glu_bwd(y, z, dmixed, w, g, name="glu_bwd"):
    S, W = y.shape
    ts = _pick(S, (512, 256, 128))
    dcb = MLA_PAD // W

    def body(y_ref, z_ref, dsn_ref, w_ref, g_ref, dy_ref, dz_ref, dg_ref, db_ref):
        i = pl.program_id(0)
        yv, zv, gv = y_ref[...], z_ref[...], g_ref[...]
        sig = jax.nn.sigmoid(zv)
        s = yv * sig
        rstd = lax.rsqrt(jnp.sum(s * s, axis=-1, keepdims=True) * (1.0 / W) + EPS)
        sh = s * rstd
        dsn = dsn_ref[...]
        dsh = dsn * gv
        ds = rstd * (dsh - sh * (jnp.sum(dsh * sh, axis=-1, keepdims=True) * (1.0 / W)))
        dz = ds * s * (1.0 - sig)
        dzb = dz.astype(bf16)
        dge = lax.dot_general(dzb, w_ref[...], (((1,), (1,)), ((), ())), preferred_element_type=f32)
        t = jnp.tanh(_GELU_K * (yv + _GELU_C * (yv * yv * yv)))
        dgelu = 0.5 * (1.0 + t) + 0.5 * yv * (1.0 - t * t) * _GELU_K * (1.0 + 3.0 * _GELU_C * yv * yv)
        dy_ref[...] = ds * sig + dge * dgelu
        dz_ref[...] = dzb

        @pl.when(i == 0)
        def _():
            dg_ref[...] = jnp.zeros_like(dg_ref)
            db_ref[...] = jnp.zeros_like(db_ref)

        dg_ref[...] += jnp.sum(dsn * sh, axis=0, keepdims=True)
        db_ref[...] += jnp.sum(dz, axis=0, keepdims=True)

    blk = pl.BlockSpec((ts, W), lambda i: (i, 0))
    row = pl.BlockSpec((1, W), lambda i: (0, 0))
    return _pc(body, grid=(S // ts,),
               in_specs=[blk, blk, pl.BlockSpec((ts, W), lambda i: (i, dcb)), pl.BlockSpec((W, W), lambda i: (0, 0)), row],
               out_specs=[blk, blk, row, row],
               out_shape=[jax.ShapeDtypeStruct((S, W), f32), jax.ShapeDtypeStruct((S, W), bf16),
                          jax.ShapeDtypeStruct((1, W), f32), jax.ShapeDtypeStruct((1, W), f32)],
               compiler_params=_cp(("arbitrary",)), name=name)(y, z, dmixed, w, g.reshape(1, W))


def _ffn_up(hn, wg, wu, name="ffn_up"):
    S, K = hn.shape
    F = wg.shape[0]
    tm, tn = _pick(S, (512, 256, 128)), _pick(F, (1408, 256, 128))

    def body(h_ref, wg_ref, wu_ref, g_ref, u_ref, a_ref):
        hv = h_ref[...]
        nt = (((1,), (1,)), ((), ()))
        gv = lax.dot_general(hv, wg_ref[...], nt, preferred_element_type=f32)
        uv = lax.dot_general(hv, wu_ref[...], nt, preferred_element_type=f32)
        g_ref[...] = gv.astype(bf16)
        u_ref[...] = uv.astype(bf16)
        a_ref[...] = (gv * jax.nn.sigmoid(gv) * uv).astype(bf16)

    wspec = pl.BlockSpec((tn, K), lambda i, j: (j, 0))
    ospec = pl.BlockSpec((tm, tn), lambda i, j: (i, j))
    return _pc(body, grid=(S // tm, F // tn), in_specs=[pl.BlockSpec((tm, K), lambda i, j: (i, 0)), wspec, wspec],
               out_specs=[ospec] * 3,
               out_shape=[jax.ShapeDtypeStruct((S, F), bf16), jax.ShapeDtypeStruct((S, F), bf16),
                          jax.ShapeDtypeStruct((S, F), bf16)],
               compiler_params=_cp(("parallel", "parallel")), name=name)(hn, wg, wu)


def _ffn_bwd_act(dh, wd, gate, up, name="ffn_bwd_act"):
    S, K = dh.shape
    F = wd.shape[0]
    tm, tn = _pick(S, (512, 256, 128)), _pick(F, (1408, 256, 128))

    def body(dh_ref, wd_ref, g_ref, u_ref, dg_ref, du_ref):
        dact = lax.dot_general(dh_ref[...].astype(bf16), wd_ref[...], (((1,), (1,)), ((), ())),
                               preferred_element_type=f32)
        gv, uv = g_ref[...].astype(f32), u_ref[...].astype(f32)
        sig = jax.nn.sigmoid(gv)
        dg_ref[...] = (dact * uv * (sig * (1.0 + gv * (1.0 - sig)))).astype(bf16)
        du_ref[...] = (dact * (gv * sig)).astype(bf16)

    ospec = pl.BlockSpec((tm, tn), lambda i, j: (i, j))
    return _pc(body, grid=(S // tm, F // tn),
               in_specs=[pl.BlockSpec((tm, K), lambda i, j: (i, 0)), pl.BlockSpec((tn, K), lambda i, j: (j, 0)),
                         ospec, ospec],
               out_specs=[ospec] * 2, out_shape=[jax.ShapeDtypeStruct((S, F), bf16)] * 2,
               compiler_params=_cp(("parallel", "parallel")), name=name)(dh, wd, gate, up)


def _pad_heads(w, per_head, pieces):
    K = w.shape[0]
    w3 = w.reshape(K, MLA_HEADS, per_head)
    out = jnp.zeros((K, MLA_HEADS, HEAD_PAD), w.dtype)
    for s0, s1, d0 in pieces:
        out = out.at[:, :, d0:d0 + (s1 - s0)].set(w3[:, :, s0:s1])
    return out.reshape(K, MLA_PAD)


def _unpad_heads(wp, per_head, pieces):
    K = wp.shape[0]
    w3 = wp.reshape(K, MLA_HEADS, HEAD_PAD)
    out = jnp.zeros((K, MLA_HEADS, per_head), wp.dtype)
    for s0, s1, d0 in pieces:
        out = out.at[:, :, s0:s1].set(w3[:, :, d0:d0 + (s1 - s0)])
    return out.reshape(K, MLA_HEADS * per_head)


_Q_PIECES = [(0, QK_NOPE + QK_ROPE, 0)]
_K_PIECES = [(0, QK_NOPE, 0)]
_V_PIECES = [(QK_NOPE, QK_NOPE + V_HEAD, 0)]
_KR0 = Q_LORA + KV_LORA


def _pack_win(w):
    z = jnp.zeros((w.shape[0], LANES - QK_ROPE), w.dtype)
    return jnp.concatenate([w[:, :_KR0 + QK_ROPE], z, w[:, _KR0 + QK_ROPE:]], axis=1)


def _unpack_win(wp):
    return jnp.concatenate([wp[:, :_KR0 + QK_ROPE], wp[:, _KR0 + LANES:]], axis=1)


def _pack_wout(w):
    wa = w[:MLA_WIDTH].reshape(MLA_HEADS, V_HEAD, D_MODEL)
    wa = jnp.concatenate([wa, jnp.zeros_like(wa)], axis=1).reshape(MLA_PAD, D_MODEL)
    return jnp.concatenate([wa, w[MLA_WIDTH:]], axis=0)


def _unpack_wout(wp):
    wa = wp[:MLA_PAD].reshape(MLA_HEADS, HEAD_PAD, D_MODEL)[:, :V_HEAD].reshape(MLA_WIDTH, D_MODEL)
    return jnp.concatenate([wa, wp[MLA_PAD:]], axis=0)


def _pad_gain(g):
    g2 = g.reshape(MLA_HEADS, V_HEAD)
    return jnp.concatenate([g2, jnp.zeros_like(g2)], axis=1).reshape(MLA_PAD)


def _unpad_gain(gp):
    return gp.reshape(MLA_HEADS, HEAD_PAD)[:, :V_HEAD].reshape(MLA_WIDTH)


def _ssm_prep(lam_re, lam_im, log_dt, b_re, b_im, c_re, c_im):
    lam = lax.complex(lam_re, lam_im)
    dt = jnp.exp(log_dt)[:, None]
    a_bar = jnp.exp(lam * dt)
    b_bar = ((a_bar - 1.0) / lam)[:, None, :] * lax.complex(b_re, b_im)
    G8 = SSM_GROUPS // SSM_MACRO
    eye = jnp.eye(G8, dtype=f32)

    def bmat(part):
        p4 = part.reshape(SSM_MACRO, G8, SSM_GROUP, SSM_STATE)
        return jnp.einsum('mgcp,gh->mgchp', p4, eye).reshape(SSM_MACRO, MACRO_CH, MACRO_ST)

    def cmat(part):
        p4 = part.reshape(SSM_MACRO, G8, SSM_GROUP, SSM_STATE)
        return jnp.einsum('mgcp,gh->mgphc', p4, eye).reshape(SSM_MACRO, MACRO_ST, MACRO_CH)

    bm = jnp.concatenate([bmat(b_bar.real), bmat(b_bar.imag)], axis=2)
    cm = jnp.concatenate([cmat(c_re), -cmat(c_im)], axis=1)
    a4 = a_bar.reshape(SSM_MACRO, 1, MACRO_ST)
    a = jnp.concatenate([a4.real, a4.imag], axis=2)
    return bm, cm, a


def _rope_tables(positions):
    freqs = ROPE_THETA ** (-jnp.arange(0, QK_ROPE, 2, dtype=f32) / QK_ROPE)
    ang = positions.astype(f32)[:, None] * freqs
    cos, sin = jnp.cos(ang), jnp.sin(ang)
    S = positions.shape[0]
    half = QK_ROPE // 2
    one, zero = jnp.ones((S, QK_NOPE), f32), jnp.zeros((S, half), f32)
    z64, z32 = jnp.zeros((S, QK_NOPE), f32), jnp.zeros((S, LANES - QK_NOPE - QK_ROPE), f32)
    tc = jnp.concatenate([one, cos, cos, z32], axis=1)
    s1 = jnp.concatenate([z64, -sin, zero, z32], axis=1)
    s2 = jnp.concatenate([z64, zero, sin, z32], axis=1)
    return tc, s1, s2


def _layer_params(W, l):
    p = {}
    p['win'] = _pack_win(W['w_in'][l])
    p['wuq'] = _pad_heads(W['w_uq'][l], QK_NOPE + QK_ROPE, _Q_PIECES)
    wukv = W['w_ukv'][l]
    p['wukv'] = jnp.concatenate([_pad_heads(wukv, QK_NOPE + V_HEAD, _K_PIECES),
                                 _pad_heads(wukv, QK_NOPE + V_HEAD, _V_PIECES)], axis=1)
    p['attn_g'] = _pad_gain(W['attn_out_g'][l])
    return p


def _forward_layer(h, memn_in, tabs, W, l, name, gather=None):
    p = _layer_params(W, l)
    sv = {'h0': h, 'p': p}
    xn = _rms_fwd(h, W['norm_mix_g'][l], name=name + "rms_mix")
    proj = _mm([(xn, p['win'])], 'nn', f32, name=name + "mm_in")
    cqn = _rms_fwd(proj, W['q_norm_g'][l], col0=0, width=Q_LORA, name=name + "rms_q")
    ckvn = _rms_fwd(proj, W['kv_norm_g'][l], col0=Q_LORA, width=KV_LORA, name=name + "rms_kv")
    q = _mm([(cqn, p['wuq'])], 'nn', f32, name=name + "mm_uq")
    kv = _mm([(ckvn, p['wukv'])], 'nn', f32, name=name + "mm_ukv")
    qh, kh, vh = _rope_fwd(q, kv, proj, tabs, name=name + "rope")
    oh, lse, carried = _attn_fwd(qh, kh, vh, plan=gather[0] if gather else None, name=name + "attn")
    if gather:
        for n, t in gather[1](carried).items():
            W[n][l] = t
    p['wout'] = _pack_wout(W['w_out'][l])
    an = _rms_fwd(oh, p['attn_g'], n_valid=MLA_WIDTH, name=name + "rms_attn")
    bm, cm, a = W['ssm'][l]
    bmb, cmb = bm.astype(bf16), cm.astype(bf16)
    y, xc = _ssm_fwd(proj, bmb, cmb, a, W['ssm_d'][l], name=name + "ssm")
    z, sn, ge = _glu_fwd(y, W['ssm_w_glu'][l], W['ssm_b_glu'][l], W['ssm_out_g'][l], name=name + "glu")
    h1a = _mm([(an, p['wout'][:MLA_PAD])], 'nn', f32, res=h, name=name + "mm_out_a")
    h1 = _mm([(sn, p['wout'][MLA_PAD:])], 'nn', f32, res=h1a, name=name + "mm_out_s")
    hn2 = _rms_fwd(h1, W['norm_x_g'][l], name=name + "rms_x")
    memn = _rms_fwd(memn_in, W['mem_norm_g'][l], name=name + "rms_mem")
    qx = _mm([(hn2, W['w_xq'][l])], 'nn', bf16, name=name + "mm_xq")
    kvx = _mm([(memn, W['w_xkv'][l])], 'nn', bf16, name=name + "mm_xkv")
    ox = _xattn_fwd(qx, kvx, name=name + "xattn")
    h2 = _mm([(ox, W['w_xo'][l])], 'nn', f32, res=h1, name=name + "mm_xo")
    hn3 = _rms_fwd(h2, W['norm_ffn_g'][l], name=name + "rms_ffn")
    gate, up, act = _ffn_up(hn3, W['w_gate'][l], W['w_up'][l], name=name + "ffn_up")
    h3 = _mm([(act, W['w_down'][l])], 'nn', f32, res=h2, name=name + "mm_down")
    sv.update(xn=xn, proj=proj, cqn=cqn, ckvn=ckvn, qh=qh, kh=kh, vh=vh, oh=oh, lse=lse, an=an, bmb=bmb, cmb=cmb,
              a=a, y=y, xc=xc, z=z, sn=sn, ge=ge, h1=h1, hn2=hn2, memn=memn, qx=qx, kvx=kvx, ox=ox, h2=h2, hn3=hn3,
              gate=gate, up=up, act=act)
    return h3, sv


def _backward_layer(dh3, sv, memn_in, tabs, W, l, hook, name):
    p = sv['p']
    G = {}
    G['w_down'] = _mm([(sv['act'], dh3)], 'tn', f32, name=name + "dw_down")
    dgate, dup = _ffn_bwd_act(dh3, W['w_down'][l], sv['gate'], sv['up'], name=name + "ffn_bwd_act")
    dhn3 = _mm([(dgate, W['w_gate'][l]), (dup, W['w_up'][l])], 'nn', f32, name=name + "mm_dffn")
    G['w_gate'] = _mm([(dgate, sv['hn3'])], 'tn', f32, name=name + "dw_gate")
    G['w_up'] = _mm([(dup, sv['hn3'])], 'tn', f32, name=name + "dw_up")
    xplan = hook['ffn'](l, G) if hook else None
    dh2, dg = _rms_bwd(sv['h2'], W['norm_ffn_g'][l], dhn3, res=dh3, name=name + "rmsb_ffn")
    G['norm_ffn_g'] = dg[0]
    G['w_xo'] = _mm([(sv['ox'], dh2)], 'tn', f32, name=name + "dw_xo")
    dox = _mm([(dh2, W['w_xo'][l])], 'nt', bf16, name=name + "mm_dxo")
    dqx, dkvx, carried = _xattn_bwd(sv['qx'], sv['kvx'], dox, plan=xplan, name=name + "xattn_bwd")
    if xplan:
        xplan['done'](carried)
    G['w_xq'] = _mm([(sv['hn2'], dqx)], 'tn', f32, name=name + "dw_xq")
    G['w_xkv'] = _mm([(sv['memn'], dkvx)], 'tn', f32, name=name + "dw_xkv")
    dhn2 = _mm([(dqx, W['w_xq'][l])], 'nt', f32, name=name + "mm_dxq")
    dmemn = _mm([(dkvx, W['w_xkv'][l])], 'nt', f32, name=name + "mm_dxkv")
    dh1, dg = _rms_bwd(sv['h1'], W['norm_x_g'][l], dhn2, res=dh2, name=name + "rmsb_x")
    G['norm_x_g'] = dg[0]
    _, dg = _rms_bwd(memn_in, W['mem_norm_g'][l], dmemn, name=name + "rmsb_mem")
    G['mem_norm_g'] = dg[0]
    dwo_a = _mm([(sv['an'], dh1)], 'tn', f32, name=name + "dw_out_a")
    dwo_s = _mm([(sv['sn'], dh1)], 'tn', f32, name=name + "dw_out_s")
    G['w_out'] = _unpack_wout(jnp.concatenate([dwo_a, dwo_s], axis=0))
    dmixed = _mm([(dh1, p['wout'])], 'nt', f32, name=name + "mm_dout")
    dy, dz, dg, db = _glu_bwd(sv['y'], sv['z'], dmixed, W['ssm_w_glu'][l], W['ssm_out_g'][l], name=name + "glu_bwd")
    G['ssm_out_g'], G['ssm_b_glu'] = dg[0], db[0]
    G['ssm_w_glu'] = _mm([(sv['ge'], dz)], 'tn', f32, name=name + "dw_glu")
    xplan = hook['mid'](l, G) if hook else None
    du, dbm, dcm, da, dd, carried = _ssm_bwd(sv['proj'], dy, sv['xc'], sv['bmb'], sv['cmb'], sv['a'], W['ssm_d'][l],
                                             plan=xplan, name=name + "ssm_bwd")
    if xplan:
        xplan['done'](carried)
    G['ssm_d'] = dd[0]
    G['ssm_raw'] = (dbm, dcm, da)
    doh, dg, delta = _rms_bwd(sv['oh'], p['attn_g'], dmixed, width=MLA_PAD, n_valid=MLA_WIDTH, delta=True,
                              out_dtype=bf16, name=name + "rmsb_attn")
    G['attn_out_g'] = _unpad_gain(dg[0])
    plans = hook['take']() if hook else []
    plan = _merge_plans(plans) if plans else None
    dqh, dkh, dvh, carried = _attn_bwd(sv['qh'], sv['kh'], sv['vh'], doh, sv['lse'], delta, plan=plan,
                                       name=name + "attn_bwd")
    if plan:
        plan['done'](carried)
    dq, dkv, dkr = _rope_bwd(dqh, dkh, dvh, tabs, name=name + "rope_bwd")
    G['w_uq'] = _unpad_heads(_mm([(sv['cqn'], dq)], 'tn', f32, name=name + "dw_uq"), QK_NOPE + QK_ROPE, _Q_PIECES)
    dwukv = _mm([(sv['ckvn'], dkv)], 'tn', f32, name=name + "dw_ukv")
    G['w_ukv'] = (_unpad_heads(dwukv[:, :MLA_PAD], QK_NOPE + V_HEAD, _K_PIECES)
                  + _unpad_heads(dwukv[:, MLA_PAD:], QK_NOPE + V_HEAD, _V_PIECES))
    dcqn = _mm([(dq, p['wuq'])], 'nt', f32, name=name + "mm_duq")
    dckvn = _mm([(dkv, p['wukv'])], 'nt', f32, name=name + "mm_dukv")
    dcq, dg = _rms_bwd(sv['proj'], W['q_norm_g'][l], dcqn, col0=0, width=Q_LORA, out_dtype=bf16, name=name + "rmsb_q")
    G['q_norm_g'] = dg[0]
    dckv, dg = _rms_bwd(sv['proj'], W['kv_norm_g'][l], dckvn, col0=Q_LORA, width=KV_LORA, out_dtype=bf16,
                        name=name + "rmsb_kv")
    G['kv_norm_g'] = dg[0]
    dproj = jnp.concatenate([dcq, dckv, dkr, du.astype(bf16)], axis=1)
    G['w_in'] = _unpack_win(_mm([(sv['xn'], dproj)], 'tn', f32, name=name + "dw_in"))
    dxn = _mm([(dproj, p['win'])], 'nt', f32, name=name + "mm_din")
    dh0, dg = _rms_bwd(sv['h0'], W['norm_mix_g'][l], dxn, res=dh1, name=name + "rmsb_mix")
    G['norm_mix_g'] = dg[0]
    return dh0, G


def _local_step(x, mem, positions, target, W, gathers=None, hook=None):
    tabs = _rope_tables(positions)
    ssm_in = [(W['ssm_lambda_re'][l], W['ssm_lambda_im'][l], W['ssm_log_dt'][l], W['ssm_b_re'][l], W['ssm_b_im'][l],
               W['ssm_c_re'][l], W['ssm_c_im'][l]) for l in range(DEPTH)]
    preps = [jax.vjp(_ssm_prep, *ssm_in[l]) for l in range(DEPTH)]
    W = dict(W)
    W['ssm'] = [preps[l][0] for l in range(DEPTH)]
    h = x
    saved = []
    for l in range(DEPTH):
        h, sv = _forward_layer(h, mem, tabs, W, l, f"l{l}_", gathers[l] if gathers else None)
        saved.append(sv)
    dh, dgf, loss = _loss_head(h, W['final_norm_g'], target)
    grads = [None] * DEPTH
    for l in reversed(range(DEPTH)):
        dh, G = _backward_layer(dh, saved[l], mem, tabs, W, l, hook, f"l{l}b_")
        dbm, dcm, da = G.pop('ssm_raw')
        names = ['ssm_lambda_re', 'ssm_lambda_im', 'ssm_log_dt', 'ssm_b_re', 'ssm_b_im', 'ssm_c_re', 'ssm_c_im']
        for n, g in zip(names, preps[l][1]((dbm, dcm, da))):
            G[n] = g
        grads[l] = G
        if hook is not None and l > 0:
            hook['rest'](l, G)
    out = {n: [grads[l][n] for l in range(DEPTH)] if n in SHARDED else jnp.stack([grads[l][n] for l in range(DEPTH)])
           for n in grads[0]}
    out['final_norm_g'] = dgf[0]
    return loss[0, 0], dh, out


_HBM = pl.BlockSpec(memory_space=pltpu.HBM)


def _me():
    return lax.axis_index("x"), lax.axis_index("y"), lax.axis_index("c")


def _chip_peers(x, y, c):
    devs = [(1 - x, y, c), (x, 1 - y, c), (1 - x, 1 - y, c)]
    return devs, [2 * d[0] + d[1] for d in devs]


def _gather_plan(xs, half_first):
    n = len(xs)
    if half_first:
        ins = [t.reshape(2, 1, *t.shape[1:]) for t in xs]
        outs = [jax.ShapeDtypeStruct((2, 4, *t.shape[1:]), t.dtype) for t in xs]
    else:
        ins = [t.reshape(1, 2, t.shape[0] // 2, t.shape[1]) for t in xs]
        outs = [jax.ShapeDtypeStruct((4, 2, t.shape[0] // 2, t.shape[1]), t.dtype) for t in xs]

    def own(ref, h):
        return ref.at[h] if half_first else ref.at[:, h]

    def slot(ref, h, j):
        return ref.at[h, pl.ds(j, 1)] if half_first else ref.at[pl.ds(j, 1), h]

    def copies(src, dst, send, recv):
        x, y, c = _me()
        jme = 2 * x + y
        devs, js = _chip_peers(x, y, c)
        half, other = pl.ds(c, 1), pl.ds(1 - c, 1)
        mk = pltpu.make_async_remote_copy
        for i in range(n):
            for k in range(3):
                out_cp = mk(own(src[i], half), slot(dst[i], half, jme), send.at[6 * i + k], recv.at[6 * i + k],
                            device_id=devs[k], device_id_type=MESH)
                in_cp = mk(own(src[i], half), slot(dst[i], half, js[k]), send.at[6 * i + k], recv.at[6 * i + k],
                           device_id=devs[k], device_id_type=MESH)
                pass_cp = mk(slot(dst[i], half, js[k]), slot(dst[i], half, js[k]), send.at[6 * i + 3 + k],
                             recv.at[6 * i + 3 + k], device_id=(x, y, 1 - c), device_id_type=MESH)
                got_cp = mk(slot(dst[i], other, js[k]), slot(dst[i], other, js[k]), send.at[6 * i + 3 + k],
                            recv.at[6 * i + 3 + k], device_id=(x, y, 1 - c), device_id_type=MESH)
                yield out_cp, in_cp, pass_cp, got_cp

    def start(*refs):
        for out_cp, _, _, _ in copies(*refs):
            out_cp.start()

    def forward(*refs):
        for _, in_cp, pass_cp, _ in copies(*refs):
            in_cp.wait_recv()
            pass_cp.start()

    def finish(*refs):
        for out_cp, _, pass_cp, got_cp in copies(*refs):
            got_cp.wait_recv()
            out_cp.wait_send()
            pass_cp.wait_send()

    return dict(n=n, ins=ins, outs=outs, nsem=6 * n, start=start, forward=forward, finish=finish)


def _plan_refs(plan, refs):
    n = plan['n']
    return refs[:n], refs[n:2 * n], refs[2 * n], refs[2 * n + 1]


def _run_plan(plan, name):
    n = plan['n']

    def body(*refs):
        args = _plan_refs(plan, refs)
        plan['start'](*args)
        plan['forward'](*args)
        plan['finish'](*args)

    return _pc(body, in_specs=[_HBM] * n, out_specs=[_HBM] * n, out_shape=plan['outs'],
               scratch_shapes=[pltpu.SemaphoreType.DMA((plan['nsem'],)), pltpu.SemaphoreType.DMA((plan['nsem'],))],
               compiler_params=pltpu.CompilerParams(has_side_effects=True), name=name)(*plan['ins'])


def _fill_own(gathered, own, half_first):
    jme = (2 * lax.axis_index("x") + lax.axis_index("y")).astype(jnp.int32)
    zero = jnp.int32(0)
    if half_first:
        return lax.dynamic_update_slice(gathered, own[:, None], (zero, jme, zero, zero))
    return lax.dynamic_update_slice(gathered, own.reshape(1, *gathered.shape[1:]), (jme, zero, zero, zero))


def _exchange_plan(gs, done=None):
    n = len(gs)

    def copies(src, dst, send, recv):
        x, y, c = _me()
        for i in range(n):
            yield pltpu.make_async_remote_copy(src[i].at[:, pl.ds(1 - c, 1)], dst[i], send.at[i], recv.at[i],
                                               device_id=(x, y, 1 - c), device_id_type=MESH)

    def start(*refs):
        for cp in copies(*refs):
            cp.start()

    def finish(*refs):
        for cp in copies(*refs):
            cp.wait()

    outs = [jax.ShapeDtypeStruct((4, 1, *g.shape[2:]), g.dtype) for g in gs]
    return dict(n=n, ins=list(gs), outs=outs, nsem=n, start=start, forward=lambda *refs: None, finish=finish, done=done)


def _plan_extras(plan):
    if not plan:
        return [], [], [], [], []
    anyspec = pl.BlockSpec(memory_space=pl.ANY)
    sems = [pltpu.SemaphoreType.DMA((plan['nsem'],)), pltpu.SemaphoreType.DMA((plan['nsem'],))]
    return [anyspec] * plan['n'], [anyspec] * plan['n'], list(plan['outs']), sems, list(plan['ins'])


def _scatter_plan(ps, done=None):
    n = len(ps)

    def copies(src, dst, send, recv, off):
        x, y, c = _me()
        devs, js = _chip_peers(x, y, c)
        for i in range(n):
            for k in range(3):
                yield pltpu.make_async_remote_copy(src[i].at[pl.ds(js[k], 1)], dst[i].at[k], send.at[off + 3 * i + k],
                                                   recv.at[off + 3 * i + k], device_id=devs[k], device_id_type=MESH)

    def start(src, dst, send, recv, off=0):
        for cp in copies(src, dst, send, recv, off):
            cp.start()

    def finish(src, dst, send, recv, off=0):
        for cp in copies(src, dst, send, recv, off):
            cp.wait()

    outs = [jax.ShapeDtypeStruct((3, 1, *p.shape[1:]), p.dtype) for p in ps]
    return dict(n=n, ins=list(ps), outs=outs, nsem=3 * n, start=start, forward=lambda *refs: None, finish=finish,
                done=done)


def _merge_plans(plans):
    def run(which):
        def f(src, dst, send, recv):
            o = s = 0
            for p in plans:
                p[which](src[o:o + p['n']], dst[o:o + p['n']], send, recv, off=s)
                o, s = o + p['n'], s + p['nsem']
        return f

    def done(results):
        o = 0
        for p in plans:
            p['done'](results[o:o + p['n']])
            o += p['n']

    return dict(n=sum(p['n'] for p in plans), ins=[t for p in plans for t in p['ins']],
                outs=[t for p in plans for t in p['outs']], nsem=sum(p['nsem'] for p in plans),
                start=run('start'), forward=lambda *refs: None, finish=run('finish'), done=done)


def _swap_sibling(hs, name):
    n = len(hs)

    def body(*refs):
        src, dst = refs[:n], refs[n:2 * n]
        send, recv = refs[2 * n:]
        x, y, c = _me()
        cps = []
        for i in range(n):
            cp = pltpu.make_async_remote_copy(src[i], dst[i], send.at[i], recv.at[i], device_id=(x, y, 1 - c),
                                              device_id_type=MESH)
            cp.start()
            cps.append(cp)
        for cp in cps:
            cp.wait()

    outs = [jax.ShapeDtypeStruct(h.shape, h.dtype) for h in hs]
    return _pc(body, in_specs=[_HBM] * n, out_specs=[_HBM] * n, out_shape=outs,
               scratch_shapes=[pltpu.SemaphoreType.DMA((n,)), pltpu.SemaphoreType.DMA((n,))],
               compiler_params=pltpu.CompilerParams(has_side_effects=True), name=name)(*hs)


ELEMWISE_VMEM_BUDGET = 24 * 1024 * 1024


def _row_tile(r, n, narrays):
    limit = ELEMWISE_VMEM_BUDGET // (2 * 4 * narrays * n)
    best = SUBLANES
    for t in range(16, r + 1, 16):
        if r % t == 0 and t <= limit:
            best = t
    return best


def _add_half(g, r1, cidx, name):
    _, _, r, n = g.shape
    tr = _row_tile(r, n, 3)

    def body(c_ref, g_ref, r_ref, o_ref):
        o_ref[...] = (g_ref[...] + r_ref[...]).astype(GRAD_TRANSIT)

    blk = lambda f: pl.BlockSpec((None, None, tr, n), f)
    gs = pltpu.PrefetchScalarGridSpec(
        num_scalar_prefetch=1, grid=(4, r // tr),
        in_specs=[blk(lambda j, i, c: (j, c[0], i, 0)), blk(lambda j, i, c: (j, 0, i, 0))],
        out_specs=pl.BlockSpec((None, tr, n), lambda j, i, c: (j, i, 0)))
    return _pc(body, grid_spec=gs, out_shape=jax.ShapeDtypeStruct((4, r, n), GRAD_TRANSIT),
               compiler_params=_cp(("parallel", "parallel")), name=name)(cidx, g, r1)


def _add_chips(p, r3, jidx, name):
    _, r, n = p.shape
    tr = _row_tile(r, n, 5)

    def body(j_ref, p_ref, a_ref, b_ref, c_ref, o_ref):
        o_ref[...] = ((p_ref[...].astype(f32) + a_ref[...].astype(f32)) + b_ref[...].astype(f32)) + c_ref[...].astype(f32)

    rblk = lambda k: pl.BlockSpec((None, None, tr, n), lambda i, j: (k, 0, i, 0))
    gs = pltpu.PrefetchScalarGridSpec(
        num_scalar_prefetch=1, grid=(r // tr,),
        in_specs=[pl.BlockSpec((None, tr, n), lambda i, j: (j[0], i, 0)), rblk(0), rblk(1), rblk(2)],
        out_specs=pl.BlockSpec((tr, n), lambda i, j: (i, 0)))
    return _pc(body, grid_spec=gs, out_shape=jax.ShapeDtypeStruct((r, n), f32),
               compiler_params=_cp(("parallel",)), name=name)(jidx, p, r3, r3, r3)


def _adamw_halves(w, mine, theirs, m, v, cidx, name):
    L, r, n = w.shape
    r2 = r // 2
    tr = _row_tile(r2, n, 11)
    c1 = 1.0 / (1.0 - ADAM_B1 ** ADAM_STEP)
    c2 = 1.0 / (1.0 - ADAM_B2 ** ADAM_STEP)

    def body(c_ref, w_ref, a0_ref, b0_ref, a1_ref, b1_ref, m_ref, v_ref, g_ref, d_ref, mo_ref, vo_ref):
        l, hf = pl.program_id(0), pl.program_id(1)
        own = hf == c_ref[0]
        gv = jnp.where(l == 0, jnp.where(own, a0_ref[...], b0_ref[...]), jnp.where(own, a1_ref[...], b1_ref[...]))
        m2 = ADAM_B1 * m_ref[...] + (1.0 - ADAM_B1) * gv
        v2 = ADAM_B2 * v_ref[...] + (1.0 - ADAM_B2) * (gv * gv)
        g_ref[...] = gv
        d_ref[...] = -ADAM_LR * ((m2 * c1) / (jnp.sqrt(v2 * c2) + ADAM_EPS) + ADAM_WD * w_ref[...])
        mo_ref[...] = m2
        vo_ref[...] = v2

    full = pl.BlockSpec((None, None, tr, n), lambda l, hf, i, c: (l, hf, i, 0))

    def half(layer, own):
        return pl.BlockSpec((tr, n), lambda l, hf, i, c: (jnp.where((l == layer) & ((hf == c[0]) == own), i, 0), 0))

    gs = pltpu.PrefetchScalarGridSpec(
        num_scalar_prefetch=1, grid=(L, 2, r2 // tr),
        in_specs=[full, half(0, True), half(0, False), half(1, True), half(1, False), full, full], out_specs=[full] * 4)
    four = lambda t: t.reshape(L, 2, r2, n)
    outs = _pc(body, grid_spec=gs, out_shape=[jax.ShapeDtypeStruct((L, 2, r2, n), f32)] * 4,
               compiler_params=_cp(("parallel", "parallel", "parallel")), name=name)(
        cidx, four(w), mine[0], theirs[0], mine[1], theirs[1], four(m), four(v))
    return [t.reshape(w.shape) for t in outs]


def _adamw_whole(w, g, m, v, name):
    c1 = 1.0 / (1.0 - ADAM_B1 ** ADAM_STEP)
    c2 = 1.0 / (1.0 - ADAM_B2 ** ADAM_STEP)

    def body(w_ref, g_ref, m_ref, v_ref, d_ref, mo_ref, vo_ref):
        gv = g_ref[...]
        m2 = ADAM_B1 * m_ref[...] + (1.0 - ADAM_B1) * gv
        v2 = ADAM_B2 * v_ref[...] + (1.0 - ADAM_B2) * (gv * gv)
        d_ref[...] = -ADAM_LR * ((m2 * c1) / (jnp.sqrt(v2 * c2) + ADAM_EPS) + ADAM_WD * w_ref[...])
        mo_ref[...] = m2
        vo_ref[...] = v2

    return _pc(body, out_shape=[jax.ShapeDtypeStruct(w.shape, f32)] * 3, name=name)(w, g, m, v)


def _full_from_gathered(name, t):
    r, n = 2 * t.shape[2], t.shape[3]
    if SHARDED[name] == 1 or name in TRANSPOSED:
        return t.reshape(4 * r, n)
    return t.reshape(4, r, n).transpose(1, 0, 2).reshape(r, 4 * n)


def _shard_major(name, g):
    R, C = g.shape
    if SHARDED[name] == 1 or name in TRANSPOSED:
        return g.reshape(4, 2, R // 8, C)
    return g.reshape(R, 4, C // 4).transpose(1, 0, 2).reshape(4, 2, R // 2, C // 4)


_SMALL_ROWS = 288


def _pack_small(d):
    flat = jnp.concatenate([d[n].reshape(-1) for n in SMALL])
    total = 2 * 4 * _SMALL_ROWS * LANES
    flat = jnp.concatenate([flat, jnp.zeros((total - flat.shape[0],), f32)])
    return flat.reshape(4, 2, _SMALL_ROWS, LANES)


def _unpack_small(t, like):
    flat = t.reshape(-1)
    out, off = {}, 0
    for n in SMALL:
        sz = math.prod(like[n].shape)
        out[n] = flat[off:off + sz].reshape(like[n].shape)
        off += sz
    return out


def kernel(x, mem, positions, norm_mix_g, w_in, q_norm_g, w_uq, kv_norm_g, w_ukv, ssm_lambda_re, ssm_lambda_im, ssm_log_dt, ssm_b_re, ssm_b_im, ssm_c_re, ssm_c_im, ssm_d, ssm_w_glu, ssm_b_glu, attn_out_g, ssm_out_g, w_out, norm_x_g, mem_norm_g, w_xq, w_xkv, w_xo, norm_ffn_g, w_gate, w_up, w_down, final_norm_g, loss_target, m_norm_mix_g, m_w_in, m_q_norm_g, m_w_uq, m_kv_norm_g, m_w_ukv, m_ssm_lambda_re, m_ssm_lambda_im, m_ssm_log_dt, m_ssm_b_re, m_ssm_b_im, m_ssm_c_re, m_ssm_c_im, m_ssm_d, m_ssm_w_glu, m_ssm_b_glu, m_attn_out_g, m_ssm_out_g, m_w_out, m_norm_x_g, m_mem_norm_g, m_w_xq, m_w_xkv, m_w_xo, m_norm_ffn_g, m_w_gate, m_w_up, m_w_down, m_final_norm_g, v_norm_mix_g, v_w_in, v_q_norm_g, v_w_uq, v_kv_norm_g, v_w_ukv, v_ssm_lambda_re, v_ssm_lambda_im, v_ssm_log_dt, v_ssm_b_re, v_ssm_b_im, v_ssm_c_re, v_ssm_c_im, v_ssm_d, v_ssm_w_glu, v_ssm_b_glu, v_attn_out_g, v_ssm_out_g, v_w_out, v_norm_x_g, v_mem_norm_g, v_w_xq, v_w_xkv, v_w_xo, v_norm_ffn_g, v_w_gate, v_w_up, v_w_down, v_final_norm_g):
    given = dict(locals())
    swap = lambda n, t: jnp.swapaxes(t, *TRANSPOSED[n]) if n in TRANSPOSED else t
    w = {n: swap(n, given[n]) for n in WEIGHTS}
    m = {n: swap(n, given["m_" + n]) for n in WEIGHTS}
    v = {n: swap(n, given["v_" + n]) for n in WEIGHTS}
    big = list(SHARDED)

    shards = {n: w[n].astype(bf16) for n in big}
    early = [n for n in big if n in EARLY_WEIGHTS]
    rest = [n for n in big if n not in EARLY_WEIGHTS]

    def full(names, results, l):
        return {n: _full_from_gathered(n, _fill_own(t, shards[n][l], False)) for n, t in zip(names, results)}

    first = _run_plan(_gather_plan([shards[n][l] for l in range(DEPTH) for n in early], False), "allgather_weights_early")
    W = {n: [None] * DEPTH for n in big}
    for l in range(DEPTH):
        for n, t in full(early, first[l * len(early):(l + 1) * len(early)], l).items():
            W[n][l] = t
    gathers = [(_gather_plan([shards[n][l] for n in rest], False), functools.partial(full, rest, l=l))
               for l in range(DEPTH)]
    W.update({n: w[n] for n in SMALL})

    cidx = lax.axis_index("c").astype(jnp.int32).reshape(1)
    jidx = (2 * lax.axis_index("x") + lax.axis_index("y")).astype(jnp.int32).reshape(1)
    ffn = [n for n in big if n in FFN_WEIGHTS]
    tail = [n for n in big if n in EARLY_WEIGHTS]
    mid = [n for n in big if n not in FFN_WEIGHTS and n not in EARLY_WEIGHTS]
    sums, got = {}, {}
    ready = []

    def exchange(names, tag, l, G, extra=()):
        keys = [(l, n) for n in names] + [(l, n) for n, _ in extra]
        gs = [_shard_major(n, G[n]) for n in names] + [g for _, g in extra]

        def done(r1):
            ps = [_add_half(g, r, cidx, f"grad_add_half_{tag}_{k[1]}") for k, g, r in zip(keys, gs, r1)]
            sums.update(zip(keys, ps))
            ready.append(_scatter_plan(ps, done=lambda results: got.update(zip(keys, results))))

        return _exchange_plan(gs, done)

    def take():
        plans = list(ready)
        ready.clear()
        return plans

    def now(plan, name):
        plan['done'](_run_plan(plan, name))

    hook = {'ffn': lambda l, G: exchange(ffn, f"l{l}_ffn", l, G), 'mid': lambda l, G: exchange(mid, f"l{l}_mid", l, G),
            'rest': lambda l, G: now(exchange(tail, f"l{l}_tail", l, G), f"grad_exchange_halves_l{l}_tail"),
            'take': take}
    loss, dx, grads = _local_step(x[0], mem[0], positions[0], loss_target[0], W, gathers=gathers, hook=hook)
    loss = lax.psum(loss, ("x", "y", "c"))

    now(exchange(tail, "l0_tail", 0, {n: grads[n][0] for n in tail}, extra=[("small", _pack_small(grads))]),
        "grad_exchange_halves_l0_tail")
    now(_merge_plans(take()), "grad_scatter_chips_l0_tail")
    keys = list(sums)
    hs = dict(zip(keys, [_add_chips(sums[k], got[k], jidx, f"grad_add_chips_l{k[0]}_{k[1]}") for k in keys]))
    ts = dict(zip(keys, _swap_sibling([hs[k] for k in keys], "grad_swap_sibling")))

    out_g, out_d, out_m, out_v = {}, {}, {}, {}
    for n in big:
        mine, theirs = [hs[(l, n)] for l in range(DEPTH)], [ts[(l, n)] for l in range(DEPTH)]
        out_g[n], out_d[n], out_m[n], out_v[n] = _adamw_halves(w[n], mine, theirs, m[n], v[n], cidx, f"adamw_{n}")
    both = jnp.stack([hs[(0, "small")], ts[(0, "small")]])
    piece = jnp.where(cidx[0] == 0, both, both[::-1]).reshape(2 * _SMALL_ROWS, LANES)
    gsm = _fill_own(_run_plan(_gather_plan([piece], False), "allgather_small")[0], piece, False)
    out_g.update(_unpack_small(gsm, w))
    for n in SMALL:
        two = lambda t: t.reshape(1, -1) if t.ndim == 1 else t
        d_, m_, v_ = _adamw_whole(two(w[n]), two(out_g[n]), two(m[n]), two(v[n]), f"adamw_{n}")
        out_d[n], out_m[n], out_v[n] = (t.reshape(w[n].shape) for t in (d_, m_, v_))

    outs = [[swap(n, d[n]) for n in WEIGHTS] for d in (out_g, out_d, out_m, out_v)]
    return (loss, dx.reshape(x.shape), *outs[0], *outs[1], *outs[2], *outs[3])
```

```python
import functools
import math

import jax
import jax.numpy as jnp
from jax import lax
from jax.experimental import pallas as pl
from jax.experimental.pallas import tpu as pltpu

f32, bf16 = jnp.float32, jnp.bfloat16

D_MODEL = 1024
DEPTH = 2
MLA_HEADS = 8
QK_NOPE = 64
QK_ROPE = 32
V_HEAD = 64
Q_LORA = 256
KV_LORA = 128
MLA_WIDTH = MLA_HEADS * V_HEAD
ROPE_THETA = 10000.0
SSM_WIDTH = 512
SSM_GROUP = 16
SSM_GROUPS = 32
SSM_STATE = 64
IN_WIDTH = Q_LORA + KV_LORA + QK_ROPE + SSM_WIDTH
X_HEADS = 4
X_HEAD_DIM = D_MODEL // X_HEADS
D_FF = 2816
EPS = 1e-6
ADAM_LR, ADAM_B1, ADAM_B2, ADAM_EPS, ADAM_WD, ADAM_STEP = 0.001, 0.9, 0.999, 1e-08, 0.01, 10

LANES = 128
SUBLANES = 8
HEAD_PAD = 128
MLA_PAD = MLA_HEADS * HEAD_PAD
SSM_MACRO = 4
MACRO_CH = SSM_WIDTH // SSM_MACRO
MACRO_ST = SSM_GROUPS // SSM_MACRO * SSM_STATE
VMEM_LIMIT = 56 * 1024 * 1024
GRAD_TRANSIT = bf16

WEIGHTS = ['norm_mix_g', 'w_in', 'q_norm_g', 'w_uq', 'kv_norm_g', 'w_ukv', 'ssm_lambda_re', 'ssm_lambda_im',
           'ssm_log_dt', 'ssm_b_re', 'ssm_b_im', 'ssm_c_re', 'ssm_c_im', 'ssm_d', 'ssm_w_glu', 'ssm_b_glu',
           'attn_out_g', 'ssm_out_g', 'w_out', 'norm_x_g', 'mem_norm_g', 'w_xq', 'w_xkv', 'w_xo', 'norm_ffn_g',
           'w_gate', 'w_up', 'w_down', 'final_norm_g']
SHARDED = {'w_in': 1, 'w_uq': 2, 'w_ukv': 2, 'ssm_w_glu': 1, 'w_out': 1, 'w_xq': 1, 'w_xkv': 2, 'w_xo': 1,
           'w_gate': 2, 'w_up': 2, 'w_down': 1}
SMALL = [n for n in WEIGHTS if n not in SHARDED]
EARLY_WEIGHTS = ('w_in', 'w_uq', 'w_ukv')
FFN_WEIGHTS = ('w_gate', 'w_up', 'w_down')
TRANSPOSED = {'w_gate': (1, 2), 'w_up': (1, 2), 'ssm_b_re': (2, 3), 'ssm_b_im': (2, 3)}
MESH = pl.DeviceIdType.MESH


def _pc(body, **kw):
    return pl.pallas_call(body, **kw)


def _pick(n, prefs):
    for p in prefs:
        if n % p == 0:
            return p
    return n


def _cp(sem=None):
    return pltpu.CompilerParams(dimension_semantics=sem, vmem_limit_bytes=VMEM_LIMIT)


_TILE_CANDS = (1024, 1408, 512, 256, 128)
MM_VMEM_BUDGET = 40 * 1024 * 1024
MM_MIN_STEPS = 8


def _mm_tiles(M, K, N, a_bytes, b_bytes, o_bytes, npair, has_res, need_acc):
    best = None
    for tm in _TILE_CANDS:
        for tk in _TILE_CANDS:
            if M % tm or K % tk:
                continue
            vm = npair * (2 * tm * tk * a_bytes + 2 * tk * N * b_bytes) + 2 * tm * N * o_bytes
            vm += tm * N * 4 * (1 + need_acc + 2 * has_res)
            if a_bytes == 4:
                vm += npair * tm * tk * 2
            if b_bytes == 4:
                vm += npair * tk * N * 2
            score = (min((M // tm) * (K // tk), MM_MIN_STEPS), tm * tk, tk)
            if vm <= MM_VMEM_BUDGET and (best is None or score > best[0]):
                best = (score, tm, tk)
    if best is None:
        return _pick(M, (256, 128)), _pick(K, (256, 128))
    return best[1], best[2]


def _mm(pairs, mode, out_dtype, res=None, name="mm"):
    a0, b0 = pairs[0]
    if mode == 'nn':
        (M, K), N = a0.shape, b0.shape[1]
        dims = (((1,), (0,)), ((), ()))
    elif mode == 'nt':
        (M, K), N = a0.shape, b0.shape[0]
        dims = (((1,), (1,)), ((), ()))
    else:
        (K, M), N = a0.shape, b0.shape[1]
        dims = (((0,), (0,)), ((), ()))
    npair = len(pairs)
    has_res = res is not None
    direct = out_dtype == f32
    tm, tk = _mm_tiles(M, K, N, a0.dtype.itemsize, b0.dtype.itemsize, jnp.dtype(out_dtype).itemsize, npair, has_res,
                       not direct)
    nk = K // tk

    def body(*refs):
        ins = refs[:2 * npair]
        res_ref = refs[2 * npair] if has_res else None
        o_ref = refs[2 * npair + has_res]
        acc = o_ref if direct else refs[2 * npair + has_res + 1]
        k = pl.program_id(1)
        s = None
        for p in range(npair):
            d = lax.dot_general(ins[2 * p][...].astype(bf16), ins[2 * p + 1][...].astype(bf16), dims,
                                preferred_element_type=f32)
            s = d if s is None else s + d

        @pl.when(k == 0)
        def _():
            acc[...] = s

        @pl.when(k > 0)
        def _():
            acc[...] += s

        if has_res or not direct:
            @pl.when(k == nk - 1)
            def _():
                r = acc[...]
                if has_res:
                    r = r + res_ref[...]
                o_ref[...] = r.astype(out_dtype)

    if mode == 'nn':
        a_spec = pl.BlockSpec((tm, tk), lambda i, k: (i, k))
        b_spec = pl.BlockSpec((tk, N), lambda i, k: (k, 0))
    elif mode == 'nt':
        a_spec = pl.BlockSpec((tm, tk), lambda i, k: (i, k))
        b_spec = pl.BlockSpec((N, tk), lambda i, k: (0, k))
    else:
        a_spec = pl.BlockSpec((tk, tm), lambda i, k: (k, i))
        b_spec = pl.BlockSpec((tk, N), lambda i, k: (k, 0))
    o_spec = pl.BlockSpec((tm, N), lambda i, k: (i, 0))
    in_specs = [a_spec, b_spec] * npair + ([o_spec] if has_res else [])
    args = [t for p in pairs for t in p] + ([res] if has_res else [])
    return _pc(body, grid=(M // tm, nk), in_specs=in_specs, out_specs=o_spec,
               out_shape=jax.ShapeDtypeStruct((M, N), out_dtype),
               scratch_shapes=[] if direct else [pltpu.VMEM((tm, N), f32)],
               compiler_params=_cp(("parallel", "arbitrary")), name=name)(*args)


def _rms_fwd(x, g, *, col0=0, width=None, n_valid=None, out_dtype=bf16, name="rms_fwd"):
    S = x.shape[0]
    width = width or x.shape[1]
    n_valid = n_valid or width
    ts = _pick(S, (512, 256, 128))
    cb = col0 // width

    def body(x_ref, g_ref, o_ref):
        xv = x_ref[...]
        ms = jnp.sum(xv * xv, axis=-1, keepdims=True) * (1.0 / n_valid)
        o_ref[...] = (xv * lax.rsqrt(ms + EPS) * g_ref[...]).astype(out_dtype)

    return _pc(body, grid=(S // ts,),
               in_specs=[pl.BlockSpec((ts, width), lambda i: (i, cb)), pl.BlockSpec((1, width), lambda i: (0, 0))],
               out_specs=pl.BlockSpec((ts, width), lambda i: (i, 0)),
               out_shape=jax.ShapeDtypeStruct((S, width), out_dtype),
               compiler_params=_cp(("parallel",)), name=name)(x, g.reshape(1, width))


def _rms_bwd(x, g, dy, *, col0=0, dcol0=0, width=None, n_valid=None, res=None, out_dtype=f32, delta=False,
             name="rms_bwd"):
    S = x.shape[0]
    width = width or x.shape[1]
    n_valid = n_valid or width
    ts = _pick(S, (512, 256, 128))
    cb, dcb = col0 // width, dcol0 // width
    has_res = res is not None

    def body(*refs):
        x_ref, g_ref, dy_ref = refs[:3]
        res_ref = refs[3] if has_res else None
        outs = refs[3 + has_res:]
        dx_ref, dg_ref = outs[0], outs[1]
        i = pl.program_id(0)
        xv = x_ref[...]
        gv = g_ref[...]
        dyv = dy_ref[...].astype(f32)
        rstd = lax.rsqrt(jnp.sum(xv * xv, axis=-1, keepdims=True) * (1.0 / n_valid) + EPS)
        xh = xv * rstd
        dxh = dyv * gv
        mean = jnp.sum(dxh * xh, axis=-1, keepdims=True) * (1.0 / n_valid)
        dx = rstd * (dxh - xh * mean)
        if delta:
            d_ref = outs[2]
            for h in range(width // LANES):
                sl = slice(h * LANES, (h + 1) * LANES)
                dsum = jnp.sum(dx[:, sl] * xv[:, sl], axis=-1, keepdims=True)
                d_ref[:, sl] = jnp.broadcast_to(dsum, (ts, LANES))
        if has_res:
            dx = dx + res_ref[...]
        dx_ref[...] = dx.astype(out_dtype)

        @pl.when(i == 0)
        def _():
            dg_ref[...] = jnp.zeros_like(dg_ref)

        dg_ref[...] += jnp.sum(dyv * xh, axis=0, keepdims=True)

    blk = lambda c: pl.BlockSpec((ts, width), lambda i: (i, c))
    in_specs = [blk(cb), pl.BlockSpec((1, width), lambda i: (0, 0)), blk(dcb)] + ([blk(0)] if has_res else [])
    out_specs = [blk(0), pl.BlockSpec((1, width), lambda i: (0, 0))] + ([blk(0)] if delta else [])
    out_shape = [jax.ShapeDtypeStruct((S, width), out_dtype), jax.ShapeDtypeStruct((1, width), f32)] + (
        [jax.ShapeDtypeStruct((S, width), f32)] if delta else [])
    args = [x, g.reshape(1, width), dy] + ([res] if has_res else [])
    return _pc(body, grid=(S // ts,), in_specs=in_specs, out_specs=out_specs, out_shape=out_shape,
               compiler_params=_cp(("arbitrary",)), name=name)(*args)


def _loss_head(h, g, target, name="loss_head"):
    S, D = h.shape
    ts = _pick(S, (512, 256, 128))

    def body(h_ref, g_ref, t_ref, dh_ref, dg_ref, loss_ref):
        i = pl.program_id(0)
        xv = h_ref[...]
        gv = g_ref[...]
        rstd = lax.rsqrt(jnp.sum(xv * xv, axis=-1, keepdims=True) * (1.0 / D) + EPS)
        xh = xv * rstd
        err = xh * gv - t_ref[...]
        dyv = err * (1.0 / D)
        dxh = dyv * gv
        mean = jnp.sum(dxh * xh, axis=-1, keepdims=True) * (1.0 / D)
        dh_ref[...] = rstd * (dxh - xh * mean)

        @pl.when(i == 0)
        def _():
            dg_ref[...] = jnp.zeros_like(dg_ref)
            loss_ref[...] = jnp.zeros_like(loss_ref)

        dg_ref[...] += jnp.sum(dyv * xh, axis=0, keepdims=True)
        part = jnp.sum(jnp.sum(err * err, axis=-1, keepdims=True), axis=0, keepdims=True) * (0.5 / D)
        loss_ref[...] += jnp.broadcast_to(part, (1, LANES))

    blk = pl.BlockSpec((ts, D), lambda i: (i, 0))
    row = pl.BlockSpec((1, D), lambda i: (0, 0))
    return _pc(body, grid=(S // ts,), in_specs=[blk, row, blk],
               out_specs=[blk, row, pl.BlockSpec((1, LANES), lambda i: (0, 0))],
               out_shape=[jax.ShapeDtypeStruct((S, D), f32), jax.ShapeDtypeStruct((1, D), f32),
                          jax.ShapeDtypeStruct((1, LANES), f32)],
               compiler_params=_cp(("arbitrary",)), name=name)(h, g.reshape(1, D), target)


def _rope_apply(x, tc, s1, s2):
    return x * tc + pltpu.roll(x, LANES - 16, 1) * s1 + pltpu.roll(x, 16, 1) * s2


def _rope_apply_t(dy, tc, s1, s2):
    return dy * tc + pltpu.roll(dy * s1, 16, 1) + pltpu.roll(dy * s2, LANES - 16, 1)


def _rope_fwd(q, kv, proj, tabs, name="rope_fwd"):
    S = q.shape[0]
    ts = _pick(S, (512, 256, 128))
    scale = (QK_NOPE + QK_ROPE) ** -0.5

    def body(q_ref, kk_ref, kvv_ref, kr_ref, tc_ref, s1_ref, s2_ref, qh_ref, kh_ref, vh_ref):
        tc, s1, s2 = tc_ref[...], s1_ref[...], s2_ref[...]
        krr = _rope_apply(pltpu.roll(kr_ref[...], QK_NOPE, 1), tc, s1, s2)
        for h in range(MLA_HEADS):
            sl = slice(h * HEAD_PAD, (h + 1) * HEAD_PAD)
            qh_ref[:, sl] = (_rope_apply(q_ref[:, sl], tc, s1, s2) * scale).astype(bf16)
            kh_ref[:, sl] = (kk_ref[:, sl] + krr).astype(bf16)
        vh_ref[...] = kvv_ref[...].astype(bf16)

    wide = lambda c: pl.BlockSpec((ts, MLA_PAD), lambda i: (i, c))
    tab = pl.BlockSpec((ts, LANES), lambda i: (i, 0))
    return _pc(body, grid=(S // ts,),
               in_specs=[wide(0), wide(0), wide(1), pl.BlockSpec((ts, LANES), lambda i: (i, 3)), tab, tab, tab],
               out_specs=[wide(0)] * 3, out_shape=[jax.ShapeDtypeStruct((S, MLA_PAD), bf16)] * 3,
               compiler_params=_cp(("parallel",)), name=name)(q, kv, kv, proj, *tabs)


def _rope_bwd(dqh, dkh, dvh, tabs, name="rope_bwd"):
    S = dqh.shape[0]
    ts = _pick(S, (512, 256, 128))
    scale = (QK_NOPE + QK_ROPE) ** -0.5

    def body(dq_ref, dk_ref, dv_ref, tc_ref, s1_ref, s2_ref, oq_ref, okv_ref, okr_ref):
        tc, s1, s2 = tc_ref[...], s1_ref[...], s2_ref[...]
        ksum = None
        for h in range(MLA_HEADS):
            sl = slice(h * HEAD_PAD, (h + 1) * HEAD_PAD)
            oq_ref[:, sl] = (_rope_apply_t(dq_ref[:, sl], tc, s1, s2) * scale).astype(bf16)
            dk = dk_ref[:, sl]
            okv_ref[:, sl] = dk.astype(bf16)
            ksum = dk if ksum is None else ksum + dk
        okv_ref[:, MLA_PAD:] = dv_ref[...].astype(bf16)
        dkr = pltpu.roll(_rope_apply_t(ksum, tc, s1, s2), LANES - QK_NOPE, 1)
        lane = lax.broadcasted_iota(jnp.int32, (ts, LANES), 1)
        okr_ref[...] = jnp.where(lane < QK_ROPE, dkr, 0.0).astype(bf16)

    wide = pl.BlockSpec((ts, MLA_PAD), lambda i: (i, 0))
    tab = pl.BlockSpec((ts, LANES), lambda i: (i, 0))
    return _pc(body, grid=(S // ts,), in_specs=[wide, wide, wide, tab, tab, tab],
               out_specs=[wide, pl.BlockSpec((ts, 2 * MLA_PAD), lambda i: (i, 0)), tab],
               out_shape=[jax.ShapeDtypeStruct((S, MLA_PAD), bf16), jax.ShapeDtypeStruct((S, 2 * MLA_PAD), bf16),
                          jax.ShapeDtypeStruct((S, LANES), bf16)],
               compiler_params=_cp(("parallel",)), name=name)(dqh, dkh, dvh, *tabs)


ATT_BLK = 1024
_DIAG_QUARTERS = ((0, 0), (1, 0), (1, 1))


def _attn_fwd(qh, kh, vh, plan=None, name="attn_fwd"):
    S = qh.shape[0]
    tq = tk = min(S, ATT_BLK)
    nq, nk = S // tq, S // tk
    npl = plan['n'] if plan else 0

    def body(*refs):
        q_ref, k_ref, v_ref = refs[:3]
        o_ref, lse_ref = refs[3 + npl:5 + npl]
        m_sc, l_sc, acc_sc = refs[5 + 2 * npl:8 + 2 * npl]
        h, i, j = pl.program_id(0), pl.program_id(1), pl.program_id(2)
        if plan:
            pargs = (refs[3:3 + npl], refs[5 + npl:5 + 2 * npl], refs[8 + 2 * npl], refs[9 + 2 * npl])
            first = (i == 0) & (j == 0)
            pl.when((h == 0) & first)(functools.partial(plan['start'], *pargs))
            pl.when((h == (3 * MLA_HEADS) // 4) & first)(functools.partial(plan['forward'], *pargs))
            pl.when((h == MLA_HEADS - 1) & (i == nq - 1) & (j == nk - 1))(functools.partial(plan['finish'], *pargs))

        @pl.when(j == 0)
        def _():
            m_sc[...] = jnp.full_like(m_sc, -1e30)
            l_sc[...] = jnp.zeros_like(l_sc)
            acc_sc[...] = jnp.zeros_like(acc_sc)

        def part(rows, cols, n, masked):
            s = lax.dot_general(q_ref[rows, :], k_ref[cols, :], (((1,), (1,)), ((), ())), preferred_element_type=f32)
            if masked:
                row = lax.broadcasted_iota(jnp.int32, (n, n), 0)
                col = lax.broadcasted_iota(jnp.int32, (n, n), 1)
                s = jnp.where(col <= row, s, -1e30)
            m_prev = m_sc[rows, :]
            m_new = jnp.maximum(m_prev, jnp.max(s, axis=-1, keepdims=True))
            alpha = jnp.exp(m_prev - m_new)
            p = jnp.exp(s - m_new)
            l_sc[rows, :] = alpha * l_sc[rows, :] + jnp.sum(p, axis=-1, keepdims=True)
            acc_sc[rows, :] = alpha * acc_sc[rows, :] + jnp.dot(p.astype(bf16), v_ref[cols, :],
                                                                  preferred_element_type=f32)
            m_sc[rows, :] = m_new

        whole = (slice(0, tq), slice(0, tk), tq)
        pl.when(j < i)(functools.partial(part, *whole, False))
        pl.when(j == i)(functools.partial(part, *whole, True))

        @pl.when(j == nk - 1)
        def _():
            l = l_sc[...]
            o_ref[...] = acc_sc[...] / l
            lse_ref[...] = jnp.broadcast_to(m_sc[...] + jnp.log(l), (tq, LANES))

    qspec = pl.BlockSpec((tq, HEAD_PAD), lambda h, i, j: (i, h))
    kspec = pl.BlockSpec((tk, HEAD_PAD), lambda h, i, j: (jnp.minimum(j, i), h))
    anyspec = pl.BlockSpec(memory_space=pl.ANY)
    scratch = [pltpu.VMEM((tq, 1), f32), pltpu.VMEM((tq, 1), f32), pltpu.VMEM((tq, HEAD_PAD), f32)]
    if plan:
        scratch += [pltpu.SemaphoreType.DMA((plan['nsem'],)), pltpu.SemaphoreType.DMA((plan['nsem'],))]
    outs = _pc(body, grid=(MLA_HEADS, nq, nk), in_specs=[qspec, kspec, kspec] + [anyspec] * npl,
               out_specs=[qspec, qspec] + [anyspec] * npl,
               out_shape=[jax.ShapeDtypeStruct((S, MLA_PAD), f32)] * 2 + (plan['outs'] if plan else []),
               scratch_shapes=scratch,
               compiler_params=_cp(("arbitrary", "arbitrary", "arbitrary") if plan else ("parallel", "parallel", "arbitrary")),
               name=name)(qh, kh, vh, *(plan['ins'] if plan else []))
    return outs[0], outs[1], outs[2:]


def _attn_bwd(qh, kh, vh, do, lse, delta, plan=None, name="attn_bwd"):
    S = qh.shape[0]
    tq = tk = min(S, ATT_BLK)
    nq, nk = S // tq, S // tk
    npl = plan['n'] if plan else 0

    def body(*refs):
        q_ref, k_ref, v_ref, do_ref, lse_ref, dl_ref = refs[:6]
        dq_ref, dk_ref, dv_ref = refs[6 + npl:9 + npl]
        h, j, i = pl.program_id(0), pl.program_id(1), pl.program_id(2)
        if plan:
            pargs = (refs[6:6 + npl], refs[9 + npl:9 + 2 * npl], refs[9 + 2 * npl], refs[10 + 2 * npl])
            pl.when((h == 0) & (j == 0) & (i == 0))(functools.partial(plan['start'], *pargs))
            pl.when((h == MLA_HEADS - 1) & (j == nk - 1) & (i == nq - 1))(functools.partial(plan['finish'], *pargs))

        @pl.when((j == 0) & (i == 0))
        def _():
            dq_ref[...] = jnp.zeros_like(dq_ref)

        @pl.when(i == 0)
        def _():
            dk_ref[...] = jnp.zeros_like(dk_ref)
            dv_ref[...] = jnp.zeros_like(dv_ref)

        def part(r0, c0, n, masked):
            nt = (((1,), (1,)), ((), ()))
            tn = (((0,), (0,)), ((), ()))
            rows, cols = slice(r0, r0 + n), slice(c0, c0 + n)
            qv, kv_, dov = q_ref[rows, :], k_ref[cols, :], do_ref[rows, :]
            s = lax.dot_general(qv, kv_, nt, preferred_element_type=f32)
            p = jnp.exp(s - lse_ref[rows, :1])
            if masked:
                row = lax.broadcasted_iota(jnp.int32, (n, n), 0)
                col = lax.broadcasted_iota(jnp.int32, (n, n), 1)
                p = jnp.where(col <= row, p, 0.0)
            dp = lax.dot_general(dov, v_ref[cols, :], nt, preferred_element_type=f32)
            ds = (p * (dp - dl_ref[rows, :1])).astype(bf16)
            dv_ref[cols, :] += lax.dot_general(p.astype(bf16), dov, tn, preferred_element_type=f32)
            dk_ref[cols, :] += lax.dot_general(ds, qv, tn, preferred_element_type=f32)
            qrows = pl.ds(pl.multiple_of(i * tq + r0, n), n)
            dq_ref[qrows, :] += jnp.dot(ds, kv_, preferred_element_type=f32)

        def below():
            part(0, 0, tq, False)

        def diagonal():
            for r0, c0 in _DIAG_QUARTERS:
                part(r0 * hq, c0 * hq, hq, r0 == c0)

        hq = tq // 2
        pl.when(i > j)(below)
        pl.when(i == j)(diagonal)

    qspec = pl.BlockSpec((tq, HEAD_PAD), lambda h, j, i: (jnp.maximum(i, j), h))
    kspec = pl.BlockSpec((tk, HEAD_PAD), lambda h, j, i: (j, h))
    colspec = pl.BlockSpec((S, HEAD_PAD), lambda h, j, i: (0, h))
    anyspec = pl.BlockSpec(memory_space=pl.ANY)
    scratch = [pltpu.SemaphoreType.DMA((plan['nsem'],)), pltpu.SemaphoreType.DMA((plan['nsem'],))] if plan else []
    outs = _pc(body, grid=(MLA_HEADS, nk, nq), in_specs=[qspec, kspec, kspec, qspec, qspec, qspec] + [anyspec] * npl,
               out_specs=[colspec, kspec, kspec] + [anyspec] * npl,
               out_shape=[jax.ShapeDtypeStruct((S, MLA_PAD), f32)] * 3 + (plan['outs'] if plan else []),
               scratch_shapes=scratch,
               compiler_params=_cp(("arbitrary" if plan else "parallel", "arbitrary", "arbitrary")),
               name=name)(qh, kh, vh, do, lse, delta, *(plan['ins'] if plan else []))
    return outs[0], outs[1], outs[2], outs[3:]


def _xattn_fwd(q, kv, name="xattn_fwd"):
    S = q.shape[0]
    M = kv.shape[0]
    tq = _pick(S, (512, 256, 128))
    scale = X_HEAD_DIM ** -0.5

    def body(q_ref, kv_ref, o_ref):
        for h in range(X_HEADS):
            sl = slice(h * X_HEAD_DIM, (h + 1) * X_HEAD_DIM)
            k = kv_ref[:, sl]
            v = kv_ref[:, D_MODEL + h * X_HEAD_DIM:D_MODEL + (h + 1) * X_HEAD_DIM]
            s = lax.dot_general(q_ref[:, sl], k, (((1,), (1,)), ((), ())), preferred_element_type=f32) * scale
            e = jnp.exp(s - jnp.max(s, axis=-1, keepdims=True))
            p = e / jnp.sum(e, axis=-1, keepdims=True)
            o_ref[:, sl] = jnp.dot(p.astype(bf16), v, preferred_element_type=f32).astype(bf16)

    blk = pl.BlockSpec((tq, D_MODEL), lambda i: (i, 0))
    return _pc(body, grid=(S // tq,), in_specs=[blk, pl.BlockSpec((M, 2 * D_MODEL), lambda i: (0, 0))],
               out_specs=blk, out_shape=jax.ShapeDtypeStruct((S, D_MODEL), bf16),
               compiler_params=_cp(("parallel",)), name=name)(q, kv)


def _xattn_bwd(q, kv, do, plan=None, name="xattn_bwd"):
    S = q.shape[0]
    M = kv.shape[0]
    tq = _pick(S, (512, 256, 128))
    scale = X_HEAD_DIM ** -0.5
    npl = plan['n'] if plan else 0
    p_in, p_out, p_shapes, p_sems, p_args = _plan_extras(plan)

    def body(*refs):
        q_ref, kv_ref, do_ref = refs[:3]
        dq_ref, dkv_ref = refs[3 + npl:5 + npl]
        i = pl.program_id(0)
        if plan:
            pargs = (refs[3:3 + npl], refs[5 + npl:5 + 2 * npl], refs[5 + 2 * npl], refs[6 + 2 * npl])
            pl.when(i == 0)(functools.partial(plan['start'], *pargs))
            pl.when(i == S // tq - 1)(functools.partial(plan['finish'], *pargs))

        @pl.when(i == 0)
        def _():
            dkv_ref[...] = jnp.zeros_like(dkv_ref)

        nt = (((1,), (1,)), ((), ()))
        tn = (((0,), (0,)), ((), ()))
        for h in range(X_HEADS):
            sl = slice(h * X_HEAD_DIM, (h + 1) * X_HEAD_DIM)
            vsl = slice(D_MODEL + h * X_HEAD_DIM, D_MODEL + (h + 1) * X_HEAD_DIM)
            k, v, qv, dov = kv_ref[:, sl], kv_ref[:, vsl], q_ref[:, sl], do_ref[:, sl]
            s = lax.dot_general(qv, k, nt, preferred_element_type=f32) * scale
            e = jnp.exp(s - jnp.max(s, axis=-1, keepdims=True))
            p = e / jnp.sum(e, axis=-1, keepdims=True)
            dp = lax.dot_general(dov, v, nt, preferred_element_type=f32)
            ds = (p * (dp - jnp.sum(dp * p, axis=-1, keepdims=True)) * scale).astype(bf16)
            dq_ref[:, sl] = jnp.dot(ds, k, preferred_element_type=f32).astype(bf16)
            dkv_ref[:, sl] += lax.dot_general(ds, qv, tn, preferred_element_type=f32)
            dkv_ref[:, vsl] += lax.dot_general(p.astype(bf16), dov, tn, preferred_element_type=f32)

    blk = pl.BlockSpec((tq, D_MODEL), lambda i: (i, 0))
    full = pl.BlockSpec((M, 2 * D_MODEL), lambda i: (0, 0))
    outs = _pc(body, grid=(S // tq,), in_specs=[blk, full, blk] + p_in, out_specs=[blk, full] + p_out,
               out_shape=[jax.ShapeDtypeStruct((S, D_MODEL), bf16), jax.ShapeDtypeStruct((M, 2 * D_MODEL), f32)] + p_shapes,
               scratch_shapes=p_sems, compiler_params=_cp(("arbitrary",)), name=name)(q, kv, do, *p_args)
    return outs[0], outs[1], outs[2:]


ST_CHUNKS = 1


def _apow_init(a_ref, ap_ref, bp_ref, seg):
    P = MACRO_ST
    ar, ai = a_ref[:, :P], a_ref[:, P:]
    pr, pi = ar, ai
    for r in range(seg):
        ap_ref[r:r + 1, :P] = pr
        ap_ref[r:r + 1, P:] = pi
        if r < seg - 1:
            pr, pi = pr * ar - pi * ai, pr * ai + pi * ar
    br, bi = pr, pi
    for k in range(SUBLANES):
        bp_ref[k:k + 1, :P] = pr
        bp_ref[k:k + 1, P:] = pi
        pr, pi = pr * br - pi * bi, pr * bi + pi * br


def _segment_perm(tS):
    seg = tS // SUBLANES
    rows = jnp.arange(tS)
    src = (rows % SUBLANES) * seg + rows // SUBLANES
    return (src[:, None] == jnp.arange(tS)[None, :]).astype(f32)


def _unpermute_rows(pt, v):
    hi = v.astype(bf16)
    r1 = v - hi.astype(f32)
    mid = r1.astype(bf16)
    lo = (r1 - mid.astype(f32)).astype(bf16)
    out = jnp.dot(pt, jnp.concatenate([hi, mid, lo], axis=1), preferred_element_type=f32)
    w = v.shape[1]
    return (out[:, :w] + out[:, w:2 * w]) + out[:, 2 * w:]


def _scan_block(sc_ref, ap_ref, bp_ref, carry_ref, e_ref, seg, reverse):
    P = MACRO_ST
    sgn = -1.0 if reverse else 1.0
    CH = P // ST_CHUNKS
    rid = lax.broadcasted_iota(jnp.int32, (SUBLANES, CH), 0)
    for c in range(ST_CHUNKS):
        lr, li = slice(c * CH, (c + 1) * CH), slice(P + c * CH, P + (c + 1) * CH)
        ar, ai = ap_ref[0:1, lr], sgn * ap_ref[0:1, li]
        xr = xi = None
        for i in range(seg):
            r = seg - 1 - i if reverse else i
            rows = slice(SUBLANES * r, SUBLANES * (r + 1))
            sr, si = sc_ref[rows, lr], sc_ref[rows, li]
            if i == 0:
                xr, xi = sr, si
            else:
                xr, xi = ar * xr - ai * xi + sr, ar * xi + ai * xr + si
                sc_ref[rows, lr] = xr
                sc_ref[rows, li] = xi
        for sh in (1, 2, 4):
            pr, pi = bp_ref[sh - 1:sh, lr], sgn * bp_ref[sh - 1:sh, li]
            if reverse:
                tr = jnp.where(rid < SUBLANES - sh, pltpu.roll(xr, SUBLANES - sh, 0), 0.0)
                ti = jnp.where(rid < SUBLANES - sh, pltpu.roll(xi, SUBLANES - sh, 0), 0.0)
            else:
                tr = jnp.where(rid >= sh, pltpu.roll(xr, sh, 0), 0.0)
                ti = jnp.where(rid >= sh, pltpu.roll(xi, sh, 0), 0.0)
            xr, xi = xr + pr * tr - pi * ti, xi + pr * ti + pi * tr
        if reverse:
            bpr = jnp.zeros((SUBLANES, CH), f32)
            bpi = jnp.zeros((SUBLANES, CH), f32)
            for r in range(SUBLANES):
                bpr = jnp.where(rid == r, bp_ref[SUBLANES - 1 - r:SUBLANES - r, lr], bpr)
                bpi = jnp.where(rid == r, -bp_ref[SUBLANES - 1 - r:SUBLANES - r, li], bpi)
        else:
            bpr, bpi = bp_ref[:, lr], bp_ref[:, li]
        cr, cim = carry_ref[:, lr], carry_ref[:, li]
        xr, xi = xr + bpr * cr - bpi * cim, xi + bpr * cim + bpi * cr
        edge = 0 if reverse else SUBLANES - 1
        carry_ref[:, lr] = jnp.sum(jnp.where(rid == edge, xr, 0.0), axis=0, keepdims=True)
        carry_ref[:, li] = jnp.sum(jnp.where(rid == edge, xi, 0.0), axis=0, keepdims=True)
        if reverse:
            er = jnp.where(rid == SUBLANES - 1, cr, pltpu.roll(xr, SUBLANES - 1, 0))
            ei = jnp.where(rid == SUBLANES - 1, cim, pltpu.roll(xi, SUBLANES - 1, 0))
        else:
            er = jnp.where(rid == 0, cr, pltpu.roll(xr, 1, 0))
            ei = jnp.where(rid == 0, cim, pltpu.roll(xi, 1, 0))
        if e_ref is not None:
            e_ref[:, lr] = er
            e_ref[:, li] = ei
        for i in range(seg):
            r = seg - 1 - i if reverse else i
            rows = slice(SUBLANES * r, SUBLANES * (r + 1))
            pr, pi = ap_ref[i:i + 1, lr], sgn * ap_ref[i:i + 1, li]
            sc_ref[rows, lr] += pr * er - pi * ei
            sc_ref[rows, li] += pr * ei + pi * er


def _ssm_fwd(proj, bm, cm, a, d, name="ssm_fwd"):
    S = proj.shape[0]
    tS = _pick(S, (256, 128))
    nb = S // tS
    P2 = 2 * MACRO_ST
    seg = tS // SUBLANES
    ucol0 = (D_MODEL - SSM_WIDTH) // MACRO_CH

    perm = _segment_perm(tS)

    def body(u_ref, b_ref, c_ref, a_ref, d_ref, pm_ref, pt_ref, y_ref, xc_ref, bu_sc, ap_sc, bp_sc, car_sc):
        t = pl.program_id(1)

        @pl.when(t == 0)
        def _():
            _apow_init(a_ref, ap_sc, bp_sc, seg)
            car_sc[...] = jnp.zeros_like(car_sc)

        uv = u_ref[...]
        up = jnp.dot(pm_ref[...], uv.astype(bf16), preferred_element_type=f32).astype(bf16)
        bu_sc[...] = jnp.dot(up, b_ref[...], preferred_element_type=f32)
        xc_ref[...] = car_sc[...]
        _scan_block(bu_sc, ap_sc, bp_sc, car_sc, None, seg, False)
        yp = jnp.dot(bu_sc[...].astype(bf16), c_ref[...], preferred_element_type=f32)
        y_ref[...] = _unpermute_rows(pt_ref[...], yp) + d_ref[...] * uv

    sq = pl.BlockSpec((tS, tS), lambda m, t: (0, 0))
    return _pc(body, grid=(SSM_MACRO, nb),
               in_specs=[pl.BlockSpec((tS, MACRO_CH), lambda m, t: (t, ucol0 + m)),
                         pl.BlockSpec((None, MACRO_CH, P2), lambda m, t: (m, 0, 0)),
                         pl.BlockSpec((None, P2, MACRO_CH), lambda m, t: (m, 0, 0)),
                         pl.BlockSpec((None, 1, P2), lambda m, t: (m, 0, 0)),
                         pl.BlockSpec((1, MACRO_CH), lambda m, t: (0, m)), sq, sq],
               out_specs=[pl.BlockSpec((tS, MACRO_CH), lambda m, t: (t, m)),
                          pl.BlockSpec((None, None, 1, P2), lambda m, t: (m, t, 0, 0))],
               out_shape=[jax.ShapeDtypeStruct((S, SSM_WIDTH), f32), jax.ShapeDtypeStruct((SSM_MACRO, nb, 1, P2), f32)],
               scratch_shapes=[pltpu.VMEM((tS, P2), f32), pltpu.VMEM((seg, P2), f32),
                               pltpu.VMEM((SUBLANES, P2), f32), pltpu.VMEM((1, P2), f32)],
               compiler_params=_cp(("arbitrary", "arbitrary")), name=name)(
        proj, bm, cm, a, d.reshape(1, SSM_WIDTH), perm.astype(bf16), perm.T.astype(bf16))


def _ssm_bwd(proj, dy, xc, bm, cm, a, d, plan=None, name="ssm_bwd"):
    S = proj.shape[0]
    tS = _pick(S, (256, 128))
    nb = S // tS
    P = MACRO_ST
    P2 = 2 * P
    seg = tS // SUBLANES
    ucol0 = (D_MODEL - SSM_WIDTH) // MACRO_CH

    perm = _segment_perm(tS)
    npl = plan['n'] if plan else 0
    p_in, p_out, p_shapes, p_sems, p_args = _plan_extras(plan)

    def body(*refs):
        u_ref, dy_ref, xc_ref, b_ref, c_ref, a_ref, d_ref, pm_ref, pt_ref = refs[:9]
        du_ref, db_ref, dc_ref, da_ref, dd_ref = refs[9 + npl:14 + npl]
        x_sc, g_sc, ap_sc, bp_sc, e_sc, xcar_sc, gcar_sc = refs[14 + 2 * npl:21 + 2 * npl]
        t = pl.program_id(1)
        if plan:
            mg = pl.program_id(0)
            pargs = (refs[9:9 + npl], refs[14 + npl:14 + 2 * npl], refs[21 + 2 * npl], refs[22 + 2 * npl])
            pl.when((mg == 0) & (t == 0))(functools.partial(plan['start'], *pargs))
            pl.when((mg == SSM_MACRO - 1) & (t == nb - 1))(functools.partial(plan['finish'], *pargs))

        @pl.when(t == 0)
        def _():
            _apow_init(a_ref, ap_sc, bp_sc, seg)
            gcar_sc[...] = jnp.zeros_like(gcar_sc)
            db_ref[...] = jnp.zeros_like(db_ref)
            dc_ref[...] = jnp.zeros_like(dc_ref)
            da_ref[...] = jnp.zeros_like(da_ref)
            dd_ref[...] = jnp.zeros_like(dd_ref)

        nt = (((1,), (1,)), ((), ()))
        tn = (((0,), (0,)), ((), ()))
        uv = u_ref[...]
        dyv = dy_ref[...]
        pm = pm_ref[...]
        ub = jnp.dot(pm, uv.astype(bf16), preferred_element_type=f32).astype(bf16)
        dyb = jnp.dot(pm, dyv.astype(bf16), preferred_element_type=f32).astype(bf16)
        x_sc[...] = jnp.dot(ub, b_ref[...], preferred_element_type=f32)
        xcar_sc[...] = xc_ref[...]
        _scan_block(x_sc, ap_sc, bp_sc, xcar_sc, e_sc, seg, False)
        g_sc[...] = lax.dot_general(dyb, c_ref[...], nt, preferred_element_type=f32)
        _scan_block(g_sc, ap_sc, bp_sc, gcar_sc, None, seg, True)
        xv = x_sc[...]
        gv = g_sc[...]
        gb = gv.astype(bf16)
        dc_ref[...] += lax.dot_general(xv.astype(bf16), dyb, tn, preferred_element_type=f32)
        db_ref[...] += lax.dot_general(ub, gb, tn, preferred_element_type=f32)
        dup = lax.dot_general(gb, b_ref[...], nt, preferred_element_type=f32)
        du_ref[...] = _unpermute_rows(pt_ref[...], dup) + d_ref[...] * dyv
        dd_ref[...] += jnp.sum(dyv * uv, axis=0, keepdims=True)
        xp = jnp.concatenate([e_sc[...], xv[:tS - SUBLANES]], axis=0)
        xpr, xpi, ggr, ggi = xp[:, :P], xp[:, P:], gv[:, :P], gv[:, P:]
        da_ref[:, :P] += jnp.sum(ggr * xpr + ggi * xpi, axis=0, keepdims=True)
        da_ref[:, P:] += jnp.sum(ggi * xpr - ggr * xpi, axis=0, keepdims=True)

    rev = lambda t: nb - 1 - t
    outs = _pc(body, grid=(SSM_MACRO, nb),
               in_specs=[pl.BlockSpec((tS, MACRO_CH), lambda m, t: (rev(t), ucol0 + m)),
                         pl.BlockSpec((tS, MACRO_CH), lambda m, t: (rev(t), m)),
                         pl.BlockSpec((None, None, 1, P2), lambda m, t: (m, rev(t), 0, 0)),
                         pl.BlockSpec((None, MACRO_CH, P2), lambda m, t: (m, 0, 0)),
                         pl.BlockSpec((None, P2, MACRO_CH), lambda m, t: (m, 0, 0)),
                         pl.BlockSpec((None, 1, P2), lambda m, t: (m, 0, 0)),
                         pl.BlockSpec((1, MACRO_CH), lambda m, t: (0, m)),
                         pl.BlockSpec((tS, tS), lambda m, t: (0, 0)), pl.BlockSpec((tS, tS), lambda m, t: (0, 0))] + p_in,
               out_specs=[pl.BlockSpec((tS, MACRO_CH), lambda m, t: (rev(t), m)),
                          pl.BlockSpec((None, MACRO_CH, P2), lambda m, t: (m, 0, 0)),
                          pl.BlockSpec((None, P2, MACRO_CH), lambda m, t: (m, 0, 0)),
                          pl.BlockSpec((None, 1, P2), lambda m, t: (m, 0, 0)),
                          pl.BlockSpec((1, MACRO_CH), lambda m, t: (0, m))] + p_out,
               out_shape=[jax.ShapeDtypeStruct((S, SSM_WIDTH), f32),
                          jax.ShapeDtypeStruct((SSM_MACRO, MACRO_CH, P2), f32),
                          jax.ShapeDtypeStruct((SSM_MACRO, P2, MACRO_CH), f32),
                          jax.ShapeDtypeStruct((SSM_MACRO, 1, P2), f32),
                          jax.ShapeDtypeStruct((1, SSM_WIDTH), f32)] + p_shapes,
               scratch_shapes=[pltpu.VMEM((tS, P2), f32), pltpu.VMEM((tS, P2), f32),
                               pltpu.VMEM((seg, P2), f32), pltpu.VMEM((SUBLANES, P2), f32), pltpu.VMEM((SUBLANES, P2), f32),
                               pltpu.VMEM((1, P2), f32), pltpu.VMEM((1, P2), f32)] + p_sems,
               compiler_params=_cp(("arbitrary", "arbitrary")), name=name)(
        proj, dy, xc, bm, cm, a, d.reshape(1, SSM_WIDTH), perm.astype(bf16), perm.T.astype(bf16), *p_args)
    return outs[0], outs[1], outs[2], outs[3], outs[4], outs[5:]


_GELU_K = math.sqrt(2.0 / math.pi)
_GELU_C = 0.044715


def _glu_fwd(y, w, b, g, name="glu_fwd"):
    S, W = y.shape
    ts = _pick(S, (512, 256, 128))

    def body(y_ref, w_ref, b_ref, g_ref, z_ref, sn_ref, ge_ref):
        yv = y_ref[...]
        cdf = 0.5 * (1.0 + jnp.tanh(_GELU_K * (yv + _GELU_C * (yv * yv * yv))))
        ge = (yv * cdf).astype(bf16)
        z = jnp.dot(ge, w_ref[...], preferred_element_type=f32) + b_ref[...]
        s = yv * jax.nn.sigmoid(z)
        rstd = lax.rsqrt(jnp.sum(s * s, axis=-1, keepdims=True) * (1.0 / W) + EPS)
        z_ref[...] = z
        sn_ref[...] = (s * rstd * g_ref[...]).astype(bf16)
        ge_ref[...] = ge

    blk = pl.BlockSpec((ts, W), lambda i: (i, 0))
    row = pl.BlockSpec((1, W), lambda i: (0, 0))
    return _pc(body, grid=(S // ts,), in_specs=[blk, pl.BlockSpec((W, W), lambda i: (0, 0)), row, row],
               out_specs=[blk, blk, blk],
               out_shape=[jax.ShapeDtypeStruct((S, W), f32), jax.ShapeDtypeStruct((S, W), bf16),
                          jax.ShapeDtypeStruct((S, W), bf16)],
               compiler_params=_cp(("parallel",)), name=name)(y, w, b.reshape(1, W), g.reshape(1, W))


def _glu_bwd(y, z, dmixed, w, g, name="glu_bwd"):
    S, W = y.shape
    ts = _pick(S, (512, 256, 128))
    dcb = MLA_PAD // W

    def body(y_ref, z_ref, dsn_ref, w_ref, g_ref, dy_ref, dz_ref, dg_ref, db_ref):
        i = pl.program_id(0)
        yv, zv, gv = y_ref[...], z_ref[...], g_ref[...]
        sig = jax.nn.sigmoid(zv)
        s = yv * sig
        rstd = lax.rsqrt(jnp.sum(s * s, axis=-1, keepdims=True) * (1.0 / W) + EPS)
        sh = s * rstd
        dsn = dsn_ref[...]
        dsh = dsn * gv
        ds = rstd * (dsh - sh * (jnp.sum(dsh * sh, axis=-1, keepdims=True) * (1.0 / W)))
        dz = ds * s * (1.0 - sig)
        dzb = dz.astype(bf16)
        dge = lax.dot_general(dzb, w_ref[...], (((1,), (1,)), ((), ())), preferred_element_type=f32)
        t = jnp.tanh(_GELU_K * (yv + _GELU_C * (yv * yv * yv)))
        dgelu = 0.5 * (1.0 + t) + 0.5 * yv * (1.0 - t * t) * _GELU_K * (1.0 + 3.0 * _GELU_C * yv * yv)
        dy_ref[...] = ds * sig + dge * dgelu
        dz_ref[...] = dzb

        @pl.when(i == 0)
        def _():
            dg_ref[...] = jnp.zeros_like(dg_ref)
            db_ref[...] = jnp.zeros_like(db_ref)

        dg_ref[...] += jnp.sum(dsn * sh, axis=0, keepdims=True)
        db_ref[...] += jnp.sum(dz, axis=0, keepdims=True)

    blk = pl.BlockSpec((ts, W), lambda i: (i, 0))
    row = pl.BlockSpec((1, W), lambda i: (0, 0))
    return _pc(body, grid=(S // ts,),
               in_specs=[blk, blk, pl.BlockSpec((ts, W), lambda i: (i, dcb)), pl.BlockSpec((W, W), lambda i: (0, 0)), row],
               out_specs=[blk, blk, row, row],
               out_shape=[jax.ShapeDtypeStruct((S, W), f32), jax.ShapeDtypeStruct((S, W), bf16),
                          jax.ShapeDtypeStruct((1, W), f32), jax.ShapeDtypeStruct((1, W), f32)],
               compiler_params=_cp(("arbitrary",)), name=name)(y, z, dmixed, w, g.reshape(1, W))


def _ffn_up(hn, wg, wu, name="ffn_up"):
    S, K = hn.shape
    F = wg.shape[0]
    tm, tn = _pick(S, (512, 256, 128)), _pick(F, (1408, 256, 128))

    def body(h_ref, wg_ref, wu_ref, g_ref, u_ref, a_ref):
        hv = h_ref[...]
        nt = (((1,), (1,)), ((), ()))
        gv = lax.dot_general(hv, wg_ref[...], nt, preferred_element_type=f32)
        uv = lax.dot_general(hv, wu_ref[...], nt, preferred_element_type=f32)
        g_ref[...] = gv.astype(bf16)
        u_ref[...] = uv.astype(bf16)
        a_ref[...] = (gv * jax.nn.sigmoid(gv) * uv).astype(bf16)

    wspec = pl.BlockSpec((tn, K), lambda i, j: (j, 0))
    ospec = pl.BlockSpec((tm, tn), lambda i, j: (i, j))
    return _pc(body, grid=(S // tm, F // tn), in_specs=[pl.BlockSpec((tm, K), lambda i, j: (i, 0)), wspec, wspec],
               out_specs=[ospec] * 3,
               out_shape=[jax.ShapeDtypeStruct((S, F), bf16), jax.ShapeDtypeStruct((S, F), bf16),
                          jax.ShapeDtypeStruct((S, F), bf16)],
               compiler_params=_cp(("parallel", "parallel")), name=name)(hn, wg, wu)


def _ffn_bwd_act(dh, wd, gate, up, name="ffn_bwd_act"):
    S, K = dh.shape
    F = wd.shape[0]
    tm, tn = _pick(S, (512, 256, 128)), _pick(F, (1408, 256, 128))

    def body(dh_ref, wd_ref, g_ref, u_ref, dg_ref, du_ref):
        dact = lax.dot_general(dh_ref[...].astype(bf16), wd_ref[...], (((1,), (1,)), ((), ())),
                               preferred_element_type=f32)
        gv, uv = g_ref[...].astype(f32), u_ref[...].astype(f32)
        sig = jax.nn.sigmoid(gv)
        dg_ref[...] = (dact * uv * (sig * (1.0 + gv * (1.0 - sig)))).astype(bf16)
        du_ref[...] = (dact * (gv * sig)).astype(bf16)

    ospec = pl.BlockSpec((tm, tn), lambda i, j: (i, j))
    return _pc(body, grid=(S // tm, F // tn),
               in_specs=[pl.BlockSpec((tm, K), lambda i, j: (i, 0)), pl.BlockSpec((tn, K), lambda i, j: (j, 0)),
                         ospec, ospec],
               out_specs=[ospec] * 2, out_shape=[jax.ShapeDtypeStruct((S, F), bf16)] * 2,
               compiler_params=_cp(("parallel", "parallel")), name=name)(dh, wd, gate, up)


def _pad_heads(w, per_head, pieces):
    K = w.shape[0]
    w3 = w.reshape(K, MLA_HEADS, per_head)
    out = jnp.zeros((K, MLA_HEADS, HEAD_PAD), w.dtype)
    for s0, s1, d0 in pieces:
        out = out.at[:, :, d0:d0 + (s1 - s0)].set(w3[:, :, s0:s1])
    return out.reshape(K, MLA_PAD)


def _unpad_heads(wp, per_head, pieces):
    K = wp.shape[0]
    w3 = wp.reshape(K, MLA_HEADS, HEAD_PAD)
    out = jnp.zeros((K, MLA_HEADS, per_head), wp.dtype)
    for s0, s1, d0 in pieces:
        out = out.at[:, :, s0:s1].set(w3[:, :, d0:d0 + (s1 - s0)])
    return out.reshape(K, MLA_HEADS * per_head)


_Q_PIECES = [(0, QK_NOPE + QK_ROPE, 0)]
_K_PIECES = [(0, QK_NOPE, 0)]
_V_PIECES = [(QK_NOPE, QK_NOPE + V_HEAD, 0)]
_KR0 = Q_LORA + KV_LORA


def _pack_win(w):
    z = jnp.zeros((w.shape[0], LANES - QK_ROPE), w.dtype)
    return jnp.concatenate([w[:, :_KR0 + QK_ROPE], z, w[:, _KR0 + QK_ROPE:]], axis=1)


def _unpack_win(wp):
    return jnp.concatenate([wp[:, :_KR0 + QK_ROPE], wp[:, _KR0 + LANES:]], axis=1)


def _pack_wout(w):
    wa = w[:MLA_WIDTH].reshape(MLA_HEADS, V_HEAD, D_MODEL)
    wa = jnp.concatenate([wa, jnp.zeros_like(wa)], axis=1).reshape(MLA_PAD, D_MODEL)
    return jnp.concatenate([wa, w[MLA_WIDTH:]], axis=0)


def _unpack_wout(wp):
    wa = wp[:MLA_PAD].reshape(MLA_HEADS, HEAD_PAD, D_MODEL)[:, :V_HEAD].reshape(MLA_WIDTH, D_MODEL)
    return jnp.concatenate([wa, wp[MLA_PAD:]], axis=0)


def _pad_gain(g):
    g2 = g.reshape(MLA_HEADS, V_HEAD)
    return jnp.concatenate([g2, jnp.zeros_like(g2)], axis=1).reshape(MLA_PAD)


def _unpad_gain(gp):
    return gp.reshape(MLA_HEADS, HEAD_PAD)[:, :V_HEAD].reshape(MLA_WIDTH)


def _ssm_prep(lam_re, lam_im, log_dt, b_re, b_im, c_re, c_im):
    lam = lax.complex(lam_re, lam_im)
    dt = jnp.exp(log_dt)[:, None]
    a_bar = jnp.exp(lam * dt)
    b_bar = ((a_bar - 1.0) / lam)[:, None, :] * lax.complex(b_re, b_im)
    G8 = SSM_GROUPS // SSM_MACRO
    eye = jnp.eye(G8, dtype=f32)

    def bmat(part):
        p4 = part.reshape(SSM_MACRO, G8, SSM_GROUP, SSM_STATE)
        return jnp.einsum('mgcp,gh->mgchp', p4, eye).reshape(SSM_MACRO, MACRO_CH, MACRO_ST)

    def cmat(part):
        p4 = part.reshape(SSM_MACRO, G8, SSM_GROUP, SSM_STATE)
        return jnp.einsum('mgcp,gh->mgphc', p4, eye).reshape(SSM_MACRO, MACRO_ST, MACRO_CH)

    bm = jnp.concatenate([bmat(b_bar.real), bmat(b_bar.imag)], axis=2)
    cm = jnp.concatenate([cmat(c_re), -cmat(c_im)], axis=1)
    a4 = a_bar.reshape(SSM_MACRO, 1, MACRO_ST)
    a = jnp.concatenate([a4.real, a4.imag], axis=2)
    return bm, cm, a


def _rope_tables(positions):
    freqs = ROPE_THETA ** (-jnp.arange(0, QK_ROPE, 2, dtype=f32) / QK_ROPE)
    ang = positions.astype(f32)[:, None] * freqs
    cos, sin = jnp.cos(ang), jnp.sin(ang)
    S = positions.shape[0]
    half = QK_ROPE // 2
    one, zero = jnp.ones((S, QK_NOPE), f32), jnp.zeros((S, half), f32)
    z64, z32 = jnp.zeros((S, QK_NOPE), f32), jnp.zeros((S, LANES - QK_NOPE - QK_ROPE), f32)
    tc = jnp.concatenate([one, cos, cos, z32], axis=1)
    s1 = jnp.concatenate([z64, -sin, zero, z32], axis=1)
    s2 = jnp.concatenate([z64, zero, sin, z32], axis=1)
    return tc, s1, s2


def _layer_params(W, l):
    p = {}
    p['win'] = _pack_win(W['w_in'][l])
    p['wuq'] = _pad_heads(W['w_uq'][l], QK_NOPE + QK_ROPE, _Q_PIECES)
    wukv = W['w_ukv'][l]
    p['wukv'] = jnp.concatenate([_pad_heads(wukv, QK_NOPE + V_HEAD, _K_PIECES),
                                 _pad_heads(wukv, QK_NOPE + V_HEAD, _V_PIECES)], axis=1)
    p['attn_g'] = _pad_gain(W['attn_out_g'][l])
    return p


def _forward_layer(h, memn_in, tabs, W, l, name, gather=None):
    p = _layer_params(W, l)
    sv = {'h0': h, 'p': p}
    xn = _rms_fwd(h, W['norm_mix_g'][l], name=name + "rms_mix")
    proj = _mm([(xn, p['win'])], 'nn', f32, name=name + "mm_in")
    cqn = _rms_fwd(proj, W['q_norm_g'][l], col0=0, width=Q_LORA, name=name + "rms_q")
    ckvn = _rms_fwd(proj, W['kv_norm_g'][l], col0=Q_LORA, width=KV_LORA, name=name + "rms_kv")
    q = _mm([(cqn, p['wuq'])], 'nn', f32, name=name + "mm_uq")
    kv = _mm([(ckvn, p['wukv'])], 'nn', f32, name=name + "mm_ukv")
    qh, kh, vh = _rope_fwd(q, kv, proj, tabs, name=name + "rope")
    oh, lse, carried = _attn_fwd(qh, kh, vh, plan=gather[0] if gather else None, name=name + "attn")
    if gather:
        for n, t in gather[1](carried).items():
            W[n][l] = t
    p['wout'] = _pack_wout(W['w_out'][l])
    an = _rms_fwd(oh, p['attn_g'], n_valid=MLA_WIDTH, name=name + "rms_attn")
    bm, cm, a = W['ssm'][l]
    bmb, cmb = bm.astype(bf16), cm.astype(bf16)
    y, xc = _ssm_fwd(proj, bmb, cmb, a, W['ssm_d'][l], name=name + "ssm")
    z, sn, ge = _glu_fwd(y, W['ssm_w_glu'][l], W['ssm_b_glu'][l], W['ssm_out_g'][l], name=name + "glu")
    h1a = _mm([(an, p['wout'][:MLA_PAD])], 'nn', f32, res=h, name=name + "mm_out_a")
    h1 = _mm([(sn, p['wout'][MLA_PAD:])], 'nn', f32, res=h1a, name=name + "mm_out_s")
    hn2 = _rms_fwd(h1, W['norm_x_g'][l], name=name + "rms_x")
    memn = _rms_fwd(memn_in, W['mem_norm_g'][l], name=name + "rms_mem")
    qx = _mm([(hn2, W['w_xq'][l])], 'nn', bf16, name=name + "mm_xq")
    kvx = _mm([(memn, W['w_xkv'][l])], 'nn', bf16, name=name + "mm_xkv")
    ox = _xattn_fwd(qx, kvx, name=name + "xattn")
    h2 = _mm([(ox, W['w_xo'][l])], 'nn', f32, res=h1, name=name + "mm_xo")
    hn3 = _rms_fwd(h2, W['norm_ffn_g'][l], name=name + "rms_ffn")
    gate, up, act = _ffn_up(hn3, W['w_gate'][l], W['w_up'][l], name=name + "ffn_up")
    h3 = _mm([(act, W['w_down'][l])], 'nn', f32, res=h2, name=name + "mm_down")
    sv.update(xn=xn, proj=proj, cqn=cqn, ckvn=ckvn, qh=qh, kh=kh, vh=vh, oh=oh, lse=lse, an=an, bmb=bmb, cmb=cmb,
              a=a, y=y, xc=xc, z=z, sn=sn, ge=ge, h1=h1, hn2=hn2, memn=memn, qx=qx, kvx=kvx, ox=ox, h2=h2, hn3=hn3,
              gate=gate, up=up, act=act)
    return h3, sv


def _backward_layer(dh3, sv, memn_in, tabs, W, l, hook, name):
    p = sv['p']
    G = {}
    G['w_down'] = _mm([(sv['act'], dh3)], 'tn', f32, name=name + "dw_down")
    dgate, dup = _ffn_bwd_act(dh3, W['w_down'][l], sv['gate'], sv['up'], name=name + "ffn_bwd_act")
    dhn3 = _mm([(dgate, W['w_gate'][l]), (dup, W['w_up'][l])], 'nn', f32, name=name + "mm_dffn")
    G['w_gate'] = _mm([(dgate, sv['hn3'])], 'tn', f32, name=name + "dw_gate")
    G['w_up'] = _mm([(dup, sv['hn3'])], 'tn', f32, name=name + "dw_up")
    xplan = hook['ffn'](l, G) if hook else None
    dh2, dg = _rms_bwd(sv['h2'], W['norm_ffn_g'][l], dhn3, res=dh3, name=name + "rmsb_ffn")
    G['norm_ffn_g'] = dg[0]
    G['w_xo'] = _mm([(sv['ox'], dh2)], 'tn', f32, name=name + "dw_xo")
    dox = _mm([(dh2, W['w_xo'][l])], 'nt', bf16, name=name + "mm_dxo")
    dqx, dkvx, carried = _xattn_bwd(sv['qx'], sv['kvx'], dox, plan=xplan, name=name + "xattn_bwd")
    if xplan:
        xplan['done'](carried)
    G['w_xq'] = _mm([(sv['hn2'], dqx)], 'tn', f32, name=name + "dw_xq")
    G['w_xkv'] = _mm([(sv['memn'], dkvx)], 'tn', f32, name=name + "dw_xkv")
    dhn2 = _mm([(dqx, W['w_xq'][l])], 'nt', f32, name=name + "mm_dxq")
    dmemn = _mm([(dkvx, W['w_xkv'][l])], 'nt', f32, name=name + "mm_dxkv")
    dh1, dg = _rms_bwd(sv['h1'], W['norm_x_g'][l], dhn2, res=dh2, name=name + "rmsb_x")
    G['norm_x_g'] = dg[0]
    _, dg = _rms_bwd(memn_in, W['mem_norm_g'][l], dmemn, name=name + "rmsb_mem")
    G['mem_norm_g'] = dg[0]
    dwo_a = _mm([(sv['an'], dh1)], 'tn', f32, name=name + "dw_out_a")
    dwo_s = _mm([(sv['sn'], dh1)], 'tn', f32, name=name + "dw_out_s")
    G['w_out'] = _unpack_wout(jnp.concatenate([dwo_a, dwo_s], axis=0))
    dmixed = _mm([(dh1, p['wout'])], 'nt', f32, name=name + "mm_dout")
    dy, dz, dg, db = _glu_bwd(sv['y'], sv['z'], dmixed, W['ssm_w_glu'][l], W['ssm_out_g'][l], name=name + "glu_bwd")
    G['ssm_out_g'], G['ssm_b_glu'] = dg[0], db[0]
    G['ssm_w_glu'] = _mm([(sv['ge'], dz)], 'tn', f32, name=name + "dw_glu")
    xplan = hook['mid'](l, G) if hook else None
    du, dbm, dcm, da, dd, carried = _ssm_bwd(sv['proj'], dy, sv['xc'], sv['bmb'], sv['cmb'], sv['a'], W['ssm_d'][l],
                                             plan=xplan, name=name + "ssm_bwd")
    if xplan:
        xplan['done'](carried)
    G['ssm_d'] = dd[0]
    G['ssm_raw'] = (dbm, dcm, da)
    doh, dg, delta = _rms_bwd(sv['oh'], p['attn_g'], dmixed, width=MLA_PAD, n_valid=MLA_WIDTH, delta=True,
                              out_dtype=bf16, name=name + "rmsb_attn")
    G['attn_out_g'] = _unpad_gain(dg[0])
    plans = hook['take']() if hook else []
    plan = _merge_plans(plans) if plans else None
    dqh, dkh, dvh, carried = _attn_bwd(sv['qh'], sv['kh'], sv['vh'], doh, sv['lse'], delta, plan=plan,
                                       name=name + "attn_bwd")
    if plan:
        plan['done'](carried)
    dq, dkv, dkr = _rope_bwd(dqh, dkh, dvh, tabs, name=name + "rope_bwd")
    G['w_uq'] = _unpad_heads(_mm([(sv['cqn'], dq)], 'tn', f32, name=name + "dw_uq"), QK_NOPE + QK_ROPE, _Q_PIECES)
    dwukv = _mm([(sv['ckvn'], dkv)], 'tn', f32, name=name + "dw_ukv")
    G['w_ukv'] = (_unpad_heads(dwukv[:, :MLA_PAD], QK_NOPE + V_HEAD, _K_PIECES)
                  + _unpad_heads(dwukv[:, MLA_PAD:], QK_NOPE + V_HEAD, _V_PIECES))
    dcqn = _mm([(dq, p['wuq'])], 'nt', f32, name=name + "mm_duq")
    dckvn = _mm([(dkv, p['wukv'])], 'nt', f32, name=name + "mm_dukv")
    dcq, dg = _rms_bwd(sv['proj'], W['q_norm_g'][l], dcqn, col0=0, width=Q_LORA, out_dtype=bf16, name=name + "rmsb_q")
    G['q_norm_g'] = dg[0]
    dckv, dg = _rms_bwd(sv['proj'], W['kv_norm_g'][l], dckvn, col0=Q_LORA, width=KV_LORA, out_dtype=bf16,
                        name=name + "rmsb_kv")
    G['kv_norm_g'] = dg[0]
    dproj = jnp.concatenate([dcq, dckv, dkr, du.astype(bf16)], axis=1)
    G['w_in'] = _unpack_win(_mm([(sv['xn'], dproj)], 'tn', f32, name=name + "dw_in"))
    dxn = _mm([(dproj, p['win'])], 'nt', f32, name=name + "mm_din")
    dh0, dg = _rms_bwd(sv['h0'], W['norm_mix_g'][l], dxn, res=dh1, name=name + "rmsb_mix")
    G['norm_mix_g'] = dg[0]
    return dh0, G


def _local_step(x, mem, positions, target, W, gathers=None, hook=None):
    tabs = _rope_tables(positions)
    ssm_in = [(W['ssm_lambda_re'][l], W['ssm_lambda_im'][l], W['ssm_log_dt'][l], W['ssm_b_re'][l], W['ssm_b_im'][l],
               W['ssm_c_re'][l], W['ssm_c_im'][l]) for l in range(DEPTH)]
    preps = [jax.vjp(_ssm_prep, *ssm_in[l]) for l in range(DEPTH)]
    W = dict(W)
    W['ssm'] = [preps[l][0] for l in range(DEPTH)]
    h = x
    saved = []
    for l in range(DEPTH):
        h, sv = _forward_layer(h, mem, tabs, W, l, f"l{l}_", gathers[l] if gathers else None)
        saved.append(sv)
    dh, dgf, loss = _loss_head(h, W['final_norm_g'], target)
    grads = [None] * DEPTH
    for l in reversed(range(DEPTH)):
        dh, G = _backward_layer(dh, saved[l], mem, tabs, W, l, hook, f"l{l}b_")
        dbm, dcm, da = G.pop('ssm_raw')
        names = ['ssm_lambda_re', 'ssm_lambda_im', 'ssm_log_dt', 'ssm_b_re', 'ssm_b_im', 'ssm_c_re', 'ssm_c_im']
        for n, g in zip(names, preps[l][1]((dbm, dcm, da))):
            G[n] = g
        grads[l] = G
        if hook is not None and l > 0:
            hook['rest'](l, G)
    out = {n: [grads[l][n] for l in range(DEPTH)] if n in SHARDED else jnp.stack([grads[l][n] for l in range(DEPTH)])
           for n in grads[0]}
    out['final_norm_g'] = dgf[0]
    return loss[0, 0], dh, out


_HBM = pl.BlockSpec(memory_space=pltpu.HBM)


def _me():
    return lax.axis_index("x"), lax.axis_index("y"), lax.axis_index("c")


def _chip_peers(x, y, c):
    devs = [(1 - x, y, c), (x, 1 - y, c), (1 - x, 1 - y, c)]
    return devs, [2 * d[0] + d[1] for d in devs]


def _gather_plan(xs, half_first):
    n = len(xs)
    if half_first:
        ins = [t.reshape(2, 1, *t.shape[1:]) for t in xs]
        outs = [jax.ShapeDtypeStruct((2, 4, *t.shape[1:]), t.dtype) for t in xs]
    else:
        ins = [t.reshape(1, 2, t.shape[0] // 2, t.shape[1]) for t in xs]
        outs = [jax.ShapeDtypeStruct((4, 2, t.shape[0] // 2, t.shape[1]), t.dtype) for t in xs]

    def own(ref, h):
        return ref.at[h] if half_first else ref.at[:, h]

    def slot(ref, h, j):
        return ref.at[h, pl.ds(j, 1)] if half_first else ref.at[pl.ds(j, 1), h]

    def copies(src, dst, send, recv):
        x, y, c = _me()
        jme = 2 * x + y
        devs, js = _chip_peers(x, y, c)
        half, other = pl.ds(c, 1), pl.ds(1 - c, 1)
        mk = pltpu.make_async_remote_copy
        for i in range(n):
            for k in range(3):
                out_cp = mk(own(src[i], half), slot(dst[i], half, jme), send.at[6 * i + k], recv.at[6 * i + k],
                            device_id=devs[k], device_id_type=MESH)
                in_cp = mk(own(src[i], half), slot(dst[i], half, js[k]), send.at[6 * i + k], recv.at[6 * i + k],
                           device_id=devs[k], device_id_type=MESH)
                pass_cp = mk(slot(dst[i], half, js[k]), slot(dst[i], half, js[k]), send.at[6 * i + 3 + k],
                             recv.at[6 * i + 3 + k], device_id=(x, y, 1 - c), device_id_type=MESH)
                got_cp = mk(slot(dst[i], other, js[k]), slot(dst[i], other, js[k]), send.at[6 * i + 3 + k],
                            recv.at[6 * i + 3 + k], device_id=(x, y, 1 - c), device_id_type=MESH)
                yield out_cp, in_cp, pass_cp, got_cp

    def start(*refs):
        for out_cp, _, _, _ in copies(*refs):
            out_cp.start()

    def forward(*refs):
        for _, in_cp, pass_cp, _ in copies(*refs):
            in_cp.wait_recv()
            pass_cp.start()

    def finish(*refs):
        for out_cp, _, pass_cp, got_cp in copies(*refs):
            got_cp.wait_recv()
            out_cp.wait_send()
            pass_cp.wait_send()

    return dict(n=n, ins=ins, outs=outs, nsem=6 * n, start=start, forward=forward, finish=finish)


def _plan_refs(plan, refs):
    n = plan['n']
    return refs[:n], refs[n:2 * n], refs[2 * n], refs[2 * n + 1]


def _run_plan(plan, name):
    n = plan['n']

    def body(*refs):
        args = _plan_refs(plan, refs)
        plan['start'](*args)
        plan['forward'](*args)
        plan['finish'](*args)

    return _pc(body, in_specs=[_HBM] * n, out_specs=[_HBM] * n, out_shape=plan['outs'],
               scratch_shapes=[pltpu.SemaphoreType.DMA((plan['nsem'],)), pltpu.SemaphoreType.DMA((plan['nsem'],))],
               compiler_params=pltpu.CompilerParams(has_side_effects=True), name=name)(*plan['ins'])


def _fill_own(gathered, own, half_first):
    jme = (2 * lax.axis_index("x") + lax.axis_index("y")).astype(jnp.int32)
    zero = jnp.int32(0)
    if half_first:
        return lax.dynamic_update_slice(gathered, own[:, None], (zero, jme, zero, zero))
    return lax.dynamic_update_slice(gathered, own.reshape(1, *gathered.shape[1:]), (jme, zero, zero, zero))


def _exchange_plan(gs, done=None):
    n = len(gs)

    def copies(src, dst, send, recv):
        x, y, c = _me()
        for i in range(n):
            yield pltpu.make_async_remote_copy(src[i].at[:, pl.ds(1 - c, 1)], dst[i], send.at[i], recv.at[i],
                                               device_id=(x, y, 1 - c), device_id_type=MESH)

    def start(*refs):
        for cp in copies(*refs):
            cp.start()

    def finish(*refs):
        for cp in copies(*refs):
            cp.wait()

    outs = [jax.ShapeDtypeStruct((4, 1, *g.shape[2:]), g.dtype) for g in gs]
    return dict(n=n, ins=list(gs), outs=outs, nsem=n, start=start, forward=lambda *refs: None, finish=finish, done=done)


def _plan_extras(plan):
    if not plan:
        return [], [], [], [], []
    anyspec = pl.BlockSpec(memory_space=pl.ANY)
    sems = [pltpu.SemaphoreType.DMA((plan['nsem'],)), pltpu.SemaphoreType.DMA((plan['nsem'],))]
    return [anyspec] * plan['n'], [anyspec] * plan['n'], list(plan['outs']), sems, list(plan['ins'])


def _scatter_plan(ps, done=None):
    n = len(ps)

    def copies(src, dst, send, recv, off):
        x, y, c = _me()
        devs, js = _chip_peers(x, y, c)
        for i in range(n):
            for k in range(3):
                yield pltpu.make_async_remote_copy(src[i].at[pl.ds(js[k], 1)], dst[i].at[k], send.at[off + 3 * i + k],
                                                   recv.at[off + 3 * i + k], device_id=devs[k], device_id_type=MESH)

    def start(src, dst, send, recv, off=0):
        for cp in copies(src, dst, send, recv, off):
            cp.start()

    def finish(src, dst, send, recv, off=0):
        for cp in copies(src, dst, send, recv, off):
            cp.wait()

    outs = [jax.ShapeDtypeStruct((3, 1, *p.shape[1:]), p.dtype) for p in ps]
    return dict(n=n, ins=list(ps), outs=outs, nsem=3 * n, start=start, forward=lambda *refs: None, finish=finish,
                done=done)


def _merge_plans(plans):
    def run(which):
        def f(src, dst, send, recv):
            o = s = 0
            for p in plans:
                p[which](src[o:o + p['n']], dst[o:o + p['n']], send, recv, off=s)
                o, s = o + p['n'], s + p['nsem']
        return f

    def done(results):
        o = 0
        for p in plans:
            p['done'](results[o:o + p['n']])
            o += p['n']

    return dict(n=sum(p['n'] for p in plans), ins=[t for p in plans for t in p['ins']],
                outs=[t for p in plans for t in p['outs']], nsem=sum(p['nsem'] for p in plans),
                start=run('start'), forward=lambda *refs: None, finish=run('finish'), done=done)


def _swap_sibling(hs, name):
    n = len(hs)

    def body(*refs):
        src, dst = refs[:n], refs[n:2 * n]
        send, recv = refs[2 * n:]
        x, y, c = _me()
        cps = []
        for i in range(n):
            cp = pltpu.make_async_remote_copy(src[i], dst[i], send.at[i], recv.at[i], device_id=(x, y, 1 - c),
                                              device_id_type=MESH)
            cp.start()
            cps.append(cp)
        for cp in cps:
            cp.wait()

    outs = [jax.ShapeDtypeStruct(h.shape, h.dtype) for h in hs]
    return _pc(body, in_specs=[_HBM] * n, out_specs=[_HBM] * n, out_shape=outs,
               scratch_shapes=[pltpu.SemaphoreType.DMA((n,)), pltpu.SemaphoreType.DMA((n,))],
               compiler_params=pltpu.CompilerParams(has_side_effects=True), name=name)(*hs)


ELEMWISE_VMEM_BUDGET = 24 * 1024 * 1024


def _row_tile(r, n, narrays):
    limit = ELEMWISE_VMEM_BUDGET // (2 * 4 * narrays * n)
    best = SUBLANES
    for t in range(16, r + 1, 16):
        if r % t == 0 and t <= limit:
            best = t
    return best


def _add_half(g, r1, cidx, name):
    _, _, r, n = g.shape
    tr = _row_tile(r, n, 3)

    def body(c_ref, g_ref, r_ref, o_ref):
        o_ref[...] = (g_ref[...] + r_ref[...]).astype(GRAD_TRANSIT)

    blk = lambda f: pl.BlockSpec((None, None, tr, n), f)
    gs = pltpu.PrefetchScalarGridSpec(
        num_scalar_prefetch=1, grid=(4, r // tr),
        in_specs=[blk(lambda j, i, c: (j, c[0], i, 0)), blk(lambda j, i, c: (j, 0, i, 0))],
        out_specs=pl.BlockSpec((None, tr, n), lambda j, i, c: (j, i, 0)))
    return _pc(body, grid_spec=gs, out_shape=jax.ShapeDtypeStruct((4, r, n), GRAD_TRANSIT),
               compiler_params=_cp(("parallel", "parallel")), name=name)(cidx, g, r1)


def _add_chips(p, r3, jidx, name):
    _, r, n = p.shape
    tr = _row_tile(r, n, 5)

    def body(j_ref, p_ref, a_ref, b_ref, c_ref, o_ref):
        o_ref[...] = ((p_ref[...].astype(f32) + a_ref[...].astype(f32)) + b_ref[...].astype(f32)) + c_ref[...].astype(f32)

    rblk = lambda k: pl.BlockSpec((None, None, tr, n), lambda i, j: (k, 0, i, 0))
    gs = pltpu.PrefetchScalarGridSpec(
        num_scalar_prefetch=1, grid=(r // tr,),
        in_specs=[pl.BlockSpec((None, tr, n), lambda i, j: (j[0], i, 0)), rblk(0), rblk(1), rblk(2)],
        out_specs=pl.BlockSpec((tr, n), lambda i, j: (i, 0)))
    return _pc(body, grid_spec=gs, out_shape=jax.ShapeDtypeStruct((r, n), f32),
               compiler_params=_cp(("parallel",)), name=name)(jidx, p, r3, r3, r3)


def _adamw_halves(w, mine, theirs, m, v, cidx, name):
    L, r, n = w.shape
    r2 = r // 2
    tr = _row_tile(r2, n, 11)
    c1 = 1.0 / (1.0 - ADAM_B1 ** ADAM_STEP)
    c2 = 1.0 / (1.0 - ADAM_B2 ** ADAM_STEP)

    def body(c_ref, w_ref, a0_ref, b0_ref, a1_ref, b1_ref, m_ref, v_ref, g_ref, d_ref, mo_ref, vo_ref):
        l, hf = pl.program_id(0), pl.program_id(1)
        own = hf == c_ref[0]
        gv = jnp.where(l == 0, jnp.where(own, a0_ref[...], b0_ref[...]), jnp.where(own, a1_ref[...], b1_ref[...]))
        m2 = ADAM_B1 * m_ref[...] + (1.0 - ADAM_B1) * gv
        v2 = ADAM_B2 * v_ref[...] + (1.0 - ADAM_B2) * (gv * gv)
        g_ref[...] = gv
        d_ref[...] = -ADAM_LR * ((m2 * c1) / (jnp.sqrt(v2 * c2) + ADAM_EPS) + ADAM_WD * w_ref[...])
        mo_ref[...] = m2
        vo_ref[...] = v2

    full = pl.BlockSpec((None, None, tr, n), lambda l, hf, i, c: (l, hf, i, 0))

    def half(layer, own):
        return pl.BlockSpec((tr, n), lambda l, hf, i, c: (jnp.where((l == layer) & ((hf == c[0]) == own), i, 0), 0))

    gs = pltpu.PrefetchScalarGridSpec(
        num_scalar_prefetch=1, grid=(L, 2, r2 // tr),
        in_specs=[full, half(0, True), half(0, False), half(1, True), half(1, False), full, full], out_specs=[full] * 4)
    four = lambda t: t.reshape(L, 2, r2, n)
    outs = _pc(body, grid_spec=gs, out_shape=[jax.ShapeDtypeStruct((L, 2, r2, n), f32)] * 4,
               compiler_params=_cp(("parallel", "parallel", "parallel")), name=name)(
        cidx, four(w), mine[0], theirs[0], mine[1], theirs[1], four(m), four(v))
    return [t.reshape(w.shape) for t in outs]


def _adamw_whole(w, g, m, v, name):
    c1 = 1.0 / (1.0 - ADAM_B1 ** ADAM_STEP)
    c2 = 1.0 / (1.0 - ADAM_B2 ** ADAM_STEP)

    def body(w_ref, g_ref, m_ref, v_ref, d_ref, mo_ref, vo_ref):
        gv = g_ref[...]
        m2 = ADAM_B1 * m_ref[...] + (1.0 - ADAM_B1) * gv
        v2 = ADAM_B2 * v_ref[...] + (1.0 - ADAM_B2) * (gv * gv)
        d_ref[...] = -ADAM_LR * ((m2 * c1) / (jnp.sqrt(v2 * c2) + ADAM_EPS) + ADAM_WD * w_ref[...])
        mo_ref[...] = m2
        vo_ref[...] = v2

    return _pc(body, out_shape=[jax.ShapeDtypeStruct(w.shape, f32)] * 3, name=name)(w, g, m, v)


def _full_from_gathered(name, t):
    r, n = 2 * t.shape[2], t.shape[3]
    if SHARDED[name] == 1 or name in TRANSPOSED:
        return t.reshape(4 * r, n)
    return t.reshape(4, r, n).transpose(1, 0, 2).reshape(r, 4 * n)


def _shard_major(name, g):
    R, C = g.shape
    if SHARDED[name] == 1 or name in TRANSPOSED:
        return g.reshape(4, 2, R // 8, C)
    return g.reshape(R, 4, C // 4).transpose(1, 0, 2).reshape(4, 2, R // 2, C // 4)


_SMALL_ROWS = 288


def _pack_small(d):
    flat = jnp.concatenate([d[n].reshape(-1) for n in SMALL])
    total = 2 * 4 * _SMALL_ROWS * LANES
    flat = jnp.concatenate([flat, jnp.zeros((total - flat.shape[0],), f32)])
    return flat.reshape(4, 2, _SMALL_ROWS, LANES)


def _unpack_small(t, like):
    flat = t.reshape(-1)
    out, off = {}, 0
    for n in SMALL:
        sz = math.prod(like[n].shape)
        out[n] = flat[off:off + sz].reshape(like[n].shape)
        off += sz
    return out


def kernel(x, mem, positions, norm_mix_g, w_in, q_norm_g, w_uq, kv_norm_g, w_ukv, ssm_lambda_re, ssm_lambda_im, ssm_log_dt, ssm_b_re, ssm_b_im, ssm_c_re, ssm_c_im, ssm_d, ssm_w_glu, ssm_b_glu, attn_out_g, ssm_out_g, w_out, norm_x_g, mem_norm_g, w_xq, w_xkv, w_xo, norm_ffn_g, w_gate, w_up, w_down, final_norm_g, loss_target, m_norm_mix_g, m_w_in, m_q_norm_g, m_w_uq, m_kv_norm_g, m_w_ukv, m_ssm_lambda_re, m_ssm_lambda_im, m_ssm_log_dt, m_ssm_b_re, m_ssm_b_im, m_ssm_c_re, m_ssm_c_im, m_ssm_d, m_ssm_w_glu, m_ssm_b_glu, m_attn_out_g, m_ssm_out_g, m_w_out, m_norm_x_g, m_mem_norm_g, m_w_xq, m_w_xkv, m_w_xo, m_norm_ffn_g, m_w_gate, m_w_up, m_w_down, m_final_norm_g, v_norm_mix_g, v_w_in, v_q_norm_g, v_w_uq, v_kv_norm_g, v_w_ukv, v_ssm_lambda_re, v_ssm_lambda_im, v_ssm_log_dt, v_ssm_b_re, v_ssm_b_im, v_ssm_c_re, v_ssm_c_im, v_ssm_d, v_ssm_w_glu, v_ssm_b_glu, v_attn_out_g, v_ssm_out_g, v_w_out, v_norm_x_g, v_mem_norm_g, v_w_xq, v_w_xkv, v_w_xo, v_norm_ffn_g, v_w_gate, v_w_up, v_w_down, v_final_norm_g):
    given = dict(locals())
    swap = lambda n, t: jnp.swapaxes(t, *TRANSPOSED[n]) if n in TRANSPOSED else t
    w = {n: swap(n, given[n]) for n in WEIGHTS}
    m = {n: swap(n, given["m_" + n]) for n in WEIGHTS}
    v = {n: swap(n, given["v_" + n]) for n in WEIGHTS}
    big = list(SHARDED)

    shards = {n: w[n].astype(bf16) for n in big}
    early = [n for n in big if n in EARLY_WEIGHTS]
    rest = [n for n in big if n not in EARLY_WEIGHTS]

    def full(names, results, l):
        return {n: _full_from_gathered(n, _fill_own(t, shards[n][l], False)) for n, t in zip(names, results)}

    first = _run_plan(_gather_plan([shards[n][l] for l in range(DEPTH) for n in early], False), "allgather_weights_early")
    W = {n: [None] * DEPTH for n in big}
    for l in range(DEPTH):
        for n, t in full(early, first[l * len(early):(l + 1) * len(early)], l).items():
            W[n][l] = t
    gathers = [(_gather_plan([shards[n][l] for n in rest], False), functools.partial(full, rest, l=l))
               for l in range(DEPTH)]
    W.update({n: w[n] for n in SMALL})

    cidx = lax.axis_index("c").astype(jnp.int32).reshape(1)
    jidx = (2 * lax.axis_index("x") + lax.axis_index("y")).astype(jnp.int32).reshape(1)
    ffn = [n for n in big if n in FFN_WEIGHTS]
    tail = [n for n in big if n in EARLY_WEIGHTS]
    mid = [n for n in big if n not in FFN_WEIGHTS and n not in EARLY_WEIGHTS]
    sums, got = {}, {}
    ready = []

    def exchange(names, tag, l, G, extra=()):
        keys = [(l, n) for n in names] + [(l, n) for n, _ in extra]
        gs = [_shard_major(n, G[n]) for n in names] + [g for _, g in extra]

        def done(r1):
            ps = [_add_half(g, r, cidx, f"grad_add_half_{tag}_{k[1]}") for k, g, r in zip(keys, gs, r1)]
            sums.update(zip(keys, ps))
            ready.append(_scatter_plan(ps, done=lambda results: got.update(zip(keys, results))))

        return _exchange_plan(gs, done)

    def take():
        plans = list(ready)
        ready.clear()
        return plans

    def now(plan, name):
        plan['done'](_run_plan(plan, name))

    hook = {'ffn': lambda l, G: exchange(ffn, f"l{l}_ffn", l, G), 'mid': lambda l, G: exchange(mid, f"l{l}_mid", l, G),
            'rest': lambda l, G: now(exchange(tail, f"l{l}_tail", l, G), f"grad_exchange_halves_l{l}_tail"),
            'take': take}
    loss, dx, grads = _local_step(x[0], mem[0], positions[0], loss_target[0], W, gathers=gathers, hook=hook)
    loss = lax.psum(loss, ("x", "y", "c"))

    now(exchange(tail, "l0_tail", 0, {n: grads[n][0] for n in tail}, extra=[("small", _pack_small(grads))]),
        "grad_exchange_halves_l0_tail")
    now(_merge_plans(take()), "grad_scatter_chips_l0_tail")
    keys = list(sums)
    hs = dict(zip(keys, [_add_chips(sums[k], got[k], jidx, f"grad_add_chips_l{k[0]}_{k[1]}") for k in keys]))
    ts = dict(zip(keys, _swap_sibling([hs[k] for k in keys], "grad_swap_sibling")))

    out_g, out_d, out_m, out_v = {}, {}, {}, {}
    for n in big:
        mine, theirs = [hs[(l, n)] for l in range(DEPTH)], [ts[(l, n)] for l in range(DEPTH)]
        out_g[n], out_d[n], out_m[n], out_v[n] = _adamw_halves(w[n], mine, theirs, m[n], v[n], cidx, f"adamw_{n}")
    both = jnp.stack([hs[(0, "small")], ts[(0, "small")]])
    piece = jnp.where(cidx[0] == 0, both, both[::-1]).reshape(2 * _SMALL_ROWS, LANES)
    gsm = _fill_own(_run_plan(_gather_plan([piece], False), "allgather_small")[0], piece, False)
    out_g.update(_unpack_small(gsm, w))
    for n in SMALL:
        two = lambda t: t.reshape(1, -1) if t.ndim == 1 else t
        d_, m_, v_ = _adamw_whole(two(w[n]), two(out_g[n]), two(m[n]), two(v[n]), f"adamw_{n}")
        out_d[n], out_m[n], out_v[n] = (t.reshape(w[n].shape) for t in (d_, m_, v_))

    outs = [[swap(n, d[n]) for n in WEIGHTS] for d in (out_g, out_d, out_m, out_v)]
    return (loss, dx.reshape(x.shape), *outs[0], *outs[1], *outs[2], *outs[3])
```

```python
import functools
import math

import jax
import jax.numpy as jnp
from jax import lax
from jax.experimental import pallas as pl
from jax.experimental.pallas import tpu as pltpu

f32, bf16 = jnp.float32, jnp.bfloat16

D_MODEL = 1024
DEPTH = 2
MLA_HEADS = 8
QK_NOPE = 64
QK_ROPE = 32
V_HEAD = 64
Q_LORA = 256
KV_LORA = 128
MLA_WIDTH = MLA_HEADS * V_HEAD
ROPE_THETA = 10000.0
SSM_WIDTH = 512
SSM_GROUP = 16
SSM_GROUPS = 32
SSM_STATE = 64
IN_WIDTH = Q_LORA + KV_LORA + QK_ROPE + SSM_WIDTH
X_HEADS = 4
X_HEAD_DIM = D_MODEL // X_HEADS
D_FF = 2816
EPS = 1e-6
ADAM_LR, ADAM_B1, ADAM_B2, ADAM_EPS, ADAM_WD, ADAM_STEP = 0.001, 0.9, 0.999, 1e-08, 0.01, 10

LANES = 128
SUBLANES = 8
HEAD_PAD = 128
MLA_PAD = MLA_HEADS * HEAD_PAD
SSM_MACRO = 4
MACRO_CH = SSM_WIDTH // SSM_MACRO
MACRO_ST = SSM_GROUPS // SSM_MACRO * SSM_STATE
VMEM_LIMIT = 56 * 1024 * 1024
GRAD_TRANSIT = bf16

WEIGHTS = ['norm_mix_g', 'w_in', 'q_norm_g', 'w_uq', 'kv_norm_g', 'w_ukv', 'ssm_lambda_re', 'ssm_lambda_im',
           'ssm_log_dt', 'ssm_b_re', 'ssm_b_im', 'ssm_c_re', 'ssm_c_im', 'ssm_d', 'ssm_w_glu', 'ssm_b_glu',
           'attn_out_g', 'ssm_out_g', 'w_out', 'norm_x_g', 'mem_norm_g', 'w_xq', 'w_xkv', 'w_xo', 'norm_ffn_g',
           'w_gate', 'w_up', 'w_down', 'final_norm_g']
SHARDED = {'w_in': 1, 'w_uq': 2, 'w_ukv': 2, 'ssm_w_glu': 1, 'w_out': 1, 'w_xq': 1, 'w_xkv': 2, 'w_xo': 1,
           'w_gate': 2, 'w_up': 2, 'w_down': 1}
SMALL = [n for n in WEIGHTS if n not in SHARDED]
EARLY_WEIGHTS = ('w_in', 'w_uq', 'w_ukv')
FFN_WEIGHTS = ('w_gate', 'w_up', 'w_down')
TRANSPOSED = {'w_gate': (1, 2), 'w_up': (1, 2), 'ssm_b_re': (2, 3), 'ssm_b_im': (2, 3)}
MESH = pl.DeviceIdType.MESH


def _pc(body, **kw):
    return pl.pallas_call(body, **kw)


def _pick(n, prefs):
    for p in prefs:
        if n % p == 0:
            return p
    return n


def _cp(sem=None):
    return pltpu.CompilerParams(dimension_semantics=sem, vmem_limit_bytes=VMEM_LIMIT)


_TILE_CANDS = (1024, 1408, 512, 256, 128)
MM_VMEM_BUDGET = 40 * 1024 * 1024


def _mm_tiles(M, K, N, a_bytes, b_bytes, o_bytes, npair, has_res, need_acc):
    best = None
    for tm in _TILE_CANDS:
        for tk in _TILE_CANDS:
            if M % tm or K % tk:
                continue
            vm = npair * (2 * tm * tk * a_bytes + 2 * tk * N * b_bytes) + 2 * tm * N * o_bytes
            vm += tm * N * 4 * (1 + need_acc + 2 * has_res)
            if a_bytes == 4:
                vm += npair * tm * tk * 2
            if b_bytes == 4:
                vm += npair * tk * N * 2
            if vm <= MM_VMEM_BUDGET and (best is None or tm * tk > best[0]):
                best = (tm * tk, tm, tk)
    if best is None:
        return _pick(M, (256, 128)), _pick(K, (256, 128))
    return best[1], best[2]


def _mm(pairs, mode, out_dtype, res=None, name="mm"):
    a0, b0 = pairs[0]
    if mode == 'nn':
        (M, K), N = a0.shape, b0.shape[1]
        dims = (((1,), (0,)), ((), ()))
    elif mode == 'nt':
        (M, K), N = a0.shape, b0.shape[0]
        dims = (((1,), (1,)), ((), ()))
    else:
        (K, M), N = a0.shape, b0.shape[1]
        dims = (((0,), (0,)), ((), ()))
    npair = len(pairs)
    has_res = res is not None
    direct = out_dtype == f32
    tm, tk = _mm_tiles(M, K, N, a0.dtype.itemsize, b0.dtype.itemsize, jnp.dtype(out_dtype).itemsize, npair, has_res,
                       not direct)
    nk = K // tk

    def body(*refs):
        ins = refs[:2 * npair]
        res_ref = refs[2 * npair] if has_res else None
        o_ref = refs[2 * npair + has_res]
        acc = o_ref if direct else refs[2 * npair + has_res + 1]
        k = pl.program_id(1)
        s = None
        for p in range(npair):
            d = lax.dot_general(ins[2 * p][...].astype(bf16), ins[2 * p + 1][...].astype(bf16), dims,
                                preferred_element_type=f32)
            s = d if s is None else s + d

        @pl.when(k == 0)
        def _():
            acc[...] = s

        @pl.when(k > 0)
        def _():
            acc[...] += s

        if has_res or not direct:
            @pl.when(k == nk - 1)
            def _():
                r = acc[...]
                if has_res:
                    r = r + res_ref[...]
                o_ref[...] = r.astype(out_dtype)

    if mode == 'nn':
        a_spec = pl.BlockSpec((tm, tk), lambda i, k: (i, k))
        b_spec = pl.BlockSpec((tk, N), lambda i, k: (k, 0))
    elif mode == 'nt':
        a_spec = pl.BlockSpec((tm, tk), lambda i, k: (i, k))
        b_spec = pl.BlockSpec((N, tk), lambda i, k: (0, k))
    else:
        a_spec = pl.BlockSpec((tk, tm), lambda i, k: (k, i))
        b_spec = pl.BlockSpec((tk, N), lambda i, k: (k, 0))
    o_spec = pl.BlockSpec((tm, N), lambda i, k: (i, 0))
    in_specs = [a_spec, b_spec] * npair + ([o_spec] if has_res else [])
    args = [t for p in pairs for t in p] + ([res] if has_res else [])
    return _pc(body, grid=(M // tm, nk), in_specs=in_specs, out_specs=o_spec,
               out_shape=jax.ShapeDtypeStruct((M, N), out_dtype),
               scratch_shapes=[] if direct else [pltpu.VMEM((tm, N), f32)],
               compiler_params=_cp(("parallel", "arbitrary")), name=name)(*args)


def _rms_fwd(x, g, *, col0=0, width=None, n_valid=None, out_dtype=bf16, name="rms_fwd"):
    S = x.shape[0]
    width = width or x.shape[1]
    n_valid = n_valid or width
    ts = _pick(S, (512, 256, 128))
    cb = col0 // width

    def body(x_ref, g_ref, o_ref):
        xv = x_ref[...]
        ms = jnp.sum(xv * xv, axis=-1, keepdims=True) * (1.0 / n_valid)
        o_ref[...] = (xv * lax.rsqrt(ms + EPS) * g_ref[...]).astype(out_dtype)

    return _pc(body, grid=(S // ts,),
               in_specs=[pl.BlockSpec((ts, width), lambda i: (i, cb)), pl.BlockSpec((1, width), lambda i: (0, 0))],
               out_specs=pl.BlockSpec((ts, width), lambda i: (i, 0)),
               out_shape=jax.ShapeDtypeStruct((S, width), out_dtype),
               compiler_params=_cp(("parallel",)), name=name)(x, g.reshape(1, width))


def _rms_bwd(x, g, dy, *, col0=0, dcol0=0, width=None, n_valid=None, res=None, out_dtype=f32, delta=False,
             name="rms_bwd"):
    S = x.shape[0]
    width = width or x.shape[1]
    n_valid = n_valid or width
    ts = _pick(S, (512, 256, 128))
    cb, dcb = col0 // width, dcol0 // width
    has_res = res is not None

    def body(*refs):
        x_ref, g_ref, dy_ref = refs[:3]
        res_ref = refs[3] if has_res else None
        outs = refs[3 + has_res:]
        dx_ref, dg_ref = outs[0], outs[1]
        i = pl.program_id(0)
        xv = x_ref[...]
        gv = g_ref[...]
        dyv = dy_ref[...].astype(f32)
        rstd = lax.rsqrt(jnp.sum(xv * xv, axis=-1, keepdims=True) * (1.0 / n_valid) + EPS)
        xh = xv * rstd
        dxh = dyv * gv
        mean = jnp.sum(dxh * xh, axis=-1, keepdims=True) * (1.0 / n_valid)
        dx = rstd * (dxh - xh * mean)
        if delta:
            d_ref = outs[2]
            for h in range(width // LANES):
                sl = slice(h * LANES, (h + 1) * LANES)
                dsum = jnp.sum(dx[:, sl] * xv[:, sl], axis=-1, keepdims=True)
                d_ref[:, sl] = jnp.broadcast_to(dsum, (ts, LANES))
        if has_res:
            dx = dx + res_ref[...]
        dx_ref[...] = dx.astype(out_dtype)

        @pl.when(i == 0)
        def _():
            dg_ref[...] = jnp.zeros_like(dg_ref)

        dg_ref[...] += jnp.sum(dyv * xh, axis=0, keepdims=True)

    blk = lambda c: pl.BlockSpec((ts, width), lambda i: (i, c))
    in_specs = [blk(cb), pl.BlockSpec((1, width), lambda i: (0, 0)), blk(dcb)] + ([blk(0)] if has_res else [])
    out_specs = [blk(0), pl.BlockSpec((1, width), lambda i: (0, 0))] + ([blk(0)] if delta else [])
    out_shape = [jax.ShapeDtypeStruct((S, width), out_dtype), jax.ShapeDtypeStruct((1, width), f32)] + (
        [jax.ShapeDtypeStruct((S, width), f32)] if delta else [])
    args = [x, g.reshape(1, width), dy] + ([res] if has_res else [])
    return _pc(body, grid=(S // ts,), in_specs=in_specs, out_specs=out_specs, out_shape=out_shape,
               compiler_params=_cp(("arbitrary",)), name=name)(*args)


def _loss_head(h, g, target, name="loss_head"):
    S, D = h.shape
    ts = _pick(S, (512, 256, 128))

    def body(h_ref, g_ref, t_ref, dh_ref, dg_ref, loss_ref):
        i = pl.program_id(0)
        xv = h_ref[...]
        gv = g_ref[...]
        rstd = lax.rsqrt(jnp.sum(xv * xv, axis=-1, keepdims=True) * (1.0 / D) + EPS)
        xh = xv * rstd
        err = xh * gv - t_ref[...]
        dyv = err * (1.0 / D)
        dxh = dyv * gv
        mean = jnp.sum(dxh * xh, axis=-1, keepdims=True) * (1.0 / D)
        dh_ref[...] = rstd * (dxh - xh * mean)

        @pl.when(i == 0)
        def _():
            dg_ref[...] = jnp.zeros_like(dg_ref)
            loss_ref[...] = jnp.zeros_like(loss_ref)

        dg_ref[...] += jnp.sum(dyv * xh, axis=0, keepdims=True)
        part = jnp.sum(jnp.sum(err * err, axis=-1, keepdims=True), axis=0, keepdims=True) * (0.5 / D)
        loss_ref[...] += jnp.broadcast_to(part, (1, LANES))

    blk = pl.BlockSpec((ts, D), lambda i: (i, 0))
    row = pl.BlockSpec((1, D), lambda i: (0, 0))
    return _pc(body, grid=(S // ts,), in_specs=[blk, row, blk],
               out_specs=[blk, row, pl.BlockSpec((1, LANES), lambda i: (0, 0))],
               out_shape=[jax.ShapeDtypeStruct((S, D), f32), jax.ShapeDtypeStruct((1, D), f32),
                          jax.ShapeDtypeStruct((1, LANES), f32)],
               compiler_params=_cp(("arbitrary",)), name=name)(h, g.reshape(1, D), target)


def _rope_apply(x, tc, s1, s2):
    return x * tc + pltpu.roll(x, LANES - 16, 1) * s1 + pltpu.roll(x, 16, 1) * s2


def _rope_apply_t(dy, tc, s1, s2):
    return dy * tc + pltpu.roll(dy * s1, 16, 1) + pltpu.roll(dy * s2, LANES - 16, 1)


def _rope_fwd(q, kv, proj, tabs, name="rope_fwd"):
    S = q.shape[0]
    ts = _pick(S, (512, 256, 128))
    scale = (QK_NOPE + QK_ROPE) ** -0.5

    def body(q_ref, kk_ref, kvv_ref, kr_ref, tc_ref, s1_ref, s2_ref, qh_ref, kh_ref, vh_ref):
        tc, s1, s2 = tc_ref[...], s1_ref[...], s2_ref[...]
        krr = _rope_apply(pltpu.roll(kr_ref[...], QK_NOPE, 1), tc, s1, s2)
        for h in range(MLA_HEADS):
            sl = slice(h * HEAD_PAD, (h + 1) * HEAD_PAD)
            qh_ref[:, sl] = (_rope_apply(q_ref[:, sl], tc, s1, s2) * scale).astype(bf16)
            kh_ref[:, sl] = (kk_ref[:, sl] + krr).astype(bf16)
        vh_ref[...] = kvv_ref[...].astype(bf16)

    wide = lambda c: pl.BlockSpec((ts, MLA_PAD), lambda i: (i, c))
    tab = pl.BlockSpec((ts, LANES), lambda i: (i, 0))
    return _pc(body, grid=(S // ts,),
               in_specs=[wide(0), wide(0), wide(1), pl.BlockSpec((ts, LANES), lambda i: (i, 3)), tab, tab, tab],
               out_specs=[wide(0)] * 3, out_shape=[jax.ShapeDtypeStruct((S, MLA_PAD), bf16)] * 3,
               compiler_params=_cp(("parallel",)), name=name)(q, kv, kv, proj, *tabs)


def _rope_bwd(dqh, dkh, dvh, tabs, name="rope_bwd"):
    S = dqh.shape[0]
    ts = _pick(S, (512, 256, 128))
    scale = (QK_NOPE + QK_ROPE) ** -0.5

    def body(dq_ref, dk_ref, dv_ref, tc_ref, s1_ref, s2_ref, oq_ref, okv_ref, okr_ref):
        tc, s1, s2 = tc_ref[...], s1_ref[...], s2_ref[...]
        ksum = None
        for h in range(MLA_HEADS):
            sl = slice(h * HEAD_PAD, (h + 1) * HEAD_PAD)
            oq_ref[:, sl] = (_rope_apply_t(dq_ref[:, sl], tc, s1, s2) * scale).astype(bf16)
            dk = dk_ref[:, sl]
            okv_ref[:, sl] = dk.astype(bf16)
            ksum = dk if ksum is None else ksum + dk
        okv_ref[:, MLA_PAD:] = dv_ref[...].astype(bf16)
        dkr = pltpu.roll(_rope_apply_t(ksum, tc, s1, s2), LANES - QK_NOPE, 1)
        lane = lax.broadcasted_iota(jnp.int32, (ts, LANES), 1)
        okr_ref[...] = jnp.where(lane < QK_ROPE, dkr, 0.0).astype(bf16)

    wide = pl.BlockSpec((ts, MLA_PAD), lambda i: (i, 0))
    tab = pl.BlockSpec((ts, LANES), lambda i: (i, 0))
    return _pc(body, grid=(S // ts,), in_specs=[wide, wide, wide, tab, tab, tab],
               out_specs=[wide, pl.BlockSpec((ts, 2 * MLA_PAD), lambda i: (i, 0)), tab],
               out_shape=[jax.ShapeDtypeStruct((S, MLA_PAD), bf16), jax.ShapeDtypeStruct((S, 2 * MLA_PAD), bf16),
                          jax.ShapeDtypeStruct((S, LANES), bf16)],
               compiler_params=_cp(("parallel",)), name=name)(dqh, dkh, dvh, *tabs)


ATT_BLK = 1024
_DIAG_QUARTERS = ((0, 0), (1, 0), (1, 1))


def _attn_fwd(qh, kh, vh, plan=None, name="attn_fwd"):
    S = qh.shape[0]
    tq = tk = min(S, ATT_BLK)
    nq, nk = S // tq, S // tk
    npl = plan['n'] if plan else 0

    def body(*refs):
        q_ref, k_ref, v_ref = refs[:3]
        o_ref, lse_ref = refs[3 + npl:5 + npl]
        m_sc, l_sc, acc_sc = refs[5 + 2 * npl:8 + 2 * npl]
        h, i, j = pl.program_id(0), pl.program_id(1), pl.program_id(2)
        if plan:
            pargs = (refs[3:3 + npl], refs[5 + npl:5 + 2 * npl], refs[8 + 2 * npl], refs[9 + 2 * npl])
            first = (i == 0) & (j == 0)
            pl.when((h == 0) & first)(functools.partial(plan['start'], *pargs))
            pl.when((h == (3 * MLA_HEADS) // 4) & first)(functools.partial(plan['forward'], *pargs))
            pl.when((h == MLA_HEADS - 1) & (i == nq - 1) & (j == nk - 1))(functools.partial(plan['finish'], *pargs))

        @pl.when(j == 0)
        def _():
            m_sc[...] = jnp.full_like(m_sc, -1e30)
            l_sc[...] = jnp.zeros_like(l_sc)
            acc_sc[...] = jnp.zeros_like(acc_sc)

        def part(rows, cols, n, masked):
            s = lax.dot_general(q_ref[rows, :], k_ref[cols, :], (((1,), (1,)), ((), ())), preferred_element_type=f32)
            if masked:
                row = lax.broadcasted_iota(jnp.int32, (n, n), 0)
                col = lax.broadcasted_iota(jnp.int32, (n, n), 1)
                s = jnp.where(col <= row, s, -1e30)
            m_prev = m_sc[rows, :]
            m_new = jnp.maximum(m_prev, jnp.max(s, axis=-1, keepdims=True))
            alpha = jnp.exp(m_prev - m_new)
            p = jnp.exp(s - m_new)
            l_sc[rows, :] = alpha * l_sc[rows, :] + jnp.sum(p, axis=-1, keepdims=True)
            acc_sc[rows, :] = alpha * acc_sc[rows, :] + jnp.dot(p.astype(bf16), v_ref[cols, :],
                                                                  preferred_element_type=f32)
            m_sc[rows, :] = m_new

        whole = (slice(0, tq), slice(0, tk), tq)
        pl.when(j < i)(functools.partial(part, *whole, False))
        pl.when(j == i)(functools.partial(part, *whole, True))

        @pl.when(j == nk - 1)
        def _():
            l = l_sc[...]
            o_ref[...] = acc_sc[...] / l
            lse_ref[...] = jnp.broadcast_to(m_sc[...] + jnp.log(l), (tq, LANES))

    qspec = pl.BlockSpec((tq, HEAD_PAD), lambda h, i, j: (i, h))
    kspec = pl.BlockSpec((tk, HEAD_PAD), lambda h, i, j: (jnp.minimum(j, i), h))
    anyspec = pl.BlockSpec(memory_space=pl.ANY)
    scratch = [pltpu.VMEM((tq, 1), f32), pltpu.VMEM((tq, 1), f32), pltpu.VMEM((tq, HEAD_PAD), f32)]
    if plan:
        scratch += [pltpu.SemaphoreType.DMA((plan['nsem'],)), pltpu.SemaphoreType.DMA((plan['nsem'],))]
    outs = _pc(body, grid=(MLA_HEADS, nq, nk), in_specs=[qspec, kspec, kspec] + [anyspec] * npl,
               out_specs=[qspec, qspec] + [anyspec] * npl,
               out_shape=[jax.ShapeDtypeStruct((S, MLA_PAD), f32)] * 2 + (plan['outs'] if plan else []),
               scratch_shapes=scratch,
               compiler_params=_cp(("arbitrary", "arbitrary", "arbitrary") if plan else ("parallel", "parallel", "arbitrary")),
               name=name)(qh, kh, vh, *(plan['ins'] if plan else []))
    return outs[0], outs[1], outs[2:]


def _attn_bwd(qh, kh, vh, do, lse, delta, plan=None, name="attn_bwd"):
    S = qh.shape[0]
    tq = tk = min(S, ATT_BLK)
    nq, nk = S // tq, S // tk
    npl = plan['n'] if plan else 0

    def body(*refs):
        q_ref, k_ref, v_ref, do_ref, lse_ref, dl_ref = refs[:6]
        dq_ref, dk_ref, dv_ref = refs[6 + npl:9 + npl]
        h, j, i = pl.program_id(0), pl.program_id(1), pl.program_id(2)
        if plan:
            pargs = (refs[6:6 + npl], refs[9 + npl:9 + 2 * npl], refs[9 + 2 * npl], refs[10 + 2 * npl])
            pl.when((h == 0) & (j == 0) & (i == 0))(functools.partial(plan['start'], *pargs))
            pl.when((h == MLA_HEADS - 1) & (j == nk - 1) & (i == nq - 1))(functools.partial(plan['finish'], *pargs))

        @pl.when((j == 0) & (i == 0))
        def _():
            dq_ref[...] = jnp.zeros_like(dq_ref)

        @pl.when(i == 0)
        def _():
            dk_ref[...] = jnp.zeros_like(dk_ref)
            dv_ref[...] = jnp.zeros_like(dv_ref)

        def part(r0, c0, n, masked):
            nt = (((1,), (1,)), ((), ()))
            tn = (((0,), (0,)), ((), ()))
            rows, cols = slice(r0, r0 + n), slice(c0, c0 + n)
            qv, kv_, dov = q_ref[rows, :], k_ref[cols, :], do_ref[rows, :]
            s = lax.dot_general(qv, kv_, nt, preferred_element_type=f32)
            p = jnp.exp(s - lse_ref[rows, :1])
            if masked:
                row = lax.broadcasted_iota(jnp.int32, (n, n), 0)
                col = lax.broadcasted_iota(jnp.int32, (n, n), 1)
                p = jnp.where(col <= row, p, 0.0)
            dp = lax.dot_general(dov, v_ref[cols, :], nt, preferred_element_type=f32)
            ds = (p * (dp - dl_ref[rows, :1])).astype(bf16)
            dv_ref[cols, :] += lax.dot_general(p.astype(bf16), dov, tn, preferred_element_type=f32)
            dk_ref[cols, :] += lax.dot_general(ds, qv, tn, preferred_element_type=f32)
            qrows = pl.ds(pl.multiple_of(i * tq + r0, n), n)
            dq_ref[qrows, :] += jnp.dot(ds, kv_, preferred_element_type=f32)

        def below():
            part(0, 0, tq, False)

        def diagonal():
            for r0, c0 in _DIAG_QUARTERS:
                part(r0 * hq, c0 * hq, hq, r0 == c0)

        hq = tq // 2
        pl.when(i > j)(below)
        pl.when(i == j)(diagonal)

    qspec = pl.BlockSpec((tq, HEAD_PAD), lambda h, j, i: (jnp.maximum(i, j), h))
    kspec = pl.BlockSpec((tk, HEAD_PAD), lambda h, j, i: (j, h))
    colspec = pl.BlockSpec((S, HEAD_PAD), lambda h, j, i: (0, h))
    anyspec = pl.BlockSpec(memory_space=pl.ANY)
    scratch = [pltpu.SemaphoreType.DMA((plan['nsem'],)), pltpu.SemaphoreType.DMA((plan['nsem'],))] if plan else []
    outs = _pc(body, grid=(MLA_HEADS, nk, nq), in_specs=[qspec, kspec, kspec, qspec, qspec, qspec] + [anyspec] * npl,
               out_specs=[colspec, kspec, kspec] + [anyspec] * npl,
               out_shape=[jax.ShapeDtypeStruct((S, MLA_PAD), f32)] * 3 + (plan['outs'] if plan else []),
               scratch_shapes=scratch,
               compiler_params=_cp(("arbitrary" if plan else "parallel", "arbitrary", "arbitrary")),
               name=name)(qh, kh, vh, do, lse, delta, *(plan['ins'] if plan else []))
    return outs[0], outs[1], outs[2], outs[3:]


def _xattn_fwd(q, kv, name="xattn_fwd"):
    S = q.shape[0]
    M = kv.shape[0]
    tq = _pick(S, (1024, 512, 256, 128))
    scale = X_HEAD_DIM ** -0.5

    def body(q_ref, kv_ref, o_ref):
        for h in range(X_HEADS):
            sl = slice(h * X_HEAD_DIM, (h + 1) * X_HEAD_DIM)
            k = kv_ref[:, sl]
            v = kv_ref[:, D_MODEL + h * X_HEAD_DIM:D_MODEL + (h + 1) * X_HEAD_DIM]
            s = lax.dot_general(q_ref[:, sl], k, (((1,), (1,)), ((), ())), preferred_element_type=f32) * scale
            e = jnp.exp(s - jnp.max(s, axis=-1, keepdims=True))
            p = e / jnp.sum(e, axis=-1, keepdims=True)
            o_ref[:, sl] = jnp.dot(p.astype(bf16), v, preferred_element_type=f32).astype(bf16)

    blk = pl.BlockSpec((tq, D_MODEL), lambda i: (i, 0))
    return _pc(body, grid=(S // tq,), in_specs=[blk, pl.BlockSpec((M, 2 * D_MODEL), lambda i: (0, 0))],
               out_specs=blk, out_shape=jax.ShapeDtypeStruct((S, D_MODEL), bf16),
               compiler_params=_cp(("parallel",)), name=name)(q, kv)


def _xattn_bwd(q, kv, do, plan=None, name="xattn_bwd"):
    S = q.shape[0]
    M = kv.shape[0]
    tq = _pick(S, (1024, 512, 256, 128))
    scale = X_HEAD_DIM ** -0.5
    npl = plan['n'] if plan else 0
    p_in, p_out, p_shapes, p_sems, p_args = _plan_extras(plan)

    def body(*refs):
        q_ref, kv_ref, do_ref = refs[:3]
        dq_ref, dkv_ref = refs[3 + npl:5 + npl]
        i = pl.program_id(0)
        if plan:
            pargs = (refs[3:3 + npl], refs[5 + npl:5 + 2 * npl], refs[5 + 2 * npl], refs[6 + 2 * npl])
            pl.when(i == 0)(functools.partial(plan['start'], *pargs))
            pl.when(i == S // tq - 1)(functools.partial(plan['finish'], *pargs))

        @pl.when(i == 0)
        def _():
            dkv_ref[...] = jnp.zeros_like(dkv_ref)

        nt = (((1,), (1,)), ((), ()))
        tn = (((0,), (0,)), ((), ()))
        for h in range(X_HEADS):
            sl = slice(h * X_HEAD_DIM, (h + 1) * X_HEAD_DIM)
            vsl = slice(D_MODEL + h * X_HEAD_DIM, D_MODEL + (h + 1) * X_HEAD_DIM)
            k, v, qv, dov = kv_ref[:, sl], kv_ref[:, vsl], q_ref[:, sl], do_ref[:, sl]
            s = lax.dot_general(qv, k, nt, preferred_element_type=f32) * scale
            e = jnp.exp(s - jnp.max(s, axis=-1, keepdims=True))
            p = e / jnp.sum(e, axis=-1, keepdims=True)
            dp = lax.dot_general(dov, v, nt, preferred_element_type=f32)
            ds = (p * (dp - jnp.sum(dp * p, axis=-1, keepdims=True)) * scale).astype(bf16)
            dq_ref[:, sl] = jnp.dot(ds, k, preferred_element_type=f32).astype(bf16)
            dkv_ref[:, sl] += lax.dot_general(ds, qv, tn, preferred_element_type=f32)
            dkv_ref[:, vsl] += lax.dot_general(p.astype(bf16), dov, tn, preferred_element_type=f32)

    blk = pl.BlockSpec((tq, D_MODEL), lambda i: (i, 0))
    full = pl.BlockSpec((M, 2 * D_MODEL), lambda i: (0, 0))
    outs = _pc(body, grid=(S // tq,), in_specs=[blk, full, blk] + p_in, out_specs=[blk, full] + p_out,
               out_shape=[jax.ShapeDtypeStruct((S, D_MODEL), bf16), jax.ShapeDtypeStruct((M, 2 * D_MODEL), f32)] + p_shapes,
               scratch_shapes=p_sems, compiler_params=_cp(("arbitrary",)), name=name)(q, kv, do, *p_args)
    return outs[0], outs[1], outs[2:]


ST_CHUNKS = 1


def _apow_init(a_ref, ap_ref, bp_ref, seg):
    P = MACRO_ST
    ar, ai = a_ref[:, :P], a_ref[:, P:]
    pr, pi = ar, ai
    for r in range(seg):
        ap_ref[r:r + 1, :P] = pr
        ap_ref[r:r + 1, P:] = pi
        if r < seg - 1:
            pr, pi = pr * ar - pi * ai, pr * ai + pi * ar
    br, bi = pr, pi
    for k in range(SUBLANES):
        bp_ref[k:k + 1, :P] = pr
        bp_ref[k:k + 1, P:] = pi
        pr, pi = pr * br - pi * bi, pr * bi + pi * br


def _segment_perm(tS):
    seg = tS // SUBLANES
    rows = jnp.arange(tS)
    src = (rows % SUBLANES) * seg + rows // SUBLANES
    return (src[:, None] == jnp.arange(tS)[None, :]).astype(f32)


def _unpermute_rows(pt, v):
    hi = v.astype(bf16)
    r1 = v - hi.astype(f32)
    mid = r1.astype(bf16)
    lo = (r1 - mid.astype(f32)).astype(bf16)
    out = jnp.dot(pt, jnp.concatenate([hi, mid, lo], axis=1), preferred_element_type=f32)
    w = v.shape[1]
    return (out[:, :w] + out[:, w:2 * w]) + out[:, 2 * w:]


def _scan_block(sc_ref, ap_ref, bp_ref, carry_ref, e_ref, seg, reverse):
    P = MACRO_ST
    sgn = -1.0 if reverse else 1.0
    CH = P // ST_CHUNKS
    rid = lax.broadcasted_iota(jnp.int32, (SUBLANES, CH), 0)
    for c in range(ST_CHUNKS):
        lr, li = slice(c * CH, (c + 1) * CH), slice(P + c * CH, P + (c + 1) * CH)
        ar, ai = ap_ref[0:1, lr], sgn * ap_ref[0:1, li]
        xr = xi = None
        for i in range(seg):
            r = seg - 1 - i if reverse else i
            rows = slice(SUBLANES * r, SUBLANES * (r + 1))
            sr, si = sc_ref[rows, lr], sc_ref[rows, li]
            if i == 0:
                xr, xi = sr, si
            else:
                xr, xi = ar * xr - ai * xi + sr, ar * xi + ai * xr + si
                sc_ref[rows, lr] = xr
                sc_ref[rows, li] = xi
        for sh in (1, 2, 4):
            pr, pi = bp_ref[sh - 1:sh, lr], sgn * bp_ref[sh - 1:sh, li]
            if reverse:
                tr = jnp.where(rid < SUBLANES - sh, pltpu.roll(xr, SUBLANES - sh, 0), 0.0)
                ti = jnp.where(rid < SUBLANES - sh, pltpu.roll(xi, SUBLANES - sh, 0), 0.0)
            else:
                tr = jnp.where(rid >= sh, pltpu.roll(xr, sh, 0), 0.0)
                ti = jnp.where(rid >= sh, pltpu.roll(xi, sh, 0), 0.0)
            xr, xi = xr + pr * tr - pi * ti, xi + pr * ti + pi * tr
        if reverse:
            bpr = jnp.zeros((SUBLANES, CH), f32)
            bpi = jnp.zeros((SUBLANES, CH), f32)
            for r in range(SUBLANES):
                bpr = jnp.where(rid == r, bp_ref[SUBLANES - 1 - r:SUBLANES - r, lr], bpr)
                bpi = jnp.where(rid == r, -bp_ref[SUBLANES - 1 - r:SUBLANES - r, li], bpi)
        else:
            bpr, bpi = bp_ref[:, lr], bp_ref[:, li]
        cr, cim = carry_ref[:, lr], carry_ref[:, li]
        xr, xi = xr + bpr * cr - bpi * cim, xi + bpr * cim + bpi * cr
        edge = 0 if reverse else SUBLANES - 1
        carry_ref[:, lr] = jnp.sum(jnp.where(rid == edge, xr, 0.0), axis=0, keepdims=True)
        carry_ref[:, li] = jnp.sum(jnp.where(rid == edge, xi, 0.0), axis=0, keepdims=True)
        if reverse:
            er = jnp.where(rid == SUBLANES - 1, cr, pltpu.roll(xr, SUBLANES - 1, 0))
            ei = jnp.where(rid == SUBLANES - 1, cim, pltpu.roll(xi, SUBLANES - 1, 0))
        else:
            er = jnp.where(rid == 0, cr, pltpu.roll(xr, 1, 0))
            ei = jnp.where(rid == 0, cim, pltpu.roll(xi, 1, 0))
        if e_ref is not None:
            e_ref[:, lr] = er
            e_ref[:, li] = ei
        for i in range(seg):
            r = seg - 1 - i if reverse else i
            rows = slice(SUBLANES * r, SUBLANES * (r + 1))
            pr, pi = ap_ref[i:i + 1, lr], sgn * ap_ref[i:i + 1, li]
            sc_ref[rows, lr] += pr * er - pi * ei
            sc_ref[rows, li] += pr * ei + pi * er


def _ssm_fwd(proj, bm, cm, a, d, name="ssm_fwd"):
    S = proj.shape[0]
    tS = _pick(S, (256, 128))
    nb = S // tS
    P2 = 2 * MACRO_ST
    seg = tS // SUBLANES
    ucol0 = (D_MODEL - SSM_WIDTH) // MACRO_CH

    perm = _segment_perm(tS)

    def body(u_ref, b_ref, c_ref, a_ref, d_ref, pm_ref, pt_ref, y_ref, xc_ref, bu_sc, ap_sc, bp_sc, car_sc):
        t = pl.program_id(1)

        @pl.when(t == 0)
        def _():
            _apow_init(a_ref, ap_sc, bp_sc, seg)
            car_sc[...] = jnp.zeros_like(car_sc)

        uv = u_ref[...]
        up = jnp.dot(pm_ref[...], uv.astype(bf16), preferred_element_type=f32).astype(bf16)
        bu_sc[...] = jnp.dot(up, b_ref[...], preferred_element_type=f32)
        xc_ref[...] = car_sc[...]
        _scan_block(bu_sc, ap_sc, bp_sc, car_sc, None, seg, False)
        yp = jnp.dot(bu_sc[...].astype(bf16), c_ref[...], preferred_element_type=f32)
        y_ref[...] = _unpermute_rows(pt_ref[...], yp) + d_ref[...] * uv

    sq = pl.BlockSpec((tS, tS), lambda m, t: (0, 0))
    return _pc(body, grid=(SSM_MACRO, nb),
               in_specs=[pl.BlockSpec((tS, MACRO_CH), lambda m, t: (t, ucol0 + m)),
                         pl.BlockSpec((None, MACRO_CH, P2), lambda m, t: (m, 0, 0)),
                         pl.BlockSpec((None, P2, MACRO_CH), lambda m, t: (m, 0, 0)),
                         pl.BlockSpec((None, 1, P2), lambda m, t: (m, 0, 0)),
                         pl.BlockSpec((1, MACRO_CH), lambda m, t: (0, m)), sq, sq],
               out_specs=[pl.BlockSpec((tS, MACRO_CH), lambda m, t: (t, m)),
                          pl.BlockSpec((None, None, 1, P2), lambda m, t: (m, t, 0, 0))],
               out_shape=[jax.ShapeDtypeStruct((S, SSM_WIDTH), f32), jax.ShapeDtypeStruct((SSM_MACRO, nb, 1, P2), f32)],
               scratch_shapes=[pltpu.VMEM((tS, P2), f32), pltpu.VMEM((seg, P2), f32),
                               pltpu.VMEM((SUBLANES, P2), f32), pltpu.VMEM((1, P2), f32)],
               compiler_params=_cp(("arbitrary", "arbitrary")), name=name)(
        proj, bm, cm, a, d.reshape(1, SSM_WIDTH), perm.astype(bf16), perm.T.astype(bf16))


def _ssm_bwd(proj, dy, xc, bm, cm, a, d, plan=None, name="ssm_bwd"):
    S = proj.shape[0]
    tS = _pick(S, (256, 128))
    nb = S // tS
    P = MACRO_ST
    P2 = 2 * P
    seg = tS // SUBLANES
    ucol0 = (D_MODEL - SSM_WIDTH) // MACRO_CH

    perm = _segment_perm(tS)
    npl = plan['n'] if plan else 0
    p_in, p_out, p_shapes, p_sems, p_args = _plan_extras(plan)

    def body(*refs):
        u_ref, dy_ref, xc_ref, b_ref, c_ref, a_ref, d_ref, pm_ref, pt_ref = refs[:9]
        du_ref, db_ref, dc_ref, da_ref, dd_ref = refs[9 + npl:14 + npl]
        x_sc, g_sc, ap_sc, bp_sc, e_sc, xcar_sc, gcar_sc = refs[14 + 2 * npl:21 + 2 * npl]
        t = pl.program_id(1)
        if plan:
            mg = pl.program_id(0)
            pargs = (refs[9:9 + npl], refs[14 + npl:14 + 2 * npl], refs[21 + 2 * npl], refs[22 + 2 * npl])
            pl.when((mg == 0) & (t == 0))(functools.partial(plan['start'], *pargs))
            pl.when((mg == SSM_MACRO - 1) & (t == nb - 1))(functools.partial(plan['finish'], *pargs))

        @pl.when(t == 0)
        def _():
            _apow_init(a_ref, ap_sc, bp_sc, seg)
            gcar_sc[...] = jnp.zeros_like(gcar_sc)
            db_ref[...] = jnp.zeros_like(db_ref)
            dc_ref[...] = jnp.zeros_like(dc_ref)
            da_ref[...] = jnp.zeros_like(da_ref)
            dd_ref[...] = jnp.zeros_like(dd_ref)

        nt = (((1,), (1,)), ((), ()))
        tn = (((0,), (0,)), ((), ()))
        uv = u_ref[...]
        dyv = dy_ref[...]
        pm = pm_ref[...]
        ub = jnp.dot(pm, uv.astype(bf16), preferred_element_type=f32).astype(bf16)
        dyb = jnp.dot(pm, dyv.astype(bf16), preferred_element_type=f32).astype(bf16)
        x_sc[...] = jnp.dot(ub, b_ref[...], preferred_element_type=f32)
        xcar_sc[...] = xc_ref[...]
        _scan_block(x_sc, ap_sc, bp_sc, xcar_sc, e_sc, seg, False)
        g_sc[...] = lax.dot_general(dyb, c_ref[...], nt, preferred_element_type=f32)
        _scan_block(g_sc, ap_sc, bp_sc, gcar_sc, None, seg, True)
        xv = x_sc[...]
        gv = g_sc[...]
        gb = gv.astype(bf16)
        dc_ref[...] += lax.dot_general(xv.astype(bf16), dyb, tn, preferred_element_type=f32)
        db_ref[...] += lax.dot_general(ub, gb, tn, preferred_element_type=f32)
        dup = lax.dot_general(gb, b_ref[...], nt, preferred_element_type=f32)
        du_ref[...] = _unpermute_rows(pt_ref[...], dup) + d_ref[...] * dyv
        dd_ref[...] += jnp.sum(dyv * uv, axis=0, keepdims=True)
        xp = jnp.concatenate([e_sc[...], xv[:tS - SUBLANES]], axis=0)
        xpr, xpi, ggr, ggi = xp[:, :P], xp[:, P:], gv[:, :P], gv[:, P:]
        da_ref[:, :P] += jnp.sum(ggr * xpr + ggi * xpi, axis=0, keepdims=True)
        da_ref[:, P:] += jnp.sum(ggi * xpr - ggr * xpi, axis=0, keepdims=True)

    rev = lambda t: nb - 1 - t
    outs = _pc(body, grid=(SSM_MACRO, nb),
               in_specs=[pl.BlockSpec((tS, MACRO_CH), lambda m, t: (rev(t), ucol0 + m)),
                         pl.BlockSpec((tS, MACRO_CH), lambda m, t: (rev(t), m)),
                         pl.BlockSpec((None, None, 1, P2), lambda m, t: (m, rev(t), 0, 0)),
                         pl.BlockSpec((None, MACRO_CH, P2), lambda m, t: (m, 0, 0)),
                         pl.BlockSpec((None, P2, MACRO_CH), lambda m, t: (m, 0, 0)),
                         pl.BlockSpec((None, 1, P2), lambda m, t: (m, 0, 0)),
                         pl.BlockSpec((1, MACRO_CH), lambda m, t: (0, m)),
                         pl.BlockSpec((tS, tS), lambda m, t: (0, 0)), pl.BlockSpec((tS, tS), lambda m, t: (0, 0))] + p_in,
               out_specs=[pl.BlockSpec((tS, MACRO_CH), lambda m, t: (rev(t), m)),
                          pl.BlockSpec((None, MACRO_CH, P2), lambda m, t: (m, 0, 0)),
                          pl.BlockSpec((None, P2, MACRO_CH), lambda m, t: (m, 0, 0)),
                          pl.BlockSpec((None, 1, P2), lambda m, t: (m, 0, 0)),
                          pl.BlockSpec((1, MACRO_CH), lambda m, t: (0, m))] + p_out,
               out_shape=[jax.ShapeDtypeStruct((S, SSM_WIDTH), f32),
                          jax.ShapeDtypeStruct((SSM_MACRO, MACRO_CH, P2), f32),
                          jax.ShapeDtypeStruct((SSM_MACRO, P2, MACRO_CH), f32),
                          jax.ShapeDtypeStruct((SSM_MACRO, 1, P2), f32),
                          jax.ShapeDtypeStruct((1, SSM_WIDTH), f32)] + p_shapes,
               scratch_shapes=[pltpu.VMEM((tS, P2), f32), pltpu.VMEM((tS, P2), f32),
                               pltpu.VMEM((seg, P2), f32), pltpu.VMEM((SUBLANES, P2), f32), pltpu.VMEM((SUBLANES, P2), f32),
                               pltpu.VMEM((1, P2), f32), pltpu.VMEM((1, P2), f32)] + p_sems,
               compiler_params=_cp(("arbitrary", "arbitrary")), name=name)(
        proj, dy, xc, bm, cm, a, d.reshape(1, SSM_WIDTH), perm.astype(bf16), perm.T.astype(bf16), *p_args)
    return outs[0], outs[1], outs[2], outs[3], outs[4], outs[5:]


_GELU_K = math.sqrt(2.0 / math.pi)
_GELU_C = 0.044715


def _glu_fwd(y, w, b, g, name="glu_fwd"):
    S, W = y.shape
    ts = _pick(S, (512, 256, 128))

    def body(y_ref, w_ref, b_ref, g_ref, z_ref, sn_ref, ge_ref):
        yv = y_ref[...]
        cdf = 0.5 * (1.0 + jnp.tanh(_GELU_K * (yv + _GELU_C * (yv * yv * yv))))
        ge = (yv * cdf).astype(bf16)
        z = jnp.dot(ge, w_ref[...], preferred_element_type=f32) + b_ref[...]
        s = yv * jax.nn.sigmoid(z)
        rstd = lax.rsqrt(jnp.sum(s * s, axis=-1, keepdims=True) * (1.0 / W) + EPS)
        z_ref[...] = z
        sn_ref[...] = (s * rstd * g_ref[...]).astype(bf16)
        ge_ref[...] = ge

    blk = pl.BlockSpec((ts, W), lambda i: (i, 0))
    row = pl.BlockSpec((1, W), lambda i: (0, 0))
    return _pc(body, grid=(S // ts,), in_specs=[blk, pl.BlockSpec((W, W), lambda i: (0, 0)), row, row],
               out_specs=[blk, blk, blk],
               out_shape=[jax.ShapeDtypeStruct((S, W), f32), jax.ShapeDtypeStruct((S, W), bf16),
                          jax.ShapeDtypeStruct((S, W), bf16)],
               compiler_params=_cp(("parallel",)), name=name)(y, w, b.reshape(1, W), g.reshape(1, W))


def _glu_bwd(y, z, dmixed, w, g, name="glu_bwd"):
    S, W = y.shape
    ts = _pick(S, (512, 256, 128))
    dcb = MLA_PAD // W

    def body(y_ref, z_ref, dsn_ref, w_ref, g_ref, dy_ref, dz_ref, dg_ref, db_ref):
        i = pl.program_id(0)
        yv, zv, gv = y_ref[...], z_ref[...], g_ref[...]
        sig = jax.nn.sigmoid(zv)
        s = yv * sig
        rstd = lax.rsqrt(jnp.sum(s * s, axis=-1, keepdims=True) * (1.0 / W) + EPS)
        sh = s * rstd
        dsn = dsn_ref[...]
        dsh = dsn * gv
        ds = rstd * (dsh - sh * (jnp.sum(dsh * sh, axis=-1, keepdims=True) * (1.0 / W)))
        dz = ds * s * (1.0 - sig)
        dzb = dz.astype(bf16)
        dge = lax.dot_general(dzb, w_ref[...], (((1,), (1,)), ((), ())), preferred_element_type=f32)
        t = jnp.tanh(_GELU_K * (yv + _GELU_C * (yv * yv * yv)))
        dgelu = 0.5 * (1.0 + t) + 0.5 * yv * (1.0 - t * t) * _GELU_K * (1.0 + 3.0 * _GELU_C * yv * yv)
        dy_ref[...] = ds * sig + dge * dgelu
        dz_ref[...] = dzb

        @pl.when(i == 0)
        def _():
            dg_ref[...] = jnp.zeros_like(dg_ref)
            db_ref[...] = jnp.zeros_like(db_ref)

        dg_ref[...] += jnp.sum(dsn * sh, axis=0, keepdims=True)
        db_ref[...] += jnp.sum(dz, axis=0, keepdims=True)

    blk = pl.BlockSpec((ts, W), lambda i: (i, 0))
    row = pl.BlockSpec((1, W), lambda i: (0, 0))
    return _pc(body, grid=(S // ts,),
               in_specs=[blk, blk, pl.BlockSpec((ts, W), lambda i: (i, dcb)), pl.BlockSpec((W, W), lambda i: (0, 0)), row],
               out_specs=[blk, blk, row, row],
               out_shape=[jax.ShapeDtypeStruct((S, W), f32), jax.ShapeDtypeStruct((S, W), bf16),
                          jax.ShapeDtypeStruct((1, W), f32), jax.ShapeDtypeStruct((1, W), f32)],
               compiler_params=_cp(("arbitrary",)), name=name)(y, z, dmixed, w, g.reshape(1, W))


def _ffn_up(hn, wg, wu, name="ffn_up"):
    S, K = hn.shape
    F = wg.shape[0]
    tm, tn = _pick(S, (512, 256, 128)), _pick(F, (1408, 256, 128))

    def body(h_ref, wg_ref, wu_ref, g_ref, u_ref, a_ref):
        hv = h_ref[...]
        nt = (((1,), (1,)), ((), ()))
        gv = lax.dot_general(hv, wg_ref[...], nt, preferred_element_type=f32)
        uv = lax.dot_general(hv, wu_ref[...], nt, preferred_element_type=f32)
        g_ref[...] = gv.astype(bf16)
        u_ref[...] = uv.astype(bf16)
        a_ref[...] = (gv * jax.nn.sigmoid(gv) * uv).astype(bf16)

    wspec = pl.BlockSpec((tn, K), lambda i, j: (j, 0))
    ospec = pl.BlockSpec((tm, tn), lambda i, j: (i, j))
    return _pc(body, grid=(S // tm, F // tn), in_specs=[pl.BlockSpec((tm, K), lambda i, j: (i, 0)), wspec, wspec],
               out_specs=[ospec] * 3,
               out_shape=[jax.ShapeDtypeStruct((S, F), bf16), jax.ShapeDtypeStruct((S, F), bf16),
                          jax.ShapeDtypeStruct((S, F), bf16)],
               compiler_params=_cp(("parallel", "parallel")), name=name)(hn, wg, wu)


def _ffn_bwd_act(dh, wd, gate, up, name="ffn_bwd_act"):
    S, K = dh.shape
    F = wd.shape[0]
    tm, tn = _pick(S, (512, 256, 128)), _pick(F, (1408, 256, 128))

    def body(dh_ref, wd_ref, g_ref, u_ref, dg_ref, du_ref):
        dact = lax.dot_general(dh_ref[...].astype(bf16), wd_ref[...], (((1,), (1,)), ((), ())),
                               preferred_element_type=f32)
        gv, uv = g_ref[...].astype(f32), u_ref[...].astype(f32)
        sig = jax.nn.sigmoid(gv)
        dg_ref[...] = (dact * uv * (sig * (1.0 + gv * (1.0 - sig)))).astype(bf16)
        du_ref[...] = (dact * (gv * sig)).astype(bf16)

    ospec = pl.BlockSpec((tm, tn), lambda i, j: (i, j))
    return _pc(body, grid=(S // tm, F // tn),
               in_specs=[pl.BlockSpec((tm, K), lambda i, j: (i, 0)), pl.BlockSpec((tn, K), lambda i, j: (j, 0)),
                         ospec, ospec],
               out_specs=[ospec] * 2, out_shape=[jax.ShapeDtypeStruct((S, F), bf16)] * 2,
               compiler_params=_cp(("parallel", "parallel")), name=name)(dh, wd, gate, up)


def _pad_heads(w, per_head, pieces):
    K = w.shape[0]
    w3 = w.reshape(K, MLA_HEADS, per_head)
    out = jnp.zeros((K, MLA_HEADS, HEAD_PAD), w.dtype)
    for s0, s1, d0 in pieces:
        out = out.at[:, :, d0:d0 + (s1 - s0)].set(w3[:, :, s0:s1])
    return out.reshape(K, MLA_PAD)


def _unpad_heads(wp, per_head, pieces):
    K = wp.shape[0]
    w3 = wp.reshape(K, MLA_HEADS, HEAD_PAD)
    out = jnp.zeros((K, MLA_HEADS, per_head), wp.dtype)
    for s0, s1, d0 in pieces:
        out = out.at[:, :, s0:s1].set(w3[:, :, d0:d0 + (s1 - s0)])
    return out.reshape(K, MLA_HEADS * per_head)


_Q_PIECES = [(0, QK_NOPE + QK_ROPE, 0)]
_K_PIECES = [(0, QK_NOPE, 0)]
_V_PIECES = [(QK_NOPE, QK_NOPE + V_HEAD, 0)]
_KR0 = Q_LORA + KV_LORA


def _pack_win(w):
    z = jnp.zeros((w.shape[0], LANES - QK_ROPE), w.dtype)
    return jnp.concatenate([w[:, :_KR0 + QK_ROPE], z, w[:, _KR0 + QK_ROPE:]], axis=1)


def _unpack_win(wp):
    return jnp.concatenate([wp[:, :_KR0 + QK_ROPE], wp[:, _KR0 + LANES:]], axis=1)


def _pack_wout(w):
    wa = w[:MLA_WIDTH].reshape(MLA_HEADS, V_HEAD, D_MODEL)
    wa = jnp.concatenate([wa, jnp.zeros_like(wa)], axis=1).reshape(MLA_PAD, D_MODEL)
    return jnp.concatenate([wa, w[MLA_WIDTH:]], axis=0)


def _unpack_wout(wp):
    wa = wp[:MLA_PAD].reshape(MLA_HEADS, HEAD_PAD, D_MODEL)[:, :V_HEAD].reshape(MLA_WIDTH, D_MODEL)
    return jnp.concatenate([wa, wp[MLA_PAD:]], axis=0)


def _pad_gain(g):
    g2 = g.reshape(MLA_HEADS, V_HEAD)
    return jnp.concatenate([g2, jnp.zeros_like(g2)], axis=1).reshape(MLA_PAD)


def _unpad_gain(gp):
    return gp.reshape(MLA_HEADS, HEAD_PAD)[:, :V_HEAD].reshape(MLA_WIDTH)


def _ssm_prep(lam_re, lam_im, log_dt, b_re, b_im, c_re, c_im):
    lam = lax.complex(lam_re, lam_im)
    dt = jnp.exp(log_dt)[:, None]
    a_bar = jnp.exp(lam * dt)
    b_bar = ((a_bar - 1.0) / lam)[:, None, :] * lax.complex(b_re, b_im)
    G8 = SSM_GROUPS // SSM_MACRO
    eye = jnp.eye(G8, dtype=f32)

    def bmat(part):
        p4 = part.reshape(SSM_MACRO, G8, SSM_GROUP, SSM_STATE)
        return jnp.einsum('mgcp,gh->mgchp', p4, eye).reshape(SSM_MACRO, MACRO_CH, MACRO_ST)

    def cmat(part):
        p4 = part.reshape(SSM_MACRO, G8, SSM_GROUP, SSM_STATE)
        return jnp.einsum('mgcp,gh->mgphc', p4, eye).reshape(SSM_MACRO, MACRO_ST, MACRO_CH)

    bm = jnp.concatenate([bmat(b_bar.real), bmat(b_bar.imag)], axis=2)
    cm = jnp.concatenate([cmat(c_re), -cmat(c_im)], axis=1)
    a4 = a_bar.reshape(SSM_MACRO, 1, MACRO_ST)
    a = jnp.concatenate([a4.real, a4.imag], axis=2)
    return bm, cm, a


def _rope_tables(positions):
    freqs = ROPE_THETA ** (-jnp.arange(0, QK_ROPE, 2, dtype=f32) / QK_ROPE)
    ang = positions.astype(f32)[:, None] * freqs
    cos, sin = jnp.cos(ang), jnp.sin(ang)
    S = positions.shape[0]
    half = QK_ROPE // 2
    one, zero = jnp.ones((S, QK_NOPE), f32), jnp.zeros((S, half), f32)
    z64, z32 = jnp.zeros((S, QK_NOPE), f32), jnp.zeros((S, LANES - QK_NOPE - QK_ROPE), f32)
    tc = jnp.concatenate([one, cos, cos, z32], axis=1)
    s1 = jnp.concatenate([z64, -sin, zero, z32], axis=1)
    s2 = jnp.concatenate([z64, zero, sin, z32], axis=1)
    return tc, s1, s2


def _layer_params(W, l):
    p = {}
    p['win'] = _pack_win(W['w_in'][l])
    p['wuq'] = _pad_heads(W['w_uq'][l], QK_NOPE + QK_ROPE, _Q_PIECES)
    wukv = W['w_ukv'][l]
    p['wukv'] = jnp.concatenate([_pad_heads(wukv, QK_NOPE + V_HEAD, _K_PIECES),
                                 _pad_heads(wukv, QK_NOPE + V_HEAD, _V_PIECES)], axis=1)
    p['attn_g'] = _pad_gain(W['attn_out_g'][l])
    return p


def _forward_layer(h, memn_in, tabs, W, l, name, gather=None):
    p = _layer_params(W, l)
    sv = {'h0': h, 'p': p}
    xn = _rms_fwd(h, W['norm_mix_g'][l], name=name + "rms_mix")
    proj = _mm([(xn, p['win'])], 'nn', f32, name=name + "mm_in")
    cqn = _rms_fwd(proj, W['q_norm_g'][l], col0=0, width=Q_LORA, name=name + "rms_q")
    ckvn = _rms_fwd(proj, W['kv_norm_g'][l], col0=Q_LORA, width=KV_LORA, name=name + "rms_kv")
    q = _mm([(cqn, p['wuq'])], 'nn', f32, name=name + "mm_uq")
    kv = _mm([(ckvn, p['wukv'])], 'nn', f32, name=name + "mm_ukv")
    qh, kh, vh = _rope_fwd(q, kv, proj, tabs, name=name + "rope")
    oh, lse, carried = _attn_fwd(qh, kh, vh, plan=gather[0] if gather else None, name=name + "attn")
    if gather:
        for n, t in gather[1](carried).items():
            W[n][l] = t
    p['wout'] = _pack_wout(W['w_out'][l])
    an = _rms_fwd(oh, p['attn_g'], n_valid=MLA_WIDTH, name=name + "rms_attn")
    bm, cm, a = W['ssm'][l]
    bmb, cmb = bm.astype(bf16), cm.astype(bf16)
    y, xc = _ssm_fwd(proj, bmb, cmb, a, W['ssm_d'][l], name=name + "ssm")
    z, sn, ge = _glu_fwd(y, W['ssm_w_glu'][l], W['ssm_b_glu'][l], W['ssm_out_g'][l], name=name + "glu")
    h1a = _mm([(an, p['wout'][:MLA_PAD])], 'nn', f32, res=h, name=name + "mm_out_a")
    h1 = _mm([(sn, p['wout'][MLA_PAD:])], 'nn', f32, res=h1a, name=name + "mm_out_s")
    hn2 = _rms_fwd(h1, W['norm_x_g'][l], name=name + "rms_x")
    memn = _rms_fwd(memn_in, W['mem_norm_g'][l], name=name + "rms_mem")
    qx = _mm([(hn2, W['w_xq'][l])], 'nn', bf16, name=name + "mm_xq")
    kvx = _mm([(memn, W['w_xkv'][l])], 'nn', bf16, name=name + "mm_xkv")
    ox = _xattn_fwd(qx, kvx, name=name + "xattn")
    h2 = _mm([(ox, W['w_xo'][l])], 'nn', f32, res=h1, name=name + "mm_xo")
    hn3 = _rms_fwd(h2, W['norm_ffn_g'][l], name=name + "rms_ffn")
    gate, up, act = _ffn_up(hn3, W['w_gate'][l], W['w_up'][l], name=name + "ffn_up")
    h3 = _mm([(act, W['w_down'][l])], 'nn', f32, res=h2, name=name + "mm_down")
    sv.update(xn=xn, proj=proj, cqn=cqn, ckvn=ckvn, qh=qh, kh=kh, vh=vh, oh=oh, lse=lse, an=an, bmb=bmb, cmb=cmb,
              a=a, y=y, xc=xc, z=z, sn=sn, ge=ge, h1=h1, hn2=hn2, memn=memn, qx=qx, kvx=kvx, ox=ox, h2=h2, hn3=hn3,
              gate=gate, up=up, act=act)
    return h3, sv


def _backward_layer(dh3, sv, memn_in, tabs, W, l, hook, name):
    p = sv['p']
    G = {}
    G['w_down'] = _mm([(sv['act'], dh3)], 'tn', f32, name=name + "dw_down")
    dgate, dup = _ffn_bwd_act(dh3, W['w_down'][l], sv['gate'], sv['up'], name=name + "ffn_bwd_act")
    dhn3 = _mm([(dgate, W['w_gate'][l]), (dup, W['w_up'][l])], 'nn', f32, name=name + "mm_dffn")
    G['w_gate'] = _mm([(dgate, sv['hn3'])], 'tn', f32, name=name + "dw_gate")
    G['w_up'] = _mm([(dup, sv['hn3'])], 'tn', f32, name=name + "dw_up")
    xplan = hook['ffn'](l, G) if hook else None
    dh2, dg = _rms_bwd(sv['h2'], W['norm_ffn_g'][l], dhn3, res=dh3, name=name + "rmsb_ffn")
    G['norm_ffn_g'] = dg[0]
    G['w_xo'] = _mm([(sv['ox'], dh2)], 'tn', f32, name=name + "dw_xo")
    dox = _mm([(dh2, W['w_xo'][l])], 'nt', bf16, name=name + "mm_dxo")
    dqx, dkvx, carried = _xattn_bwd(sv['qx'], sv['kvx'], dox, plan=xplan, name=name + "xattn_bwd")
    if xplan:
        xplan['done'](carried)
    G['w_xq'] = _mm([(sv['hn2'], dqx)], 'tn', f32, name=name + "dw_xq")
    G['w_xkv'] = _mm([(sv['memn'], dkvx)], 'tn', f32, name=name + "dw_xkv")
    dhn2 = _mm([(dqx, W['w_xq'][l])], 'nt', f32, name=name + "mm_dxq")
    dmemn = _mm([(dkvx, W['w_xkv'][l])], 'nt', f32, name=name + "mm_dxkv")
    dh1, dg = _rms_bwd(sv['h1'], W['norm_x_g'][l], dhn2, res=dh2, name=name + "rmsb_x")
    G['norm_x_g'] = dg[0]
    _, dg = _rms_bwd(memn_in, W['mem_norm_g'][l], dmemn, name=name + "rmsb_mem")
    G['mem_norm_g'] = dg[0]
    dwo_a = _mm([(sv['an'], dh1)], 'tn', f32, name=name + "dw_out_a")
    dwo_s = _mm([(sv['sn'], dh1)], 'tn', f32, name=name + "dw_out_s")
    G['w_out'] = _unpack_wout(jnp.concatenate([dwo_a, dwo_s], axis=0))
    dmixed = _mm([(dh1, p['wout'])], 'nt', f32, name=name + "mm_dout")
    dy, dz, dg, db = _glu_bwd(sv['y'], sv['z'], dmixed, W['ssm_w_glu'][l], W['ssm_out_g'][l], name=name + "glu_bwd")
    G['ssm_out_g'], G['ssm_b_glu'] = dg[0], db[0]
    G['ssm_w_glu'] = _mm([(sv['ge'], dz)], 'tn', f32, name=name + "dw_glu")
    xplan = hook['mid'](l, G) if hook else None
    du, dbm, dcm, da, dd, carried = _ssm_bwd(sv['proj'], dy, sv['xc'], sv['bmb'], sv['cmb'], sv['a'], W['ssm_d'][l],
                                             plan=xplan, name=name + "ssm_bwd")
    if xplan:
        xplan['done'](carried)
    G['ssm_d'] = dd[0]
    G['ssm_raw'] = (dbm, dcm, da)
    doh, dg, delta = _rms_bwd(sv['oh'], p['attn_g'], dmixed, width=MLA_PAD, n_valid=MLA_WIDTH, delta=True,
                              out_dtype=bf16, name=name + "rmsb_attn")
    G['attn_out_g'] = _unpad_gain(dg[0])
    plans = hook['take']() if hook else []
    plan = _merge_plans(plans) if plans else None
    dqh, dkh, dvh, carried = _attn_bwd(sv['qh'], sv['kh'], sv['vh'], doh, sv['lse'], delta, plan=plan,
                                       name=name + "attn_bwd")
    if plan:
        plan['done'](carried)
    dq, dkv, dkr = _rope_bwd(dqh, dkh, dvh, tabs, name=name + "rope_bwd")
    G['w_uq'] = _unpad_heads(_mm([(sv['cqn'], dq)], 'tn', f32, name=name + "dw_uq"), QK_NOPE + QK_ROPE, _Q_PIECES)
    dwukv = _mm([(sv['ckvn'], dkv)], 'tn', f32, name=name + "dw_ukv")
    G['w_ukv'] = (_unpad_heads(dwukv[:, :MLA_PAD], QK_NOPE + V_HEAD, _K_PIECES)
                  + _unpad_heads(dwukv[:, MLA_PAD:], QK_NOPE + V_HEAD, _V_PIECES))
    dcqn = _mm([(dq, p['wuq'])], 'nt', f32, name=name + "mm_duq")
    dckvn = _mm([(dkv, p['wukv'])], 'nt', f32, name=name + "mm_dukv")
    dcq, dg = _rms_bwd(sv['proj'], W['q_norm_g'][l], dcqn, col0=0, width=Q_LORA, out_dtype=bf16, name=name + "rmsb_q")
    G['q_norm_g'] = dg[0]
    dckv, dg = _rms_bwd(sv['proj'], W['kv_norm_g'][l], dckvn, col0=Q_LORA, width=KV_LORA, out_dtype=bf16,
                        name=name + "rmsb_kv")
    G['kv_norm_g'] = dg[0]
    dproj = jnp.concatenate([dcq, dckv, dkr, du.astype(bf16)], axis=1)
    G['w_in'] = _unpack_win(_mm([(sv['xn'], dproj)], 'tn', f32, name=name + "dw_in"))
    dxn = _mm([(dproj, p['win'])], 'nt', f32, name=name + "mm_din")
    dh0, dg = _rms_bwd(sv['h0'], W['norm_mix_g'][l], dxn, res=dh1, name=name + "rmsb_mix")
    G['norm_mix_g'] = dg[0]
    return dh0, G


def _local_step(x, mem, positions, target, W, gathers=None, hook=None):
    tabs = _rope_tables(positions)
    ssm_in = [(W['ssm_lambda_re'][l], W['ssm_lambda_im'][l], W['ssm_log_dt'][l], W['ssm_b_re'][l], W['ssm_b_im'][l],
               W['ssm_c_re'][l], W['ssm_c_im'][l]) for l in range(DEPTH)]
    preps = [jax.vjp(_ssm_prep, *ssm_in[l]) for l in range(DEPTH)]
    W = dict(W)
    W['ssm'] = [preps[l][0] for l in range(DEPTH)]
    h = x
    saved = []
    for l in range(DEPTH):
        h, sv = _forward_layer(h, mem, tabs, W, l, f"l{l}_", gathers[l] if gathers else None)
        saved.append(sv)
    dh, dgf, loss = _loss_head(h, W['final_norm_g'], target)
    grads = [None] * DEPTH
    for l in reversed(range(DEPTH)):
        dh, G = _backward_layer(dh, saved[l], mem, tabs, W, l, hook, f"l{l}b_")
        dbm, dcm, da = G.pop('ssm_raw')
        names = ['ssm_lambda_re', 'ssm_lambda_im', 'ssm_log_dt', 'ssm_b_re', 'ssm_b_im', 'ssm_c_re', 'ssm_c_im']
        for n, g in zip(names, preps[l][1]((dbm, dcm, da))):
            G[n] = g
        grads[l] = G
        if hook is not None and l > 0:
            hook['rest'](l, G)
    out = {n: [grads[l][n] for l in range(DEPTH)] if n in SHARDED else jnp.stack([grads[l][n] for l in range(DEPTH)])
           for n in grads[0]}
    out['final_norm_g'] = dgf[0]
    return loss[0, 0], dh, out


_HBM = pl.BlockSpec(memory_space=pltpu.HBM)


def _me():
    return lax.axis_index("x"), lax.axis_index("y"), lax.axis_index("c")


def _chip_peers(x, y, c):
    devs = [(1 - x, y, c), (x, 1 - y, c), (1 - x, 1 - y, c)]
    return devs, [2 * d[0] + d[1] for d in devs]


def _gather_plan(xs, half_first):
    n = len(xs)
    if half_first:
        ins = [t.reshape(2, 1, *t.shape[1:]) for t in xs]
        outs = [jax.ShapeDtypeStruct((2, 4, *t.shape[1:]), t.dtype) for t in xs]
    else:
        ins = [t.reshape(1, 2, t.shape[0] // 2, t.shape[1]) for t in xs]
        outs = [jax.ShapeDtypeStruct((4, 2, t.shape[0] // 2, t.shape[1]), t.dtype) for t in xs]

    def own(ref, h):
        return ref.at[h] if half_first else ref.at[:, h]

    def slot(ref, h, j):
        return ref.at[h, pl.ds(j, 1)] if half_first else ref.at[pl.ds(j, 1), h]

    def copies(src, dst, send, recv):
        x, y, c = _me()
        jme = 2 * x + y
        devs, js = _chip_peers(x, y, c)
        half, other = pl.ds(c, 1), pl.ds(1 - c, 1)
        mk = pltpu.make_async_remote_copy
        for i in range(n):
            for k in range(3):
                out_cp = mk(own(src[i], half), slot(dst[i], half, jme), send.at[6 * i + k], recv.at[6 * i + k],
                            device_id=devs[k], device_id_type=MESH)
                in_cp = mk(own(src[i], half), slot(dst[i], half, js[k]), send.at[6 * i + k], recv.at[6 * i + k],
                           device_id=devs[k], device_id_type=MESH)
                pass_cp = mk(slot(dst[i], half, js[k]), slot(dst[i], half, js[k]), send.at[6 * i + 3 + k],
                             recv.at[6 * i + 3 + k], device_id=(x, y, 1 - c), device_id_type=MESH)
                got_cp = mk(slot(dst[i], other, js[k]), slot(dst[i], other, js[k]), send.at[6 * i + 3 + k],
                            recv.at[6 * i + 3 + k], device_id=(x, y, 1 - c), device_id_type=MESH)
                yield out_cp, in_cp, pass_cp, got_cp

    def start(*refs):
        for out_cp, _, _, _ in copies(*refs):
            out_cp.start()

    def forward(*refs):
        for _, in_cp, pass_cp, _ in copies(*refs):
            in_cp.wait_recv()
            pass_cp.start()

    def finish(*refs):
        for out_cp, _, pass_cp, got_cp in copies(*refs):
            got_cp.wait_recv()
            out_cp.wait_send()
            pass_cp.wait_send()

    return dict(n=n, ins=ins, outs=outs, nsem=6 * n, start=start, forward=forward, finish=finish)


def _plan_refs(plan, refs):
    n = plan['n']
    return refs[:n], refs[n:2 * n], refs[2 * n], refs[2 * n + 1]


def _run_plan(plan, name):
    n = plan['n']

    def body(*refs):
        args = _plan_refs(plan, refs)
        plan['start'](*args)
        plan['forward'](*args)
        plan['finish'](*args)

    return _pc(body, in_specs=[_HBM] * n, out_specs=[_HBM] * n, out_shape=plan['outs'],
               scratch_shapes=[pltpu.SemaphoreType.DMA((plan['nsem'],)), pltpu.SemaphoreType.DMA((plan['nsem'],))],
               compiler_params=pltpu.CompilerParams(has_side_effects=True), name=name)(*plan['ins'])


def _fill_own(gathered, own, half_first):
    jme = (2 * lax.axis_index("x") + lax.axis_index("y")).astype(jnp.int32)
    zero = jnp.int32(0)
    if half_first:
        return lax.dynamic_update_slice(gathered, own[:, None], (zero, jme, zero, zero))
    return lax.dynamic_update_slice(gathered, own.reshape(1, *gathered.shape[1:]), (jme, zero, zero, zero))


def _exchange_plan(gs, done=None):
    n = len(gs)

    def copies(src, dst, send, recv):
        x, y, c = _me()
        for i in range(n):
            yield pltpu.make_async_remote_copy(src[i].at[:, pl.ds(1 - c, 1)], dst[i], send.at[i], recv.at[i],
                                               device_id=(x, y, 1 - c), device_id_type=MESH)

    def start(*refs):
        for cp in copies(*refs):
            cp.start()

    def finish(*refs):
        for cp in copies(*refs):
            cp.wait()

    outs = [jax.ShapeDtypeStruct((4, 1, *g.shape[2:]), g.dtype) for g in gs]
    return dict(n=n, ins=list(gs), outs=outs, nsem=n, start=start, forward=lambda *refs: None, finish=finish, done=done)


def _plan_extras(plan):
    if not plan:
        return [], [], [], [], []
    anyspec = pl.BlockSpec(memory_space=pl.ANY)
    sems = [pltpu.SemaphoreType.DMA((plan['nsem'],)), pltpu.SemaphoreType.DMA((plan['nsem'],))]
    return [anyspec] * plan['n'], [anyspec] * plan['n'], list(plan['outs']), sems, list(plan['ins'])


def _scatter_plan(ps, done=None):
    n = len(ps)

    def copies(src, dst, send, recv, off):
        x, y, c = _me()
        devs, js = _chip_peers(x, y, c)
        for i in range(n):
            for k in range(3):
                yield pltpu.make_async_remote_copy(src[i].at[pl.ds(js[k], 1)], dst[i].at[k], send.at[off + 3 * i + k],
                                                   recv.at[off + 3 * i + k], device_id=devs[k], device_id_type=MESH)

    def start(src, dst, send, recv, off=0):
        for cp in copies(src, dst, send, recv, off):
            cp.start()

    def finish(src, dst, send, recv, off=0):
        for cp in copies(src, dst, send, recv, off):
            cp.wait()

    outs = [jax.ShapeDtypeStruct((3, 1, *p.shape[1:]), p.dtype) for p in ps]
    return dict(n=n, ins=list(ps), outs=outs, nsem=3 * n, start=start, forward=lambda *refs: None, finish=finish,
                done=done)


def _merge_plans(plans):
    def run(which):
        def f(src, dst, send, recv):
            o = s = 0
            for p in plans:
                p[which](src[o:o + p['n']], dst[o:o + p['n']], send, recv, off=s)
                o, s = o + p['n'], s + p['nsem']
        return f

    def done(results):
        o = 0
        for p in plans:
            p['done'](results[o:o + p['n']])
            o += p['n']

    return dict(n=sum(p['n'] for p in plans), ins=[t for p in plans for t in p['ins']],
                outs=[t for p in plans for t in p['outs']], nsem=sum(p['nsem'] for p in plans),
                start=run('start'), forward=lambda *refs: None, finish=run('finish'), done=done)


def _swap_sibling(hs, name):
    n = len(hs)

    def body(*refs):
        src, dst = refs[:n], refs[n:2 * n]
        send, recv = refs[2 * n:]
        x, y, c = _me()
        cps = []
        for i in range(n):
            cp = pltpu.make_async_remote_copy(src[i], dst[i], send.at[i], recv.at[i], device_id=(x, y, 1 - c),
                                              device_id_type=MESH)
            cp.start()
            cps.append(cp)
        for cp in cps:
            cp.wait()

    outs = [jax.ShapeDtypeStruct(h.shape, h.dtype) for h in hs]
    return _pc(body, in_specs=[_HBM] * n, out_specs=[_HBM] * n, out_shape=outs,
               scratch_shapes=[pltpu.SemaphoreType.DMA((n,)), pltpu.SemaphoreType.DMA((n,))],
               compiler_params=pltpu.CompilerParams(has_side_effects=True), name=name)(*hs)


ELEMWISE_VMEM_BUDGET = 24 * 1024 * 1024


def _row_tile(r, n, narrays):
    limit = ELEMWISE_VMEM_BUDGET // (2 * 4 * narrays * n)
    best = SUBLANES
    for t in range(16, r + 1, 16):
        if r % t == 0 and t <= limit:
            best = t
    return best


def _add_half(g, r1, cidx, name):
    _, _, r, n = g.shape
    tr = _row_tile(r, n, 3)

    def body(c_ref, g_ref, r_ref, o_ref):
        o_ref[...] = (g_ref[...] + r_ref[...]).astype(GRAD_TRANSIT)

    blk = lambda f: pl.BlockSpec((None, None, tr, n), f)
    gs = pltpu.PrefetchScalarGridSpec(
        num_scalar_prefetch=1, grid=(4, r // tr),
        in_specs=[blk(lambda j, i, c: (j, c[0], i, 0)), blk(lambda j, i, c: (j, 0, i, 0))],
        out_specs=pl.BlockSpec((None, tr, n), lambda j, i, c: (j, i, 0)))
    return _pc(body, grid_spec=gs, out_shape=jax.ShapeDtypeStruct((4, r, n), GRAD_TRANSIT),
               compiler_params=_cp(("parallel", "parallel")), name=name)(cidx, g, r1)


def _add_chips(p, r3, jidx, name):
    _, r, n = p.shape
    tr = _row_tile(r, n, 5)

    def body(j_ref, p_ref, a_ref, b_ref, c_ref, o_ref):
        o_ref[...] = ((p_ref[...].astype(f32) + a_ref[...].astype(f32)) + b_ref[...].astype(f32)) + c_ref[...].astype(f32)

    rblk = lambda k: pl.BlockSpec((None, None, tr, n), lambda i, j: (k, 0, i, 0))
    gs = pltpu.PrefetchScalarGridSpec(
        num_scalar_prefetch=1, grid=(r // tr,),
        in_specs=[pl.BlockSpec((None, tr, n), lambda i, j: (j[0], i, 0)), rblk(0), rblk(1), rblk(2)],
        out_specs=pl.BlockSpec((tr, n), lambda i, j: (i, 0)))
    return _pc(body, grid_spec=gs, out_shape=jax.ShapeDtypeStruct((r, n), f32),
               compiler_params=_cp(("parallel",)), name=name)(jidx, p, r3, r3, r3)


def _adamw_halves(w, mine, theirs, m, v, cidx, name):
    L, r, n = w.shape
    r2 = r // 2
    tr = _row_tile(r2, n, 11)
    c1 = 1.0 / (1.0 - ADAM_B1 ** ADAM_STEP)
    c2 = 1.0 / (1.0 - ADAM_B2 ** ADAM_STEP)

    def body(c_ref, w_ref, a0_ref, b0_ref, a1_ref, b1_ref, m_ref, v_ref, g_ref, d_ref, mo_ref, vo_ref):
        l, hf = pl.program_id(0), pl.program_id(1)
        own = hf == c_ref[0]
        gv = jnp.where(l == 0, jnp.where(own, a0_ref[...], b0_ref[...]), jnp.where(own, a1_ref[...], b1_ref[...]))
        m2 = ADAM_B1 * m_ref[...] + (1.0 - ADAM_B1) * gv
        v2 = ADAM_B2 * v_ref[...] + (1.0 - ADAM_B2) * (gv * gv)
        g_ref[...] = gv
        d_ref[...] = -ADAM_LR * ((m2 * c1) / (jnp.sqrt(v2 * c2) + ADAM_EPS) + ADAM_WD * w_ref[...])
        mo_ref[...] = m2
        vo_ref[...] = v2

    full = pl.BlockSpec((None, None, tr, n), lambda l, hf, i, c: (l, hf, i, 0))

    def half(layer, own):
        return pl.BlockSpec((tr, n), lambda l, hf, i, c: (jnp.where((l == layer) & ((hf == c[0]) == own), i, 0), 0))

    gs = pltpu.PrefetchScalarGridSpec(
        num_scalar_prefetch=1, grid=(L, 2, r2 // tr),
        in_specs=[full, half(0, True), half(0, False), half(1, True), half(1, False), full, full], out_specs=[full] * 4)
    four = lambda t: t.reshape(L, 2, r2, n)
    outs = _pc(body, grid_spec=gs, out_shape=[jax.ShapeDtypeStruct((L, 2, r2, n), f32)] * 4,
               compiler_params=_cp(("parallel", "parallel", "parallel")), name=name)(
        cidx, four(w), mine[0], theirs[0], mine[1], theirs[1], four(m), four(v))
    return [t.reshape(w.shape) for t in outs]


def _adamw_whole(w, g, m, v, name):
    c1 = 1.0 / (1.0 - ADAM_B1 ** ADAM_STEP)
    c2 = 1.0 / (1.0 - ADAM_B2 ** ADAM_STEP)

    def body(w_ref, g_ref, m_ref, v_ref, d_ref, mo_ref, vo_ref):
        gv = g_ref[...]
        m2 = ADAM_B1 * m_ref[...] + (1.0 - ADAM_B1) * gv
        v2 = ADAM_B2 * v_ref[...] + (1.0 - ADAM_B2) * (gv * gv)
        d_ref[...] = -ADAM_LR * ((m2 * c1) / (jnp.sqrt(v2 * c2) + ADAM_EPS) + ADAM_WD * w_ref[...])
        mo_ref[...] = m2
        vo_ref[...] = v2

    return _pc(body, out_shape=[jax.ShapeDtypeStruct(w.shape, f32)] * 3, name=name)(w, g, m, v)


def _full_from_gathered(name, t):
    r, n = 2 * t.shape[2], t.shape[3]
    if SHARDED[name] == 1 or name in TRANSPOSED:
        return t.reshape(4 * r, n)
    return t.reshape(4, r, n).transpose(1, 0, 2).reshape(r, 4 * n)


def _shard_major(name, g):
    R, C = g.shape
    if SHARDED[name] == 1 or name in TRANSPOSED:
        return g.reshape(4, 2, R // 8, C)
    return g.reshape(R, 4, C // 4).transpose(1, 0, 2).reshape(4, 2, R // 2, C // 4)


_SMALL_ROWS = 288


def _pack_small(d):
    flat = jnp.concatenate([d[n].reshape(-1) for n in SMALL])
    total = 2 * 4 * _SMALL_ROWS * LANES
    flat = jnp.concatenate([flat, jnp.zeros((total - flat.shape[0],), f32)])
    return flat.reshape(4, 2, _SMALL_ROWS, LANES)


def _unpack_small(t, like):
    flat = t.reshape(-1)
    out, off = {}, 0
    for n in SMALL:
        sz = math.prod(like[n].shape)
        out[n] = flat[off:off + sz].reshape(like[n].shape)
        off += sz
    return out


def kernel(x, mem, positions, norm_mix_g, w_in, q_norm_g, w_uq, kv_norm_g, w_ukv, ssm_lambda_re, ssm_lambda_im, ssm_log_dt, ssm_b_re, ssm_b_im, ssm_c_re, ssm_c_im, ssm_d, ssm_w_glu, ssm_b_glu, attn_out_g, ssm_out_g, w_out, norm_x_g, mem_norm_g, w_xq, w_xkv, w_xo, norm_ffn_g, w_gate, w_up, w_down, final_norm_g, loss_target, m_norm_mix_g, m_w_in, m_q_norm_g, m_w_uq, m_kv_norm_g, m_w_ukv, m_ssm_lambda_re, m_ssm_lambda_im, m_ssm_log_dt, m_ssm_b_re, m_ssm_b_im, m_ssm_c_re, m_ssm_c_im, m_ssm_d, m_ssm_w_glu, m_ssm_b_glu, m_attn_out_g, m_ssm_out_g, m_w_out, m_norm_x_g, m_mem_norm_g, m_w_xq, m_w_xkv, m_w_xo, m_norm_ffn_g, m_w_gate, m_w_up, m_w_down, m_final_norm_g, v_norm_mix_g, v_w_in, v_q_norm_g, v_w_uq, v_kv_norm_g, v_w_ukv, v_ssm_lambda_re, v_ssm_lambda_im, v_ssm_log_dt, v_ssm_b_re, v_ssm_b_im, v_ssm_c_re, v_ssm_c_im, v_ssm_d, v_ssm_w_glu, v_ssm_b_glu, v_attn_out_g, v_ssm_out_g, v_w_out, v_norm_x_g, v_mem_norm_g, v_w_xq, v_w_xkv, v_w_xo, v_norm_ffn_g, v_w_gate, v_w_up, v_w_down, v_final_norm_g):
    given = dict(locals())
    swap = lambda n, t: jnp.swapaxes(t, *TRANSPOSED[n]) if n in TRANSPOSED else t
    w = {n: swap(n, given[n]) for n in WEIGHTS}
    m = {n: swap(n, given["m_" + n]) for n in WEIGHTS}
    v = {n: swap(n, given["v_" + n]) for n in WEIGHTS}
    big = list(SHARDED)

    shards = {n: w[n].astype(bf16) for n in big}
    early = [n for n in big if n in EARLY_WEIGHTS]
    rest = [n for n in big if n not in EARLY_WEIGHTS]

    def full(names, results, l):
        return {n: _full_from_gathered(n, _fill_own(t, shards[n][l], False)) for n, t in zip(names, results)}

    first = _run_plan(_gather_plan([shards[n][l] for l in range(DEPTH) for n in early], False), "allgather_weights_early")
    W = {n: [None] * DEPTH for n in big}
    for l in range(DEPTH):
        for n, t in full(early, first[l * len(early):(l + 1) * len(early)], l).items():
            W[n][l] = t
    gathers = [(_gather_plan([shards[n][l] for n in rest], False), functools.partial(full, rest, l=l))
               for l in range(DEPTH)]
    W.update({n: w[n] for n in SMALL})

    cidx = lax.axis_index("c").astype(jnp.int32).reshape(1)
    jidx = (2 * lax.axis_index("x") + lax.axis_index("y")).astype(jnp.int32).reshape(1)
    ffn = [n for n in big if n in FFN_WEIGHTS]
    tail = [n for n in big if n in EARLY_WEIGHTS]
    mid = [n for n in big if n not in FFN_WEIGHTS and n not in EARLY_WEIGHTS]
    sums, got = {}, {}
    ready = []

    def exchange(names, tag, l, G, extra=()):
        keys = [(l, n) for n in names] + [(l, n) for n, _ in extra]
        gs = [_shard_major(n, G[n]) for n in names] + [g for _, g in extra]

        def done(r1):
            ps = [_add_half(g, r, cidx, f"grad_add_half_{tag}_{k[1]}") for k, g, r in zip(keys, gs, r1)]
            sums.update(zip(keys, ps))
            ready.append(_scatter_plan(ps, done=lambda results: got.update(zip(keys, results))))

        return _exchange_plan(gs, done)

    def take():
        plans = list(ready)
        ready.clear()
        return plans

    def now(plan, name):
        plan['done'](_run_plan(plan, name))

    hook = {'ffn': lambda l, G: exchange(ffn, f"l{l}_ffn", l, G), 'mid': lambda l, G: exchange(mid, f"l{l}_mid", l, G),
            'rest': lambda l, G: now(exchange(tail, f"l{l}_tail", l, G), f"grad_exchange_halves_l{l}_tail"),
            'take': take}
    loss, dx, grads = _local_step(x[0], mem[0], positions[0], loss_target[0], W, gathers=gathers, hook=hook)
    loss = lax.psum(loss, ("x", "y", "c"))

    now(exchange(tail, "l0_tail", 0, {n: grads[n][0] for n in tail}, extra=[("small", _pack_small(grads))]),
        "grad_exchange_halves_l0_tail")
    now(_merge_plans(take()), "grad_scatter_chips_l0_tail")
    keys = list(sums)
    hs = dict(zip(keys, [_add_chips(sums[k], got[k], jidx, f"grad_add_chips_l{k[0]}_{k[1]}") for k in keys]))
    ts = dict(zip(keys, _swap_sibling([hs[k] for k in keys], "grad_swap_sibling")))

    out_g, out_d, out_m, out_v = {}, {}, {}, {}
    for n in big:
        mine, theirs = [hs[(l, n)] for l in range(DEPTH)], [ts[(l, n)] for l in range(DEPTH)]
        out_g[n], out_d[n], out_m[n], out_v[n] = _adamw_halves(w[n], mine, theirs, m[n], v[n], cidx, f"adamw_{n}")
    both = jnp.stack([hs[(0, "small")], ts[(0, "small")]])
    piece = jnp.where(cidx[0] == 0, both, both[::-1]).reshape(2 * _SMALL_ROWS, LANES)
    gsm = _fill_own(_run_plan(_gather_plan([piece], False), "allgather_small")[0], piece, False)
    out_g.update(_unpack_small(gsm, w))
    for n in SMALL:
        two = lambda t: t.reshape(1, -1) if t.ndim == 1 else t
        d_, m_, v_ = _adamw_whole(two(w[n]), two(out_g[n]), two(m[n]), two(v[n]), f"adamw_{n}")
        out_d[n], out_m[n], out_v[n] = (t.reshape(w[n].shape) for t in (d_, m_, v_))

    outs = [[swap(n, d[n]) for n in WEIGHTS] for d in (out_g, out_d, out_m, out_v)]
    return (loss, dx.reshape(x.shape), *outs[0], *outs[1], *outs[2], *outs[3])
```

```python
import functools
import math

import jax
import jax.numpy as jnp
from jax import lax
from jax.experimental import pallas as pl
from jax.experimental.pallas import tpu as pltpu

f32, bf16 = jnp.float32, jnp.bfloat16

D_MODEL = 1024
DEPTH = 2
MLA_HEADS = 8
QK_NOPE = 64
QK_ROPE = 32
V_HEAD = 64
Q_LORA = 256
KV_LORA = 128
MLA_WIDTH = MLA_HEADS * V_HEAD
ROPE_THETA = 10000.0
SSM_WIDTH = 512
SSM_GROUP = 16
SSM_GROUPS = 32
SSM_STATE = 64
IN_WIDTH = Q_LORA + KV_LORA + QK_ROPE + SSM_WIDTH
X_HEADS = 4
X_HEAD_DIM = D_MODEL // X_HEADS
D_FF = 2816
EPS = 1e-6
ADAM_LR, ADAM_B1, ADAM_B2, ADAM_EPS, ADAM_WD, ADAM_STEP = 0.001, 0.9, 0.999, 1e-08, 0.01, 10

LANES = 128
SUBLANES = 8
HEAD_PAD = 128
MLA_PAD = MLA_HEADS * HEAD_PAD
SSM_MACRO = 4
MACRO_CH = SSM_WIDTH // SSM_MACRO
MACRO_ST = SSM_GROUPS // SSM_MACRO * SSM_STATE
VMEM_LIMIT = 56 * 1024 * 1024
GRAD_TRANSIT = bf16

WEIGHTS = ['norm_mix_g', 'w_in', 'q_norm_g', 'w_uq', 'kv_norm_g', 'w_ukv', 'ssm_lambda_re', 'ssm_lambda_im',
           'ssm_log_dt', 'ssm_b_re', 'ssm_b_im', 'ssm_c_re', 'ssm_c_im', 'ssm_d', 'ssm_w_glu', 'ssm_b_glu',
           'attn_out_g', 'ssm_out_g', 'w_out', 'norm_x_g', 'mem_norm_g', 'w_xq', 'w_xkv', 'w_xo', 'norm_ffn_g',
           'w_gate', 'w_up', 'w_down', 'final_norm_g']
SHARDED = {'w_in': 1, 'w_uq': 2, 'w_ukv': 2, 'ssm_w_glu': 1, 'w_out': 1, 'w_xq': 1, 'w_xkv': 2, 'w_xo': 1,
           'w_gate': 2, 'w_up': 2, 'w_down': 1}
SMALL = [n for n in WEIGHTS if n not in SHARDED]
EARLY_WEIGHTS = ('w_in', 'w_uq', 'w_ukv')
FFN_WEIGHTS = ('w_gate', 'w_up', 'w_down')
TRANSPOSED = {'w_gate': (1, 2), 'w_up': (1, 2), 'ssm_b_re': (2, 3), 'ssm_b_im': (2, 3)}
MESH = pl.DeviceIdType.MESH


def _pc(body, **kw):
    return pl.pallas_call(body, **kw)


def _pick(n, prefs):
    for p in prefs:
        if n % p == 0:
            return p
    return n


def _cp(sem=None):
    return pltpu.CompilerParams(dimension_semantics=sem, vmem_limit_bytes=VMEM_LIMIT)


_TILE_CANDS = (1024, 1408, 512, 256, 128)
MM_VMEM_BUDGET = 40 * 1024 * 1024


def _mm_tiles(M, K, N, a_bytes, b_bytes, o_bytes, npair, has_res, need_acc):
    best = None
    for tm in _TILE_CANDS:
        for tk in _TILE_CANDS:
            if M % tm or K % tk:
                continue
            vm = npair * (2 * tm * tk * a_bytes + 2 * tk * N * b_bytes) + 2 * tm * N * o_bytes
            vm += tm * N * 4 * (1 + need_acc + 2 * has_res)
            if a_bytes == 4:
                vm += npair * tm * tk * 2
            if b_bytes == 4:
                vm += npair * tk * N * 2
            if vm <= MM_VMEM_BUDGET and (best is None or tm * tk > best[0]):
                best = (tm * tk, tm, tk)
    if best is None:
        return _pick(M, (256, 128)), _pick(K, (256, 128))
    return best[1], best[2]


def _mm(pairs, mode, out_dtype, res=None, name="mm"):
    a0, b0 = pairs[0]
    if mode == 'nn':
        (M, K), N = a0.shape, b0.shape[1]
        dims = (((1,), (0,)), ((), ()))
    elif mode == 'nt':
        (M, K), N = a0.shape, b0.shape[0]
        dims = (((1,), (1,)), ((), ()))
    else:
        (K, M), N = a0.shape, b0.shape[1]
        dims = (((0,), (0,)), ((), ()))
    npair = len(pairs)
    has_res = res is not None
    direct = out_dtype == f32
    tm, tk = _mm_tiles(M, K, N, a0.dtype.itemsize, b0.dtype.itemsize, jnp.dtype(out_dtype).itemsize, npair, has_res,
                       not direct)
    nk = K // tk

    def body(*refs):
        ins = refs[:2 * npair]
        res_ref = refs[2 * npair] if has_res else None
        o_ref = refs[2 * npair + has_res]
        acc = o_ref if direct else refs[2 * npair + has_res + 1]
        k = pl.program_id(1)
        s = None
        for p in range(npair):
            d = lax.dot_general(ins[2 * p][...].astype(bf16), ins[2 * p + 1][...].astype(bf16), dims,
                                preferred_element_type=f32)
            s = d if s is None else s + d

        @pl.when(k == 0)
        def _():
            acc[...] = s

        @pl.when(k > 0)
        def _():
            acc[...] += s

        if has_res or not direct:
            @pl.when(k == nk - 1)
            def _():
                r = acc[...]
                if has_res:
                    r = r + res_ref[...]
                o_ref[...] = r.astype(out_dtype)

    if mode == 'nn':
        a_spec = pl.BlockSpec((tm, tk), lambda i, k: (i, k))
        b_spec = pl.BlockSpec((tk, N), lambda i, k: (k, 0))
    elif mode == 'nt':
        a_spec = pl.BlockSpec((tm, tk), lambda i, k: (i, k))
        b_spec = pl.BlockSpec((N, tk), lambda i, k: (0, k))
    else:
        a_spec = pl.BlockSpec((tk, tm), lambda i, k: (k, i))
        b_spec = pl.BlockSpec((tk, N), lambda i, k: (k, 0))
    o_spec = pl.BlockSpec((tm, N), lambda i, k: (i, 0))
    in_specs = [a_spec, b_spec] * npair + ([o_spec] if has_res else [])
    args = [t for p in pairs for t in p] + ([res] if has_res else [])
    return _pc(body, grid=(M // tm, nk), in_specs=in_specs, out_specs=o_spec,
               out_shape=jax.ShapeDtypeStruct((M, N), out_dtype),
               scratch_shapes=[] if direct else [pltpu.VMEM((tm, N), f32)],
               compiler_params=_cp(("parallel", "arbitrary")), name=name)(*args)


def _rms_fwd(x, g, *, col0=0, width=None, n_valid=None, out_dtype=bf16, name="rms_fwd"):
    S = x.shape[0]
    width = width or x.shape[1]
    n_valid = n_valid or width
    ts = _pick(S, (512, 256, 128))
    cb = col0 // width

    def body(x_ref, g_ref, o_ref):
        xv = x_ref[...]
        ms = jnp.sum(xv * xv, axis=-1, keepdims=True) * (1.0 / n_valid)
        o_ref[...] = (xv * lax.rsqrt(ms + EPS) * g_ref[...]).astype(out_dtype)

    return _pc(body, grid=(S // ts,),
               in_specs=[pl.BlockSpec((ts, width), lambda i: (i, cb)), pl.BlockSpec((1, width), lambda i: (0, 0))],
               out_specs=pl.BlockSpec((ts, width), lambda i: (i, 0)),
               out_shape=jax.ShapeDtypeStruct((S, width), out_dtype),
               compiler_params=_cp(("parallel",)), name=name)(x, g.reshape(1, width))


def _rms_bwd(x, g, dy, *, col0=0, dcol0=0, width=None, n_valid=None, res=None, out_dtype=f32, delta=False,
             name="rms_bwd"):
    S = x.shape[0]
    width = width or x.shape[1]
    n_valid = n_valid or width
    ts = _pick(S, (512, 256, 128))
    cb, dcb = col0 // width, dcol0 // width
    has_res = res is not None

    def body(*refs):
        x_ref, g_ref, dy_ref = refs[:3]
        res_ref = refs[3] if has_res else None
        outs = refs[3 + has_res:]
        dx_ref, dg_ref = outs[0], outs[1]
        i = pl.program_id(0)
        xv = x_ref[...]
        gv = g_ref[...]
        dyv = dy_ref[...].astype(f32)
        rstd = lax.rsqrt(jnp.sum(xv * xv, axis=-1, keepdims=True) * (1.0 / n_valid) + EPS)
        xh = xv * rstd
        dxh = dyv * gv
        mean = jnp.sum(dxh * xh, axis=-1, keepdims=True) * (1.0 / n_valid)
        dx = rstd * (dxh - xh * mean)
        if delta:
            d_ref = outs[2]
            for h in range(width // LANES):
                sl = slice(h * LANES, (h + 1) * LANES)
                dsum = jnp.sum(dx[:, sl] * xv[:, sl], axis=-1, keepdims=True)
                d_ref[:, sl] = jnp.broadcast_to(dsum, (ts, LANES))
        if has_res:
            dx = dx + res_ref[...]
        dx_ref[...] = dx.astype(out_dtype)

        @pl.when(i == 0)
        def _():
            dg_ref[...] = jnp.zeros_like(dg_ref)

        dg_ref[...] += jnp.sum(dyv * xh, axis=0, keepdims=True)

    blk = lambda c: pl.BlockSpec((ts, width), lambda i: (i, c))
    in_specs = [blk(cb), pl.BlockSpec((1, width), lambda i: (0, 0)), blk(dcb)] + ([blk(0)] if has_res else [])
    out_specs = [blk(0), pl.BlockSpec((1, width), lambda i: (0, 0))] + ([blk(0)] if delta else [])
    out_shape = [jax.ShapeDtypeStruct((S, width), out_dtype), jax.ShapeDtypeStruct((1, width), f32)] + (
        [jax.ShapeDtypeStruct((S, width), f32)] if delta else [])
    args = [x, g.reshape(1, width), dy] + ([res] if has_res else [])
    return _pc(body, grid=(S // ts,), in_specs=in_specs, out_specs=out_specs, out_shape=out_shape,
               compiler_params=_cp(("arbitrary",)), name=name)(*args)


def _loss_head(h, g, target, name="loss_head"):
    S, D = h.shape
    ts = _pick(S, (512, 256, 128))

    def body(h_ref, g_ref, t_ref, dh_ref, dg_ref, loss_ref):
        i = pl.program_id(0)
        xv = h_ref[...]
        gv = g_ref[...]
        rstd = lax.rsqrt(jnp.sum(xv * xv, axis=-1, keepdims=True) * (1.0 / D) + EPS)
        xh = xv * rstd
        err = xh * gv - t_ref[...]
        dyv = err * (1.0 / D)
        dxh = dyv * gv
        mean = jnp.sum(dxh * xh, axis=-1, keepdims=True) * (1.0 / D)
        dh_ref[...] = rstd * (dxh - xh * mean)

        @pl.when(i == 0)
        def _():
            dg_ref[...] = jnp.zeros_like(dg_ref)
            loss_ref[...] = jnp.zeros_like(loss_ref)

        dg_ref[...] += jnp.sum(dyv * xh, axis=0, keepdims=True)
        part = jnp.sum(jnp.sum(err * err, axis=-1, keepdims=True), axis=0, keepdims=True) * (0.5 / D)
        loss_ref[...] += jnp.broadcast_to(part, (1, LANES))

    blk = pl.BlockSpec((ts, D), lambda i: (i, 0))
    row = pl.BlockSpec((1, D), lambda i: (0, 0))
    return _pc(body, grid=(S // ts,), in_specs=[blk, row, blk],
               out_specs=[blk, row, pl.BlockSpec((1, LANES), lambda i: (0, 0))],
               out_shape=[jax.ShapeDtypeStruct((S, D), f32), jax.ShapeDtypeStruct((1, D), f32),
                          jax.ShapeDtypeStruct((1, LANES), f32)],
               compiler_params=_cp(("arbitrary",)), name=name)(h, g.reshape(1, D), target)


def _rope_apply(x, tc, s1, s2):
    return x * tc + pltpu.roll(x, LANES - 16, 1) * s1 + pltpu.roll(x, 16, 1) * s2


def _rope_apply_t(dy, tc, s1, s2):
    return dy * tc + pltpu.roll(dy * s1, 16, 1) + pltpu.roll(dy * s2, LANES - 16, 1)


def _rope_fwd(q, kv, proj, tabs, name="rope_fwd"):
    S = q.shape[0]
    ts = _pick(S, (512, 256, 128))
    scale = (QK_NOPE + QK_ROPE) ** -0.5

    def body(q_ref, kk_ref, kvv_ref, kr_ref, tc_ref, s1_ref, s2_ref, qh_ref, kh_ref, vh_ref):
        tc, s1, s2 = tc_ref[...], s1_ref[...], s2_ref[...]
        krr = _rope_apply(pltpu.roll(kr_ref[...], QK_NOPE, 1), tc, s1, s2)
        for h in range(MLA_HEADS):
            sl = slice(h * HEAD_PAD, (h + 1) * HEAD_PAD)
            qh_ref[:, sl] = (_rope_apply(q_ref[:, sl], tc, s1, s2) * scale).astype(bf16)
            kh_ref[:, sl] = (kk_ref[:, sl] + krr).astype(bf16)
        vh_ref[...] = kvv_ref[...].astype(bf16)

    wide = lambda c: pl.BlockSpec((ts, MLA_PAD), lambda i: (i, c))
    tab = pl.BlockSpec((ts, LANES), lambda i: (i, 0))
    return _pc(body, grid=(S // ts,),
               in_specs=[wide(0), wide(0), wide(1), pl.BlockSpec((ts, LANES), lambda i: (i, 3)), tab, tab, tab],
               out_specs=[wide(0)] * 3, out_shape=[jax.ShapeDtypeStruct((S, MLA_PAD), bf16)] * 3,
               compiler_params=_cp(("parallel",)), name=name)(q, kv, kv, proj, *tabs)


def _rope_bwd(dqh, dkh, dvh, tabs, name="rope_bwd"):
    S = dqh.shape[0]
    ts = _pick(S, (512, 256, 128))
    scale = (QK_NOPE + QK_ROPE) ** -0.5

    def body(dq_ref, dk_ref, dv_ref, tc_ref, s1_ref, s2_ref, oq_ref, okv_ref, okr_ref):
        tc, s1, s2 = tc_ref[...], s1_ref[...], s2_ref[...]
        ksum = None
        for h in range(MLA_HEADS):
            sl = slice(h * HEAD_PAD, (h + 1) * HEAD_PAD)
            oq_ref[:, sl] = (_rope_apply_t(dq_ref[:, sl], tc, s1, s2) * scale).astype(bf16)
            dk = dk_ref[:, sl]
            okv_ref[:, sl] = dk.astype(bf16)
            ksum = dk if ksum is None else ksum + dk
        okv_ref[:, MLA_PAD:] = dv_ref[...].astype(bf16)
        dkr = pltpu.roll(_rope_apply_t(ksum, tc, s1, s2), LANES - QK_NOPE, 1)
        lane = lax.broadcasted_iota(jnp.int32, (ts, LANES), 1)
        okr_ref[...] = jnp.where(lane < QK_ROPE, dkr, 0.0).astype(bf16)

    wide = pl.BlockSpec((ts, MLA_PAD), lambda i: (i, 0))
    tab = pl.BlockSpec((ts, LANES), lambda i: (i, 0))
    return _pc(body, grid=(S // ts,), in_specs=[wide, wide, wide, tab, tab, tab],
               out_specs=[wide, pl.BlockSpec((ts, 2 * MLA_PAD), lambda i: (i, 0)), tab],
               out_shape=[jax.ShapeDtypeStruct((S, MLA_PAD), bf16), jax.ShapeDtypeStruct((S, 2 * MLA_PAD), bf16),
                          jax.ShapeDtypeStruct((S, LANES), bf16)],
               compiler_params=_cp(("parallel",)), name=name)(dqh, dkh, dvh, *tabs)


ATT_BLK = 1024
_DIAG_QUARTERS = ((0, 0), (1, 0), (1, 1))


def _attn_fwd(qh, kh, vh, plan=None, name="attn_fwd"):
    S = qh.shape[0]
    tq = tk = min(S, ATT_BLK)
    nq, nk = S // tq, S // tk
    npl = plan['n'] if plan else 0

    def body(*refs):
        q_ref, k_ref, v_ref = refs[:3]
        o_ref, lse_ref = refs[3 + npl:5 + npl]
        m_sc, l_sc, acc_sc = refs[5 + 2 * npl:8 + 2 * npl]
        h, i, j = pl.program_id(0), pl.program_id(1), pl.program_id(2)
        if plan:
            pargs = (refs[3:3 + npl], refs[5 + npl:5 + 2 * npl], refs[8 + 2 * npl], refs[9 + 2 * npl])
            first = (i == 0) & (j == 0)
            pl.when((h == 0) & first)(functools.partial(plan['start'], *pargs))
            pl.when((h == (3 * MLA_HEADS) // 4) & first)(functools.partial(plan['forward'], *pargs))
            pl.when((h == MLA_HEADS - 1) & (i == nq - 1) & (j == nk - 1))(functools.partial(plan['finish'], *pargs))

        @pl.when(j == 0)
        def _():
            m_sc[...] = jnp.full_like(m_sc, -1e30)
            l_sc[...] = jnp.zeros_like(l_sc)
            acc_sc[...] = jnp.zeros_like(acc_sc)

        def part(rows, cols, n, masked):
            s = lax.dot_general(q_ref[rows, :], k_ref[cols, :], (((1,), (1,)), ((), ())), preferred_element_type=f32)
            if masked:
                row = lax.broadcasted_iota(jnp.int32, (n, n), 0)
                col = lax.broadcasted_iota(jnp.int32, (n, n), 1)
                s = jnp.where(col <= row, s, -1e30)
            m_prev = m_sc[rows, :]
            m_new = jnp.maximum(m_prev, jnp.max(s, axis=-1, keepdims=True))
            alpha = jnp.exp(m_prev - m_new)
            p = jnp.exp(s - m_new)
            l_sc[rows, :] = alpha * l_sc[rows, :] + jnp.sum(p, axis=-1, keepdims=True)
            acc_sc[rows, :] = alpha * acc_sc[rows, :] + jnp.dot(p.astype(bf16), v_ref[cols, :],
                                                                  preferred_element_type=f32)
            m_sc[rows, :] = m_new

        whole = (slice(0, tq), slice(0, tk), tq)
        pl.when(j < i)(functools.partial(part, *whole, False))
        pl.when(j == i)(functools.partial(part, *whole, True))

        @pl.when(j == nk - 1)
        def _():
            l = l_sc[...]
            o_ref[...] = acc_sc[...] / l
            lse_ref[...] = jnp.broadcast_to(m_sc[...] + jnp.log(l), (tq, LANES))

    qspec = pl.BlockSpec((tq, HEAD_PAD), lambda h, i, j: (i, h))
    kspec = pl.BlockSpec((tk, HEAD_PAD), lambda h, i, j: (jnp.minimum(j, i), h))
    anyspec = pl.BlockSpec(memory_space=pl.ANY)
    scratch = [pltpu.VMEM((tq, 1), f32), pltpu.VMEM((tq, 1), f32), pltpu.VMEM((tq, HEAD_PAD), f32)]
    if plan:
        scratch += [pltpu.SemaphoreType.DMA((plan['nsem'],)), pltpu.SemaphoreType.DMA((plan['nsem'],))]
    outs = _pc(body, grid=(MLA_HEADS, nq, nk), in_specs=[qspec, kspec, kspec] + [anyspec] * npl,
               out_specs=[qspec, qspec] + [anyspec] * npl,
               out_shape=[jax.ShapeDtypeStruct((S, MLA_PAD), f32)] * 2 + (plan['outs'] if plan else []),
               scratch_shapes=scratch,
               compiler_params=_cp(("arbitrary", "arbitrary", "arbitrary") if plan else ("parallel", "parallel", "arbitrary")),
               name=name)(qh, kh, vh, *(plan['ins'] if plan else []))
    return outs[0], outs[1], outs[2:]


def _attn_bwd(qh, kh, vh, do, lse, delta, plan=None, name="attn_bwd"):
    S = qh.shape[0]
    tq = tk = min(S, ATT_BLK)
    nq, nk = S // tq, S // tk
    npl = plan['n'] if plan else 0

    def body(*refs):
        q_ref, k_ref, v_ref, do_ref, lse_ref, dl_ref = refs[:6]
        dq_ref, dk_ref, dv_ref = refs[6 + npl:9 + npl]
        h, j, i = pl.program_id(0), pl.program_id(1), pl.program_id(2)
        if plan:
            pargs = (refs[6:6 + npl], refs[9 + npl:9 + 2 * npl], refs[9 + 2 * npl], refs[10 + 2 * npl])
            pl.when((h == 0) & (j == 0) & (i == 0))(functools.partial(plan['start'], *pargs))
            pl.when((h == MLA_HEADS - 1) & (j == nk - 1) & (i == nq - 1))(functools.partial(plan['finish'], *pargs))

        @pl.when((j == 0) & (i == 0))
        def _():
            dq_ref[...] = jnp.zeros_like(dq_ref)

        @pl.when(i == 0)
        def _():
            dk_ref[...] = jnp.zeros_like(dk_ref)
            dv_ref[...] = jnp.zeros_like(dv_ref)

        def part(r0, c0, n, masked):
            nt = (((1,), (1,)), ((), ()))
            tn = (((0,), (0,)), ((), ()))
            rows, cols = slice(r0, r0 + n), slice(c0, c0 + n)
            qv, kv_, dov = q_ref[rows, :], k_ref[cols, :], do_ref[rows, :]
            s = lax.dot_general(qv, kv_, nt, preferred_element_type=f32)
            p = jnp.exp(s - lse_ref[rows, :1])
            if masked:
                row = lax.broadcasted_iota(jnp.int32, (n, n), 0)
                col = lax.broadcasted_iota(jnp.int32, (n, n), 1)
                p = jnp.where(col <= row, p, 0.0)
            dp = lax.dot_general(dov, v_ref[cols, :], nt, preferred_element_type=f32)
            ds = (p * (dp - dl_ref[rows, :1])).astype(bf16)
            dv_ref[cols, :] += lax.dot_general(p.astype(bf16), dov, tn, preferred_element_type=f32)
            dk_ref[cols, :] += lax.dot_general(ds, qv, tn, preferred_element_type=f32)
            qrows = pl.ds(pl.multiple_of(i * tq + r0, n), n)
            dq_ref[qrows, :] += jnp.dot(ds, kv_, preferred_element_type=f32)

        def below():
            part(0, 0, tq, False)

        def diagonal():
            for r0, c0 in _DIAG_QUARTERS:
                part(r0 * hq, c0 * hq, hq, r0 == c0)

        hq = tq // 2
        pl.when(i > j)(below)
        pl.when(i == j)(diagonal)

    qspec = pl.BlockSpec((tq, HEAD_PAD), lambda h, j, i: (jnp.maximum(i, j), h))
    kspec = pl.BlockSpec((tk, HEAD_PAD), lambda h, j, i: (j, h))
    colspec = pl.BlockSpec((S, HEAD_PAD), lambda h, j, i: (0, h))
    anyspec = pl.BlockSpec(memory_space=pl.ANY)
    scratch = [pltpu.SemaphoreType.DMA((plan['nsem'],)), pltpu.SemaphoreType.DMA((plan['nsem'],))] if plan else []
    outs = _pc(body, grid=(MLA_HEADS, nk, nq), in_specs=[qspec, kspec, kspec, qspec, qspec, qspec] + [anyspec] * npl,
               out_specs=[colspec, kspec, kspec] + [anyspec] * npl,
               out_shape=[jax.ShapeDtypeStruct((S, MLA_PAD), f32)] * 3 + (plan['outs'] if plan else []),
               scratch_shapes=scratch,
               compiler_params=_cp(("arbitrary" if plan else "parallel", "arbitrary", "arbitrary")),
               name=name)(qh, kh, vh, do, lse, delta, *(plan['ins'] if plan else []))
    return outs[0], outs[1], outs[2], outs[3:]


def _xattn_fwd(q, kv, name="xattn_fwd"):
    S = q.shape[0]
    M = kv.shape[0]
    tq = _pick(S, (1024, 512, 256, 128))
    scale = X_HEAD_DIM ** -0.5

    def body(q_ref, kv_ref, o_ref):
        for h in range(X_HEADS):
            sl = slice(h * X_HEAD_DIM, (h + 1) * X_HEAD_DIM)
            k = kv_ref[:, sl]
            v = kv_ref[:, D_MODEL + h * X_HEAD_DIM:D_MODEL + (h + 1) * X_HEAD_DIM]
            s = lax.dot_general(q_ref[:, sl], k, (((1,), (1,)), ((), ())), preferred_element_type=f32) * scale
            e = jnp.exp(s - jnp.max(s, axis=-1, keepdims=True))
            p = e / jnp.sum(e, axis=-1, keepdims=True)
            o_ref[:, sl] = jnp.dot(p.astype(bf16), v, preferred_element_type=f32).astype(bf16)

    blk = pl.BlockSpec((tq, D_MODEL), lambda i: (i, 0))
    return _pc(body, grid=(S // tq,), in_specs=[blk, pl.BlockSpec((M, 2 * D_MODEL), lambda i: (0, 0))],
               out_specs=blk, out_shape=jax.ShapeDtypeStruct((S, D_MODEL), bf16),
               compiler_params=_cp(("parallel",)), name=name)(q, kv)


def _xattn_bwd(q, kv, do, plan=None, name="xattn_bwd"):
    S = q.shape[0]
    M = kv.shape[0]
    tq = _pick(S, (1024, 512, 256, 128))
    scale = X_HEAD_DIM ** -0.5
    npl = plan['n'] if plan else 0
    p_in, p_out, p_shapes, p_sems, p_args = _plan_extras(plan)

    def body(*refs):
        q_ref, kv_ref, do_ref = refs[:3]
        dq_ref, dkv_ref = refs[3 + npl:5 + npl]
        i = pl.program_id(0)
        if plan:
            pargs = (refs[3:3 + npl], refs[5 + npl:5 + 2 * npl], refs[5 + 2 * npl], refs[6 + 2 * npl])
            pl.when(i == 0)(functools.partial(plan['start'], *pargs))
            pl.when(i == S // tq - 1)(functools.partial(plan['finish'], *pargs))

        @pl.when(i == 0)
        def _():
            dkv_ref[...] = jnp.zeros_like(dkv_ref)

        nt = (((1,), (1,)), ((), ()))
        tn = (((0,), (0,)), ((), ()))
        for h in range(X_HEADS):
            sl = slice(h * X_HEAD_DIM, (h + 1) * X_HEAD_DIM)
            vsl = slice(D_MODEL + h * X_HEAD_DIM, D_MODEL + (h + 1) * X_HEAD_DIM)
            k, v, qv, dov = kv_ref[:, sl], kv_ref[:, vsl], q_ref[:, sl], do_ref[:, sl]
            s = lax.dot_general(qv, k, nt, preferred_element_type=f32) * scale
            e = jnp.exp(s - jnp.max(s, axis=-1, keepdims=True))
            p = e / jnp.sum(e, axis=-1, keepdims=True)
            dp = lax.dot_general(dov, v, nt, preferred_element_type=f32)
            ds = (p * (dp - jnp.sum(dp * p, axis=-1, keepdims=True)) * scale).astype(bf16)
            dq_ref[:, sl] = jnp.dot(ds, k, preferred_element_type=f32).astype(bf16)
            dkv_ref[:, sl] += lax.dot_general(ds, qv, tn, preferred_element_type=f32)
            dkv_ref[:, vsl] += lax.dot_general(p.astype(bf16), dov, tn, preferred_element_type=f32)

    blk = pl.BlockSpec((tq, D_MODEL), lambda i: (i, 0))
    full = pl.BlockSpec((M, 2 * D_MODEL), lambda i: (0, 0))
    outs = _pc(body, grid=(S // tq,), in_specs=[blk, full, blk] + p_in, out_specs=[blk, full] + p_out,
               out_shape=[jax.ShapeDtypeStruct((S, D_MODEL), bf16), jax.ShapeDtypeStruct((M, 2 * D_MODEL), f32)] + p_shapes,
               scratch_shapes=p_sems, compiler_params=_cp(("arbitrary",)), name=name)(q, kv, do, *p_args)
    return outs[0], outs[1], outs[2:]


ST_CHUNKS = 1


def _apow_init(a_ref, ap_ref, bp_ref, seg):
    P = MACRO_ST
    ar, ai = a_ref[:, :P], a_ref[:, P:]
    pr, pi = ar, ai
    for r in range(seg):
        ap_ref[r:r + 1, :P] = pr
        ap_ref[r:r + 1, P:] = pi
        if r < seg - 1:
            pr, pi = pr * ar - pi * ai, pr * ai + pi * ar
    br, bi = pr, pi
    for k in range(SUBLANES):
        bp_ref[k:k + 1, :P] = pr
        bp_ref[k:k + 1, P:] = pi
        pr, pi = pr * br - pi * bi, pr * bi + pi * br


def _segment_perm(tS):
    seg = tS // SUBLANES
    rows = jnp.arange(tS)
    src = (rows % SUBLANES) * seg + rows // SUBLANES
    return (src[:, None] == jnp.arange(tS)[None, :]).astype(f32)


def _unpermute_rows(pt, v):
    hi = v.astype(bf16)
    r1 = v - hi.astype(f32)
    mid = r1.astype(bf16)
    lo = (r1 - mid.astype(f32)).astype(bf16)
    out = jnp.dot(pt, jnp.concatenate([hi, mid, lo], axis=1), preferred_element_type=f32)
    w = v.shape[1]
    return (out[:, :w] + out[:, w:2 * w]) + out[:, 2 * w:]


def _scan_block(sc_ref, ap_ref, bp_ref, carry_ref, e_ref, seg, reverse):
    P = MACRO_ST
    sgn = -1.0 if reverse else 1.0
    CH = P // ST_CHUNKS
    rid = lax.broadcasted_iota(jnp.int32, (SUBLANES, CH), 0)
    for c in range(ST_CHUNKS):
        lr, li = slice(c * CH, (c + 1) * CH), slice(P + c * CH, P + (c + 1) * CH)
        ar, ai = ap_ref[0:1, lr], sgn * ap_ref[0:1, li]
        xr = xi = None
        for i in range(seg):
            r = seg - 1 - i if reverse else i
            rows = slice(SUBLANES * r, SUBLANES * (r + 1))
            sr, si = sc_ref[rows, lr], sc_ref[rows, li]
            if i == 0:
                xr, xi = sr, si
            else:
                xr, xi = ar * xr - ai * xi + sr, ar * xi + ai * xr + si
                sc_ref[rows, lr] = xr
                sc_ref[rows, li] = xi
        for sh in (1, 2, 4):
            pr, pi = bp_ref[sh - 1:sh, lr], sgn * bp_ref[sh - 1:sh, li]
            if reverse:
                tr = jnp.where(rid < SUBLANES - sh, pltpu.roll(xr, SUBLANES - sh, 0), 0.0)
                ti = jnp.where(rid < SUBLANES - sh, pltpu.roll(xi, SUBLANES - sh, 0), 0.0)
            else:
                tr = jnp.where(rid >= sh, pltpu.roll(xr, sh, 0), 0.0)
                ti = jnp.where(rid >= sh, pltpu.roll(xi, sh, 0), 0.0)
            xr, xi = xr + pr * tr - pi * ti, xi + pr * ti + pi * tr
        if reverse:
            bpr = jnp.zeros((SUBLANES, CH), f32)
            bpi = jnp.zeros((SUBLANES, CH), f32)
            for r in range(SUBLANES):
                bpr = jnp.where(rid == r, bp_ref[SUBLANES - 1 - r:SUBLANES - r, lr], bpr)
                bpi = jnp.where(rid == r, -bp_ref[SUBLANES - 1 - r:SUBLANES - r, li], bpi)
        else:
            bpr, bpi = bp_ref[:, lr], bp_ref[:, li]
        cr, cim = carry_ref[:, lr], carry_ref[:, li]
        xr, xi = xr + bpr * cr - bpi * cim, xi + bpr * cim + bpi * cr
        edge = 0 if reverse else SUBLANES - 1
        carry_ref[:, lr] = jnp.sum(jnp.where(rid == edge, xr, 0.0), axis=0, keepdims=True)
        carry_ref[:, li] = jnp.sum(jnp.where(rid == edge, xi, 0.0), axis=0, keepdims=True)
        if reverse:
            er = jnp.where(rid == SUBLANES - 1, cr, pltpu.roll(xr, SUBLANES - 1, 0))
            ei = jnp.where(rid == SUBLANES - 1, cim, pltpu.roll(xi, SUBLANES - 1, 0))
        else:
            er = jnp.where(rid == 0, cr, pltpu.roll(xr, 1, 0))
            ei = jnp.where(rid == 0, cim, pltpu.roll(xi, 1, 0))
        if e_ref is not None:
            e_ref[:, lr] = er
            e_ref[:, li] = ei
        for i in range(seg):
            r = seg - 1 - i if reverse else i
            rows = slice(SUBLANES * r, SUBLANES * (r + 1))
            pr, pi = ap_ref[i:i + 1, lr], sgn * ap_ref[i:i + 1, li]
            sc_ref[rows, lr] += pr * er - pi * ei
            sc_ref[rows, li] += pr * ei + pi * er


def _ssm_fwd(proj, bm, cm, a, d, name="ssm_fwd"):
    S = proj.shape[0]
    tS = _pick(S, (256, 128))
    nb = S // tS
    P2 = 2 * MACRO_ST
    seg = tS // SUBLANES
    ucol0 = (D_MODEL - SSM_WIDTH) // MACRO_CH

    perm = _segment_perm(tS)

    def body(u_ref, b_ref, c_ref, a_ref, d_ref, pm_ref, pt_ref, y_ref, xc_ref, bu_sc, ap_sc, bp_sc, car_sc):
        t = pl.program_id(1)

        @pl.when(t == 0)
        def _():
            _apow_init(a_ref, ap_sc, bp_sc, seg)
            car_sc[...] = jnp.zeros_like(car_sc)

        uv = u_ref[...]
        up = jnp.dot(pm_ref[...], uv.astype(bf16), preferred_element_type=f32).astype(bf16)
        bu_sc[...] = jnp.dot(up, b_ref[...], preferred_element_type=f32)
        xc_ref[...] = car_sc[...]
        _scan_block(bu_sc, ap_sc, bp_sc, car_sc, None, seg, False)
        yp = jnp.dot(bu_sc[...].astype(bf16), c_ref[...], preferred_element_type=f32)
        y_ref[...] = _unpermute_rows(pt_ref[...], yp) + d_ref[...] * uv

    sq = pl.BlockSpec((tS, tS), lambda m, t: (0, 0))
    return _pc(body, grid=(SSM_MACRO, nb),
               in_specs=[pl.BlockSpec((tS, MACRO_CH), lambda m, t: (t, ucol0 + m)),
                         pl.BlockSpec((None, MACRO_CH, P2), lambda m, t: (m, 0, 0)),
                         pl.BlockSpec((None, P2, MACRO_CH), lambda m, t: (m, 0, 0)),
                         pl.BlockSpec((None, 1, P2), lambda m, t: (m, 0, 0)),
                         pl.BlockSpec((1, MACRO_CH), lambda m, t: (0, m)), sq, sq],
               out_specs=[pl.BlockSpec((tS, MACRO_CH), lambda m, t: (t, m)),
                          pl.BlockSpec((None, None, 1, P2), lambda m, t: (m, t, 0, 0))],
               out_shape=[jax.ShapeDtypeStruct((S, SSM_WIDTH), f32), jax.ShapeDtypeStruct((SSM_MACRO, nb, 1, P2), f32)],
               scratch_shapes=[pltpu.VMEM((tS, P2), f32), pltpu.VMEM((seg, P2), f32),
                               pltpu.VMEM((SUBLANES, P2), f32), pltpu.VMEM((1, P2), f32)],
               compiler_params=_cp(("arbitrary", "arbitrary")), name=name)(
        proj, bm, cm, a, d.reshape(1, SSM_WIDTH), perm.astype(bf16), perm.T.astype(bf16))


def _ssm_bwd(proj, dy, xc, bm, cm, a, d, plan=None, name="ssm_bwd"):
    S = proj.shape[0]
    tS = _pick(S, (512, 256, 128))
    nb = S // tS
    P = MACRO_ST
    P2 = 2 * P
    seg = tS // SUBLANES
    ucol0 = (D_MODEL - SSM_WIDTH) // MACRO_CH

    perm = _segment_perm(tS)
    npl = plan['n'] if plan else 0
    p_in, p_out, p_shapes, p_sems, p_args = _plan_extras(plan)

    def body(*refs):
        u_ref, dy_ref, xc_ref, b_ref, c_ref, a_ref, d_ref, pm_ref, pt_ref = refs[:9]
        du_ref, db_ref, dc_ref, da_ref, dd_ref = refs[9 + npl:14 + npl]
        x_sc, g_sc, ap_sc, bp_sc, e_sc, xcar_sc, gcar_sc = refs[14 + 2 * npl:21 + 2 * npl]
        t = pl.program_id(1)
        if plan:
            mg = pl.program_id(0)
            pargs = (refs[9:9 + npl], refs[14 + npl:14 + 2 * npl], refs[21 + 2 * npl], refs[22 + 2 * npl])
            pl.when((mg == 0) & (t == 0))(functools.partial(plan['start'], *pargs))
            pl.when((mg == SSM_MACRO - 1) & (t == nb - 1))(functools.partial(plan['finish'], *pargs))

        @pl.when(t == 0)
        def _():
            _apow_init(a_ref, ap_sc, bp_sc, seg)
            gcar_sc[...] = jnp.zeros_like(gcar_sc)
            db_ref[...] = jnp.zeros_like(db_ref)
            dc_ref[...] = jnp.zeros_like(dc_ref)
            da_ref[...] = jnp.zeros_like(da_ref)
            dd_ref[...] = jnp.zeros_like(dd_ref)

        nt = (((1,), (1,)), ((), ()))
        tn = (((0,), (0,)), ((), ()))
        uv = u_ref[...]
        dyv = dy_ref[...]
        pm = pm_ref[...]
        ub = jnp.dot(pm, uv.astype(bf16), preferred_element_type=f32).astype(bf16)
        dyb = jnp.dot(pm, dyv.astype(bf16), preferred_element_type=f32).astype(bf16)
        x_sc[...] = jnp.dot(ub, b_ref[...], preferred_element_type=f32)
        xcar_sc[...] = xc_ref[...]
        _scan_block(x_sc, ap_sc, bp_sc, xcar_sc, e_sc, seg, False)
        g_sc[...] = lax.dot_general(dyb, c_ref[...], nt, preferred_element_type=f32)
        _scan_block(g_sc, ap_sc, bp_sc, gcar_sc, None, seg, True)
        xv = x_sc[...]
        gv = g_sc[...]
        gb = gv.astype(bf16)
        dc_ref[...] += lax.dot_general(xv.astype(bf16), dyb, tn, preferred_element_type=f32)
        db_ref[...] += lax.dot_general(ub, gb, tn, preferred_element_type=f32)
        dup = lax.dot_general(gb, b_ref[...], nt, preferred_element_type=f32)
        du_ref[...] = _unpermute_rows(pt_ref[...], dup) + d_ref[...] * dyv
        dd_ref[...] += jnp.sum(dyv * uv, axis=0, keepdims=True)
        xp = jnp.concatenate([e_sc[...], xv[:tS - SUBLANES]], axis=0)
        xpr, xpi, ggr, ggi = xp[:, :P], xp[:, P:], gv[:, :P], gv[:, P:]
        da_ref[:, :P] += jnp.sum(ggr * xpr + ggi * xpi, axis=0, keepdims=True)
        da_ref[:, P:] += jnp.sum(ggi * xpr - ggr * xpi, axis=0, keepdims=True)

    rev = lambda t: nb - 1 - t
    outs = _pc(body, grid=(SSM_MACRO, nb),
               in_specs=[pl.BlockSpec((tS, MACRO_CH), lambda m, t: (rev(t), ucol0 + m)),
                         pl.BlockSpec((tS, MACRO_CH), lambda m, t: (rev(t), m)),
                         pl.BlockSpec((None, None, 1, P2), lambda m, t: (m, rev(t) * (xc.shape[1] // nb), 0, 0)),
                         pl.BlockSpec((None, MACRO_CH, P2), lambda m, t: (m, 0, 0)),
                         pl.BlockSpec((None, P2, MACRO_CH), lambda m, t: (m, 0, 0)),
                         pl.BlockSpec((None, 1, P2), lambda m, t: (m, 0, 0)),
                         pl.BlockSpec((1, MACRO_CH), lambda m, t: (0, m)),
                         pl.BlockSpec((tS, tS), lambda m, t: (0, 0)), pl.BlockSpec((tS, tS), lambda m, t: (0, 0))] + p_in,
               out_specs=[pl.BlockSpec((tS, MACRO_CH), lambda m, t: (rev(t), m)),
                          pl.BlockSpec((None, MACRO_CH, P2), lambda m, t: (m, 0, 0)),
                          pl.BlockSpec((None, P2, MACRO_CH), lambda m, t: (m, 0, 0)),
                          pl.BlockSpec((None, 1, P2), lambda m, t: (m, 0, 0)),
                          pl.BlockSpec((1, MACRO_CH), lambda m, t: (0, m))] + p_out,
               out_shape=[jax.ShapeDtypeStruct((S, SSM_WIDTH), f32),
                          jax.ShapeDtypeStruct((SSM_MACRO, MACRO_CH, P2), f32),
                          jax.ShapeDtypeStruct((SSM_MACRO, P2, MACRO_CH), f32),
                          jax.ShapeDtypeStruct((SSM_MACRO, 1, P2), f32),
                          jax.ShapeDtypeStruct((1, SSM_WIDTH), f32)] + p_shapes,
               scratch_shapes=[pltpu.VMEM((tS, P2), f32), pltpu.VMEM((tS, P2), f32),
                               pltpu.VMEM((seg, P2), f32), pltpu.VMEM((SUBLANES, P2), f32), pltpu.VMEM((SUBLANES, P2), f32),
                               pltpu.VMEM((1, P2), f32), pltpu.VMEM((1, P2), f32)] + p_sems,
               compiler_params=_cp(("arbitrary", "arbitrary")), name=name)(
        proj, dy, xc, bm, cm, a, d.reshape(1, SSM_WIDTH), perm.astype(bf16), perm.T.astype(bf16), *p_args)
    return outs[0], outs[1], outs[2], outs[3], outs[4], outs[5:]


_GELU_K = math.sqrt(2.0 / math.pi)
_GELU_C = 0.044715


def _glu_fwd(y, w, b, g, name="glu_fwd"):
    S, W = y.shape
    ts = _pick(S, (512, 256, 128))

    def body(y_ref, w_ref, b_ref, g_ref, z_ref, sn_ref, ge_ref):
        yv = y_ref[...]
        cdf = 0.5 * (1.0 + jnp.tanh(_GELU_K * (yv + _GELU_C * (yv * yv * yv))))
        ge = (yv * cdf).astype(bf16)
        z = jnp.dot(ge, w_ref[...], preferred_element_type=f32) + b_ref[...]
        s = yv * jax.nn.sigmoid(z)
        rstd = lax.rsqrt(jnp.sum(s * s, axis=-1, keepdims=True) * (1.0 / W) + EPS)
        z_ref[...] = z
        sn_ref[...] = (s * rstd * g_ref[...]).astype(bf16)
        ge_ref[...] = ge

    blk = pl.BlockSpec((ts, W), lambda i: (i, 0))
    row = pl.BlockSpec((1, W), lambda i: (0, 0))
    return _pc(body, grid=(S // ts,), in_specs=[blk, pl.BlockSpec((W, W), lambda i: (0, 0)), row, row],
               out_specs=[blk, blk, blk],
               out_shape=[jax.ShapeDtypeStruct((S, W), f32), jax.ShapeDtypeStruct((S, W), bf16),
                          jax.ShapeDtypeStruct((S, W), bf16)],
               compiler_params=_cp(("parallel",)), name=name)(y, w, b.reshape(1, W), g.reshape(1, W))


def _glu_bwd(y, z, dmixed, w, g, name="glu_bwd"):
    S, W = y.shape
    ts = _pick(S, (512, 256, 128))
    dcb = MLA_PAD // W

    def body(y_ref, z_ref, dsn_ref, w_ref, g_ref, dy_ref, dz_ref, dg_ref, db_ref):
        i = pl.program_id(0)
        yv, zv, gv = y_ref[...], z_ref[...], g_ref[...]
        sig = jax.nn.sigmoid(zv)
        s = yv * sig
        rstd = lax.rsqrt(jnp.sum(s * s, axis=-1, keepdims=True) * (1.0 / W) + EPS)
        sh = s * rstd
        dsn = dsn_ref[...]
        dsh = dsn * gv
        ds = rstd * (dsh - sh * (jnp.sum(dsh * sh, axis=-1, keepdims=True) * (1.0 / W)))
        dz = ds * s * (1.0 - sig)
        dzb = dz.astype(bf16)
        dge = lax.dot_general(dzb, w_ref[...], (((1,), (1,)), ((), ())), preferred_element_type=f32)
        t = jnp.tanh(_GELU_K * (yv + _GELU_C * (yv * yv * yv)))
        dgelu = 0.5 * (1.0 + t) + 0.5 * yv * (1.0 - t * t) * _GELU_K * (1.0 + 3.0 * _GELU_C * yv * yv)
        dy_ref[...] = ds * sig + dge * dgelu
        dz_ref[...] = dzb

        @pl.when(i == 0)
        def _():
            dg_ref[...] = jnp.zeros_like(dg_ref)
            db_ref[...] = jnp.zeros_like(db_ref)

        dg_ref[...] += jnp.sum(dsn * sh, axis=0, keepdims=True)
        db_ref[...] += jnp.sum(dz, axis=0, keepdims=True)

    blk = pl.BlockSpec((ts, W), lambda i: (i, 0))
    row = pl.BlockSpec((1, W), lambda i: (0, 0))
    return _pc(body, grid=(S // ts,),
               in_specs=[blk, blk, pl.BlockSpec((ts, W), lambda i: (i, dcb)), pl.BlockSpec((W, W), lambda i: (0, 0)), row],
               out_specs=[blk, blk, row, row],
               out_shape=[jax.ShapeDtypeStruct((S, W), f32), jax.ShapeDtypeStruct((S, W), bf16),
                          jax.ShapeDtypeStruct((1, W), f32), jax.ShapeDtypeStruct((1, W), f32)],
               compiler_params=_cp(("arbitrary",)), name=name)(y, z, dmixed, w, g.reshape(1, W))


def _ffn_up(hn, wg, wu, name="ffn_up"):
    S, K = hn.shape
    F = wg.shape[0]
    tm, tn = _pick(S, (512, 256, 128)), _pick(F, (1408, 256, 128))

    def body(h_ref, wg_ref, wu_ref, g_ref, u_ref, a_ref):
        hv = h_ref[...]
        nt = (((1,), (1,)), ((), ()))
        gv = lax.dot_general(hv, wg_ref[...], nt, preferred_element_type=f32)
        uv = lax.dot_general(hv, wu_ref[...], nt, preferred_element_type=f32)
        g_ref[...] = gv.astype(bf16)
        u_ref[...] = uv.astype(bf16)
        a_ref[...] = (gv * jax.nn.sigmoid(gv) * uv).astype(bf16)

    wspec = pl.BlockSpec((tn, K), lambda i, j: (j, 0))
    ospec = pl.BlockSpec((tm, tn), lambda i, j: (i, j))
    return _pc(body, grid=(S // tm, F // tn), in_specs=[pl.BlockSpec((tm, K), lambda i, j: (i, 0)), wspec, wspec],
               out_specs=[ospec] * 3,
               out_shape=[jax.ShapeDtypeStruct((S, F), bf16), jax.ShapeDtypeStruct((S, F), bf16),
                          jax.ShapeDtypeStruct((S, F), bf16)],
               compiler_params=_cp(("parallel", "parallel")), name=name)(hn, wg, wu)


def _ffn_bwd_act(dh, wd, gate, up, name="ffn_bwd_act"):
    S, K = dh.shape
    F = wd.shape[0]
    tm, tn = _pick(S, (512, 256, 128)), _pick(F, (1408, 256, 128))

    def body(dh_ref, wd_ref, g_ref, u_ref, dg_ref, du_ref):
        dact = lax.dot_general(dh_ref[...].astype(bf16), wd_ref[...], (((1,), (1,)), ((), ())),
                               preferred_element_type=f32)
        gv, uv = g_ref[...].astype(f32), u_ref[...].astype(f32)
        sig = jax.nn.sigmoid(gv)
        dg_ref[...] = (dact * uv * (sig * (1.0 + gv * (1.0 - sig)))).astype(bf16)
        du_ref[...] = (dact * (gv * sig)).astype(bf16)

    ospec = pl.BlockSpec((tm, tn), lambda i, j: (i, j))
    return _pc(body, grid=(S // tm, F // tn),
               in_specs=[pl.BlockSpec((tm, K), lambda i, j: (i, 0)), pl.BlockSpec((tn, K), lambda i, j: (j, 0)),
                         ospec, ospec],
               out_specs=[ospec] * 2, out_shape=[jax.ShapeDtypeStruct((S, F), bf16)] * 2,
               compiler_params=_cp(("parallel", "parallel")), name=name)(dh, wd, gate, up)


def _pad_heads(w, per_head, pieces):
    K = w.shape[0]
    w3 = w.reshape(K, MLA_HEADS, per_head)
    out = jnp.zeros((K, MLA_HEADS, HEAD_PAD), w.dtype)
    for s0, s1, d0 in pieces:
        out = out.at[:, :, d0:d0 + (s1 - s0)].set(w3[:, :, s0:s1])
    return out.reshape(K, MLA_PAD)


def _unpad_heads(wp, per_head, pieces):
    K = wp.shape[0]
    w3 = wp.reshape(K, MLA_HEADS, HEAD_PAD)
    out = jnp.zeros((K, MLA_HEADS, per_head), wp.dtype)
    for s0, s1, d0 in pieces:
        out = out.at[:, :, s0:s1].set(w3[:, :, d0:d0 + (s1 - s0)])
    return out.reshape(K, MLA_HEADS * per_head)


_Q_PIECES = [(0, QK_NOPE + QK_ROPE, 0)]
_K_PIECES = [(0, QK_NOPE, 0)]
_V_PIECES = [(QK_NOPE, QK_NOPE + V_HEAD, 0)]
_KR0 = Q_LORA + KV_LORA


def _pack_win(w):
    z = jnp.zeros((w.shape[0], LANES - QK_ROPE), w.dtype)
    return jnp.concatenate([w[:, :_KR0 + QK_ROPE], z, w[:, _KR0 + QK_ROPE:]], axis=1)


def _unpack_win(wp):
    return jnp.concatenate([wp[:, :_KR0 + QK_ROPE], wp[:, _KR0 + LANES:]], axis=1)


def _pack_wout(w):
    wa = w[:MLA_WIDTH].reshape(MLA_HEADS, V_HEAD, D_MODEL)
    wa = jnp.concatenate([wa, jnp.zeros_like(wa)], axis=1).reshape(MLA_PAD, D_MODEL)
    return jnp.concatenate([wa, w[MLA_WIDTH:]], axis=0)


def _unpack_wout(wp):
    wa = wp[:MLA_PAD].reshape(MLA_HEADS, HEAD_PAD, D_MODEL)[:, :V_HEAD].reshape(MLA_WIDTH, D_MODEL)
    return jnp.concatenate([wa, wp[MLA_PAD:]], axis=0)


def _pad_gain(g):
    g2 = g.reshape(MLA_HEADS, V_HEAD)
    return jnp.concatenate([g2, jnp.zeros_like(g2)], axis=1).reshape(MLA_PAD)


def _unpad_gain(gp):
    return gp.reshape(MLA_HEADS, HEAD_PAD)[:, :V_HEAD].reshape(MLA_WIDTH)


def _ssm_prep(lam_re, lam_im, log_dt, b_re, b_im, c_re, c_im):
    lam = lax.complex(lam_re, lam_im)
    dt = jnp.exp(log_dt)[:, None]
    a_bar = jnp.exp(lam * dt)
    b_bar = ((a_bar - 1.0) / lam)[:, None, :] * lax.complex(b_re, b_im)
    G8 = SSM_GROUPS // SSM_MACRO
    eye = jnp.eye(G8, dtype=f32)

    def bmat(part):
        p4 = part.reshape(SSM_MACRO, G8, SSM_GROUP, SSM_STATE)
        return jnp.einsum('mgcp,gh->mgchp', p4, eye).reshape(SSM_MACRO, MACRO_CH, MACRO_ST)

    def cmat(part):
        p4 = part.reshape(SSM_MACRO, G8, SSM_GROUP, SSM_STATE)
        return jnp.einsum('mgcp,gh->mgphc', p4, eye).reshape(SSM_MACRO, MACRO_ST, MACRO_CH)

    bm = jnp.concatenate([bmat(b_bar.real), bmat(b_bar.imag)], axis=2)
    cm = jnp.concatenate([cmat(c_re), -cmat(c_im)], axis=1)
    a4 = a_bar.reshape(SSM_MACRO, 1, MACRO_ST)
    a = jnp.concatenate([a4.real, a4.imag], axis=2)
    return bm, cm, a


def _rope_tables(positions):
    freqs = ROPE_THETA ** (-jnp.arange(0, QK_ROPE, 2, dtype=f32) / QK_ROPE)
    ang = positions.astype(f32)[:, None] * freqs
    cos, sin = jnp.cos(ang), jnp.sin(ang)
    S = positions.shape[0]
    half = QK_ROPE // 2
    one, zero = jnp.ones((S, QK_NOPE), f32), jnp.zeros((S, half), f32)
    z64, z32 = jnp.zeros((S, QK_NOPE), f32), jnp.zeros((S, LANES - QK_NOPE - QK_ROPE), f32)
    tc = jnp.concatenate([one, cos, cos, z32], axis=1)
    s1 = jnp.concatenate([z64, -sin, zero, z32], axis=1)
    s2 = jnp.concatenate([z64, zero, sin, z32], axis=1)
    return tc, s1, s2


def _layer_params(W, l):
    p = {}
    p['win'] = _pack_win(W['w_in'][l])
    p['wuq'] = _pad_heads(W['w_uq'][l], QK_NOPE + QK_ROPE, _Q_PIECES)
    wukv = W['w_ukv'][l]
    p['wukv'] = jnp.concatenate([_pad_heads(wukv, QK_NOPE + V_HEAD, _K_PIECES),
                                 _pad_heads(wukv, QK_NOPE + V_HEAD, _V_PIECES)], axis=1)
    p['attn_g'] = _pad_gain(W['attn_out_g'][l])
    return p


def _forward_layer(h, memn_in, tabs, W, l, name, gather=None):
    p = _layer_params(W, l)
    sv = {'h0': h, 'p': p}
    xn = _rms_fwd(h, W['norm_mix_g'][l], name=name + "rms_mix")
    proj = _mm([(xn, p['win'])], 'nn', f32, name=name + "mm_in")
    cqn = _rms_fwd(proj, W['q_norm_g'][l], col0=0, width=Q_LORA, name=name + "rms_q")
    ckvn = _rms_fwd(proj, W['kv_norm_g'][l], col0=Q_LORA, width=KV_LORA, name=name + "rms_kv")
    q = _mm([(cqn, p['wuq'])], 'nn', f32, name=name + "mm_uq")
    kv = _mm([(ckvn, p['wukv'])], 'nn', f32, name=name + "mm_ukv")
    qh, kh, vh = _rope_fwd(q, kv, proj, tabs, name=name + "rope")
    oh, lse, carried = _attn_fwd(qh, kh, vh, plan=gather[0] if gather else None, name=name + "attn")
    if gather:
        for n, t in gather[1](carried).items():
            W[n][l] = t
    p['wout'] = _pack_wout(W['w_out'][l])
    an = _rms_fwd(oh, p['attn_g'], n_valid=MLA_WIDTH, name=name + "rms_attn")
    bm, cm, a = W['ssm'][l]
    bmb, cmb = bm.astype(bf16), cm.astype(bf16)
    y, xc = _ssm_fwd(proj, bmb, cmb, a, W['ssm_d'][l], name=name + "ssm")
    z, sn, ge = _glu_fwd(y, W['ssm_w_glu'][l], W['ssm_b_glu'][l], W['ssm_out_g'][l], name=name + "glu")
    h1a = _mm([(an, p['wout'][:MLA_PAD])], 'nn', f32, res=h, name=name + "mm_out_a")
    h1 = _mm([(sn, p['wout'][MLA_PAD:])], 'nn', f32, res=h1a, name=name + "mm_out_s")
    hn2 = _rms_fwd(h1, W['norm_x_g'][l], name=name + "rms_x")
    memn = _rms_fwd(memn_in, W['mem_norm_g'][l], name=name + "rms_mem")
    qx = _mm([(hn2, W['w_xq'][l])], 'nn', bf16, name=name + "mm_xq")
    kvx = _mm([(memn, W['w_xkv'][l])], 'nn', bf16, name=name + "mm_xkv")
    ox = _xattn_fwd(qx, kvx, name=name + "xattn")
    h2 = _mm([(ox, W['w_xo'][l])], 'nn', f32, res=h1, name=name + "mm_xo")
    hn3 = _rms_fwd(h2, W['norm_ffn_g'][l], name=name + "rms_ffn")
    gate, up, act = _ffn_up(hn3, W['w_gate'][l], W['w_up'][l], name=name + "ffn_up")
    h3 = _mm([(act, W['w_down'][l])], 'nn', f32, res=h2, name=name + "mm_down")
    sv.update(xn=xn, proj=proj, cqn=cqn, ckvn=ckvn, qh=qh, kh=kh, vh=vh, oh=oh, lse=lse, an=an, bmb=bmb, cmb=cmb,
              a=a, y=y, xc=xc, z=z, sn=sn, ge=ge, h1=h1, hn2=hn2, memn=memn, qx=qx, kvx=kvx, ox=ox, h2=h2, hn3=hn3,
              gate=gate, up=up, act=act)
    return h3, sv


def _backward_layer(dh3, sv, memn_in, tabs, W, l, hook, name):
    p = sv['p']
    G = {}
    G['w_down'] = _mm([(sv['act'], dh3)], 'tn', f32, name=name + "dw_down")
    dgate, dup = _ffn_bwd_act(dh3, W['w_down'][l], sv['gate'], sv['up'], name=name + "ffn_bwd_act")
    dhn3 = _mm([(dgate, W['w_gate'][l]), (dup, W['w_up'][l])], 'nn', f32, name=name + "mm_dffn")
    G['w_gate'] = _mm([(dgate, sv['hn3'])], 'tn', f32, name=name + "dw_gate")
    G['w_up'] = _mm([(dup, sv['hn3'])], 'tn', f32, name=name + "dw_up")
    xplan = hook['ffn'](l, G) if hook else None
    dh2, dg = _rms_bwd(sv['h2'], W['norm_ffn_g'][l], dhn3, res=dh3, name=name + "rmsb_ffn")
    G['norm_ffn_g'] = dg[0]
    G['w_xo'] = _mm([(sv['ox'], dh2)], 'tn', f32, name=name + "dw_xo")
    dox = _mm([(dh2, W['w_xo'][l])], 'nt', bf16, name=name + "mm_dxo")
    dqx, dkvx, carried = _xattn_bwd(sv['qx'], sv['kvx'], dox, plan=xplan, name=name + "xattn_bwd")
    if xplan:
        xplan['done'](carried)
    G['w_xq'] = _mm([(sv['hn2'], dqx)], 'tn', f32, name=name + "dw_xq")
    G['w_xkv'] = _mm([(sv['memn'], dkvx)], 'tn', f32, name=name + "dw_xkv")
    dhn2 = _mm([(dqx, W['w_xq'][l])], 'nt', f32, name=name + "mm_dxq")
    dmemn = _mm([(dkvx, W['w_xkv'][l])], 'nt', f32, name=name + "mm_dxkv")
    dh1, dg = _rms_bwd(sv['h1'], W['norm_x_g'][l], dhn2, res=dh2, name=name + "rmsb_x")
    G['norm_x_g'] = dg[0]
    _, dg = _rms_bwd(memn_in, W['mem_norm_g'][l], dmemn, name=name + "rmsb_mem")
    G['mem_norm_g'] = dg[0]
    dwo_a = _mm([(sv['an'], dh1)], 'tn', f32, name=name + "dw_out_a")
    dwo_s = _mm([(sv['sn'], dh1)], 'tn', f32, name=name + "dw_out_s")
    G['w_out'] = _unpack_wout(jnp.concatenate([dwo_a, dwo_s], axis=0))
    dmixed = _mm([(dh1, p['wout'])], 'nt', f32, name=name + "mm_dout")
    dy, dz, dg, db = _glu_bwd(sv['y'], sv['z'], dmixed, W['ssm_w_glu'][l], W['ssm_out_g'][l], name=name + "glu_bwd")
    G['ssm_out_g'], G['ssm_b_glu'] = dg[0], db[0]
    G['ssm_w_glu'] = _mm([(sv['ge'], dz)], 'tn', f32, name=name + "dw_glu")
    xplan = hook['mid'](l, G) if hook else None
    du, dbm, dcm, da, dd, carried = _ssm_bwd(sv['proj'], dy, sv['xc'], sv['bmb'], sv['cmb'], sv['a'], W['ssm_d'][l],
                                             plan=xplan, name=name + "ssm_bwd")
    if xplan:
        xplan['done'](carried)
    G['ssm_d'] = dd[0]
    G['ssm_raw'] = (dbm, dcm, da)
    doh, dg, delta = _rms_bwd(sv['oh'], p['attn_g'], dmixed, width=MLA_PAD, n_valid=MLA_WIDTH, delta=True,
                              out_dtype=bf16, name=name + "rmsb_attn")
    G['attn_out_g'] = _unpad_gain(dg[0])
    plans = hook['take']() if hook else []
    plan = _merge_plans(plans) if plans else None
    dqh, dkh, dvh, carried = _attn_bwd(sv['qh'], sv['kh'], sv['vh'], doh, sv['lse'], delta, plan=plan,
                                       name=name + "attn_bwd")
    if plan:
        plan['done'](carried)
    dq, dkv, dkr = _rope_bwd(dqh, dkh, dvh, tabs, name=name + "rope_bwd")
    G['w_uq'] = _unpad_heads(_mm([(sv['cqn'], dq)], 'tn', f32, name=name + "dw_uq"), QK_NOPE + QK_ROPE, _Q_PIECES)
    dwukv = _mm([(sv['ckvn'], dkv)], 'tn', f32, name=name + "dw_ukv")
    G['w_ukv'] = (_unpad_heads(dwukv[:, :MLA_PAD], QK_NOPE + V_HEAD, _K_PIECES)
                  + _unpad_heads(dwukv[:, MLA_PAD:], QK_NOPE + V_HEAD, _V_PIECES))
    dcqn = _mm([(dq, p['wuq'])], 'nt', f32, name=name + "mm_duq")
    dckvn = _mm([(dkv, p['wukv'])], 'nt', f32, name=name + "mm_dukv")
    dcq, dg = _rms_bwd(sv['proj'], W['q_norm_g'][l], dcqn, col0=0, width=Q_LORA, out_dtype=bf16, name=name + "rmsb_q")
    G['q_norm_g'] = dg[0]
    dckv, dg = _rms_bwd(sv['proj'], W['kv_norm_g'][l], dckvn, col0=Q_LORA, width=KV_LORA, out_dtype=bf16,
                        name=name + "rmsb_kv")
    G['kv_norm_g'] = dg[0]
    dproj = jnp.concatenate([dcq, dckv, dkr, du.astype(bf16)], axis=1)
    G['w_in'] = _unpack_win(_mm([(sv['xn'], dproj)], 'tn', f32, name=name + "dw_in"))
    dxn = _mm([(dproj, p['win'])], 'nt', f32, name=name + "mm_din")
    dh0, dg = _rms_bwd(sv['h0'], W['norm_mix_g'][l], dxn, res=dh1, name=name + "rmsb_mix")
    G['norm_mix_g'] = dg[0]
    return dh0, G


def _local_step(x, mem, positions, target, W, gathers=None, hook=None):
    tabs = _rope_tables(positions)
    ssm_in = [(W['ssm_lambda_re'][l], W['ssm_lambda_im'][l], W['ssm_log_dt'][l], W['ssm_b_re'][l], W['ssm_b_im'][l],
               W['ssm_c_re'][l], W['ssm_c_im'][l]) for l in range(DEPTH)]
    preps = [jax.vjp(_ssm_prep, *ssm_in[l]) for l in range(DEPTH)]
    W = dict(W)
    W['ssm'] = [preps[l][0] for l in range(DEPTH)]
    h = x
    saved = []
    for l in range(DEPTH):
        h, sv = _forward_layer(h, mem, tabs, W, l, f"l{l}_", gathers[l] if gathers else None)
        saved.append(sv)
    dh, dgf, loss = _loss_head(h, W['final_norm_g'], target)
    grads = [None] * DEPTH
    for l in reversed(range(DEPTH)):
        dh, G = _backward_layer(dh, saved[l], mem, tabs, W, l, hook, f"l{l}b_")
        dbm, dcm, da = G.pop('ssm_raw')
        names = ['ssm_lambda_re', 'ssm_lambda_im', 'ssm_log_dt', 'ssm_b_re', 'ssm_b_im', 'ssm_c_re', 'ssm_c_im']
        for n, g in zip(names, preps[l][1]((dbm, dcm, da))):
            G[n] = g
        grads[l] = G
        if hook is not None and l > 0:
            hook['rest'](l, G)
    out = {n: [grads[l][n] for l in range(DEPTH)] if n in SHARDED else jnp.stack([grads[l][n] for l in range(DEPTH)])
           for n in grads[0]}
    out['final_norm_g'] = dgf[0]
    return loss[0, 0], dh, out


_HBM = pl.BlockSpec(memory_space=pltpu.HBM)


def _me():
    return lax.axis_index("x"), lax.axis_index("y"), lax.axis_index("c")


def _chip_peers(x, y, c):
    devs = [(1 - x, y, c), (x, 1 - y, c), (1 - x, 1 - y, c)]
    return devs, [2 * d[0] + d[1] for d in devs]


def _gather_plan(xs, half_first):
    n = len(xs)
    if half_first:
        ins = [t.reshape(2, 1, *t.shape[1:]) for t in xs]
        outs = [jax.ShapeDtypeStruct((2, 4, *t.shape[1:]), t.dtype) for t in xs]
    else:
        ins = [t.reshape(1, 2, t.shape[0] // 2, t.shape[1]) for t in xs]
        outs = [jax.ShapeDtypeStruct((4, 2, t.shape[0] // 2, t.shape[1]), t.dtype) for t in xs]

    def own(ref, h):
        return ref.at[h] if half_first else ref.at[:, h]

    def slot(ref, h, j):
        return ref.at[h, pl.ds(j, 1)] if half_first else ref.at[pl.ds(j, 1), h]

    def copies(src, dst, send, recv):
        x, y, c = _me()
        jme = 2 * x + y
        devs, js = _chip_peers(x, y, c)
        half, other = pl.ds(c, 1), pl.ds(1 - c, 1)
        mk = pltpu.make_async_remote_copy
        for i in range(n):
            for k in range(3):
                out_cp = mk(own(src[i], half), slot(dst[i], half, jme), send.at[6 * i + k], recv.at[6 * i + k],
                            device_id=devs[k], device_id_type=MESH)
                in_cp = mk(own(src[i], half), slot(dst[i], half, js[k]), send.at[6 * i + k], recv.at[6 * i + k],
                           device_id=devs[k], device_id_type=MESH)
                pass_cp = mk(slot(dst[i], half, js[k]), slot(dst[i], half, js[k]), send.at[6 * i + 3 + k],
                             recv.at[6 * i + 3 + k], device_id=(x, y, 1 - c), device_id_type=MESH)
                got_cp = mk(slot(dst[i], other, js[k]), slot(dst[i], other, js[k]), send.at[6 * i + 3 + k],
                            recv.at[6 * i + 3 + k], device_id=(x, y, 1 - c), device_id_type=MESH)
                yield out_cp, in_cp, pass_cp, got_cp

    def start(*refs):
        for out_cp, _, _, _ in copies(*refs):
            out_cp.start()

    def forward(*refs):
        for _, in_cp, pass_cp, _ in copies(*refs):
            in_cp.wait_recv()
            pass_cp.start()

    def finish(*refs):
        for out_cp, _, pass_cp, got_cp in copies(*refs):
            got_cp.wait_recv()
            out_cp.wait_send()
            pass_cp.wait_send()

    return dict(n=n, ins=ins, outs=outs, nsem=6 * n, start=start, forward=forward, finish=finish)


def _plan_refs(plan, refs):
    n = plan['n']
    return refs[:n], refs[n:2 * n], refs[2 * n], refs[2 * n + 1]


def _run_plan(plan, name):
    n = plan['n']

    def body(*refs):
        args = _plan_refs(plan, refs)
        plan['start'](*args)
        plan['forward'](*args)
        plan['finish'](*args)

    return _pc(body, in_specs=[_HBM] * n, out_specs=[_HBM] * n, out_shape=plan['outs'],
               scratch_shapes=[pltpu.SemaphoreType.DMA((plan['nsem'],)), pltpu.SemaphoreType.DMA((plan['nsem'],))],
               compiler_params=pltpu.CompilerParams(has_side_effects=True), name=name)(*plan['ins'])


def _fill_own(gathered, own, half_first):
    jme = (2 * lax.axis_index("x") + lax.axis_index("y")).astype(jnp.int32)
    zero = jnp.int32(0)
    if half_first:
        return lax.dynamic_update_slice(gathered, own[:, None], (zero, jme, zero, zero))
    return lax.dynamic_update_slice(gathered, own.reshape(1, *gathered.shape[1:]), (jme, zero, zero, zero))


def _exchange_plan(gs, done=None):
    n = len(gs)

    def copies(src, dst, send, recv):
        x, y, c = _me()
        for i in range(n):
            yield pltpu.make_async_remote_copy(src[i].at[:, pl.ds(1 - c, 1)], dst[i], send.at[i], recv.at[i],
                                               device_id=(x, y, 1 - c), device_id_type=MESH)

    def start(*refs):
        for cp in copies(*refs):
            cp.start()

    def finish(*refs):
        for cp in copies(*refs):
            cp.wait()

    outs = [jax.ShapeDtypeStruct((4, 1, *g.shape[2:]), g.dtype) for g in gs]
    return dict(n=n, ins=list(gs), outs=outs, nsem=n, start=start, forward=lambda *refs: None, finish=finish, done=done)


def _plan_extras(plan):
    if not plan:
        return [], [], [], [], []
    anyspec = pl.BlockSpec(memory_space=pl.ANY)
    sems = [pltpu.SemaphoreType.DMA((plan['nsem'],)), pltpu.SemaphoreType.DMA((plan['nsem'],))]
    return [anyspec] * plan['n'], [anyspec] * plan['n'], list(plan['outs']), sems, list(plan['ins'])


def _scatter_plan(ps, done=None):
    n = len(ps)

    def copies(src, dst, send, recv, off):
        x, y, c = _me()
        devs, js = _chip_peers(x, y, c)
        for i in range(n):
            for k in range(3):
                yield pltpu.make_async_remote_copy(src[i].at[pl.ds(js[k], 1)], dst[i].at[k], send.at[off + 3 * i + k],
                                                   recv.at[off + 3 * i + k], device_id=devs[k], device_id_type=MESH)

    def start(src, dst, send, recv, off=0):
        for cp in copies(src, dst, send, recv, off):
            cp.start()

    def finish(src, dst, send, recv, off=0):
        for cp in copies(src, dst, send, recv, off):
            cp.wait()

    outs = [jax.ShapeDtypeStruct((3, 1, *p.shape[1:]), p.dtype) for p in ps]
    return dict(n=n, ins=list(ps), outs=outs, nsem=3 * n, start=start, forward=lambda *refs: None, finish=finish,
                done=done)


def _merge_plans(plans):
    def run(which):
        def f(src, dst, send, recv):
            o = s = 0
            for p in plans:
                p[which](src[o:o + p['n']], dst[o:o + p['n']], send, recv, off=s)
                o, s = o + p['n'], s + p['nsem']
        return f

    def done(results):
        o = 0
        for p in plans:
            p['done'](results[o:o + p['n']])
            o += p['n']

    return dict(n=sum(p['n'] for p in plans), ins=[t for p in plans for t in p['ins']],
                outs=[t for p in plans for t in p['outs']], nsem=sum(p['nsem'] for p in plans),
                start=run('start'), forward=lambda *refs: None, finish=run('finish'), done=done)


def _swap_sibling(hs, name):
    n = len(hs)

    def body(*refs):
        src, dst = refs[:n], refs[n:2 * n]
        send, recv = refs[2 * n:]
        x, y, c = _me()
        cps = []
        for i in range(n):
            cp = pltpu.make_async_remote_copy(src[i], dst[i], send.at[i], recv.at[i], device_id=(x, y, 1 - c),
                                              device_id_type=MESH)
            cp.start()
            cps.append(cp)
        for cp in cps:
            cp.wait()

    outs = [jax.ShapeDtypeStruct(h.shape, h.dtype) for h in hs]
    return _pc(body, in_specs=[_HBM] * n, out_specs=[_HBM] * n, out_shape=outs,
               scratch_shapes=[pltpu.SemaphoreType.DMA((n,)), pltpu.SemaphoreType.DMA((n,))],
               compiler_params=pltpu.CompilerParams(has_side_effects=True), name=name)(*hs)


ELEMWISE_VMEM_BUDGET = 24 * 1024 * 1024


def _row_tile(r, n, narrays):
    limit = ELEMWISE_VMEM_BUDGET // (2 * 4 * narrays * n)
    best = SUBLANES
    for t in range(16, r + 1, 16):
        if r % t == 0 and t <= limit:
            best = t
    return best


def _add_half(g, r1, cidx, name):
    _, _, r, n = g.shape
    tr = _row_tile(r, n, 3)

    def body(c_ref, g_ref, r_ref, o_ref):
        o_ref[...] = (g_ref[...] + r_ref[...]).astype(GRAD_TRANSIT)

    blk = lambda f: pl.BlockSpec((None, None, tr, n), f)
    gs = pltpu.PrefetchScalarGridSpec(
        num_scalar_prefetch=1, grid=(4, r // tr),
        in_specs=[blk(lambda j, i, c: (j, c[0], i, 0)), blk(lambda j, i, c: (j, 0, i, 0))],
        out_specs=pl.BlockSpec((None, tr, n), lambda j, i, c: (j, i, 0)))
    return _pc(body, grid_spec=gs, out_shape=jax.ShapeDtypeStruct((4, r, n), GRAD_TRANSIT),
               compiler_params=_cp(("parallel", "parallel")), name=name)(cidx, g, r1)


def _add_chips(p, r3, jidx, name):
    _, r, n = p.shape
    tr = _row_tile(r, n, 5)

    def body(j_ref, p_ref, a_ref, b_ref, c_ref, o_ref):
        o_ref[...] = ((p_ref[...].astype(f32) + a_ref[...].astype(f32)) + b_ref[...].astype(f32)) + c_ref[...].astype(f32)

    rblk = lambda k: pl.BlockSpec((None, None, tr, n), lambda i, j: (k, 0, i, 0))
    gs = pltpu.PrefetchScalarGridSpec(
        num_scalar_prefetch=1, grid=(r // tr,),
        in_specs=[pl.BlockSpec((None, tr, n), lambda i, j: (j[0], i, 0)), rblk(0), rblk(1), rblk(2)],
        out_specs=pl.BlockSpec((tr, n), lambda i, j: (i, 0)))
    return _pc(body, grid_spec=gs, out_shape=jax.ShapeDtypeStruct((r, n), f32),
               compiler_params=_cp(("parallel",)), name=name)(jidx, p, r3, r3, r3)


def _adamw_halves(w, mine, theirs, m, v, cidx, name):
    L, r, n = w.shape
    r2 = r // 2
    tr = _row_tile(r2, n, 11)
    c1 = 1.0 / (1.0 - ADAM_B1 ** ADAM_STEP)
    c2 = 1.0 / (1.0 - ADAM_B2 ** ADAM_STEP)

    def body(c_ref, w_ref, a0_ref, b0_ref, a1_ref, b1_ref, m_ref, v_ref, g_ref, d_ref, mo_ref, vo_ref):
        l, hf = pl.program_id(0), pl.program_id(1)
        own = hf == c_ref[0]
        gv = jnp.where(l == 0, jnp.where(own, a0_ref[...], b0_ref[...]), jnp.where(own, a1_ref[...], b1_ref[...]))
        m2 = ADAM_B1 * m_ref[...] + (1.0 - ADAM_B1) * gv
        v2 = ADAM_B2 * v_ref[...] + (1.0 - ADAM_B2) * (gv * gv)
        g_ref[...] = gv
        d_ref[...] = -ADAM_LR * ((m2 * c1) / (jnp.sqrt(v2 * c2) + ADAM_EPS) + ADAM_WD * w_ref[...])
        mo_ref[...] = m2
        vo_ref[...] = v2

    full = pl.BlockSpec((None, None, tr, n), lambda l, hf, i, c: (l, hf, i, 0))

    def half(layer, own):
        return pl.BlockSpec((tr, n), lambda l, hf, i, c: (jnp.where((l == layer) & ((hf == c[0]) == own), i, 0), 0))

    gs = pltpu.PrefetchScalarGridSpec(
        num_scalar_prefetch=1, grid=(L, 2, r2 // tr),
        in_specs=[full, half(0, True), half(0, False), half(1, True), half(1, False), full, full], out_specs=[full] * 4)
    four = lambda t: t.reshape(L, 2, r2, n)
    outs = _pc(body, grid_spec=gs, out_shape=[jax.ShapeDtypeStruct((L, 2, r2, n), f32)] * 4,
               compiler_params=_cp(("parallel", "parallel", "parallel")), name=name)(
        cidx, four(w), mine[0], theirs[0], mine[1], theirs[1], four(m), four(v))
    return [t.reshape(w.shape) for t in outs]


def _adamw_whole(w, g, m, v, name):
    c1 = 1.0 / (1.0 - ADAM_B1 ** ADAM_STEP)
    c2 = 1.0 / (1.0 - ADAM_B2 ** ADAM_STEP)

    def body(w_ref, g_ref, m_ref, v_ref, d_ref, mo_ref, vo_ref):
        gv = g_ref[...]
        m2 = ADAM_B1 * m_ref[...] + (1.0 - ADAM_B1) * gv
        v2 = ADAM_B2 * v_ref[...] + (1.0 - ADAM_B2) * (gv * gv)
        d_ref[...] = -ADAM_LR * ((m2 * c1) / (jnp.sqrt(v2 * c2) + ADAM_EPS) + ADAM_WD * w_ref[...])
        mo_ref[...] = m2
        vo_ref[...] = v2

    return _pc(body, out_shape=[jax.ShapeDtypeStruct(w.shape, f32)] * 3, name=name)(w, g, m, v)


def _full_from_gathered(name, t):
    r, n = 2 * t.shape[2], t.shape[3]
    if SHARDED[name] == 1 or name in TRANSPOSED:
        return t.reshape(4 * r, n)
    return t.reshape(4, r, n).transpose(1, 0, 2).reshape(r, 4 * n)


def _shard_major(name, g):
    R, C = g.shape
    if SHARDED[name] == 1 or name in TRANSPOSED:
        return g.reshape(4, 2, R // 8, C)
    return g.reshape(R, 4, C // 4).transpose(1, 0, 2).reshape(4, 2, R // 2, C // 4)


_SMALL_ROWS = 288


def _pack_small(d):
    flat = jnp.concatenate([d[n].reshape(-1) for n in SMALL])
    total = 2 * 4 * _SMALL_ROWS * LANES
    flat = jnp.concatenate([flat, jnp.zeros((total - flat.shape[0],), f32)])
    return flat.reshape(4, 2, _SMALL_ROWS, LANES)


def _unpack_small(t, like):
    flat = t.reshape(-1)
    out, off = {}, 0
    for n in SMALL:
        sz = math.prod(like[n].shape)
        out[n] = flat[off:off + sz].reshape(like[n].shape)
        off += sz
    return out


def kernel(x, mem, positions, norm_mix_g, w_in, q_norm_g, w_uq, kv_norm_g, w_ukv, ssm_lambda_re, ssm_lambda_im, ssm_log_dt, ssm_b_re, ssm_b_im, ssm_c_re, ssm_c_im, ssm_d, ssm_w_glu, ssm_b_glu, attn_out_g, ssm_out_g, w_out, norm_x_g, mem_norm_g, w_xq, w_xkv, w_xo, norm_ffn_g, w_gate, w_up, w_down, final_norm_g, loss_target, m_norm_mix_g, m_w_in, m_q_norm_g, m_w_uq, m_kv_norm_g, m_w_ukv, m_ssm_lambda_re, m_ssm_lambda_im, m_ssm_log_dt, m_ssm_b_re, m_ssm_b_im, m_ssm_c_re, m_ssm_c_im, m_ssm_d, m_ssm_w_glu, m_ssm_b_glu, m_attn_out_g, m_ssm_out_g, m_w_out, m_norm_x_g, m_mem_norm_g, m_w_xq, m_w_xkv, m_w_xo, m_norm_ffn_g, m_w_gate, m_w_up, m_w_down, m_final_norm_g, v_norm_mix_g, v_w_in, v_q_norm_g, v_w_uq, v_kv_norm_g, v_w_ukv, v_ssm_lambda_re, v_ssm_lambda_im, v_ssm_log_dt, v_ssm_b_re, v_ssm_b_im, v_ssm_c_re, v_ssm_c_im, v_ssm_d, v_ssm_w_glu, v_ssm_b_glu, v_attn_out_g, v_ssm_out_g, v_w_out, v_norm_x_g, v_mem_norm_g, v_w_xq, v_w_xkv, v_w_xo, v_norm_ffn_g, v_w_gate, v_w_up, v_w_down, v_final_norm_g):
    given = dict(locals())
    swap = lambda n, t: jnp.swapaxes(t, *TRANSPOSED[n]) if n in TRANSPOSED else t
    w = {n: swap(n, given[n]) for n in WEIGHTS}
    m = {n: swap(n, given["m_" + n]) for n in WEIGHTS}
    v = {n: swap(n, given["v_" + n]) for n in WEIGHTS}
    big = list(SHARDED)

    shards = {n: w[n].astype(bf16) for n in big}
    early = [n for n in big if n in EARLY_WEIGHTS]
    rest = [n for n in big if n not in EARLY_WEIGHTS]

    def full(names, results, l):
        return {n: _full_from_gathered(n, _fill_own(t, shards[n][l], False)) for n, t in zip(names, results)}

    first = _run_plan(_gather_plan([shards[n][l] for l in range(DEPTH) for n in early], False), "allgather_weights_early")
    W = {n: [None] * DEPTH for n in big}
    for l in range(DEPTH):
        for n, t in full(early, first[l * len(early):(l + 1) * len(early)], l).items():
            W[n][l] = t
    gathers = [(_gather_plan([shards[n][l] for n in rest], False), functools.partial(full, rest, l=l))
               for l in range(DEPTH)]
    W.update({n: w[n] for n in SMALL})

    cidx = lax.axis_index("c").astype(jnp.int32).reshape(1)
    jidx = (2 * lax.axis_index("x") + lax.axis_index("y")).astype(jnp.int32).reshape(1)
    ffn = [n for n in big if n in FFN_WEIGHTS]
    tail = [n for n in big if n in EARLY_WEIGHTS]
    mid = [n for n in big if n not in FFN_WEIGHTS and n not in EARLY_WEIGHTS]
    sums, got = {}, {}
    ready = []

    def exchange(names, tag, l, G, extra=()):
        keys = [(l, n) for n in names] + [(l, n) for n, _ in extra]
        gs = [_shard_major(n, G[n]) for n in names] + [g for _, g in extra]

        def done(r1):
            ps = [_add_half(g, r, cidx, f"grad_add_half_{tag}_{k[1]}") for k, g, r in zip(keys, gs, r1)]
            sums.update(zip(keys, ps))
            ready.append(_scatter_plan(ps, done=lambda results: got.update(zip(keys, results))))

        return _exchange_plan(gs, done)

    def take():
        plans = list(ready)
        ready.clear()
        return plans

    def now(plan, name):
        plan['done'](_run_plan(plan, name))

    hook = {'ffn': lambda l, G: exchange(ffn, f"l{l}_ffn", l, G), 'mid': lambda l, G: exchange(mid, f"l{l}_mid", l, G),
            'rest': lambda l, G: now(exchange(tail, f"l{l}_tail", l, G), f"grad_exchange_halves_l{l}_tail"),
            'take': take}
    loss, dx, grads = _local_step(x[0], mem[0], positions[0], loss_target[0], W, gathers=gathers, hook=hook)
    loss = lax.psum(loss, ("x", "y", "c"))

    now(exchange(tail, "l0_tail", 0, {n: grads[n][0] for n in tail}, extra=[("small", _pack_small(grads))]),
        "grad_exchange_halves_l0_tail")
    now(_merge_plans(take()), "grad_scatter_chips_l0_tail")
    keys = list(sums)
    hs = dict(zip(keys, [_add_chips(sums[k], got[k], jidx, f"grad_add_chips_l{k[0]}_{k[1]}") for k in keys]))
    ts = dict(zip(keys, _swap_sibling([hs[k] for k in keys], "grad_swap_sibling")))

    out_g, out_d, out_m, out_v = {}, {}, {}, {}
    for n in big:
        mine, theirs = [hs[(l, n)] for l in range(DEPTH)], [ts[(l, n)] for l in range(DEPTH)]
        out_g[n], out_d[n], out_m[n], out_v[n] = _adamw_halves(w[n], mine, theirs, m[n], v[n], cidx, f"adamw_{n}")
    both = jnp.stack([hs[(0, "small")], ts[(0, "small")]])
    piece = jnp.where(cidx[0] == 0, both, both[::-1]).reshape(2 * _SMALL_ROWS, LANES)
    gsm = _fill_own(_run_plan(_gather_plan([piece], False), "allgather_small")[0], piece, False)
    out_g.update(_unpack_small(gsm, w))
    for n in SMALL:
        two = lambda t: t.reshape(1, -1) if t.ndim == 1 else t
        d_, m_, v_ = _adamw_whole(two(w[n]), two(out_g[n]), two(m[n]), two(v[n]), f"adamw_{n}")
        out_d[n], out_m[n], out_v[n] = (t.reshape(w[n].shape) for t in (d_, m_, v_))

    outs = [[swap(n, d[n]) for n in WEIGHTS] for d in (out_g, out_d, out_m, out_v)]
    return (loss, dx.reshape(x.shape), *outs[0], *outs[1], *outs[2], *outs[3])
```

```python
import functools
import math

import jax
import jax.numpy as jnp
from jax import lax
from jax.experimental import pallas as pl
from jax.experimental.pallas import tpu as pltpu

f32, bf16 = jnp.float32, jnp.bfloat16

D_MODEL = 1024
DEPTH = 2
MLA_HEADS = 8
QK_NOPE = 64
QK_ROPE = 32
V_HEAD = 64
Q_LORA = 256
KV_LORA = 128
MLA_WIDTH = MLA_HEADS * V_HEAD
ROPE_THETA = 10000.0
SSM_WIDTH = 512
SSM_GROUP = 16
SSM_GROUPS = 32
SSM_STATE = 64
IN_WIDTH = Q_LORA + KV_LORA + QK_ROPE + SSM_WIDTH
X_HEADS = 4
X_HEAD_DIM = D_MODEL // X_HEADS
D_FF = 2816
EPS = 1e-6
ADAM_LR, ADAM_B1, ADAM_B2, ADAM_EPS, ADAM_WD, ADAM_STEP = 0.001, 0.9, 0.999, 1e-08, 0.01, 10

LANES = 128
SUBLANES = 8
HEAD_PAD = 128
MLA_PAD = MLA_HEADS * HEAD_PAD
SSM_MACRO = 4
MACRO_CH = SSM_WIDTH // SSM_MACRO
MACRO_ST = SSM_GROUPS // SSM_MACRO * SSM_STATE
VMEM_LIMIT = 56 * 1024 * 1024
GRAD_TRANSIT = bf16

WEIGHTS = ['norm_mix_g', 'w_in', 'q_norm_g', 'w_uq', 'kv_norm_g', 'w_ukv', 'ssm_lambda_re', 'ssm_lambda_im',
           'ssm_log_dt', 'ssm_b_re', 'ssm_b_im', 'ssm_c_re', 'ssm_c_im', 'ssm_d', 'ssm_w_glu', 'ssm_b_glu',
           'attn_out_g', 'ssm_out_g', 'w_out', 'norm_x_g', 'mem_norm_g', 'w_xq', 'w_xkv', 'w_xo', 'norm_ffn_g',
           'w_gate', 'w_up', 'w_down', 'final_norm_g']
SHARDED = {'w_in': 1, 'w_uq': 2, 'w_ukv': 2, 'ssm_w_glu': 1, 'w_out': 1, 'w_xq': 1, 'w_xkv': 2, 'w_xo': 1,
           'w_gate': 2, 'w_up': 2, 'w_down': 1}
SMALL = [n for n in WEIGHTS if n not in SHARDED]
EARLY_WEIGHTS = ('w_in', 'w_uq', 'w_ukv')
FFN_WEIGHTS = ('w_gate', 'w_up', 'w_down')
TRANSPOSED = {'w_gate': (1, 2), 'w_up': (1, 2), 'ssm_b_re': (2, 3), 'ssm_b_im': (2, 3)}
MESH = pl.DeviceIdType.MESH


def _pc(body, **kw):
    return pl.pallas_call(body, **kw)


def _pick(n, prefs):
    for p in prefs:
        if n % p == 0:
            return p
    return n


def _cp(sem=None):
    return pltpu.CompilerParams(dimension_semantics=sem, vmem_limit_bytes=VMEM_LIMIT)


_TILE_CANDS = (1024, 1408, 512, 256, 128)
MM_VMEM_BUDGET = 40 * 1024 * 1024


def _mm_tiles(M, K, N, a_bytes, b_bytes, o_bytes, npair, has_res, need_acc):
    best = None
    for tm in _TILE_CANDS:
        for tk in _TILE_CANDS:
            if M % tm or K % tk:
                continue
            vm = npair * (2 * tm * tk * a_bytes + 2 * tk * N * b_bytes) + 2 * tm * N * o_bytes
            vm += tm * N * 4 * (1 + need_acc + 2 * has_res)
            if a_bytes == 4:
                vm += npair * tm * tk * 2
            if b_bytes == 4:
                vm += npair * tk * N * 2
            if vm <= MM_VMEM_BUDGET and (best is None or tm * tk > best[0]):
                best = (tm * tk, tm, tk)
    if best is None:
        return _pick(M, (256, 128)), _pick(K, (256, 128))
    return best[1], best[2]


def _mm(pairs, mode, out_dtype, res=None, name="mm"):
    a0, b0 = pairs[0]
    if mode == 'nn':
        (M, K), N = a0.shape, b0.shape[1]
        dims = (((1,), (0,)), ((), ()))
    elif mode == 'nt':
        (M, K), N = a0.shape, b0.shape[0]
        dims = (((1,), (1,)), ((), ()))
    else:
        (K, M), N = a0.shape, b0.shape[1]
        dims = (((0,), (0,)), ((), ()))
    npair = len(pairs)
    has_res = res is not None
    direct = out_dtype == f32
    tm, tk = _mm_tiles(M, K, N, a0.dtype.itemsize, b0.dtype.itemsize, jnp.dtype(out_dtype).itemsize, npair, has_res,
                       not direct)
    nk = K // tk

    def body(*refs):
        ins = refs[:2 * npair]
        res_ref = refs[2 * npair] if has_res else None
        o_ref = refs[2 * npair + has_res]
        acc = o_ref if direct else refs[2 * npair + has_res + 1]
        k = pl.program_id(1)
        s = None
        for p in range(npair):
            d = lax.dot_general(ins[2 * p][...].astype(bf16), ins[2 * p + 1][...].astype(bf16), dims,
                                preferred_element_type=f32)
            s = d if s is None else s + d

        @pl.when(k == 0)
        def _():
            acc[...] = s

        @pl.when(k > 0)
        def _():
            acc[...] += s

        if has_res or not direct:
            @pl.when(k == nk - 1)
            def _():
                r = acc[...]
                if has_res:
                    r = r + res_ref[...]
                o_ref[...] = r.astype(out_dtype)

    if mode == 'nn':
        a_spec = pl.BlockSpec((tm, tk), lambda i, k: (i, k))
        b_spec = pl.BlockSpec((tk, N), lambda i, k: (k, 0))
    elif mode == 'nt':
        a_spec = pl.BlockSpec((tm, tk), lambda i, k: (i, k))
        b_spec = pl.BlockSpec((N, tk), lambda i, k: (0, k))
    else:
        a_spec = pl.BlockSpec((tk, tm), lambda i, k: (k, i))
        b_spec = pl.BlockSpec((tk, N), lambda i, k: (k, 0))
    o_spec = pl.BlockSpec((tm, N), lambda i, k: (i, 0))
    in_specs = [a_spec, b_spec] * npair + ([o_spec] if has_res else [])
    args = [t for p in pairs for t in p] + ([res] if has_res else [])
    return _pc(body, grid=(M // tm, nk), in_specs=in_specs, out_specs=o_spec,
               out_shape=jax.ShapeDtypeStruct((M, N), out_dtype),
               scratch_shapes=[] if direct else [pltpu.VMEM((tm, N), f32)],
               compiler_params=_cp(("parallel", "arbitrary")), name=name)(*args)


def _rms_fwd(x, g, *, col0=0, width=None, n_valid=None, out_dtype=bf16, name="rms_fwd"):
    S = x.shape[0]
    width = width or x.shape[1]
    n_valid = n_valid or width
    ts = _pick(S, (512, 256, 128))
    cb = col0 // width

    def body(x_ref, g_ref, o_ref):
        xv = x_ref[...]
        ms = jnp.sum(xv * xv, axis=-1, keepdims=True) * (1.0 / n_valid)
        o_ref[...] = (xv * lax.rsqrt(ms + EPS) * g_ref[...]).astype(out_dtype)

    return _pc(body, grid=(S // ts,),
               in_specs=[pl.BlockSpec((ts, width), lambda i: (i, cb)), pl.BlockSpec((1, width), lambda i: (0, 0))],
               out_specs=pl.BlockSpec((ts, width), lambda i: (i, 0)),
               out_shape=jax.ShapeDtypeStruct((S, width), out_dtype),
               compiler_params=_cp(("parallel",)), name=name)(x, g.reshape(1, width))


def _rms_bwd(x, g, dy, *, col0=0, dcol0=0, width=None, n_valid=None, res=None, out_dtype=f32, delta=False,
             name="rms_bwd"):
    S = x.shape[0]
    width = width or x.shape[1]
    n_valid = n_valid or width
    ts = _pick(S, (512, 256, 128))
    cb, dcb = col0 // width, dcol0 // width
    has_res = res is not None

    def body(*refs):
        x_ref, g_ref, dy_ref = refs[:3]
        res_ref = refs[3] if has_res else None
        outs = refs[3 + has_res:]
        dx_ref, dg_ref = outs[0], outs[1]
        i = pl.program_id(0)
        xv = x_ref[...]
        gv = g_ref[...]
        dyv = dy_ref[...].astype(f32)
        rstd = lax.rsqrt(jnp.sum(xv * xv, axis=-1, keepdims=True) * (1.0 / n_valid) + EPS)
        xh = xv * rstd
        dxh = dyv * gv
        mean = jnp.sum(dxh * xh, axis=-1, keepdims=True) * (1.0 / n_valid)
        dx = rstd * (dxh - xh * mean)
        if delta:
            d_ref = outs[2]
            for h in range(width // LANES):
                sl = slice(h * LANES, (h + 1) * LANES)
                dsum = jnp.sum(dx[:, sl] * xv[:, sl], axis=-1, keepdims=True)
                d_ref[:, sl] = jnp.broadcast_to(dsum, (ts, LANES))
        if has_res:
            dx = dx + res_ref[...]
        dx_ref[...] = dx.astype(out_dtype)

        @pl.when(i == 0)
        def _():
            dg_ref[...] = jnp.zeros_like(dg_ref)

        dg_ref[...] += jnp.sum(dyv * xh, axis=0, keepdims=True)

    blk = lambda c: pl.BlockSpec((ts, width), lambda i: (i, c))
    in_specs = [blk(cb), pl.BlockSpec((1, width), lambda i: (0, 0)), blk(dcb)] + ([blk(0)] if has_res else [])
    out_specs = [blk(0), pl.BlockSpec((1, width), lambda i: (0, 0))] + ([blk(0)] if delta else [])
    out_shape = [jax.ShapeDtypeStruct((S, width), out_dtype), jax.ShapeDtypeStruct((1, width), f32)] + (
        [jax.ShapeDtypeStruct((S, width), f32)] if delta else [])
    args = [x, g.reshape(1, width), dy] + ([res] if has_res else [])
    return _pc(body, grid=(S // ts,), in_specs=in_specs, out_specs=out_specs, out_shape=out_shape,
               compiler_params=_cp(("arbitrary",)), name=name)(*args)


def _loss_head(h, g, target, name="loss_head"):
    S, D = h.shape
    ts = _pick(S, (512, 256, 128))

    def body(h_ref, g_ref, t_ref, dh_ref, dg_ref, loss_ref):
        i = pl.program_id(0)
        xv = h_ref[...]
        gv = g_ref[...]
        rstd = lax.rsqrt(jnp.sum(xv * xv, axis=-1, keepdims=True) * (1.0 / D) + EPS)
        xh = xv * rstd
        err = xh * gv - t_ref[...]
        dyv = err * (1.0 / D)
        dxh = dyv * gv
        mean = jnp.sum(dxh * xh, axis=-1, keepdims=True) * (1.0 / D)
        dh_ref[...] = rstd * (dxh - xh * mean)

        @pl.when(i == 0)
        def _():
            dg_ref[...] = jnp.zeros_like(dg_ref)
            loss_ref[...] = jnp.zeros_like(loss_ref)

        dg_ref[...] += jnp.sum(dyv * xh, axis=0, keepdims=True)
        part = jnp.sum(jnp.sum(err * err, axis=-1, keepdims=True), axis=0, keepdims=True) * (0.5 / D)
        loss_ref[...] += jnp.broadcast_to(part, (1, LANES))

    blk = pl.BlockSpec((ts, D), lambda i: (i, 0))
    row = pl.BlockSpec((1, D), lambda i: (0, 0))
    return _pc(body, grid=(S // ts,), in_specs=[blk, row, blk],
               out_specs=[blk, row, pl.BlockSpec((1, LANES), lambda i: (0, 0))],
               out_shape=[jax.ShapeDtypeStruct((S, D), f32), jax.ShapeDtypeStruct((1, D), f32),
                          jax.ShapeDtypeStruct((1, LANES), f32)],
               compiler_params=_cp(("arbitrary",)), name=name)(h, g.reshape(1, D), target)


def _rope_apply(x, tc, s1, s2):
    return x * tc + pltpu.roll(x, LANES - 16, 1) * s1 + pltpu.roll(x, 16, 1) * s2


def _rope_apply_t(dy, tc, s1, s2):
    return dy * tc + pltpu.roll(dy * s1, 16, 1) + pltpu.roll(dy * s2, LANES - 16, 1)


def _rope_fwd(q, kv, proj, tabs, name="rope_fwd"):
    S = q.shape[0]
    ts = _pick(S, (512, 256, 128))
    scale = (QK_NOPE + QK_ROPE) ** -0.5

    def body(q_ref, kk_ref, kvv_ref, kr_ref, tc_ref, s1_ref, s2_ref, qh_ref, kh_ref, vh_ref):
        tc, s1, s2 = tc_ref[...], s1_ref[...], s2_ref[...]
        krr = _rope_apply(pltpu.roll(kr_ref[...], QK_NOPE, 1), tc, s1, s2)
        for h in range(MLA_HEADS):
            sl = slice(h * HEAD_PAD, (h + 1) * HEAD_PAD)
            qh_ref[:, sl] = (_rope_apply(q_ref[:, sl], tc, s1, s2) * scale).astype(bf16)
            kh_ref[:, sl] = (kk_ref[:, sl] + krr).astype(bf16)
        vh_ref[...] = kvv_ref[...].astype(bf16)

    wide = lambda c: pl.BlockSpec((ts, MLA_PAD), lambda i: (i, c))
    tab = pl.BlockSpec((ts, LANES), lambda i: (i, 0))
    return _pc(body, grid=(S // ts,),
               in_specs=[wide(0), wide(0), wide(1), pl.BlockSpec((ts, LANES), lambda i: (i, 3)), tab, tab, tab],
               out_specs=[wide(0)] * 3, out_shape=[jax.ShapeDtypeStruct((S, MLA_PAD), bf16)] * 3,
               compiler_params=_cp(("parallel",)), name=name)(q, kv, kv, proj, *tabs)


def _rope_bwd(dqh, dkh, dvh, tabs, name="rope_bwd"):
    S = dqh.shape[0]
    ts = _pick(S, (512, 256, 128))
    scale = (QK_NOPE + QK_ROPE) ** -0.5

    def body(dq_ref, dk_ref, dv_ref, tc_ref, s1_ref, s2_ref, oq_ref, okv_ref, okr_ref):
        tc, s1, s2 = tc_ref[...], s1_ref[...], s2_ref[...]
        ksum = None
        for h in range(MLA_HEADS):
            sl = slice(h * HEAD_PAD, (h + 1) * HEAD_PAD)
            oq_ref[:, sl] = (_rope_apply_t(dq_ref[:, sl], tc, s1, s2) * scale).astype(bf16)
            dk = dk_ref[:, sl]
            okv_ref[:, sl] = dk.astype(bf16)
            ksum = dk if ksum is None else ksum + dk
        okv_ref[:, MLA_PAD:] = dv_ref[...].astype(bf16)
        dkr = pltpu.roll(_rope_apply_t(ksum, tc, s1, s2), LANES - QK_NOPE, 1)
        lane = lax.broadcasted_iota(jnp.int32, (ts, LANES), 1)
        okr_ref[...] = jnp.where(lane < QK_ROPE, dkr, 0.0).astype(bf16)

    wide = pl.BlockSpec((ts, MLA_PAD), lambda i: (i, 0))
    tab = pl.BlockSpec((ts, LANES), lambda i: (i, 0))
    return _pc(body, grid=(S // ts,), in_specs=[wide, wide, wide, tab, tab, tab],
               out_specs=[wide, pl.BlockSpec((ts, 2 * MLA_PAD), lambda i: (i, 0)), tab],
               out_shape=[jax.ShapeDtypeStruct((S, MLA_PAD), bf16), jax.ShapeDtypeStruct((S, 2 * MLA_PAD), bf16),
                          jax.ShapeDtypeStruct((S, LANES), bf16)],
               compiler_params=_cp(("parallel",)), name=name)(dqh, dkh, dvh, *tabs)


ATT_BLK = 1024
_DIAG_QUARTERS = ((0, 0), (1, 0), (1, 1))


def _attn_fwd(qh, kh, vh, plan=None, name="attn_fwd"):
    S = qh.shape[0]
    tq = tk = min(S, ATT_BLK)
    nq, nk = S // tq, S // tk
    npl = plan['n'] if plan else 0

    def body(*refs):
        q_ref, k_ref, v_ref = refs[:3]
        o_ref, lse_ref = refs[3 + npl:5 + npl]
        m_sc, l_sc, acc_sc = refs[5 + 2 * npl:8 + 2 * npl]
        h, i, j = pl.program_id(0), pl.program_id(1), pl.program_id(2)
        if plan:
            pargs = (refs[3:3 + npl], refs[5 + npl:5 + 2 * npl], refs[8 + 2 * npl], refs[9 + 2 * npl])
            first = (i == 0) & (j == 0)
            pl.when((h == 0) & first)(functools.partial(plan['start'], *pargs))
            pl.when((h == (3 * MLA_HEADS) // 4) & first)(functools.partial(plan['forward'], *pargs))
            pl.when((h == MLA_HEADS - 1) & (i == nq - 1) & (j == nk - 1))(functools.partial(plan['finish'], *pargs))

        @pl.when(j == 0)
        def _():
            m_sc[...] = jnp.full_like(m_sc, -1e30)
            l_sc[...] = jnp.zeros_like(l_sc)
            acc_sc[...] = jnp.zeros_like(acc_sc)

        def part(rows, cols, n, masked):
            s = lax.dot_general(q_ref[rows, :], k_ref[cols, :], (((1,), (1,)), ((), ())), preferred_element_type=f32)
            if masked:
                row = lax.broadcasted_iota(jnp.int32, (n, n), 0)
                col = lax.broadcasted_iota(jnp.int32, (n, n), 1)
                s = jnp.where(col <= row, s, -1e30)
            m_prev = m_sc[rows, :]
            m_new = jnp.maximum(m_prev, jnp.max(s, axis=-1, keepdims=True))
            alpha = jnp.exp(m_prev - m_new)
            p = jnp.exp(s - m_new)
            l_sc[rows, :] = alpha * l_sc[rows, :] + jnp.sum(p, axis=-1, keepdims=True)
            acc_sc[rows, :] = alpha * acc_sc[rows, :] + jnp.dot(p.astype(bf16), v_ref[cols, :],
                                                                  preferred_element_type=f32)
            m_sc[rows, :] = m_new

        whole = (slice(0, tq), slice(0, tk), tq)
        pl.when(j < i)(functools.partial(part, *whole, False))
        pl.when(j == i)(functools.partial(part, *whole, True))

        @pl.when(j == nk - 1)
        def _():
            l = l_sc[...]
            o_ref[...] = acc_sc[...] / l
            lse_ref[...] = jnp.broadcast_to(m_sc[...] + jnp.log(l), (tq, LANES))

    qspec = pl.BlockSpec((tq, HEAD_PAD), lambda h, i, j: (i, h))
    kspec = pl.BlockSpec((tk, HEAD_PAD), lambda h, i, j: (jnp.minimum(j, i), h))
    anyspec = pl.BlockSpec(memory_space=pl.ANY)
    scratch = [pltpu.VMEM((tq, 1), f32), pltpu.VMEM((tq, 1), f32), pltpu.VMEM((tq, HEAD_PAD), f32)]
    if plan:
        scratch += [pltpu.SemaphoreType.DMA((plan['nsem'],)), pltpu.SemaphoreType.DMA((plan['nsem'],))]
    outs = _pc(body, grid=(MLA_HEADS, nq, nk), in_specs=[qspec, kspec, kspec] + [anyspec] * npl,
               out_specs=[qspec, qspec] + [anyspec] * npl,
               out_shape=[jax.ShapeDtypeStruct((S, MLA_PAD), f32)] * 2 + (plan['outs'] if plan else []),
               scratch_shapes=scratch,
               compiler_params=_cp(("arbitrary", "arbitrary", "arbitrary") if plan else ("parallel", "parallel", "arbitrary")),
               name=name)(qh, kh, vh, *(plan['ins'] if plan else []))
    return outs[0], outs[1], outs[2:]


def _attn_bwd(qh, kh, vh, do, lse, delta, plan=None, name="attn_bwd"):
    S = qh.shape[0]
    tq = tk = min(S, ATT_BLK)
    nq, nk = S // tq, S // tk
    npl = plan['n'] if plan else 0

    def body(*refs):
        q_ref, k_ref, v_ref, do_ref, lse_ref, dl_ref = refs[:6]
        dq_ref, dk_ref, dv_ref = refs[6 + npl:9 + npl]
        h, j, i = pl.program_id(0), pl.program_id(1), pl.program_id(2)
        if plan:
            pargs = (refs[6:6 + npl], refs[9 + npl:9 + 2 * npl], refs[9 + 2 * npl], refs[10 + 2 * npl])
            pl.when((h == 0) & (j == 0) & (i == 0))(functools.partial(plan['start'], *pargs))
            pl.when((h == MLA_HEADS - 1) & (j == nk - 1) & (i == nq - 1))(functools.partial(plan['finish'], *pargs))

        @pl.when((j == 0) & (i == 0))
        def _():
            dq_ref[...] = jnp.zeros_like(dq_ref)

        @pl.when(i == 0)
        def _():
            dk_ref[...] = jnp.zeros_like(dk_ref)
            dv_ref[...] = jnp.zeros_like(dv_ref)

        def part(r0, c0, n, masked):
            nt = (((1,), (1,)), ((), ()))
            tn = (((0,), (0,)), ((), ()))
            rows, cols = slice(r0, r0 + n), slice(c0, c0 + n)
            qv, kv_, dov = q_ref[rows, :], k_ref[cols, :], do_ref[rows, :]
            s = lax.dot_general(qv, kv_, nt, preferred_element_type=f32)
            p = jnp.exp(s - lse_ref[rows, :1])
            if masked:
                row = lax.broadcasted_iota(jnp.int32, (n, n), 0)
                col = lax.broadcasted_iota(jnp.int32, (n, n), 1)
                p = jnp.where(col <= row, p, 0.0)
            dp = lax.dot_general(dov, v_ref[cols, :], nt, preferred_element_type=f32)
            ds = (p * (dp - dl_ref[rows, :1])).astype(bf16)
            dv_ref[cols, :] += lax.dot_general(p.astype(bf16), dov, tn, preferred_element_type=f32)
            dk_ref[cols, :] += lax.dot_general(ds, qv, tn, preferred_element_type=f32)
            qrows = pl.ds(pl.multiple_of(i * tq + r0, n), n)
            dq_ref[qrows, :] += jnp.dot(ds, kv_, preferred_element_type=f32)

        def below():
            part(0, 0, tq, False)

        def diagonal():
            for r0, c0 in _DIAG_QUARTERS:
                part(r0 * hq, c0 * hq, hq, r0 == c0)

        hq = tq // 2
        pl.when(i > j)(below)
        pl.when(i == j)(diagonal)

    qspec = pl.BlockSpec((tq, HEAD_PAD), lambda h, j, i: (jnp.maximum(i, j), h))
    kspec = pl.BlockSpec((tk, HEAD_PAD), lambda h, j, i: (j, h))
    colspec = pl.BlockSpec((S, HEAD_PAD), lambda h, j, i: (0, h))
    anyspec = pl.BlockSpec(memory_space=pl.ANY)
    scratch = [pltpu.SemaphoreType.DMA((plan['nsem'],)), pltpu.SemaphoreType.DMA((plan['nsem'],))] if plan else []
    outs = _pc(body, grid=(MLA_HEADS, nk, nq), in_specs=[qspec, kspec, kspec, qspec, qspec, qspec] + [anyspec] * npl,
               out_specs=[colspec, kspec, kspec] + [anyspec] * npl,
               out_shape=[jax.ShapeDtypeStruct((S, MLA_PAD), f32)] * 3 + (plan['outs'] if plan else []),
               scratch_shapes=scratch,
               compiler_params=_cp(("arbitrary" if plan else "parallel", "arbitrary", "arbitrary")),
               name=name)(qh, kh, vh, do, lse, delta, *(plan['ins'] if plan else []))
    return outs[0], outs[1], outs[2], outs[3:]


def _xattn_fwd(q, kv, name="xattn_fwd"):
    S = q.shape[0]
    M = kv.shape[0]
    tq = _pick(S, (1024, 512, 256, 128))
    scale = X_HEAD_DIM ** -0.5

    def body(q_ref, kv_ref, o_ref):
        for h in range(X_HEADS):
            sl = slice(h * X_HEAD_DIM, (h + 1) * X_HEAD_DIM)
            k = kv_ref[:, sl]
            v = kv_ref[:, D_MODEL + h * X_HEAD_DIM:D_MODEL + (h + 1) * X_HEAD_DIM]
            s = lax.dot_general(q_ref[:, sl], k, (((1,), (1,)), ((), ())), preferred_element_type=f32) * scale
            e = jnp.exp(s - jnp.max(s, axis=-1, keepdims=True))
            p = e / jnp.sum(e, axis=-1, keepdims=True)
            o_ref[:, sl] = jnp.dot(p.astype(bf16), v, preferred_element_type=f32).astype(bf16)

    blk = pl.BlockSpec((tq, D_MODEL), lambda i: (i, 0))
    return _pc(body, grid=(S // tq,), in_specs=[blk, pl.BlockSpec((M, 2 * D_MODEL), lambda i: (0, 0))],
               out_specs=blk, out_shape=jax.ShapeDtypeStruct((S, D_MODEL), bf16),
               compiler_params=_cp(("parallel",)), name=name)(q, kv)


def _xattn_bwd(q, kv, do, plan=None, name="xattn_bwd"):
    S = q.shape[0]
    M = kv.shape[0]
    tq = _pick(S, (512, 256, 128))
    scale = X_HEAD_DIM ** -0.5
    npl = plan['n'] if plan else 0
    p_in, p_out, p_shapes, p_sems, p_args = _plan_extras(plan)

    def body(*refs):
        q_ref, kv_ref, do_ref = refs[:3]
        dq_ref, dkv_ref = refs[3 + npl:5 + npl]
        i = pl.program_id(0)
        if plan:
            pargs = (refs[3:3 + npl], refs[5 + npl:5 + 2 * npl], refs[5 + 2 * npl], refs[6 + 2 * npl])
            pl.when(i == 0)(functools.partial(plan['start'], *pargs))
            pl.when(i == S // tq - 1)(functools.partial(plan['finish'], *pargs))

        @pl.when(i == 0)
        def _():
            dkv_ref[...] = jnp.zeros_like(dkv_ref)

        nt = (((1,), (1,)), ((), ()))
        tn = (((0,), (0,)), ((), ()))
        for h in range(X_HEADS):
            sl = slice(h * X_HEAD_DIM, (h + 1) * X_HEAD_DIM)
            vsl = slice(D_MODEL + h * X_HEAD_DIM, D_MODEL + (h + 1) * X_HEAD_DIM)
            k, v, qv, dov = kv_ref[:, sl], kv_ref[:, vsl], q_ref[:, sl], do_ref[:, sl]
            s = lax.dot_general(qv, k, nt, preferred_element_type=f32) * scale
            e = jnp.exp(s - jnp.max(s, axis=-1, keepdims=True))
            p = e / jnp.sum(e, axis=-1, keepdims=True)
            dp = lax.dot_general(dov, v, nt, preferred_element_type=f32)
            ds = (p * (dp - jnp.sum(dp * p, axis=-1, keepdims=True)) * scale).astype(bf16)
            dq_ref[:, sl] = jnp.dot(ds, k, preferred_element_type=f32).astype(bf16)
            dkv_ref[:, sl] += lax.dot_general(ds, qv, tn, preferred_element_type=f32)
            dkv_ref[:, vsl] += lax.dot_general(p.astype(bf16), dov, tn, preferred_element_type=f32)

    blk = pl.BlockSpec((tq, D_MODEL), lambda i: (i, 0))
    full = pl.BlockSpec((M, 2 * D_MODEL), lambda i: (0, 0))
    outs = _pc(body, grid=(S // tq,), in_specs=[blk, full, blk] + p_in, out_specs=[blk, full] + p_out,
               out_shape=[jax.ShapeDtypeStruct((S, D_MODEL), bf16), jax.ShapeDtypeStruct((M, 2 * D_MODEL), f32)] + p_shapes,
               scratch_shapes=p_sems, compiler_params=_cp(("arbitrary",)), name=name)(q, kv, do, *p_args)
    return outs[0], outs[1], outs[2:]


ST_CHUNKS = 1


def _apow_init(a_ref, ap_ref, bp_ref, seg):
    P = MACRO_ST
    ar, ai = a_ref[:, :P], a_ref[:, P:]
    pr, pi = ar, ai
    for r in range(seg):
        ap_ref[r:r + 1, :P] = pr
        ap_ref[r:r + 1, P:] = pi
        if r < seg - 1:
            pr, pi = pr * ar - pi * ai, pr * ai + pi * ar
    br, bi = pr, pi
    for k in range(SUBLANES):
        bp_ref[k:k + 1, :P] = pr
        bp_ref[k:k + 1, P:] = pi
        pr, pi = pr * br - pi * bi, pr * bi + pi * br


def _segment_perm(tS):
    seg = tS // SUBLANES
    rows = jnp.arange(tS)
    src = (rows % SUBLANES) * seg + rows // SUBLANES
    return (src[:, None] == jnp.arange(tS)[None, :]).astype(f32)


def _unpermute_rows(pt, v):
    hi = v.astype(bf16)
    r1 = v - hi.astype(f32)
    mid = r1.astype(bf16)
    lo = (r1 - mid.astype(f32)).astype(bf16)
    out = jnp.dot(pt, jnp.concatenate([hi, mid, lo], axis=1), preferred_element_type=f32)
    w = v.shape[1]
    return (out[:, :w] + out[:, w:2 * w]) + out[:, 2 * w:]


def _scan_block(sc_ref, ap_ref, bp_ref, carry_ref, e_ref, seg, reverse):
    P = MACRO_ST
    sgn = -1.0 if reverse else 1.0
    CH = P // ST_CHUNKS
    rid = lax.broadcasted_iota(jnp.int32, (SUBLANES, CH), 0)
    for c in range(ST_CHUNKS):
        lr, li = slice(c * CH, (c + 1) * CH), slice(P + c * CH, P + (c + 1) * CH)
        ar, ai = ap_ref[0:1, lr], sgn * ap_ref[0:1, li]
        xr = xi = None
        for i in range(seg):
            r = seg - 1 - i if reverse else i
            rows = slice(SUBLANES * r, SUBLANES * (r + 1))
            sr, si = sc_ref[rows, lr], sc_ref[rows, li]
            if i == 0:
                xr, xi = sr, si
            else:
                xr, xi = ar * xr - ai * xi + sr, ar * xi + ai * xr + si
                sc_ref[rows, lr] = xr
                sc_ref[rows, li] = xi
        for sh in (1, 2, 4):
            pr, pi = bp_ref[sh - 1:sh, lr], sgn * bp_ref[sh - 1:sh, li]
            if reverse:
                tr = jnp.where(rid < SUBLANES - sh, pltpu.roll(xr, SUBLANES - sh, 0), 0.0)
                ti = jnp.where(rid < SUBLANES - sh, pltpu.roll(xi, SUBLANES - sh, 0), 0.0)
            else:
                tr = jnp.where(rid >= sh, pltpu.roll(xr, sh, 0), 0.0)
                ti = jnp.where(rid >= sh, pltpu.roll(xi, sh, 0), 0.0)
            xr, xi = xr + pr * tr - pi * ti, xi + pr * ti + pi * tr
        if reverse:
            bpr = jnp.zeros((SUBLANES, CH), f32)
            bpi = jnp.zeros((SUBLANES, CH), f32)
            for r in range(SUBLANES):
                bpr = jnp.where(rid == r, bp_ref[SUBLANES - 1 - r:SUBLANES - r, lr], bpr)
                bpi = jnp.where(rid == r, -bp_ref[SUBLANES - 1 - r:SUBLANES - r, li], bpi)
        else:
            bpr, bpi = bp_ref[:, lr], bp_ref[:, li]
        cr, cim = carry_ref[:, lr], carry_ref[:, li]
        xr, xi = xr + bpr * cr - bpi * cim, xi + bpr * cim + bpi * cr
        edge = 0 if reverse else SUBLANES - 1
        carry_ref[:, lr] = jnp.sum(jnp.where(rid == edge, xr, 0.0), axis=0, keepdims=True)
        carry_ref[:, li] = jnp.sum(jnp.where(rid == edge, xi, 0.0), axis=0, keepdims=True)
        if reverse:
            er = jnp.where(rid == SUBLANES - 1, cr, pltpu.roll(xr, SUBLANES - 1, 0))
            ei = jnp.where(rid == SUBLANES - 1, cim, pltpu.roll(xi, SUBLANES - 1, 0))
        else:
            er = jnp.where(rid == 0, cr, pltpu.roll(xr, 1, 0))
            ei = jnp.where(rid == 0, cim, pltpu.roll(xi, 1, 0))
        if e_ref is not None:
            e_ref[:, lr] = er
            e_ref[:, li] = ei
        for i in range(seg):
            r = seg - 1 - i if reverse else i
            rows = slice(SUBLANES * r, SUBLANES * (r + 1))
            pr, pi = ap_ref[i:i + 1, lr], sgn * ap_ref[i:i + 1, li]
            sc_ref[rows, lr] += pr * er - pi * ei
            sc_ref[rows, li] += pr * ei + pi * er


def _ssm_fwd(proj, bm, cm, a, d, name="ssm_fwd"):
    S = proj.shape[0]
    tS = _pick(S, (256, 128))
    nb = S // tS
    P2 = 2 * MACRO_ST
    seg = tS // SUBLANES
    ucol0 = (D_MODEL - SSM_WIDTH) // MACRO_CH

    perm = _segment_perm(tS)

    def body(u_ref, b_ref, c_ref, a_ref, d_ref, pm_ref, pt_ref, y_ref, xc_ref, bu_sc, ap_sc, bp_sc, car_sc):
        t = pl.program_id(1)

        @pl.when(t == 0)
        def _():
            _apow_init(a_ref, ap_sc, bp_sc, seg)
            car_sc[...] = jnp.zeros_like(car_sc)

        uv = u_ref[...]
        up = jnp.dot(pm_ref[...], uv.astype(bf16), preferred_element_type=f32).astype(bf16)
        bu_sc[...] = jnp.dot(up, b_ref[...], preferred_element_type=f32)
        xc_ref[...] = car_sc[...]
        _scan_block(bu_sc, ap_sc, bp_sc, car_sc, None, seg, False)
        yp = jnp.dot(bu_sc[...].astype(bf16), c_ref[...], preferred_element_type=f32)
        y_ref[...] = _unpermute_rows(pt_ref[...], yp) + d_ref[...] * uv

    sq = pl.BlockSpec((tS, tS), lambda m, t: (0, 0))
    return _pc(body, grid=(SSM_MACRO, nb),
               in_specs=[pl.BlockSpec((tS, MACRO_CH), lambda m, t: (t, ucol0 + m)),
                         pl.BlockSpec((None, MACRO_CH, P2), lambda m, t: (m, 0, 0)),
                         pl.BlockSpec((None, P2, MACRO_CH), lambda m, t: (m, 0, 0)),
                         pl.BlockSpec((None, 1, P2), lambda m, t: (m, 0, 0)),
                         pl.BlockSpec((1, MACRO_CH), lambda m, t: (0, m)), sq, sq],
               out_specs=[pl.BlockSpec((tS, MACRO_CH), lambda m, t: (t, m)),
                          pl.BlockSpec((None, None, 1, P2), lambda m, t: (m, t, 0, 0))],
               out_shape=[jax.ShapeDtypeStruct((S, SSM_WIDTH), f32), jax.ShapeDtypeStruct((SSM_MACRO, nb, 1, P2), f32)],
               scratch_shapes=[pltpu.VMEM((tS, P2), f32), pltpu.VMEM((seg, P2), f32),
                               pltpu.VMEM((SUBLANES, P2), f32), pltpu.VMEM((1, P2), f32)],
               compiler_params=_cp(("arbitrary", "arbitrary")), name=name)(
        proj, bm, cm, a, d.reshape(1, SSM_WIDTH), perm.astype(bf16), perm.T.astype(bf16))


def _ssm_bwd(proj, dy, xc, bm, cm, a, d, plan=None, name="ssm_bwd"):
    S = proj.shape[0]
    tS = _pick(S, (512, 256, 128))
    nb = S // tS
    P = MACRO_ST
    P2 = 2 * P
    seg = tS // SUBLANES
    ucol0 = (D_MODEL - SSM_WIDTH) // MACRO_CH

    perm = _segment_perm(tS)
    npl = plan['n'] if plan else 0
    p_in, p_out, p_shapes, p_sems, p_args = _plan_extras(plan)

    def body(*refs):
        u_ref, dy_ref, xc_ref, b_ref, c_ref, a_ref, d_ref, pm_ref, pt_ref = refs[:9]
        du_ref, db_ref, dc_ref, da_ref, dd_ref = refs[9 + npl:14 + npl]
        x_sc, g_sc, ap_sc, bp_sc, e_sc, xcar_sc, gcar_sc = refs[14 + 2 * npl:21 + 2 * npl]
        t = pl.program_id(1)
        if plan:
            mg = pl.program_id(0)
            pargs = (refs[9:9 + npl], refs[14 + npl:14 + 2 * npl], refs[21 + 2 * npl], refs[22 + 2 * npl])
            pl.when((mg == 0) & (t == 0))(functools.partial(plan['start'], *pargs))
            pl.when((mg == SSM_MACRO - 1) & (t == nb - 1))(functools.partial(plan['finish'], *pargs))

        @pl.when(t == 0)
        def _():
            _apow_init(a_ref, ap_sc, bp_sc, seg)
            gcar_sc[...] = jnp.zeros_like(gcar_sc)
            db_ref[...] = jnp.zeros_like(db_ref)
            dc_ref[...] = jnp.zeros_like(dc_ref)
            da_ref[...] = jnp.zeros_like(da_ref)
            dd_ref[...] = jnp.zeros_like(dd_ref)

        nt = (((1,), (1,)), ((), ()))
        tn = (((0,), (0,)), ((), ()))
        uv = u_ref[...]
        dyv = dy_ref[...]
        pm = pm_ref[...]
        ub = jnp.dot(pm, uv.astype(bf16), preferred_element_type=f32).astype(bf16)
        dyb = jnp.dot(pm, dyv.astype(bf16), preferred_element_type=f32).astype(bf16)
        x_sc[...] = jnp.dot(ub, b_ref[...], preferred_element_type=f32)
        xcar_sc[...] = xc_ref[...]
        _scan_block(x_sc, ap_sc, bp_sc, xcar_sc, e_sc, seg, False)
        g_sc[...] = lax.dot_general(dyb, c_ref[...], nt, preferred_element_type=f32)
        _scan_block(g_sc, ap_sc, bp_sc, gcar_sc, None, seg, True)
        xv = x_sc[...]
        gv = g_sc[...]
        gb = gv.astype(bf16)
        dc_ref[...] += lax.dot_general(xv.astype(bf16), dyb, tn, preferred_element_type=f32)
        db_ref[...] += lax.dot_general(ub, gb, tn, preferred_element_type=f32)
        dup = lax.dot_general(gb, b_ref[...], nt, preferred_element_type=f32)
        du_ref[...] = _unpermute_rows(pt_ref[...], dup) + d_ref[...] * dyv
        dd_ref[...] += jnp.sum(dyv * uv, axis=0, keepdims=True)
        xp = jnp.concatenate([e_sc[...], xv[:tS - SUBLANES]], axis=0)
        xpr, xpi, ggr, ggi = xp[:, :P], xp[:, P:], gv[:, :P], gv[:, P:]
        da_ref[:, :P] += jnp.sum(ggr * xpr + ggi * xpi, axis=0, keepdims=True)
        da_ref[:, P:] += jnp.sum(ggi * xpr - ggr * xpi, axis=0, keepdims=True)

    rev = lambda t: nb - 1 - t
    outs = _pc(body, grid=(SSM_MACRO, nb),
               in_specs=[pl.BlockSpec((tS, MACRO_CH), lambda m, t: (rev(t), ucol0 + m)),
                         pl.BlockSpec((tS, MACRO_CH), lambda m, t: (rev(t), m)),
                         pl.BlockSpec((None, None, 1, P2), lambda m, t: (m, rev(t) * (xc.shape[1] // nb), 0, 0)),
                         pl.BlockSpec((None, MACRO_CH, P2), lambda m, t: (m, 0, 0)),
                         pl.BlockSpec((None, P2, MACRO_CH), lambda m, t: (m, 0, 0)),
                         pl.BlockSpec((None, 1, P2), lambda m, t: (m, 0, 0)),
                         pl.BlockSpec((1, MACRO_CH), lambda m, t: (0, m)),
                         pl.BlockSpec((tS, tS), lambda m, t: (0, 0)), pl.BlockSpec((tS, tS), lambda m, t: (0, 0))] + p_in,
               out_specs=[pl.BlockSpec((tS, MACRO_CH), lambda m, t: (rev(t), m)),
                          pl.BlockSpec((None, MACRO_CH, P2), lambda m, t: (m, 0, 0)),
                          pl.BlockSpec((None, P2, MACRO_CH), lambda m, t: (m, 0, 0)),
                          pl.BlockSpec((None, 1, P2), lambda m, t: (m, 0, 0)),
                          pl.BlockSpec((1, MACRO_CH), lambda m, t: (0, m))] + p_out,
               out_shape=[jax.ShapeDtypeStruct((S, SSM_WIDTH), f32),
                          jax.ShapeDtypeStruct((SSM_MACRO, MACRO_CH, P2), f32),
                          jax.ShapeDtypeStruct((SSM_MACRO, P2, MACRO_CH), f32),
                          jax.ShapeDtypeStruct((SSM_MACRO, 1, P2), f32),
                          jax.ShapeDtypeStruct((1, SSM_WIDTH), f32)] + p_shapes,
               scratch_shapes=[pltpu.VMEM((tS, P2), f32), pltpu.VMEM((tS, P2), f32),
                               pltpu.VMEM((seg, P2), f32), pltpu.VMEM((SUBLANES, P2), f32), pltpu.VMEM((SUBLANES, P2), f32),
                               pltpu.VMEM((1, P2), f32), pltpu.VMEM((1, P2), f32)] + p_sems,
               compiler_params=_cp(("arbitrary", "arbitrary")), name=name)(
        proj, dy, xc, bm, cm, a, d.reshape(1, SSM_WIDTH), perm.astype(bf16), perm.T.astype(bf16), *p_args)
    return outs[0], outs[1], outs[2], outs[3], outs[4], outs[5:]


_GELU_K = math.sqrt(2.0 / math.pi)
_GELU_C = 0.044715


def _glu_fwd(y, w, b, g, name="glu_fwd"):
    S, W = y.shape
    ts = _pick(S, (512, 256, 128))

    def body(y_ref, w_ref, b_ref, g_ref, z_ref, sn_ref, ge_ref):
        yv = y_ref[...]
        cdf = 0.5 * (1.0 + jnp.tanh(_GELU_K * (yv + _GELU_C * (yv * yv * yv))))
        ge = (yv * cdf).astype(bf16)
        z = jnp.dot(ge, w_ref[...], preferred_element_type=f32) + b_ref[...]
        s = yv * jax.nn.sigmoid(z)
        rstd = lax.rsqrt(jnp.sum(s * s, axis=-1, keepdims=True) * (1.0 / W) + EPS)
        z_ref[...] = z
        sn_ref[...] = (s * rstd * g_ref[...]).astype(bf16)
        ge_ref[...] = ge

    blk = pl.BlockSpec((ts, W), lambda i: (i, 0))
    row = pl.BlockSpec((1, W), lambda i: (0, 0))
    return _pc(body, grid=(S // ts,), in_specs=[blk, pl.BlockSpec((W, W), lambda i: (0, 0)), row, row],
               out_specs=[blk, blk, blk],
               out_shape=[jax.ShapeDtypeStruct((S, W), f32), jax.ShapeDtypeStruct((S, W), bf16),
                          jax.ShapeDtypeStruct((S, W), bf16)],
               compiler_params=_cp(("parallel",)), name=name)(y, w, b.reshape(1, W), g.reshape(1, W))


def _glu_bwd(y, z, dmixed, w, g, name="glu_bwd"):
    S, W = y.shape
    ts = _pick(S, (512, 256, 128))
    dcb = MLA_PAD // W

    def body(y_ref, z_ref, dsn_ref, w_ref, g_ref, dy_ref, dz_ref, dg_ref, db_ref):
        i = pl.program_id(0)
        yv, zv, gv = y_ref[...], z_ref[...], g_ref[...]
        sig = jax.nn.sigmoid(zv)
        s = yv * sig
        rstd = lax.rsqrt(jnp.sum(s * s, axis=-1, keepdims=True) * (1.0 / W) + EPS)
        sh = s * rstd
        dsn = dsn_ref[...]
        dsh = dsn * gv
        ds = rstd * (dsh - sh * (jnp.sum(dsh * sh, axis=-1, keepdims=True) * (1.0 / W)))
        dz = ds * s * (1.0 - sig)
        dzb = dz.astype(bf16)
        dge = lax.dot_general(dzb, w_ref[...], (((1,), (1,)), ((), ())), preferred_element_type=f32)
        t = jnp.tanh(_GELU_K * (yv + _GELU_C * (yv * yv * yv)))
        dgelu = 0.5 * (1.0 + t) + 0.5 * yv * (1.0 - t * t) * _GELU_K * (1.0 + 3.0 * _GELU_C * yv * yv)
        dy_ref[...] = ds * sig + dge * dgelu
        dz_ref[...] = dzb

        @pl.when(i == 0)
        def _():
            dg_ref[...] = jnp.zeros_like(dg_ref)
            db_ref[...] = jnp.zeros_like(db_ref)

        dg_ref[...] += jnp.sum(dsn * sh, axis=0, keepdims=True)
        db_ref[...] += jnp.sum(dz, axis=0, keepdims=True)

    blk = pl.BlockSpec((ts, W), lambda i: (i, 0))
    row = pl.BlockSpec((1, W), lambda i: (0, 0))
    return _pc(body, grid=(S // ts,),
               in_specs=[blk, blk, pl.BlockSpec((ts, W), lambda i: (i, dcb)), pl.BlockSpec((W, W), lambda i: (0, 0)), row],
               out_specs=[blk, blk, row, row],
               out_shape=[jax.ShapeDtypeStruct((S, W), f32), jax.ShapeDtypeStruct((S, W), bf16),
                          jax.ShapeDtypeStruct((1, W), f32), jax.ShapeDtypeStruct((1, W), f32)],
               compiler_params=_cp(("arbitrary",)), name=name)(y, z, dmixed, w, g.reshape(1, W))


def _ffn_up(hn, wg, wu, name="ffn_up"):
    S, K = hn.shape
    F = wg.shape[0]
    tm, tn = _pick(S, (512, 256, 128)), _pick(F, (1408, 256, 128))

    def body(h_ref, wg_ref, wu_ref, g_ref, u_ref, a_ref):
        hv = h_ref[...]
        nt = (((1,), (1,)), ((), ()))
        gv = lax.dot_general(hv, wg_ref[...], nt, preferred_element_type=f32)
        uv = lax.dot_general(hv, wu_ref[...], nt, preferred_element_type=f32)
        g_ref[...] = gv.astype(bf16)
        u_ref[...] = uv.astype(bf16)
        a_ref[...] = (gv * jax.nn.sigmoid(gv) * uv).astype(bf16)

    wspec = pl.BlockSpec((tn, K), lambda i, j: (j, 0))
    ospec = pl.BlockSpec((tm, tn), lambda i, j: (i, j))
    return _pc(body, grid=(S // tm, F // tn), in_specs=[pl.BlockSpec((tm, K), lambda i, j: (i, 0)), wspec, wspec],
               out_specs=[ospec] * 3,
               out_shape=[jax.ShapeDtypeStruct((S, F), bf16), jax.ShapeDtypeStruct((S, F), bf16),
                          jax.ShapeDtypeStruct((S, F), bf16)],
               compiler_params=_cp(("parallel", "parallel")), name=name)(hn, wg, wu)


def _ffn_bwd_act(dh, wd, gate, up, name="ffn_bwd_act"):
    S, K = dh.shape
    F = wd.shape[0]
    tm, tn = _pick(S, (512, 256, 128)), _pick(F, (1408, 256, 128))

    def body(dh_ref, wd_ref, g_ref, u_ref, dg_ref, du_ref):
        dact = lax.dot_general(dh_ref[...].astype(bf16), wd_ref[...], (((1,), (1,)), ((), ())),
                               preferred_element_type=f32)
        gv, uv = g_ref[...].astype(f32), u_ref[...].astype(f32)
        sig = jax.nn.sigmoid(gv)
        dg_ref[...] = (dact * uv * (sig * (1.0 + gv * (1.0 - sig)))).astype(bf16)
        du_ref[...] = (dact * (gv * sig)).astype(bf16)

    ospec = pl.BlockSpec((tm, tn), lambda i, j: (i, j))
    return _pc(body, grid=(S // tm, F // tn),
               in_specs=[pl.BlockSpec((tm, K), lambda i, j: (i, 0)), pl.BlockSpec((tn, K), lambda i, j: (j, 0)),
                         ospec, ospec],
               out_specs=[ospec] * 2, out_shape=[jax.ShapeDtypeStruct((S, F), bf16)] * 2,
               compiler_params=_cp(("parallel", "parallel")), name=name)(dh, wd, gate, up)


def _pad_heads(w, per_head, pieces):
    K = w.shape[0]
    w3 = w.reshape(K, MLA_HEADS, per_head)
    out = jnp.zeros((K, MLA_HEADS, HEAD_PAD), w.dtype)
    for s0, s1, d0 in pieces:
        out = out.at[:, :, d0:d0 + (s1 - s0)].set(w3[:, :, s0:s1])
    return out.reshape(K, MLA_PAD)


def _unpad_heads(wp, per_head, pieces):
    K = wp.shape[0]
    w3 = wp.reshape(K, MLA_HEADS, HEAD_PAD)
    out = jnp.zeros((K, MLA_HEADS, per_head), wp.dtype)
    for s0, s1, d0 in pieces:
        out = out.at[:, :, s0:s1].set(w3[:, :, d0:d0 + (s1 - s0)])
    return out.reshape(K, MLA_HEADS * per_head)


_Q_PIECES = [(0, QK_NOPE + QK_ROPE, 0)]
_K_PIECES = [(0, QK_NOPE, 0)]
_V_PIECES = [(QK_NOPE, QK_NOPE + V_HEAD, 0)]
_KR0 = Q_LORA + KV_LORA


def _pack_win(w):
    z = jnp.zeros((w.shape[0], LANES - QK_ROPE), w.dtype)
    return jnp.concatenate([w[:, :_KR0 + QK_ROPE], z, w[:, _KR0 + QK_ROPE:]], axis=1)


def _unpack_win(wp):
    return jnp.concatenate([wp[:, :_KR0 + QK_ROPE], wp[:, _KR0 + LANES:]], axis=1)


def _pack_wout(w):
    wa = w[:MLA_WIDTH].reshape(MLA_HEADS, V_HEAD, D_MODEL)
    wa = jnp.concatenate([wa, jnp.zeros_like(wa)], axis=1).reshape(MLA_PAD, D_MODEL)
    return jnp.concatenate([wa, w[MLA_WIDTH:]], axis=0)


def _unpack_wout(wp):
    wa = wp[:MLA_PAD].reshape(MLA_HEADS, HEAD_PAD, D_MODEL)[:, :V_HEAD].reshape(MLA_WIDTH, D_MODEL)
    return jnp.concatenate([wa, wp[MLA_PAD:]], axis=0)


def _pad_gain(g):
    g2 = g.reshape(MLA_HEADS, V_HEAD)
    return jnp.concatenate([g2, jnp.zeros_like(g2)], axis=1).reshape(MLA_PAD)


def _unpad_gain(gp):
    return gp.reshape(MLA_HEADS, HEAD_PAD)[:, :V_HEAD].reshape(MLA_WIDTH)


def _ssm_prep(lam_re, lam_im, log_dt, b_re, b_im, c_re, c_im):
    lam = lax.complex(lam_re, lam_im)
    dt = jnp.exp(log_dt)[:, None]
    a_bar = jnp.exp(lam * dt)
    b_bar = ((a_bar - 1.0) / lam)[:, None, :] * lax.complex(b_re, b_im)
    G8 = SSM_GROUPS // SSM_MACRO
    eye = jnp.eye(G8, dtype=f32)

    def bmat(part):
        p4 = part.reshape(SSM_MACRO, G8, SSM_GROUP, SSM_STATE)
        return jnp.einsum('mgcp,gh->mgchp', p4, eye).reshape(SSM_MACRO, MACRO_CH, MACRO_ST)

    def cmat(part):
        p4 = part.reshape(SSM_MACRO, G8, SSM_GROUP, SSM_STATE)
        return jnp.einsum('mgcp,gh->mgphc', p4, eye).reshape(SSM_MACRO, MACRO_ST, MACRO_CH)

    bm = jnp.concatenate([bmat(b_bar.real), bmat(b_bar.imag)], axis=2)
    cm = jnp.concatenate([cmat(c_re), -cmat(c_im)], axis=1)
    a4 = a_bar.reshape(SSM_MACRO, 1, MACRO_ST)
    a = jnp.concatenate([a4.real, a4.imag], axis=2)
    return bm, cm, a


def _rope_tables(positions):
    freqs = ROPE_THETA ** (-jnp.arange(0, QK_ROPE, 2, dtype=f32) / QK_ROPE)
    ang = positions.astype(f32)[:, None] * freqs
    cos, sin = jnp.cos(ang), jnp.sin(ang)
    S = positions.shape[0]
    half = QK_ROPE // 2
    one, zero = jnp.ones((S, QK_NOPE), f32), jnp.zeros((S, half), f32)
    z64, z32 = jnp.zeros((S, QK_NOPE), f32), jnp.zeros((S, LANES - QK_NOPE - QK_ROPE), f32)
    tc = jnp.concatenate([one, cos, cos, z32], axis=1)
    s1 = jnp.concatenate([z64, -sin, zero, z32], axis=1)
    s2 = jnp.concatenate([z64, zero, sin, z32], axis=1)
    return tc, s1, s2


def _layer_params(W, l):
    p = {}
    p['win'] = _pack_win(W['w_in'][l])
    p['wuq'] = _pad_heads(W['w_uq'][l], QK_NOPE + QK_ROPE, _Q_PIECES)
    wukv = W['w_ukv'][l]
    p['wukv'] = jnp.concatenate([_pad_heads(wukv, QK_NOPE + V_HEAD, _K_PIECES),
                                 _pad_heads(wukv, QK_NOPE + V_HEAD, _V_PIECES)], axis=1)
    p['attn_g'] = _pad_gain(W['attn_out_g'][l])
    return p


def _forward_layer(h, memn_in, tabs, W, l, name, gather=None):
    p = _layer_params(W, l)
    sv = {'h0': h, 'p': p}
    xn = _rms_fwd(h, W['norm_mix_g'][l], name=name + "rms_mix")
    proj = _mm([(xn, p['win'])], 'nn', f32, name=name + "mm_in")
    cqn = _rms_fwd(proj, W['q_norm_g'][l], col0=0, width=Q_LORA, name=name + "rms_q")
    ckvn = _rms_fwd(proj, W['kv_norm_g'][l], col0=Q_LORA, width=KV_LORA, name=name + "rms_kv")
    q = _mm([(cqn, p['wuq'])], 'nn', f32, name=name + "mm_uq")
    kv = _mm([(ckvn, p['wukv'])], 'nn', f32, name=name + "mm_ukv")
    qh, kh, vh = _rope_fwd(q, kv, proj, tabs, name=name + "rope")
    oh, lse, carried = _attn_fwd(qh, kh, vh, plan=gather[0] if gather else None, name=name + "attn")
    if gather:
        for n, t in gather[1](carried).items():
            W[n][l] = t
    p['wout'] = _pack_wout(W['w_out'][l])
    an = _rms_fwd(oh, p['attn_g'], n_valid=MLA_WIDTH, name=name + "rms_attn")
    bm, cm, a = W['ssm'][l]
    bmb, cmb = bm.astype(bf16), cm.astype(bf16)
    y, xc = _ssm_fwd(proj, bmb, cmb, a, W['ssm_d'][l], name=name + "ssm")
    z, sn, ge = _glu_fwd(y, W['ssm_w_glu'][l], W['ssm_b_glu'][l], W['ssm_out_g'][l], name=name + "glu")
    h1a = _mm([(an, p['wout'][:MLA_PAD])], 'nn', f32, res=h, name=name + "mm_out_a")
    h1 = _mm([(sn, p['wout'][MLA_PAD:])], 'nn', f32, res=h1a, name=name + "mm_out_s")
    hn2 = _rms_fwd(h1, W['norm_x_g'][l], name=name + "rms_x")
    memn = _rms_fwd(memn_in, W['mem_norm_g'][l], name=name + "rms_mem")
    qx = _mm([(hn2, W['w_xq'][l])], 'nn', bf16, name=name + "mm_xq")
    kvx = _mm([(memn, W['w_xkv'][l])], 'nn', bf16, name=name + "mm_xkv")
    ox = _xattn_fwd(qx, kvx, name=name + "xattn")
    h2 = _mm([(ox, W['w_xo'][l])], 'nn', f32, res=h1, name=name + "mm_xo")
    hn3 = _rms_fwd(h2, W['norm_ffn_g'][l], name=name + "rms_ffn")
    gate, up, act = _ffn_up(hn3, W['w_gate'][l], W['w_up'][l], name=name + "ffn_up")
    h3 = _mm([(act, W['w_down'][l])], 'nn', f32, res=h2, name=name + "mm_down")
    sv.update(xn=xn, proj=proj, cqn=cqn, ckvn=ckvn, qh=qh, kh=kh, vh=vh, oh=oh, lse=lse, an=an, bmb=bmb, cmb=cmb,
              a=a, y=y, xc=xc, z=z, sn=sn, ge=ge, h1=h1, hn2=hn2, memn=memn, qx=qx, kvx=kvx, ox=ox, h2=h2, hn3=hn3,
              gate=gate, up=up, act=act)
    return h3, sv


def _backward_layer(dh3, sv, memn_in, tabs, W, l, hook, name):
    p = sv['p']
    G = {}
    G['w_down'] = _mm([(sv['act'], dh3)], 'tn', f32, name=name + "dw_down")
    dgate, dup = _ffn_bwd_act(dh3, W['w_down'][l], sv['gate'], sv['up'], name=name + "ffn_bwd_act")
    dhn3 = _mm([(dgate, W['w_gate'][l]), (dup, W['w_up'][l])], 'nn', f32, name=name + "mm_dffn")
    G['w_gate'] = _mm([(dgate, sv['hn3'])], 'tn', f32, name=name + "dw_gate")
    G['w_up'] = _mm([(dup, sv['hn3'])], 'tn', f32, name=name + "dw_up")
    xplan = hook['ffn'](l, G) if hook else None
    dh2, dg = _rms_bwd(sv['h2'], W['norm_ffn_g'][l], dhn3, res=dh3, name=name + "rmsb_ffn")
    G['norm_ffn_g'] = dg[0]
    G['w_xo'] = _mm([(sv['ox'], dh2)], 'tn', f32, name=name + "dw_xo")
    dox = _mm([(dh2, W['w_xo'][l])], 'nt', bf16, name=name + "mm_dxo")
    dqx, dkvx, carried = _xattn_bwd(sv['qx'], sv['kvx'], dox, plan=xplan, name=name + "xattn_bwd")
    if xplan:
        xplan['done'](carried)
    G['w_xq'] = _mm([(sv['hn2'], dqx)], 'tn', f32, name=name + "dw_xq")
    G['w_xkv'] = _mm([(sv['memn'], dkvx)], 'tn', f32, name=name + "dw_xkv")
    dhn2 = _mm([(dqx, W['w_xq'][l])], 'nt', f32, name=name + "mm_dxq")
    dmemn = _mm([(dkvx, W['w_xkv'][l])], 'nt', f32, name=name + "mm_dxkv")
    dh1, dg = _rms_bwd(sv['h1'], W['norm_x_g'][l], dhn2, res=dh2, name=name + "rmsb_x")
    G['norm_x_g'] = dg[0]
    _, dg = _rms_bwd(memn_in, W['mem_norm_g'][l], dmemn, name=name + "rmsb_mem")
    G['mem_norm_g'] = dg[0]
    dwo_a = _mm([(sv['an'], dh1)], 'tn', f32, name=name + "dw_out_a")
    dwo_s = _mm([(sv['sn'], dh1)], 'tn', f32, name=name + "dw_out_s")
    G['w_out'] = _unpack_wout(jnp.concatenate([dwo_a, dwo_s], axis=0))
    dmixed = _mm([(dh1, p['wout'])], 'nt', f32, name=name + "mm_dout")
    dy, dz, dg, db = _glu_bwd(sv['y'], sv['z'], dmixed, W['ssm_w_glu'][l], W['ssm_out_g'][l], name=name + "glu_bwd")
    G['ssm_out_g'], G['ssm_b_glu'] = dg[0], db[0]
    G['ssm_w_glu'] = _mm([(sv['ge'], dz)], 'tn', f32, name=name + "dw_glu")
    xplan = hook['mid'](l, G) if hook else None
    du, dbm, dcm, da, dd, carried = _ssm_bwd(sv['proj'], dy, sv['xc'], sv['bmb'], sv['cmb'], sv['a'], W['ssm_d'][l],
                                             plan=xplan, name=name + "ssm_bwd")
    if xplan:
        xplan['done'](carried)
    G['ssm_d'] = dd[0]
    G['ssm_raw'] = (dbm, dcm, da)
    doh, dg, delta = _rms_bwd(sv['oh'], p['attn_g'], dmixed, width=MLA_PAD, n_valid=MLA_WIDTH, delta=True,
                              out_dtype=bf16, name=name + "rmsb_attn")
    G['attn_out_g'] = _unpad_gain(dg[0])
    plans = hook['take']() if hook else []
    plan = _merge_plans(plans) if plans else None
    dqh, dkh, dvh, carried = _attn_bwd(sv['qh'], sv['kh'], sv['vh'], doh, sv['lse'], delta, plan=plan,
                                       name=name + "attn_bwd")
    if plan:
        plan['done'](carried)
    dq, dkv, dkr = _rope_bwd(dqh, dkh, dvh, tabs, name=name + "rope_bwd")
    G['w_uq'] = _unpad_heads(_mm([(sv['cqn'], dq)], 'tn', f32, name=name + "dw_uq"), QK_NOPE + QK_ROPE, _Q_PIECES)
    dwukv = _mm([(sv['ckvn'], dkv)], 'tn', f32, name=name + "dw_ukv")
    G['w_ukv'] = (_unpad_heads(dwukv[:, :MLA_PAD], QK_NOPE + V_HEAD, _K_PIECES)
                  + _unpad_heads(dwukv[:, MLA_PAD:], QK_NOPE + V_HEAD, _V_PIECES))
    dcqn = _mm([(dq, p['wuq'])], 'nt', f32, name=name + "mm_duq")
    dckvn = _mm([(dkv, p['wukv'])], 'nt', f32, name=name + "mm_dukv")
    dcq, dg = _rms_bwd(sv['proj'], W['q_norm_g'][l], dcqn, col0=0, width=Q_LORA, out_dtype=bf16, name=name + "rmsb_q")
    G['q_norm_g'] = dg[0]
    dckv, dg = _rms_bwd(sv['proj'], W['kv_norm_g'][l], dckvn, col0=Q_LORA, width=KV_LORA, out_dtype=bf16,
                        name=name + "rmsb_kv")
    G['kv_norm_g'] = dg[0]
    dproj = jnp.concatenate([dcq, dckv, dkr, du.astype(bf16)], axis=1)
    G['w_in'] = _unpack_win(_mm([(sv['xn'], dproj)], 'tn', f32, name=name + "dw_in"))
    dxn = _mm([(dproj, p['win'])], 'nt', f32, name=name + "mm_din")
    dh0, dg = _rms_bwd(sv['h0'], W['norm_mix_g'][l], dxn, res=dh1, name=name + "rmsb_mix")
    G['norm_mix_g'] = dg[0]
    return dh0, G


def _local_step(x, mem, positions, target, W, gathers=None, hook=None):
    tabs = _rope_tables(positions)
    ssm_in = [(W['ssm_lambda_re'][l], W['ssm_lambda_im'][l], W['ssm_log_dt'][l], W['ssm_b_re'][l], W['ssm_b_im'][l],
               W['ssm_c_re'][l], W['ssm_c_im'][l]) for l in range(DEPTH)]
    preps = [jax.vjp(_ssm_prep, *ssm_in[l]) for l in range(DEPTH)]
    W = dict(W)
    W['ssm'] = [preps[l][0] for l in range(DEPTH)]
    h = x
    saved = []
    for l in range(DEPTH):
        h, sv = _forward_layer(h, mem, tabs, W, l, f"l{l}_", gathers[l] if gathers else None)
        saved.append(sv)
    dh, dgf, loss = _loss_head(h, W['final_norm_g'], target)
    grads = [None] * DEPTH
    for l in reversed(range(DEPTH)):
        dh, G = _backward_layer(dh, saved[l], mem, tabs, W, l, hook, f"l{l}b_")
        dbm, dcm, da = G.pop('ssm_raw')
        names = ['ssm_lambda_re', 'ssm_lambda_im', 'ssm_log_dt', 'ssm_b_re', 'ssm_b_im', 'ssm_c_re', 'ssm_c_im']
        for n, g in zip(names, preps[l][1]((dbm, dcm, da))):
            G[n] = g
        grads[l] = G
        if hook is not None and l > 0:
            hook['rest'](l, G)
    out = {n: [grads[l][n] for l in range(DEPTH)] if n in SHARDED else jnp.stack([grads[l][n] for l in range(DEPTH)])
           for n in grads[0]}
    out['final_norm_g'] = dgf[0]
    return loss[0, 0], dh, out


_HBM = pl.BlockSpec(memory_space=pltpu.HBM)


def _me():
    return lax.axis_index("x"), lax.axis_index("y"), lax.axis_index("c")


def _chip_peers(x, y, c):
    devs = [(1 - x, y, c), (x, 1 - y, c), (1 - x, 1 - y, c)]
    return devs, [2 * d[0] + d[1] for d in devs]


def _gather_plan(xs, half_first):
    n = len(xs)
    if half_first:
        ins = [t.reshape(2, 1, *t.shape[1:]) for t in xs]
        outs = [jax.ShapeDtypeStruct((2, 4, *t.shape[1:]), t.dtype) for t in xs]
    else:
        ins = [t.reshape(1, 2, t.shape[0] // 2, t.shape[1]) for t in xs]
        outs = [jax.ShapeDtypeStruct((4, 2, t.shape[0] // 2, t.shape[1]), t.dtype) for t in xs]

    def own(ref, h):
        return ref.at[h] if half_first else ref.at[:, h]

    def slot(ref, h, j):
        return ref.at[h, pl.ds(j, 1)] if half_first else ref.at[pl.ds(j, 1), h]

    def copies(src, dst, send, recv):
        x, y, c = _me()
        jme = 2 * x + y
        devs, js = _chip_peers(x, y, c)
        half, other = pl.ds(c, 1), pl.ds(1 - c, 1)
        mk = pltpu.make_async_remote_copy
        for i in range(n):
            for k in range(3):
                out_cp = mk(own(src[i], half), slot(dst[i], half, jme), send.at[6 * i + k], recv.at[6 * i + k],
                            device_id=devs[k], device_id_type=MESH)
                in_cp = mk(own(src[i], half), slot(dst[i], half, js[k]), send.at[6 * i + k], recv.at[6 * i + k],
                           device_id=devs[k], device_id_type=MESH)
                pass_cp = mk(slot(dst[i], half, js[k]), slot(dst[i], half, js[k]), send.at[6 * i + 3 + k],
                             recv.at[6 * i + 3 + k], device_id=(x, y, 1 - c), device_id_type=MESH)
                got_cp = mk(slot(dst[i], other, js[k]), slot(dst[i], other, js[k]), send.at[6 * i + 3 + k],
                            recv.at[6 * i + 3 + k], device_id=(x, y, 1 - c), device_id_type=MESH)
                yield out_cp, in_cp, pass_cp, got_cp

    def start(*refs):
        for out_cp, _, _, _ in copies(*refs):
            out_cp.start()

    def forward(*refs):
        for _, in_cp, pass_cp, _ in copies(*refs):
            in_cp.wait_recv()
            pass_cp.start()

    def finish(*refs):
        for out_cp, _, pass_cp, got_cp in copies(*refs):
            got_cp.wait_recv()
            out_cp.wait_send()
            pass_cp.wait_send()

    return dict(n=n, ins=ins, outs=outs, nsem=6 * n, start=start, forward=forward, finish=finish)


def _plan_refs(plan, refs):
    n = plan['n']
    return refs[:n], refs[n:2 * n], refs[2 * n], refs[2 * n + 1]


def _run_plan(plan, name):
    n = plan['n']

    def body(*refs):
        args = _plan_refs(plan, refs)
        plan['start'](*args)
        plan['forward'](*args)
        plan['finish'](*args)

    return _pc(body, in_specs=[_HBM] * n, out_specs=[_HBM] * n, out_shape=plan['outs'],
               scratch_shapes=[pltpu.SemaphoreType.DMA((plan['nsem'],)), pltpu.SemaphoreType.DMA((plan['nsem'],))],
               compiler_params=pltpu.CompilerParams(has_side_effects=True), name=name)(*plan['ins'])


def _fill_own(gathered, own, half_first):
    jme = (2 * lax.axis_index("x") + lax.axis_index("y")).astype(jnp.int32)
    zero = jnp.int32(0)
    if half_first:
        return lax.dynamic_update_slice(gathered, own[:, None], (zero, jme, zero, zero))
    return lax.dynamic_update_slice(gathered, own.reshape(1, *gathered.shape[1:]), (jme, zero, zero, zero))


def _exchange_plan(gs, done=None):
    n = len(gs)

    def copies(src, dst, send, recv):
        x, y, c = _me()
        for i in range(n):
            yield pltpu.make_async_remote_copy(src[i].at[:, pl.ds(1 - c, 1)], dst[i], send.at[i], recv.at[i],
                                               device_id=(x, y, 1 - c), device_id_type=MESH)

    def start(*refs):
        for cp in copies(*refs):
            cp.start()

    def finish(*refs):
        for cp in copies(*refs):
            cp.wait()

    outs = [jax.ShapeDtypeStruct((4, 1, *g.shape[2:]), g.dtype) for g in gs]
    return dict(n=n, ins=list(gs), outs=outs, nsem=n, start=start, forward=lambda *refs: None, finish=finish, done=done)


def _plan_extras(plan):
    if not plan:
        return [], [], [], [], []
    anyspec = pl.BlockSpec(memory_space=pl.ANY)
    sems = [pltpu.SemaphoreType.DMA((plan['nsem'],)), pltpu.SemaphoreType.DMA((plan['nsem'],))]
    return [anyspec] * plan['n'], [anyspec] * plan['n'], list(plan['outs']), sems, list(plan['ins'])


def _scatter_plan(ps, done=None):
    n = len(ps)

    def copies(src, dst, send, recv, off):
        x, y, c = _me()
        devs, js = _chip_peers(x, y, c)
        for i in range(n):
            for k in range(3):
                yield pltpu.make_async_remote_copy(src[i].at[pl.ds(js[k], 1)], dst[i].at[k], send.at[off + 3 * i + k],
                                                   recv.at[off + 3 * i + k], device_id=devs[k], device_id_type=MESH)

    def start(src, dst, send, recv, off=0):
        for cp in copies(src, dst, send, recv, off):
            cp.start()

    def finish(src, dst, send, recv, off=0):
        for cp in copies(src, dst, send, recv, off):
            cp.wait()

    outs = [jax.ShapeDtypeStruct((3, 1, *p.shape[1:]), p.dtype) for p in ps]
    return dict(n=n, ins=list(ps), outs=outs, nsem=3 * n, start=start, forward=lambda *refs: None, finish=finish,
                done=done)


def _merge_plans(plans):
    def run(which):
        def f(src, dst, send, recv):
            o = s = 0
            for p in plans:
                p[which](src[o:o + p['n']], dst[o:o + p['n']], send, recv, off=s)
                o, s = o + p['n'], s + p['nsem']
        return f

    def done(results):
        o = 0
        for p in plans:
            p['done'](results[o:o + p['n']])
            o += p['n']

    return dict(n=sum(p['n'] for p in plans), ins=[t for p in plans for t in p['ins']],
                outs=[t for p in plans for t in p['outs']], nsem=sum(p['nsem'] for p in plans),
                start=run('start'), forward=lambda *refs: None, finish=run('finish'), done=done)


def _swap_sibling(hs, name):
    n = len(hs)

    def body(*refs):
        src, dst = refs[:n], refs[n:2 * n]
        send, recv = refs[2 * n:]
        x, y, c = _me()
        cps = []
        for i in range(n):
            cp = pltpu.make_async_remote_copy(src[i], dst[i], send.at[i], recv.at[i], device_id=(x, y, 1 - c),
                                              device_id_type=MESH)
            cp.start()
            cps.append(cp)
        for cp in cps:
            cp.wait()

    outs = [jax.ShapeDtypeStruct(h.shape, h.dtype) for h in hs]
    return _pc(body, in_specs=[_HBM] * n, out_specs=[_HBM] * n, out_shape=outs,
               scratch_shapes=[pltpu.SemaphoreType.DMA((n,)), pltpu.SemaphoreType.DMA((n,))],
               compiler_params=pltpu.CompilerParams(has_side_effects=True), name=name)(*hs)


ELEMWISE_VMEM_BUDGET = 24 * 1024 * 1024


def _row_tile(r, n, narrays):
    limit = ELEMWISE_VMEM_BUDGET // (2 * 4 * narrays * n)
    best = SUBLANES
    for t in range(16, r + 1, 16):
        if r % t == 0 and t <= limit:
            best = t
    return best


def _add_half(g, r1, cidx, name):
    _, _, r, n = g.shape
    tr = _row_tile(r, n, 3)

    def body(c_ref, g_ref, r_ref, o_ref):
        o_ref[...] = (g_ref[...] + r_ref[...]).astype(GRAD_TRANSIT)

    blk = lambda f: pl.BlockSpec((None, None, tr, n), f)
    gs = pltpu.PrefetchScalarGridSpec(
        num_scalar_prefetch=1, grid=(4, r // tr),
        in_specs=[blk(lambda j, i, c: (j, c[0], i, 0)), blk(lambda j, i, c: (j, 0, i, 0))],
        out_specs=pl.BlockSpec((None, tr, n), lambda j, i, c: (j, i, 0)))
    return _pc(body, grid_spec=gs, out_shape=jax.ShapeDtypeStruct((4, r, n), GRAD_TRANSIT),
               compiler_params=_cp(("parallel", "parallel")), name=name)(cidx, g, r1)


def _add_chips(p, r3, jidx, name):
    _, r, n = p.shape
    tr = _row_tile(r, n, 5)

    def body(j_ref, p_ref, a_ref, b_ref, c_ref, o_ref):
        o_ref[...] = ((p_ref[...].astype(f32) + a_ref[...].astype(f32)) + b_ref[...].astype(f32)) + c_ref[...].astype(f32)

    rblk = lambda k: pl.BlockSpec((None, None, tr, n), lambda i, j: (k, 0, i, 0))
    gs = pltpu.PrefetchScalarGridSpec(
        num_scalar_prefetch=1, grid=(r // tr,),
        in_specs=[pl.BlockSpec((None, tr, n), lambda i, j: (j[0], i, 0)), rblk(0), rblk(1), rblk(2)],
        out_specs=pl.BlockSpec((tr, n), lambda i, j: (i, 0)))
    return _pc(body, grid_spec=gs, out_shape=jax.ShapeDtypeStruct((r, n), f32),
               compiler_params=_cp(("parallel",)), name=name)(jidx, p, r3, r3, r3)


def _adamw_halves(w, mine, theirs, m, v, cidx, name):
    L, r, n = w.shape
    r2 = r // 2
    tr = _row_tile(r2, n, 11)
    c1 = 1.0 / (1.0 - ADAM_B1 ** ADAM_STEP)
    c2 = 1.0 / (1.0 - ADAM_B2 ** ADAM_STEP)

    def body(c_ref, w_ref, a0_ref, b0_ref, a1_ref, b1_ref, m_ref, v_ref, g_ref, d_ref, mo_ref, vo_ref):
        l, hf = pl.program_id(0), pl.program_id(1)
        own = hf == c_ref[0]
        gv = jnp.where(l == 0, jnp.where(own, a0_ref[...], b0_ref[...]), jnp.where(own, a1_ref[...], b1_ref[...]))
        m2 = ADAM_B1 * m_ref[...] + (1.0 - ADAM_B1) * gv
        v2 = ADAM_B2 * v_ref[...] + (1.0 - ADAM_B2) * (gv * gv)
        g_ref[...] = gv
        d_ref[...] = -ADAM_LR * ((m2 * c1) / (jnp.sqrt(v2 * c2) + ADAM_EPS) + ADAM_WD * w_ref[...])
        mo_ref[...] = m2
        vo_ref[...] = v2

    full = pl.BlockSpec((None, None, tr, n), lambda l, hf, i, c: (l, hf, i, 0))

    def half(layer, own):
        return pl.BlockSpec((tr, n), lambda l, hf, i, c: (jnp.where((l == layer) & ((hf == c[0]) == own), i, 0), 0))

    gs = pltpu.PrefetchScalarGridSpec(
        num_scalar_prefetch=1, grid=(L, 2, r2 // tr),
        in_specs=[full, half(0, True), half(0, False), half(1, True), half(1, False), full, full], out_specs=[full] * 4)
    four = lambda t: t.reshape(L, 2, r2, n)
    outs = _pc(body, grid_spec=gs, out_shape=[jax.ShapeDtypeStruct((L, 2, r2, n), f32)] * 4,
               compiler_params=_cp(("parallel", "parallel", "parallel")), name=name)(
        cidx, four(w), mine[0], theirs[0], mine[1], theirs[1], four(m), four(v))
    return [t.reshape(w.shape) for t in outs]


def _adamw_whole(w, g, m, v, name):
    c1 = 1.0 / (1.0 - ADAM_B1 ** ADAM_STEP)
    c2 = 1.0 / (1.0 - ADAM_B2 ** ADAM_STEP)

    def body(w_ref, g_ref, m_ref, v_ref, d_ref, mo_ref, vo_ref):
        gv = g_ref[...]
        m2 = ADAM_B1 * m_ref[...] + (1.0 - ADAM_B1) * gv
        v2 = ADAM_B2 * v_ref[...] + (1.0 - ADAM_B2) * (gv * gv)
        d_ref[...] = -ADAM_LR * ((m2 * c1) / (jnp.sqrt(v2 * c2) + ADAM_EPS) + ADAM_WD * w_ref[...])
        mo_ref[...] = m2
        vo_ref[...] = v2

    return _pc(body, out_shape=[jax.ShapeDtypeStruct(w.shape, f32)] * 3, name=name)(w, g, m, v)


def _full_from_gathered(name, t):
    r, n = 2 * t.shape[2], t.shape[3]
    if SHARDED[name] == 1 or name in TRANSPOSED:
        return t.reshape(4 * r, n)
    return t.reshape(4, r, n).transpose(1, 0, 2).reshape(r, 4 * n)


def _shard_major(name, g):
    R, C = g.shape
    if SHARDED[name] == 1 or name in TRANSPOSED:
        return g.reshape(4, 2, R // 8, C)
    return g.reshape(R, 4, C // 4).transpose(1, 0, 2).reshape(4, 2, R // 2, C // 4)


_SMALL_ROWS = 288


def _pack_small(d):
    flat = jnp.concatenate([d[n].reshape(-1) for n in SMALL])
    total = 2 * 4 * _SMALL_ROWS * LANES
    flat = jnp.concatenate([flat, jnp.zeros((total - flat.shape[0],), f32)])
    return flat.reshape(4, 2, _SMALL_ROWS, LANES)


def _unpack_small(t, like):
    flat = t.reshape(-1)
    out, off = {}, 0
    for n in SMALL:
        sz = math.prod(like[n].shape)
        out[n] = flat[off:off + sz].reshape(like[n].shape)
        off += sz
    return out


def kernel(x, mem, positions, norm_mix_g, w_in, q_norm_g, w_uq, kv_norm_g, w_ukv, ssm_lambda_re, ssm_lambda_im, ssm_log_dt, ssm_b_re, ssm_b_im, ssm_c_re, ssm_c_im, ssm_d, ssm_w_glu, ssm_b_glu, attn_out_g, ssm_out_g, w_out, norm_x_g, mem_norm_g, w_xq, w_xkv, w_xo, norm_ffn_g, w_gate, w_up, w_down, final_norm_g, loss_target, m_norm_mix_g, m_w_in, m_q_norm_g, m_w_uq, m_kv_norm_g, m_w_ukv, m_ssm_lambda_re, m_ssm_lambda_im, m_ssm_log_dt, m_ssm_b_re, m_ssm_b_im, m_ssm_c_re, m_ssm_c_im, m_ssm_d, m_ssm_w_glu, m_ssm_b_glu, m_attn_out_g, m_ssm_out_g, m_w_out, m_norm_x_g, m_mem_norm_g, m_w_xq, m_w_xkv, m_w_xo, m_norm_ffn_g, m_w_gate, m_w_up, m_w_down, m_final_norm_g, v_norm_mix_g, v_w_in, v_q_norm_g, v_w_uq, v_kv_norm_g, v_w_ukv, v_ssm_lambda_re, v_ssm_lambda_im, v_ssm_log_dt, v_ssm_b_re, v_ssm_b_im, v_ssm_c_re, v_ssm_c_im, v_ssm_d, v_ssm_w_glu, v_ssm_b_glu, v_attn_out_g, v_ssm_out_g, v_w_out, v_norm_x_g, v_mem_norm_g, v_w_xq, v_w_xkv, v_w_xo, v_norm_ffn_g, v_w_gate, v_w_up, v_w_down, v_final_norm_g):
    given = dict(locals())
    swap = lambda n, t: jnp.swapaxes(t, *TRANSPOSED[n]) if n in TRANSPOSED else t
    w = {n: swap(n, given[n]) for n in WEIGHTS}
    m = {n: swap(n, given["m_" + n]) for n in WEIGHTS}
    v = {n: swap(n, given["v_" + n]) for n in WEIGHTS}
    big = list(SHARDED)

    shards = {n: w[n].astype(bf16) for n in big}
    early = [n for n in big if n in EARLY_WEIGHTS]
    rest = [n for n in big if n not in EARLY_WEIGHTS]

    def full(names, results, l):
        return {n: _full_from_gathered(n, _fill_own(t, shards[n][l], False)) for n, t in zip(names, results)}

    first = _run_plan(_gather_plan([shards[n][l] for l in range(DEPTH) for n in early], False), "allgather_weights_early")
    W = {n: [None] * DEPTH for n in big}
    for l in range(DEPTH):
        for n, t in full(early, first[l * len(early):(l + 1) * len(early)], l).items():
            W[n][l] = t
    gathers = [(_gather_plan([shards[n][l] for n in rest], False), functools.partial(full, rest, l=l))
               for l in range(DEPTH)]
    W.update({n: w[n] for n in SMALL})

    cidx = lax.axis_index("c").astype(jnp.int32).reshape(1)
    jidx = (2 * lax.axis_index("x") + lax.axis_index("y")).astype(jnp.int32).reshape(1)
    ffn = [n for n in big if n in FFN_WEIGHTS]
    tail = [n for n in big if n in EARLY_WEIGHTS]
    mid = [n for n in big if n not in FFN_WEIGHTS and n not in EARLY_WEIGHTS]
    sums, got = {}, {}
    ready = []

    def exchange(names, tag, l, G, extra=()):
        keys = [(l, n) for n in names] + [(l, n) for n, _ in extra]
        gs = [_shard_major(n, G[n]) for n in names] + [g for _, g in extra]

        def done(r1):
            ps = [_add_half(g, r, cidx, f"grad_add_half_{tag}_{k[1]}") for k, g, r in zip(keys, gs, r1)]
            sums.update(zip(keys, ps))
            ready.append(_scatter_plan(ps, done=lambda results: got.update(zip(keys, results))))

        return _exchange_plan(gs, done)

    def take():
        plans = list(ready)
        ready.clear()
        return plans

    def now(plan, name):
        plan['done'](_run_plan(plan, name))

    hook = {'ffn': lambda l, G: exchange(ffn, f"l{l}_ffn", l, G), 'mid': lambda l, G: exchange(mid, f"l{l}_mid", l, G),
            'rest': lambda l, G: now(exchange(tail, f"l{l}_tail", l, G), f"grad_exchange_halves_l{l}_tail"),
            'take': take}
    loss, dx, grads = _local_step(x[0], mem[0], positions[0], loss_target[0], W, gathers=gathers, hook=hook)
    loss = lax.psum(loss, ("x", "y", "c"))

    now(exchange(tail, "l0_tail", 0, {n: grads[n][0] for n in tail}, extra=[("small", _pack_small(grads))]),
        "grad_exchange_halves_l0_tail")
    now(_merge_plans(take()), "grad_scatter_chips_l0_tail")
    keys = list(sums)
    hs = dict(zip(keys, [_add_chips(sums[k], got[k], jidx, f"grad_add_chips_l{k[0]}_{k[1]}") for k in keys]))
    ts = dict(zip(keys, _swap_sibling([hs[k] for k in keys], "grad_swap_sibling")))

    out_g, out_d, out_m, out_v = {}, {}, {}, {}
    for n in big:
        mine, theirs = [hs[(l, n)] for l in range(DEPTH)], [ts[(l, n)] for l in range(DEPTH)]
        out_g[n], out_d[n], out_m[n], out_v[n] = _adamw_halves(w[n], mine, theirs, m[n], v[n], cidx, f"adamw_{n}")
    both = jnp.stack([hs[(0, "small")], ts[(0, "small")]])
    piece = jnp.where(cidx[0] == 0, both, both[::-1]).reshape(2 * _SMALL_ROWS, LANES)
    gsm = _fill_own(_run_plan(_gather_plan([piece], False), "allgather_small")[0], piece, False)
    out_g.update(_unpack_small(gsm, w))
    for n in SMALL:
        two = lambda t: t.reshape(1, -1) if t.ndim == 1 else t
        d_, m_, v_ = _adamw_whole(two(w[n]), two(out_g[n]), two(m[n]), two(v[n]), f"adamw_{n}")
        out_d[n], out_m[n], out_v[n] = (t.reshape(w[n].shape) for t in (d_, m_, v_))

    outs = [[swap(n, d[n]) for n in WEIGHTS] for d in (out_g, out_d, out_m, out_v)]
    return (loss, dx.reshape(x.shape), *outs[0], *outs[1], *outs[2], *outs[3])
```

```python
import functools
import math

import jax
import jax.numpy as jnp
from jax import lax
from jax.experimental import pallas as pl
from jax.experimental.pallas import tpu as pltpu

f32, bf16 = jnp.float32, jnp.bfloat16

D_MODEL = 1024
DEPTH = 2
MLA_HEADS = 8
QK_NOPE = 64
QK_ROPE = 32
V_HEAD = 64
Q_LORA = 256
KV_LORA = 128
MLA_WIDTH = MLA_HEADS * V_HEAD
ROPE_THETA = 10000.0
SSM_WIDTH = 512
SSM_GROUP = 16
SSM_GROUPS = 32
SSM_STATE = 64
IN_WIDTH = Q_LORA + KV_LORA + QK_ROPE + SSM_WIDTH
X_HEADS = 4
X_HEAD_DIM = D_MODEL // X_HEADS
D_FF = 2816
EPS = 1e-6
ADAM_LR, ADAM_B1, ADAM_B2, ADAM_EPS, ADAM_WD, ADAM_STEP = 0.001, 0.9, 0.999, 1e-08, 0.01, 10

LANES = 128
SUBLANES = 8
HEAD_PAD = 128
MLA_PAD = MLA_HEADS * HEAD_PAD
SSM_MACRO = 4
MACRO_CH = SSM_WIDTH // SSM_MACRO
MACRO_ST = SSM_GROUPS // SSM_MACRO * SSM_STATE
VMEM_LIMIT = 56 * 1024 * 1024
GRAD_TRANSIT = bf16

WEIGHTS = ['norm_mix_g', 'w_in', 'q_norm_g', 'w_uq', 'kv_norm_g', 'w_ukv', 'ssm_lambda_re', 'ssm_lambda_im',
           'ssm_log_dt', 'ssm_b_re', 'ssm_b_im', 'ssm_c_re', 'ssm_c_im', 'ssm_d', 'ssm_w_glu', 'ssm_b_glu',
           'attn_out_g', 'ssm_out_g', 'w_out', 'norm_x_g', 'mem_norm_g', 'w_xq', 'w_xkv', 'w_xo', 'norm_ffn_g',
           'w_gate', 'w_up', 'w_down', 'final_norm_g']
SHARDED = {'w_in': 1, 'w_uq': 2, 'w_ukv': 2, 'ssm_w_glu': 1, 'w_out': 1, 'w_xq': 1, 'w_xkv': 2, 'w_xo': 1,
           'w_gate': 2, 'w_up': 2, 'w_down': 1}
SMALL = [n for n in WEIGHTS if n not in SHARDED]
EARLY_WEIGHTS = ('w_in', 'w_uq', 'w_ukv')
FFN_WEIGHTS = ('w_gate', 'w_up', 'w_down')
TRANSPOSED = {'w_gate': (1, 2), 'w_up': (1, 2), 'ssm_b_re': (2, 3), 'ssm_b_im': (2, 3)}
MESH = pl.DeviceIdType.MESH


def _pc(body, **kw):
    return pl.pallas_call(body, **kw)


def _pick(n, prefs):
    for p in prefs:
        if n % p == 0:
            return p
    return n


def _cp(sem=None):
    return pltpu.CompilerParams(dimension_semantics=sem, vmem_limit_bytes=VMEM_LIMIT)


_TILE_CANDS = (1024, 1408, 512, 256, 128)
MM_VMEM_BUDGET = 40 * 1024 * 1024


def _mm_tiles(M, K, N, a_bytes, b_bytes, o_bytes, npair, has_res, need_acc):
    best = None
    for tm in _TILE_CANDS:
        for tk in _TILE_CANDS:
            if M % tm or K % tk:
                continue
            vm = npair * (2 * tm * tk * a_bytes + 2 * tk * N * b_bytes) + 2 * tm * N * o_bytes
            vm += tm * N * 4 * (1 + need_acc + 2 * has_res)
            if a_bytes == 4:
                vm += npair * tm * tk * 2
            if b_bytes == 4:
                vm += npair * tk * N * 2
            if vm <= MM_VMEM_BUDGET and (best is None or tm * tk > best[0]):
                best = (tm * tk, tm, tk)
    if best is None:
        return _pick(M, (256, 128)), _pick(K, (256, 128))
    return best[1], best[2]


def _mm(pairs, mode, out_dtype, res=None, name="mm"):
    a0, b0 = pairs[0]
    if mode == 'nn':
        (M, K), N = a0.shape, b0.shape[1]
        dims = (((1,), (0,)), ((), ()))
    elif mode == 'nt':
        (M, K), N = a0.shape, b0.shape[0]
        dims = (((1,), (1,)), ((), ()))
    else:
        (K, M), N = a0.shape, b0.shape[1]
        dims = (((0,), (0,)), ((), ()))
    npair = len(pairs)
    has_res = res is not None
    direct = out_dtype == f32
    tm, tk = _mm_tiles(M, K, N, a0.dtype.itemsize, b0.dtype.itemsize, jnp.dtype(out_dtype).itemsize, npair, has_res,
                       not direct)
    nk = K // tk

    def body(*refs):
        ins = refs[:2 * npair]
        res_ref = refs[2 * npair] if has_res else None
        o_ref = refs[2 * npair + has_res]
        acc = o_ref if direct else refs[2 * npair + has_res + 1]
        k = pl.program_id(1)
        s = None
        for p in range(npair):
            d = lax.dot_general(ins[2 * p][...].astype(bf16), ins[2 * p + 1][...].astype(bf16), dims,
                                preferred_element_type=f32)
            s = d if s is None else s + d

        @pl.when(k == 0)
        def _():
            acc[...] = s

        @pl.when(k > 0)
        def _():
            acc[...] += s

        if has_res or not direct:
            @pl.when(k == nk - 1)
            def _():
                r = acc[...]
                if has_res:
                    r = r + res_ref[...]
                o_ref[...] = r.astype(out_dtype)

    if mode == 'nn':
        a_spec = pl.BlockSpec((tm, tk), lambda i, k: (i, k))
        b_spec = pl.BlockSpec((tk, N), lambda i, k: (k, 0))
    elif mode == 'nt':
        a_spec = pl.BlockSpec((tm, tk), lambda i, k: (i, k))
        b_spec = pl.BlockSpec((N, tk), lambda i, k: (0, k))
    else:
        a_spec = pl.BlockSpec((tk, tm), lambda i, k: (k, i))
        b_spec = pl.BlockSpec((tk, N), lambda i, k: (k, 0))
    o_spec = pl.BlockSpec((tm, N), lambda i, k: (i, 0))
    in_specs = [a_spec, b_spec] * npair + ([o_spec] if has_res else [])
    args = [t for p in pairs for t in p] + ([res] if has_res else [])
    return _pc(body, grid=(M // tm, nk), in_specs=in_specs, out_specs=o_spec,
               out_shape=jax.ShapeDtypeStruct((M, N), out_dtype),
               scratch_shapes=[] if direct else [pltpu.VMEM((tm, N), f32)],
               compiler_params=_cp(("parallel", "arbitrary")), name=name)(*args)


def _rms_fwd(x, g, *, col0=0, width=None, n_valid=None, out_dtype=bf16, name="rms_fwd"):
    S = x.shape[0]
    width = width or x.shape[1]
    n_valid = n_valid or width
    ts = _pick(S, (512, 256, 128))
    cb = col0 // width

    def body(x_ref, g_ref, o_ref):
        xv = x_ref[...]
        ms = jnp.sum(xv * xv, axis=-1, keepdims=True) * (1.0 / n_valid)
        o_ref[...] = (xv * lax.rsqrt(ms + EPS) * g_ref[...]).astype(out_dtype)

    return _pc(body, grid=(S // ts,),
               in_specs=[pl.BlockSpec((ts, width), lambda i: (i, cb)), pl.BlockSpec((1, width), lambda i: (0, 0))],
               out_specs=pl.BlockSpec((ts, width), lambda i: (i, 0)),
               out_shape=jax.ShapeDtypeStruct((S, width), out_dtype),
               compiler_params=_cp(("parallel",)), name=name)(x, g.reshape(1, width))


def _rms_bwd(x, g, dy, *, col0=0, dcol0=0, width=None, n_valid=None, res=None, out_dtype=f32, delta=False,
             name="rms_bwd"):
    S = x.shape[0]
    width = width or x.shape[1]
    n_valid = n_valid or width
    ts = _pick(S, (512, 256, 128))
    cb, dcb = col0 // width, dcol0 // width
    has_res = res is not None

    def body(*refs):
        x_ref, g_ref, dy_ref = refs[:3]
        res_ref = refs[3] if has_res else None
        outs = refs[3 + has_res:]
        dx_ref, dg_ref = outs[0], outs[1]
        i = pl.program_id(0)
        xv = x_ref[...]
        gv = g_ref[...]
        dyv = dy_ref[...].astype(f32)
        rstd = lax.rsqrt(jnp.sum(xv * xv, axis=-1, keepdims=True) * (1.0 / n_valid) + EPS)
        xh = xv * rstd
        dxh = dyv * gv
        mean = jnp.sum(dxh * xh, axis=-1, keepdims=True) * (1.0 / n_valid)
        dx = rstd * (dxh - xh * mean)
        if delta:
            d_ref = outs[2]
            for h in range(width // LANES):
                sl = slice(h * LANES, (h + 1) * LANES)
                dsum = jnp.sum(dx[:, sl] * xv[:, sl], axis=-1, keepdims=True)
                d_ref[:, sl] = jnp.broadcast_to(dsum, (ts, LANES))
        if has_res:
            dx = dx + res_ref[...]
        dx_ref[...] = dx.astype(out_dtype)

        @pl.when(i == 0)
        def _():
            dg_ref[...] = jnp.zeros_like(dg_ref)

        dg_ref[...] += jnp.sum(dyv * xh, axis=0, keepdims=True)

    blk = lambda c: pl.BlockSpec((ts, width), lambda i: (i, c))
    in_specs = [blk(cb), pl.BlockSpec((1, width), lambda i: (0, 0)), blk(dcb)] + ([blk(0)] if has_res else [])
    out_specs = [blk(0), pl.BlockSpec((1, width), lambda i: (0, 0))] + ([blk(0)] if delta else [])
    out_shape = [jax.ShapeDtypeStruct((S, width), out_dtype), jax.ShapeDtypeStruct((1, width), f32)] + (
        [jax.ShapeDtypeStruct((S, width), f32)] if delta else [])
    args = [x, g.reshape(1, width), dy] + ([res] if has_res else [])
    return _pc(body, grid=(S // ts,), in_specs=in_specs, out_specs=out_specs, out_shape=out_shape,
               compiler_params=_cp(("arbitrary",)), name=name)(*args)


def _loss_head(h, g, target, name="loss_head"):
    S, D = h.shape
    ts = _pick(S, (512, 256, 128))

    def body(h_ref, g_ref, t_ref, dh_ref, dg_ref, loss_ref):
        i = pl.program_id(0)
        xv = h_ref[...]
        gv = g_ref[...]
        rstd = lax.rsqrt(jnp.sum(xv * xv, axis=-1, keepdims=True) * (1.0 / D) + EPS)
        xh = xv * rstd
        err = xh * gv - t_ref[...]
        dyv = err * (1.0 / D)
        dxh = dyv * gv
        mean = jnp.sum(dxh * xh, axis=-1, keepdims=True) * (1.0 / D)
        dh_ref[...] = rstd * (dxh - xh * mean)

        @pl.when(i == 0)
        def _():
            dg_ref[...] = jnp.zeros_like(dg_ref)
            loss_ref[...] = jnp.zeros_like(loss_ref)

        dg_ref[...] += jnp.sum(dyv * xh, axis=0, keepdims=True)
        part = jnp.sum(jnp.sum(err * err, axis=-1, keepdims=True), axis=0, keepdims=True) * (0.5 / D)
        loss_ref[...] += jnp.broadcast_to(part, (1, LANES))

    blk = pl.BlockSpec((ts, D), lambda i: (i, 0))
    row = pl.BlockSpec((1, D), lambda i: (0, 0))
    return _pc(body, grid=(S // ts,), in_specs=[blk, row, blk],
               out_specs=[blk, row, pl.BlockSpec((1, LANES), lambda i: (0, 0))],
               out_shape=[jax.ShapeDtypeStruct((S, D), f32), jax.ShapeDtypeStruct((1, D), f32),
                          jax.ShapeDtypeStruct((1, LANES), f32)],
               compiler_params=_cp(("arbitrary",)), name=name)(h, g.reshape(1, D), target)


def _rope_apply(x, tc, s1, s2):
    return x * tc + pltpu.roll(x, LANES - 16, 1) * s1 + pltpu.roll(x, 16, 1) * s2


def _rope_apply_t(dy, tc, s1, s2):
    return dy * tc + pltpu.roll(dy * s1, 16, 1) + pltpu.roll(dy * s2, LANES - 16, 1)


def _rope_fwd(q, kv, proj, tabs, name="rope_fwd"):
    S = q.shape[0]
    ts = _pick(S, (512, 256, 128))
    scale = (QK_NOPE + QK_ROPE) ** -0.5

    def body(q_ref, kk_ref, kvv_ref, kr_ref, tc_ref, s1_ref, s2_ref, qh_ref, kh_ref, vh_ref):
        tc, s1, s2 = tc_ref[...], s1_ref[...], s2_ref[...]
        krr = _rope_apply(pltpu.roll(kr_ref[...], QK_NOPE, 1), tc, s1, s2)
        for h in range(MLA_HEADS):
            sl = slice(h * HEAD_PAD, (h + 1) * HEAD_PAD)
            qh_ref[:, sl] = (_rope_apply(q_ref[:, sl], tc, s1, s2) * scale).astype(bf16)
            kh_ref[:, sl] = (kk_ref[:, sl] + krr).astype(bf16)
        vh_ref[...] = kvv_ref[...].astype(bf16)

    wide = lambda c: pl.BlockSpec((ts, MLA_PAD), lambda i: (i, c))
    tab = pl.BlockSpec((ts, LANES), lambda i: (i, 0))
    return _pc(body, grid=(S // ts,),
               in_specs=[wide(0), wide(0), wide(1), pl.BlockSpec((ts, LANES), lambda i: (i, 3)), tab, tab, tab],
               out_specs=[wide(0)] * 3, out_shape=[jax.ShapeDtypeStruct((S, MLA_PAD), bf16)] * 3,
               compiler_params=_cp(("parallel",)), name=name)(q, kv, kv, proj, *tabs)


def _rope_bwd(dqh, dkh, dvh, tabs, name="rope_bwd"):
    S = dqh.shape[0]
    ts = _pick(S, (512, 256, 128))
    scale = (QK_NOPE + QK_ROPE) ** -0.5

    def body(dq_ref, dk_ref, dv_ref, tc_ref, s1_ref, s2_ref, oq_ref, okv_ref, okr_ref):
        tc, s1, s2 = tc_ref[...], s1_ref[...], s2_ref[...]
        ksum = None
        for h in range(MLA_HEADS):
            sl = slice(h * HEAD_PAD, (h + 1) * HEAD_PAD)
            oq_ref[:, sl] = (_rope_apply_t(dq_ref[:, sl], tc, s1, s2) * scale).astype(bf16)
            dk = dk_ref[:, sl]
            okv_ref[:, sl] = dk.astype(bf16)
            ksum = dk if ksum is None else ksum + dk
        okv_ref[:, MLA_PAD:] = dv_ref[...].astype(bf16)
        dkr = pltpu.roll(_rope_apply_t(ksum, tc, s1, s2), LANES - QK_NOPE, 1)
        lane = lax.broadcasted_iota(jnp.int32, (ts, LANES), 1)
        okr_ref[...] = jnp.where(lane < QK_ROPE, dkr, 0.0).astype(bf16)

    wide = pl.BlockSpec((ts, MLA_PAD), lambda i: (i, 0))
    tab = pl.BlockSpec((ts, LANES), lambda i: (i, 0))
    return _pc(body, grid=(S // ts,), in_specs=[wide, wide, wide, tab, tab, tab],
               out_specs=[wide, pl.BlockSpec((ts, 2 * MLA_PAD), lambda i: (i, 0)), tab],
               out_shape=[jax.ShapeDtypeStruct((S, MLA_PAD), bf16), jax.ShapeDtypeStruct((S, 2 * MLA_PAD), bf16),
                          jax.ShapeDtypeStruct((S, LANES), bf16)],
               compiler_params=_cp(("parallel",)), name=name)(dqh, dkh, dvh, *tabs)


ATT_BLK = 1024
_DIAG_QUARTERS = ((0, 0), (1, 0), (1, 1))


def _attn_fwd(qh, kh, vh, plan=None, name="attn_fwd"):
    S = qh.shape[0]
    tq = tk = min(S, ATT_BLK)
    nq, nk = S // tq, S // tk
    npl = plan['n'] if plan else 0

    def body(*refs):
        q_ref, k_ref, v_ref = refs[:3]
        o_ref, lse_ref = refs[3 + npl:5 + npl]
        m_sc, l_sc, acc_sc = refs[5 + 2 * npl:8 + 2 * npl]
        h, i, j = pl.program_id(0), pl.program_id(1), pl.program_id(2)
        if plan:
            pargs = (refs[3:3 + npl], refs[5 + npl:5 + 2 * npl], refs[8 + 2 * npl], refs[9 + 2 * npl])
            first = (i == 0) & (j == 0)
            pl.when((h == 0) & first)(functools.partial(plan['start'], *pargs))
            pl.when((h == (3 * MLA_HEADS) // 4) & first)(functools.partial(plan['forward'], *pargs))
            pl.when((h == MLA_HEADS - 1) & (i == nq - 1) & (j == nk - 1))(functools.partial(plan['finish'], *pargs))

        @pl.when(j == 0)
        def _():
            m_sc[...] = jnp.full_like(m_sc, -1e30)
            l_sc[...] = jnp.zeros_like(l_sc)
            acc_sc[...] = jnp.zeros_like(acc_sc)

        def part(rows, cols, n, masked):
            s = lax.dot_general(q_ref[rows, :], k_ref[cols, :], (((1,), (1,)), ((), ())), preferred_element_type=f32)
            if masked:
                row = lax.broadcasted_iota(jnp.int32, (n, n), 0)
                col = lax.broadcasted_iota(jnp.int32, (n, n), 1)
                s = jnp.where(col <= row, s, -1e30)
            m_prev = m_sc[rows, :]
            m_new = jnp.maximum(m_prev, jnp.max(s, axis=-1, keepdims=True))
            alpha = jnp.exp(m_prev - m_new)
            p = jnp.exp(s - m_new)
            l_sc[rows, :] = alpha * l_sc[rows, :] + jnp.sum(p, axis=-1, keepdims=True)
            acc_sc[rows, :] = alpha * acc_sc[rows, :] + jnp.dot(p.astype(bf16), v_ref[cols, :],
                                                                  preferred_element_type=f32)
            m_sc[rows, :] = m_new

        whole = (slice(0, tq), slice(0, tk), tq)
        pl.when(j < i)(functools.partial(part, *whole, False))
        pl.when(j == i)(functools.partial(part, *whole, True))

        @pl.when(j == nk - 1)
        def _():
            l = l_sc[...]
            o_ref[...] = acc_sc[...] / l
            lse_ref[...] = jnp.broadcast_to(m_sc[...] + jnp.log(l), (tq, LANES))

    qspec = pl.BlockSpec((tq, HEAD_PAD), lambda h, i, j: (i, h))
    kspec = pl.BlockSpec((tk, HEAD_PAD), lambda h, i, j: (jnp.minimum(j, i), h))
    anyspec = pl.BlockSpec(memory_space=pl.ANY)
    scratch = [pltpu.VMEM((tq, 1), f32), pltpu.VMEM((tq, 1), f32), pltpu.VMEM((tq, HEAD_PAD), f32)]
    if plan:
        scratch += [pltpu.SemaphoreType.DMA((plan['nsem'],)), pltpu.SemaphoreType.DMA((plan['nsem'],))]
    outs = _pc(body, grid=(MLA_HEADS, nq, nk), in_specs=[qspec, kspec, kspec] + [anyspec] * npl,
               out_specs=[qspec, qspec] + [anyspec] * npl,
               out_shape=[jax.ShapeDtypeStruct((S, MLA_PAD), f32)] * 2 + (plan['outs'] if plan else []),
               scratch_shapes=scratch,
               compiler_params=_cp(("arbitrary", "arbitrary", "arbitrary") if plan else ("parallel", "parallel", "arbitrary")),
               name=name)(qh, kh, vh, *(plan['ins'] if plan else []))
    return outs[0], outs[1], outs[2:]


def _attn_bwd(qh, kh, vh, do, lse, delta, plan=None, name="attn_bwd"):
    S = qh.shape[0]
    tq = tk = min(S, ATT_BLK)
    nq, nk = S // tq, S // tk
    npl = plan['n'] if plan else 0

    def body(*refs):
        q_ref, k_ref, v_ref, do_ref, lse_ref, dl_ref = refs[:6]
        dq_ref, dk_ref, dv_ref = refs[6 + npl:9 + npl]
        h, j, i = pl.program_id(0), pl.program_id(1), pl.program_id(2)
        if plan:
            pargs = (refs[6:6 + npl], refs[9 + npl:9 + 2 * npl], refs[9 + 2 * npl], refs[10 + 2 * npl])
            pl.when((h == 0) & (j == 0) & (i == 0))(functools.partial(plan['start'], *pargs))
            pl.when((h == MLA_HEADS - 1) & (j == nk - 1) & (i == nq - 1))(functools.partial(plan['finish'], *pargs))

        @pl.when((j == 0) & (i == 0))
        def _():
            dq_ref[...] = jnp.zeros_like(dq_ref)

        @pl.when(i == 0)
        def _():
            dk_ref[...] = jnp.zeros_like(dk_ref)
            dv_ref[...] = jnp.zeros_like(dv_ref)

        def part(r0, c0, n, masked):
            nt = (((1,), (1,)), ((), ()))
            tn = (((0,), (0,)), ((), ()))
            rows, cols = slice(r0, r0 + n), slice(c0, c0 + n)
            qv, kv_, dov = q_ref[rows, :], k_ref[cols, :], do_ref[rows, :]
            s = lax.dot_general(qv, kv_, nt, preferred_element_type=f32)
            p = jnp.exp(s - lse_ref[rows, :1])
            if masked:
                row = lax.broadcasted_iota(jnp.int32, (n, n), 0)
                col = lax.broadcasted_iota(jnp.int32, (n, n), 1)
                p = jnp.where(col <= row, p, 0.0)
            dp = lax.dot_general(dov, v_ref[cols, :], nt, preferred_element_type=f32)
            ds = (p * (dp - dl_ref[rows, :1])).astype(bf16)
            dv_ref[cols, :] += lax.dot_general(p.astype(bf16), dov, tn, preferred_element_type=f32)
            dk_ref[cols, :] += lax.dot_general(ds, qv, tn, preferred_element_type=f32)
            qrows = pl.ds(pl.multiple_of(i * tq + r0, n), n)
            dq_ref[qrows, :] += jnp.dot(ds, kv_, preferred_element_type=f32)

        def below():
            part(0, 0, tq, False)

        def diagonal():
            for r0, c0 in _DIAG_QUARTERS:
                part(r0 * hq, c0 * hq, hq, r0 == c0)

        hq = tq // 2
        pl.when(i > j)(below)
        pl.when(i == j)(diagonal)

    qspec = pl.BlockSpec((tq, HEAD_PAD), lambda h, j, i: (jnp.maximum(i, j), h))
    kspec = pl.BlockSpec((tk, HEAD_PAD), lambda h, j, i: (j, h))
    colspec = pl.BlockSpec((S, HEAD_PAD), lambda h, j, i: (0, h))
    anyspec = pl.BlockSpec(memory_space=pl.ANY)
    scratch = [pltpu.SemaphoreType.DMA((plan['nsem'],)), pltpu.SemaphoreType.DMA((plan['nsem'],))] if plan else []
    outs = _pc(body, grid=(MLA_HEADS, nk, nq), in_specs=[qspec, kspec, kspec, qspec, qspec, qspec] + [anyspec] * npl,
               out_specs=[colspec, kspec, kspec] + [anyspec] * npl,
               out_shape=[jax.ShapeDtypeStruct((S, MLA_PAD), f32)] * 3 + (plan['outs'] if plan else []),
               scratch_shapes=scratch,
               compiler_params=_cp(("arbitrary" if plan else "parallel", "arbitrary", "arbitrary")),
               name=name)(qh, kh, vh, do, lse, delta, *(plan['ins'] if plan else []))
    return outs[0], outs[1], outs[2], outs[3:]


def _xattn_fwd(q, kv, name="xattn_fwd"):
    S = q.shape[0]
    M = kv.shape[0]
    tq = _pick(S, (1024, 512, 256, 128))
    scale = X_HEAD_DIM ** -0.5

    def body(q_ref, kv_ref, o_ref):
        for h in range(X_HEADS):
            sl = slice(h * X_HEAD_DIM, (h + 1) * X_HEAD_DIM)
            k = kv_ref[:, sl]
            v = kv_ref[:, D_MODEL + h * X_HEAD_DIM:D_MODEL + (h + 1) * X_HEAD_DIM]
            s = lax.dot_general(q_ref[:, sl], k, (((1,), (1,)), ((), ())), preferred_element_type=f32) * scale
            e = jnp.exp(s - jnp.max(s, axis=-1, keepdims=True))
            p = e / jnp.sum(e, axis=-1, keepdims=True)
            o_ref[:, sl] = jnp.dot(p.astype(bf16), v, preferred_element_type=f32).astype(bf16)

    blk = pl.BlockSpec((tq, D_MODEL), lambda i: (i, 0))
    return _pc(body, grid=(S // tq,), in_specs=[blk, pl.BlockSpec((M, 2 * D_MODEL), lambda i: (0, 0))],
               out_specs=blk, out_shape=jax.ShapeDtypeStruct((S, D_MODEL), bf16),
               compiler_params=_cp(("parallel",)), name=name)(q, kv)


def _xattn_bwd(q, kv, do, plan=None, name="xattn_bwd"):
    S = q.shape[0]
    M = kv.shape[0]
    tq = _pick(S, (512, 256, 128))
    scale = X_HEAD_DIM ** -0.5
    npl = plan['n'] if plan else 0
    p_in, p_out, p_shapes, p_sems, p_args = _plan_extras(plan)

    def body(*refs):
        q_ref, kv_ref, do_ref = refs[:3]
        dq_ref, dkv_ref = refs[3 + npl:5 + npl]
        i = pl.program_id(0)
        if plan:
            pargs = (refs[3:3 + npl], refs[5 + npl:5 + 2 * npl], refs[5 + 2 * npl], refs[6 + 2 * npl])
            pl.when(i == 0)(functools.partial(plan['start'], *pargs))
            pl.when(i == S // tq - 1)(functools.partial(plan['finish'], *pargs))

        @pl.when(i == 0)
        def _():
            dkv_ref[...] = jnp.zeros_like(dkv_ref)

        nt = (((1,), (1,)), ((), ()))
        tn = (((0,), (0,)), ((), ()))
        for h in range(X_HEADS):
            sl = slice(h * X_HEAD_DIM, (h + 1) * X_HEAD_DIM)
            vsl = slice(D_MODEL + h * X_HEAD_DIM, D_MODEL + (h + 1) * X_HEAD_DIM)
            k, v, qv, dov = kv_ref[:, sl], kv_ref[:, vsl], q_ref[:, sl], do_ref[:, sl]
            s = lax.dot_general(qv, k, nt, preferred_element_type=f32) * scale
            e = jnp.exp(s - jnp.max(s, axis=-1, keepdims=True))
            p = e / jnp.sum(e, axis=-1, keepdims=True)
            dp = lax.dot_general(dov, v, nt, preferred_element_type=f32)
            ds = (p * (dp - jnp.sum(dp * p, axis=-1, keepdims=True)) * scale).astype(bf16)
            dq_ref[:, sl] = jnp.dot(ds, k, preferred_element_type=f32).astype(bf16)
            dkv_ref[:, sl] += lax.dot_general(ds, qv, tn, preferred_element_type=f32)
            dkv_ref[:, vsl] += lax.dot_general(p.astype(bf16), dov, tn, preferred_element_type=f32)

    blk = pl.BlockSpec((tq, D_MODEL), lambda i: (i, 0))
    full = pl.BlockSpec((M, 2 * D_MODEL), lambda i: (0, 0))
    outs = _pc(body, grid=(S // tq,), in_specs=[blk, full, blk] + p_in, out_specs=[blk, full] + p_out,
               out_shape=[jax.ShapeDtypeStruct((S, D_MODEL), bf16), jax.ShapeDtypeStruct((M, 2 * D_MODEL), f32)] + p_shapes,
               scratch_shapes=p_sems, compiler_params=_cp(("arbitrary",)), name=name)(q, kv, do, *p_args)
    return outs[0], outs[1], outs[2:]


ST_CHUNKS = 1


def _apow_init(a_ref, ap_ref, bp_ref, seg):
    P = MACRO_ST
    ar, ai = a_ref[:, :P], a_ref[:, P:]
    pr, pi = ar, ai
    for r in range(seg):
        ap_ref[r:r + 1, :P] = pr
        ap_ref[r:r + 1, P:] = pi
        if r < seg - 1:
            pr, pi = pr * ar - pi * ai, pr * ai + pi * ar
    br, bi = pr, pi
    for k in range(SUBLANES):
        bp_ref[k:k + 1, :P] = pr
        bp_ref[k:k + 1, P:] = pi
        pr, pi = pr * br - pi * bi, pr * bi + pi * br


def _segment_perm(tS):
    seg = tS // SUBLANES
    rows = jnp.arange(tS)
    src = (rows % SUBLANES) * seg + rows // SUBLANES
    return (src[:, None] == jnp.arange(tS)[None, :]).astype(f32)


def _unpermute_rows(pt, v):
    hi = v.astype(bf16)
    r1 = v - hi.astype(f32)
    mid = r1.astype(bf16)
    lo = (r1 - mid.astype(f32)).astype(bf16)
    out = jnp.dot(pt, jnp.concatenate([hi, mid, lo], axis=1), preferred_element_type=f32)
    w = v.shape[1]
    return (out[:, :w] + out[:, w:2 * w]) + out[:, 2 * w:]


def _scan_block(sc_ref, ap_ref, bp_ref, carry_ref, e_ref, seg, reverse):
    P = MACRO_ST
    sgn = -1.0 if reverse else 1.0
    CH = P // ST_CHUNKS
    rid = lax.broadcasted_iota(jnp.int32, (SUBLANES, CH), 0)
    for c in range(ST_CHUNKS):
        lr, li = slice(c * CH, (c + 1) * CH), slice(P + c * CH, P + (c + 1) * CH)
        ar, ai = ap_ref[0:1, lr], sgn * ap_ref[0:1, li]
        xr = xi = None
        for i in range(seg):
            r = seg - 1 - i if reverse else i
            rows = slice(SUBLANES * r, SUBLANES * (r + 1))
            sr, si = sc_ref[rows, lr], sc_ref[rows, li]
            if i == 0:
                xr, xi = sr, si
            else:
                xr, xi = ar * xr - ai * xi + sr, ar * xi + ai * xr + si
                sc_ref[rows, lr] = xr
                sc_ref[rows, li] = xi
        for sh in (1, 2, 4):
            pr, pi = bp_ref[sh - 1:sh, lr], sgn * bp_ref[sh - 1:sh, li]
            if reverse:
                tr = jnp.where(rid < SUBLANES - sh, pltpu.roll(xr, SUBLANES - sh, 0), 0.0)
                ti = jnp.where(rid < SUBLANES - sh, pltpu.roll(xi, SUBLANES - sh, 0), 0.0)
            else:
                tr = jnp.where(rid >= sh, pltpu.roll(xr, sh, 0), 0.0)
                ti = jnp.where(rid >= sh, pltpu.roll(xi, sh, 0), 0.0)
            xr, xi = xr + pr * tr - pi * ti, xi + pr * ti + pi * tr
        if reverse:
            bpr = jnp.zeros((SUBLANES, CH), f32)
            bpi = jnp.zeros((SUBLANES, CH), f32)
            for r in range(SUBLANES):
                bpr = jnp.where(rid == r, bp_ref[SUBLANES - 1 - r:SUBLANES - r, lr], bpr)
                bpi = jnp.where(rid == r, -bp_ref[SUBLANES - 1 - r:SUBLANES - r, li], bpi)
        else:
            bpr, bpi = bp_ref[:, lr], bp_ref[:, li]
        cr, cim = carry_ref[:, lr], carry_ref[:, li]
        xr, xi = xr + bpr * cr - bpi * cim, xi + bpr * cim + bpi * cr
        edge = 0 if reverse else SUBLANES - 1
        carry_ref[:, lr] = jnp.sum(jnp.where(rid == edge, xr, 0.0), axis=0, keepdims=True)
        carry_ref[:, li] = jnp.sum(jnp.where(rid == edge, xi, 0.0), axis=0, keepdims=True)
        if reverse:
            er = jnp.where(rid == SUBLANES - 1, cr, pltpu.roll(xr, SUBLANES - 1, 0))
            ei = jnp.where(rid == SUBLANES - 1, cim, pltpu.roll(xi, SUBLANES - 1, 0))
        else:
            er = jnp.where(rid == 0, cr, pltpu.roll(xr, 1, 0))
            ei = jnp.where(rid == 0, cim, pltpu.roll(xi, 1, 0))
        if e_ref is not None:
            e_ref[:, lr] = er
            e_ref[:, li] = ei
        for i in range(seg):
            r = seg - 1 - i if reverse else i
            rows = slice(SUBLANES * r, SUBLANES * (r + 1))
            pr, pi = ap_ref[i:i + 1, lr], sgn * ap_ref[i:i + 1, li]
            sc_ref[rows, lr] += pr * er - pi * ei
            sc_ref[rows, li] += pr * ei + pi * er


def _ssm_fwd(proj, bm, cm, a, d, name="ssm_fwd"):
    S = proj.shape[0]
    tS = _pick(S, (256, 128))
    nb = S // tS
    P2 = 2 * MACRO_ST
    seg = tS // SUBLANES
    ucol0 = (D_MODEL - SSM_WIDTH) // MACRO_CH

    perm = _segment_perm(tS)

    def body(u_ref, b_ref, c_ref, a_ref, d_ref, pm_ref, pt_ref, y_ref, xc_ref, bu_sc, ap_sc, bp_sc, car_sc):
        t = pl.program_id(1)

        @pl.when(t == 0)
        def _():
            _apow_init(a_ref, ap_sc, bp_sc, seg)
            car_sc[...] = jnp.zeros_like(car_sc)

        uv = u_ref[...]
        up = jnp.dot(pm_ref[...], uv.astype(bf16), preferred_element_type=f32).astype(bf16)
        bu_sc[...] = jnp.dot(up, b_ref[...], preferred_element_type=f32)
        xc_ref[...] = car_sc[...]
        _scan_block(bu_sc, ap_sc, bp_sc, car_sc, None, seg, False)
        yp = jnp.dot(bu_sc[...].astype(bf16), c_ref[...], preferred_element_type=f32)
        y_ref[...] = _unpermute_rows(pt_ref[...], yp) + d_ref[...] * uv

    sq = pl.BlockSpec((tS, tS), lambda m, t: (0, 0))
    return _pc(body, grid=(SSM_MACRO, nb),
               in_specs=[pl.BlockSpec((tS, MACRO_CH), lambda m, t: (t, ucol0 + m)),
                         pl.BlockSpec((None, MACRO_CH, P2), lambda m, t: (m, 0, 0)),
                         pl.BlockSpec((None, P2, MACRO_CH), lambda m, t: (m, 0, 0)),
                         pl.BlockSpec((None, 1, P2), lambda m, t: (m, 0, 0)),
                         pl.BlockSpec((1, MACRO_CH), lambda m, t: (0, m)), sq, sq],
               out_specs=[pl.BlockSpec((tS, MACRO_CH), lambda m, t: (t, m)),
                          pl.BlockSpec((None, None, 1, P2), lambda m, t: (m, t, 0, 0))],
               out_shape=[jax.ShapeDtypeStruct((S, SSM_WIDTH), f32), jax.ShapeDtypeStruct((SSM_MACRO, nb, 1, P2), f32)],
               scratch_shapes=[pltpu.VMEM((tS, P2), f32), pltpu.VMEM((seg, P2), f32),
                               pltpu.VMEM((SUBLANES, P2), f32), pltpu.VMEM((1, P2), f32)],
               compiler_params=_cp(("arbitrary", "arbitrary")), name=name)(
        proj, bm, cm, a, d.reshape(1, SSM_WIDTH), perm.astype(bf16), perm.T.astype(bf16))


def _ssm_bwd(proj, dy, xc, bm, cm, a, d, plan=None, name="ssm_bwd"):
    S = proj.shape[0]
    tS = _pick(S, (512, 256, 128))
    nb = S // tS
    P = MACRO_ST
    P2 = 2 * P
    seg = tS // SUBLANES
    ucol0 = (D_MODEL - SSM_WIDTH) // MACRO_CH

    perm = _segment_perm(tS)
    npl = plan['n'] if plan else 0
    p_in, p_out, p_shapes, p_sems, p_args = _plan_extras(plan)

    def body(*refs):
        u_ref, dy_ref, xc_ref, b_ref, c_ref, a_ref, d_ref, pm_ref, pt_ref = refs[:9]
        du_ref, db_ref, dc_ref, da_ref, dd_ref = refs[9 + npl:14 + npl]
        x_sc, g_sc, ap_sc, bp_sc, e_sc, xcar_sc, gcar_sc = refs[14 + 2 * npl:21 + 2 * npl]
        t = pl.program_id(1)
        if plan:
            mg = pl.program_id(0)
            pargs = (refs[9:9 + npl], refs[14 + npl:14 + 2 * npl], refs[21 + 2 * npl], refs[22 + 2 * npl])
            pl.when((mg == 0) & (t == 0))(functools.partial(plan['start'], *pargs))
            pl.when((mg == SSM_MACRO - 1) & (t == nb - 1))(functools.partial(plan['finish'], *pargs))

        @pl.when(t == 0)
        def _():
            _apow_init(a_ref, ap_sc, bp_sc, seg)
            gcar_sc[...] = jnp.zeros_like(gcar_sc)
            db_ref[...] = jnp.zeros_like(db_ref)
            dc_ref[...] = jnp.zeros_like(dc_ref)
            da_ref[...] = jnp.zeros_like(da_ref)
            dd_ref[...] = jnp.zeros_like(dd_ref)

        nt = (((1,), (1,)), ((), ()))
        tn = (((0,), (0,)), ((), ()))
        uv = u_ref[...]
        dyv = dy_ref[...]
        pm = pm_ref[...]
        ub = jnp.dot(pm, uv.astype(bf16), preferred_element_type=f32).astype(bf16)
        dyb = jnp.dot(pm, dyv.astype(bf16), preferred_element_type=f32).astype(bf16)
        x_sc[...] = jnp.dot(ub, b_ref[...], preferred_element_type=f32)
        xcar_sc[...] = xc_ref[...]
        _scan_block(x_sc, ap_sc, bp_sc, xcar_sc, e_sc, seg, False)
        g_sc[...] = lax.dot_general(dyb, c_ref[...], nt, preferred_element_type=f32)
        _scan_block(g_sc, ap_sc, bp_sc, gcar_sc, None, seg, True)
        xv = x_sc[...]
        gv = g_sc[...]
        gb = gv.astype(bf16)
        dc_ref[...] += lax.dot_general(xv.astype(bf16), dyb, tn, preferred_element_type=f32)
        db_ref[...] += lax.dot_general(ub, gb, tn, preferred_element_type=f32)
        dup = lax.dot_general(gb, b_ref[...], nt, preferred_element_type=f32)
        du_ref[...] = _unpermute_rows(pt_ref[...], dup) + d_ref[...] * dyv
        dd_ref[...] += jnp.sum(dyv * uv, axis=0, keepdims=True)
        xp = jnp.concatenate([e_sc[...], xv[:tS - SUBLANES]], axis=0)
        xpr, xpi, ggr, ggi = xp[:, :P], xp[:, P:], gv[:, :P], gv[:, P:]
        da_ref[:, :P] += jnp.sum(ggr * xpr + ggi * xpi, axis=0, keepdims=True)
        da_ref[:, P:] += jnp.sum(ggi * xpr - ggr * xpi, axis=0, keepdims=True)

    rev = lambda t: nb - 1 - t
    outs = _pc(body, grid=(SSM_MACRO, nb),
               in_specs=[pl.BlockSpec((tS, MACRO_CH), lambda m, t: (rev(t), ucol0 + m)),
                         pl.BlockSpec((tS, MACRO_CH), lambda m, t: (rev(t), m)),
                         pl.BlockSpec((None, None, 1, P2), lambda m, t: (m, rev(t) * (xc.shape[1] // nb), 0, 0)),
                         pl.BlockSpec((None, MACRO_CH, P2), lambda m, t: (m, 0, 0)),
                         pl.BlockSpec((None, P2, MACRO_CH), lambda m, t: (m, 0, 0)),
                         pl.BlockSpec((None, 1, P2), lambda m, t: (m, 0, 0)),
                         pl.BlockSpec((1, MACRO_CH), lambda m, t: (0, m)),
                         pl.BlockSpec((tS, tS), lambda m, t: (0, 0)), pl.BlockSpec((tS, tS), lambda m, t: (0, 0))] + p_in,
               out_specs=[pl.BlockSpec((tS, MACRO_CH), lambda m, t: (rev(t), m)),
                          pl.BlockSpec((None, MACRO_CH, P2), lambda m, t: (m, 0, 0)),
                          pl.BlockSpec((None, P2, MACRO_CH), lambda m, t: (m, 0, 0)),
                          pl.BlockSpec((None, 1, P2), lambda m, t: (m, 0, 0)),
                          pl.BlockSpec((1, MACRO_CH), lambda m, t: (0, m))] + p_out,
               out_shape=[jax.ShapeDtypeStruct((S, SSM_WIDTH), f32),
                          jax.ShapeDtypeStruct((SSM_MACRO, MACRO_CH, P2), f32),
                          jax.ShapeDtypeStruct((SSM_MACRO, P2, MACRO_CH), f32),
                          jax.ShapeDtypeStruct((SSM_MACRO, 1, P2), f32),
                          jax.ShapeDtypeStruct((1, SSM_WIDTH), f32)] + p_shapes,
               scratch_shapes=[pltpu.VMEM((tS, P2), f32), pltpu.VMEM((tS, P2), f32),
                               pltpu.VMEM((seg, P2), f32), pltpu.VMEM((SUBLANES, P2), f32), pltpu.VMEM((SUBLANES, P2), f32),
                               pltpu.VMEM((1, P2), f32), pltpu.VMEM((1, P2), f32)] + p_sems,
               compiler_params=_cp(("arbitrary", "arbitrary")), name=name)(
        proj, dy, xc, bm, cm, a, d.reshape(1, SSM_WIDTH), perm.astype(bf16), perm.T.astype(bf16), *p_args)
    return outs[0], outs[1], outs[2], outs[3], outs[4], outs[5:]


_GELU_K = math.sqrt(2.0 / math.pi)
_GELU_C = 0.044715


def _glu_fwd(y, w, b, g, name="glu_fwd"):
    S, W = y.shape
    ts = _pick(S, (512, 256, 128))

    def body(y_ref, w_ref, b_ref, g_ref, z_ref, sn_ref, ge_ref):
        yv = y_ref[...]
        cdf = 0.5 * (1.0 + jnp.tanh(_GELU_K * (yv + _GELU_C * (yv * yv * yv))))
        ge = (yv * cdf).astype(bf16)
        z = jnp.dot(ge, w_ref[...], preferred_element_type=f32) + b_ref[...]
        s = yv * jax.nn.sigmoid(z)
        rstd = lax.rsqrt(jnp.sum(s * s, axis=-1, keepdims=True) * (1.0 / W) + EPS)
        z_ref[...] = z
        sn_ref[...] = (s * rstd * g_ref[...]).astype(bf16)
        ge_ref[...] = ge

    blk = pl.BlockSpec((ts, W), lambda i: (i, 0))
    row = pl.BlockSpec((1, W), lambda i: (0, 0))
    return _pc(body, grid=(S // ts,), in_specs=[blk, pl.BlockSpec((W, W), lambda i: (0, 0)), row, row],
               out_specs=[blk, blk, blk],
               out_shape=[jax.ShapeDtypeStruct((S, W), f32), jax.ShapeDtypeStruct((S, W), bf16),
                          jax.ShapeDtypeStruct((S, W), bf16)],
               compiler_params=_cp(("parallel",)), name=name)(y, w, b.reshape(1, W), g.reshape(1, W))


def _glu_bwd(y, z, dmixed, w, g, name="glu_bwd"):
    S, W = y.shape
    ts = _pick(S, (512, 256, 128))
    dcb = MLA_PAD // W

    def body(y_ref, z_ref, dsn_ref, w_ref, g_ref, dy_ref, dz_ref, dg_ref, db_ref):
        i = pl.program_id(0)
        yv, zv, gv = y_ref[...], z_ref[...], g_ref[...]
        sig = jax.nn.sigmoid(zv)
        s = yv * sig
        rstd = lax.rsqrt(jnp.sum(s * s, axis=-1, keepdims=True) * (1.0 / W) + EPS)
        sh = s * rstd
        dsn = dsn_ref[...]
        dsh = dsn * gv
        ds = rstd * (dsh - sh * (jnp.sum(dsh * sh, axis=-1, keepdims=True) * (1.0 / W)))
        dz = ds * s * (1.0 - sig)
        dzb = dz.astype(bf16)
        dge = lax.dot_general(dzb, w_ref[...], (((1,), (1,)), ((), ())), preferred_element_type=f32)
        t = jnp.tanh(_GELU_K * (yv + _GELU_C * (yv * yv * yv)))
        dgelu = 0.5 * (1.0 + t) + 0.5 * yv * (1.0 - t * t) * _GELU_K * (1.0 + 3.0 * _GELU_C * yv * yv)
        dy_ref[...] = ds * sig + dge * dgelu
        dz_ref[...] = dzb

        @pl.when(i == 0)
        def _():
            dg_ref[...] = jnp.zeros_like(dg_ref)
            db_ref[...] = jnp.zeros_like(db_ref)

        dg_ref[...] += jnp.sum(dsn * sh, axis=0, keepdims=True)
        db_ref[...] += jnp.sum(dz, axis=0, keepdims=True)

    blk = pl.BlockSpec((ts, W), lambda i: (i, 0))
    row = pl.BlockSpec((1, W), lambda i: (0, 0))
    return _pc(body, grid=(S // ts,),
               in_specs=[blk, blk, pl.BlockSpec((ts, W), lambda i: (i, dcb)), pl.BlockSpec((W, W), lambda i: (0, 0)), row],
               out_specs=[blk, blk, row, row],
               out_shape=[jax.ShapeDtypeStruct((S, W), f32), jax.ShapeDtypeStruct((S, W), bf16),
                          jax.ShapeDtypeStruct((1, W), f32), jax.ShapeDtypeStruct((1, W), f32)],
               compiler_params=_cp(("arbitrary",)), name=name)(y, z, dmixed, w, g.reshape(1, W))


def _ffn_up(hn, wg, wu, name="ffn_up"):
    S, K = hn.shape
    F = wg.shape[0]
    tm, tn = _pick(S, (512, 256, 128)), _pick(F, (1408, 256, 128))

    def body(h_ref, wg_ref, wu_ref, g_ref, u_ref, a_ref):
        hv = h_ref[...]
        nt = (((1,), (1,)), ((), ()))
        gv = lax.dot_general(hv, wg_ref[...], nt, preferred_element_type=f32)
        uv = lax.dot_general(hv, wu_ref[...], nt, preferred_element_type=f32)
        g_ref[...] = gv.astype(bf16)
        u_ref[...] = uv.astype(bf16)
        a_ref[...] = (gv * jax.nn.sigmoid(gv) * uv).astype(bf16)

    wspec = pl.BlockSpec((tn, K), lambda i, j: (j, 0))
    ospec = pl.BlockSpec((tm, tn), lambda i, j: (i, j))
    return _pc(body, grid=(S // tm, F // tn), in_specs=[pl.BlockSpec((tm, K), lambda i, j: (i, 0)), wspec, wspec],
               out_specs=[ospec] * 3,
               out_shape=[jax.ShapeDtypeStruct((S, F), bf16), jax.ShapeDtypeStruct((S, F), bf16),
                          jax.ShapeDtypeStruct((S, F), bf16)],
               compiler_params=_cp(("parallel", "parallel")), name=name)(hn, wg, wu)


def _ffn_bwd_act(dh, wd, gate, up, name="ffn_bwd_act"):
    S, K = dh.shape
    F = wd.shape[0]
    tm, tn = _pick(S, (512, 256, 128)), _pick(F, (1408, 256, 128))

    def body(dh_ref, wd_ref, g_ref, u_ref, dg_ref, du_ref):
        dact = lax.dot_general(dh_ref[...].astype(bf16), wd_ref[...], (((1,), (1,)), ((), ())),
                               preferred_element_type=f32)
        gv, uv = g_ref[...].astype(f32), u_ref[...].astype(f32)
        sig = jax.nn.sigmoid(gv)
        dg_ref[...] = (dact * uv * (sig * (1.0 + gv * (1.0 - sig)))).astype(bf16)
        du_ref[...] = (dact * (gv * sig)).astype(bf16)

    ospec = pl.BlockSpec((tm, tn), lambda i, j: (i, j))
    return _pc(body, grid=(S // tm, F // tn),
               in_specs=[pl.BlockSpec((tm, K), lambda i, j: (i, 0)), pl.BlockSpec((tn, K), lambda i, j: (j, 0)),
                         ospec, ospec],
               out_specs=[ospec] * 2, out_shape=[jax.ShapeDtypeStruct((S, F), bf16)] * 2,
               compiler_params=_cp(("parallel", "parallel")), name=name)(dh, wd, gate, up)


def _pad_heads(w, per_head, pieces):
    K = w.shape[0]
    w3 = w.reshape(K, MLA_HEADS, per_head)
    out = jnp.zeros((K, MLA_HEADS, HEAD_PAD), w.dtype)
    for s0, s1, d0 in pieces:
        out = out.at[:, :, d0:d0 + (s1 - s0)].set(w3[:, :, s0:s1])
    return out.reshape(K, MLA_PAD)


def _unpad_heads(wp, per_head, pieces):
    K = wp.shape[0]
    w3 = wp.reshape(K, MLA_HEADS, HEAD_PAD)
    out = jnp.zeros((K, MLA_HEADS, per_head), wp.dtype)
    for s0, s1, d0 in pieces:
        out = out.at[:, :, s0:s1].set(w3[:, :, d0:d0 + (s1 - s0)])
    return out.reshape(K, MLA_HEADS * per_head)


_Q_PIECES = [(0, QK_NOPE + QK_ROPE, 0)]
_K_PIECES = [(0, QK_NOPE, 0)]
_V_PIECES = [(QK_NOPE, QK_NOPE + V_HEAD, 0)]
_KR0 = Q_LORA + KV_LORA


def _pack_win(w):
    z = jnp.zeros((w.shape[0], LANES - QK_ROPE), w.dtype)
    return jnp.concatenate([w[:, :_KR0 + QK_ROPE], z, w[:, _KR0 + QK_ROPE:]], axis=1)


def _unpack_win(wp):
    return jnp.concatenate([wp[:, :_KR0 + QK_ROPE], wp[:, _KR0 + LANES:]], axis=1)


def _pack_wout(w):
    wa = w[:MLA_WIDTH].reshape(MLA_HEADS, V_HEAD, D_MODEL)
    wa = jnp.concatenate([wa, jnp.zeros_like(wa)], axis=1).reshape(MLA_PAD, D_MODEL)
    return jnp.concatenate([wa, w[MLA_WIDTH:]], axis=0)


def _unpack_wout(wp):
    wa = wp[:MLA_PAD].reshape(MLA_HEADS, HEAD_PAD, D_MODEL)[:, :V_HEAD].reshape(MLA_WIDTH, D_MODEL)
    return jnp.concatenate([wa, wp[MLA_PAD:]], axis=0)


def _pad_gain(g):
    g2 = g.reshape(MLA_HEADS, V_HEAD)
    return jnp.concatenate([g2, jnp.zeros_like(g2)], axis=1).reshape(MLA_PAD)


def _unpad_gain(gp):
    return gp.reshape(MLA_HEADS, HEAD_PAD)[:, :V_HEAD].reshape(MLA_WIDTH)


def _ssm_prep(lam_re, lam_im, log_dt, b_re, b_im, c_re, c_im):
    lam = lax.complex(lam_re, lam_im)
    dt = jnp.exp(log_dt)[:, None]
    a_bar = jnp.exp(lam * dt)
    b_bar = ((a_bar - 1.0) / lam)[:, None, :] * lax.complex(b_re, b_im)
    G8 = SSM_GROUPS // SSM_MACRO
    eye = jnp.eye(G8, dtype=f32)

    def bmat(part):
        p4 = part.reshape(SSM_MACRO, G8, SSM_GROUP, SSM_STATE)
        return jnp.einsum('mgcp,gh->mgchp', p4, eye).reshape(SSM_MACRO, MACRO_CH, MACRO_ST)

    def cmat(part):
        p4 = part.reshape(SSM_MACRO, G8, SSM_GROUP, SSM_STATE)
        return jnp.einsum('mgcp,gh->mgphc', p4, eye).reshape(SSM_MACRO, MACRO_ST, MACRO_CH)

    bm = jnp.concatenate([bmat(b_bar.real), bmat(b_bar.imag)], axis=2)
    cm = jnp.concatenate([cmat(c_re), -cmat(c_im)], axis=1)
    a4 = a_bar.reshape(SSM_MACRO, 1, MACRO_ST)
    a = jnp.concatenate([a4.real, a4.imag], axis=2)
    return bm, cm, a


def _rope_tables(positions):
    freqs = ROPE_THETA ** (-jnp.arange(0, QK_ROPE, 2, dtype=f32) / QK_ROPE)
    ang = positions.astype(f32)[:, None] * freqs
    cos, sin = jnp.cos(ang), jnp.sin(ang)
    S = positions.shape[0]
    half = QK_ROPE // 2
    one, zero = jnp.ones((S, QK_NOPE), f32), jnp.zeros((S, half), f32)
    z64, z32 = jnp.zeros((S, QK_NOPE), f32), jnp.zeros((S, LANES - QK_NOPE - QK_ROPE), f32)
    tc = jnp.concatenate([one, cos, cos, z32], axis=1)
    s1 = jnp.concatenate([z64, -sin, zero, z32], axis=1)
    s2 = jnp.concatenate([z64, zero, sin, z32], axis=1)
    return tc, s1, s2


def _layer_params(W, l):
    p = {}
    p['win'] = _pack_win(W['w_in'][l])
    p['wuq'] = _pad_heads(W['w_uq'][l], QK_NOPE + QK_ROPE, _Q_PIECES)
    wukv = W['w_ukv'][l]
    p['wukv'] = jnp.concatenate([_pad_heads(wukv, QK_NOPE + V_HEAD, _K_PIECES),
                                 _pad_heads(wukv, QK_NOPE + V_HEAD, _V_PIECES)], axis=1)
    p['attn_g'] = _pad_gain(W['attn_out_g'][l])
    return p


def _forward_layer(h, memn_in, tabs, W, l, name, gather=None):
    p = _layer_params(W, l)
    sv = {'h0': h, 'p': p}
    xn = _rms_fwd(h, W['norm_mix_g'][l], name=name + "rms_mix")
    proj = _mm([(xn, p['win'])], 'nn', f32, name=name + "mm_in")
    cqn = _rms_fwd(proj, W['q_norm_g'][l], col0=0, width=Q_LORA, name=name + "rms_q")
    ckvn = _rms_fwd(proj, W['kv_norm_g'][l], col0=Q_LORA, width=KV_LORA, name=name + "rms_kv")
    q = _mm([(cqn, p['wuq'])], 'nn', f32, name=name + "mm_uq")
    kv = _mm([(ckvn, p['wukv'])], 'nn', f32, name=name + "mm_ukv")
    qh, kh, vh = _rope_fwd(q, kv, proj, tabs, name=name + "rope")
    oh, lse, carried = _attn_fwd(qh, kh, vh, plan=gather[0] if gather else None, name=name + "attn")
    if gather:
        for n, t in gather[1](carried).items():
            W[n][l] = t
    p['wout'] = _pack_wout(W['w_out'][l])
    an = _rms_fwd(oh, p['attn_g'], n_valid=MLA_WIDTH, name=name + "rms_attn")
    bm, cm, a = W['ssm'][l]
    bmb, cmb = bm.astype(bf16), cm.astype(bf16)
    y, xc = _ssm_fwd(proj, bmb, cmb, a, W['ssm_d'][l], name=name + "ssm")
    z, sn, ge = _glu_fwd(y, W['ssm_w_glu'][l], W['ssm_b_glu'][l], W['ssm_out_g'][l], name=name + "glu")
    h1a = _mm([(an, p['wout'][:MLA_PAD])], 'nn', f32, res=h, name=name + "mm_out_a")
    h1 = _mm([(sn, p['wout'][MLA_PAD:])], 'nn', f32, res=h1a, name=name + "mm_out_s")
    hn2 = _rms_fwd(h1, W['norm_x_g'][l], name=name + "rms_x")
    memn = _rms_fwd(memn_in, W['mem_norm_g'][l], name=name + "rms_mem")
    qx = _mm([(hn2, W['w_xq'][l])], 'nn', bf16, name=name + "mm_xq")
    kvx = _mm([(memn, W['w_xkv'][l])], 'nn', bf16, name=name + "mm_xkv")
    ox = _xattn_fwd(qx, kvx, name=name + "xattn")
    h2 = _mm([(ox, W['w_xo'][l])], 'nn', f32, res=h1, name=name + "mm_xo")
    hn3 = _rms_fwd(h2, W['norm_ffn_g'][l], name=name + "rms_ffn")
    gate, up, act = _ffn_up(hn3, W['w_gate'][l], W['w_up'][l], name=name + "ffn_up")
    h3 = _mm([(act, W['w_down'][l])], 'nn', f32, res=h2, name=name + "mm_down")
    sv.update(xn=xn, proj=proj, cqn=cqn, ckvn=ckvn, qh=qh, kh=kh, vh=vh, oh=oh, lse=lse, an=an, bmb=bmb, cmb=cmb,
              a=a, y=y, xc=xc, z=z, sn=sn, ge=ge, h1=h1, hn2=hn2, memn=memn, qx=qx, kvx=kvx, ox=ox, h2=h2, hn3=hn3,
              gate=gate, up=up, act=act)
    return h3, sv


def _backward_layer(dh3, sv, memn_in, tabs, W, l, hook, name):
    p = sv['p']
    G = {}
    G['w_down'] = _mm([(sv['act'], dh3)], 'tn', f32, name=name + "dw_down")
    dgate, dup = _ffn_bwd_act(dh3, W['w_down'][l], sv['gate'], sv['up'], name=name + "ffn_bwd_act")
    dhn3 = _mm([(dgate, W['w_gate'][l]), (dup, W['w_up'][l])], 'nn', f32, name=name + "mm_dffn")
    G['w_gate'] = _mm([(dgate, sv['hn3'])], 'tn', f32, name=name + "dw_gate")
    G['w_up'] = _mm([(dup, sv['hn3'])], 'tn', f32, name=name + "dw_up")
    xplan = hook['ffn'](l, G) if hook else None
    dh2, dg = _rms_bwd(sv['h2'], W['norm_ffn_g'][l], dhn3, res=dh3, name=name + "rmsb_ffn")
    G['norm_ffn_g'] = dg[0]
    G['w_xo'] = _mm([(sv['ox'], dh2)], 'tn', f32, name=name + "dw_xo")
    dox = _mm([(dh2, W['w_xo'][l])], 'nt', bf16, name=name + "mm_dxo")
    dqx, dkvx, carried = _xattn_bwd(sv['qx'], sv['kvx'], dox, plan=xplan, name=name + "xattn_bwd")
    if xplan:
        xplan['done'](carried)
    G['w_xq'] = _mm([(sv['hn2'], dqx)], 'tn', f32, name=name + "dw_xq")
    G['w_xkv'] = _mm([(sv['memn'], dkvx)], 'tn', f32, name=name + "dw_xkv")
    dhn2 = _mm([(dqx, W['w_xq'][l])], 'nt', f32, name=name + "mm_dxq")
    dmemn = _mm([(dkvx, W['w_xkv'][l])], 'nt', f32, name=name + "mm_dxkv")
    dh1, dg = _rms_bwd(sv['h1'], W['norm_x_g'][l], dhn2, res=dh2, name=name + "rmsb_x")
    G['norm_x_g'] = dg[0]
    _, dg = _rms_bwd(memn_in, W['mem_norm_g'][l], dmemn, name=name + "rmsb_mem")
    G['mem_norm_g'] = dg[0]
    dwo_a = _mm([(sv['an'], dh1)], 'tn', f32, name=name + "dw_out_a")
    dwo_s = _mm([(sv['sn'], dh1)], 'tn', f32, name=name + "dw_out_s")
    G['w_out'] = _unpack_wout(jnp.concatenate([dwo_a, dwo_s], axis=0))
    dmixed = _mm([(dh1, p['wout'])], 'nt', f32, name=name + "mm_dout")
    dy, dz, dg, db = _glu_bwd(sv['y'], sv['z'], dmixed, W['ssm_w_glu'][l], W['ssm_out_g'][l], name=name + "glu_bwd")
    G['ssm_out_g'], G['ssm_b_glu'] = dg[0], db[0]
    G['ssm_w_glu'] = _mm([(sv['ge'], dz)], 'tn', f32, name=name + "dw_glu")
    xplan = hook['mid'](l, G) if hook else None
    du, dbm, dcm, da, dd, carried = _ssm_bwd(sv['proj'], dy, sv['xc'], sv['bmb'], sv['cmb'], sv['a'], W['ssm_d'][l],
                                             plan=xplan, name=name + "ssm_bwd")
    if xplan:
        xplan['done'](carried)
    G['ssm_d'] = dd[0]
    G['ssm_raw'] = (dbm, dcm, da)
    doh, dg, delta = _rms_bwd(sv['oh'], p['attn_g'], dmixed, width=MLA_PAD, n_valid=MLA_WIDTH, delta=True,
                              out_dtype=bf16, name=name + "rmsb_attn")
    G['attn_out_g'] = _unpad_gain(dg[0])
    plans = hook['take']() if hook else []
    plan = _merge_plans(plans) if plans else None
    dqh, dkh, dvh, carried = _attn_bwd(sv['qh'], sv['kh'], sv['vh'], doh, sv['lse'], delta, plan=plan,
                                       name=name + "attn_bwd")
    if plan:
        plan['done'](carried)
    dq, dkv, dkr = _rope_bwd(dqh, dkh, dvh, tabs, name=name + "rope_bwd")
    G['w_uq'] = _unpad_heads(_mm([(sv['cqn'], dq)], 'tn', f32, name=name + "dw_uq"), QK_NOPE + QK_ROPE, _Q_PIECES)
    dwukv = _mm([(sv['ckvn'], dkv)], 'tn', f32, name=name + "dw_ukv")
    G['w_ukv'] = (_unpad_heads(dwukv[:, :MLA_PAD], QK_NOPE + V_HEAD, _K_PIECES)
                  + _unpad_heads(dwukv[:, MLA_PAD:], QK_NOPE + V_HEAD, _V_PIECES))
    dcqn = _mm([(dq, p['wuq'])], 'nt', f32, name=name + "mm_duq")
    dckvn = _mm([(dkv, p['wukv'])], 'nt', f32, name=name + "mm_dukv")
    dcq, dg = _rms_bwd(sv['proj'], W['q_norm_g'][l], dcqn, col0=0, width=Q_LORA, out_dtype=bf16, name=name + "rmsb_q")
    G['q_norm_g'] = dg[0]
    dckv, dg = _rms_bwd(sv['proj'], W['kv_norm_g'][l], dckvn, col0=Q_LORA, width=KV_LORA, out_dtype=bf16,
                        name=name + "rmsb_kv")
    G['kv_norm_g'] = dg[0]
    dproj = jnp.concatenate([dcq, dckv, dkr, du.astype(bf16)], axis=1)
    G['w_in'] = _unpack_win(_mm([(sv['xn'], dproj)], 'tn', f32, name=name + "dw_in"))
    dxn = _mm([(dproj, p['win'])], 'nt', f32, name=name + "mm_din")
    dh0, dg = _rms_bwd(sv['h0'], W['norm_mix_g'][l], dxn, res=dh1, name=name + "rmsb_mix")
    G['norm_mix_g'] = dg[0]
    return dh0, G


def _local_step(x, mem, positions, target, W, gathers=None, hook=None):
    tabs = _rope_tables(positions)
    ssm_in = [(W['ssm_lambda_re'][l], W['ssm_lambda_im'][l], W['ssm_log_dt'][l], W['ssm_b_re'][l], W['ssm_b_im'][l],
               W['ssm_c_re'][l], W['ssm_c_im'][l]) for l in range(DEPTH)]
    preps = [jax.vjp(_ssm_prep, *ssm_in[l]) for l in range(DEPTH)]
    W = dict(W)
    W['ssm'] = [preps[l][0] for l in range(DEPTH)]
    h = x
    saved = []
    for l in range(DEPTH):
        h, sv = _forward_layer(h, mem, tabs, W, l, f"l{l}_", gathers[l] if gathers else None)
        saved.append(sv)
    dh, dgf, loss = _loss_head(h, W['final_norm_g'], target)
    grads = [None] * DEPTH
    for l in reversed(range(DEPTH)):
        dh, G = _backward_layer(dh, saved[l], mem, tabs, W, l, hook, f"l{l}b_")
        dbm, dcm, da = G.pop('ssm_raw')
        names = ['ssm_lambda_re', 'ssm_lambda_im', 'ssm_log_dt', 'ssm_b_re', 'ssm_b_im', 'ssm_c_re', 'ssm_c_im']
        for n, g in zip(names, preps[l][1]((dbm, dcm, da))):
            G[n] = g
        grads[l] = G
        if hook is not None and l > 0:
            hook['rest'](l, G)
    out = {n: [grads[l][n] for l in range(DEPTH)] if n in SHARDED else jnp.stack([grads[l][n] for l in range(DEPTH)])
           for n in grads[0]}
    out['final_norm_g'] = dgf[0]
    return loss[0, 0], dh, out


_HBM = pl.BlockSpec(memory_space=pltpu.HBM)


def _me():
    return lax.axis_index("x"), lax.axis_index("y"), lax.axis_index("c")


def _chip_peers(x, y, c):
    devs = [(1 - x, y, c), (x, 1 - y, c), (1 - x, 1 - y, c)]
    return devs, [2 * d[0] + d[1] for d in devs]


def _gather_plan(xs, half_first):
    n = len(xs)
    if half_first:
        ins = [t.reshape(2, 1, *t.shape[1:]) for t in xs]
        outs = [jax.ShapeDtypeStruct((2, 4, *t.shape[1:]), t.dtype) for t in xs]
    else:
        ins = [t.reshape(1, 2, t.shape[0] // 2, t.shape[1]) for t in xs]
        outs = [jax.ShapeDtypeStruct((4, 2, t.shape[0] // 2, t.shape[1]), t.dtype) for t in xs]

    def own(ref, h):
        return ref.at[h] if half_first else ref.at[:, h]

    def slot(ref, h, j):
        return ref.at[h, pl.ds(j, 1)] if half_first else ref.at[pl.ds(j, 1), h]

    def copies(src, dst, send, recv):
        x, y, c = _me()
        jme = 2 * x + y
        devs, js = _chip_peers(x, y, c)
        half, other = pl.ds(c, 1), pl.ds(1 - c, 1)
        mk = pltpu.make_async_remote_copy
        for i in range(n):
            for k in range(3):
                out_cp = mk(own(src[i], half), slot(dst[i], half, jme), send.at[6 * i + k], recv.at[6 * i + k],
                            device_id=devs[k], device_id_type=MESH)
                in_cp = mk(own(src[i], half), slot(dst[i], half, js[k]), send.at[6 * i + k], recv.at[6 * i + k],
                           device_id=devs[k], device_id_type=MESH)
                pass_cp = mk(slot(dst[i], half, js[k]), slot(dst[i], half, js[k]), send.at[6 * i + 3 + k],
                             recv.at[6 * i + 3 + k], device_id=(x, y, 1 - c), device_id_type=MESH)
                got_cp = mk(slot(dst[i], other, js[k]), slot(dst[i], other, js[k]), send.at[6 * i + 3 + k],
                            recv.at[6 * i + 3 + k], device_id=(x, y, 1 - c), device_id_type=MESH)
                yield out_cp, in_cp, pass_cp, got_cp

    def start(*refs):
        for out_cp, _, _, _ in copies(*refs):
            out_cp.start()

    def forward(*refs):
        for _, in_cp, pass_cp, _ in copies(*refs):
            in_cp.wait_recv()
            pass_cp.start()

    def finish(*refs):
        for out_cp, _, pass_cp, got_cp in copies(*refs):
            got_cp.wait_recv()
            out_cp.wait_send()
            pass_cp.wait_send()

    return dict(n=n, ins=ins, outs=outs, nsem=6 * n, start=start, forward=forward, finish=finish)


def _plan_refs(plan, refs):
    n = plan['n']
    return refs[:n], refs[n:2 * n], refs[2 * n], refs[2 * n + 1]


def _run_plan(plan, name):
    n = plan['n']

    def body(*refs):
        args = _plan_refs(plan, refs)
        plan['start'](*args)
        plan['forward'](*args)
        plan['finish'](*args)

    return _pc(body, in_specs=[_HBM] * n, out_specs=[_HBM] * n, out_shape=plan['outs'],
               scratch_shapes=[pltpu.SemaphoreType.DMA((plan['nsem'],)), pltpu.SemaphoreType.DMA((plan['nsem'],))],
               compiler_params=pltpu.CompilerParams(has_side_effects=True), name=name)(*plan['ins'])


def _fill_own(gathered, own, half_first):
    jme = (2 * lax.axis_index("x") + lax.axis_index("y")).astype(jnp.int32)
    zero = jnp.int32(0)
    if half_first:
        return lax.dynamic_update_slice(gathered, own[:, None], (zero, jme, zero, zero))
    return lax.dynamic_update_slice(gathered, own.reshape(1, *gathered.shape[1:]), (jme, zero, zero, zero))


def _exchange_plan(gs, done=None):
    n = len(gs)

    def copies(src, dst, send, recv):
        x, y, c = _me()
        for i in range(n):
            yield pltpu.make_async_remote_copy(src[i].at[:, pl.ds(1 - c, 1)], dst[i], send.at[i], recv.at[i],
                                               device_id=(x, y, 1 - c), device_id_type=MESH)

    def start(*refs):
        for cp in copies(*refs):
            cp.start()

    def finish(*refs):
        for cp in copies(*refs):
            cp.wait()

    outs = [jax.ShapeDtypeStruct((4, 1, *g.shape[2:]), g.dtype) for g in gs]
    return dict(n=n, ins=list(gs), outs=outs, nsem=n, start=start, forward=lambda *refs: None, finish=finish, done=done)


def _plan_extras(plan):
    if not plan:
        return [], [], [], [], []
    anyspec = pl.BlockSpec(memory_space=pl.ANY)
    sems = [pltpu.SemaphoreType.DMA((plan['nsem'],)), pltpu.SemaphoreType.DMA((plan['nsem'],))]
    return [anyspec] * plan['n'], [anyspec] * plan['n'], list(plan['outs']), sems, list(plan['ins'])


def _scatter_plan(ps, done=None):
    n = len(ps)

    def copies(src, dst, send, recv, off):
        x, y, c = _me()
        devs, js = _chip_peers(x, y, c)
        for i in range(n):
            for k in range(3):
                yield pltpu.make_async_remote_copy(src[i].at[pl.ds(js[k], 1)], dst[i].at[k], send.at[off + 3 * i + k],
                                                   recv.at[off + 3 * i + k], device_id=devs[k], device_id_type=MESH)

    def start(src, dst, send, recv, off=0):
        for cp in copies(src, dst, send, recv, off):
            cp.start()

    def finish(src, dst, send, recv, off=0):
        for cp in copies(src, dst, send, recv, off):
            cp.wait()

    outs = [jax.ShapeDtypeStruct((3, 1, *p.shape[1:]), p.dtype) for p in ps]
    return dict(n=n, ins=list(ps), outs=outs, nsem=3 * n, start=start, forward=lambda *refs: None, finish=finish,
                done=done)


def _merge_plans(plans):
    def run(which):
        def f(src, dst, send, recv):
            o = s = 0
            for p in plans:
                p[which](src[o:o + p['n']], dst[o:o + p['n']], send, recv, off=s)
                o, s = o + p['n'], s + p['nsem']
        return f

    def done(results):
        o = 0
        for p in plans:
            p['done'](results[o:o + p['n']])
            o += p['n']

    return dict(n=sum(p['n'] for p in plans), ins=[t for p in plans for t in p['ins']],
                outs=[t for p in plans for t in p['outs']], nsem=sum(p['nsem'] for p in plans),
                start=run('start'), forward=lambda *refs: None, finish=run('finish'), done=done)


def _swap_sibling(hs, name):
    n = len(hs)

    def body(*refs):
        src, dst = refs[:n], refs[n:2 * n]
        send, recv = refs[2 * n:]
        x, y, c = _me()
        cps = []
        for i in range(n):
            cp = pltpu.make_async_remote_copy(src[i], dst[i], send.at[i], recv.at[i], device_id=(x, y, 1 - c),
                                              device_id_type=MESH)
            cp.start()
            cps.append(cp)
        for cp in cps:
            cp.wait()

    outs = [jax.ShapeDtypeStruct(h.shape, h.dtype) for h in hs]
    return _pc(body, in_specs=[_HBM] * n, out_specs=[_HBM] * n, out_shape=outs,
               scratch_shapes=[pltpu.SemaphoreType.DMA((n,)), pltpu.SemaphoreType.DMA((n,))],
               compiler_params=pltpu.CompilerParams(has_side_effects=True), name=name)(*hs)


ELEMWISE_VMEM_BUDGET = 36 * 1024 * 1024


def _row_tile(r, n, narrays):
    limit = ELEMWISE_VMEM_BUDGET // (2 * 4 * narrays * n)
    best = SUBLANES
    for t in range(16, r + 1, 16):
        if r % t == 0 and t <= limit:
            best = t
    return best


def _add_half(g, r1, cidx, name):
    _, _, r, n = g.shape
    tr = _row_tile(r, n, 3)

    def body(c_ref, g_ref, r_ref, o_ref):
        o_ref[...] = (g_ref[...] + r_ref[...]).astype(GRAD_TRANSIT)

    blk = lambda f: pl.BlockSpec((None, None, tr, n), f)
    gs = pltpu.PrefetchScalarGridSpec(
        num_scalar_prefetch=1, grid=(4, r // tr),
        in_specs=[blk(lambda j, i, c: (j, c[0], i, 0)), blk(lambda j, i, c: (j, 0, i, 0))],
        out_specs=pl.BlockSpec((None, tr, n), lambda j, i, c: (j, i, 0)))
    return _pc(body, grid_spec=gs, out_shape=jax.ShapeDtypeStruct((4, r, n), GRAD_TRANSIT),
               compiler_params=_cp(("parallel", "parallel")), name=name)(cidx, g, r1)


def _add_chips(p, r3, jidx, name):
    _, r, n = p.shape
    tr = _row_tile(r, n, 5)

    def body(j_ref, p_ref, a_ref, b_ref, c_ref, o_ref):
        o_ref[...] = ((p_ref[...].astype(f32) + a_ref[...].astype(f32)) + b_ref[...].astype(f32)) + c_ref[...].astype(f32)

    rblk = lambda k: pl.BlockSpec((None, None, tr, n), lambda i, j: (k, 0, i, 0))
    gs = pltpu.PrefetchScalarGridSpec(
        num_scalar_prefetch=1, grid=(r // tr,),
        in_specs=[pl.BlockSpec((None, tr, n), lambda i, j: (j[0], i, 0)), rblk(0), rblk(1), rblk(2)],
        out_specs=pl.BlockSpec((tr, n), lambda i, j: (i, 0)))
    return _pc(body, grid_spec=gs, out_shape=jax.ShapeDtypeStruct((r, n), f32),
               compiler_params=_cp(("parallel",)), name=name)(jidx, p, r3, r3, r3)


def _adamw_halves(w, mine, theirs, m, v, cidx, name):
    L, r, n = w.shape
    r2 = r // 2
    tr = _row_tile(r2, n, 11)
    c1 = 1.0 / (1.0 - ADAM_B1 ** ADAM_STEP)
    c2 = 1.0 / (1.0 - ADAM_B2 ** ADAM_STEP)

    def body(c_ref, w_ref, a0_ref, b0_ref, a1_ref, b1_ref, m_ref, v_ref, g_ref, d_ref, mo_ref, vo_ref):
        l, hf = pl.program_id(0), pl.program_id(1)
        own = hf == c_ref[0]
        gv = jnp.where(l == 0, jnp.where(own, a0_ref[...], b0_ref[...]), jnp.where(own, a1_ref[...], b1_ref[...]))
        m2 = ADAM_B1 * m_ref[...] + (1.0 - ADAM_B1) * gv
        v2 = ADAM_B2 * v_ref[...] + (1.0 - ADAM_B2) * (gv * gv)
        g_ref[...] = gv
        d_ref[...] = -ADAM_LR * ((m2 * c1) / (jnp.sqrt(v2 * c2) + ADAM_EPS) + ADAM_WD * w_ref[...])
        mo_ref[...] = m2
        vo_ref[...] = v2

    full = pl.BlockSpec((None, None, tr, n), lambda l, hf, i, c: (l, hf, i, 0))

    def half(layer, own):
        return pl.BlockSpec((tr, n), lambda l, hf, i, c: (jnp.where((l == layer) & ((hf == c[0]) == own), i, 0), 0))

    gs = pltpu.PrefetchScalarGridSpec(
        num_scalar_prefetch=1, grid=(L, 2, r2 // tr),
        in_specs=[full, half(0, True), half(0, False), half(1, True), half(1, False), full, full], out_specs=[full] * 4)
    four = lambda t: t.reshape(L, 2, r2, n)
    outs = _pc(body, grid_spec=gs, out_shape=[jax.ShapeDtypeStruct((L, 2, r2, n), f32)] * 4,
               compiler_params=_cp(("parallel", "parallel", "parallel")), name=name)(
        cidx, four(w), mine[0], theirs[0], mine[1], theirs[1], four(m), four(v))
    return [t.reshape(w.shape) for t in outs]


def _adamw_whole(w, g, m, v, name):
    c1 = 1.0 / (1.0 - ADAM_B1 ** ADAM_STEP)
    c2 = 1.0 / (1.0 - ADAM_B2 ** ADAM_STEP)

    def body(w_ref, g_ref, m_ref, v_ref, d_ref, mo_ref, vo_ref):
        gv = g_ref[...]
        m2 = ADAM_B1 * m_ref[...] + (1.0 - ADAM_B1) * gv
        v2 = ADAM_B2 * v_ref[...] + (1.0 - ADAM_B2) * (gv * gv)
        d_ref[...] = -ADAM_LR * ((m2 * c1) / (jnp.sqrt(v2 * c2) + ADAM_EPS) + ADAM_WD * w_ref[...])
        mo_ref[...] = m2
        vo_ref[...] = v2

    return _pc(body, out_shape=[jax.ShapeDtypeStruct(w.shape, f32)] * 3, name=name)(w, g, m, v)


def _full_from_gathered(name, t):
    r, n = 2 * t.shape[2], t.shape[3]
    if SHARDED[name] == 1 or name in TRANSPOSED:
        return t.reshape(4 * r, n)
    return t.reshape(4, r, n).transpose(1, 0, 2).reshape(r, 4 * n)


def _shard_major(name, g):
    R, C = g.shape
    if SHARDED[name] == 1 or name in TRANSPOSED:
        return g.reshape(4, 2, R // 8, C)
    return g.reshape(R, 4, C // 4).transpose(1, 0, 2).reshape(4, 2, R // 2, C // 4)


_SMALL_ROWS = 288


def _pack_small(d):
    flat = jnp.concatenate([d[n].reshape(-1) for n in SMALL])
    total = 2 * 4 * _SMALL_ROWS * LANES
    flat = jnp.concatenate([flat, jnp.zeros((total - flat.shape[0],), f32)])
    return flat.reshape(4, 2, _SMALL_ROWS, LANES)


def _unpack_small(t, like):
    flat = t.reshape(-1)
    out, off = {}, 0
    for n in SMALL:
        sz = math.prod(like[n].shape)
        out[n] = flat[off:off + sz].reshape(like[n].shape)
        off += sz
    return out


def kernel(x, mem, positions, norm_mix_g, w_in, q_norm_g, w_uq, kv_norm_g, w_ukv, ssm_lambda_re, ssm_lambda_im, ssm_log_dt, ssm_b_re, ssm_b_im, ssm_c_re, ssm_c_im, ssm_d, ssm_w_glu, ssm_b_glu, attn_out_g, ssm_out_g, w_out, norm_x_g, mem_norm_g, w_xq, w_xkv, w_xo, norm_ffn_g, w_gate, w_up, w_down, final_norm_g, loss_target, m_norm_mix_g, m_w_in, m_q_norm_g, m_w_uq, m_kv_norm_g, m_w_ukv, m_ssm_lambda_re, m_ssm_lambda_im, m_ssm_log_dt, m_ssm_b_re, m_ssm_b_im, m_ssm_c_re, m_ssm_c_im, m_ssm_d, m_ssm_w_glu, m_ssm_b_glu, m_attn_out_g, m_ssm_out_g, m_w_out, m_norm_x_g, m_mem_norm_g, m_w_xq, m_w_xkv, m_w_xo, m_norm_ffn_g, m_w_gate, m_w_up, m_w_down, m_final_norm_g, v_norm_mix_g, v_w_in, v_q_norm_g, v_w_uq, v_kv_norm_g, v_w_ukv, v_ssm_lambda_re, v_ssm_lambda_im, v_ssm_log_dt, v_ssm_b_re, v_ssm_b_im, v_ssm_c_re, v_ssm_c_im, v_ssm_d, v_ssm_w_glu, v_ssm_b_glu, v_attn_out_g, v_ssm_out_g, v_w_out, v_norm_x_g, v_mem_norm_g, v_w_xq, v_w_xkv, v_w_xo, v_norm_ffn_g, v_w_gate, v_w_up, v_w_down, v_final_norm_g):
    given = dict(locals())
    swap = lambda n, t: jnp.swapaxes(t, *TRANSPOSED[n]) if n in TRANSPOSED else t
    w = {n: swap(n, given[n]) for n in WEIGHTS}
    m = {n: swap(n, given["m_" + n]) for n in WEIGHTS}
    v = {n: swap(n, given["v_" + n]) for n in WEIGHTS}
    big = list(SHARDED)

    shards = {n: w[n].astype(bf16) for n in big}
    early = [n for n in big if n in EARLY_WEIGHTS]
    rest = [n for n in big if n not in EARLY_WEIGHTS]

    def full(names, results, l):
        return {n: _full_from_gathered(n, _fill_own(t, shards[n][l], False)) for n, t in zip(names, results)}

    first = _run_plan(_gather_plan([shards[n][l] for l in range(DEPTH) for n in early], False), "allgather_weights_early")
    W = {n: [None] * DEPTH for n in big}
    for l in range(DEPTH):
        for n, t in full(early, first[l * len(early):(l + 1) * len(early)], l).items():
            W[n][l] = t
    gathers = [(_gather_plan([shards[n][l] for n in rest], False), functools.partial(full, rest, l=l))
               for l in range(DEPTH)]
    W.update({n: w[n] for n in SMALL})

    cidx = lax.axis_index("c").astype(jnp.int32).reshape(1)
    jidx = (2 * lax.axis_index("x") + lax.axis_index("y")).astype(jnp.int32).reshape(1)
    ffn = [n for n in big if n in FFN_WEIGHTS]
    tail = [n for n in big if n in EARLY_WEIGHTS]
    mid = [n for n in big if n not in FFN_WEIGHTS and n not in EARLY_WEIGHTS]
    sums, got = {}, {}
    ready = []

    def exchange(names, tag, l, G, extra=()):
        keys = [(l, n) for n in names] + [(l, n) for n, _ in extra]
        gs = [_shard_major(n, G[n]) for n in names] + [g for _, g in extra]

        def done(r1):
            ps = [_add_half(g, r, cidx, f"grad_add_half_{tag}_{k[1]}") for k, g, r in zip(keys, gs, r1)]
            sums.update(zip(keys, ps))
            ready.append(_scatter_plan(ps, done=lambda results: got.update(zip(keys, results))))

        return _exchange_plan(gs, done)

    def take():
        plans = list(ready)
        ready.clear()
        return plans

    def now(plan, name):
        plan['done'](_run_plan(plan, name))

    hook = {'ffn': lambda l, G: exchange(ffn, f"l{l}_ffn", l, G), 'mid': lambda l, G: exchange(mid, f"l{l}_mid", l, G),
            'rest': lambda l, G: now(exchange(tail, f"l{l}_tail", l, G), f"grad_exchange_halves_l{l}_tail"),
            'take': take}
    loss, dx, grads = _local_step(x[0], mem[0], positions[0], loss_target[0], W, gathers=gathers, hook=hook)
    loss = lax.psum(loss, ("x", "y", "c"))

    now(exchange(tail, "l0_tail", 0, {n: grads[n][0] for n in tail}, extra=[("small", _pack_small(grads))]),
        "grad_exchange_halves_l0_tail")
    now(_merge_plans(take()), "grad_scatter_chips_l0_tail")
    keys = list(sums)
    hs = dict(zip(keys, [_add_chips(sums[k], got[k], jidx, f"grad_add_chips_l{k[0]}_{k[1]}") for k in keys]))
    ts = dict(zip(keys, _swap_sibling([hs[k] for k in keys], "grad_swap_sibling")))

    out_g, out_d, out_m, out_v = {}, {}, {}, {}
    for n in big:
        mine, theirs = [hs[(l, n)] for l in range(DEPTH)], [ts[(l, n)] for l in range(DEPTH)]
        out_g[n], out_d[n], out_m[n], out_v[n] = _adamw_halves(w[n], mine, theirs, m[n], v[n], cidx, f"adamw_{n}")
    both = jnp.stack([hs[(0, "small")], ts[(0, "small")]])
    piece = jnp.where(cidx[0] == 0, both, both[::-1]).reshape(2 * _SMALL_ROWS, LANES)
    gsm = _fill_own(_run_plan(_gather_plan([piece], False), "allgather_small")[0], piece, False)
    out_g.update(_unpack_small(gsm, w))
    for n in SMALL:
        two = lambda t: t.reshape(1, -1) if t.ndim == 1 else t
        d_, m_, v_ = _adamw_whole(two(w[n]), two(out_g[n]), two(m[n]), two(v[n]), f"adamw_{n}")
        out_d[n], out_m[n], out_v[n] = (t.reshape(w[n].shape) for t in (d_, m_, v_))

    outs = [[swap(n, d[n]) for n in WEIGHTS] for d in (out_g, out_d, out_m, out_v)]
    return (loss, dx.reshape(x.shape), *outs[0], *outs[1], *outs[2], *outs[3])
```

```python
import functools
import math

import jax
import jax.numpy as jnp
from jax import lax
from jax.experimental import pallas as pl
from jax.experimental.pallas import tpu as pltpu

f32, bf16 = jnp.float32, jnp.bfloat16

D_MODEL = 1024
DEPTH = 2
MLA_HEADS = 8
QK_NOPE = 64
QK_ROPE = 32
V_HEAD = 64
Q_LORA = 256
KV_LORA = 128
MLA_WIDTH = MLA_HEADS * V_HEAD
ROPE_THETA = 10000.0
SSM_WIDTH = 512
SSM_GROUP = 16
SSM_GROUPS = 32
SSM_STATE = 64
IN_WIDTH = Q_LORA + KV_LORA + QK_ROPE + SSM_WIDTH
X_HEADS = 4
X_HEAD_DIM = D_MODEL // X_HEADS
D_FF = 2816
EPS = 1e-6
ADAM_LR, ADAM_B1, ADAM_B2, ADAM_EPS, ADAM_WD, ADAM_STEP = 0.001, 0.9, 0.999, 1e-08, 0.01, 10

LANES = 128
SUBLANES = 8
HEAD_PAD = 128
MLA_PAD = MLA_HEADS * HEAD_PAD
SSM_MACRO = 4
MACRO_CH = SSM_WIDTH // SSM_MACRO
MACRO_ST = SSM_GROUPS // SSM_MACRO * SSM_STATE
VMEM_LIMIT = 56 * 1024 * 1024
GRAD_TRANSIT = bf16

WEIGHTS = ['norm_mix_g', 'w_in', 'q_norm_g', 'w_uq', 'kv_norm_g', 'w_ukv', 'ssm_lambda_re', 'ssm_lambda_im',
           'ssm_log_dt', 'ssm_b_re', 'ssm_b_im', 'ssm_c_re', 'ssm_c_im', 'ssm_d', 'ssm_w_glu', 'ssm_b_glu',
           'attn_out_g', 'ssm_out_g', 'w_out', 'norm_x_g', 'mem_norm_g', 'w_xq', 'w_xkv', 'w_xo', 'norm_ffn_g',
           'w_gate', 'w_up', 'w_down', 'final_norm_g']
SHARDED = {'w_in': 1, 'w_uq': 2, 'w_ukv': 2, 'ssm_w_glu': 1, 'w_out': 1, 'w_xq': 1, 'w_xkv': 2, 'w_xo': 1,
           'w_gate': 2, 'w_up': 2, 'w_down': 1}
SMALL = [n for n in WEIGHTS if n not in SHARDED]
EARLY_WEIGHTS = ('w_in', 'w_uq', 'w_ukv')
FFN_WEIGHTS = ('w_gate', 'w_up', 'w_down')
TRANSPOSED = {'w_gate': (1, 2), 'w_up': (1, 2), 'ssm_b_re': (2, 3), 'ssm_b_im': (2, 3)}
MESH = pl.DeviceIdType.MESH


def _pc(body, **kw):
    return pl.pallas_call(body, **kw)


def _pick(n, prefs):
    for p in prefs:
        if n % p == 0:
            return p
    return n


def _cp(sem=None):
    return pltpu.CompilerParams(dimension_semantics=sem, vmem_limit_bytes=VMEM_LIMIT)


_TILE_CANDS = (1024, 1408, 512, 256, 128)
MM_VMEM_BUDGET = 40 * 1024 * 1024


def _mm_tiles(M, K, N, a_bytes, b_bytes, o_bytes, npair, has_res, need_acc):
    best = None
    for tm in _TILE_CANDS:
        for tk in _TILE_CANDS:
            if M % tm or K % tk:
                continue
            vm = npair * (2 * tm * tk * a_bytes + 2 * tk * N * b_bytes) + 2 * tm * N * o_bytes
            vm += tm * N * 4 * (1 + need_acc + 2 * has_res)
            if a_bytes == 4:
                vm += npair * tm * tk * 2
            if b_bytes == 4:
                vm += npair * tk * N * 2
            if vm <= MM_VMEM_BUDGET and (best is None or tm * tk > best[0]):
                best = (tm * tk, tm, tk)
    if best is None:
        return _pick(M, (256, 128)), _pick(K, (256, 128))
    return best[1], best[2]


def _mm(pairs, mode, out_dtype, res=None, name="mm"):
    a0, b0 = pairs[0]
    if mode == 'nn':
        (M, K), N = a0.shape, b0.shape[1]
        dims = (((1,), (0,)), ((), ()))
    elif mode == 'nt':
        (M, K), N = a0.shape, b0.shape[0]
        dims = (((1,), (1,)), ((), ()))
    else:
        (K, M), N = a0.shape, b0.shape[1]
        dims = (((0,), (0,)), ((), ()))
    npair = len(pairs)
    has_res = res is not None
    direct = out_dtype == f32
    tm, tk = _mm_tiles(M, K, N, a0.dtype.itemsize, b0.dtype.itemsize, jnp.dtype(out_dtype).itemsize, npair, has_res,
                       not direct)
    nk = K // tk

    def body(*refs):
        ins = refs[:2 * npair]
        res_ref = refs[2 * npair] if has_res else None
        o_ref = refs[2 * npair + has_res]
        acc = o_ref if direct else refs[2 * npair + has_res + 1]
        k = pl.program_id(1)
        s = None
        for p in range(npair):
            d = lax.dot_general(ins[2 * p][...].astype(bf16), ins[2 * p + 1][...].astype(bf16), dims,
                                preferred_element_type=f32)
            s = d if s is None else s + d

        @pl.when(k == 0)
        def _():
            acc[...] = s

        @pl.when(k > 0)
        def _():
            acc[...] += s

        if has_res or not direct:
            @pl.when(k == nk - 1)
            def _():
                r = acc[...]
                if has_res:
                    r = r + res_ref[...]
                o_ref[...] = r.astype(out_dtype)

    if mode == 'nn':
        a_spec = pl.BlockSpec((tm, tk), lambda i, k: (i, k))
        b_spec = pl.BlockSpec((tk, N), lambda i, k: (k, 0))
    elif mode == 'nt':
        a_spec = pl.BlockSpec((tm, tk), lambda i, k: (i, k))
        b_spec = pl.BlockSpec((N, tk), lambda i, k: (0, k))
    else:
        a_spec = pl.BlockSpec((tk, tm), lambda i, k: (k, i))
        b_spec = pl.BlockSpec((tk, N), lambda i, k: (k, 0))
    o_spec = pl.BlockSpec((tm, N), lambda i, k: (i, 0))
    in_specs = [a_spec, b_spec] * npair + ([o_spec] if has_res else [])
    args = [t for p in pairs for t in p] + ([res] if has_res else [])
    return _pc(body, grid=(M // tm, nk), in_specs=in_specs, out_specs=o_spec,
               out_shape=jax.ShapeDtypeStruct((M, N), out_dtype),
               scratch_shapes=[] if direct else [pltpu.VMEM((tm, N), f32)],
               compiler_params=_cp(("parallel", "arbitrary")), name=name)(*args)


def _rms_fwd(x, g, *, col0=0, width=None, n_valid=None, out_dtype=bf16, name="rms_fwd"):
    S = x.shape[0]
    width = width or x.shape[1]
    n_valid = n_valid or width
    ts = _pick(S, (512, 256, 128))
    cb = col0 // width

    def body(x_ref, g_ref, o_ref):
        xv = x_ref[...]
        ms = jnp.sum(xv * xv, axis=-1, keepdims=True) * (1.0 / n_valid)
        o_ref[...] = (xv * lax.rsqrt(ms + EPS) * g_ref[...]).astype(out_dtype)

    return _pc(body, grid=(S // ts,),
               in_specs=[pl.BlockSpec((ts, width), lambda i: (i, cb)), pl.BlockSpec((1, width), lambda i: (0, 0))],
               out_specs=pl.BlockSpec((ts, width), lambda i: (i, 0)),
               out_shape=jax.ShapeDtypeStruct((S, width), out_dtype),
               compiler_params=_cp(("parallel",)), name=name)(x, g.reshape(1, width))


def _rms_bwd(x, g, dy, *, col0=0, dcol0=0, width=None, n_valid=None, res=None, out_dtype=f32, delta=False,
             dup_bf16=False, name="rms_bwd"):
    S = x.shape[0]
    width = width or x.shape[1]
    n_valid = n_valid or width
    ts = _pick(S, (512, 256, 128))
    cb, dcb = col0 // width, dcol0 // width
    has_res = res is not None

    def body(*refs):
        x_ref, g_ref, dy_ref = refs[:3]
        res_ref = refs[3] if has_res else None
        outs = refs[3 + has_res:]
        dx_ref, dg_ref = outs[0], outs[1]
        i = pl.program_id(0)
        xv = x_ref[...]
        gv = g_ref[...]
        dyv = dy_ref[...].astype(f32)
        rstd = lax.rsqrt(jnp.sum(xv * xv, axis=-1, keepdims=True) * (1.0 / n_valid) + EPS)
        xh = xv * rstd
        dxh = dyv * gv
        mean = jnp.sum(dxh * xh, axis=-1, keepdims=True) * (1.0 / n_valid)
        dx = rstd * (dxh - xh * mean)
        if delta:
            d_ref = outs[2]
            for h in range(width // LANES):
                sl = slice(h * LANES, (h + 1) * LANES)
                dsum = jnp.sum(dx[:, sl] * xv[:, sl], axis=-1, keepdims=True)
                d_ref[:, sl] = jnp.broadcast_to(dsum, (ts, LANES))
        if has_res:
            dx = dx + res_ref[...]
        dx_ref[...] = dx.astype(out_dtype)
        if dup_bf16:
            outs[-1][...] = dx.astype(bf16)

        @pl.when(i == 0)
        def _():
            dg_ref[...] = jnp.zeros_like(dg_ref)

        dg_ref[...] += jnp.sum(dyv * xh, axis=0, keepdims=True)

    blk = lambda c: pl.BlockSpec((ts, width), lambda i: (i, c))
    in_specs = [blk(cb), pl.BlockSpec((1, width), lambda i: (0, 0)), blk(dcb)] + ([blk(0)] if has_res else [])
    out_specs = [blk(0), pl.BlockSpec((1, width), lambda i: (0, 0))] + ([blk(0)] if delta else []) + (
        [blk(0)] if dup_bf16 else [])
    out_shape = [jax.ShapeDtypeStruct((S, width), out_dtype), jax.ShapeDtypeStruct((1, width), f32)] + (
        [jax.ShapeDtypeStruct((S, width), f32)] if delta else []) + (
        [jax.ShapeDtypeStruct((S, width), bf16)] if dup_bf16 else [])
    args = [x, g.reshape(1, width), dy] + ([res] if has_res else [])
    return _pc(body, grid=(S // ts,), in_specs=in_specs, out_specs=out_specs, out_shape=out_shape,
               compiler_params=_cp(("arbitrary",)), name=name)(*args)


def _loss_head(h, g, target, name="loss_head"):
    S, D = h.shape
    ts = _pick(S, (512, 256, 128))

    def body(h_ref, g_ref, t_ref, dh_ref, dg_ref, loss_ref, dhb_ref):
        i = pl.program_id(0)
        xv = h_ref[...]
        gv = g_ref[...]
        rstd = lax.rsqrt(jnp.sum(xv * xv, axis=-1, keepdims=True) * (1.0 / D) + EPS)
        xh = xv * rstd
        err = xh * gv - t_ref[...]
        dyv = err * (1.0 / D)
        dxh = dyv * gv
        mean = jnp.sum(dxh * xh, axis=-1, keepdims=True) * (1.0 / D)
        dh = rstd * (dxh - xh * mean)
        dh_ref[...] = dh
        dhb_ref[...] = dh.astype(bf16)

        @pl.when(i == 0)
        def _():
            dg_ref[...] = jnp.zeros_like(dg_ref)
            loss_ref[...] = jnp.zeros_like(loss_ref)

        dg_ref[...] += jnp.sum(dyv * xh, axis=0, keepdims=True)
        part = jnp.sum(jnp.sum(err * err, axis=-1, keepdims=True), axis=0, keepdims=True) * (0.5 / D)
        loss_ref[...] += jnp.broadcast_to(part, (1, LANES))

    blk = pl.BlockSpec((ts, D), lambda i: (i, 0))
    row = pl.BlockSpec((1, D), lambda i: (0, 0))
    return _pc(body, grid=(S // ts,), in_specs=[blk, row, blk],
               out_specs=[blk, row, pl.BlockSpec((1, LANES), lambda i: (0, 0)), blk],
               out_shape=[jax.ShapeDtypeStruct((S, D), f32), jax.ShapeDtypeStruct((1, D), f32),
                          jax.ShapeDtypeStruct((1, LANES), f32), jax.ShapeDtypeStruct((S, D), bf16)],
               compiler_params=_cp(("arbitrary",)), name=name)(h, g.reshape(1, D), target)


def _rope_apply(x, tc, s1, s2):
    return x * tc + pltpu.roll(x, LANES - 16, 1) * s1 + pltpu.roll(x, 16, 1) * s2


def _rope_apply_t(dy, tc, s1, s2):
    return dy * tc + pltpu.roll(dy * s1, 16, 1) + pltpu.roll(dy * s2, LANES - 16, 1)


def _rope_fwd(q, kv, proj, tabs, name="rope_fwd"):
    S = q.shape[0]
    ts = _pick(S, (512, 256, 128))
    scale = (QK_NOPE + QK_ROPE) ** -0.5

    def body(q_ref, kk_ref, kvv_ref, kr_ref, tc_ref, s1_ref, s2_ref, qh_ref, kh_ref, vh_ref):
        tc, s1, s2 = tc_ref[...], s1_ref[...], s2_ref[...]
        krr = _rope_apply(pltpu.roll(kr_ref[...], QK_NOPE, 1), tc, s1, s2)
        for h in range(MLA_HEADS):
            sl = slice(h * HEAD_PAD, (h + 1) * HEAD_PAD)
            qh_ref[:, sl] = (_rope_apply(q_ref[:, sl], tc, s1, s2) * scale).astype(bf16)
            kh_ref[:, sl] = (kk_ref[:, sl] + krr).astype(bf16)
        vh_ref[...] = kvv_ref[...].astype(bf16)

    wide = lambda c: pl.BlockSpec((ts, MLA_PAD), lambda i: (i, c))
    tab = pl.BlockSpec((ts, LANES), lambda i: (i, 0))
    return _pc(body, grid=(S // ts,),
               in_specs=[wide(0), wide(0), wide(1), pl.BlockSpec((ts, LANES), lambda i: (i, 3)), tab, tab, tab],
               out_specs=[wide(0)] * 3, out_shape=[jax.ShapeDtypeStruct((S, MLA_PAD), bf16)] * 3,
               compiler_params=_cp(("parallel",)), name=name)(q, kv, kv, proj, *tabs)


def _rope_bwd(dqh, dkh, dvh, tabs, name="rope_bwd"):
    S = dqh.shape[0]
    ts = _pick(S, (512, 256, 128))
    scale = (QK_NOPE + QK_ROPE) ** -0.5

    def body(dq_ref, dk_ref, dv_ref, tc_ref, s1_ref, s2_ref, oq_ref, okv_ref, okr_ref):
        tc, s1, s2 = tc_ref[...], s1_ref[...], s2_ref[...]
        ksum = None
        for h in range(MLA_HEADS):
            sl = slice(h * HEAD_PAD, (h + 1) * HEAD_PAD)
            oq_ref[:, sl] = (_rope_apply_t(dq_ref[:, sl], tc, s1, s2) * scale).astype(bf16)
            dk = dk_ref[:, sl]
            okv_ref[:, sl] = dk.astype(bf16)
            ksum = dk if ksum is None else ksum + dk
        okv_ref[:, MLA_PAD:] = dv_ref[...].astype(bf16)
        dkr = pltpu.roll(_rope_apply_t(ksum, tc, s1, s2), LANES - QK_NOPE, 1)
        lane = lax.broadcasted_iota(jnp.int32, (ts, LANES), 1)
        okr_ref[...] = jnp.where(lane < QK_ROPE, dkr, 0.0).astype(bf16)

    wide = pl.BlockSpec((ts, MLA_PAD), lambda i: (i, 0))
    tab = pl.BlockSpec((ts, LANES), lambda i: (i, 0))
    return _pc(body, grid=(S // ts,), in_specs=[wide, wide, wide, tab, tab, tab],
               out_specs=[wide, pl.BlockSpec((ts, 2 * MLA_PAD), lambda i: (i, 0)), tab],
               out_shape=[jax.ShapeDtypeStruct((S, MLA_PAD), bf16), jax.ShapeDtypeStruct((S, 2 * MLA_PAD), bf16),
                          jax.ShapeDtypeStruct((S, LANES), bf16)],
               compiler_params=_cp(("parallel",)), name=name)(dqh, dkh, dvh, *tabs)


ATT_BLK = 1024
_DIAG_QUARTERS = ((0, 0), (1, 0), (1, 1))


def _attn_fwd(qh, kh, vh, plan=None, name="attn_fwd"):
    S = qh.shape[0]
    tq = tk = min(S, ATT_BLK)
    nq, nk = S // tq, S // tk
    npl = plan['n'] if plan else 0

    def body(*refs):
        q_ref, k_ref, v_ref = refs[:3]
        o_ref, lse_ref = refs[3 + npl:5 + npl]
        m_sc, l_sc, acc_sc = refs[5 + 2 * npl:8 + 2 * npl]
        h, i, j = pl.program_id(0), pl.program_id(1), pl.program_id(2)
        if plan:
            pargs = (refs[3:3 + npl], refs[5 + npl:5 + 2 * npl], refs[8 + 2 * npl], refs[9 + 2 * npl])
            first = (i == 0) & (j == 0)
            pl.when((h == 0) & first)(functools.partial(plan['start'], *pargs))
            pl.when((h == (3 * MLA_HEADS) // 4) & first)(functools.partial(plan['forward'], *pargs))
            pl.when((h == MLA_HEADS - 1) & (i == nq - 1) & (j == nk - 1))(functools.partial(plan['finish'], *pargs))

        @pl.when(j == 0)
        def _():
            m_sc[...] = jnp.full_like(m_sc, -1e30)
            l_sc[...] = jnp.zeros_like(l_sc)
            acc_sc[...] = jnp.zeros_like(acc_sc)

        def part(rows, cols, n, masked):
            s = lax.dot_general(q_ref[rows, :], k_ref[cols, :], (((1,), (1,)), ((), ())), preferred_element_type=f32)
            if masked:
                row = lax.broadcasted_iota(jnp.int32, (n, n), 0)
                col = lax.broadcasted_iota(jnp.int32, (n, n), 1)
                s = jnp.where(col <= row, s, -1e30)
            m_prev = m_sc[rows, :]
            m_new = jnp.maximum(m_prev, jnp.max(s, axis=-1, keepdims=True))
            alpha = jnp.exp(m_prev - m_new)
            p = jnp.exp(s - m_new)
            l_sc[rows, :] = alpha * l_sc[rows, :] + jnp.sum(p, axis=-1, keepdims=True)
            acc_sc[rows, :] = alpha * acc_sc[rows, :] + jnp.dot(p.astype(bf16), v_ref[cols, :],
                                                                  preferred_element_type=f32)
            m_sc[rows, :] = m_new

        whole = (slice(0, tq), slice(0, tk), tq)
        pl.when(j < i)(functools.partial(part, *whole, False))
        pl.when(j == i)(functools.partial(part, *whole, True))

        @pl.when(j == nk - 1)
        def _():
            l = l_sc[...]
            o_ref[...] = acc_sc[...] / l
            lse_ref[...] = jnp.broadcast_to(m_sc[...] + jnp.log(l), (tq, LANES))

    qspec = pl.BlockSpec((tq, HEAD_PAD), lambda h, i, j: (i, h))
    kspec = pl.BlockSpec((tk, HEAD_PAD), lambda h, i, j: (jnp.minimum(j, i), h))
    anyspec = pl.BlockSpec(memory_space=pl.ANY)
    scratch = [pltpu.VMEM((tq, 1), f32), pltpu.VMEM((tq, 1), f32), pltpu.VMEM((tq, HEAD_PAD), f32)]
    if plan:
        scratch += [pltpu.SemaphoreType.DMA((plan['nsem'],)), pltpu.SemaphoreType.DMA((plan['nsem'],))]
    outs = _pc(body, grid=(MLA_HEADS, nq, nk), in_specs=[qspec, kspec, kspec] + [anyspec] * npl,
               out_specs=[qspec, qspec] + [anyspec] * npl,
               out_shape=[jax.ShapeDtypeStruct((S, MLA_PAD), f32)] * 2 + (plan['outs'] if plan else []),
               scratch_shapes=scratch,
               compiler_params=_cp(("arbitrary", "arbitrary", "arbitrary") if plan else ("parallel", "parallel", "arbitrary")),
               name=name)(qh, kh, vh, *(plan['ins'] if plan else []))
    return outs[0], outs[1], outs[2:]


def _attn_bwd(qh, kh, vh, do, lse, delta, plan=None, name="attn_bwd"):
    S = qh.shape[0]
    tq = tk = min(S, ATT_BLK)
    nq, nk = S // tq, S // tk
    npl = plan['n'] if plan else 0

    def body(*refs):
        q_ref, k_ref, v_ref, do_ref, lse_ref, dl_ref = refs[:6]
        dq_ref, dk_ref, dv_ref = refs[6 + npl:9 + npl]
        h, j, i = pl.program_id(0), pl.program_id(1), pl.program_id(2)
        if plan:
            pargs = (refs[6:6 + npl], refs[9 + npl:9 + 2 * npl], refs[9 + 2 * npl], refs[10 + 2 * npl])
            pl.when((h == 0) & (j == 0) & (i == 0))(functools.partial(plan['start'], *pargs))
            pl.when((h == MLA_HEADS - 1) & (j == nk - 1) & (i == nq - 1))(functools.partial(plan['finish'], *pargs))

        @pl.when((j == 0) & (i == 0))
        def _():
            dq_ref[...] = jnp.zeros_like(dq_ref)

        @pl.when(i == 0)
        def _():
            dk_ref[...] = jnp.zeros_like(dk_ref)
            dv_ref[...] = jnp.zeros_like(dv_ref)

        def part(r0, c0, n, masked):
            nt = (((1,), (1,)), ((), ()))
            tn = (((0,), (0,)), ((), ()))
            rows, cols = slice(r0, r0 + n), slice(c0, c0 + n)
            qv, kv_, dov = q_ref[rows, :], k_ref[cols, :], do_ref[rows, :]
            s = lax.dot_general(qv, kv_, nt, preferred_element_type=f32)
            p = jnp.exp(s - lse_ref[rows, :1])
            if masked:
                row = lax.broadcasted_iota(jnp.int32, (n, n), 0)
                col = lax.broadcasted_iota(jnp.int32, (n, n), 1)
                p = jnp.where(col <= row, p, 0.0)
            dp = lax.dot_general(dov, v_ref[cols, :], nt, preferred_element_type=f32)
            ds = (p * (dp - dl_ref[rows, :1])).astype(bf16)
            dv_ref[cols, :] += lax.dot_general(p.astype(bf16), dov, tn, preferred_element_type=f32)
            dk_ref[cols, :] += lax.dot_general(ds, qv, tn, preferred_element_type=f32)
            qrows = pl.ds(pl.multiple_of(i * tq + r0, n), n)
            dq_ref[qrows, :] += jnp.dot(ds, kv_, preferred_element_type=f32)

        def below():
            part(0, 0, tq, False)

        def diagonal():
            for r0, c0 in _DIAG_QUARTERS:
                part(r0 * hq, c0 * hq, hq, r0 == c0)

        hq = tq // 2
        pl.when(i > j)(below)
        pl.when(i == j)(diagonal)

    qspec = pl.BlockSpec((tq, HEAD_PAD), lambda h, j, i: (jnp.maximum(i, j), h))
    kspec = pl.BlockSpec((tk, HEAD_PAD), lambda h, j, i: (j, h))
    colspec = pl.BlockSpec((S, HEAD_PAD), lambda h, j, i: (0, h))
    anyspec = pl.BlockSpec(memory_space=pl.ANY)
    scratch = [pltpu.SemaphoreType.DMA((plan['nsem'],)), pltpu.SemaphoreType.DMA((plan['nsem'],))] if plan else []
    outs = _pc(body, grid=(MLA_HEADS, nk, nq), in_specs=[qspec, kspec, kspec, qspec, qspec, qspec] + [anyspec] * npl,
               out_specs=[colspec, kspec, kspec] + [anyspec] * npl,
               out_shape=[jax.ShapeDtypeStruct((S, MLA_PAD), f32)] * 3 + (plan['outs'] if plan else []),
               scratch_shapes=scratch,
               compiler_params=_cp(("arbitrary" if plan else "parallel", "arbitrary", "arbitrary")),
               name=name)(qh, kh, vh, do, lse, delta, *(plan['ins'] if plan else []))
    return outs[0], outs[1], outs[2], outs[3:]


def _xattn_fwd(q, kv, name="xattn_fwd"):
    S = q.shape[0]
    M = kv.shape[0]
    tq = _pick(S, (1024, 512, 256, 128))
    scale = X_HEAD_DIM ** -0.5

    def body(q_ref, kv_ref, o_ref):
        for h in range(X_HEADS):
            sl = slice(h * X_HEAD_DIM, (h + 1) * X_HEAD_DIM)
            k = kv_ref[:, sl]
            v = kv_ref[:, D_MODEL + h * X_HEAD_DIM:D_MODEL + (h + 1) * X_HEAD_DIM]
            s = lax.dot_general(q_ref[:, sl], k, (((1,), (1,)), ((), ())), preferred_element_type=f32) * scale
            e = jnp.exp(s - jnp.max(s, axis=-1, keepdims=True))
            p = e / jnp.sum(e, axis=-1, keepdims=True)
            o_ref[:, sl] = jnp.dot(p.astype(bf16), v, preferred_element_type=f32).astype(bf16)

    blk = pl.BlockSpec((tq, D_MODEL), lambda i: (i, 0))
    return _pc(body, grid=(S // tq,), in_specs=[blk, pl.BlockSpec((M, 2 * D_MODEL), lambda i: (0, 0))],
               out_specs=blk, out_shape=jax.ShapeDtypeStruct((S, D_MODEL), bf16),
               compiler_params=_cp(("parallel",)), name=name)(q, kv)


def _xattn_bwd(q, kv, do, plan=None, name="xattn_bwd"):
    S = q.shape[0]
    M = kv.shape[0]
    tq = _pick(S, (512, 256, 128))
    scale = X_HEAD_DIM ** -0.5
    npl = plan['n'] if plan else 0
    p_in, p_out, p_shapes, p_sems, p_args = _plan_extras(plan)

    def body(*refs):
        q_ref, kv_ref, do_ref = refs[:3]
        dq_ref, dkv_ref = refs[3 + npl:5 + npl]
        i = pl.program_id(0)
        if plan:
            pargs = (refs[3:3 + npl], refs[5 + npl:5 + 2 * npl], refs[5 + 2 * npl], refs[6 + 2 * npl])
            pl.when(i == 0)(functools.partial(plan['start'], *pargs))
            pl.when(i == S // tq - 1)(functools.partial(plan['finish'], *pargs))

        @pl.when(i == 0)
        def _():
            dkv_ref[...] = jnp.zeros_like(dkv_ref)

        nt = (((1,), (1,)), ((), ()))
        tn = (((0,), (0,)), ((), ()))
        for h in range(X_HEADS):
            sl = slice(h * X_HEAD_DIM, (h + 1) * X_HEAD_DIM)
            vsl = slice(D_MODEL + h * X_HEAD_DIM, D_MODEL + (h + 1) * X_HEAD_DIM)
            k, v, qv, dov = kv_ref[:, sl], kv_ref[:, vsl], q_ref[:, sl], do_ref[:, sl]
            s = lax.dot_general(qv, k, nt, preferred_element_type=f32) * scale
            e = jnp.exp(s - jnp.max(s, axis=-1, keepdims=True))
            p = e / jnp.sum(e, axis=-1, keepdims=True)
            dp = lax.dot_general(dov, v, nt, preferred_element_type=f32)
            ds = (p * (dp - jnp.sum(dp * p, axis=-1, keepdims=True)) * scale).astype(bf16)
            dq_ref[:, sl] = jnp.dot(ds, k, preferred_element_type=f32).astype(bf16)
            dkv_ref[:, sl] += lax.dot_general(ds, qv, tn, preferred_element_type=f32)
            dkv_ref[:, vsl] += lax.dot_general(p.astype(bf16), dov, tn, preferred_element_type=f32)

    blk = pl.BlockSpec((tq, D_MODEL), lambda i: (i, 0))
    full = pl.BlockSpec((M, 2 * D_MODEL), lambda i: (0, 0))
    outs = _pc(body, grid=(S // tq,), in_specs=[blk, full, blk] + p_in, out_specs=[blk, full] + p_out,
               out_shape=[jax.ShapeDtypeStruct((S, D_MODEL), bf16), jax.ShapeDtypeStruct((M, 2 * D_MODEL), f32)] + p_shapes,
               scratch_shapes=p_sems, compiler_params=_cp(("arbitrary",)), name=name)(q, kv, do, *p_args)
    return outs[0], outs[1], outs[2:]


ST_CHUNKS = 1


def _apow_init(a_ref, ap_ref, bp_ref, seg):
    P = MACRO_ST
    ar, ai = a_ref[:, :P], a_ref[:, P:]
    pr, pi = ar, ai
    for r in range(seg):
        ap_ref[r:r + 1, :P] = pr
        ap_ref[r:r + 1, P:] = pi
        if r < seg - 1:
            pr, pi = pr * ar - pi * ai, pr * ai + pi * ar
    br, bi = pr, pi
    for k in range(SUBLANES):
        bp_ref[k:k + 1, :P] = pr
        bp_ref[k:k + 1, P:] = pi
        pr, pi = pr * br - pi * bi, pr * bi + pi * br


def _segment_perm(tS):
    seg = tS // SUBLANES
    rows = jnp.arange(tS)
    src = (rows % SUBLANES) * seg + rows // SUBLANES
    return (src[:, None] == jnp.arange(tS)[None, :]).astype(f32)


def _unpermute_rows(pt, v):
    hi = v.astype(bf16)
    r1 = v - hi.astype(f32)
    mid = r1.astype(bf16)
    lo = (r1 - mid.astype(f32)).astype(bf16)
    out = jnp.dot(pt, jnp.concatenate([hi, mid, lo], axis=1), preferred_element_type=f32)
    w = v.shape[1]
    return (out[:, :w] + out[:, w:2 * w]) + out[:, 2 * w:]


def _scan_block(sc_ref, ap_ref, bp_ref, carry_ref, e_ref, seg, reverse):
    P = MACRO_ST
    sgn = -1.0 if reverse else 1.0
    CH = P // ST_CHUNKS
    rid = lax.broadcasted_iota(jnp.int32, (SUBLANES, CH), 0)
    for c in range(ST_CHUNKS):
        lr, li = slice(c * CH, (c + 1) * CH), slice(P + c * CH, P + (c + 1) * CH)
        ar, ai = ap_ref[0:1, lr], sgn * ap_ref[0:1, li]
        xr = xi = None
        for i in range(seg):
            r = seg - 1 - i if reverse else i
            rows = slice(SUBLANES * r, SUBLANES * (r + 1))
            sr, si = sc_ref[rows, lr], sc_ref[rows, li]
            if i == 0:
                xr, xi = sr, si
            else:
                xr, xi = ar * xr - ai * xi + sr, ar * xi + ai * xr + si
                sc_ref[rows, lr] = xr
                sc_ref[rows, li] = xi
        for sh in (1, 2, 4):
            pr, pi = bp_ref[sh - 1:sh, lr], sgn * bp_ref[sh - 1:sh, li]
            if reverse:
                tr = jnp.where(rid < SUBLANES - sh, pltpu.roll(xr, SUBLANES - sh, 0), 0.0)
                ti = jnp.where(rid < SUBLANES - sh, pltpu.roll(xi, SUBLANES - sh, 0), 0.0)
            else:
                tr = jnp.where(rid >= sh, pltpu.roll(xr, sh, 0), 0.0)
                ti = jnp.where(rid >= sh, pltpu.roll(xi, sh, 0), 0.0)
            xr, xi = xr + pr * tr - pi * ti, xi + pr * ti + pi * tr
        if reverse:
            bpr = jnp.zeros((SUBLANES, CH), f32)
            bpi = jnp.zeros((SUBLANES, CH), f32)
            for r in range(SUBLANES):
                bpr = jnp.where(rid == r, bp_ref[SUBLANES - 1 - r:SUBLANES - r, lr], bpr)
                bpi = jnp.where(rid == r, -bp_ref[SUBLANES - 1 - r:SUBLANES - r, li], bpi)
        else:
            bpr, bpi = bp_ref[:, lr], bp_ref[:, li]
        cr, cim = carry_ref[:, lr], carry_ref[:, li]
        xr, xi = xr + bpr * cr - bpi * cim, xi + bpr * cim + bpi * cr
        edge = 0 if reverse else SUBLANES - 1
        carry_ref[:, lr] = jnp.sum(jnp.where(rid == edge, xr, 0.0), axis=0, keepdims=True)
        carry_ref[:, li] = jnp.sum(jnp.where(rid == edge, xi, 0.0), axis=0, keepdims=True)
        if reverse:
            er = jnp.where(rid == SUBLANES - 1, cr, pltpu.roll(xr, SUBLANES - 1, 0))
            ei = jnp.where(rid == SUBLANES - 1, cim, pltpu.roll(xi, SUBLANES - 1, 0))
        else:
            er = jnp.where(rid == 0, cr, pltpu.roll(xr, 1, 0))
            ei = jnp.where(rid == 0, cim, pltpu.roll(xi, 1, 0))
        if e_ref is not None:
            e_ref[:, lr] = er
            e_ref[:, li] = ei
        for i in range(seg):
            r = seg - 1 - i if reverse else i
            rows = slice(SUBLANES * r, SUBLANES * (r + 1))
            pr, pi = ap_ref[i:i + 1, lr], sgn * ap_ref[i:i + 1, li]
            sc_ref[rows, lr] += pr * er - pi * ei
            sc_ref[rows, li] += pr * ei + pi * er


def _ssm_fwd(proj, bm, cm, a, d, name="ssm_fwd"):
    S = proj.shape[0]
    tS = _pick(S, (256, 128))
    nb = S // tS
    P2 = 2 * MACRO_ST
    seg = tS // SUBLANES
    ucol0 = (D_MODEL - SSM_WIDTH) // MACRO_CH

    perm = _segment_perm(tS)

    def body(u_ref, b_ref, c_ref, a_ref, d_ref, pm_ref, pt_ref, y_ref, xc_ref, bu_sc, ap_sc, bp_sc, car_sc):
        t = pl.program_id(1)

        @pl.when(t == 0)
        def _():
            _apow_init(a_ref, ap_sc, bp_sc, seg)
            car_sc[...] = jnp.zeros_like(car_sc)

        uv = u_ref[...]
        up = jnp.dot(pm_ref[...], uv.astype(bf16), preferred_element_type=f32).astype(bf16)
        bu_sc[...] = jnp.dot(up, b_ref[...], preferred_element_type=f32)
        xc_ref[...] = car_sc[...]
        _scan_block(bu_sc, ap_sc, bp_sc, car_sc, None, seg, False)
        yp = jnp.dot(bu_sc[...].astype(bf16), c_ref[...], preferred_element_type=f32)
        y_ref[...] = _unpermute_rows(pt_ref[...], yp) + d_ref[...] * uv

    sq = pl.BlockSpec((tS, tS), lambda m, t: (0, 0))
    return _pc(body, grid=(SSM_MACRO, nb),
               in_specs=[pl.BlockSpec((tS, MACRO_CH), lambda m, t: (t, ucol0 + m)),
                         pl.BlockSpec((None, MACRO_CH, P2), lambda m, t: (m, 0, 0)),
                         pl.BlockSpec((None, P2, MACRO_CH), lambda m, t: (m, 0, 0)),
                         pl.BlockSpec((None, 1, P2), lambda m, t: (m, 0, 0)),
                         pl.BlockSpec((1, MACRO_CH), lambda m, t: (0, m)), sq, sq],
               out_specs=[pl.BlockSpec((tS, MACRO_CH), lambda m, t: (t, m)),
                          pl.BlockSpec((None, None, 1, P2), lambda m, t: (m, t, 0, 0))],
               out_shape=[jax.ShapeDtypeStruct((S, SSM_WIDTH), f32), jax.ShapeDtypeStruct((SSM_MACRO, nb, 1, P2), f32)],
               scratch_shapes=[pltpu.VMEM((tS, P2), f32), pltpu.VMEM((seg, P2), f32),
                               pltpu.VMEM((SUBLANES, P2), f32), pltpu.VMEM((1, P2), f32)],
               compiler_params=_cp(("arbitrary", "arbitrary")), name=name)(
        proj, bm, cm, a, d.reshape(1, SSM_WIDTH), perm.astype(bf16), perm.T.astype(bf16))


def _ssm_bwd(proj, dy, xc, bm, cm, a, d, plan=None, name="ssm_bwd"):
    S = proj.shape[0]
    tS = _pick(S, (512, 256, 128))
    nb = S // tS
    P = MACRO_ST
    P2 = 2 * P
    seg = tS // SUBLANES
    ucol0 = (D_MODEL - SSM_WIDTH) // MACRO_CH

    perm = _segment_perm(tS)
    npl = plan['n'] if plan else 0
    p_in, p_out, p_shapes, p_sems, p_args = _plan_extras(plan)

    def body(*refs):
        u_ref, dy_ref, xc_ref, b_ref, c_ref, a_ref, d_ref, pm_ref, pt_ref = refs[:9]
        du_ref, db_ref, dc_ref, da_ref, dd_ref = refs[9 + npl:14 + npl]
        x_sc, g_sc, ap_sc, bp_sc, e_sc, xcar_sc, gcar_sc = refs[14 + 2 * npl:21 + 2 * npl]
        t = pl.program_id(1)
        if plan:
            mg = pl.program_id(0)
            pargs = (refs[9:9 + npl], refs[14 + npl:14 + 2 * npl], refs[21 + 2 * npl], refs[22 + 2 * npl])
            pl.when((mg == 0) & (t == 0))(functools.partial(plan['start'], *pargs))
            pl.when((mg == SSM_MACRO - 1) & (t == nb - 1))(functools.partial(plan['finish'], *pargs))

        @pl.when(t == 0)
        def _():
            _apow_init(a_ref, ap_sc, bp_sc, seg)
            gcar_sc[...] = jnp.zeros_like(gcar_sc)
            db_ref[...] = jnp.zeros_like(db_ref)
            dc_ref[...] = jnp.zeros_like(dc_ref)
            da_ref[...] = jnp.zeros_like(da_ref)
            dd_ref[...] = jnp.zeros_like(dd_ref)

        nt = (((1,), (1,)), ((), ()))
        tn = (((0,), (0,)), ((), ()))
        uv = u_ref[...]
        dyv = dy_ref[...]
        pm = pm_ref[...]
        ub = jnp.dot(pm, uv.astype(bf16), preferred_element_type=f32).astype(bf16)
        dyb = jnp.dot(pm, dyv.astype(bf16), preferred_element_type=f32).astype(bf16)
        x_sc[...] = jnp.dot(ub, b_ref[...], preferred_element_type=f32)
        xcar_sc[...] = xc_ref[...]
        _scan_block(x_sc, ap_sc, bp_sc, xcar_sc, e_sc, seg, False)
        g_sc[...] = lax.dot_general(dyb, c_ref[...], nt, preferred_element_type=f32)
        _scan_block(g_sc, ap_sc, bp_sc, gcar_sc, None, seg, True)
        xv = x_sc[...]
        gv = g_sc[...]
        gb = gv.astype(bf16)
        dc_ref[...] += lax.dot_general(xv.astype(bf16), dyb, tn, preferred_element_type=f32)
        db_ref[...] += lax.dot_general(ub, gb, tn, preferred_element_type=f32)
        dup = lax.dot_general(gb, b_ref[...], nt, preferred_element_type=f32)
        du_ref[...] = _unpermute_rows(pt_ref[...], dup) + d_ref[...] * dyv
        dd_ref[...] += jnp.sum(dyv * uv, axis=0, keepdims=True)
        xp = jnp.concatenate([e_sc[...], xv[:tS - SUBLANES]], axis=0)
        xpr, xpi, ggr, ggi = xp[:, :P], xp[:, P:], gv[:, :P], gv[:, P:]
        da_ref[:, :P] += jnp.sum(ggr * xpr + ggi * xpi, axis=0, keepdims=True)
        da_ref[:, P:] += jnp.sum(ggi * xpr - ggr * xpi, axis=0, keepdims=True)

    rev = lambda t: nb - 1 - t
    outs = _pc(body, grid=(SSM_MACRO, nb),
               in_specs=[pl.BlockSpec((tS, MACRO_CH), lambda m, t: (rev(t), ucol0 + m)),
                         pl.BlockSpec((tS, MACRO_CH), lambda m, t: (rev(t), m)),
                         pl.BlockSpec((None, None, 1, P2), lambda m, t: (m, rev(t) * (xc.shape[1] // nb), 0, 0)),
                         pl.BlockSpec((None, MACRO_CH, P2), lambda m, t: (m, 0, 0)),
                         pl.BlockSpec((None, P2, MACRO_CH), lambda m, t: (m, 0, 0)),
                         pl.BlockSpec((None, 1, P2), lambda m, t: (m, 0, 0)),
                         pl.BlockSpec((1, MACRO_CH), lambda m, t: (0, m)),
                         pl.BlockSpec((tS, tS), lambda m, t: (0, 0)), pl.BlockSpec((tS, tS), lambda m, t: (0, 0))] + p_in,
               out_specs=[pl.BlockSpec((tS, MACRO_CH), lambda m, t: (rev(t), m)),
                          pl.BlockSpec((None, MACRO_CH, P2), lambda m, t: (m, 0, 0)),
                          pl.BlockSpec((None, P2, MACRO_CH), lambda m, t: (m, 0, 0)),
                          pl.BlockSpec((None, 1, P2), lambda m, t: (m, 0, 0)),
                          pl.BlockSpec((1, MACRO_CH), lambda m, t: (0, m))] + p_out,
               out_shape=[jax.ShapeDtypeStruct((S, SSM_WIDTH), f32),
                          jax.ShapeDtypeStruct((SSM_MACRO, MACRO_CH, P2), f32),
                          jax.ShapeDtypeStruct((SSM_MACRO, P2, MACRO_CH), f32),
                          jax.ShapeDtypeStruct((SSM_MACRO, 1, P2), f32),
                          jax.ShapeDtypeStruct((1, SSM_WIDTH), f32)] + p_shapes,
               scratch_shapes=[pltpu.VMEM((tS, P2), f32), pltpu.VMEM((tS, P2), f32),
                               pltpu.VMEM((seg, P2), f32), pltpu.VMEM((SUBLANES, P2), f32), pltpu.VMEM((SUBLANES, P2), f32),
                               pltpu.VMEM((1, P2), f32), pltpu.VMEM((1, P2), f32)] + p_sems,
               compiler_params=_cp(("arbitrary", "arbitrary")), name=name)(
        proj, dy, xc, bm, cm, a, d.reshape(1, SSM_WIDTH), perm.astype(bf16), perm.T.astype(bf16), *p_args)
    return outs[0], outs[1], outs[2], outs[3], outs[4], outs[5:]


_GELU_K = math.sqrt(2.0 / math.pi)
_GELU_C = 0.044715


def _glu_fwd(y, w, b, g, name="glu_fwd"):
    S, W = y.shape
    ts = _pick(S, (512, 256, 128))

    def body(y_ref, w_ref, b_ref, g_ref, z_ref, sn_ref, ge_ref):
        yv = y_ref[...]
        cdf = 0.5 * (1.0 + jnp.tanh(_GELU_K * (yv + _GELU_C * (yv * yv * yv))))
        ge = (yv * cdf).astype(bf16)
        z = jnp.dot(ge, w_ref[...], preferred_element_type=f32) + b_ref[...]
        s = yv * jax.nn.sigmoid(z)
        rstd = lax.rsqrt(jnp.sum(s * s, axis=-1, keepdims=True) * (1.0 / W) + EPS)
        z_ref[...] = z
        sn_ref[...] = (s * rstd * g_ref[...]).astype(bf16)
        ge_ref[...] = ge

    blk = pl.BlockSpec((ts, W), lambda i: (i, 0))
    row = pl.BlockSpec((1, W), lambda i: (0, 0))
    return _pc(body, grid=(S // ts,), in_specs=[blk, pl.BlockSpec((W, W), lambda i: (0, 0)), row, row],
               out_specs=[blk, blk, blk],
               out_shape=[jax.ShapeDtypeStruct((S, W), f32), jax.ShapeDtypeStruct((S, W), bf16),
                          jax.ShapeDtypeStruct((S, W), bf16)],
               compiler_params=_cp(("parallel",)), name=name)(y, w, b.reshape(1, W), g.reshape(1, W))


def _glu_bwd(y, z, dmixed, w, g, name="glu_bwd"):
    S, W = y.shape
    ts = _pick(S, (512, 256, 128))
    dcb = MLA_PAD // W

    def body(y_ref, z_ref, dsn_ref, w_ref, g_ref, dy_ref, dz_ref, dg_ref, db_ref):
        i = pl.program_id(0)
        yv, zv, gv = y_ref[...], z_ref[...], g_ref[...]
        sig = jax.nn.sigmoid(zv)
        s = yv * sig
        rstd = lax.rsqrt(jnp.sum(s * s, axis=-1, keepdims=True) * (1.0 / W) + EPS)
        sh = s * rstd
        dsn = dsn_ref[...]
        dsh = dsn * gv
        ds = rstd * (dsh - sh * (jnp.sum(dsh * sh, axis=-1, keepdims=True) * (1.0 / W)))
        dz = ds * s * (1.0 - sig)
        dzb = dz.astype(bf16)
        dge = lax.dot_general(dzb, w_ref[...], (((1,), (1,)), ((), ())), preferred_element_type=f32)
        t = jnp.tanh(_GELU_K * (yv + _GELU_C * (yv * yv * yv)))
        dgelu = 0.5 * (1.0 + t) + 0.5 * yv * (1.0 - t * t) * _GELU_K * (1.0 + 3.0 * _GELU_C * yv * yv)
        dy_ref[...] = ds * sig + dge * dgelu
        dz_ref[...] = dzb

        @pl.when(i == 0)
        def _():
            dg_ref[...] = jnp.zeros_like(dg_ref)
            db_ref[...] = jnp.zeros_like(db_ref)

        dg_ref[...] += jnp.sum(dsn * sh, axis=0, keepdims=True)
        db_ref[...] += jnp.sum(dz, axis=0, keepdims=True)

    blk = pl.BlockSpec((ts, W), lambda i: (i, 0))
    row = pl.BlockSpec((1, W), lambda i: (0, 0))
    return _pc(body, grid=(S // ts,),
               in_specs=[blk, blk, pl.BlockSpec((ts, W), lambda i: (i, dcb)), pl.BlockSpec((W, W), lambda i: (0, 0)), row],
               out_specs=[blk, blk, row, row],
               out_shape=[jax.ShapeDtypeStruct((S, W), f32), jax.ShapeDtypeStruct((S, W), bf16),
                          jax.ShapeDtypeStruct((1, W), f32), jax.ShapeDtypeStruct((1, W), f32)],
               compiler_params=_cp(("arbitrary",)), name=name)(y, z, dmixed, w, g.reshape(1, W))


def _ffn_up(hn, wg, wu, name="ffn_up"):
    S, K = hn.shape
    F = wg.shape[0]
    tm, tn = _pick(S, (512, 256, 128)), _pick(F, (1408, 256, 128))

    def body(h_ref, wg_ref, wu_ref, g_ref, u_ref, a_ref):
        hv = h_ref[...]
        nt = (((1,), (1,)), ((), ()))
        gv = lax.dot_general(hv, wg_ref[...], nt, preferred_element_type=f32)
        uv = lax.dot_general(hv, wu_ref[...], nt, preferred_element_type=f32)
        g_ref[...] = gv.astype(bf16)
        u_ref[...] = uv.astype(bf16)
        a_ref[...] = (gv * jax.nn.sigmoid(gv) * uv).astype(bf16)

    wspec = pl.BlockSpec((tn, K), lambda i, j: (j, 0))
    ospec = pl.BlockSpec((tm, tn), lambda i, j: (i, j))
    return _pc(body, grid=(S // tm, F // tn), in_specs=[pl.BlockSpec((tm, K), lambda i, j: (i, 0)), wspec, wspec],
               out_specs=[ospec] * 3,
               out_shape=[jax.ShapeDtypeStruct((S, F), bf16), jax.ShapeDtypeStruct((S, F), bf16),
                          jax.ShapeDtypeStruct((S, F), bf16)],
               compiler_params=_cp(("parallel", "parallel")), name=name)(hn, wg, wu)


def _ffn_bwd_act(dh, wd, gate, up, name="ffn_bwd_act"):
    S, K = dh.shape
    F = wd.shape[0]
    tm, tn = _pick(S, (512, 256, 128)), _pick(F, (1408, 256, 128))

    def body(dh_ref, wd_ref, g_ref, u_ref, dg_ref, du_ref):
        dact = lax.dot_general(dh_ref[...].astype(bf16), wd_ref[...], (((1,), (1,)), ((), ())),
                               preferred_element_type=f32)
        gv, uv = g_ref[...].astype(f32), u_ref[...].astype(f32)
        sig = jax.nn.sigmoid(gv)
        dg_ref[...] = (dact * uv * (sig * (1.0 + gv * (1.0 - sig)))).astype(bf16)
        du_ref[...] = (dact * (gv * sig)).astype(bf16)

    ospec = pl.BlockSpec((tm, tn), lambda i, j: (i, j))
    return _pc(body, grid=(S // tm, F // tn),
               in_specs=[pl.BlockSpec((tm, K), lambda i, j: (i, 0)), pl.BlockSpec((tn, K), lambda i, j: (j, 0)),
                         ospec, ospec],
               out_specs=[ospec] * 2, out_shape=[jax.ShapeDtypeStruct((S, F), bf16)] * 2,
               compiler_params=_cp(("parallel", "parallel")), name=name)(dh, wd, gate, up)


def _pad_heads(w, per_head, pieces):
    K = w.shape[0]
    w3 = w.reshape(K, MLA_HEADS, per_head)
    out = jnp.zeros((K, MLA_HEADS, HEAD_PAD), w.dtype)
    for s0, s1, d0 in pieces:
        out = out.at[:, :, d0:d0 + (s1 - s0)].set(w3[:, :, s0:s1])
    return out.reshape(K, MLA_PAD)


def _unpad_heads(wp, per_head, pieces):
    K = wp.shape[0]
    w3 = wp.reshape(K, MLA_HEADS, HEAD_PAD)
    out = jnp.zeros((K, MLA_HEADS, per_head), wp.dtype)
    for s0, s1, d0 in pieces:
        out = out.at[:, :, s0:s1].set(w3[:, :, d0:d0 + (s1 - s0)])
    return out.reshape(K, MLA_HEADS * per_head)


_Q_PIECES = [(0, QK_NOPE + QK_ROPE, 0)]
_K_PIECES = [(0, QK_NOPE, 0)]
_V_PIECES = [(QK_NOPE, QK_NOPE + V_HEAD, 0)]
_KR0 = Q_LORA + KV_LORA


def _pack_win(w):
    z = jnp.zeros((w.shape[0], LANES - QK_ROPE), w.dtype)
    return jnp.concatenate([w[:, :_KR0 + QK_ROPE], z, w[:, _KR0 + QK_ROPE:]], axis=1)


def _unpack_win(wp):
    return jnp.concatenate([wp[:, :_KR0 + QK_ROPE], wp[:, _KR0 + LANES:]], axis=1)


def _pack_wout(w):
    wa = w[:MLA_WIDTH].reshape(MLA_HEADS, V_HEAD, D_MODEL)
    wa = jnp.concatenate([wa, jnp.zeros_like(wa)], axis=1).reshape(MLA_PAD, D_MODEL)
    return jnp.concatenate([wa, w[MLA_WIDTH:]], axis=0)


def _unpack_wout(wp):
    wa = wp[:MLA_PAD].reshape(MLA_HEADS, HEAD_PAD, D_MODEL)[:, :V_HEAD].reshape(MLA_WIDTH, D_MODEL)
    return jnp.concatenate([wa, wp[MLA_PAD:]], axis=0)


def _pad_gain(g):
    g2 = g.reshape(MLA_HEADS, V_HEAD)
    return jnp.concatenate([g2, jnp.zeros_like(g2)], axis=1).reshape(MLA_PAD)


def _unpad_gain(gp):
    return gp.reshape(MLA_HEADS, HEAD_PAD)[:, :V_HEAD].reshape(MLA_WIDTH)


def _ssm_prep(lam_re, lam_im, log_dt, b_re, b_im, c_re, c_im):
    lam = lax.complex(lam_re, lam_im)
    dt = jnp.exp(log_dt)[:, None]
    a_bar = jnp.exp(lam * dt)
    b_bar = ((a_bar - 1.0) / lam)[:, None, :] * lax.complex(b_re, b_im)
    G8 = SSM_GROUPS // SSM_MACRO
    eye = jnp.eye(G8, dtype=f32)

    def bmat(part):
        p4 = part.reshape(SSM_MACRO, G8, SSM_GROUP, SSM_STATE)
        return jnp.einsum('mgcp,gh->mgchp', p4, eye).reshape(SSM_MACRO, MACRO_CH, MACRO_ST)

    def cmat(part):
        p4 = part.reshape(SSM_MACRO, G8, SSM_GROUP, SSM_STATE)
        return jnp.einsum('mgcp,gh->mgphc', p4, eye).reshape(SSM_MACRO, MACRO_ST, MACRO_CH)

    bm = jnp.concatenate([bmat(b_bar.real), bmat(b_bar.imag)], axis=2)
    cm = jnp.concatenate([cmat(c_re), -cmat(c_im)], axis=1)
    a4 = a_bar.reshape(SSM_MACRO, 1, MACRO_ST)
    a = jnp.concatenate([a4.real, a4.imag], axis=2)
    return bm, cm, a


def _rope_tables(positions):
    freqs = ROPE_THETA ** (-jnp.arange(0, QK_ROPE, 2, dtype=f32) / QK_ROPE)
    ang = positions.astype(f32)[:, None] * freqs
    cos, sin = jnp.cos(ang), jnp.sin(ang)
    S = positions.shape[0]
    half = QK_ROPE // 2
    one, zero = jnp.ones((S, QK_NOPE), f32), jnp.zeros((S, half), f32)
    z64, z32 = jnp.zeros((S, QK_NOPE), f32), jnp.zeros((S, LANES - QK_NOPE - QK_ROPE), f32)
    tc = jnp.concatenate([one, cos, cos, z32], axis=1)
    s1 = jnp.concatenate([z64, -sin, zero, z32], axis=1)
    s2 = jnp.concatenate([z64, zero, sin, z32], axis=1)
    return tc, s1, s2


def _layer_params(W, l):
    p = {}
    p['win'] = _pack_win(W['w_in'][l])
    p['wuq'] = _pad_heads(W['w_uq'][l], QK_NOPE + QK_ROPE, _Q_PIECES)
    wukv = W['w_ukv'][l]
    p['wukv'] = jnp.concatenate([_pad_heads(wukv, QK_NOPE + V_HEAD, _K_PIECES),
                                 _pad_heads(wukv, QK_NOPE + V_HEAD, _V_PIECES)], axis=1)
    p['attn_g'] = _pad_gain(W['attn_out_g'][l])
    return p


def _forward_layer(h, memn_in, tabs, W, l, name, gather=None):
    p = _layer_params(W, l)
    sv = {'h0': h, 'p': p}
    xn = _rms_fwd(h, W['norm_mix_g'][l], name=name + "rms_mix")
    proj = _mm([(xn, p['win'])], 'nn', f32, name=name + "mm_in")
    cqn = _rms_fwd(proj, W['q_norm_g'][l], col0=0, width=Q_LORA, name=name + "rms_q")
    ckvn = _rms_fwd(proj, W['kv_norm_g'][l], col0=Q_LORA, width=KV_LORA, name=name + "rms_kv")
    q = _mm([(cqn, p['wuq'])], 'nn', f32, name=name + "mm_uq")
    kv = _mm([(ckvn, p['wukv'])], 'nn', f32, name=name + "mm_ukv")
    qh, kh, vh = _rope_fwd(q, kv, proj, tabs, name=name + "rope")
    oh, lse, carried = _attn_fwd(qh, kh, vh, plan=gather[0] if gather else None, name=name + "attn")
    if gather:
        for n, t in gather[1](carried).items():
            W[n][l] = t
    p['wout'] = _pack_wout(W['w_out'][l])
    an = _rms_fwd(oh, p['attn_g'], n_valid=MLA_WIDTH, name=name + "rms_attn")
    bm, cm, a = W['ssm'][l]
    bmb, cmb = bm.astype(bf16), cm.astype(bf16)
    y, xc = _ssm_fwd(proj, bmb, cmb, a, W['ssm_d'][l], name=name + "ssm")
    z, sn, ge = _glu_fwd(y, W['ssm_w_glu'][l], W['ssm_b_glu'][l], W['ssm_out_g'][l], name=name + "glu")
    h1a = _mm([(an, p['wout'][:MLA_PAD])], 'nn', f32, res=h, name=name + "mm_out_a")
    h1 = _mm([(sn, p['wout'][MLA_PAD:])], 'nn', f32, res=h1a, name=name + "mm_out_s")
    hn2 = _rms_fwd(h1, W['norm_x_g'][l], name=name + "rms_x")
    memn = _rms_fwd(memn_in, W['mem_norm_g'][l], name=name + "rms_mem")
    qx = _mm([(hn2, W['w_xq'][l])], 'nn', bf16, name=name + "mm_xq")
    kvx = _mm([(memn, W['w_xkv'][l])], 'nn', bf16, name=name + "mm_xkv")
    ox = _xattn_fwd(qx, kvx, name=name + "xattn")
    h2 = _mm([(ox, W['w_xo'][l])], 'nn', f32, res=h1, name=name + "mm_xo")
    hn3 = _rms_fwd(h2, W['norm_ffn_g'][l], name=name + "rms_ffn")
    gate, up, act = _ffn_up(hn3, W['w_gate'][l], W['w_up'][l], name=name + "ffn_up")
    h3 = _mm([(act, W['w_down'][l])], 'nn', f32, res=h2, name=name + "mm_down")
    sv.update(xn=xn, proj=proj, cqn=cqn, ckvn=ckvn, qh=qh, kh=kh, vh=vh, oh=oh, lse=lse, an=an, bmb=bmb, cmb=cmb,
              a=a, y=y, xc=xc, z=z, sn=sn, ge=ge, h1=h1, hn2=hn2, memn=memn, qx=qx, kvx=kvx, ox=ox, h2=h2, hn3=hn3,
              gate=gate, up=up, act=act)
    return h3, sv


def _backward_layer(dh3, dh3b, sv, memn_in, tabs, W, l, hook, name):
    p = sv['p']
    G = {}
    G['w_down'] = _mm([(sv['act'], dh3b)], 'tn', f32, name=name + "dw_down")
    dgate, dup = _ffn_bwd_act(dh3b, W['w_down'][l], sv['gate'], sv['up'], name=name + "ffn_bwd_act")
    dhn3 = _mm([(dgate, W['w_gate'][l]), (dup, W['w_up'][l])], 'nn', f32, name=name + "mm_dffn")
    G['w_gate'] = _mm([(dgate, sv['hn3'])], 'tn', f32, name=name + "dw_gate")
    G['w_up'] = _mm([(dup, sv['hn3'])], 'tn', f32, name=name + "dw_up")
    xplan = hook['ffn'](l, G) if hook else None
    dh2, dg, dh2b = _rms_bwd(sv['h2'], W['norm_ffn_g'][l], dhn3, res=dh3, dup_bf16=True, name=name + "rmsb_ffn")
    G['norm_ffn_g'] = dg[0]
    G['w_xo'] = _mm([(sv['ox'], dh2b)], 'tn', f32, name=name + "dw_xo")
    dox = _mm([(dh2b, W['w_xo'][l])], 'nt', bf16, name=name + "mm_dxo")
    dqx, dkvx, carried = _xattn_bwd(sv['qx'], sv['kvx'], dox, plan=xplan, name=name + "xattn_bwd")
    if xplan:
        xplan['done'](carried)
    G['w_xq'] = _mm([(sv['hn2'], dqx)], 'tn', f32, name=name + "dw_xq")
    G['w_xkv'] = _mm([(sv['memn'], dkvx)], 'tn', f32, name=name + "dw_xkv")
    dhn2 = _mm([(dqx, W['w_xq'][l])], 'nt', f32, name=name + "mm_dxq")
    dmemn = _mm([(dkvx, W['w_xkv'][l])], 'nt', f32, name=name + "mm_dxkv")
    dh1, dg, dh1b = _rms_bwd(sv['h1'], W['norm_x_g'][l], dhn2, res=dh2, dup_bf16=True, name=name + "rmsb_x")
    G['norm_x_g'] = dg[0]
    _, dg = _rms_bwd(memn_in, W['mem_norm_g'][l], dmemn, name=name + "rmsb_mem")
    G['mem_norm_g'] = dg[0]
    dwo_a = _mm([(sv['an'], dh1b)], 'tn', f32, name=name + "dw_out_a")
    dwo_s = _mm([(sv['sn'], dh1b)], 'tn', f32, name=name + "dw_out_s")
    G['w_out'] = _unpack_wout(jnp.concatenate([dwo_a, dwo_s], axis=0))
    dmixed = _mm([(dh1b, p['wout'])], 'nt', f32, name=name + "mm_dout")
    dy, dz, dg, db = _glu_bwd(sv['y'], sv['z'], dmixed, W['ssm_w_glu'][l], W['ssm_out_g'][l], name=name + "glu_bwd")
    G['ssm_out_g'], G['ssm_b_glu'] = dg[0], db[0]
    G['ssm_w_glu'] = _mm([(sv['ge'], dz)], 'tn', f32, name=name + "dw_glu")
    xplan = hook['mid'](l, G) if hook else None
    du, dbm, dcm, da, dd, carried = _ssm_bwd(sv['proj'], dy, sv['xc'], sv['bmb'], sv['cmb'], sv['a'], W['ssm_d'][l],
                                             plan=xplan, name=name + "ssm_bwd")
    if xplan:
        xplan['done'](carried)
    G['ssm_d'] = dd[0]
    G['ssm_raw'] = (dbm, dcm, da)
    doh, dg, delta = _rms_bwd(sv['oh'], p['attn_g'], dmixed, width=MLA_PAD, n_valid=MLA_WIDTH, delta=True,
                              out_dtype=bf16, name=name + "rmsb_attn")
    G['attn_out_g'] = _unpad_gain(dg[0])
    plans = hook['take']() if hook else []
    plan = _merge_plans(plans) if plans else None
    dqh, dkh, dvh, carried = _attn_bwd(sv['qh'], sv['kh'], sv['vh'], doh, sv['lse'], delta, plan=plan,
                                       name=name + "attn_bwd")
    if plan:
        plan['done'](carried)
    dq, dkv, dkr = _rope_bwd(dqh, dkh, dvh, tabs, name=name + "rope_bwd")
    G['w_uq'] = _unpad_heads(_mm([(sv['cqn'], dq)], 'tn', f32, name=name + "dw_uq"), QK_NOPE + QK_ROPE, _Q_PIECES)
    dwukv = _mm([(sv['ckvn'], dkv)], 'tn', f32, name=name + "dw_ukv")
    G['w_ukv'] = (_unpad_heads(dwukv[:, :MLA_PAD], QK_NOPE + V_HEAD, _K_PIECES)
                  + _unpad_heads(dwukv[:, MLA_PAD:], QK_NOPE + V_HEAD, _V_PIECES))
    dcqn = _mm([(dq, p['wuq'])], 'nt', f32, name=name + "mm_duq")
    dckvn = _mm([(dkv, p['wukv'])], 'nt', f32, name=name + "mm_dukv")
    dcq, dg = _rms_bwd(sv['proj'], W['q_norm_g'][l], dcqn, col0=0, width=Q_LORA, out_dtype=bf16, name=name + "rmsb_q")
    G['q_norm_g'] = dg[0]
    dckv, dg = _rms_bwd(sv['proj'], W['kv_norm_g'][l], dckvn, col0=Q_LORA, width=KV_LORA, out_dtype=bf16,
                        name=name + "rmsb_kv")
    G['kv_norm_g'] = dg[0]
    dproj = jnp.concatenate([dcq, dckv, dkr, du.astype(bf16)], axis=1)
    G['w_in'] = _unpack_win(_mm([(sv['xn'], dproj)], 'tn', f32, name=name + "dw_in"))
    dxn = _mm([(dproj, p['win'])], 'nt', f32, name=name + "mm_din")
    dh0, dg, dh0b = _rms_bwd(sv['h0'], W['norm_mix_g'][l], dxn, res=dh1, dup_bf16=True, name=name + "rmsb_mix")
    G['norm_mix_g'] = dg[0]
    return dh0, dh0b, G


def _local_step(x, mem, positions, target, W, gathers=None, hook=None):
    tabs = _rope_tables(positions)
    ssm_in = [(W['ssm_lambda_re'][l], W['ssm_lambda_im'][l], W['ssm_log_dt'][l], W['ssm_b_re'][l], W['ssm_b_im'][l],
               W['ssm_c_re'][l], W['ssm_c_im'][l]) for l in range(DEPTH)]
    preps = [jax.vjp(_ssm_prep, *ssm_in[l]) for l in range(DEPTH)]
    W = dict(W)
    W['ssm'] = [preps[l][0] for l in range(DEPTH)]
    h = x
    saved = []
    for l in range(DEPTH):
        h, sv = _forward_layer(h, mem, tabs, W, l, f"l{l}_", gathers[l] if gathers else None)
        saved.append(sv)
    dh, dgf, loss, dhb = _loss_head(h, W['final_norm_g'], target)
    grads = [None] * DEPTH
    for l in reversed(range(DEPTH)):
        dh, dhb, G = _backward_layer(dh, dhb, saved[l], mem, tabs, W, l, hook, f"l{l}b_")
        dbm, dcm, da = G.pop('ssm_raw')
        names = ['ssm_lambda_re', 'ssm_lambda_im', 'ssm_log_dt', 'ssm_b_re', 'ssm_b_im', 'ssm_c_re', 'ssm_c_im']
        for n, g in zip(names, preps[l][1]((dbm, dcm, da))):
            G[n] = g
        grads[l] = G
        if hook is not None and l > 0:
            hook['rest'](l, G)
    out = {n: [grads[l][n] for l in range(DEPTH)] if n in SHARDED else jnp.stack([grads[l][n] for l in range(DEPTH)])
           for n in grads[0]}
    out['final_norm_g'] = dgf[0]
    return loss[0, 0], dh, out


_HBM = pl.BlockSpec(memory_space=pltpu.HBM)


def _me():
    return lax.axis_index("x"), lax.axis_index("y"), lax.axis_index("c")


def _chip_peers(x, y, c):
    devs = [(1 - x, y, c), (x, 1 - y, c), (1 - x, 1 - y, c)]
    return devs, [2 * d[0] + d[1] for d in devs]


def _gather_plan(xs, half_first):
    n = len(xs)
    if half_first:
        ins = [t.reshape(2, 1, *t.shape[1:]) for t in xs]
        outs = [jax.ShapeDtypeStruct((2, 4, *t.shape[1:]), t.dtype) for t in xs]
    else:
        ins = [t.reshape(1, 2, t.shape[0] // 2, t.shape[1]) for t in xs]
        outs = [jax.ShapeDtypeStruct((4, 2, t.shape[0] // 2, t.shape[1]), t.dtype) for t in xs]

    def own(ref, h):
        return ref.at[h] if half_first else ref.at[:, h]

    def slot(ref, h, j):
        return ref.at[h, pl.ds(j, 1)] if half_first else ref.at[pl.ds(j, 1), h]

    def copies(src, dst, send, recv):
        x, y, c = _me()
        jme = 2 * x + y
        devs, js = _chip_peers(x, y, c)
        half, other = pl.ds(c, 1), pl.ds(1 - c, 1)
        mk = pltpu.make_async_remote_copy
        for i in range(n):
            for k in range(3):
                out_cp = mk(own(src[i], half), slot(dst[i], half, jme), send.at[6 * i + k], recv.at[6 * i + k],
                            device_id=devs[k], device_id_type=MESH)
                in_cp = mk(own(src[i], half), slot(dst[i], half, js[k]), send.at[6 * i + k], recv.at[6 * i + k],
                           device_id=devs[k], device_id_type=MESH)
                pass_cp = mk(slot(dst[i], half, js[k]), slot(dst[i], half, js[k]), send.at[6 * i + 3 + k],
                             recv.at[6 * i + 3 + k], device_id=(x, y, 1 - c), device_id_type=MESH)
                got_cp = mk(slot(dst[i], other, js[k]), slot(dst[i], other, js[k]), send.at[6 * i + 3 + k],
                            recv.at[6 * i + 3 + k], device_id=(x, y, 1 - c), device_id_type=MESH)
                yield out_cp, in_cp, pass_cp, got_cp

    def start(*refs):
        for out_cp, _, _, _ in copies(*refs):
            out_cp.start()

    def forward(*refs):
        for _, in_cp, pass_cp, _ in copies(*refs):
            in_cp.wait_recv()
            pass_cp.start()

    def finish(*refs):
        for out_cp, _, pass_cp, got_cp in copies(*refs):
            got_cp.wait_recv()
            out_cp.wait_send()
            pass_cp.wait_send()

    return dict(n=n, ins=ins, outs=outs, nsem=6 * n, start=start, forward=forward, finish=finish)


def _plan_refs(plan, refs):
    n = plan['n']
    return refs[:n], refs[n:2 * n], refs[2 * n], refs[2 * n + 1]


def _run_plan(plan, name):
    n = plan['n']

    def body(*refs):
        args = _plan_refs(plan, refs)
        plan['start'](*args)
        plan['forward'](*args)
        plan['finish'](*args)

    return _pc(body, in_specs=[_HBM] * n, out_specs=[_HBM] * n, out_shape=plan['outs'],
               scratch_shapes=[pltpu.SemaphoreType.DMA((plan['nsem'],)), pltpu.SemaphoreType.DMA((plan['nsem'],))],
               compiler_params=pltpu.CompilerParams(has_side_effects=True), name=name)(*plan['ins'])


def _fill_own(gathered, own, half_first):
    jme = (2 * lax.axis_index("x") + lax.axis_index("y")).astype(jnp.int32)
    zero = jnp.int32(0)
    if half_first:
        return lax.dynamic_update_slice(gathered, own[:, None], (zero, jme, zero, zero))
    return lax.dynamic_update_slice(gathered, own.reshape(1, *gathered.shape[1:]), (jme, zero, zero, zero))


def _exchange_plan(gs, done=None):
    n = len(gs)

    def copies(src, dst, send, recv):
        x, y, c = _me()
        for i in range(n):
            yield pltpu.make_async_remote_copy(src[i].at[:, pl.ds(1 - c, 1)], dst[i], send.at[i], recv.at[i],
                                               device_id=(x, y, 1 - c), device_id_type=MESH)

    def start(*refs):
        for cp in copies(*refs):
            cp.start()

    def finish(*refs):
        for cp in copies(*refs):
            cp.wait()

    outs = [jax.ShapeDtypeStruct((4, 1, *g.shape[2:]), g.dtype) for g in gs]
    return dict(n=n, ins=list(gs), outs=outs, nsem=n, start=start, forward=lambda *refs: None, finish=finish, done=done)


def _plan_extras(plan):
    if not plan:
        return [], [], [], [], []
    anyspec = pl.BlockSpec(memory_space=pl.ANY)
    sems = [pltpu.SemaphoreType.DMA((plan['nsem'],)), pltpu.SemaphoreType.DMA((plan['nsem'],))]
    return [anyspec] * plan['n'], [anyspec] * plan['n'], list(plan['outs']), sems, list(plan['ins'])


def _scatter_plan(ps, done=None):
    n = len(ps)

    def copies(src, dst, send, recv, off):
        x, y, c = _me()
        devs, js = _chip_peers(x, y, c)
        for i in range(n):
            for k in range(3):
                yield pltpu.make_async_remote_copy(src[i].at[pl.ds(js[k], 1)], dst[i].at[k], send.at[off + 3 * i + k],
                                                   recv.at[off + 3 * i + k], device_id=devs[k], device_id_type=MESH)

    def start(src, dst, send, recv, off=0):
        for cp in copies(src, dst, send, recv, off):
            cp.start()

    def finish(src, dst, send, recv, off=0):
        for cp in copies(src, dst, send, recv, off):
            cp.wait()

    outs = [jax.ShapeDtypeStruct((3, 1, *p.shape[1:]), p.dtype) for p in ps]
    return dict(n=n, ins=list(ps), outs=outs, nsem=3 * n, start=start, forward=lambda *refs: None, finish=finish,
                done=done)


def _merge_plans(plans):
    def run(which):
        def f(src, dst, send, recv):
            o = s = 0
            for p in plans:
                p[which](src[o:o + p['n']], dst[o:o + p['n']], send, recv, off=s)
                o, s = o + p['n'], s + p['nsem']
        return f

    def done(results):
        o = 0
        for p in plans:
            p['done'](results[o:o + p['n']])
            o += p['n']

    return dict(n=sum(p['n'] for p in plans), ins=[t for p in plans for t in p['ins']],
                outs=[t for p in plans for t in p['outs']], nsem=sum(p['nsem'] for p in plans),
                start=run('start'), forward=lambda *refs: None, finish=run('finish'), done=done)


def _swap_sibling(hs, name):
    n = len(hs)

    def body(*refs):
        src, dst = refs[:n], refs[n:2 * n]
        send, recv = refs[2 * n:]
        x, y, c = _me()
        cps = []
        for i in range(n):
            cp = pltpu.make_async_remote_copy(src[i], dst[i], send.at[i], recv.at[i], device_id=(x, y, 1 - c),
                                              device_id_type=MESH)
            cp.start()
            cps.append(cp)
        for cp in cps:
            cp.wait()

    outs = [jax.ShapeDtypeStruct(h.shape, h.dtype) for h in hs]
    return _pc(body, in_specs=[_HBM] * n, out_specs=[_HBM] * n, out_shape=outs,
               scratch_shapes=[pltpu.SemaphoreType.DMA((n,)), pltpu.SemaphoreType.DMA((n,))],
               compiler_params=pltpu.CompilerParams(has_side_effects=True), name=name)(*hs)


ELEMWISE_VMEM_BUDGET = 36 * 1024 * 1024


def _row_tile(r, n, narrays):
    limit = ELEMWISE_VMEM_BUDGET // (2 * 4 * narrays * n)
    best = SUBLANES
    for t in range(16, r + 1, 16):
        if r % t == 0 and t <= limit:
            best = t
    return best


def _add_half(g, r1, cidx, name):
    _, _, r, n = g.shape
    tr = _row_tile(r, n, 3)

    def body(c_ref, g_ref, r_ref, o_ref):
        o_ref[...] = (g_ref[...] + r_ref[...]).astype(GRAD_TRANSIT)

    blk = lambda f: pl.BlockSpec((None, None, tr, n), f)
    gs = pltpu.PrefetchScalarGridSpec(
        num_scalar_prefetch=1, grid=(4, r // tr),
        in_specs=[blk(lambda j, i, c: (j, c[0], i, 0)), blk(lambda j, i, c: (j, 0, i, 0))],
        out_specs=pl.BlockSpec((None, tr, n), lambda j, i, c: (j, i, 0)))
    return _pc(body, grid_spec=gs, out_shape=jax.ShapeDtypeStruct((4, r, n), GRAD_TRANSIT),
               compiler_params=_cp(("parallel", "parallel")), name=name)(cidx, g, r1)


def _add_chips(p, r3, jidx, name):
    _, r, n = p.shape
    tr = _row_tile(r, n, 5)

    def body(j_ref, p_ref, a_ref, b_ref, c_ref, o_ref):
        o_ref[...] = ((p_ref[...].astype(f32) + a_ref[...].astype(f32)) + b_ref[...].astype(f32)) + c_ref[...].astype(f32)

    rblk = lambda k: pl.BlockSpec((None, None, tr, n), lambda i, j: (k, 0, i, 0))
    gs = pltpu.PrefetchScalarGridSpec(
        num_scalar_prefetch=1, grid=(r // tr,),
        in_specs=[pl.BlockSpec((None, tr, n), lambda i, j: (j[0], i, 0)), rblk(0), rblk(1), rblk(2)],
        out_specs=pl.BlockSpec((tr, n), lambda i, j: (i, 0)))
    return _pc(body, grid_spec=gs, out_shape=jax.ShapeDtypeStruct((r, n), f32),
               compiler_params=_cp(("parallel",)), name=name)(jidx, p, r3, r3, r3)


def _adamw_halves(w, mine, theirs, m, v, cidx, name):
    L, r, n = w.shape
    r2 = r // 2
    tr = _row_tile(r2, n, 11)
    c1 = 1.0 / (1.0 - ADAM_B1 ** ADAM_STEP)
    c2 = 1.0 / (1.0 - ADAM_B2 ** ADAM_STEP)

    def body(c_ref, w_ref, a0_ref, b0_ref, a1_ref, b1_ref, m_ref, v_ref, g_ref, d_ref, mo_ref, vo_ref):
        l, hf = pl.program_id(0), pl.program_id(1)
        own = hf == c_ref[0]
        gv = jnp.where(l == 0, jnp.where(own, a0_ref[...], b0_ref[...]), jnp.where(own, a1_ref[...], b1_ref[...]))
        m2 = ADAM_B1 * m_ref[...] + (1.0 - ADAM_B1) * gv
        v2 = ADAM_B2 * v_ref[...] + (1.0 - ADAM_B2) * (gv * gv)
        g_ref[...] = gv
        d_ref[...] = -ADAM_LR * ((m2 * c1) / (jnp.sqrt(v2 * c2) + ADAM_EPS) + ADAM_WD * w_ref[...])
        mo_ref[...] = m2
        vo_ref[...] = v2

    full = pl.BlockSpec((None, None, tr, n), lambda l, hf, i, c: (l, hf, i, 0))

    def half(layer, own):
        return pl.BlockSpec((tr, n), lambda l, hf, i, c: (jnp.where((l == layer) & ((hf == c[0]) == own), i, 0), 0))

    gs = pltpu.PrefetchScalarGridSpec(
        num_scalar_prefetch=1, grid=(L, 2, r2 // tr),
        in_specs=[full, half(0, True), half(0, False), half(1, True), half(1, False), full, full], out_specs=[full] * 4)
    four = lambda t: t.reshape(L, 2, r2, n)
    outs = _pc(body, grid_spec=gs, out_shape=[jax.ShapeDtypeStruct((L, 2, r2, n), f32)] * 4,
               compiler_params=_cp(("parallel", "parallel", "parallel")), name=name)(
        cidx, four(w), mine[0], theirs[0], mine[1], theirs[1], four(m), four(v))
    return [t.reshape(w.shape) for t in outs]


def _adamw_whole(w, g, m, v, name):
    c1 = 1.0 / (1.0 - ADAM_B1 ** ADAM_STEP)
    c2 = 1.0 / (1.0 - ADAM_B2 ** ADAM_STEP)

    def body(w_ref, g_ref, m_ref, v_ref, d_ref, mo_ref, vo_ref):
        gv = g_ref[...]
        m2 = ADAM_B1 * m_ref[...] + (1.0 - ADAM_B1) * gv
        v2 = ADAM_B2 * v_ref[...] + (1.0 - ADAM_B2) * (gv * gv)
        d_ref[...] = -ADAM_LR * ((m2 * c1) / (jnp.sqrt(v2 * c2) + ADAM_EPS) + ADAM_WD * w_ref[...])
        mo_ref[...] = m2
        vo_ref[...] = v2

    return _pc(body, out_shape=[jax.ShapeDtypeStruct(w.shape, f32)] * 3, name=name)(w, g, m, v)


def _full_from_gathered(name, t):
    r, n = 2 * t.shape[2], t.shape[3]
    if SHARDED[name] == 1 or name in TRANSPOSED:
        return t.reshape(4 * r, n)
    return t.reshape(4, r, n).transpose(1, 0, 2).reshape(r, 4 * n)


def _shard_major(name, g):
    R, C = g.shape
    if SHARDED[name] == 1 or name in TRANSPOSED:
        return g.reshape(4, 2, R // 8, C)
    return g.reshape(R, 4, C // 4).transpose(1, 0, 2).reshape(4, 2, R // 2, C // 4)


_SMALL_ROWS = 288


def _pack_small(d):
    flat = jnp.concatenate([d[n].reshape(-1) for n in SMALL])
    total = 2 * 4 * _SMALL_ROWS * LANES
    flat = jnp.concatenate([flat, jnp.zeros((total - flat.shape[0],), f32)])
    return flat.reshape(4, 2, _SMALL_ROWS, LANES)


def _unpack_small(t, like):
    flat = t.reshape(-1)
    out, off = {}, 0
    for n in SMALL:
        sz = math.prod(like[n].shape)
        out[n] = flat[off:off + sz].reshape(like[n].shape)
        off += sz
    return out


def kernel(x, mem, positions, norm_mix_g, w_in, q_norm_g, w_uq, kv_norm_g, w_ukv, ssm_lambda_re, ssm_lambda_im, ssm_log_dt, ssm_b_re, ssm_b_im, ssm_c_re, ssm_c_im, ssm_d, ssm_w_glu, ssm_b_glu, attn_out_g, ssm_out_g, w_out, norm_x_g, mem_norm_g, w_xq, w_xkv, w_xo, norm_ffn_g, w_gate, w_up, w_down, final_norm_g, loss_target, m_norm_mix_g, m_w_in, m_q_norm_g, m_w_uq, m_kv_norm_g, m_w_ukv, m_ssm_lambda_re, m_ssm_lambda_im, m_ssm_log_dt, m_ssm_b_re, m_ssm_b_im, m_ssm_c_re, m_ssm_c_im, m_ssm_d, m_ssm_w_glu, m_ssm_b_glu, m_attn_out_g, m_ssm_out_g, m_w_out, m_norm_x_g, m_mem_norm_g, m_w_xq, m_w_xkv, m_w_xo, m_norm_ffn_g, m_w_gate, m_w_up, m_w_down, m_final_norm_g, v_norm_mix_g, v_w_in, v_q_norm_g, v_w_uq, v_kv_norm_g, v_w_ukv, v_ssm_lambda_re, v_ssm_lambda_im, v_ssm_log_dt, v_ssm_b_re, v_ssm_b_im, v_ssm_c_re, v_ssm_c_im, v_ssm_d, v_ssm_w_glu, v_ssm_b_glu, v_attn_out_g, v_ssm_out_g, v_w_out, v_norm_x_g, v_mem_norm_g, v_w_xq, v_w_xkv, v_w_xo, v_norm_ffn_g, v_w_gate, v_w_up, v_w_down, v_final_norm_g):
    given = dict(locals())
    swap = lambda n, t: jnp.swapaxes(t, *TRANSPOSED[n]) if n in TRANSPOSED else t
    w = {n: swap(n, given[n]) for n in WEIGHTS}
    m = {n: swap(n, given["m_" + n]) for n in WEIGHTS}
    v = {n: swap(n, given["v_" + n]) for n in WEIGHTS}
    big = list(SHARDED)

    shards = {n: w[n].astype(bf16) for n in big}
    early = [n for n in big if n in EARLY_WEIGHTS]
    rest = [n for n in big if n not in EARLY_WEIGHTS]

    def full(names, results, l):
        return {n: _full_from_gathered(n, _fill_own(t, shards[n][l], False)) for n, t in zip(names, results)}

    first = _run_plan(_gather_plan([shards[n][l] for l in range(DEPTH) for n in early], False), "allgather_weights_early")
    W = {n: [None] * DEPTH for n in big}
    for l in range(DEPTH):
        for n, t in full(early, first[l * len(early):(l + 1) * len(early)], l).items():
            W[n][l] = t
    gathers = [(_gather_plan([shards[n][l] for n in rest], False), functools.partial(full, rest, l=l))
               for l in range(DEPTH)]
    W.update({n: w[n] for n in SMALL})

    cidx = lax.axis_index("c").astype(jnp.int32).reshape(1)
    jidx = (2 * lax.axis_index("x") + lax.axis_index("y")).astype(jnp.int32).reshape(1)
    ffn = [n for n in big if n in FFN_WEIGHTS]
    tail = [n for n in big if n in EARLY_WEIGHTS]
    mid = [n for n in big if n not in FFN_WEIGHTS and n not in EARLY_WEIGHTS]
    sums, got = {}, {}
    ready = []

    def exchange(names, tag, l, G, extra=()):
        keys = [(l, n) for n in names] + [(l, n) for n, _ in extra]
        gs = [_shard_major(n, G[n]) for n in names] + [g for _, g in extra]

        def done(r1):
            ps = [_add_half(g, r, cidx, f"grad_add_half_{tag}_{k[1]}") for k, g, r in zip(keys, gs, r1)]
            sums.update(zip(keys, ps))
            ready.append(_scatter_plan(ps, done=lambda results: got.update(zip(keys, results))))

        return _exchange_plan(gs, done)

    def take():
        plans = list(ready)
        ready.clear()
        return plans

    def now(plan, name):
        plan['done'](_run_plan(plan, name))

    hook = {'ffn': lambda l, G: exchange(ffn, f"l{l}_ffn", l, G), 'mid': lambda l, G: exchange(mid, f"l{l}_mid", l, G),
            'rest': lambda l, G: now(exchange(tail, f"l{l}_tail", l, G), f"grad_exchange_halves_l{l}_tail"),
            'take': take}
    loss, dx, grads = _local_step(x[0], mem[0], positions[0], loss_target[0], W, gathers=gathers, hook=hook)
    loss = lax.psum(loss, ("x", "y", "c"))

    now(exchange(tail, "l0_tail", 0, {n: grads[n][0] for n in tail}, extra=[("small", _pack_small(grads))]),
        "grad_exchange_halves_l0_tail")
    now(_merge_plans(take()), "grad_scatter_chips_l0_tail")
    keys = list(sums)
    hs = dict(zip(keys, [_add_chips(sums[k], got[k], jidx, f"grad_add_chips_l{k[0]}_{k[1]}") for k in keys]))
    ts = dict(zip(keys, _swap_sibling([hs[k] for k in keys], "grad_swap_sibling")))

    out_g, out_d, out_m, out_v = {}, {}, {}, {}
    for n in big:
        mine, theirs = [hs[(l, n)] for l in range(DEPTH)], [ts[(l, n)] for l in range(DEPTH)]
        out_g[n], out_d[n], out_m[n], out_v[n] = _adamw_halves(w[n], mine, theirs, m[n], v[n], cidx, f"adamw_{n}")
    both = jnp.stack([hs[(0, "small")], ts[(0, "small")]])
    piece = jnp.where(cidx[0] == 0, both, both[::-1]).reshape(2 * _SMALL_ROWS, LANES)
    gsm = _fill_own(_run_plan(_gather_plan([piece], False), "allgather_small")[0], piece, False)
    out_g.update(_unpack_small(gsm, w))
    for n in SMALL:
        two = lambda t: t.reshape(1, -1) if t.ndim == 1 else t
        d_, m_, v_ = _adamw_whole(two(w[n]), two(out_g[n]), two(m[n]), two(v[n]), f"adamw_{n}")
        out_d[n], out_m[n], out_v[n] = (t.reshape(w[n].shape) for t in (d_, m_, v_))

    outs = [[swap(n, d[n]) for n in WEIGHTS] for d in (out_g, out_d, out_m, out_v)]
    return (loss, dx.reshape(x.shape), *outs[0], *outs[1], *outs[2], *outs[3])
```
